```python
import jax, jax.numpy as jnp
from jax import lax
import numpy as np

D_MODEL = 1024
BATCH = 8
SEQ = 2048
DEPTH = 1

D_MIX = D_MODEL
D_CONV = D_MIX // 2
D_ATTN = D_MIX - D_CONV
N_HEADS = 8
HEAD_DIM = D_ATTN // N_HEADS
CONV_K = 31
DILATED_PATTERNS = ((128, 1), (512, 4), (2048, 16))
ATTN_BLOCK = 128
D_FF = 128 * ((8 * D_MODEL // 3 + 127) // 128)
FFN_CONV_K = 3
D_IN = 2 * D_CONV + 3 * D_ATTN
N_MOD = 6
EPS = 1e-6

kernel_name = "hymba_conformer_dilated_attn_convffn_adaln"


def rmsnorm(x, g):
    xf = x.astype(jnp.float32)
    y = xf * lax.rsqrt(jnp.mean(xf * xf, axis=-1, keepdims=True) + EPS)
    return (y * g.astype(jnp.float32)).astype(x.dtype)


def layernorm(x, g, b):
    xf = x.astype(jnp.float32)
    mu = jnp.mean(xf, axis=-1, keepdims=True)
    var = jnp.mean(jnp.square(xf - mu), axis=-1, keepdims=True)
    y = (xf - mu) * lax.rsqrt(var + EPS)
    return (y * g.astype(jnp.float32) + b.astype(jnp.float32)).astype(x.dtype)


def modulate(h, shift, scale):
    return h * (1 + scale[:, None, :]) + shift[:, None, :]


def causal_dwconv(x, w, b):
    k, ch = w.shape
    y = lax.conv_general_dilated(
        x, w[:, None, :].astype(x.dtype), window_strides=(1,), padding=((k - 1, 0),),
        dimension_numbers=("NWC", "WIO", "NWC"), feature_group_count=ch)
    return y + b.astype(x.dtype)


def dilated_window_attn(q, k, v, window, dilation):
    b, s, h, hd = q.shape
    r = dilation
    nw = window // r
    blk = ATTN_BLOCK
    l_sub = -(-s // r)
    lp = -(-l_sub // blk) * blk
    sp = lp * r
    nb = lp // blk

    def to_sub(t):
        t = jnp.pad(t, ((0, 0), (0, sp - s), (0, 0), (0, 0))).reshape(b, lp, r, h, hd)
        return t.transpose(0, 2, 3, 1, 4)

    def kv_blocks(t):
        t = jnp.pad(t, ((0, 0), (0, 0), (0, 0), (blk, 0), (0, 0))).reshape(b, r, h, nb + 1, blk, hd)
        return jnp.concatenate([t[:, :, :, :-1], t[:, :, :, 1:]], axis=4)

    qb = to_sub(q).reshape(b, r, h, nb, blk, hd).astype(jnp.float32)
    kb = kv_blocks(to_sub(k)).astype(jnp.float32)
    vb = kv_blocks(to_sub(v)).astype(jnp.float32)

    scores = jnp.einsum("brhnqd,brhnkd->brhnqk", qb, kb) * (hd ** -0.5)
    qi = jnp.arange(blk)[:, None]
    ki = jnp.arange(2 * blk)[None, :]
    dist = qi + blk - ki
    n_idx = jnp.arange(nb)[:, None, None]
    valid = (dist >= 0) & (dist <= nw) & ((n_idx > 0) | (ki >= blk))
    scores = jnp.where(valid, scores, -jnp.inf)
    m = jnp.max(scores, axis=-1, keepdims=True)
    p = jnp.exp(scores - m)
    denom = jnp.sum(p, axis=-1, keepdims=True)
    o = jnp.einsum("brhnqk,brhnkd->brhnqd", p, vb) / denom
    lse = (m + jnp.log(denom))[..., 0]

    o = o.reshape(b, r, h, lp, hd).transpose(0, 3, 1, 2, 4).reshape(b, sp, h, hd)[:, :s]
    lse = lse.reshape(b, r, h, lp).transpose(0, 3, 1, 2).reshape(b, sp, h)[:, :s]
    return o, lse


def longnet_mixture(q, k, v):
    outs, lses = [], []
    for window, dilation in DILATED_PATTERNS:
        o, lse = dilated_window_attn(q, k, v, window, dilation)
        outs.append(o)
        lses.append(lse)
    alpha = jax.nn.softmax(jnp.stack(lses, axis=0), axis=0)
    o = jnp.sum(alpha[..., None] * jnp.stack(outs, axis=0), axis=0)
    return o.astype(q.dtype)


def _fwd_setup_inputs(seed: int = 0) -> dict:
    key = jax.random.key(seed)
    ks = jax.random.split(key, 24)
    f32 = jnp.float32
    nrm = lambda kk, shape, scale: jax.random.normal(kk, shape, f32) * scale
    L = DEPTH
    return {
        "x": nrm(ks[0], (BATCH, SEQ, D_MODEL), 1.0),
        "c": nrm(ks[1], (BATCH, D_MODEL), 1.0),
        "w_ada": nrm(ks[2], (L, D_MODEL, N_MOD * D_MODEL), 0.5 * D_MODEL ** -0.5),
        "b_ada": nrm(ks[3], (L, N_MOD * D_MODEL), 0.02),
        "g_norm_mix": 1.0 + nrm(ks[4], (L, D_MODEL), 0.02),
        "w_in": nrm(ks[5], (L, D_MODEL, D_IN), D_MODEL ** -0.5),
        "w_conv_dw": nrm(ks[6], (L, CONV_K, D_CONV), CONV_K ** -0.5),
        "b_conv_dw": nrm(ks[7], (L, D_CONV), 0.02),
        "ln_conv_g": 1.0 + nrm(ks[8], (L, D_CONV), 0.02),
        "ln_conv_b": nrm(ks[9], (L, D_CONV), 0.02),
        "g_conv_out": 1.0 + nrm(ks[10], (L, D_CONV), 0.02),
        "g_attn_out": 1.0 + nrm(ks[11], (L, D_ATTN), 0.02),
        "w_out": nrm(ks[12], (L, D_MIX, D_MODEL), D_MIX ** -0.5),
        "g_norm_ffn": 1.0 + nrm(ks[13], (L, D_MODEL), 0.02),
        "w_up": nrm(ks[14], (L, D_MODEL, 2 * D_FF), D_MODEL ** -0.5),
        "w_ffn_dw": nrm(ks[15], (L, FFN_CONV_K, 2 * D_FF), FFN_CONV_K ** -0.5),
        "b_ffn_dw": nrm(ks[16], (L, 2 * D_FF), 0.02),
        "w_down": nrm(ks[17], (L, D_FF, D_MODEL), D_FF ** -0.5),
        "g_final": 1.0 + nrm(ks[18], (D_MODEL,), 0.02),
    }


def _fwd_reference(x, c, w_ada, b_ada, g_norm_mix, w_in, w_conv_dw, b_conv_dw, ln_conv_g,
              ln_conv_b, g_conv_out, g_attn_out, w_out, g_norm_ffn, w_up, w_ffn_dw,
              b_ffn_dw, w_down, g_final):
    b, s, d = x.shape
    for i in range(DEPTH):
        mod = jax.nn.silu(c) @ w_ada[i] + b_ada[i]
        sh_m, sc_m, ga_m, sh_f, sc_f, ga_f = jnp.split(mod, N_MOD, axis=-1)

        h = modulate(rmsnorm(x, g_norm_mix[i]), sh_m, sc_m)
        proj = h @ w_in[i]
        a_val, a_gate, q, k, v = jnp.split(
            proj, np.cumsum([D_CONV, D_CONV, D_ATTN, D_ATTN]).tolist(), axis=-1)

        u = a_val * jax.nn.sigmoid(a_gate)
        u = causal_dwconv(u, w_conv_dw[i], b_conv_dw[i])
        u = jax.nn.silu(layernorm(u, ln_conv_g[i], ln_conv_b[i]))

        qh = q.reshape(b, s, N_HEADS, HEAD_DIM)
        kh = k.reshape(b, s, N_HEADS, HEAD_DIM)
        vh = v.reshape(b, s, N_HEADS, HEAD_DIM)
        att = longnet_mixture(qh, kh, vh).reshape(b, s, D_ATTN)

        mixed = jnp.concatenate(
            [rmsnorm(u, g_conv_out[i]), rmsnorm(att, g_attn_out[i])], axis=-1)
        x = x + ga_m[:, None, :] * (mixed @ w_out[i])

        h = modulate(rmsnorm(x, g_norm_ffn[i]), sh_f, sc_f)
        up = causal_dwconv(h @ w_up[i], w_ffn_dw[i], b_ffn_dw[i])
        gate, val = jnp.split(up, 2, axis=-1)
        x = x + ga_f[:, None, :] * ((jax.nn.silu(gate) * val) @ w_down[i])

    return rmsnorm(x, g_final)


import jax as _jax
import jax.numpy as _jnp

TWIN_FORMAT = 'train_step'
FWD_PARAMS = ['x', 'c', 'w_ada', 'b_ada', 'g_norm_mix', 'w_in', 'w_conv_dw', 'b_conv_dw', 'ln_conv_g', 'ln_conv_b', 'g_conv_out', 'g_attn_out', 'w_out', 'g_norm_ffn', 'w_up', 'w_ffn_dw', 'b_ffn_dw', 'w_down', 'g_final']
TWIN_WEIGHTS = ['w_ada', 'b_ada', 'g_norm_mix', 'w_in', 'w_conv_dw', 'b_conv_dw', 'ln_conv_g', 'ln_conv_b', 'g_conv_out', 'g_attn_out', 'w_out', 'g_norm_ffn', 'w_up', 'w_ffn_dw', 'b_ffn_dw', 'w_down', 'g_final']
TWIN_DIFF_INPUT = 'x'
TWIN_INPUTS = ['x', 'c', 'w_ada', 'b_ada', 'g_norm_mix', 'w_in', 'w_conv_dw', 'b_conv_dw', 'ln_conv_g', 'ln_conv_b', 'g_conv_out', 'g_attn_out', 'w_out', 'g_norm_ffn', 'w_up', 'w_ffn_dw', 'b_ffn_dw', 'w_down', 'g_final', 'loss_target', 'm_w_ada', 'm_b_ada', 'm_g_norm_mix', 'm_w_in', 'm_w_conv_dw', 'm_b_conv_dw', 'm_ln_conv_g', 'm_ln_conv_b', 'm_g_conv_out', 'm_g_attn_out', 'm_w_out', 'm_g_norm_ffn', 'm_w_up', 'm_w_ffn_dw', 'm_b_ffn_dw', 'm_w_down', 'm_g_final', 'v_w_ada', 'v_b_ada', 'v_g_norm_mix', 'v_w_in', 'v_w_conv_dw', 'v_b_conv_dw', 'v_ln_conv_g', 'v_ln_conv_b', 'v_g_conv_out', 'v_g_attn_out', 'v_w_out', 'v_g_norm_ffn', 'v_w_up', 'v_w_ffn_dw', 'v_b_ffn_dw', 'v_w_down', 'v_g_final']
TWIN_OUTPUTS = ['loss', 'grad_x', 'grad_w_ada', 'grad_b_ada', 'grad_g_norm_mix', 'grad_w_in', 'grad_w_conv_dw', 'grad_b_conv_dw', 'grad_ln_conv_g', 'grad_ln_conv_b', 'grad_g_conv_out', 'grad_g_attn_out', 'grad_w_out', 'grad_g_norm_ffn', 'grad_w_up', 'grad_w_ffn_dw', 'grad_b_ffn_dw', 'grad_w_down', 'grad_g_final', 'delta_w_ada', 'delta_b_ada', 'delta_g_norm_mix', 'delta_w_in', 'delta_w_conv_dw', 'delta_b_conv_dw', 'delta_ln_conv_g', 'delta_ln_conv_b', 'delta_g_conv_out', 'delta_g_attn_out', 'delta_w_out', 'delta_g_norm_ffn', 'delta_w_up', 'delta_w_ffn_dw', 'delta_b_ffn_dw', 'delta_w_down', 'delta_g_final', 'new_m_w_ada', 'new_m_b_ada', 'new_m_g_norm_mix', 'new_m_w_in', 'new_m_w_conv_dw', 'new_m_b_conv_dw', 'new_m_ln_conv_g', 'new_m_ln_conv_b', 'new_m_g_conv_out', 'new_m_g_attn_out', 'new_m_w_out', 'new_m_g_norm_ffn', 'new_m_w_up', 'new_m_w_ffn_dw', 'new_m_b_ffn_dw', 'new_m_w_down', 'new_m_g_final', 'new_v_w_ada', 'new_v_b_ada', 'new_v_g_norm_mix', 'new_v_w_in', 'new_v_w_conv_dw', 'new_v_b_conv_dw', 'new_v_ln_conv_g', 'new_v_ln_conv_b', 'new_v_g_conv_out', 'new_v_g_attn_out', 'new_v_w_out', 'new_v_g_norm_ffn', 'new_v_w_up', 'new_v_w_ffn_dw', 'new_v_b_ffn_dw', 'new_v_w_down', 'new_v_g_final']
TWIN_LEAF_KINDS = {'loss': 'loss', 'grad_x': 'grad_x', 'grad_w_ada': 'grad_w', 'grad_b_ada': 'grad_w', 'grad_g_norm_mix': 'grad_w', 'grad_w_in': 'grad_w', 'grad_w_conv_dw': 'grad_w', 'grad_b_conv_dw': 'grad_w', 'grad_ln_conv_g': 'grad_w', 'grad_ln_conv_b': 'grad_w', 'grad_g_conv_out': 'grad_w', 'grad_g_attn_out': 'grad_w', 'grad_w_out': 'grad_w', 'grad_g_norm_ffn': 'grad_w', 'grad_w_up': 'grad_w', 'grad_w_ffn_dw': 'grad_w', 'grad_b_ffn_dw': 'grad_w', 'grad_w_down': 'grad_w', 'grad_g_final': 'grad_w', 'delta_w_ada': 'delta_w', 'delta_b_ada': 'delta_w', 'delta_g_norm_mix': 'delta_w', 'delta_w_in': 'delta_w', 'delta_w_conv_dw': 'delta_w', 'delta_b_conv_dw': 'delta_w', 'delta_ln_conv_g': 'delta_w', 'delta_ln_conv_b': 'delta_w', 'delta_g_conv_out': 'delta_w', 'delta_g_attn_out': 'delta_w', 'delta_w_out': 'delta_w', 'delta_g_norm_ffn': 'delta_w', 'delta_w_up': 'delta_w', 'delta_w_ffn_dw': 'delta_w', 'delta_b_ffn_dw': 'delta_w', 'delta_w_down': 'delta_w', 'delta_g_final': 'delta_w', 'new_m_w_ada': 'new_m', 'new_m_b_ada': 'new_m', 'new_m_g_norm_mix': 'new_m', 'new_m_w_in': 'new_m', 'new_m_w_conv_dw': 'new_m', 'new_m_b_conv_dw': 'new_m', 'new_m_ln_conv_g': 'new_m', 'new_m_ln_conv_b': 'new_m', 'new_m_g_conv_out': 'new_m', 'new_m_g_attn_out': 'new_m', 'new_m_w_out': 'new_m', 'new_m_g_norm_ffn': 'new_m', 'new_m_w_up': 'new_m', 'new_m_w_ffn_dw': 'new_m', 'new_m_b_ffn_dw': 'new_m', 'new_m_w_down': 'new_m', 'new_m_g_final': 'new_m', 'new_v_w_ada': 'new_v', 'new_v_b_ada': 'new_v', 'new_v_g_norm_mix': 'new_v', 'new_v_w_in': 'new_v', 'new_v_w_conv_dw': 'new_v', 'new_v_b_conv_dw': 'new_v', 'new_v_ln_conv_g': 'new_v', 'new_v_ln_conv_b': 'new_v', 'new_v_g_conv_out': 'new_v', 'new_v_g_attn_out': 'new_v', 'new_v_w_out': 'new_v', 'new_v_g_norm_ffn': 'new_v', 'new_v_w_up': 'new_v', 'new_v_w_ffn_dw': 'new_v', 'new_v_b_ffn_dw': 'new_v', 'new_v_w_down': 'new_v', 'new_v_g_final': 'new_v'}


def _forward(args):
    return _fwd_reference(*[args[k] for k in FWD_PARAMS])


def _output_shape():
    out = _jax.eval_shape(lambda: _forward(_fwd_setup_inputs(0)))
    return out.shape, out.dtype

N_MICROBATCH = 1
ADAM_LR = 0.001
ADAM_B1 = 0.9
ADAM_B2 = 0.999
ADAM_EPS = 1e-08
ADAM_WD = 0.01
ADAM_STEP = 10
PER_EXAMPLE_BATCH_AXIS = {'x': 0, 'c': 0, 'loss_target': 0}
SHARED_INPUTS = []
_WEIGHT_DTYPES = {'w_ada': _jnp.float32, 'b_ada': _jnp.float32, 'g_norm_mix': _jnp.float32, 'w_in': _jnp.float32, 'w_conv_dw': _jnp.float32, 'b_conv_dw': _jnp.float32, 'ln_conv_g': _jnp.float32, 'ln_conv_b': _jnp.float32, 'g_conv_out': _jnp.float32, 'g_attn_out': _jnp.float32, 'w_out': _jnp.float32, 'g_norm_ffn': _jnp.float32, 'w_up': _jnp.float32, 'w_ffn_dw': _jnp.float32, 'b_ffn_dw': _jnp.float32, 'w_down': _jnp.float32, 'g_final': _jnp.float32}
MOMENT_SCALE = {'w_ada': 5.127508e-02, 'b_ada': 8.323274e-02, 'g_norm_mix': 3.945918e-02, 'w_in': 2.867183e-02, 'w_conv_dw': 3.903863e-02, 'b_conv_dw': 7.427029e-02, 'ln_conv_g': 4.666490e-02, 'ln_conv_b': 4.496117e-02, 'g_conv_out': 3.968531e-02, 'g_attn_out': 4.198091e-02, 'w_out': 3.973382e-02, 'g_norm_ffn': 3.763199e-02, 'w_up': 1.655782e-02, 'w_ffn_dw': 1.637403e-02, 'b_ffn_dw': 1.482449e-02, 'w_down': 2.713413e-02, 'g_final': 1.606962e+01}


def _to_microbatches(a, axis):
    t = _jnp.moveaxis(a, axis, 0)
    t = t.reshape((N_MICROBATCH, t.shape[0] // N_MICROBATCH) + t.shape[1:])
    return _jnp.moveaxis(t, 1, axis + 1)


def setup_inputs(seed: int = 0) -> dict:
    inp = _fwd_setup_inputs(seed)
    key = _jax.random.fold_in(_jax.random.key(seed), 7919)
    shape, _ = _output_shape()
    out = dict(inp)
    out["loss_target"] = _jax.random.normal(_jax.random.fold_in(key, 0), shape, _jnp.float32)
    for i, name in enumerate(TWIN_WEIGHTS):
        w = inp[name].astype(_jnp.float32)
        if MOMENT_SCALE is None:
            s = _jnp.sqrt(_jnp.mean(_jnp.square(w)) + 1e-30)
        else:
            s = MOMENT_SCALE[name]
        km, kv = _jax.random.split(_jax.random.fold_in(key, i + 1))
        out[name] = w
        out["m_" + name] = s * _jax.random.normal(km, w.shape, _jnp.float32)
        out["v_" + name] = (s * s) * _jax.random.uniform(kv, w.shape, _jnp.float32, 0.5, 1.5)
    if N_MICROBATCH > 1:
        for name, axis in PER_EXAMPLE_BATCH_AXIS.items():
            out[name] = _to_microbatches(out[name], axis)
    return {'x': out['x'], 'c': out['c'], 'w_ada': out['w_ada'], 'b_ada': out['b_ada'], 'g_norm_mix': out['g_norm_mix'], 'w_in': out['w_in'], 'w_conv_dw': out['w_conv_dw'], 'b_conv_dw': out['b_conv_dw'], 'ln_conv_g': out['ln_conv_g'], 'ln_conv_b': out['ln_conv_b'], 'g_conv_out': out['g_conv_out'], 'g_attn_out': out['g_attn_out'], 'w_out': out['w_out'], 'g_norm_ffn': out['g_norm_ffn'], 'w_up': out['w_up'], 'w_ffn_dw': out['w_ffn_dw'], 'b_ffn_dw': out['b_ffn_dw'], 'w_down': out['w_down'], 'g_final': out['g_final'], 'loss_target': out['loss_target'], 'm_w_ada': out['m_w_ada'], 'm_b_ada': out['m_b_ada'], 'm_g_norm_mix': out['m_g_norm_mix'], 'm_w_in': out['m_w_in'], 'm_w_conv_dw': out['m_w_conv_dw'], 'm_b_conv_dw': out['m_b_conv_dw'], 'm_ln_conv_g': out['m_ln_conv_g'], 'm_ln_conv_b': out['m_ln_conv_b'], 'm_g_conv_out': out['m_g_conv_out'], 'm_g_attn_out': out['m_g_attn_out'], 'm_w_out': out['m_w_out'], 'm_g_norm_ffn': out['m_g_norm_ffn'], 'm_w_up': out['m_w_up'], 'm_w_ffn_dw': out['m_w_ffn_dw'], 'm_b_ffn_dw': out['m_b_ffn_dw'], 'm_w_down': out['m_w_down'], 'm_g_final': out['m_g_final'], 'v_w_ada': out['v_w_ada'], 'v_b_ada': out['v_b_ada'], 'v_g_norm_mix': out['v_g_norm_mix'], 'v_w_in': out['v_w_in'], 'v_w_conv_dw': out['v_w_conv_dw'], 'v_b_conv_dw': out['v_b_conv_dw'], 'v_ln_conv_g': out['v_ln_conv_g'], 'v_ln_conv_b': out['v_ln_conv_b'], 'v_g_conv_out': out['v_g_conv_out'], 'v_g_attn_out': out['v_g_attn_out'], 'v_w_out': out['v_w_out'], 'v_g_norm_ffn': out['v_g_norm_ffn'], 'v_w_up': out['v_w_up'], 'v_w_ffn_dw': out['v_w_ffn_dw'], 'v_b_ffn_dw': out['v_b_ffn_dw'], 'v_w_down': out['v_w_down'], 'v_g_final': out['v_g_final']}


def _loss(weights, diff, rest, loss_target):
    with _jax.named_scope("forward"):
        args = {**rest, TWIN_DIFF_INPUT: diff, **{k: w.astype(_WEIGHT_DTYPES[k]) for k, w in weights.items()}}
        y = _forward(args)
    with _jax.named_scope("loss_head"):
        err = _jnp.square(y.astype(_jnp.float32) - loss_target)
        return 0.5 * _jnp.sum(_jnp.mean(err, axis=-1)) if err.ndim else 0.5 * err


def _adamw(w, g, m, v):
    m = ADAM_B1 * m + (1.0 - ADAM_B1) * g
    v = ADAM_B2 * v + (1.0 - ADAM_B2) * _jnp.square(g)
    m_hat = m / (1.0 - ADAM_B1 ** ADAM_STEP)
    v_hat = v / (1.0 - ADAM_B2 ** ADAM_STEP)
    delta = -ADAM_LR * (m_hat / (_jnp.sqrt(v_hat) + ADAM_EPS) + ADAM_WD * w)
    return delta, m, v


def reference(x, c, w_ada, b_ada, g_norm_mix, w_in, w_conv_dw, b_conv_dw, ln_conv_g, ln_conv_b, g_conv_out, g_attn_out, w_out, g_norm_ffn, w_up, w_ffn_dw, b_ffn_dw, w_down, g_final, loss_target, m_w_ada, m_b_ada, m_g_norm_mix, m_w_in, m_w_conv_dw, m_b_conv_dw, m_ln_conv_g, m_ln_conv_b, m_g_conv_out, m_g_attn_out, m_w_out, m_g_norm_ffn, m_w_up, m_w_ffn_dw, m_b_ffn_dw, m_w_down, m_g_final, v_w_ada, v_b_ada, v_g_norm_mix, v_w_in, v_w_conv_dw, v_b_conv_dw, v_ln_conv_g, v_ln_conv_b, v_g_conv_out, v_g_attn_out, v_w_out, v_g_norm_ffn, v_w_up, v_w_ffn_dw, v_b_ffn_dw, v_w_down, v_g_final):
    given = dict(x=x, c=c, w_ada=w_ada, b_ada=b_ada, g_norm_mix=g_norm_mix, w_in=w_in, w_conv_dw=w_conv_dw, b_conv_dw=b_conv_dw, ln_conv_g=ln_conv_g, ln_conv_b=ln_conv_b, g_conv_out=g_conv_out, g_attn_out=g_attn_out, w_out=w_out, g_norm_ffn=g_norm_ffn, w_up=w_up, w_ffn_dw=w_ffn_dw, b_ffn_dw=b_ffn_dw, w_down=w_down, g_final=g_final, loss_target=loss_target, m_w_ada=m_w_ada, m_b_ada=m_b_ada, m_g_norm_mix=m_g_norm_mix, m_w_in=m_w_in, m_w_conv_dw=m_w_conv_dw, m_b_conv_dw=m_b_conv_dw, m_ln_conv_g=m_ln_conv_g, m_ln_conv_b=m_ln_conv_b, m_g_conv_out=m_g_conv_out, m_g_attn_out=m_g_attn_out, m_w_out=m_w_out, m_g_norm_ffn=m_g_norm_ffn, m_w_up=m_w_up, m_w_ffn_dw=m_w_ffn_dw, m_b_ffn_dw=m_b_ffn_dw, m_w_down=m_w_down, m_g_final=m_g_final, v_w_ada=v_w_ada, v_b_ada=v_b_ada, v_g_norm_mix=v_g_norm_mix, v_w_in=v_w_in, v_w_conv_dw=v_w_conv_dw, v_b_conv_dw=v_b_conv_dw, v_ln_conv_g=v_ln_conv_g, v_ln_conv_b=v_ln_conv_b, v_g_conv_out=v_g_conv_out, v_g_attn_out=v_g_attn_out, v_w_out=v_w_out, v_g_norm_ffn=v_g_norm_ffn, v_w_up=v_w_up, v_w_ffn_dw=v_w_ffn_dw, v_b_ffn_dw=v_b_ffn_dw, v_w_down=v_w_down, v_g_final=v_g_final)
    weights = {n: given[n] for n in TWIN_WEIGHTS}
    shared = {n: given[n] for n in SHARED_INPUTS}
    per_example = {n: given[n] for n in ['x', 'c']}
    grad_fn = _jax.value_and_grad(_loss, argnums=(0, 1))

    def one_microbatch(ex, loss_target):
        ex = dict(ex)
        diff = ex.pop(TWIN_DIFF_INPUT)
        return grad_fn(weights, diff, {**shared, **ex}, loss_target)

    if N_MICROBATCH == 1:
        loss, (grad_w, grad_x) = one_microbatch(per_example, given["loss_target"])
    else:
        def body(carry, xs):
            loss_sum, grad_sum = carry
            l_k, (gw_k, gx_k) = one_microbatch(xs[0], xs[1])
            with _jax.named_scope("update"):
                return (loss_sum + l_k, _jax.tree.map(_jnp.add, grad_sum, gw_k)), gx_k

        init = (_jnp.zeros((), _jnp.float32), _jax.tree.map(_jnp.zeros_like, weights))
        (loss, grad_w), grad_x = _jax.lax.scan(body, init, (per_example, given["loss_target"]))
    with _jax.named_scope("update"):
        delta_w, new_m, new_v = {}, {}, {}
        for n in TWIN_WEIGHTS:
            delta_w[n], new_m[n], new_v[n] = _adamw(weights[n], grad_w[n], given["m_" + n], given["v_" + n])
    return (loss, grad_x, *[grad_w[n] for n in TWIN_WEIGHTS], *[delta_w[n] for n in TWIN_WEIGHTS],
            *[new_m[n] for n in TWIN_WEIGHTS], *[new_v[n] for n in TWIN_WEIGHTS])
```

```python
import functools

import jax
import jax.numpy as jnp
from jax import lax
from jax.experimental import pallas as pl
from jax.experimental.pallas import tpu as pltpu

F32 = jnp.float32
BF16 = jnp.bfloat16

N_DEV = 8
SEQ = 2048
D_MODEL = 1024
D_CONV = 512
D_ATTN = 512
HEAD_DIM = 64
CONV_K = 31
D_FF = 2816
FFN_K = 3
D_IN = 2 * D_CONV + 3 * D_ATTN
N_MOD = 6
EPS = 1e-6
ATTN_BLOCK = 128
PATTERNS = ((2048, 1), (512, 4), (128, 16))
NEG = -1e30

ADAM_LR, ADAM_B1, ADAM_B2, ADAM_EPS, ADAM_WD, ADAM_STEP = 0.001, 0.9, 0.999, 1e-08, 0.01, 10

ROWS = 256
CONV_HALO = 32
FFN_HALO = 8
FFN_TN = 1408
VMEM_LIMIT = 56 * 1024 * 1024
PACK_ROWS, PACK_W = 16, 6144


def _cp(*sem):
    return pltpu.CompilerParams(dimension_semantics=sem if sem else None, vmem_limit_bytes=VMEM_LIMIT)


def _sig(x):
    return 1.0 / (1.0 + jnp.exp(-x))


def _rsum(x):
    return jnp.sum(x, axis=0, keepdims=True)


def _mean(x):
    return jnp.mean(x, axis=-1, keepdims=True)


def _acc(ref, val, first):
    @pl.when(first)
    def _():
        ref[...] = val

    @pl.when(jnp.logical_not(first))
    def _():
        ref[...] += val


def matmul(a, b, kind, out_dtype, tm, tn, name):
    if kind == "nn":
        (m, k), n = a.shape, b.shape[1]
        a_spec = pl.BlockSpec((tm, k), lambda j, i: (i, 0))
        b_spec = pl.BlockSpec((k, tn), lambda j, i: (0, j))
        dims = (((1,), (0,)), ((), ()))
    elif kind == "nt":
        (m, k), n = a.shape, b.shape[0]
        a_spec = pl.BlockSpec((tm, k), lambda j, i: (i, 0))
        b_spec = pl.BlockSpec((tn, k), lambda j, i: (j, 0))
        dims = (((1,), (1,)), ((), ()))
    else:
        (k, m), n = a.shape, b.shape[1]
        a_spec = pl.BlockSpec((k, tm), lambda j, i: (0, i))
        b_spec = pl.BlockSpec((k, tn), lambda j, i: (0, j))
        dims = (((0,), (0,)), ((), ()))
    assert m % tm == 0 and n % tn == 0, (name, m, n, tm, tn)

    def body(a_ref, b_ref, o_ref):
        o_ref[...] = lax.dot_general(a_ref[...], b_ref[...], dims, preferred_element_type=F32).astype(o_ref.dtype)

    return pl.pallas_call(
        body, out_shape=jax.ShapeDtypeStruct((m, n), out_dtype), grid=(n // tn, m // tm),
        in_specs=[a_spec, b_spec], out_specs=pl.BlockSpec((tm, tn), lambda j, i: (i, j)),
        name=name, compiler_params=_cp("parallel", "parallel"))(a, b)


def _row_spec(width, col=0):
    return pl.BlockSpec((ROWS, width), lambda i: (i, col))


def _vec_spec(width, rows=1):
    return pl.BlockSpec((rows, width), lambda i: (0, 0))


def rms_mod_fwd(x, g, mod, sh_row, sc_row, name):
    def body(x_ref, g_ref, mod_ref, h_ref):
        xx = x_ref[...]
        r = lax.rsqrt(_mean(xx * xx) + EPS)
        h = xx * r * g_ref[...]
        h_ref[...] = (h * (1.0 + mod_ref[sc_row:sc_row + 1, :]) + mod_ref[sh_row:sh_row + 1, :]).astype(BF16)

    return pl.pallas_call(
        body, out_shape=jax.ShapeDtypeStruct((SEQ, D_MODEL), BF16), grid=(SEQ // ROWS,),
        in_specs=[_row_spec(D_MODEL), _vec_spec(D_MODEL), _vec_spec(D_MODEL, 8)],
        out_specs=_row_spec(D_MODEL), name=name, compiler_params=_cp("parallel"))(x, g, mod)


def resid_rms_mod_fwd(x, y, g, mod, ga_row, sh_row, sc_row, name):
    def body(x_ref, y_ref, g_ref, mod_ref, x1_ref, h_ref):
        x1 = x_ref[...] + mod_ref[ga_row:ga_row + 1, :] * y_ref[...]
        x1_ref[...] = x1
        r = lax.rsqrt(_mean(x1 * x1) + EPS)
        h = x1 * r * g_ref[...]
        h_ref[...] = (h * (1.0 + mod_ref[sc_row:sc_row + 1, :]) + mod_ref[sh_row:sh_row + 1, :]).astype(BF16)

    return pl.pallas_call(
        body, out_shape=(jax.ShapeDtypeStruct((SEQ, D_MODEL), F32), jax.ShapeDtypeStruct((SEQ, D_MODEL), BF16)),
        grid=(SEQ // ROWS,),
        in_specs=[_row_spec(D_MODEL), _row_spec(D_MODEL), _vec_spec(D_MODEL), _vec_spec(D_MODEL, 8)],
        out_specs=(_row_spec(D_MODEL), _row_spec(D_MODEL)), name=name, compiler_params=_cp("parallel"))(x, y, g, mod)


def final_loss_bwd(x1, y2, tgt, g, mod, ga_row, name):
    def body(x1_ref, y2_ref, t_ref, g_ref, mod_ref, loss_ref, dx2_ref, dy2_ref, dg_ref, dga_ref):
        first = pl.program_id(0) == 0
        ga = mod_ref[ga_row:ga_row + 1, :]
        y2 = y2_ref[...]
        x2 = x1_ref[...] + ga * y2
        r = lax.rsqrt(_mean(x2 * x2) + EPS)
        xn = x2 * r
        err = xn * g_ref[...] - t_ref[...]
        _acc(loss_ref, jnp.broadcast_to(0.5 * jnp.sum(_mean(err * err)), (8, 128)), first)
        dy = err * (1.0 / D_MODEL)
        _acc(dg_ref, _rsum(dy * xn), first)
        dxn = dy * g_ref[...]
        dx2 = r * (dxn - xn * _mean(dxn * xn))
        dx2_ref[...] = dx2
        dy2_ref[...] = (dx2 * ga).astype(BF16)
        _acc(dga_ref, _rsum(dx2 * y2), first)

    vec = jax.ShapeDtypeStruct((1, D_MODEL), F32)
    return pl.pallas_call(
        body,
        out_shape=(jax.ShapeDtypeStruct((8, 128), F32), jax.ShapeDtypeStruct((SEQ, D_MODEL), F32),
                   jax.ShapeDtypeStruct((SEQ, D_MODEL), BF16), vec, vec),
        grid=(SEQ // ROWS,),
        in_specs=[_row_spec(D_MODEL), _row_spec(D_MODEL), _row_spec(D_MODEL), _vec_spec(D_MODEL), _vec_spec(D_MODEL, 8)],
        out_specs=(pl.BlockSpec((8, 128), lambda i: (0, 0)), _row_spec(D_MODEL), _row_spec(D_MODEL),
                   _vec_spec(D_MODEL), _vec_spec(D_MODEL)),
        name=name, compiler_params=_cp("arbitrary"))(x1, y2, tgt, g, mod)


def rms_mod_bwd(x, dh, dres, g, mod, sc_row, y, ga_row, name):
    gated = y is not None

    def body(*refs):
        if gated:
            x_ref, dh_ref, dres_ref, g_ref, mod_ref, y_ref, dx_ref, dsh_ref, dsc_ref, dg_ref, dy_ref, dga_ref = refs
        else:
            x_ref, dh_ref, dres_ref, g_ref, mod_ref, dx_ref, dsh_ref, dsc_ref, dg_ref = refs
        first = pl.program_id(0) == 0
        xx = x_ref[...]
        dh = dh_ref[...]
        gg = g_ref[...]
        r = lax.rsqrt(_mean(xx * xx) + EPS)
        xn = xx * r
        _acc(dsh_ref, _rsum(dh), first)
        _acc(dsc_ref, _rsum(dh * (xn * gg)), first)
        dt = dh * (1.0 + mod_ref[sc_row:sc_row + 1, :])
        _acc(dg_ref, _rsum(dt * xn), first)
        dxn = dt * gg
        dx = dres_ref[...] + r * (dxn - xn * _mean(dxn * xn))
        dx_ref[...] = dx
        if gated:
            _acc(dga_ref, _rsum(dx * y_ref[...]), first)
            dy_ref[...] = (dx * mod_ref[ga_row:ga_row + 1, :]).astype(BF16)

    vec = jax.ShapeDtypeStruct((1, D_MODEL), F32)
    in_specs = [_row_spec(D_MODEL), _row_spec(D_MODEL), _row_spec(D_MODEL), _vec_spec(D_MODEL), _vec_spec(D_MODEL, 8)]
    out_shape = [jax.ShapeDtypeStruct((SEQ, D_MODEL), F32), vec, vec, vec]
    out_specs = [_row_spec(D_MODEL), _vec_spec(D_MODEL), _vec_spec(D_MODEL), _vec_spec(D_MODEL)]
    args = [x, dh, dres, g, mod]
    if gated:
        in_specs.append(_row_spec(D_MODEL))
        out_shape += [jax.ShapeDtypeStruct((SEQ, D_MODEL), BF16), vec]
        out_specs += [_row_spec(D_MODEL), _vec_spec(D_MODEL)]
        args.append(y)
    return pl.pallas_call(
        body, out_shape=tuple(out_shape), grid=(SEQ // ROWS,), in_specs=in_specs, out_specs=tuple(out_specs),
        name=name, compiler_params=_cp("arbitrary"))(*args)


def _prev_halo(halo, width, col):
    per = ROWS // halo
    return pl.BlockSpec((halo, width), lambda i: (jnp.maximum(i * per - 1, 0), col))


def _next_halo(halo, width, col):
    per = ROWS // halo
    last = SEQ // halo - 1
    return pl.BlockSpec((halo, width), lambda i: (jnp.minimum((i + 1) * per, last), col))


def _conv_module_forward(av_ref, ag_ref, avh_ref, agh_ref, wc_ref, bc_ref, lg_ref, lb_ref, u0p):
    i = pl.program_id(0)
    hv = avh_ref[...] * _sig(agh_ref[...])
    u0p[0:CONV_HALO, :] = jnp.where(i > 0, hv, 0.0)
    u0p[CONV_HALO:, :] = av_ref[...] * _sig(ag_ref[...])
    u1 = jnp.broadcast_to(bc_ref[...], (ROWS, D_CONV))
    for j in range(CONV_K):
        u1 = u1 + wc_ref[j:j + 1, :] * u0p[pl.ds(CONV_HALO - (CONV_K - 1) + j, ROWS), :]
    mu = _mean(u1)
    cen = u1 - mu
    rs = lax.rsqrt(_mean(cen * cen) + EPS)
    z = cen * rs
    ln = z * lg_ref[...] + lb_ref[...]
    s = _sig(ln)
    return z, rs, ln, s, ln * s


def conv_module_fwd(proj, wc, bc, lg, lb, gco, name):
    def body(av_ref, ag_ref, avh_ref, agh_ref, wc_ref, bc_ref, lg_ref, lb_ref, gco_ref, out_ref, u0p):
        _, _, _, _, u2 = _conv_module_forward(av_ref, ag_ref, avh_ref, agh_ref, wc_ref, bc_ref, lg_ref, lb_ref, u0p)
        rc = lax.rsqrt(_mean(u2 * u2) + EPS)
        out_ref[...] = (u2 * rc * gco_ref[...]).astype(BF16)

    v = _vec_spec(D_CONV)
    return pl.pallas_call(
        body, out_shape=jax.ShapeDtypeStruct((SEQ, D_CONV), BF16), grid=(SEQ // ROWS,),
        in_specs=[_row_spec(D_CONV, 0), _row_spec(D_CONV, 1), _prev_halo(CONV_HALO, D_CONV, 0),
                  _prev_halo(CONV_HALO, D_CONV, 1), _vec_spec(D_CONV, CONV_K), v, v, v, v],
        out_specs=_row_spec(D_CONV), scratch_shapes=[pltpu.VMEM((ROWS + CONV_HALO, D_CONV), F32)],
        name=name, compiler_params=_cp("parallel"))(proj, proj, proj, proj, wc, bc, lg, lb, gco)


def conv_module_bwd_a(proj, dmixed, wc, bc, lg, lb, gco, name):
    def body(av_ref, ag_ref, avh_ref, agh_ref, dm_ref, wc_ref, bc_ref, lg_ref, lb_ref, gco_ref,
             du1_ref, dgco_ref, dlg_ref, dlb_ref, dbc_ref, dwc_ref, u0p):
        first = pl.program_id(0) == 0
        z, rs, ln, s, u2 = _conv_module_forward(av_ref, ag_ref, avh_ref, agh_ref, wc_ref, bc_ref, lg_ref, lb_ref, u0p)
        rc = lax.rsqrt(_mean(u2 * u2) + EPS)
        xn = u2 * rc
        dm = dm_ref[...]
        _acc(dgco_ref, _rsum(dm * xn), first)
        dyn = dm * gco_ref[...]
        du2 = rc * (dyn - xn * _mean(dyn * xn))
        dln = du2 * (s * (1.0 + ln * (1.0 - s)))
        _acc(dlg_ref, _rsum(dln * z), first)
        _acc(dlb_ref, _rsum(dln), first)
        dz = dln * lg_ref[...]
        du1 = rs * (dz - _mean(dz) - z * _mean(dz * z))
        du1_ref[...] = du1
        _acc(dbc_ref, _rsum(du1), first)

        @pl.when(first)
        def _():
            dwc_ref[...] = jnp.zeros_like(dwc_ref)

        for j in range(CONV_K):
            dwc_ref[j:j + 1, :] += _rsum(du1 * u0p[pl.ds(CONV_HALO - (CONV_K - 1) + j, ROWS), :])

    v = _vec_spec(D_CONV)
    vec = jax.ShapeDtypeStruct((1, D_CONV), F32)
    return pl.pallas_call(
        body,
        out_shape=(jax.ShapeDtypeStruct((SEQ, D_CONV), F32), vec, vec, vec, vec, jax.ShapeDtypeStruct((CONV_K, D_CONV), F32)),
        grid=(SEQ // ROWS,),
        in_specs=[_row_spec(D_CONV, 0), _row_spec(D_CONV, 1), _prev_halo(CONV_HALO, D_CONV, 0),
                  _prev_halo(CONV_HALO, D_CONV, 1), _row_spec(D_CONV, 0), _vec_spec(D_CONV, CONV_K), v, v, v, v],
        out_specs=(_row_spec(D_CONV), v, v, v, v, _vec_spec(D_CONV, CONV_K)),
        scratch_shapes=[pltpu.VMEM((ROWS + CONV_HALO, D_CONV), F32)],
        name=name, compiler_params=_cp("arbitrary"))(proj, proj, proj, proj, dmixed, wc, bc, lg, lb, gco)


def conv_module_bwd_b(proj, du1, wc, name):
    def body(av_ref, ag_ref, du1_ref, du1n_ref, wc_ref, out_ref, dup):
        i = pl.program_id(0)
        dup[0:ROWS, :] = du1_ref[...]
        dup[ROWS:, :] = jnp.where(i < SEQ // ROWS - 1, du1n_ref[...], 0.0)
        du0 = jnp.zeros((ROWS, D_CONV), F32)
        for j in range(CONV_K):
            du0 = du0 + wc_ref[j:j + 1, :] * dup[pl.ds(CONV_K - 1 - j, ROWS), :]
        sg = _sig(ag_ref[...])
        out_ref[:, 0:D_CONV] = (du0 * sg).astype(BF16)
        out_ref[:, D_CONV:] = (du0 * av_ref[...] * sg * (1.0 - sg)).astype(BF16)

    return pl.pallas_call(
        body, out_shape=jax.ShapeDtypeStruct((SEQ, 2 * D_CONV), BF16), grid=(SEQ // ROWS,),
        in_specs=[_row_spec(D_CONV, 0), _row_spec(D_CONV, 1), _row_spec(D_CONV, 0), _next_halo(CONV_HALO, D_CONV, 0),
                  _vec_spec(D_CONV, CONV_K)],
        out_specs=_row_spec(2 * D_CONV), scratch_shapes=[pltpu.VMEM((ROWS + CONV_HALO, D_CONV), F32)],
        name=name, compiler_params=_cp("parallel"))(proj, proj, du1, du1, wc)


def _attn_specs(sub_len, r):
    q = pl.BlockSpec((sub_len, 128), lambda rho, hp: (0, rho * 12 + hp))
    k = pl.BlockSpec((sub_len, 128), lambda rho, hp: (0, rho * 12 + 4 + hp))
    v = pl.BlockSpec((sub_len, 128), lambda rho, hp: (0, rho * 12 + 8 + hp))
    o = pl.BlockSpec((sub_len, 128), lambda rho, hp: (0, rho * 4 + hp))
    return q, k, v, o


def _attn_block(q_ref, k_ref, v_ref, n, hs, win):
    q0 = pl.multiple_of(n * ATTN_BLOCK, ATTN_BLOCK)
    k0 = pl.multiple_of(jnp.maximum(n - 1, 0) * ATTN_BLOCK, ATTN_BLOCK)
    qb = q_ref[pl.ds(q0, ATTN_BLOCK), hs]
    kw = k_ref[pl.ds(k0, win), hs]
    vw = v_ref[pl.ds(k0, win), hs]
    s = lax.dot_general(qb, kw, (((1,), (1,)), ((), ())), preferred_element_type=F32) * (HEAD_DIM ** -0.5)
    dist = (q0 - k0) + lax.broadcasted_iota(jnp.int32, (ATTN_BLOCK, win), 0) \
        - lax.broadcasted_iota(jnp.int32, (ATTN_BLOCK, win), 1)
    s = jnp.where((dist >= 0) & (dist <= ATTN_BLOCK), s, NEG)
    return q0, k0, qb, kw, vw, s


def attn_fwd(qkv_r, sub_len, r, name):
    nb = sub_len // ATTN_BLOCK
    win = 2 * ATTN_BLOCK if nb > 1 else ATTN_BLOCK

    def body(q_ref, k_ref, v_ref, o_ref, l_ref):
        for h in range(2):
            hs = slice(h * HEAD_DIM, (h + 1) * HEAD_DIM)

            def block(n, carry):
                q0, _, _, _, vw, s = _attn_block(q_ref, k_ref, v_ref, n, hs, win)
                m = jnp.max(s, axis=1, keepdims=True)
                p = jnp.exp(s - m)
                den = jnp.sum(p, axis=1, keepdims=True)
                o = jnp.dot(p.astype(BF16), vw, preferred_element_type=F32) / den
                o_ref[pl.ds(q0, ATTN_BLOCK), hs] = o
                l_ref[pl.ds(q0, ATTN_BLOCK), hs] = jnp.broadcast_to(m + jnp.log(den), (ATTN_BLOCK, HEAD_DIM))
                return carry

            lax.fori_loop(0, nb, block, 0)

    q, k, v, o = _attn_specs(sub_len, r)
    shp = jax.ShapeDtypeStruct((sub_len, r * D_ATTN), F32)
    return pl.pallas_call(
        body, out_shape=(shp, shp), grid=(r, 4), in_specs=[q, k, v], out_specs=(o, o),
        name=name, compiler_params=_cp("parallel", "parallel"))(qkv_r, qkv_r, qkv_r)


def attn_bwd(qkv_r, do_r, lse_r, dd_r, sub_len, r, name):
    nb = sub_len // ATTN_BLOCK
    win = 2 * ATTN_BLOCK if nb > 1 else ATTN_BLOCK

    def body(q_ref, k_ref, v_ref, do_ref, l_ref, dd_ref, dq_ref, dk_ref, dv_ref):
        dk_ref[...] = jnp.zeros_like(dk_ref)
        dv_ref[...] = jnp.zeros_like(dv_ref)
        for h in range(2):
            hs = slice(h * HEAD_DIM, (h + 1) * HEAD_DIM)
            h1 = slice(h * HEAD_DIM, h * HEAD_DIM + 1)

            def block(n, carry):
                q0, k0, qb, kw, vw, s = _attn_block(q_ref, k_ref, v_ref, n, hs, win)
                dob = do_ref[pl.ds(q0, ATTN_BLOCK), hs]
                p = jnp.exp(s - l_ref[pl.ds(q0, ATTN_BLOCK), h1])
                dp = lax.dot_general(dob, vw, (((1,), (1,)), ((), ())), preferred_element_type=F32)
                ds = (p * (dp - dd_ref[pl.ds(q0, ATTN_BLOCK), h1]) * (HEAD_DIM ** -0.5)).astype(BF16)
                dq_ref[pl.ds(q0, ATTN_BLOCK), hs] = jnp.dot(ds, kw, preferred_element_type=F32)
                dk_ref[pl.ds(k0, win), hs] += lax.dot_general(ds, qb, (((0,), (0,)), ((), ())), preferred_element_type=F32)
                dv_ref[pl.ds(k0, win), hs] += lax.dot_general(p.astype(BF16), dob, (((0,), (0,)), ((), ())),
                                                              preferred_element_type=F32)
                return carry

            lax.fori_loop(0, nb, block, 0)

    q, k, v, o = _attn_specs(sub_len, r)
    shp = jax.ShapeDtypeStruct((sub_len, r * D_ATTN), F32)
    return pl.pallas_call(
        body, out_shape=(shp, shp, shp), grid=(r, 4), in_specs=[q, k, v, o, o, o], out_specs=(o, o, o),
        name=name, compiler_params=_cp("parallel", "parallel"))(qkv_r, qkv_r, qkv_r, do_r, lse_r, dd_r)


def attn_combine_fwd(outs, lses, gao, name):
    def body(o1, o2, o3, l1, l2, l3, g_ref, att_ref, lse_ref, mix_ref):
        a1, a2, a3 = l1[...], l2[...], l3[...]
        m = jnp.maximum(jnp.maximum(a1, a2), a3)
        w1, w2, w3 = jnp.exp(a1 - m), jnp.exp(a2 - m), jnp.exp(a3 - m)
        den = w1 + w2 + w3
        att = (w1 * o1[...] + w2 * o2[...] + w3 * o3[...]) / den
        att_ref[...] = att
        lse_ref[...] = m + jnp.log(den)
        mix_ref[...] = (att * lax.rsqrt(_mean(att * att) + EPS) * g_ref[...]).astype(BF16)

    rs = _row_spec(D_ATTN)
    f = jax.ShapeDtypeStruct((SEQ, D_ATTN), F32)
    return pl.pallas_call(
        body, out_shape=(f, f, jax.ShapeDtypeStruct((SEQ, D_ATTN), BF16)), grid=(SEQ // ROWS,),
        in_specs=[rs] * 6 + [_vec_spec(D_ATTN)], out_specs=(rs, rs, rs),
        name=name, compiler_params=_cp("parallel"))(*outs, *lses, gao)


def attn_combine_bwd(dmixed, att, gao, name):
    def body(dm_ref, att_ref, g_ref, do_ref, dd_ref, dg_ref):
        first = pl.program_id(0) == 0
        att = att_ref[...]
        r = lax.rsqrt(_mean(att * att) + EPS)
        xn = att * r
        dm = dm_ref[...]
        _acc(dg_ref, _rsum(dm * xn), first)
        dyn = dm * g_ref[...]
        do = r * (dyn - xn * _mean(dyn * xn))
        do_ref[...] = do.astype(BF16)
        same_head = (jnp.right_shift(lax.broadcasted_iota(jnp.int32, (D_ATTN, D_ATTN), 0), 6)
                     == jnp.right_shift(lax.broadcasted_iota(jnp.int32, (D_ATTN, D_ATTN), 1), 6)).astype(F32)
        dd_ref[...] = jnp.dot(do * att, same_head, preferred_element_type=F32, precision=lax.Precision.HIGHEST)

    rs = _row_spec(D_ATTN)
    return pl.pallas_call(
        body,
        out_shape=(jax.ShapeDtypeStruct((SEQ, D_ATTN), BF16), jax.ShapeDtypeStruct((SEQ, D_ATTN), F32),
                   jax.ShapeDtypeStruct((1, D_ATTN), F32)),
        grid=(SEQ // ROWS,), in_specs=[_row_spec(D_ATTN, 1), rs, _vec_spec(D_ATTN)],
        out_specs=(rs, rs, _vec_spec(D_ATTN)), name=name, compiler_params=_cp("arbitrary"))(dmixed, att, gao)


def sum3_bf16(a, b, c, name):
    def body(a_ref, b_ref, c_ref, o_ref):
        o_ref[...] = (a_ref[...] + b_ref[...] + c_ref[...]).astype(BF16)

    w = a.shape[1]
    rs = _row_spec(w)
    return pl.pallas_call(
        body, out_shape=jax.ShapeDtypeStruct(a.shape, BF16), grid=(SEQ // ROWS,), in_specs=[rs, rs, rs], out_specs=rs,
        name=name, compiler_params=_cp("parallel"))(a, b, c)


N_FT = D_FF // FFN_TN


def _ffn_specs():
    per = ROWS // FFN_HALO
    cur_g = pl.BlockSpec((ROWS, FFN_TN), lambda j, i: (i, j))
    cur_v = pl.BlockSpec((ROWS, FFN_TN), lambda j, i: (i, j + N_FT))
    halo_g = pl.BlockSpec((FFN_HALO, FFN_TN), lambda j, i: (jnp.maximum(i * per - 1, 0), j))
    halo_v = pl.BlockSpec((FFN_HALO, FFN_TN), lambda j, i: (jnp.maximum(i * per - 1, 0), j + N_FT))
    w_g = pl.BlockSpec((FFN_K, FFN_TN), lambda j, i: (0, j))
    w_v = pl.BlockSpec((FFN_K, FFN_TN), lambda j, i: (0, j + N_FT))
    b_g = pl.BlockSpec((1, FFN_TN), lambda j, i: (0, j))
    b_v = pl.BlockSpec((1, FFN_TN), lambda j, i: (0, j + N_FT))
    return [cur_g, cur_v, halo_g, halo_v, w_g, w_v, b_g, b_v]


def _ffn_conv(cur_ref, halo_ref, w_ref, b_ref, pad):
    i = pl.program_id(1)
    pad[0:FFN_HALO, :] = jnp.where(i > 0, halo_ref[...], 0.0)
    pad[FFN_HALO:, :] = cur_ref[...]
    up = jnp.broadcast_to(b_ref[...], (ROWS, FFN_TN))
    for j in range(FFN_K):
        up = up + w_ref[j:j + 1, :] * pad[pl.ds(FFN_HALO - (FFN_K - 1) + j, ROWS), :]
    return up


def ffn_act_fwd(up0, wf, bf, name):
    def body(g_ref, v_ref, gh_ref, vh_ref, wg_ref, wv_ref, bg_ref, bv_ref, act_ref, gpad, vpad):
        gate = _ffn_conv(g_ref, gh_ref, wg_ref, bg_ref, gpad)
        val = _ffn_conv(v_ref, vh_ref, wv_ref, bv_ref, vpad)
        act_ref[...] = (gate * _sig(gate) * val).astype(BF16)

    pad = pltpu.VMEM((ROWS + FFN_HALO, FFN_TN), F32)
    return pl.pallas_call(
        body, out_shape=jax.ShapeDtypeStruct((SEQ, D_FF), BF16), grid=(N_FT, SEQ // ROWS),
        in_specs=_ffn_specs(), out_specs=pl.BlockSpec((ROWS, FFN_TN), lambda j, i: (i, j)),
        scratch_shapes=[pad, pad], name=name, compiler_params=_cp("parallel", "parallel"))(up0, up0, up0, up0, wf, wf, bf, bf)


def ffn_bwd_a(up0, dact, wf, bf, name):
    def body(g_ref, v_ref, gh_ref, vh_ref, wg_ref, wv_ref, bg_ref, bv_ref, da_ref,
             dg_ref, dv_ref, dbg_ref, dbv_ref, dwg_ref, dwv_ref, gpad, vpad):
        first = pl.program_id(1) == 0
        gate = _ffn_conv(g_ref, gh_ref, wg_ref, bg_ref, gpad)
        val = _ffn_conv(v_ref, vh_ref, wv_ref, bv_ref, vpad)
        s = _sig(gate)
        da = da_ref[...]
        dgate = da * val * (s * (1.0 + gate * (1.0 - s)))
        dval = da * (gate * s)
        dg_ref[...] = dgate
        dv_ref[...] = dval
        _acc(dbg_ref, _rsum(dgate), first)
        _acc(dbv_ref, _rsum(dval), first)

        @pl.when(first)
        def _():
            dwg_ref[...] = jnp.zeros_like(dwg_ref)
            dwv_ref[...] = jnp.zeros_like(dwv_ref)

        for j in range(FFN_K):
            off = FFN_HALO - (FFN_K - 1) + j
            dwg_ref[j:j + 1, :] += _rsum(dgate * gpad[pl.ds(off, ROWS), :])
            dwv_ref[j:j + 1, :] += _rsum(dval * vpad[pl.ds(off, ROWS), :])

    pad = pltpu.VMEM((ROWS + FFN_HALO, FFN_TN), F32)
    full = jax.ShapeDtypeStruct((SEQ, D_FF), F32)
    vec = jax.ShapeDtypeStruct((1, D_FF), F32)
    taps = jax.ShapeDtypeStruct((FFN_K, D_FF), F32)
    cur = pl.BlockSpec((ROWS, FFN_TN), lambda j, i: (i, j))
    vs = pl.BlockSpec((1, FFN_TN), lambda j, i: (0, j))
    ts = pl.BlockSpec((FFN_K, FFN_TN), lambda j, i: (0, j))
    return pl.pallas_call(
        body, out_shape=(full, full, vec, vec, taps, taps), grid=(N_FT, SEQ // ROWS),
        in_specs=_ffn_specs() + [cur], out_specs=(cur, cur, vs, vs, ts, ts), scratch_shapes=[pad, pad],
        name=name, compiler_params=_cp("parallel", "arbitrary"))(up0, up0, up0, up0, wf, wf, bf, bf, dact)


def ffn_bwd_b(dgate, dval, wf, name):
    per = ROWS // FFN_HALO
    last = SEQ // FFN_HALO - 1

    def body(g_ref, v_ref, gn_ref, vn_ref, w_ref, out_ref, pad):
        j = pl.program_id(0)
        i = pl.program_id(1)

        def run(cur_ref, nxt_ref):
            pad[0:ROWS, :] = cur_ref[...]
            pad[ROWS:, :] = jnp.where(i < SEQ // ROWS - 1, nxt_ref[...], 0.0)
            acc = jnp.zeros((ROWS, FFN_TN), F32)
            for t in range(FFN_K):
                acc = acc + w_ref[t:t + 1, :] * pad[pl.ds(FFN_K - 1 - t, ROWS), :]
            out_ref[...] = acc.astype(BF16)

        @pl.when(j < N_FT)
        def _():
            run(g_ref, gn_ref)

        @pl.when(j >= N_FT)
        def _():
            run(v_ref, vn_ref)

    cur_g = pl.BlockSpec((ROWS, FFN_TN), lambda j, i: (i, jnp.minimum(j, N_FT - 1)))
    cur_v = pl.BlockSpec((ROWS, FFN_TN), lambda j, i: (i, jnp.maximum(j - N_FT, 0)))
    nxt_g = pl.BlockSpec((FFN_HALO, FFN_TN), lambda j, i: (jnp.minimum((i + 1) * per, last), jnp.minimum(j, N_FT - 1)))
    nxt_v = pl.BlockSpec((FFN_HALO, FFN_TN), lambda j, i: (jnp.minimum((i + 1) * per, last), jnp.maximum(j - N_FT, 0)))
    return pl.pallas_call(
        body, out_shape=jax.ShapeDtypeStruct((SEQ, 2 * D_FF), BF16), grid=(2 * N_FT, SEQ // ROWS),
        in_specs=[cur_g, cur_v, nxt_g, nxt_v, pl.BlockSpec((FFN_K, FFN_TN), lambda j, i: (0, j))],
        out_specs=pl.BlockSpec((ROWS, FFN_TN), lambda j, i: (i, j)),
        scratch_shapes=[pltpu.VMEM((ROWS + FFN_HALO, FFN_TN), F32)],
        name=name, compiler_params=_cp("parallel", "parallel"))(dgate, dval, dgate, dval, wf)


def ada_fwd(c_all, w_ada, b_cols, name):
    def body(c_ref, w_ref, b_ref, o_ref):
        cc = c_ref[...]
        sc = (cc * _sig(cc)).astype(BF16)
        o_ref[...] = jnp.dot(sc, w_ref[...].astype(BF16), preferred_element_type=F32) + b_ref[...]

    return pl.pallas_call(body, out_shape=jax.ShapeDtypeStruct((N_DEV, w_ada.shape[1]), F32), name=name,
                          compiler_params=_cp())(c_all, w_ada, b_cols)


def _adam(w, g, m, v):
    m = ADAM_B1 * m + (1.0 - ADAM_B1) * g
    v = ADAM_B2 * v + (1.0 - ADAM_B2) * (g * g)
    m_hat = m / (1.0 - ADAM_B1 ** ADAM_STEP)
    v_hat = v / (1.0 - ADAM_B2 ** ADAM_STEP)
    delta = -ADAM_LR * (m_hat / (jnp.sqrt(v_hat) + ADAM_EPS) + ADAM_WD * w)
    return delta, m, v


def ada_bwd_adamw(c_all_t, dmod_cols, w, m, v, name):
    rows, cols = w.shape
    tr = 256

    def body(ct_ref, dm_ref, w_ref, m_ref, v_ref, g_ref, d_ref, nm_ref, nv_ref):
        ct = ct_ref[...]
        sc = ct * _sig(ct)
        g = sc[:, 0:1] * dm_ref[0:1, :]
        for b in range(1, N_DEV):
            g = g + sc[:, b:b + 1] * dm_ref[b:b + 1, :]
        g_ref[...] = g
        d_ref[...], nm_ref[...], nv_ref[...] = _adam(w_ref[...], g, m_ref[...], v_ref[...])

    blk = pl.BlockSpec((tr, cols), lambda i: (i, 0))
    shp = jax.ShapeDtypeStruct((rows, cols), F32)
    return pl.pallas_call(
        body, out_shape=(shp, shp, shp, shp), grid=(rows // tr,),
        in_specs=[pl.BlockSpec((tr, N_DEV), lambda i: (i, 0)), pl.BlockSpec((N_DEV, cols), lambda i: (0, 0)), blk, blk, blk],
        out_specs=(blk, blk, blk, blk), name=name, compiler_params=_cp("parallel"))(c_all_t, dmod_cols, w, m, v)


def sum_adamw(parts, w, m, v, tr, name):
    n_parts, rows, cols = parts.shape

    def body(p_ref, w_ref, m_ref, v_ref, g_ref, d_ref, nm_ref, nv_ref):
        g = p_ref[0].astype(F32)
        for k in range(1, n_parts):
            g = g + p_ref[k].astype(F32)
        g_ref[...] = g
        d_ref[...], nm_ref[...], nv_ref[...] = _adam(w_ref[...], g, m_ref[...], v_ref[...])

    blk = pl.BlockSpec((tr, cols), lambda i: (i, 0))
    shp = jax.ShapeDtypeStruct((rows, cols), F32)
    return pl.pallas_call(
        body, out_shape=(shp, shp, shp, shp), grid=(rows // tr,),
        in_specs=[pl.BlockSpec((n_parts, tr, cols), lambda i: (0, i, 0)), blk, blk, blk],
        out_specs=(blk, blk, blk, blk), name=name, compiler_params=_cp("parallel"))(parts, w, m, v)


MESH = pl.DeviceIdType.MESH
ANY = pl.BlockSpec(memory_space=pl.ANY)


def all_gather(block, name):
    def body(x_ref, out_ref, send_sems, recv_sems, local_sem):
        x, y, c = lax.axis_index("x"), lax.axis_index("y"), lax.axis_index("c")
        me, sibling = (x, y, c), (x, y, 1 - c)
        chips = [(1 - x, y), (x, 1 - y), (1 - x, 1 - y)]

        def slot(px, py, pc):
            return out_ref.at[4 * px + 2 * py + pc]

        def copy(k, blk, to, src=None):
            return pltpu.make_async_remote_copy(
                src_ref=slot(*blk) if src is None else src, dst_ref=slot(*blk),
                send_sem=send_sems.at[k], recv_sem=recv_sems.at[k], device_id=to, device_id_type=MESH)

        mine = pltpu.make_async_copy(x_ref, slot(*me), local_sem)
        mine.start()
        first = [copy(0, me, sibling, src=x_ref)]
        first += [copy(1 + j, me, (*chip, c), src=x_ref) for j, chip in enumerate(chips)]
        for cp in first:
            cp.start()
        passed = [copy(4 + j, (*chip, c), sibling) for j, chip in enumerate(chips)]
        for j, chip in enumerate(chips):
            copy(1 + j, (*chip, c), me).wait_recv()
            passed[j].start()
        copy(0, sibling, me).wait_recv()
        for j, chip in enumerate(chips):
            copy(4 + j, (*chip, 1 - c), me).wait_recv()
        for cp in first + passed:
            cp.wait_send()
        mine.wait()

    return pl.pallas_call(
        body, out_shape=jax.ShapeDtypeStruct((N_DEV,) + block.shape, block.dtype), in_specs=[ANY], out_specs=ANY,
        scratch_shapes=[pltpu.SemaphoreType.DMA((7,)), pltpu.SemaphoreType.DMA((7,)), pltpu.SemaphoreType.DMA],
        name=name)(block)


def all_to_all(parts, name):
    def body(p_ref, out_ref, send_sems, recv_sems, local_sem):
        x, y, c = lax.axis_index("x"), lax.axis_index("y"), lax.axis_index("c")
        me = 4 * x + 2 * y + c
        mine = pltpu.make_async_copy(p_ref.at[me], out_ref.at[0], local_sem)
        mine.start()
        copies = []
        for k in range(1, N_DEV):
            px = 1 - x if k & 4 else x
            py = 1 - y if k & 2 else y
            pc = 1 - c if k & 1 else c
            copies.append(pltpu.make_async_remote_copy(
                src_ref=p_ref.at[4 * px + 2 * py + pc], dst_ref=out_ref.at[k],
                send_sem=send_sems.at[k - 1], recv_sem=recv_sems.at[k - 1], device_id=(px, py, pc), device_id_type=MESH))
        for cp in copies:
            cp.start()
        for cp in copies:
            cp.wait_recv()
        for cp in copies:
            cp.wait_send()
        mine.wait()

    return pl.pallas_call(
        body, out_shape=jax.ShapeDtypeStruct(parts.shape, parts.dtype), in_specs=[ANY], out_specs=ANY,
        scratch_shapes=[pltpu.SemaphoreType.DMA((7,)), pltpu.SemaphoreType.DMA((7,)), pltpu.SemaphoreType.DMA],
        name=name)(parts)


def _to_pattern(a, r):
    return a.reshape(SEQ // r, r * a.shape[1])


def local_step(x, tgt, mod, w_in, w_out, w_up, w_down, wc, wf, g_mix, bc, lg, lb, gco, gao, g_ffn, bf, g_fin):
    h1 = rms_mod_fwd(x, g_mix, mod, 0, 1, "h1_fwd")
    proj = matmul(h1, w_in, "nn", F32, 256, D_IN, "proj_fwd")
    mix_a = conv_module_fwd(proj, wc, bc, lg, lb, gco, "conv_module_fwd")
    qkv = proj[:, 2 * D_CONV:].astype(BF16)
    qkv_r = [_to_pattern(qkv, r) for _, r in PATTERNS]
    outs, lses = [], []
    for (sub_len, r), q_r in zip(PATTERNS, qkv_r):
        o, l = attn_fwd(q_r, sub_len, r, f"attn_fwd_r{r}")
        outs.append(o.reshape(SEQ, D_ATTN))
        lses.append(l.reshape(SEQ, D_ATTN))
    att, lse, mix_b = attn_combine_fwd(outs, lses, gao, "attn_combine_fwd")
    mixed = jnp.concatenate([mix_a, mix_b], axis=1)
    y1 = matmul(mixed, w_out, "nn", F32, 256, D_MODEL, "out_proj_fwd")
    x1, h2 = resid_rms_mod_fwd(x, y1, g_ffn, mod, 2, 3, 4, "x1_h2_fwd")
    up0 = matmul(h2, w_up, "nn", F32, 256, D_FF, "up_fwd")
    act = ffn_act_fwd(up0, wf, bf, "ffn_act_fwd")
    y2 = matmul(act, w_down, "nn", F32, 256, D_MODEL, "down_fwd")
    loss_t, dx2, dy2, d_gfin, d_gaf = final_loss_bwd(x1, y2, tgt, g_fin, mod, 5, "loss_bwd")
    dact = matmul(dy2, w_down, "nt", F32, 256, FFN_TN, "down_bwd_x")
    dw_down = matmul(act, dy2, "tn", BF16, 256, D_MODEL, "down_bwd_w")
    dgate, dval, dbf_g, dbf_v, dwf_g, dwf_v = ffn_bwd_a(up0, dact, wf, bf, "ffn_bwd_a")
    dup0 = ffn_bwd_b(dgate, dval, wf, "ffn_bwd_b")
    dh2 = matmul(dup0, w_up, "nt", F32, 256, 512, "up_bwd_x")
    dw_up = matmul(h2, dup0, "tn", BF16, 256, 512, "up_bwd_w")
    dx1, d_shf, d_scf, d_gffn, dy1, d_gam = rms_mod_bwd(x1, dh2, dx2, g_ffn, mod, 4, y1, 2, "h2_bwd")
    dmixed = matmul(dy1, w_out, "nt", F32, 256, D_MODEL, "out_proj_bwd_x")
    dw_out = matmul(mixed, dy1, "tn", BF16, 256, D_MODEL, "out_proj_bwd_w")
    do, dd, d_gao = attn_combine_bwd(dmixed, att, gao, "attn_combine_bwd")
    dqkv = []
    for (sub_len, r), q_r in zip(PATTERNS, qkv_r):
        dq, dk, dv = attn_bwd(q_r, _to_pattern(do, r), _to_pattern(lse, r), _to_pattern(dd, r), sub_len, r, f"attn_bwd_r{r}")
        dqkv.append(jnp.concatenate([t.reshape(SEQ, D_ATTN) for t in (dq, dk, dv)], axis=1))
    dqkv = sum3_bf16(*dqkv, "dqkv_sum")
    du1, d_gco, d_lg, d_lb, d_bc, d_wc = conv_module_bwd_a(proj, dmixed, wc, bc, lg, lb, gco, "conv_module_bwd_a")
    dproj_a = conv_module_bwd_b(proj, du1, wc, "conv_module_bwd_b")
    dproj = jnp.concatenate([dproj_a, dqkv], axis=1)
    dh1 = matmul(dproj, w_in, "nt", F32, 256, D_MODEL, "proj_bwd_x")
    dw_in = matmul(h1, dproj, "tn", BF16, 256, 512, "proj_bwd_w")
    dx, d_shm, d_scm, d_gmix = rms_mod_bwd(x, dh1, dx1, g_mix, mod, 1, None, 0, "h1_bwd")
    dmod = jnp.concatenate([d_shm, d_scm, d_gam, d_shf, d_scf, d_gaf], axis=1)
    small = dict(g_norm_mix=d_gmix, b_conv_dw=d_bc, ln_conv_g=d_lg, ln_conv_b=d_lb, g_conv_out=d_gco, g_attn_out=d_gao,
                 g_norm_ffn=d_gffn, b_ffn_dw=jnp.concatenate([dbf_g, dbf_v], axis=1), g_final=d_gfin,
                 w_conv_dw=d_wc, w_ffn_dw=jnp.concatenate([dwf_g, dwf_v], axis=1), dmod=dmod)
    return loss_t[0, 0], dx, dict(w_in=dw_in, w_out=dw_out, w_up=dw_up, w_down=dw_down), small


def _padw(a, width):
    return jnp.pad(a, ((0, 0), (0, width - a.shape[1])))


def pack_small(t):
    wide = jnp.concatenate([_padw(t["dmod"], PACK_W), _padw(t["b_ffn_dw"], PACK_W), _padw(t["w_ffn_dw"], PACK_W),
                            jnp.zeros((3, PACK_W), F32)], axis=0)
    z512 = jnp.zeros((1, 512), F32)
    narrow = jnp.concatenate([
        t["g_norm_mix"], t["g_norm_ffn"], t["g_final"],
        jnp.concatenate([t["b_conv_dw"], t["ln_conv_g"]], axis=1),
        jnp.concatenate([t["ln_conv_b"], t["g_conv_out"]], axis=1),
        jnp.concatenate([t["g_attn_out"], z512], axis=1),
        jnp.zeros((2, 1024), F32),
        jnp.pad(t["w_conv_dw"], ((0, 1), (0, 0))).reshape(16, 1024)], axis=0)
    return jnp.concatenate([wide, narrow.reshape(4, PACK_W), jnp.zeros((4, PACK_W), F32)], axis=0)


def unpack_small(p):
    narrow = p[8:12].reshape(24, 1024)
    return dict(
        dmod=p[0:1], b_ffn_dw=p[1:2, :2 * D_FF], w_ffn_dw=p[2:5, :2 * D_FF],
        g_norm_mix=narrow[0:1], g_norm_ffn=narrow[1:2], g_final=narrow[2:3],
        b_conv_dw=narrow[3:4, :512], ln_conv_g=narrow[3:4, 512:], ln_conv_b=narrow[4:5, :512], g_conv_out=narrow[4:5, 512:],
        g_attn_out=narrow[5:6, :512], w_conv_dw=narrow[8:24].reshape(32, 512)[:CONV_K])


def _embed(local, width, me):
    return lax.dynamic_update_slice(jnp.zeros((local.shape[0], width), F32), local, (0, me * local.shape[1]))


def _shard(full, n_cols, me):
    return lax.dynamic_slice(full, (0, me * n_cols), (full.shape[0], n_cols))


WEIGHTS = ["w_ada", "b_ada", "g_norm_mix", "w_in", "w_conv_dw", "b_conv_dw", "ln_conv_g", "ln_conv_b", "g_conv_out",
           "g_attn_out", "w_out", "g_norm_ffn", "w_up", "w_ffn_dw", "b_ffn_dw", "w_down", "g_final"]
SMALL_REPLICATED = ["g_norm_mix", "b_conv_dw", "ln_conv_g", "ln_conv_b", "g_conv_out", "g_attn_out", "g_norm_ffn",
                    "b_ffn_dw", "g_final"]


def kernel(x, c, w_ada, b_ada, g_norm_mix, w_in, w_conv_dw, b_conv_dw, ln_conv_g, ln_conv_b, g_conv_out, g_attn_out, w_out, g_norm_ffn, w_up, w_ffn_dw, b_ffn_dw, w_down, g_final, loss_target, m_w_ada, m_b_ada, m_g_norm_mix, m_w_in, m_w_conv_dw, m_b_conv_dw, m_ln_conv_g, m_ln_conv_b, m_g_conv_out, m_g_attn_out, m_w_out, m_g_norm_ffn, m_w_up, m_w_ffn_dw, m_b_ffn_dw, m_w_down, m_g_final, v_w_ada, v_b_ada, v_g_norm_mix, v_w_in, v_w_conv_dw, v_b_conv_dw, v_ln_conv_g, v_ln_conv_b, v_g_conv_out, v_g_attn_out, v_w_out, v_g_norm_ffn, v_w_up, v_w_ffn_dw, v_b_ffn_dw, v_w_down, v_g_final):
    args = dict(locals())
    me = 4 * lax.axis_index("x") + 2 * lax.axis_index("y") + lax.axis_index("c")

    def flat(name, prefix=""):
        a = args[prefix + name]
        return a.reshape(a.shape[-2] if a.ndim > 1 else 1, a.shape[-1])

    c_all = all_gather(jnp.pad(c, ((0, 7), (0, 0))), "gather_c")[:, 0, :]
    n_ada = w_ada.shape[2]
    mod_cols = ada_fwd(c_all, flat("w_ada"), _shard(flat("b_ada"), n_ada, me), "ada_fwd")
    mod_all = all_gather(mod_cols, "gather_mod")
    mod = lax.dynamic_index_in_dim(mod_all, me, axis=1, keepdims=False).reshape(N_MOD, D_MODEL)
    mod = jnp.pad(mod, ((0, 2), (0, 0)))

    n_in, n_up = w_in.shape[2], w_up.shape[2]
    col_block = jnp.concatenate([flat("w_in"), flat("w_up")], axis=1).astype(BF16)
    row_block = jnp.concatenate([flat("w_out"), flat("w_down")], axis=0).astype(BF16)
    gathered = all_gather(jnp.concatenate([col_block, row_block], axis=0), "gather_weights")
    cols = gathered[:, :D_MODEL, :]
    w_in_full = cols[:, :, :n_in].transpose(1, 0, 2).reshape(D_MODEL, D_IN)
    w_up_full = cols[:, :, n_in:].transpose(1, 0, 2).reshape(D_MODEL, 2 * D_FF)
    rows = gathered[:, D_MODEL:, :]
    r_out = w_out.shape[1]
    w_out_full = rows[:, :r_out, :].reshape(D_MODEL, D_MODEL)
    w_down_full = rows[:, r_out:, :].reshape(D_FF, D_MODEL)
    wc_full = all_gather(jnp.pad(flat("w_conv_dw"), ((0, 1), (0, 0))), "gather_conv_taps")
    wc_full = wc_full.transpose(1, 0, 2).reshape(32, D_CONV)[:CONV_K]
    wf_full = all_gather(jnp.pad(flat("w_ffn_dw"), ((0, 5), (0, 0))), "gather_ffn_taps")
    wf_full = wf_full.transpose(1, 0, 2).reshape(8, 2 * D_FF)[:FFN_K]

    loss_part, grad_x, big, small = local_step(
        x[0], loss_target[0], mod, w_in_full, w_out_full, w_up_full, w_down_full, wc_full, wf_full,
        flat("g_norm_mix"), flat("b_conv_dw"), flat("ln_conv_g"), flat("ln_conv_b"), flat("g_conv_out"),
        flat("g_attn_out"), flat("g_norm_ffn"), flat("b_ffn_dw"), flat("g_final"))
    loss = lax.psum(loss_part, ("x", "y", "c"))

    out = {}
    dev_major = dict(
        w_in=big["w_in"].reshape(D_MODEL, N_DEV, n_in).transpose(1, 0, 2),
        w_up=big["w_up"].reshape(D_MODEL, N_DEV, n_up).transpose(1, 0, 2),
        w_out=big["w_out"].reshape(N_DEV, r_out, D_MODEL),
        w_down=big["w_down"].reshape(N_DEV, w_down.shape[1], D_MODEL))
    for name, tr in (("w_down", 352), ("w_up", 256), ("w_out", 128), ("w_in", 256)):
        parts = all_to_all(dev_major[name], "exchange_" + name)
        out[name] = sum_adamw(parts, flat(name), flat(name, "m_"), flat(name, "v_"), tr, "adamw_" + name)

    small_all = all_gather(pack_small(small), "gather_small")

    def packed(prefix):
        t = {n: flat(n, prefix) for n in SMALL_REPLICATED}
        t["dmod"] = flat("b_ada", prefix)
        t["w_conv_dw"] = _embed(flat("w_conv_dw", prefix), D_CONV, me)
        t["w_ffn_dw"] = _embed(flat("w_ffn_dw", prefix), 2 * D_FF, me)
        return pack_small(t)

    res = sum_adamw(small_all, packed(""), packed("m_"), packed("v_"), PACK_ROWS, "adamw_small")
    res = [unpack_small(r) for r in res]
    for n in SMALL_REPLICATED:
        out[n] = tuple(r[n] for r in res)
    out["b_ada"] = tuple(r["dmod"] for r in res)
    out["w_conv_dw"] = tuple(_shard(r["w_conv_dw"], w_conv_dw.shape[2], me) for r in res)
    out["w_ffn_dw"] = tuple(_shard(r["w_ffn_dw"], w_ffn_dw.shape[2], me) for r in res)

    dmod_cols = _shard(small_all[:, 0, :], n_ada, me)
    out["w_ada"] = ada_bwd_adamw(c_all.T, dmod_cols, flat("w_ada"), flat("w_ada", "m_"), flat("w_ada", "v_"), "adamw_w_ada")

    result = [loss, grad_x[None]]
    for k in range(4):
        result += [out[n][k].reshape(args[n].shape) for n in WEIGHTS]
    return tuple(result)
```

```python
import functools

import jax
import jax.numpy as jnp
from jax import lax
from jax.experimental import pallas as pl
from jax.experimental.pallas import tpu as pltpu

F32 = jnp.float32
BF16 = jnp.bfloat16

N_DEV = 8
SEQ = 2048
D_MODEL = 1024
D_CONV = 512
D_ATTN = 512
HEAD_DIM = 64
CONV_K = 31
D_FF = 2816
FFN_K = 3
D_IN = 2 * D_CONV + 3 * D_ATTN
N_MOD = 6
EPS = 1e-6
ATTN_BLOCK = 128
PATTERNS = ((2048, 1), (512, 4), (128, 16))
NEG = -1e30

ADAM_LR, ADAM_B1, ADAM_B2, ADAM_EPS, ADAM_WD, ADAM_STEP = 0.001, 0.9, 0.999, 1e-08, 0.01, 10

ROWS = 256
CONV_HALO = 32
FFN_HALO = 8
FFN_TN = 1408
VMEM_LIMIT = 56 * 1024 * 1024
PACK_ROWS, PACK_W = 16, 6144


def _cp(*sem):
    return pltpu.CompilerParams(dimension_semantics=sem if sem else None, vmem_limit_bytes=VMEM_LIMIT)


def _sig(x):
    return 1.0 / (1.0 + jnp.exp(-x))


def _rsum(x):
    return jnp.sum(x, axis=0, keepdims=True)


def _mean(x):
    return jnp.mean(x, axis=-1, keepdims=True)


def _acc(ref, val, first):
    @pl.when(first)
    def _():
        ref[...] = val

    @pl.when(jnp.logical_not(first))
    def _():
        ref[...] += val


def matmul(a, b, kind, out_dtype, tm, tn, name):
    if kind == "nn":
        (m, k), n = a.shape, b.shape[1]
        a_spec = pl.BlockSpec((tm, k), lambda j, i: (i, 0))
        b_spec = pl.BlockSpec((k, tn), lambda j, i: (0, j))
        dims = (((1,), (0,)), ((), ()))
    elif kind == "nt":
        (m, k), n = a.shape, b.shape[0]
        a_spec = pl.BlockSpec((tm, k), lambda j, i: (i, 0))
        b_spec = pl.BlockSpec((tn, k), lambda j, i: (j, 0))
        dims = (((1,), (1,)), ((), ()))
    else:
        (k, m), n = a.shape, b.shape[1]
        a_spec = pl.BlockSpec((k, tm), lambda j, i: (0, i))
        b_spec = pl.BlockSpec((k, tn), lambda j, i: (0, j))
        dims = (((0,), (0,)), ((), ()))
    assert m % tm == 0 and n % tn == 0, (name, m, n, tm, tn)

    def body(a_ref, b_ref, o_ref):
        o_ref[...] = lax.dot_general(a_ref[...], b_ref[...], dims, preferred_element_type=F32).astype(o_ref.dtype)

    return pl.pallas_call(
        body, out_shape=jax.ShapeDtypeStruct((m, n), out_dtype), grid=(n // tn, m // tm),
        in_specs=[a_spec, b_spec], out_specs=pl.BlockSpec((tm, tn), lambda j, i: (i, j)),
        name=name, compiler_params=_cp("parallel", "parallel"))(a, b)


def _row_spec(width, col=0):
    return pl.BlockSpec((ROWS, width), lambda i: (i, col))


def _vec_spec(width, rows=1):
    return pl.BlockSpec((rows, width), lambda i: (0, 0))


def rms_mod_fwd(x, g, mod, sh_row, sc_row, name):
    def body(x_ref, g_ref, mod_ref, h_ref):
        xx = x_ref[...]
        r = lax.rsqrt(_mean(xx * xx) + EPS)
        h = xx * r * g_ref[...]
        h_ref[...] = (h * (1.0 + mod_ref[sc_row:sc_row + 1, :]) + mod_ref[sh_row:sh_row + 1, :]).astype(BF16)

    return pl.pallas_call(
        body, out_shape=jax.ShapeDtypeStruct((SEQ, D_MODEL), BF16), grid=(SEQ // ROWS,),
        in_specs=[_row_spec(D_MODEL), _vec_spec(D_MODEL), _vec_spec(D_MODEL, 8)],
        out_specs=_row_spec(D_MODEL), name=name, compiler_params=_cp("parallel"))(x, g, mod)


def resid_rms_mod_fwd(x, y, g, mod, ga_row, sh_row, sc_row, name):
    def body(x_ref, y_ref, g_ref, mod_ref, x1_ref, h_ref):
        x1 = x_ref[...] + mod_ref[ga_row:ga_row + 1, :] * y_ref[...]
        x1_ref[...] = x1
        r = lax.rsqrt(_mean(x1 * x1) + EPS)
        h = x1 * r * g_ref[...]
        h_ref[...] = (h * (1.0 + mod_ref[sc_row:sc_row + 1, :]) + mod_ref[sh_row:sh_row + 1, :]).astype(BF16)

    return pl.pallas_call(
        body, out_shape=(jax.ShapeDtypeStruct((SEQ, D_MODEL), F32), jax.ShapeDtypeStruct((SEQ, D_MODEL), BF16)),
        grid=(SEQ // ROWS,),
        in_specs=[_row_spec(D_MODEL), _row_spec(D_MODEL), _vec_spec(D_MODEL), _vec_spec(D_MODEL, 8)],
        out_specs=(_row_spec(D_MODEL), _row_spec(D_MODEL)), name=name, compiler_params=_cp("parallel"))(x, y, g, mod)


def final_loss_bwd(x1, y2, tgt, g, mod, ga_row, name):
    def body(x1_ref, y2_ref, t_ref, g_ref, mod_ref, loss_ref, dx2_ref, dy2_ref, dg_ref, dga_ref):
        first = pl.program_id(0) == 0
        ga = mod_ref[ga_row:ga_row + 1, :]
        y2 = y2_ref[...]
        x2 = x1_ref[...] + ga * y2
        r = lax.rsqrt(_mean(x2 * x2) + EPS)
        xn = x2 * r
        err = xn * g_ref[...] - t_ref[...]
        _acc(loss_ref, jnp.broadcast_to(0.5 * jnp.sum(_mean(err * err)), (8, 128)), first)
        dy = err * (1.0 / D_MODEL)
        _acc(dg_ref, _rsum(dy * xn), first)
        dxn = dy * g_ref[...]
        dx2 = r * (dxn - xn * _mean(dxn * xn))
        dx2_ref[...] = dx2
        dy2_ref[...] = (dx2 * ga).astype(BF16)
        _acc(dga_ref, _rsum(dx2 * y2), first)

    vec = jax.ShapeDtypeStruct((1, D_MODEL), F32)
    return pl.pallas_call(
        body,
        out_shape=(jax.ShapeDtypeStruct((8, 128), F32), jax.ShapeDtypeStruct((SEQ, D_MODEL), F32),
                   jax.ShapeDtypeStruct((SEQ, D_MODEL), BF16), vec, vec),
        grid=(SEQ // ROWS,),
        in_specs=[_row_spec(D_MODEL), _row_spec(D_MODEL), _row_spec(D_MODEL), _vec_spec(D_MODEL), _vec_spec(D_MODEL, 8)],
        out_specs=(pl.BlockSpec((8, 128), lambda i: (0, 0)), _row_spec(D_MODEL), _row_spec(D_MODEL),
                   _vec_spec(D_MODEL), _vec_spec(D_MODEL)),
        name=name, compiler_params=_cp("arbitrary"))(x1, y2, tgt, g, mod)


def rms_mod_bwd(x, dh, dres, g, mod, sc_row, y, ga_row, name):
    gated = y is not None

    def body(*refs):
        if gated:
            x_ref, dh_ref, dres_ref, g_ref, mod_ref, y_ref, dx_ref, dsh_ref, dsc_ref, dg_ref, dy_ref, dga_ref = refs
        else:
            x_ref, dh_ref, dres_ref, g_ref, mod_ref, dx_ref, dsh_ref, dsc_ref, dg_ref = refs
        first = pl.program_id(0) == 0
        xx = x_ref[...]
        dh = dh_ref[...]
        gg = g_ref[...]
        r = lax.rsqrt(_mean(xx * xx) + EPS)
        xn = xx * r
        _acc(dsh_ref, _rsum(dh), first)
        _acc(dsc_ref, _rsum(dh * (xn * gg)), first)
        dt = dh * (1.0 + mod_ref[sc_row:sc_row + 1, :])
        _acc(dg_ref, _rsum(dt * xn), first)
        dxn = dt * gg
        dx = dres_ref[...] + r * (dxn - xn * _mean(dxn * xn))
        dx_ref[...] = dx
        if gated:
            _acc(dga_ref, _rsum(dx * y_ref[...]), first)
            dy_ref[...] = (dx * mod_ref[ga_row:ga_row + 1, :]).astype(BF16)

    vec = jax.ShapeDtypeStruct((1, D_MODEL), F32)
    in_specs = [_row_spec(D_MODEL), _row_spec(D_MODEL), _row_spec(D_MODEL), _vec_spec(D_MODEL), _vec_spec(D_MODEL, 8)]
    out_shape = [jax.ShapeDtypeStruct((SEQ, D_MODEL), F32), vec, vec, vec]
    out_specs = [_row_spec(D_MODEL), _vec_spec(D_MODEL), _vec_spec(D_MODEL), _vec_spec(D_MODEL)]
    args = [x, dh, dres, g, mod]
    if gated:
        in_specs.append(_row_spec(D_MODEL))
        out_shape += [jax.ShapeDtypeStruct((SEQ, D_MODEL), BF16), vec]
        out_specs += [_row_spec(D_MODEL), _vec_spec(D_MODEL)]
        args.append(y)
    return pl.pallas_call(
        body, out_shape=tuple(out_shape), grid=(SEQ // ROWS,), in_specs=in_specs, out_specs=tuple(out_specs),
        name=name, compiler_params=_cp("arbitrary"))(*args)


def _prev_halo(halo, width, col):
    per = ROWS // halo
    return pl.BlockSpec((halo, width), lambda i: (jnp.maximum(i * per - 1, 0), col))


def _next_halo(halo, width, col):
    per = ROWS // halo
    last = SEQ // halo - 1
    return pl.BlockSpec((halo, width), lambda i: (jnp.minimum((i + 1) * per, last), col))


def _conv_module_forward(av_ref, ag_ref, avh_ref, agh_ref, wc_ref, bc_ref, lg_ref, lb_ref, u0p):
    i = pl.program_id(0)
    hv = avh_ref[...] * _sig(agh_ref[...])
    u0p[0:CONV_HALO, :] = jnp.where(i > 0, hv, 0.0)
    u0p[CONV_HALO:, :] = av_ref[...] * _sig(ag_ref[...])
    u1 = jnp.broadcast_to(bc_ref[...], (ROWS, D_CONV))
    for j in range(CONV_K):
        u1 = u1 + wc_ref[j:j + 1, :] * u0p[pl.ds(CONV_HALO - (CONV_K - 1) + j, ROWS), :]
    mu = _mean(u1)
    cen = u1 - mu
    rs = lax.rsqrt(_mean(cen * cen) + EPS)
    z = cen * rs
    ln = z * lg_ref[...] + lb_ref[...]
    s = _sig(ln)
    return z, rs, ln, s, ln * s


def conv_module_fwd(proj, wc, bc, lg, lb, gco, name):
    def body(av_ref, ag_ref, avh_ref, agh_ref, wc_ref, bc_ref, lg_ref, lb_ref, gco_ref, out_ref, u0p):
        _, _, _, _, u2 = _conv_module_forward(av_ref, ag_ref, avh_ref, agh_ref, wc_ref, bc_ref, lg_ref, lb_ref, u0p)
        rc = lax.rsqrt(_mean(u2 * u2) + EPS)
        out_ref[...] = (u2 * rc * gco_ref[...]).astype(BF16)

    v = _vec_spec(D_CONV)
    return pl.pallas_call(
        body, out_shape=jax.ShapeDtypeStruct((SEQ, D_CONV), BF16), grid=(SEQ // ROWS,),
        in_specs=[_row_spec(D_CONV, 0), _row_spec(D_CONV, 1), _prev_halo(CONV_HALO, D_CONV, 0),
                  _prev_halo(CONV_HALO, D_CONV, 1), _vec_spec(D_CONV, CONV_K), v, v, v, v],
        out_specs=_row_spec(D_CONV), scratch_shapes=[pltpu.VMEM((ROWS + CONV_HALO, D_CONV), F32)],
        name=name, compiler_params=_cp("parallel"))(proj, proj, proj, proj, wc, bc, lg, lb, gco)


def conv_module_bwd_a(proj, dmixed, wc, bc, lg, lb, gco, name):
    def body(av_ref, ag_ref, avh_ref, agh_ref, dm_ref, wc_ref, bc_ref, lg_ref, lb_ref, gco_ref,
             du1_ref, dgco_ref, dlg_ref, dlb_ref, dbc_ref, dwc_ref, u0p):
        first = pl.program_id(0) == 0
        z, rs, ln, s, u2 = _conv_module_forward(av_ref, ag_ref, avh_ref, agh_ref, wc_ref, bc_ref, lg_ref, lb_ref, u0p)
        rc = lax.rsqrt(_mean(u2 * u2) + EPS)
        xn = u2 * rc
        dm = dm_ref[...]
        _acc(dgco_ref, _rsum(dm * xn), first)
        dyn = dm * gco_ref[...]
        du2 = rc * (dyn - xn * _mean(dyn * xn))
        dln = du2 * (s * (1.0 + ln * (1.0 - s)))
        _acc(dlg_ref, _rsum(dln * z), first)
        _acc(dlb_ref, _rsum(dln), first)
        dz = dln * lg_ref[...]
        du1 = rs * (dz - _mean(dz) - z * _mean(dz * z))
        du1_ref[...] = du1
        _acc(dbc_ref, _rsum(du1), first)

        @pl.when(first)
        def _():
            dwc_ref[...] = jnp.zeros_like(dwc_ref)

        for j in range(CONV_K):
            dwc_ref[j:j + 1, :] += _rsum(du1 * u0p[pl.ds(CONV_HALO - (CONV_K - 1) + j, ROWS), :])

    v = _vec_spec(D_CONV)
    vec = jax.ShapeDtypeStruct((1, D_CONV), F32)
    return pl.pallas_call(
        body,
        out_shape=(jax.ShapeDtypeStruct((SEQ, D_CONV), F32), vec, vec, vec, vec, jax.ShapeDtypeStruct((CONV_K, D_CONV), F32)),
        grid=(SEQ // ROWS,),
        in_specs=[_row_spec(D_CONV, 0), _row_spec(D_CONV, 1), _prev_halo(CONV_HALO, D_CONV, 0),
                  _prev_halo(CONV_HALO, D_CONV, 1), _row_spec(D_CONV, 0), _vec_spec(D_CONV, CONV_K), v, v, v, v],
        out_specs=(_row_spec(D_CONV), v, v, v, v, _vec_spec(D_CONV, CONV_K)),
        scratch_shapes=[pltpu.VMEM((ROWS + CONV_HALO, D_CONV), F32)],
        name=name, compiler_params=_cp("arbitrary"))(proj, proj, proj, proj, dmixed, wc, bc, lg, lb, gco)


def conv_module_bwd_b(proj, du1, wc, name):
    def body(av_ref, ag_ref, du1_ref, du1n_ref, wc_ref, out_ref, dup):
        i = pl.program_id(0)
        dup[0:ROWS, :] = du1_ref[...]
        dup[ROWS:, :] = jnp.where(i < SEQ // ROWS - 1, du1n_ref[...], 0.0)
        du0 = jnp.zeros((ROWS, D_CONV), F32)
        for j in range(CONV_K):
            du0 = du0 + wc_ref[j:j + 1, :] * dup[pl.ds(CONV_K - 1 - j, ROWS), :]
        sg = _sig(ag_ref[...])
        out_ref[:, 0:D_CONV] = (du0 * sg).astype(BF16)
        out_ref[:, D_CONV:] = (du0 * av_ref[...] * sg * (1.0 - sg)).astype(BF16)

    return pl.pallas_call(
        body, out_shape=jax.ShapeDtypeStruct((SEQ, 2 * D_CONV), BF16), grid=(SEQ // ROWS,),
        in_specs=[_row_spec(D_CONV, 0), _row_spec(D_CONV, 1), _row_spec(D_CONV, 0), _next_halo(CONV_HALO, D_CONV, 0),
                  _vec_spec(D_CONV, CONV_K)],
        out_specs=_row_spec(2 * D_CONV), scratch_shapes=[pltpu.VMEM((ROWS + CONV_HALO, D_CONV), F32)],
        name=name, compiler_params=_cp("parallel"))(proj, proj, du1, du1, wc)


def _attn_specs(sub_len, r):
    q = pl.BlockSpec((sub_len, 128), lambda rho, hp: (0, rho * 12 + hp))
    k = pl.BlockSpec((sub_len, 128), lambda rho, hp: (0, rho * 12 + 4 + hp))
    v = pl.BlockSpec((sub_len, 128), lambda rho, hp: (0, rho * 12 + 8 + hp))
    o = pl.BlockSpec((sub_len, 128), lambda rho, hp: (0, rho * 4 + hp))
    return q, k, v, o


def _attn_block(q_ref, k_ref, v_ref, n, hs, win):
    q0 = pl.multiple_of(n * ATTN_BLOCK, ATTN_BLOCK)
    k0 = pl.multiple_of(jnp.maximum(n - 1, 0) * ATTN_BLOCK, ATTN_BLOCK)
    qb = q_ref[pl.ds(q0, ATTN_BLOCK), hs]
    kw = k_ref[pl.ds(k0, win), hs]
    vw = v_ref[pl.ds(k0, win), hs]
    s = lax.dot_general(qb, kw, (((1,), (1,)), ((), ())), preferred_element_type=F32) * (HEAD_DIM ** -0.5)
    dist = (q0 - k0) + lax.broadcasted_iota(jnp.int32, (ATTN_BLOCK, win), 0) \
        - lax.broadcasted_iota(jnp.int32, (ATTN_BLOCK, win), 1)
    s = jnp.where((dist >= 0) & (dist <= ATTN_BLOCK), s, NEG)
    return q0, k0, qb, kw, vw, s


def attn_fwd(qkv_r, sub_len, r, name):
    nb = sub_len // ATTN_BLOCK
    win = 2 * ATTN_BLOCK if nb > 1 else ATTN_BLOCK

    def body(q_ref, k_ref, v_ref, o_ref, l_ref):
        for h in range(2):
            hs = slice(h * HEAD_DIM, (h + 1) * HEAD_DIM)

            def block(n, carry):
                q0, _, _, _, vw, s = _attn_block(q_ref, k_ref, v_ref, n, hs, win)
                m = jnp.max(s, axis=1, keepdims=True)
                p = jnp.exp(s - m)
                den = jnp.sum(p, axis=1, keepdims=True)
                o = jnp.dot(p.astype(BF16), vw, preferred_element_type=F32) / den
                o_ref[pl.ds(q0, ATTN_BLOCK), hs] = o
                l_ref[pl.ds(q0, ATTN_BLOCK), hs] = jnp.broadcast_to(m + jnp.log(den), (ATTN_BLOCK, HEAD_DIM))
                return carry

            lax.fori_loop(0, nb, block, 0)

    q, k, v, o = _attn_specs(sub_len, r)
    shp = jax.ShapeDtypeStruct((sub_len, r * D_ATTN), F32)
    return pl.pallas_call(
        body, out_shape=(shp, shp), grid=(r, 4), in_specs=[q, k, v], out_specs=(o, o),
        name=name, compiler_params=_cp("parallel", "parallel"))(qkv_r, qkv_r, qkv_r)


def attn_bwd(qkv_r, do_r, lse_r, dd_r, sub_len, r, name):
    nb = sub_len // ATTN_BLOCK
    win = 2 * ATTN_BLOCK if nb > 1 else ATTN_BLOCK

    def body(q_ref, k_ref, v_ref, do_ref, l_ref, dd_ref, dq_ref, dk_ref, dv_ref):
        dk_ref[...] = jnp.zeros_like(dk_ref)
        dv_ref[...] = jnp.zeros_like(dv_ref)
        for h in range(2):
            hs = slice(h * HEAD_DIM, (h + 1) * HEAD_DIM)
            h1 = slice(h * HEAD_DIM, h * HEAD_DIM + 1)

            def block(n, carry):
                q0, k0, qb, kw, vw, s = _attn_block(q_ref, k_ref, v_ref, n, hs, win)
                dob = do_ref[pl.ds(q0, ATTN_BLOCK), hs]
                p = jnp.exp(s - l_ref[pl.ds(q0, ATTN_BLOCK), h1])
                dp = lax.dot_general(dob, vw, (((1,), (1,)), ((), ())), preferred_element_type=F32)
                ds = (p * (dp - dd_ref[pl.ds(q0, ATTN_BLOCK), h1]) * (HEAD_DIM ** -0.5)).astype(BF16)
                dq_ref[pl.ds(q0, ATTN_BLOCK), hs] = jnp.dot(ds, kw, preferred_element_type=F32)
                dk_ref[pl.ds(k0, win), hs] += lax.dot_general(ds, qb, (((0,), (0,)), ((), ())), preferred_element_type=F32)
                dv_ref[pl.ds(k0, win), hs] += lax.dot_general(p.astype(BF16), dob, (((0,), (0,)), ((), ())),
                                                              preferred_element_type=F32)
                return carry

            lax.fori_loop(0, nb, block, 0)

    q, k, v, o = _attn_specs(sub_len, r)
    shp = jax.ShapeDtypeStruct((sub_len, r * D_ATTN), F32)
    return pl.pallas_call(
        body, out_shape=(shp, shp, shp), grid=(r, 4), in_specs=[q, k, v, o, o, o], out_specs=(o, o, o),
        name=name, compiler_params=_cp("parallel", "parallel"))(qkv_r, qkv_r, qkv_r, do_r, lse_r, dd_r)


def attn_combine_fwd(outs, lses, gao, name):
    def body(o1, o2, o3, l1, l2, l3, g_ref, att_ref, lse_ref, mix_ref):
        a1, a2, a3 = l1[...], l2[...], l3[...]
        m = jnp.maximum(jnp.maximum(a1, a2), a3)
        w1, w2, w3 = jnp.exp(a1 - m), jnp.exp(a2 - m), jnp.exp(a3 - m)
        den = w1 + w2 + w3
        att = (w1 * o1[...] + w2 * o2[...] + w3 * o3[...]) / den
        att_ref[...] = att
        lse_ref[...] = m + jnp.log(den)
        mix_ref[...] = (att * lax.rsqrt(_mean(att * att) + EPS) * g_ref[...]).astype(BF16)

    rs = _row_spec(D_ATTN)
    f = jax.ShapeDtypeStruct((SEQ, D_ATTN), F32)
    return pl.pallas_call(
        body, out_shape=(f, f, jax.ShapeDtypeStruct((SEQ, D_ATTN), BF16)), grid=(SEQ // ROWS,),
        in_specs=[rs] * 6 + [_vec_spec(D_ATTN)], out_specs=(rs, rs, rs),
        name=name, compiler_params=_cp("parallel"))(*outs, *lses, gao)


def attn_combine_bwd(dmixed, att, gao, name):
    def body(dm_ref, att_ref, g_ref, do_ref, dd_ref, dg_ref):
        first = pl.program_id(0) == 0
        att = att_ref[...]
        r = lax.rsqrt(_mean(att * att) + EPS)
        xn = att * r
        dm = dm_ref[...]
        _acc(dg_ref, _rsum(dm * xn), first)
        dyn = dm * g_ref[...]
        do = r * (dyn - xn * _mean(dyn * xn))
        do_ref[...] = do.astype(BF16)
        same_head = (jnp.right_shift(lax.broadcasted_iota(jnp.int32, (D_ATTN, D_ATTN), 0), 6)
                     == jnp.right_shift(lax.broadcasted_iota(jnp.int32, (D_ATTN, D_ATTN), 1), 6)).astype(F32)
        dd_ref[...] = jnp.dot(do * att, same_head, preferred_element_type=F32, precision=lax.Precision.HIGHEST)

    rs = _row_spec(D_ATTN)
    return pl.pallas_call(
        body,
        out_shape=(jax.ShapeDtypeStruct((SEQ, D_ATTN), BF16), jax.ShapeDtypeStruct((SEQ, D_ATTN), F32),
                   jax.ShapeDtypeStruct((1, D_ATTN), F32)),
        grid=(SEQ // ROWS,), in_specs=[_row_spec(D_ATTN, 1), rs, _vec_spec(D_ATTN)],
        out_specs=(rs, rs, _vec_spec(D_ATTN)), name=name, compiler_params=_cp("arbitrary"))(dmixed, att, gao)


def sum3_bf16(a, b, c, name):
    def body(a_ref, b_ref, c_ref, o_ref):
        o_ref[...] = (a_ref[...] + b_ref[...] + c_ref[...]).astype(BF16)

    w = a.shape[1]
    rs = _row_spec(w)
    return pl.pallas_call(
        body, out_shape=jax.ShapeDtypeStruct(a.shape, BF16), grid=(SEQ // ROWS,), in_specs=[rs, rs, rs], out_specs=rs,
        name=name, compiler_params=_cp("parallel"))(a, b, c)


N_FT = D_FF // FFN_TN


def _ffn_specs():
    per = ROWS // FFN_HALO
    cur_g = pl.BlockSpec((ROWS, FFN_TN), lambda j, i: (i, j))
    cur_v = pl.BlockSpec((ROWS, FFN_TN), lambda j, i: (i, j + N_FT))
    halo_g = pl.BlockSpec((FFN_HALO, FFN_TN), lambda j, i: (jnp.maximum(i * per - 1, 0), j))
    halo_v = pl.BlockSpec((FFN_HALO, FFN_TN), lambda j, i: (jnp.maximum(i * per - 1, 0), j + N_FT))
    w_g = pl.BlockSpec((FFN_K, FFN_TN), lambda j, i: (0, j))
    w_v = pl.BlockSpec((FFN_K, FFN_TN), lambda j, i: (0, j + N_FT))
    b_g = pl.BlockSpec((1, FFN_TN), lambda j, i: (0, j))
    b_v = pl.BlockSpec((1, FFN_TN), lambda j, i: (0, j + N_FT))
    return [cur_g, cur_v, halo_g, halo_v, w_g, w_v, b_g, b_v]


def _ffn_conv(cur_ref, halo_ref, w_ref, b_ref, pad):
    i = pl.program_id(1)
    pad[0:FFN_HALO, :] = jnp.where(i > 0, halo_ref[...], 0.0)
    pad[FFN_HALO:, :] = cur_ref[...]
    up = jnp.broadcast_to(b_ref[...], (ROWS, FFN_TN))
    for j in range(FFN_K):
        up = up + w_ref[j:j + 1, :] * pad[pl.ds(FFN_HALO - (FFN_K - 1) + j, ROWS), :]
    return up


def ffn_act_fwd(up0, wf, bf, name):
    def body(g_ref, v_ref, gh_ref, vh_ref, wg_ref, wv_ref, bg_ref, bv_ref, act_ref, gpad, vpad):
        gate = _ffn_conv(g_ref, gh_ref, wg_ref, bg_ref, gpad)
        val = _ffn_conv(v_ref, vh_ref, wv_ref, bv_ref, vpad)
        act_ref[...] = (gate * _sig(gate) * val).astype(BF16)

    pad = pltpu.VMEM((ROWS + FFN_HALO, FFN_TN), F32)
    return pl.pallas_call(
        body, out_shape=jax.ShapeDtypeStruct((SEQ, D_FF), BF16), grid=(N_FT, SEQ // ROWS),
        in_specs=_ffn_specs(), out_specs=pl.BlockSpec((ROWS, FFN_TN), lambda j, i: (i, j)),
        scratch_shapes=[pad, pad], name=name, compiler_params=_cp("parallel", "parallel"))(up0, up0, up0, up0, wf, wf, bf, bf)


def ffn_bwd_a(up0, dact, wf, bf, name):
    def body(g_ref, v_ref, gh_ref, vh_ref, wg_ref, wv_ref, bg_ref, bv_ref, da_ref,
             dg_ref, dv_ref, dbg_ref, dbv_ref, dwg_ref, dwv_ref, gpad, vpad):
        first = pl.program_id(1) == 0
        gate = _ffn_conv(g_ref, gh_ref, wg_ref, bg_ref, gpad)
        val = _ffn_conv(v_ref, vh_ref, wv_ref, bv_ref, vpad)
        s = _sig(gate)
        da = da_ref[...]
        dgate = da * val * (s * (1.0 + gate * (1.0 - s)))
        dval = da * (gate * s)
        dg_ref[...] = dgate
        dv_ref[...] = dval
        _acc(dbg_ref, _rsum(dgate), first)
        _acc(dbv_ref, _rsum(dval), first)

        @pl.when(first)
        def _():
            dwg_ref[...] = jnp.zeros_like(dwg_ref)
            dwv_ref[...] = jnp.zeros_like(dwv_ref)

        for j in range(FFN_K):
            off = FFN_HALO - (FFN_K - 1) + j
            dwg_ref[j:j + 1, :] += _rsum(dgate * gpad[pl.ds(off, ROWS), :])
            dwv_ref[j:j + 1, :] += _rsum(dval * vpad[pl.ds(off, ROWS), :])

    pad = pltpu.VMEM((ROWS + FFN_HALO, FFN_TN), F32)
    full = jax.ShapeDtypeStruct((SEQ, D_FF), F32)
    vec = jax.ShapeDtypeStruct((1, D_FF), F32)
    taps = jax.ShapeDtypeStruct((FFN_K, D_FF), F32)
    cur = pl.BlockSpec((ROWS, FFN_TN), lambda j, i: (i, j))
    vs = pl.BlockSpec((1, FFN_TN), lambda j, i: (0, j))
    ts = pl.BlockSpec((FFN_K, FFN_TN), lambda j, i: (0, j))
    return pl.pallas_call(
        body, out_shape=(full, full, vec, vec, taps, taps), grid=(N_FT, SEQ // ROWS),
        in_specs=_ffn_specs() + [cur], out_specs=(cur, cur, vs, vs, ts, ts), scratch_shapes=[pad, pad],
        name=name, compiler_params=_cp("parallel", "arbitrary"))(up0, up0, up0, up0, wf, wf, bf, bf, dact)


def ffn_bwd_b(dgate, dval, wf, name):
    per = ROWS // FFN_HALO
    last = SEQ // FFN_HALO - 1

    def body(g_ref, v_ref, gn_ref, vn_ref, w_ref, out_ref, pad):
        j = pl.program_id(0)
        i = pl.program_id(1)

        def run(cur_ref, nxt_ref):
            pad[0:ROWS, :] = cur_ref[...]
            pad[ROWS:, :] = jnp.where(i < SEQ // ROWS - 1, nxt_ref[...], 0.0)
            acc = jnp.zeros((ROWS, FFN_TN), F32)
            for t in range(FFN_K):
                acc = acc + w_ref[t:t + 1, :] * pad[pl.ds(FFN_K - 1 - t, ROWS), :]
            out_ref[...] = acc.astype(BF16)

        @pl.when(j < N_FT)
        def _():
            run(g_ref, gn_ref)

        @pl.when(j >= N_FT)
        def _():
            run(v_ref, vn_ref)

    cur_g = pl.BlockSpec((ROWS, FFN_TN), lambda j, i: (i, jnp.minimum(j, N_FT - 1)))
    cur_v = pl.BlockSpec((ROWS, FFN_TN), lambda j, i: (i, jnp.maximum(j - N_FT, 0)))
    nxt_g = pl.BlockSpec((FFN_HALO, FFN_TN), lambda j, i: (jnp.minimum((i + 1) * per, last), jnp.minimum(j, N_FT - 1)))
    nxt_v = pl.BlockSpec((FFN_HALO, FFN_TN), lambda j, i: (jnp.minimum((i + 1) * per, last), jnp.maximum(j - N_FT, 0)))
    return pl.pallas_call(
        body, out_shape=jax.ShapeDtypeStruct((SEQ, 2 * D_FF), BF16), grid=(2 * N_FT, SEQ // ROWS),
        in_specs=[cur_g, cur_v, nxt_g, nxt_v, pl.BlockSpec((FFN_K, FFN_TN), lambda j, i: (0, j))],
        out_specs=pl.BlockSpec((ROWS, FFN_TN), lambda j, i: (i, j)),
        scratch_shapes=[pltpu.VMEM((ROWS + FFN_HALO, FFN_TN), F32)],
        name=name, compiler_params=_cp("parallel", "parallel"))(dgate, dval, dgate, dval, wf)


def ada_fwd(c_all, w_ada, b_cols, name):
    def body(c_ref, w_ref, b_ref, o_ref):
        cc = c_ref[...]
        sc = (cc * _sig(cc)).astype(BF16)
        o_ref[...] = jnp.dot(sc, w_ref[...].astype(BF16), preferred_element_type=F32) + b_ref[...]

    return pl.pallas_call(body, out_shape=jax.ShapeDtypeStruct((N_DEV, w_ada.shape[1]), F32), name=name,
                          compiler_params=_cp())(c_all, w_ada, b_cols)


def _adam(w, g, m, v):
    m = ADAM_B1 * m + (1.0 - ADAM_B1) * g
    v = ADAM_B2 * v + (1.0 - ADAM_B2) * (g * g)
    m_hat = m / (1.0 - ADAM_B1 ** ADAM_STEP)
    v_hat = v / (1.0 - ADAM_B2 ** ADAM_STEP)
    delta = -ADAM_LR * (m_hat / (jnp.sqrt(v_hat) + ADAM_EPS) + ADAM_WD * w)
    return delta, m, v


def ada_bwd_adamw(c_all_t, dmod_cols, w, m, v, name):
    rows, cols = w.shape
    tr = 256

    def body(ct_ref, dm_ref, w_ref, m_ref, v_ref, g_ref, d_ref, nm_ref, nv_ref):
        ct = ct_ref[...]
        sc = ct * _sig(ct)
        g = sc[:, 0:1] * dm_ref[0:1, :]
        for b in range(1, N_DEV):
            g = g + sc[:, b:b + 1] * dm_ref[b:b + 1, :]
        g_ref[...] = g
        d_ref[...], nm_ref[...], nv_ref[...] = _adam(w_ref[...], g, m_ref[...], v_ref[...])

    blk = pl.BlockSpec((tr, cols), lambda i: (i, 0))
    shp = jax.ShapeDtypeStruct((rows, cols), F32)
    return pl.pallas_call(
        body, out_shape=(shp, shp, shp, shp), grid=(rows // tr,),
        in_specs=[pl.BlockSpec((tr, N_DEV), lambda i: (i, 0)), pl.BlockSpec((N_DEV, cols), lambda i: (0, 0)), blk, blk, blk],
        out_specs=(blk, blk, blk, blk), name=name, compiler_params=_cp("parallel"))(c_all_t, dmod_cols, w, m, v)


def sum_adamw(parts, own, w, m, v, tr, name):
    n_parts, rows, cols = parts.shape

    def body(*refs):
        if own is None:
            p_ref, w_ref, m_ref, v_ref, g_ref, d_ref, nm_ref, nv_ref = refs
            g = p_ref[0].astype(F32)
        else:
            p_ref, own_ref, w_ref, m_ref, v_ref, g_ref, d_ref, nm_ref, nv_ref = refs
            g = own_ref[...].astype(F32)
        for k in range(1, n_parts):
            g = g + p_ref[k].astype(F32)
        g_ref[...] = g
        d_ref[...], nm_ref[...], nv_ref[...] = _adam(w_ref[...], g, m_ref[...], v_ref[...])

    blk = pl.BlockSpec((tr, cols), lambda i: (i, 0))
    shp = jax.ShapeDtypeStruct((rows, cols), F32)
    args = [parts] + ([] if own is None else [own]) + [w, m, v]
    return pl.pallas_call(
        body, out_shape=(shp, shp, shp, shp), grid=(rows // tr,),
        in_specs=[pl.BlockSpec((n_parts, tr, cols), lambda i: (0, i, 0))] + [blk] * (len(args) - 1),
        out_specs=(blk, blk, blk, blk), name=name, compiler_params=_cp("parallel"))(*args)


MESH = pl.DeviceIdType.MESH
ANY = pl.BlockSpec(memory_space=pl.ANY)


def all_gather(block, name):
    def body(x_ref, out_ref, send_sems, recv_sems, local_sem):
        x, y, c = lax.axis_index("x"), lax.axis_index("y"), lax.axis_index("c")
        me, sibling = (x, y, c), (x, y, 1 - c)
        chips = [(1 - x, y), (x, 1 - y), (1 - x, 1 - y)]

        def slot(px, py, pc):
            return out_ref.at[4 * px + 2 * py + pc]

        def copy(k, blk, to, src=None):
            return pltpu.make_async_remote_copy(
                src_ref=slot(*blk) if src is None else src, dst_ref=slot(*blk),
                send_sem=send_sems.at[k], recv_sem=recv_sems.at[k], device_id=to, device_id_type=MESH)

        mine = pltpu.make_async_copy(x_ref, slot(*me), local_sem)
        mine.start()
        first = [copy(0, me, sibling, src=x_ref)]
        first += [copy(1 + j, me, (*chip, c), src=x_ref) for j, chip in enumerate(chips)]
        for cp in first:
            cp.start()
        passed = [copy(4 + j, (*chip, c), sibling) for j, chip in enumerate(chips)]
        for j, chip in enumerate(chips):
            copy(1 + j, (*chip, c), me).wait_recv()
            passed[j].start()
        copy(0, sibling, me).wait_recv()
        for j, chip in enumerate(chips):
            copy(4 + j, (*chip, 1 - c), me).wait_recv()
        for cp in first + passed:
            cp.wait_send()
        mine.wait()

    return pl.pallas_call(
        body, out_shape=jax.ShapeDtypeStruct((N_DEV,) + block.shape, block.dtype), in_specs=[ANY], out_specs=ANY,
        scratch_shapes=[pltpu.SemaphoreType.DMA((7,)), pltpu.SemaphoreType.DMA((7,)), pltpu.SemaphoreType.DMA],
        name=name)(block)


HBM = pl.BlockSpec(memory_space=pltpu.HBM)
SEM = pl.BlockSpec(memory_space=pltpu.SEMAPHORE)
EFFECT = pltpu.SideEffectType.DATAFLOW_SIDE_EFFECTING


def _peer_copies(src_ref, land_ref, send_sems, recv_sems, gather):
    x, y, c = lax.axis_index("x"), lax.axis_index("y"), lax.axis_index("c")
    me = 4 * x + 2 * y + c
    copies = []
    for k in range(1, N_DEV):
        px = 1 - x if k & 4 else x
        py = 1 - y if k & 2 else y
        pc = 1 - c if k & 1 else c
        copies.append(pltpu.make_async_remote_copy(
            src_ref=src_ref if gather else src_ref.at[4 * px + 2 * py + pc],
            dst_ref=land_ref.at[me] if gather else land_ref.at[k],
            send_sem=send_sems.at[k - 1], recv_sem=recv_sems.at[k - 1], device_id=(px, py, pc), device_id_type=MESH))
    return copies


def exchange_start(src, gather, name):
    land_shape = (N_DEV,) + src.shape if gather else src.shape

    def body(src_ref, land_ref, send_sems, recv_sems, src_thru, land_thru, token):
        for cp in _peer_copies(src_ref, land_ref, send_sems, recv_sems, gather):
            cp.start()
        token[...] = jnp.zeros_like(token)

    send_sems, recv_sems, src_thru, land_thru, token = pl.pallas_call(
        body, name=name,
        out_shape=(pltpu.SemaphoreType.DMA((N_DEV - 1,)), pltpu.SemaphoreType.DMA((N_DEV - 1,)),
                   pltpu.HBM(src.shape, src.dtype), pltpu.HBM(land_shape, src.dtype), jax.ShapeDtypeStruct((8, 128), F32)),
        in_specs=(HBM, HBM), out_specs=(SEM, SEM, HBM, HBM, pl.BlockSpec(memory_space=pltpu.VMEM)),
        input_output_aliases={0: 2, 1: 3}, compiler_params=pltpu.CompilerParams(has_side_effects=EFFECT),
    )(pltpu.with_memory_space_constraint(src, pltpu.HBM),
      pltpu.with_memory_space_constraint(lax.empty(land_shape, src.dtype), pltpu.HBM))
    return (send_sems, recv_sems, src_thru, land_thru), token[0, 0]


def exchange_wait(handles, after, gather, name):
    send_sems, recv_sems, src_thru, land_thru = handles

    def body(src_ref, land_ref, send_sems, recv_sems, after_ref, src_dead, got_ref):
        for cp in _peer_copies(src_ref, land_ref, send_sems, recv_sems, gather):
            cp.wait_send()
            cp.wait_recv()

    return pl.pallas_call(
        body, name=name,
        out_shape=(pltpu.HBM(src_thru.shape, src_thru.dtype), pltpu.HBM(land_thru.shape, land_thru.dtype)),
        in_specs=(HBM, HBM, SEM, SEM, ANY), out_specs=(HBM, HBM), input_output_aliases={0: 0, 1: 1},
        compiler_params=pltpu.CompilerParams(has_side_effects=EFFECT),
    )(src_thru, land_thru, send_sems, recv_sems, after)[1]


def _to_pattern(a, r):
    return a.reshape(SEQ // r, r * a.shape[1])


def local_step(x, tgt, mod, get_w, put_grad, wc, wf, g_mix, bc, lg, lb, gco, gao, g_ffn, bf, g_fin):
    h1 = rms_mod_fwd(x, g_mix, mod, 0, 1, "h1_fwd")
    w_in = get_w("w_in", h1)
    proj = matmul(h1, w_in, "nn", F32, 256, D_IN, "proj_fwd")
    mix_a = conv_module_fwd(proj, wc, bc, lg, lb, gco, "conv_module_fwd")
    qkv = proj[:, 2 * D_CONV:].astype(BF16)
    qkv_r = [_to_pattern(qkv, r) for _, r in PATTERNS]
    outs, lses = [], []
    for (sub_len, r), q_r in zip(PATTERNS, qkv_r):
        o, l = attn_fwd(q_r, sub_len, r, f"attn_fwd_r{r}")
        outs.append(o.reshape(SEQ, D_ATTN))
        lses.append(l.reshape(SEQ, D_ATTN))
    att, lse, mix_b = attn_combine_fwd(outs, lses, gao, "attn_combine_fwd")
    mixed = jnp.concatenate([mix_a, mix_b], axis=1)
    w_out = get_w("w_out", mixed)
    y1 = matmul(mixed, w_out, "nn", F32, 256, D_MODEL, "out_proj_fwd")
    x1, h2 = resid_rms_mod_fwd(x, y1, g_ffn, mod, 2, 3, 4, "x1_h2_fwd")
    w_up = get_w("w_up", h2)
    up0 = matmul(h2, w_up, "nn", F32, 256, D_FF, "up_fwd")
    act = ffn_act_fwd(up0, wf, bf, "ffn_act_fwd")
    w_down = get_w("w_down", act)
    y2 = matmul(act, w_down, "nn", F32, 256, D_MODEL, "down_fwd")
    loss_t, dx2, dy2, d_gfin, d_gaf = final_loss_bwd(x1, y2, tgt, g_fin, mod, 5, "loss_bwd")
    dact = matmul(dy2, w_down, "nt", F32, 256, FFN_TN, "down_bwd_x")
    dw_down = matmul(act, dy2, "tn", BF16, 256, D_MODEL, "down_bwd_w")
    dgate, dval, dbf_g, dbf_v, dwf_g, dwf_v = ffn_bwd_a(up0, dact, wf + put_grad("w_down", dw_down), bf, "ffn_bwd_a")
    dup0 = ffn_bwd_b(dgate, dval, wf, "ffn_bwd_b")
    dh2 = matmul(dup0, w_up, "nt", F32, 256, 512, "up_bwd_x")
    dw_up = matmul(h2, dup0, "tn", BF16, 256, 512, "up_bwd_w")
    dx1, d_shf, d_scf, d_gffn, dy1, d_gam = rms_mod_bwd(x1, dh2, dx2, g_ffn + put_grad("w_up", dw_up), mod, 4, y1, 2, "h2_bwd")
    dmixed = matmul(dy1, w_out, "nt", F32, 256, D_MODEL, "out_proj_bwd_x")
    dw_out = matmul(mixed, dy1, "tn", BF16, 256, D_MODEL, "out_proj_bwd_w")
    do, dd, d_gao = attn_combine_bwd(dmixed, att, gao + put_grad("w_out", dw_out), "attn_combine_bwd")
    dqkv = []
    for (sub_len, r), q_r in zip(PATTERNS, qkv_r):
        dq, dk, dv = attn_bwd(q_r, _to_pattern(do, r), _to_pattern(lse, r), _to_pattern(dd, r), sub_len, r, f"attn_bwd_r{r}")
        dqkv.append(jnp.concatenate([t.reshape(SEQ, D_ATTN) for t in (dq, dk, dv)], axis=1))
    dqkv = sum3_bf16(*dqkv, "dqkv_sum")
    du1, d_gco, d_lg, d_lb, d_bc, d_wc = conv_module_bwd_a(proj, dmixed, wc, bc, lg, lb, gco, "conv_module_bwd_a")
    dproj_a = conv_module_bwd_b(proj, du1, wc, "conv_module_bwd_b")
    dproj = jnp.concatenate([dproj_a, dqkv], axis=1)
    dh1 = matmul(dproj, w_in, "nt", F32, 256, D_MODEL, "proj_bwd_x")
    dw_in = matmul(h1, dproj, "tn", BF16, 256, 512, "proj_bwd_w")
    dx, d_shm, d_scm, d_gmix = rms_mod_bwd(x, dh1, dx1, g_mix + put_grad("w_in", dw_in), mod, 1, None, 0, "h1_bwd")
    dmod = jnp.concatenate([d_shm, d_scm, d_gam, d_shf, d_scf, d_gaf], axis=1)
    small = dict(g_norm_mix=d_gmix, b_conv_dw=d_bc, ln_conv_g=d_lg, ln_conv_b=d_lb, g_conv_out=d_gco, g_attn_out=d_gao,
                 g_norm_ffn=d_gffn, b_ffn_dw=jnp.concatenate([dbf_g, dbf_v], axis=1), g_final=d_gfin,
                 w_conv_dw=d_wc, w_ffn_dw=jnp.concatenate([dwf_g, dwf_v], axis=1), dmod=dmod)
    return loss_t[0, 0], dx, small


def _padw(a, width):
    return jnp.pad(a, ((0, 0), (0, width - a.shape[1])))


def pack_small(t):
    wide = jnp.concatenate([_padw(t["dmod"], PACK_W), _padw(t["b_ffn_dw"], PACK_W), _padw(t["w_ffn_dw"], PACK_W),
                            jnp.zeros((3, PACK_W), F32)], axis=0)
    z512 = jnp.zeros((1, 512), F32)
    narrow = jnp.concatenate([
        t["g_norm_mix"], t["g_norm_ffn"], t["g_final"],
        jnp.concatenate([t["b_conv_dw"], t["ln_conv_g"]], axis=1),
        jnp.concatenate([t["ln_conv_b"], t["g_conv_out"]], axis=1),
        jnp.concatenate([t["g_attn_out"], z512], axis=1),
        jnp.zeros((2, 1024), F32),
        jnp.pad(t["w_conv_dw"], ((0, 1), (0, 0))).reshape(16, 1024)], axis=0)
    return jnp.concatenate([wide, narrow.reshape(4, PACK_W), jnp.zeros((4, PACK_W), F32)], axis=0)


def unpack_small(p):
    narrow = p[8:12].reshape(24, 1024)
    return dict(
        dmod=p[0:1], b_ffn_dw=p[1:2, :2 * D_FF], w_ffn_dw=p[2:5, :2 * D_FF],
        g_norm_mix=narrow[0:1], g_norm_ffn=narrow[1:2], g_final=narrow[2:3],
        b_conv_dw=narrow[3:4, :512], ln_conv_g=narrow[3:4, 512:], ln_conv_b=narrow[4:5, :512], g_conv_out=narrow[4:5, 512:],
        g_attn_out=narrow[5:6, :512], w_conv_dw=narrow[8:24].reshape(32, 512)[:CONV_K])


def _embed(local, width, me):
    return lax.dynamic_update_slice(jnp.zeros((local.shape[0], width), F32), local, (0, me * local.shape[1]))


def _shard(full, n_cols, me):
    return lax.dynamic_slice(full, (0, me * n_cols), (full.shape[0], n_cols))


WEIGHTS = ["w_ada", "b_ada", "g_norm_mix", "w_in", "w_conv_dw", "b_conv_dw", "ln_conv_g", "ln_conv_b", "g_conv_out",
           "g_attn_out", "w_out", "g_norm_ffn", "w_up", "w_ffn_dw", "b_ffn_dw", "w_down", "g_final"]
SMALL_REPLICATED = ["g_norm_mix", "b_conv_dw", "ln_conv_g", "ln_conv_b", "g_conv_out", "g_attn_out", "g_norm_ffn",
                    "b_ffn_dw", "g_final"]


def kernel(x, c, w_ada, b_ada, g_norm_mix, w_in, w_conv_dw, b_conv_dw, ln_conv_g, ln_conv_b, g_conv_out, g_attn_out, w_out, g_norm_ffn, w_up, w_ffn_dw, b_ffn_dw, w_down, g_final, loss_target, m_w_ada, m_b_ada, m_g_norm_mix, m_w_in, m_w_conv_dw, m_b_conv_dw, m_ln_conv_g, m_ln_conv_b, m_g_conv_out, m_g_attn_out, m_w_out, m_g_norm_ffn, m_w_up, m_w_ffn_dw, m_b_ffn_dw, m_w_down, m_g_final, v_w_ada, v_b_ada, v_g_norm_mix, v_w_in, v_w_conv_dw, v_b_conv_dw, v_ln_conv_g, v_ln_conv_b, v_g_conv_out, v_g_attn_out, v_w_out, v_g_norm_ffn, v_w_up, v_w_ffn_dw, v_b_ffn_dw, v_w_down, v_g_final):
    args = dict(locals())
    me = 4 * lax.axis_index("x") + 2 * lax.axis_index("y") + lax.axis_index("c")

    def flat(name, prefix=""):
        a = args[prefix + name]
        return a.reshape(a.shape[-2] if a.ndim > 1 else 1, a.shape[-1])

    n_in, n_up, r_out, r_down = w_in.shape[2], w_up.shape[2], w_out.shape[1], w_down.shape[1]
    blocks = dict(w_in=flat("w_in").astype(BF16), w_up=flat("w_up").astype(BF16),
                  rows=jnp.concatenate([flat("w_out"), flat("w_down")], axis=0).astype(BF16))
    gathers, tok = {}, jnp.zeros((), F32)
    for name in ("w_in", "rows", "w_up"):
        gathers[name], tok = exchange_start(blocks[name] + tok.astype(BF16), True, f"gather_{name}_start")
    slot = lax.broadcasted_iota(jnp.int32, (N_DEV, 1, 1), 0)
    full = {}

    def gathered(name, after):
        land = exchange_wait(gathers[name], after, True, f"gather_{name}_wait")
        return jnp.where(slot == me, blocks[name][None], land)

    def get_w(name, after):
        if name in ("w_out", "w_down"):
            if "rows" not in full:
                full["rows"] = gathered("rows", after)
            rows = full["rows"]
            return rows[:, :r_out, :].reshape(D_MODEL, D_MODEL) if name == "w_out" else rows[:, r_out:, :].reshape(D_FF, D_MODEL)
        return gathered(name, after).transpose(1, 0, 2).reshape(D_MODEL, -1)

    c_all = all_gather(jnp.pad(c + tok, ((0, 7), (0, 0))), "gather_c")[:, 0, :]
    n_ada = w_ada.shape[2]
    mod_cols = ada_fwd(c_all, flat("w_ada"), _shard(flat("b_ada"), n_ada, me), "ada_fwd")
    mod_all = all_gather(mod_cols, "gather_mod")
    mod = lax.dynamic_index_in_dim(mod_all, me, axis=1, keepdims=False).reshape(N_MOD, D_MODEL)
    mod = jnp.pad(mod, ((0, 2), (0, 0)))
    wc_full = all_gather(jnp.pad(flat("w_conv_dw"), ((0, 1), (0, 0))), "gather_conv_taps")
    wc_full = wc_full.transpose(1, 0, 2).reshape(32, D_CONV)[:CONV_K]
    wf_full = all_gather(jnp.pad(flat("w_ffn_dw"), ((0, 5), (0, 0))), "gather_ffn_taps")
    wf_full = wf_full.transpose(1, 0, 2).reshape(8, 2 * D_FF)[:FFN_K]

    exchanges, own = {}, {}

    def put_grad(name, dw):
        if name in ("w_in", "w_up"):
            dev_major = dw.reshape(D_MODEL, N_DEV, -1).transpose(1, 0, 2)
        else:
            dev_major = dw.reshape(N_DEV, -1, D_MODEL)
        own[name] = lax.dynamic_index_in_dim(dev_major, me, axis=0, keepdims=False)
        exchanges[name], token = exchange_start(dev_major, False, f"exchange_{name}_start")
        return token

    loss_part, grad_x, small = local_step(
        x[0], loss_target[0], mod, get_w, put_grad, wc_full, wf_full,
        flat("g_norm_mix"), flat("b_conv_dw"), flat("ln_conv_g"), flat("ln_conv_b"), flat("g_conv_out"),
        flat("g_attn_out"), flat("g_norm_ffn"), flat("b_ffn_dw"), flat("g_final"))
    loss = lax.psum(loss_part, ("x", "y", "c"))

    out = {}

    def finish(name, tr, after):
        parts = exchange_wait(exchanges[name], after, False, f"exchange_{name}_wait")
        out[name] = sum_adamw(parts, own[name], flat(name), flat(name, "m_"), flat(name, "v_"), tr, "adamw_" + name)
        return out[name][0]

    after = finish("w_down", r_down, grad_x)
    after = finish("w_up", 256, after)
    after = finish("w_out", r_out, after)

    small_all = all_gather(pack_small(small), "gather_small")

    def packed(prefix):
        t = {n: flat(n, prefix) for n in SMALL_REPLICATED}
        t["dmod"] = flat("b_ada", prefix)
        t["w_conv_dw"] = _embed(flat("w_conv_dw", prefix), D_CONV, me)
        t["w_ffn_dw"] = _embed(flat("w_ffn_dw", prefix), 2 * D_FF, me)
        return pack_small(t)

    res = sum_adamw(small_all, None, packed(""), packed("m_"), packed("v_"), PACK_ROWS, "adamw_small")
    res = [unpack_small(r) for r in res]
    for n in SMALL_REPLICATED:
        out[n] = tuple(r[n] for r in res)
    out["b_ada"] = tuple(r["dmod"] for r in res)
    out["w_conv_dw"] = tuple(_shard(r["w_conv_dw"], w_conv_dw.shape[2], me) for r in res)
    out["w_ffn_dw"] = tuple(_shard(r["w_ffn_dw"], w_ffn_dw.shape[2], me) for r in res)

    dmod_cols = _shard(small_all[:, 0, :], n_ada, me)
    out["w_ada"] = ada_bwd_adamw(c_all.T, dmod_cols, flat("w_ada"), flat("w_ada", "m_"), flat("w_ada", "v_"), "adamw_w_ada")
    finish("w_in", 256, out["w_ada"][0])

    result = [loss, grad_x[None]]
    for k in range(4):
        result += [out[n][k].reshape(args[n].shape) for n in WEIGHTS]
    return tuple(result)
```

```python
import functools

import jax
import jax.numpy as jnp
from jax import lax
from jax.experimental import pallas as pl
from jax.experimental.pallas import tpu as pltpu

F32 = jnp.float32
BF16 = jnp.bfloat16

N_DEV = 8
SEQ = 2048
D_MODEL = 1024
D_CONV = 512
D_ATTN = 512
HEAD_DIM = 64
CONV_K = 31
D_FF = 2816
FFN_K = 3
D_IN = 2 * D_CONV + 3 * D_ATTN
N_MOD = 6
EPS = 1e-6
ATTN_BLOCK = 128
PATTERNS = ((2048, 1), (512, 4), (128, 16))
NEG = -1e30

ADAM_LR, ADAM_B1, ADAM_B2, ADAM_EPS, ADAM_WD, ADAM_STEP = 0.001, 0.9, 0.999, 1e-08, 0.01, 10

ROWS = 256
CONV_HALO = 32
FFN_HALO = 8
FFN_TN = 1408
VMEM_LIMIT = 56 * 1024 * 1024
PACK_ROWS, PACK_W = 16, 6144


def _cp(*sem):
    return pltpu.CompilerParams(dimension_semantics=sem if sem else None, vmem_limit_bytes=VMEM_LIMIT)


def _sig(x):
    return 1.0 / (1.0 + jnp.exp(-x))


def _rsum(x):
    return jnp.sum(x, axis=0, keepdims=True)


def _mean(x):
    return jnp.mean(x, axis=-1, keepdims=True)


def _acc(ref, val, first):
    @pl.when(first)
    def _():
        ref[...] = val

    @pl.when(jnp.logical_not(first))
    def _():
        ref[...] += val


def matmul(a, b, kind, out_dtype, tm, tn, name):
    if kind == "nn":
        (m, k), n = a.shape, b.shape[1]
        a_spec = pl.BlockSpec((tm, k), lambda j, i: (i, 0))
        b_spec = pl.BlockSpec((k, tn), lambda j, i: (0, j))
        dims = (((1,), (0,)), ((), ()))
    elif kind == "nt":
        (m, k), n = a.shape, b.shape[0]
        a_spec = pl.BlockSpec((tm, k), lambda j, i: (i, 0))
        b_spec = pl.BlockSpec((tn, k), lambda j, i: (j, 0))
        dims = (((1,), (1,)), ((), ()))
    else:
        (k, m), n = a.shape, b.shape[1]
        a_spec = pl.BlockSpec((k, tm), lambda j, i: (0, i))
        b_spec = pl.BlockSpec((k, tn), lambda j, i: (0, j))
        dims = (((0,), (0,)), ((), ()))
    assert m % tm == 0 and n % tn == 0, (name, m, n, tm, tn)

    def body(a_ref, b_ref, o_ref):
        o_ref[...] = lax.dot_general(a_ref[...], b_ref[...], dims, preferred_element_type=F32).astype(o_ref.dtype)

    return pl.pallas_call(
        body, out_shape=jax.ShapeDtypeStruct((m, n), out_dtype), grid=(n // tn, m // tm),
        in_specs=[a_spec, b_spec], out_specs=pl.BlockSpec((tm, tn), lambda j, i: (i, j)),
        name=name, compiler_params=_cp("parallel", "parallel"))(a, b)


def _row_spec(width, col=0):
    return pl.BlockSpec((ROWS, width), lambda i: (i, col))


def _vec_spec(width, rows=1):
    return pl.BlockSpec((rows, width), lambda i: (0, 0))


def rms_mod_fwd(x, g, mod, sh_row, sc_row, name):
    def body(x_ref, g_ref, mod_ref, h_ref):
        xx = x_ref[...]
        r = lax.rsqrt(_mean(xx * xx) + EPS)
        h = xx * r * g_ref[...]
        h_ref[...] = (h * (1.0 + mod_ref[sc_row:sc_row + 1, :]) + mod_ref[sh_row:sh_row + 1, :]).astype(BF16)

    return pl.pallas_call(
        body, out_shape=jax.ShapeDtypeStruct((SEQ, D_MODEL), BF16), grid=(SEQ // ROWS,),
        in_specs=[_row_spec(D_MODEL), _vec_spec(D_MODEL), _vec_spec(D_MODEL, 8)],
        out_specs=_row_spec(D_MODEL), name=name, compiler_params=_cp("parallel"))(x, g, mod)


def resid_rms_mod_fwd(x, y, g, mod, ga_row, sh_row, sc_row, name):
    def body(x_ref, y_ref, g_ref, mod_ref, x1_ref, h_ref):
        x1 = x_ref[...] + mod_ref[ga_row:ga_row + 1, :] * y_ref[...]
        x1_ref[...] = x1
        r = lax.rsqrt(_mean(x1 * x1) + EPS)
        h = x1 * r * g_ref[...]
        h_ref[...] = (h * (1.0 + mod_ref[sc_row:sc_row + 1, :]) + mod_ref[sh_row:sh_row + 1, :]).astype(BF16)

    return pl.pallas_call(
        body, out_shape=(jax.ShapeDtypeStruct((SEQ, D_MODEL), F32), jax.ShapeDtypeStruct((SEQ, D_MODEL), BF16)),
        grid=(SEQ // ROWS,),
        in_specs=[_row_spec(D_MODEL), _row_spec(D_MODEL), _vec_spec(D_MODEL), _vec_spec(D_MODEL, 8)],
        out_specs=(_row_spec(D_MODEL), _row_spec(D_MODEL)), name=name, compiler_params=_cp("parallel"))(x, y, g, mod)


def final_loss_bwd(x1, y2, tgt, g, mod, ga_row, name):
    def body(x1_ref, y2_ref, t_ref, g_ref, mod_ref, loss_ref, dx2_ref, dy2_ref, dg_ref, dga_ref):
        first = pl.program_id(0) == 0
        ga = mod_ref[ga_row:ga_row + 1, :]
        y2 = y2_ref[...]
        x2 = x1_ref[...] + ga * y2
        r = lax.rsqrt(_mean(x2 * x2) + EPS)
        xn = x2 * r
        err = xn * g_ref[...] - t_ref[...]
        _acc(loss_ref, jnp.broadcast_to(0.5 * jnp.sum(_mean(err * err)), (8, 128)), first)
        dy = err * (1.0 / D_MODEL)
        _acc(dg_ref, _rsum(dy * xn), first)
        dxn = dy * g_ref[...]
        dx2 = r * (dxn - xn * _mean(dxn * xn))
        dx2_ref[...] = dx2
        dy2_ref[...] = (dx2 * ga).astype(BF16)
        _acc(dga_ref, _rsum(dx2 * y2), first)

    vec = jax.ShapeDtypeStruct((1, D_MODEL), F32)
    return pl.pallas_call(
        body,
        out_shape=(jax.ShapeDtypeStruct((8, 128), F32), jax.ShapeDtypeStruct((SEQ, D_MODEL), F32),
                   jax.ShapeDtypeStruct((SEQ, D_MODEL), BF16), vec, vec),
        grid=(SEQ // ROWS,),
        in_specs=[_row_spec(D_MODEL), _row_spec(D_MODEL), _row_spec(D_MODEL), _vec_spec(D_MODEL), _vec_spec(D_MODEL, 8)],
        out_specs=(pl.BlockSpec((8, 128), lambda i: (0, 0)), _row_spec(D_MODEL), _row_spec(D_MODEL),
                   _vec_spec(D_MODEL), _vec_spec(D_MODEL)),
        name=name, compiler_params=_cp("arbitrary"))(x1, y2, tgt, g, mod)


def rms_mod_bwd(x, dh, dres, g, mod, sc_row, y, ga_row, name):
    gated = y is not None

    def body(*refs):
        if gated:
            x_ref, dh_ref, dres_ref, g_ref, mod_ref, y_ref, dx_ref, dsh_ref, dsc_ref, dg_ref, dy_ref, dga_ref = refs
        else:
            x_ref, dh_ref, dres_ref, g_ref, mod_ref, dx_ref, dsh_ref, dsc_ref, dg_ref = refs
        first = pl.program_id(0) == 0
        xx = x_ref[...]
        dh = dh_ref[...]
        gg = g_ref[...]
        r = lax.rsqrt(_mean(xx * xx) + EPS)
        xn = xx * r
        _acc(dsh_ref, _rsum(dh), first)
        _acc(dsc_ref, _rsum(dh * (xn * gg)), first)
        dt = dh * (1.0 + mod_ref[sc_row:sc_row + 1, :])
        _acc(dg_ref, _rsum(dt * xn), first)
        dxn = dt * gg
        dx = dres_ref[...] + r * (dxn - xn * _mean(dxn * xn))
        dx_ref[...] = dx
        if gated:
            _acc(dga_ref, _rsum(dx * y_ref[...]), first)
            dy_ref[...] = (dx * mod_ref[ga_row:ga_row + 1, :]).astype(BF16)

    vec = jax.ShapeDtypeStruct((1, D_MODEL), F32)
    in_specs = [_row_spec(D_MODEL), _row_spec(D_MODEL), _row_spec(D_MODEL), _vec_spec(D_MODEL), _vec_spec(D_MODEL, 8)]
    out_shape = [jax.ShapeDtypeStruct((SEQ, D_MODEL), F32), vec, vec, vec]
    out_specs = [_row_spec(D_MODEL), _vec_spec(D_MODEL), _vec_spec(D_MODEL), _vec_spec(D_MODEL)]
    args = [x, dh, dres, g, mod]
    if gated:
        in_specs.append(_row_spec(D_MODEL))
        out_shape += [jax.ShapeDtypeStruct((SEQ, D_MODEL), BF16), vec]
        out_specs += [_row_spec(D_MODEL), _vec_spec(D_MODEL)]
        args.append(y)
    return pl.pallas_call(
        body, out_shape=tuple(out_shape), grid=(SEQ // ROWS,), in_specs=in_specs, out_specs=tuple(out_specs),
        name=name, compiler_params=_cp("arbitrary"))(*args)


def _prev_halo(halo, width, col):
    per = ROWS // halo
    return pl.BlockSpec((halo, width), lambda i: (jnp.maximum(i * per - 1, 0), col))


def _next_halo(halo, width, col):
    per = ROWS // halo
    last = SEQ // halo - 1
    return pl.BlockSpec((halo, width), lambda i: (jnp.minimum((i + 1) * per, last), col))


def _conv_module_forward(av_ref, ag_ref, avh_ref, agh_ref, wc_ref, bc_ref, lg_ref, lb_ref, u0p):
    i = pl.program_id(0)
    hv = avh_ref[...] * _sig(agh_ref[...])
    u0p[0:CONV_HALO, :] = jnp.where(i > 0, hv, 0.0)
    u0p[CONV_HALO:, :] = av_ref[...] * _sig(ag_ref[...])
    u1 = jnp.broadcast_to(bc_ref[...], (ROWS, D_CONV))
    for j in range(CONV_K):
        u1 = u1 + wc_ref[j:j + 1, :] * u0p[pl.ds(CONV_HALO - (CONV_K - 1) + j, ROWS), :]
    mu = _mean(u1)
    cen = u1 - mu
    rs = lax.rsqrt(_mean(cen * cen) + EPS)
    z = cen * rs
    ln = z * lg_ref[...] + lb_ref[...]
    s = _sig(ln)
    return z, rs, ln, s, ln * s


def conv_module_fwd(proj, wc, bc, lg, lb, gco, name):
    def body(av_ref, ag_ref, avh_ref, agh_ref, wc_ref, bc_ref, lg_ref, lb_ref, gco_ref, out_ref, u0p):
        _, _, _, _, u2 = _conv_module_forward(av_ref, ag_ref, avh_ref, agh_ref, wc_ref, bc_ref, lg_ref, lb_ref, u0p)
        rc = lax.rsqrt(_mean(u2 * u2) + EPS)
        out_ref[...] = (u2 * rc * gco_ref[...]).astype(BF16)

    v = _vec_spec(D_CONV)
    return pl.pallas_call(
        body, out_shape=jax.ShapeDtypeStruct((SEQ, D_CONV), BF16), grid=(SEQ // ROWS,),
        in_specs=[_row_spec(D_CONV, 0), _row_spec(D_CONV, 1), _prev_halo(CONV_HALO, D_CONV, 0),
                  _prev_halo(CONV_HALO, D_CONV, 1), _vec_spec(D_CONV, CONV_K), v, v, v, v],
        out_specs=_row_spec(D_CONV), scratch_shapes=[pltpu.VMEM((ROWS + CONV_HALO, D_CONV), F32)],
        name=name, compiler_params=_cp("parallel"))(proj, proj, proj, proj, wc, bc, lg, lb, gco)


def conv_module_bwd_a(proj, dmixed, wc, bc, lg, lb, gco, name):
    def body(av_ref, ag_ref, avh_ref, agh_ref, dm_ref, wc_ref, bc_ref, lg_ref, lb_ref, gco_ref,
             du1_ref, dgco_ref, dlg_ref, dlb_ref, dbc_ref, dwc_ref, u0p):
        first = pl.program_id(0) == 0
        z, rs, ln, s, u2 = _conv_module_forward(av_ref, ag_ref, avh_ref, agh_ref, wc_ref, bc_ref, lg_ref, lb_ref, u0p)
        rc = lax.rsqrt(_mean(u2 * u2) + EPS)
        xn = u2 * rc
        dm = dm_ref[...]
        _acc(dgco_ref, _rsum(dm * xn), first)
        dyn = dm * gco_ref[...]
        du2 = rc * (dyn - xn * _mean(dyn * xn))
        dln = du2 * (s * (1.0 + ln * (1.0 - s)))
        _acc(dlg_ref, _rsum(dln * z), first)
        _acc(dlb_ref, _rsum(dln), first)
        dz = dln * lg_ref[...]
        du1 = rs * (dz - _mean(dz) - z * _mean(dz * z))
        du1_ref[...] = du1
        _acc(dbc_ref, _rsum(du1), first)

        @pl.when(first)
        def _():
            dwc_ref[...] = jnp.zeros_like(dwc_ref)

        for j in range(CONV_K):
            dwc_ref[j:j + 1, :] += _rsum(du1 * u0p[pl.ds(CONV_HALO - (CONV_K - 1) + j, ROWS), :])

    v = _vec_spec(D_CONV)
    vec = jax.ShapeDtypeStruct((1, D_CONV), F32)
    return pl.pallas_call(
        body,
        out_shape=(jax.ShapeDtypeStruct((SEQ, D_CONV), F32), vec, vec, vec, vec, jax.ShapeDtypeStruct((CONV_K, D_CONV), F32)),
        grid=(SEQ // ROWS,),
        in_specs=[_row_spec(D_CONV, 0), _row_spec(D_CONV, 1), _prev_halo(CONV_HALO, D_CONV, 0),
                  _prev_halo(CONV_HALO, D_CONV, 1), _row_spec(D_CONV, 0), _vec_spec(D_CONV, CONV_K), v, v, v, v],
        out_specs=(_row_spec(D_CONV), v, v, v, v, _vec_spec(D_CONV, CONV_K)),
        scratch_shapes=[pltpu.VMEM((ROWS + CONV_HALO, D_CONV), F32)],
        name=name, compiler_params=_cp("arbitrary"))(proj, proj, proj, proj, dmixed, wc, bc, lg, lb, gco)


def conv_module_bwd_b(proj, du1, wc, name):
    def body(av_ref, ag_ref, du1_ref, du1n_ref, wc_ref, out_ref, dup):
        i = pl.program_id(0)
        dup[0:ROWS, :] = du1_ref[...]
        dup[ROWS:, :] = jnp.where(i < SEQ // ROWS - 1, du1n_ref[...], 0.0)
        du0 = jnp.zeros((ROWS, D_CONV), F32)
        for j in range(CONV_K):
            du0 = du0 + wc_ref[j:j + 1, :] * dup[pl.ds(CONV_K - 1 - j, ROWS), :]
        sg = _sig(ag_ref[...])
        out_ref[:, 0:D_CONV] = (du0 * sg).astype(BF16)
        out_ref[:, D_CONV:] = (du0 * av_ref[...] * sg * (1.0 - sg)).astype(BF16)

    return pl.pallas_call(
        body, out_shape=jax.ShapeDtypeStruct((SEQ, 2 * D_CONV), BF16), grid=(SEQ // ROWS,),
        in_specs=[_row_spec(D_CONV, 0), _row_spec(D_CONV, 1), _row_spec(D_CONV, 0), _next_halo(CONV_HALO, D_CONV, 0),
                  _vec_spec(D_CONV, CONV_K)],
        out_specs=_row_spec(2 * D_CONV), scratch_shapes=[pltpu.VMEM((ROWS + CONV_HALO, D_CONV), F32)],
        name=name, compiler_params=_cp("parallel"))(proj, proj, du1, du1, wc)


def _attn_specs(sub_len, r):
    q = pl.BlockSpec((sub_len, 128), lambda rho, hp: (0, rho * 12 + hp))
    k = pl.BlockSpec((sub_len, 128), lambda rho, hp: (0, rho * 12 + 4 + hp))
    v = pl.BlockSpec((sub_len, 128), lambda rho, hp: (0, rho * 12 + 8 + hp))
    o = pl.BlockSpec((sub_len, 128), lambda rho, hp: (0, rho * 4 + hp))
    return q, k, v, o


def _attn_block(q_ref, k_ref, v_ref, n, hs, win):
    q0 = pl.multiple_of(n * ATTN_BLOCK, ATTN_BLOCK)
    k0 = pl.multiple_of(jnp.maximum(n - 1, 0) * ATTN_BLOCK, ATTN_BLOCK)
    qb = q_ref[pl.ds(q0, ATTN_BLOCK), hs]
    kw = k_ref[pl.ds(k0, win), hs]
    vw = v_ref[pl.ds(k0, win), hs]
    s = lax.dot_general(qb, kw, (((1,), (1,)), ((), ())), preferred_element_type=F32) * (HEAD_DIM ** -0.5)
    dist = (q0 - k0) + lax.broadcasted_iota(jnp.int32, (ATTN_BLOCK, win), 0) \
        - lax.broadcasted_iota(jnp.int32, (ATTN_BLOCK, win), 1)
    s = jnp.where((dist >= 0) & (dist <= ATTN_BLOCK), s, NEG)
    return q0, k0, qb, kw, vw, s


def attn_fwd(qkv_r, sub_len, r, name):
    nb = sub_len // ATTN_BLOCK
    win = 2 * ATTN_BLOCK if nb > 1 else ATTN_BLOCK

    def body(q_ref, k_ref, v_ref, o_ref, l_ref):
        for h in range(2):
            hs = slice(h * HEAD_DIM, (h + 1) * HEAD_DIM)

            def block(n, carry):
                q0, _, _, _, vw, s = _attn_block(q_ref, k_ref, v_ref, n, hs, win)
                m = jnp.max(s, axis=1, keepdims=True)
                p = jnp.exp(s - m)
                den = jnp.sum(p, axis=1, keepdims=True)
                o = jnp.dot(p.astype(BF16), vw, preferred_element_type=F32) / den
                o_ref[pl.ds(q0, ATTN_BLOCK), hs] = o
                l_ref[pl.ds(q0, ATTN_BLOCK), hs] = jnp.broadcast_to(m + jnp.log(den), (ATTN_BLOCK, HEAD_DIM))
                return carry

            lax.fori_loop(0, nb, block, 0)

    q, k, v, o = _attn_specs(sub_len, r)
    shp = jax.ShapeDtypeStruct((sub_len, r * D_ATTN), F32)
    return pl.pallas_call(
        body, out_shape=(shp, shp), grid=(r, 4), in_specs=[q, k, v], out_specs=(o, o),
        name=name, compiler_params=_cp("parallel", "parallel"))(qkv_r, qkv_r, qkv_r)


def attn_bwd(qkv_r, do_r, lse_r, dd_r, sub_len, r, name):
    nb = sub_len // ATTN_BLOCK
    win = 2 * ATTN_BLOCK if nb > 1 else ATTN_BLOCK

    def body(q_ref, k_ref, v_ref, do_ref, l_ref, dd_ref, dq_ref, dk_ref, dv_ref):
        dk_ref[...] = jnp.zeros_like(dk_ref)
        dv_ref[...] = jnp.zeros_like(dv_ref)
        for h in range(2):
            hs = slice(h * HEAD_DIM, (h + 1) * HEAD_DIM)
            h1 = slice(h * HEAD_DIM, h * HEAD_DIM + 1)

            def block(n, carry):
                q0, k0, qb, kw, vw, s = _attn_block(q_ref, k_ref, v_ref, n, hs, win)
                dob = do_ref[pl.ds(q0, ATTN_BLOCK), hs]
                p = jnp.exp(s - l_ref[pl.ds(q0, ATTN_BLOCK), h1])
                dp = lax.dot_general(dob, vw, (((1,), (1,)), ((), ())), preferred_element_type=F32)
                ds = (p * (dp - dd_ref[pl.ds(q0, ATTN_BLOCK), h1]) * (HEAD_DIM ** -0.5)).astype(BF16)
                dq_ref[pl.ds(q0, ATTN_BLOCK), hs] = jnp.dot(ds, kw, preferred_element_type=F32)
                dk_ref[pl.ds(k0, win), hs] += lax.dot_general(ds, qb, (((0,), (0,)), ((), ())), preferred_element_type=F32)
                dv_ref[pl.ds(k0, win), hs] += lax.dot_general(p.astype(BF16), dob, (((0,), (0,)), ((), ())),
                                                              preferred_element_type=F32)
                return carry

            lax.fori_loop(0, nb, block, 0)

    q, k, v, o = _attn_specs(sub_len, r)
    shp = jax.ShapeDtypeStruct((sub_len, r * D_ATTN), F32)
    return pl.pallas_call(
        body, out_shape=(shp, shp, shp), grid=(r, 4), in_specs=[q, k, v, o, o, o], out_specs=(o, o, o),
        name=name, compiler_params=_cp("parallel", "parallel"))(qkv_r, qkv_r, qkv_r, do_r, lse_r, dd_r)


def attn_combine_fwd(outs, lses, gao, name):
    def body(o1, o2, o3, l1, l2, l3, g_ref, att_ref, lse_ref, mix_ref):
        a1, a2, a3 = l1[...], l2[...], l3[...]
        m = jnp.maximum(jnp.maximum(a1, a2), a3)
        w1, w2, w3 = jnp.exp(a1 - m), jnp.exp(a2 - m), jnp.exp(a3 - m)
        den = w1 + w2 + w3
        att = (w1 * o1[...] + w2 * o2[...] + w3 * o3[...]) / den
        att_ref[...] = att
        lse_ref[...] = m + jnp.log(den)
        mix_ref[...] = (att * lax.rsqrt(_mean(att * att) + EPS) * g_ref[...]).astype(BF16)

    rs = _row_spec(D_ATTN)
    f = jax.ShapeDtypeStruct((SEQ, D_ATTN), F32)
    return pl.pallas_call(
        body, out_shape=(f, f, jax.ShapeDtypeStruct((SEQ, D_ATTN), BF16)), grid=(SEQ // ROWS,),
        in_specs=[rs] * 6 + [_vec_spec(D_ATTN)], out_specs=(rs, rs, rs),
        name=name, compiler_params=_cp("parallel"))(*outs, *lses, gao)


def attn_combine_bwd(dmixed, att, gao, name):
    def body(dm_ref, att_ref, g_ref, do_ref, dd_ref, dg_ref):
        first = pl.program_id(0) == 0
        att = att_ref[...]
        r = lax.rsqrt(_mean(att * att) + EPS)
        xn = att * r
        dm = dm_ref[...]
        _acc(dg_ref, _rsum(dm * xn), first)
        dyn = dm * g_ref[...]
        do = r * (dyn - xn * _mean(dyn * xn))
        do_ref[...] = do.astype(BF16)
        same_head = (jnp.right_shift(lax.broadcasted_iota(jnp.int32, (D_ATTN, D_ATTN), 0), 6)
                     == jnp.right_shift(lax.broadcasted_iota(jnp.int32, (D_ATTN, D_ATTN), 1), 6)).astype(F32)
        dd_ref[...] = jnp.dot(do * att, same_head, preferred_element_type=F32, precision=lax.Precision.HIGHEST)

    rs = _row_spec(D_ATTN)
    return pl.pallas_call(
        body,
        out_shape=(jax.ShapeDtypeStruct((SEQ, D_ATTN), BF16), jax.ShapeDtypeStruct((SEQ, D_ATTN), F32),
                   jax.ShapeDtypeStruct((1, D_ATTN), F32)),
        grid=(SEQ // ROWS,), in_specs=[_row_spec(D_ATTN, 1), rs, _vec_spec(D_ATTN)],
        out_specs=(rs, rs, _vec_spec(D_ATTN)), name=name, compiler_params=_cp("arbitrary"))(dmixed, att, gao)


def sum3_bf16(a, b, c, name):
    def body(a_ref, b_ref, c_ref, o_ref):
        o_ref[...] = (a_ref[...] + b_ref[...] + c_ref[...]).astype(BF16)

    w = a.shape[1]
    rs = _row_spec(w)
    return pl.pallas_call(
        body, out_shape=jax.ShapeDtypeStruct(a.shape, BF16), grid=(SEQ // ROWS,), in_specs=[rs, rs, rs], out_specs=rs,
        name=name, compiler_params=_cp("parallel"))(a, b, c)


N_FT = D_FF // FFN_TN


def _ffn_specs():
    per = ROWS // FFN_HALO
    cur_g = pl.BlockSpec((ROWS, FFN_TN), lambda j, i: (i, j))
    cur_v = pl.BlockSpec((ROWS, FFN_TN), lambda j, i: (i, j + N_FT))
    halo_g = pl.BlockSpec((FFN_HALO, FFN_TN), lambda j, i: (jnp.maximum(i * per - 1, 0), j))
    halo_v = pl.BlockSpec((FFN_HALO, FFN_TN), lambda j, i: (jnp.maximum(i * per - 1, 0), j + N_FT))
    w_g = pl.BlockSpec((FFN_K, FFN_TN), lambda j, i: (0, j))
    w_v = pl.BlockSpec((FFN_K, FFN_TN), lambda j, i: (0, j + N_FT))
    b_g = pl.BlockSpec((1, FFN_TN), lambda j, i: (0, j))
    b_v = pl.BlockSpec((1, FFN_TN), lambda j, i: (0, j + N_FT))
    return [cur_g, cur_v, halo_g, halo_v, w_g, w_v, b_g, b_v]


def _ffn_conv(cur_ref, halo_ref, w_ref, b_ref, pad):
    i = pl.program_id(1)
    pad[0:FFN_HALO, :] = jnp.where(i > 0, halo_ref[...], 0.0)
    pad[FFN_HALO:, :] = cur_ref[...]
    up = jnp.broadcast_to(b_ref[...], (ROWS, FFN_TN))
    for j in range(FFN_K):
        up = up + w_ref[j:j + 1, :] * pad[pl.ds(FFN_HALO - (FFN_K - 1) + j, ROWS), :]
    return up


def ffn_act_fwd(up0, wf, bf, name):
    def body(g_ref, v_ref, gh_ref, vh_ref, wg_ref, wv_ref, bg_ref, bv_ref, act_ref, gpad, vpad):
        gate = _ffn_conv(g_ref, gh_ref, wg_ref, bg_ref, gpad)
        val = _ffn_conv(v_ref, vh_ref, wv_ref, bv_ref, vpad)
        act_ref[...] = (gate * _sig(gate) * val).astype(BF16)

    pad = pltpu.VMEM((ROWS + FFN_HALO, FFN_TN), F32)
    return pl.pallas_call(
        body, out_shape=jax.ShapeDtypeStruct((SEQ, D_FF), BF16), grid=(N_FT, SEQ // ROWS),
        in_specs=_ffn_specs(), out_specs=pl.BlockSpec((ROWS, FFN_TN), lambda j, i: (i, j)),
        scratch_shapes=[pad, pad], name=name, compiler_params=_cp("parallel", "parallel"))(up0, up0, up0, up0, wf, wf, bf, bf)


def ffn_bwd_a(up0, dact, wf, bf, name):
    def body(g_ref, v_ref, gh_ref, vh_ref, wg_ref, wv_ref, bg_ref, bv_ref, da_ref,
             dg_ref, dv_ref, dbg_ref, dbv_ref, dwg_ref, dwv_ref, gpad, vpad):
        first = pl.program_id(1) == 0
        gate = _ffn_conv(g_ref, gh_ref, wg_ref, bg_ref, gpad)
        val = _ffn_conv(v_ref, vh_ref, wv_ref, bv_ref, vpad)
        s = _sig(gate)
        da = da_ref[...]
        dgate = da * val * (s * (1.0 + gate * (1.0 - s)))
        dval = da * (gate * s)
        dg_ref[...] = dgate
        dv_ref[...] = dval
        _acc(dbg_ref, _rsum(dgate), first)
        _acc(dbv_ref, _rsum(dval), first)

        @pl.when(first)
        def _():
            dwg_ref[...] = jnp.zeros_like(dwg_ref)
            dwv_ref[...] = jnp.zeros_like(dwv_ref)

        for j in range(FFN_K):
            off = FFN_HALO - (FFN_K - 1) + j
            dwg_ref[j:j + 1, :] += _rsum(dgate * gpad[pl.ds(off, ROWS), :])
            dwv_ref[j:j + 1, :] += _rsum(dval * vpad[pl.ds(off, ROWS), :])

    pad = pltpu.VMEM((ROWS + FFN_HALO, FFN_TN), F32)
    full = jax.ShapeDtypeStruct((SEQ, D_FF), F32)
    vec = jax.ShapeDtypeStruct((1, D_FF), F32)
    taps = jax.ShapeDtypeStruct((FFN_K, D_FF), F32)
    cur = pl.BlockSpec((ROWS, FFN_TN), lambda j, i: (i, j))
    vs = pl.BlockSpec((1, FFN_TN), lambda j, i: (0, j))
    ts = pl.BlockSpec((FFN_K, FFN_TN), lambda j, i: (0, j))
    return pl.pallas_call(
        body, out_shape=(full, full, vec, vec, taps, taps), grid=(N_FT, SEQ // ROWS),
        in_specs=_ffn_specs() + [cur], out_specs=(cur, cur, vs, vs, ts, ts), scratch_shapes=[pad, pad],
        name=name, compiler_params=_cp("parallel", "arbitrary"))(up0, up0, up0, up0, wf, wf, bf, bf, dact)


def ffn_bwd_b(dgate, dval, wf, name):
    per = ROWS // FFN_HALO
    last = SEQ // FFN_HALO - 1

    def body(g_ref, v_ref, gn_ref, vn_ref, w_ref, out_ref, pad):
        j = pl.program_id(0)
        i = pl.program_id(1)

        def run(cur_ref, nxt_ref):
            pad[0:ROWS, :] = cur_ref[...]
            pad[ROWS:, :] = jnp.where(i < SEQ // ROWS - 1, nxt_ref[...], 0.0)
            acc = jnp.zeros((ROWS, FFN_TN), F32)
            for t in range(FFN_K):
                acc = acc + w_ref[t:t + 1, :] * pad[pl.ds(FFN_K - 1 - t, ROWS), :]
            out_ref[...] = acc.astype(BF16)

        @pl.when(j < N_FT)
        def _():
            run(g_ref, gn_ref)

        @pl.when(j >= N_FT)
        def _():
            run(v_ref, vn_ref)

    cur_g = pl.BlockSpec((ROWS, FFN_TN), lambda j, i: (i, jnp.minimum(j, N_FT - 1)))
    cur_v = pl.BlockSpec((ROWS, FFN_TN), lambda j, i: (i, jnp.maximum(j - N_FT, 0)))
    nxt_g = pl.BlockSpec((FFN_HALO, FFN_TN), lambda j, i: (jnp.minimum((i + 1) * per, last), jnp.minimum(j, N_FT - 1)))
    nxt_v = pl.BlockSpec((FFN_HALO, FFN_TN), lambda j, i: (jnp.minimum((i + 1) * per, last), jnp.maximum(j - N_FT, 0)))
    return pl.pallas_call(
        body, out_shape=jax.ShapeDtypeStruct((SEQ, 2 * D_FF), BF16), grid=(2 * N_FT, SEQ // ROWS),
        in_specs=[cur_g, cur_v, nxt_g, nxt_v, pl.BlockSpec((FFN_K, FFN_TN), lambda j, i: (0, j))],
        out_specs=pl.BlockSpec((ROWS, FFN_TN), lambda j, i: (i, j)),
        scratch_shapes=[pltpu.VMEM((ROWS + FFN_HALO, FFN_TN), F32)],
        name=name, compiler_params=_cp("parallel", "parallel"))(dgate, dval, dgate, dval, wf)


def ada_fwd(c_all, w_ada, b_cols, name):
    def body(c_ref, w_ref, b_ref, o_ref):
        cc = c_ref[...]
        sc = (cc * _sig(cc)).astype(BF16)
        o_ref[...] = jnp.dot(sc, w_ref[...].astype(BF16), preferred_element_type=F32) + b_ref[...]

    return pl.pallas_call(body, out_shape=jax.ShapeDtypeStruct((N_DEV, w_ada.shape[1]), F32), name=name,
                          compiler_params=_cp())(c_all, w_ada, b_cols)


def _adam(w, g, m, v):
    m = ADAM_B1 * m + (1.0 - ADAM_B1) * g
    v = ADAM_B2 * v + (1.0 - ADAM_B2) * (g * g)
    m_hat = m / (1.0 - ADAM_B1 ** ADAM_STEP)
    v_hat = v / (1.0 - ADAM_B2 ** ADAM_STEP)
    delta = -ADAM_LR * (m_hat / (jnp.sqrt(v_hat) + ADAM_EPS) + ADAM_WD * w)
    return delta, m, v


def ada_bwd_adamw(c_all_t, dmod_cols, w, m, v, name):
    rows, cols = w.shape
    tr = 256

    def body(ct_ref, dm_ref, w_ref, m_ref, v_ref, g_ref, d_ref, nm_ref, nv_ref):
        ct = ct_ref[...]
        sc = ct * _sig(ct)
        g = sc[:, 0:1] * dm_ref[0:1, :]
        for b in range(1, N_DEV):
            g = g + sc[:, b:b + 1] * dm_ref[b:b + 1, :]
        g_ref[...] = g
        d_ref[...], nm_ref[...], nv_ref[...] = _adam(w_ref[...], g, m_ref[...], v_ref[...])

    blk = pl.BlockSpec((tr, cols), lambda i: (i, 0))
    shp = jax.ShapeDtypeStruct((rows, cols), F32)
    return pl.pallas_call(
        body, out_shape=(shp, shp, shp, shp), grid=(rows // tr,),
        in_specs=[pl.BlockSpec((tr, N_DEV), lambda i: (i, 0)), pl.BlockSpec((N_DEV, cols), lambda i: (0, 0)), blk, blk, blk],
        out_specs=(blk, blk, blk, blk), name=name, compiler_params=_cp("parallel"))(c_all_t, dmod_cols, w, m, v)


def sum_adamw(parts, own, w, m, v, tr, name):
    n_parts, rows, cols = parts.shape

    def body(*refs):
        if own is None:
            p_ref, w_ref, m_ref, v_ref, g_ref, d_ref, nm_ref, nv_ref = refs
            g = p_ref[0].astype(F32)
        else:
            p_ref, own_ref, w_ref, m_ref, v_ref, g_ref, d_ref, nm_ref, nv_ref = refs
            g = own_ref[...].astype(F32)
        for k in range(1, n_parts):
            g = g + p_ref[k].astype(F32)
        g_ref[...] = g
        d_ref[...], nm_ref[...], nv_ref[...] = _adam(w_ref[...], g, m_ref[...], v_ref[...])

    blk = pl.BlockSpec((tr, cols), lambda i: (i, 0))
    shp = jax.ShapeDtypeStruct((rows, cols), F32)
    args = [parts] + ([] if own is None else [own]) + [w, m, v]
    return pl.pallas_call(
        body, out_shape=(shp, shp, shp, shp), grid=(rows // tr,),
        in_specs=[pl.BlockSpec((n_parts, tr, cols), lambda i: (0, i, 0))] + [blk] * (len(args) - 1),
        out_specs=(blk, blk, blk, blk), name=name, compiler_params=_cp("parallel"))(*args)


MESH = pl.DeviceIdType.MESH
ANY = pl.BlockSpec(memory_space=pl.ANY)


def all_gather(block, name):
    def body(x_ref, out_ref, send_sems, recv_sems, local_sem):
        x, y, c = lax.axis_index("x"), lax.axis_index("y"), lax.axis_index("c")
        me, sibling = (x, y, c), (x, y, 1 - c)
        chips = [(1 - x, y), (x, 1 - y), (1 - x, 1 - y)]

        def slot(px, py, pc):
            return out_ref.at[4 * px + 2 * py + pc]

        def copy(k, blk, to, src=None):
            return pltpu.make_async_remote_copy(
                src_ref=slot(*blk) if src is None else src, dst_ref=slot(*blk),
                send_sem=send_sems.at[k], recv_sem=recv_sems.at[k], device_id=to, device_id_type=MESH)

        mine = pltpu.make_async_copy(x_ref, slot(*me), local_sem)
        mine.start()
        first = [copy(0, me, sibling, src=x_ref)]
        first += [copy(1 + j, me, (*chip, c), src=x_ref) for j, chip in enumerate(chips)]
        for cp in first:
            cp.start()
        passed = [copy(4 + j, (*chip, c), sibling) for j, chip in enumerate(chips)]
        for j, chip in enumerate(chips):
            copy(1 + j, (*chip, c), me).wait_recv()
            passed[j].start()
        copy(0, sibling, me).wait_recv()
        for j, chip in enumerate(chips):
            copy(4 + j, (*chip, 1 - c), me).wait_recv()
        for cp in first + passed:
            cp.wait_send()
        mine.wait()

    return pl.pallas_call(
        body, out_shape=jax.ShapeDtypeStruct((N_DEV,) + block.shape, block.dtype), in_specs=[ANY], out_specs=ANY,
        scratch_shapes=[pltpu.SemaphoreType.DMA((7,)), pltpu.SemaphoreType.DMA((7,)), pltpu.SemaphoreType.DMA],
        name=name)(block)


HBM = pl.BlockSpec(memory_space=pltpu.HBM)
SEM = pl.BlockSpec(memory_space=pltpu.SEMAPHORE)
EFFECT = pltpu.SideEffectType.DATAFLOW_SIDE_EFFECTING


def _peer_copies(src_ref, land_ref, send_sems, recv_sems, gather):
    x, y, c = lax.axis_index("x"), lax.axis_index("y"), lax.axis_index("c")
    me = 4 * x + 2 * y + c
    copies = []
    for k in range(1, N_DEV):
        px = 1 - x if k & 4 else x
        py = 1 - y if k & 2 else y
        pc = 1 - c if k & 1 else c
        copies.append(pltpu.make_async_remote_copy(
            src_ref=src_ref if gather else src_ref.at[4 * px + 2 * py + pc],
            dst_ref=land_ref.at[me] if gather else land_ref.at[k],
            send_sem=send_sems.at[k - 1], recv_sem=recv_sems.at[k - 1], device_id=(px, py, pc), device_id_type=MESH))
    return copies


def exchange_start(src, gather, name, after=None):
    land_shape = (N_DEV,) + src.shape if gather else src.shape
    extra = () if after is None else (after,)

    def body(src_ref, land_ref, *refs):
        send_sems, recv_sems, src_thru, land_thru, token = refs[len(extra):]
        for cp in _peer_copies(src_ref, land_ref, send_sems, recv_sems, gather):
            cp.start()
        token[...] = jnp.zeros_like(token)

    send_sems, recv_sems, src_thru, land_thru, token = pl.pallas_call(
        body, name=name,
        out_shape=(pltpu.SemaphoreType.DMA((N_DEV - 1,)), pltpu.SemaphoreType.DMA((N_DEV - 1,)),
                   pltpu.HBM(src.shape, src.dtype), pltpu.HBM(land_shape, src.dtype), jax.ShapeDtypeStruct((8, 128), F32)),
        in_specs=(HBM, HBM) + (ANY,) * len(extra), out_specs=(SEM, SEM, HBM, HBM, pl.BlockSpec(memory_space=pltpu.VMEM)),
        input_output_aliases={0: 2, 1: 3}, compiler_params=pltpu.CompilerParams(has_side_effects=EFFECT),
    )(pltpu.with_memory_space_constraint(src, pltpu.HBM),
      pltpu.with_memory_space_constraint(lax.empty(land_shape, src.dtype), pltpu.HBM), *extra)
    return (send_sems, recv_sems, src_thru, land_thru), token[0, 0]


def exchange_wait(handles, after, gather, name):
    send_sems, recv_sems, src_thru, land_thru = handles

    def body(src_ref, land_ref, send_sems, recv_sems, after_ref, src_dead, got_ref):
        for cp in _peer_copies(src_ref, land_ref, send_sems, recv_sems, gather):
            cp.wait_send()
            cp.wait_recv()

    return pl.pallas_call(
        body, name=name,
        out_shape=(pltpu.HBM(src_thru.shape, src_thru.dtype), pltpu.HBM(land_thru.shape, land_thru.dtype)),
        in_specs=(HBM, HBM, SEM, SEM, ANY), out_specs=(HBM, HBM), input_output_aliases={0: 0, 1: 1},
        compiler_params=pltpu.CompilerParams(has_side_effects=EFFECT),
    )(src_thru, land_thru, send_sems, recv_sems, after)[1]


def _to_pattern(a, r):
    return a.reshape(SEQ // r, r * a.shape[1])


def local_step(x, tgt, mod, get_w, put_grad, put_small, wc, wf, g_mix, bc, lg, lb, gco, gao, g_ffn, bf, g_fin):
    h1 = rms_mod_fwd(x, g_mix, mod, 0, 1, "h1_fwd")
    w_in = get_w("w_in", h1)
    proj = matmul(h1, w_in, "nn", F32, 256, D_IN, "proj_fwd")
    mix_a = conv_module_fwd(proj, wc, bc, lg, lb, gco, "conv_module_fwd")
    qkv = proj[:, 2 * D_CONV:].astype(BF16)
    qkv_r = [_to_pattern(qkv, r) for _, r in PATTERNS]
    outs, lses = [], []
    for (sub_len, r), q_r in zip(PATTERNS, qkv_r):
        o, l = attn_fwd(q_r, sub_len, r, f"attn_fwd_r{r}")
        outs.append(o.reshape(SEQ, D_ATTN))
        lses.append(l.reshape(SEQ, D_ATTN))
    att, lse, mix_b = attn_combine_fwd(outs, lses, gao, "attn_combine_fwd")
    mixed = jnp.concatenate([mix_a, mix_b], axis=1)
    w_out = get_w("w_out", mixed)
    y1 = matmul(mixed, w_out, "nn", F32, 256, D_MODEL, "out_proj_fwd")
    x1, h2 = resid_rms_mod_fwd(x, y1, g_ffn, mod, 2, 3, 4, "x1_h2_fwd")
    w_up = get_w("w_up", h2)
    up0 = matmul(h2, w_up, "nn", F32, 256, D_FF, "up_fwd")
    act = ffn_act_fwd(up0, wf, bf, "ffn_act_fwd")
    w_down = get_w("w_down", act)
    y2 = matmul(act, w_down, "nn", F32, 256, D_MODEL, "down_fwd")
    loss_t, dx2, dy2, d_gfin, d_gaf = final_loss_bwd(x1, y2, tgt, g_fin, mod, 5, "loss_bwd")
    dact = matmul(dy2, w_down, "nt", F32, 256, FFN_TN, "down_bwd_x")
    dw_down = matmul(act, dy2, "tn", BF16, 256, D_MODEL, "down_bwd_w")
    dgate, dval, dbf_g, dbf_v, dwf_g, dwf_v = ffn_bwd_a(up0, dact, wf + put_grad("w_down", dw_down), bf, "ffn_bwd_a")
    dup0 = ffn_bwd_b(dgate, dval, wf, "ffn_bwd_b")
    dh2 = matmul(dup0, w_up, "nt", F32, 256, 512, "up_bwd_x")
    dw_up = matmul(h2, dup0, "tn", BF16, 256, 512, "up_bwd_w")
    dx1, d_shf, d_scf, d_gffn, dy1, d_gam = rms_mod_bwd(x1, dh2, dx2, g_ffn + put_grad("w_up", dw_up), mod, 4, y1, 2, "h2_bwd")
    dmixed = matmul(dy1, w_out, "nt", F32, 256, D_MODEL, "out_proj_bwd_x")
    dw_out = matmul(mixed, dy1, "tn", BF16, 256, D_MODEL, "out_proj_bwd_w")
    do, dd, d_gao = attn_combine_bwd(dmixed, att, gao + put_grad("w_out", dw_out), "attn_combine_bwd")
    dqkv = []
    for (sub_len, r), q_r in zip(PATTERNS, qkv_r):
        dq, dk, dv = attn_bwd(q_r, _to_pattern(do, r), _to_pattern(lse, r), _to_pattern(dd, r), sub_len, r, f"attn_bwd_r{r}")
        dqkv.append(jnp.concatenate([t.reshape(SEQ, D_ATTN) for t in (dq, dk, dv)], axis=1))
    dqkv = sum3_bf16(*dqkv, "dqkv_sum")
    du1, d_gco, d_lg, d_lb, d_bc, d_wc = conv_module_bwd_a(proj, dmixed, wc, bc, lg, lb, gco, "conv_module_bwd_a")
    dproj_a = conv_module_bwd_b(proj, du1, wc, "conv_module_bwd_b")
    dproj = jnp.concatenate([dproj_a, dqkv], axis=1)
    dh1 = matmul(dproj, w_in, "nt", F32, 256, D_MODEL, "proj_bwd_x")
    dx, d_shm, d_scm, d_gmix = rms_mod_bwd(x, dh1, dx1, g_mix, mod, 1, None, 0, "h1_bwd")
    dmod = jnp.concatenate([d_shm, d_scm, d_gam, d_shf, d_scf, d_gaf], axis=1)
    small = dict(g_norm_mix=d_gmix, b_conv_dw=d_bc, ln_conv_g=d_lg, ln_conv_b=d_lb, g_conv_out=d_gco, g_attn_out=d_gao,
                 g_norm_ffn=d_gffn, b_ffn_dw=jnp.concatenate([dbf_g, dbf_v], axis=1), g_final=d_gfin,
                 w_conv_dw=d_wc, w_ffn_dw=jnp.concatenate([dwf_g, dwf_v], axis=1), dmod=dmod)
    after = put_small(small)
    dw_in = matmul(h1, dproj, "tn", BF16, 256, 512, "proj_bwd_w")
    put_grad("w_in", dw_in, after)
    return loss_t[0, 0], dx


def _padw(a, width):
    return jnp.pad(a, ((0, 0), (0, width - a.shape[1])))


def pack_small(t):
    wide = jnp.concatenate([_padw(t["dmod"], PACK_W), _padw(t["b_ffn_dw"], PACK_W), _padw(t["w_ffn_dw"], PACK_W),
                            jnp.zeros((3, PACK_W), F32)], axis=0)
    z512 = jnp.zeros((1, 512), F32)
    narrow = jnp.concatenate([
        t["g_norm_mix"], t["g_norm_ffn"], t["g_final"],
        jnp.concatenate([t["b_conv_dw"], t["ln_conv_g"]], axis=1),
        jnp.concatenate([t["ln_conv_b"], t["g_conv_out"]], axis=1),
        jnp.concatenate([t["g_attn_out"], z512], axis=1),
        jnp.zeros((2, 1024), F32),
        jnp.pad(t["w_conv_dw"], ((0, 1), (0, 0))).reshape(16, 1024)], axis=0)
    return jnp.concatenate([wide, narrow.reshape(4, PACK_W), jnp.zeros((4, PACK_W), F32)], axis=0)


def unpack_small(p):
    narrow = p[8:12].reshape(24, 1024)
    return dict(
        dmod=p[0:1], b_ffn_dw=p[1:2, :2 * D_FF], w_ffn_dw=p[2:5, :2 * D_FF],
        g_norm_mix=narrow[0:1], g_norm_ffn=narrow[1:2], g_final=narrow[2:3],
        b_conv_dw=narrow[3:4, :512], ln_conv_g=narrow[3:4, 512:], ln_conv_b=narrow[4:5, :512], g_conv_out=narrow[4:5, 512:],
        g_attn_out=narrow[5:6, :512], w_conv_dw=narrow[8:24].reshape(32, 512)[:CONV_K])


def _embed(local, width, me):
    return lax.dynamic_update_slice(jnp.zeros((local.shape[0], width), F32), local, (0, me * local.shape[1]))


def _shard(full, n_cols, me):
    return lax.dynamic_slice(full, (0, me * n_cols), (full.shape[0], n_cols))


WEIGHTS = ["w_ada", "b_ada", "g_norm_mix", "w_in", "w_conv_dw", "b_conv_dw", "ln_conv_g", "ln_conv_b", "g_conv_out",
           "g_attn_out", "w_out", "g_norm_ffn", "w_up", "w_ffn_dw", "b_ffn_dw", "w_down", "g_final"]
SMALL_REPLICATED = ["g_norm_mix", "b_conv_dw", "ln_conv_g", "ln_conv_b", "g_conv_out", "g_attn_out", "g_norm_ffn",
                    "b_ffn_dw", "g_final"]


def kernel(x, c, w_ada, b_ada, g_norm_mix, w_in, w_conv_dw, b_conv_dw, ln_conv_g, ln_conv_b, g_conv_out, g_attn_out, w_out, g_norm_ffn, w_up, w_ffn_dw, b_ffn_dw, w_down, g_final, loss_target, m_w_ada, m_b_ada, m_g_norm_mix, m_w_in, m_w_conv_dw, m_b_conv_dw, m_ln_conv_g, m_ln_conv_b, m_g_conv_out, m_g_attn_out, m_w_out, m_g_norm_ffn, m_w_up, m_w_ffn_dw, m_b_ffn_dw, m_w_down, m_g_final, v_w_ada, v_b_ada, v_g_norm_mix, v_w_in, v_w_conv_dw, v_b_conv_dw, v_ln_conv_g, v_ln_conv_b, v_g_conv_out, v_g_attn_out, v_w_out, v_g_norm_ffn, v_w_up, v_w_ffn_dw, v_b_ffn_dw, v_w_down, v_g_final):
    args = dict(locals())
    me = 4 * lax.axis_index("x") + 2 * lax.axis_index("y") + lax.axis_index("c")

    def flat(name, prefix=""):
        a = args[prefix + name]
        return a.reshape(a.shape[-2] if a.ndim > 1 else 1, a.shape[-1])

    n_in, n_up, r_out, r_down = w_in.shape[2], w_up.shape[2], w_out.shape[1], w_down.shape[1]
    n_ada, n_wc, n_wf = w_ada.shape[2], w_conv_dw.shape[2], w_ffn_dw.shape[2]
    taps_c = jnp.pad(flat("w_conv_dw").reshape(1, CONV_K * n_wc), ((0, 0), (0, 2 * D_MODEL - CONV_K * n_wc)))
    taps_f = jnp.pad(flat("w_ffn_dw").reshape(1, FFN_K * n_wf), ((0, 0), (0, 3 * D_MODEL - FFN_K * n_wf)))
    first = jnp.concatenate([c, taps_c.reshape(2, D_MODEL), taps_f.reshape(3, D_MODEL), jnp.zeros((2, D_MODEL), F32)], axis=0)
    first_all = all_gather(first, "gather_c_taps")
    c_all = first_all[:, 0, :]
    wc_full = first_all[:, 1:3, :].reshape(N_DEV, 2 * D_MODEL)[:, :CONV_K * n_wc].reshape(N_DEV, CONV_K, n_wc)
    wc_full = wc_full.transpose(1, 0, 2).reshape(CONV_K, D_CONV)
    wf_full = first_all[:, 3:6, :].reshape(N_DEV, 3 * D_MODEL)[:, :FFN_K * n_wf].reshape(N_DEV, FFN_K, n_wf)
    wf_full = wf_full.transpose(1, 0, 2).reshape(FFN_K, 2 * D_FF)
    mod_cols = ada_fwd(c_all, flat("w_ada"), _shard(flat("b_ada"), n_ada, me), "ada_fwd")
    mod_all = all_gather(mod_cols, "gather_mod")
    mod = lax.dynamic_index_in_dim(mod_all, me, axis=1, keepdims=False).reshape(N_MOD, D_MODEL)
    mod = jnp.pad(mod, ((0, 2), (0, 0)))

    blocks = dict(w_in=flat("w_in").astype(BF16), w_up=flat("w_up").astype(BF16),
                  rows=jnp.concatenate([flat("w_out"), flat("w_down")], axis=0).astype(BF16))
    gathers, tok = {}, None
    for name in ("w_in", "rows", "w_up"):
        src = blocks[name] if tok is None else blocks[name] + tok.astype(BF16)
        gathers[name], tok = exchange_start(src, True, f"gather_{name}_start", mod_all)
    mod = mod + tok
    slot = lax.broadcasted_iota(jnp.int32, (N_DEV, 1, 1), 0)
    full = {}

    def gathered(name, after):
        land = exchange_wait(gathers[name], after, True, f"gather_{name}_wait")
        return jnp.where(slot == me, blocks[name][None], land)

    def get_w(name, after):
        if name in ("w_out", "w_down"):
            if "rows" not in full:
                full["rows"] = gathered("rows", after)
            rows = full["rows"]
            return rows[:, :r_out, :].reshape(D_MODEL, D_MODEL) if name == "w_out" else rows[:, r_out:, :].reshape(D_FF, D_MODEL)
        return gathered(name, after).transpose(1, 0, 2).reshape(D_MODEL, -1)

    exchanges, own, kept = {}, {}, {}

    def put_grad(name, dw, after=None):
        if name in ("w_in", "w_up"):
            dev_major = dw.reshape(D_MODEL, N_DEV, -1).transpose(1, 0, 2)
        else:
            dev_major = dw.reshape(N_DEV, -1, D_MODEL)
        own[name] = lax.dynamic_index_in_dim(dev_major, me, axis=0, keepdims=False)
        exchanges[name], token = exchange_start(dev_major, False, f"exchange_{name}_start", after)
        return token

    def put_small(small):
        kept["small_all"] = all_gather(pack_small(small), "gather_small")
        return kept["small_all"]

    loss_part, grad_x = local_step(
        x[0], loss_target[0], mod, get_w, put_grad, put_small, wc_full, wf_full,
        flat("g_norm_mix"), flat("b_conv_dw"), flat("ln_conv_g"), flat("ln_conv_b"), flat("g_conv_out"),
        flat("g_attn_out"), flat("g_norm_ffn"), flat("b_ffn_dw"), flat("g_final"))
    loss = lax.psum(loss_part, ("x", "y", "c"))
    small_all = kept["small_all"]

    out = {}

    def finish(name, tr, after):
        parts = exchange_wait(exchanges[name], after, False, f"exchange_{name}_wait")
        out[name] = sum_adamw(parts, own[name], flat(name), flat(name, "m_"), flat(name, "v_"), tr, "adamw_" + name)
        return out[name][0]

    def packed(prefix):
        t = {n: flat(n, prefix) for n in SMALL_REPLICATED}
        t["dmod"] = flat("b_ada", prefix)
        t["w_conv_dw"] = _embed(flat("w_conv_dw", prefix), D_CONV, me)
        t["w_ffn_dw"] = _embed(flat("w_ffn_dw", prefix), 2 * D_FF, me)
        return pack_small(t)

    res = sum_adamw(small_all, None, packed(""), packed("m_"), packed("v_"), PACK_ROWS, "adamw_small")
    after = res[0]
    res = [unpack_small(r) for r in res]
    for n in SMALL_REPLICATED:
        out[n] = tuple(r[n] for r in res)
    out["b_ada"] = tuple(r["dmod"] for r in res)
    out["w_conv_dw"] = tuple(_shard(r["w_conv_dw"], n_wc, me) for r in res)
    out["w_ffn_dw"] = tuple(_shard(r["w_ffn_dw"], n_wf, me) for r in res)

    dmod_cols = _shard(small_all[:, 0, :], n_ada, me)
    out["w_ada"] = ada_bwd_adamw(c_all.T, dmod_cols, flat("w_ada"), flat("w_ada", "m_"), flat("w_ada", "v_"), "adamw_w_ada")

    after = finish("w_down", r_down, out["w_ada"][0])
    after = finish("w_up", 256, after)
    after = finish("w_out", r_out, after)
    finish("w_in", 256, after)

    result = [loss, grad_x[None]]
    for k in range(4):
        result += [out[n][k].reshape(args[n].shape) for n in WEIGHTS]
    return tuple(result)
```

```python
import functools

import jax
import jax.numpy as jnp
from jax import lax
from jax.experimental import pallas as pl
from jax.experimental.pallas import tpu as pltpu

F32 = jnp.float32
BF16 = jnp.bfloat16

N_DEV = 8
SEQ = 2048
D_MODEL = 1024
D_CONV = 512
D_ATTN = 512
HEAD_DIM = 64
CONV_K = 31
D_FF = 2816
FFN_K = 3
D_IN = 2 * D_CONV + 3 * D_ATTN
N_MOD = 6
EPS = 1e-6
ATTN_BLOCK = 128
PATTERNS = ((2048, 1), (512, 4), (128, 16))
NEG = -1e30

ADAM_LR, ADAM_B1, ADAM_B2, ADAM_EPS, ADAM_WD, ADAM_STEP = 0.001, 0.9, 0.999, 1e-08, 0.01, 10

ROWS = 256
CONV_HALO = 32
FFN_HALO = 8
FFN_TN = 1408
VMEM_LIMIT = 56 * 1024 * 1024
PACK_ROWS, PACK_W = 16, 6144


def _cp(*sem):
    return pltpu.CompilerParams(dimension_semantics=sem if sem else None, vmem_limit_bytes=VMEM_LIMIT)


def _sig(x):
    return 1.0 / (1.0 + jnp.exp(-x))


def _rsum(x):
    return jnp.sum(x, axis=0, keepdims=True)


def _mean(x):
    return jnp.mean(x, axis=-1, keepdims=True)


def _acc(ref, val, first):
    @pl.when(first)
    def _():
        ref[...] = val

    @pl.when(jnp.logical_not(first))
    def _():
        ref[...] += val


def matmul(a, b, kind, out_dtype, tm, tn, name):
    if kind == "nn":
        (m, k), n = a.shape, b.shape[1]
        a_spec = pl.BlockSpec((tm, k), lambda j, i: (i, 0))
        b_spec = pl.BlockSpec((k, tn), lambda j, i: (0, j))
        dims = (((1,), (0,)), ((), ()))
    elif kind == "nt":
        (m, k), n = a.shape, b.shape[0]
        a_spec = pl.BlockSpec((tm, k), lambda j, i: (i, 0))
        b_spec = pl.BlockSpec((tn, k), lambda j, i: (j, 0))
        dims = (((1,), (1,)), ((), ()))
    else:
        (k, m), n = a.shape, b.shape[1]
        a_spec = pl.BlockSpec((k, tm), lambda j, i: (0, i))
        b_spec = pl.BlockSpec((k, tn), lambda j, i: (0, j))
        dims = (((0,), (0,)), ((), ()))
    assert m % tm == 0 and n % tn == 0, (name, m, n, tm, tn)

    def body(a_ref, b_ref, o_ref):
        o_ref[...] = lax.dot_general(a_ref[...], b_ref[...], dims, preferred_element_type=F32).astype(o_ref.dtype)

    return pl.pallas_call(
        body, out_shape=jax.ShapeDtypeStruct((m, n), out_dtype), grid=(n // tn, m // tm),
        in_specs=[a_spec, b_spec], out_specs=pl.BlockSpec((tm, tn), lambda j, i: (i, j)),
        name=name, compiler_params=_cp("parallel", "parallel"))(a, b)


def _row_spec(width, col=0):
    return pl.BlockSpec((ROWS, width), lambda i: (i, col))


def _vec_spec(width, rows=1):
    return pl.BlockSpec((rows, width), lambda i: (0, 0))


def rms_mod_fwd(x, g, mod, sh_row, sc_row, name):
    def body(x_ref, g_ref, mod_ref, h_ref):
        xx = x_ref[...]
        r = lax.rsqrt(_mean(xx * xx) + EPS)
        h = xx * r * g_ref[...]
        h_ref[...] = (h * (1.0 + mod_ref[sc_row:sc_row + 1, :]) + mod_ref[sh_row:sh_row + 1, :]).astype(BF16)

    return pl.pallas_call(
        body, out_shape=jax.ShapeDtypeStruct((SEQ, D_MODEL), BF16), grid=(SEQ // ROWS,),
        in_specs=[_row_spec(D_MODEL), _vec_spec(D_MODEL), _vec_spec(D_MODEL, 8)],
        out_specs=_row_spec(D_MODEL), name=name, compiler_params=_cp("parallel"))(x, g, mod)


def resid_rms_mod_fwd(x, y, g, mod, ga_row, sh_row, sc_row, name):
    def body(x_ref, y_ref, g_ref, mod_ref, x1_ref, h_ref):
        x1 = x_ref[...] + mod_ref[ga_row:ga_row + 1, :] * y_ref[...]
        x1_ref[...] = x1
        r = lax.rsqrt(_mean(x1 * x1) + EPS)
        h = x1 * r * g_ref[...]
        h_ref[...] = (h * (1.0 + mod_ref[sc_row:sc_row + 1, :]) + mod_ref[sh_row:sh_row + 1, :]).astype(BF16)

    return pl.pallas_call(
        body, out_shape=(jax.ShapeDtypeStruct((SEQ, D_MODEL), F32), jax.ShapeDtypeStruct((SEQ, D_MODEL), BF16)),
        grid=(SEQ // ROWS,),
        in_specs=[_row_spec(D_MODEL), _row_spec(D_MODEL), _vec_spec(D_MODEL), _vec_spec(D_MODEL, 8)],
        out_specs=(_row_spec(D_MODEL), _row_spec(D_MODEL)), name=name, compiler_params=_cp("parallel"))(x, y, g, mod)


def final_loss_bwd(x1, y2, tgt, g, mod, ga_row, name):
    def body(x1_ref, y2_ref, t_ref, g_ref, mod_ref, loss_ref, dx2_ref, dy2_ref, dg_ref, dga_ref):
        first = pl.program_id(0) == 0
        ga = mod_ref[ga_row:ga_row + 1, :]
        y2 = y2_ref[...]
        x2 = x1_ref[...] + ga * y2
        r = lax.rsqrt(_mean(x2 * x2) + EPS)
        xn = x2 * r
        err = xn * g_ref[...] - t_ref[...]
        _acc(loss_ref, jnp.broadcast_to(0.5 * jnp.sum(_mean(err * err)), (8, 128)), first)
        dy = err * (1.0 / D_MODEL)
        _acc(dg_ref, _rsum(dy * xn), first)
        dxn = dy * g_ref[...]
        dx2 = r * (dxn - xn * _mean(dxn * xn))
        dx2_ref[...] = dx2
        dy2_ref[...] = (dx2 * ga).astype(BF16)
        _acc(dga_ref, _rsum(dx2 * y2), first)

    vec = jax.ShapeDtypeStruct((1, D_MODEL), F32)
    return pl.pallas_call(
        body,
        out_shape=(jax.ShapeDtypeStruct((8, 128), F32), jax.ShapeDtypeStruct((SEQ, D_MODEL), F32),
                   jax.ShapeDtypeStruct((SEQ, D_MODEL), BF16), vec, vec),
        grid=(SEQ // ROWS,),
        in_specs=[_row_spec(D_MODEL), _row_spec(D_MODEL), _row_spec(D_MODEL), _vec_spec(D_MODEL), _vec_spec(D_MODEL, 8)],
        out_specs=(pl.BlockSpec((8, 128), lambda i: (0, 0)), _row_spec(D_MODEL), _row_spec(D_MODEL),
                   _vec_spec(D_MODEL), _vec_spec(D_MODEL)),
        name=name, compiler_params=_cp("arbitrary"))(x1, y2, tgt, g, mod)


def rms_mod_bwd(x, dh, dres, g, mod, sc_row, y, ga_row, name):
    gated = y is not None

    def body(*refs):
        if gated:
            x_ref, dh_ref, dres_ref, g_ref, mod_ref, y_ref, dx_ref, dsh_ref, dsc_ref, dg_ref, dy_ref, dga_ref = refs
        else:
            x_ref, dh_ref, dres_ref, g_ref, mod_ref, dx_ref, dsh_ref, dsc_ref, dg_ref = refs
        first = pl.program_id(0) == 0
        xx = x_ref[...]
        dh = dh_ref[...]
        gg = g_ref[...]
        r = lax.rsqrt(_mean(xx * xx) + EPS)
        xn = xx * r
        _acc(dsh_ref, _rsum(dh), first)
        _acc(dsc_ref, _rsum(dh * (xn * gg)), first)
        dt = dh * (1.0 + mod_ref[sc_row:sc_row + 1, :])
        _acc(dg_ref, _rsum(dt * xn), first)
        dxn = dt * gg
        dx = dres_ref[...] + r * (dxn - xn * _mean(dxn * xn))
        dx_ref[...] = dx
        if gated:
            _acc(dga_ref, _rsum(dx * y_ref[...]), first)
            dy_ref[...] = (dx * mod_ref[ga_row:ga_row + 1, :]).astype(BF16)

    vec = jax.ShapeDtypeStruct((1, D_MODEL), F32)
    in_specs = [_row_spec(D_MODEL), _row_spec(D_MODEL), _row_spec(D_MODEL), _vec_spec(D_MODEL), _vec_spec(D_MODEL, 8)]
    out_shape = [jax.ShapeDtypeStruct((SEQ, D_MODEL), F32), vec, vec, vec]
    out_specs = [_row_spec(D_MODEL), _vec_spec(D_MODEL), _vec_spec(D_MODEL), _vec_spec(D_MODEL)]
    args = [x, dh, dres, g, mod]
    if gated:
        in_specs.append(_row_spec(D_MODEL))
        out_shape += [jax.ShapeDtypeStruct((SEQ, D_MODEL), BF16), vec]
        out_specs += [_row_spec(D_MODEL), _vec_spec(D_MODEL)]
        args.append(y)
    return pl.pallas_call(
        body, out_shape=tuple(out_shape), grid=(SEQ // ROWS,), in_specs=in_specs, out_specs=tuple(out_specs),
        name=name, compiler_params=_cp("arbitrary"))(*args)


def _prev_halo(halo, width, col):
    per = ROWS // halo
    return pl.BlockSpec((halo, width), lambda i: (jnp.maximum(i * per - 1, 0), col))


def _next_halo(halo, width, col):
    per = ROWS // halo
    last = SEQ // halo - 1
    return pl.BlockSpec((halo, width), lambda i: (jnp.minimum((i + 1) * per, last), col))


def _conv_module_forward(av_ref, ag_ref, avh_ref, agh_ref, wc_ref, bc_ref, lg_ref, lb_ref, u0p):
    i = pl.program_id(0)
    hv = avh_ref[...] * _sig(agh_ref[...])
    u0p[0:CONV_HALO, :] = jnp.where(i > 0, hv, 0.0)
    u0p[CONV_HALO:, :] = av_ref[...] * _sig(ag_ref[...])
    u1 = jnp.broadcast_to(bc_ref[...], (ROWS, D_CONV))
    for j in range(CONV_K):
        u1 = u1 + wc_ref[j:j + 1, :] * u0p[pl.ds(CONV_HALO - (CONV_K - 1) + j, ROWS), :]
    mu = _mean(u1)
    cen = u1 - mu
    rs = lax.rsqrt(_mean(cen * cen) + EPS)
    z = cen * rs
    ln = z * lg_ref[...] + lb_ref[...]
    s = _sig(ln)
    return z, rs, ln, s, ln * s


def conv_module_fwd(proj, wc, bc, lg, lb, gco, name):
    def body(av_ref, ag_ref, avh_ref, agh_ref, wc_ref, bc_ref, lg_ref, lb_ref, gco_ref, out_ref, u0p):
        _, _, _, _, u2 = _conv_module_forward(av_ref, ag_ref, avh_ref, agh_ref, wc_ref, bc_ref, lg_ref, lb_ref, u0p)
        rc = lax.rsqrt(_mean(u2 * u2) + EPS)
        out_ref[...] = (u2 * rc * gco_ref[...]).astype(BF16)

    v = _vec_spec(D_CONV)
    return pl.pallas_call(
        body, out_shape=jax.ShapeDtypeStruct((SEQ, D_CONV), BF16), grid=(SEQ // ROWS,),
        in_specs=[_row_spec(D_CONV, 0), _row_spec(D_CONV, 1), _prev_halo(CONV_HALO, D_CONV, 0),
                  _prev_halo(CONV_HALO, D_CONV, 1), _vec_spec(D_CONV, CONV_K), v, v, v, v],
        out_specs=_row_spec(D_CONV), scratch_shapes=[pltpu.VMEM((ROWS + CONV_HALO, D_CONV), F32)],
        name=name, compiler_params=_cp("parallel"))(proj, proj, proj, proj, wc, bc, lg, lb, gco)


def conv_module_bwd_a(proj, dmixed, wc, bc, lg, lb, gco, name):
    def body(av_ref, ag_ref, avh_ref, agh_ref, dm_ref, wc_ref, bc_ref, lg_ref, lb_ref, gco_ref,
             du1_ref, dgco_ref, dlg_ref, dlb_ref, dbc_ref, dwc_ref, u0p):
        first = pl.program_id(0) == 0
        z, rs, ln, s, u2 = _conv_module_forward(av_ref, ag_ref, avh_ref, agh_ref, wc_ref, bc_ref, lg_ref, lb_ref, u0p)
        rc = lax.rsqrt(_mean(u2 * u2) + EPS)
        xn = u2 * rc
        dm = dm_ref[...]
        _acc(dgco_ref, _rsum(dm * xn), first)
        dyn = dm * gco_ref[...]
        du2 = rc * (dyn - xn * _mean(dyn * xn))
        dln = du2 * (s * (1.0 + ln * (1.0 - s)))
        _acc(dlg_ref, _rsum(dln * z), first)
        _acc(dlb_ref, _rsum(dln), first)
        dz = dln * lg_ref[...]
        du1 = rs * (dz - _mean(dz) - z * _mean(dz * z))
        du1_ref[...] = du1
        _acc(dbc_ref, _rsum(du1), first)

        @pl.when(first)
        def _():
            dwc_ref[...] = jnp.zeros_like(dwc_ref)

        for j in range(CONV_K):
            dwc_ref[j:j + 1, :] += _rsum(du1 * u0p[pl.ds(CONV_HALO - (CONV_K - 1) + j, ROWS), :])

    v = _vec_spec(D_CONV)
    vec = jax.ShapeDtypeStruct((1, D_CONV), F32)
    return pl.pallas_call(
        body,
        out_shape=(jax.ShapeDtypeStruct((SEQ, D_CONV), F32), vec, vec, vec, vec, jax.ShapeDtypeStruct((CONV_K, D_CONV), F32)),
        grid=(SEQ // ROWS,),
        in_specs=[_row_spec(D_CONV, 0), _row_spec(D_CONV, 1), _prev_halo(CONV_HALO, D_CONV, 0),
                  _prev_halo(CONV_HALO, D_CONV, 1), _row_spec(D_CONV, 0), _vec_spec(D_CONV, CONV_K), v, v, v, v],
        out_specs=(_row_spec(D_CONV), v, v, v, v, _vec_spec(D_CONV, CONV_K)),
        scratch_shapes=[pltpu.VMEM((ROWS + CONV_HALO, D_CONV), F32)],
        name=name, compiler_params=_cp("arbitrary"))(proj, proj, proj, proj, dmixed, wc, bc, lg, lb, gco)


def conv_module_bwd_b(proj, du1, wc, name):
    def body(av_ref, ag_ref, du1_ref, du1n_ref, wc_ref, out_ref, dup):
        i = pl.program_id(0)
        dup[0:ROWS, :] = du1_ref[...]
        dup[ROWS:, :] = jnp.where(i < SEQ // ROWS - 1, du1n_ref[...], 0.0)
        du0 = jnp.zeros((ROWS, D_CONV), F32)
        for j in range(CONV_K):
            du0 = du0 + wc_ref[j:j + 1, :] * dup[pl.ds(CONV_K - 1 - j, ROWS), :]
        sg = _sig(ag_ref[...])
        out_ref[:, 0:D_CONV] = (du0 * sg).astype(BF16)
        out_ref[:, D_CONV:] = (du0 * av_ref[...] * sg * (1.0 - sg)).astype(BF16)

    return pl.pallas_call(
        body, out_shape=jax.ShapeDtypeStruct((SEQ, 2 * D_CONV), BF16), grid=(SEQ // ROWS,),
        in_specs=[_row_spec(D_CONV, 0), _row_spec(D_CONV, 1), _row_spec(D_CONV, 0), _next_halo(CONV_HALO, D_CONV, 0),
                  _vec_spec(D_CONV, CONV_K)],
        out_specs=_row_spec(2 * D_CONV), scratch_shapes=[pltpu.VMEM((ROWS + CONV_HALO, D_CONV), F32)],
        name=name, compiler_params=_cp("parallel"))(proj, proj, du1, du1, wc)


def _attn_specs(sub_len, pairs):
    ng, width = 4 // pairs, 128 * pairs
    q = pl.BlockSpec((sub_len, width), lambda rho, g: (0, rho * 3 * ng + g))
    k = pl.BlockSpec((sub_len, width), lambda rho, g: (0, rho * 3 * ng + ng + g))
    v = pl.BlockSpec((sub_len, width), lambda rho, g: (0, rho * 3 * ng + 2 * ng + g))
    o = pl.BlockSpec((sub_len, width), lambda rho, g: (0, rho * ng + g))
    return q, k, v, o


ATTN_PAIRS = {1: 1, 4: 1, 16: 4}


def _attn_block(q_ref, k_ref, v_ref, n, hs, win):
    q0 = pl.multiple_of(n * ATTN_BLOCK, ATTN_BLOCK)
    k0 = pl.multiple_of(jnp.maximum(n - 1, 0) * ATTN_BLOCK, ATTN_BLOCK)
    qb = q_ref[pl.ds(q0, ATTN_BLOCK), hs]
    kw = k_ref[pl.ds(k0, win), hs]
    vw = v_ref[pl.ds(k0, win), hs]
    s = lax.dot_general(qb, kw, (((1,), (1,)), ((), ())), preferred_element_type=F32) * (HEAD_DIM ** -0.5)
    dist = (q0 - k0) + lax.broadcasted_iota(jnp.int32, (ATTN_BLOCK, win), 0) \
        - lax.broadcasted_iota(jnp.int32, (ATTN_BLOCK, win), 1)
    s = jnp.where((dist >= 0) & (dist <= ATTN_BLOCK), s, NEG)
    return q0, k0, qb, kw, vw, s


def attn_fwd(qkv_r, sub_len, r, name):
    nb = sub_len // ATTN_BLOCK
    win = 2 * ATTN_BLOCK if nb > 1 else ATTN_BLOCK
    pairs = ATTN_PAIRS[r]

    def body(q_ref, k_ref, v_ref, o_ref, l_ref):
        def block(n, carry):
            for h in range(2 * pairs):
                hs = slice(h * HEAD_DIM, (h + 1) * HEAD_DIM)
                q0, _, _, _, vw, s = _attn_block(q_ref, k_ref, v_ref, n, hs, win)
                m = jnp.max(s, axis=1, keepdims=True)
                p = jnp.exp(s - m)
                den = jnp.sum(p, axis=1, keepdims=True)
                o = jnp.dot(p.astype(BF16), vw, preferred_element_type=F32) / den
                o_ref[pl.ds(q0, ATTN_BLOCK), hs] = o
                l_ref[pl.ds(q0, ATTN_BLOCK), hs] = jnp.broadcast_to(m + jnp.log(den), (ATTN_BLOCK, HEAD_DIM))
            return carry

        lax.fori_loop(0, nb, block, 0, unroll=min(nb, 2))

    q, k, v, o = _attn_specs(sub_len, pairs)
    shp = jax.ShapeDtypeStruct((sub_len, r * D_ATTN), F32)
    return pl.pallas_call(
        body, out_shape=(shp, shp), grid=(r, 4 // pairs), in_specs=[q, k, v], out_specs=(o, o),
        name=name, compiler_params=_cp("parallel", "parallel"))(qkv_r, qkv_r, qkv_r)


def attn_bwd(qkv_r, do_r, lse_r, dd_r, sub_len, r, name):
    nb = sub_len // ATTN_BLOCK
    win = 2 * ATTN_BLOCK if nb > 1 else ATTN_BLOCK
    pairs = ATTN_PAIRS[r]

    def body(q_ref, k_ref, v_ref, do_ref, l_ref, dd_ref, dq_ref, dk_ref, dv_ref):
        dk_ref[...] = jnp.zeros_like(dk_ref)
        dv_ref[...] = jnp.zeros_like(dv_ref)

        def block(n, carry):
            for h in range(2 * pairs):
                hs = slice(h * HEAD_DIM, (h + 1) * HEAD_DIM)
                h1 = slice(h * HEAD_DIM, h * HEAD_DIM + 1)
                q0, k0, qb, kw, vw, s = _attn_block(q_ref, k_ref, v_ref, n, hs, win)
                dob = do_ref[pl.ds(q0, ATTN_BLOCK), hs]
                p = jnp.exp(s - l_ref[pl.ds(q0, ATTN_BLOCK), h1])
                dp = lax.dot_general(dob, vw, (((1,), (1,)), ((), ())), preferred_element_type=F32)
                ds = (p * (dp - dd_ref[pl.ds(q0, ATTN_BLOCK), h1]) * (HEAD_DIM ** -0.5)).astype(BF16)
                dq_ref[pl.ds(q0, ATTN_BLOCK), hs] = jnp.dot(ds, kw, preferred_element_type=F32)
                dk_ref[pl.ds(k0, win), hs] += lax.dot_general(ds, qb, (((0,), (0,)), ((), ())), preferred_element_type=F32)
                dv_ref[pl.ds(k0, win), hs] += lax.dot_general(p.astype(BF16), dob, (((0,), (0,)), ((), ())),
                                                              preferred_element_type=F32)
            return carry

        lax.fori_loop(0, nb, block, 0)

    q, k, v, o = _attn_specs(sub_len, pairs)
    shp = jax.ShapeDtypeStruct((sub_len, r * D_ATTN), F32)
    return pl.pallas_call(
        body, out_shape=(shp, shp, shp), grid=(r, 4 // pairs), in_specs=[q, k, v, o, o, o], out_specs=(o, o, o),
        name=name, compiler_params=_cp("parallel", "parallel"))(qkv_r, qkv_r, qkv_r, do_r, lse_r, dd_r)


def _rows(start, size, r):
    return pl.ds(start, size) if r == 1 else pl.ds(start, size, stride=r)


def _attn_unit(q_ref, k_ref, v_ref, r, rho, n, nb):
    win = 2 * ATTN_BLOCK if nb > 1 else ATTN_BLOCK
    if isinstance(n, int):
        kb = max(n - 1, 0)
        q_rows = _rows(rho + r * ATTN_BLOCK * n, ATTN_BLOCK, r)
        k_rows = _rows(rho + r * ATTN_BLOCK * kb, win, r)
    else:
        kb = jnp.maximum(n - 1, 0)
        q_rows = pl.ds(pl.multiple_of(n * ATTN_BLOCK, ATTN_BLOCK), ATTN_BLOCK)
        k_rows = pl.ds(pl.multiple_of(kb * ATTN_BLOCK, ATTN_BLOCK), win)
    q2 = q_ref[q_rows, :].astype(BF16)
    k2 = k_ref[k_rows, :].astype(BF16)
    v2 = v_ref[k_rows, :].astype(BF16)
    dist = (n - kb) * ATTN_BLOCK + lax.broadcasted_iota(jnp.int32, (ATTN_BLOCK, win), 0) \
        - lax.broadcasted_iota(jnp.int32, (ATTN_BLOCK, win), 1)
    valid = (dist >= 0) & (dist <= ATTN_BLOCK)

    def score(h):
        hs = slice(h * HEAD_DIM, (h + 1) * HEAD_DIM)
        s = lax.dot_general(q2[:, hs], k2[:, hs], (((1,), (1,)), ((), ())), preferred_element_type=F32)
        return jnp.where(valid, s * (HEAD_DIM ** -0.5), NEG), q2[:, hs], k2[:, hs], v2[:, hs]

    return q_rows, k_rows, score


def _attn_units(r, nb, unit):
    if r == 1:
        def two(i, carry):
            unit(0, 2 * i)
            unit(0, 2 * i + 1)
            return carry
        lax.fori_loop(0, nb // 2, two, 0)
    else:
        for rho in range(r):
            for n in range(nb):
                unit(rho, n)


def attn_fwd_all(proj, name):
    def body(q_ref, k_ref, v_ref, att_ref, lse_ref):
        for idx, (sub_len, r) in enumerate(PATTERNS):
            nb = sub_len // ATTN_BLOCK

            def unit(rho, n, r=r, nb=nb, idx=idx):
                q_rows, _, score = _attn_unit(q_ref, k_ref, v_ref, r, rho, n, nb)
                outs, lses = [], []
                for h in range(2):
                    s, _, _, vw = score(h)
                    m = jnp.max(s, axis=1, keepdims=True)
                    p = jnp.exp(s - m)
                    den = jnp.sum(p, axis=1, keepdims=True)
                    outs.append(jnp.dot(p.astype(BF16), vw, preferred_element_type=F32) / den)
                    lses.append(jnp.broadcast_to(m + jnp.log(den), (ATTN_BLOCK, HEAD_DIM)))
                o = jnp.concatenate(outs, axis=1)
                lse = jnp.concatenate(lses, axis=1)
                if idx > 0:
                    old = lse_ref[q_rows, :]
                    top = jnp.maximum(old, lse)
                    new = top + jnp.log(jnp.exp(old - top) + jnp.exp(lse - top))
                    o = att_ref[q_rows, :] * jnp.exp(old - new) + o * jnp.exp(lse - new)
                    lse = new
                att_ref[q_rows, :] = o
                lse_ref[q_rows, :] = lse

            _attn_units(r, nb, unit)

    blk = lambda first: pl.BlockSpec((SEQ, 128), lambda g: (0, first + g))
    shp = jax.ShapeDtypeStruct((SEQ, D_ATTN), F32)
    return pl.pallas_call(
        body, out_shape=(shp, shp), grid=(4,), in_specs=[blk(8), blk(12), blk(16)], out_specs=(blk(0), blk(0)),
        name=name, compiler_params=_cp("parallel"))(proj, proj, proj)


def attn_bwd_all(proj, do, lse, dd, name):
    def body(q_ref, k_ref, v_ref, do_ref, l_ref, dd_ref, out_ref, dq_s, dk_s, dv_s):
        dq_s[...] = jnp.zeros_like(dq_s)
        dk_s[...] = jnp.zeros_like(dk_s)
        dv_s[...] = jnp.zeros_like(dv_s)
        for sub_len, r in PATTERNS:
            nb = sub_len // ATTN_BLOCK

            def unit(rho, n, r=r, nb=nb):
                q_rows, k_rows, score = _attn_unit(q_ref, k_ref, v_ref, r, rho, n, nb)
                do2 = do_ref[q_rows, :].astype(BF16)
                l2 = l_ref[q_rows, :]
                d2 = dd_ref[q_rows, :]
                dq, dk, dv = [], [], []
                for h in range(2):
                    hs = slice(h * HEAD_DIM, (h + 1) * HEAD_DIM)
                    h1 = slice(h * HEAD_DIM, h * HEAD_DIM + 1)
                    s, qb, kw, vw = score(h)
                    dob = do2[:, hs]
                    p = jnp.exp(s - l2[:, h1])
                    dp = lax.dot_general(dob, vw, (((1,), (1,)), ((), ())), preferred_element_type=F32)
                    ds = (p * (dp - d2[:, h1]) * (HEAD_DIM ** -0.5)).astype(BF16)
                    dq.append(jnp.dot(ds, kw, preferred_element_type=F32))
                    dk.append(lax.dot_general(ds, qb, (((0,), (0,)), ((), ())), preferred_element_type=F32))
                    dv.append(lax.dot_general(p.astype(BF16), dob, (((0,), (0,)), ((), ())), preferred_element_type=F32))
                dq_s[q_rows, :] += jnp.concatenate(dq, axis=1)
                dk_s[k_rows, :] += jnp.concatenate(dk, axis=1)
                dv_s[k_rows, :] += jnp.concatenate(dv, axis=1)

            _attn_units(r, nb, unit)
        out_ref[0] = dq_s[...].astype(BF16)
        out_ref[1] = dk_s[...].astype(BF16)
        out_ref[2] = dv_s[...].astype(BF16)

    blk = lambda first: pl.BlockSpec((SEQ, 128), lambda g: (0, first + g))
    acc = pltpu.VMEM((SEQ, 128), F32)
    return pl.pallas_call(
        body, out_shape=jax.ShapeDtypeStruct((3, SEQ, D_ATTN), BF16), grid=(4,),
        in_specs=[blk(8), blk(12), blk(16), blk(0), blk(0), blk(0)],
        out_specs=pl.BlockSpec((3, SEQ, 128), lambda g: (0, 0, g)), scratch_shapes=[acc, acc, acc],
        name=name, compiler_params=_cp("parallel"))(proj, proj, proj, do, lse, dd)


def rms_gain_bf16(a, g, name):
    def body(a_ref, g_ref, o_ref):
        aa = a_ref[...]
        o_ref[...] = (aa * lax.rsqrt(_mean(aa * aa) + EPS) * g_ref[...]).astype(BF16)

    w = a.shape[1]
    return pl.pallas_call(
        body, out_shape=jax.ShapeDtypeStruct(a.shape, BF16), grid=(SEQ // ROWS,), in_specs=[_row_spec(w), _vec_spec(w)],
        out_specs=_row_spec(w), name=name, compiler_params=_cp("parallel"))(a, g)


def attn_combine_fwd(outs, lses, gao, name):
    def body(o1, o2, o3, l1, l2, l3, g_ref, att_ref, lse_ref, mix_ref):
        a1, a2, a3 = l1[...], l2[...], l3[...]
        m = jnp.maximum(jnp.maximum(a1, a2), a3)
        w1, w2, w3 = jnp.exp(a1 - m), jnp.exp(a2 - m), jnp.exp(a3 - m)
        den = w1 + w2 + w3
        att = (w1 * o1[...] + w2 * o2[...] + w3 * o3[...]) / den
        att_ref[...] = att
        lse_ref[...] = m + jnp.log(den)
        mix_ref[...] = (att * lax.rsqrt(_mean(att * att) + EPS) * g_ref[...]).astype(BF16)

    rs = _row_spec(D_ATTN)
    f = jax.ShapeDtypeStruct((SEQ, D_ATTN), F32)
    return pl.pallas_call(
        body, out_shape=(f, f, jax.ShapeDtypeStruct((SEQ, D_ATTN), BF16)), grid=(SEQ // ROWS,),
        in_specs=[rs] * 6 + [_vec_spec(D_ATTN)], out_specs=(rs, rs, rs),
        name=name, compiler_params=_cp("parallel"))(*outs, *lses, gao)


def attn_combine_bwd(dmixed, att, gao, name):
    def body(dm_ref, att_ref, g_ref, do_ref, dd_ref, dg_ref):
        first = pl.program_id(0) == 0
        att = att_ref[...]
        r = lax.rsqrt(_mean(att * att) + EPS)
        xn = att * r
        dm = dm_ref[...]
        _acc(dg_ref, _rsum(dm * xn), first)
        dyn = dm * g_ref[...]
        do = r * (dyn - xn * _mean(dyn * xn))
        do_ref[...] = do
        same_head = (jnp.right_shift(lax.broadcasted_iota(jnp.int32, (D_ATTN, D_ATTN), 0), 6)
                     == jnp.right_shift(lax.broadcasted_iota(jnp.int32, (D_ATTN, D_ATTN), 1), 6)).astype(F32)
        dd_ref[...] = jnp.dot(do * att, same_head, preferred_element_type=F32, precision=lax.Precision.HIGHEST)

    rs = _row_spec(D_ATTN)
    return pl.pallas_call(
        body,
        out_shape=(jax.ShapeDtypeStruct((SEQ, D_ATTN), F32), jax.ShapeDtypeStruct((SEQ, D_ATTN), F32),
                   jax.ShapeDtypeStruct((1, D_ATTN), F32)),
        grid=(SEQ // ROWS,), in_specs=[_row_spec(D_ATTN, 1), rs, _vec_spec(D_ATTN)],
        out_specs=(rs, rs, _vec_spec(D_ATTN)), name=name, compiler_params=_cp("arbitrary"))(dmixed, att, gao)


def sum3_bf16(a, b, c, name):
    def body(a_ref, b_ref, c_ref, o_ref):
        o_ref[...] = (a_ref[...] + b_ref[...] + c_ref[...]).astype(BF16)

    w = a.shape[1]
    rs = _row_spec(w)
    return pl.pallas_call(
        body, out_shape=jax.ShapeDtypeStruct(a.shape, BF16), grid=(SEQ // ROWS,), in_specs=[rs, rs, rs], out_specs=rs,
        name=name, compiler_params=_cp("parallel"))(a, b, c)


N_FT = D_FF // FFN_TN


def _ffn_specs():
    per = ROWS // FFN_HALO
    cur_g = pl.BlockSpec((ROWS, FFN_TN), lambda j, i: (i, j))
    cur_v = pl.BlockSpec((ROWS, FFN_TN), lambda j, i: (i, j + N_FT))
    halo_g = pl.BlockSpec((FFN_HALO, FFN_TN), lambda j, i: (jnp.maximum(i * per - 1, 0), j))
    halo_v = pl.BlockSpec((FFN_HALO, FFN_TN), lambda j, i: (jnp.maximum(i * per - 1, 0), j + N_FT))
    w_g = pl.BlockSpec((FFN_K, FFN_TN), lambda j, i: (0, j))
    w_v = pl.BlockSpec((FFN_K, FFN_TN), lambda j, i: (0, j + N_FT))
    b_g = pl.BlockSpec((1, FFN_TN), lambda j, i: (0, j))
    b_v = pl.BlockSpec((1, FFN_TN), lambda j, i: (0, j + N_FT))
    return [cur_g, cur_v, halo_g, halo_v, w_g, w_v, b_g, b_v]


def _ffn_conv(cur_ref, halo_ref, w_ref, b_ref, pad):
    i = pl.program_id(1)
    pad[0:FFN_HALO, :] = jnp.where(i > 0, halo_ref[...], 0.0)
    pad[FFN_HALO:, :] = cur_ref[...]
    up = jnp.broadcast_to(b_ref[...], (ROWS, FFN_TN))
    for j in range(FFN_K):
        up = up + w_ref[j:j + 1, :] * pad[pl.ds(FFN_HALO - (FFN_K - 1) + j, ROWS), :]
    return up


def ffn_act_fwd(up0, wf, bf, name):
    def body(g_ref, v_ref, gh_ref, vh_ref, wg_ref, wv_ref, bg_ref, bv_ref, act_ref, gpad, vpad):
        gate = _ffn_conv(g_ref, gh_ref, wg_ref, bg_ref, gpad)
        val = _ffn_conv(v_ref, vh_ref, wv_ref, bv_ref, vpad)
        act_ref[...] = (gate * _sig(gate) * val).astype(BF16)

    pad = pltpu.VMEM((ROWS + FFN_HALO, FFN_TN), F32)
    return pl.pallas_call(
        body, out_shape=jax.ShapeDtypeStruct((SEQ, D_FF), BF16), grid=(N_FT, SEQ // ROWS),
        in_specs=_ffn_specs(), out_specs=pl.BlockSpec((ROWS, FFN_TN), lambda j, i: (i, j)),
        scratch_shapes=[pad, pad], name=name, compiler_params=_cp("parallel", "parallel"))(up0, up0, up0, up0, wf, wf, bf, bf)


def ffn_bwd_a(up0, dact, wf, bf, name):
    def body(g_ref, v_ref, gh_ref, vh_ref, wg_ref, wv_ref, bg_ref, bv_ref, da_ref,
             dg_ref, dv_ref, dbg_ref, dbv_ref, dwg_ref, dwv_ref, gpad, vpad):
        first = pl.program_id(1) == 0
        gate = _ffn_conv(g_ref, gh_ref, wg_ref, bg_ref, gpad)
        val = _ffn_conv(v_ref, vh_ref, wv_ref, bv_ref, vpad)
        s = _sig(gate)
        da = da_ref[...]
        dgate = da * val * (s * (1.0 + gate * (1.0 - s)))
        dval = da * (gate * s)
        dg_ref[...] = dgate
        dv_ref[...] = dval
        _acc(dbg_ref, _rsum(dgate), first)
        _acc(dbv_ref, _rsum(dval), first)

        @pl.when(first)
        def _():
            dwg_ref[...] = jnp.zeros_like(dwg_ref)
            dwv_ref[...] = jnp.zeros_like(dwv_ref)

        for j in range(FFN_K):
            off = FFN_HALO - (FFN_K - 1) + j
            dwg_ref[j:j + 1, :] += _rsum(dgate * gpad[pl.ds(off, ROWS), :])
            dwv_ref[j:j + 1, :] += _rsum(dval * vpad[pl.ds(off, ROWS), :])

    pad = pltpu.VMEM((ROWS + FFN_HALO, FFN_TN), F32)
    full = jax.ShapeDtypeStruct((SEQ, D_FF), F32)
    vec = jax.ShapeDtypeStruct((1, D_FF), F32)
    taps = jax.ShapeDtypeStruct((FFN_K, D_FF), F32)
    cur = pl.BlockSpec((ROWS, FFN_TN), lambda j, i: (i, j))
    vs = pl.BlockSpec((1, FFN_TN), lambda j, i: (0, j))
    ts = pl.BlockSpec((FFN_K, FFN_TN), lambda j, i: (0, j))
    return pl.pallas_call(
        body, out_shape=(full, full, vec, vec, taps, taps), grid=(N_FT, SEQ // ROWS),
        in_specs=_ffn_specs() + [cur], out_specs=(cur, cur, vs, vs, ts, ts), scratch_shapes=[pad, pad],
        name=name, compiler_params=_cp("parallel", "arbitrary"))(up0, up0, up0, up0, wf, wf, bf, bf, dact)


def ffn_bwd_b(dgate, dval, wf, name):
    per = ROWS // FFN_HALO
    last = SEQ // FFN_HALO - 1

    def body(g_ref, v_ref, gn_ref, vn_ref, w_ref, out_ref, pad):
        j = pl.program_id(0)
        i = pl.program_id(1)

        def run(cur_ref, nxt_ref):
            pad[0:ROWS, :] = cur_ref[...]
            pad[ROWS:, :] = jnp.where(i < SEQ // ROWS - 1, nxt_ref[...], 0.0)
            acc = jnp.zeros((ROWS, FFN_TN), F32)
            for t in range(FFN_K):
                acc = acc + w_ref[t:t + 1, :] * pad[pl.ds(FFN_K - 1 - t, ROWS), :]
            out_ref[...] = acc.astype(BF16)

        @pl.when(j < N_FT)
        def _():
            run(g_ref, gn_ref)

        @pl.when(j >= N_FT)
        def _():
            run(v_ref, vn_ref)

    cur_g = pl.BlockSpec((ROWS, FFN_TN), lambda j, i: (i, jnp.minimum(j, N_FT - 1)))
    cur_v = pl.BlockSpec((ROWS, FFN_TN), lambda j, i: (i, jnp.maximum(j - N_FT, 0)))
    nxt_g = pl.BlockSpec((FFN_HALO, FFN_TN), lambda j, i: (jnp.minimum((i + 1) * per, last), jnp.minimum(j, N_FT - 1)))
    nxt_v = pl.BlockSpec((FFN_HALO, FFN_TN), lambda j, i: (jnp.minimum((i + 1) * per, last), jnp.maximum(j - N_FT, 0)))
    return pl.pallas_call(
        body, out_shape=jax.ShapeDtypeStruct((SEQ, 2 * D_FF), BF16), grid=(2 * N_FT, SEQ // ROWS),
        in_specs=[cur_g, cur_v, nxt_g, nxt_v, pl.BlockSpec((FFN_K, FFN_TN), lambda j, i: (0, j))],
        out_specs=pl.BlockSpec((ROWS, FFN_TN), lambda j, i: (i, j)),
        scratch_shapes=[pltpu.VMEM((ROWS + FFN_HALO, FFN_TN), F32)],
        name=name, compiler_params=_cp("parallel", "parallel"))(dgate, dval, dgate, dval, wf)


def ada_fwd(c_all, w_ada, b_cols, name):
    def body(c_ref, w_ref, b_ref, o_ref):
        cc = c_ref[...]
        sc = (cc * _sig(cc)).astype(BF16)
        o_ref[...] = jnp.dot(sc, w_ref[...].astype(BF16), preferred_element_type=F32) + b_ref[...]

    return pl.pallas_call(body, out_shape=jax.ShapeDtypeStruct((N_DEV, w_ada.shape[1]), F32), name=name,
                          compiler_params=_cp())(c_all, w_ada, b_cols)


def _adam(w, g, m, v):
    m = ADAM_B1 * m + (1.0 - ADAM_B1) * g
    v = ADAM_B2 * v + (1.0 - ADAM_B2) * (g * g)
    m_hat = m / (1.0 - ADAM_B1 ** ADAM_STEP)
    v_hat = v / (1.0 - ADAM_B2 ** ADAM_STEP)
    delta = -ADAM_LR * (m_hat / (jnp.sqrt(v_hat) + ADAM_EPS) + ADAM_WD * w)
    return delta, m, v


def ada_bwd_adamw(c_all_t, dmod_cols, w, m, v, name):
    rows, cols = w.shape
    tr = 256

    def body(ct_ref, dm_ref, w_ref, m_ref, v_ref, g_ref, d_ref, nm_ref, nv_ref):
        ct = ct_ref[...]
        sc = ct * _sig(ct)
        g = sc[:, 0:1] * dm_ref[0:1, :]
        for b in range(1, N_DEV):
            g = g + sc[:, b:b + 1] * dm_ref[b:b + 1, :]
        g_ref[...] = g
        d_ref[...], nm_ref[...], nv_ref[...] = _adam(w_ref[...], g, m_ref[...], v_ref[...])

    blk = pl.BlockSpec((tr, cols), lambda i: (i, 0))
    shp = jax.ShapeDtypeStruct((rows, cols), F32)
    return pl.pallas_call(
        body, out_shape=(shp, shp, shp, shp), grid=(rows // tr,),
        in_specs=[pl.BlockSpec((tr, N_DEV), lambda i: (i, 0)), pl.BlockSpec((N_DEV, cols), lambda i: (0, 0)), blk, blk, blk],
        out_specs=(blk, blk, blk, blk), name=name, compiler_params=_cp("parallel"))(c_all_t, dmod_cols, w, m, v)


def sum_adamw(parts, own, w, m, v, tr, name):
    n_parts, rows, cols = parts.shape

    def body(*refs):
        if own is None:
            p_ref, w_ref, m_ref, v_ref, g_ref, d_ref, nm_ref, nv_ref = refs
            g = p_ref[0].astype(F32)
        else:
            p_ref, own_ref, w_ref, m_ref, v_ref, g_ref, d_ref, nm_ref, nv_ref = refs
            g = own_ref[...].astype(F32)
        for k in range(1, n_parts):
            g = g + p_ref[k].astype(F32)
        g_ref[...] = g
        d_ref[...], nm_ref[...], nv_ref[...] = _adam(w_ref[...], g, m_ref[...], v_ref[...])

    blk = pl.BlockSpec((tr, cols), lambda i: (i, 0))
    shp = jax.ShapeDtypeStruct((rows, cols), F32)
    args = [parts] + ([] if own is None else [own]) + [w, m, v]
    return pl.pallas_call(
        body, out_shape=(shp, shp, shp, shp), grid=(rows // tr,),
        in_specs=[pl.BlockSpec((n_parts, tr, cols), lambda i: (0, i, 0))] + [blk] * (len(args) - 1),
        out_specs=(blk, blk, blk, blk), name=name, compiler_params=_cp("parallel"))(*args)


MESH = pl.DeviceIdType.MESH
ANY = pl.BlockSpec(memory_space=pl.ANY)


def all_gather(block, name):
    def body(x_ref, out_ref, send_sems, recv_sems, local_sem):
        x, y, c = lax.axis_index("x"), lax.axis_index("y"), lax.axis_index("c")
        me, sibling = (x, y, c), (x, y, 1 - c)
        chips = [(1 - x, y), (x, 1 - y), (1 - x, 1 - y)]

        def slot(px, py, pc):
            return out_ref.at[4 * px + 2 * py + pc]

        def copy(k, blk, to, src=None):
            return pltpu.make_async_remote_copy(
                src_ref=slot(*blk) if src is None else src, dst_ref=slot(*blk),
                send_sem=send_sems.at[k], recv_sem=recv_sems.at[k], device_id=to, device_id_type=MESH)

        mine = pltpu.make_async_copy(x_ref, slot(*me), local_sem)
        mine.start()
        first = [copy(0, me, sibling, src=x_ref)]
        first += [copy(1 + j, me, (*chip, c), src=x_ref) for j, chip in enumerate(chips)]
        for cp in first:
            cp.start()
        passed = [copy(4 + j, (*chip, c), sibling) for j, chip in enumerate(chips)]
        for j, chip in enumerate(chips):
            copy(1 + j, (*chip, c), me).wait_recv()
            passed[j].start()
        copy(0, sibling, me).wait_recv()
        for j, chip in enumerate(chips):
            copy(4 + j, (*chip, 1 - c), me).wait_recv()
        for cp in first + passed:
            cp.wait_send()
        mine.wait()

    return pl.pallas_call(
        body, out_shape=jax.ShapeDtypeStruct((N_DEV,) + block.shape, block.dtype), in_specs=[ANY], out_specs=ANY,
        scratch_shapes=[pltpu.SemaphoreType.DMA((7,)), pltpu.SemaphoreType.DMA((7,)), pltpu.SemaphoreType.DMA],
        name=name)(block)


HBM = pl.BlockSpec(memory_space=pltpu.HBM)
SEM = pl.BlockSpec(memory_space=pltpu.SEMAPHORE)
EFFECT = pltpu.SideEffectType.DATAFLOW_SIDE_EFFECTING


def _peer_copies(src_ref, land_ref, send_sems, recv_sems, gather):
    x, y, c = lax.axis_index("x"), lax.axis_index("y"), lax.axis_index("c")
    me = 4 * x + 2 * y + c
    copies = []
    for k in range(1, N_DEV):
        px = 1 - x if k & 4 else x
        py = 1 - y if k & 2 else y
        pc = 1 - c if k & 1 else c
        copies.append(pltpu.make_async_remote_copy(
            src_ref=src_ref if gather else src_ref.at[4 * px + 2 * py + pc],
            dst_ref=land_ref.at[me] if gather else land_ref.at[k],
            send_sem=send_sems.at[k - 1], recv_sem=recv_sems.at[k - 1], device_id=(px, py, pc), device_id_type=MESH))
    return copies


def exchange_start(src, gather, name, after=None):
    land_shape = (N_DEV,) + src.shape if gather else src.shape
    extra = () if after is None else (after,)

    def body(src_ref, land_ref, *refs):
        send_sems, recv_sems, src_thru, land_thru, token = refs[len(extra):]
        for cp in _peer_copies(src_ref, land_ref, send_sems, recv_sems, gather):
            cp.start()
        token[...] = jnp.zeros_like(token)

    send_sems, recv_sems, src_thru, land_thru, token = pl.pallas_call(
        body, name=name,
        out_shape=(pltpu.SemaphoreType.DMA((N_DEV - 1,)), pltpu.SemaphoreType.DMA((N_DEV - 1,)),
                   pltpu.HBM(src.shape, src.dtype), pltpu.HBM(land_shape, src.dtype), jax.ShapeDtypeStruct((8, 128), F32)),
        in_specs=(HBM, HBM) + (ANY,) * len(extra), out_specs=(SEM, SEM, HBM, HBM, pl.BlockSpec(memory_space=pltpu.VMEM)),
        input_output_aliases={0: 2, 1: 3}, compiler_params=pltpu.CompilerParams(has_side_effects=EFFECT),
    )(pltpu.with_memory_space_constraint(src, pltpu.HBM),
      pltpu.with_memory_space_constraint(lax.empty(land_shape, src.dtype), pltpu.HBM), *extra)
    return (send_sems, recv_sems, src_thru, land_thru), token[0, 0]


def exchange_wait(handles, after, gather, name):
    send_sems, recv_sems, src_thru, land_thru = handles

    def body(src_ref, land_ref, send_sems, recv_sems, after_ref, src_dead, got_ref):
        for cp in _peer_copies(src_ref, land_ref, send_sems, recv_sems, gather):
            cp.wait_send()
            cp.wait_recv()

    return pl.pallas_call(
        body, name=name,
        out_shape=(pltpu.HBM(src_thru.shape, src_thru.dtype), pltpu.HBM(land_thru.shape, land_thru.dtype)),
        in_specs=(HBM, HBM, SEM, SEM, ANY), out_specs=(HBM, HBM), input_output_aliases={0: 0, 1: 1},
        compiler_params=pltpu.CompilerParams(has_side_effects=EFFECT),
    )(src_thru, land_thru, send_sems, recv_sems, after)[1]


def _to_pattern(a, r):
    return a.reshape(SEQ // r, r * a.shape[1])


def local_step(x, tgt, mod, get_w, put_grad, put_small, wc, wf, g_mix, bc, lg, lb, gco, gao, g_ffn, bf, g_fin):
    h1 = rms_mod_fwd(x, g_mix, mod, 0, 1, "h1_fwd")
    w_in = get_w("w_in", h1)
    proj = matmul(h1, w_in, "nn", F32, 256, D_IN, "proj_fwd")
    mix_a = conv_module_fwd(proj, wc, bc, lg, lb, gco, "conv_module_fwd")
    att, lse = attn_fwd_all(proj, "attn_fwd")
    mix_b = rms_gain_bf16(att, gao, "attn_out_norm")
    mixed = jnp.concatenate([mix_a, mix_b], axis=1)
    w_out = get_w("w_out", mixed)
    y1 = matmul(mixed, w_out, "nn", F32, 256, D_MODEL, "out_proj_fwd")
    x1, h2 = resid_rms_mod_fwd(x, y1, g_ffn, mod, 2, 3, 4, "x1_h2_fwd")
    w_up = get_w("w_up", h2)
    up0 = matmul(h2, w_up, "nn", F32, 256, D_FF, "up_fwd")
    act = ffn_act_fwd(up0, wf, bf, "ffn_act_fwd")
    w_down = get_w("w_down", act)
    y2 = matmul(act, w_down, "nn", F32, 256, D_MODEL, "down_fwd")
    loss_t, dx2, dy2, d_gfin, d_gaf = final_loss_bwd(x1, y2, tgt, g_fin, mod, 5, "loss_bwd")
    dact = matmul(dy2, w_down, "nt", F32, 256, FFN_TN, "down_bwd_x")
    dw_down = matmul(act, dy2, "tn", BF16, 256, D_MODEL, "down_bwd_w")
    dgate, dval, dbf_g, dbf_v, dwf_g, dwf_v = ffn_bwd_a(up0, dact, wf + put_grad("w_down", dw_down), bf, "ffn_bwd_a")
    dup0 = ffn_bwd_b(dgate, dval, wf, "ffn_bwd_b")
    dh2 = matmul(dup0, w_up, "nt", F32, 256, 512, "up_bwd_x")
    dw_up = matmul(h2, dup0, "tn", BF16, 256, 512, "up_bwd_w")
    dx1, d_shf, d_scf, d_gffn, dy1, d_gam = rms_mod_bwd(x1, dh2, dx2, g_ffn + put_grad("w_up", dw_up), mod, 4, y1, 2, "h2_bwd")
    dmixed = matmul(dy1, w_out, "nt", F32, 256, D_MODEL, "out_proj_bwd_x")
    dw_out = matmul(mixed, dy1, "tn", BF16, 256, D_MODEL, "out_proj_bwd_w")
    do, dd, d_gao = attn_combine_bwd(dmixed, att, gao + put_grad("w_out", dw_out), "attn_combine_bwd")
    dqkv = attn_bwd_all(proj, do, lse, dd, "attn_bwd")
    du1, d_gco, d_lg, d_lb, d_bc, d_wc = conv_module_bwd_a(proj, dmixed, wc, bc, lg, lb, gco, "conv_module_bwd_a")
    dproj_a = conv_module_bwd_b(proj, du1, wc, "conv_module_bwd_b")
    dproj = jnp.concatenate([dproj_a, dqkv[0], dqkv[1], dqkv[2]], axis=1)
    dh1 = matmul(dproj, w_in, "nt", F32, 256, D_MODEL, "proj_bwd_x")
    dx, d_shm, d_scm, d_gmix = rms_mod_bwd(x, dh1, dx1, g_mix, mod, 1, None, 0, "h1_bwd")
    dmod = jnp.concatenate([d_shm, d_scm, d_gam, d_shf, d_scf, d_gaf], axis=1)
    small = dict(g_norm_mix=d_gmix, b_conv_dw=d_bc, ln_conv_g=d_lg, ln_conv_b=d_lb, g_conv_out=d_gco, g_attn_out=d_gao,
                 g_norm_ffn=d_gffn, b_ffn_dw=jnp.concatenate([dbf_g, dbf_v], axis=1), g_final=d_gfin,
                 w_conv_dw=d_wc, w_ffn_dw=jnp.concatenate([dwf_g, dwf_v], axis=1), dmod=dmod)
    after = put_small(small)
    dw_in = matmul(h1, dproj, "tn", BF16, 256, 512, "proj_bwd_w")
    put_grad("w_in", dw_in, after)
    return loss_t[0, 0], dx


def _padw(a, width):
    return jnp.pad(a, ((0, 0), (0, width - a.shape[1])))


def pack_small(t):
    wide = jnp.concatenate([_padw(t["dmod"], PACK_W), _padw(t["b_ffn_dw"], PACK_W), _padw(t["w_ffn_dw"], PACK_W),
                            jnp.zeros((3, PACK_W), F32)], axis=0)
    z512 = jnp.zeros((1, 512), F32)
    narrow = jnp.concatenate([
        t["g_norm_mix"], t["g_norm_ffn"], t["g_final"],
        jnp.concatenate([t["b_conv_dw"], t["ln_conv_g"]], axis=1),
        jnp.concatenate([t["ln_conv_b"], t["g_conv_out"]], axis=1),
        jnp.concatenate([t["g_attn_out"], z512], axis=1),
        jnp.zeros((2, 1024), F32),
        jnp.pad(t["w_conv_dw"], ((0, 1), (0, 0))).reshape(16, 1024)], axis=0)
    return jnp.concatenate([wide, narrow.reshape(4, PACK_W), jnp.zeros((4, PACK_W), F32)], axis=0)


def unpack_small(p):
    narrow = p[8:12].reshape(24, 1024)
    return dict(
        dmod=p[0:1], b_ffn_dw=p[1:2, :2 * D_FF], w_ffn_dw=p[2:5, :2 * D_FF],
        g_norm_mix=narrow[0:1], g_norm_ffn=narrow[1:2], g_final=narrow[2:3],
        b_conv_dw=narrow[3:4, :512], ln_conv_g=narrow[3:4, 512:], ln_conv_b=narrow[4:5, :512], g_conv_out=narrow[4:5, 512:],
        g_attn_out=narrow[5:6, :512], w_conv_dw=narrow[8:24].reshape(32, 512)[:CONV_K])


def _embed(local, width, me):
    return lax.dynamic_update_slice(jnp.zeros((local.shape[0], width), F32), local, (0, me * local.shape[1]))


def _shard(full, n_cols, me):
    return lax.dynamic_slice(full, (0, me * n_cols), (full.shape[0], n_cols))


WEIGHTS = ["w_ada", "b_ada", "g_norm_mix", "w_in", "w_conv_dw", "b_conv_dw", "ln_conv_g", "ln_conv_b", "g_conv_out",
           "g_attn_out", "w_out", "g_norm_ffn", "w_up", "w_ffn_dw", "b_ffn_dw", "w_down", "g_final"]
SMALL_REPLICATED = ["g_norm_mix", "b_conv_dw", "ln_conv_g", "ln_conv_b", "g_conv_out", "g_attn_out", "g_norm_ffn",
                    "b_ffn_dw", "g_final"]


def kernel(x, c, w_ada, b_ada, g_norm_mix, w_in, w_conv_dw, b_conv_dw, ln_conv_g, ln_conv_b, g_conv_out, g_attn_out, w_out, g_norm_ffn, w_up, w_ffn_dw, b_ffn_dw, w_down, g_final, loss_target, m_w_ada, m_b_ada, m_g_norm_mix, m_w_in, m_w_conv_dw, m_b_conv_dw, m_ln_conv_g, m_ln_conv_b, m_g_conv_out, m_g_attn_out, m_w_out, m_g_norm_ffn, m_w_up, m_w_ffn_dw, m_b_ffn_dw, m_w_down, m_g_final, v_w_ada, v_b_ada, v_g_norm_mix, v_w_in, v_w_conv_dw, v_b_conv_dw, v_ln_conv_g, v_ln_conv_b, v_g_conv_out, v_g_attn_out, v_w_out, v_g_norm_ffn, v_w_up, v_w_ffn_dw, v_b_ffn_dw, v_w_down, v_g_final):
    args = dict(locals())
    me = 4 * lax.axis_index("x") + 2 * lax.axis_index("y") + lax.axis_index("c")

    def flat(name, prefix=""):
        a = args[prefix + name]
        return a.reshape(a.shape[-2] if a.ndim > 1 else 1, a.shape[-1])

    n_in, n_up, r_out, r_down = w_in.shape[2], w_up.shape[2], w_out.shape[1], w_down.shape[1]
    n_ada, n_wc, n_wf = w_ada.shape[2], w_conv_dw.shape[2], w_ffn_dw.shape[2]
    taps_c = jnp.pad(flat("w_conv_dw").reshape(1, CONV_K * n_wc), ((0, 0), (0, 2 * D_MODEL - CONV_K * n_wc)))
    taps_f = jnp.pad(flat("w_ffn_dw").reshape(1, FFN_K * n_wf), ((0, 0), (0, 3 * D_MODEL - FFN_K * n_wf)))
    first = jnp.concatenate([c, taps_c.reshape(2, D_MODEL), taps_f.reshape(3, D_MODEL), jnp.zeros((2, D_MODEL), F32)], axis=0)
    first_all = all_gather(first, "gather_c_taps")
    c_all = first_all[:, 0, :]
    wc_full = first_all[:, 1:3, :].reshape(N_DEV, 2 * D_MODEL)[:, :CONV_K * n_wc].reshape(N_DEV, CONV_K, n_wc)
    wc_full = wc_full.transpose(1, 0, 2).reshape(CONV_K, D_CONV)
    wf_full = first_all[:, 3:6, :].reshape(N_DEV, 3 * D_MODEL)[:, :FFN_K * n_wf].reshape(N_DEV, FFN_K, n_wf)
    wf_full = wf_full.transpose(1, 0, 2).reshape(FFN_K, 2 * D_FF)
    mod_cols = ada_fwd(c_all, flat("w_ada"), _shard(flat("b_ada"), n_ada, me), "ada_fwd")
    mod_all = all_gather(mod_cols, "gather_mod")
    mod = lax.dynamic_index_in_dim(mod_all, me, axis=1, keepdims=False).reshape(N_MOD, D_MODEL)
    mod = jnp.pad(mod, ((0, 2), (0, 0)))

    blocks = dict(w_in=flat("w_in").astype(BF16), w_up=flat("w_up").astype(BF16),
                  rows=jnp.concatenate([flat("w_out"), flat("w_down")], axis=0).astype(BF16))
    gathers, tok = {}, None
    for name in ("w_in", "rows", "w_up"):
        src = blocks[name] if tok is None else blocks[name] + tok.astype(BF16)
        gathers[name], tok = exchange_start(src, True, f"gather_{name}_start", mod_all)
    mod = mod + tok
    slot = lax.broadcasted_iota(jnp.int32, (N_DEV, 1, 1), 0)
    full = {}

    def gathered(name, after):
        land = exchange_wait(gathers[name], after, True, f"gather_{name}_wait")
        return jnp.where(slot == me, blocks[name][None], land)

    def get_w(name, after):
        if name in ("w_out", "w_down"):
            if "rows" not in full:
                full["rows"] = gathered("rows", after)
            rows = full["rows"]
            return rows[:, :r_out, :].reshape(D_MODEL, D_MODEL) if name == "w_out" else rows[:, r_out:, :].reshape(D_FF, D_MODEL)
        return gathered(name, after).transpose(1, 0, 2).reshape(D_MODEL, -1)

    exchanges, own, kept = {}, {}, {}

    def put_grad(name, dw, after=None):
        if name in ("w_in", "w_up"):
            dev_major = dw.reshape(D_MODEL, N_DEV, -1).transpose(1, 0, 2)
        else:
            dev_major = dw.reshape(N_DEV, -1, D_MODEL)
        own[name] = lax.dynamic_index_in_dim(dev_major, me, axis=0, keepdims=False)
        exchanges[name], token = exchange_start(dev_major, False, f"exchange_{name}_start", after)
        return token

    def put_small(small):
        kept["small_all"] = all_gather(pack_small(small), "gather_small")
        return kept["small_all"]

    loss_part, grad_x = local_step(
        x[0], loss_target[0], mod, get_w, put_grad, put_small, wc_full, wf_full,
        flat("g_norm_mix"), flat("b_conv_dw"), flat("ln_conv_g"), flat("ln_conv_b"), flat("g_conv_out"),
        flat("g_attn_out"), flat("g_norm_ffn"), flat("b_ffn_dw"), flat("g_final"))
    loss = lax.psum(loss_part, ("x", "y", "c"))
    small_all = kept["small_all"]

    out = {}

    def finish(name, tr, after):
        parts = exchange_wait(exchanges[name], after, False, f"exchange_{name}_wait")
        out[name] = sum_adamw(parts, own[name], flat(name), flat(name, "m_"), flat(name, "v_"), tr, "adamw_" + name)
        return out[name][0]

    def packed(prefix):
        t = {n: flat(n, prefix) for n in SMALL_REPLICATED}
        t["dmod"] = flat("b_ada", prefix)
        t["w_conv_dw"] = _embed(flat("w_conv_dw", prefix), D_CONV, me)
        t["w_ffn_dw"] = _embed(flat("w_ffn_dw", prefix), 2 * D_FF, me)
        return pack_small(t)

    res = sum_adamw(small_all, None, packed(""), packed("m_"), packed("v_"), PACK_ROWS, "adamw_small")
    after = res[0]
    res = [unpack_small(r) for r in res]
    for n in SMALL_REPLICATED:
        out[n] = tuple(r[n] for r in res)
    out["b_ada"] = tuple(r["dmod"] for r in res)
    out["w_conv_dw"] = tuple(_shard(r["w_conv_dw"], n_wc, me) for r in res)
    out["w_ffn_dw"] = tuple(_shard(r["w_ffn_dw"], n_wf, me) for r in res)

    dmod_cols = _shard(small_all[:, 0, :], n_ada, me)
    out["w_ada"] = ada_bwd_adamw(c_all.T, dmod_cols, flat("w_ada"), flat("w_ada", "m_"), flat("w_ada", "v_"), "adamw_w_ada")

    after = finish("w_down", r_down, out["w_ada"][0])
    after = finish("w_up", 256, after)
    after = finish("w_out", r_out, after)
    finish("w_in", 256, after)

    result = [loss, grad_x[None]]
    for k in range(4):
        result += [out[n][k].reshape(args[n].shape) for n in WEIGHTS]
    return tuple(result)
```

```python
import functools

import jax
import jax.numpy as jnp
from jax import lax
from jax.experimental import pallas as pl
from jax.experimental.pallas import tpu as pltpu

F32 = jnp.float32
BF16 = jnp.bfloat16

N_DEV = 8
SEQ = 2048
D_MODEL = 1024
D_CONV = 512
D_ATTN = 512
HEAD_DIM = 64
CONV_K = 31
D_FF = 2816
FFN_K = 3
D_IN = 2 * D_CONV + 3 * D_ATTN
N_MOD = 6
EPS = 1e-6
ATTN_BLOCK = 128
PATTERNS = ((2048, 1), (512, 4), (128, 16))
NEG = -1e30

ADAM_LR, ADAM_B1, ADAM_B2, ADAM_EPS, ADAM_WD, ADAM_STEP = 0.001, 0.9, 0.999, 1e-08, 0.01, 10

ROWS = 256
CONV_HALO = 32
FFN_HALO = 8
FFN_TN = 1408
VMEM_LIMIT = 56 * 1024 * 1024
PACK_ROWS, PACK_W = 16, 6144


def _cp(*sem):
    return pltpu.CompilerParams(dimension_semantics=sem if sem else None, vmem_limit_bytes=VMEM_LIMIT)


def _sig(x):
    return 1.0 / (1.0 + jnp.exp(-x))


def _rsum(x):
    return jnp.sum(x, axis=0, keepdims=True)


def _mean(x):
    return jnp.mean(x, axis=-1, keepdims=True)


def _acc(ref, val, first):
    @pl.when(first)
    def _():
        ref[...] = val

    @pl.when(jnp.logical_not(first))
    def _():
        ref[...] += val


def matmul(a, b, kind, out_dtype, tm, tn, name):
    if kind == "nn":
        (m, k), n = a.shape, b.shape[1]
        a_spec = pl.BlockSpec((tm, k), lambda j, i: (i, 0))
        b_spec = pl.BlockSpec((k, tn), lambda j, i: (0, j))
        dims = (((1,), (0,)), ((), ()))
    elif kind == "nt":
        (m, k), n = a.shape, b.shape[0]
        a_spec = pl.BlockSpec((tm, k), lambda j, i: (i, 0))
        b_spec = pl.BlockSpec((tn, k), lambda j, i: (j, 0))
        dims = (((1,), (1,)), ((), ()))
    else:
        (k, m), n = a.shape, b.shape[1]
        a_spec = pl.BlockSpec((k, tm), lambda j, i: (0, i))
        b_spec = pl.BlockSpec((k, tn), lambda j, i: (0, j))
        dims = (((0,), (0,)), ((), ()))
    assert m % tm == 0 and n % tn == 0, (name, m, n, tm, tn)

    def body(a_ref, b_ref, o_ref):
        o_ref[...] = lax.dot_general(a_ref[...], b_ref[...], dims, preferred_element_type=F32).astype(o_ref.dtype)

    return pl.pallas_call(
        body, out_shape=jax.ShapeDtypeStruct((m, n), out_dtype), grid=(n // tn, m // tm),
        in_specs=[a_spec, b_spec], out_specs=pl.BlockSpec((tm, tn), lambda j, i: (i, j)),
        name=name, compiler_params=_cp("parallel", "parallel"))(a, b)


def _row_spec(width, col=0):
    return pl.BlockSpec((ROWS, width), lambda i: (i, col))


def _vec_spec(width, rows=1):
    return pl.BlockSpec((rows, width), lambda i: (0, 0))


SUB = 16


def _for_chunks(fn):
    def step(i, carry):
        fn(pl.ds(pl.multiple_of(i * SUB, SUB), SUB))
        return carry

    lax.fori_loop(0, ROWS // SUB, step, 0)


def _flush(acc, outs, first):
    for k, ref in enumerate(outs):
        _acc(ref, _rsum(acc[k]), first)


def rms_mod_fwd(x, g, mod, sh_row, sc_row, name):
    def body(x_ref, g_ref, mod_ref, h_ref):
        def chunk(rs):
            xx = x_ref[rs, :]
            r = lax.rsqrt(_mean(xx * xx) + EPS)
            h = xx * r * g_ref[...]
            h_ref[rs, :] = (h * (1.0 + mod_ref[sc_row:sc_row + 1, :]) + mod_ref[sh_row:sh_row + 1, :]).astype(BF16)

        _for_chunks(chunk)

    return pl.pallas_call(
        body, out_shape=jax.ShapeDtypeStruct((SEQ, D_MODEL), BF16), grid=(SEQ // ROWS,),
        in_specs=[_row_spec(D_MODEL), _vec_spec(D_MODEL), _vec_spec(D_MODEL, 8)],
        out_specs=_row_spec(D_MODEL), name=name, compiler_params=_cp("parallel"))(x, g, mod)


def resid_rms_mod_fwd(x, y, g, mod, ga_row, sh_row, sc_row, name):
    def body(x_ref, y_ref, g_ref, mod_ref, x1_ref, h_ref):
        def chunk(rs):
            x1 = x_ref[rs, :] + mod_ref[ga_row:ga_row + 1, :] * y_ref[rs, :]
            x1_ref[rs, :] = x1
            r = lax.rsqrt(_mean(x1 * x1) + EPS)
            h = x1 * r * g_ref[...]
            h_ref[rs, :] = (h * (1.0 + mod_ref[sc_row:sc_row + 1, :]) + mod_ref[sh_row:sh_row + 1, :]).astype(BF16)

        _for_chunks(chunk)

    return pl.pallas_call(
        body, out_shape=(jax.ShapeDtypeStruct((SEQ, D_MODEL), F32), jax.ShapeDtypeStruct((SEQ, D_MODEL), BF16)),
        grid=(SEQ // ROWS,),
        in_specs=[_row_spec(D_MODEL), _row_spec(D_MODEL), _vec_spec(D_MODEL), _vec_spec(D_MODEL, 8)],
        out_specs=(_row_spec(D_MODEL), _row_spec(D_MODEL)), name=name, compiler_params=_cp("parallel"))(x, y, g, mod)


def final_loss_bwd(x1, y2, tgt, g, mod, ga_row, name):
    def body(x1_ref, y2_ref, t_ref, g_ref, mod_ref, loss_ref, dx2_ref, dy2_ref, dg_ref, dga_ref, acc, lacc):
        first = pl.program_id(0) == 0
        acc[...] = jnp.zeros_like(acc)
        lacc[...] = jnp.zeros_like(lacc)

        def chunk(rs):
            ga = mod_ref[ga_row:ga_row + 1, :]
            y2 = y2_ref[rs, :]
            x2 = x1_ref[rs, :] + ga * y2
            r = lax.rsqrt(_mean(x2 * x2) + EPS)
            xn = x2 * r
            err = xn * g_ref[...] - t_ref[rs, :]
            lacc[...] += _mean(err * err)
            dy = err * (1.0 / D_MODEL)
            acc[0] += dy * xn
            dxn = dy * g_ref[...]
            dx2 = r * (dxn - xn * _mean(dxn * xn))
            dx2_ref[rs, :] = dx2
            dy2_ref[rs, :] = (dx2 * ga).astype(BF16)
            acc[1] += dx2 * y2

        _for_chunks(chunk)
        _acc(loss_ref, jnp.broadcast_to(0.5 * jnp.sum(lacc[...]), (8, 128)), first)
        _flush(acc, (dg_ref, dga_ref), first)

    vec = jax.ShapeDtypeStruct((1, D_MODEL), F32)
    return pl.pallas_call(
        body,
        out_shape=(jax.ShapeDtypeStruct((8, 128), F32), jax.ShapeDtypeStruct((SEQ, D_MODEL), F32),
                   jax.ShapeDtypeStruct((SEQ, D_MODEL), BF16), vec, vec),
        grid=(SEQ // ROWS,),
        in_specs=[_row_spec(D_MODEL), _row_spec(D_MODEL), _row_spec(D_MODEL), _vec_spec(D_MODEL), _vec_spec(D_MODEL, 8)],
        out_specs=(pl.BlockSpec((8, 128), lambda i: (0, 0)), _row_spec(D_MODEL), _row_spec(D_MODEL),
                   _vec_spec(D_MODEL), _vec_spec(D_MODEL)),
        scratch_shapes=[pltpu.VMEM((2, SUB, D_MODEL), F32), pltpu.VMEM((SUB, 1), F32)],
        name=name, compiler_params=_cp("arbitrary"))(x1, y2, tgt, g, mod)


def rms_mod_bwd(x, dh, dres, g, mod, sc_row, y, ga_row, name):
    gated = y is not None

    def body(*refs):
        if gated:
            x_ref, dh_ref, dres_ref, g_ref, mod_ref, y_ref, dx_ref, dsh_ref, dsc_ref, dg_ref, dy_ref, dga_ref, acc = refs
        else:
            x_ref, dh_ref, dres_ref, g_ref, mod_ref, dx_ref, dsh_ref, dsc_ref, dg_ref, acc = refs
        first = pl.program_id(0) == 0
        acc[...] = jnp.zeros_like(acc)

        def chunk(rs):
            xx = x_ref[rs, :]
            dh = dh_ref[rs, :]
            gg = g_ref[...]
            r = lax.rsqrt(_mean(xx * xx) + EPS)
            xn = xx * r
            acc[0] += dh
            acc[1] += dh * (xn * gg)
            dt = dh * (1.0 + mod_ref[sc_row:sc_row + 1, :])
            acc[2] += dt * xn
            dxn = dt * gg
            dx = dres_ref[rs, :] + r * (dxn - xn * _mean(dxn * xn))
            dx_ref[rs, :] = dx
            if gated:
                acc[3] += dx * y_ref[rs, :]
                dy_ref[rs, :] = (dx * mod_ref[ga_row:ga_row + 1, :]).astype(BF16)

        _for_chunks(chunk)
        _flush(acc, (dsh_ref, dsc_ref, dg_ref) + ((dga_ref,) if gated else ()), first)

    vec = jax.ShapeDtypeStruct((1, D_MODEL), F32)
    in_specs = [_row_spec(D_MODEL), _row_spec(D_MODEL), _row_spec(D_MODEL), _vec_spec(D_MODEL), _vec_spec(D_MODEL, 8)]
    out_shape = [jax.ShapeDtypeStruct((SEQ, D_MODEL), F32), vec, vec, vec]
    out_specs = [_row_spec(D_MODEL), _vec_spec(D_MODEL), _vec_spec(D_MODEL), _vec_spec(D_MODEL)]
    args = [x, dh, dres, g, mod]
    if gated:
        in_specs.append(_row_spec(D_MODEL))
        out_shape += [jax.ShapeDtypeStruct((SEQ, D_MODEL), BF16), vec]
        out_specs += [_row_spec(D_MODEL), _vec_spec(D_MODEL)]
        args.append(y)
    return pl.pallas_call(
        body, out_shape=tuple(out_shape), grid=(SEQ // ROWS,), in_specs=in_specs, out_specs=tuple(out_specs),
        scratch_shapes=[pltpu.VMEM((4 if gated else 3, SUB, D_MODEL), F32)],
        name=name, compiler_params=_cp("arbitrary"))(*args)


def _prev_halo(halo, width, col):
    per = ROWS // halo
    return pl.BlockSpec((halo, width), lambda i: (jnp.maximum(i * per - 1, 0), col))


def _next_halo(halo, width, col):
    per = ROWS // halo
    last = SEQ // halo - 1
    return pl.BlockSpec((halo, width), lambda i: (jnp.minimum((i + 1) * per, last), col))


def _conv_module_forward(av_ref, ag_ref, avh_ref, agh_ref, wc_ref, bc_ref, lg_ref, lb_ref, u0p):
    i = pl.program_id(0)
    hv = avh_ref[...] * _sig(agh_ref[...])
    u0p[0:CONV_HALO, :] = jnp.where(i > 0, hv, 0.0)
    u0p[CONV_HALO:, :] = av_ref[...] * _sig(ag_ref[...])
    u1 = jnp.broadcast_to(bc_ref[...], (ROWS, D_CONV))
    for j in range(CONV_K):
        u1 = u1 + wc_ref[j:j + 1, :] * u0p[pl.ds(CONV_HALO - (CONV_K - 1) + j, ROWS), :]
    mu = _mean(u1)
    cen = u1 - mu
    rs = lax.rsqrt(_mean(cen * cen) + EPS)
    z = cen * rs
    ln = z * lg_ref[...] + lb_ref[...]
    s = _sig(ln)
    return z, rs, ln, s, ln * s


def conv_module_fwd(proj, wc, bc, lg, lb, gco, name):
    def body(av_ref, ag_ref, avh_ref, agh_ref, wc_ref, bc_ref, lg_ref, lb_ref, gco_ref, out_ref, u0p):
        _, _, _, _, u2 = _conv_module_forward(av_ref, ag_ref, avh_ref, agh_ref, wc_ref, bc_ref, lg_ref, lb_ref, u0p)
        rc = lax.rsqrt(_mean(u2 * u2) + EPS)
        out_ref[...] = (u2 * rc * gco_ref[...]).astype(BF16)

    v = _vec_spec(D_CONV)
    return pl.pallas_call(
        body, out_shape=jax.ShapeDtypeStruct((SEQ, D_CONV), BF16), grid=(SEQ // ROWS,),
        in_specs=[_row_spec(D_CONV, 0), _row_spec(D_CONV, 1), _prev_halo(CONV_HALO, D_CONV, 0),
                  _prev_halo(CONV_HALO, D_CONV, 1), _vec_spec(D_CONV, CONV_K), v, v, v, v],
        out_specs=_row_spec(D_CONV), scratch_shapes=[pltpu.VMEM((ROWS + CONV_HALO, D_CONV), F32)],
        name=name, compiler_params=_cp("parallel"))(proj, proj, proj, proj, wc, bc, lg, lb, gco)


def conv_module_bwd_a(proj, dmixed, wc, bc, lg, lb, gco, name):
    def body(av_ref, ag_ref, avh_ref, agh_ref, dm_ref, wc_ref, bc_ref, lg_ref, lb_ref, gco_ref,
             du1_ref, dgco_ref, dlg_ref, dlb_ref, dbc_ref, dwc_ref, u0p):
        first = pl.program_id(0) == 0
        z, rs, ln, s, u2 = _conv_module_forward(av_ref, ag_ref, avh_ref, agh_ref, wc_ref, bc_ref, lg_ref, lb_ref, u0p)
        rc = lax.rsqrt(_mean(u2 * u2) + EPS)
        xn = u2 * rc
        dm = dm_ref[...]
        _acc(dgco_ref, _rsum(dm * xn), first)
        dyn = dm * gco_ref[...]
        du2 = rc * (dyn - xn * _mean(dyn * xn))
        dln = du2 * (s * (1.0 + ln * (1.0 - s)))
        _acc(dlg_ref, _rsum(dln * z), first)
        _acc(dlb_ref, _rsum(dln), first)
        dz = dln * lg_ref[...]
        du1 = rs * (dz - _mean(dz) - z * _mean(dz * z))
        du1_ref[...] = du1
        _acc(dbc_ref, _rsum(du1), first)

        @pl.when(first)
        def _():
            dwc_ref[...] = jnp.zeros_like(dwc_ref)

        for j in range(CONV_K):
            dwc_ref[j:j + 1, :] += _rsum(du1 * u0p[pl.ds(CONV_HALO - (CONV_K - 1) + j, ROWS), :])

    v = _vec_spec(D_CONV)
    vec = jax.ShapeDtypeStruct((1, D_CONV), F32)
    return pl.pallas_call(
        body,
        out_shape=(jax.ShapeDtypeStruct((SEQ, D_CONV), F32), vec, vec, vec, vec, jax.ShapeDtypeStruct((CONV_K, D_CONV), F32)),
        grid=(SEQ // ROWS,),
        in_specs=[_row_spec(D_CONV, 0), _row_spec(D_CONV, 1), _prev_halo(CONV_HALO, D_CONV, 0),
                  _prev_halo(CONV_HALO, D_CONV, 1), _row_spec(D_CONV, 0), _vec_spec(D_CONV, CONV_K), v, v, v, v],
        out_specs=(_row_spec(D_CONV), v, v, v, v, _vec_spec(D_CONV, CONV_K)),
        scratch_shapes=[pltpu.VMEM((ROWS + CONV_HALO, D_CONV), F32)],
        name=name, compiler_params=_cp("arbitrary"))(proj, proj, proj, proj, dmixed, wc, bc, lg, lb, gco)


def conv_module_bwd_b(proj, du1, wc, name):
    def body(av_ref, ag_ref, du1_ref, du1n_ref, wc_ref, out_ref, dup):
        i = pl.program_id(0)
        dup[0:ROWS, :] = du1_ref[...]
        dup[ROWS:, :] = jnp.where(i < SEQ // ROWS - 1, du1n_ref[...], 0.0)
        du0 = jnp.zeros((ROWS, D_CONV), F32)
        for j in range(CONV_K):
            du0 = du0 + wc_ref[j:j + 1, :] * dup[pl.ds(CONV_K - 1 - j, ROWS), :]
        sg = _sig(ag_ref[...])
        out_ref[:, 0:D_CONV] = (du0 * sg).astype(BF16)
        out_ref[:, D_CONV:] = (du0 * av_ref[...] * sg * (1.0 - sg)).astype(BF16)

    return pl.pallas_call(
        body, out_shape=jax.ShapeDtypeStruct((SEQ, 2 * D_CONV), BF16), grid=(SEQ // ROWS,),
        in_specs=[_row_spec(D_CONV, 0), _row_spec(D_CONV, 1), _row_spec(D_CONV, 0), _next_halo(CONV_HALO, D_CONV, 0),
                  _vec_spec(D_CONV, CONV_K)],
        out_specs=_row_spec(2 * D_CONV), scratch_shapes=[pltpu.VMEM((ROWS + CONV_HALO, D_CONV), F32)],
        name=name, compiler_params=_cp("parallel"))(proj, proj, du1, du1, wc)


def _attn_specs(sub_len, pairs):
    ng, width = 4 // pairs, 128 * pairs
    q = pl.BlockSpec((sub_len, width), lambda rho, g: (0, rho * 3 * ng + g))
    k = pl.BlockSpec((sub_len, width), lambda rho, g: (0, rho * 3 * ng + ng + g))
    v = pl.BlockSpec((sub_len, width), lambda rho, g: (0, rho * 3 * ng + 2 * ng + g))
    o = pl.BlockSpec((sub_len, width), lambda rho, g: (0, rho * ng + g))
    return q, k, v, o


ATTN_PAIRS = {1: 1, 4: 1, 16: 4}


def _attn_block(q_ref, k_ref, v_ref, n, hs, win):
    q0 = pl.multiple_of(n * ATTN_BLOCK, ATTN_BLOCK)
    k0 = pl.multiple_of(jnp.maximum(n - 1, 0) * ATTN_BLOCK, ATTN_BLOCK)
    qb = q_ref[pl.ds(q0, ATTN_BLOCK), hs]
    kw = k_ref[pl.ds(k0, win), hs]
    vw = v_ref[pl.ds(k0, win), hs]
    s = lax.dot_general(qb, kw, (((1,), (1,)), ((), ())), preferred_element_type=F32) * (HEAD_DIM ** -0.5)
    dist = (q0 - k0) + lax.broadcasted_iota(jnp.int32, (ATTN_BLOCK, win), 0) \
        - lax.broadcasted_iota(jnp.int32, (ATTN_BLOCK, win), 1)
    s = jnp.where((dist >= 0) & (dist <= ATTN_BLOCK), s, NEG)
    return q0, k0, qb, kw, vw, s


def attn_fwd(qkv_r, sub_len, r, name):
    nb = sub_len // ATTN_BLOCK
    win = 2 * ATTN_BLOCK if nb > 1 else ATTN_BLOCK
    pairs = ATTN_PAIRS[r]

    def body(q_ref, k_ref, v_ref, o_ref, l_ref):
        def block(n, carry):
            for h in range(2 * pairs):
                hs = slice(h * HEAD_DIM, (h + 1) * HEAD_DIM)
                q0, _, _, _, vw, s = _attn_block(q_ref, k_ref, v_ref, n, hs, win)
                m = jnp.max(s, axis=1, keepdims=True)
                p = jnp.exp(s - m)
                den = jnp.sum(p, axis=1, keepdims=True)
                o = jnp.dot(p.astype(BF16), vw, preferred_element_type=F32) / den
                o_ref[pl.ds(q0, ATTN_BLOCK), hs] = o
                l_ref[pl.ds(q0, ATTN_BLOCK), hs] = jnp.broadcast_to(m + jnp.log(den), (ATTN_BLOCK, HEAD_DIM))
            return carry

        lax.fori_loop(0, nb, block, 0, unroll=min(nb, 2))

    q, k, v, o = _attn_specs(sub_len, pairs)
    shp = jax.ShapeDtypeStruct((sub_len, r * D_ATTN), F32)
    return pl.pallas_call(
        body, out_shape=(shp, shp), grid=(r, 4 // pairs), in_specs=[q, k, v], out_specs=(o, o),
        name=name, compiler_params=_cp("parallel", "parallel"))(qkv_r, qkv_r, qkv_r)


def attn_bwd(qkv_r, do_r, lse_r, dd_r, sub_len, r, name):
    nb = sub_len // ATTN_BLOCK
    win = 2 * ATTN_BLOCK if nb > 1 else ATTN_BLOCK
    pairs = ATTN_PAIRS[r]

    def body(q_ref, k_ref, v_ref, do_ref, l_ref, dd_ref, dq_ref, dk_ref, dv_ref):
        dk_ref[...] = jnp.zeros_like(dk_ref)
        dv_ref[...] = jnp.zeros_like(dv_ref)

        def block(n, carry):
            for h in range(2 * pairs):
                hs = slice(h * HEAD_DIM, (h + 1) * HEAD_DIM)
                h1 = slice(h * HEAD_DIM, h * HEAD_DIM + 1)
                q0, k0, qb, kw, vw, s = _attn_block(q_ref, k_ref, v_ref, n, hs, win)
                dob = do_ref[pl.ds(q0, ATTN_BLOCK), hs]
                p = jnp.exp(s - l_ref[pl.ds(q0, ATTN_BLOCK), h1])
                dp = lax.dot_general(dob, vw, (((1,), (1,)), ((), ())), preferred_element_type=F32)
                ds = (p * (dp - dd_ref[pl.ds(q0, ATTN_BLOCK), h1]) * (HEAD_DIM ** -0.5)).astype(BF16)
                dq_ref[pl.ds(q0, ATTN_BLOCK), hs] = jnp.dot(ds, kw, preferred_element_type=F32)
                dk_ref[pl.ds(k0, win), hs] += lax.dot_general(ds, qb, (((0,), (0,)), ((), ())), preferred_element_type=F32)
                dv_ref[pl.ds(k0, win), hs] += lax.dot_general(p.astype(BF16), dob, (((0,), (0,)), ((), ())),
                                                              preferred_element_type=F32)
            return carry

        lax.fori_loop(0, nb, block, 0)

    q, k, v, o = _attn_specs(sub_len, pairs)
    shp = jax.ShapeDtypeStruct((sub_len, r * D_ATTN), F32)
    return pl.pallas_call(
        body, out_shape=(shp, shp, shp), grid=(r, 4 // pairs), in_specs=[q, k, v, o, o, o], out_specs=(o, o, o),
        name=name, compiler_params=_cp("parallel", "parallel"))(qkv_r, qkv_r, qkv_r, do_r, lse_r, dd_r)


def _rows(start, size, r):
    return pl.ds(start, size) if r == 1 else pl.ds(start, size, stride=r)


def _attn_unit(q_ref, k_ref, v_ref, r, rho, n, nb):
    win = 2 * ATTN_BLOCK if nb > 1 else ATTN_BLOCK
    if isinstance(n, int):
        kb = max(n - 1, 0)
        q_rows = _rows(rho + r * ATTN_BLOCK * n, ATTN_BLOCK, r)
        k_rows = _rows(rho + r * ATTN_BLOCK * kb, win, r)
    else:
        kb = jnp.maximum(n - 1, 0)
        q_rows = pl.ds(pl.multiple_of(n * ATTN_BLOCK, ATTN_BLOCK), ATTN_BLOCK)
        k_rows = pl.ds(pl.multiple_of(kb * ATTN_BLOCK, ATTN_BLOCK), win)
    q2 = q_ref[q_rows, :].astype(BF16)
    k2 = k_ref[k_rows, :].astype(BF16)
    v2 = v_ref[k_rows, :].astype(BF16)
    dist = (n - kb) * ATTN_BLOCK + lax.broadcasted_iota(jnp.int32, (ATTN_BLOCK, win), 0) \
        - lax.broadcasted_iota(jnp.int32, (ATTN_BLOCK, win), 1)
    valid = (dist >= 0) & (dist <= ATTN_BLOCK)

    def score(h):
        hs = slice(h * HEAD_DIM, (h + 1) * HEAD_DIM)
        s = lax.dot_general(q2[:, hs], k2[:, hs], (((1,), (1,)), ((), ())), preferred_element_type=F32)
        return jnp.where(valid, s * (HEAD_DIM ** -0.5), NEG), q2[:, hs], k2[:, hs], v2[:, hs]

    return q_rows, k_rows, score


def _attn_units(r, nb, unit):
    if r == 1:
        def two(i, carry):
            unit(0, 2 * i)
            unit(0, 2 * i + 1)
            return carry
        lax.fori_loop(0, nb // 2, two, 0)
    else:
        for rho in range(r):
            for n in range(nb):
                unit(rho, n)


def attn_fwd_all(proj, name):
    def body(q_ref, k_ref, v_ref, att_ref, lse_ref):
        for idx, (sub_len, r) in enumerate(PATTERNS):
            nb = sub_len // ATTN_BLOCK

            def unit(rho, n, r=r, nb=nb, idx=idx):
                q_rows, _, score = _attn_unit(q_ref, k_ref, v_ref, r, rho, n, nb)
                outs, lses = [], []
                for h in range(2):
                    s, _, _, vw = score(h)
                    m = jnp.max(s, axis=1, keepdims=True)
                    p = jnp.exp(s - m)
                    den = jnp.sum(p, axis=1, keepdims=True)
                    outs.append(jnp.dot(p.astype(BF16), vw, preferred_element_type=F32) / den)
                    lses.append(jnp.broadcast_to(m + jnp.log(den), (ATTN_BLOCK, HEAD_DIM)))
                o = jnp.concatenate(outs, axis=1)
                lse = jnp.concatenate(lses, axis=1)
                if idx > 0:
                    old = lse_ref[q_rows, :]
                    top = jnp.maximum(old, lse)
                    new = top + jnp.log(jnp.exp(old - top) + jnp.exp(lse - top))
                    o = att_ref[q_rows, :] * jnp.exp(old - new) + o * jnp.exp(lse - new)
                    lse = new
                att_ref[q_rows, :] = o
                lse_ref[q_rows, :] = lse

            _attn_units(r, nb, unit)

    blk = lambda first: pl.BlockSpec((SEQ, 128), lambda g: (0, first + g))
    shp = jax.ShapeDtypeStruct((SEQ, D_ATTN), F32)
    return pl.pallas_call(
        body, out_shape=(shp, shp), grid=(4,), in_specs=[blk(8), blk(12), blk(16)], out_specs=(blk(0), blk(0)),
        name=name, compiler_params=_cp("parallel"))(proj, proj, proj)


def attn_bwd_all(proj, do, lse, dd, name):
    def body(q_ref, k_ref, v_ref, do_ref, l_ref, dd_ref, out_ref, dq_s, dk_s, dv_s):
        dq_s[...] = jnp.zeros_like(dq_s)
        dk_s[...] = jnp.zeros_like(dk_s)
        dv_s[...] = jnp.zeros_like(dv_s)
        for sub_len, r in PATTERNS:
            nb = sub_len // ATTN_BLOCK

            def unit(rho, n, r=r, nb=nb):
                q_rows, k_rows, score = _attn_unit(q_ref, k_ref, v_ref, r, rho, n, nb)
                do2 = do_ref[q_rows, :].astype(BF16)
                l2 = l_ref[q_rows, :]
                d2 = dd_ref[q_rows, :]
                dq, dk, dv = [], [], []
                for h in range(2):
                    hs = slice(h * HEAD_DIM, (h + 1) * HEAD_DIM)
                    h1 = slice(h * HEAD_DIM, h * HEAD_DIM + 1)
                    s, qb, kw, vw = score(h)
                    dob = do2[:, hs]
                    p = jnp.exp(s - l2[:, h1])
                    dp = lax.dot_general(dob, vw, (((1,), (1,)), ((), ())), preferred_element_type=F32)
                    ds = (p * (dp - d2[:, h1]) * (HEAD_DIM ** -0.5)).astype(BF16)
                    dq.append(jnp.dot(ds, kw, preferred_element_type=F32))
                    dk.append(lax.dot_general(ds, qb, (((0,), (0,)), ((), ())), preferred_element_type=F32))
                    dv.append(lax.dot_general(p.astype(BF16), dob, (((0,), (0,)), ((), ())), preferred_element_type=F32))
                dq_s[q_rows, :] += jnp.concatenate(dq, axis=1)
                dk_s[k_rows, :] += jnp.concatenate(dk, axis=1)
                dv_s[k_rows, :] += jnp.concatenate(dv, axis=1)

            _attn_units(r, nb, unit)
        out_ref[0] = dq_s[...].astype(BF16)
        out_ref[1] = dk_s[...].astype(BF16)
        out_ref[2] = dv_s[...].astype(BF16)

    blk = lambda first: pl.BlockSpec((SEQ, 128), lambda g: (0, first + g))
    acc = pltpu.VMEM((SEQ, 128), F32)
    return pl.pallas_call(
        body, out_shape=jax.ShapeDtypeStruct((3, SEQ, D_ATTN), BF16), grid=(4,),
        in_specs=[blk(8), blk(12), blk(16), blk(0), blk(0), blk(0)],
        out_specs=pl.BlockSpec((3, SEQ, 128), lambda g: (0, 0, g)), scratch_shapes=[acc, acc, acc],
        name=name, compiler_params=_cp("parallel"))(proj, proj, proj, do, lse, dd)


def rms_gain_bf16(a, g, name):
    def body(a_ref, g_ref, o_ref):
        def chunk(rs):
            aa = a_ref[rs, :]
            o_ref[rs, :] = (aa * lax.rsqrt(_mean(aa * aa) + EPS) * g_ref[...]).astype(BF16)

        _for_chunks(chunk)

    w = a.shape[1]
    return pl.pallas_call(
        body, out_shape=jax.ShapeDtypeStruct(a.shape, BF16), grid=(SEQ // ROWS,), in_specs=[_row_spec(w), _vec_spec(w)],
        out_specs=_row_spec(w), name=name, compiler_params=_cp("parallel"))(a, g)


def attn_combine_fwd(outs, lses, gao, name):
    def body(o1, o2, o3, l1, l2, l3, g_ref, att_ref, lse_ref, mix_ref):
        a1, a2, a3 = l1[...], l2[...], l3[...]
        m = jnp.maximum(jnp.maximum(a1, a2), a3)
        w1, w2, w3 = jnp.exp(a1 - m), jnp.exp(a2 - m), jnp.exp(a3 - m)
        den = w1 + w2 + w3
        att = (w1 * o1[...] + w2 * o2[...] + w3 * o3[...]) / den
        att_ref[...] = att
        lse_ref[...] = m + jnp.log(den)
        mix_ref[...] = (att * lax.rsqrt(_mean(att * att) + EPS) * g_ref[...]).astype(BF16)

    rs = _row_spec(D_ATTN)
    f = jax.ShapeDtypeStruct((SEQ, D_ATTN), F32)
    return pl.pallas_call(
        body, out_shape=(f, f, jax.ShapeDtypeStruct((SEQ, D_ATTN), BF16)), grid=(SEQ // ROWS,),
        in_specs=[rs] * 6 + [_vec_spec(D_ATTN)], out_specs=(rs, rs, rs),
        name=name, compiler_params=_cp("parallel"))(*outs, *lses, gao)


def attn_combine_bwd(dmixed, att, gao, name):
    def body(dm_ref, att_ref, g_ref, do_ref, dd_ref, dg_ref, acc):
        first = pl.program_id(0) == 0
        acc[...] = jnp.zeros_like(acc)
        same_head = (jnp.right_shift(lax.broadcasted_iota(jnp.int32, (D_ATTN, D_ATTN), 0), 6)
                     == jnp.right_shift(lax.broadcasted_iota(jnp.int32, (D_ATTN, D_ATTN), 1), 6)).astype(F32)

        def chunk(rs):
            att = att_ref[rs, :]
            r = lax.rsqrt(_mean(att * att) + EPS)
            xn = att * r
            dm = dm_ref[rs, :]
            acc[0] += dm * xn
            dyn = dm * g_ref[...]
            do = r * (dyn - xn * _mean(dyn * xn))
            do_ref[rs, :] = do

        _for_chunks(chunk)
        _flush(acc, (dg_ref,), first)
        dd_ref[...] = jnp.dot(do_ref[...] * att_ref[...], same_head, preferred_element_type=F32,
                              precision=lax.Precision.HIGHEST)

    rs = _row_spec(D_ATTN)
    return pl.pallas_call(
        body,
        out_shape=(jax.ShapeDtypeStruct((SEQ, D_ATTN), F32), jax.ShapeDtypeStruct((SEQ, D_ATTN), F32),
                   jax.ShapeDtypeStruct((1, D_ATTN), F32)),
        grid=(SEQ // ROWS,), in_specs=[_row_spec(D_ATTN, 1), rs, _vec_spec(D_ATTN)],
        out_specs=(rs, rs, _vec_spec(D_ATTN)), scratch_shapes=[pltpu.VMEM((1, SUB, D_ATTN), F32)],
        name=name, compiler_params=_cp("arbitrary"))(dmixed, att, gao)


def sum3_bf16(a, b, c, name):
    def body(a_ref, b_ref, c_ref, o_ref):
        o_ref[...] = (a_ref[...] + b_ref[...] + c_ref[...]).astype(BF16)

    w = a.shape[1]
    rs = _row_spec(w)
    return pl.pallas_call(
        body, out_shape=jax.ShapeDtypeStruct(a.shape, BF16), grid=(SEQ // ROWS,), in_specs=[rs, rs, rs], out_specs=rs,
        name=name, compiler_params=_cp("parallel"))(a, b, c)


N_FT = D_FF // FFN_TN


def _ffn_specs():
    per = ROWS // FFN_HALO
    cur_g = pl.BlockSpec((ROWS, FFN_TN), lambda j, i: (i, j))
    cur_v = pl.BlockSpec((ROWS, FFN_TN), lambda j, i: (i, j + N_FT))
    halo_g = pl.BlockSpec((FFN_HALO, FFN_TN), lambda j, i: (jnp.maximum(i * per - 1, 0), j))
    halo_v = pl.BlockSpec((FFN_HALO, FFN_TN), lambda j, i: (jnp.maximum(i * per - 1, 0), j + N_FT))
    w_g = pl.BlockSpec((FFN_K, FFN_TN), lambda j, i: (0, j))
    w_v = pl.BlockSpec((FFN_K, FFN_TN), lambda j, i: (0, j + N_FT))
    b_g = pl.BlockSpec((1, FFN_TN), lambda j, i: (0, j))
    b_v = pl.BlockSpec((1, FFN_TN), lambda j, i: (0, j + N_FT))
    return [cur_g, cur_v, halo_g, halo_v, w_g, w_v, b_g, b_v]


def _ffn_shifted(cur_ref, halo_ref, pad, s1, s2):
    i = pl.program_id(1)
    pad[0:FFN_HALO, :] = jnp.where(i > 0, halo_ref[...], 0.0)
    pad[FFN_HALO:, :] = cur_ref[0:FFN_HALO, :]
    for k, dst in ((1, s1), (2, s2)):
        dst[0:FFN_HALO, :] = pad[pl.ds(FFN_HALO - k, FFN_HALO), :]
        dst[FFN_HALO:, :] = cur_ref[pl.ds(FFN_HALO - k, ROWS - FFN_HALO), :]


def _ffn_conv(rs, cur_ref, s1, s2, w_ref, b_ref):
    return b_ref[...] + w_ref[0:1, :] * s2[rs, :] + w_ref[1:2, :] * s1[rs, :] + w_ref[2:3, :] * cur_ref[rs, :]


def ffn_act_fwd(up0, wf, bf, name):
    def body(g_ref, v_ref, gh_ref, vh_ref, wg_ref, wv_ref, bg_ref, bv_ref, act_ref, pad, g1, g2, v1, v2):
        _ffn_shifted(g_ref, gh_ref, pad, g1, g2)
        _ffn_shifted(v_ref, vh_ref, pad, v1, v2)

        def chunk(rs):
            gate = _ffn_conv(rs, g_ref, g1, g2, wg_ref, bg_ref)
            val = _ffn_conv(rs, v_ref, v1, v2, wv_ref, bv_ref)
            act_ref[rs, :] = (gate * _sig(gate) * val).astype(BF16)

        _for_chunks(chunk)

    tile = pltpu.VMEM((ROWS, FFN_TN), F32)
    return pl.pallas_call(
        body, out_shape=jax.ShapeDtypeStruct((SEQ, D_FF), BF16), grid=(N_FT, SEQ // ROWS),
        in_specs=_ffn_specs(), out_specs=pl.BlockSpec((ROWS, FFN_TN), lambda j, i: (i, j)),
        scratch_shapes=[pltpu.VMEM((2 * FFN_HALO, FFN_TN), F32), tile, tile, tile, tile],
        name=name, compiler_params=_cp("parallel", "parallel"))(up0, up0, up0, up0, wf, wf, bf, bf)


def ffn_bwd_a(up0, dact, wf, bf, name):
    def body(g_ref, v_ref, gh_ref, vh_ref, wg_ref, wv_ref, bg_ref, bv_ref, da_ref,
             dg_ref, dv_ref, dbg_ref, dbv_ref, dwg_ref, dwv_ref, pad, g1, g2, v1, v2, acc):
        first = pl.program_id(1) == 0
        _ffn_shifted(g_ref, gh_ref, pad, g1, g2)
        _ffn_shifted(v_ref, vh_ref, pad, v1, v2)
        acc[...] = jnp.zeros_like(acc)

        def chunk(rs):
            gate = _ffn_conv(rs, g_ref, g1, g2, wg_ref, bg_ref)
            val = _ffn_conv(rs, v_ref, v1, v2, wv_ref, bv_ref)
            s = _sig(gate)
            da = da_ref[rs, :]
            dgate = da * val * (s * (1.0 + gate * (1.0 - s)))
            dval = da * (gate * s)
            dg_ref[rs, :] = dgate
            dv_ref[rs, :] = dval
            acc[0] += dgate
            acc[1] += dval
            for t, (sg, sv) in enumerate(((g2, v2), (g1, v1), (g_ref, v_ref))):
                acc[2 + t] += dgate * sg[rs, :]
                acc[5 + t] += dval * sv[rs, :]

        _for_chunks(chunk)
        _acc(dbg_ref, _rsum(acc[0]), first)
        _acc(dbv_ref, _rsum(acc[1]), first)
        _acc(dwg_ref, jnp.concatenate([_rsum(acc[2 + t]) for t in range(FFN_K)], axis=0), first)
        _acc(dwv_ref, jnp.concatenate([_rsum(acc[5 + t]) for t in range(FFN_K)], axis=0), first)

    tile = pltpu.VMEM((ROWS, FFN_TN), F32)
    full = jax.ShapeDtypeStruct((SEQ, D_FF), F32)
    vec = jax.ShapeDtypeStruct((1, D_FF), F32)
    taps = jax.ShapeDtypeStruct((FFN_K, D_FF), F32)
    cur = pl.BlockSpec((ROWS, FFN_TN), lambda j, i: (i, j))
    vs = pl.BlockSpec((1, FFN_TN), lambda j, i: (0, j))
    ts = pl.BlockSpec((FFN_K, FFN_TN), lambda j, i: (0, j))
    return pl.pallas_call(
        body, out_shape=(full, full, vec, vec, taps, taps), grid=(N_FT, SEQ // ROWS),
        in_specs=_ffn_specs() + [cur], out_specs=(cur, cur, vs, vs, ts, ts),
        scratch_shapes=[pltpu.VMEM((2 * FFN_HALO, FFN_TN), F32), tile, tile, tile, tile,
                        pltpu.VMEM((2 + 2 * FFN_K, SUB, FFN_TN), F32)],
        name=name, compiler_params=_cp("parallel", "arbitrary"))(up0, up0, up0, up0, wf, wf, bf, bf, dact)


def ffn_bwd_b(dgate, dval, wf, name):
    per = ROWS // FFN_HALO
    last = SEQ // FFN_HALO - 1

    def body(g_ref, v_ref, gn_ref, vn_ref, w_ref, out_ref, pad, s1, s2):
        j = pl.program_id(0)
        i = pl.program_id(1)

        def run(cur_ref, nxt_ref):
            pad[0:FFN_HALO, :] = cur_ref[ROWS - FFN_HALO:, :]
            pad[FFN_HALO:, :] = jnp.where(i < SEQ // ROWS - 1, nxt_ref[...], 0.0)
            for k, dst in ((1, s1), (2, s2)):
                dst[0:ROWS - FFN_HALO, :] = cur_ref[pl.ds(k, ROWS - FFN_HALO), :]
                dst[ROWS - FFN_HALO:, :] = pad[pl.ds(k, FFN_HALO), :]

            def chunk(rs):
                out_ref[rs, :] = (w_ref[2:3, :] * cur_ref[rs, :] + w_ref[1:2, :] * s1[rs, :]
                                  + w_ref[0:1, :] * s2[rs, :]).astype(BF16)

            _for_chunks(chunk)

        @pl.when(j < N_FT)
        def _():
            run(g_ref, gn_ref)

        @pl.when(j >= N_FT)
        def _():
            run(v_ref, vn_ref)

    cur_g = pl.BlockSpec((ROWS, FFN_TN), lambda j, i: (i, jnp.minimum(j, N_FT - 1)))
    cur_v = pl.BlockSpec((ROWS, FFN_TN), lambda j, i: (i, jnp.maximum(j - N_FT, 0)))
    nxt_g = pl.BlockSpec((FFN_HALO, FFN_TN), lambda j, i: (jnp.minimum((i + 1) * per, last), jnp.minimum(j, N_FT - 1)))
    nxt_v = pl.BlockSpec((FFN_HALO, FFN_TN), lambda j, i: (jnp.minimum((i + 1) * per, last), jnp.maximum(j - N_FT, 0)))
    tile = pltpu.VMEM((ROWS, FFN_TN), F32)
    return pl.pallas_call(
        body, out_shape=jax.ShapeDtypeStruct((SEQ, 2 * D_FF), BF16), grid=(2 * N_FT, SEQ // ROWS),
        in_specs=[cur_g, cur_v, nxt_g, nxt_v, pl.BlockSpec((FFN_K, FFN_TN), lambda j, i: (0, j))],
        out_specs=pl.BlockSpec((ROWS, FFN_TN), lambda j, i: (i, j)),
        scratch_shapes=[pltpu.VMEM((2 * FFN_HALO, FFN_TN), F32), tile, tile],
        name=name, compiler_params=_cp("parallel", "parallel"))(dgate, dval, dgate, dval, wf)


def ada_fwd(c_all, w_ada, b_cols, name):
    def body(c_ref, w_ref, b_ref, o_ref):
        cc = c_ref[...]
        sc = (cc * _sig(cc)).astype(BF16)
        o_ref[...] = jnp.dot(sc, w_ref[...].astype(BF16), preferred_element_type=F32) + b_ref[...]

    return pl.pallas_call(body, out_shape=jax.ShapeDtypeStruct((N_DEV, w_ada.shape[1]), F32), name=name,
                          compiler_params=_cp())(c_all, w_ada, b_cols)


def _adam(w, g, m, v):
    m = ADAM_B1 * m + (1.0 - ADAM_B1) * g
    v = ADAM_B2 * v + (1.0 - ADAM_B2) * (g * g)
    m_hat = m / (1.0 - ADAM_B1 ** ADAM_STEP)
    v_hat = v / (1.0 - ADAM_B2 ** ADAM_STEP)
    delta = -ADAM_LR * (m_hat / (jnp.sqrt(v_hat) + ADAM_EPS) + ADAM_WD * w)
    return delta, m, v


def ada_bwd_adamw(c_all_t, dmod_cols, w, m, v, name):
    rows, cols = w.shape
    tr = 256

    def body(ct_ref, dm_ref, w_ref, m_ref, v_ref, g_ref, d_ref, nm_ref, nv_ref):
        ct = ct_ref[...]
        sc = ct * _sig(ct)
        g = sc[:, 0:1] * dm_ref[0:1, :]
        for b in range(1, N_DEV):
            g = g + sc[:, b:b + 1] * dm_ref[b:b + 1, :]
        g_ref[...] = g
        d_ref[...], nm_ref[...], nv_ref[...] = _adam(w_ref[...], g, m_ref[...], v_ref[...])

    blk = pl.BlockSpec((tr, cols), lambda i: (i, 0))
    shp = jax.ShapeDtypeStruct((rows, cols), F32)
    return pl.pallas_call(
        body, out_shape=(shp, shp, shp, shp), grid=(rows // tr,),
        in_specs=[pl.BlockSpec((tr, N_DEV), lambda i: (i, 0)), pl.BlockSpec((N_DEV, cols), lambda i: (0, 0)), blk, blk, blk],
        out_specs=(blk, blk, blk, blk), name=name, compiler_params=_cp("parallel"))(c_all_t, dmod_cols, w, m, v)


def sum_adamw(parts, own, w, m, v, tr, name):
    n_parts, rows, cols = parts.shape

    def body(*refs):
        if own is None:
            p_ref, w_ref, m_ref, v_ref, g_ref, d_ref, nm_ref, nv_ref = refs
            g = p_ref[0].astype(F32)
        else:
            p_ref, own_ref, w_ref, m_ref, v_ref, g_ref, d_ref, nm_ref, nv_ref = refs
            g = own_ref[...].astype(F32)
        for k in range(1, n_parts):
            g = g + p_ref[k].astype(F32)
        g_ref[...] = g
        d_ref[...], nm_ref[...], nv_ref[...] = _adam(w_ref[...], g, m_ref[...], v_ref[...])

    blk = pl.BlockSpec((tr, cols), lambda i: (i, 0))
    shp = jax.ShapeDtypeStruct((rows, cols), F32)
    args = [parts] + ([] if own is None else [own]) + [w, m, v]
    return pl.pallas_call(
        body, out_shape=(shp, shp, shp, shp), grid=(rows // tr,),
        in_specs=[pl.BlockSpec((n_parts, tr, cols), lambda i: (0, i, 0))] + [blk] * (len(args) - 1),
        out_specs=(blk, blk, blk, blk), name=name, compiler_params=_cp("parallel"))(*args)


MESH = pl.DeviceIdType.MESH
ANY = pl.BlockSpec(memory_space=pl.ANY)


def all_gather(block, name):
    def body(x_ref, out_ref, send_sems, recv_sems, local_sem):
        x, y, c = lax.axis_index("x"), lax.axis_index("y"), lax.axis_index("c")
        me, sibling = (x, y, c), (x, y, 1 - c)
        chips = [(1 - x, y), (x, 1 - y), (1 - x, 1 - y)]

        def slot(px, py, pc):
            return out_ref.at[4 * px + 2 * py + pc]

        def copy(k, blk, to, src=None):
            return pltpu.make_async_remote_copy(
                src_ref=slot(*blk) if src is None else src, dst_ref=slot(*blk),
                send_sem=send_sems.at[k], recv_sem=recv_sems.at[k], device_id=to, device_id_type=MESH)

        mine = pltpu.make_async_copy(x_ref, slot(*me), local_sem)
        mine.start()
        first = [copy(0, me, sibling, src=x_ref)]
        first += [copy(1 + j, me, (*chip, c), src=x_ref) for j, chip in enumerate(chips)]
        for cp in first:
            cp.start()
        passed = [copy(4 + j, (*chip, c), sibling) for j, chip in enumerate(chips)]
        for j, chip in enumerate(chips):
            copy(1 + j, (*chip, c), me).wait_recv()
            passed[j].start()
        copy(0, sibling, me).wait_recv()
        for j, chip in enumerate(chips):
            copy(4 + j, (*chip, 1 - c), me).wait_recv()
        for cp in first + passed:
            cp.wait_send()
        mine.wait()

    return pl.pallas_call(
        body, out_shape=jax.ShapeDtypeStruct((N_DEV,) + block.shape, block.dtype), in_specs=[ANY], out_specs=ANY,
        scratch_shapes=[pltpu.SemaphoreType.DMA((7,)), pltpu.SemaphoreType.DMA((7,)), pltpu.SemaphoreType.DMA],
        name=name)(block)


HBM = pl.BlockSpec(memory_space=pltpu.HBM)
SEM = pl.BlockSpec(memory_space=pltpu.SEMAPHORE)
EFFECT = pltpu.SideEffectType.DATAFLOW_SIDE_EFFECTING


def _peer_copies(src_ref, land_ref, send_sems, recv_sems, gather):
    x, y, c = lax.axis_index("x"), lax.axis_index("y"), lax.axis_index("c")
    me = 4 * x + 2 * y + c
    copies = []
    for k in range(1, N_DEV):
        px = 1 - x if k & 4 else x
        py = 1 - y if k & 2 else y
        pc = 1 - c if k & 1 else c
        copies.append(pltpu.make_async_remote_copy(
            src_ref=src_ref if gather else src_ref.at[4 * px + 2 * py + pc],
            dst_ref=land_ref.at[me] if gather else land_ref.at[k],
            send_sem=send_sems.at[k - 1], recv_sem=recv_sems.at[k - 1], device_id=(px, py, pc), device_id_type=MESH))
    return copies


def exchange_start(src, gather, name, after=None):
    land_shape = (N_DEV,) + src.shape if gather else src.shape
    extra = () if after is None else (after,)

    def body(src_ref, land_ref, *refs):
        send_sems, recv_sems, src_thru, land_thru, token = refs[len(extra):]
        for cp in _peer_copies(src_ref, land_ref, send_sems, recv_sems, gather):
            cp.start()
        token[...] = jnp.zeros_like(token)

    send_sems, recv_sems, src_thru, land_thru, token = pl.pallas_call(
        body, name=name,
        out_shape=(pltpu.SemaphoreType.DMA((N_DEV - 1,)), pltpu.SemaphoreType.DMA((N_DEV - 1,)),
                   pltpu.HBM(src.shape, src.dtype), pltpu.HBM(land_shape, src.dtype), jax.ShapeDtypeStruct((8, 128), F32)),
        in_specs=(HBM, HBM) + (ANY,) * len(extra), out_specs=(SEM, SEM, HBM, HBM, pl.BlockSpec(memory_space=pltpu.VMEM)),
        input_output_aliases={0: 2, 1: 3}, compiler_params=pltpu.CompilerParams(has_side_effects=EFFECT),
    )(pltpu.with_memory_space_constraint(src, pltpu.HBM),
      pltpu.with_memory_space_constraint(lax.empty(land_shape, src.dtype), pltpu.HBM), *extra)
    return (send_sems, recv_sems, src_thru, land_thru), token[0, 0]


def exchange_wait(handles, after, gather, name):
    send_sems, recv_sems, src_thru, land_thru = handles

    def body(src_ref, land_ref, send_sems, recv_sems, after_ref, src_dead, got_ref):
        for cp in _peer_copies(src_ref, land_ref, send_sems, recv_sems, gather):
            cp.wait_send()
            cp.wait_recv()

    return pl.pallas_call(
        body, name=name,
        out_shape=(pltpu.HBM(src_thru.shape, src_thru.dtype), pltpu.HBM(land_thru.shape, land_thru.dtype)),
        in_specs=(HBM, HBM, SEM, SEM, ANY), out_specs=(HBM, HBM), input_output_aliases={0: 0, 1: 1},
        compiler_params=pltpu.CompilerParams(has_side_effects=EFFECT),
    )(src_thru, land_thru, send_sems, recv_sems, after)[1]


def _to_pattern(a, r):
    return a.reshape(SEQ // r, r * a.shape[1])


def local_step(x, tgt, mod, get_w, put_grad, put_small, wc, wf, g_mix, bc, lg, lb, gco, gao, g_ffn, bf, g_fin):
    h1 = rms_mod_fwd(x, g_mix, mod, 0, 1, "h1_fwd")
    w_in = get_w("w_in", h1)
    proj = matmul(h1, w_in, "nn", F32, 256, D_IN, "proj_fwd")
    mix_a = conv_module_fwd(proj, wc, bc, lg, lb, gco, "conv_module_fwd")
    att, lse = attn_fwd_all(proj, "attn_fwd")
    mix_b = rms_gain_bf16(att, gao, "attn_out_norm")
    mixed = jnp.concatenate([mix_a, mix_b], axis=1)
    w_out = get_w("w_out", mixed)
    y1 = matmul(mixed, w_out, "nn", F32, 256, D_MODEL, "out_proj_fwd")
    x1, h2 = resid_rms_mod_fwd(x, y1, g_ffn, mod, 2, 3, 4, "x1_h2_fwd")
    w_up = get_w("w_up", h2)
    up0 = matmul(h2, w_up, "nn", F32, 256, D_FF, "up_fwd")
    act = ffn_act_fwd(up0, wf, bf, "ffn_act_fwd")
    w_down = get_w("w_down", act)
    y2 = matmul(act, w_down, "nn", F32, 256, D_MODEL, "down_fwd")
    loss_t, dx2, dy2, d_gfin, d_gaf = final_loss_bwd(x1, y2, tgt, g_fin, mod, 5, "loss_bwd")
    dact = matmul(dy2, w_down, "nt", F32, 256, FFN_TN, "down_bwd_x")
    dw_down = matmul(act, dy2, "tn", BF16, 256, D_MODEL, "down_bwd_w")
    dgate, dval, dbf_g, dbf_v, dwf_g, dwf_v = ffn_bwd_a(up0, dact, wf + put_grad("w_down", dw_down), bf, "ffn_bwd_a")
    dup0 = ffn_bwd_b(dgate, dval, wf, "ffn_bwd_b")
    dh2 = matmul(dup0, w_up, "nt", F32, 256, 512, "up_bwd_x")
    dw_up = matmul(h2, dup0, "tn", BF16, 256, 512, "up_bwd_w")
    dx1, d_shf, d_scf, d_gffn, dy1, d_gam = rms_mod_bwd(x1, dh2, dx2, g_ffn + put_grad("w_up", dw_up), mod, 4, y1, 2, "h2_bwd")
    dmixed = matmul(dy1, w_out, "nt", F32, 256, D_MODEL, "out_proj_bwd_x")
    dw_out = matmul(mixed, dy1, "tn", BF16, 256, D_MODEL, "out_proj_bwd_w")
    do, dd, d_gao = attn_combine_bwd(dmixed, att, gao + put_grad("w_out", dw_out), "attn_combine_bwd")
    dqkv = attn_bwd_all(proj, do, lse, dd, "attn_bwd")
    du1, d_gco, d_lg, d_lb, d_bc, d_wc = conv_module_bwd_a(proj, dmixed, wc, bc, lg, lb, gco, "conv_module_bwd_a")
    dproj_a = conv_module_bwd_b(proj, du1, wc, "conv_module_bwd_b")
    dproj = jnp.concatenate([dproj_a, dqkv[0], dqkv[1], dqkv[2]], axis=1)
    dh1 = matmul(dproj, w_in, "nt", F32, 256, D_MODEL, "proj_bwd_x")
    dx, d_shm, d_scm, d_gmix = rms_mod_bwd(x, dh1, dx1, g_mix, mod, 1, None, 0, "h1_bwd")
    dmod = jnp.concatenate([d_shm, d_scm, d_gam, d_shf, d_scf, d_gaf], axis=1)
    small = dict(g_norm_mix=d_gmix, b_conv_dw=d_bc, ln_conv_g=d_lg, ln_conv_b=d_lb, g_conv_out=d_gco, g_attn_out=d_gao,
                 g_norm_ffn=d_gffn, b_ffn_dw=jnp.concatenate([dbf_g, dbf_v], axis=1), g_final=d_gfin,
                 w_conv_dw=d_wc, w_ffn_dw=jnp.concatenate([dwf_g, dwf_v], axis=1), dmod=dmod)
    after = put_small(small)
    dw_in = matmul(h1, dproj, "tn", BF16, 256, 512, "proj_bwd_w")
    put_grad("w_in", dw_in, after)
    return loss_t[0, 0], dx


def _padw(a, width):
    return jnp.pad(a, ((0, 0), (0, width - a.shape[1])))


def pack_small(t):
    wide = jnp.concatenate([_padw(t["dmod"], PACK_W), _padw(t["b_ffn_dw"], PACK_W), _padw(t["w_ffn_dw"], PACK_W),
                            jnp.zeros((3, PACK_W), F32)], axis=0)
    z512 = jnp.zeros((1, 512), F32)
    narrow = jnp.concatenate([
        t["g_norm_mix"], t["g_norm_ffn"], t["g_final"],
        jnp.concatenate([t["b_conv_dw"], t["ln_conv_g"]], axis=1),
        jnp.concatenate([t["ln_conv_b"], t["g_conv_out"]], axis=1),
        jnp.concatenate([t["g_attn_out"], z512], axis=1),
        jnp.zeros((2, 1024), F32),
        jnp.pad(t["w_conv_dw"], ((0, 1), (0, 0))).reshape(16, 1024)], axis=0)
    return jnp.concatenate([wide, narrow.reshape(4, PACK_W), jnp.zeros((4, PACK_W), F32)], axis=0)


def unpack_small(p):
    narrow = p[8:12].reshape(24, 1024)
    return dict(
        dmod=p[0:1], b_ffn_dw=p[1:2, :2 * D_FF], w_ffn_dw=p[2:5, :2 * D_FF],
        g_norm_mix=narrow[0:1], g_norm_ffn=narrow[1:2], g_final=narrow[2:3],
        b_conv_dw=narrow[3:4, :512], ln_conv_g=narrow[3:4, 512:], ln_conv_b=narrow[4:5, :512], g_conv_out=narrow[4:5, 512:],
        g_attn_out=narrow[5:6, :512], w_conv_dw=narrow[8:24].reshape(32, 512)[:CONV_K])


def _embed(local, width, me):
    return lax.dynamic_update_slice(jnp.zeros((local.shape[0], width), F32), local, (0, me * local.shape[1]))


def _shard(full, n_cols, me):
    return lax.dynamic_slice(full, (0, me * n_cols), (full.shape[0], n_cols))


WEIGHTS = ["w_ada", "b_ada", "g_norm_mix", "w_in", "w_conv_dw", "b_conv_dw", "ln_conv_g", "ln_conv_b", "g_conv_out",
           "g_attn_out", "w_out", "g_norm_ffn", "w_up", "w_ffn_dw", "b_ffn_dw", "w_down", "g_final"]
SMALL_REPLICATED = ["g_norm_mix", "b_conv_dw", "ln_conv_g", "ln_conv_b", "g_conv_out", "g_attn_out", "g_norm_ffn",
                    "b_ffn_dw", "g_final"]


def kernel(x, c, w_ada, b_ada, g_norm_mix, w_in, w_conv_dw, b_conv_dw, ln_conv_g, ln_conv_b, g_conv_out, g_attn_out, w_out, g_norm_ffn, w_up, w_ffn_dw, b_ffn_dw, w_down, g_final, loss_target, m_w_ada, m_b_ada, m_g_norm_mix, m_w_in, m_w_conv_dw, m_b_conv_dw, m_ln_conv_g, m_ln_conv_b, m_g_conv_out, m_g_attn_out, m_w_out, m_g_norm_ffn, m_w_up, m_w_ffn_dw, m_b_ffn_dw, m_w_down, m_g_final, v_w_ada, v_b_ada, v_g_norm_mix, v_w_in, v_w_conv_dw, v_b_conv_dw, v_ln_conv_g, v_ln_conv_b, v_g_conv_out, v_g_attn_out, v_w_out, v_g_norm_ffn, v_w_up, v_w_ffn_dw, v_b_ffn_dw, v_w_down, v_g_final):
    args = dict(locals())
    me = 4 * lax.axis_index("x") + 2 * lax.axis_index("y") + lax.axis_index("c")

    def flat(name, prefix=""):
        a = args[prefix + name]
        return a.reshape(a.shape[-2] if a.ndim > 1 else 1, a.shape[-1])

    n_in, n_up, r_out, r_down = w_in.shape[2], w_up.shape[2], w_out.shape[1], w_down.shape[1]
    n_ada, n_wc, n_wf = w_ada.shape[2], w_conv_dw.shape[2], w_ffn_dw.shape[2]
    taps_c = jnp.pad(flat("w_conv_dw").reshape(1, CONV_K * n_wc), ((0, 0), (0, 2 * D_MODEL - CONV_K * n_wc)))
    taps_f = jnp.pad(flat("w_ffn_dw").reshape(1, FFN_K * n_wf), ((0, 0), (0, 3 * D_MODEL - FFN_K * n_wf)))
    first = jnp.concatenate([c, taps_c.reshape(2, D_MODEL), taps_f.reshape(3, D_MODEL), jnp.zeros((2, D_MODEL), F32)], axis=0)
    first_all = all_gather(first, "gather_c_taps")
    c_all = first_all[:, 0, :]
    wc_full = first_all[:, 1:3, :].reshape(N_DEV, 2 * D_MODEL)[:, :CONV_K * n_wc].reshape(N_DEV, CONV_K, n_wc)
    wc_full = wc_full.transpose(1, 0, 2).reshape(CONV_K, D_CONV)
    wf_full = first_all[:, 3:6, :].reshape(N_DEV, 3 * D_MODEL)[:, :FFN_K * n_wf].reshape(N_DEV, FFN_K, n_wf)
    wf_full = wf_full.transpose(1, 0, 2).reshape(FFN_K, 2 * D_FF)
    mod_cols = ada_fwd(c_all, flat("w_ada"), _shard(flat("b_ada"), n_ada, me), "ada_fwd")
    mod_all = all_gather(mod_cols, "gather_mod")
    mod = lax.dynamic_index_in_dim(mod_all, me, axis=1, keepdims=False).reshape(N_MOD, D_MODEL)
    mod = jnp.pad(mod, ((0, 2), (0, 0)))

    blocks = dict(w_in=flat("w_in").astype(BF16), w_up=flat("w_up").astype(BF16),
                  rows=jnp.concatenate([flat("w_out"), flat("w_down")], axis=0).astype(BF16))
    gathers, tok = {}, None
    for name in ("w_in", "rows", "w_up"):
        src = blocks[name] if tok is None else blocks[name] + tok.astype(BF16)
        gathers[name], tok = exchange_start(src, True, f"gather_{name}_start", mod_all)
    mod = mod + tok
    slot = lax.broadcasted_iota(jnp.int32, (N_DEV, 1, 1), 0)
    full = {}

    def gathered(name, after):
        land = exchange_wait(gathers[name], after, True, f"gather_{name}_wait")
        return jnp.where(slot == me, blocks[name][None], land)

    def get_w(name, after):
        if name in ("w_out", "w_down"):
            if "rows" not in full:
                full["rows"] = gathered("rows", after)
            rows = full["rows"]
            return rows[:, :r_out, :].reshape(D_MODEL, D_MODEL) if name == "w_out" else rows[:, r_out:, :].reshape(D_FF, D_MODEL)
        return gathered(name, after).transpose(1, 0, 2).reshape(D_MODEL, -1)

    exchanges, own, kept = {}, {}, {}

    def put_grad(name, dw, after=None):
        if name in ("w_in", "w_up"):
            dev_major = dw.reshape(D_MODEL, N_DEV, -1).transpose(1, 0, 2)
        else:
            dev_major = dw.reshape(N_DEV, -1, D_MODEL)
        own[name] = lax.dynamic_index_in_dim(dev_major, me, axis=0, keepdims=False)
        exchanges[name], token = exchange_start(dev_major, False, f"exchange_{name}_start", after)
        return token

    def put_small(small):
        kept["small_all"] = all_gather(pack_small(small), "gather_small")
        return kept["small_all"]

    loss_part, grad_x = local_step(
        x[0], loss_target[0], mod, get_w, put_grad, put_small, wc_full, wf_full,
        flat("g_norm_mix"), flat("b_conv_dw"), flat("ln_conv_g"), flat("ln_conv_b"), flat("g_conv_out"),
        flat("g_attn_out"), flat("g_norm_ffn"), flat("b_ffn_dw"), flat("g_final"))
    loss = lax.psum(loss_part, ("x", "y", "c"))
    small_all = kept["small_all"]

    out = {}

    def finish(name, tr, after):
        parts = exchange_wait(exchanges[name], after, False, f"exchange_{name}_wait")
        out[name] = sum_adamw(parts, own[name], flat(name), flat(name, "m_"), flat(name, "v_"), tr, "adamw_" + name)
        return out[name][0]

    def packed(prefix):
        t = {n: flat(n, prefix) for n in SMALL_REPLICATED}
        t["dmod"] = flat("b_ada", prefix)
        t["w_conv_dw"] = _embed(flat("w_conv_dw", prefix), D_CONV, me)
        t["w_ffn_dw"] = _embed(flat("w_ffn_dw", prefix), 2 * D_FF, me)
        return pack_small(t)

    res = sum_adamw(small_all, None, packed(""), packed("m_"), packed("v_"), PACK_ROWS, "adamw_small")
    after = res[0]
    res = [unpack_small(r) for r in res]
    for n in SMALL_REPLICATED:
        out[n] = tuple(r[n] for r in res)
    out["b_ada"] = tuple(r["dmod"] for r in res)
    out["w_conv_dw"] = tuple(_shard(r["w_conv_dw"], n_wc, me) for r in res)
    out["w_ffn_dw"] = tuple(_shard(r["w_ffn_dw"], n_wf, me) for r in res)

    dmod_cols = _shard(small_all[:, 0, :], n_ada, me)
    out["w_ada"] = ada_bwd_adamw(c_all.T, dmod_cols, flat("w_ada"), flat("w_ada", "m_"), flat("w_ada", "v_"), "adamw_w_ada")

    after = finish("w_down", r_down, out["w_ada"][0])
    after = finish("w_up", 256, after)
    after = finish("w_out", r_out, after)
    finish("w_in", 256, after)

    result = [loss, grad_x[None]]
    for k in range(4):
        result += [out[n][k].reshape(args[n].shape) for n in WEIGHTS]
    return tuple(result)
```

```python
import functools

import jax
import jax.numpy as jnp
from jax import lax
from jax.experimental import pallas as pl
from jax.experimental.pallas import tpu as pltpu

F32 = jnp.float32
BF16 = jnp.bfloat16

N_DEV = 8
SEQ = 2048
D_MODEL = 1024
D_CONV = 512
D_ATTN = 512
HEAD_DIM = 64
CONV_K = 31
D_FF = 2816
FFN_K = 3
D_IN = 2 * D_CONV + 3 * D_ATTN
N_MOD = 6
EPS = 1e-6
ATTN_BLOCK = 128
PATTERNS = ((2048, 1), (512, 4), (128, 16))
NEG = -1e30

ADAM_LR, ADAM_B1, ADAM_B2, ADAM_EPS, ADAM_WD, ADAM_STEP = 0.001, 0.9, 0.999, 1e-08, 0.01, 10

ROWS = 256
CONV_HALO = 32
FFN_HALO = 8
FFN_TN = 1408
VMEM_LIMIT = 56 * 1024 * 1024
PACK_ROWS, PACK_W = 16, 6144


def _cp(*sem):
    return pltpu.CompilerParams(dimension_semantics=sem if sem else None, vmem_limit_bytes=VMEM_LIMIT)


def _sig(x):
    return 1.0 / (1.0 + jnp.exp(-x))


def _rsum(x):
    return jnp.sum(x, axis=0, keepdims=True)


def _mean(x):
    return jnp.mean(x, axis=-1, keepdims=True)


def _acc(ref, val, first):
    @pl.when(first)
    def _():
        ref[...] = val

    @pl.when(jnp.logical_not(first))
    def _():
        ref[...] += val


def matmul(a, b, kind, out_dtype, tm, tn, name):
    if kind == "nn":
        (m, k), n = a.shape, b.shape[1]
        a_spec = pl.BlockSpec((tm, k), lambda j, i: (i, 0))
        b_spec = pl.BlockSpec((k, tn), lambda j, i: (0, j))
        dims = (((1,), (0,)), ((), ()))
    elif kind == "nt":
        (m, k), n = a.shape, b.shape[0]
        a_spec = pl.BlockSpec((tm, k), lambda j, i: (i, 0))
        b_spec = pl.BlockSpec((tn, k), lambda j, i: (j, 0))
        dims = (((1,), (1,)), ((), ()))
    else:
        (k, m), n = a.shape, b.shape[1]
        a_spec = pl.BlockSpec((k, tm), lambda j, i: (0, i))
        b_spec = pl.BlockSpec((k, tn), lambda j, i: (0, j))
        dims = (((0,), (0,)), ((), ()))
    assert m % tm == 0 and n % tn == 0, (name, m, n, tm, tn)

    def body(a_ref, b_ref, o_ref):
        o_ref[...] = lax.dot_general(a_ref[...], b_ref[...], dims, preferred_element_type=F32).astype(o_ref.dtype)

    return pl.pallas_call(
        body, out_shape=jax.ShapeDtypeStruct((m, n), out_dtype), grid=(n // tn, m // tm),
        in_specs=[a_spec, b_spec], out_specs=pl.BlockSpec((tm, tn), lambda j, i: (i, j)),
        name=name, compiler_params=_cp("parallel", "parallel"))(a, b)


def _row_spec(width, col=0):
    return pl.BlockSpec((ROWS, width), lambda i: (i, col))


def _vec_spec(width, rows=1):
    return pl.BlockSpec((rows, width), lambda i: (0, 0))


SUB = 16


def _for_chunks(fn, unroll=1):
    def step(i, carry):
        fn(pl.ds(pl.multiple_of(i * SUB, SUB), SUB))
        return carry

    lax.fori_loop(0, ROWS // SUB, step, 0, unroll=unroll)


def rms_mod_fwd(x, g, mod, sh_row, sc_row, name):
    def body(x_ref, g_ref, mod_ref, h_ref):
        xx = x_ref[...]
        r = lax.rsqrt(_mean(xx * xx) + EPS)
        h = xx * r * g_ref[...]
        h_ref[...] = (h * (1.0 + mod_ref[sc_row:sc_row + 1, :]) + mod_ref[sh_row:sh_row + 1, :]).astype(BF16)

    return pl.pallas_call(
        body, out_shape=jax.ShapeDtypeStruct((SEQ, D_MODEL), BF16), grid=(SEQ // ROWS,),
        in_specs=[_row_spec(D_MODEL), _vec_spec(D_MODEL), _vec_spec(D_MODEL, 8)],
        out_specs=_row_spec(D_MODEL), name=name, compiler_params=_cp("parallel"))(x, g, mod)


def resid_rms_mod_fwd(x, y, g, mod, ga_row, sh_row, sc_row, name):
    def body(x_ref, y_ref, g_ref, mod_ref, x1_ref, h_ref):
        x1 = x_ref[...] + mod_ref[ga_row:ga_row + 1, :] * y_ref[...]
        x1_ref[...] = x1
        r = lax.rsqrt(_mean(x1 * x1) + EPS)
        h = x1 * r * g_ref[...]
        h_ref[...] = (h * (1.0 + mod_ref[sc_row:sc_row + 1, :]) + mod_ref[sh_row:sh_row + 1, :]).astype(BF16)

    return pl.pallas_call(
        body, out_shape=(jax.ShapeDtypeStruct((SEQ, D_MODEL), F32), jax.ShapeDtypeStruct((SEQ, D_MODEL), BF16)),
        grid=(SEQ // ROWS,),
        in_specs=[_row_spec(D_MODEL), _row_spec(D_MODEL), _vec_spec(D_MODEL), _vec_spec(D_MODEL, 8)],
        out_specs=(_row_spec(D_MODEL), _row_spec(D_MODEL)), name=name, compiler_params=_cp("parallel"))(x, y, g, mod)


def final_loss_bwd(x1, y2, tgt, g, mod, ga_row, name):
    def body(x1_ref, y2_ref, t_ref, g_ref, mod_ref, loss_ref, dx2_ref, dy2_ref, dg_ref, dga_ref):
        first = pl.program_id(0) == 0
        ga = mod_ref[ga_row:ga_row + 1, :]
        y2 = y2_ref[...]
        x2 = x1_ref[...] + ga * y2
        r = lax.rsqrt(_mean(x2 * x2) + EPS)
        xn = x2 * r
        err = xn * g_ref[...] - t_ref[...]
        _acc(loss_ref, jnp.broadcast_to(0.5 * jnp.sum(_mean(err * err)), (8, 128)), first)
        dy = err * (1.0 / D_MODEL)
        _acc(dg_ref, _rsum(dy * xn), first)
        dxn = dy * g_ref[...]
        dx2 = r * (dxn - xn * _mean(dxn * xn))
        dx2_ref[...] = dx2
        dy2_ref[...] = (dx2 * ga).astype(BF16)
        _acc(dga_ref, _rsum(dx2 * y2), first)

    vec = jax.ShapeDtypeStruct((1, D_MODEL), F32)
    return pl.pallas_call(
        body,
        out_shape=(jax.ShapeDtypeStruct((8, 128), F32), jax.ShapeDtypeStruct((SEQ, D_MODEL), F32),
                   jax.ShapeDtypeStruct((SEQ, D_MODEL), BF16), vec, vec),
        grid=(SEQ // ROWS,),
        in_specs=[_row_spec(D_MODEL), _row_spec(D_MODEL), _row_spec(D_MODEL), _vec_spec(D_MODEL), _vec_spec(D_MODEL, 8)],
        out_specs=(pl.BlockSpec((8, 128), lambda i: (0, 0)), _row_spec(D_MODEL), _row_spec(D_MODEL),
                   _vec_spec(D_MODEL), _vec_spec(D_MODEL)),
        name=name, compiler_params=_cp("arbitrary"))(x1, y2, tgt, g, mod)


def rms_mod_bwd(x, dh, dres, g, mod, sc_row, y, ga_row, name):
    gated = y is not None

    def body(*refs):
        if gated:
            x_ref, dh_ref, dres_ref, g_ref, mod_ref, y_ref, dx_ref, dsh_ref, dsc_ref, dg_ref, dy_ref, dga_ref = refs
        else:
            x_ref, dh_ref, dres_ref, g_ref, mod_ref, dx_ref, dsh_ref, dsc_ref, dg_ref = refs
        first = pl.program_id(0) == 0
        xx = x_ref[...]
        dh = dh_ref[...]
        gg = g_ref[...]
        r = lax.rsqrt(_mean(xx * xx) + EPS)
        xn = xx * r
        _acc(dsh_ref, _rsum(dh), first)
        _acc(dsc_ref, _rsum(dh * (xn * gg)), first)
        dt = dh * (1.0 + mod_ref[sc_row:sc_row + 1, :])
        _acc(dg_ref, _rsum(dt * xn), first)
        dxn = dt * gg
        dx = dres_ref[...] + r * (dxn - xn * _mean(dxn * xn))
        dx_ref[...] = dx
        if gated:
            _acc(dga_ref, _rsum(dx * y_ref[...]), first)
            dy_ref[...] = (dx * mod_ref[ga_row:ga_row + 1, :]).astype(BF16)

    vec = jax.ShapeDtypeStruct((1, D_MODEL), F32)
    in_specs = [_row_spec(D_MODEL), _row_spec(D_MODEL), _row_spec(D_MODEL), _vec_spec(D_MODEL), _vec_spec(D_MODEL, 8)]
    out_shape = [jax.ShapeDtypeStruct((SEQ, D_MODEL), F32), vec, vec, vec]
    out_specs = [_row_spec(D_MODEL), _vec_spec(D_MODEL), _vec_spec(D_MODEL), _vec_spec(D_MODEL)]
    args = [x, dh, dres, g, mod]
    if gated:
        in_specs.append(_row_spec(D_MODEL))
        out_shape += [jax.ShapeDtypeStruct((SEQ, D_MODEL), BF16), vec]
        out_specs += [_row_spec(D_MODEL), _vec_spec(D_MODEL)]
        args.append(y)
    return pl.pallas_call(
        body, out_shape=tuple(out_shape), grid=(SEQ // ROWS,), in_specs=in_specs, out_specs=tuple(out_specs),
        name=name, compiler_params=_cp("arbitrary"))(*args)


def _prev_halo(halo, width, col):
    per = ROWS // halo
    return pl.BlockSpec((halo, width), lambda i: (jnp.maximum(i * per - 1, 0), col))


def _next_halo(halo, width, col):
    per = ROWS // halo
    last = SEQ // halo - 1
    return pl.BlockSpec((halo, width), lambda i: (jnp.minimum((i + 1) * per, last), col))


def _conv_module_forward(av_ref, ag_ref, avh_ref, agh_ref, wc_ref, bc_ref, lg_ref, lb_ref, u0p):
    i = pl.program_id(0)
    hv = avh_ref[...] * _sig(agh_ref[...])
    u0p[0:CONV_HALO, :] = jnp.where(i > 0, hv, 0.0)
    u0p[CONV_HALO:, :] = av_ref[...] * _sig(ag_ref[...])
    u1 = jnp.broadcast_to(bc_ref[...], (ROWS, D_CONV))
    for j in range(CONV_K):
        u1 = u1 + wc_ref[j:j + 1, :] * u0p[pl.ds(CONV_HALO - (CONV_K - 1) + j, ROWS), :]
    mu = _mean(u1)
    cen = u1 - mu
    rs = lax.rsqrt(_mean(cen * cen) + EPS)
    z = cen * rs
    ln = z * lg_ref[...] + lb_ref[...]
    s = _sig(ln)
    return z, rs, ln, s, ln * s


def conv_module_fwd(proj, wc, bc, lg, lb, gco, name):
    def body(av_ref, ag_ref, avh_ref, agh_ref, wc_ref, bc_ref, lg_ref, lb_ref, gco_ref, out_ref, u0p):
        _, _, _, _, u2 = _conv_module_forward(av_ref, ag_ref, avh_ref, agh_ref, wc_ref, bc_ref, lg_ref, lb_ref, u0p)
        rc = lax.rsqrt(_mean(u2 * u2) + EPS)
        out_ref[...] = (u2 * rc * gco_ref[...]).astype(BF16)

    v = _vec_spec(D_CONV)
    return pl.pallas_call(
        body, out_shape=jax.ShapeDtypeStruct((SEQ, D_CONV), BF16), grid=(SEQ // ROWS,),
        in_specs=[_row_spec(D_CONV, 0), _row_spec(D_CONV, 1), _prev_halo(CONV_HALO, D_CONV, 0),
                  _prev_halo(CONV_HALO, D_CONV, 1), _vec_spec(D_CONV, CONV_K), v, v, v, v],
        out_specs=_row_spec(D_CONV), scratch_shapes=[pltpu.VMEM((ROWS + CONV_HALO, D_CONV), F32)],
        name=name, compiler_params=_cp("parallel"))(proj, proj, proj, proj, wc, bc, lg, lb, gco)


def conv_module_bwd_a(proj, dmixed, wc, bc, lg, lb, gco, name):
    def body(av_ref, ag_ref, avh_ref, agh_ref, dm_ref, wc_ref, bc_ref, lg_ref, lb_ref, gco_ref,
             du1_ref, dgco_ref, dlg_ref, dlb_ref, dbc_ref, dwc_ref, u0p):
        first = pl.program_id(0) == 0
        z, rs, ln, s, u2 = _conv_module_forward(av_ref, ag_ref, avh_ref, agh_ref, wc_ref, bc_ref, lg_ref, lb_ref, u0p)
        rc = lax.rsqrt(_mean(u2 * u2) + EPS)
        xn = u2 * rc
        dm = dm_ref[...]
        _acc(dgco_ref, _rsum(dm * xn), first)
        dyn = dm * gco_ref[...]
        du2 = rc * (dyn - xn * _mean(dyn * xn))
        dln = du2 * (s * (1.0 + ln * (1.0 - s)))
        _acc(dlg_ref, _rsum(dln * z), first)
        _acc(dlb_ref, _rsum(dln), first)
        dz = dln * lg_ref[...]
        du1 = rs * (dz - _mean(dz) - z * _mean(dz * z))
        du1_ref[...] = du1
        _acc(dbc_ref, _rsum(du1), first)

        @pl.when(first)
        def _():
            dwc_ref[...] = jnp.zeros_like(dwc_ref)

        for j in range(CONV_K):
            dwc_ref[j:j + 1, :] += _rsum(du1 * u0p[pl.ds(CONV_HALO - (CONV_K - 1) + j, ROWS), :])

    v = _vec_spec(D_CONV)
    vec = jax.ShapeDtypeStruct((1, D_CONV), F32)
    return pl.pallas_call(
        body,
        out_shape=(jax.ShapeDtypeStruct((SEQ, D_CONV), F32), vec, vec, vec, vec, jax.ShapeDtypeStruct((CONV_K, D_CONV), F32)),
        grid=(SEQ // ROWS,),
        in_specs=[_row_spec(D_CONV, 0), _row_spec(D_CONV, 1), _prev_halo(CONV_HALO, D_CONV, 0),
                  _prev_halo(CONV_HALO, D_CONV, 1), _row_spec(D_CONV, 0), _vec_spec(D_CONV, CONV_K), v, v, v, v],
        out_specs=(_row_spec(D_CONV), v, v, v, v, _vec_spec(D_CONV, CONV_K)),
        scratch_shapes=[pltpu.VMEM((ROWS + CONV_HALO, D_CONV), F32)],
        name=name, compiler_params=_cp("arbitrary"))(proj, proj, proj, proj, dmixed, wc, bc, lg, lb, gco)


def conv_module_bwd_b(proj, du1, wc, name):
    def body(av_ref, ag_ref, du1_ref, du1n_ref, wc_ref, out_ref, dup):
        i = pl.program_id(0)
        dup[0:ROWS, :] = du1_ref[...]
        dup[ROWS:, :] = jnp.where(i < SEQ // ROWS - 1, du1n_ref[...], 0.0)
        du0 = jnp.zeros((ROWS, D_CONV), F32)
        for j in range(CONV_K):
            du0 = du0 + wc_ref[j:j + 1, :] * dup[pl.ds(CONV_K - 1 - j, ROWS), :]
        sg = _sig(ag_ref[...])
        out_ref[:, 0:D_CONV] = (du0 * sg).astype(BF16)
        out_ref[:, D_CONV:] = (du0 * av_ref[...] * sg * (1.0 - sg)).astype(BF16)

    return pl.pallas_call(
        body, out_shape=jax.ShapeDtypeStruct((SEQ, 2 * D_CONV), BF16), grid=(SEQ // ROWS,),
        in_specs=[_row_spec(D_CONV, 0), _row_spec(D_CONV, 1), _row_spec(D_CONV, 0), _next_halo(CONV_HALO, D_CONV, 0),
                  _vec_spec(D_CONV, CONV_K)],
        out_specs=_row_spec(2 * D_CONV), scratch_shapes=[pltpu.VMEM((ROWS + CONV_HALO, D_CONV), F32)],
        name=name, compiler_params=_cp("parallel"))(proj, proj, du1, du1, wc)


def _attn_specs(sub_len, pairs):
    ng, width = 4 // pairs, 128 * pairs
    q = pl.BlockSpec((sub_len, width), lambda rho, g: (0, rho * 3 * ng + g))
    k = pl.BlockSpec((sub_len, width), lambda rho, g: (0, rho * 3 * ng + ng + g))
    v = pl.BlockSpec((sub_len, width), lambda rho, g: (0, rho * 3 * ng + 2 * ng + g))
    o = pl.BlockSpec((sub_len, width), lambda rho, g: (0, rho * ng + g))
    return q, k, v, o


ATTN_PAIRS = {1: 1, 4: 1, 16: 4}


def _attn_block(q_ref, k_ref, v_ref, n, hs, win):
    q0 = pl.multiple_of(n * ATTN_BLOCK, ATTN_BLOCK)
    k0 = pl.multiple_of(jnp.maximum(n - 1, 0) * ATTN_BLOCK, ATTN_BLOCK)
    qb = q_ref[pl.ds(q0, ATTN_BLOCK), hs]
    kw = k_ref[pl.ds(k0, win), hs]
    vw = v_ref[pl.ds(k0, win), hs]
    s = lax.dot_general(qb, kw, (((1,), (1,)), ((), ())), preferred_element_type=F32) * (HEAD_DIM ** -0.5)
    dist = (q0 - k0) + lax.broadcasted_iota(jnp.int32, (ATTN_BLOCK, win), 0) \
        - lax.broadcasted_iota(jnp.int32, (ATTN_BLOCK, win), 1)
    s = jnp.where((dist >= 0) & (dist <= ATTN_BLOCK), s, NEG)
    return q0, k0, qb, kw, vw, s


def attn_fwd(qkv_r, sub_len, r, name):
    nb = sub_len // ATTN_BLOCK
    win = 2 * ATTN_BLOCK if nb > 1 else ATTN_BLOCK
    pairs = ATTN_PAIRS[r]

    def body(q_ref, k_ref, v_ref, o_ref, l_ref):
        def block(n, carry):
            for h in range(2 * pairs):
                hs = slice(h * HEAD_DIM, (h + 1) * HEAD_DIM)
                q0, _, _, _, vw, s = _attn_block(q_ref, k_ref, v_ref, n, hs, win)
                m = jnp.max(s, axis=1, keepdims=True)
                p = jnp.exp(s - m)
                den = jnp.sum(p, axis=1, keepdims=True)
                o = jnp.dot(p.astype(BF16), vw, preferred_element_type=F32) / den
                o_ref[pl.ds(q0, ATTN_BLOCK), hs] = o
                l_ref[pl.ds(q0, ATTN_BLOCK), hs] = jnp.broadcast_to(m + jnp.log(den), (ATTN_BLOCK, HEAD_DIM))
            return carry

        lax.fori_loop(0, nb, block, 0, unroll=min(nb, 2))

    q, k, v, o = _attn_specs(sub_len, pairs)
    shp = jax.ShapeDtypeStruct((sub_len, r * D_ATTN), F32)
    return pl.pallas_call(
        body, out_shape=(shp, shp), grid=(r, 4 // pairs), in_specs=[q, k, v], out_specs=(o, o),
        name=name, compiler_params=_cp("parallel", "parallel"))(qkv_r, qkv_r, qkv_r)


def attn_bwd(qkv_r, do_r, lse_r, dd_r, sub_len, r, name):
    nb = sub_len // ATTN_BLOCK
    win = 2 * ATTN_BLOCK if nb > 1 else ATTN_BLOCK
    pairs = ATTN_PAIRS[r]

    def body(q_ref, k_ref, v_ref, do_ref, l_ref, dd_ref, dq_ref, dk_ref, dv_ref):
        dk_ref[...] = jnp.zeros_like(dk_ref)
        dv_ref[...] = jnp.zeros_like(dv_ref)

        def block(n, carry):
            for h in range(2 * pairs):
                hs = slice(h * HEAD_DIM, (h + 1) * HEAD_DIM)
                h1 = slice(h * HEAD_DIM, h * HEAD_DIM + 1)
                q0, k0, qb, kw, vw, s = _attn_block(q_ref, k_ref, v_ref, n, hs, win)
                dob = do_ref[pl.ds(q0, ATTN_BLOCK), hs]
                p = jnp.exp(s - l_ref[pl.ds(q0, ATTN_BLOCK), h1])
                dp = lax.dot_general(dob, vw, (((1,), (1,)), ((), ())), preferred_element_type=F32)
                ds = (p * (dp - dd_ref[pl.ds(q0, ATTN_BLOCK), h1]) * (HEAD_DIM ** -0.5)).astype(BF16)
                dq_ref[pl.ds(q0, ATTN_BLOCK), hs] = jnp.dot(ds, kw, preferred_element_type=F32)
                dk_ref[pl.ds(k0, win), hs] += lax.dot_general(ds, qb, (((0,), (0,)), ((), ())), preferred_element_type=F32)
                dv_ref[pl.ds(k0, win), hs] += lax.dot_general(p.astype(BF16), dob, (((0,), (0,)), ((), ())),
                                                              preferred_element_type=F32)
            return carry

        lax.fori_loop(0, nb, block, 0)

    q, k, v, o = _attn_specs(sub_len, pairs)
    shp = jax.ShapeDtypeStruct((sub_len, r * D_ATTN), F32)
    return pl.pallas_call(
        body, out_shape=(shp, shp, shp), grid=(r, 4 // pairs), in_specs=[q, k, v, o, o, o], out_specs=(o, o, o),
        name=name, compiler_params=_cp("parallel", "parallel"))(qkv_r, qkv_r, qkv_r, do_r, lse_r, dd_r)


def _rows(start, size, r):
    return pl.ds(start, size) if r == 1 else pl.ds(start, size, stride=r)


def _attn_unit(q_ref, k_ref, v_ref, r, rho, n, nb):
    win = 2 * ATTN_BLOCK if nb > 1 else ATTN_BLOCK
    if isinstance(n, int):
        kb = max(n - 1, 0)
        q_rows = _rows(rho + r * ATTN_BLOCK * n, ATTN_BLOCK, r)
        k_rows = _rows(rho + r * ATTN_BLOCK * kb, win, r)
    else:
        kb = jnp.maximum(n - 1, 0)
        q_rows = pl.ds(pl.multiple_of(n * ATTN_BLOCK, ATTN_BLOCK), ATTN_BLOCK)
        k_rows = pl.ds(pl.multiple_of(kb * ATTN_BLOCK, ATTN_BLOCK), win)
    q2 = q_ref[q_rows, :].astype(BF16)
    k2 = k_ref[k_rows, :].astype(BF16)
    v2 = v_ref[k_rows, :].astype(BF16)
    dist = (n - kb) * ATTN_BLOCK + lax.broadcasted_iota(jnp.int32, (ATTN_BLOCK, win), 0) \
        - lax.broadcasted_iota(jnp.int32, (ATTN_BLOCK, win), 1)
    valid = (dist >= 0) & (dist <= ATTN_BLOCK)

    def score(h):
        hs = slice(h * HEAD_DIM, (h + 1) * HEAD_DIM)
        s = lax.dot_general(q2[:, hs], k2[:, hs], (((1,), (1,)), ((), ())), preferred_element_type=F32)
        return jnp.where(valid, s * (HEAD_DIM ** -0.5), NEG), q2[:, hs], k2[:, hs], v2[:, hs]

    return q_rows, k_rows, score


def _attn_units(r, nb, unit):
    if r == 1:
        def two(i, carry):
            unit(0, 2 * i)
            unit(0, 2 * i + 1)
            return carry
        lax.fori_loop(0, nb // 2, two, 0)
    else:
        for rho in range(r):
            for n in range(nb):
                unit(rho, n)


def attn_fwd_all(proj, name):
    def body(q_ref, k_ref, v_ref, att_ref, lse_ref):
        for idx, (sub_len, r) in enumerate(PATTERNS):
            nb = sub_len // ATTN_BLOCK

            def unit(rho, n, r=r, nb=nb, idx=idx):
                q_rows, _, score = _attn_unit(q_ref, k_ref, v_ref, r, rho, n, nb)
                outs, lses = [], []
                for h in range(2):
                    s, _, _, vw = score(h)
                    m = jnp.max(s, axis=1, keepdims=True)
                    p = jnp.exp(s - m)
                    den = jnp.sum(p, axis=1, keepdims=True)
                    outs.append(jnp.dot(p.astype(BF16), vw, preferred_element_type=F32) / den)
                    lses.append(jnp.broadcast_to(m + jnp.log(den), (ATTN_BLOCK, HEAD_DIM)))
                o = jnp.concatenate(outs, axis=1)
                lse = jnp.concatenate(lses, axis=1)
                if idx > 0:
                    old = lse_ref[q_rows, :]
                    top = jnp.maximum(old, lse)
                    new = top + jnp.log(jnp.exp(old - top) + jnp.exp(lse - top))
                    o = att_ref[q_rows, :] * jnp.exp(old - new) + o * jnp.exp(lse - new)
                    lse = new
                att_ref[q_rows, :] = o
                lse_ref[q_rows, :] = lse

            _attn_units(r, nb, unit)

    blk = lambda first: pl.BlockSpec((SEQ, 128), lambda g: (0, first + g))
    shp = jax.ShapeDtypeStruct((SEQ, D_ATTN), F32)
    return pl.pallas_call(
        body, out_shape=(shp, shp), grid=(4,), in_specs=[blk(8), blk(12), blk(16)], out_specs=(blk(0), blk(0)),
        name=name, compiler_params=_cp("parallel"))(proj, proj, proj)


def attn_bwd_all(proj, do, lse, dd, name):
    def body(q_ref, k_ref, v_ref, do_ref, l_ref, dd_ref, out_ref, dq_s, dk_s, dv_s):
        dq_s[...] = jnp.zeros_like(dq_s)
        dk_s[...] = jnp.zeros_like(dk_s)
        dv_s[...] = jnp.zeros_like(dv_s)
        for sub_len, r in PATTERNS:
            nb = sub_len // ATTN_BLOCK

            def unit(rho, n, r=r, nb=nb):
                q_rows, k_rows, score = _attn_unit(q_ref, k_ref, v_ref, r, rho, n, nb)
                do2 = do_ref[q_rows, :].astype(BF16)
                l2 = l_ref[q_rows, :]
                d2 = dd_ref[q_rows, :]
                dq, dk, dv = [], [], []
                for h in range(2):
                    hs = slice(h * HEAD_DIM, (h + 1) * HEAD_DIM)
                    h1 = slice(h * HEAD_DIM, h * HEAD_DIM + 1)
                    s, qb, kw, vw = score(h)
                    dob = do2[:, hs]
                    p = jnp.exp(s - l2[:, h1])
                    dp = lax.dot_general(dob, vw, (((1,), (1,)), ((), ())), preferred_element_type=F32)
                    ds = (p * (dp - d2[:, h1]) * (HEAD_DIM ** -0.5)).astype(BF16)
                    dq.append(jnp.dot(ds, kw, preferred_element_type=F32))
                    dk.append(lax.dot_general(ds, qb, (((0,), (0,)), ((), ())), preferred_element_type=F32))
                    dv.append(lax.dot_general(p.astype(BF16), dob, (((0,), (0,)), ((), ())), preferred_element_type=F32))
                dq_s[q_rows, :] += jnp.concatenate(dq, axis=1)
                dk_s[k_rows, :] += jnp.concatenate(dk, axis=1)
                dv_s[k_rows, :] += jnp.concatenate(dv, axis=1)

            _attn_units(r, nb, unit)
        out_ref[0] = dq_s[...].astype(BF16)
        out_ref[1] = dk_s[...].astype(BF16)
        out_ref[2] = dv_s[...].astype(BF16)

    blk = lambda first: pl.BlockSpec((SEQ, 128), lambda g: (0, first + g))
    acc = pltpu.VMEM((SEQ, 128), F32)
    return pl.pallas_call(
        body, out_shape=jax.ShapeDtypeStruct((3, SEQ, D_ATTN), BF16), grid=(4,),
        in_specs=[blk(8), blk(12), blk(16), blk(0), blk(0), blk(0)],
        out_specs=pl.BlockSpec((3, SEQ, 128), lambda g: (0, 0, g)), scratch_shapes=[acc, acc, acc],
        name=name, compiler_params=_cp("parallel"))(proj, proj, proj, do, lse, dd)


def rms_gain_bf16(a, g, name):
    def body(a_ref, g_ref, o_ref):
        aa = a_ref[...]
        o_ref[...] = (aa * lax.rsqrt(_mean(aa * aa) + EPS) * g_ref[...]).astype(BF16)

    w = a.shape[1]
    return pl.pallas_call(
        body, out_shape=jax.ShapeDtypeStruct(a.shape, BF16), grid=(SEQ // ROWS,), in_specs=[_row_spec(w), _vec_spec(w)],
        out_specs=_row_spec(w), name=name, compiler_params=_cp("parallel"))(a, g)


def attn_combine_fwd(outs, lses, gao, name):
    def body(o1, o2, o3, l1, l2, l3, g_ref, att_ref, lse_ref, mix_ref):
        a1, a2, a3 = l1[...], l2[...], l3[...]
        m = jnp.maximum(jnp.maximum(a1, a2), a3)
        w1, w2, w3 = jnp.exp(a1 - m), jnp.exp(a2 - m), jnp.exp(a3 - m)
        den = w1 + w2 + w3
        att = (w1 * o1[...] + w2 * o2[...] + w3 * o3[...]) / den
        att_ref[...] = att
        lse_ref[...] = m + jnp.log(den)
        mix_ref[...] = (att * lax.rsqrt(_mean(att * att) + EPS) * g_ref[...]).astype(BF16)

    rs = _row_spec(D_ATTN)
    f = jax.ShapeDtypeStruct((SEQ, D_ATTN), F32)
    return pl.pallas_call(
        body, out_shape=(f, f, jax.ShapeDtypeStruct((SEQ, D_ATTN), BF16)), grid=(SEQ // ROWS,),
        in_specs=[rs] * 6 + [_vec_spec(D_ATTN)], out_specs=(rs, rs, rs),
        name=name, compiler_params=_cp("parallel"))(*outs, *lses, gao)


def attn_combine_bwd(dmixed, att, gao, name):
    def body(dm_ref, att_ref, g_ref, do_ref, dd_ref, dg_ref):
        first = pl.program_id(0) == 0
        att = att_ref[...]
        r = lax.rsqrt(_mean(att * att) + EPS)
        xn = att * r
        dm = dm_ref[...]
        _acc(dg_ref, _rsum(dm * xn), first)
        dyn = dm * g_ref[...]
        do = r * (dyn - xn * _mean(dyn * xn))
        do_ref[...] = do
        same_head = (jnp.right_shift(lax.broadcasted_iota(jnp.int32, (D_ATTN, D_ATTN), 0), 6)
                     == jnp.right_shift(lax.broadcasted_iota(jnp.int32, (D_ATTN, D_ATTN), 1), 6)).astype(F32)
        dd_ref[...] = jnp.dot(do * att, same_head, preferred_element_type=F32, precision=lax.Precision.HIGHEST)

    rs = _row_spec(D_ATTN)
    return pl.pallas_call(
        body,
        out_shape=(jax.ShapeDtypeStruct((SEQ, D_ATTN), F32), jax.ShapeDtypeStruct((SEQ, D_ATTN), F32),
                   jax.ShapeDtypeStruct((1, D_ATTN), F32)),
        grid=(SEQ // ROWS,), in_specs=[_row_spec(D_ATTN, 1), rs, _vec_spec(D_ATTN)],
        out_specs=(rs, rs, _vec_spec(D_ATTN)), name=name, compiler_params=_cp("arbitrary"))(dmixed, att, gao)


def sum3_bf16(a, b, c, name):
    def body(a_ref, b_ref, c_ref, o_ref):
        o_ref[...] = (a_ref[...] + b_ref[...] + c_ref[...]).astype(BF16)

    w = a.shape[1]
    rs = _row_spec(w)
    return pl.pallas_call(
        body, out_shape=jax.ShapeDtypeStruct(a.shape, BF16), grid=(SEQ // ROWS,), in_specs=[rs, rs, rs], out_specs=rs,
        name=name, compiler_params=_cp("parallel"))(a, b, c)


N_FT = D_FF // FFN_TN


def _ffn_specs():
    per = ROWS // FFN_HALO
    cur_g = pl.BlockSpec((ROWS, FFN_TN), lambda j, i: (i, j))
    cur_v = pl.BlockSpec((ROWS, FFN_TN), lambda j, i: (i, j + N_FT))
    halo_g = pl.BlockSpec((FFN_HALO, FFN_TN), lambda j, i: (jnp.maximum(i * per - 1, 0), j))
    halo_v = pl.BlockSpec((FFN_HALO, FFN_TN), lambda j, i: (jnp.maximum(i * per - 1, 0), j + N_FT))
    w_g = pl.BlockSpec((FFN_K, FFN_TN), lambda j, i: (0, j))
    w_v = pl.BlockSpec((FFN_K, FFN_TN), lambda j, i: (0, j + N_FT))
    b_g = pl.BlockSpec((1, FFN_TN), lambda j, i: (0, j))
    b_v = pl.BlockSpec((1, FFN_TN), lambda j, i: (0, j + N_FT))
    return [cur_g, cur_v, halo_g, halo_v, w_g, w_v, b_g, b_v]


def _ffn_shifted(cur_ref, halo_ref, pad, s1, s2):
    i = pl.program_id(1)
    pad[0:FFN_HALO, :] = jnp.where(i > 0, halo_ref[...], 0.0)
    pad[FFN_HALO:, :] = cur_ref[0:FFN_HALO, :]
    for k, dst in ((1, s1), (2, s2)):
        dst[0:FFN_HALO, :] = pad[pl.ds(FFN_HALO - k, FFN_HALO), :]
        dst[FFN_HALO:, :] = cur_ref[pl.ds(FFN_HALO - k, ROWS - FFN_HALO), :]


def _ffn_conv(rs, cur_ref, s1, s2, w_ref, b_ref):
    return b_ref[...] + w_ref[0:1, :] * s2[rs, :] + w_ref[1:2, :] * s1[rs, :] + w_ref[2:3, :] * cur_ref[rs, :]


def ffn_act_fwd(up0, wf, bf, name):
    def body(g_ref, v_ref, gh_ref, vh_ref, wg_ref, wv_ref, bg_ref, bv_ref, act_ref, pad, g1, g2, v1, v2):
        _ffn_shifted(g_ref, gh_ref, pad, g1, g2)
        _ffn_shifted(v_ref, vh_ref, pad, v1, v2)

        def chunk(rs):
            gate = _ffn_conv(rs, g_ref, g1, g2, wg_ref, bg_ref)
            val = _ffn_conv(rs, v_ref, v1, v2, wv_ref, bv_ref)
            act_ref[rs, :] = (gate * _sig(gate) * val).astype(BF16)

        _for_chunks(chunk)

    tile = pltpu.VMEM((ROWS, FFN_TN), F32)
    return pl.pallas_call(
        body, out_shape=jax.ShapeDtypeStruct((SEQ, D_FF), BF16), grid=(N_FT, SEQ // ROWS),
        in_specs=_ffn_specs(), out_specs=pl.BlockSpec((ROWS, FFN_TN), lambda j, i: (i, j)),
        scratch_shapes=[pltpu.VMEM((2 * FFN_HALO, FFN_TN), F32), tile, tile, tile, tile],
        name=name, compiler_params=_cp("parallel", "parallel"))(up0, up0, up0, up0, wf, wf, bf, bf)


def ffn_bwd_a(up0, dact, wf, bf, name):
    def body(g_ref, v_ref, gh_ref, vh_ref, wg_ref, wv_ref, bg_ref, bv_ref, da_ref,
             dg_ref, dv_ref, dbg_ref, dbv_ref, dwg_ref, dwv_ref, pad, g1, g2, v1, v2, acc):
        first = pl.program_id(1) == 0
        _ffn_shifted(g_ref, gh_ref, pad, g1, g2)
        _ffn_shifted(v_ref, vh_ref, pad, v1, v2)
        acc[...] = jnp.zeros_like(acc)

        def chunk(rs):
            gate = _ffn_conv(rs, g_ref, g1, g2, wg_ref, bg_ref)
            val = _ffn_conv(rs, v_ref, v1, v2, wv_ref, bv_ref)
            s = _sig(gate)
            da = da_ref[rs, :]
            dgate = da * val * (s * (1.0 + gate * (1.0 - s)))
            dval = da * (gate * s)
            dg_ref[rs, :] = dgate
            dv_ref[rs, :] = dval
            acc[0] += dgate
            acc[1] += dval
            for t, (sg, sv) in enumerate(((g2, v2), (g1, v1), (g_ref, v_ref))):
                acc[2 + t] += dgate * sg[rs, :]
                acc[5 + t] += dval * sv[rs, :]

        _for_chunks(chunk)
        _acc(dbg_ref, _rsum(acc[0]), first)
        _acc(dbv_ref, _rsum(acc[1]), first)
        _acc(dwg_ref, jnp.concatenate([_rsum(acc[2 + t]) for t in range(FFN_K)], axis=0), first)
        _acc(dwv_ref, jnp.concatenate([_rsum(acc[5 + t]) for t in range(FFN_K)], axis=0), first)

    tile = pltpu.VMEM((ROWS, FFN_TN), F32)
    full = jax.ShapeDtypeStruct((SEQ, D_FF), F32)
    vec = jax.ShapeDtypeStruct((1, D_FF), F32)
    taps = jax.ShapeDtypeStruct((FFN_K, D_FF), F32)
    cur = pl.BlockSpec((ROWS, FFN_TN), lambda j, i: (i, j))
    vs = pl.BlockSpec((1, FFN_TN), lambda j, i: (0, j))
    ts = pl.BlockSpec((FFN_K, FFN_TN), lambda j, i: (0, j))
    return pl.pallas_call(
        body, out_shape=(full, full, vec, vec, taps, taps), grid=(N_FT, SEQ // ROWS),
        in_specs=_ffn_specs() + [cur], out_specs=(cur, cur, vs, vs, ts, ts),
        scratch_shapes=[pltpu.VMEM((2 * FFN_HALO, FFN_TN), F32), tile, tile, tile, tile,
                        pltpu.VMEM((2 + 2 * FFN_K, SUB, FFN_TN), F32)],
        name=name, compiler_params=_cp("parallel", "arbitrary"))(up0, up0, up0, up0, wf, wf, bf, bf, dact)


def ffn_bwd_b(dgate, dval, wf, name):
    per = ROWS // FFN_HALO
    last = SEQ // FFN_HALO - 1

    def body(g_ref, v_ref, gn_ref, vn_ref, w_ref, out_ref, pad, s1, s2):
        j = pl.program_id(0)
        i = pl.program_id(1)

        def run(cur_ref, nxt_ref):
            pad[0:FFN_HALO, :] = cur_ref[ROWS - FFN_HALO:, :]
            pad[FFN_HALO:, :] = jnp.where(i < SEQ // ROWS - 1, nxt_ref[...], 0.0)
            for k, dst in ((1, s1), (2, s2)):
                dst[0:ROWS - FFN_HALO, :] = cur_ref[pl.ds(k, ROWS - FFN_HALO), :]
                dst[ROWS - FFN_HALO:, :] = pad[pl.ds(k, FFN_HALO), :]

            def chunk(rs):
                out_ref[rs, :] = (w_ref[2:3, :] * cur_ref[rs, :] + w_ref[1:2, :] * s1[rs, :]
                                  + w_ref[0:1, :] * s2[rs, :]).astype(BF16)

            _for_chunks(chunk)

        @pl.when(j < N_FT)
        def _():
            run(g_ref, gn_ref)

        @pl.when(j >= N_FT)
        def _():
            run(v_ref, vn_ref)

    cur_g = pl.BlockSpec((ROWS, FFN_TN), lambda j, i: (i, jnp.minimum(j, N_FT - 1)))
    cur_v = pl.BlockSpec((ROWS, FFN_TN), lambda j, i: (i, jnp.maximum(j - N_FT, 0)))
    nxt_g = pl.BlockSpec((FFN_HALO, FFN_TN), lambda j, i: (jnp.minimum((i + 1) * per, last), jnp.minimum(j, N_FT - 1)))
    nxt_v = pl.BlockSpec((FFN_HALO, FFN_TN), lambda j, i: (jnp.minimum((i + 1) * per, last), jnp.maximum(j - N_FT, 0)))
    tile = pltpu.VMEM((ROWS, FFN_TN), F32)
    return pl.pallas_call(
        body, out_shape=jax.ShapeDtypeStruct((SEQ, 2 * D_FF), BF16), grid=(2 * N_FT, SEQ // ROWS),
        in_specs=[cur_g, cur_v, nxt_g, nxt_v, pl.BlockSpec((FFN_K, FFN_TN), lambda j, i: (0, j))],
        out_specs=pl.BlockSpec((ROWS, FFN_TN), lambda j, i: (i, j)),
        scratch_shapes=[pltpu.VMEM((2 * FFN_HALO, FFN_TN), F32), tile, tile],
        name=name, compiler_params=_cp("parallel", "parallel"))(dgate, dval, dgate, dval, wf)


def ada_fwd(c_all, w_ada, b_cols, name):
    def body(c_ref, w_ref, b_ref, o_ref):
        cc = c_ref[...]
        sc = (cc * _sig(cc)).astype(BF16)
        o_ref[...] = jnp.dot(sc, w_ref[...].astype(BF16), preferred_element_type=F32) + b_ref[...]

    return pl.pallas_call(body, out_shape=jax.ShapeDtypeStruct((N_DEV, w_ada.shape[1]), F32), name=name,
                          compiler_params=_cp())(c_all, w_ada, b_cols)


def _adam(w, g, m, v):
    m = ADAM_B1 * m + (1.0 - ADAM_B1) * g
    v = ADAM_B2 * v + (1.0 - ADAM_B2) * (g * g)
    m_hat = m / (1.0 - ADAM_B1 ** ADAM_STEP)
    v_hat = v / (1.0 - ADAM_B2 ** ADAM_STEP)
    delta = -ADAM_LR * (m_hat / (jnp.sqrt(v_hat) + ADAM_EPS) + ADAM_WD * w)
    return delta, m, v


def ada_bwd_adamw(c_all_t, dmod_cols, w, m, v, name):
    rows, cols = w.shape
    tr = 256

    def body(ct_ref, dm_ref, w_ref, m_ref, v_ref, g_ref, d_ref, nm_ref, nv_ref):
        ct = ct_ref[...]
        sc = ct * _sig(ct)
        g = sc[:, 0:1] * dm_ref[0:1, :]
        for b in range(1, N_DEV):
            g = g + sc[:, b:b + 1] * dm_ref[b:b + 1, :]
        g_ref[...] = g
        d_ref[...], nm_ref[...], nv_ref[...] = _adam(w_ref[...], g, m_ref[...], v_ref[...])

    blk = pl.BlockSpec((tr, cols), lambda i: (i, 0))
    shp = jax.ShapeDtypeStruct((rows, cols), F32)
    return pl.pallas_call(
        body, out_shape=(shp, shp, shp, shp), grid=(rows // tr,),
        in_specs=[pl.BlockSpec((tr, N_DEV), lambda i: (i, 0)), pl.BlockSpec((N_DEV, cols), lambda i: (0, 0)), blk, blk, blk],
        out_specs=(blk, blk, blk, blk), name=name, compiler_params=_cp("parallel"))(c_all_t, dmod_cols, w, m, v)


def sum_adamw(parts, own, w, m, v, tr, name):
    n_parts, rows, cols = parts.shape

    def body(*refs):
        if own is None:
            p_ref, w_ref, m_ref, v_ref, g_ref, d_ref, nm_ref, nv_ref = refs
            g = p_ref[0].astype(F32)
        else:
            p_ref, own_ref, w_ref, m_ref, v_ref, g_ref, d_ref, nm_ref, nv_ref = refs
            g = own_ref[...].astype(F32)
        for k in range(1, n_parts):
            g = g + p_ref[k].astype(F32)
        g_ref[...] = g
        d_ref[...], nm_ref[...], nv_ref[...] = _adam(w_ref[...], g, m_ref[...], v_ref[...])

    blk = pl.BlockSpec((tr, cols), lambda i: (i, 0))
    shp = jax.ShapeDtypeStruct((rows, cols), F32)
    args = [parts] + ([] if own is None else [own]) + [w, m, v]
    return pl.pallas_call(
        body, out_shape=(shp, shp, shp, shp), grid=(rows // tr,),
        in_specs=[pl.BlockSpec((n_parts, tr, cols), lambda i: (0, i, 0))] + [blk] * (len(args) - 1),
        out_specs=(blk, blk, blk, blk), name=name, compiler_params=_cp("parallel"))(*args)


MESH = pl.DeviceIdType.MESH
ANY = pl.BlockSpec(memory_space=pl.ANY)


def all_gather(block, name):
    def body(x_ref, out_ref, send_sems, recv_sems, local_sem):
        x, y, c = lax.axis_index("x"), lax.axis_index("y"), lax.axis_index("c")
        me, sibling = (x, y, c), (x, y, 1 - c)
        chips = [(1 - x, y), (x, 1 - y), (1 - x, 1 - y)]

        def slot(px, py, pc):
            return out_ref.at[4 * px + 2 * py + pc]

        def copy(k, blk, to, src=None):
            return pltpu.make_async_remote_copy(
                src_ref=slot(*blk) if src is None else src, dst_ref=slot(*blk),
                send_sem=send_sems.at[k], recv_sem=recv_sems.at[k], device_id=to, device_id_type=MESH)

        mine = pltpu.make_async_copy(x_ref, slot(*me), local_sem)
        mine.start()
        first = [copy(0, me, sibling, src=x_ref)]
        first += [copy(1 + j, me, (*chip, c), src=x_ref) for j, chip in enumerate(chips)]
        for cp in first:
            cp.start()
        passed = [copy(4 + j, (*chip, c), sibling) for j, chip in enumerate(chips)]
        for j, chip in enumerate(chips):
            copy(1 + j, (*chip, c), me).wait_recv()
            passed[j].start()
        copy(0, sibling, me).wait_recv()
        for j, chip in enumerate(chips):
            copy(4 + j, (*chip, 1 - c), me).wait_recv()
        for cp in first + passed:
            cp.wait_send()
        mine.wait()

    return pl.pallas_call(
        body, out_shape=jax.ShapeDtypeStruct((N_DEV,) + block.shape, block.dtype), in_specs=[ANY], out_specs=ANY,
        scratch_shapes=[pltpu.SemaphoreType.DMA((7,)), pltpu.SemaphoreType.DMA((7,)), pltpu.SemaphoreType.DMA],
        name=name)(block)


HBM = pl.BlockSpec(memory_space=pltpu.HBM)
SEM = pl.BlockSpec(memory_space=pltpu.SEMAPHORE)
EFFECT = pltpu.SideEffectType.DATAFLOW_SIDE_EFFECTING


def _peer_copies(src_ref, land_ref, send_sems, recv_sems, gather):
    x, y, c = lax.axis_index("x"), lax.axis_index("y"), lax.axis_index("c")
    me = 4 * x + 2 * y + c
    copies = []
    for k in range(1, N_DEV):
        px = 1 - x if k & 4 else x
        py = 1 - y if k & 2 else y
        pc = 1 - c if k & 1 else c
        copies.append(pltpu.make_async_remote_copy(
            src_ref=src_ref if gather else src_ref.at[4 * px + 2 * py + pc],
            dst_ref=land_ref.at[me] if gather else land_ref.at[k],
            send_sem=send_sems.at[k - 1], recv_sem=recv_sems.at[k - 1], device_id=(px, py, pc), device_id_type=MESH))
    return copies


def exchange_start(src, gather, name, after=None):
    land_shape = (N_DEV,) + src.shape if gather else src.shape
    extra = () if after is None else (after,)

    def body(src_ref, land_ref, *refs):
        send_sems, recv_sems, src_thru, land_thru, token = refs[len(extra):]
        for cp in _peer_copies(src_ref, land_ref, send_sems, recv_sems, gather):
            cp.start()
        token[...] = jnp.zeros_like(token)

    send_sems, recv_sems, src_thru, land_thru, token = pl.pallas_call(
        body, name=name,
        out_shape=(pltpu.SemaphoreType.DMA((N_DEV - 1,)), pltpu.SemaphoreType.DMA((N_DEV - 1,)),
                   pltpu.HBM(src.shape, src.dtype), pltpu.HBM(land_shape, src.dtype), jax.ShapeDtypeStruct((8, 128), F32)),
        in_specs=(HBM, HBM) + (ANY,) * len(extra), out_specs=(SEM, SEM, HBM, HBM, pl.BlockSpec(memory_space=pltpu.VMEM)),
        input_output_aliases={0: 2, 1: 3}, compiler_params=pltpu.CompilerParams(has_side_effects=EFFECT),
    )(pltpu.with_memory_space_constraint(src, pltpu.HBM),
      pltpu.with_memory_space_constraint(lax.empty(land_shape, src.dtype), pltpu.HBM), *extra)
    return (send_sems, recv_sems, src_thru, land_thru), token[0, 0]


def exchange_wait(handles, after, gather, name):
    send_sems, recv_sems, src_thru, land_thru = handles

    def body(src_ref, land_ref, send_sems, recv_sems, after_ref, src_dead, got_ref):
        for cp in _peer_copies(src_ref, land_ref, send_sems, recv_sems, gather):
            cp.wait_send()
            cp.wait_recv()

    return pl.pallas_call(
        body, name=name,
        out_shape=(pltpu.HBM(src_thru.shape, src_thru.dtype), pltpu.HBM(land_thru.shape, land_thru.dtype)),
        in_specs=(HBM, HBM, SEM, SEM, ANY), out_specs=(HBM, HBM), input_output_aliases={0: 0, 1: 1},
        compiler_params=pltpu.CompilerParams(has_side_effects=EFFECT),
    )(src_thru, land_thru, send_sems, recv_sems, after)[1]


def _to_pattern(a, r):
    return a.reshape(SEQ // r, r * a.shape[1])


def local_step(x, tgt, mod, get_w, put_grad, put_small, wc, wf, g_mix, bc, lg, lb, gco, gao, g_ffn, bf, g_fin):
    h1 = rms_mod_fwd(x, g_mix, mod, 0, 1, "h1_fwd")
    w_in = get_w("w_in", h1)
    proj = matmul(h1, w_in, "nt", F32, 256, D_IN, "proj_fwd")
    mix_a = conv_module_fwd(proj, wc, bc, lg, lb, gco, "conv_module_fwd")
    att, lse = attn_fwd_all(proj, "attn_fwd")
    mix_b = rms_gain_bf16(att, gao, "attn_out_norm")
    mixed = jnp.concatenate([mix_a, mix_b], axis=1)
    w_out = get_w("w_out", mixed)
    y1 = matmul(mixed, w_out, "nn", F32, 256, D_MODEL, "out_proj_fwd")
    x1, h2 = resid_rms_mod_fwd(x, y1, g_ffn, mod, 2, 3, 4, "x1_h2_fwd")
    w_up = get_w("w_up", h2)
    up0 = matmul(h2, w_up, "nt", F32, 256, D_FF, "up_fwd")
    act = ffn_act_fwd(up0, wf, bf, "ffn_act_fwd")
    w_down = get_w("w_down", act)
    y2 = matmul(act, w_down, "nn", F32, 256, D_MODEL, "down_fwd")
    loss_t, dx2, dy2, d_gfin, d_gaf = final_loss_bwd(x1, y2, tgt, g_fin, mod, 5, "loss_bwd")
    dact = matmul(dy2, w_down, "nt", F32, 256, FFN_TN, "down_bwd_x")
    dw_down = matmul(act, dy2, "tn", BF16, 256, D_MODEL, "down_bwd_w")
    dgate, dval, dbf_g, dbf_v, dwf_g, dwf_v = ffn_bwd_a(up0, dact, wf + put_grad("w_down", dw_down), bf, "ffn_bwd_a")
    dup0 = ffn_bwd_b(dgate, dval, wf, "ffn_bwd_b")
    dh2 = matmul(dup0, w_up, "nn", F32, 256, 512, "up_bwd_x")
    dw_up = matmul(dup0, h2, "tn", BF16, 512, D_MODEL, "up_bwd_w")
    dx1, d_shf, d_scf, d_gffn, dy1, d_gam = rms_mod_bwd(x1, dh2, dx2, g_ffn + put_grad("w_up", dw_up), mod, 4, y1, 2, "h2_bwd")
    dmixed = matmul(dy1, w_out, "nt", F32, 256, D_MODEL, "out_proj_bwd_x")
    dw_out = matmul(mixed, dy1, "tn", BF16, 256, D_MODEL, "out_proj_bwd_w")
    do, dd, d_gao = attn_combine_bwd(dmixed, att, gao + put_grad("w_out", dw_out), "attn_combine_bwd")
    dqkv = attn_bwd_all(proj, do, lse, dd, "attn_bwd")
    du1, d_gco, d_lg, d_lb, d_bc, d_wc = conv_module_bwd_a(proj, dmixed, wc, bc, lg, lb, gco, "conv_module_bwd_a")
    dproj_a = conv_module_bwd_b(proj, du1, wc, "conv_module_bwd_b")
    dproj = jnp.concatenate([dproj_a, dqkv[0], dqkv[1], dqkv[2]], axis=1)
    dh1 = matmul(dproj, w_in, "nn", F32, 256, D_MODEL, "proj_bwd_x")
    dx, d_shm, d_scm, d_gmix = rms_mod_bwd(x, dh1, dx1, g_mix, mod, 1, None, 0, "h1_bwd")
    dmod = jnp.concatenate([d_shm, d_scm, d_gam, d_shf, d_scf, d_gaf], axis=1)
    small = dict(g_norm_mix=d_gmix, b_conv_dw=d_bc, ln_conv_g=d_lg, ln_conv_b=d_lb, g_conv_out=d_gco, g_attn_out=d_gao,
                 g_norm_ffn=d_gffn, b_ffn_dw=jnp.concatenate([dbf_g, dbf_v], axis=1), g_final=d_gfin,
                 w_conv_dw=d_wc, w_ffn_dw=jnp.concatenate([dwf_g, dwf_v], axis=1), dmod=dmod, loss=loss_t[0:1, 0:1])
    after = put_small(small)
    dw_in = matmul(dproj, h1, "tn", BF16, 512, D_MODEL, "proj_bwd_w")
    put_grad("w_in", dw_in, after)
    return dx


def _padw(a, width):
    return jnp.pad(a, ((0, 0), (0, width - a.shape[1])))


def pack_small(t):
    wide = jnp.concatenate([_padw(t["dmod"], PACK_W), _padw(t["b_ffn_dw"], PACK_W), _padw(t["w_ffn_dw"], PACK_W),
                            _padw(t["loss"], PACK_W), jnp.zeros((2, PACK_W), F32)], axis=0)
    z512 = jnp.zeros((1, 512), F32)
    narrow = jnp.concatenate([
        t["g_norm_mix"], t["g_norm_ffn"], t["g_final"],
        jnp.concatenate([t["b_conv_dw"], t["ln_conv_g"]], axis=1),
        jnp.concatenate([t["ln_conv_b"], t["g_conv_out"]], axis=1),
        jnp.concatenate([t["g_attn_out"], z512], axis=1),
        jnp.zeros((2, 1024), F32),
        jnp.pad(t["w_conv_dw"], ((0, 1), (0, 0))).reshape(16, 1024)], axis=0)
    return jnp.concatenate([wide, narrow.reshape(4, PACK_W), jnp.zeros((4, PACK_W), F32)], axis=0)


def unpack_small(p):
    narrow = p[8:12].reshape(24, 1024)
    return dict(
        dmod=p[0:1], b_ffn_dw=p[1:2, :2 * D_FF], w_ffn_dw=p[2:5, :2 * D_FF], loss=p[5, 0],
        g_norm_mix=narrow[0:1], g_norm_ffn=narrow[1:2], g_final=narrow[2:3],
        b_conv_dw=narrow[3:4, :512], ln_conv_g=narrow[3:4, 512:], ln_conv_b=narrow[4:5, :512], g_conv_out=narrow[4:5, 512:],
        g_attn_out=narrow[5:6, :512], w_conv_dw=narrow[8:24].reshape(32, 512)[:CONV_K])


def _embed(local, width, me):
    return lax.dynamic_update_slice(jnp.zeros((local.shape[0], width), F32), local, (0, me * local.shape[1]))


def _shard(full, n_cols, me):
    return lax.dynamic_slice(full, (0, me * n_cols), (full.shape[0], n_cols))


WEIGHTS = ["w_ada", "b_ada", "g_norm_mix", "w_in", "w_conv_dw", "b_conv_dw", "ln_conv_g", "ln_conv_b", "g_conv_out",
           "g_attn_out", "w_out", "g_norm_ffn", "w_up", "w_ffn_dw", "b_ffn_dw", "w_down", "g_final"]
SMALL_REPLICATED = ["g_norm_mix", "b_conv_dw", "ln_conv_g", "ln_conv_b", "g_conv_out", "g_attn_out", "g_norm_ffn",
                    "b_ffn_dw", "g_final"]


def kernel(x, c, w_ada, b_ada, g_norm_mix, w_in, w_conv_dw, b_conv_dw, ln_conv_g, ln_conv_b, g_conv_out, g_attn_out, w_out, g_norm_ffn, w_up, w_ffn_dw, b_ffn_dw, w_down, g_final, loss_target, m_w_ada, m_b_ada, m_g_norm_mix, m_w_in, m_w_conv_dw, m_b_conv_dw, m_ln_conv_g, m_ln_conv_b, m_g_conv_out, m_g_attn_out, m_w_out, m_g_norm_ffn, m_w_up, m_w_ffn_dw, m_b_ffn_dw, m_w_down, m_g_final, v_w_ada, v_b_ada, v_g_norm_mix, v_w_in, v_w_conv_dw, v_b_conv_dw, v_ln_conv_g, v_ln_conv_b, v_g_conv_out, v_g_attn_out, v_w_out, v_g_norm_ffn, v_w_up, v_w_ffn_dw, v_b_ffn_dw, v_w_down, v_g_final):
    args = dict(locals())
    me = 4 * lax.axis_index("x") + 2 * lax.axis_index("y") + lax.axis_index("c")

    def flat(name, prefix=""):
        a = args[prefix + name]
        return a.reshape(a.shape[-2] if a.ndim > 1 else 1, a.shape[-1])

    n_in, n_up, r_out, r_down = w_in.shape[2], w_up.shape[2], w_out.shape[1], w_down.shape[1]
    n_ada, n_wc, n_wf = w_ada.shape[2], w_conv_dw.shape[2], w_ffn_dw.shape[2]
    taps_c = jnp.pad(flat("w_conv_dw").reshape(1, CONV_K * n_wc), ((0, 0), (0, 2 * D_MODEL - CONV_K * n_wc)))
    taps_f = jnp.pad(flat("w_ffn_dw").reshape(1, FFN_K * n_wf), ((0, 0), (0, 3 * D_MODEL - FFN_K * n_wf)))
    first = jnp.concatenate([c, taps_c.reshape(2, D_MODEL), taps_f.reshape(3, D_MODEL), jnp.zeros((2, D_MODEL), F32)], axis=0)
    first_all = all_gather(first, "gather_c_taps")
    c_all = first_all[:, 0, :]
    wc_full = first_all[:, 1:3, :].reshape(N_DEV, 2 * D_MODEL)[:, :CONV_K * n_wc].reshape(N_DEV, CONV_K, n_wc)
    wc_full = wc_full.transpose(1, 0, 2).reshape(CONV_K, D_CONV)
    wf_full = first_all[:, 3:6, :].reshape(N_DEV, 3 * D_MODEL)[:, :FFN_K * n_wf].reshape(N_DEV, FFN_K, n_wf)
    wf_full = wf_full.transpose(1, 0, 2).reshape(FFN_K, 2 * D_FF)
    mod_cols = ada_fwd(c_all, flat("w_ada"), _shard(flat("b_ada"), n_ada, me), "ada_fwd")
    mod_all = all_gather(mod_cols, "gather_mod")
    mod = lax.dynamic_index_in_dim(mod_all, me, axis=1, keepdims=False).reshape(N_MOD, D_MODEL)
    mod = jnp.pad(mod, ((0, 2), (0, 0)))

    def flat_t(name, prefix=""):
        return args[prefix + name][0].T

    blocks = dict(w_in=flat_t("w_in").astype(BF16), w_up=flat_t("w_up").astype(BF16),
                  rows=jnp.concatenate([flat("w_out"), flat("w_down")], axis=0).astype(BF16))
    gathers, tok = {}, None
    for name in ("w_in", "rows", "w_up"):
        src = blocks[name] if tok is None else blocks[name] + tok.astype(BF16)
        gathers[name], tok = exchange_start(src, True, f"gather_{name}_start", mod_all)
    mod = mod + tok
    slot = lax.broadcasted_iota(jnp.int32, (N_DEV, 1, 1), 0)
    full = {}

    def gathered(name, after):
        land = exchange_wait(gathers[name], after, True, f"gather_{name}_wait")
        return jnp.where(slot == me, blocks[name][None], land)

    def get_w(name, after):
        if name in ("w_out", "w_down"):
            if "rows" not in full:
                full["rows"] = gathered("rows", after)
            rows = full["rows"]
            return rows[:, :r_out, :].reshape(D_MODEL, D_MODEL) if name == "w_out" else rows[:, r_out:, :].reshape(D_FF, D_MODEL)
        return gathered(name, after).reshape(-1, D_MODEL)

    exchanges, own, kept = {}, {}, {}

    def put_grad(name, dw, after=None):
        dev_major = dw.reshape(N_DEV, -1, D_MODEL)
        own[name] = lax.dynamic_index_in_dim(dev_major, me, axis=0, keepdims=False)
        exchanges[name], token = exchange_start(dev_major, False, f"exchange_{name}_start", after)
        return token

    def put_small(small):
        kept["small_all"] = all_gather(pack_small(small), "gather_small")
        return kept["small_all"]

    grad_x = local_step(
        x[0], loss_target[0], mod, get_w, put_grad, put_small, wc_full, wf_full,
        flat("g_norm_mix"), flat("b_conv_dw"), flat("ln_conv_g"), flat("ln_conv_b"), flat("g_conv_out"),
        flat("g_attn_out"), flat("g_norm_ffn"), flat("b_ffn_dw"), flat("g_final"))
    small_all = kept["small_all"]

    out = {}

    def finish(name, tr, after):
        parts = exchange_wait(exchanges[name], after, False, f"exchange_{name}_wait")
        if name in ("w_in", "w_up"):
            res = sum_adamw(parts, own[name], flat_t(name), flat_t(name, "m_"), flat_t(name, "v_"), tr, "adamw_" + name)
            out[name] = tuple(r.T for r in res)
        else:
            res = out[name] = sum_adamw(parts, own[name], flat(name), flat(name, "m_"), flat(name, "v_"), tr, "adamw_" + name)
        return res[0]

    def packed(prefix):
        t = {n: flat(n, prefix) for n in SMALL_REPLICATED}
        t["dmod"] = flat("b_ada", prefix)
        t["loss"] = jnp.zeros((1, 1), F32)
        t["w_conv_dw"] = _embed(flat("w_conv_dw", prefix), D_CONV, me)
        t["w_ffn_dw"] = _embed(flat("w_ffn_dw", prefix), 2 * D_FF, me)
        return pack_small(t)

    res = sum_adamw(small_all, None, packed(""), packed("m_"), packed("v_"), PACK_ROWS, "adamw_small")
    after = res[0]
    res = [unpack_small(r) for r in res]
    loss = res[0]["loss"]
    for n in SMALL_REPLICATED:
        out[n] = tuple(r[n] for r in res)
    out["b_ada"] = tuple(r["dmod"] for r in res)
    out["w_conv_dw"] = tuple(_shard(r["w_conv_dw"], n_wc, me) for r in res)
    out["w_ffn_dw"] = tuple(_shard(r["w_ffn_dw"], n_wf, me) for r in res)

    dmod_cols = _shard(small_all[:, 0, :], n_ada, me)
    out["w_ada"] = ada_bwd_adamw(c_all.T, dmod_cols, flat("w_ada"), flat("w_ada", "m_"), flat("w_ada", "v_"), "adamw_w_ada")

    after = finish("w_down", r_down, out["w_ada"][0])
    after = finish("w_up", n_up // 2, after)
    after = finish("w_out", r_out, after)
    finish("w_in", n_in, after)

    result = [loss, grad_x[None]]
    for k in range(4):
        result += [out[n][k].reshape(args[n].shape) for n in WEIGHTS]
    return tuple(result)
```

```python
import functools

import jax
import jax.numpy as jnp
from jax import lax
from jax.experimental import pallas as pl
from jax.experimental.pallas import tpu as pltpu

F32 = jnp.float32
BF16 = jnp.bfloat16

N_DEV = 8
SEQ = 2048
D_MODEL = 1024
D_CONV = 512
D_ATTN = 512
HEAD_DIM = 64
CONV_K = 31
D_FF = 2816
FFN_K = 3
D_IN = 2 * D_CONV + 3 * D_ATTN
N_MOD = 6
EPS = 1e-6
ATTN_BLOCK = 128
PATTERNS = ((2048, 1), (512, 4), (128, 16))
NEG = -1e30

ADAM_LR, ADAM_B1, ADAM_B2, ADAM_EPS, ADAM_WD, ADAM_STEP = 0.001, 0.9, 0.999, 1e-08, 0.01, 10

ROWS = 256
CONV_HALO = 32
FFN_HALO = 8
FFN_TN = 1408
VMEM_LIMIT = 56 * 1024 * 1024
PACK_ROWS, PACK_W = 16, 6144


def _cp(*sem):
    return pltpu.CompilerParams(dimension_semantics=sem if sem else None, vmem_limit_bytes=VMEM_LIMIT)


def _sig(x):
    return 1.0 / (1.0 + jnp.exp(-x))


def _rsum(x):
    return jnp.sum(x, axis=0, keepdims=True)


def _mean(x):
    return jnp.mean(x, axis=-1, keepdims=True)


def _acc(ref, val, first):
    @pl.when(first)
    def _():
        ref[...] = val

    @pl.when(jnp.logical_not(first))
    def _():
        ref[...] += val


SUB = 16


def _for_chunks(fn, unroll=1):
    def step(i, carry):
        fn(pl.ds(pl.multiple_of(i * SUB, SUB), SUB))
        return carry

    lax.fori_loop(0, ROWS // SUB, step, 0, unroll=unroll)


def rms_mod_fwd(x, g, mod, sh_row, sc_row, name):
    def body(x_ref, g_ref, mod_ref, h_ref):
        xx = x_ref[...]
        r = lax.rsqrt(_mean(xx * xx) + EPS)
        h = xx * r * g_ref[...]
        h_ref[...] = (h * (1.0 + mod_ref[sc_row:sc_row + 1, :]) + mod_ref[sh_row:sh_row + 1, :]).astype(BF16)

    return pl.pallas_call(
        body, out_shape=jax.ShapeDtypeStruct((SEQ, D_MODEL), BF16), grid=(SEQ // ROWS,),
        in_specs=[_row_spec(D_MODEL), _vec_spec(D_MODEL), _vec_spec(D_MODEL, 8)],
        out_specs=_row_spec(D_MODEL), name=name, compiler_params=_cp("parallel"))(x, g, mod)


def resid_rms_mod_fwd(x, y, g, mod, ga_row, sh_row, sc_row, name):
    def body(x_ref, y_ref, g_ref, mod_ref, x1_ref, h_ref):
        x1 = x_ref[...] + mod_ref[ga_row:ga_row + 1, :] * y_ref[...]
        x1_ref[...] = x1
        r = lax.rsqrt(_mean(x1 * x1) + EPS)
        h = x1 * r * g_ref[...]
        h_ref[...] = (h * (1.0 + mod_ref[sc_row:sc_row + 1, :]) + mod_ref[sh_row:sh_row + 1, :]).astype(BF16)

    return pl.pallas_call(
        body, out_shape=(jax.ShapeDtypeStruct((SEQ, D_MODEL), F32), jax.ShapeDtypeStruct((SEQ, D_MODEL), BF16)),
        grid=(SEQ // ROWS,),
        in_specs=[_row_spec(D_MODEL), _row_spec(D_MODEL), _vec_spec(D_MODEL), _vec_spec(D_MODEL, 8)],
        out_specs=(_row_spec(D_MODEL), _row_spec(D_MODEL)), name=name, compiler_params=_cp("parallel"))(x, y, g, mod)


def final_loss_bwd(x1, y2, tgt, g, mod, ga_row, name):
    def body(x1_ref, y2_ref, t_ref, g_ref, mod_ref, loss_ref, dx2_ref, dy2_ref, dg_ref, dga_ref):
        first = pl.program_id(0) == 0
        ga = mod_ref[ga_row:ga_row + 1, :]
        y2 = y2_ref[...]
        x2 = x1_ref[...] + ga * y2
        r = lax.rsqrt(_mean(x2 * x2) + EPS)
        xn = x2 * r
        err = xn * g_ref[...] - t_ref[...]
        _acc(loss_ref, jnp.broadcast_to(0.5 * jnp.sum(_mean(err * err)), (8, 128)), first)
        dy = err * (1.0 / D_MODEL)
        _acc(dg_ref, _rsum(dy * xn), first)
        dxn = dy * g_ref[...]
        dx2 = r * (dxn - xn * _mean(dxn * xn))
        dx2_ref[...] = dx2
        dy2_ref[...] = (dx2 * ga).astype(BF16)
        _acc(dga_ref, _rsum(dx2 * y2), first)

    vec = jax.ShapeDtypeStruct((1, D_MODEL), F32)
    return pl.pallas_call(
        body,
        out_shape=(jax.ShapeDtypeStruct((8, 128), F32), jax.ShapeDtypeStruct((SEQ, D_MODEL), F32),
                   jax.ShapeDtypeStruct((SEQ, D_MODEL), BF16), vec, vec),
        grid=(SEQ // ROWS,),
        in_specs=[_row_spec(D_MODEL), _row_spec(D_MODEL), _row_spec(D_MODEL), _vec_spec(D_MODEL), _vec_spec(D_MODEL, 8)],
        out_specs=(pl.BlockSpec((8, 128), lambda i: (0, 0)), _row_spec(D_MODEL), _row_spec(D_MODEL),
                   _vec_spec(D_MODEL), _vec_spec(D_MODEL)),
        name=name, compiler_params=_cp("arbitrary"))(x1, y2, tgt, g, mod)


def rms_mod_bwd(x, dh, dres, g, mod, sc_row, y, ga_row, name):
    gated = y is not None

    def body(*refs):
        if gated:
            x_ref, dh_ref, dres_ref, g_ref, mod_ref, y_ref, dx_ref, dsh_ref, dsc_ref, dg_ref, dy_ref, dga_ref = refs
        else:
            x_ref, dh_ref, dres_ref, g_ref, mod_ref, dx_ref, dsh_ref, dsc_ref, dg_ref = refs
        first = pl.program_id(0) == 0
        xx = x_ref[...]
        dh = dh_ref[...]
        gg = g_ref[...]
        r = lax.rsqrt(_mean(xx * xx) + EPS)
        xn = xx * r
        _acc(dsh_ref, _rsum(dh), first)
        _acc(dsc_ref, _rsum(dh * (xn * gg)), first)
        dt = dh * (1.0 + mod_ref[sc_row:sc_row + 1, :])
        _acc(dg_ref, _rsum(dt * xn), first)
        dxn = dt * gg
        dx = dres_ref[...] + r * (dxn - xn * _mean(dxn * xn))
        dx_ref[...] = dx
        if gated:
            _acc(dga_ref, _rsum(dx * y_ref[...]), first)
            dy_ref[...] = (dx * mod_ref[ga_row:ga_row + 1, :]).astype(BF16)

    vec = jax.ShapeDtypeStruct((1, D_MODEL), F32)
    in_specs = [_row_spec(D_MODEL), _row_spec(D_MODEL), _row_spec(D_MODEL), _vec_spec(D_MODEL), _vec_spec(D_MODEL, 8)]
    out_shape = [jax.ShapeDtypeStruct((SEQ, D_MODEL), F32), vec, vec, vec]
    out_specs = [_row_spec(D_MODEL), _vec_spec(D_MODEL), _vec_spec(D_MODEL), _vec_spec(D_MODEL)]
    args = [x, dh, dres, g, mod]
    if gated:
        in_specs.append(_row_spec(D_MODEL))
        out_shape += [jax.ShapeDtypeStruct((SEQ, D_MODEL), BF16), vec]
        out_specs += [_row_spec(D_MODEL), _vec_spec(D_MODEL)]
        args.append(y)
    return pl.pallas_call(
        body, out_shape=tuple(out_shape), grid=(SEQ // ROWS,), in_specs=in_specs, out_specs=tuple(out_specs),
        name=name, compiler_params=_cp("arbitrary"))(*args)


def _prev_halo(halo, width, col):
    per = ROWS // halo
    return pl.BlockSpec((halo, width), lambda i: (jnp.maximum(i * per - 1, 0), col))


def _next_halo(halo, width, col):
    per = ROWS // halo
    last = SEQ // halo - 1
    return pl.BlockSpec((halo, width), lambda i: (jnp.minimum((i + 1) * per, last), col))


CONV_PAD = ROWS + CONV_HALO


def _shift_copies(sh):
    for b in range(1, 8):
        sh[b, 0:CONV_PAD - 8, :] = sh[0, pl.ds(b, CONV_PAD - 8), :]


def _tap(sh, rs_start, offset):
    return sh[offset % 8, pl.ds(pl.multiple_of(rs_start + (offset // 8) * 8, 8), SUB), :]


def _conv_module_forward(av_ref, ag_ref, avh_ref, agh_ref, wc_ref, bc_ref, lg_ref, lb_ref, sh, u1_s):
    i = pl.program_id(0)
    hv = avh_ref[...] * _sig(agh_ref[...])
    sh[0, 0:CONV_HALO, :] = jnp.where(i > 0, hv, 0.0)

    def glu(rs):
        sh[0, pl.ds(pl.multiple_of(rs.start + CONV_HALO, SUB), SUB), :] = av_ref[rs, :] * _sig(ag_ref[rs, :])

    _for_chunks(glu)
    _shift_copies(sh)

    def conv(rs):
        u1 = jnp.broadcast_to(bc_ref[...], (SUB, D_CONV))
        for j in range(CONV_K):
            u1 = u1 + wc_ref[j:j + 1, :] * _tap(sh, rs.start, CONV_HALO - (CONV_K - 1) + j)
        u1_s[rs, :] = u1

    _for_chunks(conv)
    u1 = u1_s[...]
    mu = _mean(u1)
    cen = u1 - mu
    rs = lax.rsqrt(_mean(cen * cen) + EPS)
    z = cen * rs
    ln = z * lg_ref[...] + lb_ref[...]
    s = _sig(ln)
    return z, rs, ln, s, ln * s


_CONV_SCRATCH = [pltpu.VMEM((8, CONV_PAD, D_CONV), F32), pltpu.VMEM((ROWS, D_CONV), F32)]


def conv_module_fwd(proj, wc, bc, lg, lb, gco, name):
    def body(av_ref, ag_ref, avh_ref, agh_ref, wc_ref, bc_ref, lg_ref, lb_ref, gco_ref, out_ref, sh, u1_s):
        _, _, _, _, u2 = _conv_module_forward(av_ref, ag_ref, avh_ref, agh_ref, wc_ref, bc_ref, lg_ref, lb_ref, sh, u1_s)
        rc = lax.rsqrt(_mean(u2 * u2) + EPS)
        out_ref[...] = (u2 * rc * gco_ref[...]).astype(BF16)

    v = _vec_spec(D_CONV)
    return pl.pallas_call(
        body, out_shape=jax.ShapeDtypeStruct((SEQ, D_CONV), BF16), grid=(SEQ // ROWS,),
        in_specs=[_row_spec(D_CONV, 0), _row_spec(D_CONV, 1), _prev_halo(CONV_HALO, D_CONV, 0),
                  _prev_halo(CONV_HALO, D_CONV, 1), _vec_spec(D_CONV, CONV_K), v, v, v, v],
        out_specs=_row_spec(D_CONV), scratch_shapes=list(_CONV_SCRATCH),
        name=name, compiler_params=_cp("parallel"))(proj, proj, proj, proj, wc, bc, lg, lb, gco)


def conv_module_bwd_a(proj, dmixed, wc, bc, lg, lb, gco, name):
    def body(av_ref, ag_ref, avh_ref, agh_ref, dm_ref, wc_ref, bc_ref, lg_ref, lb_ref, gco_ref,
             du1_ref, dgco_ref, dlg_ref, dlb_ref, dbc_ref, dwc_ref, sh, u1_s, acc):
        first = pl.program_id(0) == 0
        z, rs, ln, s, u2 = _conv_module_forward(av_ref, ag_ref, avh_ref, agh_ref, wc_ref, bc_ref, lg_ref, lb_ref, sh, u1_s)
        rc = lax.rsqrt(_mean(u2 * u2) + EPS)
        xn = u2 * rc
        dm = dm_ref[...]
        _acc(dgco_ref, _rsum(dm * xn), first)
        dyn = dm * gco_ref[...]
        du2 = rc * (dyn - xn * _mean(dyn * xn))
        dln = du2 * (s * (1.0 + ln * (1.0 - s)))
        _acc(dlg_ref, _rsum(dln * z), first)
        _acc(dlb_ref, _rsum(dln), first)
        dz = dln * lg_ref[...]
        du1 = rs * (dz - _mean(dz) - z * _mean(dz * z))
        du1_ref[...] = du1
        _acc(dbc_ref, _rsum(du1), first)
        acc[...] = jnp.zeros_like(acc)

        def taps(rs):
            d = du1_ref[rs, :]
            for j in range(CONV_K):
                acc[j] += d * _tap(sh, rs.start, CONV_HALO - (CONV_K - 1) + j)

        _for_chunks(taps)

        @pl.when(first)
        def _():
            dwc_ref[...] = jnp.zeros_like(dwc_ref)

        for j in range(CONV_K):
            dwc_ref[j:j + 1, :] += _rsum(acc[j])

    v = _vec_spec(D_CONV)
    vec = jax.ShapeDtypeStruct((1, D_CONV), F32)
    return pl.pallas_call(
        body,
        out_shape=(jax.ShapeDtypeStruct((SEQ, D_CONV), F32), vec, vec, vec, vec, jax.ShapeDtypeStruct((CONV_K, D_CONV), F32)),
        grid=(SEQ // ROWS,),
        in_specs=[_row_spec(D_CONV, 0), _row_spec(D_CONV, 1), _prev_halo(CONV_HALO, D_CONV, 0),
                  _prev_halo(CONV_HALO, D_CONV, 1), _row_spec(D_CONV, 0), _vec_spec(D_CONV, CONV_K), v, v, v, v],
        out_specs=(_row_spec(D_CONV), v, v, v, v, _vec_spec(D_CONV, CONV_K)),
        scratch_shapes=list(_CONV_SCRATCH) + [pltpu.VMEM((CONV_K, SUB, D_CONV), F32)],
        name=name, compiler_params=_cp("arbitrary"))(proj, proj, proj, proj, dmixed, wc, bc, lg, lb, gco)


def conv_module_bwd_b(proj, du1, wc, name):
    def body(av_ref, ag_ref, du1_ref, du1n_ref, wc_ref, out_ref, sh):
        i = pl.program_id(0)
        sh[0, 0:ROWS, :] = du1_ref[...]
        sh[0, ROWS:, :] = jnp.where(i < SEQ // ROWS - 1, du1n_ref[...], 0.0)
        _shift_copies(sh)

        def chunk(rs):
            du0 = jnp.zeros((SUB, D_CONV), F32)
            for j in range(CONV_K):
                du0 = du0 + wc_ref[j:j + 1, :] * _tap(sh, rs.start, CONV_K - 1 - j)
            sg = _sig(ag_ref[rs, :])
            out_ref[rs, 0:D_CONV] = (du0 * sg).astype(BF16)
            out_ref[rs, D_CONV:] = (du0 * av_ref[rs, :] * sg * (1.0 - sg)).astype(BF16)

        _for_chunks(chunk)

    return pl.pallas_call(
        body, out_shape=jax.ShapeDtypeStruct((SEQ, 2 * D_CONV), BF16), grid=(SEQ // ROWS,),
        in_specs=[_row_spec(D_CONV, 0), _row_spec(D_CONV, 1), _row_spec(D_CONV, 0), _next_halo(CONV_HALO, D_CONV, 0),
                  _vec_spec(D_CONV, CONV_K)],
        out_specs=_row_spec(2 * D_CONV), scratch_shapes=[pltpu.VMEM((8, CONV_PAD, D_CONV), F32)],
        name=name, compiler_params=_cp("parallel"))(proj, proj, du1, du1, wc)


def _attn_specs(sub_len, pairs):
    ng, width = 4 // pairs, 128 * pairs
    q = pl.BlockSpec((sub_len, width), lambda rho, g: (0, rho * 3 * ng + g))
    k = pl.BlockSpec((sub_len, width), lambda rho, g: (0, rho * 3 * ng + ng + g))
    v = pl.BlockSpec((sub_len, width), lambda rho, g: (0, rho * 3 * ng + 2 * ng + g))
    o = pl.BlockSpec((sub_len, width), lambda rho, g: (0, rho * ng + g))
    return q, k, v, o


ATTN_PAIRS = {1: 1, 4: 1, 16: 4}


def _attn_block(q_ref, k_ref, v_ref, n, hs, win):
    q0 = pl.multiple_of(n * ATTN_BLOCK, ATTN_BLOCK)
    k0 = pl.multiple_of(jnp.maximum(n - 1, 0) * ATTN_BLOCK, ATTN_BLOCK)
    qb = q_ref[pl.ds(q0, ATTN_BLOCK), hs]
    kw = k_ref[pl.ds(k0, win), hs]
    vw = v_ref[pl.ds(k0, win), hs]
    s = lax.dot_general(qb, kw, (((1,), (1,)), ((), ())), preferred_element_type=F32) * (HEAD_DIM ** -0.5)
    dist = (q0 - k0) + lax.broadcasted_iota(jnp.int32, (ATTN_BLOCK, win), 0) \
        - lax.broadcasted_iota(jnp.int32, (ATTN_BLOCK, win), 1)
    s = jnp.where((dist >= 0) & (dist <= ATTN_BLOCK), s, NEG)
    return q0, k0, qb, kw, vw, s


def attn_fwd(qkv_r, sub_len, r, name):
    nb = sub_len // ATTN_BLOCK
    win = 2 * ATTN_BLOCK if nb > 1 else ATTN_BLOCK
    pairs = ATTN_PAIRS[r]

    def body(q_ref, k_ref, v_ref, o_ref, l_ref):
        def block(n, carry):
            for h in range(2 * pairs):
                hs = slice(h * HEAD_DIM, (h + 1) * HEAD_DIM)
                q0, _, _, _, vw, s = _attn_block(q_ref, k_ref, v_ref, n, hs, win)
                m = jnp.max(s, axis=1, keepdims=True)
                p = jnp.exp(s - m)
                den = jnp.sum(p, axis=1, keepdims=True)
                o = jnp.dot(p.astype(BF16), vw, preferred_element_type=F32) / den
                o_ref[pl.ds(q0, ATTN_BLOCK), hs] = o
                l_ref[pl.ds(q0, ATTN_BLOCK), hs] = jnp.broadcast_to(m + jnp.log(den), (ATTN_BLOCK, HEAD_DIM))
            return carry

        lax.fori_loop(0, nb, block, 0, unroll=min(nb, 2))

    q, k, v, o = _attn_specs(sub_len, pairs)
    shp = jax.ShapeDtypeStruct((sub_len, r * D_ATTN), F32)
    return pl.pallas_call(
        body, out_shape=(shp, shp), grid=(r, 4 // pairs), in_specs=[q, k, v], out_specs=(o, o),
        name=name, compiler_params=_cp("parallel", "parallel"))(qkv_r, qkv_r, qkv_r)


def attn_bwd(qkv_r, do_r, lse_r, dd_r, sub_len, r, name):
    nb = sub_len // ATTN_BLOCK
    win = 2 * ATTN_BLOCK if nb > 1 else ATTN_BLOCK
    pairs = ATTN_PAIRS[r]

    def body(q_ref, k_ref, v_ref, do_ref, l_ref, dd_ref, dq_ref, dk_ref, dv_ref):
        dk_ref[...] = jnp.zeros_like(dk_ref)
        dv_ref[...] = jnp.zeros_like(dv_ref)

        def block(n, carry):
            for h in range(2 * pairs):
                hs = slice(h * HEAD_DIM, (h + 1) * HEAD_DIM)
                h1 = slice(h * HEAD_DIM, h * HEAD_DIM + 1)
                q0, k0, qb, kw, vw, s = _attn_block(q_ref, k_ref, v_ref, n, hs, win)
                dob = do_ref[pl.ds(q0, ATTN_BLOCK), hs]
                p = jnp.exp(s - l_ref[pl.ds(q0, ATTN_BLOCK), h1])
                dp = lax.dot_general(dob, vw, (((1,), (1,)), ((), ())), preferred_element_type=F32)
                ds = (p * (dp - dd_ref[pl.ds(q0, ATTN_BLOCK), h1]) * (HEAD_DIM ** -0.5)).astype(BF16)
                dq_ref[pl.ds(q0, ATTN_BLOCK), hs] = jnp.dot(ds, kw, preferred_element_type=F32)
                dk_ref[pl.ds(k0, win), hs] += lax.dot_general(ds, qb, (((0,), (0,)), ((), ())), preferred_element_type=F32)
                dv_ref[pl.ds(k0, win), hs] += lax.dot_general(p.astype(BF16), dob, (((0,), (0,)), ((), ())),
                                                              preferred_element_type=F32)
            return carry

        lax.fori_loop(0, nb, block, 0)

    q, k, v, o = _attn_specs(sub_len, pairs)
    shp = jax.ShapeDtypeStruct((sub_len, r * D_ATTN), F32)
    return pl.pallas_call(
        body, out_shape=(shp, shp, shp), grid=(r, 4 // pairs), in_specs=[q, k, v, o, o, o], out_specs=(o, o, o),
        name=name, compiler_params=_cp("parallel", "parallel"))(qkv_r, qkv_r, qkv_r, do_r, lse_r, dd_r)


def _rows(start, size, r):
    return pl.ds(start, size) if r == 1 else pl.ds(start, size, stride=r)


def _attn_unit(q_ref, k_ref, v_ref, r, rho, n, nb):
    win = 2 * ATTN_BLOCK if nb > 1 else ATTN_BLOCK
    if isinstance(n, int):
        kb = max(n - 1, 0)
        q_rows = _rows(rho + r * ATTN_BLOCK * n, ATTN_BLOCK, r)
        k_rows = _rows(rho + r * ATTN_BLOCK * kb, win, r)
    else:
        kb = jnp.maximum(n - 1, 0)
        q_rows = pl.ds(pl.multiple_of(n * ATTN_BLOCK, ATTN_BLOCK), ATTN_BLOCK)
        k_rows = pl.ds(pl.multiple_of(kb * ATTN_BLOCK, ATTN_BLOCK), win)
    q2 = q_ref[q_rows, :].astype(BF16)
    k2 = k_ref[k_rows, :].astype(BF16)
    v2 = v_ref[k_rows, :].astype(BF16)
    dist = (n - kb) * ATTN_BLOCK + lax.broadcasted_iota(jnp.int32, (ATTN_BLOCK, win), 0) \
        - lax.broadcasted_iota(jnp.int32, (ATTN_BLOCK, win), 1)
    valid = (dist >= 0) & (dist <= ATTN_BLOCK)

    def score(h):
        hs = slice(h * HEAD_DIM, (h + 1) * HEAD_DIM)
        s = lax.dot_general(q2[:, hs], k2[:, hs], (((1,), (1,)), ((), ())), preferred_element_type=F32)
        return jnp.where(valid, s * (HEAD_DIM ** -0.5), NEG), q2[:, hs], k2[:, hs], v2[:, hs]

    return q_rows, k_rows, score


def _attn_units(r, nb, unit):
    if r == 1:
        def two(i, carry):
            unit(0, 2 * i)
            unit(0, 2 * i + 1)
            return carry
        lax.fori_loop(0, nb // 2, two, 0)
    else:
        for rho in range(r):
            for n in range(nb):
                unit(rho, n)


def attn_fwd_all(proj, name):
    def body(q_ref, k_ref, v_ref, att_ref, lse_ref):
        for idx, (sub_len, r) in enumerate(PATTERNS):
            nb = sub_len // ATTN_BLOCK

            def unit(rho, n, r=r, nb=nb, idx=idx):
                q_rows, _, score = _attn_unit(q_ref, k_ref, v_ref, r, rho, n, nb)
                outs, lses = [], []
                for h in range(2):
                    s, _, _, vw = score(h)
                    m = jnp.max(s, axis=1, keepdims=True)
                    p = jnp.exp(s - m)
                    den = jnp.sum(p, axis=1, keepdims=True)
                    outs.append(jnp.dot(p.astype(BF16), vw, preferred_element_type=F32) / den)
                    lses.append(jnp.broadcast_to(m + jnp.log(den), (ATTN_BLOCK, HEAD_DIM)))
                o = jnp.concatenate(outs, axis=1)
                lse = jnp.concatenate(lses, axis=1)
                if idx > 0:
                    old = lse_ref[q_rows, :]
                    top = jnp.maximum(old, lse)
                    new = top + jnp.log(jnp.exp(old - top) + jnp.exp(lse - top))
                    o = att_ref[q_rows, :] * jnp.exp(old - new) + o * jnp.exp(lse - new)
                    lse = new
                att_ref[q_rows, :] = o
                lse_ref[q_rows, :] = lse

            _attn_units(r, nb, unit)

    blk = lambda first: pl.BlockSpec((SEQ, 128), lambda g: (0, first + g))
    shp = jax.ShapeDtypeStruct((SEQ, D_ATTN), F32)
    return pl.pallas_call(
        body, out_shape=(shp, shp), grid=(4,), in_specs=[blk(8), blk(12), blk(16)], out_specs=(blk(0), blk(0)),
        name=name, compiler_params=_cp("parallel"))(proj, proj, proj)


def attn_bwd_all(proj, do, lse, dd, name):
    def body(q_ref, k_ref, v_ref, do_ref, l_ref, dd_ref, out_ref, dq_s, dk_s, dv_s):
        dq_s[...] = jnp.zeros_like(dq_s)
        dk_s[...] = jnp.zeros_like(dk_s)
        dv_s[...] = jnp.zeros_like(dv_s)
        for sub_len, r in PATTERNS:
            nb = sub_len // ATTN_BLOCK

            def unit(rho, n, r=r, nb=nb):
                q_rows, k_rows, score = _attn_unit(q_ref, k_ref, v_ref, r, rho, n, nb)
                do2 = do_ref[q_rows, :].astype(BF16)
                l2 = l_ref[q_rows, :]
                d2 = dd_ref[q_rows, :]
                dq, dk, dv = [], [], []
                for h in range(2):
                    hs = slice(h * HEAD_DIM, (h + 1) * HEAD_DIM)
                    h1 = slice(h * HEAD_DIM, h * HEAD_DIM + 1)
                    s, qb, kw, vw = score(h)
                    dob = do2[:, hs]
                    p = jnp.exp(s - l2[:, h1])
                    dp = lax.dot_general(dob, vw, (((1,), (1,)), ((), ())), preferred_element_type=F32)
                    ds = (p * (dp - d2[:, h1]) * (HEAD_DIM ** -0.5)).astype(BF16)
                    dq.append(jnp.dot(ds, kw, preferred_element_type=F32))
                    dk.append(lax.dot_general(ds, qb, (((0,), (0,)), ((), ())), preferred_element_type=F32))
                    dv.append(lax.dot_general(p.astype(BF16), dob, (((0,), (0,)), ((), ())), preferred_element_type=F32))
                dq_s[q_rows, :] += jnp.concatenate(dq, axis=1)
                dk_s[k_rows, :] += jnp.concatenate(dk, axis=1)
                dv_s[k_rows, :] += jnp.concatenate(dv, axis=1)

            _attn_units(r, nb, unit)
        out_ref[0] = dq_s[...].astype(BF16)
        out_ref[1] = dk_s[...].astype(BF16)
        out_ref[2] = dv_s[...].astype(BF16)

    blk = lambda first: pl.BlockSpec((SEQ, 128), lambda g: (0, first + g))
    acc = pltpu.VMEM((SEQ, 128), F32)
    return pl.pallas_call(
        body, out_shape=jax.ShapeDtypeStruct((3, SEQ, D_ATTN), BF16), grid=(4,),
        in_specs=[blk(8), blk(12), blk(16), blk(0), blk(0), blk(0)],
        out_specs=pl.BlockSpec((3, SEQ, 128), lambda g: (0, 0, g)), scratch_shapes=[acc, acc, acc],
        name=name, compiler_params=_cp("parallel"))(proj, proj, proj, do, lse, dd)


def rms_gain_bf16(a, g, name):
    def body(a_ref, g_ref, o_ref):
        aa = a_ref[...]
        o_ref[...] = (aa * lax.rsqrt(_mean(aa * aa) + EPS) * g_ref[...]).astype(BF16)

    w = a.shape[1]
    return pl.pallas_call(
        body, out_shape=jax.ShapeDtypeStruct(a.shape, BF16), grid=(SEQ // ROWS,), in_specs=[_row_spec(w), _vec_spec(w)],
        out_specs=_row_spec(w), name=name, compiler_params=_cp("parallel"))(a, g)


def attn_combine_fwd(outs, lses, gao, name):
    def body(o1, o2, o3, l1, l2, l3, g_ref, att_ref, lse_ref, mix_ref):
        a1, a2, a3 = l1[...], l2[...], l3[...]
        m = jnp.maximum(jnp.maximum(a1, a2), a3)
        w1, w2, w3 = jnp.exp(a1 - m), jnp.exp(a2 - m), jnp.exp(a3 - m)
        den = w1 + w2 + w3
        att = (w1 * o1[...] + w2 * o2[...] + w3 * o3[...]) / den
        att_ref[...] = att
        lse_ref[...] = m + jnp.log(den)
        mix_ref[...] = (att * lax.rsqrt(_mean(att * att) + EPS) * g_ref[...]).astype(BF16)

    rs = _row_spec(D_ATTN)
    f = jax.ShapeDtypeStruct((SEQ, D_ATTN), F32)
    return pl.pallas_call(
        body, out_shape=(f, f, jax.ShapeDtypeStruct((SEQ, D_ATTN), BF16)), grid=(SEQ // ROWS,),
        in_specs=[rs] * 6 + [_vec_spec(D_ATTN)], out_specs=(rs, rs, rs),
        name=name, compiler_params=_cp("parallel"))(*outs, *lses, gao)


def attn_combine_bwd(dmixed, att, gao, name):
    def body(dm_ref, att_ref, g_ref, do_ref, dd_ref, dg_ref):
        first = pl.program_id(0) == 0
        att = att_ref[...]
        r = lax.rsqrt(_mean(att * att) + EPS)
        xn = att * r
        dm = dm_ref[...]
        _acc(dg_ref, _rsum(dm * xn), first)
        dyn = dm * g_ref[...]
        do = r * (dyn - xn * _mean(dyn * xn))
        do_ref[...] = do
        same_head = (jnp.right_shift(lax.broadcasted_iota(jnp.int32, (D_ATTN, D_ATTN), 0), 6)
                     == jnp.right_shift(lax.broadcasted_iota(jnp.int32, (D_ATTN, D_ATTN), 1), 6)).astype(F32)
        dd_ref[...] = jnp.dot(do * att, same_head, preferred_element_type=F32, precision=lax.Precision.HIGHEST)

    rs = _row_spec(D_ATTN)
    return pl.pallas_call(
        body,
        out_shape=(jax.ShapeDtypeStruct((SEQ, D_ATTN), F32), jax.ShapeDtypeStruct((SEQ, D_ATTN), F32),
                   jax.ShapeDtypeStruct((1, D_ATTN), F32)),
        grid=(SEQ // ROWS,), in_specs=[_row_spec(D_ATTN, 1), rs, _vec_spec(D_ATTN)],
        out_specs=(rs, rs, _vec_spec(D_ATTN)), name=name, compiler_params=_cp("arbitrary"))(dmixed, att, gao)


def sum3_bf16(a, b, c, name):
    def body(a_ref, b_ref, c_ref, o_ref):
        o_ref[...] = (a_ref[...] + b_ref[...] + c_ref[...]).astype(BF16)

    w = a.shape[1]
    rs = _row_spec(w)
    return pl.pallas_call(
        body, out_shape=jax.ShapeDtypeStruct(a.shape, BF16), grid=(SEQ // ROWS,), in_specs=[rs, rs, rs], out_specs=rs,
        name=name, compiler_params=_cp("parallel"))(a, b, c)


N_FT = D_FF // FFN_TN


def _ffn_specs():
    per = ROWS // FFN_HALO
    cur_g = pl.BlockSpec((ROWS, FFN_TN), lambda j, i: (i, j))
    cur_v = pl.BlockSpec((ROWS, FFN_TN), lambda j, i: (i, j + N_FT))
    halo_g = pl.BlockSpec((FFN_HALO, FFN_TN), lambda j, i: (jnp.maximum(i * per - 1, 0), j))
    halo_v = pl.BlockSpec((FFN_HALO, FFN_TN), lambda j, i: (jnp.maximum(i * per - 1, 0), j + N_FT))
    w_g = pl.BlockSpec((FFN_K, FFN_TN), lambda j, i: (0, j))
    w_v = pl.BlockSpec((FFN_K, FFN_TN), lambda j, i: (0, j + N_FT))
    b_g = pl.BlockSpec((1, FFN_TN), lambda j, i: (0, j))
    b_v = pl.BlockSpec((1, FFN_TN), lambda j, i: (0, j + N_FT))
    return [cur_g, cur_v, halo_g, halo_v, w_g, w_v, b_g, b_v]


def matmul(a, b, kind, out_dtype, tm, tn, name):
    if kind == "nn":
        (m, k), n = a.shape, b.shape[1]
        a_spec = pl.BlockSpec((tm, k), lambda j, i: (i, 0))
        b_spec = pl.BlockSpec((k, tn), lambda j, i: (0, j))
        dims = (((1,), (0,)), ((), ()))
    elif kind == "nt":
        (m, k), n = a.shape, b.shape[0]
        a_spec = pl.BlockSpec((tm, k), lambda j, i: (i, 0))
        b_spec = pl.BlockSpec((tn, k), lambda j, i: (j, 0))
        dims = (((1,), (1,)), ((), ()))
    else:
        (k, m), n = a.shape, b.shape[1]
        a_spec = pl.BlockSpec((k, tm), lambda j, i: (0, i))
        b_spec = pl.BlockSpec((k, tn), lambda j, i: (0, j))
        dims = (((0,), (0,)), ((), ()))
    assert m % tm == 0 and n % tn == 0, (name, m, n, tm, tn)

    def body(a_ref, b_ref, o_ref):
        o_ref[...] = lax.dot_general(a_ref[...], b_ref[...], dims, preferred_element_type=F32).astype(o_ref.dtype)

    return pl.pallas_call(
        body, out_shape=jax.ShapeDtypeStruct((m, n), out_dtype), grid=(n // tn, m // tm),
        in_specs=[a_spec, b_spec], out_specs=pl.BlockSpec((tm, tn), lambda j, i: (i, j)),
        name=name, compiler_params=_cp("parallel", "parallel"))(a, b)


def _row_spec(width, col=0):
    return pl.BlockSpec((ROWS, width), lambda i: (i, col))


def _vec_spec(width, rows=1):
    return pl.BlockSpec((rows, width), lambda i: (0, 0))


def _ffn_shifted(cur_ref, halo_ref, pad, s1, s2):
    i = pl.program_id(1)
    pad[0:FFN_HALO, :] = jnp.where(i > 0, halo_ref[...], 0.0)
    pad[FFN_HALO:, :] = cur_ref[0:FFN_HALO, :]
    for k, dst in ((1, s1), (2, s2)):
        dst[0:FFN_HALO, :] = pad[pl.ds(FFN_HALO - k, FFN_HALO), :]
        dst[FFN_HALO:, :] = cur_ref[pl.ds(FFN_HALO - k, ROWS - FFN_HALO), :]


def _ffn_conv(rs, cur_ref, s1, s2, w_ref, b_ref):
    return b_ref[...] + w_ref[0:1, :] * s2[rs, :] + w_ref[1:2, :] * s1[rs, :] + w_ref[2:3, :] * cur_ref[rs, :]


def ffn_act_fwd(up0, wf, bf, name):
    def body(g_ref, v_ref, gh_ref, vh_ref, wg_ref, wv_ref, bg_ref, bv_ref, act_ref, pad, g1, g2, v1, v2):
        _ffn_shifted(g_ref, gh_ref, pad, g1, g2)
        _ffn_shifted(v_ref, vh_ref, pad, v1, v2)

        def chunk(rs):
            gate = _ffn_conv(rs, g_ref, g1, g2, wg_ref, bg_ref)
            val = _ffn_conv(rs, v_ref, v1, v2, wv_ref, bv_ref)
            act_ref[rs, :] = (gate * _sig(gate) * val).astype(BF16)

        _for_chunks(chunk)

    tile = pltpu.VMEM((ROWS, FFN_TN), F32)
    return pl.pallas_call(
        body, out_shape=jax.ShapeDtypeStruct((SEQ, D_FF), BF16), grid=(N_FT, SEQ // ROWS),
        in_specs=_ffn_specs(), out_specs=pl.BlockSpec((ROWS, FFN_TN), lambda j, i: (i, j)),
        scratch_shapes=[pltpu.VMEM((2 * FFN_HALO, FFN_TN), F32), tile, tile, tile, tile],
        name=name, compiler_params=_cp("parallel", "parallel"))(up0, up0, up0, up0, wf, wf, bf, bf)


def ffn_bwd_a(up0, dact, wf, bf, name):
    def body(g_ref, v_ref, gh_ref, vh_ref, wg_ref, wv_ref, bg_ref, bv_ref, da_ref,
             dg_ref, dv_ref, dbg_ref, dbv_ref, dwg_ref, dwv_ref, pad, g1, g2, v1, v2, acc):
        first = pl.program_id(1) == 0
        _ffn_shifted(g_ref, gh_ref, pad, g1, g2)
        _ffn_shifted(v_ref, vh_ref, pad, v1, v2)
        acc[...] = jnp.zeros_like(acc)

        def chunk(rs):
            gate = _ffn_conv(rs, g_ref, g1, g2, wg_ref, bg_ref)
            val = _ffn_conv(rs, v_ref, v1, v2, wv_ref, bv_ref)
            s = _sig(gate)
            da = da_ref[rs, :]
            dgate = da * val * (s * (1.0 + gate * (1.0 - s)))
            dval = da * (gate * s)
            dg_ref[rs, :] = dgate
            dv_ref[rs, :] = dval
            acc[0] += dgate
            acc[1] += dval
            for t, (sg, sv) in enumerate(((g2, v2), (g1, v1), (g_ref, v_ref))):
                acc[2 + t] += dgate * sg[rs, :]
                acc[5 + t] += dval * sv[rs, :]

        _for_chunks(chunk)
        _acc(dbg_ref, _rsum(acc[0]), first)
        _acc(dbv_ref, _rsum(acc[1]), first)
        _acc(dwg_ref, jnp.concatenate([_rsum(acc[2 + t]) for t in range(FFN_K)], axis=0), first)
        _acc(dwv_ref, jnp.concatenate([_rsum(acc[5 + t]) for t in range(FFN_K)], axis=0), first)

    tile = pltpu.VMEM((ROWS, FFN_TN), F32)
    full = jax.ShapeDtypeStruct((SEQ, D_FF), F32)
    vec = jax.ShapeDtypeStruct((1, D_FF), F32)
    taps = jax.ShapeDtypeStruct((FFN_K, D_FF), F32)
    cur = pl.BlockSpec((ROWS, FFN_TN), lambda j, i: (i, j))
    vs = pl.BlockSpec((1, FFN_TN), lambda j, i: (0, j))
    ts = pl.BlockSpec((FFN_K, FFN_TN), lambda j, i: (0, j))
    return pl.pallas_call(
        body, out_shape=(full, full, vec, vec, taps, taps), grid=(N_FT, SEQ // ROWS),
        in_specs=_ffn_specs() + [cur], out_specs=(cur, cur, vs, vs, ts, ts),
        scratch_shapes=[pltpu.VMEM((2 * FFN_HALO, FFN_TN), F32), tile, tile, tile, tile,
                        pltpu.VMEM((2 + 2 * FFN_K, SUB, FFN_TN), F32)],
        name=name, compiler_params=_cp("parallel", "arbitrary"))(up0, up0, up0, up0, wf, wf, bf, bf, dact)


def ffn_bwd_b(dgate, dval, wf, name):
    per = ROWS // FFN_HALO
    last = SEQ // FFN_HALO - 1

    def body(g_ref, v_ref, gn_ref, vn_ref, w_ref, out_ref, pad, s1, s2):
        j = pl.program_id(0)
        i = pl.program_id(1)

        def run(cur_ref, nxt_ref):
            pad[0:FFN_HALO, :] = cur_ref[ROWS - FFN_HALO:, :]
            pad[FFN_HALO:, :] = jnp.where(i < SEQ // ROWS - 1, nxt_ref[...], 0.0)
            for k, dst in ((1, s1), (2, s2)):
                dst[0:ROWS - FFN_HALO, :] = cur_ref[pl.ds(k, ROWS - FFN_HALO), :]
                dst[ROWS - FFN_HALO:, :] = pad[pl.ds(k, FFN_HALO), :]

            def chunk(rs):
                out_ref[rs, :] = (w_ref[2:3, :] * cur_ref[rs, :] + w_ref[1:2, :] * s1[rs, :]
                                  + w_ref[0:1, :] * s2[rs, :]).astype(BF16)

            _for_chunks(chunk)

        @pl.when(j < N_FT)
        def _():
            run(g_ref, gn_ref)

        @pl.when(j >= N_FT)
        def _():
            run(v_ref, vn_ref)

    cur_g = pl.BlockSpec((ROWS, FFN_TN), lambda j, i: (i, jnp.minimum(j, N_FT - 1)))
    cur_v = pl.BlockSpec((ROWS, FFN_TN), lambda j, i: (i, jnp.maximum(j - N_FT, 0)))
    nxt_g = pl.BlockSpec((FFN_HALO, FFN_TN), lambda j, i: (jnp.minimum((i + 1) * per, last), jnp.minimum(j, N_FT - 1)))
    nxt_v = pl.BlockSpec((FFN_HALO, FFN_TN), lambda j, i: (jnp.minimum((i + 1) * per, last), jnp.maximum(j - N_FT, 0)))
    tile = pltpu.VMEM((ROWS, FFN_TN), F32)
    return pl.pallas_call(
        body, out_shape=jax.ShapeDtypeStruct((SEQ, 2 * D_FF), BF16), grid=(2 * N_FT, SEQ // ROWS),
        in_specs=[cur_g, cur_v, nxt_g, nxt_v, pl.BlockSpec((FFN_K, FFN_TN), lambda j, i: (0, j))],
        out_specs=pl.BlockSpec((ROWS, FFN_TN), lambda j, i: (i, j)),
        scratch_shapes=[pltpu.VMEM((2 * FFN_HALO, FFN_TN), F32), tile, tile],
        name=name, compiler_params=_cp("parallel", "parallel"))(dgate, dval, dgate, dval, wf)


def ada_fwd(c_all, w_ada, b_cols, name):
    def body(c_ref, w_ref, b_ref, o_ref):
        cc = c_ref[...]
        sc = (cc * _sig(cc)).astype(BF16)
        o_ref[...] = jnp.dot(sc, w_ref[...].astype(BF16), preferred_element_type=F32) + b_ref[...]

    return pl.pallas_call(body, out_shape=jax.ShapeDtypeStruct((N_DEV, w_ada.shape[1]), F32), name=name,
                          compiler_params=_cp())(c_all, w_ada, b_cols)


def _adam(w, g, m, v):
    m = ADAM_B1 * m + (1.0 - ADAM_B1) * g
    v = ADAM_B2 * v + (1.0 - ADAM_B2) * (g * g)
    m_hat = m / (1.0 - ADAM_B1 ** ADAM_STEP)
    v_hat = v / (1.0 - ADAM_B2 ** ADAM_STEP)
    delta = -ADAM_LR * (m_hat / (jnp.sqrt(v_hat) + ADAM_EPS) + ADAM_WD * w)
    return delta, m, v


def ada_bwd_adamw(c_all_t, dmod_cols, w, m, v, name):
    rows, cols = w.shape
    tr = 256

    def body(ct_ref, dm_ref, w_ref, m_ref, v_ref, g_ref, d_ref, nm_ref, nv_ref):
        ct = ct_ref[...]
        sc = ct * _sig(ct)
        g = sc[:, 0:1] * dm_ref[0:1, :]
        for b in range(1, N_DEV):
            g = g + sc[:, b:b + 1] * dm_ref[b:b + 1, :]
        g_ref[...] = g
        d_ref[...], nm_ref[...], nv_ref[...] = _adam(w_ref[...], g, m_ref[...], v_ref[...])

    blk = pl.BlockSpec((tr, cols), lambda i: (i, 0))
    shp = jax.ShapeDtypeStruct((rows, cols), F32)
    return pl.pallas_call(
        body, out_shape=(shp, shp, shp, shp), grid=(rows // tr,),
        in_specs=[pl.BlockSpec((tr, N_DEV), lambda i: (i, 0)), pl.BlockSpec((N_DEV, cols), lambda i: (0, 0)), blk, blk, blk],
        out_specs=(blk, blk, blk, blk), name=name, compiler_params=_cp("parallel"))(c_all_t, dmod_cols, w, m, v)


def sum_adamw(parts, own, w, m, v, tr, name):
    n_parts, rows, cols = parts.shape

    def body(*refs):
        if own is None:
            p_ref, w_ref, m_ref, v_ref, g_ref, d_ref, nm_ref, nv_ref = refs
            g = p_ref[0].astype(F32)
        else:
            p_ref, own_ref, w_ref, m_ref, v_ref, g_ref, d_ref, nm_ref, nv_ref = refs
            g = own_ref[...].astype(F32)
        for k in range(1, n_parts):
            g = g + p_ref[k].astype(F32)
        g_ref[...] = g
        d_ref[...], nm_ref[...], nv_ref[...] = _adam(w_ref[...], g, m_ref[...], v_ref[...])

    blk = pl.BlockSpec((tr, cols), lambda i: (i, 0))
    shp = jax.ShapeDtypeStruct((rows, cols), F32)
    args = [parts] + ([] if own is None else [own]) + [w, m, v]
    return pl.pallas_call(
        body, out_shape=(shp, shp, shp, shp), grid=(rows // tr,),
        in_specs=[pl.BlockSpec((n_parts, tr, cols), lambda i: (0, i, 0))] + [blk] * (len(args) - 1),
        out_specs=(blk, blk, blk, blk), name=name, compiler_params=_cp("parallel"))(*args)


MESH = pl.DeviceIdType.MESH
ANY = pl.BlockSpec(memory_space=pl.ANY)


def all_gather(block, name, after=None):
    extra = () if after is None else (after,)

    def body(x_ref, *refs):
        out_ref, send_sems, recv_sems, local_sem = refs[len(extra):]
        x, y, c = lax.axis_index("x"), lax.axis_index("y"), lax.axis_index("c")
        me, sibling = (x, y, c), (x, y, 1 - c)
        chips = [(1 - x, y), (x, 1 - y), (1 - x, 1 - y)]

        def slot(px, py, pc):
            return out_ref.at[4 * px + 2 * py + pc]

        def copy(k, blk, to, src=None):
            return pltpu.make_async_remote_copy(
                src_ref=slot(*blk) if src is None else src, dst_ref=slot(*blk),
                send_sem=send_sems.at[k], recv_sem=recv_sems.at[k], device_id=to, device_id_type=MESH)

        mine = pltpu.make_async_copy(x_ref, slot(*me), local_sem)
        mine.start()
        first = [copy(0, me, sibling, src=x_ref)]
        first += [copy(1 + j, me, (*chip, c), src=x_ref) for j, chip in enumerate(chips)]
        for cp in first:
            cp.start()
        passed = [copy(4 + j, (*chip, c), sibling) for j, chip in enumerate(chips)]
        for j, chip in enumerate(chips):
            copy(1 + j, (*chip, c), me).wait_recv()
            passed[j].start()
        copy(0, sibling, me).wait_recv()
        for j, chip in enumerate(chips):
            copy(4 + j, (*chip, 1 - c), me).wait_recv()
        for cp in first + passed:
            cp.wait_send()
        mine.wait()

    return pl.pallas_call(
        body, out_shape=jax.ShapeDtypeStruct((N_DEV,) + block.shape, block.dtype), in_specs=[ANY] * (1 + len(extra)), out_specs=ANY,
        scratch_shapes=[pltpu.SemaphoreType.DMA((7,)), pltpu.SemaphoreType.DMA((7,)), pltpu.SemaphoreType.DMA],
        name=name)(block, *extra)


HBM = pl.BlockSpec(memory_space=pltpu.HBM)
SEM = pl.BlockSpec(memory_space=pltpu.SEMAPHORE)
EFFECT = pltpu.SideEffectType.DATAFLOW_SIDE_EFFECTING


def _peer_copies(src_ref, land_ref, send_sems, recv_sems, gather):
    x, y, c = lax.axis_index("x"), lax.axis_index("y"), lax.axis_index("c")
    me = 4 * x + 2 * y + c
    copies = []
    for k in range(1, N_DEV):
        px = 1 - x if k & 4 else x
        py = 1 - y if k & 2 else y
        pc = 1 - c if k & 1 else c
        copies.append(pltpu.make_async_remote_copy(
            src_ref=src_ref if gather else src_ref.at[4 * px + 2 * py + pc],
            dst_ref=land_ref.at[me] if gather else land_ref.at[k],
            send_sem=send_sems.at[k - 1], recv_sem=recv_sems.at[k - 1], device_id=(px, py, pc), device_id_type=MESH))
    return copies


def exchange_start(src, gather, name, after=None):
    land_shape = (N_DEV,) + src.shape if gather else src.shape
    extra = () if after is None else (after,)

    def body(src_ref, land_ref, *refs):
        send_sems, recv_sems, src_thru, land_thru, token = refs[len(extra):]
        for cp in _peer_copies(src_ref, land_ref, send_sems, recv_sems, gather):
            cp.start()
        token[...] = jnp.zeros_like(token)

    send_sems, recv_sems, src_thru, land_thru, token = pl.pallas_call(
        body, name=name,
        out_shape=(pltpu.SemaphoreType.DMA((N_DEV - 1,)), pltpu.SemaphoreType.DMA((N_DEV - 1,)),
                   pltpu.HBM(src.shape, src.dtype), pltpu.HBM(land_shape, src.dtype), jax.ShapeDtypeStruct((8, 128), F32)),
        in_specs=(HBM, HBM) + (ANY,) * len(extra), out_specs=(SEM, SEM, HBM, HBM, pl.BlockSpec(memory_space=pltpu.VMEM)),
        input_output_aliases={0: 2, 1: 3}, compiler_params=pltpu.CompilerParams(has_side_effects=EFFECT),
    )(pltpu.with_memory_space_constraint(src, pltpu.HBM),
      pltpu.with_memory_space_constraint(lax.empty(land_shape, src.dtype), pltpu.HBM), *extra)
    return (send_sems, recv_sems, src_thru, land_thru), token[0, 0]


def exchange_wait(handles, after, gather, name):
    send_sems, recv_sems, src_thru, land_thru = handles

    def body(src_ref, land_ref, send_sems, recv_sems, after_ref, src_dead, got_ref):
        for cp in _peer_copies(src_ref, land_ref, send_sems, recv_sems, gather):
            cp.wait_send()
            cp.wait_recv()

    return pl.pallas_call(
        body, name=name,
        out_shape=(pltpu.HBM(src_thru.shape, src_thru.dtype), pltpu.HBM(land_thru.shape, land_thru.dtype)),
        in_specs=(HBM, HBM, SEM, SEM, ANY), out_specs=(HBM, HBM), input_output_aliases={0: 0, 1: 1},
        compiler_params=pltpu.CompilerParams(has_side_effects=EFFECT),
    )(src_thru, land_thru, send_sems, recv_sems, after)[1]


def _to_pattern(a, r):
    return a.reshape(SEQ // r, r * a.shape[1])


def local_step(x, tgt, mod, get_w, put_grad, wc, wf, g_mix, bc, lg, lb, gco, gao, g_ffn, bf, g_fin):
    h1 = rms_mod_fwd(x, g_mix, mod, 0, 1, "h1_fwd")
    w_in = get_w("w_in", h1)
    proj = matmul(h1, w_in, "nt", F32, 512, D_IN, "proj_fwd")
    mix_a = conv_module_fwd(proj, wc, bc, lg, lb, gco, "conv_module_fwd")
    att, lse = attn_fwd_all(proj, "attn_fwd")
    mix_b = rms_gain_bf16(att, gao, "attn_out_norm")
    mixed = jnp.concatenate([mix_a, mix_b], axis=1)
    w_out = get_w("w_out", mixed)
    y1 = matmul(mixed, w_out, "nn", F32, 512, D_MODEL, "out_proj_fwd")
    x1, h2 = resid_rms_mod_fwd(x, y1, g_ffn, mod, 2, 3, 4, "x1_h2_fwd")
    w_up = get_w("w_up", h2)
    up0 = matmul(h2, w_up, "nt", F32, 512, D_FF, "up_fwd")
    act = ffn_act_fwd(up0, wf, bf, "ffn_act_fwd")
    w_down = get_w("w_down", act)
    y2 = matmul(act, w_down, "nn", F32, 512, D_MODEL, "down_fwd")
    loss_t, dx2, dy2, d_gfin, d_gaf = final_loss_bwd(x1, y2, tgt, g_fin, mod, 5, "loss_bwd")
    dact = matmul(dy2, w_down, "nt", F32, 512, FFN_TN, "down_bwd_x")
    dw_down = matmul(act, dy2, "tn", BF16, 256, D_MODEL, "down_bwd_w")
    dgate, dval, dbf_g, dbf_v, dwf_g, dwf_v = ffn_bwd_a(up0, dact, wf + put_grad("w_down", dw_down), bf, "ffn_bwd_a")
    dup0 = ffn_bwd_b(dgate, dval, wf, "ffn_bwd_b")
    dh2 = matmul(dup0, w_up, "nn", F32, 512, 512, "up_bwd_x")
    dw_up = matmul(dup0, h2, "tn", BF16, 512, D_MODEL, "up_bwd_w")
    dx1, d_shf, d_scf, d_gffn, dy1, d_gam = rms_mod_bwd(x1, dh2, dx2, g_ffn + put_grad("w_up", dw_up), mod, 4, y1, 2, "h2_bwd")
    dmixed = matmul(dy1, w_out, "nt", F32, 512, D_MODEL, "out_proj_bwd_x")
    dw_out = matmul(mixed, dy1, "tn", BF16, 256, D_MODEL, "out_proj_bwd_w")
    do, dd, d_gao = attn_combine_bwd(dmixed, att, gao + put_grad("w_out", dw_out), "attn_combine_bwd")
    dqkv = attn_bwd_all(proj, do, lse, dd, "attn_bwd")
    du1, d_gco, d_lg, d_lb, d_bc, d_wc = conv_module_bwd_a(proj, dmixed, wc, bc, lg, lb, gco, "conv_module_bwd_a")
    dproj_a = conv_module_bwd_b(proj, du1, wc, "conv_module_bwd_b")
    dproj = jnp.concatenate([dproj_a, dqkv[0], dqkv[1], dqkv[2]], axis=1)
    dw_in = matmul(dproj, h1, "tn", BF16, 512, D_MODEL, "proj_bwd_w")
    dh1 = matmul(dproj, w_in, "nn", F32, 512, D_MODEL, "proj_bwd_x")
    dx, d_shm, d_scm, d_gmix = rms_mod_bwd(x, dh1, dx1, g_mix + put_grad("w_in", dw_in), mod, 1, None, 0, "h1_bwd")
    dmod = jnp.concatenate([d_shm, d_scm, d_gam, d_shf, d_scf, d_gaf], axis=1)
    small = dict(g_norm_mix=d_gmix, b_conv_dw=d_bc, ln_conv_g=d_lg, ln_conv_b=d_lb, g_conv_out=d_gco, g_attn_out=d_gao,
                 g_norm_ffn=d_gffn, b_ffn_dw=jnp.concatenate([dbf_g, dbf_v], axis=1), g_final=d_gfin,
                 w_conv_dw=d_wc, w_ffn_dw=jnp.concatenate([dwf_g, dwf_v], axis=1), dmod=dmod, loss=loss_t[0:1, 0:1])
    return dx, small


def _padw(a, width):
    return jnp.pad(a, ((0, 0), (0, width - a.shape[1])))


def pack_small(t):
    wide = jnp.concatenate([_padw(t["dmod"], PACK_W), _padw(t["b_ffn_dw"], PACK_W), _padw(t["w_ffn_dw"], PACK_W),
                            _padw(t["loss"], PACK_W), jnp.zeros((2, PACK_W), F32)], axis=0)
    z512 = jnp.zeros((1, 512), F32)
    narrow = jnp.concatenate([
        t["g_norm_mix"], t["g_norm_ffn"], t["g_final"],
        jnp.concatenate([t["b_conv_dw"], t["ln_conv_g"]], axis=1),
        jnp.concatenate([t["ln_conv_b"], t["g_conv_out"]], axis=1),
        jnp.concatenate([t["g_attn_out"], z512], axis=1),
        jnp.zeros((2, 1024), F32),
        jnp.pad(t["w_conv_dw"], ((0, 1), (0, 0))).reshape(16, 1024)], axis=0)
    return jnp.concatenate([wide, narrow.reshape(4, PACK_W), jnp.zeros((4, PACK_W), F32)], axis=0)


def unpack_small(p):
    narrow = p[8:12].reshape(24, 1024)
    return dict(
        dmod=p[0:1], b_ffn_dw=p[1:2, :2 * D_FF], w_ffn_dw=p[2:5, :2 * D_FF], loss=p[5, 0],
        g_norm_mix=narrow[0:1], g_norm_ffn=narrow[1:2], g_final=narrow[2:3],
        b_conv_dw=narrow[3:4, :512], ln_conv_g=narrow[3:4, 512:], ln_conv_b=narrow[4:5, :512], g_conv_out=narrow[4:5, 512:],
        g_attn_out=narrow[5:6, :512], w_conv_dw=narrow[8:24].reshape(32, 512)[:CONV_K])


def _embed(local, width, me):
    return lax.dynamic_update_slice(jnp.zeros((local.shape[0], width), F32), local, (0, me * local.shape[1]))


def _shard(full, n_cols, me):
    return lax.dynamic_slice(full, (0, me * n_cols), (full.shape[0], n_cols))


WEIGHTS = ["w_ada", "b_ada", "g_norm_mix", "w_in", "w_conv_dw", "b_conv_dw", "ln_conv_g", "ln_conv_b", "g_conv_out",
           "g_attn_out", "w_out", "g_norm_ffn", "w_up", "w_ffn_dw", "b_ffn_dw", "w_down", "g_final"]
SMALL_REPLICATED = ["g_norm_mix", "b_conv_dw", "ln_conv_g", "ln_conv_b", "g_conv_out", "g_attn_out", "g_norm_ffn",
                    "b_ffn_dw", "g_final"]


def kernel(x, c, w_ada, b_ada, g_norm_mix, w_in, w_conv_dw, b_conv_dw, ln_conv_g, ln_conv_b, g_conv_out, g_attn_out, w_out, g_norm_ffn, w_up, w_ffn_dw, b_ffn_dw, w_down, g_final, loss_target, m_w_ada, m_b_ada, m_g_norm_mix, m_w_in, m_w_conv_dw, m_b_conv_dw, m_ln_conv_g, m_ln_conv_b, m_g_conv_out, m_g_attn_out, m_w_out, m_g_norm_ffn, m_w_up, m_w_ffn_dw, m_b_ffn_dw, m_w_down, m_g_final, v_w_ada, v_b_ada, v_g_norm_mix, v_w_in, v_w_conv_dw, v_b_conv_dw, v_ln_conv_g, v_ln_conv_b, v_g_conv_out, v_g_attn_out, v_w_out, v_g_norm_ffn, v_w_up, v_w_ffn_dw, v_b_ffn_dw, v_w_down, v_g_final):
    args = dict(locals())
    me = 4 * lax.axis_index("x") + 2 * lax.axis_index("y") + lax.axis_index("c")

    def flat(name, prefix=""):
        a = args[prefix + name]
        return a.reshape(a.shape[-2] if a.ndim > 1 else 1, a.shape[-1])

    n_in, n_up, r_out, r_down = w_in.shape[2], w_up.shape[2], w_out.shape[1], w_down.shape[1]
    n_ada, n_wc, n_wf = w_ada.shape[2], w_conv_dw.shape[2], w_ffn_dw.shape[2]
    taps_c = jnp.pad(flat("w_conv_dw").reshape(1, CONV_K * n_wc), ((0, 0), (0, 2 * D_MODEL - CONV_K * n_wc)))
    taps_f = jnp.pad(flat("w_ffn_dw").reshape(1, FFN_K * n_wf), ((0, 0), (0, 3 * D_MODEL - FFN_K * n_wf)))
    first = jnp.concatenate([c, taps_c.reshape(2, D_MODEL), taps_f.reshape(3, D_MODEL), jnp.zeros((2, D_MODEL), F32)], axis=0)
    first_all = all_gather(first, "gather_c_taps")
    c_all = first_all[:, 0, :]
    wc_full = first_all[:, 1:3, :].reshape(N_DEV, 2 * D_MODEL)[:, :CONV_K * n_wc].reshape(N_DEV, CONV_K, n_wc)
    wc_full = wc_full.transpose(1, 0, 2).reshape(CONV_K, D_CONV)
    wf_full = first_all[:, 3:6, :].reshape(N_DEV, 3 * D_MODEL)[:, :FFN_K * n_wf].reshape(N_DEV, FFN_K, n_wf)
    wf_full = wf_full.transpose(1, 0, 2).reshape(FFN_K, 2 * D_FF)
    mod_cols = ada_fwd(c_all, flat("w_ada"), _shard(flat("b_ada"), n_ada, me), "ada_fwd")
    mod_all = all_gather(mod_cols, "gather_mod")
    mod = lax.dynamic_index_in_dim(mod_all, me, axis=1, keepdims=False).reshape(N_MOD, D_MODEL)
    mod = jnp.pad(mod, ((0, 2), (0, 0)))

    def flat_t(name, prefix=""):
        return args[prefix + name][0].T

    blocks = dict(w_in=flat_t("w_in").astype(BF16), w_up=flat_t("w_up").astype(BF16),
                  rows=jnp.concatenate([flat("w_out"), flat("w_down")], axis=0).astype(BF16))
    gathers, tok = {}, None
    for name in ("w_in", "rows", "w_up"):
        src = blocks[name] if tok is None else blocks[name] + tok.astype(BF16)
        gathers[name], tok = exchange_start(src, True, f"gather_{name}_start", mod_all)
    mod = mod + tok
    slot = lax.broadcasted_iota(jnp.int32, (N_DEV, 1, 1), 0)
    full = {}

    def gathered(name, after):
        land = exchange_wait(gathers[name], after, True, f"gather_{name}_wait")
        return jnp.where(slot == me, blocks[name][None], land)

    def get_w(name, after):
        if name in ("w_out", "w_down"):
            if "rows" not in full:
                full["rows"] = gathered("rows", after)
            rows = full["rows"]
            return rows[:, :r_out, :].reshape(D_MODEL, D_MODEL) if name == "w_out" else rows[:, r_out:, :].reshape(D_FF, D_MODEL)
        return gathered(name, after).reshape(-1, D_MODEL)

    exchanges, own = {}, {}

    def put_grad(name, dw, after=None):
        dev_major = dw.reshape(N_DEV, -1, D_MODEL)
        own[name] = lax.dynamic_index_in_dim(dev_major, me, axis=0, keepdims=False)
        exchanges[name], token = exchange_start(dev_major, False, f"exchange_{name}_start", after)
        return token

    grad_x, small = local_step(
        x[0], loss_target[0], mod, get_w, put_grad, wc_full, wf_full,
        flat("g_norm_mix"), flat("b_conv_dw"), flat("ln_conv_g"), flat("ln_conv_b"), flat("g_conv_out"),
        flat("g_attn_out"), flat("g_norm_ffn"), flat("b_ffn_dw"), flat("g_final"))

    out = {}

    def finish(name, tr, after):
        parts = exchange_wait(exchanges[name], after, False, f"exchange_{name}_wait")
        if name in ("w_in", "w_up"):
            res = sum_adamw(parts, own[name], flat_t(name), flat_t(name, "m_"), flat_t(name, "v_"), tr, "adamw_" + name)
            out[name] = tuple(r.T for r in res)
        else:
            res = out[name] = sum_adamw(parts, own[name], flat(name), flat(name, "m_"), flat(name, "v_"), tr, "adamw_" + name)
        return res[0]

    after = finish("w_down", r_down, grad_x)
    after = finish("w_up", n_up // 2, after)
    after = finish("w_out", r_out, after)
    after = finish("w_in", n_in, after)

    small_all = all_gather(pack_small(small), "gather_small", after)

    def packed(prefix):
        t = {n: flat(n, prefix) for n in SMALL_REPLICATED}
        t["dmod"] = flat("b_ada", prefix)
        t["loss"] = jnp.zeros((1, 1), F32)
        t["w_conv_dw"] = _embed(flat("w_conv_dw", prefix), D_CONV, me)
        t["w_ffn_dw"] = _embed(flat("w_ffn_dw", prefix), 2 * D_FF, me)
        return pack_small(t)

    res = sum_adamw(small_all, None, packed(""), packed("m_"), packed("v_"), PACK_ROWS, "adamw_small")
    res = [unpack_small(r) for r in res]
    loss = res[0]["loss"]
    for n in SMALL_REPLICATED:
        out[n] = tuple(r[n] for r in res)
    out["b_ada"] = tuple(r["dmod"] for r in res)
    out["w_conv_dw"] = tuple(_shard(r["w_conv_dw"], n_wc, me) for r in res)
    out["w_ffn_dw"] = tuple(_shard(r["w_ffn_dw"], n_wf, me) for r in res)

    dmod_cols = _shard(small_all[:, 0, :], n_ada, me)
    out["w_ada"] = ada_bwd_adamw(c_all.T, dmod_cols, flat("w_ada"), flat("w_ada", "m_"), flat("w_ada", "v_"), "adamw_w_ada")

    result = [loss, grad_x[None]]
    for k in range(4):
        result += [out[n][k].reshape(args[n].shape) for n in WEIGHTS]
    return tuple(result)
```

```python
import functools

import jax
import jax.numpy as jnp
from jax import lax
from jax.experimental import pallas as pl
from jax.experimental.pallas import tpu as pltpu

F32 = jnp.float32
BF16 = jnp.bfloat16

N_DEV = 8
SEQ = 2048
D_MODEL = 1024
D_CONV = 512
D_ATTN = 512
HEAD_DIM = 64
CONV_K = 31
D_FF = 2816
FFN_K = 3
D_IN = 2 * D_CONV + 3 * D_ATTN
N_MOD = 6
EPS = 1e-6
ATTN_BLOCK = 128
PATTERNS = ((2048, 1), (512, 4), (128, 16))
NEG = -1e30

ADAM_LR, ADAM_B1, ADAM_B2, ADAM_EPS, ADAM_WD, ADAM_STEP = 0.001, 0.9, 0.999, 1e-08, 0.01, 10

ROWS = 256
CONV_HALO = 32
FFN_HALO = 8
FFN_TN = 1408
VMEM_LIMIT = 56 * 1024 * 1024
PACK_ROWS, PACK_W = 16, 6144


def _cp(*sem):
    return pltpu.CompilerParams(dimension_semantics=sem if sem else None, vmem_limit_bytes=VMEM_LIMIT)


def _sig(x):
    return 1.0 / (1.0 + jnp.exp(-x))


def _rsum(x):
    return jnp.sum(x, axis=0, keepdims=True)


def _mean(x):
    return jnp.mean(x, axis=-1, keepdims=True)


def _acc(ref, val, first):
    @pl.when(first)
    def _():
        ref[...] = val

    @pl.when(jnp.logical_not(first))
    def _():
        ref[...] += val


SUB = 16


def _for_chunks(fn, unroll=1):
    def step(i, carry):
        fn(pl.ds(pl.multiple_of(i * SUB, SUB), SUB))
        return carry

    lax.fori_loop(0, ROWS // SUB, step, 0, unroll=unroll)


def rms_mod_fwd(x, g, mod, sh_row, sc_row, name):
    def body(x_ref, g_ref, mod_ref, h_ref):
        xx = x_ref[...]
        r = lax.rsqrt(_mean(xx * xx) + EPS)
        h = xx * r * g_ref[...]
        h_ref[...] = (h * (1.0 + mod_ref[sc_row:sc_row + 1, :]) + mod_ref[sh_row:sh_row + 1, :]).astype(BF16)

    return pl.pallas_call(
        body, out_shape=jax.ShapeDtypeStruct((SEQ, D_MODEL), BF16), grid=(SEQ // ROWS,),
        in_specs=[_row_spec(D_MODEL), _vec_spec(D_MODEL), _vec_spec(D_MODEL, 8)],
        out_specs=_row_spec(D_MODEL), name=name, compiler_params=_cp("parallel"))(x, g, mod)


def resid_rms_mod_fwd(x, y, g, mod, ga_row, sh_row, sc_row, name):
    def body(x_ref, y_ref, g_ref, mod_ref, x1_ref, h_ref):
        x1 = x_ref[...] + mod_ref[ga_row:ga_row + 1, :] * y_ref[...]
        x1_ref[...] = x1
        r = lax.rsqrt(_mean(x1 * x1) + EPS)
        h = x1 * r * g_ref[...]
        h_ref[...] = (h * (1.0 + mod_ref[sc_row:sc_row + 1, :]) + mod_ref[sh_row:sh_row + 1, :]).astype(BF16)

    return pl.pallas_call(
        body, out_shape=(jax.ShapeDtypeStruct((SEQ, D_MODEL), F32), jax.ShapeDtypeStruct((SEQ, D_MODEL), BF16)),
        grid=(SEQ // ROWS,),
        in_specs=[_row_spec(D_MODEL), _row_spec(D_MODEL), _vec_spec(D_MODEL), _vec_spec(D_MODEL, 8)],
        out_specs=(_row_spec(D_MODEL), _row_spec(D_MODEL)), name=name, compiler_params=_cp("parallel"))(x, y, g, mod)


def final_loss_bwd(x1, y2, tgt, g, mod, ga_row, name):
    def body(x1_ref, y2_ref, t_ref, g_ref, mod_ref, loss_ref, dx2_ref, dy2_ref, dg_ref, dga_ref):
        first = pl.program_id(0) == 0
        ga = mod_ref[ga_row:ga_row + 1, :]
        y2 = y2_ref[...]
        x2 = x1_ref[...] + ga * y2
        r = lax.rsqrt(_mean(x2 * x2) + EPS)
        xn = x2 * r
        err = xn * g_ref[...] - t_ref[...]
        _acc(loss_ref, jnp.broadcast_to(0.5 * jnp.sum(_mean(err * err)), (8, 128)), first)
        dy = err * (1.0 / D_MODEL)
        _acc(dg_ref, _rsum(dy * xn), first)
        dxn = dy * g_ref[...]
        dx2 = r * (dxn - xn * _mean(dxn * xn))
        dx2_ref[...] = dx2
        dy2_ref[...] = (dx2 * ga).astype(BF16)
        _acc(dga_ref, _rsum(dx2 * y2), first)

    vec = jax.ShapeDtypeStruct((1, D_MODEL), F32)
    return pl.pallas_call(
        body,
        out_shape=(jax.ShapeDtypeStruct((8, 128), F32), jax.ShapeDtypeStruct((SEQ, D_MODEL), F32),
                   jax.ShapeDtypeStruct((SEQ, D_MODEL), BF16), vec, vec),
        grid=(SEQ // ROWS,),
        in_specs=[_row_spec(D_MODEL), _row_spec(D_MODEL), _row_spec(D_MODEL), _vec_spec(D_MODEL), _vec_spec(D_MODEL, 8)],
        out_specs=(pl.BlockSpec((8, 128), lambda i: (0, 0)), _row_spec(D_MODEL), _row_spec(D_MODEL),
                   _vec_spec(D_MODEL), _vec_spec(D_MODEL)),
        name=name, compiler_params=_cp("arbitrary"))(x1, y2, tgt, g, mod)


def rms_mod_bwd(x, dh, dres, g, mod, sc_row, y, ga_row, name):
    gated = y is not None

    def body(*refs):
        if gated:
            x_ref, dh_ref, dres_ref, g_ref, mod_ref, y_ref, dx_ref, dsh_ref, dsc_ref, dg_ref, dy_ref, dga_ref = refs
        else:
            x_ref, dh_ref, dres_ref, g_ref, mod_ref, dx_ref, dsh_ref, dsc_ref, dg_ref = refs
        first = pl.program_id(0) == 0
        xx = x_ref[...]
        dh = dh_ref[...]
        gg = g_ref[...]
        r = lax.rsqrt(_mean(xx * xx) + EPS)
        xn = xx * r
        _acc(dsh_ref, _rsum(dh), first)
        _acc(dsc_ref, _rsum(dh * (xn * gg)), first)
        dt = dh * (1.0 + mod_ref[sc_row:sc_row + 1, :])
        _acc(dg_ref, _rsum(dt * xn), first)
        dxn = dt * gg
        dx = dres_ref[...] + r * (dxn - xn * _mean(dxn * xn))
        dx_ref[...] = dx
        if gated:
            _acc(dga_ref, _rsum(dx * y_ref[...]), first)
            dy_ref[...] = (dx * mod_ref[ga_row:ga_row + 1, :]).astype(BF16)

    vec = jax.ShapeDtypeStruct((1, D_MODEL), F32)
    in_specs = [_row_spec(D_MODEL), _row_spec(D_MODEL), _row_spec(D_MODEL), _vec_spec(D_MODEL), _vec_spec(D_MODEL, 8)]
    out_shape = [jax.ShapeDtypeStruct((SEQ, D_MODEL), F32), vec, vec, vec]
    out_specs = [_row_spec(D_MODEL), _vec_spec(D_MODEL), _vec_spec(D_MODEL), _vec_spec(D_MODEL)]
    args = [x, dh, dres, g, mod]
    if gated:
        in_specs.append(_row_spec(D_MODEL))
        out_shape += [jax.ShapeDtypeStruct((SEQ, D_MODEL), BF16), vec]
        out_specs += [_row_spec(D_MODEL), _vec_spec(D_MODEL)]
        args.append(y)
    return pl.pallas_call(
        body, out_shape=tuple(out_shape), grid=(SEQ // ROWS,), in_specs=in_specs, out_specs=tuple(out_specs),
        name=name, compiler_params=_cp("arbitrary"))(*args)


def _prev_halo(halo, width, col):
    per = ROWS // halo
    return pl.BlockSpec((halo, width), lambda i: (jnp.maximum(i * per - 1, 0), col))


def _next_halo(halo, width, col):
    per = ROWS // halo
    last = SEQ // halo - 1
    return pl.BlockSpec((halo, width), lambda i: (jnp.minimum((i + 1) * per, last), col))


CONV_PAD = ROWS + CONV_HALO


def _shift_copies(sh):
    for b in range(1, 8):
        sh[b, 0:CONV_PAD - 8, :] = sh[0, pl.ds(b, CONV_PAD - 8), :]


def _tap(sh, rs_start, offset):
    return sh[offset % 8, pl.ds(pl.multiple_of(rs_start + (offset // 8) * 8, 8), SUB), :]


def _conv_module_forward(av_ref, ag_ref, avh_ref, agh_ref, wc_ref, bc_ref, lg_ref, lb_ref, sh, u1_s):
    i = pl.program_id(0)
    hv = avh_ref[...] * _sig(agh_ref[...])
    sh[0, 0:CONV_HALO, :] = jnp.where(i > 0, hv, 0.0)

    def glu(rs):
        sh[0, pl.ds(pl.multiple_of(rs.start + CONV_HALO, SUB), SUB), :] = av_ref[rs, :] * _sig(ag_ref[rs, :])

    _for_chunks(glu)
    _shift_copies(sh)

    def conv(rs):
        u1 = jnp.broadcast_to(bc_ref[...], (SUB, D_CONV))
        for j in range(CONV_K):
            u1 = u1 + wc_ref[j:j + 1, :] * _tap(sh, rs.start, CONV_HALO - (CONV_K - 1) + j)
        u1_s[rs, :] = u1

    _for_chunks(conv)
    u1 = u1_s[...]
    mu = _mean(u1)
    cen = u1 - mu
    rs = lax.rsqrt(_mean(cen * cen) + EPS)
    z = cen * rs
    ln = z * lg_ref[...] + lb_ref[...]
    s = _sig(ln)
    return z, rs, ln, s, ln * s


_CONV_SCRATCH = [pltpu.VMEM((8, CONV_PAD, D_CONV), F32), pltpu.VMEM((ROWS, D_CONV), F32)]


def conv_module_fwd(proj, wc, bc, lg, lb, gco, name):
    def body(av_ref, ag_ref, avh_ref, agh_ref, wc_ref, bc_ref, lg_ref, lb_ref, gco_ref, out_ref, sh, u1_s):
        _, _, _, _, u2 = _conv_module_forward(av_ref, ag_ref, avh_ref, agh_ref, wc_ref, bc_ref, lg_ref, lb_ref, sh, u1_s)
        rc = lax.rsqrt(_mean(u2 * u2) + EPS)
        out_ref[...] = (u2 * rc * gco_ref[...]).astype(BF16)

    v = _vec_spec(D_CONV)
    return pl.pallas_call(
        body, out_shape=jax.ShapeDtypeStruct((SEQ, D_CONV), BF16), grid=(SEQ // ROWS,),
        in_specs=[_row_spec(D_CONV, 0), _row_spec(D_CONV, 1), _prev_halo(CONV_HALO, D_CONV, 0),
                  _prev_halo(CONV_HALO, D_CONV, 1), _vec_spec(D_CONV, CONV_K), v, v, v, v],
        out_specs=_row_spec(D_CONV), scratch_shapes=list(_CONV_SCRATCH),
        name=name, compiler_params=_cp("parallel"))(proj, proj, proj, proj, wc, bc, lg, lb, gco)


def conv_module_bwd_a(proj, dmixed, wc, bc, lg, lb, gco, name):
    def body(av_ref, ag_ref, avh_ref, agh_ref, dm_ref, wc_ref, bc_ref, lg_ref, lb_ref, gco_ref,
             du1_ref, dgco_ref, dlg_ref, dlb_ref, dbc_ref, dwc_ref, sh, u1_s, acc):
        first = pl.program_id(0) == 0
        z, rs, ln, s, u2 = _conv_module_forward(av_ref, ag_ref, avh_ref, agh_ref, wc_ref, bc_ref, lg_ref, lb_ref, sh, u1_s)
        rc = lax.rsqrt(_mean(u2 * u2) + EPS)
        xn = u2 * rc
        dm = dm_ref[...]
        _acc(dgco_ref, _rsum(dm * xn), first)
        dyn = dm * gco_ref[...]
        du2 = rc * (dyn - xn * _mean(dyn * xn))
        dln = du2 * (s * (1.0 + ln * (1.0 - s)))
        _acc(dlg_ref, _rsum(dln * z), first)
        _acc(dlb_ref, _rsum(dln), first)
        dz = dln * lg_ref[...]
        du1 = rs * (dz - _mean(dz) - z * _mean(dz * z))
        du1_ref[...] = du1
        _acc(dbc_ref, _rsum(du1), first)
        acc[...] = jnp.zeros_like(acc)

        def taps(rs):
            d = du1_ref[rs, :]
            for j in range(CONV_K):
                acc[j] += d * _tap(sh, rs.start, CONV_HALO - (CONV_K - 1) + j)

        _for_chunks(taps)

        @pl.when(first)
        def _():
            dwc_ref[...] = jnp.zeros_like(dwc_ref)

        for j in range(CONV_K):
            dwc_ref[j:j + 1, :] += _rsum(acc[j])

    v = _vec_spec(D_CONV)
    vec = jax.ShapeDtypeStruct((1, D_CONV), F32)
    return pl.pallas_call(
        body,
        out_shape=(jax.ShapeDtypeStruct((SEQ, D_CONV), F32), vec, vec, vec, vec, jax.ShapeDtypeStruct((CONV_K, D_CONV), F32)),
        grid=(SEQ // ROWS,),
        in_specs=[_row_spec(D_CONV, 0), _row_spec(D_CONV, 1), _prev_halo(CONV_HALO, D_CONV, 0),
                  _prev_halo(CONV_HALO, D_CONV, 1), _row_spec(D_CONV, 0), _vec_spec(D_CONV, CONV_K), v, v, v, v],
        out_specs=(_row_spec(D_CONV), v, v, v, v, _vec_spec(D_CONV, CONV_K)),
        scratch_shapes=list(_CONV_SCRATCH) + [pltpu.VMEM((CONV_K, SUB, D_CONV), F32)],
        name=name, compiler_params=_cp("arbitrary"))(proj, proj, proj, proj, dmixed, wc, bc, lg, lb, gco)


def conv_module_bwd_b(proj, du1, wc, name):
    def body(av_ref, ag_ref, du1_ref, du1n_ref, wc_ref, out_ref, sh):
        i = pl.program_id(0)
        sh[0, 0:ROWS, :] = du1_ref[...]
        sh[0, ROWS:, :] = jnp.where(i < SEQ // ROWS - 1, du1n_ref[...], 0.0)
        _shift_copies(sh)

        def chunk(rs):
            du0 = jnp.zeros((SUB, D_CONV), F32)
            for j in range(CONV_K):
                du0 = du0 + wc_ref[j:j + 1, :] * _tap(sh, rs.start, CONV_K - 1 - j)
            sg = _sig(ag_ref[rs, :])
            out_ref[rs, 0:D_CONV] = (du0 * sg).astype(BF16)
            out_ref[rs, D_CONV:] = (du0 * av_ref[rs, :] * sg * (1.0 - sg)).astype(BF16)

        _for_chunks(chunk)

    return pl.pallas_call(
        body, out_shape=jax.ShapeDtypeStruct((SEQ, 2 * D_CONV), BF16), grid=(SEQ // ROWS,),
        in_specs=[_row_spec(D_CONV, 0), _row_spec(D_CONV, 1), _row_spec(D_CONV, 0), _next_halo(CONV_HALO, D_CONV, 0),
                  _vec_spec(D_CONV, CONV_K)],
        out_specs=_row_spec(2 * D_CONV), scratch_shapes=[pltpu.VMEM((8, CONV_PAD, D_CONV), F32)],
        name=name, compiler_params=_cp("parallel"))(proj, proj, du1, du1, wc)


def _attn_specs(sub_len, pairs):
    ng, width = 4 // pairs, 128 * pairs
    q = pl.BlockSpec((sub_len, width), lambda rho, g: (0, rho * 3 * ng + g))
    k = pl.BlockSpec((sub_len, width), lambda rho, g: (0, rho * 3 * ng + ng + g))
    v = pl.BlockSpec((sub_len, width), lambda rho, g: (0, rho * 3 * ng + 2 * ng + g))
    o = pl.BlockSpec((sub_len, width), lambda rho, g: (0, rho * ng + g))
    return q, k, v, o


ATTN_PAIRS = {1: 1, 4: 1, 16: 4}


def _attn_block(q_ref, k_ref, v_ref, n, hs, win):
    q0 = pl.multiple_of(n * ATTN_BLOCK, ATTN_BLOCK)
    k0 = pl.multiple_of(jnp.maximum(n - 1, 0) * ATTN_BLOCK, ATTN_BLOCK)
    qb = q_ref[pl.ds(q0, ATTN_BLOCK), hs]
    kw = k_ref[pl.ds(k0, win), hs]
    vw = v_ref[pl.ds(k0, win), hs]
    s = lax.dot_general(qb, kw, (((1,), (1,)), ((), ())), preferred_element_type=F32) * (HEAD_DIM ** -0.5)
    dist = (q0 - k0) + lax.broadcasted_iota(jnp.int32, (ATTN_BLOCK, win), 0) \
        - lax.broadcasted_iota(jnp.int32, (ATTN_BLOCK, win), 1)
    s = jnp.where((dist >= 0) & (dist <= ATTN_BLOCK), s, NEG)
    return q0, k0, qb, kw, vw, s


def attn_fwd(qkv_r, sub_len, r, name):
    nb = sub_len // ATTN_BLOCK
    win = 2 * ATTN_BLOCK if nb > 1 else ATTN_BLOCK
    pairs = ATTN_PAIRS[r]

    def body(q_ref, k_ref, v_ref, o_ref, l_ref):
        def block(n, carry):
            for h in range(2 * pairs):
                hs = slice(h * HEAD_DIM, (h + 1) * HEAD_DIM)
                q0, _, _, _, vw, s = _attn_block(q_ref, k_ref, v_ref, n, hs, win)
                m = jnp.max(s, axis=1, keepdims=True)
                p = jnp.exp(s - m)
                den = jnp.sum(p, axis=1, keepdims=True)
                o = jnp.dot(p.astype(BF16), vw, preferred_element_type=F32) / den
                o_ref[pl.ds(q0, ATTN_BLOCK), hs] = o
                l_ref[pl.ds(q0, ATTN_BLOCK), hs] = jnp.broadcast_to(m + jnp.log(den), (ATTN_BLOCK, HEAD_DIM))
            return carry

        lax.fori_loop(0, nb, block, 0, unroll=min(nb, 2))

    q, k, v, o = _attn_specs(sub_len, pairs)
    shp = jax.ShapeDtypeStruct((sub_len, r * D_ATTN), F32)
    return pl.pallas_call(
        body, out_shape=(shp, shp), grid=(r, 4 // pairs), in_specs=[q, k, v], out_specs=(o, o),
        name=name, compiler_params=_cp("parallel", "parallel"))(qkv_r, qkv_r, qkv_r)


def attn_bwd(qkv_r, do_r, lse_r, dd_r, sub_len, r, name):
    nb = sub_len // ATTN_BLOCK
    win = 2 * ATTN_BLOCK if nb > 1 else ATTN_BLOCK
    pairs = ATTN_PAIRS[r]

    def body(q_ref, k_ref, v_ref, do_ref, l_ref, dd_ref, dq_ref, dk_ref, dv_ref):
        dk_ref[...] = jnp.zeros_like(dk_ref)
        dv_ref[...] = jnp.zeros_like(dv_ref)

        def block(n, carry):
            for h in range(2 * pairs):
                hs = slice(h * HEAD_DIM, (h + 1) * HEAD_DIM)
                h1 = slice(h * HEAD_DIM, h * HEAD_DIM + 1)
                q0, k0, qb, kw, vw, s = _attn_block(q_ref, k_ref, v_ref, n, hs, win)
                dob = do_ref[pl.ds(q0, ATTN_BLOCK), hs]
                p = jnp.exp(s - l_ref[pl.ds(q0, ATTN_BLOCK), h1])
                dp = lax.dot_general(dob, vw, (((1,), (1,)), ((), ())), preferred_element_type=F32)
                ds = (p * (dp - dd_ref[pl.ds(q0, ATTN_BLOCK), h1]) * (HEAD_DIM ** -0.5)).astype(BF16)
                dq_ref[pl.ds(q0, ATTN_BLOCK), hs] = jnp.dot(ds, kw, preferred_element_type=F32)
                dk_ref[pl.ds(k0, win), hs] += lax.dot_general(ds, qb, (((0,), (0,)), ((), ())), preferred_element_type=F32)
                dv_ref[pl.ds(k0, win), hs] += lax.dot_general(p.astype(BF16), dob, (((0,), (0,)), ((), ())),
                                                              preferred_element_type=F32)
            return carry

        lax.fori_loop(0, nb, block, 0)

    q, k, v, o = _attn_specs(sub_len, pairs)
    shp = jax.ShapeDtypeStruct((sub_len, r * D_ATTN), F32)
    return pl.pallas_call(
        body, out_shape=(shp, shp, shp), grid=(r, 4 // pairs), in_specs=[q, k, v, o, o, o], out_specs=(o, o, o),
        name=name, compiler_params=_cp("parallel", "parallel"))(qkv_r, qkv_r, qkv_r, do_r, lse_r, dd_r)


def _rows(start, size, r):
    return pl.ds(start, size) if r == 1 else pl.ds(start, size, stride=r)


def _attn_unit(q_ref, k_ref, v_ref, r, rho, n, nb):
    win = 2 * ATTN_BLOCK if nb > 1 else ATTN_BLOCK
    if isinstance(n, int):
        kb = max(n - 1, 0)
        q_rows = _rows(rho + r * ATTN_BLOCK * n, ATTN_BLOCK, r)
        k_rows = _rows(rho + r * ATTN_BLOCK * kb, win, r)
    else:
        kb = jnp.maximum(n - 1, 0)
        q_rows = pl.ds(pl.multiple_of(n * ATTN_BLOCK, ATTN_BLOCK), ATTN_BLOCK)
        k_rows = pl.ds(pl.multiple_of(kb * ATTN_BLOCK, ATTN_BLOCK), win)
    q2 = q_ref[q_rows, :].astype(BF16)
    k2 = k_ref[k_rows, :].astype(BF16)
    v2 = v_ref[k_rows, :].astype(BF16)
    dist = (n - kb) * ATTN_BLOCK + lax.broadcasted_iota(jnp.int32, (ATTN_BLOCK, win), 0) \
        - lax.broadcasted_iota(jnp.int32, (ATTN_BLOCK, win), 1)
    valid = (dist >= 0) & (dist <= ATTN_BLOCK)

    def score(h):
        hs = slice(h * HEAD_DIM, (h + 1) * HEAD_DIM)
        s = lax.dot_general(q2[:, hs], k2[:, hs], (((1,), (1,)), ((), ())), preferred_element_type=F32)
        return jnp.where(valid, s * (HEAD_DIM ** -0.5), NEG), q2[:, hs], k2[:, hs], v2[:, hs]

    return q_rows, k_rows, score


def _attn_units(r, nb, unit):
    if r == 1:
        def two(i, carry):
            unit(0, 2 * i)
            unit(0, 2 * i + 1)
            return carry
        lax.fori_loop(0, nb // 2, two, 0)
    else:
        for rho in range(r):
            for n in range(nb):
                unit(rho, n)


def attn_fwd_all(proj, name):
    def body(q_ref, k_ref, v_ref, att_ref, lse_ref):
        for idx, (sub_len, r) in enumerate(PATTERNS):
            nb = sub_len // ATTN_BLOCK

            def unit(rho, n, r=r, nb=nb, idx=idx):
                q_rows, _, score = _attn_unit(q_ref, k_ref, v_ref, r, rho, n, nb)
                outs, lses = [], []
                for h in range(2):
                    s, _, _, vw = score(h)
                    m = jnp.max(s, axis=1, keepdims=True)
                    p = jnp.exp(s - m)
                    den = jnp.sum(p, axis=1, keepdims=True)
                    outs.append(jnp.dot(p.astype(BF16), vw, preferred_element_type=F32) / den)
                    lses.append(jnp.broadcast_to(m + jnp.log(den), (ATTN_BLOCK, HEAD_DIM)))
                o = jnp.concatenate(outs, axis=1)
                lse = jnp.concatenate(lses, axis=1)
                if idx > 0:
                    old = lse_ref[q_rows, :]
                    top = jnp.maximum(old, lse)
                    new = top + jnp.log(jnp.exp(old - top) + jnp.exp(lse - top))
                    o = att_ref[q_rows, :] * jnp.exp(old - new) + o * jnp.exp(lse - new)
                    lse = new
                att_ref[q_rows, :] = o
                lse_ref[q_rows, :] = lse

            _attn_units(r, nb, unit)

    blk = lambda first: pl.BlockSpec((SEQ, 128), lambda g: (0, first + g))
    shp = jax.ShapeDtypeStruct((SEQ, D_ATTN), F32)
    return pl.pallas_call(
        body, out_shape=(shp, shp), grid=(4,), in_specs=[blk(8), blk(12), blk(16)], out_specs=(blk(0), blk(0)),
        name=name, compiler_params=_cp("parallel"))(proj, proj, proj)


def attn_bwd_all(proj, do, lse, dd, name):
    def body(q_ref, k_ref, v_ref, do_ref, l_ref, dd_ref, out_ref, dq_s, dk_s, dv_s):
        dq_s[...] = jnp.zeros_like(dq_s)
        dk_s[...] = jnp.zeros_like(dk_s)
        dv_s[...] = jnp.zeros_like(dv_s)
        for sub_len, r in PATTERNS:
            nb = sub_len // ATTN_BLOCK

            def unit(rho, n, r=r, nb=nb):
                q_rows, k_rows, score = _attn_unit(q_ref, k_ref, v_ref, r, rho, n, nb)
                do2 = do_ref[q_rows, :].astype(BF16)
                l2 = l_ref[q_rows, :]
                d2 = dd_ref[q_rows, :]
                dq, dk, dv = [], [], []
                for h in range(2):
                    hs = slice(h * HEAD_DIM, (h + 1) * HEAD_DIM)
                    h1 = slice(h * HEAD_DIM, h * HEAD_DIM + 1)
                    s, qb, kw, vw = score(h)
                    dob = do2[:, hs]
                    p = jnp.exp(s - l2[:, h1])
                    dp = lax.dot_general(dob, vw, (((1,), (1,)), ((), ())), preferred_element_type=F32)
                    ds = (p * (dp - d2[:, h1]) * (HEAD_DIM ** -0.5)).astype(BF16)
                    dq.append(jnp.dot(ds, kw, preferred_element_type=F32))
                    dk.append(lax.dot_general(ds, qb, (((0,), (0,)), ((), ())), preferred_element_type=F32))
                    dv.append(lax.dot_general(p.astype(BF16), dob, (((0,), (0,)), ((), ())), preferred_element_type=F32))
                dq_s[q_rows, :] += jnp.concatenate(dq, axis=1)
                dk_s[k_rows, :] += jnp.concatenate(dk, axis=1)
                dv_s[k_rows, :] += jnp.concatenate(dv, axis=1)

            _attn_units(r, nb, unit)
        out_ref[0] = dq_s[...].astype(BF16)
        out_ref[1] = dk_s[...].astype(BF16)
        out_ref[2] = dv_s[...].astype(BF16)

    blk = lambda first: pl.BlockSpec((SEQ, 128), lambda g: (0, first + g))
    acc = pltpu.VMEM((SEQ, 128), F32)
    return pl.pallas_call(
        body, out_shape=jax.ShapeDtypeStruct((3, SEQ, D_ATTN), BF16), grid=(4,),
        in_specs=[blk(8), blk(12), blk(16), blk(0), blk(0), blk(0)],
        out_specs=pl.BlockSpec((3, SEQ, 128), lambda g: (0, 0, g)), scratch_shapes=[acc, acc, acc],
        name=name, compiler_params=_cp("parallel"))(proj, proj, proj, do, lse, dd)


def rms_gain_bf16(a, g, name):
    def body(a_ref, g_ref, o_ref):
        aa = a_ref[...]
        o_ref[...] = (aa * lax.rsqrt(_mean(aa * aa) + EPS) * g_ref[...]).astype(BF16)

    w = a.shape[1]
    return pl.pallas_call(
        body, out_shape=jax.ShapeDtypeStruct(a.shape, BF16), grid=(SEQ // ROWS,), in_specs=[_row_spec(w), _vec_spec(w)],
        out_specs=_row_spec(w), name=name, compiler_params=_cp("parallel"))(a, g)


def attn_combine_fwd(outs, lses, gao, name):
    def body(o1, o2, o3, l1, l2, l3, g_ref, att_ref, lse_ref, mix_ref):
        a1, a2, a3 = l1[...], l2[...], l3[...]
        m = jnp.maximum(jnp.maximum(a1, a2), a3)
        w1, w2, w3 = jnp.exp(a1 - m), jnp.exp(a2 - m), jnp.exp(a3 - m)
        den = w1 + w2 + w3
        att = (w1 * o1[...] + w2 * o2[...] + w3 * o3[...]) / den
        att_ref[...] = att
        lse_ref[...] = m + jnp.log(den)
        mix_ref[...] = (att * lax.rsqrt(_mean(att * att) + EPS) * g_ref[...]).astype(BF16)

    rs = _row_spec(D_ATTN)
    f = jax.ShapeDtypeStruct((SEQ, D_ATTN), F32)
    return pl.pallas_call(
        body, out_shape=(f, f, jax.ShapeDtypeStruct((SEQ, D_ATTN), BF16)), grid=(SEQ // ROWS,),
        in_specs=[rs] * 6 + [_vec_spec(D_ATTN)], out_specs=(rs, rs, rs),
        name=name, compiler_params=_cp("parallel"))(*outs, *lses, gao)


def attn_combine_bwd(dmixed, att, gao, name):
    def body(dm_ref, att_ref, g_ref, do_ref, dd_ref, dg_ref):
        first = pl.program_id(0) == 0
        att = att_ref[...]
        r = lax.rsqrt(_mean(att * att) + EPS)
        xn = att * r
        dm = dm_ref[...]
        _acc(dg_ref, _rsum(dm * xn), first)
        dyn = dm * g_ref[...]
        do = r * (dyn - xn * _mean(dyn * xn))
        do_ref[...] = do
        same_head = (jnp.right_shift(lax.broadcasted_iota(jnp.int32, (D_ATTN, D_ATTN), 0), 6)
                     == jnp.right_shift(lax.broadcasted_iota(jnp.int32, (D_ATTN, D_ATTN), 1), 6)).astype(F32)
        dd_ref[...] = jnp.dot(do * att, same_head, preferred_element_type=F32, precision=lax.Precision.HIGHEST)

    rs = _row_spec(D_ATTN)
    return pl.pallas_call(
        body,
        out_shape=(jax.ShapeDtypeStruct((SEQ, D_ATTN), F32), jax.ShapeDtypeStruct((SEQ, D_ATTN), F32),
                   jax.ShapeDtypeStruct((1, D_ATTN), F32)),
        grid=(SEQ // ROWS,), in_specs=[_row_spec(D_ATTN, 1), rs, _vec_spec(D_ATTN)],
        out_specs=(rs, rs, _vec_spec(D_ATTN)), name=name, compiler_params=_cp("arbitrary"))(dmixed, att, gao)


def sum3_bf16(a, b, c, name):
    def body(a_ref, b_ref, c_ref, o_ref):
        o_ref[...] = (a_ref[...] + b_ref[...] + c_ref[...]).astype(BF16)

    w = a.shape[1]
    rs = _row_spec(w)
    return pl.pallas_call(
        body, out_shape=jax.ShapeDtypeStruct(a.shape, BF16), grid=(SEQ // ROWS,), in_specs=[rs, rs, rs], out_specs=rs,
        name=name, compiler_params=_cp("parallel"))(a, b, c)


N_FT = D_FF // FFN_TN


def _ffn_specs():
    per = ROWS // FFN_HALO
    cur_g = pl.BlockSpec((ROWS, FFN_TN), lambda j, i: (i, j))
    cur_v = pl.BlockSpec((ROWS, FFN_TN), lambda j, i: (i, j + N_FT))
    halo_g = pl.BlockSpec((FFN_HALO, FFN_TN), lambda j, i: (jnp.maximum(i * per - 1, 0), j))
    halo_v = pl.BlockSpec((FFN_HALO, FFN_TN), lambda j, i: (jnp.maximum(i * per - 1, 0), j + N_FT))
    w_g = pl.BlockSpec((FFN_K, FFN_TN), lambda j, i: (0, j))
    w_v = pl.BlockSpec((FFN_K, FFN_TN), lambda j, i: (0, j + N_FT))
    b_g = pl.BlockSpec((1, FFN_TN), lambda j, i: (0, j))
    b_v = pl.BlockSpec((1, FFN_TN), lambda j, i: (0, j + N_FT))
    return [cur_g, cur_v, halo_g, halo_v, w_g, w_v, b_g, b_v]


def matmul(a, b, kind, out_dtype, tm, tn, name):
    if kind == "nn":
        (m, k), n = a.shape, b.shape[1]
        a_spec = pl.BlockSpec((tm, k), lambda j, i: (i, 0))
        b_spec = pl.BlockSpec((k, tn), lambda j, i: (0, j))
        dims = (((1,), (0,)), ((), ()))
    elif kind == "nt":
        (m, k), n = a.shape, b.shape[0]
        a_spec = pl.BlockSpec((tm, k), lambda j, i: (i, 0))
        b_spec = pl.BlockSpec((tn, k), lambda j, i: (j, 0))
        dims = (((1,), (1,)), ((), ()))
    else:
        (k, m), n = a.shape, b.shape[1]
        a_spec = pl.BlockSpec((k, tm), lambda j, i: (0, i))
        b_spec = pl.BlockSpec((k, tn), lambda j, i: (0, j))
        dims = (((0,), (0,)), ((), ()))
    assert m % tm == 0 and n % tn == 0, (name, m, n, tm, tn)

    def body(a_ref, b_ref, o_ref):
        o_ref[...] = lax.dot_general(a_ref[...], b_ref[...], dims, preferred_element_type=F32).astype(o_ref.dtype)

    return pl.pallas_call(
        body, out_shape=jax.ShapeDtypeStruct((m, n), out_dtype), grid=(n // tn, m // tm),
        in_specs=[a_spec, b_spec], out_specs=pl.BlockSpec((tm, tn), lambda j, i: (i, j)),
        name=name, compiler_params=_cp("parallel", "parallel"))(a, b)


def matmul_halves(a, b, tm, tn, name):
    _, m, k = a.shape
    n = b.shape[1]
    b3 = b.reshape(2, k, n)

    def body(a_ref, b_ref, o_ref):
        o_ref[...] = (jnp.dot(a_ref[0], b_ref[0], preferred_element_type=F32)
                      + jnp.dot(a_ref[1], b_ref[1], preferred_element_type=F32))

    return pl.pallas_call(
        body, out_shape=jax.ShapeDtypeStruct((m, n), F32), grid=(n // tn, m // tm),
        in_specs=[pl.BlockSpec((2, tm, k), lambda j, i: (0, i, 0)), pl.BlockSpec((2, k, tn), lambda j, i: (0, 0, j))],
        out_specs=pl.BlockSpec((tm, tn), lambda j, i: (i, j)), name=name, compiler_params=_cp("parallel", "parallel"))(a, b3)


def matmul_tn_halves(a, b, tm, name):
    _, k, m = a.shape
    n = b.shape[1]

    def body(a_ref, b_ref, o_ref):
        o_ref[0] = lax.dot_general(a_ref[0], b_ref[...], (((0,), (0,)), ((), ())), preferred_element_type=F32).astype(BF16)

    return pl.pallas_call(
        body, out_shape=jax.ShapeDtypeStruct((2, m, n), BF16), grid=(2, m // tm),
        in_specs=[pl.BlockSpec((1, k, tm), lambda h, i: (h, 0, i)), pl.BlockSpec((k, n), lambda h, i: (0, 0))],
        out_specs=pl.BlockSpec((1, tm, n), lambda h, i: (h, i, 0)), name=name,
        compiler_params=_cp("parallel", "parallel"))(a, b).reshape(2 * m, n)


def _row_spec(width, col=0):
    return pl.BlockSpec((ROWS, width), lambda i: (i, col))


def _vec_spec(width, rows=1):
    return pl.BlockSpec((rows, width), lambda i: (0, 0))


def _ffn_shifted(cur_ref, halo_ref, pad, s1, s2):
    i = pl.program_id(1)
    pad[0:FFN_HALO, :] = jnp.where(i > 0, halo_ref[...], 0.0)
    pad[FFN_HALO:, :] = cur_ref[0:FFN_HALO, :]
    for k, dst in ((1, s1), (2, s2)):
        dst[0:FFN_HALO, :] = pad[pl.ds(FFN_HALO - k, FFN_HALO), :]
        dst[FFN_HALO:, :] = cur_ref[pl.ds(FFN_HALO - k, ROWS - FFN_HALO), :]


def _ffn_conv(rs, cur_ref, s1, s2, w_ref, b_ref):
    return b_ref[...] + w_ref[0:1, :] * s2[rs, :] + w_ref[1:2, :] * s1[rs, :] + w_ref[2:3, :] * cur_ref[rs, :]


def ffn_act_fwd(up0, wf, bf, name):
    def body(g_ref, v_ref, gh_ref, vh_ref, wg_ref, wv_ref, bg_ref, bv_ref, act_ref, pad, g1, g2, v1, v2):
        _ffn_shifted(g_ref, gh_ref, pad, g1, g2)
        _ffn_shifted(v_ref, vh_ref, pad, v1, v2)

        def chunk(rs):
            gate = _ffn_conv(rs, g_ref, g1, g2, wg_ref, bg_ref)
            val = _ffn_conv(rs, v_ref, v1, v2, wv_ref, bv_ref)
            act_ref[rs, :] = (gate * _sig(gate) * val).astype(BF16)

        _for_chunks(chunk)

    tile = pltpu.VMEM((ROWS, FFN_TN), F32)
    return pl.pallas_call(
        body, out_shape=jax.ShapeDtypeStruct((SEQ, D_FF), BF16), grid=(N_FT, SEQ // ROWS),
        in_specs=_ffn_specs(), out_specs=pl.BlockSpec((ROWS, FFN_TN), lambda j, i: (i, j)),
        scratch_shapes=[pltpu.VMEM((2 * FFN_HALO, FFN_TN), F32), tile, tile, tile, tile],
        name=name, compiler_params=_cp("parallel", "parallel"))(up0, up0, up0, up0, wf, wf, bf, bf)


def ffn_bwd(up0, dact, wf, bf, name):
    per = ROWS // FFN_HALO
    last = SEQ // FFN_HALO - 1

    def body(g_ref, v_ref, gh_ref, vh_ref, wg_ref, wv_ref, bg_ref, bv_ref, da_ref, gn_ref, vn_ref, dan_ref,
             out_ref, dbg_ref, dbv_ref, dwg_ref, dwv_ref, pad, g1, g2, v1, v2, dgp, dvp, acc):
        i = pl.program_id(1)
        first = i == 0
        _ffn_shifted(g_ref, gh_ref, pad, g1, g2)
        _ffn_shifted(v_ref, vh_ref, pad, v1, v2)
        acc[...] = jnp.zeros_like(acc)

        def grads(gate, val, da):
            s = _sig(gate)
            return da * val * (s * (1.0 + gate * (1.0 - s))), da * (gate * s)

        def chunk(rs):
            gate = _ffn_conv(rs, g_ref, g1, g2, wg_ref, bg_ref)
            val = _ffn_conv(rs, v_ref, v1, v2, wv_ref, bv_ref)
            dgate, dval = grads(gate, val, da_ref[rs, :])
            dgp[rs, :] = dgate
            dvp[rs, :] = dval
            acc[0] += dgate
            acc[1] += dval
            for t, (sg, sv) in enumerate(((g2, v2), (g1, v1), (g_ref, v_ref))):
                acc[2 + t] += dgate * sg[rs, :]
                acc[5 + t] += dval * sv[rs, :]

        _for_chunks(chunk)
        _acc(dbg_ref, _rsum(acc[0]), first)
        _acc(dbv_ref, _rsum(acc[1]), first)
        _acc(dwg_ref, jnp.concatenate([_rsum(acc[2 + t]) for t in range(FFN_K)], axis=0), first)
        _acc(dwv_ref, jnp.concatenate([_rsum(acc[5 + t]) for t in range(FFN_K)], axis=0), first)

        def conv_next(cur_ref, nxt_ref, w_ref, b_ref):
            pad[0:FFN_HALO, :] = cur_ref[ROWS - FFN_HALO:, :]
            pad[FFN_HALO:, :] = nxt_ref[...]
            return (b_ref[...] + w_ref[0:1, :] * pad[pl.ds(FFN_HALO - 2, FFN_HALO), :]
                    + w_ref[1:2, :] * pad[pl.ds(FFN_HALO - 1, FFN_HALO), :] + w_ref[2:3, :] * nxt_ref[...])

        gate_n = conv_next(g_ref, gn_ref, wg_ref, bg_ref)
        val_n = conv_next(v_ref, vn_ref, wv_ref, bv_ref)
        dgate_n, dval_n = grads(gate_n, val_n, dan_ref[...])
        inside = i < SEQ // ROWS - 1
        dgp[ROWS:, :] = jnp.where(inside, dgate_n, 0.0)
        dvp[ROWS:, :] = jnp.where(inside, dval_n, 0.0)

        for half, (dp, s1, s2, w_ref) in enumerate(((dgp, g1, g2, wg_ref), (dvp, v1, v2, wv_ref))):
            s1[...] = dp[pl.ds(1, ROWS), :]
            s2[...] = dp[pl.ds(2, ROWS), :]

            def back(rs, dp=dp, s1=s1, s2=s2, w_ref=w_ref, half=half):
                out_ref[half, rs, :] = (w_ref[2:3, :] * dp[rs, :] + w_ref[1:2, :] * s1[rs, :]
                                        + w_ref[0:1, :] * s2[rs, :]).astype(BF16)

            _for_chunks(back)

    tile = pltpu.VMEM((ROWS, FFN_TN), F32)
    ext = pltpu.VMEM((ROWS + FFN_HALO, FFN_TN), F32)
    vec = jax.ShapeDtypeStruct((1, D_FF), F32)
    taps = jax.ShapeDtypeStruct((FFN_K, D_FF), F32)
    cur = pl.BlockSpec((ROWS, FFN_TN), lambda j, i: (i, j))
    nxt = lambda off: pl.BlockSpec((FFN_HALO, FFN_TN), lambda j, i: (jnp.minimum((i + 1) * per, last), j + off))
    vs = pl.BlockSpec((1, FFN_TN), lambda j, i: (0, j))
    ts = pl.BlockSpec((FFN_K, FFN_TN), lambda j, i: (0, j))
    return pl.pallas_call(
        body, out_shape=(jax.ShapeDtypeStruct((2, SEQ, D_FF), BF16), vec, vec, taps, taps), grid=(N_FT, SEQ // ROWS),
        in_specs=_ffn_specs() + [cur, nxt(0), nxt(N_FT), nxt(0)],
        out_specs=(pl.BlockSpec((2, ROWS, FFN_TN), lambda j, i: (0, i, j)), vs, vs, ts, ts),
        scratch_shapes=[pltpu.VMEM((2 * FFN_HALO, FFN_TN), F32), tile, tile, tile, tile, ext, ext,
                        pltpu.VMEM((2 + 2 * FFN_K, SUB, FFN_TN), F32)],
        name=name, compiler_params=_cp("parallel", "arbitrary"))(up0, up0, up0, up0, wf, wf, bf, bf, dact, up0, up0, dact)


def ada_fwd(c_all, w_ada, b_cols, name):
    def body(c_ref, w_ref, b_ref, o_ref):
        cc = c_ref[...]
        sc = (cc * _sig(cc)).astype(BF16)
        o_ref[...] = jnp.dot(sc, w_ref[...].astype(BF16), preferred_element_type=F32) + b_ref[...]

    return pl.pallas_call(body, out_shape=jax.ShapeDtypeStruct((N_DEV, w_ada.shape[1]), F32), name=name,
                          compiler_params=_cp())(c_all, w_ada, b_cols)


def _adam(w, g, m, v):
    m = ADAM_B1 * m + (1.0 - ADAM_B1) * g
    v = ADAM_B2 * v + (1.0 - ADAM_B2) * (g * g)
    m_hat = m / (1.0 - ADAM_B1 ** ADAM_STEP)
    v_hat = v / (1.0 - ADAM_B2 ** ADAM_STEP)
    delta = -ADAM_LR * (m_hat / (jnp.sqrt(v_hat) + ADAM_EPS) + ADAM_WD * w)
    return delta, m, v


def ada_bwd_adamw(c_all_t, dmod_cols, w, m, v, name):
    rows, cols = w.shape
    tr = 256

    def body(ct_ref, dm_ref, w_ref, m_ref, v_ref, g_ref, d_ref, nm_ref, nv_ref):
        ct = ct_ref[...]
        sc = ct * _sig(ct)
        g = sc[:, 0:1] * dm_ref[0:1, :]
        for b in range(1, N_DEV):
            g = g + sc[:, b:b + 1] * dm_ref[b:b + 1, :]
        g_ref[...] = g
        d_ref[...], nm_ref[...], nv_ref[...] = _adam(w_ref[...], g, m_ref[...], v_ref[...])

    blk = pl.BlockSpec((tr, cols), lambda i: (i, 0))
    shp = jax.ShapeDtypeStruct((rows, cols), F32)
    return pl.pallas_call(
        body, out_shape=(shp, shp, shp, shp), grid=(rows // tr,),
        in_specs=[pl.BlockSpec((tr, N_DEV), lambda i: (i, 0)), pl.BlockSpec((N_DEV, cols), lambda i: (0, 0)), blk, blk, blk],
        out_specs=(blk, blk, blk, blk), name=name, compiler_params=_cp("parallel"))(c_all_t, dmod_cols, w, m, v)


def sum_adamw(parts, own, w, m, v, tr, name):
    n_parts, rows, cols = parts.shape

    def body(*refs):
        if own is None:
            p_ref, w_ref, m_ref, v_ref, g_ref, d_ref, nm_ref, nv_ref = refs
            g = p_ref[0].astype(F32)
        else:
            p_ref, own_ref, w_ref, m_ref, v_ref, g_ref, d_ref, nm_ref, nv_ref = refs
            g = own_ref[...].astype(F32)
        for k in range(1, n_parts):
            g = g + p_ref[k].astype(F32)
        g_ref[...] = g
        d_ref[...], nm_ref[...], nv_ref[...] = _adam(w_ref[...], g, m_ref[...], v_ref[...])

    blk = pl.BlockSpec((tr, cols), lambda i: (i, 0))
    shp = jax.ShapeDtypeStruct((rows, cols), F32)
    args = [parts] + ([] if own is None else [own]) + [w, m, v]
    return pl.pallas_call(
        body, out_shape=(shp, shp, shp, shp), grid=(rows // tr,),
        in_specs=[pl.BlockSpec((n_parts, tr, cols), lambda i: (0, i, 0))] + [blk] * (len(args) - 1),
        out_specs=(blk, blk, blk, blk), name=name, compiler_params=_cp("parallel"))(*args)


MESH = pl.DeviceIdType.MESH
ANY = pl.BlockSpec(memory_space=pl.ANY)


def all_gather(block, name, after=None):
    extra = () if after is None else (after,)

    def body(x_ref, *refs):
        out_ref, send_sems, recv_sems, local_sem = refs[len(extra):]
        x, y, c = lax.axis_index("x"), lax.axis_index("y"), lax.axis_index("c")
        me, sibling = (x, y, c), (x, y, 1 - c)
        chips = [(1 - x, y), (x, 1 - y), (1 - x, 1 - y)]

        def slot(px, py, pc):
            return out_ref.at[4 * px + 2 * py + pc]

        def copy(k, blk, to, src=None):
            return pltpu.make_async_remote_copy(
                src_ref=slot(*blk) if src is None else src, dst_ref=slot(*blk),
                send_sem=send_sems.at[k], recv_sem=recv_sems.at[k], device_id=to, device_id_type=MESH)

        mine = pltpu.make_async_copy(x_ref, slot(*me), local_sem)
        mine.start()
        first = [copy(0, me, sibling, src=x_ref)]
        first += [copy(1 + j, me, (*chip, c), src=x_ref) for j, chip in enumerate(chips)]
        for cp in first:
            cp.start()
        passed = [copy(4 + j, (*chip, c), sibling) for j, chip in enumerate(chips)]
        for j, chip in enumerate(chips):
            copy(1 + j, (*chip, c), me).wait_recv()
            passed[j].start()
        copy(0, sibling, me).wait_recv()
        for j, chip in enumerate(chips):
            copy(4 + j, (*chip, 1 - c), me).wait_recv()
        for cp in first + passed:
            cp.wait_send()
        mine.wait()

    return pl.pallas_call(
        body, out_shape=jax.ShapeDtypeStruct((N_DEV,) + block.shape, block.dtype), in_specs=[ANY] * (1 + len(extra)), out_specs=ANY,
        scratch_shapes=[pltpu.SemaphoreType.DMA((7,)), pltpu.SemaphoreType.DMA((7,)), pltpu.SemaphoreType.DMA],
        name=name)(block, *extra)


HBM = pl.BlockSpec(memory_space=pltpu.HBM)
SEM = pl.BlockSpec(memory_space=pltpu.SEMAPHORE)
EFFECT = pltpu.SideEffectType.DATAFLOW_SIDE_EFFECTING


def _peer_copies(src_ref, land_ref, send_sems, recv_sems, gather):
    x, y, c = lax.axis_index("x"), lax.axis_index("y"), lax.axis_index("c")
    me = 4 * x + 2 * y + c
    copies = []
    for k in range(1, N_DEV):
        px = 1 - x if k & 4 else x
        py = 1 - y if k & 2 else y
        pc = 1 - c if k & 1 else c
        copies.append(pltpu.make_async_remote_copy(
            src_ref=src_ref if gather else src_ref.at[4 * px + 2 * py + pc],
            dst_ref=land_ref.at[me] if gather else land_ref.at[k],
            send_sem=send_sems.at[k - 1], recv_sem=recv_sems.at[k - 1], device_id=(px, py, pc), device_id_type=MESH))
    return copies


def exchange_start(src, gather, name, after=None):
    land_shape = (N_DEV,) + src.shape if gather else src.shape
    extra = () if after is None else (after,)

    def body(src_ref, land_ref, *refs):
        send_sems, recv_sems, src_thru, land_thru, token = refs[len(extra):]
        for cp in _peer_copies(src_ref, land_ref, send_sems, recv_sems, gather):
            cp.start()
        token[...] = jnp.zeros_like(token)

    send_sems, recv_sems, src_thru, land_thru, token = pl.pallas_call(
        body, name=name,
        out_shape=(pltpu.SemaphoreType.DMA((N_DEV - 1,)), pltpu.SemaphoreType.DMA((N_DEV - 1,)),
                   pltpu.HBM(src.shape, src.dtype), pltpu.HBM(land_shape, src.dtype), jax.ShapeDtypeStruct((8, 128), F32)),
        in_specs=(HBM, HBM) + (ANY,) * len(extra), out_specs=(SEM, SEM, HBM, HBM, pl.BlockSpec(memory_space=pltpu.VMEM)),
        input_output_aliases={0: 2, 1: 3}, compiler_params=pltpu.CompilerParams(has_side_effects=EFFECT),
    )(pltpu.with_memory_space_constraint(src, pltpu.HBM),
      pltpu.with_memory_space_constraint(lax.empty(land_shape, src.dtype), pltpu.HBM), *extra)
    return (send_sems, recv_sems, src_thru, land_thru), token[0, 0]


def exchange_wait(handles, after, gather, name):
    send_sems, recv_sems, src_thru, land_thru = handles

    def body(src_ref, land_ref, send_sems, recv_sems, after_ref, src_dead, got_ref, local_sem):
        me = 4 * lax.axis_index("x") + 2 * lax.axis_index("y") + lax.axis_index("c")
        if gather:
            mine = pltpu.make_async_copy(src_ref, land_ref.at[me], local_sem)
        else:
            mine = pltpu.make_async_copy(src_ref.at[me], land_ref.at[0], local_sem)
        mine.start()
        for cp in _peer_copies(src_ref, land_ref, send_sems, recv_sems, gather):
            cp.wait_send()
            cp.wait_recv()
        mine.wait()

    return pl.pallas_call(
        body, name=name,
        out_shape=(pltpu.HBM(src_thru.shape, src_thru.dtype), pltpu.HBM(land_thru.shape, land_thru.dtype)),
        in_specs=(HBM, HBM, SEM, SEM, ANY), out_specs=(HBM, HBM), input_output_aliases={0: 0, 1: 1},
        scratch_shapes=[pltpu.SemaphoreType.DMA],
        compiler_params=pltpu.CompilerParams(has_side_effects=EFFECT),
    )(src_thru, land_thru, send_sems, recv_sems, after)[1]


def _to_pattern(a, r):
    return a.reshape(SEQ // r, r * a.shape[1])


def local_step(x, tgt, mod, get_w, put_grad, wc, wf, g_mix, bc, lg, lb, gco, gao, g_ffn, bf, g_fin):
    h1 = rms_mod_fwd(x, g_mix, mod, 0, 1, "h1_fwd")
    w_in = get_w("w_in", h1)
    proj = matmul(h1, w_in, "nt", F32, 512, D_IN, "proj_fwd")
    mix_a = conv_module_fwd(proj, wc, bc, lg, lb, gco, "conv_module_fwd")
    att, lse = attn_fwd_all(proj, "attn_fwd")
    mix_b = rms_gain_bf16(att, gao, "attn_out_norm")
    mixed = jnp.concatenate([mix_a, mix_b], axis=1)
    w_out = get_w("w_out", mixed)
    y1 = matmul(mixed, w_out, "nn", F32, 512, D_MODEL, "out_proj_fwd")
    x1, h2 = resid_rms_mod_fwd(x, y1, g_ffn, mod, 2, 3, 4, "x1_h2_fwd")
    w_up = get_w("w_up", h2)
    up0 = matmul(h2, w_up, "nt", F32, 512, D_FF, "up_fwd")
    act = ffn_act_fwd(up0, wf, bf, "ffn_act_fwd")
    w_down = get_w("w_down", act)
    y2 = matmul(act, w_down, "nn", F32, 512, D_MODEL, "down_fwd")
    loss_t, dx2, dy2, d_gfin, d_gaf = final_loss_bwd(x1, y2, tgt, g_fin, mod, 5, "loss_bwd")
    dact = matmul(dy2, w_down, "nt", F32, 512, FFN_TN, "down_bwd_x")
    dw_down = matmul(act, dy2, "tn", BF16, 256, D_MODEL, "down_bwd_w")
    dup0, dbf_g, dbf_v, dwf_g, dwf_v = ffn_bwd(up0, dact, wf + put_grad("w_down", dw_down), bf, "ffn_bwd")
    dh2 = matmul_halves(dup0, w_up, 512, 512, "up_bwd_x")
    dw_up = matmul_tn_halves(dup0, h2, 256, "up_bwd_w")
    dx1, d_shf, d_scf, d_gffn, dy1, d_gam = rms_mod_bwd(x1, dh2, dx2, g_ffn + put_grad("w_up", dw_up), mod, 4, y1, 2, "h2_bwd")
    dmixed = matmul(dy1, w_out, "nt", F32, 512, D_MODEL, "out_proj_bwd_x")
    dw_out = matmul(mixed, dy1, "tn", BF16, 256, D_MODEL, "out_proj_bwd_w")
    do, dd, d_gao = attn_combine_bwd(dmixed, att, gao + put_grad("w_out", dw_out), "attn_combine_bwd")
    dqkv = attn_bwd_all(proj, do, lse, dd, "attn_bwd")
    du1, d_gco, d_lg, d_lb, d_bc, d_wc = conv_module_bwd_a(proj, dmixed, wc, bc, lg, lb, gco, "conv_module_bwd_a")
    dproj_a = conv_module_bwd_b(proj, du1, wc, "conv_module_bwd_b")
    dproj = jnp.concatenate([dproj_a, dqkv[0], dqkv[1], dqkv[2]], axis=1)
    dw_in = matmul(dproj, h1, "tn", BF16, 512, D_MODEL, "proj_bwd_w")
    dh1 = matmul(dproj, w_in, "nn", F32, 512, D_MODEL, "proj_bwd_x")
    dx, d_shm, d_scm, d_gmix = rms_mod_bwd(x, dh1, dx1, g_mix + put_grad("w_in", dw_in), mod, 1, None, 0, "h1_bwd")
    dmod = jnp.concatenate([d_shm, d_scm, d_gam, d_shf, d_scf, d_gaf], axis=1)
    small = dict(g_norm_mix=d_gmix, b_conv_dw=d_bc, ln_conv_g=d_lg, ln_conv_b=d_lb, g_conv_out=d_gco, g_attn_out=d_gao,
                 g_norm_ffn=d_gffn, b_ffn_dw=jnp.concatenate([dbf_g, dbf_v], axis=1), g_final=d_gfin,
                 w_conv_dw=d_wc, w_ffn_dw=jnp.concatenate([dwf_g, dwf_v], axis=1), dmod=dmod, loss=loss_t[0:1, 0:1])
    return dx, small


def _padw(a, width):
    return jnp.pad(a, ((0, 0), (0, width - a.shape[1])))


def pack_small(t):
    wide = jnp.concatenate([_padw(t["dmod"], PACK_W), _padw(t["b_ffn_dw"], PACK_W), _padw(t["w_ffn_dw"], PACK_W),
                            _padw(t["loss"], PACK_W), jnp.zeros((2, PACK_W), F32)], axis=0)
    z512 = jnp.zeros((1, 512), F32)
    narrow = jnp.concatenate([
        t["g_norm_mix"], t["g_norm_ffn"], t["g_final"],
        jnp.concatenate([t["b_conv_dw"], t["ln_conv_g"]], axis=1),
        jnp.concatenate([t["ln_conv_b"], t["g_conv_out"]], axis=1),
        jnp.concatenate([t["g_attn_out"], z512], axis=1),
        jnp.zeros((2, 1024), F32),
        jnp.pad(t["w_conv_dw"], ((0, 1), (0, 0))).reshape(16, 1024)], axis=0)
    return jnp.concatenate([wide, narrow.reshape(4, PACK_W), jnp.zeros((4, PACK_W), F32)], axis=0)


def unpack_small(p):
    narrow = p[8:12].reshape(24, 1024)
    return dict(
        dmod=p[0:1], b_ffn_dw=p[1:2, :2 * D_FF], w_ffn_dw=p[2:5, :2 * D_FF], loss=p[5, 0],
        g_norm_mix=narrow[0:1], g_norm_ffn=narrow[1:2], g_final=narrow[2:3],
        b_conv_dw=narrow[3:4, :512], ln_conv_g=narrow[3:4, 512:], ln_conv_b=narrow[4:5, :512], g_conv_out=narrow[4:5, 512:],
        g_attn_out=narrow[5:6, :512], w_conv_dw=narrow[8:24].reshape(32, 512)[:CONV_K])


def _embed(local, width, me):
    return lax.dynamic_update_slice(jnp.zeros((local.shape[0], width), F32), local, (0, me * local.shape[1]))


def _shard(full, n_cols, me):
    return lax.dynamic_slice(full, (0, me * n_cols), (full.shape[0], n_cols))


WEIGHTS = ["w_ada", "b_ada", "g_norm_mix", "w_in", "w_conv_dw", "b_conv_dw", "ln_conv_g", "ln_conv_b", "g_conv_out",
           "g_attn_out", "w_out", "g_norm_ffn", "w_up", "w_ffn_dw", "b_ffn_dw", "w_down", "g_final"]
SMALL_REPLICATED = ["g_norm_mix", "b_conv_dw", "ln_conv_g", "ln_conv_b", "g_conv_out", "g_attn_out", "g_norm_ffn",
                    "b_ffn_dw", "g_final"]


def kernel(x, c, w_ada, b_ada, g_norm_mix, w_in, w_conv_dw, b_conv_dw, ln_conv_g, ln_conv_b, g_conv_out, g_attn_out, w_out, g_norm_ffn, w_up, w_ffn_dw, b_ffn_dw, w_down, g_final, loss_target, m_w_ada, m_b_ada, m_g_norm_mix, m_w_in, m_w_conv_dw, m_b_conv_dw, m_ln_conv_g, m_ln_conv_b, m_g_conv_out, m_g_attn_out, m_w_out, m_g_norm_ffn, m_w_up, m_w_ffn_dw, m_b_ffn_dw, m_w_down, m_g_final, v_w_ada, v_b_ada, v_g_norm_mix, v_w_in, v_w_conv_dw, v_b_conv_dw, v_ln_conv_g, v_ln_conv_b, v_g_conv_out, v_g_attn_out, v_w_out, v_g_norm_ffn, v_w_up, v_w_ffn_dw, v_b_ffn_dw, v_w_down, v_g_final):
    args = dict(locals())
    me = 4 * lax.axis_index("x") + 2 * lax.axis_index("y") + lax.axis_index("c")

    def flat(name, prefix=""):
        a = args[prefix + name]
        return a.reshape(a.shape[-2] if a.ndim > 1 else 1, a.shape[-1])

    n_in, n_up, r_out, r_down = w_in.shape[2], w_up.shape[2], w_out.shape[1], w_down.shape[1]
    n_ada, n_wc, n_wf = w_ada.shape[2], w_conv_dw.shape[2], w_ffn_dw.shape[2]
    taps_c = jnp.pad(flat("w_conv_dw").reshape(1, CONV_K * n_wc), ((0, 0), (0, 2 * D_MODEL - CONV_K * n_wc)))
    taps_f = jnp.pad(flat("w_ffn_dw").reshape(1, FFN_K * n_wf), ((0, 0), (0, 3 * D_MODEL - FFN_K * n_wf)))
    first = jnp.concatenate([c, taps_c.reshape(2, D_MODEL), taps_f.reshape(3, D_MODEL), jnp.zeros((2, D_MODEL), F32)], axis=0)
    first_all = all_gather(first, "gather_c_taps")
    c_all = first_all[:, 0, :]
    wc_full = first_all[:, 1:3, :].reshape(N_DEV, 2 * D_MODEL)[:, :CONV_K * n_wc].reshape(N_DEV, CONV_K, n_wc)
    wc_full = wc_full.transpose(1, 0, 2).reshape(CONV_K, D_CONV)
    wf_full = first_all[:, 3:6, :].reshape(N_DEV, 3 * D_MODEL)[:, :FFN_K * n_wf].reshape(N_DEV, FFN_K, n_wf)
    wf_full = wf_full.transpose(1, 0, 2).reshape(FFN_K, 2 * D_FF)
    mod_cols = ada_fwd(c_all, flat("w_ada"), _shard(flat("b_ada"), n_ada, me), "ada_fwd")
    mod_all = all_gather(mod_cols, "gather_mod")
    mod = lax.dynamic_index_in_dim(mod_all, me, axis=1, keepdims=False).reshape(N_MOD, D_MODEL)
    mod = jnp.pad(mod, ((0, 2), (0, 0)))

    def flat_t(name, prefix=""):
        return args[prefix + name][0].T

    blocks = dict(w_in=flat_t("w_in").astype(BF16), w_up=flat_t("w_up").astype(BF16),
                  rows=jnp.concatenate([flat("w_out"), flat("w_down")], axis=0).astype(BF16))
    gathers, tok = {}, None
    for name in ("w_in", "rows", "w_up"):
        src = blocks[name] if tok is None else blocks[name] + tok.astype(BF16)
        gathers[name], tok = exchange_start(src, True, f"gather_{name}_start", mod_all)
    mod = mod + tok
    full = {}

    def gathered(name, after):
        return exchange_wait(gathers[name], after, True, f"gather_{name}_wait")

    def get_w(name, after):
        if name in ("w_out", "w_down"):
            if "rows" not in full:
                full["rows"] = gathered("rows", after)
            rows = full["rows"]
            return rows[:, :r_out, :].reshape(D_MODEL, D_MODEL) if name == "w_out" else rows[:, r_out:, :].reshape(D_FF, D_MODEL)
        return gathered(name, after).reshape(-1, D_MODEL)

    exchanges = {}

    def put_grad(name, dw, after=None):
        dev_major = dw.reshape(N_DEV, -1, D_MODEL)
        exchanges[name], token = exchange_start(dev_major, False, f"exchange_{name}_start", after)
        return token

    grad_x, small = local_step(
        x[0], loss_target[0], mod, get_w, put_grad, wc_full, wf_full,
        flat("g_norm_mix"), flat("b_conv_dw"), flat("ln_conv_g"), flat("ln_conv_b"), flat("g_conv_out"),
        flat("g_attn_out"), flat("g_norm_ffn"), flat("b_ffn_dw"), flat("g_final"))

    out = {}

    def finish(name, tr, after):
        parts = exchange_wait(exchanges[name], after, False, f"exchange_{name}_wait")
        if name in ("w_in", "w_up"):
            res = sum_adamw(parts, None, flat_t(name), flat_t(name, "m_"), flat_t(name, "v_"), tr, "adamw_" + name)
            out[name] = tuple(r.T for r in res)
        else:
            res = out[name] = sum_adamw(parts, None, flat(name), flat(name, "m_"), flat(name, "v_"), tr, "adamw_" + name)
        return res[0]

    after = finish("w_down", r_down, grad_x)
    after = finish("w_up", n_up // 2, after)
    after = finish("w_out", r_out, after)
    after = finish("w_in", n_in, after)

    small_all = all_gather(pack_small(small), "gather_small", after)

    def packed(prefix):
        t = {n: flat(n, prefix) for n in SMALL_REPLICATED}
        t["dmod"] = flat("b_ada", prefix)
        t["loss"] = jnp.zeros((1, 1), F32)
        t["w_conv_dw"] = _embed(flat("w_conv_dw", prefix), D_CONV, me)
        t["w_ffn_dw"] = _embed(flat("w_ffn_dw", prefix), 2 * D_FF, me)
        return pack_small(t)

    res = sum_adamw(small_all, None, packed(""), packed("m_"), packed("v_"), PACK_ROWS, "adamw_small")
    res = [unpack_small(r) for r in res]
    loss = res[0]["loss"]
    for n in SMALL_REPLICATED:
        out[n] = tuple(r[n] for r in res)
    out["b_ada"] = tuple(r["dmod"] for r in res)
    out["w_conv_dw"] = tuple(_shard(r["w_conv_dw"], n_wc, me) for r in res)
    out["w_ffn_dw"] = tuple(_shard(r["w_ffn_dw"], n_wf, me) for r in res)

    dmod_cols = _shard(small_all[:, 0, :], n_ada, me)
    out["w_ada"] = ada_bwd_adamw(c_all.T, dmod_cols, flat("w_ada"), flat("w_ada", "m_"), flat("w_ada", "v_"), "adamw_w_ada")

    result = [loss, grad_x[None]]
    for k in range(4):
        result += [out[n][k].reshape(args[n].shape) for n in WEIGHTS]
    return tuple(result)
```

```python
import functools

import jax
import jax.numpy as jnp
from jax import lax
from jax.experimental import pallas as pl
from jax.experimental.pallas import tpu as pltpu

F32 = jnp.float32
BF16 = jnp.bfloat16

N_DEV = 8
SEQ = 2048
D_MODEL = 1024
D_CONV = 512
D_ATTN = 512
HEAD_DIM = 64
CONV_K = 31
D_FF = 2816
FFN_K = 3
D_IN = 2 * D_CONV + 3 * D_ATTN
N_MOD = 6
EPS = 1e-6
ATTN_BLOCK = 128
PATTERNS = ((2048, 1), (512, 4), (128, 16))
NEG = -1e30

ADAM_LR, ADAM_B1, ADAM_B2, ADAM_EPS, ADAM_WD, ADAM_STEP = 0.001, 0.9, 0.999, 1e-08, 0.01, 10

ROWS = 256
CONV_HALO = 32
FFN_HALO = 8
FFN_TN = 1408
VMEM_LIMIT = 56 * 1024 * 1024
PACK_ROWS, PACK_W = 16, 6144


def _cp(*sem):
    return pltpu.CompilerParams(dimension_semantics=sem if sem else None, vmem_limit_bytes=VMEM_LIMIT)


def _sig(x):
    return 1.0 / (1.0 + jnp.exp(-x))


def _rsum(x):
    return jnp.sum(x, axis=0, keepdims=True)


def _mean(x):
    return jnp.mean(x, axis=-1, keepdims=True)


def _acc(ref, val, first):
    @pl.when(first)
    def _():
        ref[...] = val

    @pl.when(jnp.logical_not(first))
    def _():
        ref[...] += val


SUB = 16


def _for_chunks(fn, unroll=1):
    def step(i, carry):
        fn(pl.ds(pl.multiple_of(i * SUB, SUB), SUB))
        return carry

    lax.fori_loop(0, ROWS // SUB, step, 0, unroll=unroll)


def rms_mod_fwd(x, g, mod, sh_row, sc_row, name):
    def body(x_ref, g_ref, mod_ref, h_ref):
        xx = x_ref[...]
        r = lax.rsqrt(_mean(xx * xx) + EPS)
        h = xx * r * g_ref[...]
        h_ref[...] = (h * (1.0 + mod_ref[sc_row:sc_row + 1, :]) + mod_ref[sh_row:sh_row + 1, :]).astype(BF16)

    return pl.pallas_call(
        body, out_shape=jax.ShapeDtypeStruct((SEQ, D_MODEL), BF16), grid=(SEQ // ROWS,),
        in_specs=[_row_spec(D_MODEL), _vec_spec(D_MODEL), _vec_spec(D_MODEL, 8)],
        out_specs=_row_spec(D_MODEL), name=name, compiler_params=_cp("parallel"))(x, g, mod)


def resid_rms_mod_fwd(x, y, g, mod, ga_row, sh_row, sc_row, name):
    def body(x_ref, y_ref, g_ref, mod_ref, x1_ref, h_ref):
        x1 = x_ref[...] + mod_ref[ga_row:ga_row + 1, :] * y_ref[...]
        x1_ref[...] = x1
        r = lax.rsqrt(_mean(x1 * x1) + EPS)
        h = x1 * r * g_ref[...]
        h_ref[...] = (h * (1.0 + mod_ref[sc_row:sc_row + 1, :]) + mod_ref[sh_row:sh_row + 1, :]).astype(BF16)

    return pl.pallas_call(
        body, out_shape=(jax.ShapeDtypeStruct((SEQ, D_MODEL), F32), jax.ShapeDtypeStruct((SEQ, D_MODEL), BF16)),
        grid=(SEQ // ROWS,),
        in_specs=[_row_spec(D_MODEL), _row_spec(D_MODEL), _vec_spec(D_MODEL), _vec_spec(D_MODEL, 8)],
        out_specs=(_row_spec(D_MODEL), _row_spec(D_MODEL)), name=name, compiler_params=_cp("parallel"))(x, y, g, mod)


def final_loss_bwd(x1, y2, tgt, g, mod, ga_row, name):
    def body(x1_ref, y2_ref, t_ref, g_ref, mod_ref, loss_ref, dx2_ref, dy2_ref, dg_ref, dga_ref):
        first = pl.program_id(0) == 0
        ga = mod_ref[ga_row:ga_row + 1, :]
        y2 = y2_ref[...]
        x2 = x1_ref[...] + ga * y2
        r = lax.rsqrt(_mean(x2 * x2) + EPS)
        xn = x2 * r
        err = xn * g_ref[...] - t_ref[...]
        _acc(loss_ref, jnp.broadcast_to(0.5 * jnp.sum(_mean(err * err)), (8, 128)), first)
        dy = err * (1.0 / D_MODEL)
        _acc(dg_ref, _rsum(dy * xn), first)
        dxn = dy * g_ref[...]
        dx2 = r * (dxn - xn * _mean(dxn * xn))
        dx2_ref[...] = dx2
        dy2_ref[...] = (dx2 * ga).astype(BF16)
        _acc(dga_ref, _rsum(dx2 * y2), first)

    vec = jax.ShapeDtypeStruct((1, D_MODEL), F32)
    return pl.pallas_call(
        body,
        out_shape=(jax.ShapeDtypeStruct((8, 128), F32), jax.ShapeDtypeStruct((SEQ, D_MODEL), F32),
                   jax.ShapeDtypeStruct((SEQ, D_MODEL), BF16), vec, vec),
        grid=(SEQ // ROWS,),
        in_specs=[_row_spec(D_MODEL), _row_spec(D_MODEL), _row_spec(D_MODEL), _vec_spec(D_MODEL), _vec_spec(D_MODEL, 8)],
        out_specs=(pl.BlockSpec((8, 128), lambda i: (0, 0)), _row_spec(D_MODEL), _row_spec(D_MODEL),
                   _vec_spec(D_MODEL), _vec_spec(D_MODEL)),
        name=name, compiler_params=_cp("arbitrary"))(x1, y2, tgt, g, mod)


def rms_mod_bwd(x, dh, dres, g, mod, sc_row, y, ga_row, name):
    gated = y is not None

    def body(*refs):
        if gated:
            x_ref, dh_ref, dres_ref, g_ref, mod_ref, y_ref, dx_ref, dsh_ref, dsc_ref, dg_ref, dy_ref, dga_ref = refs
        else:
            x_ref, dh_ref, dres_ref, g_ref, mod_ref, dx_ref, dsh_ref, dsc_ref, dg_ref = refs
        first = pl.program_id(0) == 0
        xx = x_ref[...]
        dh = dh_ref[...]
        gg = g_ref[...]
        r = lax.rsqrt(_mean(xx * xx) + EPS)
        xn = xx * r
        _acc(dsh_ref, _rsum(dh), first)
        _acc(dsc_ref, _rsum(dh * (xn * gg)), first)
        dt = dh * (1.0 + mod_ref[sc_row:sc_row + 1, :])
        _acc(dg_ref, _rsum(dt * xn), first)
        dxn = dt * gg
        dx = dres_ref[...] + r * (dxn - xn * _mean(dxn * xn))
        dx_ref[...] = dx
        if gated:
            _acc(dga_ref, _rsum(dx * y_ref[...]), first)
            dy_ref[...] = (dx * mod_ref[ga_row:ga_row + 1, :]).astype(BF16)

    vec = jax.ShapeDtypeStruct((1, D_MODEL), F32)
    in_specs = [_row_spec(D_MODEL), _row_spec(D_MODEL), _row_spec(D_MODEL), _vec_spec(D_MODEL), _vec_spec(D_MODEL, 8)]
    out_shape = [jax.ShapeDtypeStruct((SEQ, D_MODEL), F32), vec, vec, vec]
    out_specs = [_row_spec(D_MODEL), _vec_spec(D_MODEL), _vec_spec(D_MODEL), _vec_spec(D_MODEL)]
    args = [x, dh, dres, g, mod]
    if gated:
        in_specs.append(_row_spec(D_MODEL))
        out_shape += [jax.ShapeDtypeStruct((SEQ, D_MODEL), BF16), vec]
        out_specs += [_row_spec(D_MODEL), _vec_spec(D_MODEL)]
        args.append(y)
    return pl.pallas_call(
        body, out_shape=tuple(out_shape), grid=(SEQ // ROWS,), in_specs=in_specs, out_specs=tuple(out_specs),
        name=name, compiler_params=_cp("arbitrary"))(*args)


def _prev_halo(halo, width, col):
    per = ROWS // halo
    return pl.BlockSpec((halo, width), lambda i: (jnp.maximum(i * per - 1, 0), col))


def _next_halo(halo, width, col):
    per = ROWS // halo
    last = SEQ // halo - 1
    return pl.BlockSpec((halo, width), lambda i: (jnp.minimum((i + 1) * per, last), col))


CONV_PAD = ROWS + CONV_HALO


def _shift_copies(sh):
    for b in range(1, 8):
        sh[b, 0:CONV_PAD - 8, :] = sh[0, pl.ds(b, CONV_PAD - 8), :]


def _tap(sh, rs_start, offset):
    return sh[offset % 8, pl.ds(pl.multiple_of(rs_start + (offset // 8) * 8, 8), SUB), :]


def _conv_module_forward(av_ref, ag_ref, avh_ref, agh_ref, wc_ref, bc_ref, lg_ref, lb_ref, sh, u1_s):
    i = pl.program_id(0)
    hv = avh_ref[...] * _sig(agh_ref[...])
    sh[0, 0:CONV_HALO, :] = jnp.where(i > 0, hv, 0.0)

    def glu(rs):
        sh[0, pl.ds(pl.multiple_of(rs.start + CONV_HALO, SUB), SUB), :] = av_ref[rs, :] * _sig(ag_ref[rs, :])

    _for_chunks(glu)
    _shift_copies(sh)

    def conv(rs):
        u1 = jnp.broadcast_to(bc_ref[...], (SUB, D_CONV))
        for j in range(CONV_K):
            u1 = u1 + wc_ref[j:j + 1, :] * _tap(sh, rs.start, CONV_HALO - (CONV_K - 1) + j)
        u1_s[rs, :] = u1

    _for_chunks(conv)
    u1 = u1_s[...]
    mu = _mean(u1)
    cen = u1 - mu
    rs = lax.rsqrt(_mean(cen * cen) + EPS)
    z = cen * rs
    ln = z * lg_ref[...] + lb_ref[...]
    s = _sig(ln)
    return z, rs, ln, s, ln * s


_CONV_SCRATCH = [pltpu.VMEM((8, CONV_PAD, D_CONV), F32), pltpu.VMEM((ROWS, D_CONV), F32)]


def conv_module_fwd(proj, wc, bc, lg, lb, gco, name):
    def body(av_ref, ag_ref, avh_ref, agh_ref, wc_ref, bc_ref, lg_ref, lb_ref, gco_ref, out_ref, sh, u1_s):
        _, _, _, _, u2 = _conv_module_forward(av_ref, ag_ref, avh_ref, agh_ref, wc_ref, bc_ref, lg_ref, lb_ref, sh, u1_s)
        rc = lax.rsqrt(_mean(u2 * u2) + EPS)
        out_ref[...] = (u2 * rc * gco_ref[...]).astype(BF16)

    v = _vec_spec(D_CONV)
    return pl.pallas_call(
        body, out_shape=jax.ShapeDtypeStruct((SEQ, D_CONV), BF16), grid=(SEQ // ROWS,),
        in_specs=[_row_spec(D_CONV, 0), _row_spec(D_CONV, 1), _prev_halo(CONV_HALO, D_CONV, 0),
                  _prev_halo(CONV_HALO, D_CONV, 1), _vec_spec(D_CONV, CONV_K), v, v, v, v],
        out_specs=_row_spec(D_CONV), scratch_shapes=list(_CONV_SCRATCH),
        name=name, compiler_params=_cp("parallel"))(proj, proj, proj, proj, wc, bc, lg, lb, gco)


def conv_module_bwd_a(proj, dmixed, wc, bc, lg, lb, gco, name):
    def body(av_ref, ag_ref, avh_ref, agh_ref, dm_ref, wc_ref, bc_ref, lg_ref, lb_ref, gco_ref,
             du1_ref, dgco_ref, dlg_ref, dlb_ref, dbc_ref, dwc_ref, sh, u1_s, acc):
        first = pl.program_id(0) == 0
        z, rs, ln, s, u2 = _conv_module_forward(av_ref, ag_ref, avh_ref, agh_ref, wc_ref, bc_ref, lg_ref, lb_ref, sh, u1_s)
        rc = lax.rsqrt(_mean(u2 * u2) + EPS)
        xn = u2 * rc
        dm = dm_ref[...]
        _acc(dgco_ref, _rsum(dm * xn), first)
        dyn = dm * gco_ref[...]
        du2 = rc * (dyn - xn * _mean(dyn * xn))
        dln = du2 * (s * (1.0 + ln * (1.0 - s)))
        _acc(dlg_ref, _rsum(dln * z), first)
        _acc(dlb_ref, _rsum(dln), first)
        dz = dln * lg_ref[...]
        du1 = rs * (dz - _mean(dz) - z * _mean(dz * z))
        du1_ref[...] = du1
        _acc(dbc_ref, _rsum(du1), first)
        acc[...] = jnp.zeros_like(acc)

        def taps(rs):
            d = du1_ref[rs, :]
            for j in range(CONV_K):
                acc[j] += d * _tap(sh, rs.start, CONV_HALO - (CONV_K - 1) + j)

        _for_chunks(taps)

        @pl.when(first)
        def _():
            dwc_ref[...] = jnp.zeros_like(dwc_ref)

        for j in range(CONV_K):
            dwc_ref[j:j + 1, :] += _rsum(acc[j])

    v = _vec_spec(D_CONV)
    vec = jax.ShapeDtypeStruct((1, D_CONV), F32)
    return pl.pallas_call(
        body,
        out_shape=(jax.ShapeDtypeStruct((SEQ, D_CONV), F32), vec, vec, vec, vec, jax.ShapeDtypeStruct((CONV_K, D_CONV), F32)),
        grid=(SEQ // ROWS,),
        in_specs=[_row_spec(D_CONV, 0), _row_spec(D_CONV, 1), _prev_halo(CONV_HALO, D_CONV, 0),
                  _prev_halo(CONV_HALO, D_CONV, 1), _row_spec(D_CONV, 0), _vec_spec(D_CONV, CONV_K), v, v, v, v],
        out_specs=(_row_spec(D_CONV), v, v, v, v, _vec_spec(D_CONV, CONV_K)),
        scratch_shapes=list(_CONV_SCRATCH) + [pltpu.VMEM((CONV_K, SUB, D_CONV), F32)],
        name=name, compiler_params=_cp("arbitrary"))(proj, proj, proj, proj, dmixed, wc, bc, lg, lb, gco)


def conv_module_bwd_b(proj, du1, wc, name):
    def body(av_ref, ag_ref, du1_ref, du1n_ref, wc_ref, out_ref, sh):
        i = pl.program_id(0)
        sh[0, 0:ROWS, :] = du1_ref[...]
        sh[0, ROWS:, :] = jnp.where(i < SEQ // ROWS - 1, du1n_ref[...], 0.0)
        _shift_copies(sh)

        def chunk(rs):
            du0 = jnp.zeros((SUB, D_CONV), F32)
            for j in range(CONV_K):
                du0 = du0 + wc_ref[j:j + 1, :] * _tap(sh, rs.start, CONV_K - 1 - j)
            sg = _sig(ag_ref[rs, :])
            out_ref[rs, 0:D_CONV] = (du0 * sg).astype(BF16)
            out_ref[rs, D_CONV:] = (du0 * av_ref[rs, :] * sg * (1.0 - sg)).astype(BF16)

        _for_chunks(chunk)

    return pl.pallas_call(
        body, out_shape=jax.ShapeDtypeStruct((SEQ, 2 * D_CONV), BF16), grid=(SEQ // ROWS,),
        in_specs=[_row_spec(D_CONV, 0), _row_spec(D_CONV, 1), _row_spec(D_CONV, 0), _next_halo(CONV_HALO, D_CONV, 0),
                  _vec_spec(D_CONV, CONV_K)],
        out_specs=_row_spec(2 * D_CONV), scratch_shapes=[pltpu.VMEM((8, CONV_PAD, D_CONV), F32)],
        name=name, compiler_params=_cp("parallel"))(proj, proj, du1, du1, wc)


def _attn_specs(sub_len, pairs):
    ng, width = 4 // pairs, 128 * pairs
    q = pl.BlockSpec((sub_len, width), lambda rho, g: (0, rho * 3 * ng + g))
    k = pl.BlockSpec((sub_len, width), lambda rho, g: (0, rho * 3 * ng + ng + g))
    v = pl.BlockSpec((sub_len, width), lambda rho, g: (0, rho * 3 * ng + 2 * ng + g))
    o = pl.BlockSpec((sub_len, width), lambda rho, g: (0, rho * ng + g))
    return q, k, v, o


ATTN_PAIRS = {1: 1, 4: 1, 16: 4}


def _attn_block(q_ref, k_ref, v_ref, n, hs, win):
    q0 = pl.multiple_of(n * ATTN_BLOCK, ATTN_BLOCK)
    k0 = pl.multiple_of(jnp.maximum(n - 1, 0) * ATTN_BLOCK, ATTN_BLOCK)
    qb = q_ref[pl.ds(q0, ATTN_BLOCK), hs]
    kw = k_ref[pl.ds(k0, win), hs]
    vw = v_ref[pl.ds(k0, win), hs]
    s = lax.dot_general(qb, kw, (((1,), (1,)), ((), ())), preferred_element_type=F32) * (HEAD_DIM ** -0.5)
    dist = (q0 - k0) + lax.broadcasted_iota(jnp.int32, (ATTN_BLOCK, win), 0) \
        - lax.broadcasted_iota(jnp.int32, (ATTN_BLOCK, win), 1)
    s = jnp.where((dist >= 0) & (dist <= ATTN_BLOCK), s, NEG)
    return q0, k0, qb, kw, vw, s


def attn_fwd(qkv_r, sub_len, r, name):
    nb = sub_len // ATTN_BLOCK
    win = 2 * ATTN_BLOCK if nb > 1 else ATTN_BLOCK
    pairs = ATTN_PAIRS[r]

    def body(q_ref, k_ref, v_ref, o_ref, l_ref):
        def block(n, carry):
            for h in range(2 * pairs):
                hs = slice(h * HEAD_DIM, (h + 1) * HEAD_DIM)
                q0, _, _, _, vw, s = _attn_block(q_ref, k_ref, v_ref, n, hs, win)
                m = jnp.max(s, axis=1, keepdims=True)
                p = jnp.exp(s - m)
                den = jnp.sum(p, axis=1, keepdims=True)
                o = jnp.dot(p.astype(BF16), vw, preferred_element_type=F32) / den
                o_ref[pl.ds(q0, ATTN_BLOCK), hs] = o
                l_ref[pl.ds(q0, ATTN_BLOCK), hs] = jnp.broadcast_to(m + jnp.log(den), (ATTN_BLOCK, HEAD_DIM))
            return carry

        lax.fori_loop(0, nb, block, 0, unroll=min(nb, 2))

    q, k, v, o = _attn_specs(sub_len, pairs)
    shp = jax.ShapeDtypeStruct((sub_len, r * D_ATTN), F32)
    return pl.pallas_call(
        body, out_shape=(shp, shp), grid=(r, 4 // pairs), in_specs=[q, k, v], out_specs=(o, o),
        name=name, compiler_params=_cp("parallel", "parallel"))(qkv_r, qkv_r, qkv_r)


def attn_bwd(qkv_r, do_r, lse_r, dd_r, sub_len, r, name):
    nb = sub_len // ATTN_BLOCK
    win = 2 * ATTN_BLOCK if nb > 1 else ATTN_BLOCK
    pairs = ATTN_PAIRS[r]

    def body(q_ref, k_ref, v_ref, do_ref, l_ref, dd_ref, dq_ref, dk_ref, dv_ref):
        dk_ref[...] = jnp.zeros_like(dk_ref)
        dv_ref[...] = jnp.zeros_like(dv_ref)

        def block(n, carry):
            for h in range(2 * pairs):
                hs = slice(h * HEAD_DIM, (h + 1) * HEAD_DIM)
                h1 = slice(h * HEAD_DIM, h * HEAD_DIM + 1)
                q0, k0, qb, kw, vw, s = _attn_block(q_ref, k_ref, v_ref, n, hs, win)
                dob = do_ref[pl.ds(q0, ATTN_BLOCK), hs]
                p = jnp.exp(s - l_ref[pl.ds(q0, ATTN_BLOCK), h1])
                dp = lax.dot_general(dob, vw, (((1,), (1,)), ((), ())), preferred_element_type=F32)
                ds = (p * (dp - dd_ref[pl.ds(q0, ATTN_BLOCK), h1]) * (HEAD_DIM ** -0.5)).astype(BF16)
                dq_ref[pl.ds(q0, ATTN_BLOCK), hs] = jnp.dot(ds, kw, preferred_element_type=F32)
                dk_ref[pl.ds(k0, win), hs] += lax.dot_general(ds, qb, (((0,), (0,)), ((), ())), preferred_element_type=F32)
                dv_ref[pl.ds(k0, win), hs] += lax.dot_general(p.astype(BF16), dob, (((0,), (0,)), ((), ())),
                                                              preferred_element_type=F32)
            return carry

        lax.fori_loop(0, nb, block, 0)

    q, k, v, o = _attn_specs(sub_len, pairs)
    shp = jax.ShapeDtypeStruct((sub_len, r * D_ATTN), F32)
    return pl.pallas_call(
        body, out_shape=(shp, shp, shp), grid=(r, 4 // pairs), in_specs=[q, k, v, o, o, o], out_specs=(o, o, o),
        name=name, compiler_params=_cp("parallel", "parallel"))(qkv_r, qkv_r, qkv_r, do_r, lse_r, dd_r)


def _rows(start, size, r):
    return pl.ds(start, size) if r == 1 else pl.ds(start, size, stride=r)


def _attn_unit(q_ref, k_ref, v_ref, r, rho, n, nb):
    win = 2 * ATTN_BLOCK if nb > 1 else ATTN_BLOCK
    if isinstance(n, int):
        kb = max(n - 1, 0)
        q_rows = _rows(rho + r * ATTN_BLOCK * n, ATTN_BLOCK, r)
        k_rows = _rows(rho + r * ATTN_BLOCK * kb, win, r)
    else:
        kb = jnp.maximum(n - 1, 0)
        q_rows = pl.ds(pl.multiple_of(n * ATTN_BLOCK, ATTN_BLOCK), ATTN_BLOCK)
        k_rows = pl.ds(pl.multiple_of(kb * ATTN_BLOCK, ATTN_BLOCK), win)
    q2 = q_ref[q_rows, :].astype(BF16)
    k2 = k_ref[k_rows, :].astype(BF16)
    v2 = v_ref[k_rows, :].astype(BF16)
    dist = (n - kb) * ATTN_BLOCK + lax.broadcasted_iota(jnp.int32, (ATTN_BLOCK, win), 0) \
        - lax.broadcasted_iota(jnp.int32, (ATTN_BLOCK, win), 1)
    valid = (dist >= 0) & (dist <= ATTN_BLOCK)

    def score(h):
        hs = slice(h * HEAD_DIM, (h + 1) * HEAD_DIM)
        s = lax.dot_general(q2[:, hs], k2[:, hs], (((1,), (1,)), ((), ())), preferred_element_type=F32)
        return jnp.where(valid, s * (HEAD_DIM ** -0.5), NEG), q2[:, hs], k2[:, hs], v2[:, hs]

    return q_rows, k_rows, score


def _attn_units(r, nb, unit):
    if r == 1:
        def two(i, carry):
            unit(0, 2 * i)
            unit(0, 2 * i + 1)
            return carry
        lax.fori_loop(0, nb // 2, two, 0)
    else:
        for rho in range(r):
            for n in range(nb):
                unit(rho, n)


def attn_fwd_all(proj, name):
    def body(q_ref, k_ref, v_ref, att_ref, lse_ref):
        for idx, (sub_len, r) in enumerate(PATTERNS):
            nb = sub_len // ATTN_BLOCK

            def unit(rho, n, r=r, nb=nb, idx=idx):
                q_rows, _, score = _attn_unit(q_ref, k_ref, v_ref, r, rho, n, nb)
                outs, lses = [], []
                for h in range(2):
                    s, _, _, vw = score(h)
                    m = jnp.max(s, axis=1, keepdims=True)
                    p = jnp.exp(s - m)
                    den = jnp.sum(p, axis=1, keepdims=True)
                    outs.append(jnp.dot(p.astype(BF16), vw, preferred_element_type=F32) / den)
                    lses.append(jnp.broadcast_to(m + jnp.log(den), (ATTN_BLOCK, HEAD_DIM)))
                o = jnp.concatenate(outs, axis=1)
                lse = jnp.concatenate(lses, axis=1)
                if idx > 0:
                    old = lse_ref[q_rows, :]
                    top = jnp.maximum(old, lse)
                    new = top + jnp.log(jnp.exp(old - top) + jnp.exp(lse - top))
                    o = att_ref[q_rows, :] * jnp.exp(old - new) + o * jnp.exp(lse - new)
                    lse = new
                att_ref[q_rows, :] = o
                lse_ref[q_rows, :] = lse

            _attn_units(r, nb, unit)

    blk = lambda first: pl.BlockSpec((SEQ, 128), lambda g: (0, first + g))
    shp = jax.ShapeDtypeStruct((SEQ, D_ATTN), F32)
    return pl.pallas_call(
        body, out_shape=(shp, shp), grid=(4,), in_specs=[blk(8), blk(12), blk(16)], out_specs=(blk(0), blk(0)),
        name=name, compiler_params=_cp("parallel"))(proj, proj, proj)


def attn_bwd_all(proj, do, lse, dd, name):
    def body(q_ref, k_ref, v_ref, do_ref, l_ref, dd_ref, out_ref, dq_s, dk_s, dv_s):
        dq_s[...] = jnp.zeros_like(dq_s)
        dk_s[...] = jnp.zeros_like(dk_s)
        dv_s[...] = jnp.zeros_like(dv_s)
        for sub_len, r in PATTERNS:
            nb = sub_len // ATTN_BLOCK

            def unit(rho, n, r=r, nb=nb):
                q_rows, k_rows, score = _attn_unit(q_ref, k_ref, v_ref, r, rho, n, nb)
                do2 = do_ref[q_rows, :].astype(BF16)
                l2 = l_ref[q_rows, :]
                d2 = dd_ref[q_rows, :]
                dq, dk, dv = [], [], []
                for h in range(2):
                    hs = slice(h * HEAD_DIM, (h + 1) * HEAD_DIM)
                    h1 = slice(h * HEAD_DIM, h * HEAD_DIM + 1)
                    s, qb, kw, vw = score(h)
                    dob = do2[:, hs]
                    p = jnp.exp(s - l2[:, h1])
                    dp = lax.dot_general(dob, vw, (((1,), (1,)), ((), ())), preferred_element_type=F32)
                    ds = (p * (dp - d2[:, h1]) * (HEAD_DIM ** -0.5)).astype(BF16)
                    dq.append(jnp.dot(ds, kw, preferred_element_type=F32))
                    dk.append(lax.dot_general(ds, qb, (((0,), (0,)), ((), ())), preferred_element_type=F32))
                    dv.append(lax.dot_general(p.astype(BF16), dob, (((0,), (0,)), ((), ())), preferred_element_type=F32))
                dq_s[q_rows, :] += jnp.concatenate(dq, axis=1)
                dk_s[k_rows, :] += jnp.concatenate(dk, axis=1)
                dv_s[k_rows, :] += jnp.concatenate(dv, axis=1)

            _attn_units(r, nb, unit)
        out_ref[0] = dq_s[...].astype(BF16)
        out_ref[1] = dk_s[...].astype(BF16)
        out_ref[2] = dv_s[...].astype(BF16)

    blk = lambda first: pl.BlockSpec((SEQ, 128), lambda g: (0, first + g))
    acc = pltpu.VMEM((SEQ, 128), F32)
    return pl.pallas_call(
        body, out_shape=jax.ShapeDtypeStruct((3, SEQ, D_ATTN), BF16), grid=(4,),
        in_specs=[blk(8), blk(12), blk(16), blk(0), blk(0), blk(0)],
        out_specs=pl.BlockSpec((3, SEQ, 128), lambda g: (0, 0, g)), scratch_shapes=[acc, acc, acc],
        name=name, compiler_params=_cp("parallel"))(proj, proj, proj, do, lse, dd)


def rms_gain_bf16(a, g, name):
    def body(a_ref, g_ref, o_ref):
        aa = a_ref[...]
        o_ref[...] = (aa * lax.rsqrt(_mean(aa * aa) + EPS) * g_ref[...]).astype(BF16)

    w = a.shape[1]
    return pl.pallas_call(
        body, out_shape=jax.ShapeDtypeStruct(a.shape, BF16), grid=(SEQ // ROWS,), in_specs=[_row_spec(w), _vec_spec(w)],
        out_specs=_row_spec(w), name=name, compiler_params=_cp("parallel"))(a, g)


def attn_combine_fwd(outs, lses, gao, name):
    def body(o1, o2, o3, l1, l2, l3, g_ref, att_ref, lse_ref, mix_ref):
        a1, a2, a3 = l1[...], l2[...], l3[...]
        m = jnp.maximum(jnp.maximum(a1, a2), a3)
        w1, w2, w3 = jnp.exp(a1 - m), jnp.exp(a2 - m), jnp.exp(a3 - m)
        den = w1 + w2 + w3
        att = (w1 * o1[...] + w2 * o2[...] + w3 * o3[...]) / den
        att_ref[...] = att
        lse_ref[...] = m + jnp.log(den)
        mix_ref[...] = (att * lax.rsqrt(_mean(att * att) + EPS) * g_ref[...]).astype(BF16)

    rs = _row_spec(D_ATTN)
    f = jax.ShapeDtypeStruct((SEQ, D_ATTN), F32)
    return pl.pallas_call(
        body, out_shape=(f, f, jax.ShapeDtypeStruct((SEQ, D_ATTN), BF16)), grid=(SEQ // ROWS,),
        in_specs=[rs] * 6 + [_vec_spec(D_ATTN)], out_specs=(rs, rs, rs),
        name=name, compiler_params=_cp("parallel"))(*outs, *lses, gao)


def attn_combine_bwd(dmixed, att, gao, name):
    def body(dm_ref, att_ref, g_ref, do_ref, dd_ref, dg_ref):
        first = pl.program_id(0) == 0
        att = att_ref[...]
        r = lax.rsqrt(_mean(att * att) + EPS)
        xn = att * r
        dm = dm_ref[...]
        _acc(dg_ref, _rsum(dm * xn), first)
        dyn = dm * g_ref[...]
        do = r * (dyn - xn * _mean(dyn * xn))
        do_ref[...] = do
        same_head = (jnp.right_shift(lax.broadcasted_iota(jnp.int32, (D_ATTN, D_ATTN), 0), 6)
                     == jnp.right_shift(lax.broadcasted_iota(jnp.int32, (D_ATTN, D_ATTN), 1), 6)).astype(F32)
        dd_ref[...] = jnp.dot(do * att, same_head, preferred_element_type=F32, precision=lax.Precision.HIGHEST)

    rs = _row_spec(D_ATTN)
    return pl.pallas_call(
        body,
        out_shape=(jax.ShapeDtypeStruct((SEQ, D_ATTN), F32), jax.ShapeDtypeStruct((SEQ, D_ATTN), F32),
                   jax.ShapeDtypeStruct((1, D_ATTN), F32)),
        grid=(SEQ // ROWS,), in_specs=[_row_spec(D_ATTN, 1), rs, _vec_spec(D_ATTN)],
        out_specs=(rs, rs, _vec_spec(D_ATTN)), name=name, compiler_params=_cp("arbitrary"))(dmixed, att, gao)


def sum3_bf16(a, b, c, name):
    def body(a_ref, b_ref, c_ref, o_ref):
        o_ref[...] = (a_ref[...] + b_ref[...] + c_ref[...]).astype(BF16)

    w = a.shape[1]
    rs = _row_spec(w)
    return pl.pallas_call(
        body, out_shape=jax.ShapeDtypeStruct(a.shape, BF16), grid=(SEQ // ROWS,), in_specs=[rs, rs, rs], out_specs=rs,
        name=name, compiler_params=_cp("parallel"))(a, b, c)


N_FT = D_FF // FFN_TN


def _ffn_specs():
    per = ROWS // FFN_HALO
    cur_g = pl.BlockSpec((ROWS, FFN_TN), lambda j, i: (i, j))
    cur_v = pl.BlockSpec((ROWS, FFN_TN), lambda j, i: (i, j + N_FT))
    halo_g = pl.BlockSpec((FFN_HALO, FFN_TN), lambda j, i: (jnp.maximum(i * per - 1, 0), j))
    halo_v = pl.BlockSpec((FFN_HALO, FFN_TN), lambda j, i: (jnp.maximum(i * per - 1, 0), j + N_FT))
    w_g = pl.BlockSpec((FFN_K, FFN_TN), lambda j, i: (0, j))
    w_v = pl.BlockSpec((FFN_K, FFN_TN), lambda j, i: (0, j + N_FT))
    b_g = pl.BlockSpec((1, FFN_TN), lambda j, i: (0, j))
    b_v = pl.BlockSpec((1, FFN_TN), lambda j, i: (0, j + N_FT))
    return [cur_g, cur_v, halo_g, halo_v, w_g, w_v, b_g, b_v]


def matmul(a, b, kind, out_dtype, tm, tn, name):
    if kind == "nn":
        (m, k), n = a.shape, b.shape[1]
        a_spec = pl.BlockSpec((tm, k), lambda j, i: (i, 0))
        b_spec = pl.BlockSpec((k, tn), lambda j, i: (0, j))
        dims = (((1,), (0,)), ((), ()))
    elif kind == "nt":
        (m, k), n = a.shape, b.shape[0]
        a_spec = pl.BlockSpec((tm, k), lambda j, i: (i, 0))
        b_spec = pl.BlockSpec((tn, k), lambda j, i: (j, 0))
        dims = (((1,), (1,)), ((), ()))
    else:
        (k, m), n = a.shape, b.shape[1]
        a_spec = pl.BlockSpec((k, tm), lambda j, i: (0, i))
        b_spec = pl.BlockSpec((k, tn), lambda j, i: (0, j))
        dims = (((0,), (0,)), ((), ()))
    assert m % tm == 0 and n % tn == 0, (name, m, n, tm, tn)

    def body(a_ref, b_ref, o_ref):
        o_ref[...] = lax.dot_general(a_ref[...], b_ref[...], dims, preferred_element_type=F32).astype(o_ref.dtype)

    return pl.pallas_call(
        body, out_shape=jax.ShapeDtypeStruct((m, n), out_dtype), grid=(n // tn, m // tm),
        in_specs=[a_spec, b_spec], out_specs=pl.BlockSpec((tm, tn), lambda j, i: (i, j)),
        name=name, compiler_params=_cp("parallel", "parallel"))(a, b)


def matmul_halves(a, b, tm, tn, name):
    _, m, k = a.shape
    n = b.shape[1]
    b3 = b.reshape(2, k, n)

    def body(a_ref, b_ref, o_ref):
        o_ref[...] = (jnp.dot(a_ref[0], b_ref[0], preferred_element_type=F32)
                      + jnp.dot(a_ref[1], b_ref[1], preferred_element_type=F32))

    return pl.pallas_call(
        body, out_shape=jax.ShapeDtypeStruct((m, n), F32), grid=(n // tn, m // tm),
        in_specs=[pl.BlockSpec((2, tm, k), lambda j, i: (0, i, 0)), pl.BlockSpec((2, k, tn), lambda j, i: (0, 0, j))],
        out_specs=pl.BlockSpec((tm, tn), lambda j, i: (i, j)), name=name, compiler_params=_cp("parallel", "parallel"))(a, b3)


def matmul_tn_halves(a, b, tm, name):
    _, k, m = a.shape
    n = b.shape[1]

    def body(a_ref, b_ref, o_ref):
        o_ref[0] = lax.dot_general(a_ref[0], b_ref[...], (((0,), (0,)), ((), ())), preferred_element_type=F32).astype(BF16)

    return pl.pallas_call(
        body, out_shape=jax.ShapeDtypeStruct((2, m, n), BF16), grid=(2, m // tm),
        in_specs=[pl.BlockSpec((1, k, tm), lambda h, i: (h, 0, i)), pl.BlockSpec((k, n), lambda h, i: (0, 0))],
        out_specs=pl.BlockSpec((1, tm, n), lambda h, i: (h, i, 0)), name=name,
        compiler_params=_cp("parallel", "parallel"))(a, b).reshape(2 * m, n)


def _row_spec(width, col=0):
    return pl.BlockSpec((ROWS, width), lambda i: (i, col))


def _vec_spec(width, rows=1):
    return pl.BlockSpec((rows, width), lambda i: (0, 0))


def _ffn_shifted(cur_ref, halo_ref, pad, s1, s2):
    i = pl.program_id(1)
    pad[0:FFN_HALO, :] = jnp.where(i > 0, halo_ref[...], 0.0)
    pad[FFN_HALO:, :] = cur_ref[0:FFN_HALO, :]
    for k, dst in ((1, s1), (2, s2)):
        dst[0:FFN_HALO, :] = pad[pl.ds(FFN_HALO - k, FFN_HALO), :]
        dst[FFN_HALO:, :] = cur_ref[pl.ds(FFN_HALO - k, ROWS - FFN_HALO), :]


def _ffn_conv(rs, cur_ref, s1, s2, w_ref, b_ref):
    return b_ref[...] + w_ref[0:1, :] * s2[rs, :] + w_ref[1:2, :] * s1[rs, :] + w_ref[2:3, :] * cur_ref[rs, :]


def ffn_act_fwd(up0, wf, bf, name):
    def body(g_ref, v_ref, gh_ref, vh_ref, wg_ref, wv_ref, bg_ref, bv_ref, act_ref, pad, g1, g2, v1, v2):
        _ffn_shifted(g_ref, gh_ref, pad, g1, g2)
        _ffn_shifted(v_ref, vh_ref, pad, v1, v2)

        def chunk(rs):
            gate = _ffn_conv(rs, g_ref, g1, g2, wg_ref, bg_ref)
            val = _ffn_conv(rs, v_ref, v1, v2, wv_ref, bv_ref)
            act_ref[rs, :] = (gate * _sig(gate) * val).astype(BF16)

        _for_chunks(chunk)

    tile = pltpu.VMEM((ROWS, FFN_TN), F32)
    return pl.pallas_call(
        body, out_shape=jax.ShapeDtypeStruct((SEQ, D_FF), BF16), grid=(N_FT, SEQ // ROWS),
        in_specs=_ffn_specs(), out_specs=pl.BlockSpec((ROWS, FFN_TN), lambda j, i: (i, j)),
        scratch_shapes=[pltpu.VMEM((2 * FFN_HALO, FFN_TN), F32), tile, tile, tile, tile],
        name=name, compiler_params=_cp("parallel", "parallel"))(up0, up0, up0, up0, wf, wf, bf, bf)


def ffn_bwd(up0, dact, wf, bf, name):
    per = ROWS // FFN_HALO
    last = SEQ // FFN_HALO - 1

    def body(g_ref, v_ref, gh_ref, vh_ref, wg_ref, wv_ref, bg_ref, bv_ref, da_ref, gn_ref, vn_ref, dan_ref,
             out_ref, dbg_ref, dbv_ref, dwg_ref, dwv_ref, pad, g1, g2, v1, v2, dgp, dvp, acc):
        i = pl.program_id(1)
        first = i == 0
        _ffn_shifted(g_ref, gh_ref, pad, g1, g2)
        _ffn_shifted(v_ref, vh_ref, pad, v1, v2)
        acc[...] = jnp.zeros_like(acc)

        def grads(gate, val, da):
            s = _sig(gate)
            return da * val * (s * (1.0 + gate * (1.0 - s))), da * (gate * s)

        def chunk(rs):
            gate = _ffn_conv(rs, g_ref, g1, g2, wg_ref, bg_ref)
            val = _ffn_conv(rs, v_ref, v1, v2, wv_ref, bv_ref)
            dgate, dval = grads(gate, val, da_ref[rs, :])
            dgp[rs, :] = dgate
            dvp[rs, :] = dval
            acc[0] += dgate
            acc[1] += dval
            for t, (sg, sv) in enumerate(((g2, v2), (g1, v1), (g_ref, v_ref))):
                acc[2 + t] += dgate * sg[rs, :]
                acc[5 + t] += dval * sv[rs, :]

        _for_chunks(chunk)
        _acc(dbg_ref, _rsum(acc[0]), first)
        _acc(dbv_ref, _rsum(acc[1]), first)
        _acc(dwg_ref, jnp.concatenate([_rsum(acc[2 + t]) for t in range(FFN_K)], axis=0), first)
        _acc(dwv_ref, jnp.concatenate([_rsum(acc[5 + t]) for t in range(FFN_K)], axis=0), first)

        def conv_next(cur_ref, nxt_ref, w_ref, b_ref):
            pad[0:FFN_HALO, :] = cur_ref[ROWS - FFN_HALO:, :]
            pad[FFN_HALO:, :] = nxt_ref[...]
            return (b_ref[...] + w_ref[0:1, :] * pad[pl.ds(FFN_HALO - 2, FFN_HALO), :]
                    + w_ref[1:2, :] * pad[pl.ds(FFN_HALO - 1, FFN_HALO), :] + w_ref[2:3, :] * nxt_ref[...])

        gate_n = conv_next(g_ref, gn_ref, wg_ref, bg_ref)
        val_n = conv_next(v_ref, vn_ref, wv_ref, bv_ref)
        dgate_n, dval_n = grads(gate_n, val_n, dan_ref[...])
        inside = i < SEQ // ROWS - 1
        dgp[ROWS:, :] = jnp.where(inside, dgate_n, 0.0)
        dvp[ROWS:, :] = jnp.where(inside, dval_n, 0.0)

        for half, (dp, s1, s2, w_ref) in enumerate(((dgp, g1, g2, wg_ref), (dvp, v1, v2, wv_ref))):
            s1[...] = dp[pl.ds(1, ROWS), :]
            s2[...] = dp[pl.ds(2, ROWS), :]

            def back(rs, dp=dp, s1=s1, s2=s2, w_ref=w_ref, half=half):
                out_ref[half, rs, :] = (w_ref[2:3, :] * dp[rs, :] + w_ref[1:2, :] * s1[rs, :]
                                        + w_ref[0:1, :] * s2[rs, :]).astype(BF16)

            _for_chunks(back)

    tile = pltpu.VMEM((ROWS, FFN_TN), F32)
    ext = pltpu.VMEM((ROWS + FFN_HALO, FFN_TN), F32)
    vec = jax.ShapeDtypeStruct((1, D_FF), F32)
    taps = jax.ShapeDtypeStruct((FFN_K, D_FF), F32)
    cur = pl.BlockSpec((ROWS, FFN_TN), lambda j, i: (i, j))
    nxt = lambda off: pl.BlockSpec((FFN_HALO, FFN_TN), lambda j, i: (jnp.minimum((i + 1) * per, last), j + off))
    vs = pl.BlockSpec((1, FFN_TN), lambda j, i: (0, j))
    ts = pl.BlockSpec((FFN_K, FFN_TN), lambda j, i: (0, j))
    return pl.pallas_call(
        body, out_shape=(jax.ShapeDtypeStruct((2, SEQ, D_FF), BF16), vec, vec, taps, taps), grid=(N_FT, SEQ // ROWS),
        in_specs=_ffn_specs() + [cur, nxt(0), nxt(N_FT), nxt(0)],
        out_specs=(pl.BlockSpec((2, ROWS, FFN_TN), lambda j, i: (0, i, j)), vs, vs, ts, ts),
        scratch_shapes=[pltpu.VMEM((2 * FFN_HALO, FFN_TN), F32), tile, tile, tile, tile, ext, ext,
                        pltpu.VMEM((2 + 2 * FFN_K, SUB, FFN_TN), F32)],
        name=name, compiler_params=_cp("parallel", "arbitrary"))(up0, up0, up0, up0, wf, wf, bf, bf, dact, up0, up0, dact)


def ada_fwd(c_all, w_ada, b_cols, name):
    def body(c_ref, w_ref, b_ref, o_ref):
        cc = c_ref[...]
        sc = (cc * _sig(cc)).astype(BF16)
        o_ref[...] = jnp.dot(sc, w_ref[...].astype(BF16), preferred_element_type=F32) + b_ref[...]

    return pl.pallas_call(body, out_shape=jax.ShapeDtypeStruct((N_DEV, w_ada.shape[1]), F32), name=name,
                          compiler_params=_cp())(c_all, w_ada, b_cols)


def _adam(w, g, m, v):
    m = ADAM_B1 * m + (1.0 - ADAM_B1) * g
    v = ADAM_B2 * v + (1.0 - ADAM_B2) * (g * g)
    m_hat = m / (1.0 - ADAM_B1 ** ADAM_STEP)
    v_hat = v / (1.0 - ADAM_B2 ** ADAM_STEP)
    delta = -ADAM_LR * (m_hat / (jnp.sqrt(v_hat) + ADAM_EPS) + ADAM_WD * w)
    return delta, m, v


def ada_bwd_adamw(c_all_t, dmod_cols, w, m, v, name):
    rows, cols = w.shape
    tr = 256

    def body(ct_ref, dm_ref, w_ref, m_ref, v_ref, g_ref, d_ref, nm_ref, nv_ref):
        ct = ct_ref[...]
        sc = ct * _sig(ct)
        g = sc[:, 0:1] * dm_ref[0:1, :]
        for b in range(1, N_DEV):
            g = g + sc[:, b:b + 1] * dm_ref[b:b + 1, :]
        g_ref[...] = g
        d_ref[...], nm_ref[...], nv_ref[...] = _adam(w_ref[...], g, m_ref[...], v_ref[...])

    blk = pl.BlockSpec((tr, cols), lambda i: (i, 0))
    shp = jax.ShapeDtypeStruct((rows, cols), F32)
    return pl.pallas_call(
        body, out_shape=(shp, shp, shp, shp), grid=(rows // tr,),
        in_specs=[pl.BlockSpec((tr, N_DEV), lambda i: (i, 0)), pl.BlockSpec((N_DEV, cols), lambda i: (0, 0)), blk, blk, blk],
        out_specs=(blk, blk, blk, blk), name=name, compiler_params=_cp("parallel"))(c_all_t, dmod_cols, w, m, v)


def sum_adamw(parts, own, w, m, v, tr, name):
    n_parts, rows, cols = parts.shape

    def body(*refs):
        if own is None:
            p_ref, w_ref, m_ref, v_ref, g_ref, d_ref, nm_ref, nv_ref = refs
            g = p_ref[0].astype(F32)
        else:
            p_ref, own_ref, w_ref, m_ref, v_ref, g_ref, d_ref, nm_ref, nv_ref = refs
            g = own_ref[...].astype(F32)
        for k in range(1, n_parts):
            g = g + p_ref[k].astype(F32)
        g_ref[...] = g
        d_ref[...], nm_ref[...], nv_ref[...] = _adam(w_ref[...], g, m_ref[...], v_ref[...])

    blk = pl.BlockSpec((tr, cols), lambda i: (i, 0))
    shp = jax.ShapeDtypeStruct((rows, cols), F32)
    args = [parts] + ([] if own is None else [own]) + [w, m, v]
    return pl.pallas_call(
        body, out_shape=(shp, shp, shp, shp), grid=(rows // tr,),
        in_specs=[pl.BlockSpec((n_parts, tr, cols), lambda i: (0, i, 0))] + [blk] * (len(args) - 1),
        out_specs=(blk, blk, blk, blk), name=name, compiler_params=_cp("parallel"))(*args)


MESH = pl.DeviceIdType.MESH
ANY = pl.BlockSpec(memory_space=pl.ANY)


def all_gather(block, name, after=None):
    extra = () if after is None else (after,)

    def body(x_ref, *refs):
        out_ref, send_sems, recv_sems, local_sem = refs[len(extra):]
        x, y, c = lax.axis_index("x"), lax.axis_index("y"), lax.axis_index("c")
        me, sibling = (x, y, c), (x, y, 1 - c)
        chips = [(1 - x, y), (x, 1 - y), (1 - x, 1 - y)]

        def slot(px, py, pc):
            return out_ref.at[4 * px + 2 * py + pc]

        def copy(k, blk, to, src=None):
            return pltpu.make_async_remote_copy(
                src_ref=slot(*blk) if src is None else src, dst_ref=slot(*blk),
                send_sem=send_sems.at[k], recv_sem=recv_sems.at[k], device_id=to, device_id_type=MESH)

        mine = pltpu.make_async_copy(x_ref, slot(*me), local_sem)
        mine.start()
        first = [copy(0, me, sibling, src=x_ref)]
        first += [copy(1 + j, me, (*chip, c), src=x_ref) for j, chip in enumerate(chips)]
        for cp in first:
            cp.start()
        passed = [copy(4 + j, (*chip, c), sibling) for j, chip in enumerate(chips)]
        for j, chip in enumerate(chips):
            copy(1 + j, (*chip, c), me).wait_recv()
            passed[j].start()
        copy(0, sibling, me).wait_recv()
        for j, chip in enumerate(chips):
            copy(4 + j, (*chip, 1 - c), me).wait_recv()
        for cp in first + passed:
            cp.wait_send()
        mine.wait()

    return pl.pallas_call(
        body, out_shape=jax.ShapeDtypeStruct((N_DEV,) + block.shape, block.dtype), in_specs=[ANY] * (1 + len(extra)), out_specs=ANY,
        scratch_shapes=[pltpu.SemaphoreType.DMA((7,)), pltpu.SemaphoreType.DMA((7,)), pltpu.SemaphoreType.DMA],
        name=name)(block, *extra)


HBM = pl.BlockSpec(memory_space=pltpu.HBM)
SEM = pl.BlockSpec(memory_space=pltpu.SEMAPHORE)
EFFECT = pltpu.SideEffectType.DATAFLOW_SIDE_EFFECTING


def _peer_copies(src_ref, land_ref, send_sems, recv_sems, gather):
    x, y, c = lax.axis_index("x"), lax.axis_index("y"), lax.axis_index("c")
    me = 4 * x + 2 * y + c
    copies = []
    for k in range(1, N_DEV):
        px = 1 - x if k & 4 else x
        py = 1 - y if k & 2 else y
        pc = 1 - c if k & 1 else c
        copies.append(pltpu.make_async_remote_copy(
            src_ref=src_ref if gather else src_ref.at[4 * px + 2 * py + pc],
            dst_ref=land_ref.at[me] if gather else land_ref.at[k],
            send_sem=send_sems.at[k - 1], recv_sem=recv_sems.at[k - 1], device_id=(px, py, pc), device_id_type=MESH))
    return copies


def exchange_start(src, gather, name, after=None):
    land_shape = (N_DEV,) + src.shape if gather else src.shape
    extra = () if after is None else (after,)

    def body(src_ref, land_ref, *refs):
        send_sems, recv_sems, src_thru, land_thru, token = refs[len(extra):]
        for cp in _peer_copies(src_ref, land_ref, send_sems, recv_sems, gather):
            cp.start()
        token[...] = jnp.zeros_like(token)

    send_sems, recv_sems, src_thru, land_thru, token = pl.pallas_call(
        body, name=name,
        out_shape=(pltpu.SemaphoreType.DMA((N_DEV - 1,)), pltpu.SemaphoreType.DMA((N_DEV - 1,)),
                   pltpu.HBM(src.shape, src.dtype), pltpu.HBM(land_shape, src.dtype), jax.ShapeDtypeStruct((8, 128), F32)),
        in_specs=(HBM, HBM) + (ANY,) * len(extra), out_specs=(SEM, SEM, HBM, HBM, pl.BlockSpec(memory_space=pltpu.VMEM)),
        input_output_aliases={0: 2, 1: 3}, compiler_params=pltpu.CompilerParams(has_side_effects=EFFECT),
    )(pltpu.with_memory_space_constraint(src, pltpu.HBM),
      pltpu.with_memory_space_constraint(lax.empty(land_shape, src.dtype), pltpu.HBM), *extra)
    return (send_sems, recv_sems, src_thru, land_thru), token[0, 0]


def exchange_wait(handles, after, gather, name):
    send_sems, recv_sems, src_thru, land_thru = handles

    def body(src_ref, land_ref, send_sems, recv_sems, after_ref, src_dead, got_ref):
        for cp in _peer_copies(src_ref, land_ref, send_sems, recv_sems, gather):
            cp.wait_send()
            cp.wait_recv()

    return pl.pallas_call(
        body, name=name,
        out_shape=(pltpu.HBM(src_thru.shape, src_thru.dtype), pltpu.HBM(land_thru.shape, land_thru.dtype)),
        in_specs=(HBM, HBM, SEM, SEM, ANY), out_specs=(HBM, HBM), input_output_aliases={0: 0, 1: 1},
        compiler_params=pltpu.CompilerParams(has_side_effects=EFFECT),
    )(src_thru, land_thru, send_sems, recv_sems, after)[1]


def _to_pattern(a, r):
    return a.reshape(SEQ // r, r * a.shape[1])


def local_step(x, tgt, mod, get_w, put_grad, wc, wf, g_mix, bc, lg, lb, gco, gao, g_ffn, bf, g_fin):
    h1 = rms_mod_fwd(x, g_mix, mod, 0, 1, "h1_fwd")
    w_in = get_w("w_in", h1)
    proj = matmul(h1, w_in, "nt", F32, 512, D_IN, "proj_fwd")
    mix_a = conv_module_fwd(proj, wc, bc, lg, lb, gco, "conv_module_fwd")
    att, lse = attn_fwd_all(proj, "attn_fwd")
    mix_b = rms_gain_bf16(att, gao, "attn_out_norm")
    mixed = jnp.concatenate([mix_a, mix_b], axis=1)
    w_out = get_w("w_out", mixed)
    y1 = matmul(mixed, w_out, "nn", F32, 512, D_MODEL, "out_proj_fwd")
    x1, h2 = resid_rms_mod_fwd(x, y1, g_ffn, mod, 2, 3, 4, "x1_h2_fwd")
    w_up = get_w("w_up", h2)
    up0 = matmul(h2, w_up, "nt", F32, 512, D_FF, "up_fwd")
    act = ffn_act_fwd(up0, wf, bf, "ffn_act_fwd")
    w_down = get_w("w_down", act)
    y2 = matmul(act, w_down, "nn", F32, 512, D_MODEL, "down_fwd")
    loss_t, dx2, dy2, d_gfin, d_gaf = final_loss_bwd(x1, y2, tgt, g_fin, mod, 5, "loss_bwd")
    dact = matmul(dy2, w_down, "nt", F32, 512, FFN_TN, "down_bwd_x")
    dw_down = matmul(act, dy2, "tn", BF16, 256, D_MODEL, "down_bwd_w")
    dup0, dbf_g, dbf_v, dwf_g, dwf_v = ffn_bwd(up0, dact, wf + put_grad("w_down", dw_down), bf, "ffn_bwd")
    dh2 = matmul_halves(dup0, w_up, 512, 512, "up_bwd_x")
    dw_up = matmul_tn_halves(dup0, h2, 256, "up_bwd_w")
    dx1, d_shf, d_scf, d_gffn, dy1, d_gam = rms_mod_bwd(x1, dh2, dx2, g_ffn + put_grad("w_up", dw_up), mod, 4, y1, 2, "h2_bwd")
    dmixed = matmul(dy1, w_out, "nt", F32, 512, D_MODEL, "out_proj_bwd_x")
    dw_out = matmul(mixed, dy1, "tn", BF16, 256, D_MODEL, "out_proj_bwd_w")
    do, dd, d_gao = attn_combine_bwd(dmixed, att, gao + put_grad("w_out", dw_out), "attn_combine_bwd")
    dqkv = attn_bwd_all(proj, do, lse, dd, "attn_bwd")
    du1, d_gco, d_lg, d_lb, d_bc, d_wc = conv_module_bwd_a(proj, dmixed, wc, bc, lg, lb, gco, "conv_module_bwd_a")
    dproj_a = conv_module_bwd_b(proj, du1, wc, "conv_module_bwd_b")
    dproj = jnp.concatenate([dproj_a, dqkv[0], dqkv[1], dqkv[2]], axis=1)
    dw_in = matmul(dproj, h1, "tn", BF16, 512, D_MODEL, "proj_bwd_w")
    dh1 = matmul(dproj, w_in, "nn", F32, 512, D_MODEL, "proj_bwd_x")
    dx, d_shm, d_scm, d_gmix = rms_mod_bwd(x, dh1, dx1, g_mix + put_grad("w_in", dw_in), mod, 1, None, 0, "h1_bwd")
    dmod = jnp.concatenate([d_shm, d_scm, d_gam, d_shf, d_scf, d_gaf], axis=1)
    small = dict(g_norm_mix=d_gmix, b_conv_dw=d_bc, ln_conv_g=d_lg, ln_conv_b=d_lb, g_conv_out=d_gco, g_attn_out=d_gao,
                 g_norm_ffn=d_gffn, b_ffn_dw=jnp.concatenate([dbf_g, dbf_v], axis=1), g_final=d_gfin,
                 w_conv_dw=d_wc, w_ffn_dw=jnp.concatenate([dwf_g, dwf_v], axis=1), dmod=dmod, loss=loss_t[0:1, 0:1])
    return dx, small


def _padw(a, width):
    return jnp.pad(a, ((0, 0), (0, width - a.shape[1])))


def pack_small(t):
    wide = jnp.concatenate([_padw(t["dmod"], PACK_W), _padw(t["b_ffn_dw"], PACK_W), _padw(t["w_ffn_dw"], PACK_W),
                            _padw(t["loss"], PACK_W), jnp.zeros((2, PACK_W), F32)], axis=0)
    z512 = jnp.zeros((1, 512), F32)
    narrow = jnp.concatenate([
        t["g_norm_mix"], t["g_norm_ffn"], t["g_final"],
        jnp.concatenate([t["b_conv_dw"], t["ln_conv_g"]], axis=1),
        jnp.concatenate([t["ln_conv_b"], t["g_conv_out"]], axis=1),
        jnp.concatenate([t["g_attn_out"], z512], axis=1),
        jnp.zeros((2, 1024), F32),
        jnp.pad(t["w_conv_dw"], ((0, 1), (0, 0))).reshape(16, 1024)], axis=0)
    return jnp.concatenate([wide, narrow.reshape(4, PACK_W), jnp.zeros((4, PACK_W), F32)], axis=0)


def unpack_small(p):
    narrow = p[8:12].reshape(24, 1024)
    return dict(
        dmod=p[0:1], b_ffn_dw=p[1:2, :2 * D_FF], w_ffn_dw=p[2:5, :2 * D_FF], loss=p[5, 0],
        g_norm_mix=narrow[0:1], g_norm_ffn=narrow[1:2], g_final=narrow[2:3],
        b_conv_dw=narrow[3:4, :512], ln_conv_g=narrow[3:4, 512:], ln_conv_b=narrow[4:5, :512], g_conv_out=narrow[4:5, 512:],
        g_attn_out=narrow[5:6, :512], w_conv_dw=narrow[8:24].reshape(32, 512)[:CONV_K])


def _embed(local, width, me):
    return lax.dynamic_update_slice(jnp.zeros((local.shape[0], width), F32), local, (0, me * local.shape[1]))


def _shard(full, n_cols, me):
    return lax.dynamic_slice(full, (0, me * n_cols), (full.shape[0], n_cols))


WEIGHTS = ["w_ada", "b_ada", "g_norm_mix", "w_in", "w_conv_dw", "b_conv_dw", "ln_conv_g", "ln_conv_b", "g_conv_out",
           "g_attn_out", "w_out", "g_norm_ffn", "w_up", "w_ffn_dw", "b_ffn_dw", "w_down", "g_final"]
SMALL_REPLICATED = ["g_norm_mix", "b_conv_dw", "ln_conv_g", "ln_conv_b", "g_conv_out", "g_attn_out", "g_norm_ffn",
                    "b_ffn_dw", "g_final"]


def kernel(x, c, w_ada, b_ada, g_norm_mix, w_in, w_conv_dw, b_conv_dw, ln_conv_g, ln_conv_b, g_conv_out, g_attn_out, w_out, g_norm_ffn, w_up, w_ffn_dw, b_ffn_dw, w_down, g_final, loss_target, m_w_ada, m_b_ada, m_g_norm_mix, m_w_in, m_w_conv_dw, m_b_conv_dw, m_ln_conv_g, m_ln_conv_b, m_g_conv_out, m_g_attn_out, m_w_out, m_g_norm_ffn, m_w_up, m_w_ffn_dw, m_b_ffn_dw, m_w_down, m_g_final, v_w_ada, v_b_ada, v_g_norm_mix, v_w_in, v_w_conv_dw, v_b_conv_dw, v_ln_conv_g, v_ln_conv_b, v_g_conv_out, v_g_attn_out, v_w_out, v_g_norm_ffn, v_w_up, v_w_ffn_dw, v_b_ffn_dw, v_w_down, v_g_final):
    args = dict(locals())
    me = 4 * lax.axis_index("x") + 2 * lax.axis_index("y") + lax.axis_index("c")

    def flat(name, prefix=""):
        a = args[prefix + name]
        return a.reshape(a.shape[-2] if a.ndim > 1 else 1, a.shape[-1])

    n_in, n_up, r_out, r_down = w_in.shape[2], w_up.shape[2], w_out.shape[1], w_down.shape[1]
    n_ada, n_wc, n_wf = w_ada.shape[2], w_conv_dw.shape[2], w_ffn_dw.shape[2]
    taps_c = jnp.pad(flat("w_conv_dw").reshape(1, CONV_K * n_wc), ((0, 0), (0, 2 * D_MODEL - CONV_K * n_wc)))
    taps_f = jnp.pad(flat("w_ffn_dw").reshape(1, FFN_K * n_wf), ((0, 0), (0, 3 * D_MODEL - FFN_K * n_wf)))
    first = jnp.concatenate([c, taps_c.reshape(2, D_MODEL), taps_f.reshape(3, D_MODEL), jnp.zeros((2, D_MODEL), F32)], axis=0)
    first_all = all_gather(first, "gather_c_taps")
    c_all = first_all[:, 0, :]
    wc_full = first_all[:, 1:3, :].reshape(N_DEV, 2 * D_MODEL)[:, :CONV_K * n_wc].reshape(N_DEV, CONV_K, n_wc)
    wc_full = wc_full.transpose(1, 0, 2).reshape(CONV_K, D_CONV)
    wf_full = first_all[:, 3:6, :].reshape(N_DEV, 3 * D_MODEL)[:, :FFN_K * n_wf].reshape(N_DEV, FFN_K, n_wf)
    wf_full = wf_full.transpose(1, 0, 2).reshape(FFN_K, 2 * D_FF)
    mod_cols = ada_fwd(c_all, flat("w_ada"), _shard(flat("b_ada"), n_ada, me), "ada_fwd")
    mod_all = all_gather(mod_cols, "gather_mod")
    mod = lax.dynamic_index_in_dim(mod_all, me, axis=1, keepdims=False).reshape(N_MOD, D_MODEL)
    mod = jnp.pad(mod, ((0, 2), (0, 0)))

    def flat_t(name, prefix=""):
        return args[prefix + name][0].T

    blocks = dict(w_in=flat_t("w_in").astype(BF16), w_up=flat_t("w_up").astype(BF16),
                  rows=jnp.concatenate([flat("w_out"), flat("w_down")], axis=0).astype(BF16))
    gathers, tok = {}, None
    for name in ("w_in", "rows", "w_up"):
        src = blocks[name] if tok is None else blocks[name] + tok.astype(BF16)
        gathers[name], tok = exchange_start(src, True, f"gather_{name}_start", mod_all)
    mod = mod + tok
    full = {}

    slot = lax.broadcasted_iota(jnp.int32, (N_DEV, 1, 1), 0)

    def gathered(name, after):
        land = exchange_wait(gathers[name], after, True, f"gather_{name}_wait")
        return jnp.where(slot == me, blocks[name][None], land)

    def get_w(name, after):
        if name in ("w_out", "w_down"):
            if "rows" not in full:
                full["rows"] = gathered("rows", after)
            rows = full["rows"]
            return rows[:, :r_out, :].reshape(D_MODEL, D_MODEL) if name == "w_out" else rows[:, r_out:, :].reshape(D_FF, D_MODEL)
        return gathered(name, after).reshape(-1, D_MODEL)

    exchanges, own = {}, {}

    def put_grad(name, dw, after=None):
        dev_major = dw.reshape(N_DEV, -1, D_MODEL)
        own[name] = lax.dynamic_index_in_dim(dev_major, me, axis=0, keepdims=False)
        exchanges[name], token = exchange_start(dev_major, False, f"exchange_{name}_start", after)
        return token

    grad_x, small = local_step(
        x[0], loss_target[0], mod, get_w, put_grad, wc_full, wf_full,
        flat("g_norm_mix"), flat("b_conv_dw"), flat("ln_conv_g"), flat("ln_conv_b"), flat("g_conv_out"),
        flat("g_attn_out"), flat("g_norm_ffn"), flat("b_ffn_dw"), flat("g_final"))

    out = {}

    def finish(name, tr, after):
        parts = exchange_wait(exchanges[name], after, False, f"exchange_{name}_wait")
        if name in ("w_in", "w_up"):
            res = sum_adamw(parts, own[name], flat_t(name), flat_t(name, "m_"), flat_t(name, "v_"), tr, "adamw_" + name)
            out[name] = tuple(r.T for r in res)
        else:
            res = out[name] = sum_adamw(parts, own[name], flat(name), flat(name, "m_"), flat(name, "v_"), tr, "adamw_" + name)
        return res[0]

    after = finish("w_down", r_down, grad_x)
    after = finish("w_up", n_up // 2, after)
    after = finish("w_out", r_out, after)
    after = finish("w_in", n_in, after)

    small_all = all_gather(pack_small(small), "gather_small", after)

    def packed(prefix):
        t = {n: flat(n, prefix) for n in SMALL_REPLICATED}
        t["dmod"] = flat("b_ada", prefix)
        t["loss"] = jnp.zeros((1, 1), F32)
        t["w_conv_dw"] = _embed(flat("w_conv_dw", prefix), D_CONV, me)
        t["w_ffn_dw"] = _embed(flat("w_ffn_dw", prefix), 2 * D_FF, me)
        return pack_small(t)

    res = sum_adamw(small_all, None, packed(""), packed("m_"), packed("v_"), PACK_ROWS, "adamw_small")
    res = [unpack_small(r) for r in res]
    loss = res[0]["loss"]
    for n in SMALL_REPLICATED:
        out[n] = tuple(r[n] for r in res)
    out["b_ada"] = tuple(r["dmod"] for r in res)
    out["w_conv_dw"] = tuple(_shard(r["w_conv_dw"], n_wc, me) for r in res)
    out["w_ffn_dw"] = tuple(_shard(r["w_ffn_dw"], n_wf, me) for r in res)

    dmod_cols = _shard(small_all[:, 0, :], n_ada, me)
    out["w_ada"] = ada_bwd_adamw(c_all.T, dmod_cols, flat("w_ada"), flat("w_ada", "m_"), flat("w_ada", "v_"), "adamw_w_ada")

    result = [loss, grad_x[None]]
    for k in range(4):
        result += [out[n][k].reshape(args[n].shape) for n in WEIGHTS]
    return tuple(result)
```

```python
import functools

import jax
import jax.numpy as jnp
from jax import lax
from jax.experimental import pallas as pl
from jax.experimental.pallas import tpu as pltpu

F32 = jnp.float32
BF16 = jnp.bfloat16

N_DEV = 8
SEQ = 2048
D_MODEL = 1024
D_CONV = 512
D_ATTN = 512
HEAD_DIM = 64
CONV_K = 31
D_FF = 2816
FFN_K = 3
D_IN = 2 * D_CONV + 3 * D_ATTN
N_MOD = 6
EPS = 1e-6
ATTN_BLOCK = 128
PATTERNS = ((2048, 1), (512, 4), (128, 16))
NEG = -1e30

ADAM_LR, ADAM_B1, ADAM_B2, ADAM_EPS, ADAM_WD, ADAM_STEP = 0.001, 0.9, 0.999, 1e-08, 0.01, 10

ROWS = 256
CONV_HALO = 32
FFN_HALO = 8
FFN_TN = 1408
VMEM_LIMIT = 56 * 1024 * 1024
PACK_ROWS, PACK_W = 16, 6144


def _cp(*sem):
    return pltpu.CompilerParams(dimension_semantics=sem if sem else None, vmem_limit_bytes=VMEM_LIMIT)


def _sig(x):
    return 1.0 / (1.0 + jnp.exp(-x))


def _rsum(x):
    return jnp.sum(x, axis=0, keepdims=True)


def _mean(x):
    return jnp.mean(x, axis=-1, keepdims=True)


def _acc(ref, val, first):
    @pl.when(first)
    def _():
        ref[...] = val

    @pl.when(jnp.logical_not(first))
    def _():
        ref[...] += val


SUB = 16


def _for_chunks(fn, unroll=1):
    def step(i, carry):
        fn(pl.ds(pl.multiple_of(i * SUB, SUB), SUB))
        return carry

    lax.fori_loop(0, ROWS // SUB, step, 0, unroll=unroll)


def rms_mod_fwd(x, g, mod, sh_row, sc_row, name):
    def body(x_ref, g_ref, mod_ref, h_ref):
        xx = x_ref[...]
        r = lax.rsqrt(_mean(xx * xx) + EPS)
        h = xx * r * g_ref[...]
        h_ref[...] = (h * (1.0 + mod_ref[sc_row:sc_row + 1, :]) + mod_ref[sh_row:sh_row + 1, :]).astype(BF16)

    return pl.pallas_call(
        body, out_shape=jax.ShapeDtypeStruct((SEQ, D_MODEL), BF16), grid=(SEQ // ROWS,),
        in_specs=[_row_spec(D_MODEL), _vec_spec(D_MODEL), _vec_spec(D_MODEL, 8)],
        out_specs=_row_spec(D_MODEL), name=name, compiler_params=_cp("parallel"))(x, g, mod)


def resid_rms_mod_fwd(x, y, g, mod, ga_row, sh_row, sc_row, name):
    def body(x_ref, y_ref, g_ref, mod_ref, x1_ref, h_ref):
        x1 = x_ref[...] + mod_ref[ga_row:ga_row + 1, :] * y_ref[...]
        x1_ref[...] = x1
        r = lax.rsqrt(_mean(x1 * x1) + EPS)
        h = x1 * r * g_ref[...]
        h_ref[...] = (h * (1.0 + mod_ref[sc_row:sc_row + 1, :]) + mod_ref[sh_row:sh_row + 1, :]).astype(BF16)

    return pl.pallas_call(
        body, out_shape=(jax.ShapeDtypeStruct((SEQ, D_MODEL), F32), jax.ShapeDtypeStruct((SEQ, D_MODEL), BF16)),
        grid=(SEQ // ROWS,),
        in_specs=[_row_spec(D_MODEL), _row_spec(D_MODEL), _vec_spec(D_MODEL), _vec_spec(D_MODEL, 8)],
        out_specs=(_row_spec(D_MODEL), _row_spec(D_MODEL)), name=name, compiler_params=_cp("parallel"))(x, y, g, mod)


def final_loss_bwd(x1, y2, tgt, g, mod, ga_row, name):
    def body(x1_ref, y2_ref, t_ref, g_ref, mod_ref, loss_ref, dx2_ref, dy2_ref, dg_ref, dga_ref):
        first = pl.program_id(0) == 0
        ga = mod_ref[ga_row:ga_row + 1, :]
        y2 = y2_ref[...]
        x2 = x1_ref[...] + ga * y2
        r = lax.rsqrt(_mean(x2 * x2) + EPS)
        xn = x2 * r
        err = xn * g_ref[...] - t_ref[...]
        _acc(loss_ref, jnp.broadcast_to(0.5 * jnp.sum(_mean(err * err)), (8, 128)), first)
        dy = err * (1.0 / D_MODEL)
        _acc(dg_ref, _rsum(dy * xn), first)
        dxn = dy * g_ref[...]
        dx2 = r * (dxn - xn * _mean(dxn * xn))
        dx2_ref[...] = dx2
        dy2_ref[...] = (dx2 * ga).astype(BF16)
        _acc(dga_ref, _rsum(dx2 * y2), first)

    vec = jax.ShapeDtypeStruct((1, D_MODEL), F32)
    return pl.pallas_call(
        body,
        out_shape=(jax.ShapeDtypeStruct((8, 128), F32), jax.ShapeDtypeStruct((SEQ, D_MODEL), F32),
                   jax.ShapeDtypeStruct((SEQ, D_MODEL), BF16), vec, vec),
        grid=(SEQ // ROWS,),
        in_specs=[_row_spec(D_MODEL), _row_spec(D_MODEL), _row_spec(D_MODEL), _vec_spec(D_MODEL), _vec_spec(D_MODEL, 8)],
        out_specs=(pl.BlockSpec((8, 128), lambda i: (0, 0)), _row_spec(D_MODEL), _row_spec(D_MODEL),
                   _vec_spec(D_MODEL), _vec_spec(D_MODEL)),
        name=name, compiler_params=_cp("arbitrary"))(x1, y2, tgt, g, mod)


def rms_mod_bwd(x, dh, dres, g, mod, sc_row, y, ga_row, name):
    gated = y is not None

    def body(*refs):
        if gated:
            x_ref, dh_ref, dres_ref, g_ref, mod_ref, y_ref, dx_ref, dsh_ref, dsc_ref, dg_ref, dy_ref, dga_ref = refs
        else:
            x_ref, dh_ref, dres_ref, g_ref, mod_ref, dx_ref, dsh_ref, dsc_ref, dg_ref = refs
        first = pl.program_id(0) == 0
        xx = x_ref[...]
        dh = dh_ref[...]
        gg = g_ref[...]
        r = lax.rsqrt(_mean(xx * xx) + EPS)
        xn = xx * r
        _acc(dsh_ref, _rsum(dh), first)
        _acc(dsc_ref, _rsum(dh * (xn * gg)), first)
        dt = dh * (1.0 + mod_ref[sc_row:sc_row + 1, :])
        _acc(dg_ref, _rsum(dt * xn), first)
        dxn = dt * gg
        dx = dres_ref[...] + r * (dxn - xn * _mean(dxn * xn))
        dx_ref[...] = dx
        if gated:
            _acc(dga_ref, _rsum(dx * y_ref[...]), first)
            dy_ref[...] = (dx * mod_ref[ga_row:ga_row + 1, :]).astype(BF16)

    vec = jax.ShapeDtypeStruct((1, D_MODEL), F32)
    in_specs = [_row_spec(D_MODEL), _row_spec(D_MODEL), _row_spec(D_MODEL), _vec_spec(D_MODEL), _vec_spec(D_MODEL, 8)]
    out_shape = [jax.ShapeDtypeStruct((SEQ, D_MODEL), F32), vec, vec, vec]
    out_specs = [_row_spec(D_MODEL), _vec_spec(D_MODEL), _vec_spec(D_MODEL), _vec_spec(D_MODEL)]
    args = [x, dh, dres, g, mod]
    if gated:
        in_specs.append(_row_spec(D_MODEL))
        out_shape += [jax.ShapeDtypeStruct((SEQ, D_MODEL), BF16), vec]
        out_specs += [_row_spec(D_MODEL), _vec_spec(D_MODEL)]
        args.append(y)
    return pl.pallas_call(
        body, out_shape=tuple(out_shape), grid=(SEQ // ROWS,), in_specs=in_specs, out_specs=tuple(out_specs),
        name=name, compiler_params=_cp("arbitrary"))(*args)


def _prev_halo(halo, width, col):
    per = ROWS // halo
    return pl.BlockSpec((halo, width), lambda i: (jnp.maximum(i * per - 1, 0), col))


def _next_halo(halo, width, col):
    per = ROWS // halo
    last = SEQ // halo - 1
    return pl.BlockSpec((halo, width), lambda i: (jnp.minimum((i + 1) * per, last), col))


CONV_PAD = ROWS + CONV_HALO


def _shift_copies(sh):
    for b in range(1, 8):
        sh[b, 0:CONV_PAD - 8, :] = sh[0, pl.ds(b, CONV_PAD - 8), :]


def _tap(sh, rs_start, offset):
    return sh[offset % 8, pl.ds(pl.multiple_of(rs_start + (offset // 8) * 8, 8), SUB), :]


def _conv_module_forward(av_ref, ag_ref, avh_ref, agh_ref, wc_ref, bc_ref, lg_ref, lb_ref, sh, u1_s):
    i = pl.program_id(0)
    hv = avh_ref[...] * _sig(agh_ref[...])
    sh[0, 0:CONV_HALO, :] = jnp.where(i > 0, hv, 0.0)

    def glu(rs):
        sh[0, pl.ds(pl.multiple_of(rs.start + CONV_HALO, SUB), SUB), :] = av_ref[rs, :] * _sig(ag_ref[rs, :])

    _for_chunks(glu)
    _shift_copies(sh)

    def conv(rs):
        u1 = jnp.broadcast_to(bc_ref[...], (SUB, D_CONV))
        for j in range(CONV_K):
            u1 = u1 + wc_ref[j:j + 1, :] * _tap(sh, rs.start, CONV_HALO - (CONV_K - 1) + j)
        u1_s[rs, :] = u1

    _for_chunks(conv)
    u1 = u1_s[...]
    mu = _mean(u1)
    cen = u1 - mu
    rs = lax.rsqrt(_mean(cen * cen) + EPS)
    z = cen * rs
    ln = z * lg_ref[...] + lb_ref[...]
    s = _sig(ln)
    return z, rs, ln, s, ln * s


_CONV_SCRATCH = [pltpu.VMEM((8, CONV_PAD, D_CONV), F32), pltpu.VMEM((ROWS, D_CONV), F32)]


def conv_module_fwd(proj, wc, bc, lg, lb, gco, name):
    def body(av_ref, ag_ref, avh_ref, agh_ref, wc_ref, bc_ref, lg_ref, lb_ref, gco_ref, out_ref, sh, u1_s):
        _, _, _, _, u2 = _conv_module_forward(av_ref, ag_ref, avh_ref, agh_ref, wc_ref, bc_ref, lg_ref, lb_ref, sh, u1_s)
        rc = lax.rsqrt(_mean(u2 * u2) + EPS)
        out_ref[...] = (u2 * rc * gco_ref[...]).astype(BF16)

    v = _vec_spec(D_CONV)
    return pl.pallas_call(
        body, out_shape=jax.ShapeDtypeStruct((SEQ, D_CONV), BF16), grid=(SEQ // ROWS,),
        in_specs=[_row_spec(D_CONV, 0), _row_spec(D_CONV, 1), _prev_halo(CONV_HALO, D_CONV, 0),
                  _prev_halo(CONV_HALO, D_CONV, 1), _vec_spec(D_CONV, CONV_K), v, v, v, v],
        out_specs=_row_spec(D_CONV), scratch_shapes=list(_CONV_SCRATCH),
        name=name, compiler_params=_cp("parallel"))(proj, proj, proj, proj, wc, bc, lg, lb, gco)


def conv_module_bwd_a(proj, dmixed, wc, bc, lg, lb, gco, name):
    def body(av_ref, ag_ref, avh_ref, agh_ref, dm_ref, wc_ref, bc_ref, lg_ref, lb_ref, gco_ref,
             du1_ref, dgco_ref, dlg_ref, dlb_ref, dbc_ref, dwc_ref, sh, u1_s, acc):
        first = pl.program_id(0) == 0
        z, rs, ln, s, u2 = _conv_module_forward(av_ref, ag_ref, avh_ref, agh_ref, wc_ref, bc_ref, lg_ref, lb_ref, sh, u1_s)
        rc = lax.rsqrt(_mean(u2 * u2) + EPS)
        xn = u2 * rc
        dm = dm_ref[...]
        _acc(dgco_ref, _rsum(dm * xn), first)
        dyn = dm * gco_ref[...]
        du2 = rc * (dyn - xn * _mean(dyn * xn))
        dln = du2 * (s * (1.0 + ln * (1.0 - s)))
        _acc(dlg_ref, _rsum(dln * z), first)
        _acc(dlb_ref, _rsum(dln), first)
        dz = dln * lg_ref[...]
        du1 = rs * (dz - _mean(dz) - z * _mean(dz * z))
        du1_ref[...] = du1
        _acc(dbc_ref, _rsum(du1), first)
        acc[...] = jnp.zeros_like(acc)

        def taps(rs):
            d = du1_ref[rs, :]
            for j in range(CONV_K):
                acc[j] += d * _tap(sh, rs.start, CONV_HALO - (CONV_K - 1) + j)

        _for_chunks(taps)

        @pl.when(first)
        def _():
            dwc_ref[...] = jnp.zeros_like(dwc_ref)

        for j in range(CONV_K):
            dwc_ref[j:j + 1, :] += _rsum(acc[j])

    v = _vec_spec(D_CONV)
    vec = jax.ShapeDtypeStruct((1, D_CONV), F32)
    return pl.pallas_call(
        body,
        out_shape=(jax.ShapeDtypeStruct((SEQ, D_CONV), F32), vec, vec, vec, vec, jax.ShapeDtypeStruct((CONV_K, D_CONV), F32)),
        grid=(SEQ // ROWS,),
        in_specs=[_row_spec(D_CONV, 0), _row_spec(D_CONV, 1), _prev_halo(CONV_HALO, D_CONV, 0),
                  _prev_halo(CONV_HALO, D_CONV, 1), _row_spec(D_CONV, 0), _vec_spec(D_CONV, CONV_K), v, v, v, v],
        out_specs=(_row_spec(D_CONV), v, v, v, v, _vec_spec(D_CONV, CONV_K)),
        scratch_shapes=list(_CONV_SCRATCH) + [pltpu.VMEM((CONV_K, SUB, D_CONV), F32)],
        name=name, compiler_params=_cp("arbitrary"))(proj, proj, proj, proj, dmixed, wc, bc, lg, lb, gco)


def conv_module_bwd_b(proj, du1, wc, name):
    def body(av_ref, ag_ref, du1_ref, du1n_ref, wc_ref, out_ref, sh):
        i = pl.program_id(0)
        sh[0, 0:ROWS, :] = du1_ref[...]
        sh[0, ROWS:, :] = jnp.where(i < SEQ // ROWS - 1, du1n_ref[...], 0.0)
        _shift_copies(sh)

        def chunk(rs):
            du0 = jnp.zeros((SUB, D_CONV), F32)
            for j in range(CONV_K):
                du0 = du0 + wc_ref[j:j + 1, :] * _tap(sh, rs.start, CONV_K - 1 - j)
            sg = _sig(ag_ref[rs, :])
            out_ref[rs, 0:D_CONV] = (du0 * sg).astype(BF16)
            out_ref[rs, D_CONV:] = (du0 * av_ref[rs, :] * sg * (1.0 - sg)).astype(BF16)

        _for_chunks(chunk)

    return pl.pallas_call(
        body, out_shape=jax.ShapeDtypeStruct((SEQ, 2 * D_CONV), BF16), grid=(SEQ // ROWS,),
        in_specs=[_row_spec(D_CONV, 0), _row_spec(D_CONV, 1), _row_spec(D_CONV, 0), _next_halo(CONV_HALO, D_CONV, 0),
                  _vec_spec(D_CONV, CONV_K)],
        out_specs=_row_spec(2 * D_CONV), scratch_shapes=[pltpu.VMEM((8, CONV_PAD, D_CONV), F32)],
        name=name, compiler_params=_cp("parallel"))(proj, proj, du1, du1, wc)


def _attn_specs(sub_len, pairs):
    ng, width = 4 // pairs, 128 * pairs
    q = pl.BlockSpec((sub_len, width), lambda rho, g: (0, rho * 3 * ng + g))
    k = pl.BlockSpec((sub_len, width), lambda rho, g: (0, rho * 3 * ng + ng + g))
    v = pl.BlockSpec((sub_len, width), lambda rho, g: (0, rho * 3 * ng + 2 * ng + g))
    o = pl.BlockSpec((sub_len, width), lambda rho, g: (0, rho * ng + g))
    return q, k, v, o


ATTN_PAIRS = {1: 1, 4: 1, 16: 4}


def _attn_block(q_ref, k_ref, v_ref, n, hs, win):
    q0 = pl.multiple_of(n * ATTN_BLOCK, ATTN_BLOCK)
    k0 = pl.multiple_of(jnp.maximum(n - 1, 0) * ATTN_BLOCK, ATTN_BLOCK)
    qb = q_ref[pl.ds(q0, ATTN_BLOCK), hs]
    kw = k_ref[pl.ds(k0, win), hs]
    vw = v_ref[pl.ds(k0, win), hs]
    s = lax.dot_general(qb, kw, (((1,), (1,)), ((), ())), preferred_element_type=F32) * (HEAD_DIM ** -0.5)
    dist = (q0 - k0) + lax.broadcasted_iota(jnp.int32, (ATTN_BLOCK, win), 0) \
        - lax.broadcasted_iota(jnp.int32, (ATTN_BLOCK, win), 1)
    s = jnp.where((dist >= 0) & (dist <= ATTN_BLOCK), s, NEG)
    return q0, k0, qb, kw, vw, s


def attn_fwd(qkv_r, sub_len, r, name):
    nb = sub_len // ATTN_BLOCK
    win = 2 * ATTN_BLOCK if nb > 1 else ATTN_BLOCK
    pairs = ATTN_PAIRS[r]

    def body(q_ref, k_ref, v_ref, o_ref, l_ref):
        def block(n, carry):
            for h in range(2 * pairs):
                hs = slice(h * HEAD_DIM, (h + 1) * HEAD_DIM)
                q0, _, _, _, vw, s = _attn_block(q_ref, k_ref, v_ref, n, hs, win)
                m = jnp.max(s, axis=1, keepdims=True)
                p = jnp.exp(s - m)
                den = jnp.sum(p, axis=1, keepdims=True)
                o = jnp.dot(p.astype(BF16), vw, preferred_element_type=F32) / den
                o_ref[pl.ds(q0, ATTN_BLOCK), hs] = o
                l_ref[pl.ds(q0, ATTN_BLOCK), hs] = jnp.broadcast_to(m + jnp.log(den), (ATTN_BLOCK, HEAD_DIM))
            return carry

        lax.fori_loop(0, nb, block, 0, unroll=min(nb, 2))

    q, k, v, o = _attn_specs(sub_len, pairs)
    shp = jax.ShapeDtypeStruct((sub_len, r * D_ATTN), F32)
    return pl.pallas_call(
        body, out_shape=(shp, shp), grid=(r, 4 // pairs), in_specs=[q, k, v], out_specs=(o, o),
        name=name, compiler_params=_cp("parallel", "parallel"))(qkv_r, qkv_r, qkv_r)


def attn_bwd(qkv_r, do_r, lse_r, dd_r, sub_len, r, name):
    nb = sub_len // ATTN_BLOCK
    win = 2 * ATTN_BLOCK if nb > 1 else ATTN_BLOCK
    pairs = ATTN_PAIRS[r]

    def body(q_ref, k_ref, v_ref, do_ref, l_ref, dd_ref, dq_ref, dk_ref, dv_ref):
        dk_ref[...] = jnp.zeros_like(dk_ref)
        dv_ref[...] = jnp.zeros_like(dv_ref)

        def block(n, carry):
            for h in range(2 * pairs):
                hs = slice(h * HEAD_DIM, (h + 1) * HEAD_DIM)
                h1 = slice(h * HEAD_DIM, h * HEAD_DIM + 1)
                q0, k0, qb, kw, vw, s = _attn_block(q_ref, k_ref, v_ref, n, hs, win)
                dob = do_ref[pl.ds(q0, ATTN_BLOCK), hs]
                p = jnp.exp(s - l_ref[pl.ds(q0, ATTN_BLOCK), h1])
                dp = lax.dot_general(dob, vw, (((1,), (1,)), ((), ())), preferred_element_type=F32)
                ds = (p * (dp - dd_ref[pl.ds(q0, ATTN_BLOCK), h1]) * (HEAD_DIM ** -0.5)).astype(BF16)
                dq_ref[pl.ds(q0, ATTN_BLOCK), hs] = jnp.dot(ds, kw, preferred_element_type=F32)
                dk_ref[pl.ds(k0, win), hs] += lax.dot_general(ds, qb, (((0,), (0,)), ((), ())), preferred_element_type=F32)
                dv_ref[pl.ds(k0, win), hs] += lax.dot_general(p.astype(BF16), dob, (((0,), (0,)), ((), ())),
                                                              preferred_element_type=F32)
            return carry

        lax.fori_loop(0, nb, block, 0)

    q, k, v, o = _attn_specs(sub_len, pairs)
    shp = jax.ShapeDtypeStruct((sub_len, r * D_ATTN), F32)
    return pl.pallas_call(
        body, out_shape=(shp, shp, shp), grid=(r, 4 // pairs), in_specs=[q, k, v, o, o, o], out_specs=(o, o, o),
        name=name, compiler_params=_cp("parallel", "parallel"))(qkv_r, qkv_r, qkv_r, do_r, lse_r, dd_r)


def _rows(start, size, r):
    return pl.ds(start, size) if r == 1 else pl.ds(start, size, stride=r)


def _unit_rows(r, rho, n, nb):
    win = 2 * ATTN_BLOCK if nb > 1 else ATTN_BLOCK
    if isinstance(n, int):
        kb = max(n - 1, 0)
        q_rows = _rows(rho + r * ATTN_BLOCK * n, ATTN_BLOCK, r)
        k_rows = _rows(rho + r * ATTN_BLOCK * kb, win, r)
    else:
        kb = jnp.maximum(n - 1, 0)
        q_rows = pl.ds(pl.multiple_of(n * ATTN_BLOCK, ATTN_BLOCK), ATTN_BLOCK)
        k_rows = pl.ds(pl.multiple_of(kb * ATTN_BLOCK, ATTN_BLOCK), win)
    dist = (n - kb) * ATTN_BLOCK + lax.broadcasted_iota(jnp.int32, (ATTN_BLOCK, win), 0) \
        - lax.broadcasted_iota(jnp.int32, (ATTN_BLOCK, win), 1)
    return q_rows, k_rows, (dist >= 0) & (dist <= ATTN_BLOCK)


def _heads(x):
    return [x[:, h * HEAD_DIM:(h + 1) * HEAD_DIM] for h in range(2)]


def _masked_scores(q2, k2, valid):
    return [jnp.where(valid, lax.dot_general(qh, kh, (((1,), (1,)), ((), ())), preferred_element_type=F32)
                      * (HEAD_DIM ** -0.5), NEG) for qh, kh in zip(_heads(q2), _heads(k2))]


def _attn_units(r, nb, unit):
    if r == 1:
        def four(i, carry):
            for k in range(4):
                unit(0, 4 * i + k)
            return carry
        lax.fori_loop(0, nb // 4, four, 0)
    else:
        for rho in range(r):
            for n in range(nb):
                unit(rho, n)


N_UNITS = 16


def attn_fwd_all(proj, name):
    def body(q_ref, k_ref, v_ref, att_ref, lse_ref, s_scr, p_scr, lse_scr, den_scr):
        for idx, (sub_len, r) in enumerate(PATTERNS):
            nb = sub_len // ATTN_BLOCK
            win = 2 * ATTN_BLOCK if nb > 1 else ATTN_BLOCK

            def scores(rho, n, r=r, nb=nb, win=win):
                u = rho * nb + n
                q_rows, k_rows, valid = _unit_rows(r, rho, n, nb)
                ss = _masked_scores(q_ref[q_rows, :].astype(BF16), k_ref[k_rows, :].astype(BF16), valid)
                for h in range(2):
                    s_scr[2 * u + h, :, 0:win] = ss[h]

            _attn_units(r, nb, scores)

            def softmax(u, carry, win=win):
                lses, dens = [], []
                for h in range(2):
                    sc = s_scr[2 * u + h, :, 0:win]
                    m = jnp.max(sc, axis=1, keepdims=True)
                    p = jnp.exp(sc - m)
                    den = jnp.sum(p, axis=1, keepdims=True)
                    p_scr[2 * u + h, :, 0:win] = p.astype(BF16)
                    lses.append(jnp.broadcast_to(m + jnp.log(den), (ATTN_BLOCK, HEAD_DIM)))
                    dens.append(jnp.broadcast_to(den, (ATTN_BLOCK, HEAD_DIM)))
                lse_scr[u] = jnp.concatenate(lses, axis=1)
                den_scr[u] = jnp.concatenate(dens, axis=1)
                return carry

            lax.fori_loop(0, N_UNITS, softmax, 0, unroll=2)

            def outputs(rho, n, r=r, nb=nb, win=win, idx=idx):
                u = rho * nb + n
                q_rows, k_rows, _ = _unit_rows(r, rho, n, nb)
                vs = _heads(v_ref[k_rows, :].astype(BF16))
                o = jnp.concatenate([jnp.dot(p_scr[2 * u + h, :, 0:win], vs[h], preferred_element_type=F32)
                                     for h in range(2)], axis=1) / den_scr[u]
                lse = lse_scr[u]
                if idx > 0:
                    old = lse_ref[q_rows, :]
                    top = jnp.maximum(old, lse)
                    new = top + jnp.log(jnp.exp(old - top) + jnp.exp(lse - top))
                    o = att_ref[q_rows, :] * jnp.exp(old - new) + o * jnp.exp(lse - new)
                    lse = new
                att_ref[q_rows, :] = o
                lse_ref[q_rows, :] = lse

            _attn_units(r, nb, outputs)

    blk = lambda first: pl.BlockSpec((SEQ, 128), lambda g: (0, first + g))
    shp = jax.ShapeDtypeStruct((SEQ, D_ATTN), F32)
    big = (2 * N_UNITS, ATTN_BLOCK, 2 * ATTN_BLOCK)
    small = pltpu.VMEM((N_UNITS, ATTN_BLOCK, 128), F32)
    return pl.pallas_call(
        body, out_shape=(shp, shp), grid=(4,), in_specs=[blk(8), blk(12), blk(16)], out_specs=(blk(0), blk(0)),
        scratch_shapes=[pltpu.VMEM(big, F32), pltpu.VMEM(big, BF16), small, small],
        name=name, compiler_params=_cp("parallel"))(proj, proj, proj)


def attn_bwd_all(proj, do, lse, dd, name):
    def body(q_ref, k_ref, v_ref, do_ref, l_ref, dd_ref, out_ref, dq_s, dk_s, dv_s,
             s_scr, dp_scr, p_scr, ds_scr, qb_scr, kb_scr, dob_scr):
        dq_s[...] = jnp.zeros_like(dq_s)
        dk_s[...] = jnp.zeros_like(dk_s)
        dv_s[...] = jnp.zeros_like(dv_s)
        for sub_len, r in PATTERNS:
            nb = sub_len // ATTN_BLOCK
            win = 2 * ATTN_BLOCK if nb > 1 else ATTN_BLOCK

            def scores(rho, n, r=r, nb=nb, win=win):
                u = rho * nb + n
                q_rows, k_rows, valid = _unit_rows(r, rho, n, nb)
                q2 = q_ref[q_rows, :].astype(BF16)
                k2 = k_ref[k_rows, :].astype(BF16)
                do2 = do_ref[q_rows, :].astype(BF16)
                qb_scr[u] = q2
                kb_scr[u, 0:win, :] = k2
                dob_scr[u] = do2
                l2 = l_ref[q_rows, :]
                d2 = dd_ref[q_rows, :]
                ss = _masked_scores(q2, k2, valid)
                vs = _heads(v_ref[k_rows, :].astype(BF16))
                dos = _heads(do2)
                for h in range(2):
                    h1 = slice(h * HEAD_DIM, h * HEAD_DIM + 1)
                    s_scr[2 * u + h, :, 0:win] = ss[h] - l2[:, h1]
                    dp = lax.dot_general(dos[h], vs[h], (((1,), (1,)), ((), ())), preferred_element_type=F32)
                    dp_scr[2 * u + h, :, 0:win] = dp - d2[:, h1]

            _attn_units(r, nb, scores)

            def pointwise(hu, carry, win=win):
                p = jnp.exp(s_scr[hu, :, 0:win])
                p_scr[hu, :, 0:win] = p.astype(BF16)
                ds_scr[hu, :, 0:win] = (p * dp_scr[hu, :, 0:win] * (HEAD_DIM ** -0.5)).astype(BF16)
                return carry

            lax.fori_loop(0, 2 * N_UNITS, pointwise, 0, unroll=4)

            def grads(rho, n, r=r, nb=nb, win=win):
                u = rho * nb + n
                q_rows, k_rows, _ = _unit_rows(r, rho, n, nb)
                qs, ks, dos = _heads(qb_scr[u]), _heads(kb_scr[u, 0:win, :]), _heads(dob_scr[u])
                dq, dk, dv = [], [], []
                for h in range(2):
                    ds = ds_scr[2 * u + h, :, 0:win]
                    dq.append(jnp.dot(ds, ks[h], preferred_element_type=F32))
                    dk.append(lax.dot_general(ds, qs[h], (((0,), (0,)), ((), ())), preferred_element_type=F32))
                    dv.append(lax.dot_general(p_scr[2 * u + h, :, 0:win], dos[h], (((0,), (0,)), ((), ())),
                                              preferred_element_type=F32))
                dq_s[q_rows, :] += jnp.concatenate(dq, axis=1)
                dk_s[k_rows, :] += jnp.concatenate(dk, axis=1)
                dv_s[k_rows, :] += jnp.concatenate(dv, axis=1)

            _attn_units(r, nb, grads)
        out_ref[0] = dq_s[...].astype(BF16)
        out_ref[1] = dk_s[...].astype(BF16)
        out_ref[2] = dv_s[...].astype(BF16)

    blk = lambda first: pl.BlockSpec((SEQ, 128), lambda g: (0, first + g))
    acc = pltpu.VMEM((SEQ, 128), F32)
    big = (2 * N_UNITS, ATTN_BLOCK, 2 * ATTN_BLOCK)
    return pl.pallas_call(
        body, out_shape=jax.ShapeDtypeStruct((3, SEQ, D_ATTN), BF16), grid=(4,),
        in_specs=[blk(8), blk(12), blk(16), blk(0), blk(0), blk(0)],
        out_specs=pl.BlockSpec((3, SEQ, 128), lambda g: (0, 0, g)),
        scratch_shapes=[acc, acc, acc, pltpu.VMEM(big, F32), pltpu.VMEM(big, F32), pltpu.VMEM(big, BF16),
                        pltpu.VMEM(big, BF16), pltpu.VMEM((N_UNITS, ATTN_BLOCK, 128), BF16),
                        pltpu.VMEM((N_UNITS, 2 * ATTN_BLOCK, 128), BF16), pltpu.VMEM((N_UNITS, ATTN_BLOCK, 128), BF16)],
        name=name, compiler_params=_cp("parallel"))(proj, proj, proj, do, lse, dd)


def rms_gain_bf16(a, g, name):
    def body(a_ref, g_ref, o_ref):
        aa = a_ref[...]
        o_ref[...] = (aa * lax.rsqrt(_mean(aa * aa) + EPS) * g_ref[...]).astype(BF16)

    w = a.shape[1]
    return pl.pallas_call(
        body, out_shape=jax.ShapeDtypeStruct(a.shape, BF16), grid=(SEQ // ROWS,), in_specs=[_row_spec(w), _vec_spec(w)],
        out_specs=_row_spec(w), name=name, compiler_params=_cp("parallel"))(a, g)


def attn_combine_fwd(outs, lses, gao, name):
    def body(o1, o2, o3, l1, l2, l3, g_ref, att_ref, lse_ref, mix_ref):
        a1, a2, a3 = l1[...], l2[...], l3[...]
        m = jnp.maximum(jnp.maximum(a1, a2), a3)
        w1, w2, w3 = jnp.exp(a1 - m), jnp.exp(a2 - m), jnp.exp(a3 - m)
        den = w1 + w2 + w3
        att = (w1 * o1[...] + w2 * o2[...] + w3 * o3[...]) / den
        att_ref[...] = att
        lse_ref[...] = m + jnp.log(den)
        mix_ref[...] = (att * lax.rsqrt(_mean(att * att) + EPS) * g_ref[...]).astype(BF16)

    rs = _row_spec(D_ATTN)
    f = jax.ShapeDtypeStruct((SEQ, D_ATTN), F32)
    return pl.pallas_call(
        body, out_shape=(f, f, jax.ShapeDtypeStruct((SEQ, D_ATTN), BF16)), grid=(SEQ // ROWS,),
        in_specs=[rs] * 6 + [_vec_spec(D_ATTN)], out_specs=(rs, rs, rs),
        name=name, compiler_params=_cp("parallel"))(*outs, *lses, gao)


def attn_combine_bwd(dmixed, att, gao, name):
    def body(dm_ref, att_ref, g_ref, do_ref, dd_ref, dg_ref):
        first = pl.program_id(0) == 0
        att = att_ref[...]
        r = lax.rsqrt(_mean(att * att) + EPS)
        xn = att * r
        dm = dm_ref[...]
        _acc(dg_ref, _rsum(dm * xn), first)
        dyn = dm * g_ref[...]
        do = r * (dyn - xn * _mean(dyn * xn))
        do_ref[...] = do
        same_head = (jnp.right_shift(lax.broadcasted_iota(jnp.int32, (D_ATTN, D_ATTN), 0), 6)
                     == jnp.right_shift(lax.broadcasted_iota(jnp.int32, (D_ATTN, D_ATTN), 1), 6)).astype(F32)
        dd_ref[...] = jnp.dot(do * att, same_head, preferred_element_type=F32, precision=lax.Precision.HIGHEST)

    rs = _row_spec(D_ATTN)
    return pl.pallas_call(
        body,
        out_shape=(jax.ShapeDtypeStruct((SEQ, D_ATTN), F32), jax.ShapeDtypeStruct((SEQ, D_ATTN), F32),
                   jax.ShapeDtypeStruct((1, D_ATTN), F32)),
        grid=(SEQ // ROWS,), in_specs=[_row_spec(D_ATTN, 1), rs, _vec_spec(D_ATTN)],
        out_specs=(rs, rs, _vec_spec(D_ATTN)), name=name, compiler_params=_cp("arbitrary"))(dmixed, att, gao)


def sum3_bf16(a, b, c, name):
    def body(a_ref, b_ref, c_ref, o_ref):
        o_ref[...] = (a_ref[...] + b_ref[...] + c_ref[...]).astype(BF16)

    w = a.shape[1]
    rs = _row_spec(w)
    return pl.pallas_call(
        body, out_shape=jax.ShapeDtypeStruct(a.shape, BF16), grid=(SEQ // ROWS,), in_specs=[rs, rs, rs], out_specs=rs,
        name=name, compiler_params=_cp("parallel"))(a, b, c)


N_FT = D_FF // FFN_TN


def _ffn_specs():
    per = ROWS // FFN_HALO
    cur_g = pl.BlockSpec((ROWS, FFN_TN), lambda j, i: (i, j))
    cur_v = pl.BlockSpec((ROWS, FFN_TN), lambda j, i: (i, j + N_FT))
    halo_g = pl.BlockSpec((FFN_HALO, FFN_TN), lambda j, i: (jnp.maximum(i * per - 1, 0), j))
    halo_v = pl.BlockSpec((FFN_HALO, FFN_TN), lambda j, i: (jnp.maximum(i * per - 1, 0), j + N_FT))
    w_g = pl.BlockSpec((FFN_K, FFN_TN), lambda j, i: (0, j))
    w_v = pl.BlockSpec((FFN_K, FFN_TN), lambda j, i: (0, j + N_FT))
    b_g = pl.BlockSpec((1, FFN_TN), lambda j, i: (0, j))
    b_v = pl.BlockSpec((1, FFN_TN), lambda j, i: (0, j + N_FT))
    return [cur_g, cur_v, halo_g, halo_v, w_g, w_v, b_g, b_v]


def matmul(a, b, kind, out_dtype, tm, tn, name):
    if kind == "nn":
        (m, k), n = a.shape, b.shape[1]
        a_spec = pl.BlockSpec((tm, k), lambda j, i: (i, 0))
        b_spec = pl.BlockSpec((k, tn), lambda j, i: (0, j))
        dims = (((1,), (0,)), ((), ()))
    elif kind == "nt":
        (m, k), n = a.shape, b.shape[0]
        a_spec = pl.BlockSpec((tm, k), lambda j, i: (i, 0))
        b_spec = pl.BlockSpec((tn, k), lambda j, i: (j, 0))
        dims = (((1,), (1,)), ((), ()))
    else:
        (k, m), n = a.shape, b.shape[1]
        a_spec = pl.BlockSpec((k, tm), lambda j, i: (0, i))
        b_spec = pl.BlockSpec((k, tn), lambda j, i: (0, j))
        dims = (((0,), (0,)), ((), ()))
    assert m % tm == 0 and n % tn == 0, (name, m, n, tm, tn)

    def body(a_ref, b_ref, o_ref):
        o_ref[...] = lax.dot_general(a_ref[...], b_ref[...], dims, preferred_element_type=F32).astype(o_ref.dtype)

    return pl.pallas_call(
        body, out_shape=jax.ShapeDtypeStruct((m, n), out_dtype), grid=(n // tn, m // tm),
        in_specs=[a_spec, b_spec], out_specs=pl.BlockSpec((tm, tn), lambda j, i: (i, j)),
        name=name, compiler_params=_cp("parallel", "parallel"))(a, b)


def matmul_halves(a, b, tm, tn, name):
    _, m, k = a.shape
    n = b.shape[1]
    b3 = b.reshape(2, k, n)

    def body(a_ref, b_ref, o_ref):
        o_ref[...] = (jnp.dot(a_ref[0], b_ref[0], preferred_element_type=F32)
                      + jnp.dot(a_ref[1], b_ref[1], preferred_element_type=F32))

    return pl.pallas_call(
        body, out_shape=jax.ShapeDtypeStruct((m, n), F32), grid=(n // tn, m // tm),
        in_specs=[pl.BlockSpec((2, tm, k), lambda j, i: (0, i, 0)), pl.BlockSpec((2, k, tn), lambda j, i: (0, 0, j))],
        out_specs=pl.BlockSpec((tm, tn), lambda j, i: (i, j)), name=name, compiler_params=_cp("parallel", "parallel"))(a, b3)


def matmul_tn_halves(a, b, tm, name):
    _, k, m = a.shape
    n = b.shape[1]

    def body(a_ref, b_ref, o_ref):
        o_ref[0] = lax.dot_general(a_ref[0], b_ref[...], (((0,), (0,)), ((), ())), preferred_element_type=F32).astype(BF16)

    return pl.pallas_call(
        body, out_shape=jax.ShapeDtypeStruct((2, m, n), BF16), grid=(2, m // tm),
        in_specs=[pl.BlockSpec((1, k, tm), lambda h, i: (h, 0, i)), pl.BlockSpec((k, n), lambda h, i: (0, 0))],
        out_specs=pl.BlockSpec((1, tm, n), lambda h, i: (h, i, 0)), name=name,
        compiler_params=_cp("parallel", "parallel"))(a, b).reshape(2 * m, n)


def _row_spec(width, col=0):
    return pl.BlockSpec((ROWS, width), lambda i: (i, col))


def _vec_spec(width, rows=1):
    return pl.BlockSpec((rows, width), lambda i: (0, 0))


def _ffn_shifted(cur_ref, halo_ref, pad, s1, s2):
    i = pl.program_id(1)
    pad[0:FFN_HALO, :] = jnp.where(i > 0, halo_ref[...], 0.0)
    pad[FFN_HALO:, :] = cur_ref[0:FFN_HALO, :]
    for k, dst in ((1, s1), (2, s2)):
        dst[0:FFN_HALO, :] = pad[pl.ds(FFN_HALO - k, FFN_HALO), :]
        dst[FFN_HALO:, :] = cur_ref[pl.ds(FFN_HALO - k, ROWS - FFN_HALO), :]


def _ffn_conv(rs, cur_ref, s1, s2, w_ref, b_ref):
    return b_ref[...] + w_ref[0:1, :] * s2[rs, :] + w_ref[1:2, :] * s1[rs, :] + w_ref[2:3, :] * cur_ref[rs, :]


def ffn_act_fwd(up0, wf, bf, name):
    def body(g_ref, v_ref, gh_ref, vh_ref, wg_ref, wv_ref, bg_ref, bv_ref, act_ref, pad, g1, g2, v1, v2):
        _ffn_shifted(g_ref, gh_ref, pad, g1, g2)
        _ffn_shifted(v_ref, vh_ref, pad, v1, v2)

        def chunk(rs):
            gate = _ffn_conv(rs, g_ref, g1, g2, wg_ref, bg_ref)
            val = _ffn_conv(rs, v_ref, v1, v2, wv_ref, bv_ref)
            act_ref[rs, :] = (gate * _sig(gate) * val).astype(BF16)

        _for_chunks(chunk)

    tile = pltpu.VMEM((ROWS, FFN_TN), F32)
    return pl.pallas_call(
        body, out_shape=jax.ShapeDtypeStruct((SEQ, D_FF), BF16), grid=(N_FT, SEQ // ROWS),
        in_specs=_ffn_specs(), out_specs=pl.BlockSpec((ROWS, FFN_TN), lambda j, i: (i, j)),
        scratch_shapes=[pltpu.VMEM((2 * FFN_HALO, FFN_TN), F32), tile, tile, tile, tile],
        name=name, compiler_params=_cp("parallel", "parallel"))(up0, up0, up0, up0, wf, wf, bf, bf)


def ffn_bwd(up0, dact, wf, bf, name):
    per = ROWS // FFN_HALO
    last = SEQ // FFN_HALO - 1

    def body(g_ref, v_ref, gh_ref, vh_ref, wg_ref, wv_ref, bg_ref, bv_ref, da_ref, gn_ref, vn_ref, dan_ref,
             out_ref, dbg_ref, dbv_ref, dwg_ref, dwv_ref, pad, g1, g2, v1, v2, dgp, dvp, acc):
        i = pl.program_id(1)
        first = i == 0
        _ffn_shifted(g_ref, gh_ref, pad, g1, g2)
        _ffn_shifted(v_ref, vh_ref, pad, v1, v2)
        acc[...] = jnp.zeros_like(acc)

        def grads(gate, val, da):
            s = _sig(gate)
            return da * val * (s * (1.0 + gate * (1.0 - s))), da * (gate * s)

        def chunk(rs):
            gate = _ffn_conv(rs, g_ref, g1, g2, wg_ref, bg_ref)
            val = _ffn_conv(rs, v_ref, v1, v2, wv_ref, bv_ref)
            dgate, dval = grads(gate, val, da_ref[rs, :])
            dgp[rs, :] = dgate
            dvp[rs, :] = dval
            acc[0] += dgate
            acc[1] += dval
            for t, (sg, sv) in enumerate(((g2, v2), (g1, v1), (g_ref, v_ref))):
                acc[2 + t] += dgate * sg[rs, :]
                acc[5 + t] += dval * sv[rs, :]

        _for_chunks(chunk)
        _acc(dbg_ref, _rsum(acc[0]), first)
        _acc(dbv_ref, _rsum(acc[1]), first)
        _acc(dwg_ref, jnp.concatenate([_rsum(acc[2 + t]) for t in range(FFN_K)], axis=0), first)
        _acc(dwv_ref, jnp.concatenate([_rsum(acc[5 + t]) for t in range(FFN_K)], axis=0), first)

        def conv_next(cur_ref, nxt_ref, w_ref, b_ref):
            pad[0:FFN_HALO, :] = cur_ref[ROWS - FFN_HALO:, :]
            pad[FFN_HALO:, :] = nxt_ref[...]
            return (b_ref[...] + w_ref[0:1, :] * pad[pl.ds(FFN_HALO - 2, FFN_HALO), :]
                    + w_ref[1:2, :] * pad[pl.ds(FFN_HALO - 1, FFN_HALO), :] + w_ref[2:3, :] * nxt_ref[...])

        gate_n = conv_next(g_ref, gn_ref, wg_ref, bg_ref)
        val_n = conv_next(v_ref, vn_ref, wv_ref, bv_ref)
        dgate_n, dval_n = grads(gate_n, val_n, dan_ref[...])
        inside = i < SEQ // ROWS - 1
        dgp[ROWS:, :] = jnp.where(inside, dgate_n, 0.0)
        dvp[ROWS:, :] = jnp.where(inside, dval_n, 0.0)

        for half, (dp, s1, s2, w_ref) in enumerate(((dgp, g1, g2, wg_ref), (dvp, v1, v2, wv_ref))):
            s1[...] = dp[pl.ds(1, ROWS), :]
            s2[...] = dp[pl.ds(2, ROWS), :]

            def back(rs, dp=dp, s1=s1, s2=s2, w_ref=w_ref, half=half):
                out_ref[half, rs, :] = (w_ref[2:3, :] * dp[rs, :] + w_ref[1:2, :] * s1[rs, :]
                                        + w_ref[0:1, :] * s2[rs, :]).astype(BF16)

            _for_chunks(back)

    tile = pltpu.VMEM((ROWS, FFN_TN), F32)
    ext = pltpu.VMEM((ROWS + FFN_HALO, FFN_TN), F32)
    vec = jax.ShapeDtypeStruct((1, D_FF), F32)
    taps = jax.ShapeDtypeStruct((FFN_K, D_FF), F32)
    cur = pl.BlockSpec((ROWS, FFN_TN), lambda j, i: (i, j))
    nxt = lambda off: pl.BlockSpec((FFN_HALO, FFN_TN), lambda j, i: (jnp.minimum((i + 1) * per, last), j + off))
    vs = pl.BlockSpec((1, FFN_TN), lambda j, i: (0, j))
    ts = pl.BlockSpec((FFN_K, FFN_TN), lambda j, i: (0, j))
    return pl.pallas_call(
        body, out_shape=(jax.ShapeDtypeStruct((2, SEQ, D_FF), BF16), vec, vec, taps, taps), grid=(N_FT, SEQ // ROWS),
        in_specs=_ffn_specs() + [cur, nxt(0), nxt(N_FT), nxt(0)],
        out_specs=(pl.BlockSpec((2, ROWS, FFN_TN), lambda j, i: (0, i, j)), vs, vs, ts, ts),
        scratch_shapes=[pltpu.VMEM((2 * FFN_HALO, FFN_TN), F32), tile, tile, tile, tile, ext, ext,
                        pltpu.VMEM((2 + 2 * FFN_K, SUB, FFN_TN), F32)],
        name=name, compiler_params=_cp("parallel", "arbitrary"))(up0, up0, up0, up0, wf, wf, bf, bf, dact, up0, up0, dact)


def ada_fwd(c_all, w_ada, b_cols, name):
    def body(c_ref, w_ref, b_ref, o_ref):
        cc = c_ref[...]
        sc = (cc * _sig(cc)).astype(BF16)
        o_ref[...] = jnp.dot(sc, w_ref[...].astype(BF16), preferred_element_type=F32) + b_ref[...]

    return pl.pallas_call(body, out_shape=jax.ShapeDtypeStruct((N_DEV, w_ada.shape[1]), F32), name=name,
                          compiler_params=_cp())(c_all, w_ada, b_cols)


def _adam(w, g, m, v):
    m = ADAM_B1 * m + (1.0 - ADAM_B1) * g
    v = ADAM_B2 * v + (1.0 - ADAM_B2) * (g * g)
    m_hat = m / (1.0 - ADAM_B1 ** ADAM_STEP)
    v_hat = v / (1.0 - ADAM_B2 ** ADAM_STEP)
    delta = -ADAM_LR * (m_hat / (jnp.sqrt(v_hat) + ADAM_EPS) + ADAM_WD * w)
    return delta, m, v


def ada_bwd_adamw(c_all_t, dmod_cols, w, m, v, name):
    rows, cols = w.shape
    tr = 256

    def body(ct_ref, dm_ref, w_ref, m_ref, v_ref, g_ref, d_ref, nm_ref, nv_ref):
        ct = ct_ref[...]
        sc = ct * _sig(ct)
        g = sc[:, 0:1] * dm_ref[0:1, :]
        for b in range(1, N_DEV):
            g = g + sc[:, b:b + 1] * dm_ref[b:b + 1, :]
        g_ref[...] = g
        d_ref[...], nm_ref[...], nv_ref[...] = _adam(w_ref[...], g, m_ref[...], v_ref[...])

    blk = pl.BlockSpec((tr, cols), lambda i: (i, 0))
    shp = jax.ShapeDtypeStruct((rows, cols), F32)
    return pl.pallas_call(
        body, out_shape=(shp, shp, shp, shp), grid=(rows // tr,),
        in_specs=[pl.BlockSpec((tr, N_DEV), lambda i: (i, 0)), pl.BlockSpec((N_DEV, cols), lambda i: (0, 0)), blk, blk, blk],
        out_specs=(blk, blk, blk, blk), name=name, compiler_params=_cp("parallel"))(c_all_t, dmod_cols, w, m, v)


def sum_adamw(parts, own, w, m, v, tr, name):
    n_parts, rows, cols = parts.shape

    def body(*refs):
        if own is None:
            p_ref, w_ref, m_ref, v_ref, g_ref, d_ref, nm_ref, nv_ref = refs
            g = p_ref[0].astype(F32)
        else:
            p_ref, own_ref, w_ref, m_ref, v_ref, g_ref, d_ref, nm_ref, nv_ref = refs
            g = own_ref[...].astype(F32)
        for k in range(1, n_parts):
            g = g + p_ref[k].astype(F32)
        g_ref[...] = g
        d_ref[...], nm_ref[...], nv_ref[...] = _adam(w_ref[...], g, m_ref[...], v_ref[...])

    blk = pl.BlockSpec((tr, cols), lambda i: (i, 0))
    shp = jax.ShapeDtypeStruct((rows, cols), F32)
    args = [parts] + ([] if own is None else [own]) + [w, m, v]
    return pl.pallas_call(
        body, out_shape=(shp, shp, shp, shp), grid=(rows // tr,),
        in_specs=[pl.BlockSpec((n_parts, tr, cols), lambda i: (0, i, 0))] + [blk] * (len(args) - 1),
        out_specs=(blk, blk, blk, blk), name=name, compiler_params=_cp("parallel"))(*args)


MESH = pl.DeviceIdType.MESH
ANY = pl.BlockSpec(memory_space=pl.ANY)


def all_gather(block, name, after=None):
    extra = () if after is None else (after,)

    def body(x_ref, *refs):
        out_ref, send_sems, recv_sems, local_sem = refs[len(extra):]
        x, y, c = lax.axis_index("x"), lax.axis_index("y"), lax.axis_index("c")
        me, sibling = (x, y, c), (x, y, 1 - c)
        chips = [(1 - x, y), (x, 1 - y), (1 - x, 1 - y)]

        def slot(px, py, pc):
            return out_ref.at[4 * px + 2 * py + pc]

        def copy(k, blk, to, src=None):
            return pltpu.make_async_remote_copy(
                src_ref=slot(*blk) if src is None else src, dst_ref=slot(*blk),
                send_sem=send_sems.at[k], recv_sem=recv_sems.at[k], device_id=to, device_id_type=MESH)

        mine = pltpu.make_async_copy(x_ref, slot(*me), local_sem)
        mine.start()
        first = [copy(0, me, sibling, src=x_ref)]
        first += [copy(1 + j, me, (*chip, c), src=x_ref) for j, chip in enumerate(chips)]
        for cp in first:
            cp.start()
        passed = [copy(4 + j, (*chip, c), sibling) for j, chip in enumerate(chips)]
        for j, chip in enumerate(chips):
            copy(1 + j, (*chip, c), me).wait_recv()
            passed[j].start()
        copy(0, sibling, me).wait_recv()
        for j, chip in enumerate(chips):
            copy(4 + j, (*chip, 1 - c), me).wait_recv()
        for cp in first + passed:
            cp.wait_send()
        mine.wait()

    return pl.pallas_call(
        body, out_shape=jax.ShapeDtypeStruct((N_DEV,) + block.shape, block.dtype), in_specs=[ANY] * (1 + len(extra)), out_specs=ANY,
        scratch_shapes=[pltpu.SemaphoreType.DMA((7,)), pltpu.SemaphoreType.DMA((7,)), pltpu.SemaphoreType.DMA],
        name=name)(block, *extra)


HBM = pl.BlockSpec(memory_space=pltpu.HBM)
SEM = pl.BlockSpec(memory_space=pltpu.SEMAPHORE)
EFFECT = pltpu.SideEffectType.DATAFLOW_SIDE_EFFECTING


def _peer_copies(src_ref, land_ref, send_sems, recv_sems, gather):
    x, y, c = lax.axis_index("x"), lax.axis_index("y"), lax.axis_index("c")
    me = 4 * x + 2 * y + c
    copies = []
    for k in range(1, N_DEV):
        px = 1 - x if k & 4 else x
        py = 1 - y if k & 2 else y
        pc = 1 - c if k & 1 else c
        copies.append(pltpu.make_async_remote_copy(
            src_ref=src_ref if gather else src_ref.at[4 * px + 2 * py + pc],
            dst_ref=land_ref.at[me] if gather else land_ref.at[k],
            send_sem=send_sems.at[k - 1], recv_sem=recv_sems.at[k - 1], device_id=(px, py, pc), device_id_type=MESH))
    return copies


def exchange_start(src, gather, name, after=None):
    land_shape = (N_DEV,) + src.shape if gather else src.shape
    extra = () if after is None else (after,)

    def body(src_ref, land_ref, *refs):
        send_sems, recv_sems, src_thru, land_thru, token = refs[len(extra):]
        for cp in _peer_copies(src_ref, land_ref, send_sems, recv_sems, gather):
            cp.start()
        token[...] = jnp.zeros_like(token)

    send_sems, recv_sems, src_thru, land_thru, token = pl.pallas_call(
        body, name=name,
        out_shape=(pltpu.SemaphoreType.DMA((N_DEV - 1,)), pltpu.SemaphoreType.DMA((N_DEV - 1,)),
                   pltpu.HBM(src.shape, src.dtype), pltpu.HBM(land_shape, src.dtype), jax.ShapeDtypeStruct((8, 128), F32)),
        in_specs=(HBM, HBM) + (ANY,) * len(extra), out_specs=(SEM, SEM, HBM, HBM, pl.BlockSpec(memory_space=pltpu.VMEM)),
        input_output_aliases={0: 2, 1: 3}, compiler_params=pltpu.CompilerParams(has_side_effects=EFFECT),
    )(pltpu.with_memory_space_constraint(src, pltpu.HBM),
      pltpu.with_memory_space_constraint(lax.empty(land_shape, src.dtype), pltpu.HBM), *extra)
    return (send_sems, recv_sems, src_thru, land_thru), token[0, 0]


def exchange_wait(handles, after, gather, name):
    send_sems, recv_sems, src_thru, land_thru = handles

    def body(src_ref, land_ref, send_sems, recv_sems, after_ref, src_dead, got_ref):
        for cp in _peer_copies(src_ref, land_ref, send_sems, recv_sems, gather):
            cp.wait_send()
            cp.wait_recv()

    return pl.pallas_call(
        body, name=name,
        out_shape=(pltpu.HBM(src_thru.shape, src_thru.dtype), pltpu.HBM(land_thru.shape, land_thru.dtype)),
        in_specs=(HBM, HBM, SEM, SEM, ANY), out_specs=(HBM, HBM), input_output_aliases={0: 0, 1: 1},
        compiler_params=pltpu.CompilerParams(has_side_effects=EFFECT),
    )(src_thru, land_thru, send_sems, recv_sems, after)[1]


def _to_pattern(a, r):
    return a.reshape(SEQ // r, r * a.shape[1])


def local_step(x, tgt, mod, get_w, put_grad, wc, wf, g_mix, bc, lg, lb, gco, gao, g_ffn, bf, g_fin):
    h1 = rms_mod_fwd(x, g_mix, mod, 0, 1, "h1_fwd")
    w_in = get_w("w_in", h1)
    proj = matmul(h1, w_in, "nt", F32, 512, D_IN, "proj_fwd")
    mix_a = conv_module_fwd(proj, wc, bc, lg, lb, gco, "conv_module_fwd")
    att, lse = attn_fwd_all(proj, "attn_fwd")
    mix_b = rms_gain_bf16(att, gao, "attn_out_norm")
    mixed = jnp.concatenate([mix_a, mix_b], axis=1)
    w_out = get_w("w_out", mixed)
    y1 = matmul(mixed, w_out, "nn", F32, 512, D_MODEL, "out_proj_fwd")
    x1, h2 = resid_rms_mod_fwd(x, y1, g_ffn, mod, 2, 3, 4, "x1_h2_fwd")
    w_up = get_w("w_up", h2)
    up0 = matmul(h2, w_up, "nt", F32, 512, D_FF, "up_fwd")
    act = ffn_act_fwd(up0, wf, bf, "ffn_act_fwd")
    w_down = get_w("w_down", act)
    y2 = matmul(act, w_down, "nn", F32, 512, D_MODEL, "down_fwd")
    loss_t, dx2, dy2, d_gfin, d_gaf = final_loss_bwd(x1, y2, tgt, g_fin, mod, 5, "loss_bwd")
    dact = matmul(dy2, w_down, "nt", F32, 512, FFN_TN, "down_bwd_x")
    dw_down = matmul(act, dy2, "tn", BF16, 256, D_MODEL, "down_bwd_w")
    dup0, dbf_g, dbf_v, dwf_g, dwf_v = ffn_bwd(up0, dact, wf + put_grad("w_down", dw_down), bf, "ffn_bwd")
    dh2 = matmul_halves(dup0, w_up, 512, 512, "up_bwd_x")
    dw_up = matmul_tn_halves(dup0, h2, 256, "up_bwd_w")
    dx1, d_shf, d_scf, d_gffn, dy1, d_gam = rms_mod_bwd(x1, dh2, dx2, g_ffn + put_grad("w_up", dw_up), mod, 4, y1, 2, "h2_bwd")
    dmixed = matmul(dy1, w_out, "nt", F32, 512, D_MODEL, "out_proj_bwd_x")
    dw_out = matmul(mixed, dy1, "tn", BF16, 256, D_MODEL, "out_proj_bwd_w")
    do, dd, d_gao = attn_combine_bwd(dmixed, att, gao + put_grad("w_out", dw_out), "attn_combine_bwd")
    dqkv = attn_bwd_all(proj, do, lse, dd, "attn_bwd")
    du1, d_gco, d_lg, d_lb, d_bc, d_wc = conv_module_bwd_a(proj, dmixed, wc, bc, lg, lb, gco, "conv_module_bwd_a")
    dproj_a = conv_module_bwd_b(proj, du1, wc, "conv_module_bwd_b")
    dproj = jnp.concatenate([dproj_a, dqkv[0], dqkv[1], dqkv[2]], axis=1)
    dw_in = matmul(dproj, h1, "tn", BF16, 512, D_MODEL, "proj_bwd_w")
    dh1 = matmul(dproj, w_in, "nn", F32, 512, D_MODEL, "proj_bwd_x")
    dx, d_shm, d_scm, d_gmix = rms_mod_bwd(x, dh1, dx1, g_mix + put_grad("w_in", dw_in), mod, 1, None, 0, "h1_bwd")
    dmod = jnp.concatenate([d_shm, d_scm, d_gam, d_shf, d_scf, d_gaf], axis=1)
    small = dict(g_norm_mix=d_gmix, b_conv_dw=d_bc, ln_conv_g=d_lg, ln_conv_b=d_lb, g_conv_out=d_gco, g_attn_out=d_gao,
                 g_norm_ffn=d_gffn, b_ffn_dw=jnp.concatenate([dbf_g, dbf_v], axis=1), g_final=d_gfin,
                 w_conv_dw=d_wc, w_ffn_dw=jnp.concatenate([dwf_g, dwf_v], axis=1), dmod=dmod, loss=loss_t[0:1, 0:1])
    return dx, small


def _padw(a, width):
    return jnp.pad(a, ((0, 0), (0, width - a.shape[1])))


def pack_small(t):
    wide = jnp.concatenate([_padw(t["dmod"], PACK_W), _padw(t["b_ffn_dw"], PACK_W), _padw(t["w_ffn_dw"], PACK_W),
                            _padw(t["loss"], PACK_W), jnp.zeros((2, PACK_W), F32)], axis=0)
    z512 = jnp.zeros((1, 512), F32)
    narrow = jnp.concatenate([
        t["g_norm_mix"], t["g_norm_ffn"], t["g_final"],
        jnp.concatenate([t["b_conv_dw"], t["ln_conv_g"]], axis=1),
        jnp.concatenate([t["ln_conv_b"], t["g_conv_out"]], axis=1),
        jnp.concatenate([t["g_attn_out"], z512], axis=1),
        jnp.zeros((2, 1024), F32),
        jnp.pad(t["w_conv_dw"], ((0, 1), (0, 0))).reshape(16, 1024)], axis=0)
    return jnp.concatenate([wide, narrow.reshape(4, PACK_W), jnp.zeros((4, PACK_W), F32)], axis=0)


def unpack_small(p):
    narrow = p[8:12].reshape(24, 1024)
    return dict(
        dmod=p[0:1], b_ffn_dw=p[1:2, :2 * D_FF], w_ffn_dw=p[2:5, :2 * D_FF], loss=p[5, 0],
        g_norm_mix=narrow[0:1], g_norm_ffn=narrow[1:2], g_final=narrow[2:3],
        b_conv_dw=narrow[3:4, :512], ln_conv_g=narrow[3:4, 512:], ln_conv_b=narrow[4:5, :512], g_conv_out=narrow[4:5, 512:],
        g_attn_out=narrow[5:6, :512], w_conv_dw=narrow[8:24].reshape(32, 512)[:CONV_K])


def _embed(local, width, me):
    return lax.dynamic_update_slice(jnp.zeros((local.shape[0], width), F32), local, (0, me * local.shape[1]))


def _shard(full, n_cols, me):
    return lax.dynamic_slice(full, (0, me * n_cols), (full.shape[0], n_cols))


WEIGHTS = ["w_ada", "b_ada", "g_norm_mix", "w_in", "w_conv_dw", "b_conv_dw", "ln_conv_g", "ln_conv_b", "g_conv_out",
           "g_attn_out", "w_out", "g_norm_ffn", "w_up", "w_ffn_dw", "b_ffn_dw", "w_down", "g_final"]
SMALL_REPLICATED = ["g_norm_mix", "b_conv_dw", "ln_conv_g", "ln_conv_b", "g_conv_out", "g_attn_out", "g_norm_ffn",
                    "b_ffn_dw", "g_final"]


def kernel(x, c, w_ada, b_ada, g_norm_mix, w_in, w_conv_dw, b_conv_dw, ln_conv_g, ln_conv_b, g_conv_out, g_attn_out, w_out, g_norm_ffn, w_up, w_ffn_dw, b_ffn_dw, w_down, g_final, loss_target, m_w_ada, m_b_ada, m_g_norm_mix, m_w_in, m_w_conv_dw, m_b_conv_dw, m_ln_conv_g, m_ln_conv_b, m_g_conv_out, m_g_attn_out, m_w_out, m_g_norm_ffn, m_w_up, m_w_ffn_dw, m_b_ffn_dw, m_w_down, m_g_final, v_w_ada, v_b_ada, v_g_norm_mix, v_w_in, v_w_conv_dw, v_b_conv_dw, v_ln_conv_g, v_ln_conv_b, v_g_conv_out, v_g_attn_out, v_w_out, v_g_norm_ffn, v_w_up, v_w_ffn_dw, v_b_ffn_dw, v_w_down, v_g_final):
    args = dict(locals())
    me = 4 * lax.axis_index("x") + 2 * lax.axis_index("y") + lax.axis_index("c")

    def flat(name, prefix=""):
        a = args[prefix + name]
        return a.reshape(a.shape[-2] if a.ndim > 1 else 1, a.shape[-1])

    n_in, n_up, r_out, r_down = w_in.shape[2], w_up.shape[2], w_out.shape[1], w_down.shape[1]
    n_ada, n_wc, n_wf = w_ada.shape[2], w_conv_dw.shape[2], w_ffn_dw.shape[2]
    taps_c = jnp.pad(flat("w_conv_dw").reshape(1, CONV_K * n_wc), ((0, 0), (0, 2 * D_MODEL - CONV_K * n_wc)))
    taps_f = jnp.pad(flat("w_ffn_dw").reshape(1, FFN_K * n_wf), ((0, 0), (0, 3 * D_MODEL - FFN_K * n_wf)))
    first = jnp.concatenate([c, taps_c.reshape(2, D_MODEL), taps_f.reshape(3, D_MODEL), jnp.zeros((2, D_MODEL), F32)], axis=0)
    first_all = all_gather(first, "gather_c_taps")
    c_all = first_all[:, 0, :]
    wc_full = first_all[:, 1:3, :].reshape(N_DEV, 2 * D_MODEL)[:, :CONV_K * n_wc].reshape(N_DEV, CONV_K, n_wc)
    wc_full = wc_full.transpose(1, 0, 2).reshape(CONV_K, D_CONV)
    wf_full = first_all[:, 3:6, :].reshape(N_DEV, 3 * D_MODEL)[:, :FFN_K * n_wf].reshape(N_DEV, FFN_K, n_wf)
    wf_full = wf_full.transpose(1, 0, 2).reshape(FFN_K, 2 * D_FF)
    mod_cols = ada_fwd(c_all, flat("w_ada"), _shard(flat("b_ada"), n_ada, me), "ada_fwd")
    mod_all = all_gather(mod_cols, "gather_mod")
    mod = lax.dynamic_index_in_dim(mod_all, me, axis=1, keepdims=False).reshape(N_MOD, D_MODEL)
    mod = jnp.pad(mod, ((0, 2), (0, 0)))

    def flat_t(name, prefix=""):
        return args[prefix + name][0].T

    blocks = dict(w_in=flat_t("w_in").astype(BF16), w_up=flat_t("w_up").astype(BF16),
                  rows=jnp.concatenate([flat("w_out"), flat("w_down")], axis=0).astype(BF16))
    gathers, tok = {}, None
    for name in ("w_in", "rows", "w_up"):
        src = blocks[name] if tok is None else blocks[name] + tok.astype(BF16)
        gathers[name], tok = exchange_start(src, True, f"gather_{name}_start", mod_all)
    mod = mod + tok
    full = {}

    slot = lax.broadcasted_iota(jnp.int32, (N_DEV, 1, 1), 0)

    def gathered(name, after):
        land = exchange_wait(gathers[name], after, True, f"gather_{name}_wait")
        return jnp.where(slot == me, blocks[name][None], land)

    def get_w(name, after):
        if name in ("w_out", "w_down"):
            if "rows" not in full:
                full["rows"] = gathered("rows", after)
            rows = full["rows"]
            return rows[:, :r_out, :].reshape(D_MODEL, D_MODEL) if name == "w_out" else rows[:, r_out:, :].reshape(D_FF, D_MODEL)
        return gathered(name, after).reshape(-1, D_MODEL)

    exchanges, own = {}, {}

    def put_grad(name, dw, after=None):
        dev_major = dw.reshape(N_DEV, -1, D_MODEL)
        own[name] = lax.dynamic_index_in_dim(dev_major, me, axis=0, keepdims=False)
        exchanges[name], token = exchange_start(dev_major, False, f"exchange_{name}_start", after)
        return token

    grad_x, small = local_step(
        x[0], loss_target[0], mod, get_w, put_grad, wc_full, wf_full,
        flat("g_norm_mix"), flat("b_conv_dw"), flat("ln_conv_g"), flat("ln_conv_b"), flat("g_conv_out"),
        flat("g_attn_out"), flat("g_norm_ffn"), flat("b_ffn_dw"), flat("g_final"))

    out = {}

    def finish(name, tr, after):
        parts = exchange_wait(exchanges[name], after, False, f"exchange_{name}_wait")
        if name in ("w_in", "w_up"):
            res = sum_adamw(parts, own[name], flat_t(name), flat_t(name, "m_"), flat_t(name, "v_"), tr, "adamw_" + name)
            out[name] = tuple(r.T for r in res)
        else:
            res = out[name] = sum_adamw(parts, own[name], flat(name), flat(name, "m_"), flat(name, "v_"), tr, "adamw_" + name)
        return res[0]

    after = finish("w_down", r_down, grad_x)
    after = finish("w_up", n_up // 2, after)
    after = finish("w_out", r_out, after)
    after = finish("w_in", n_in, after)

    small_all = all_gather(pack_small(small), "gather_small", after)

    def packed(prefix):
        t = {n: flat(n, prefix) for n in SMALL_REPLICATED}
        t["dmod"] = flat("b_ada", prefix)
        t["loss"] = jnp.zeros((1, 1), F32)
        t["w_conv_dw"] = _embed(flat("w_conv_dw", prefix), D_CONV, me)
        t["w_ffn_dw"] = _embed(flat("w_ffn_dw", prefix), 2 * D_FF, me)
        return pack_small(t)

    res = sum_adamw(small_all, None, packed(""), packed("m_"), packed("v_"), PACK_ROWS, "adamw_small")
    res = [unpack_small(r) for r in res]
    loss = res[0]["loss"]
    for n in SMALL_REPLICATED:
        out[n] = tuple(r[n] for r in res)
    out["b_ada"] = tuple(r["dmod"] for r in res)
    out["w_conv_dw"] = tuple(_shard(r["w_conv_dw"], n_wc, me) for r in res)
    out["w_ffn_dw"] = tuple(_shard(r["w_ffn_dw"], n_wf, me) for r in res)

    dmod_cols = _shard(small_all[:, 0, :], n_ada, me)
    out["w_ada"] = ada_bwd_adamw(c_all.T, dmod_cols, flat("w_ada"), flat("w_ada", "m_"), flat("w_ada", "v_"), "adamw_w_ada")

    result = [loss, grad_x[None]]
    for k in range(4):
        result += [out[n][k].reshape(args[n].shape) for n in WEIGHTS]
    return tuple(result)
```

```python
import functools

import jax
import jax.numpy as jnp
from jax import lax
from jax.experimental import pallas as pl
from jax.experimental.pallas import tpu as pltpu

F32 = jnp.float32
BF16 = jnp.bfloat16

N_DEV = 8
SEQ = 2048
D_MODEL = 1024
D_CONV = 512
D_ATTN = 512
HEAD_DIM = 64
CONV_K = 31
D_FF = 2816
FFN_K = 3
D_IN = 2 * D_CONV + 3 * D_ATTN
N_MOD = 6
EPS = 1e-6
ATTN_BLOCK = 128
PATTERNS = ((2048, 1), (512, 4), (128, 16))
NEG = -1e30

ADAM_LR, ADAM_B1, ADAM_B2, ADAM_EPS, ADAM_WD, ADAM_STEP = 0.001, 0.9, 0.999, 1e-08, 0.01, 10

ROWS = 256
CONV_HALO = 32
FFN_HALO = 8
FFN_TN = 1408
VMEM_LIMIT = 56 * 1024 * 1024
PACK_ROWS, PACK_W = 16, 6144


def _cp(*sem):
    return pltpu.CompilerParams(dimension_semantics=sem if sem else None, vmem_limit_bytes=VMEM_LIMIT)


def _sig(x):
    return 1.0 / (1.0 + jnp.exp(-x))


def _rsum(x):
    return jnp.sum(x, axis=0, keepdims=True)


def _mean(x):
    return jnp.mean(x, axis=-1, keepdims=True)


def _acc(ref, val, first):
    @pl.when(first)
    def _():
        ref[...] = val

    @pl.when(jnp.logical_not(first))
    def _():
        ref[...] += val


SUB = 16


def _for_chunks(fn, unroll=1):
    def step(i, carry):
        fn(pl.ds(pl.multiple_of(i * SUB, SUB), SUB))
        return carry

    lax.fori_loop(0, ROWS // SUB, step, 0, unroll=unroll)


def rms_mod_fwd(x, g, mod, sh_row, sc_row, name):
    def body(x_ref, g_ref, mod_ref, h_ref):
        xx = x_ref[...]
        r = lax.rsqrt(_mean(xx * xx) + EPS)
        h = xx * r * g_ref[...]
        h_ref[...] = (h * (1.0 + mod_ref[sc_row:sc_row + 1, :]) + mod_ref[sh_row:sh_row + 1, :]).astype(BF16)

    return pl.pallas_call(
        body, out_shape=jax.ShapeDtypeStruct((SEQ, D_MODEL), BF16), grid=(SEQ // ROWS,),
        in_specs=[_row_spec(D_MODEL), _vec_spec(D_MODEL), _vec_spec(D_MODEL, 8)],
        out_specs=_row_spec(D_MODEL), name=name, compiler_params=_cp("parallel"))(x, g, mod)


def resid_rms_mod_fwd(x, y, g, mod, ga_row, sh_row, sc_row, name):
    def body(x_ref, y_ref, g_ref, mod_ref, x1_ref, h_ref):
        x1 = x_ref[...] + mod_ref[ga_row:ga_row + 1, :] * y_ref[...]
        x1_ref[...] = x1
        r = lax.rsqrt(_mean(x1 * x1) + EPS)
        h = x1 * r * g_ref[...]
        h_ref[...] = (h * (1.0 + mod_ref[sc_row:sc_row + 1, :]) + mod_ref[sh_row:sh_row + 1, :]).astype(BF16)

    return pl.pallas_call(
        body, out_shape=(jax.ShapeDtypeStruct((SEQ, D_MODEL), F32), jax.ShapeDtypeStruct((SEQ, D_MODEL), BF16)),
        grid=(SEQ // ROWS,),
        in_specs=[_row_spec(D_MODEL), _row_spec(D_MODEL), _vec_spec(D_MODEL), _vec_spec(D_MODEL, 8)],
        out_specs=(_row_spec(D_MODEL), _row_spec(D_MODEL)), name=name, compiler_params=_cp("parallel"))(x, y, g, mod)


def final_loss_bwd(x1, y2, tgt, g, mod, ga_row, name):
    def body(x1_ref, y2_ref, t_ref, g_ref, mod_ref, loss_ref, dx2_ref, dy2_ref, dg_ref, dga_ref):
        first = pl.program_id(0) == 0
        ga = mod_ref[ga_row:ga_row + 1, :]
        y2 = y2_ref[...]
        x2 = x1_ref[...] + ga * y2
        r = lax.rsqrt(_mean(x2 * x2) + EPS)
        xn = x2 * r
        err = xn * g_ref[...] - t_ref[...]
        _acc(loss_ref, jnp.broadcast_to(0.5 * jnp.sum(_mean(err * err)), (8, 128)), first)
        dy = err * (1.0 / D_MODEL)
        _acc(dg_ref, _rsum(dy * xn), first)
        dxn = dy * g_ref[...]
        dx2 = r * (dxn - xn * _mean(dxn * xn))
        dx2_ref[...] = dx2
        dy2_ref[...] = (dx2 * ga).astype(BF16)
        _acc(dga_ref, _rsum(dx2 * y2), first)

    vec = jax.ShapeDtypeStruct((1, D_MODEL), F32)
    return pl.pallas_call(
        body,
        out_shape=(jax.ShapeDtypeStruct((8, 128), F32), jax.ShapeDtypeStruct((SEQ, D_MODEL), F32),
                   jax.ShapeDtypeStruct((SEQ, D_MODEL), BF16), vec, vec),
        grid=(SEQ // ROWS,),
        in_specs=[_row_spec(D_MODEL), _row_spec(D_MODEL), _row_spec(D_MODEL), _vec_spec(D_MODEL), _vec_spec(D_MODEL, 8)],
        out_specs=(pl.BlockSpec((8, 128), lambda i: (0, 0)), _row_spec(D_MODEL), _row_spec(D_MODEL),
                   _vec_spec(D_MODEL), _vec_spec(D_MODEL)),
        name=name, compiler_params=_cp("arbitrary"))(x1, y2, tgt, g, mod)


def rms_mod_bwd(x, dh, dres, g, mod, sc_row, y, ga_row, name):
    gated = y is not None

    def body(*refs):
        if gated:
            x_ref, dh_ref, dres_ref, g_ref, mod_ref, y_ref, dx_ref, dsh_ref, dsc_ref, dg_ref, dy_ref, dga_ref = refs
        else:
            x_ref, dh_ref, dres_ref, g_ref, mod_ref, dx_ref, dsh_ref, dsc_ref, dg_ref = refs
        first = pl.program_id(0) == 0
        xx = x_ref[...]
        dh = dh_ref[...]
        gg = g_ref[...]
        r = lax.rsqrt(_mean(xx * xx) + EPS)
        xn = xx * r
        _acc(dsh_ref, _rsum(dh), first)
        _acc(dsc_ref, _rsum(dh * (xn * gg)), first)
        dt = dh * (1.0 + mod_ref[sc_row:sc_row + 1, :])
        _acc(dg_ref, _rsum(dt * xn), first)
        dxn = dt * gg
        dx = dres_ref[...] + r * (dxn - xn * _mean(dxn * xn))
        dx_ref[...] = dx
        if gated:
            _acc(dga_ref, _rsum(dx * y_ref[...]), first)
            dy_ref[...] = (dx * mod_ref[ga_row:ga_row + 1, :]).astype(BF16)

    vec = jax.ShapeDtypeStruct((1, D_MODEL), F32)
    in_specs = [_row_spec(D_MODEL), _row_spec(D_MODEL), _row_spec(D_MODEL), _vec_spec(D_MODEL), _vec_spec(D_MODEL, 8)]
    out_shape = [jax.ShapeDtypeStruct((SEQ, D_MODEL), F32), vec, vec, vec]
    out_specs = [_row_spec(D_MODEL), _vec_spec(D_MODEL), _vec_spec(D_MODEL), _vec_spec(D_MODEL)]
    args = [x, dh, dres, g, mod]
    if gated:
        in_specs.append(_row_spec(D_MODEL))
        out_shape += [jax.ShapeDtypeStruct((SEQ, D_MODEL), BF16), vec]
        out_specs += [_row_spec(D_MODEL), _vec_spec(D_MODEL)]
        args.append(y)
    return pl.pallas_call(
        body, out_shape=tuple(out_shape), grid=(SEQ // ROWS,), in_specs=in_specs, out_specs=tuple(out_specs),
        name=name, compiler_params=_cp("arbitrary"))(*args)


def _prev_halo(halo, width, col):
    per = ROWS // halo
    return pl.BlockSpec((halo, width), lambda i: (jnp.maximum(i * per - 1, 0), col))


def _next_halo(halo, width, col):
    per = ROWS // halo
    last = SEQ // halo - 1
    return pl.BlockSpec((halo, width), lambda i: (jnp.minimum((i + 1) * per, last), col))


CONV_PAD = ROWS + CONV_HALO


def _shift_copies(sh):
    for b in range(1, 8):
        sh[b, 0:CONV_PAD - 8, :] = sh[0, pl.ds(b, CONV_PAD - 8), :]


def _tap(sh, rs_start, offset):
    return sh[offset % 8, pl.ds(pl.multiple_of(rs_start + (offset // 8) * 8, 8), SUB), :]


def _conv_module_forward(av_ref, ag_ref, avh_ref, agh_ref, wc_ref, bc_ref, lg_ref, lb_ref, sh, u1_s):
    i = pl.program_id(0)
    hv = avh_ref[...] * _sig(agh_ref[...])
    sh[0, 0:CONV_HALO, :] = jnp.where(i > 0, hv, 0.0)

    def glu(rs):
        sh[0, pl.ds(pl.multiple_of(rs.start + CONV_HALO, SUB), SUB), :] = av_ref[rs, :] * _sig(ag_ref[rs, :])

    _for_chunks(glu)
    _shift_copies(sh)

    def conv(rs):
        u1 = jnp.broadcast_to(bc_ref[...], (SUB, D_CONV))
        for j in range(CONV_K):
            u1 = u1 + wc_ref[j:j + 1, :] * _tap(sh, rs.start, CONV_HALO - (CONV_K - 1) + j)
        u1_s[rs, :] = u1

    _for_chunks(conv)
    u1 = u1_s[...]
    mu = _mean(u1)
    cen = u1 - mu
    rs = lax.rsqrt(_mean(cen * cen) + EPS)
    z = cen * rs
    ln = z * lg_ref[...] + lb_ref[...]
    s = _sig(ln)
    return z, rs, ln, s, ln * s


_CONV_SCRATCH = [pltpu.VMEM((8, CONV_PAD, D_CONV), F32), pltpu.VMEM((ROWS, D_CONV), F32)]


def conv_module_fwd(proj, wc, bc, lg, lb, gco, name):
    def body(av_ref, ag_ref, avh_ref, agh_ref, wc_ref, bc_ref, lg_ref, lb_ref, gco_ref, out_ref, sh, u1_s):
        _, _, _, _, u2 = _conv_module_forward(av_ref, ag_ref, avh_ref, agh_ref, wc_ref, bc_ref, lg_ref, lb_ref, sh, u1_s)
        rc = lax.rsqrt(_mean(u2 * u2) + EPS)
        out_ref[...] = (u2 * rc * gco_ref[...]).astype(BF16)

    v = _vec_spec(D_CONV)
    return pl.pallas_call(
        body, out_shape=jax.ShapeDtypeStruct((SEQ, D_CONV), BF16), grid=(SEQ // ROWS,),
        in_specs=[_row_spec(D_CONV, 0), _row_spec(D_CONV, 1), _prev_halo(CONV_HALO, D_CONV, 0),
                  _prev_halo(CONV_HALO, D_CONV, 1), _vec_spec(D_CONV, CONV_K), v, v, v, v],
        out_specs=_row_spec(D_CONV), scratch_shapes=list(_CONV_SCRATCH),
        name=name, compiler_params=_cp("parallel"))(proj, proj, proj, proj, wc, bc, lg, lb, gco)


def conv_module_bwd_a(proj, dmixed, wc, bc, lg, lb, gco, name):
    def body(av_ref, ag_ref, avh_ref, agh_ref, dm_ref, wc_ref, bc_ref, lg_ref, lb_ref, gco_ref,
             du1_ref, dgco_ref, dlg_ref, dlb_ref, dbc_ref, dwc_ref, sh, u1_s, acc):
        first = pl.program_id(0) == 0
        z, rs, ln, s, u2 = _conv_module_forward(av_ref, ag_ref, avh_ref, agh_ref, wc_ref, bc_ref, lg_ref, lb_ref, sh, u1_s)
        rc = lax.rsqrt(_mean(u2 * u2) + EPS)
        xn = u2 * rc
        dm = dm_ref[...]
        _acc(dgco_ref, _rsum(dm * xn), first)
        dyn = dm * gco_ref[...]
        du2 = rc * (dyn - xn * _mean(dyn * xn))
        dln = du2 * (s * (1.0 + ln * (1.0 - s)))
        _acc(dlg_ref, _rsum(dln * z), first)
        _acc(dlb_ref, _rsum(dln), first)
        dz = dln * lg_ref[...]
        du1 = rs * (dz - _mean(dz) - z * _mean(dz * z))
        du1_ref[...] = du1
        _acc(dbc_ref, _rsum(du1), first)
        acc[...] = jnp.zeros_like(acc)

        def taps(rs):
            d = du1_ref[rs, :]
            for j in range(CONV_K):
                acc[j] += d * _tap(sh, rs.start, CONV_HALO - (CONV_K - 1) + j)

        _for_chunks(taps)

        @pl.when(first)
        def _():
            dwc_ref[...] = jnp.zeros_like(dwc_ref)

        for j in range(CONV_K):
            dwc_ref[j:j + 1, :] += _rsum(acc[j])

    v = _vec_spec(D_CONV)
    vec = jax.ShapeDtypeStruct((1, D_CONV), F32)
    return pl.pallas_call(
        body,
        out_shape=(jax.ShapeDtypeStruct((SEQ, D_CONV), F32), vec, vec, vec, vec, jax.ShapeDtypeStruct((CONV_K, D_CONV), F32)),
        grid=(SEQ // ROWS,),
        in_specs=[_row_spec(D_CONV, 0), _row_spec(D_CONV, 1), _prev_halo(CONV_HALO, D_CONV, 0),
                  _prev_halo(CONV_HALO, D_CONV, 1), _row_spec(D_CONV, 0), _vec_spec(D_CONV, CONV_K), v, v, v, v],
        out_specs=(_row_spec(D_CONV), v, v, v, v, _vec_spec(D_CONV, CONV_K)),
        scratch_shapes=list(_CONV_SCRATCH) + [pltpu.VMEM((CONV_K, SUB, D_CONV), F32)],
        name=name, compiler_params=_cp("arbitrary"))(proj, proj, proj, proj, dmixed, wc, bc, lg, lb, gco)


def conv_module_bwd_b(proj, du1, wc, name):
    def body(av_ref, ag_ref, du1_ref, du1n_ref, wc_ref, out_ref, sh):
        i = pl.program_id(0)
        sh[0, 0:ROWS, :] = du1_ref[...]
        sh[0, ROWS:, :] = jnp.where(i < SEQ // ROWS - 1, du1n_ref[...], 0.0)
        _shift_copies(sh)

        def chunk(rs):
            du0 = jnp.zeros((SUB, D_CONV), F32)
            for j in range(CONV_K):
                du0 = du0 + wc_ref[j:j + 1, :] * _tap(sh, rs.start, CONV_K - 1 - j)
            sg = _sig(ag_ref[rs, :])
            out_ref[rs, 0:D_CONV] = (du0 * sg).astype(BF16)
            out_ref[rs, D_CONV:] = (du0 * av_ref[rs, :] * sg * (1.0 - sg)).astype(BF16)

        _for_chunks(chunk)

    return pl.pallas_call(
        body, out_shape=jax.ShapeDtypeStruct((SEQ, 2 * D_CONV), BF16), grid=(SEQ // ROWS,),
        in_specs=[_row_spec(D_CONV, 0), _row_spec(D_CONV, 1), _row_spec(D_CONV, 0), _next_halo(CONV_HALO, D_CONV, 0),
                  _vec_spec(D_CONV, CONV_K)],
        out_specs=_row_spec(2 * D_CONV), scratch_shapes=[pltpu.VMEM((8, CONV_PAD, D_CONV), F32)],
        name=name, compiler_params=_cp("parallel"))(proj, proj, du1, du1, wc)


def _attn_specs(sub_len, pairs):
    ng, width = 4 // pairs, 128 * pairs
    q = pl.BlockSpec((sub_len, width), lambda rho, g: (0, rho * 3 * ng + g))
    k = pl.BlockSpec((sub_len, width), lambda rho, g: (0, rho * 3 * ng + ng + g))
    v = pl.BlockSpec((sub_len, width), lambda rho, g: (0, rho * 3 * ng + 2 * ng + g))
    o = pl.BlockSpec((sub_len, width), lambda rho, g: (0, rho * ng + g))
    return q, k, v, o


ATTN_PAIRS = {1: 1, 4: 1, 16: 4}


def _attn_block(q_ref, k_ref, v_ref, n, hs, win):
    q0 = pl.multiple_of(n * ATTN_BLOCK, ATTN_BLOCK)
    k0 = pl.multiple_of(jnp.maximum(n - 1, 0) * ATTN_BLOCK, ATTN_BLOCK)
    qb = q_ref[pl.ds(q0, ATTN_BLOCK), hs]
    kw = k_ref[pl.ds(k0, win), hs]
    vw = v_ref[pl.ds(k0, win), hs]
    s = lax.dot_general(qb, kw, (((1,), (1,)), ((), ())), preferred_element_type=F32) * (HEAD_DIM ** -0.5)
    dist = (q0 - k0) + lax.broadcasted_iota(jnp.int32, (ATTN_BLOCK, win), 0) \
        - lax.broadcasted_iota(jnp.int32, (ATTN_BLOCK, win), 1)
    s = jnp.where((dist >= 0) & (dist <= ATTN_BLOCK), s, NEG)
    return q0, k0, qb, kw, vw, s


def attn_fwd(qkv_r, sub_len, r, name):
    nb = sub_len // ATTN_BLOCK
    win = 2 * ATTN_BLOCK if nb > 1 else ATTN_BLOCK
    pairs = ATTN_PAIRS[r]

    def body(q_ref, k_ref, v_ref, o_ref, l_ref):
        def block(n, carry):
            for h in range(2 * pairs):
                hs = slice(h * HEAD_DIM, (h + 1) * HEAD_DIM)
                q0, _, _, _, vw, s = _attn_block(q_ref, k_ref, v_ref, n, hs, win)
                m = jnp.max(s, axis=1, keepdims=True)
                p = jnp.exp(s - m)
                den = jnp.sum(p, axis=1, keepdims=True)
                o = jnp.dot(p.astype(BF16), vw, preferred_element_type=F32) / den
                o_ref[pl.ds(q0, ATTN_BLOCK), hs] = o
                l_ref[pl.ds(q0, ATTN_BLOCK), hs] = jnp.broadcast_to(m + jnp.log(den), (ATTN_BLOCK, HEAD_DIM))
            return carry

        lax.fori_loop(0, nb, block, 0, unroll=min(nb, 2))

    q, k, v, o = _attn_specs(sub_len, pairs)
    shp = jax.ShapeDtypeStruct((sub_len, r * D_ATTN), F32)
    return pl.pallas_call(
        body, out_shape=(shp, shp), grid=(r, 4 // pairs), in_specs=[q, k, v], out_specs=(o, o),
        name=name, compiler_params=_cp("parallel", "parallel"))(qkv_r, qkv_r, qkv_r)


def attn_bwd(qkv_r, do_r, lse_r, dd_r, sub_len, r, name):
    nb = sub_len // ATTN_BLOCK
    win = 2 * ATTN_BLOCK if nb > 1 else ATTN_BLOCK
    pairs = ATTN_PAIRS[r]

    def body(q_ref, k_ref, v_ref, do_ref, l_ref, dd_ref, dq_ref, dk_ref, dv_ref):
        dk_ref[...] = jnp.zeros_like(dk_ref)
        dv_ref[...] = jnp.zeros_like(dv_ref)

        def block(n, carry):
            for h in range(2 * pairs):
                hs = slice(h * HEAD_DIM, (h + 1) * HEAD_DIM)
                h1 = slice(h * HEAD_DIM, h * HEAD_DIM + 1)
                q0, k0, qb, kw, vw, s = _attn_block(q_ref, k_ref, v_ref, n, hs, win)
                dob = do_ref[pl.ds(q0, ATTN_BLOCK), hs]
                p = jnp.exp(s - l_ref[pl.ds(q0, ATTN_BLOCK), h1])
                dp = lax.dot_general(dob, vw, (((1,), (1,)), ((), ())), preferred_element_type=F32)
                ds = (p * (dp - dd_ref[pl.ds(q0, ATTN_BLOCK), h1]) * (HEAD_DIM ** -0.5)).astype(BF16)
                dq_ref[pl.ds(q0, ATTN_BLOCK), hs] = jnp.dot(ds, kw, preferred_element_type=F32)
                dk_ref[pl.ds(k0, win), hs] += lax.dot_general(ds, qb, (((0,), (0,)), ((), ())), preferred_element_type=F32)
                dv_ref[pl.ds(k0, win), hs] += lax.dot_general(p.astype(BF16), dob, (((0,), (0,)), ((), ())),
                                                              preferred_element_type=F32)
            return carry

        lax.fori_loop(0, nb, block, 0)

    q, k, v, o = _attn_specs(sub_len, pairs)
    shp = jax.ShapeDtypeStruct((sub_len, r * D_ATTN), F32)
    return pl.pallas_call(
        body, out_shape=(shp, shp, shp), grid=(r, 4 // pairs), in_specs=[q, k, v, o, o, o], out_specs=(o, o, o),
        name=name, compiler_params=_cp("parallel", "parallel"))(qkv_r, qkv_r, qkv_r, do_r, lse_r, dd_r)


def _rows(start, size, r):
    return pl.ds(start, size) if r == 1 else pl.ds(start, size, stride=r)


def _unit_rows(r, rho, n, nb):
    win = 2 * ATTN_BLOCK if nb > 1 else ATTN_BLOCK
    if isinstance(n, int):
        kb = max(n - 1, 0)
        q_rows = _rows(rho + r * ATTN_BLOCK * n, ATTN_BLOCK, r)
        k_rows = _rows(rho + r * ATTN_BLOCK * kb, win, r)
    else:
        kb = jnp.maximum(n - 1, 0)
        q_rows = pl.ds(pl.multiple_of(n * ATTN_BLOCK, ATTN_BLOCK), ATTN_BLOCK)
        k_rows = pl.ds(pl.multiple_of(kb * ATTN_BLOCK, ATTN_BLOCK), win)
    dist = (n - kb) * ATTN_BLOCK + lax.broadcasted_iota(jnp.int32, (ATTN_BLOCK, win), 0) \
        - lax.broadcasted_iota(jnp.int32, (ATTN_BLOCK, win), 1)
    return q_rows, k_rows, (dist >= 0) & (dist <= ATTN_BLOCK)


def _per_head(x):
    lane = lax.broadcasted_iota(jnp.int32, x.shape, 1)
    zero = jnp.zeros_like(x)
    return [jnp.where(lane < HEAD_DIM, x, zero), jnp.where(lane >= HEAD_DIM, x, zero)]


NT = (((1,), (1,)), ((), ()))


def _masked_scores(q2, k2, valid):
    return [jnp.where(valid, lax.dot_general(qh, k2, NT, preferred_element_type=F32) * (HEAD_DIM ** -0.5), NEG)
            for qh in _per_head(q2)]


def _attn_units(r, nb, unit):
    if r == 1:
        def four(i, carry):
            for k in range(4):
                unit(0, 4 * i + k)
            return carry
        lax.fori_loop(0, nb // 4, four, 0)
    else:
        for rho in range(r):
            for n in range(nb):
                unit(rho, n)


N_UNITS = 16


def attn_fwd_all(proj, name):
    def body(q_ref, k_ref, v_ref, att_ref, lse_ref, s_scr, p_scr, lse_scr, den_scr):
        for idx, (sub_len, r) in enumerate(PATTERNS):
            nb = sub_len // ATTN_BLOCK
            win = 2 * ATTN_BLOCK if nb > 1 else ATTN_BLOCK

            def scores(rho, n, r=r, nb=nb, win=win):
                u = rho * nb + n
                q_rows, k_rows, valid = _unit_rows(r, rho, n, nb)
                ss = _masked_scores(q_ref[q_rows, :].astype(BF16), k_ref[k_rows, :].astype(BF16), valid)
                for h in range(2):
                    s_scr[2 * u + h, :, 0:win] = ss[h]

            _attn_units(r, nb, scores)

            def softmax(u, carry, win=win):
                lses, dens = [], []
                for h in range(2):
                    sc = s_scr[2 * u + h, :, 0:win]
                    m = jnp.max(sc, axis=1, keepdims=True)
                    p = jnp.exp(sc - m)
                    den = jnp.sum(p, axis=1, keepdims=True)
                    p_scr[2 * u + h, :, 0:win] = p.astype(BF16)
                    lses.append(jnp.broadcast_to(m + jnp.log(den), (ATTN_BLOCK, HEAD_DIM)))
                    dens.append(jnp.broadcast_to(den, (ATTN_BLOCK, HEAD_DIM)))
                lse_scr[u] = jnp.concatenate(lses, axis=1)
                den_scr[u] = jnp.concatenate(dens, axis=1)
                return carry

            lax.fori_loop(0, N_UNITS, softmax, 0, unroll=2)

            def outputs(rho, n, r=r, nb=nb, win=win, idx=idx):
                u = rho * nb + n
                q_rows, k_rows, _ = _unit_rows(r, rho, n, nb)
                vs = _per_head(v_ref[k_rows, :].astype(BF16))
                o = (jnp.dot(p_scr[2 * u, :, 0:win], vs[0], preferred_element_type=F32)
                     + jnp.dot(p_scr[2 * u + 1, :, 0:win], vs[1], preferred_element_type=F32)) / den_scr[u]
                lse = lse_scr[u]
                if idx > 0:
                    old = lse_ref[q_rows, :]
                    top = jnp.maximum(old, lse)
                    new = top + jnp.log(jnp.exp(old - top) + jnp.exp(lse - top))
                    o = att_ref[q_rows, :] * jnp.exp(old - new) + o * jnp.exp(lse - new)
                    lse = new
                att_ref[q_rows, :] = o
                lse_ref[q_rows, :] = lse

            _attn_units(r, nb, outputs)

    blk = lambda first: pl.BlockSpec((SEQ, 128), lambda g: (0, first + g))
    shp = jax.ShapeDtypeStruct((SEQ, D_ATTN), F32)
    big = (2 * N_UNITS, ATTN_BLOCK, 2 * ATTN_BLOCK)
    small = pltpu.VMEM((N_UNITS, ATTN_BLOCK, 128), F32)
    return pl.pallas_call(
        body, out_shape=(shp, shp), grid=(4,), in_specs=[blk(8), blk(12), blk(16)], out_specs=(blk(0), blk(0)),
        scratch_shapes=[pltpu.VMEM(big, F32), pltpu.VMEM(big, BF16), small, small],
        name=name, compiler_params=_cp("parallel"))(proj, proj, proj)


def attn_bwd_all(proj, do, lse, dd, name):
    scale = HEAD_DIM ** -0.5

    def body(q_ref, k_ref, v_ref, do_ref, l_ref, dd_ref, out_ref, dq_s, dk_s, dv_s,
             s_scr, dp_scr, ds_scr, st_scr, dpt_scr, pt_scr, dst_scr, qb_scr, kb_scr, dob_scr):
        dq_s[...] = jnp.zeros_like(dq_s)
        dk_s[...] = jnp.zeros_like(dk_s)
        dv_s[...] = jnp.zeros_like(dv_s)
        for sub_len, r in PATTERNS:
            nb = sub_len // ATTN_BLOCK
            win = 2 * ATTN_BLOCK if nb > 1 else ATTN_BLOCK

            def scores(rho, n, r=r, nb=nb, win=win):
                u = rho * nb + n
                q_rows, k_rows, valid = _unit_rows(r, rho, n, nb)
                kb = max(n - 1, 0) if isinstance(n, int) else jnp.maximum(n - 1, 0)
                dist_t = (n - kb) * ATTN_BLOCK + lax.broadcasted_iota(jnp.int32, (win, ATTN_BLOCK), 1) \
                    - lax.broadcasted_iota(jnp.int32, (win, ATTN_BLOCK), 0)
                valid_t = (dist_t >= 0) & (dist_t <= ATTN_BLOCK)
                q2 = q_ref[q_rows, :].astype(BF16)
                k2 = k_ref[k_rows, :].astype(BF16)
                do2 = do_ref[q_rows, :].astype(BF16)
                qb_scr[u] = q2
                kb_scr[u, 0:win, :] = k2
                dob_scr[u] = do2
                l2 = l_ref[q_rows, :]
                d2 = dd_ref[q_rows, :]
                l2t = l2.T
                d2t = d2.T
                v2 = v_ref[k_rows, :].astype(BF16)
                qs, dos = _per_head(q2), _per_head(do2)
                for h in range(2):
                    c0 = h * HEAD_DIM
                    sc = lax.dot_general(qs[h], k2, NT, preferred_element_type=F32) * scale
                    s_scr[2 * u + h, :, 0:win] = jnp.where(valid, sc, NEG) - l2[:, c0:c0 + 1]
                    dp_scr[2 * u + h, :, 0:win] = lax.dot_general(dos[h], v2, NT, preferred_element_type=F32) \
                        - d2[:, c0:c0 + 1]
                    sct = lax.dot_general(k2, qs[h], NT, preferred_element_type=F32) * scale
                    st_scr[2 * u + h, 0:win, :] = jnp.where(valid_t, sct, NEG) - l2t[c0:c0 + 1, :]
                    dpt_scr[2 * u + h, 0:win, :] = lax.dot_general(v2, dos[h], NT, preferred_element_type=F32) \
                        - d2t[c0:c0 + 1, :]

            _attn_units(r, nb, scores)

            def pointwise(hu, carry, win=win):
                ds_scr[hu, :, 0:win] = (jnp.exp(s_scr[hu, :, 0:win]) * dp_scr[hu, :, 0:win] * scale).astype(BF16)
                pt = jnp.exp(st_scr[hu, 0:win, :])
                pt_scr[hu, 0:win, :] = pt.astype(BF16)
                dst_scr[hu, 0:win, :] = (pt * dpt_scr[hu, 0:win, :] * scale).astype(BF16)
                return carry

            lax.fori_loop(0, 2 * N_UNITS, pointwise, 0, unroll=4)

            def grads(rho, n, r=r, nb=nb, win=win):
                u = rho * nb + n
                q_rows, k_rows, _ = _unit_rows(r, rho, n, nb)
                qs, ks, dos = _per_head(qb_scr[u]), _per_head(kb_scr[u, 0:win, :]), _per_head(dob_scr[u])

                def both(scr, rows, rhs):
                    return (jnp.dot(scr[(2 * u,) + rows], rhs[0], preferred_element_type=F32)
                            + jnp.dot(scr[(2 * u + 1,) + rows], rhs[1], preferred_element_type=F32))

                dq_s[q_rows, :] += both(ds_scr, (slice(None), slice(0, win)), ks)
                dk_s[k_rows, :] += both(dst_scr, (slice(0, win), slice(None)), qs)
                dv_s[k_rows, :] += both(pt_scr, (slice(0, win), slice(None)), dos)

            _attn_units(r, nb, grads)
        out_ref[0] = dq_s[...].astype(BF16)
        out_ref[1] = dk_s[...].astype(BF16)
        out_ref[2] = dv_s[...].astype(BF16)

    blk = lambda first: pl.BlockSpec((SEQ, 128), lambda g: (0, first + g))
    acc = pltpu.VMEM((SEQ, 128), F32)
    big = (2 * N_UNITS, ATTN_BLOCK, 2 * ATTN_BLOCK)
    big_t = (2 * N_UNITS, 2 * ATTN_BLOCK, ATTN_BLOCK)
    return pl.pallas_call(
        body, out_shape=jax.ShapeDtypeStruct((3, SEQ, D_ATTN), BF16), grid=(4,),
        in_specs=[blk(8), blk(12), blk(16), blk(0), blk(0), blk(0)],
        out_specs=pl.BlockSpec((3, SEQ, 128), lambda g: (0, 0, g)),
        scratch_shapes=[acc, acc, acc, pltpu.VMEM(big, F32), pltpu.VMEM(big, F32), pltpu.VMEM(big, BF16),
                        pltpu.VMEM(big_t, F32), pltpu.VMEM(big_t, F32), pltpu.VMEM(big_t, BF16), pltpu.VMEM(big_t, BF16),
                        pltpu.VMEM((N_UNITS, ATTN_BLOCK, 128), BF16),
                        pltpu.VMEM((N_UNITS, 2 * ATTN_BLOCK, 128), BF16), pltpu.VMEM((N_UNITS, ATTN_BLOCK, 128), BF16)],
        name=name, compiler_params=_cp("parallel"))(proj, proj, proj, do, lse, dd)


def rms_gain_bf16(a, g, name):
    def body(a_ref, g_ref, o_ref):
        aa = a_ref[...]
        o_ref[...] = (aa * lax.rsqrt(_mean(aa * aa) + EPS) * g_ref[...]).astype(BF16)

    w = a.shape[1]
    return pl.pallas_call(
        body, out_shape=jax.ShapeDtypeStruct(a.shape, BF16), grid=(SEQ // ROWS,), in_specs=[_row_spec(w), _vec_spec(w)],
        out_specs=_row_spec(w), name=name, compiler_params=_cp("parallel"))(a, g)


def attn_combine_fwd(outs, lses, gao, name):
    def body(o1, o2, o3, l1, l2, l3, g_ref, att_ref, lse_ref, mix_ref):
        a1, a2, a3 = l1[...], l2[...], l3[...]
        m = jnp.maximum(jnp.maximum(a1, a2), a3)
        w1, w2, w3 = jnp.exp(a1 - m), jnp.exp(a2 - m), jnp.exp(a3 - m)
        den = w1 + w2 + w3
        att = (w1 * o1[...] + w2 * o2[...] + w3 * o3[...]) / den
        att_ref[...] = att
        lse_ref[...] = m + jnp.log(den)
        mix_ref[...] = (att * lax.rsqrt(_mean(att * att) + EPS) * g_ref[...]).astype(BF16)

    rs = _row_spec(D_ATTN)
    f = jax.ShapeDtypeStruct((SEQ, D_ATTN), F32)
    return pl.pallas_call(
        body, out_shape=(f, f, jax.ShapeDtypeStruct((SEQ, D_ATTN), BF16)), grid=(SEQ // ROWS,),
        in_specs=[rs] * 6 + [_vec_spec(D_ATTN)], out_specs=(rs, rs, rs),
        name=name, compiler_params=_cp("parallel"))(*outs, *lses, gao)


def attn_combine_bwd(dmixed, att, gao, name):
    def body(dm_ref, att_ref, g_ref, do_ref, dd_ref, dg_ref):
        first = pl.program_id(0) == 0
        att = att_ref[...]
        r = lax.rsqrt(_mean(att * att) + EPS)
        xn = att * r
        dm = dm_ref[...]
        _acc(dg_ref, _rsum(dm * xn), first)
        dyn = dm * g_ref[...]
        do = r * (dyn - xn * _mean(dyn * xn))
        do_ref[...] = do
        same_head = (jnp.right_shift(lax.broadcasted_iota(jnp.int32, (D_ATTN, D_ATTN), 0), 6)
                     == jnp.right_shift(lax.broadcasted_iota(jnp.int32, (D_ATTN, D_ATTN), 1), 6)).astype(F32)
        dd_ref[...] = jnp.dot(do * att, same_head, preferred_element_type=F32, precision=lax.Precision.HIGHEST)

    rs = _row_spec(D_ATTN)
    return pl.pallas_call(
        body,
        out_shape=(jax.ShapeDtypeStruct((SEQ, D_ATTN), F32), jax.ShapeDtypeStruct((SEQ, D_ATTN), F32),
                   jax.ShapeDtypeStruct((1, D_ATTN), F32)),
        grid=(SEQ // ROWS,), in_specs=[_row_spec(D_ATTN, 1), rs, _vec_spec(D_ATTN)],
        out_specs=(rs, rs, _vec_spec(D_ATTN)), name=name, compiler_params=_cp("arbitrary"))(dmixed, att, gao)


def sum3_bf16(a, b, c, name):
    def body(a_ref, b_ref, c_ref, o_ref):
        o_ref[...] = (a_ref[...] + b_ref[...] + c_ref[...]).astype(BF16)

    w = a.shape[1]
    rs = _row_spec(w)
    return pl.pallas_call(
        body, out_shape=jax.ShapeDtypeStruct(a.shape, BF16), grid=(SEQ // ROWS,), in_specs=[rs, rs, rs], out_specs=rs,
        name=name, compiler_params=_cp("parallel"))(a, b, c)


N_FT = D_FF // FFN_TN


def _ffn_specs():
    per = ROWS // FFN_HALO
    cur_g = pl.BlockSpec((ROWS, FFN_TN), lambda j, i: (i, j))
    cur_v = pl.BlockSpec((ROWS, FFN_TN), lambda j, i: (i, j + N_FT))
    halo_g = pl.BlockSpec((FFN_HALO, FFN_TN), lambda j, i: (jnp.maximum(i * per - 1, 0), j))
    halo_v = pl.BlockSpec((FFN_HALO, FFN_TN), lambda j, i: (jnp.maximum(i * per - 1, 0), j + N_FT))
    w_g = pl.BlockSpec((FFN_K, FFN_TN), lambda j, i: (0, j))
    w_v = pl.BlockSpec((FFN_K, FFN_TN), lambda j, i: (0, j + N_FT))
    b_g = pl.BlockSpec((1, FFN_TN), lambda j, i: (0, j))
    b_v = pl.BlockSpec((1, FFN_TN), lambda j, i: (0, j + N_FT))
    return [cur_g, cur_v, halo_g, halo_v, w_g, w_v, b_g, b_v]


def matmul(a, b, kind, out_dtype, tm, tn, name):
    if kind == "nn":
        (m, k), n = a.shape, b.shape[1]
        a_spec = pl.BlockSpec((tm, k), lambda j, i: (i, 0))
        b_spec = pl.BlockSpec((k, tn), lambda j, i: (0, j))
        dims = (((1,), (0,)), ((), ()))
    elif kind == "nt":
        (m, k), n = a.shape, b.shape[0]
        a_spec = pl.BlockSpec((tm, k), lambda j, i: (i, 0))
        b_spec = pl.BlockSpec((tn, k), lambda j, i: (j, 0))
        dims = (((1,), (1,)), ((), ()))
    else:
        (k, m), n = a.shape, b.shape[1]
        a_spec = pl.BlockSpec((k, tm), lambda j, i: (0, i))
        b_spec = pl.BlockSpec((k, tn), lambda j, i: (0, j))
        dims = (((0,), (0,)), ((), ()))
    assert m % tm == 0 and n % tn == 0, (name, m, n, tm, tn)

    def body(a_ref, b_ref, o_ref):
        o_ref[...] = lax.dot_general(a_ref[...], b_ref[...], dims, preferred_element_type=F32).astype(o_ref.dtype)

    return pl.pallas_call(
        body, out_shape=jax.ShapeDtypeStruct((m, n), out_dtype), grid=(n // tn, m // tm),
        in_specs=[a_spec, b_spec], out_specs=pl.BlockSpec((tm, tn), lambda j, i: (i, j)),
        name=name, compiler_params=_cp("parallel", "parallel"))(a, b)


def matmul_halves(a, b, tm, tn, name):
    _, m, k = a.shape
    n = b.shape[1]
    b3 = b.reshape(2, k, n)

    def body(a_ref, b_ref, o_ref):
        o_ref[...] = (jnp.dot(a_ref[0], b_ref[0], preferred_element_type=F32)
                      + jnp.dot(a_ref[1], b_ref[1], preferred_element_type=F32))

    return pl.pallas_call(
        body, out_shape=jax.ShapeDtypeStruct((m, n), F32), grid=(n // tn, m // tm),
        in_specs=[pl.BlockSpec((2, tm, k), lambda j, i: (0, i, 0)), pl.BlockSpec((2, k, tn), lambda j, i: (0, 0, j))],
        out_specs=pl.BlockSpec((tm, tn), lambda j, i: (i, j)), name=name, compiler_params=_cp("parallel", "parallel"))(a, b3)


def matmul_tn_halves(a, b, tm, name):
    _, k, m = a.shape
    n = b.shape[1]

    def body(a_ref, b_ref, o_ref):
        o_ref[0] = lax.dot_general(a_ref[0], b_ref[...], (((0,), (0,)), ((), ())), preferred_element_type=F32).astype(BF16)

    return pl.pallas_call(
        body, out_shape=jax.ShapeDtypeStruct((2, m, n), BF16), grid=(2, m // tm),
        in_specs=[pl.BlockSpec((1, k, tm), lambda h, i: (h, 0, i)), pl.BlockSpec((k, n), lambda h, i: (0, 0))],
        out_specs=pl.BlockSpec((1, tm, n), lambda h, i: (h, i, 0)), name=name,
        compiler_params=_cp("parallel", "parallel"))(a, b).reshape(2 * m, n)


def _row_spec(width, col=0):
    return pl.BlockSpec((ROWS, width), lambda i: (i, col))


def _vec_spec(width, rows=1):
    return pl.BlockSpec((rows, width), lambda i: (0, 0))


def _ffn_shifted(cur_ref, halo_ref, pad, s1, s2):
    i = pl.program_id(1)
    pad[0:FFN_HALO, :] = jnp.where(i > 0, halo_ref[...], 0.0)
    pad[FFN_HALO:, :] = cur_ref[0:FFN_HALO, :]
    for k, dst in ((1, s1), (2, s2)):
        dst[0:FFN_HALO, :] = pad[pl.ds(FFN_HALO - k, FFN_HALO), :]
        dst[FFN_HALO:, :] = cur_ref[pl.ds(FFN_HALO - k, ROWS - FFN_HALO), :]


def _ffn_conv(rs, cur_ref, s1, s2, w_ref, b_ref):
    return b_ref[...] + w_ref[0:1, :] * s2[rs, :] + w_ref[1:2, :] * s1[rs, :] + w_ref[2:3, :] * cur_ref[rs, :]


def ffn_act_fwd(up0, wf, bf, name):
    def body(g_ref, v_ref, gh_ref, vh_ref, wg_ref, wv_ref, bg_ref, bv_ref, act_ref, pad, g1, g2, v1, v2):
        _ffn_shifted(g_ref, gh_ref, pad, g1, g2)
        _ffn_shifted(v_ref, vh_ref, pad, v1, v2)

        def chunk(rs):
            gate = _ffn_conv(rs, g_ref, g1, g2, wg_ref, bg_ref)
            val = _ffn_conv(rs, v_ref, v1, v2, wv_ref, bv_ref)
            act_ref[rs, :] = (gate * _sig(gate) * val).astype(BF16)

        _for_chunks(chunk)

    tile = pltpu.VMEM((ROWS, FFN_TN), F32)
    return pl.pallas_call(
        body, out_shape=jax.ShapeDtypeStruct((SEQ, D_FF), BF16), grid=(N_FT, SEQ // ROWS),
        in_specs=_ffn_specs(), out_specs=pl.BlockSpec((ROWS, FFN_TN), lambda j, i: (i, j)),
        scratch_shapes=[pltpu.VMEM((2 * FFN_HALO, FFN_TN), F32), tile, tile, tile, tile],
        name=name, compiler_params=_cp("parallel", "parallel"))(up0, up0, up0, up0, wf, wf, bf, bf)


def ffn_bwd(up0, dact, wf, bf, name):
    per = ROWS // FFN_HALO
    last = SEQ // FFN_HALO - 1

    def body(g_ref, v_ref, gh_ref, vh_ref, wg_ref, wv_ref, bg_ref, bv_ref, da_ref, gn_ref, vn_ref, dan_ref,
             out_ref, dbg_ref, dbv_ref, dwg_ref, dwv_ref, pad, g1, g2, v1, v2, dgp, dvp, acc):
        i = pl.program_id(1)
        first = i == 0
        _ffn_shifted(g_ref, gh_ref, pad, g1, g2)
        _ffn_shifted(v_ref, vh_ref, pad, v1, v2)
        acc[...] = jnp.zeros_like(acc)

        def grads(gate, val, da):
            s = _sig(gate)
            return da * val * (s * (1.0 + gate * (1.0 - s))), da * (gate * s)

        def chunk(rs):
            gate = _ffn_conv(rs, g_ref, g1, g2, wg_ref, bg_ref)
            val = _ffn_conv(rs, v_ref, v1, v2, wv_ref, bv_ref)
            dgate, dval = grads(gate, val, da_ref[rs, :])
            dgp[rs, :] = dgate
            dvp[rs, :] = dval
            acc[0] += dgate
            acc[1] += dval
            for t, (sg, sv) in enumerate(((g2, v2), (g1, v1), (g_ref, v_ref))):
                acc[2 + t] += dgate * sg[rs, :]
                acc[5 + t] += dval * sv[rs, :]

        _for_chunks(chunk)
        _acc(dbg_ref, _rsum(acc[0]), first)
        _acc(dbv_ref, _rsum(acc[1]), first)
        _acc(dwg_ref, jnp.concatenate([_rsum(acc[2 + t]) for t in range(FFN_K)], axis=0), first)
        _acc(dwv_ref, jnp.concatenate([_rsum(acc[5 + t]) for t in range(FFN_K)], axis=0), first)

        def conv_next(cur_ref, nxt_ref, w_ref, b_ref):
            pad[0:FFN_HALO, :] = cur_ref[ROWS - FFN_HALO:, :]
            pad[FFN_HALO:, :] = nxt_ref[...]
            return (b_ref[...] + w_ref[0:1, :] * pad[pl.ds(FFN_HALO - 2, FFN_HALO), :]
                    + w_ref[1:2, :] * pad[pl.ds(FFN_HALO - 1, FFN_HALO), :] + w_ref[2:3, :] * nxt_ref[...])

        gate_n = conv_next(g_ref, gn_ref, wg_ref, bg_ref)
        val_n = conv_next(v_ref, vn_ref, wv_ref, bv_ref)
        dgate_n, dval_n = grads(gate_n, val_n, dan_ref[...])
        inside = i < SEQ // ROWS - 1
        dgp[ROWS:, :] = jnp.where(inside, dgate_n, 0.0)
        dvp[ROWS:, :] = jnp.where(inside, dval_n, 0.0)

        for half, (dp, s1, s2, w_ref) in enumerate(((dgp, g1, g2, wg_ref), (dvp, v1, v2, wv_ref))):
            s1[...] = dp[pl.ds(1, ROWS), :]
            s2[...] = dp[pl.ds(2, ROWS), :]

            def back(rs, dp=dp, s1=s1, s2=s2, w_ref=w_ref, half=half):
                out_ref[half, rs, :] = (w_ref[2:3, :] * dp[rs, :] + w_ref[1:2, :] * s1[rs, :]
                                        + w_ref[0:1, :] * s2[rs, :]).astype(BF16)

            _for_chunks(back)

    tile = pltpu.VMEM((ROWS, FFN_TN), F32)
    ext = pltpu.VMEM((ROWS + FFN_HALO, FFN_TN), F32)
    vec = jax.ShapeDtypeStruct((1, D_FF), F32)
    taps = jax.ShapeDtypeStruct((FFN_K, D_FF), F32)
    cur = pl.BlockSpec((ROWS, FFN_TN), lambda j, i: (i, j))
    nxt = lambda off: pl.BlockSpec((FFN_HALO, FFN_TN), lambda j, i: (jnp.minimum((i + 1) * per, last), j + off))
    vs = pl.BlockSpec((1, FFN_TN), lambda j, i: (0, j))
    ts = pl.BlockSpec((FFN_K, FFN_TN), lambda j, i: (0, j))
    return pl.pallas_call(
        body, out_shape=(jax.ShapeDtypeStruct((2, SEQ, D_FF), BF16), vec, vec, taps, taps), grid=(N_FT, SEQ // ROWS),
        in_specs=_ffn_specs() + [cur, nxt(0), nxt(N_FT), nxt(0)],
        out_specs=(pl.BlockSpec((2, ROWS, FFN_TN), lambda j, i: (0, i, j)), vs, vs, ts, ts),
        scratch_shapes=[pltpu.VMEM((2 * FFN_HALO, FFN_TN), F32), tile, tile, tile, tile, ext, ext,
                        pltpu.VMEM((2 + 2 * FFN_K, SUB, FFN_TN), F32)],
        name=name, compiler_params=_cp("parallel", "arbitrary"))(up0, up0, up0, up0, wf, wf, bf, bf, dact, up0, up0, dact)


def ada_fwd(c_all, w_ada, b_cols, name):
    def body(c_ref, w_ref, b_ref, o_ref):
        cc = c_ref[...]
        sc = (cc * _sig(cc)).astype(BF16)
        o_ref[...] = jnp.dot(sc, w_ref[...].astype(BF16), preferred_element_type=F32) + b_ref[...]

    return pl.pallas_call(body, out_shape=jax.ShapeDtypeStruct((N_DEV, w_ada.shape[1]), F32), name=name,
                          compiler_params=_cp())(c_all, w_ada, b_cols)


def _adam(w, g, m, v):
    m = ADAM_B1 * m + (1.0 - ADAM_B1) * g
    v = ADAM_B2 * v + (1.0 - ADAM_B2) * (g * g)
    m_hat = m / (1.0 - ADAM_B1 ** ADAM_STEP)
    v_hat = v / (1.0 - ADAM_B2 ** ADAM_STEP)
    delta = -ADAM_LR * (m_hat / (jnp.sqrt(v_hat) + ADAM_EPS) + ADAM_WD * w)
    return delta, m, v


def ada_bwd_adamw(c_all_t, dmod_cols, w, m, v, name):
    rows, cols = w.shape
    tr = 256

    def body(ct_ref, dm_ref, w_ref, m_ref, v_ref, g_ref, d_ref, nm_ref, nv_ref):
        ct = ct_ref[...]
        sc = ct * _sig(ct)
        g = sc[:, 0:1] * dm_ref[0:1, :]
        for b in range(1, N_DEV):
            g = g + sc[:, b:b + 1] * dm_ref[b:b + 1, :]
        g_ref[...] = g
        d_ref[...], nm_ref[...], nv_ref[...] = _adam(w_ref[...], g, m_ref[...], v_ref[...])

    blk = pl.BlockSpec((tr, cols), lambda i: (i, 0))
    shp = jax.ShapeDtypeStruct((rows, cols), F32)
    return pl.pallas_call(
        body, out_shape=(shp, shp, shp, shp), grid=(rows // tr,),
        in_specs=[pl.BlockSpec((tr, N_DEV), lambda i: (i, 0)), pl.BlockSpec((N_DEV, cols), lambda i: (0, 0)), blk, blk, blk],
        out_specs=(blk, blk, blk, blk), name=name, compiler_params=_cp("parallel"))(c_all_t, dmod_cols, w, m, v)


def sum_adamw(parts, own, w, m, v, tr, name):
    n_parts, rows, cols = parts.shape

    def body(*refs):
        if own is None:
            p_ref, w_ref, m_ref, v_ref, g_ref, d_ref, nm_ref, nv_ref = refs
            g = p_ref[0].astype(F32)
        else:
            p_ref, own_ref, w_ref, m_ref, v_ref, g_ref, d_ref, nm_ref, nv_ref = refs
            g = own_ref[...].astype(F32)
        for k in range(1, n_parts):
            g = g + p_ref[k].astype(F32)
        g_ref[...] = g
        d_ref[...], nm_ref[...], nv_ref[...] = _adam(w_ref[...], g, m_ref[...], v_ref[...])

    blk = pl.BlockSpec((tr, cols), lambda i: (i, 0))
    shp = jax.ShapeDtypeStruct((rows, cols), F32)
    args = [parts] + ([] if own is None else [own]) + [w, m, v]
    return pl.pallas_call(
        body, out_shape=(shp, shp, shp, shp), grid=(rows // tr,),
        in_specs=[pl.BlockSpec((n_parts, tr, cols), lambda i: (0, i, 0))] + [blk] * (len(args) - 1),
        out_specs=(blk, blk, blk, blk), name=name, compiler_params=_cp("parallel"))(*args)


MESH = pl.DeviceIdType.MESH
ANY = pl.BlockSpec(memory_space=pl.ANY)


def all_gather(block, name, after=None):
    extra = () if after is None else (after,)

    def body(x_ref, *refs):
        out_ref, send_sems, recv_sems, local_sem = refs[len(extra):]
        x, y, c = lax.axis_index("x"), lax.axis_index("y"), lax.axis_index("c")
        me, sibling = (x, y, c), (x, y, 1 - c)
        chips = [(1 - x, y), (x, 1 - y), (1 - x, 1 - y)]

        def slot(px, py, pc):
            return out_ref.at[4 * px + 2 * py + pc]

        def copy(k, blk, to, src=None):
            return pltpu.make_async_remote_copy(
                src_ref=slot(*blk) if src is None else src, dst_ref=slot(*blk),
                send_sem=send_sems.at[k], recv_sem=recv_sems.at[k], device_id=to, device_id_type=MESH)

        mine = pltpu.make_async_copy(x_ref, slot(*me), local_sem)
        mine.start()
        first = [copy(0, me, sibling, src=x_ref)]
        first += [copy(1 + j, me, (*chip, c), src=x_ref) for j, chip in enumerate(chips)]
        for cp in first:
            cp.start()
        passed = [copy(4 + j, (*chip, c), sibling) for j, chip in enumerate(chips)]
        for j, chip in enumerate(chips):
            copy(1 + j, (*chip, c), me).wait_recv()
            passed[j].start()
        copy(0, sibling, me).wait_recv()
        for j, chip in enumerate(chips):
            copy(4 + j, (*chip, 1 - c), me).wait_recv()
        for cp in first + passed:
            cp.wait_send()
        mine.wait()

    return pl.pallas_call(
        body, out_shape=jax.ShapeDtypeStruct((N_DEV,) + block.shape, block.dtype), in_specs=[ANY] * (1 + len(extra)), out_specs=ANY,
        scratch_shapes=[pltpu.SemaphoreType.DMA((7,)), pltpu.SemaphoreType.DMA((7,)), pltpu.SemaphoreType.DMA],
        name=name)(block, *extra)


HBM = pl.BlockSpec(memory_space=pltpu.HBM)
SEM = pl.BlockSpec(memory_space=pltpu.SEMAPHORE)
EFFECT = pltpu.SideEffectType.DATAFLOW_SIDE_EFFECTING


def _peer_copies(src_ref, land_ref, send_sems, recv_sems, gather):
    x, y, c = lax.axis_index("x"), lax.axis_index("y"), lax.axis_index("c")
    me = 4 * x + 2 * y + c
    copies = []
    for k in range(1, N_DEV):
        px = 1 - x if k & 4 else x
        py = 1 - y if k & 2 else y
        pc = 1 - c if k & 1 else c
        copies.append(pltpu.make_async_remote_copy(
            src_ref=src_ref if gather else src_ref.at[4 * px + 2 * py + pc],
            dst_ref=land_ref.at[me] if gather else land_ref.at[k],
            send_sem=send_sems.at[k - 1], recv_sem=recv_sems.at[k - 1], device_id=(px, py, pc), device_id_type=MESH))
    return copies


def exchange_start(srcs, gather, name, after=None):
    n = len(srcs)
    land_shapes = [(N_DEV,) + src.shape if gather else src.shape for src in srcs]
    extra = () if after is None else (after,)

    def body(*refs):
        src_refs, land_refs = refs[0:n], refs[n:2 * n]
        outs = refs[2 * n + len(extra):]
        for k in range(n):
            for cp in _peer_copies(src_refs[k], land_refs[k], outs[4 * k], outs[4 * k + 1], gather):
                cp.start()
        token = outs[4 * n]
        token[...] = jnp.zeros_like(token)

    out_shape, out_specs, aliases = [], [], {}
    for k, src in enumerate(srcs):
        out_shape += [pltpu.SemaphoreType.DMA((N_DEV - 1,)), pltpu.SemaphoreType.DMA((N_DEV - 1,)),
                      pltpu.HBM(src.shape, src.dtype), pltpu.HBM(land_shapes[k], src.dtype)]
        out_specs += [SEM, SEM, HBM, HBM]
        aliases[k] = 4 * k + 2
        aliases[n + k] = 4 * k + 3
    out_shape.append(jax.ShapeDtypeStruct((8, 128), F32))
    out_specs.append(pl.BlockSpec(memory_space=pltpu.VMEM))
    res = pl.pallas_call(
        body, name=name, out_shape=tuple(out_shape), in_specs=(HBM,) * (2 * n) + (ANY,) * len(extra),
        out_specs=tuple(out_specs), input_output_aliases=aliases,
        compiler_params=pltpu.CompilerParams(has_side_effects=EFFECT),
    )(*[pltpu.with_memory_space_constraint(src, pltpu.HBM) for src in srcs],
      *[pltpu.with_memory_space_constraint(lax.empty(shp, src.dtype), pltpu.HBM) for shp, src in zip(land_shapes, srcs)],
      *extra)
    return [tuple(res[4 * k:4 * k + 4]) for k in range(n)], res[4 * n][0, 0]


def exchange_wait(handles, after, gather, name):
    send_sems, recv_sems, src_thru, land_thru = handles

    def body(src_ref, land_ref, send_sems, recv_sems, after_ref, src_dead, got_ref):
        for cp in _peer_copies(src_ref, land_ref, send_sems, recv_sems, gather):
            cp.wait_send()
            cp.wait_recv()

    return pl.pallas_call(
        body, name=name,
        out_shape=(pltpu.HBM(src_thru.shape, src_thru.dtype), pltpu.HBM(land_thru.shape, land_thru.dtype)),
        in_specs=(HBM, HBM, SEM, SEM, ANY), out_specs=(HBM, HBM), input_output_aliases={0: 0, 1: 1},
        compiler_params=pltpu.CompilerParams(has_side_effects=EFFECT),
    )(src_thru, land_thru, send_sems, recv_sems, after)[1]


def _to_pattern(a, r):
    return a.reshape(SEQ // r, r * a.shape[1])


def local_step(x, tgt, mod, get_w, put_grad, wc, wf, g_mix, bc, lg, lb, gco, gao, g_ffn, bf, g_fin):
    h1 = rms_mod_fwd(x, g_mix, mod, 0, 1, "h1_fwd")
    w_in = get_w("w_in", h1)
    proj = matmul(h1, w_in, "nt", F32, 512, D_IN, "proj_fwd")
    mix_a = conv_module_fwd(proj, wc, bc, lg, lb, gco, "conv_module_fwd")
    att, lse = attn_fwd_all(proj, "attn_fwd")
    mix_b = rms_gain_bf16(att, gao, "attn_out_norm")
    mixed = jnp.concatenate([mix_a, mix_b], axis=1)
    w_out = get_w("w_out", mixed)
    y1 = matmul(mixed, w_out, "nn", F32, 512, D_MODEL, "out_proj_fwd")
    x1, h2 = resid_rms_mod_fwd(x, y1, g_ffn, mod, 2, 3, 4, "x1_h2_fwd")
    w_up = get_w("w_up", h2)
    up0 = matmul(h2, w_up, "nt", F32, 512, D_FF, "up_fwd")
    act = ffn_act_fwd(up0, wf, bf, "ffn_act_fwd")
    w_down = get_w("w_down", act)
    y2 = matmul(act, w_down, "nn", F32, 512, D_MODEL, "down_fwd")
    loss_t, dx2, dy2, d_gfin, d_gaf = final_loss_bwd(x1, y2, tgt, g_fin, mod, 5, "loss_bwd")
    dact = matmul(dy2, w_down, "nt", F32, 512, FFN_TN, "down_bwd_x")
    dw_down = matmul(act, dy2, "tn", BF16, 256, D_MODEL, "down_bwd_w")
    dup0, dbf_g, dbf_v, dwf_g, dwf_v = ffn_bwd(up0, dact, wf + put_grad("w_down", dw_down), bf, "ffn_bwd")
    dh2 = matmul_halves(dup0, w_up, 512, 512, "up_bwd_x")
    dw_up = matmul_tn_halves(dup0, h2, 256, "up_bwd_w")
    dx1, d_shf, d_scf, d_gffn, dy1, d_gam = rms_mod_bwd(x1, dh2, dx2, g_ffn + put_grad("w_up", dw_up), mod, 4, y1, 2, "h2_bwd")
    dmixed = matmul(dy1, w_out, "nt", F32, 512, D_MODEL, "out_proj_bwd_x")
    dw_out = matmul(mixed, dy1, "tn", BF16, 256, D_MODEL, "out_proj_bwd_w")
    do, dd, d_gao = attn_combine_bwd(dmixed, att, gao + put_grad("w_out", dw_out), "attn_combine_bwd")
    dqkv = attn_bwd_all(proj, do, lse, dd, "attn_bwd")
    du1, d_gco, d_lg, d_lb, d_bc, d_wc = conv_module_bwd_a(proj, dmixed, wc, bc, lg, lb, gco, "conv_module_bwd_a")
    dproj_a = conv_module_bwd_b(proj, du1, wc, "conv_module_bwd_b")
    dproj = jnp.concatenate([dproj_a, dqkv[0], dqkv[1], dqkv[2]], axis=1)
    dw_in = matmul(dproj, h1, "tn", BF16, 512, D_MODEL, "proj_bwd_w")
    dh1 = matmul(dproj, w_in, "nn", F32, 512, D_MODEL, "proj_bwd_x")
    dx, d_shm, d_scm, d_gmix = rms_mod_bwd(x, dh1, dx1, g_mix + put_grad("w_in", dw_in), mod, 1, None, 0, "h1_bwd")
    dmod = jnp.concatenate([d_shm, d_scm, d_gam, d_shf, d_scf, d_gaf], axis=1)
    small = dict(g_norm_mix=d_gmix, b_conv_dw=d_bc, ln_conv_g=d_lg, ln_conv_b=d_lb, g_conv_out=d_gco, g_attn_out=d_gao,
                 g_norm_ffn=d_gffn, b_ffn_dw=jnp.concatenate([dbf_g, dbf_v], axis=1), g_final=d_gfin,
                 w_conv_dw=d_wc, w_ffn_dw=jnp.concatenate([dwf_g, dwf_v], axis=1), dmod=dmod, loss=loss_t[0:1, 0:1])
    return dx, small


def _padw(a, width):
    return jnp.pad(a, ((0, 0), (0, width - a.shape[1])))


def pack_small(t):
    wide = jnp.concatenate([_padw(t["dmod"], PACK_W), _padw(t["b_ffn_dw"], PACK_W), _padw(t["w_ffn_dw"], PACK_W),
                            _padw(t["loss"], PACK_W), jnp.zeros((2, PACK_W), F32)], axis=0)
    z512 = jnp.zeros((1, 512), F32)
    narrow = jnp.concatenate([
        t["g_norm_mix"], t["g_norm_ffn"], t["g_final"],
        jnp.concatenate([t["b_conv_dw"], t["ln_conv_g"]], axis=1),
        jnp.concatenate([t["ln_conv_b"], t["g_conv_out"]], axis=1),
        jnp.concatenate([t["g_attn_out"], z512], axis=1),
        jnp.zeros((2, 1024), F32),
        jnp.pad(t["w_conv_dw"], ((0, 1), (0, 0))).reshape(16, 1024)], axis=0)
    return jnp.concatenate([wide, narrow.reshape(4, PACK_W), jnp.zeros((4, PACK_W), F32)], axis=0)


def unpack_small(p):
    narrow = p[8:12].reshape(24, 1024)
    return dict(
        dmod=p[0:1], b_ffn_dw=p[1:2, :2 * D_FF], w_ffn_dw=p[2:5, :2 * D_FF], loss=p[5, 0],
        g_norm_mix=narrow[0:1], g_norm_ffn=narrow[1:2], g_final=narrow[2:3],
        b_conv_dw=narrow[3:4, :512], ln_conv_g=narrow[3:4, 512:], ln_conv_b=narrow[4:5, :512], g_conv_out=narrow[4:5, 512:],
        g_attn_out=narrow[5:6, :512], w_conv_dw=narrow[8:24].reshape(32, 512)[:CONV_K])


def _embed(local, width, me):
    return lax.dynamic_update_slice(jnp.zeros((local.shape[0], width), F32), local, (0, me * local.shape[1]))


def _shard(full, n_cols, me):
    return lax.dynamic_slice(full, (0, me * n_cols), (full.shape[0], n_cols))


WEIGHTS = ["w_ada", "b_ada", "g_norm_mix", "w_in", "w_conv_dw", "b_conv_dw", "ln_conv_g", "ln_conv_b", "g_conv_out",
           "g_attn_out", "w_out", "g_norm_ffn", "w_up", "w_ffn_dw", "b_ffn_dw", "w_down", "g_final"]
SMALL_REPLICATED = ["g_norm_mix", "b_conv_dw", "ln_conv_g", "ln_conv_b", "g_conv_out", "g_attn_out", "g_norm_ffn",
                    "b_ffn_dw", "g_final"]


def kernel(x, c, w_ada, b_ada, g_norm_mix, w_in, w_conv_dw, b_conv_dw, ln_conv_g, ln_conv_b, g_conv_out, g_attn_out, w_out, g_norm_ffn, w_up, w_ffn_dw, b_ffn_dw, w_down, g_final, loss_target, m_w_ada, m_b_ada, m_g_norm_mix, m_w_in, m_w_conv_dw, m_b_conv_dw, m_ln_conv_g, m_ln_conv_b, m_g_conv_out, m_g_attn_out, m_w_out, m_g_norm_ffn, m_w_up, m_w_ffn_dw, m_b_ffn_dw, m_w_down, m_g_final, v_w_ada, v_b_ada, v_g_norm_mix, v_w_in, v_w_conv_dw, v_b_conv_dw, v_ln_conv_g, v_ln_conv_b, v_g_conv_out, v_g_attn_out, v_w_out, v_g_norm_ffn, v_w_up, v_w_ffn_dw, v_b_ffn_dw, v_w_down, v_g_final):
    args = dict(locals())
    me = 4 * lax.axis_index("x") + 2 * lax.axis_index("y") + lax.axis_index("c")

    def flat(name, prefix=""):
        a = args[prefix + name]
        return a.reshape(a.shape[-2] if a.ndim > 1 else 1, a.shape[-1])

    n_in, n_up, r_out, r_down = w_in.shape[2], w_up.shape[2], w_out.shape[1], w_down.shape[1]
    n_ada, n_wc, n_wf = w_ada.shape[2], w_conv_dw.shape[2], w_ffn_dw.shape[2]
    taps_c = jnp.pad(flat("w_conv_dw").reshape(1, CONV_K * n_wc), ((0, 0), (0, 2 * D_MODEL - CONV_K * n_wc)))
    taps_f = jnp.pad(flat("w_ffn_dw").reshape(1, FFN_K * n_wf), ((0, 0), (0, 3 * D_MODEL - FFN_K * n_wf)))
    first = jnp.concatenate([c, taps_c.reshape(2, D_MODEL), taps_f.reshape(3, D_MODEL), jnp.zeros((2, D_MODEL), F32)], axis=0)
    first_all = all_gather(first, "gather_c_taps")
    c_all = first_all[:, 0, :]
    wc_full = first_all[:, 1:3, :].reshape(N_DEV, 2 * D_MODEL)[:, :CONV_K * n_wc].reshape(N_DEV, CONV_K, n_wc)
    wc_full = wc_full.transpose(1, 0, 2).reshape(CONV_K, D_CONV)
    wf_full = first_all[:, 3:6, :].reshape(N_DEV, 3 * D_MODEL)[:, :FFN_K * n_wf].reshape(N_DEV, FFN_K, n_wf)
    wf_full = wf_full.transpose(1, 0, 2).reshape(FFN_K, 2 * D_FF)
    mod_cols = ada_fwd(c_all, flat("w_ada"), _shard(flat("b_ada"), n_ada, me), "ada_fwd")
    mod_all = all_gather(mod_cols, "gather_mod")
    mod = lax.dynamic_index_in_dim(mod_all, me, axis=1, keepdims=False).reshape(N_MOD, D_MODEL)
    mod = jnp.pad(mod, ((0, 2), (0, 0)))

    def flat_t(name, prefix=""):
        return args[prefix + name][0].T

    order = ("w_in", "w_out", "w_up", "w_down")
    blocks = dict(w_in=flat_t("w_in").astype(BF16), w_up=flat_t("w_up").astype(BF16),
                  w_out=flat("w_out").astype(BF16), w_down=flat("w_down").astype(BF16))
    handles, tok = exchange_start([blocks[name] for name in order], True, "gather_weights_start", mod_all)
    gathers = dict(zip(order, handles))
    mod = mod + tok

    slot = lax.broadcasted_iota(jnp.int32, (N_DEV, 1, 1), 0)

    def gathered(name, after):
        land = exchange_wait(gathers[name], after, True, f"gather_{name}_wait")
        return jnp.where(slot == me, blocks[name][None], land)

    def get_w(name, after):
        return gathered(name, after).reshape(-1, D_MODEL)

    exchanges, own = {}, {}

    def put_grad(name, dw, after=None):
        dev_major = dw.reshape(N_DEV, -1, D_MODEL)
        own[name] = lax.dynamic_index_in_dim(dev_major, me, axis=0, keepdims=False)
        (exchanges[name],), token = exchange_start([dev_major], False, f"exchange_{name}_start", after)
        return token

    grad_x, small = local_step(
        x[0], loss_target[0], mod, get_w, put_grad, wc_full, wf_full,
        flat("g_norm_mix"), flat("b_conv_dw"), flat("ln_conv_g"), flat("ln_conv_b"), flat("g_conv_out"),
        flat("g_attn_out"), flat("g_norm_ffn"), flat("b_ffn_dw"), flat("g_final"))

    out = {}

    def finish(name, tr, after):
        parts = exchange_wait(exchanges[name], after, False, f"exchange_{name}_wait")
        if name in ("w_in", "w_up"):
            res = sum_adamw(parts, own[name], flat_t(name), flat_t(name, "m_"), flat_t(name, "v_"), tr, "adamw_" + name)
            out[name] = tuple(r.T for r in res)
        else:
            res = out[name] = sum_adamw(parts, own[name], flat(name), flat(name, "m_"), flat(name, "v_"), tr, "adamw_" + name)
        return res[0]

    after = finish("w_down", r_down, grad_x)
    after = finish("w_up", n_up // 2, after)
    after = finish("w_out", r_out, after)
    after = finish("w_in", n_in, after)

    small_all = all_gather(pack_small(small), "gather_small", after)

    def packed(prefix):
        t = {n: flat(n, prefix) for n in SMALL_REPLICATED}
        t["dmod"] = flat("b_ada", prefix)
        t["loss"] = jnp.zeros((1, 1), F32)
        t["w_conv_dw"] = _embed(flat("w_conv_dw", prefix), D_CONV, me)
        t["w_ffn_dw"] = _embed(flat("w_ffn_dw", prefix), 2 * D_FF, me)
        return pack_small(t)

    res = sum_adamw(small_all, None, packed(""), packed("m_"), packed("v_"), PACK_ROWS, "adamw_small")
    res = [unpack_small(r) for r in res]
    loss = res[0]["loss"]
    for n in SMALL_REPLICATED:
        out[n] = tuple(r[n] for r in res)
    out["b_ada"] = tuple(r["dmod"] for r in res)
    out["w_conv_dw"] = tuple(_shard(r["w_conv_dw"], n_wc, me) for r in res)
    out["w_ffn_dw"] = tuple(_shard(r["w_ffn_dw"], n_wf, me) for r in res)

    dmod_cols = _shard(small_all[:, 0, :], n_ada, me)
    out["w_ada"] = ada_bwd_adamw(c_all.T, dmod_cols, flat("w_ada"), flat("w_ada", "m_"), flat("w_ada", "v_"), "adamw_w_ada")

    result = [loss, grad_x[None]]
    for k in range(4):
        result += [out[n][k].reshape(args[n].shape) for n in WEIGHTS]
    return tuple(result)
```

```python
import functools

import jax
import jax.numpy as jnp
from jax import lax
from jax.experimental import pallas as pl
from jax.experimental.pallas import tpu as pltpu

F32 = jnp.float32
BF16 = jnp.bfloat16

N_DEV = 8
SEQ = 2048
D_MODEL = 1024
D_CONV = 512
D_ATTN = 512
HEAD_DIM = 64
CONV_K = 31
D_FF = 2816
FFN_K = 3
D_IN = 2 * D_CONV + 3 * D_ATTN
N_MOD = 6
EPS = 1e-6
ATTN_BLOCK = 128
PATTERNS = ((2048, 1), (512, 4), (128, 16))
NEG = -1e30

ADAM_LR, ADAM_B1, ADAM_B2, ADAM_EPS, ADAM_WD, ADAM_STEP = 0.001, 0.9, 0.999, 1e-08, 0.01, 10

ROWS = 256
CONV_HALO = 32
FFN_HALO = 8
FFN_TN = 1408
VMEM_LIMIT = 56 * 1024 * 1024
PACK_ROWS, PACK_W = 16, 6144


def _cp(*sem):
    return pltpu.CompilerParams(dimension_semantics=sem if sem else None, vmem_limit_bytes=VMEM_LIMIT)


def _sig(x):
    return 1.0 / (1.0 + jnp.exp(-x))


def _rsum(x):
    return jnp.sum(x, axis=0, keepdims=True)


def _mean(x):
    return jnp.mean(x, axis=-1, keepdims=True)


def _acc(ref, val, first):
    @pl.when(first)
    def _():
        ref[...] = val

    @pl.when(jnp.logical_not(first))
    def _():
        ref[...] += val


SUB = 16


def _for_chunks(fn, unroll=1, rows=ROWS):
    def step(i, carry):
        fn(pl.ds(pl.multiple_of(i * SUB, SUB), SUB))
        return carry

    lax.fori_loop(0, rows // SUB, step, 0, unroll=unroll)


def rms_mod_fwd(x, g, mod, sh_row, sc_row, name):
    def body(x_ref, g_ref, mod_ref, h_ref):
        xx = x_ref[...]
        r = lax.rsqrt(_mean(xx * xx) + EPS)
        h = xx * r * g_ref[...]
        h_ref[...] = (h * (1.0 + mod_ref[sc_row:sc_row + 1, :]) + mod_ref[sh_row:sh_row + 1, :]).astype(BF16)

    return pl.pallas_call(
        body, out_shape=jax.ShapeDtypeStruct((SEQ, D_MODEL), BF16), grid=(SEQ // ROWS,),
        in_specs=[_row_spec(D_MODEL), _vec_spec(D_MODEL), _vec_spec(D_MODEL, 8)],
        out_specs=_row_spec(D_MODEL), name=name, compiler_params=_cp("parallel"))(x, g, mod)


def resid_rms_mod_fwd(x, y, g, mod, ga_row, sh_row, sc_row, name):
    def body(x_ref, y_ref, g_ref, mod_ref, x1_ref, h_ref):
        x1 = x_ref[...] + mod_ref[ga_row:ga_row + 1, :] * y_ref[...]
        x1_ref[...] = x1
        r = lax.rsqrt(_mean(x1 * x1) + EPS)
        h = x1 * r * g_ref[...]
        h_ref[...] = (h * (1.0 + mod_ref[sc_row:sc_row + 1, :]) + mod_ref[sh_row:sh_row + 1, :]).astype(BF16)

    return pl.pallas_call(
        body, out_shape=(jax.ShapeDtypeStruct((SEQ, D_MODEL), F32), jax.ShapeDtypeStruct((SEQ, D_MODEL), BF16)),
        grid=(SEQ // ROWS,),
        in_specs=[_row_spec(D_MODEL), _row_spec(D_MODEL), _vec_spec(D_MODEL), _vec_spec(D_MODEL, 8)],
        out_specs=(_row_spec(D_MODEL), _row_spec(D_MODEL)), name=name, compiler_params=_cp("parallel"))(x, y, g, mod)


def final_loss_bwd(x1, y2, tgt, g, mod, ga_row, name):
    def body(x1_ref, y2_ref, t_ref, g_ref, mod_ref, loss_ref, dx2_ref, dy2_ref, dg_ref, dga_ref):
        first = pl.program_id(0) == 0
        ga = mod_ref[ga_row:ga_row + 1, :]
        y2 = y2_ref[...]
        x2 = x1_ref[...] + ga * y2
        r = lax.rsqrt(_mean(x2 * x2) + EPS)
        xn = x2 * r
        err = xn * g_ref[...] - t_ref[...]
        _acc(loss_ref, jnp.broadcast_to(0.5 * jnp.sum(_mean(err * err)), (8, 128)), first)
        dy = err * (1.0 / D_MODEL)
        _acc(dg_ref, _rsum(dy * xn), first)
        dxn = dy * g_ref[...]
        dx2 = r * (dxn - xn * _mean(dxn * xn))
        dx2_ref[...] = dx2
        dy2_ref[...] = (dx2 * ga).astype(BF16)
        _acc(dga_ref, _rsum(dx2 * y2), first)

    vec = jax.ShapeDtypeStruct((1, D_MODEL), F32)
    return pl.pallas_call(
        body,
        out_shape=(jax.ShapeDtypeStruct((8, 128), F32), jax.ShapeDtypeStruct((SEQ, D_MODEL), F32),
                   jax.ShapeDtypeStruct((SEQ, D_MODEL), BF16), vec, vec),
        grid=(SEQ // ROWS,),
        in_specs=[_row_spec(D_MODEL), _row_spec(D_MODEL), _row_spec(D_MODEL), _vec_spec(D_MODEL), _vec_spec(D_MODEL, 8)],
        out_specs=(pl.BlockSpec((8, 128), lambda i: (0, 0)), _row_spec(D_MODEL), _row_spec(D_MODEL),
                   _vec_spec(D_MODEL), _vec_spec(D_MODEL)),
        name=name, compiler_params=_cp("arbitrary"))(x1, y2, tgt, g, mod)


def rms_mod_bwd(x, dh, dres, g, mod, sc_row, y, ga_row, name):
    gated = y is not None

    def body(*refs):
        if gated:
            x_ref, dh_ref, dres_ref, g_ref, mod_ref, y_ref, dx_ref, dsh_ref, dsc_ref, dg_ref, dy_ref, dga_ref = refs
        else:
            x_ref, dh_ref, dres_ref, g_ref, mod_ref, dx_ref, dsh_ref, dsc_ref, dg_ref = refs
        first = pl.program_id(0) == 0
        xx = x_ref[...]
        dh = dh_ref[...]
        gg = g_ref[...]
        r = lax.rsqrt(_mean(xx * xx) + EPS)
        xn = xx * r
        _acc(dsh_ref, _rsum(dh), first)
        _acc(dsc_ref, _rsum(dh * (xn * gg)), first)
        dt = dh * (1.0 + mod_ref[sc_row:sc_row + 1, :])
        _acc(dg_ref, _rsum(dt * xn), first)
        dxn = dt * gg
        dx = dres_ref[...] + r * (dxn - xn * _mean(dxn * xn))
        dx_ref[...] = dx
        if gated:
            _acc(dga_ref, _rsum(dx * y_ref[...]), first)
            dy_ref[...] = (dx * mod_ref[ga_row:ga_row + 1, :]).astype(BF16)

    vec = jax.ShapeDtypeStruct((1, D_MODEL), F32)
    in_specs = [_row_spec(D_MODEL), _row_spec(D_MODEL), _row_spec(D_MODEL), _vec_spec(D_MODEL), _vec_spec(D_MODEL, 8)]
    out_shape = [jax.ShapeDtypeStruct((SEQ, D_MODEL), F32), vec, vec, vec]
    out_specs = [_row_spec(D_MODEL), _vec_spec(D_MODEL), _vec_spec(D_MODEL), _vec_spec(D_MODEL)]
    args = [x, dh, dres, g, mod]
    if gated:
        in_specs.append(_row_spec(D_MODEL))
        out_shape += [jax.ShapeDtypeStruct((SEQ, D_MODEL), BF16), vec]
        out_specs += [_row_spec(D_MODEL), _vec_spec(D_MODEL)]
        args.append(y)
    return pl.pallas_call(
        body, out_shape=tuple(out_shape), grid=(SEQ // ROWS,), in_specs=in_specs, out_specs=tuple(out_specs),
        name=name, compiler_params=_cp("arbitrary"))(*args)


def _prev_halo(halo, width, col):
    per = ROWS // halo
    return pl.BlockSpec((halo, width), lambda i: (jnp.maximum(i * per - 1, 0), col))


def _next_halo(halo, width, col):
    per = ROWS // halo
    last = SEQ // halo - 1
    return pl.BlockSpec((halo, width), lambda i: (jnp.minimum((i + 1) * per, last), col))


CONV_PAD = ROWS + CONV_HALO


def _shift_copies(sh):
    for b in range(1, 8):
        sh[b, 0:CONV_PAD - 8, :] = sh[0, pl.ds(b, CONV_PAD - 8), :]


def _tap(sh, rs_start, offset):
    return sh[offset % 8, pl.ds(pl.multiple_of(rs_start + (offset // 8) * 8, 8), SUB), :]


def _conv_module_forward(av_ref, ag_ref, avh_ref, agh_ref, wc_ref, bc_ref, lg_ref, lb_ref, sh, u1_s):
    i = pl.program_id(0)
    hv = avh_ref[...] * _sig(agh_ref[...])
    sh[0, 0:CONV_HALO, :] = jnp.where(i > 0, hv, 0.0)

    def glu(rs):
        sh[0, pl.ds(pl.multiple_of(rs.start + CONV_HALO, SUB), SUB), :] = av_ref[rs, :] * _sig(ag_ref[rs, :])

    _for_chunks(glu)
    _shift_copies(sh)

    def conv(rs):
        u1 = jnp.broadcast_to(bc_ref[...], (SUB, D_CONV))
        for j in range(CONV_K):
            u1 = u1 + wc_ref[j:j + 1, :] * _tap(sh, rs.start, CONV_HALO - (CONV_K - 1) + j)
        u1_s[rs, :] = u1

    _for_chunks(conv)
    u1 = u1_s[...]
    mu = _mean(u1)
    cen = u1 - mu
    rs = lax.rsqrt(_mean(cen * cen) + EPS)
    z = cen * rs
    ln = z * lg_ref[...] + lb_ref[...]
    s = _sig(ln)
    return z, rs, ln, s, ln * s


_CONV_SCRATCH = [pltpu.VMEM((8, CONV_PAD, D_CONV), F32), pltpu.VMEM((ROWS, D_CONV), F32)]


def conv_module_fwd(proj, wc, bc, lg, lb, gco, name):
    def body(av_ref, ag_ref, avh_ref, agh_ref, wc_ref, bc_ref, lg_ref, lb_ref, gco_ref, out_ref, sh, u1_s):
        _, _, _, _, u2 = _conv_module_forward(av_ref, ag_ref, avh_ref, agh_ref, wc_ref, bc_ref, lg_ref, lb_ref, sh, u1_s)
        rc = lax.rsqrt(_mean(u2 * u2) + EPS)
        out_ref[...] = (u2 * rc * gco_ref[...]).astype(BF16)

    v = _vec_spec(D_CONV)
    return pl.pallas_call(
        body, out_shape=jax.ShapeDtypeStruct((SEQ, D_CONV), BF16), grid=(SEQ // ROWS,),
        in_specs=[_row_spec(D_CONV, 0), _row_spec(D_CONV, 1), _prev_halo(CONV_HALO, D_CONV, 0),
                  _prev_halo(CONV_HALO, D_CONV, 1), _vec_spec(D_CONV, CONV_K), v, v, v, v],
        out_specs=_row_spec(D_CONV), scratch_shapes=list(_CONV_SCRATCH),
        name=name, compiler_params=_cp("parallel"))(proj, proj, proj, proj, wc, bc, lg, lb, gco)


def conv_module_bwd_a(proj, dmixed, wc, bc, lg, lb, gco, name):
    def body(av_ref, ag_ref, avh_ref, agh_ref, dm_ref, wc_ref, bc_ref, lg_ref, lb_ref, gco_ref,
             du1_ref, dgco_ref, dlg_ref, dlb_ref, dbc_ref, dwc_ref, sh, u1_s, acc):
        first = pl.program_id(0) == 0
        z, rs, ln, s, u2 = _conv_module_forward(av_ref, ag_ref, avh_ref, agh_ref, wc_ref, bc_ref, lg_ref, lb_ref, sh, u1_s)
        rc = lax.rsqrt(_mean(u2 * u2) + EPS)
        xn = u2 * rc
        dm = dm_ref[...]
        _acc(dgco_ref, _rsum(dm * xn), first)
        dyn = dm * gco_ref[...]
        du2 = rc * (dyn - xn * _mean(dyn * xn))
        dln = du2 * (s * (1.0 + ln * (1.0 - s)))
        _acc(dlg_ref, _rsum(dln * z), first)
        _acc(dlb_ref, _rsum(dln), first)
        dz = dln * lg_ref[...]
        du1 = rs * (dz - _mean(dz) - z * _mean(dz * z))
        du1_ref[...] = du1
        _acc(dbc_ref, _rsum(du1), first)
        acc[...] = jnp.zeros_like(acc)

        def taps(rs):
            d = du1_ref[rs, :]
            for j in range(CONV_K):
                acc[j] += d * _tap(sh, rs.start, CONV_HALO - (CONV_K - 1) + j)

        _for_chunks(taps)

        @pl.when(first)
        def _():
            dwc_ref[...] = jnp.zeros_like(dwc_ref)

        for j in range(CONV_K):
            dwc_ref[j:j + 1, :] += _rsum(acc[j])

    v = _vec_spec(D_CONV)
    vec = jax.ShapeDtypeStruct((1, D_CONV), F32)
    return pl.pallas_call(
        body,
        out_shape=(jax.ShapeDtypeStruct((SEQ, D_CONV), F32), vec, vec, vec, vec, jax.ShapeDtypeStruct((CONV_K, D_CONV), F32)),
        grid=(SEQ // ROWS,),
        in_specs=[_row_spec(D_CONV, 0), _row_spec(D_CONV, 1), _prev_halo(CONV_HALO, D_CONV, 0),
                  _prev_halo(CONV_HALO, D_CONV, 1), _row_spec(D_CONV, 0), _vec_spec(D_CONV, CONV_K), v, v, v, v],
        out_specs=(_row_spec(D_CONV), v, v, v, v, _vec_spec(D_CONV, CONV_K)),
        scratch_shapes=list(_CONV_SCRATCH) + [pltpu.VMEM((CONV_K, SUB, D_CONV), F32)],
        name=name, compiler_params=_cp("arbitrary"))(proj, proj, proj, proj, dmixed, wc, bc, lg, lb, gco)


def conv_module_bwd_b(proj, du1, wc, name):
    def body(av_ref, ag_ref, du1_ref, du1n_ref, wc_ref, out_ref, sh):
        i = pl.program_id(0)
        sh[0, 0:ROWS, :] = du1_ref[...]
        sh[0, ROWS:, :] = jnp.where(i < SEQ // ROWS - 1, du1n_ref[...], 0.0)
        _shift_copies(sh)

        def chunk(rs):
            du0 = jnp.zeros((SUB, D_CONV), F32)
            for j in range(CONV_K):
                du0 = du0 + wc_ref[j:j + 1, :] * _tap(sh, rs.start, CONV_K - 1 - j)
            sg = _sig(ag_ref[rs, :])
            out_ref[rs, 0:D_CONV] = (du0 * sg).astype(BF16)
            out_ref[rs, D_CONV:] = (du0 * av_ref[rs, :] * sg * (1.0 - sg)).astype(BF16)

        _for_chunks(chunk)

    return pl.pallas_call(
        body, out_shape=jax.ShapeDtypeStruct((SEQ, 2 * D_CONV), BF16), grid=(SEQ // ROWS,),
        in_specs=[_row_spec(D_CONV, 0), _row_spec(D_CONV, 1), _row_spec(D_CONV, 0), _next_halo(CONV_HALO, D_CONV, 0),
                  _vec_spec(D_CONV, CONV_K)],
        out_specs=_row_spec(2 * D_CONV), scratch_shapes=[pltpu.VMEM((8, CONV_PAD, D_CONV), F32)],
        name=name, compiler_params=_cp("parallel"))(proj, proj, du1, du1, wc)


def _attn_specs(sub_len, pairs):
    ng, width = 4 // pairs, 128 * pairs
    q = pl.BlockSpec((sub_len, width), lambda rho, g: (0, rho * 3 * ng + g))
    k = pl.BlockSpec((sub_len, width), lambda rho, g: (0, rho * 3 * ng + ng + g))
    v = pl.BlockSpec((sub_len, width), lambda rho, g: (0, rho * 3 * ng + 2 * ng + g))
    o = pl.BlockSpec((sub_len, width), lambda rho, g: (0, rho * ng + g))
    return q, k, v, o


ATTN_PAIRS = {1: 1, 4: 1, 16: 4}


def _attn_block(q_ref, k_ref, v_ref, n, hs, win):
    q0 = pl.multiple_of(n * ATTN_BLOCK, ATTN_BLOCK)
    k0 = pl.multiple_of(jnp.maximum(n - 1, 0) * ATTN_BLOCK, ATTN_BLOCK)
    qb = q_ref[pl.ds(q0, ATTN_BLOCK), hs]
    kw = k_ref[pl.ds(k0, win), hs]
    vw = v_ref[pl.ds(k0, win), hs]
    s = lax.dot_general(qb, kw, (((1,), (1,)), ((), ())), preferred_element_type=F32) * (HEAD_DIM ** -0.5)
    dist = (q0 - k0) + lax.broadcasted_iota(jnp.int32, (ATTN_BLOCK, win), 0) \
        - lax.broadcasted_iota(jnp.int32, (ATTN_BLOCK, win), 1)
    s = jnp.where((dist >= 0) & (dist <= ATTN_BLOCK), s, NEG)
    return q0, k0, qb, kw, vw, s


def attn_fwd(qkv_r, sub_len, r, name):
    nb = sub_len // ATTN_BLOCK
    win = 2 * ATTN_BLOCK if nb > 1 else ATTN_BLOCK
    pairs = ATTN_PAIRS[r]

    def body(q_ref, k_ref, v_ref, o_ref, l_ref):
        def block(n, carry):
            for h in range(2 * pairs):
                hs = slice(h * HEAD_DIM, (h + 1) * HEAD_DIM)
                q0, _, _, _, vw, s = _attn_block(q_ref, k_ref, v_ref, n, hs, win)
                m = jnp.max(s, axis=1, keepdims=True)
                p = jnp.exp(s - m)
                den = jnp.sum(p, axis=1, keepdims=True)
                o = jnp.dot(p.astype(BF16), vw, preferred_element_type=F32) / den
                o_ref[pl.ds(q0, ATTN_BLOCK), hs] = o
                l_ref[pl.ds(q0, ATTN_BLOCK), hs] = jnp.broadcast_to(m + jnp.log(den), (ATTN_BLOCK, HEAD_DIM))
            return carry

        lax.fori_loop(0, nb, block, 0, unroll=min(nb, 2))

    q, k, v, o = _attn_specs(sub_len, pairs)
    shp = jax.ShapeDtypeStruct((sub_len, r * D_ATTN), F32)
    return pl.pallas_call(
        body, out_shape=(shp, shp), grid=(r, 4 // pairs), in_specs=[q, k, v], out_specs=(o, o),
        name=name, compiler_params=_cp("parallel", "parallel"))(qkv_r, qkv_r, qkv_r)


def attn_bwd(qkv_r, do_r, lse_r, dd_r, sub_len, r, name):
    nb = sub_len // ATTN_BLOCK
    win = 2 * ATTN_BLOCK if nb > 1 else ATTN_BLOCK
    pairs = ATTN_PAIRS[r]

    def body(q_ref, k_ref, v_ref, do_ref, l_ref, dd_ref, dq_ref, dk_ref, dv_ref):
        dk_ref[...] = jnp.zeros_like(dk_ref)
        dv_ref[...] = jnp.zeros_like(dv_ref)

        def block(n, carry):
            for h in range(2 * pairs):
                hs = slice(h * HEAD_DIM, (h + 1) * HEAD_DIM)
                h1 = slice(h * HEAD_DIM, h * HEAD_DIM + 1)
                q0, k0, qb, kw, vw, s = _attn_block(q_ref, k_ref, v_ref, n, hs, win)
                dob = do_ref[pl.ds(q0, ATTN_BLOCK), hs]
                p = jnp.exp(s - l_ref[pl.ds(q0, ATTN_BLOCK), h1])
                dp = lax.dot_general(dob, vw, (((1,), (1,)), ((), ())), preferred_element_type=F32)
                ds = (p * (dp - dd_ref[pl.ds(q0, ATTN_BLOCK), h1]) * (HEAD_DIM ** -0.5)).astype(BF16)
                dq_ref[pl.ds(q0, ATTN_BLOCK), hs] = jnp.dot(ds, kw, preferred_element_type=F32)
                dk_ref[pl.ds(k0, win), hs] += lax.dot_general(ds, qb, (((0,), (0,)), ((), ())), preferred_element_type=F32)
                dv_ref[pl.ds(k0, win), hs] += lax.dot_general(p.astype(BF16), dob, (((0,), (0,)), ((), ())),
                                                              preferred_element_type=F32)
            return carry

        lax.fori_loop(0, nb, block, 0)

    q, k, v, o = _attn_specs(sub_len, pairs)
    shp = jax.ShapeDtypeStruct((sub_len, r * D_ATTN), F32)
    return pl.pallas_call(
        body, out_shape=(shp, shp, shp), grid=(r, 4 // pairs), in_specs=[q, k, v, o, o, o], out_specs=(o, o, o),
        name=name, compiler_params=_cp("parallel", "parallel"))(qkv_r, qkv_r, qkv_r, do_r, lse_r, dd_r)


def _rows(start, size, r):
    return pl.ds(start, size) if r == 1 else pl.ds(start, size, stride=r)


def _unit_rows(r, rho, n, nb):
    win = 2 * ATTN_BLOCK if nb > 1 else ATTN_BLOCK
    if isinstance(n, int):
        kb = max(n - 1, 0)
        q_rows = _rows(rho + r * ATTN_BLOCK * n, ATTN_BLOCK, r)
        k_rows = _rows(rho + r * ATTN_BLOCK * kb, win, r)
    else:
        kb = jnp.maximum(n - 1, 0)
        q_rows = pl.ds(pl.multiple_of(n * ATTN_BLOCK, ATTN_BLOCK), ATTN_BLOCK)
        k_rows = pl.ds(pl.multiple_of(kb * ATTN_BLOCK, ATTN_BLOCK), win)
    dist = (n - kb) * ATTN_BLOCK + lax.broadcasted_iota(jnp.int32, (ATTN_BLOCK, win), 0) \
        - lax.broadcasted_iota(jnp.int32, (ATTN_BLOCK, win), 1)
    return q_rows, k_rows, (dist >= 0) & (dist <= ATTN_BLOCK)


def _per_head(x):
    lane = lax.broadcasted_iota(jnp.int32, x.shape, 1)
    zero = jnp.zeros_like(x)
    return [jnp.where(lane < HEAD_DIM, x, zero), jnp.where(lane >= HEAD_DIM, x, zero)]


NT = (((1,), (1,)), ((), ()))


def _masked_scores(q2, k2, valid):
    return [jnp.where(valid, lax.dot_general(qh, k2, NT, preferred_element_type=F32) * (HEAD_DIM ** -0.5), NEG)
            for qh in _per_head(q2)]


def _attn_units(r, nb, unit):
    if r == 1:
        def four(i, carry):
            for k in range(4):
                unit(0, 4 * i + k)
            return carry
        lax.fori_loop(0, nb // 4, four, 0)
    else:
        for rho in range(r):
            for n in range(nb):
                unit(rho, n)


N_UNITS = 16


def attn_fwd_all(proj, name):
    def body(q_ref, k_ref, v_ref, att_ref, lse_ref, s_scr, p_scr, lse_scr, den_scr):
        for idx, (sub_len, r) in enumerate(PATTERNS):
            nb = sub_len // ATTN_BLOCK
            win = 2 * ATTN_BLOCK if nb > 1 else ATTN_BLOCK

            def scores(rho, n, r=r, nb=nb, win=win):
                u = rho * nb + n
                q_rows, k_rows, valid = _unit_rows(r, rho, n, nb)
                ss = _masked_scores(q_ref[q_rows, :].astype(BF16), k_ref[k_rows, :].astype(BF16), valid)
                for h in range(2):
                    s_scr[2 * u + h, :, 0:win] = ss[h]

            _attn_units(r, nb, scores)

            def softmax(u, carry, win=win):
                lses, dens = [], []
                for h in range(2):
                    sc = s_scr[2 * u + h, :, 0:win]
                    m = jnp.max(sc, axis=1, keepdims=True)
                    p = jnp.exp(sc - m)
                    den = jnp.sum(p, axis=1, keepdims=True)
                    p_scr[2 * u + h, :, 0:win] = p.astype(BF16)
                    lses.append(jnp.broadcast_to(m + jnp.log(den), (ATTN_BLOCK, HEAD_DIM)))
                    dens.append(jnp.broadcast_to(den, (ATTN_BLOCK, HEAD_DIM)))
                lse_scr[u] = jnp.concatenate(lses, axis=1)
                den_scr[u] = jnp.concatenate(dens, axis=1)
                return carry

            lax.fori_loop(0, N_UNITS, softmax, 0, unroll=2)

            def outputs(rho, n, r=r, nb=nb, win=win, idx=idx):
                u = rho * nb + n
                q_rows, k_rows, _ = _unit_rows(r, rho, n, nb)
                vs = _per_head(v_ref[k_rows, :].astype(BF16))
                o = (jnp.dot(p_scr[2 * u, :, 0:win], vs[0], preferred_element_type=F32)
                     + jnp.dot(p_scr[2 * u + 1, :, 0:win], vs[1], preferred_element_type=F32)) / den_scr[u]
                lse = lse_scr[u]
                if idx > 0:
                    old = lse_ref[q_rows, :]
                    top = jnp.maximum(old, lse)
                    new = top + jnp.log(jnp.exp(old - top) + jnp.exp(lse - top))
                    o = att_ref[q_rows, :] * jnp.exp(old - new) + o * jnp.exp(lse - new)
                    lse = new
                att_ref[q_rows, :] = o
                lse_ref[q_rows, :] = lse

            _attn_units(r, nb, outputs)

    blk = lambda first: pl.BlockSpec((SEQ, 128), lambda g: (0, first + g))
    shp = jax.ShapeDtypeStruct((SEQ, D_ATTN), F32)
    big = (2 * N_UNITS, ATTN_BLOCK, 2 * ATTN_BLOCK)
    small = pltpu.VMEM((N_UNITS, ATTN_BLOCK, 128), F32)
    return pl.pallas_call(
        body, out_shape=(shp, shp), grid=(4,), in_specs=[blk(8), blk(12), blk(16)], out_specs=(blk(0), blk(0)),
        scratch_shapes=[pltpu.VMEM(big, F32), pltpu.VMEM(big, BF16), small, small],
        name=name, compiler_params=_cp("parallel"))(proj, proj, proj)


def attn_bwd_all(proj, do, lse, dd, name):
    scale = HEAD_DIM ** -0.5

    def body(q_ref, k_ref, v_ref, do_ref, l_ref, dd_ref, out_ref, dq_s, dk_s, dv_s,
             s_scr, dp_scr, ds_scr, st_scr, dpt_scr, pt_scr, dst_scr, qb_scr, kb_scr, dob_scr):
        dq_s[...] = jnp.zeros_like(dq_s)
        dk_s[...] = jnp.zeros_like(dk_s)
        dv_s[...] = jnp.zeros_like(dv_s)
        for sub_len, r in PATTERNS:
            nb = sub_len // ATTN_BLOCK
            win = 2 * ATTN_BLOCK if nb > 1 else ATTN_BLOCK

            def scores(rho, n, r=r, nb=nb, win=win):
                u = rho * nb + n
                q_rows, k_rows, valid = _unit_rows(r, rho, n, nb)
                kb = max(n - 1, 0) if isinstance(n, int) else jnp.maximum(n - 1, 0)
                dist_t = (n - kb) * ATTN_BLOCK + lax.broadcasted_iota(jnp.int32, (win, ATTN_BLOCK), 1) \
                    - lax.broadcasted_iota(jnp.int32, (win, ATTN_BLOCK), 0)
                valid_t = (dist_t >= 0) & (dist_t <= ATTN_BLOCK)
                q2 = q_ref[q_rows, :].astype(BF16)
                k2 = k_ref[k_rows, :].astype(BF16)
                do2 = do_ref[q_rows, :].astype(BF16)
                qb_scr[u] = q2
                kb_scr[u, 0:win, :] = k2
                dob_scr[u] = do2
                l2 = l_ref[q_rows, :]
                d2 = dd_ref[q_rows, :]
                l2t = l2.T
                d2t = d2.T
                v2 = v_ref[k_rows, :].astype(BF16)
                qs, dos = _per_head(q2), _per_head(do2)
                for h in range(2):
                    c0 = h * HEAD_DIM
                    sc = lax.dot_general(qs[h], k2, NT, preferred_element_type=F32) * scale
                    s_scr[2 * u + h, :, 0:win] = jnp.where(valid, sc, NEG) - l2[:, c0:c0 + 1]
                    dp_scr[2 * u + h, :, 0:win] = lax.dot_general(dos[h], v2, NT, preferred_element_type=F32) \
                        - d2[:, c0:c0 + 1]
                    sct = lax.dot_general(k2, qs[h], NT, preferred_element_type=F32) * scale
                    st_scr[2 * u + h, 0:win, :] = jnp.where(valid_t, sct, NEG) - l2t[c0:c0 + 1, :]
                    dpt_scr[2 * u + h, 0:win, :] = lax.dot_general(v2, dos[h], NT, preferred_element_type=F32) \
                        - d2t[c0:c0 + 1, :]

            _attn_units(r, nb, scores)

            def pointwise(hu, carry, win=win):
                ds_scr[hu, :, 0:win] = (jnp.exp(s_scr[hu, :, 0:win]) * dp_scr[hu, :, 0:win] * scale).astype(BF16)
                pt = jnp.exp(st_scr[hu, 0:win, :])
                pt_scr[hu, 0:win, :] = pt.astype(BF16)
                dst_scr[hu, 0:win, :] = (pt * dpt_scr[hu, 0:win, :] * scale).astype(BF16)
                return carry

            lax.fori_loop(0, 2 * N_UNITS, pointwise, 0, unroll=4)

            def grads(rho, n, r=r, nb=nb, win=win):
                u = rho * nb + n
                q_rows, k_rows, _ = _unit_rows(r, rho, n, nb)
                qs, ks, dos = _per_head(qb_scr[u]), _per_head(kb_scr[u, 0:win, :]), _per_head(dob_scr[u])

                def both(scr, rows, rhs):
                    return (jnp.dot(scr[(2 * u,) + rows], rhs[0], preferred_element_type=F32)
                            + jnp.dot(scr[(2 * u + 1,) + rows], rhs[1], preferred_element_type=F32))

                dq_s[q_rows, :] += both(ds_scr, (slice(None), slice(0, win)), ks)
                dk_s[k_rows, :] += both(dst_scr, (slice(0, win), slice(None)), qs)
                dv_s[k_rows, :] += both(pt_scr, (slice(0, win), slice(None)), dos)

            _attn_units(r, nb, grads)
        out_ref[0] = dq_s[...].astype(BF16)
        out_ref[1] = dk_s[...].astype(BF16)
        out_ref[2] = dv_s[...].astype(BF16)

    blk = lambda first: pl.BlockSpec((SEQ, 128), lambda g: (0, first + g))
    acc = pltpu.VMEM((SEQ, 128), F32)
    big = (2 * N_UNITS, ATTN_BLOCK, 2 * ATTN_BLOCK)
    big_t = (2 * N_UNITS, 2 * ATTN_BLOCK, ATTN_BLOCK)
    return pl.pallas_call(
        body, out_shape=jax.ShapeDtypeStruct((3, SEQ, D_ATTN), BF16), grid=(4,),
        in_specs=[blk(8), blk(12), blk(16), blk(0), blk(0), blk(0)],
        out_specs=pl.BlockSpec((3, SEQ, 128), lambda g: (0, 0, g)),
        scratch_shapes=[acc, acc, acc, pltpu.VMEM(big, F32), pltpu.VMEM(big, F32), pltpu.VMEM(big, BF16),
                        pltpu.VMEM(big_t, F32), pltpu.VMEM(big_t, F32), pltpu.VMEM(big_t, BF16), pltpu.VMEM(big_t, BF16),
                        pltpu.VMEM((N_UNITS, ATTN_BLOCK, 128), BF16),
                        pltpu.VMEM((N_UNITS, 2 * ATTN_BLOCK, 128), BF16), pltpu.VMEM((N_UNITS, ATTN_BLOCK, 128), BF16)],
        name=name, compiler_params=_cp("parallel"))(proj, proj, proj, do, lse, dd)


def rms_gain_bf16(a, g, name):
    def body(a_ref, g_ref, o_ref):
        aa = a_ref[...]
        o_ref[...] = (aa * lax.rsqrt(_mean(aa * aa) + EPS) * g_ref[...]).astype(BF16)

    w = a.shape[1]
    return pl.pallas_call(
        body, out_shape=jax.ShapeDtypeStruct(a.shape, BF16), grid=(SEQ // ROWS,), in_specs=[_row_spec(w), _vec_spec(w)],
        out_specs=_row_spec(w), name=name, compiler_params=_cp("parallel"))(a, g)


def attn_combine_fwd(outs, lses, gao, name):
    def body(o1, o2, o3, l1, l2, l3, g_ref, att_ref, lse_ref, mix_ref):
        a1, a2, a3 = l1[...], l2[...], l3[...]
        m = jnp.maximum(jnp.maximum(a1, a2), a3)
        w1, w2, w3 = jnp.exp(a1 - m), jnp.exp(a2 - m), jnp.exp(a3 - m)
        den = w1 + w2 + w3
        att = (w1 * o1[...] + w2 * o2[...] + w3 * o3[...]) / den
        att_ref[...] = att
        lse_ref[...] = m + jnp.log(den)
        mix_ref[...] = (att * lax.rsqrt(_mean(att * att) + EPS) * g_ref[...]).astype(BF16)

    rs = _row_spec(D_ATTN)
    f = jax.ShapeDtypeStruct((SEQ, D_ATTN), F32)
    return pl.pallas_call(
        body, out_shape=(f, f, jax.ShapeDtypeStruct((SEQ, D_ATTN), BF16)), grid=(SEQ // ROWS,),
        in_specs=[rs] * 6 + [_vec_spec(D_ATTN)], out_specs=(rs, rs, rs),
        name=name, compiler_params=_cp("parallel"))(*outs, *lses, gao)


def attn_combine_bwd(dmixed, att, gao, name):
    def body(dm_ref, att_ref, g_ref, do_ref, dd_ref, dg_ref):
        first = pl.program_id(0) == 0
        att = att_ref[...]
        r = lax.rsqrt(_mean(att * att) + EPS)
        xn = att * r
        dm = dm_ref[...]
        _acc(dg_ref, _rsum(dm * xn), first)
        dyn = dm * g_ref[...]
        do = r * (dyn - xn * _mean(dyn * xn))
        do_ref[...] = do
        same_head = (jnp.right_shift(lax.broadcasted_iota(jnp.int32, (D_ATTN, D_ATTN), 0), 6)
                     == jnp.right_shift(lax.broadcasted_iota(jnp.int32, (D_ATTN, D_ATTN), 1), 6)).astype(F32)
        dd_ref[...] = jnp.dot(do * att, same_head, preferred_element_type=F32, precision=lax.Precision.HIGHEST)

    rs = _row_spec(D_ATTN)
    return pl.pallas_call(
        body,
        out_shape=(jax.ShapeDtypeStruct((SEQ, D_ATTN), F32), jax.ShapeDtypeStruct((SEQ, D_ATTN), F32),
                   jax.ShapeDtypeStruct((1, D_ATTN), F32)),
        grid=(SEQ // ROWS,), in_specs=[_row_spec(D_ATTN, 1), rs, _vec_spec(D_ATTN)],
        out_specs=(rs, rs, _vec_spec(D_ATTN)), name=name, compiler_params=_cp("arbitrary"))(dmixed, att, gao)


def sum3_bf16(a, b, c, name):
    def body(a_ref, b_ref, c_ref, o_ref):
        o_ref[...] = (a_ref[...] + b_ref[...] + c_ref[...]).astype(BF16)

    w = a.shape[1]
    rs = _row_spec(w)
    return pl.pallas_call(
        body, out_shape=jax.ShapeDtypeStruct(a.shape, BF16), grid=(SEQ // ROWS,), in_specs=[rs, rs, rs], out_specs=rs,
        name=name, compiler_params=_cp("parallel"))(a, b, c)


N_FT = D_FF // FFN_TN


def _ffn_specs():
    per = ROWS // FFN_HALO
    cur_g = pl.BlockSpec((ROWS, FFN_TN), lambda j, i: (i, j))
    cur_v = pl.BlockSpec((ROWS, FFN_TN), lambda j, i: (i, j + N_FT))
    halo_g = pl.BlockSpec((FFN_HALO, FFN_TN), lambda j, i: (jnp.maximum(i * per - 1, 0), j))
    halo_v = pl.BlockSpec((FFN_HALO, FFN_TN), lambda j, i: (jnp.maximum(i * per - 1, 0), j + N_FT))
    w_g = pl.BlockSpec((FFN_K, FFN_TN), lambda j, i: (0, j))
    w_v = pl.BlockSpec((FFN_K, FFN_TN), lambda j, i: (0, j + N_FT))
    b_g = pl.BlockSpec((1, FFN_TN), lambda j, i: (0, j))
    b_v = pl.BlockSpec((1, FFN_TN), lambda j, i: (0, j + N_FT))
    return [cur_g, cur_v, halo_g, halo_v, w_g, w_v, b_g, b_v]


def matmul(a, b, kind, out_dtype, tm, tn, name):
    if kind == "nn":
        (m, k), n = a.shape, b.shape[1]
        a_spec = pl.BlockSpec((tm, k), lambda j, i: (i, 0))
        b_spec = pl.BlockSpec((k, tn), lambda j, i: (0, j))
        dims = (((1,), (0,)), ((), ()))
    elif kind == "nt":
        (m, k), n = a.shape, b.shape[0]
        a_spec = pl.BlockSpec((tm, k), lambda j, i: (i, 0))
        b_spec = pl.BlockSpec((tn, k), lambda j, i: (j, 0))
        dims = (((1,), (1,)), ((), ()))
    else:
        (k, m), n = a.shape, b.shape[1]
        a_spec = pl.BlockSpec((k, tm), lambda j, i: (0, i))
        b_spec = pl.BlockSpec((k, tn), lambda j, i: (0, j))
        dims = (((0,), (0,)), ((), ()))
    assert m % tm == 0 and n % tn == 0, (name, m, n, tm, tn)

    def body(a_ref, b_ref, o_ref):
        o_ref[...] = lax.dot_general(a_ref[...], b_ref[...], dims, preferred_element_type=F32).astype(o_ref.dtype)

    return pl.pallas_call(
        body, out_shape=jax.ShapeDtypeStruct((m, n), out_dtype), grid=(n // tn, m // tm),
        in_specs=[a_spec, b_spec], out_specs=pl.BlockSpec((tm, tn), lambda j, i: (i, j)),
        name=name, compiler_params=_cp("parallel", "parallel"))(a, b)


def matmul_halves(a, b, tm, tn, name):
    _, m, k = a.shape
    n = b.shape[1]
    b3 = b.reshape(2, k, n)

    def body(a_ref, b_ref, o_ref):
        o_ref[...] = (jnp.dot(a_ref[0], b_ref[0], preferred_element_type=F32)
                      + jnp.dot(a_ref[1], b_ref[1], preferred_element_type=F32))

    return pl.pallas_call(
        body, out_shape=jax.ShapeDtypeStruct((m, n), F32), grid=(n // tn, m // tm),
        in_specs=[pl.BlockSpec((2, tm, k), lambda j, i: (0, i, 0)), pl.BlockSpec((2, k, tn), lambda j, i: (0, 0, j))],
        out_specs=pl.BlockSpec((tm, tn), lambda j, i: (i, j)), name=name, compiler_params=_cp("parallel", "parallel"))(a, b3)


def matmul_tn_halves(a, b, tm, name):
    _, k, m = a.shape
    n = b.shape[1]

    def body(a_ref, b_ref, o_ref):
        o_ref[0] = lax.dot_general(a_ref[0], b_ref[...], (((0,), (0,)), ((), ())), preferred_element_type=F32).astype(BF16)

    return pl.pallas_call(
        body, out_shape=jax.ShapeDtypeStruct((2, m, n), BF16), grid=(2, m // tm),
        in_specs=[pl.BlockSpec((1, k, tm), lambda h, i: (h, 0, i)), pl.BlockSpec((k, n), lambda h, i: (0, 0))],
        out_specs=pl.BlockSpec((1, tm, n), lambda h, i: (h, i, 0)), name=name,
        compiler_params=_cp("parallel", "parallel"))(a, b).reshape(2 * m, n)


def _row_spec(width, col=0):
    return pl.BlockSpec((ROWS, width), lambda i: (i, col))


def _vec_spec(width, rows=1):
    return pl.BlockSpec((rows, width), lambda i: (0, 0))


def _ffn_shifted(cur_ref, halo_ref, pad, s1, s2):
    i = pl.program_id(1)
    pad[0:FFN_HALO, :] = jnp.where(i > 0, halo_ref[...], 0.0)
    pad[FFN_HALO:, :] = cur_ref[0:FFN_HALO, :]
    for k, dst in ((1, s1), (2, s2)):
        dst[0:FFN_HALO, :] = pad[pl.ds(FFN_HALO - k, FFN_HALO), :]
        dst[FFN_HALO:, :] = cur_ref[pl.ds(FFN_HALO - k, ROWS - FFN_HALO), :]


def _ffn_conv(rs, cur_ref, s1, s2, w_ref, b_ref):
    return b_ref[...] + w_ref[0:1, :] * s2[rs, :] + w_ref[1:2, :] * s1[rs, :] + w_ref[2:3, :] * cur_ref[rs, :]


def ffn_act_fwd(up0, wf, bf, name):
    def body(g_ref, v_ref, gh_ref, vh_ref, wg_ref, wv_ref, bg_ref, bv_ref, act_ref, pad, g1, g2, v1, v2):
        _ffn_shifted(g_ref, gh_ref, pad, g1, g2)
        _ffn_shifted(v_ref, vh_ref, pad, v1, v2)

        def chunk(rs):
            gate = _ffn_conv(rs, g_ref, g1, g2, wg_ref, bg_ref)
            val = _ffn_conv(rs, v_ref, v1, v2, wv_ref, bv_ref)
            act_ref[rs, :] = (gate * _sig(gate) * val).astype(BF16)

        _for_chunks(chunk)

    tile = pltpu.VMEM((ROWS, FFN_TN), F32)
    return pl.pallas_call(
        body, out_shape=jax.ShapeDtypeStruct((SEQ, D_FF), BF16), grid=(N_FT, SEQ // ROWS),
        in_specs=_ffn_specs(), out_specs=pl.BlockSpec((ROWS, FFN_TN), lambda j, i: (i, j)),
        scratch_shapes=[pltpu.VMEM((2 * FFN_HALO, FFN_TN), F32), tile, tile, tile, tile],
        name=name, compiler_params=_cp("parallel", "parallel"))(up0, up0, up0, up0, wf, wf, bf, bf)


def ffn_bwd(up0, dact, wf, bf, name):
    per = ROWS // FFN_HALO
    last = SEQ // FFN_HALO - 1

    def body(g_ref, v_ref, gh_ref, vh_ref, wg_ref, wv_ref, bg_ref, bv_ref, da_ref, gn_ref, vn_ref, dan_ref,
             out_ref, dbg_ref, dbv_ref, dwg_ref, dwv_ref, pad, g1, g2, v1, v2, dgp, dvp, acc):
        i = pl.program_id(1)
        first = i == 0
        _ffn_shifted(g_ref, gh_ref, pad, g1, g2)
        _ffn_shifted(v_ref, vh_ref, pad, v1, v2)
        acc[...] = jnp.zeros_like(acc)

        def grads(gate, val, da):
            s = _sig(gate)
            return da * val * (s * (1.0 + gate * (1.0 - s))), da * (gate * s)

        def chunk(rs):
            gate = _ffn_conv(rs, g_ref, g1, g2, wg_ref, bg_ref)
            val = _ffn_conv(rs, v_ref, v1, v2, wv_ref, bv_ref)
            dgate, dval = grads(gate, val, da_ref[rs, :])
            dgp[rs, :] = dgate
            dvp[rs, :] = dval
            acc[0] += dgate
            acc[1] += dval
            for t, (sg, sv) in enumerate(((g2, v2), (g1, v1), (g_ref, v_ref))):
                acc[2 + t] += dgate * sg[rs, :]
                acc[5 + t] += dval * sv[rs, :]

        _for_chunks(chunk)
        _acc(dbg_ref, _rsum(acc[0]), first)
        _acc(dbv_ref, _rsum(acc[1]), first)
        _acc(dwg_ref, jnp.concatenate([_rsum(acc[2 + t]) for t in range(FFN_K)], axis=0), first)
        _acc(dwv_ref, jnp.concatenate([_rsum(acc[5 + t]) for t in range(FFN_K)], axis=0), first)

        def conv_next(cur_ref, nxt_ref, w_ref, b_ref):
            pad[0:FFN_HALO, :] = cur_ref[ROWS - FFN_HALO:, :]
            pad[FFN_HALO:, :] = nxt_ref[...]
            return (b_ref[...] + w_ref[0:1, :] * pad[pl.ds(FFN_HALO - 2, FFN_HALO), :]
                    + w_ref[1:2, :] * pad[pl.ds(FFN_HALO - 1, FFN_HALO), :] + w_ref[2:3, :] * nxt_ref[...])

        gate_n = conv_next(g_ref, gn_ref, wg_ref, bg_ref)
        val_n = conv_next(v_ref, vn_ref, wv_ref, bv_ref)
        dgate_n, dval_n = grads(gate_n, val_n, dan_ref[...])
        inside = i < SEQ // ROWS - 1
        dgp[ROWS:, :] = jnp.where(inside, dgate_n, 0.0)
        dvp[ROWS:, :] = jnp.where(inside, dval_n, 0.0)

        for half, (dp, s1, s2, w_ref) in enumerate(((dgp, g1, g2, wg_ref), (dvp, v1, v2, wv_ref))):
            s1[...] = dp[pl.ds(1, ROWS), :]
            s2[...] = dp[pl.ds(2, ROWS), :]

            def back(rs, dp=dp, s1=s1, s2=s2, w_ref=w_ref, half=half):
                out_ref[half, rs, :] = (w_ref[2:3, :] * dp[rs, :] + w_ref[1:2, :] * s1[rs, :]
                                        + w_ref[0:1, :] * s2[rs, :]).astype(BF16)

            _for_chunks(back)

    tile = pltpu.VMEM((ROWS, FFN_TN), F32)
    ext = pltpu.VMEM((ROWS + FFN_HALO, FFN_TN), F32)
    vec = jax.ShapeDtypeStruct((1, D_FF), F32)
    taps = jax.ShapeDtypeStruct((FFN_K, D_FF), F32)
    cur = pl.BlockSpec((ROWS, FFN_TN), lambda j, i: (i, j))
    nxt = lambda off: pl.BlockSpec((FFN_HALO, FFN_TN), lambda j, i: (jnp.minimum((i + 1) * per, last), j + off))
    vs = pl.BlockSpec((1, FFN_TN), lambda j, i: (0, j))
    ts = pl.BlockSpec((FFN_K, FFN_TN), lambda j, i: (0, j))
    return pl.pallas_call(
        body, out_shape=(jax.ShapeDtypeStruct((2, SEQ, D_FF), BF16), vec, vec, taps, taps), grid=(N_FT, SEQ // ROWS),
        in_specs=_ffn_specs() + [cur, nxt(0), nxt(N_FT), nxt(0)],
        out_specs=(pl.BlockSpec((2, ROWS, FFN_TN), lambda j, i: (0, i, j)), vs, vs, ts, ts),
        scratch_shapes=[pltpu.VMEM((2 * FFN_HALO, FFN_TN), F32), tile, tile, tile, tile, ext, ext,
                        pltpu.VMEM((2 + 2 * FFN_K, SUB, FFN_TN), F32)],
        name=name, compiler_params=_cp("parallel", "arbitrary"))(up0, up0, up0, up0, wf, wf, bf, bf, dact, up0, up0, dact)


def ada_fwd(c_all, w_ada, b_cols, name):
    def body(c_ref, w_ref, b_ref, o_ref):
        cc = c_ref[...]
        sc = (cc * _sig(cc)).astype(BF16)
        o_ref[...] = jnp.dot(sc, w_ref[...].astype(BF16), preferred_element_type=F32) + b_ref[...]

    return pl.pallas_call(body, out_shape=jax.ShapeDtypeStruct((N_DEV, w_ada.shape[1]), F32), name=name,
                          compiler_params=_cp())(c_all, w_ada, b_cols)


def _adam(w, g, m, v):
    m = ADAM_B1 * m + (1.0 - ADAM_B1) * g
    v = ADAM_B2 * v + (1.0 - ADAM_B2) * (g * g)
    m_hat = m / (1.0 - ADAM_B1 ** ADAM_STEP)
    v_hat = v / (1.0 - ADAM_B2 ** ADAM_STEP)
    delta = -ADAM_LR * (m_hat / (jnp.sqrt(v_hat) + ADAM_EPS) + ADAM_WD * w)
    return delta, m, v


def ada_bwd_adamw(c_all_t, dmod_cols, w, m, v, name):
    rows, cols = w.shape
    tr = 256

    def body(ct_ref, dm_ref, w_ref, m_ref, v_ref, g_ref, d_ref, nm_ref, nv_ref):
        def chunk(rs):
            ct = ct_ref[rs, :]
            sc = ct * _sig(ct)
            g = sc[:, 0:1] * dm_ref[0:1, :]
            for b in range(1, N_DEV):
                g = g + sc[:, b:b + 1] * dm_ref[b:b + 1, :]
            g_ref[rs, :] = g
            d_ref[rs, :], nm_ref[rs, :], nv_ref[rs, :] = _adam(w_ref[rs, :], g, m_ref[rs, :], v_ref[rs, :])

        _for_chunks(chunk, 2, tr)

    blk = pl.BlockSpec((tr, cols), lambda i: (i, 0))
    shp = jax.ShapeDtypeStruct((rows, cols), F32)
    return pl.pallas_call(
        body, out_shape=(shp, shp, shp, shp), grid=(rows // tr,),
        in_specs=[pl.BlockSpec((tr, N_DEV), lambda i: (i, 0)), pl.BlockSpec((N_DEV, cols), lambda i: (0, 0)), blk, blk, blk],
        out_specs=(blk, blk, blk, blk), name=name, compiler_params=_cp("parallel"))(c_all_t, dmod_cols, w, m, v)


def sum_adamw(parts, own, w, m, v, tr, name):
    n_parts, rows, cols = parts.shape

    def body(*refs):
        if own is None:
            p_ref, w_ref, m_ref, v_ref, g_ref, d_ref, nm_ref, nv_ref = refs
        else:
            p_ref, own_ref, w_ref, m_ref, v_ref, g_ref, d_ref, nm_ref, nv_ref = refs

        def chunk(rs):
            g = (p_ref[0, rs, :] if own is None else own_ref[rs, :]).astype(F32)
            for k in range(1, n_parts):
                g = g + p_ref[k, rs, :].astype(F32)
            g_ref[rs, :] = g
            d_ref[rs, :], nm_ref[rs, :], nv_ref[rs, :] = _adam(w_ref[rs, :], g, m_ref[rs, :], v_ref[rs, :])

        _for_chunks(chunk, 2 if tr > SUB else 1, tr)

    blk = pl.BlockSpec((tr, cols), lambda i: (i, 0))
    shp = jax.ShapeDtypeStruct((rows, cols), F32)
    args = [parts] + ([] if own is None else [own]) + [w, m, v]
    return pl.pallas_call(
        body, out_shape=(shp, shp, shp, shp), grid=(rows // tr,),
        in_specs=[pl.BlockSpec((n_parts, tr, cols), lambda i: (0, i, 0))] + [blk] * (len(args) - 1),
        out_specs=(blk, blk, blk, blk), name=name, compiler_params=_cp("parallel"))(*args)


MESH = pl.DeviceIdType.MESH
ANY = pl.BlockSpec(memory_space=pl.ANY)


def all_gather(block, name, after=None):
    extra = () if after is None else (after,)

    def body(x_ref, *refs):
        out_ref, send_sems, recv_sems, local_sem = refs[len(extra):]
        x, y, c = lax.axis_index("x"), lax.axis_index("y"), lax.axis_index("c")
        me, sibling = (x, y, c), (x, y, 1 - c)
        chips = [(1 - x, y), (x, 1 - y), (1 - x, 1 - y)]

        def slot(px, py, pc):
            return out_ref.at[4 * px + 2 * py + pc]

        def copy(k, blk, to, src=None):
            return pltpu.make_async_remote_copy(
                src_ref=slot(*blk) if src is None else src, dst_ref=slot(*blk),
                send_sem=send_sems.at[k], recv_sem=recv_sems.at[k], device_id=to, device_id_type=MESH)

        mine = pltpu.make_async_copy(x_ref, slot(*me), local_sem)
        mine.start()
        first = [copy(0, me, sibling, src=x_ref)]
        first += [copy(1 + j, me, (*chip, c), src=x_ref) for j, chip in enumerate(chips)]
        for cp in first:
            cp.start()
        passed = [copy(4 + j, (*chip, c), sibling) for j, chip in enumerate(chips)]
        for j, chip in enumerate(chips):
            copy(1 + j, (*chip, c), me).wait_recv()
            passed[j].start()
        copy(0, sibling, me).wait_recv()
        for j, chip in enumerate(chips):
            copy(4 + j, (*chip, 1 - c), me).wait_recv()
        for cp in first + passed:
            cp.wait_send()
        mine.wait()

    return pl.pallas_call(
        body, out_shape=jax.ShapeDtypeStruct((N_DEV,) + block.shape, block.dtype), in_specs=[ANY] * (1 + len(extra)), out_specs=ANY,
        scratch_shapes=[pltpu.SemaphoreType.DMA((7,)), pltpu.SemaphoreType.DMA((7,)), pltpu.SemaphoreType.DMA],
        name=name)(block, *extra)


HBM = pl.BlockSpec(memory_space=pltpu.HBM)
SEM = pl.BlockSpec(memory_space=pltpu.SEMAPHORE)
EFFECT = pltpu.SideEffectType.DATAFLOW_SIDE_EFFECTING


def _peer_copies(src_ref, land_ref, send_sems, recv_sems, gather):
    x, y, c = lax.axis_index("x"), lax.axis_index("y"), lax.axis_index("c")
    me = 4 * x + 2 * y + c
    copies = []
    for k in range(1, N_DEV):
        px = 1 - x if k & 4 else x
        py = 1 - y if k & 2 else y
        pc = 1 - c if k & 1 else c
        copies.append(pltpu.make_async_remote_copy(
            src_ref=src_ref if gather else src_ref.at[4 * px + 2 * py + pc],
            dst_ref=land_ref.at[me] if gather else land_ref.at[k],
            send_sem=send_sems.at[k - 1], recv_sem=recv_sems.at[k - 1], device_id=(px, py, pc), device_id_type=MESH))
    return copies


def exchange_start(srcs, gather, name, after=None):
    n = len(srcs)
    land_shapes = [(N_DEV,) + src.shape if gather else src.shape for src in srcs]
    extra = () if after is None else (after,)

    def body(*refs):
        src_refs, land_refs = refs[0:n], refs[n:2 * n]
        outs = refs[2 * n + len(extra):]
        local_sems = outs[4 * n + 1]
        mine = []
        if gather:
            me = 4 * lax.axis_index("x") + 2 * lax.axis_index("y") + lax.axis_index("c")
            mine = [pltpu.make_async_copy(src_refs[k], land_refs[k].at[me], local_sems.at[k]) for k in range(n)]
            for cp in mine:
                cp.start()
        for k in range(n):
            for cp in _peer_copies(src_refs[k], land_refs[k], outs[4 * k], outs[4 * k + 1], gather):
                cp.start()
        for cp in mine:
            cp.wait()
        token = outs[4 * n]
        token[...] = jnp.zeros_like(token)

    out_shape, out_specs, aliases = [], [], {}
    for k, src in enumerate(srcs):
        out_shape += [pltpu.SemaphoreType.DMA((N_DEV - 1,)), pltpu.SemaphoreType.DMA((N_DEV - 1,)),
                      pltpu.HBM(src.shape, src.dtype), pltpu.HBM(land_shapes[k], src.dtype)]
        out_specs += [SEM, SEM, HBM, HBM]
        aliases[k] = 4 * k + 2
        aliases[n + k] = 4 * k + 3
    out_shape.append(jax.ShapeDtypeStruct((8, 128), F32))
    out_specs.append(pl.BlockSpec(memory_space=pltpu.VMEM))
    res = pl.pallas_call(
        body, name=name, out_shape=tuple(out_shape), in_specs=(HBM,) * (2 * n) + (ANY,) * len(extra),
        out_specs=tuple(out_specs), input_output_aliases=aliases, scratch_shapes=[pltpu.SemaphoreType.DMA((n,))],
        compiler_params=pltpu.CompilerParams(has_side_effects=EFFECT),
    )(*[pltpu.with_memory_space_constraint(src, pltpu.HBM) for src in srcs],
      *[pltpu.with_memory_space_constraint(lax.empty(shp, src.dtype), pltpu.HBM) for shp, src in zip(land_shapes, srcs)],
      *extra)
    return [tuple(res[4 * k:4 * k + 4]) for k in range(n)], res[4 * n][0, 0]


def exchange_wait(handles, after, gather, name):
    send_sems, recv_sems, src_thru, land_thru = handles

    def body(src_ref, land_ref, send_sems, recv_sems, after_ref, src_dead, got_ref):
        for cp in _peer_copies(src_ref, land_ref, send_sems, recv_sems, gather):
            cp.wait_send()
            cp.wait_recv()

    return pl.pallas_call(
        body, name=name,
        out_shape=(pltpu.HBM(src_thru.shape, src_thru.dtype), pltpu.HBM(land_thru.shape, land_thru.dtype)),
        in_specs=(HBM, HBM, SEM, SEM, ANY), out_specs=(HBM, HBM), input_output_aliases={0: 0, 1: 1},
        compiler_params=pltpu.CompilerParams(has_side_effects=EFFECT),
    )(src_thru, land_thru, send_sems, recv_sems, after)[1]


def _to_pattern(a, r):
    return a.reshape(SEQ // r, r * a.shape[1])


def local_step(x, tgt, mod, get_w, put_grad, wc, wf, g_mix, bc, lg, lb, gco, gao, g_ffn, bf, g_fin):
    h1 = rms_mod_fwd(x, g_mix, mod, 0, 1, "h1_fwd")
    w_in = get_w("w_in", h1)
    proj = matmul(h1, w_in, "nt", F32, 512, D_IN, "proj_fwd")
    mix_a = conv_module_fwd(proj, wc, bc, lg, lb, gco, "conv_module_fwd")
    att, lse = attn_fwd_all(proj, "attn_fwd")
    mix_b = rms_gain_bf16(att, gao, "attn_out_norm")
    mixed = jnp.concatenate([mix_a, mix_b], axis=1)
    w_out = get_w("w_out", mixed)
    y1 = matmul(mixed, w_out, "nn", F32, 512, D_MODEL, "out_proj_fwd")
    x1, h2 = resid_rms_mod_fwd(x, y1, g_ffn, mod, 2, 3, 4, "x1_h2_fwd")
    w_up = get_w("w_up", h2)
    up0 = matmul(h2, w_up, "nt", F32, 512, D_FF, "up_fwd")
    act = ffn_act_fwd(up0, wf, bf, "ffn_act_fwd")
    w_down = get_w("w_down", act)
    y2 = matmul(act, w_down, "nn", F32, 512, D_MODEL, "down_fwd")
    loss_t, dx2, dy2, d_gfin, d_gaf = final_loss_bwd(x1, y2, tgt, g_fin, mod, 5, "loss_bwd")
    dact = matmul(dy2, w_down, "nt", F32, 512, FFN_TN, "down_bwd_x")
    dw_down = matmul(act, dy2, "tn", BF16, 256, D_MODEL, "down_bwd_w")
    dup0, dbf_g, dbf_v, dwf_g, dwf_v = ffn_bwd(up0, dact, wf + put_grad("w_down", dw_down), bf, "ffn_bwd")
    dh2 = matmul_halves(dup0, w_up, 512, 512, "up_bwd_x")
    dw_up = matmul_tn_halves(dup0, h2, 256, "up_bwd_w")
    dx1, d_shf, d_scf, d_gffn, dy1, d_gam = rms_mod_bwd(x1, dh2, dx2, g_ffn + put_grad("w_up", dw_up), mod, 4, y1, 2, "h2_bwd")
    dmixed = matmul(dy1, w_out, "nt", F32, 512, D_MODEL, "out_proj_bwd_x")
    dw_out = matmul(mixed, dy1, "tn", BF16, 256, D_MODEL, "out_proj_bwd_w")
    do, dd, d_gao = attn_combine_bwd(dmixed, att, gao + put_grad("w_out", dw_out), "attn_combine_bwd")
    dqkv = attn_bwd_all(proj, do, lse, dd, "attn_bwd")
    du1, d_gco, d_lg, d_lb, d_bc, d_wc = conv_module_bwd_a(proj, dmixed, wc, bc, lg, lb, gco, "conv_module_bwd_a")
    dproj_a = conv_module_bwd_b(proj, du1, wc, "conv_module_bwd_b")
    dproj = jnp.concatenate([dproj_a, dqkv[0], dqkv[1], dqkv[2]], axis=1)
    dw_in = matmul(dproj, h1, "tn", BF16, 512, D_MODEL, "proj_bwd_w")
    dh1 = matmul(dproj, w_in, "nn", F32, 512, D_MODEL, "proj_bwd_x")
    dx, d_shm, d_scm, d_gmix = rms_mod_bwd(x, dh1, dx1, g_mix + put_grad("w_in", dw_in), mod, 1, None, 0, "h1_bwd")
    dmod = jnp.concatenate([d_shm, d_scm, d_gam, d_shf, d_scf, d_gaf], axis=1)
    small = dict(g_norm_mix=d_gmix, b_conv_dw=d_bc, ln_conv_g=d_lg, ln_conv_b=d_lb, g_conv_out=d_gco, g_attn_out=d_gao,
                 g_norm_ffn=d_gffn, b_ffn_dw=jnp.concatenate([dbf_g, dbf_v], axis=1), g_final=d_gfin,
                 w_conv_dw=d_wc, w_ffn_dw=jnp.concatenate([dwf_g, dwf_v], axis=1), dmod=dmod, loss=loss_t[0:1, 0:1])
    return dx, small


def _padw(a, width):
    return jnp.pad(a, ((0, 0), (0, width - a.shape[1])))


def pack_small(t):
    wide = jnp.concatenate([_padw(t["dmod"], PACK_W), _padw(t["b_ffn_dw"], PACK_W), _padw(t["w_ffn_dw"], PACK_W),
                            _padw(t["loss"], PACK_W), jnp.zeros((2, PACK_W), F32)], axis=0)
    z512 = jnp.zeros((1, 512), F32)
    narrow = jnp.concatenate([
        t["g_norm_mix"], t["g_norm_ffn"], t["g_final"],
        jnp.concatenate([t["b_conv_dw"], t["ln_conv_g"]], axis=1),
        jnp.concatenate([t["ln_conv_b"], t["g_conv_out"]], axis=1),
        jnp.concatenate([t["g_attn_out"], z512], axis=1),
        jnp.zeros((2, 1024), F32),
        jnp.pad(t["w_conv_dw"], ((0, 1), (0, 0))).reshape(16, 1024)], axis=0)
    return jnp.concatenate([wide, narrow.reshape(4, PACK_W), jnp.zeros((4, PACK_W), F32)], axis=0)


def unpack_small(p):
    narrow = p[8:12].reshape(24, 1024)
    return dict(
        dmod=p[0:1], b_ffn_dw=p[1:2, :2 * D_FF], w_ffn_dw=p[2:5, :2 * D_FF], loss=p[5, 0],
        g_norm_mix=narrow[0:1], g_norm_ffn=narrow[1:2], g_final=narrow[2:3],
        b_conv_dw=narrow[3:4, :512], ln_conv_g=narrow[3:4, 512:], ln_conv_b=narrow[4:5, :512], g_conv_out=narrow[4:5, 512:],
        g_attn_out=narrow[5:6, :512], w_conv_dw=narrow[8:24].reshape(32, 512)[:CONV_K])


def _embed(local, width, me):
    return lax.dynamic_update_slice(jnp.zeros((local.shape[0], width), F32), local, (0, me * local.shape[1]))


def _shard(full, n_cols, me):
    return lax.dynamic_slice(full, (0, me * n_cols), (full.shape[0], n_cols))


WEIGHTS = ["w_ada", "b_ada", "g_norm_mix", "w_in", "w_conv_dw", "b_conv_dw", "ln_conv_g", "ln_conv_b", "g_conv_out",
           "g_attn_out", "w_out", "g_norm_ffn", "w_up", "w_ffn_dw", "b_ffn_dw", "w_down", "g_final"]
SMALL_REPLICATED = ["g_norm_mix", "b_conv_dw", "ln_conv_g", "ln_conv_b", "g_conv_out", "g_attn_out", "g_norm_ffn",
                    "b_ffn_dw", "g_final"]


def kernel(x, c, w_ada, b_ada, g_norm_mix, w_in, w_conv_dw, b_conv_dw, ln_conv_g, ln_conv_b, g_conv_out, g_attn_out, w_out, g_norm_ffn, w_up, w_ffn_dw, b_ffn_dw, w_down, g_final, loss_target, m_w_ada, m_b_ada, m_g_norm_mix, m_w_in, m_w_conv_dw, m_b_conv_dw, m_ln_conv_g, m_ln_conv_b, m_g_conv_out, m_g_attn_out, m_w_out, m_g_norm_ffn, m_w_up, m_w_ffn_dw, m_b_ffn_dw, m_w_down, m_g_final, v_w_ada, v_b_ada, v_g_norm_mix, v_w_in, v_w_conv_dw, v_b_conv_dw, v_ln_conv_g, v_ln_conv_b, v_g_conv_out, v_g_attn_out, v_w_out, v_g_norm_ffn, v_w_up, v_w_ffn_dw, v_b_ffn_dw, v_w_down, v_g_final):
    args = dict(locals())
    me = 4 * lax.axis_index("x") + 2 * lax.axis_index("y") + lax.axis_index("c")

    def flat(name, prefix=""):
        a = args[prefix + name]
        return a.reshape(a.shape[-2] if a.ndim > 1 else 1, a.shape[-1])

    def flat_t(name, prefix=""):
        return args[prefix + name][0].T

    n_in, n_up, r_out, r_down = w_in.shape[2], w_up.shape[2], w_out.shape[1], w_down.shape[1]
    n_ada, n_wc, n_wf = w_ada.shape[2], w_conv_dw.shape[2], w_ffn_dw.shape[2]
    taps_c = jnp.pad(flat("w_conv_dw").reshape(1, CONV_K * n_wc), ((0, 0), (0, 2 * D_MODEL - CONV_K * n_wc)))
    taps_f = jnp.pad(flat("w_ffn_dw").reshape(1, FFN_K * n_wf), ((0, 0), (0, 3 * D_MODEL - FFN_K * n_wf)))
    first = jnp.concatenate([c, taps_c.reshape(2, D_MODEL), taps_f.reshape(3, D_MODEL), jnp.zeros((2, D_MODEL), F32)], axis=0)
    w_in_block = flat_t("w_in").astype(BF16)
    (gather_w_in,), tok = exchange_start([w_in_block], True, "gather_w_in_start")
    first_all = all_gather(first + tok, "gather_c_taps")
    c_all = first_all[:, 0, :]
    wc_full = first_all[:, 1:3, :].reshape(N_DEV, 2 * D_MODEL)[:, :CONV_K * n_wc].reshape(N_DEV, CONV_K, n_wc)
    wc_full = wc_full.transpose(1, 0, 2).reshape(CONV_K, D_CONV)
    wf_full = first_all[:, 3:6, :].reshape(N_DEV, 3 * D_MODEL)[:, :FFN_K * n_wf].reshape(N_DEV, FFN_K, n_wf)
    wf_full = wf_full.transpose(1, 0, 2).reshape(FFN_K, 2 * D_FF)
    mod_cols = ada_fwd(c_all, flat("w_ada"), _shard(flat("b_ada"), n_ada, me), "ada_fwd")
    mod_all = all_gather(mod_cols, "gather_mod")
    mod = lax.dynamic_index_in_dim(mod_all, me, axis=1, keepdims=False).reshape(N_MOD, D_MODEL)
    mod = jnp.pad(mod, ((0, 2), (0, 0)))

    order = ("w_out", "w_up", "w_down")
    blocks = dict(w_up=flat_t("w_up").astype(BF16), w_out=flat("w_out").astype(BF16), w_down=flat("w_down").astype(BF16))
    handles, tok = exchange_start([blocks[name] for name in order], True, "gather_weights_start", mod_all)
    gathers = dict(zip(order, handles), w_in=gather_w_in)
    mod = mod + tok

    def gathered(name, after):
        return exchange_wait(gathers[name], after, True, f"gather_{name}_wait")

    def get_w(name, after):
        return gathered(name, after).reshape(-1, D_MODEL)

    exchanges, own = {}, {}

    def put_grad(name, dw, after=None):
        dev_major = dw.reshape(N_DEV, -1, D_MODEL)
        own[name] = lax.dynamic_index_in_dim(dev_major, me, axis=0, keepdims=False)
        (exchanges[name],), token = exchange_start([dev_major], False, f"exchange_{name}_start", after)
        return token

    grad_x, small = local_step(
        x[0], loss_target[0], mod, get_w, put_grad, wc_full, wf_full,
        flat("g_norm_mix"), flat("b_conv_dw"), flat("ln_conv_g"), flat("ln_conv_b"), flat("g_conv_out"),
        flat("g_attn_out"), flat("g_norm_ffn"), flat("b_ffn_dw"), flat("g_final"))

    out = {}

    def finish(name, tr, after):
        parts = exchange_wait(exchanges[name], after, False, f"exchange_{name}_wait")
        if name in ("w_in", "w_up"):
            res = sum_adamw(parts, own[name], flat_t(name), flat_t(name, "m_"), flat_t(name, "v_"), tr, "adamw_" + name)
            out[name] = tuple(r.T for r in res)
        else:
            res = out[name] = sum_adamw(parts, own[name], flat(name), flat(name, "m_"), flat(name, "v_"), tr, "adamw_" + name)
        return res[0]

    after = finish("w_down", r_down, grad_x)
    after = finish("w_up", n_up // 2, after)
    after = finish("w_out", r_out, after)
    after = finish("w_in", n_in, after)

    small_all = all_gather(pack_small(small), "gather_small", after)

    def packed(prefix):
        t = {n: flat(n, prefix) for n in SMALL_REPLICATED}
        t["dmod"] = flat("b_ada", prefix)
        t["loss"] = jnp.zeros((1, 1), F32)
        t["w_conv_dw"] = _embed(flat("w_conv_dw", prefix), D_CONV, me)
        t["w_ffn_dw"] = _embed(flat("w_ffn_dw", prefix), 2 * D_FF, me)
        return pack_small(t)

    res = sum_adamw(small_all, None, packed(""), packed("m_"), packed("v_"), PACK_ROWS, "adamw_small")
    res = [unpack_small(r) for r in res]
    loss = res[0]["loss"]
    for n in SMALL_REPLICATED:
        out[n] = tuple(r[n] for r in res)
    out["b_ada"] = tuple(r["dmod"] for r in res)
    out["w_conv_dw"] = tuple(_shard(r["w_conv_dw"], n_wc, me) for r in res)
    out["w_ffn_dw"] = tuple(_shard(r["w_ffn_dw"], n_wf, me) for r in res)

    dmod_cols = _shard(small_all[:, 0, :], n_ada, me)
    out["w_ada"] = ada_bwd_adamw(c_all.T, dmod_cols, flat("w_ada"), flat("w_ada", "m_"), flat("w_ada", "v_"), "adamw_w_ada")

    result = [loss, grad_x[None]]
    for k in range(4):
        result += [out[n][k].reshape(args[n].shape) for n in WEIGHTS]
    return tuple(result)
```

```python
import functools

import jax
import jax.numpy as jnp
from jax import lax
from jax.experimental import pallas as pl
from jax.experimental.pallas import tpu as pltpu

F32 = jnp.float32
BF16 = jnp.bfloat16

N_DEV = 8
SEQ = 2048
D_MODEL = 1024
D_CONV = 512
D_ATTN = 512
HEAD_DIM = 64
CONV_K = 31
D_FF = 2816
FFN_K = 3
D_IN = 2 * D_CONV + 3 * D_ATTN
N_MOD = 6
EPS = 1e-6
ATTN_BLOCK = 128
PATTERNS = ((2048, 1), (512, 4), (128, 16))
NEG = -1e30

ADAM_LR, ADAM_B1, ADAM_B2, ADAM_EPS, ADAM_WD, ADAM_STEP = 0.001, 0.9, 0.999, 1e-08, 0.01, 10

ROWS = 256
CONV_HALO = 32
FFN_HALO = 8
FFN_TN = 1408
VMEM_LIMIT = 56 * 1024 * 1024
PACK_ROWS, PACK_W = 16, 6144


def _cp(*sem):
    return pltpu.CompilerParams(dimension_semantics=sem if sem else None, vmem_limit_bytes=VMEM_LIMIT)


def _sig(x):
    return 1.0 / (1.0 + jnp.exp(-x))


def _rsum(x):
    return jnp.sum(x, axis=0, keepdims=True)


def _mean(x):
    return jnp.mean(x, axis=-1, keepdims=True)


def _acc(ref, val, first):
    @pl.when(first)
    def _():
        ref[...] = val

    @pl.when(jnp.logical_not(first))
    def _():
        ref[...] += val


SUB = 16


def _for_chunks(fn, unroll=1, rows=ROWS):
    def step(i, carry):
        fn(pl.ds(pl.multiple_of(i * SUB, SUB), SUB))
        return carry

    lax.fori_loop(0, rows // SUB, step, 0, unroll=unroll)


def rms_mod_fwd(x, g, mod, sh_row, sc_row, name):
    def body(x_ref, g_ref, mod_ref, h_ref):
        xx = x_ref[...]
        r = lax.rsqrt(_mean(xx * xx) + EPS)
        h = xx * r * g_ref[...]
        h_ref[...] = (h * (1.0 + mod_ref[sc_row:sc_row + 1, :]) + mod_ref[sh_row:sh_row + 1, :]).astype(BF16)

    return pl.pallas_call(
        body, out_shape=jax.ShapeDtypeStruct((SEQ, D_MODEL), BF16), grid=(SEQ // ROWS,),
        in_specs=[_row_spec(D_MODEL), _vec_spec(D_MODEL), _vec_spec(D_MODEL, 8)],
        out_specs=_row_spec(D_MODEL), name=name, compiler_params=_cp("parallel"))(x, g, mod)


def resid_rms_mod_fwd(x, y, g, mod, ga_row, sh_row, sc_row, name):
    def body(x_ref, y_ref, g_ref, mod_ref, x1_ref, h_ref):
        x1 = x_ref[...] + mod_ref[ga_row:ga_row + 1, :] * y_ref[...]
        x1_ref[...] = x1
        r = lax.rsqrt(_mean(x1 * x1) + EPS)
        h = x1 * r * g_ref[...]
        h_ref[...] = (h * (1.0 + mod_ref[sc_row:sc_row + 1, :]) + mod_ref[sh_row:sh_row + 1, :]).astype(BF16)

    return pl.pallas_call(
        body, out_shape=(jax.ShapeDtypeStruct((SEQ, D_MODEL), F32), jax.ShapeDtypeStruct((SEQ, D_MODEL), BF16)),
        grid=(SEQ // ROWS,),
        in_specs=[_row_spec(D_MODEL), _row_spec(D_MODEL), _vec_spec(D_MODEL), _vec_spec(D_MODEL, 8)],
        out_specs=(_row_spec(D_MODEL), _row_spec(D_MODEL)), name=name, compiler_params=_cp("parallel"))(x, y, g, mod)


def final_loss_bwd(x1, y2, tgt, g, mod, ga_row, name):
    def body(x1_ref, y2_ref, t_ref, g_ref, mod_ref, loss_ref, dx2_ref, dy2_ref, dg_ref, dga_ref):
        first = pl.program_id(0) == 0
        ga = mod_ref[ga_row:ga_row + 1, :]
        y2 = y2_ref[...]
        x2 = x1_ref[...] + ga * y2
        r = lax.rsqrt(_mean(x2 * x2) + EPS)
        xn = x2 * r
        err = xn * g_ref[...] - t_ref[...]
        _acc(loss_ref, jnp.broadcast_to(0.5 * jnp.sum(_mean(err * err)), (8, 128)), first)
        dy = err * (1.0 / D_MODEL)
        _acc(dg_ref, _rsum(dy * xn), first)
        dxn = dy * g_ref[...]
        dx2 = r * (dxn - xn * _mean(dxn * xn))
        dx2_ref[...] = dx2
        dy2_ref[...] = (dx2 * ga).astype(BF16)
        _acc(dga_ref, _rsum(dx2 * y2), first)

    vec = jax.ShapeDtypeStruct((1, D_MODEL), F32)
    return pl.pallas_call(
        body,
        out_shape=(jax.ShapeDtypeStruct((8, 128), F32), jax.ShapeDtypeStruct((SEQ, D_MODEL), F32),
                   jax.ShapeDtypeStruct((SEQ, D_MODEL), BF16), vec, vec),
        grid=(SEQ // ROWS,),
        in_specs=[_row_spec(D_MODEL), _row_spec(D_MODEL), _row_spec(D_MODEL), _vec_spec(D_MODEL), _vec_spec(D_MODEL, 8)],
        out_specs=(pl.BlockSpec((8, 128), lambda i: (0, 0)), _row_spec(D_MODEL), _row_spec(D_MODEL),
                   _vec_spec(D_MODEL), _vec_spec(D_MODEL)),
        name=name, compiler_params=_cp("arbitrary"))(x1, y2, tgt, g, mod)


def rms_mod_bwd(x, dh, dres, g, mod, sc_row, y, ga_row, name):
    gated = y is not None

    def body(*refs):
        if gated:
            x_ref, dh_ref, dres_ref, g_ref, mod_ref, y_ref, dx_ref, dsh_ref, dsc_ref, dg_ref, dy_ref, dga_ref = refs
        else:
            x_ref, dh_ref, dres_ref, g_ref, mod_ref, dx_ref, dsh_ref, dsc_ref, dg_ref = refs
        first = pl.program_id(0) == 0
        xx = x_ref[...]
        dh = dh_ref[...]
        gg = g_ref[...]
        r = lax.rsqrt(_mean(xx * xx) + EPS)
        xn = xx * r
        _acc(dsh_ref, _rsum(dh), first)
        _acc(dsc_ref, _rsum(dh * (xn * gg)), first)
        dt = dh * (1.0 + mod_ref[sc_row:sc_row + 1, :])
        _acc(dg_ref, _rsum(dt * xn), first)
        dxn = dt * gg
        dx = dres_ref[...] + r * (dxn - xn * _mean(dxn * xn))
        dx_ref[...] = dx
        if gated:
            _acc(dga_ref, _rsum(dx * y_ref[...]), first)
            dy_ref[...] = (dx * mod_ref[ga_row:ga_row + 1, :]).astype(BF16)

    vec = jax.ShapeDtypeStruct((1, D_MODEL), F32)
    in_specs = [_row_spec(D_MODEL), _row_spec(D_MODEL), _row_spec(D_MODEL), _vec_spec(D_MODEL), _vec_spec(D_MODEL, 8)]
    out_shape = [jax.ShapeDtypeStruct((SEQ, D_MODEL), F32), vec, vec, vec]
    out_specs = [_row_spec(D_MODEL), _vec_spec(D_MODEL), _vec_spec(D_MODEL), _vec_spec(D_MODEL)]
    args = [x, dh, dres, g, mod]
    if gated:
        in_specs.append(_row_spec(D_MODEL))
        out_shape += [jax.ShapeDtypeStruct((SEQ, D_MODEL), BF16), vec]
        out_specs += [_row_spec(D_MODEL), _vec_spec(D_MODEL)]
        args.append(y)
    return pl.pallas_call(
        body, out_shape=tuple(out_shape), grid=(SEQ // ROWS,), in_specs=in_specs, out_specs=tuple(out_specs),
        name=name, compiler_params=_cp("arbitrary"))(*args)


def _prev_halo(halo, width, col):
    per = ROWS // halo
    return pl.BlockSpec((halo, width), lambda i: (jnp.maximum(i * per - 1, 0), col))


def _next_halo(halo, width, col):
    per = ROWS // halo
    last = SEQ // halo - 1
    return pl.BlockSpec((halo, width), lambda i: (jnp.minimum((i + 1) * per, last), col))


CONV_PAD = ROWS + CONV_HALO


def _shift_copies(sh):
    for b in range(1, 8):
        sh[b, 0:CONV_PAD - 8, :] = sh[0, pl.ds(b, CONV_PAD - 8), :]


def _tap(sh, rs_start, offset):
    return sh[offset % 8, pl.ds(pl.multiple_of(rs_start + (offset // 8) * 8, 8), SUB), :]


def _conv_module_forward(av_ref, ag_ref, avh_ref, agh_ref, wc_ref, bc_ref, lg_ref, lb_ref, sh, u1_s):
    i = pl.program_id(0)
    hv = avh_ref[...] * _sig(agh_ref[...])
    sh[0, 0:CONV_HALO, :] = jnp.where(i > 0, hv, 0.0)

    def glu(rs):
        sh[0, pl.ds(pl.multiple_of(rs.start + CONV_HALO, SUB), SUB), :] = av_ref[rs, :] * _sig(ag_ref[rs, :])

    _for_chunks(glu)
    _shift_copies(sh)

    def conv(rs):
        u1 = jnp.broadcast_to(bc_ref[...], (SUB, D_CONV))
        for j in range(CONV_K):
            u1 = u1 + wc_ref[j:j + 1, :] * _tap(sh, rs.start, CONV_HALO - (CONV_K - 1) + j)
        u1_s[rs, :] = u1

    _for_chunks(conv)
    u1 = u1_s[...]
    mu = _mean(u1)
    cen = u1 - mu
    rs = lax.rsqrt(_mean(cen * cen) + EPS)
    z = cen * rs
    ln = z * lg_ref[...] + lb_ref[...]
    s = _sig(ln)
    return z, rs, ln, s, ln * s


_CONV_SCRATCH = [pltpu.VMEM((8, CONV_PAD, D_CONV), F32), pltpu.VMEM((ROWS, D_CONV), F32)]


def conv_module_fwd(proj, wc, bc, lg, lb, gco, name):
    def body(av_ref, ag_ref, avh_ref, agh_ref, wc_ref, bc_ref, lg_ref, lb_ref, gco_ref, out_ref, sh, u1_s):
        _, _, _, _, u2 = _conv_module_forward(av_ref, ag_ref, avh_ref, agh_ref, wc_ref, bc_ref, lg_ref, lb_ref, sh, u1_s)
        rc = lax.rsqrt(_mean(u2 * u2) + EPS)
        out_ref[...] = (u2 * rc * gco_ref[...]).astype(BF16)

    v = _vec_spec(D_CONV)
    return pl.pallas_call(
        body, out_shape=jax.ShapeDtypeStruct((SEQ, D_CONV), BF16), grid=(SEQ // ROWS,),
        in_specs=[_row_spec(D_CONV, 0), _row_spec(D_CONV, 1), _prev_halo(CONV_HALO, D_CONV, 0),
                  _prev_halo(CONV_HALO, D_CONV, 1), _vec_spec(D_CONV, CONV_K), v, v, v, v],
        out_specs=_row_spec(D_CONV), scratch_shapes=list(_CONV_SCRATCH),
        name=name, compiler_params=_cp("parallel"))(proj, proj, proj, proj, wc, bc, lg, lb, gco)


def conv_module_bwd_a(proj, dmixed, wc, bc, lg, lb, gco, name):
    def body(av_ref, ag_ref, avh_ref, agh_ref, dm_ref, wc_ref, bc_ref, lg_ref, lb_ref, gco_ref,
             du1_ref, dgco_ref, dlg_ref, dlb_ref, dbc_ref, dwc_ref, sh, u1_s, acc):
        first = pl.program_id(0) == 0
        z, rs, ln, s, u2 = _conv_module_forward(av_ref, ag_ref, avh_ref, agh_ref, wc_ref, bc_ref, lg_ref, lb_ref, sh, u1_s)
        rc = lax.rsqrt(_mean(u2 * u2) + EPS)
        xn = u2 * rc
        dm = dm_ref[...]
        _acc(dgco_ref, _rsum(dm * xn), first)
        dyn = dm * gco_ref[...]
        du2 = rc * (dyn - xn * _mean(dyn * xn))
        dln = du2 * (s * (1.0 + ln * (1.0 - s)))
        _acc(dlg_ref, _rsum(dln * z), first)
        _acc(dlb_ref, _rsum(dln), first)
        dz = dln * lg_ref[...]
        du1 = rs * (dz - _mean(dz) - z * _mean(dz * z))
        du1_ref[...] = du1
        _acc(dbc_ref, _rsum(du1), first)
        acc[...] = jnp.zeros_like(acc)

        def taps(rs):
            d = du1_ref[rs, :]
            for j in range(CONV_K):
                acc[j] += d * _tap(sh, rs.start, CONV_HALO - (CONV_K - 1) + j)

        _for_chunks(taps)

        @pl.when(first)
        def _():
            dwc_ref[...] = jnp.zeros_like(dwc_ref)

        for j in range(CONV_K):
            dwc_ref[j:j + 1, :] += _rsum(acc[j])

    v = _vec_spec(D_CONV)
    vec = jax.ShapeDtypeStruct((1, D_CONV), F32)
    return pl.pallas_call(
        body,
        out_shape=(jax.ShapeDtypeStruct((SEQ, D_CONV), F32), vec, vec, vec, vec, jax.ShapeDtypeStruct((CONV_K, D_CONV), F32)),
        grid=(SEQ // ROWS,),
        in_specs=[_row_spec(D_CONV, 0), _row_spec(D_CONV, 1), _prev_halo(CONV_HALO, D_CONV, 0),
                  _prev_halo(CONV_HALO, D_CONV, 1), _row_spec(D_CONV, 0), _vec_spec(D_CONV, CONV_K), v, v, v, v],
        out_specs=(_row_spec(D_CONV), v, v, v, v, _vec_spec(D_CONV, CONV_K)),
        scratch_shapes=list(_CONV_SCRATCH) + [pltpu.VMEM((CONV_K, SUB, D_CONV), F32)],
        name=name, compiler_params=_cp("arbitrary"))(proj, proj, proj, proj, dmixed, wc, bc, lg, lb, gco)


def conv_module_bwd_b(proj, du1, wc, name):
    def body(av_ref, ag_ref, du1_ref, du1n_ref, wc_ref, out_ref, sh):
        i = pl.program_id(0)
        sh[0, 0:ROWS, :] = du1_ref[...]
        sh[0, ROWS:, :] = jnp.where(i < SEQ // ROWS - 1, du1n_ref[...], 0.0)
        _shift_copies(sh)

        def chunk(rs):
            du0 = jnp.zeros((SUB, D_CONV), F32)
            for j in range(CONV_K):
                du0 = du0 + wc_ref[j:j + 1, :] * _tap(sh, rs.start, CONV_K - 1 - j)
            sg = _sig(ag_ref[rs, :])
            out_ref[rs, 0:D_CONV] = (du0 * sg).astype(BF16)
            out_ref[rs, D_CONV:] = (du0 * av_ref[rs, :] * sg * (1.0 - sg)).astype(BF16)

        _for_chunks(chunk)

    return pl.pallas_call(
        body, out_shape=jax.ShapeDtypeStruct((SEQ, 2 * D_CONV), BF16), grid=(SEQ // ROWS,),
        in_specs=[_row_spec(D_CONV, 0), _row_spec(D_CONV, 1), _row_spec(D_CONV, 0), _next_halo(CONV_HALO, D_CONV, 0),
                  _vec_spec(D_CONV, CONV_K)],
        out_specs=_row_spec(2 * D_CONV), scratch_shapes=[pltpu.VMEM((8, CONV_PAD, D_CONV), F32)],
        name=name, compiler_params=_cp("parallel"))(proj, proj, du1, du1, wc)


def _attn_specs(sub_len, pairs):
    ng, width = 4 // pairs, 128 * pairs
    q = pl.BlockSpec((sub_len, width), lambda rho, g: (0, rho * 3 * ng + g))
    k = pl.BlockSpec((sub_len, width), lambda rho, g: (0, rho * 3 * ng + ng + g))
    v = pl.BlockSpec((sub_len, width), lambda rho, g: (0, rho * 3 * ng + 2 * ng + g))
    o = pl.BlockSpec((sub_len, width), lambda rho, g: (0, rho * ng + g))
    return q, k, v, o


ATTN_PAIRS = {1: 1, 4: 1, 16: 4}


def _attn_block(q_ref, k_ref, v_ref, n, hs, win):
    q0 = pl.multiple_of(n * ATTN_BLOCK, ATTN_BLOCK)
    k0 = pl.multiple_of(jnp.maximum(n - 1, 0) * ATTN_BLOCK, ATTN_BLOCK)
    qb = q_ref[pl.ds(q0, ATTN_BLOCK), hs]
    kw = k_ref[pl.ds(k0, win), hs]
    vw = v_ref[pl.ds(k0, win), hs]
    s = lax.dot_general(qb, kw, (((1,), (1,)), ((), ())), preferred_element_type=F32) * (HEAD_DIM ** -0.5)
    dist = (q0 - k0) + lax.broadcasted_iota(jnp.int32, (ATTN_BLOCK, win), 0) \
        - lax.broadcasted_iota(jnp.int32, (ATTN_BLOCK, win), 1)
    s = jnp.where((dist >= 0) & (dist <= ATTN_BLOCK), s, NEG)
    return q0, k0, qb, kw, vw, s


def attn_fwd(qkv_r, sub_len, r, name):
    nb = sub_len // ATTN_BLOCK
    win = 2 * ATTN_BLOCK if nb > 1 else ATTN_BLOCK
    pairs = ATTN_PAIRS[r]

    def body(q_ref, k_ref, v_ref, o_ref, l_ref):
        def block(n, carry):
            for h in range(2 * pairs):
                hs = slice(h * HEAD_DIM, (h + 1) * HEAD_DIM)
                q0, _, _, _, vw, s = _attn_block(q_ref, k_ref, v_ref, n, hs, win)
                m = jnp.max(s, axis=1, keepdims=True)
                p = jnp.exp(s - m)
                den = jnp.sum(p, axis=1, keepdims=True)
                o = jnp.dot(p.astype(BF16), vw, preferred_element_type=F32) / den
                o_ref[pl.ds(q0, ATTN_BLOCK), hs] = o
                l_ref[pl.ds(q0, ATTN_BLOCK), hs] = jnp.broadcast_to(m + jnp.log(den), (ATTN_BLOCK, HEAD_DIM))
            return carry

        lax.fori_loop(0, nb, block, 0, unroll=min(nb, 2))

    q, k, v, o = _attn_specs(sub_len, pairs)
    shp = jax.ShapeDtypeStruct((sub_len, r * D_ATTN), F32)
    return pl.pallas_call(
        body, out_shape=(shp, shp), grid=(r, 4 // pairs), in_specs=[q, k, v], out_specs=(o, o),
        name=name, compiler_params=_cp("parallel", "parallel"))(qkv_r, qkv_r, qkv_r)


def attn_bwd(qkv_r, do_r, lse_r, dd_r, sub_len, r, name):
    nb = sub_len // ATTN_BLOCK
    win = 2 * ATTN_BLOCK if nb > 1 else ATTN_BLOCK
    pairs = ATTN_PAIRS[r]

    def body(q_ref, k_ref, v_ref, do_ref, l_ref, dd_ref, dq_ref, dk_ref, dv_ref):
        dk_ref[...] = jnp.zeros_like(dk_ref)
        dv_ref[...] = jnp.zeros_like(dv_ref)

        def block(n, carry):
            for h in range(2 * pairs):
                hs = slice(h * HEAD_DIM, (h + 1) * HEAD_DIM)
                h1 = slice(h * HEAD_DIM, h * HEAD_DIM + 1)
                q0, k0, qb, kw, vw, s = _attn_block(q_ref, k_ref, v_ref, n, hs, win)
                dob = do_ref[pl.ds(q0, ATTN_BLOCK), hs]
                p = jnp.exp(s - l_ref[pl.ds(q0, ATTN_BLOCK), h1])
                dp = lax.dot_general(dob, vw, (((1,), (1,)), ((), ())), preferred_element_type=F32)
                ds = (p * (dp - dd_ref[pl.ds(q0, ATTN_BLOCK), h1]) * (HEAD_DIM ** -0.5)).astype(BF16)
                dq_ref[pl.ds(q0, ATTN_BLOCK), hs] = jnp.dot(ds, kw, preferred_element_type=F32)
                dk_ref[pl.ds(k0, win), hs] += lax.dot_general(ds, qb, (((0,), (0,)), ((), ())), preferred_element_type=F32)
                dv_ref[pl.ds(k0, win), hs] += lax.dot_general(p.astype(BF16), dob, (((0,), (0,)), ((), ())),
                                                              preferred_element_type=F32)
            return carry

        lax.fori_loop(0, nb, block, 0)

    q, k, v, o = _attn_specs(sub_len, pairs)
    shp = jax.ShapeDtypeStruct((sub_len, r * D_ATTN), F32)
    return pl.pallas_call(
        body, out_shape=(shp, shp, shp), grid=(r, 4 // pairs), in_specs=[q, k, v, o, o, o], out_specs=(o, o, o),
        name=name, compiler_params=_cp("parallel", "parallel"))(qkv_r, qkv_r, qkv_r, do_r, lse_r, dd_r)


def _rows(start, size, r):
    return pl.ds(start, size) if r == 1 else pl.ds(start, size, stride=r)


def _unit_rows(r, rho, n, nb):
    win = 2 * ATTN_BLOCK if nb > 1 else ATTN_BLOCK
    if isinstance(n, int):
        kb = max(n - 1, 0)
        q_rows = _rows(rho + r * ATTN_BLOCK * n, ATTN_BLOCK, r)
        k_rows = _rows(rho + r * ATTN_BLOCK * kb, win, r)
    else:
        kb = jnp.maximum(n - 1, 0)
        q_rows = pl.ds(pl.multiple_of(n * ATTN_BLOCK, ATTN_BLOCK), ATTN_BLOCK)
        k_rows = pl.ds(pl.multiple_of(kb * ATTN_BLOCK, ATTN_BLOCK), win)
    dist = (n - kb) * ATTN_BLOCK + lax.broadcasted_iota(jnp.int32, (ATTN_BLOCK, win), 0) \
        - lax.broadcasted_iota(jnp.int32, (ATTN_BLOCK, win), 1)
    return q_rows, k_rows, (dist >= 0) & (dist <= ATTN_BLOCK)


def _per_head(x):
    lane = lax.broadcasted_iota(jnp.int32, x.shape, 1)
    zero = jnp.zeros_like(x)
    return [jnp.where(lane < HEAD_DIM, x, zero), jnp.where(lane >= HEAD_DIM, x, zero)]


NT = (((1,), (1,)), ((), ()))


def _masked_scores(q2, k2, valid):
    return [jnp.where(valid, lax.dot_general(qh, k2, NT, preferred_element_type=F32) * (HEAD_DIM ** -0.5), NEG)
            for qh in _per_head(q2)]


def _attn_units(r, nb, unit):
    if r == 1:
        def four(i, carry):
            for k in range(4):
                unit(0, 4 * i + k)
            return carry
        lax.fori_loop(0, nb // 4, four, 0)
    else:
        for rho in range(r):
            for n in range(nb):
                unit(rho, n)


N_UNITS = 16


def attn_fwd_all(proj, name):
    def body(q_ref, k_ref, v_ref, att_ref, lse_ref, s_scr, p_scr, lse_scr, den_scr):
        for idx, (sub_len, r) in enumerate(PATTERNS):
            nb = sub_len // ATTN_BLOCK
            win = 2 * ATTN_BLOCK if nb > 1 else ATTN_BLOCK

            def scores(rho, n, r=r, nb=nb, win=win):
                u = rho * nb + n
                q_rows, k_rows, valid = _unit_rows(r, rho, n, nb)
                ss = _masked_scores(q_ref[q_rows, :].astype(BF16), k_ref[k_rows, :].astype(BF16), valid)
                for h in range(2):
                    s_scr[2 * u + h, :, 0:win] = ss[h]

            _attn_units(r, nb, scores)

            def softmax(u, carry, win=win):
                lses, dens = [], []
                for h in range(2):
                    sc = s_scr[2 * u + h, :, 0:win]
                    m = jnp.max(sc, axis=1, keepdims=True)
                    p = jnp.exp(sc - m)
                    den = jnp.sum(p, axis=1, keepdims=True)
                    p_scr[2 * u + h, :, 0:win] = p.astype(BF16)
                    lses.append(jnp.broadcast_to(m + jnp.log(den), (ATTN_BLOCK, HEAD_DIM)))
                    dens.append(jnp.broadcast_to(den, (ATTN_BLOCK, HEAD_DIM)))
                lse_scr[u] = jnp.concatenate(lses, axis=1)
                den_scr[u] = jnp.concatenate(dens, axis=1)
                return carry

            lax.fori_loop(0, N_UNITS, softmax, 0, unroll=2)

            def outputs(rho, n, r=r, nb=nb, win=win, idx=idx):
                u = rho * nb + n
                q_rows, k_rows, _ = _unit_rows(r, rho, n, nb)
                vs = _per_head(v_ref[k_rows, :].astype(BF16))
                o = (jnp.dot(p_scr[2 * u, :, 0:win], vs[0], preferred_element_type=F32)
                     + jnp.dot(p_scr[2 * u + 1, :, 0:win], vs[1], preferred_element_type=F32)) / den_scr[u]
                lse = lse_scr[u]
                if idx > 0:
                    old = lse_ref[q_rows, :]
                    top = jnp.maximum(old, lse)
                    new = top + jnp.log(jnp.exp(old - top) + jnp.exp(lse - top))
                    o = att_ref[q_rows, :] * jnp.exp(old - new) + o * jnp.exp(lse - new)
                    lse = new
                att_ref[q_rows, :] = o
                lse_ref[q_rows, :] = lse

            _attn_units(r, nb, outputs)

    blk = lambda first: pl.BlockSpec((SEQ, 128), lambda g: (0, first + g))
    shp = jax.ShapeDtypeStruct((SEQ, D_ATTN), F32)
    big = (2 * N_UNITS, ATTN_BLOCK, 2 * ATTN_BLOCK)
    small = pltpu.VMEM((N_UNITS, ATTN_BLOCK, 128), F32)
    return pl.pallas_call(
        body, out_shape=(shp, shp), grid=(4,), in_specs=[blk(8), blk(12), blk(16)], out_specs=(blk(0), blk(0)),
        scratch_shapes=[pltpu.VMEM(big, F32), pltpu.VMEM(big, BF16), small, small],
        name=name, compiler_params=_cp("parallel"))(proj, proj, proj)


def attn_bwd_all(proj, do, lse, dd, name):
    scale = HEAD_DIM ** -0.5

    def body(q_ref, k_ref, v_ref, do_ref, l_ref, dd_ref, out_ref, dq_s, dk_s, dv_s,
             s_scr, dp_scr, ds_scr, st_scr, dpt_scr, pt_scr, dst_scr, qb_scr, kb_scr, dob_scr):
        dq_s[...] = jnp.zeros_like(dq_s)
        dk_s[...] = jnp.zeros_like(dk_s)
        dv_s[...] = jnp.zeros_like(dv_s)
        for sub_len, r in PATTERNS:
            nb = sub_len // ATTN_BLOCK
            win = 2 * ATTN_BLOCK if nb > 1 else ATTN_BLOCK

            def scores(rho, n, r=r, nb=nb, win=win):
                u = rho * nb + n
                q_rows, k_rows, valid = _unit_rows(r, rho, n, nb)
                kb = max(n - 1, 0) if isinstance(n, int) else jnp.maximum(n - 1, 0)
                dist_t = (n - kb) * ATTN_BLOCK + lax.broadcasted_iota(jnp.int32, (win, ATTN_BLOCK), 1) \
                    - lax.broadcasted_iota(jnp.int32, (win, ATTN_BLOCK), 0)
                valid_t = (dist_t >= 0) & (dist_t <= ATTN_BLOCK)
                q2 = q_ref[q_rows, :].astype(BF16)
                k2 = k_ref[k_rows, :].astype(BF16)
                do2 = do_ref[q_rows, :].astype(BF16)
                qb_scr[u] = q2
                kb_scr[u, 0:win, :] = k2
                dob_scr[u] = do2
                l2 = l_ref[q_rows, :]
                d2 = dd_ref[q_rows, :]
                l2t = l2.T
                d2t = d2.T
                v2 = v_ref[k_rows, :].astype(BF16)
                qs, dos = _per_head(q2), _per_head(do2)
                for h in range(2):
                    c0 = h * HEAD_DIM
                    sc = lax.dot_general(qs[h], k2, NT, preferred_element_type=F32) * scale
                    s_scr[2 * u + h, :, 0:win] = jnp.where(valid, sc, NEG) - l2[:, c0:c0 + 1]
                    dp_scr[2 * u + h, :, 0:win] = lax.dot_general(dos[h], v2, NT, preferred_element_type=F32) \
                        - d2[:, c0:c0 + 1]
                    sct = lax.dot_general(k2, qs[h], NT, preferred_element_type=F32) * scale
                    st_scr[2 * u + h, 0:win, :] = jnp.where(valid_t, sct, NEG) - l2t[c0:c0 + 1, :]
                    dpt_scr[2 * u + h, 0:win, :] = lax.dot_general(v2, dos[h], NT, preferred_element_type=F32) \
                        - d2t[c0:c0 + 1, :]

            _attn_units(r, nb, scores)

            def pointwise(hu, carry, win=win):
                ds_scr[hu, :, 0:win] = (jnp.exp(s_scr[hu, :, 0:win]) * dp_scr[hu, :, 0:win] * scale).astype(BF16)
                pt = jnp.exp(st_scr[hu, 0:win, :])
                pt_scr[hu, 0:win, :] = pt.astype(BF16)
                dst_scr[hu, 0:win, :] = (pt * dpt_scr[hu, 0:win, :] * scale).astype(BF16)
                return carry

            lax.fori_loop(0, 2 * N_UNITS, pointwise, 0, unroll=4)

            def grads(rho, n, r=r, nb=nb, win=win):
                u = rho * nb + n
                q_rows, k_rows, _ = _unit_rows(r, rho, n, nb)
                qs, ks, dos = _per_head(qb_scr[u]), _per_head(kb_scr[u, 0:win, :]), _per_head(dob_scr[u])

                def both(scr, rows, rhs):
                    return (jnp.dot(scr[(2 * u,) + rows], rhs[0], preferred_element_type=F32)
                            + jnp.dot(scr[(2 * u + 1,) + rows], rhs[1], preferred_element_type=F32))

                dq_s[q_rows, :] += both(ds_scr, (slice(None), slice(0, win)), ks)
                dk_s[k_rows, :] += both(dst_scr, (slice(0, win), slice(None)), qs)
                dv_s[k_rows, :] += both(pt_scr, (slice(0, win), slice(None)), dos)

            _attn_units(r, nb, grads)
        out_ref[0] = dq_s[...].astype(BF16)
        out_ref[1] = dk_s[...].astype(BF16)
        out_ref[2] = dv_s[...].astype(BF16)

    blk = lambda first: pl.BlockSpec((SEQ, 128), lambda g: (0, first + g))
    acc = pltpu.VMEM((SEQ, 128), F32)
    big = (2 * N_UNITS, ATTN_BLOCK, 2 * ATTN_BLOCK)
    big_t = (2 * N_UNITS, 2 * ATTN_BLOCK, ATTN_BLOCK)
    return pl.pallas_call(
        body, out_shape=jax.ShapeDtypeStruct((3, SEQ, D_ATTN), BF16), grid=(4,),
        in_specs=[blk(8), blk(12), blk(16), blk(0), blk(0), blk(0)],
        out_specs=pl.BlockSpec((3, SEQ, 128), lambda g: (0, 0, g)),
        scratch_shapes=[acc, acc, acc, pltpu.VMEM(big, F32), pltpu.VMEM(big, F32), pltpu.VMEM(big, BF16),
                        pltpu.VMEM(big_t, F32), pltpu.VMEM(big_t, F32), pltpu.VMEM(big_t, BF16), pltpu.VMEM(big_t, BF16),
                        pltpu.VMEM((N_UNITS, ATTN_BLOCK, 128), BF16),
                        pltpu.VMEM((N_UNITS, 2 * ATTN_BLOCK, 128), BF16), pltpu.VMEM((N_UNITS, ATTN_BLOCK, 128), BF16)],
        name=name, compiler_params=_cp("parallel"))(proj, proj, proj, do, lse, dd)


def rms_gain_bf16(a, g, name):
    def body(a_ref, g_ref, o_ref):
        aa = a_ref[...]
        o_ref[...] = (aa * lax.rsqrt(_mean(aa * aa) + EPS) * g_ref[...]).astype(BF16)

    w = a.shape[1]
    return pl.pallas_call(
        body, out_shape=jax.ShapeDtypeStruct(a.shape, BF16), grid=(SEQ // ROWS,), in_specs=[_row_spec(w), _vec_spec(w)],
        out_specs=_row_spec(w), name=name, compiler_params=_cp("parallel"))(a, g)


def attn_combine_fwd(outs, lses, gao, name):
    def body(o1, o2, o3, l1, l2, l3, g_ref, att_ref, lse_ref, mix_ref):
        a1, a2, a3 = l1[...], l2[...], l3[...]
        m = jnp.maximum(jnp.maximum(a1, a2), a3)
        w1, w2, w3 = jnp.exp(a1 - m), jnp.exp(a2 - m), jnp.exp(a3 - m)
        den = w1 + w2 + w3
        att = (w1 * o1[...] + w2 * o2[...] + w3 * o3[...]) / den
        att_ref[...] = att
        lse_ref[...] = m + jnp.log(den)
        mix_ref[...] = (att * lax.rsqrt(_mean(att * att) + EPS) * g_ref[...]).astype(BF16)

    rs = _row_spec(D_ATTN)
    f = jax.ShapeDtypeStruct((SEQ, D_ATTN), F32)
    return pl.pallas_call(
        body, out_shape=(f, f, jax.ShapeDtypeStruct((SEQ, D_ATTN), BF16)), grid=(SEQ // ROWS,),
        in_specs=[rs] * 6 + [_vec_spec(D_ATTN)], out_specs=(rs, rs, rs),
        name=name, compiler_params=_cp("parallel"))(*outs, *lses, gao)


def attn_combine_bwd(dmixed, att, gao, name):
    def body(dm_ref, att_ref, g_ref, do_ref, dd_ref, dg_ref):
        first = pl.program_id(0) == 0
        att = att_ref[...]
        r = lax.rsqrt(_mean(att * att) + EPS)
        xn = att * r
        dm = dm_ref[...]
        _acc(dg_ref, _rsum(dm * xn), first)
        dyn = dm * g_ref[...]
        do = r * (dyn - xn * _mean(dyn * xn))
        do_ref[...] = do
        same_head = (jnp.right_shift(lax.broadcasted_iota(jnp.int32, (D_ATTN, D_ATTN), 0), 6)
                     == jnp.right_shift(lax.broadcasted_iota(jnp.int32, (D_ATTN, D_ATTN), 1), 6)).astype(F32)
        dd_ref[...] = jnp.dot(do * att, same_head, preferred_element_type=F32, precision=lax.Precision.HIGHEST)

    rs = _row_spec(D_ATTN)
    return pl.pallas_call(
        body,
        out_shape=(jax.ShapeDtypeStruct((SEQ, D_ATTN), F32), jax.ShapeDtypeStruct((SEQ, D_ATTN), F32),
                   jax.ShapeDtypeStruct((1, D_ATTN), F32)),
        grid=(SEQ // ROWS,), in_specs=[_row_spec(D_ATTN, 1), rs, _vec_spec(D_ATTN)],
        out_specs=(rs, rs, _vec_spec(D_ATTN)), name=name, compiler_params=_cp("arbitrary"))(dmixed, att, gao)


def sum3_bf16(a, b, c, name):
    def body(a_ref, b_ref, c_ref, o_ref):
        o_ref[...] = (a_ref[...] + b_ref[...] + c_ref[...]).astype(BF16)

    w = a.shape[1]
    rs = _row_spec(w)
    return pl.pallas_call(
        body, out_shape=jax.ShapeDtypeStruct(a.shape, BF16), grid=(SEQ // ROWS,), in_specs=[rs, rs, rs], out_specs=rs,
        name=name, compiler_params=_cp("parallel"))(a, b, c)


N_FT = D_FF // FFN_TN


def _ffn_specs():
    per = ROWS // FFN_HALO
    cur_g = pl.BlockSpec((ROWS, FFN_TN), lambda j, i: (i, j))
    cur_v = pl.BlockSpec((ROWS, FFN_TN), lambda j, i: (i, j + N_FT))
    halo_g = pl.BlockSpec((FFN_HALO, FFN_TN), lambda j, i: (jnp.maximum(i * per - 1, 0), j))
    halo_v = pl.BlockSpec((FFN_HALO, FFN_TN), lambda j, i: (jnp.maximum(i * per - 1, 0), j + N_FT))
    w_g = pl.BlockSpec((FFN_K, FFN_TN), lambda j, i: (0, j))
    w_v = pl.BlockSpec((FFN_K, FFN_TN), lambda j, i: (0, j + N_FT))
    b_g = pl.BlockSpec((1, FFN_TN), lambda j, i: (0, j))
    b_v = pl.BlockSpec((1, FFN_TN), lambda j, i: (0, j + N_FT))
    return [cur_g, cur_v, halo_g, halo_v, w_g, w_v, b_g, b_v]


def matmul(a, b, kind, out_dtype, tm, tn, name):
    if kind == "nn":
        (m, k), n = a.shape, b.shape[1]
        a_spec = pl.BlockSpec((tm, k), lambda j, i: (i, 0))
        b_spec = pl.BlockSpec((k, tn), lambda j, i: (0, j))
        dims = (((1,), (0,)), ((), ()))
    elif kind == "nt":
        (m, k), n = a.shape, b.shape[0]
        a_spec = pl.BlockSpec((tm, k), lambda j, i: (i, 0))
        b_spec = pl.BlockSpec((tn, k), lambda j, i: (j, 0))
        dims = (((1,), (1,)), ((), ()))
    else:
        (k, m), n = a.shape, b.shape[1]
        a_spec = pl.BlockSpec((k, tm), lambda j, i: (0, i))
        b_spec = pl.BlockSpec((k, tn), lambda j, i: (0, j))
        dims = (((0,), (0,)), ((), ()))
    assert m % tm == 0 and n % tn == 0, (name, m, n, tm, tn)

    def body(a_ref, b_ref, o_ref):
        o_ref[...] = lax.dot_general(a_ref[...], b_ref[...], dims, preferred_element_type=F32).astype(o_ref.dtype)

    return pl.pallas_call(
        body, out_shape=jax.ShapeDtypeStruct((m, n), out_dtype), grid=(n // tn, m // tm),
        in_specs=[a_spec, b_spec], out_specs=pl.BlockSpec((tm, tn), lambda j, i: (i, j)),
        name=name, compiler_params=_cp("parallel", "parallel"))(a, b)


def matmul_halves(a, b, tm, tn, name):
    _, m, k = a.shape
    n = b.shape[1]
    b3 = b.reshape(2, k, n)

    def body(a_ref, b_ref, o_ref):
        o_ref[...] = (jnp.dot(a_ref[0], b_ref[0], preferred_element_type=F32)
                      + jnp.dot(a_ref[1], b_ref[1], preferred_element_type=F32))

    return pl.pallas_call(
        body, out_shape=jax.ShapeDtypeStruct((m, n), F32), grid=(n // tn, m // tm),
        in_specs=[pl.BlockSpec((2, tm, k), lambda j, i: (0, i, 0)), pl.BlockSpec((2, k, tn), lambda j, i: (0, 0, j))],
        out_specs=pl.BlockSpec((tm, tn), lambda j, i: (i, j)), name=name, compiler_params=_cp("parallel", "parallel"))(a, b3)


def matmul_tn_halves(a, b, tm, name):
    _, k, m = a.shape
    n = b.shape[1]

    def body(a_ref, b_ref, o_ref):
        o_ref[0] = lax.dot_general(a_ref[0], b_ref[...], (((0,), (0,)), ((), ())), preferred_element_type=F32).astype(BF16)

    return pl.pallas_call(
        body, out_shape=jax.ShapeDtypeStruct((2, m, n), BF16), grid=(2, m // tm),
        in_specs=[pl.BlockSpec((1, k, tm), lambda h, i: (h, 0, i)), pl.BlockSpec((k, n), lambda h, i: (0, 0))],
        out_specs=pl.BlockSpec((1, tm, n), lambda h, i: (h, i, 0)), name=name,
        compiler_params=_cp("parallel", "parallel"))(a, b).reshape(2 * m, n)


def _row_spec(width, col=0):
    return pl.BlockSpec((ROWS, width), lambda i: (i, col))


def _vec_spec(width, rows=1):
    return pl.BlockSpec((rows, width), lambda i: (0, 0))


def _ffn_shifted(cur_ref, halo_ref, pad, s1, s2):
    i = pl.program_id(1)
    pad[0:FFN_HALO, :] = jnp.where(i > 0, halo_ref[...], 0.0)
    pad[FFN_HALO:, :] = cur_ref[0:FFN_HALO, :]
    for k, dst in ((1, s1), (2, s2)):
        dst[0:FFN_HALO, :] = pad[pl.ds(FFN_HALO - k, FFN_HALO), :]
        dst[FFN_HALO:, :] = cur_ref[pl.ds(FFN_HALO - k, ROWS - FFN_HALO), :]


def _ffn_conv(rs, cur_ref, s1, s2, w_ref, b_ref):
    return b_ref[...] + w_ref[0:1, :] * s2[rs, :] + w_ref[1:2, :] * s1[rs, :] + w_ref[2:3, :] * cur_ref[rs, :]


def ffn_act_fwd(up0, wf, bf, name):
    def body(g_ref, v_ref, gh_ref, vh_ref, wg_ref, wv_ref, bg_ref, bv_ref, act_ref, pad, g1, g2, v1, v2):
        _ffn_shifted(g_ref, gh_ref, pad, g1, g2)
        _ffn_shifted(v_ref, vh_ref, pad, v1, v2)

        def chunk(rs):
            gate = _ffn_conv(rs, g_ref, g1, g2, wg_ref, bg_ref)
            val = _ffn_conv(rs, v_ref, v1, v2, wv_ref, bv_ref)
            act_ref[rs, :] = (gate * _sig(gate) * val).astype(BF16)

        _for_chunks(chunk)

    tile = pltpu.VMEM((ROWS, FFN_TN), F32)
    return pl.pallas_call(
        body, out_shape=jax.ShapeDtypeStruct((SEQ, D_FF), BF16), grid=(N_FT, SEQ // ROWS),
        in_specs=_ffn_specs(), out_specs=pl.BlockSpec((ROWS, FFN_TN), lambda j, i: (i, j)),
        scratch_shapes=[pltpu.VMEM((2 * FFN_HALO, FFN_TN), F32), tile, tile, tile, tile],
        name=name, compiler_params=_cp("parallel", "parallel"))(up0, up0, up0, up0, wf, wf, bf, bf)


def ffn_bwd(up0, dact, wf, bf, name):
    per = ROWS // FFN_HALO
    last = SEQ // FFN_HALO - 1

    def body(g_ref, v_ref, gh_ref, vh_ref, wg_ref, wv_ref, bg_ref, bv_ref, da_ref, gn_ref, vn_ref, dan_ref,
             out_ref, dbg_ref, dbv_ref, dwg_ref, dwv_ref, pad, g1, g2, v1, v2, dgp, dvp, acc):
        i = pl.program_id(1)
        first = i == 0
        _ffn_shifted(g_ref, gh_ref, pad, g1, g2)
        _ffn_shifted(v_ref, vh_ref, pad, v1, v2)
        acc[...] = jnp.zeros_like(acc)

        def grads(gate, val, da):
            s = _sig(gate)
            return da * val * (s * (1.0 + gate * (1.0 - s))), da * (gate * s)

        def chunk(rs):
            gate = _ffn_conv(rs, g_ref, g1, g2, wg_ref, bg_ref)
            val = _ffn_conv(rs, v_ref, v1, v2, wv_ref, bv_ref)
            dgate, dval = grads(gate, val, da_ref[rs, :])
            dgp[rs, :] = dgate
            dvp[rs, :] = dval
            acc[0] += dgate
            acc[1] += dval
            for t, (sg, sv) in enumerate(((g2, v2), (g1, v1), (g_ref, v_ref))):
                acc[2 + t] += dgate * sg[rs, :]
                acc[5 + t] += dval * sv[rs, :]

        _for_chunks(chunk)
        _acc(dbg_ref, _rsum(acc[0]), first)
        _acc(dbv_ref, _rsum(acc[1]), first)
        _acc(dwg_ref, jnp.concatenate([_rsum(acc[2 + t]) for t in range(FFN_K)], axis=0), first)
        _acc(dwv_ref, jnp.concatenate([_rsum(acc[5 + t]) for t in range(FFN_K)], axis=0), first)

        def conv_next(cur_ref, nxt_ref, w_ref, b_ref):
            pad[0:FFN_HALO, :] = cur_ref[ROWS - FFN_HALO:, :]
            pad[FFN_HALO:, :] = nxt_ref[...]
            return (b_ref[...] + w_ref[0:1, :] * pad[pl.ds(FFN_HALO - 2, FFN_HALO), :]
                    + w_ref[1:2, :] * pad[pl.ds(FFN_HALO - 1, FFN_HALO), :] + w_ref[2:3, :] * nxt_ref[...])

        gate_n = conv_next(g_ref, gn_ref, wg_ref, bg_ref)
        val_n = conv_next(v_ref, vn_ref, wv_ref, bv_ref)
        dgate_n, dval_n = grads(gate_n, val_n, dan_ref[...])
        inside = i < SEQ // ROWS - 1
        dgp[ROWS:, :] = jnp.where(inside, dgate_n, 0.0)
        dvp[ROWS:, :] = jnp.where(inside, dval_n, 0.0)

        for half, (dp, s1, s2, w_ref) in enumerate(((dgp, g1, g2, wg_ref), (dvp, v1, v2, wv_ref))):
            s1[...] = dp[pl.ds(1, ROWS), :]
            s2[...] = dp[pl.ds(2, ROWS), :]

            def back(rs, dp=dp, s1=s1, s2=s2, w_ref=w_ref, half=half):
                out_ref[half, rs, :] = (w_ref[2:3, :] * dp[rs, :] + w_ref[1:2, :] * s1[rs, :]
                                        + w_ref[0:1, :] * s2[rs, :]).astype(BF16)

            _for_chunks(back)

    tile = pltpu.VMEM((ROWS, FFN_TN), F32)
    ext = pltpu.VMEM((ROWS + FFN_HALO, FFN_TN), F32)
    vec = jax.ShapeDtypeStruct((1, D_FF), F32)
    taps = jax.ShapeDtypeStruct((FFN_K, D_FF), F32)
    cur = pl.BlockSpec((ROWS, FFN_TN), lambda j, i: (i, j))
    nxt = lambda off: pl.BlockSpec((FFN_HALO, FFN_TN), lambda j, i: (jnp.minimum((i + 1) * per, last), j + off))
    vs = pl.BlockSpec((1, FFN_TN), lambda j, i: (0, j))
    ts = pl.BlockSpec((FFN_K, FFN_TN), lambda j, i: (0, j))
    return pl.pallas_call(
        body, out_shape=(jax.ShapeDtypeStruct((2, SEQ, D_FF), BF16), vec, vec, taps, taps), grid=(N_FT, SEQ // ROWS),
        in_specs=_ffn_specs() + [cur, nxt(0), nxt(N_FT), nxt(0)],
        out_specs=(pl.BlockSpec((2, ROWS, FFN_TN), lambda j, i: (0, i, j)), vs, vs, ts, ts),
        scratch_shapes=[pltpu.VMEM((2 * FFN_HALO, FFN_TN), F32), tile, tile, tile, tile, ext, ext,
                        pltpu.VMEM((2 + 2 * FFN_K, SUB, FFN_TN), F32)],
        name=name, compiler_params=_cp("parallel", "arbitrary"))(up0, up0, up0, up0, wf, wf, bf, bf, dact, up0, up0, dact)


def ada_fwd(c_all, w_ada, b_cols, name):
    def body(c_ref, w_ref, b_ref, o_ref):
        cc = c_ref[...]
        sc = (cc * _sig(cc)).astype(BF16)
        o_ref[...] = jnp.dot(sc, w_ref[...].astype(BF16), preferred_element_type=F32) + b_ref[...]

    return pl.pallas_call(body, out_shape=jax.ShapeDtypeStruct((N_DEV, w_ada.shape[1]), F32), name=name,
                          compiler_params=_cp())(c_all, w_ada, b_cols)


def _adam(w, g, m, v):
    m = ADAM_B1 * m + (1.0 - ADAM_B1) * g
    v = ADAM_B2 * v + (1.0 - ADAM_B2) * (g * g)
    m_hat = m / (1.0 - ADAM_B1 ** ADAM_STEP)
    v_hat = v / (1.0 - ADAM_B2 ** ADAM_STEP)
    delta = -ADAM_LR * (m_hat / (jnp.sqrt(v_hat) + ADAM_EPS) + ADAM_WD * w)
    return delta, m, v


def ada_bwd_adamw(c_all_t, dmod_cols, w, m, v, name):
    rows, cols = w.shape
    tr = 256

    def body(ct_ref, dm_ref, w_ref, m_ref, v_ref, g_ref, d_ref, nm_ref, nv_ref):
        def chunk(rs):
            ct = ct_ref[rs, :]
            sc = ct * _sig(ct)
            g = sc[:, 0:1] * dm_ref[0:1, :]
            for b in range(1, N_DEV):
                g = g + sc[:, b:b + 1] * dm_ref[b:b + 1, :]
            g_ref[rs, :] = g
            d_ref[rs, :], nm_ref[rs, :], nv_ref[rs, :] = _adam(w_ref[rs, :], g, m_ref[rs, :], v_ref[rs, :])

        _for_chunks(chunk, 2, tr)

    blk = pl.BlockSpec((tr, cols), lambda i: (i, 0))
    shp = jax.ShapeDtypeStruct((rows, cols), F32)
    return pl.pallas_call(
        body, out_shape=(shp, shp, shp, shp), grid=(rows // tr,),
        in_specs=[pl.BlockSpec((tr, N_DEV), lambda i: (i, 0)), pl.BlockSpec((N_DEV, cols), lambda i: (0, 0)), blk, blk, blk],
        out_specs=(blk, blk, blk, blk), name=name, compiler_params=_cp("parallel"))(c_all_t, dmod_cols, w, m, v)


def sum_adamw(parts, own, w, m, v, tr, name):
    n_parts, rows, cols = parts.shape

    def body(*refs):
        if own is None:
            p_ref, w_ref, m_ref, v_ref, g_ref, d_ref, nm_ref, nv_ref = refs
        else:
            p_ref, own_ref, w_ref, m_ref, v_ref, g_ref, d_ref, nm_ref, nv_ref = refs

        def chunk(rs):
            g = (p_ref[0, rs, :] if own is None else own_ref[rs, :]).astype(F32)
            for k in range(1, n_parts):
                g = g + p_ref[k, rs, :].astype(F32)
            g_ref[rs, :] = g
            d_ref[rs, :], nm_ref[rs, :], nv_ref[rs, :] = _adam(w_ref[rs, :], g, m_ref[rs, :], v_ref[rs, :])

        _for_chunks(chunk, 2 if tr > SUB else 1, tr)

    blk = pl.BlockSpec((tr, cols), lambda i: (i, 0))
    shp = jax.ShapeDtypeStruct((rows, cols), F32)
    args = [parts] + ([] if own is None else [own]) + [w, m, v]
    return pl.pallas_call(
        body, out_shape=(shp, shp, shp, shp), grid=(rows // tr,),
        in_specs=[pl.BlockSpec((n_parts, tr, cols), lambda i: (0, i, 0))] + [blk] * (len(args) - 1),
        out_specs=(blk, blk, blk, blk), name=name, compiler_params=_cp("parallel"))(*args)


MESH = pl.DeviceIdType.MESH
ANY = pl.BlockSpec(memory_space=pl.ANY)


def all_gather(block, name, after=None):
    extra = () if after is None else (after,)

    def body(x_ref, *refs):
        out_ref, send_sems, recv_sems, local_sem = refs[len(extra):]
        x, y, c = lax.axis_index("x"), lax.axis_index("y"), lax.axis_index("c")
        me, sibling = (x, y, c), (x, y, 1 - c)
        chips = [(1 - x, y), (x, 1 - y), (1 - x, 1 - y)]

        def slot(px, py, pc):
            return out_ref.at[4 * px + 2 * py + pc]

        def copy(k, blk, to, src=None):
            return pltpu.make_async_remote_copy(
                src_ref=slot(*blk) if src is None else src, dst_ref=slot(*blk),
                send_sem=send_sems.at[k], recv_sem=recv_sems.at[k], device_id=to, device_id_type=MESH)

        mine = pltpu.make_async_copy(x_ref, slot(*me), local_sem)
        mine.start()
        first = [copy(0, me, sibling, src=x_ref)]
        first += [copy(1 + j, me, (*chip, c), src=x_ref) for j, chip in enumerate(chips)]
        for cp in first:
            cp.start()
        passed = [copy(4 + j, (*chip, c), sibling) for j, chip in enumerate(chips)]
        for j, chip in enumerate(chips):
            copy(1 + j, (*chip, c), me).wait_recv()
            passed[j].start()
        copy(0, sibling, me).wait_recv()
        for j, chip in enumerate(chips):
            copy(4 + j, (*chip, 1 - c), me).wait_recv()
        for cp in first + passed:
            cp.wait_send()
        mine.wait()

    return pl.pallas_call(
        body, out_shape=jax.ShapeDtypeStruct((N_DEV,) + block.shape, block.dtype), in_specs=[ANY] * (1 + len(extra)), out_specs=ANY,
        scratch_shapes=[pltpu.SemaphoreType.DMA((7,)), pltpu.SemaphoreType.DMA((7,)), pltpu.SemaphoreType.DMA],
        name=name)(block, *extra)


HBM = pl.BlockSpec(memory_space=pltpu.HBM)
SEM = pl.BlockSpec(memory_space=pltpu.SEMAPHORE)
EFFECT = pltpu.SideEffectType.DATAFLOW_SIDE_EFFECTING


def _peer_copies(src_ref, land_ref, send_sems, recv_sems, gather):
    x, y, c = lax.axis_index("x"), lax.axis_index("y"), lax.axis_index("c")
    me = 4 * x + 2 * y + c
    copies = []
    for k in range(1, N_DEV):
        px = 1 - x if k & 4 else x
        py = 1 - y if k & 2 else y
        pc = 1 - c if k & 1 else c
        copies.append(pltpu.make_async_remote_copy(
            src_ref=src_ref if gather else src_ref.at[4 * px + 2 * py + pc],
            dst_ref=land_ref.at[me] if gather else land_ref.at[k],
            send_sem=send_sems.at[k - 1], recv_sem=recv_sems.at[k - 1], device_id=(px, py, pc), device_id_type=MESH))
    return copies


def exchange_start(srcs, gather, name, after=None):
    n = len(srcs)
    land_shapes = [(N_DEV,) + src.shape if gather else src.shape for src in srcs]
    extra = () if after is None else (after,)

    def body(*refs):
        src_refs, land_refs = refs[0:n], refs[n:2 * n]
        outs = refs[2 * n + len(extra):]
        for k in range(n):
            for cp in _peer_copies(src_refs[k], land_refs[k], outs[4 * k], outs[4 * k + 1], gather):
                cp.start()
        token = outs[4 * n]
        token[...] = jnp.zeros_like(token)

    out_shape, out_specs, aliases = [], [], {}
    for k, src in enumerate(srcs):
        out_shape += [pltpu.SemaphoreType.DMA((N_DEV - 1,)), pltpu.SemaphoreType.DMA((N_DEV - 1,)),
                      pltpu.HBM(src.shape, src.dtype), pltpu.HBM(land_shapes[k], src.dtype)]
        out_specs += [SEM, SEM, HBM, HBM]
        aliases[k] = 4 * k + 2
        aliases[n + k] = 4 * k + 3
    out_shape.append(jax.ShapeDtypeStruct((8, 128), F32))
    out_specs.append(pl.BlockSpec(memory_space=pltpu.VMEM))
    res = pl.pallas_call(
        body, name=name, out_shape=tuple(out_shape), in_specs=(HBM,) * (2 * n) + (ANY,) * len(extra),
        out_specs=tuple(out_specs), input_output_aliases=aliases,
        compiler_params=pltpu.CompilerParams(has_side_effects=EFFECT),
    )(*[pltpu.with_memory_space_constraint(src, pltpu.HBM) for src in srcs],
      *[pltpu.with_memory_space_constraint(lax.empty(shp, src.dtype), pltpu.HBM) for shp, src in zip(land_shapes, srcs)],
      *extra)
    return [tuple(res[4 * k:4 * k + 4]) for k in range(n)], res[4 * n][0, 0]


def exchange_wait(handles, after, gather, name):
    send_sems, recv_sems, src_thru, land_thru = handles

    def body(src_ref, land_ref, send_sems, recv_sems, after_ref, src_dead, got_ref):
        for cp in _peer_copies(src_ref, land_ref, send_sems, recv_sems, gather):
            cp.wait_send()
            cp.wait_recv()

    return pl.pallas_call(
        body, name=name,
        out_shape=(pltpu.HBM(src_thru.shape, src_thru.dtype), pltpu.HBM(land_thru.shape, land_thru.dtype)),
        in_specs=(HBM, HBM, SEM, SEM, ANY), out_specs=(HBM, HBM), input_output_aliases={0: 0, 1: 1},
        compiler_params=pltpu.CompilerParams(has_side_effects=EFFECT),
    )(src_thru, land_thru, send_sems, recv_sems, after)[1]


def _to_pattern(a, r):
    return a.reshape(SEQ // r, r * a.shape[1])


def local_step(x, tgt, mod, get_w, put_grad, wc, wf, g_mix, bc, lg, lb, gco, gao, g_ffn, bf, g_fin):
    h1 = rms_mod_fwd(x, g_mix, mod, 0, 1, "h1_fwd")
    w_in = get_w("w_in", h1)
    proj = matmul(h1, w_in, "nt", F32, 512, D_IN, "proj_fwd")
    mix_a = conv_module_fwd(proj, wc, bc, lg, lb, gco, "conv_module_fwd")
    att, lse = attn_fwd_all(proj, "attn_fwd")
    mix_b = rms_gain_bf16(att, gao, "attn_out_norm")
    mixed = jnp.concatenate([mix_a, mix_b], axis=1)
    w_out = get_w("w_out", mixed)
    y1 = matmul(mixed, w_out, "nn", F32, 512, D_MODEL, "out_proj_fwd")
    x1, h2 = resid_rms_mod_fwd(x, y1, g_ffn, mod, 2, 3, 4, "x1_h2_fwd")
    w_up = get_w("w_up", h2)
    up0 = matmul(h2, w_up, "nt", F32, 512, D_FF, "up_fwd")
    act = ffn_act_fwd(up0, wf, bf, "ffn_act_fwd")
    w_down = get_w("w_down", act)
    y2 = matmul(act, w_down, "nn", F32, 512, D_MODEL, "down_fwd")
    loss_t, dx2, dy2, d_gfin, d_gaf = final_loss_bwd(x1, y2, tgt, g_fin, mod, 5, "loss_bwd")
    dact = matmul(dy2, w_down, "nt", F32, 512, FFN_TN, "down_bwd_x")
    dw_down = matmul(act, dy2, "tn", BF16, 256, D_MODEL, "down_bwd_w")
    dup0, dbf_g, dbf_v, dwf_g, dwf_v = ffn_bwd(up0, dact, wf + put_grad("w_down", dw_down), bf, "ffn_bwd")
    dh2 = matmul_halves(dup0, w_up, 512, 512, "up_bwd_x")
    dw_up = matmul_tn_halves(dup0, h2, 256, "up_bwd_w")
    dx1, d_shf, d_scf, d_gffn, dy1, d_gam = rms_mod_bwd(x1, dh2, dx2, g_ffn + put_grad("w_up", dw_up), mod, 4, y1, 2, "h2_bwd")
    dmixed = matmul(dy1, w_out, "nt", F32, 512, D_MODEL, "out_proj_bwd_x")
    dw_out = matmul(mixed, dy1, "tn", BF16, 256, D_MODEL, "out_proj_bwd_w")
    do, dd, d_gao = attn_combine_bwd(dmixed, att, gao + put_grad("w_out", dw_out), "attn_combine_bwd")
    dqkv = attn_bwd_all(proj, do, lse, dd, "attn_bwd")
    du1, d_gco, d_lg, d_lb, d_bc, d_wc = conv_module_bwd_a(proj, dmixed, wc, bc, lg, lb, gco, "conv_module_bwd_a")
    dproj_a = conv_module_bwd_b(proj, du1, wc, "conv_module_bwd_b")
    dproj = jnp.concatenate([dproj_a, dqkv[0], dqkv[1], dqkv[2]], axis=1)
    dw_in = matmul(dproj, h1, "tn", BF16, 512, D_MODEL, "proj_bwd_w")
    dh1 = matmul(dproj, w_in, "nn", F32, 512, D_MODEL, "proj_bwd_x")
    dx, d_shm, d_scm, d_gmix = rms_mod_bwd(x, dh1, dx1, g_mix + put_grad("w_in", dw_in), mod, 1, None, 0, "h1_bwd")
    dmod = jnp.concatenate([d_shm, d_scm, d_gam, d_shf, d_scf, d_gaf], axis=1)
    small = dict(g_norm_mix=d_gmix, b_conv_dw=d_bc, ln_conv_g=d_lg, ln_conv_b=d_lb, g_conv_out=d_gco, g_attn_out=d_gao,
                 g_norm_ffn=d_gffn, b_ffn_dw=jnp.concatenate([dbf_g, dbf_v], axis=1), g_final=d_gfin,
                 w_conv_dw=d_wc, w_ffn_dw=jnp.concatenate([dwf_g, dwf_v], axis=1), dmod=dmod, loss=loss_t[0:1, 0:1])
    return dx, small


def _padw(a, width):
    return jnp.pad(a, ((0, 0), (0, width - a.shape[1])))


def pack_small(t):
    wide = jnp.concatenate([_padw(t["dmod"], PACK_W), _padw(t["b_ffn_dw"], PACK_W), _padw(t["w_ffn_dw"], PACK_W),
                            _padw(t["loss"], PACK_W), jnp.zeros((2, PACK_W), F32)], axis=0)
    z512 = jnp.zeros((1, 512), F32)
    narrow = jnp.concatenate([
        t["g_norm_mix"], t["g_norm_ffn"], t["g_final"],
        jnp.concatenate([t["b_conv_dw"], t["ln_conv_g"]], axis=1),
        jnp.concatenate([t["ln_conv_b"], t["g_conv_out"]], axis=1),
        jnp.concatenate([t["g_attn_out"], z512], axis=1),
        jnp.zeros((2, 1024), F32),
        jnp.pad(t["w_conv_dw"], ((0, 1), (0, 0))).reshape(16, 1024)], axis=0)
    return jnp.concatenate([wide, narrow.reshape(4, PACK_W), jnp.zeros((4, PACK_W), F32)], axis=0)


def unpack_small(p):
    narrow = p[8:12].reshape(24, 1024)
    return dict(
        dmod=p[0:1], b_ffn_dw=p[1:2, :2 * D_FF], w_ffn_dw=p[2:5, :2 * D_FF], loss=p[5, 0],
        g_norm_mix=narrow[0:1], g_norm_ffn=narrow[1:2], g_final=narrow[2:3],
        b_conv_dw=narrow[3:4, :512], ln_conv_g=narrow[3:4, 512:], ln_conv_b=narrow[4:5, :512], g_conv_out=narrow[4:5, 512:],
        g_attn_out=narrow[5:6, :512], w_conv_dw=narrow[8:24].reshape(32, 512)[:CONV_K])


def _embed(local, width, me):
    return lax.dynamic_update_slice(jnp.zeros((local.shape[0], width), F32), local, (0, me * local.shape[1]))


def _shard(full, n_cols, me):
    return lax.dynamic_slice(full, (0, me * n_cols), (full.shape[0], n_cols))


WEIGHTS = ["w_ada", "b_ada", "g_norm_mix", "w_in", "w_conv_dw", "b_conv_dw", "ln_conv_g", "ln_conv_b", "g_conv_out",
           "g_attn_out", "w_out", "g_norm_ffn", "w_up", "w_ffn_dw", "b_ffn_dw", "w_down", "g_final"]
SMALL_REPLICATED = ["g_norm_mix", "b_conv_dw", "ln_conv_g", "ln_conv_b", "g_conv_out", "g_attn_out", "g_norm_ffn",
                    "b_ffn_dw", "g_final"]


def kernel(x, c, w_ada, b_ada, g_norm_mix, w_in, w_conv_dw, b_conv_dw, ln_conv_g, ln_conv_b, g_conv_out, g_attn_out, w_out, g_norm_ffn, w_up, w_ffn_dw, b_ffn_dw, w_down, g_final, loss_target, m_w_ada, m_b_ada, m_g_norm_mix, m_w_in, m_w_conv_dw, m_b_conv_dw, m_ln_conv_g, m_ln_conv_b, m_g_conv_out, m_g_attn_out, m_w_out, m_g_norm_ffn, m_w_up, m_w_ffn_dw, m_b_ffn_dw, m_w_down, m_g_final, v_w_ada, v_b_ada, v_g_norm_mix, v_w_in, v_w_conv_dw, v_b_conv_dw, v_ln_conv_g, v_ln_conv_b, v_g_conv_out, v_g_attn_out, v_w_out, v_g_norm_ffn, v_w_up, v_w_ffn_dw, v_b_ffn_dw, v_w_down, v_g_final):
    args = dict(locals())
    me = 4 * lax.axis_index("x") + 2 * lax.axis_index("y") + lax.axis_index("c")

    def flat(name, prefix=""):
        a = args[prefix + name]
        return a.reshape(a.shape[-2] if a.ndim > 1 else 1, a.shape[-1])

    def flat_t(name, prefix=""):
        return args[prefix + name][0].T

    n_in, n_up, r_out, r_down = w_in.shape[2], w_up.shape[2], w_out.shape[1], w_down.shape[1]
    n_ada, n_wc, n_wf = w_ada.shape[2], w_conv_dw.shape[2], w_ffn_dw.shape[2]
    taps_c = jnp.pad(flat("w_conv_dw").reshape(1, CONV_K * n_wc), ((0, 0), (0, 2 * D_MODEL - CONV_K * n_wc)))
    taps_f = jnp.pad(flat("w_ffn_dw").reshape(1, FFN_K * n_wf), ((0, 0), (0, 3 * D_MODEL - FFN_K * n_wf)))
    first = jnp.concatenate([c, taps_c.reshape(2, D_MODEL), taps_f.reshape(3, D_MODEL), jnp.zeros((2, D_MODEL), F32)], axis=0)
    w_in_block = flat_t("w_in").astype(BF16)
    (gather_w_in,), tok = exchange_start([w_in_block], True, "gather_w_in_start")
    first_all = all_gather(first + tok, "gather_c_taps")
    c_all = first_all[:, 0, :]
    wc_full = first_all[:, 1:3, :].reshape(N_DEV, 2 * D_MODEL)[:, :CONV_K * n_wc].reshape(N_DEV, CONV_K, n_wc)
    wc_full = wc_full.transpose(1, 0, 2).reshape(CONV_K, D_CONV)
    wf_full = first_all[:, 3:6, :].reshape(N_DEV, 3 * D_MODEL)[:, :FFN_K * n_wf].reshape(N_DEV, FFN_K, n_wf)
    wf_full = wf_full.transpose(1, 0, 2).reshape(FFN_K, 2 * D_FF)
    mod_cols = ada_fwd(c_all, flat("w_ada"), _shard(flat("b_ada"), n_ada, me), "ada_fwd")
    mod_all = all_gather(mod_cols, "gather_mod")
    mod = lax.dynamic_index_in_dim(mod_all, me, axis=1, keepdims=False).reshape(N_MOD, D_MODEL)
    mod = jnp.pad(mod, ((0, 2), (0, 0)))

    order = ("w_out", "w_up", "w_down")
    blocks = dict(w_in=w_in_block, w_up=flat_t("w_up").astype(BF16), w_out=flat("w_out").astype(BF16),
                  w_down=flat("w_down").astype(BF16))
    handles, tok = exchange_start([blocks[name] for name in order], True, "gather_weights_start", mod_all)
    gathers = dict(zip(order, handles), w_in=gather_w_in)
    mod = mod + tok

    def gathered(name, after):
        land = exchange_wait(gathers[name], after, True, f"gather_{name}_wait")
        return lax.dynamic_update_index_in_dim(land, blocks[name], me, axis=0)

    def get_w(name, after):
        return gathered(name, after).reshape(-1, D_MODEL)

    exchanges, own = {}, {}

    def put_grad(name, dw, after=None):
        dev_major = dw.reshape(N_DEV, -1, D_MODEL)
        own[name] = lax.dynamic_index_in_dim(dev_major, me, axis=0, keepdims=False)
        (exchanges[name],), token = exchange_start([dev_major], False, f"exchange_{name}_start", after)
        return token

    grad_x, small = local_step(
        x[0], loss_target[0], mod, get_w, put_grad, wc_full, wf_full,
        flat("g_norm_mix"), flat("b_conv_dw"), flat("ln_conv_g"), flat("ln_conv_b"), flat("g_conv_out"),
        flat("g_attn_out"), flat("g_norm_ffn"), flat("b_ffn_dw"), flat("g_final"))

    out = {}

    def finish(name, tr, after):
        parts = exchange_wait(exchanges[name], after, False, f"exchange_{name}_wait")
        if name in ("w_in", "w_up"):
            res = sum_adamw(parts, own[name], flat_t(name), flat_t(name, "m_"), flat_t(name, "v_"), tr, "adamw_" + name)
            out[name] = tuple(r.T for r in res)
        else:
            res = out[name] = sum_adamw(parts, own[name], flat(name), flat(name, "m_"), flat(name, "v_"), tr, "adamw_" + name)
        return res[0]

    after = finish("w_down", r_down, grad_x)
    after = finish("w_up", n_up // 2, after)
    after = finish("w_out", r_out, after)
    after = finish("w_in", n_in, after)

    small_all = all_gather(pack_small(small), "gather_small", after)

    def packed(prefix):
        t = {n: flat(n, prefix) for n in SMALL_REPLICATED}
        t["dmod"] = flat("b_ada", prefix)
        t["loss"] = jnp.zeros((1, 1), F32)
        t["w_conv_dw"] = _embed(flat("w_conv_dw", prefix), D_CONV, me)
        t["w_ffn_dw"] = _embed(flat("w_ffn_dw", prefix), 2 * D_FF, me)
        return pack_small(t)

    res = sum_adamw(small_all, None, packed(""), packed("m_"), packed("v_"), PACK_ROWS, "adamw_small")
    res = [unpack_small(r) for r in res]
    loss = res[0]["loss"]
    for n in SMALL_REPLICATED:
        out[n] = tuple(r[n] for r in res)
    out["b_ada"] = tuple(r["dmod"] for r in res)
    out["w_conv_dw"] = tuple(_shard(r["w_conv_dw"], n_wc, me) for r in res)
    out["w_ffn_dw"] = tuple(_shard(r["w_ffn_dw"], n_wf, me) for r in res)

    dmod_cols = _shard(small_all[:, 0, :], n_ada, me)
    out["w_ada"] = ada_bwd_adamw(c_all.T, dmod_cols, flat("w_ada"), flat("w_ada", "m_"), flat("w_ada", "v_"), "adamw_w_ada")

    result = [loss, grad_x[None]]
    for k in range(4):
        result += [out[n][k].reshape(args[n].shape) for n in WEIGHTS]
    return tuple(result)
```

```python
import functools

import jax
import jax.numpy as jnp
from jax import lax
from jax.experimental import pallas as pl
from jax.experimental.pallas import tpu as pltpu

F32 = jnp.float32
BF16 = jnp.bfloat16

N_DEV = 8
SEQ = 2048
D_MODEL = 1024
D_CONV = 512
D_ATTN = 512
HEAD_DIM = 64
CONV_K = 31
D_FF = 2816
FFN_K = 3
D_IN = 2 * D_CONV + 3 * D_ATTN
N_MOD = 6
EPS = 1e-6
ATTN_BLOCK = 128
PATTERNS = ((2048, 1), (512, 4), (128, 16))
NEG = -1e30

ADAM_LR, ADAM_B1, ADAM_B2, ADAM_EPS, ADAM_WD, ADAM_STEP = 0.001, 0.9, 0.999, 1e-08, 0.01, 10

ROWS = 256
CONV_HALO = 32
FFN_HALO = 8
FFN_TN = 1408
VMEM_LIMIT = 56 * 1024 * 1024
PACK_ROWS, PACK_W = 16, 6144


def _cp(*sem):
    return pltpu.CompilerParams(dimension_semantics=sem if sem else None, vmem_limit_bytes=VMEM_LIMIT)


def _sig(x):
    return 1.0 / (1.0 + jnp.exp(-x))


def _rsum(x):
    return jnp.sum(x, axis=0, keepdims=True)


def _mean(x):
    return jnp.mean(x, axis=-1, keepdims=True)


def _acc(ref, val, first):
    @pl.when(first)
    def _():
        ref[...] = val

    @pl.when(jnp.logical_not(first))
    def _():
        ref[...] += val


SUB = 16


def _for_chunks(fn, unroll=1, rows=ROWS):
    def step(i, carry):
        fn(pl.ds(pl.multiple_of(i * SUB, SUB), SUB))
        return carry

    lax.fori_loop(0, rows // SUB, step, 0, unroll=unroll)


def rms_mod_fwd(x, g, mod, sh_row, sc_row, name):
    def body(x_ref, g_ref, mod_ref, h_ref):
        xx = x_ref[...]
        r = lax.rsqrt(_mean(xx * xx) + EPS)
        h = xx * r * g_ref[...]
        h_ref[...] = (h * (1.0 + mod_ref[sc_row:sc_row + 1, :]) + mod_ref[sh_row:sh_row + 1, :]).astype(BF16)

    return pl.pallas_call(
        body, out_shape=jax.ShapeDtypeStruct((SEQ, D_MODEL), BF16), grid=(SEQ // ROWS,),
        in_specs=[_row_spec(D_MODEL), _vec_spec(D_MODEL), _vec_spec(D_MODEL, 8)],
        out_specs=_row_spec(D_MODEL), name=name, compiler_params=_cp("parallel"))(x, g, mod)


def resid_rms_mod_fwd(x, y, g, mod, ga_row, sh_row, sc_row, name):
    def body(x_ref, y_ref, g_ref, mod_ref, x1_ref, h_ref):
        x1 = x_ref[...] + mod_ref[ga_row:ga_row + 1, :] * y_ref[...]
        x1_ref[...] = x1
        r = lax.rsqrt(_mean(x1 * x1) + EPS)
        h = x1 * r * g_ref[...]
        h_ref[...] = (h * (1.0 + mod_ref[sc_row:sc_row + 1, :]) + mod_ref[sh_row:sh_row + 1, :]).astype(BF16)

    return pl.pallas_call(
        body, out_shape=(jax.ShapeDtypeStruct((SEQ, D_MODEL), F32), jax.ShapeDtypeStruct((SEQ, D_MODEL), BF16)),
        grid=(SEQ // ROWS,),
        in_specs=[_row_spec(D_MODEL), _row_spec(D_MODEL), _vec_spec(D_MODEL), _vec_spec(D_MODEL, 8)],
        out_specs=(_row_spec(D_MODEL), _row_spec(D_MODEL)), name=name, compiler_params=_cp("parallel"))(x, y, g, mod)


def final_loss_bwd(x1, y2, tgt, g, mod, ga_row, name):
    def body(x1_ref, y2_ref, t_ref, g_ref, mod_ref, loss_ref, dx2_ref, dy2_ref, dg_ref, dga_ref):
        first = pl.program_id(0) == 0
        ga = mod_ref[ga_row:ga_row + 1, :]
        y2 = y2_ref[...]
        x2 = x1_ref[...] + ga * y2
        r = lax.rsqrt(_mean(x2 * x2) + EPS)
        xn = x2 * r
        err = xn * g_ref[...] - t_ref[...]
        _acc(loss_ref, jnp.broadcast_to(0.5 * jnp.sum(_mean(err * err)), (8, 128)), first)
        dy = err * (1.0 / D_MODEL)
        _acc(dg_ref, _rsum(dy * xn), first)
        dxn = dy * g_ref[...]
        dx2 = r * (dxn - xn * _mean(dxn * xn))
        dx2_ref[...] = dx2
        dy2_ref[...] = (dx2 * ga).astype(BF16)
        _acc(dga_ref, _rsum(dx2 * y2), first)

    vec = jax.ShapeDtypeStruct((1, D_MODEL), F32)
    return pl.pallas_call(
        body,
        out_shape=(jax.ShapeDtypeStruct((8, 128), F32), jax.ShapeDtypeStruct((SEQ, D_MODEL), F32),
                   jax.ShapeDtypeStruct((SEQ, D_MODEL), BF16), vec, vec),
        grid=(SEQ // ROWS,),
        in_specs=[_row_spec(D_MODEL), _row_spec(D_MODEL), _row_spec(D_MODEL), _vec_spec(D_MODEL), _vec_spec(D_MODEL, 8)],
        out_specs=(pl.BlockSpec((8, 128), lambda i: (0, 0)), _row_spec(D_MODEL), _row_spec(D_MODEL),
                   _vec_spec(D_MODEL), _vec_spec(D_MODEL)),
        name=name, compiler_params=_cp("arbitrary"))(x1, y2, tgt, g, mod)


def rms_mod_bwd(x, dh, dres, g, mod, sc_row, y, ga_row, name):
    gated = y is not None

    def body(*refs):
        if gated:
            x_ref, dh_ref, dres_ref, g_ref, mod_ref, y_ref, dx_ref, dsh_ref, dsc_ref, dg_ref, dy_ref, dga_ref = refs
        else:
            x_ref, dh_ref, dres_ref, g_ref, mod_ref, dx_ref, dsh_ref, dsc_ref, dg_ref = refs
        first = pl.program_id(0) == 0
        xx = x_ref[...]
        dh = dh_ref[...]
        gg = g_ref[...]
        r = lax.rsqrt(_mean(xx * xx) + EPS)
        xn = xx * r
        _acc(dsh_ref, _rsum(dh), first)
        _acc(dsc_ref, _rsum(dh * (xn * gg)), first)
        dt = dh * (1.0 + mod_ref[sc_row:sc_row + 1, :])
        _acc(dg_ref, _rsum(dt * xn), first)
        dxn = dt * gg
        dx = dres_ref[...] + r * (dxn - xn * _mean(dxn * xn))
        dx_ref[...] = dx
        if gated:
            _acc(dga_ref, _rsum(dx * y_ref[...]), first)
            dy_ref[...] = (dx * mod_ref[ga_row:ga_row + 1, :]).astype(BF16)

    vec = jax.ShapeDtypeStruct((1, D_MODEL), F32)
    in_specs = [_row_spec(D_MODEL), _row_spec(D_MODEL), _row_spec(D_MODEL), _vec_spec(D_MODEL), _vec_spec(D_MODEL, 8)]
    out_shape = [jax.ShapeDtypeStruct((SEQ, D_MODEL), F32), vec, vec, vec]
    out_specs = [_row_spec(D_MODEL), _vec_spec(D_MODEL), _vec_spec(D_MODEL), _vec_spec(D_MODEL)]
    args = [x, dh, dres, g, mod]
    if gated:
        in_specs.append(_row_spec(D_MODEL))
        out_shape += [jax.ShapeDtypeStruct((SEQ, D_MODEL), BF16), vec]
        out_specs += [_row_spec(D_MODEL), _vec_spec(D_MODEL)]
        args.append(y)
    return pl.pallas_call(
        body, out_shape=tuple(out_shape), grid=(SEQ // ROWS,), in_specs=in_specs, out_specs=tuple(out_specs),
        name=name, compiler_params=_cp("arbitrary"))(*args)


def _prev_halo(halo, width, col):
    per = ROWS // halo
    return pl.BlockSpec((halo, width), lambda i: (jnp.maximum(i * per - 1, 0), col))


def _next_halo(halo, width, col):
    per = ROWS // halo
    last = SEQ // halo - 1
    return pl.BlockSpec((halo, width), lambda i: (jnp.minimum((i + 1) * per, last), col))


CONV_PAD = ROWS + CONV_HALO


def _shift_copies(sh):
    for b in range(1, 8):
        sh[b, 0:CONV_PAD - 8, :] = sh[0, pl.ds(b, CONV_PAD - 8), :]


def _tap(sh, rs_start, offset):
    return sh[offset % 8, pl.ds(pl.multiple_of(rs_start + (offset // 8) * 8, 8), SUB), :]


def _conv_module_forward(av_ref, ag_ref, avh_ref, agh_ref, wc_ref, bc_ref, lg_ref, lb_ref, sh, u1_s):
    i = pl.program_id(0)
    hv = avh_ref[...] * _sig(agh_ref[...])
    sh[0, 0:CONV_HALO, :] = jnp.where(i > 0, hv, 0.0)

    def glu(rs):
        sh[0, pl.ds(pl.multiple_of(rs.start + CONV_HALO, SUB), SUB), :] = av_ref[rs, :] * _sig(ag_ref[rs, :])

    _for_chunks(glu)
    _shift_copies(sh)

    def conv(rs):
        u1 = jnp.broadcast_to(bc_ref[...], (SUB, D_CONV))
        for j in range(CONV_K):
            u1 = u1 + wc_ref[j:j + 1, :] * _tap(sh, rs.start, CONV_HALO - (CONV_K - 1) + j)
        u1_s[rs, :] = u1

    _for_chunks(conv)
    u1 = u1_s[...]
    mu = _mean(u1)
    cen = u1 - mu
    rs = lax.rsqrt(_mean(cen * cen) + EPS)
    z = cen * rs
    ln = z * lg_ref[...] + lb_ref[...]
    s = _sig(ln)
    return z, rs, ln, s, ln * s


_CONV_SCRATCH = [pltpu.VMEM((8, CONV_PAD, D_CONV), F32), pltpu.VMEM((ROWS, D_CONV), F32)]


def conv_module_fwd(proj, wc, bc, lg, lb, gco, name):
    def body(av_ref, ag_ref, avh_ref, agh_ref, wc_ref, bc_ref, lg_ref, lb_ref, gco_ref, out_ref, sh, u1_s):
        _, _, _, _, u2 = _conv_module_forward(av_ref, ag_ref, avh_ref, agh_ref, wc_ref, bc_ref, lg_ref, lb_ref, sh, u1_s)
        rc = lax.rsqrt(_mean(u2 * u2) + EPS)
        out_ref[...] = (u2 * rc * gco_ref[...]).astype(BF16)

    v = _vec_spec(D_CONV)
    return pl.pallas_call(
        body, out_shape=jax.ShapeDtypeStruct((SEQ, D_CONV), BF16), grid=(SEQ // ROWS,),
        in_specs=[_row_spec(D_CONV, 0), _row_spec(D_CONV, 1), _prev_halo(CONV_HALO, D_CONV, 0),
                  _prev_halo(CONV_HALO, D_CONV, 1), _vec_spec(D_CONV, CONV_K), v, v, v, v],
        out_specs=_row_spec(D_CONV), scratch_shapes=list(_CONV_SCRATCH),
        name=name, compiler_params=_cp("parallel"))(proj, proj, proj, proj, wc, bc, lg, lb, gco)


def conv_module_bwd_a(proj, dmixed, wc, bc, lg, lb, gco, name):
    def body(av_ref, ag_ref, avh_ref, agh_ref, dm_ref, wc_ref, bc_ref, lg_ref, lb_ref, gco_ref,
             du1_ref, dgco_ref, dlg_ref, dlb_ref, dbc_ref, dwc_ref, sh, u1_s, acc):
        first = pl.program_id(0) == 0
        z, rs, ln, s, u2 = _conv_module_forward(av_ref, ag_ref, avh_ref, agh_ref, wc_ref, bc_ref, lg_ref, lb_ref, sh, u1_s)
        rc = lax.rsqrt(_mean(u2 * u2) + EPS)
        xn = u2 * rc
        dm = dm_ref[...]
        _acc(dgco_ref, _rsum(dm * xn), first)
        dyn = dm * gco_ref[...]
        du2 = rc * (dyn - xn * _mean(dyn * xn))
        dln = du2 * (s * (1.0 + ln * (1.0 - s)))
        _acc(dlg_ref, _rsum(dln * z), first)
        _acc(dlb_ref, _rsum(dln), first)
        dz = dln * lg_ref[...]
        du1 = rs * (dz - _mean(dz) - z * _mean(dz * z))
        du1_ref[...] = du1
        _acc(dbc_ref, _rsum(du1), first)
        acc[...] = jnp.zeros_like(acc)

        def taps(rs):
            d = du1_ref[rs, :]
            for j in range(CONV_K):
                acc[j] += d * _tap(sh, rs.start, CONV_HALO - (CONV_K - 1) + j)

        _for_chunks(taps)

        @pl.when(first)
        def _():
            dwc_ref[...] = jnp.zeros_like(dwc_ref)

        for j in range(CONV_K):
            dwc_ref[j:j + 1, :] += _rsum(acc[j])

    v = _vec_spec(D_CONV)
    vec = jax.ShapeDtypeStruct((1, D_CONV), F32)
    return pl.pallas_call(
        body,
        out_shape=(jax.ShapeDtypeStruct((SEQ, D_CONV), F32), vec, vec, vec, vec, jax.ShapeDtypeStruct((CONV_K, D_CONV), F32)),
        grid=(SEQ // ROWS,),
        in_specs=[_row_spec(D_CONV, 0), _row_spec(D_CONV, 1), _prev_halo(CONV_HALO, D_CONV, 0),
                  _prev_halo(CONV_HALO, D_CONV, 1), _row_spec(D_CONV, 0), _vec_spec(D_CONV, CONV_K), v, v, v, v],
        out_specs=(_row_spec(D_CONV), v, v, v, v, _vec_spec(D_CONV, CONV_K)),
        scratch_shapes=list(_CONV_SCRATCH) + [pltpu.VMEM((CONV_K, SUB, D_CONV), F32)],
        name=name, compiler_params=_cp("arbitrary"))(proj, proj, proj, proj, dmixed, wc, bc, lg, lb, gco)


def conv_module_bwd_b(proj, du1, wc, name):
    def body(av_ref, ag_ref, du1_ref, du1n_ref, wc_ref, out_ref, sh):
        i = pl.program_id(0)
        sh[0, 0:ROWS, :] = du1_ref[...]
        sh[0, ROWS:, :] = jnp.where(i < SEQ // ROWS - 1, du1n_ref[...], 0.0)
        _shift_copies(sh)

        def chunk(rs):
            du0 = jnp.zeros((SUB, D_CONV), F32)
            for j in range(CONV_K):
                du0 = du0 + wc_ref[j:j + 1, :] * _tap(sh, rs.start, CONV_K - 1 - j)
            sg = _sig(ag_ref[rs, :])
            out_ref[rs, 0:D_CONV] = (du0 * sg).astype(BF16)
            out_ref[rs, D_CONV:] = (du0 * av_ref[rs, :] * sg * (1.0 - sg)).astype(BF16)

        _for_chunks(chunk)

    return pl.pallas_call(
        body, out_shape=jax.ShapeDtypeStruct((SEQ, 2 * D_CONV), BF16), grid=(SEQ // ROWS,),
        in_specs=[_row_spec(D_CONV, 0), _row_spec(D_CONV, 1), _row_spec(D_CONV, 0), _next_halo(CONV_HALO, D_CONV, 0),
                  _vec_spec(D_CONV, CONV_K)],
        out_specs=_row_spec(2 * D_CONV), scratch_shapes=[pltpu.VMEM((8, CONV_PAD, D_CONV), F32)],
        name=name, compiler_params=_cp("parallel"))(proj, proj, du1, du1, wc)


def _attn_specs(sub_len, pairs):
    ng, width = 4 // pairs, 128 * pairs
    q = pl.BlockSpec((sub_len, width), lambda rho, g: (0, rho * 3 * ng + g))
    k = pl.BlockSpec((sub_len, width), lambda rho, g: (0, rho * 3 * ng + ng + g))
    v = pl.BlockSpec((sub_len, width), lambda rho, g: (0, rho * 3 * ng + 2 * ng + g))
    o = pl.BlockSpec((sub_len, width), lambda rho, g: (0, rho * ng + g))
    return q, k, v, o


ATTN_PAIRS = {1: 1, 4: 1, 16: 4}


def _attn_block(q_ref, k_ref, v_ref, n, hs, win):
    q0 = pl.multiple_of(n * ATTN_BLOCK, ATTN_BLOCK)
    k0 = pl.multiple_of(jnp.maximum(n - 1, 0) * ATTN_BLOCK, ATTN_BLOCK)
    qb = q_ref[pl.ds(q0, ATTN_BLOCK), hs]
    kw = k_ref[pl.ds(k0, win), hs]
    vw = v_ref[pl.ds(k0, win), hs]
    s = lax.dot_general(qb, kw, (((1,), (1,)), ((), ())), preferred_element_type=F32) * (HEAD_DIM ** -0.5)
    dist = (q0 - k0) + lax.broadcasted_iota(jnp.int32, (ATTN_BLOCK, win), 0) \
        - lax.broadcasted_iota(jnp.int32, (ATTN_BLOCK, win), 1)
    s = jnp.where((dist >= 0) & (dist <= ATTN_BLOCK), s, NEG)
    return q0, k0, qb, kw, vw, s


def attn_fwd(qkv_r, sub_len, r, name):
    nb = sub_len // ATTN_BLOCK
    win = 2 * ATTN_BLOCK if nb > 1 else ATTN_BLOCK
    pairs = ATTN_PAIRS[r]

    def body(q_ref, k_ref, v_ref, o_ref, l_ref):
        def block(n, carry):
            for h in range(2 * pairs):
                hs = slice(h * HEAD_DIM, (h + 1) * HEAD_DIM)
                q0, _, _, _, vw, s = _attn_block(q_ref, k_ref, v_ref, n, hs, win)
                m = jnp.max(s, axis=1, keepdims=True)
                p = jnp.exp(s - m)
                den = jnp.sum(p, axis=1, keepdims=True)
                o = jnp.dot(p.astype(BF16), vw, preferred_element_type=F32) / den
                o_ref[pl.ds(q0, ATTN_BLOCK), hs] = o
                l_ref[pl.ds(q0, ATTN_BLOCK), hs] = jnp.broadcast_to(m + jnp.log(den), (ATTN_BLOCK, HEAD_DIM))
            return carry

        lax.fori_loop(0, nb, block, 0, unroll=min(nb, 2))

    q, k, v, o = _attn_specs(sub_len, pairs)
    shp = jax.ShapeDtypeStruct((sub_len, r * D_ATTN), F32)
    return pl.pallas_call(
        body, out_shape=(shp, shp), grid=(r, 4 // pairs), in_specs=[q, k, v], out_specs=(o, o),
        name=name, compiler_params=_cp("parallel", "parallel"))(qkv_r, qkv_r, qkv_r)


def attn_bwd(qkv_r, do_r, lse_r, dd_r, sub_len, r, name):
    nb = sub_len // ATTN_BLOCK
    win = 2 * ATTN_BLOCK if nb > 1 else ATTN_BLOCK
    pairs = ATTN_PAIRS[r]

    def body(q_ref, k_ref, v_ref, do_ref, l_ref, dd_ref, dq_ref, dk_ref, dv_ref):
        dk_ref[...] = jnp.zeros_like(dk_ref)
        dv_ref[...] = jnp.zeros_like(dv_ref)

        def block(n, carry):
            for h in range(2 * pairs):
                hs = slice(h * HEAD_DIM, (h + 1) * HEAD_DIM)
                h1 = slice(h * HEAD_DIM, h * HEAD_DIM + 1)
                q0, k0, qb, kw, vw, s = _attn_block(q_ref, k_ref, v_ref, n, hs, win)
                dob = do_ref[pl.ds(q0, ATTN_BLOCK), hs]
                p = jnp.exp(s - l_ref[pl.ds(q0, ATTN_BLOCK), h1])
                dp = lax.dot_general(dob, vw, (((1,), (1,)), ((), ())), preferred_element_type=F32)
                ds = (p * (dp - dd_ref[pl.ds(q0, ATTN_BLOCK), h1]) * (HEAD_DIM ** -0.5)).astype(BF16)
                dq_ref[pl.ds(q0, ATTN_BLOCK), hs] = jnp.dot(ds, kw, preferred_element_type=F32)
                dk_ref[pl.ds(k0, win), hs] += lax.dot_general(ds, qb, (((0,), (0,)), ((), ())), preferred_element_type=F32)
                dv_ref[pl.ds(k0, win), hs] += lax.dot_general(p.astype(BF16), dob, (((0,), (0,)), ((), ())),
                                                              preferred_element_type=F32)
            return carry

        lax.fori_loop(0, nb, block, 0)

    q, k, v, o = _attn_specs(sub_len, pairs)
    shp = jax.ShapeDtypeStruct((sub_len, r * D_ATTN), F32)
    return pl.pallas_call(
        body, out_shape=(shp, shp, shp), grid=(r, 4 // pairs), in_specs=[q, k, v, o, o, o], out_specs=(o, o, o),
        name=name, compiler_params=_cp("parallel", "parallel"))(qkv_r, qkv_r, qkv_r, do_r, lse_r, dd_r)


def _rows(start, size, r):
    return pl.ds(start, size) if r == 1 else pl.ds(start, size, stride=r)


def _unit_rows(r, rho, n, nb):
    win = 2 * ATTN_BLOCK if nb > 1 else ATTN_BLOCK
    if isinstance(n, int):
        kb = max(n - 1, 0)
        q_rows = _rows(rho + r * ATTN_BLOCK * n, ATTN_BLOCK, r)
        k_rows = _rows(rho + r * ATTN_BLOCK * kb, win, r)
    else:
        kb = jnp.maximum(n - 1, 0)
        q_rows = pl.ds(pl.multiple_of(n * ATTN_BLOCK, ATTN_BLOCK), ATTN_BLOCK)
        k_rows = pl.ds(pl.multiple_of(kb * ATTN_BLOCK, ATTN_BLOCK), win)
    dist = (n - kb) * ATTN_BLOCK + lax.broadcasted_iota(jnp.int32, (ATTN_BLOCK, win), 0) \
        - lax.broadcasted_iota(jnp.int32, (ATTN_BLOCK, win), 1)
    return q_rows, k_rows, (dist >= 0) & (dist <= ATTN_BLOCK)


def _per_head(x):
    lane = lax.broadcasted_iota(jnp.int32, x.shape, 1)
    zero = jnp.zeros_like(x)
    return [jnp.where(lane < HEAD_DIM, x, zero), jnp.where(lane >= HEAD_DIM, x, zero)]


NT = (((1,), (1,)), ((), ()))


def _masked_scores(q2, k2, valid):
    return [jnp.where(valid, lax.dot_general(qh, k2, NT, preferred_element_type=F32) * (HEAD_DIM ** -0.5), NEG)
            for qh in _per_head(q2)]


def _attn_units(r, nb, unit):
    if r == 1:
        def four(i, carry):
            for k in range(4):
                unit(0, 4 * i + k)
            return carry
        lax.fori_loop(0, nb // 4, four, 0)
    else:
        for rho in range(r):
            for n in range(nb):
                unit(rho, n)


N_UNITS = 16


def attn_fwd_all(proj, name):
    def body(q_ref, k_ref, v_ref, att_ref, lse_ref, s_scr, p_scr, lse_scr, den_scr):
        for idx, (sub_len, r) in enumerate(PATTERNS):
            nb = sub_len // ATTN_BLOCK
            win = 2 * ATTN_BLOCK if nb > 1 else ATTN_BLOCK

            def scores(rho, n, r=r, nb=nb, win=win):
                u = rho * nb + n
                q_rows, k_rows, valid = _unit_rows(r, rho, n, nb)
                ss = _masked_scores(q_ref[q_rows, :].astype(BF16), k_ref[k_rows, :].astype(BF16), valid)
                for h in range(2):
                    s_scr[2 * u + h, :, 0:win] = ss[h]

            _attn_units(r, nb, scores)

            def softmax(u, carry, win=win):
                lses, dens = [], []
                for h in range(2):
                    sc = s_scr[2 * u + h, :, 0:win]
                    m = jnp.max(sc, axis=1, keepdims=True)
                    p = jnp.exp(sc - m)
                    den = jnp.sum(p, axis=1, keepdims=True)
                    p_scr[2 * u + h, :, 0:win] = p.astype(BF16)
                    lses.append(jnp.broadcast_to(m + jnp.log(den), (ATTN_BLOCK, HEAD_DIM)))
                    dens.append(jnp.broadcast_to(den, (ATTN_BLOCK, HEAD_DIM)))
                lse_scr[u] = jnp.concatenate(lses, axis=1)
                den_scr[u] = jnp.concatenate(dens, axis=1)
                return carry

            lax.fori_loop(0, N_UNITS, softmax, 0, unroll=2)

            def outputs(rho, n, r=r, nb=nb, win=win, idx=idx):
                u = rho * nb + n
                q_rows, k_rows, _ = _unit_rows(r, rho, n, nb)
                vs = _per_head(v_ref[k_rows, :].astype(BF16))
                o = (jnp.dot(p_scr[2 * u, :, 0:win], vs[0], preferred_element_type=F32)
                     + jnp.dot(p_scr[2 * u + 1, :, 0:win], vs[1], preferred_element_type=F32)) / den_scr[u]
                lse = lse_scr[u]
                if idx > 0:
                    old = lse_ref[q_rows, :]
                    top = jnp.maximum(old, lse)
                    new = top + jnp.log(jnp.exp(old - top) + jnp.exp(lse - top))
                    o = att_ref[q_rows, :] * jnp.exp(old - new) + o * jnp.exp(lse - new)
                    lse = new
                att_ref[q_rows, :] = o
                lse_ref[q_rows, :] = lse

            _attn_units(r, nb, outputs)

    blk = lambda first: pl.BlockSpec((SEQ, 128), lambda g: (0, first + g))
    shp = jax.ShapeDtypeStruct((SEQ, D_ATTN), F32)
    big = (2 * N_UNITS, ATTN_BLOCK, 2 * ATTN_BLOCK)
    small = pltpu.VMEM((N_UNITS, ATTN_BLOCK, 128), F32)
    return pl.pallas_call(
        body, out_shape=(shp, shp), grid=(4,), in_specs=[blk(8), blk(12), blk(16)], out_specs=(blk(0), blk(0)),
        scratch_shapes=[pltpu.VMEM(big, F32), pltpu.VMEM(big, BF16), small, small],
        name=name, compiler_params=_cp("parallel"))(proj, proj, proj)


def attn_bwd_all(proj, do, lse, dd, name):
    scale = HEAD_DIM ** -0.5

    def body(q_ref, k_ref, v_ref, do_ref, l_ref, dd_ref, out_ref, dq_s, dk_s, dv_s,
             s_scr, dp_scr, ds_scr, st_scr, dpt_scr, pt_scr, dst_scr, qb_scr, kb_scr, dob_scr):
        dq_s[...] = jnp.zeros_like(dq_s)
        dk_s[...] = jnp.zeros_like(dk_s)
        dv_s[...] = jnp.zeros_like(dv_s)
        for sub_len, r in PATTERNS:
            nb = sub_len // ATTN_BLOCK
            win = 2 * ATTN_BLOCK if nb > 1 else ATTN_BLOCK

            def scores(rho, n, r=r, nb=nb, win=win):
                u = rho * nb + n
                q_rows, k_rows, valid = _unit_rows(r, rho, n, nb)
                kb = max(n - 1, 0) if isinstance(n, int) else jnp.maximum(n - 1, 0)
                dist_t = (n - kb) * ATTN_BLOCK + lax.broadcasted_iota(jnp.int32, (win, ATTN_BLOCK), 1) \
                    - lax.broadcasted_iota(jnp.int32, (win, ATTN_BLOCK), 0)
                valid_t = (dist_t >= 0) & (dist_t <= ATTN_BLOCK)
                q2 = q_ref[q_rows, :].astype(BF16)
                k2 = k_ref[k_rows, :].astype(BF16)
                do2 = do_ref[q_rows, :].astype(BF16)
                qb_scr[u] = q2
                kb_scr[u, 0:win, :] = k2
                dob_scr[u] = do2
                l2 = l_ref[q_rows, :]
                d2 = dd_ref[q_rows, :]
                l2t = l2.T
                d2t = d2.T
                v2 = v_ref[k_rows, :].astype(BF16)
                qs, dos = _per_head(q2), _per_head(do2)
                for h in range(2):
                    c0 = h * HEAD_DIM
                    sc = lax.dot_general(qs[h], k2, NT, preferred_element_type=F32) * scale
                    s_scr[2 * u + h, :, 0:win] = jnp.where(valid, sc, NEG) - l2[:, c0:c0 + 1]
                    dp_scr[2 * u + h, :, 0:win] = lax.dot_general(dos[h], v2, NT, preferred_element_type=F32) \
                        - d2[:, c0:c0 + 1]
                    sct = lax.dot_general(k2, qs[h], NT, preferred_element_type=F32) * scale
                    st_scr[2 * u + h, 0:win, :] = jnp.where(valid_t, sct, NEG) - l2t[c0:c0 + 1, :]
                    dpt_scr[2 * u + h, 0:win, :] = lax.dot_general(v2, dos[h], NT, preferred_element_type=F32) \
                        - d2t[c0:c0 + 1, :]

            _attn_units(r, nb, scores)

            def pointwise(hu, carry, win=win):
                ds_scr[hu, :, 0:win] = (jnp.exp(s_scr[hu, :, 0:win]) * dp_scr[hu, :, 0:win] * scale).astype(BF16)
                pt = jnp.exp(st_scr[hu, 0:win, :])
                pt_scr[hu, 0:win, :] = pt.astype(BF16)
                dst_scr[hu, 0:win, :] = (pt * dpt_scr[hu, 0:win, :] * scale).astype(BF16)
                return carry

            lax.fori_loop(0, 2 * N_UNITS, pointwise, 0, unroll=4)

            def grads(rho, n, r=r, nb=nb, win=win):
                u = rho * nb + n
                q_rows, k_rows, _ = _unit_rows(r, rho, n, nb)
                qs, ks, dos = _per_head(qb_scr[u]), _per_head(kb_scr[u, 0:win, :]), _per_head(dob_scr[u])

                def both(scr, rows, rhs):
                    return (jnp.dot(scr[(2 * u,) + rows], rhs[0], preferred_element_type=F32)
                            + jnp.dot(scr[(2 * u + 1,) + rows], rhs[1], preferred_element_type=F32))

                dq_s[q_rows, :] += both(ds_scr, (slice(None), slice(0, win)), ks)
                dk_s[k_rows, :] += both(dst_scr, (slice(0, win), slice(None)), qs)
                dv_s[k_rows, :] += both(pt_scr, (slice(0, win), slice(None)), dos)

            _attn_units(r, nb, grads)
        out_ref[0] = dq_s[...].astype(BF16)
        out_ref[1] = dk_s[...].astype(BF16)
        out_ref[2] = dv_s[...].astype(BF16)

    blk = lambda first: pl.BlockSpec((SEQ, 128), lambda g: (0, first + g))
    acc = pltpu.VMEM((SEQ, 128), F32)
    big = (2 * N_UNITS, ATTN_BLOCK, 2 * ATTN_BLOCK)
    big_t = (2 * N_UNITS, 2 * ATTN_BLOCK, ATTN_BLOCK)
    return pl.pallas_call(
        body, out_shape=jax.ShapeDtypeStruct((3, SEQ, D_ATTN), BF16), grid=(4,),
        in_specs=[blk(8), blk(12), blk(16), blk(0), blk(0), blk(0)],
        out_specs=pl.BlockSpec((3, SEQ, 128), lambda g: (0, 0, g)),
        scratch_shapes=[acc, acc, acc, pltpu.VMEM(big, F32), pltpu.VMEM(big, F32), pltpu.VMEM(big, BF16),
                        pltpu.VMEM(big_t, F32), pltpu.VMEM(big_t, F32), pltpu.VMEM(big_t, BF16), pltpu.VMEM(big_t, BF16),
                        pltpu.VMEM((N_UNITS, ATTN_BLOCK, 128), BF16),
                        pltpu.VMEM((N_UNITS, 2 * ATTN_BLOCK, 128), BF16), pltpu.VMEM((N_UNITS, ATTN_BLOCK, 128), BF16)],
        name=name, compiler_params=_cp("parallel"))(proj, proj, proj, do, lse, dd)


def rms_gain_bf16(a, g, name):
    def body(a_ref, g_ref, o_ref):
        aa = a_ref[...]
        o_ref[...] = (aa * lax.rsqrt(_mean(aa * aa) + EPS) * g_ref[...]).astype(BF16)

    w = a.shape[1]
    return pl.pallas_call(
        body, out_shape=jax.ShapeDtypeStruct(a.shape, BF16), grid=(SEQ // ROWS,), in_specs=[_row_spec(w), _vec_spec(w)],
        out_specs=_row_spec(w), name=name, compiler_params=_cp("parallel"))(a, g)


def attn_combine_fwd(outs, lses, gao, name):
    def body(o1, o2, o3, l1, l2, l3, g_ref, att_ref, lse_ref, mix_ref):
        a1, a2, a3 = l1[...], l2[...], l3[...]
        m = jnp.maximum(jnp.maximum(a1, a2), a3)
        w1, w2, w3 = jnp.exp(a1 - m), jnp.exp(a2 - m), jnp.exp(a3 - m)
        den = w1 + w2 + w3
        att = (w1 * o1[...] + w2 * o2[...] + w3 * o3[...]) / den
        att_ref[...] = att
        lse_ref[...] = m + jnp.log(den)
        mix_ref[...] = (att * lax.rsqrt(_mean(att * att) + EPS) * g_ref[...]).astype(BF16)

    rs = _row_spec(D_ATTN)
    f = jax.ShapeDtypeStruct((SEQ, D_ATTN), F32)
    return pl.pallas_call(
        body, out_shape=(f, f, jax.ShapeDtypeStruct((SEQ, D_ATTN), BF16)), grid=(SEQ // ROWS,),
        in_specs=[rs] * 6 + [_vec_spec(D_ATTN)], out_specs=(rs, rs, rs),
        name=name, compiler_params=_cp("parallel"))(*outs, *lses, gao)


def attn_combine_bwd(dmixed, att, gao, name):
    def body(dm_ref, att_ref, g_ref, do_ref, dd_ref, dg_ref):
        first = pl.program_id(0) == 0
        att = att_ref[...]
        r = lax.rsqrt(_mean(att * att) + EPS)
        xn = att * r
        dm = dm_ref[...]
        _acc(dg_ref, _rsum(dm * xn), first)
        dyn = dm * g_ref[...]
        do = r * (dyn - xn * _mean(dyn * xn))
        do_ref[...] = do
        same_head = (jnp.right_shift(lax.broadcasted_iota(jnp.int32, (D_ATTN, D_ATTN), 0), 6)
                     == jnp.right_shift(lax.broadcasted_iota(jnp.int32, (D_ATTN, D_ATTN), 1), 6)).astype(F32)
        dd_ref[...] = jnp.dot(do * att, same_head, preferred_element_type=F32, precision=lax.Precision.HIGHEST)

    rs = _row_spec(D_ATTN)
    return pl.pallas_call(
        body,
        out_shape=(jax.ShapeDtypeStruct((SEQ, D_ATTN), F32), jax.ShapeDtypeStruct((SEQ, D_ATTN), F32),
                   jax.ShapeDtypeStruct((1, D_ATTN), F32)),
        grid=(SEQ // ROWS,), in_specs=[_row_spec(D_ATTN, 1), rs, _vec_spec(D_ATTN)],
        out_specs=(rs, rs, _vec_spec(D_ATTN)), name=name, compiler_params=_cp("arbitrary"))(dmixed, att, gao)


def sum3_bf16(a, b, c, name):
    def body(a_ref, b_ref, c_ref, o_ref):
        o_ref[...] = (a_ref[...] + b_ref[...] + c_ref[...]).astype(BF16)

    w = a.shape[1]
    rs = _row_spec(w)
    return pl.pallas_call(
        body, out_shape=jax.ShapeDtypeStruct(a.shape, BF16), grid=(SEQ // ROWS,), in_specs=[rs, rs, rs], out_specs=rs,
        name=name, compiler_params=_cp("parallel"))(a, b, c)


N_FT = D_FF // FFN_TN


def _ffn_specs():
    per = ROWS // FFN_HALO
    cur_g = pl.BlockSpec((ROWS, FFN_TN), lambda j, i: (i, j))
    cur_v = pl.BlockSpec((ROWS, FFN_TN), lambda j, i: (i, j + N_FT))
    halo_g = pl.BlockSpec((FFN_HALO, FFN_TN), lambda j, i: (jnp.maximum(i * per - 1, 0), j))
    halo_v = pl.BlockSpec((FFN_HALO, FFN_TN), lambda j, i: (jnp.maximum(i * per - 1, 0), j + N_FT))
    w_g = pl.BlockSpec((FFN_K, FFN_TN), lambda j, i: (0, j))
    w_v = pl.BlockSpec((FFN_K, FFN_TN), lambda j, i: (0, j + N_FT))
    b_g = pl.BlockSpec((1, FFN_TN), lambda j, i: (0, j))
    b_v = pl.BlockSpec((1, FFN_TN), lambda j, i: (0, j + N_FT))
    return [cur_g, cur_v, halo_g, halo_v, w_g, w_v, b_g, b_v]


def matmul(a, b, kind, out_dtype, tm, tn, name):
    if kind == "nn":
        (m, k), n = a.shape, b.shape[1]
        a_spec = pl.BlockSpec((tm, k), lambda j, i: (i, 0))
        b_spec = pl.BlockSpec((k, tn), lambda j, i: (0, j))
        dims = (((1,), (0,)), ((), ()))
    elif kind == "nt":
        (m, k), n = a.shape, b.shape[0]
        a_spec = pl.BlockSpec((tm, k), lambda j, i: (i, 0))
        b_spec = pl.BlockSpec((tn, k), lambda j, i: (j, 0))
        dims = (((1,), (1,)), ((), ()))
    else:
        (k, m), n = a.shape, b.shape[1]
        a_spec = pl.BlockSpec((k, tm), lambda j, i: (0, i))
        b_spec = pl.BlockSpec((k, tn), lambda j, i: (0, j))
        dims = (((0,), (0,)), ((), ()))
    assert m % tm == 0 and n % tn == 0, (name, m, n, tm, tn)

    def body(a_ref, b_ref, o_ref):
        o_ref[...] = lax.dot_general(a_ref[...], b_ref[...], dims, preferred_element_type=F32).astype(o_ref.dtype)

    return pl.pallas_call(
        body, out_shape=jax.ShapeDtypeStruct((m, n), out_dtype), grid=(n // tn, m // tm),
        in_specs=[a_spec, b_spec], out_specs=pl.BlockSpec((tm, tn), lambda j, i: (i, j)),
        name=name, compiler_params=_cp("parallel", "parallel"))(a, b)


def matmul_halves(a, b, tm, tn, name):
    _, m, k = a.shape
    n = b.shape[1]
    b3 = b.reshape(2, k, n)

    def body(a_ref, b_ref, o_ref):
        o_ref[...] = (jnp.dot(a_ref[0], b_ref[0], preferred_element_type=F32)
                      + jnp.dot(a_ref[1], b_ref[1], preferred_element_type=F32))

    return pl.pallas_call(
        body, out_shape=jax.ShapeDtypeStruct((m, n), F32), grid=(n // tn, m // tm),
        in_specs=[pl.BlockSpec((2, tm, k), lambda j, i: (0, i, 0)), pl.BlockSpec((2, k, tn), lambda j, i: (0, 0, j))],
        out_specs=pl.BlockSpec((tm, tn), lambda j, i: (i, j)), name=name, compiler_params=_cp("parallel", "parallel"))(a, b3)


def matmul_tn_halves(a, b, tm, name):
    _, k, m = a.shape
    n = b.shape[1]

    def body(a_ref, b_ref, o_ref):
        o_ref[0] = lax.dot_general(a_ref[0], b_ref[...], (((0,), (0,)), ((), ())), preferred_element_type=F32).astype(BF16)

    return pl.pallas_call(
        body, out_shape=jax.ShapeDtypeStruct((2, m, n), BF16), grid=(2, m // tm),
        in_specs=[pl.BlockSpec((1, k, tm), lambda h, i: (h, 0, i)), pl.BlockSpec((k, n), lambda h, i: (0, 0))],
        out_specs=pl.BlockSpec((1, tm, n), lambda h, i: (h, i, 0)), name=name,
        compiler_params=_cp("parallel", "parallel"))(a, b).reshape(2 * m, n)


def _row_spec(width, col=0):
    return pl.BlockSpec((ROWS, width), lambda i: (i, col))


def _vec_spec(width, rows=1):
    return pl.BlockSpec((rows, width), lambda i: (0, 0))


def _ffn_shifted(cur_ref, halo_ref, pad, s1, s2):
    i = pl.program_id(1)
    pad[0:FFN_HALO, :] = jnp.where(i > 0, halo_ref[...], 0.0)
    pad[FFN_HALO:, :] = cur_ref[0:FFN_HALO, :]
    for k, dst in ((1, s1), (2, s2)):
        dst[0:FFN_HALO, :] = pad[pl.ds(FFN_HALO - k, FFN_HALO), :]
        dst[FFN_HALO:, :] = cur_ref[pl.ds(FFN_HALO - k, ROWS - FFN_HALO), :]


def _ffn_conv(rs, cur_ref, s1, s2, w_ref, b_ref):
    return b_ref[...] + w_ref[0:1, :] * s2[rs, :] + w_ref[1:2, :] * s1[rs, :] + w_ref[2:3, :] * cur_ref[rs, :]


def ffn_act_fwd(up0, wf, bf, name):
    def body(g_ref, v_ref, gh_ref, vh_ref, wg_ref, wv_ref, bg_ref, bv_ref, act_ref, pad, g1, g2, v1, v2):
        _ffn_shifted(g_ref, gh_ref, pad, g1, g2)
        _ffn_shifted(v_ref, vh_ref, pad, v1, v2)

        def chunk(rs):
            gate = _ffn_conv(rs, g_ref, g1, g2, wg_ref, bg_ref)
            val = _ffn_conv(rs, v_ref, v1, v2, wv_ref, bv_ref)
            act_ref[rs, :] = (gate * _sig(gate) * val).astype(BF16)

        _for_chunks(chunk)

    tile = pltpu.VMEM((ROWS, FFN_TN), F32)
    return pl.pallas_call(
        body, out_shape=jax.ShapeDtypeStruct((SEQ, D_FF), BF16), grid=(N_FT, SEQ // ROWS),
        in_specs=_ffn_specs(), out_specs=pl.BlockSpec((ROWS, FFN_TN), lambda j, i: (i, j)),
        scratch_shapes=[pltpu.VMEM((2 * FFN_HALO, FFN_TN), F32), tile, tile, tile, tile],
        name=name, compiler_params=_cp("parallel", "parallel"))(up0, up0, up0, up0, wf, wf, bf, bf)


def ffn_bwd(up0, dact, wf, bf, name):
    per = ROWS // FFN_HALO
    last = SEQ // FFN_HALO - 1

    def body(g_ref, v_ref, gh_ref, vh_ref, wg_ref, wv_ref, bg_ref, bv_ref, da_ref, gn_ref, vn_ref, dan_ref,
             out_ref, dbg_ref, dbv_ref, dwg_ref, dwv_ref, pad, g1, g2, v1, v2, dgp, dvp, acc):
        i = pl.program_id(1)
        first = i == 0
        _ffn_shifted(g_ref, gh_ref, pad, g1, g2)
        _ffn_shifted(v_ref, vh_ref, pad, v1, v2)
        acc[...] = jnp.zeros_like(acc)

        def grads(gate, val, da):
            s = _sig(gate)
            return da * val * (s * (1.0 + gate * (1.0 - s))), da * (gate * s)

        def chunk(rs):
            gate = _ffn_conv(rs, g_ref, g1, g2, wg_ref, bg_ref)
            val = _ffn_conv(rs, v_ref, v1, v2, wv_ref, bv_ref)
            dgate, dval = grads(gate, val, da_ref[rs, :])
            dgp[rs, :] = dgate
            dvp[rs, :] = dval
            acc[0] += dgate
            acc[1] += dval
            for t, (sg, sv) in enumerate(((g2, v2), (g1, v1), (g_ref, v_ref))):
                acc[2 + t] += dgate * sg[rs, :]
                acc[5 + t] += dval * sv[rs, :]

        _for_chunks(chunk)
        _acc(dbg_ref, _rsum(acc[0]), first)
        _acc(dbv_ref, _rsum(acc[1]), first)
        _acc(dwg_ref, jnp.concatenate([_rsum(acc[2 + t]) for t in range(FFN_K)], axis=0), first)
        _acc(dwv_ref, jnp.concatenate([_rsum(acc[5 + t]) for t in range(FFN_K)], axis=0), first)

        def conv_next(cur_ref, nxt_ref, w_ref, b_ref):
            pad[0:FFN_HALO, :] = cur_ref[ROWS - FFN_HALO:, :]
            pad[FFN_HALO:, :] = nxt_ref[...]
            return (b_ref[...] + w_ref[0:1, :] * pad[pl.ds(FFN_HALO - 2, FFN_HALO), :]
                    + w_ref[1:2, :] * pad[pl.ds(FFN_HALO - 1, FFN_HALO), :] + w_ref[2:3, :] * nxt_ref[...])

        gate_n = conv_next(g_ref, gn_ref, wg_ref, bg_ref)
        val_n = conv_next(v_ref, vn_ref, wv_ref, bv_ref)
        dgate_n, dval_n = grads(gate_n, val_n, dan_ref[...])
        inside = i < SEQ // ROWS - 1
        dgp[ROWS:, :] = jnp.where(inside, dgate_n, 0.0)
        dvp[ROWS:, :] = jnp.where(inside, dval_n, 0.0)

        for half, (dp, s1, s2, w_ref) in enumerate(((dgp, g1, g2, wg_ref), (dvp, v1, v2, wv_ref))):
            s1[...] = dp[pl.ds(1, ROWS), :]
            s2[...] = dp[pl.ds(2, ROWS), :]

            def back(rs, dp=dp, s1=s1, s2=s2, w_ref=w_ref, half=half):
                out_ref[half, rs, :] = (w_ref[2:3, :] * dp[rs, :] + w_ref[1:2, :] * s1[rs, :]
                                        + w_ref[0:1, :] * s2[rs, :]).astype(BF16)

            _for_chunks(back)

    tile = pltpu.VMEM((ROWS, FFN_TN), F32)
    ext = pltpu.VMEM((ROWS + FFN_HALO, FFN_TN), F32)
    vec = jax.ShapeDtypeStruct((1, D_FF), F32)
    taps = jax.ShapeDtypeStruct((FFN_K, D_FF), F32)
    cur = pl.BlockSpec((ROWS, FFN_TN), lambda j, i: (i, j))
    nxt = lambda off: pl.BlockSpec((FFN_HALO, FFN_TN), lambda j, i: (jnp.minimum((i + 1) * per, last), j + off))
    vs = pl.BlockSpec((1, FFN_TN), lambda j, i: (0, j))
    ts = pl.BlockSpec((FFN_K, FFN_TN), lambda j, i: (0, j))
    return pl.pallas_call(
        body, out_shape=(jax.ShapeDtypeStruct((2, SEQ, D_FF), BF16), vec, vec, taps, taps), grid=(N_FT, SEQ // ROWS),
        in_specs=_ffn_specs() + [cur, nxt(0), nxt(N_FT), nxt(0)],
        out_specs=(pl.BlockSpec((2, ROWS, FFN_TN), lambda j, i: (0, i, j)), vs, vs, ts, ts),
        scratch_shapes=[pltpu.VMEM((2 * FFN_HALO, FFN_TN), F32), tile, tile, tile, tile, ext, ext,
                        pltpu.VMEM((2 + 2 * FFN_K, SUB, FFN_TN), F32)],
        name=name, compiler_params=_cp("parallel", "arbitrary"))(up0, up0, up0, up0, wf, wf, bf, bf, dact, up0, up0, dact)


def ada_fwd(c_all, w_ada, b_cols, name):
    def body(c_ref, w_ref, b_ref, o_ref):
        cc = c_ref[...]
        sc = (cc * _sig(cc)).astype(BF16)
        o_ref[...] = jnp.dot(sc, w_ref[...].astype(BF16), preferred_element_type=F32) + b_ref[...]

    return pl.pallas_call(body, out_shape=jax.ShapeDtypeStruct((N_DEV, w_ada.shape[1]), F32), name=name,
                          compiler_params=_cp())(c_all, w_ada, b_cols)


def _adam(w, g, m, v):
    m = ADAM_B1 * m + (1.0 - ADAM_B1) * g
    v = ADAM_B2 * v + (1.0 - ADAM_B2) * (g * g)
    m_hat = m / (1.0 - ADAM_B1 ** ADAM_STEP)
    v_hat = v / (1.0 - ADAM_B2 ** ADAM_STEP)
    delta = -ADAM_LR * (m_hat / (jnp.sqrt(v_hat) + ADAM_EPS) + ADAM_WD * w)
    return delta, m, v


def ada_bwd_adamw(c_all_t, dmod_cols, w, m, v, name):
    rows, cols = w.shape
    tr = 256

    def body(ct_ref, dm_ref, w_ref, m_ref, v_ref, g_ref, d_ref, nm_ref, nv_ref):
        def chunk(rs):
            ct = ct_ref[rs, :]
            sc = ct * _sig(ct)
            g = sc[:, 0:1] * dm_ref[0:1, :]
            for b in range(1, N_DEV):
                g = g + sc[:, b:b + 1] * dm_ref[b:b + 1, :]
            g_ref[rs, :] = g
            d_ref[rs, :], nm_ref[rs, :], nv_ref[rs, :] = _adam(w_ref[rs, :], g, m_ref[rs, :], v_ref[rs, :])

        _for_chunks(chunk, 2, tr)

    blk = pl.BlockSpec((tr, cols), lambda i: (i, 0))
    shp = jax.ShapeDtypeStruct((rows, cols), F32)
    return pl.pallas_call(
        body, out_shape=(shp, shp, shp, shp), grid=(rows // tr,),
        in_specs=[pl.BlockSpec((tr, N_DEV), lambda i: (i, 0)), pl.BlockSpec((N_DEV, cols), lambda i: (0, 0)), blk, blk, blk],
        out_specs=(blk, blk, blk, blk), name=name, compiler_params=_cp("parallel"))(c_all_t, dmod_cols, w, m, v)


def sum_adamw(parts, own, w, m, v, tr, name):
    n_parts, rows, cols = parts.shape

    def body(*refs):
        if own is None:
            p_ref, w_ref, m_ref, v_ref, g_ref, d_ref, nm_ref, nv_ref = refs
        else:
            p_ref, own_ref, w_ref, m_ref, v_ref, g_ref, d_ref, nm_ref, nv_ref = refs

        def chunk(rs):
            g = (p_ref[0, rs, :] if own is None else own_ref[rs, :]).astype(F32)
            for k in range(1, n_parts):
                g = g + p_ref[k, rs, :].astype(F32)
            g_ref[rs, :] = g
            d_ref[rs, :], nm_ref[rs, :], nv_ref[rs, :] = _adam(w_ref[rs, :], g, m_ref[rs, :], v_ref[rs, :])

        _for_chunks(chunk, 2 if tr > SUB else 1, tr)

    blk = pl.BlockSpec((tr, cols), lambda i: (i, 0))
    shp = jax.ShapeDtypeStruct((rows, cols), F32)
    args = [parts] + ([] if own is None else [own]) + [w, m, v]
    return pl.pallas_call(
        body, out_shape=(shp, shp, shp, shp), grid=(rows // tr,),
        in_specs=[pl.BlockSpec((n_parts, tr, cols), lambda i: (0, i, 0))] + [blk] * (len(args) - 1),
        out_specs=(blk, blk, blk, blk), name=name, compiler_params=_cp("parallel"))(*args)


MESH = pl.DeviceIdType.MESH
ANY = pl.BlockSpec(memory_space=pl.ANY)


def all_gather(block, name, after=None):
    extra = () if after is None else (after,)

    def body(x_ref, *refs):
        out_ref, send_sems, recv_sems, local_sem = refs[len(extra):]
        x, y, c = lax.axis_index("x"), lax.axis_index("y"), lax.axis_index("c")
        me, sibling = (x, y, c), (x, y, 1 - c)
        chips = [(1 - x, y), (x, 1 - y), (1 - x, 1 - y)]

        def slot(px, py, pc):
            return out_ref.at[4 * px + 2 * py + pc]

        def copy(k, blk, to, src=None):
            return pltpu.make_async_remote_copy(
                src_ref=slot(*blk) if src is None else src, dst_ref=slot(*blk),
                send_sem=send_sems.at[k], recv_sem=recv_sems.at[k], device_id=to, device_id_type=MESH)

        mine = pltpu.make_async_copy(x_ref, slot(*me), local_sem)
        mine.start()
        first = [copy(0, me, sibling, src=x_ref)]
        first += [copy(1 + j, me, (*chip, c), src=x_ref) for j, chip in enumerate(chips)]
        for cp in first:
            cp.start()
        passed = [copy(4 + j, (*chip, c), sibling) for j, chip in enumerate(chips)]
        for j, chip in enumerate(chips):
            copy(1 + j, (*chip, c), me).wait_recv()
            passed[j].start()
        copy(0, sibling, me).wait_recv()
        for j, chip in enumerate(chips):
            copy(4 + j, (*chip, 1 - c), me).wait_recv()
        for cp in first + passed:
            cp.wait_send()
        mine.wait()

    return pl.pallas_call(
        body, out_shape=jax.ShapeDtypeStruct((N_DEV,) + block.shape, block.dtype), in_specs=[ANY] * (1 + len(extra)), out_specs=ANY,
        scratch_shapes=[pltpu.SemaphoreType.DMA((7,)), pltpu.SemaphoreType.DMA((7,)), pltpu.SemaphoreType.DMA],
        name=name)(block, *extra)


HBM = pl.BlockSpec(memory_space=pltpu.HBM)
SEM = pl.BlockSpec(memory_space=pltpu.SEMAPHORE)
EFFECT = pltpu.SideEffectType.DATAFLOW_SIDE_EFFECTING


def _peer_copies(src_ref, land_ref, send_sems, recv_sems, gather):
    x, y, c = lax.axis_index("x"), lax.axis_index("y"), lax.axis_index("c")
    me = 4 * x + 2 * y + c
    copies = []
    for k in range(1, N_DEV):
        px = 1 - x if k & 4 else x
        py = 1 - y if k & 2 else y
        pc = 1 - c if k & 1 else c
        copies.append(pltpu.make_async_remote_copy(
            src_ref=src_ref if gather else src_ref.at[4 * px + 2 * py + pc],
            dst_ref=land_ref.at[me] if gather else land_ref.at[k],
            send_sem=send_sems.at[k - 1], recv_sem=recv_sems.at[k - 1], device_id=(px, py, pc), device_id_type=MESH))
    return copies


def exchange_start(srcs, gather, name, after=None):
    n = len(srcs)
    land_shapes = [(N_DEV,) + src.shape if gather else src.shape for src in srcs]
    extra = () if after is None else (after,)

    def body(*refs):
        src_refs, land_refs = refs[0:n], refs[n:2 * n]
        outs = refs[2 * n + len(extra):]
        for k in range(n):
            for cp in _peer_copies(src_refs[k], land_refs[k], outs[4 * k], outs[4 * k + 1], gather):
                cp.start()
        token = outs[4 * n]
        token[...] = jnp.zeros_like(token)

    out_shape, out_specs, aliases = [], [], {}
    for k, src in enumerate(srcs):
        out_shape += [pltpu.SemaphoreType.DMA((N_DEV - 1,)), pltpu.SemaphoreType.DMA((N_DEV - 1,)),
                      pltpu.HBM(src.shape, src.dtype), pltpu.HBM(land_shapes[k], src.dtype)]
        out_specs += [SEM, SEM, HBM, HBM]
        aliases[k] = 4 * k + 2
        aliases[n + k] = 4 * k + 3
    out_shape.append(jax.ShapeDtypeStruct((8, 128), F32))
    out_specs.append(pl.BlockSpec(memory_space=pltpu.VMEM))
    res = pl.pallas_call(
        body, name=name, out_shape=tuple(out_shape), in_specs=(HBM,) * (2 * n) + (ANY,) * len(extra),
        out_specs=tuple(out_specs), input_output_aliases=aliases,
        compiler_params=pltpu.CompilerParams(has_side_effects=EFFECT),
    )(*[pltpu.with_memory_space_constraint(src, pltpu.HBM) for src in srcs],
      *[pltpu.with_memory_space_constraint(lax.empty(shp, src.dtype), pltpu.HBM) for shp, src in zip(land_shapes, srcs)],
      *extra)
    return [tuple(res[4 * k:4 * k + 4]) for k in range(n)], res[4 * n][0, 0]


def exchange_wait(handles, after, gather, name):
    send_sems, recv_sems, src_thru, land_thru = handles

    def body(src_ref, land_ref, send_sems, recv_sems, after_ref, src_dead, got_ref):
        for cp in _peer_copies(src_ref, land_ref, send_sems, recv_sems, gather):
            cp.wait_send()
            cp.wait_recv()

    return pl.pallas_call(
        body, name=name,
        out_shape=(pltpu.HBM(src_thru.shape, src_thru.dtype), pltpu.HBM(land_thru.shape, land_thru.dtype)),
        in_specs=(HBM, HBM, SEM, SEM, ANY), out_specs=(HBM, HBM), input_output_aliases={0: 0, 1: 1},
        compiler_params=pltpu.CompilerParams(has_side_effects=EFFECT),
    )(src_thru, land_thru, send_sems, recv_sems, after)[1]


def _to_pattern(a, r):
    return a.reshape(SEQ // r, r * a.shape[1])


def local_step(x, tgt, mod, get_w, put_grad, wc, wf, g_mix, bc, lg, lb, gco, gao, g_ffn, bf, g_fin):
    h1 = rms_mod_fwd(x, g_mix, mod, 0, 1, "h1_fwd")
    w_in = get_w("w_in", h1)
    proj = matmul(h1, w_in, "nt", F32, 512, D_IN, "proj_fwd")
    mix_a = conv_module_fwd(proj, wc, bc, lg, lb, gco, "conv_module_fwd")
    att, lse = attn_fwd_all(proj, "attn_fwd")
    mix_b = rms_gain_bf16(att, gao, "attn_out_norm")
    mixed = jnp.concatenate([mix_a, mix_b], axis=1)
    w_out = get_w("w_out", mixed)
    y1 = matmul(mixed, w_out, "nn", F32, 512, D_MODEL, "out_proj_fwd")
    x1, h2 = resid_rms_mod_fwd(x, y1, g_ffn, mod, 2, 3, 4, "x1_h2_fwd")
    w_up = get_w("w_up", h2)
    up0 = matmul(h2, w_up, "nt", F32, 512, D_FF, "up_fwd")
    act = ffn_act_fwd(up0, wf, bf, "ffn_act_fwd")
    w_down = get_w("w_down", act)
    y2 = matmul(act, w_down, "nn", F32, 512, D_MODEL, "down_fwd")
    loss_t, dx2, dy2, d_gfin, d_gaf = final_loss_bwd(x1, y2, tgt, g_fin, mod, 5, "loss_bwd")
    dact = matmul(dy2, w_down, "nt", F32, 512, FFN_TN, "down_bwd_x")
    dw_down = matmul(act, dy2, "tn", BF16, 256, D_MODEL, "down_bwd_w")
    dup0, dbf_g, dbf_v, dwf_g, dwf_v = ffn_bwd(up0, dact, wf + put_grad("w_down", dw_down), bf, "ffn_bwd")
    dh2 = matmul_halves(dup0, w_up, 512, 512, "up_bwd_x")
    dw_up = matmul_tn_halves(dup0, h2, 256, "up_bwd_w")
    dx1, d_shf, d_scf, d_gffn, dy1, d_gam = rms_mod_bwd(x1, dh2, dx2, g_ffn + put_grad("w_up", dw_up), mod, 4, y1, 2, "h2_bwd")
    dmixed = matmul(dy1, w_out, "nt", F32, 512, D_MODEL, "out_proj_bwd_x")
    dw_out = matmul(mixed, dy1, "tn", BF16, 256, D_MODEL, "out_proj_bwd_w")
    do, dd, d_gao = attn_combine_bwd(dmixed, att, gao + put_grad("w_out", dw_out), "attn_combine_bwd")
    dqkv = attn_bwd_all(proj, do, lse, dd, "attn_bwd")
    du1, d_gco, d_lg, d_lb, d_bc, d_wc = conv_module_bwd_a(proj, dmixed, wc, bc, lg, lb, gco, "conv_module_bwd_a")
    dproj_a = conv_module_bwd_b(proj, du1, wc, "conv_module_bwd_b")
    dproj = jnp.concatenate([dproj_a, dqkv[0], dqkv[1], dqkv[2]], axis=1)
    dw_in = matmul(dproj, h1, "tn", BF16, 512, D_MODEL, "proj_bwd_w")
    dh1 = matmul(dproj, w_in, "nn", F32, 512, D_MODEL, "proj_bwd_x")
    dx, d_shm, d_scm, d_gmix = rms_mod_bwd(x, dh1, dx1, g_mix + put_grad("w_in", dw_in), mod, 1, None, 0, "h1_bwd")
    dmod = jnp.concatenate([d_shm, d_scm, d_gam, d_shf, d_scf, d_gaf], axis=1)
    small = dict(g_norm_mix=d_gmix, b_conv_dw=d_bc, ln_conv_g=d_lg, ln_conv_b=d_lb, g_conv_out=d_gco, g_attn_out=d_gao,
                 g_norm_ffn=d_gffn, b_ffn_dw=jnp.concatenate([dbf_g, dbf_v], axis=1), g_final=d_gfin,
                 w_conv_dw=d_wc, w_ffn_dw=jnp.concatenate([dwf_g, dwf_v], axis=1), dmod=dmod, loss=loss_t[0:1, 0:1])
    return dx, small


def _padw(a, width):
    return jnp.pad(a, ((0, 0), (0, width - a.shape[1])))


def pack_small(t):
    wide = jnp.concatenate([_padw(t["dmod"], PACK_W), _padw(t["b_ffn_dw"], PACK_W), _padw(t["w_ffn_dw"], PACK_W),
                            _padw(t["loss"], PACK_W), jnp.zeros((2, PACK_W), F32)], axis=0)
    z512 = jnp.zeros((1, 512), F32)
    narrow = jnp.concatenate([
        t["g_norm_mix"], t["g_norm_ffn"], t["g_final"],
        jnp.concatenate([t["b_conv_dw"], t["ln_conv_g"]], axis=1),
        jnp.concatenate([t["ln_conv_b"], t["g_conv_out"]], axis=1),
        jnp.concatenate([t["g_attn_out"], z512], axis=1),
        jnp.zeros((2, 1024), F32),
        jnp.pad(t["w_conv_dw"], ((0, 1), (0, 0))).reshape(16, 1024)], axis=0)
    return jnp.concatenate([wide, narrow.reshape(4, PACK_W), jnp.zeros((4, PACK_W), F32)], axis=0)


def unpack_small(p):
    narrow = p[8:12].reshape(24, 1024)
    return dict(
        dmod=p[0:1], b_ffn_dw=p[1:2, :2 * D_FF], w_ffn_dw=p[2:5, :2 * D_FF], loss=p[5, 0],
        g_norm_mix=narrow[0:1], g_norm_ffn=narrow[1:2], g_final=narrow[2:3],
        b_conv_dw=narrow[3:4, :512], ln_conv_g=narrow[3:4, 512:], ln_conv_b=narrow[4:5, :512], g_conv_out=narrow[4:5, 512:],
        g_attn_out=narrow[5:6, :512], w_conv_dw=narrow[8:24].reshape(32, 512)[:CONV_K])


def _embed(local, width, me):
    return lax.dynamic_update_slice(jnp.zeros((local.shape[0], width), F32), local, (0, me * local.shape[1]))


def _shard(full, n_cols, me):
    return lax.dynamic_slice(full, (0, me * n_cols), (full.shape[0], n_cols))


WEIGHTS = ["w_ada", "b_ada", "g_norm_mix", "w_in", "w_conv_dw", "b_conv_dw", "ln_conv_g", "ln_conv_b", "g_conv_out",
           "g_attn_out", "w_out", "g_norm_ffn", "w_up", "w_ffn_dw", "b_ffn_dw", "w_down", "g_final"]
SMALL_REPLICATED = ["g_norm_mix", "b_conv_dw", "ln_conv_g", "ln_conv_b", "g_conv_out", "g_attn_out", "g_norm_ffn",
                    "b_ffn_dw", "g_final"]


def kernel(x, c, w_ada, b_ada, g_norm_mix, w_in, w_conv_dw, b_conv_dw, ln_conv_g, ln_conv_b, g_conv_out, g_attn_out, w_out, g_norm_ffn, w_up, w_ffn_dw, b_ffn_dw, w_down, g_final, loss_target, m_w_ada, m_b_ada, m_g_norm_mix, m_w_in, m_w_conv_dw, m_b_conv_dw, m_ln_conv_g, m_ln_conv_b, m_g_conv_out, m_g_attn_out, m_w_out, m_g_norm_ffn, m_w_up, m_w_ffn_dw, m_b_ffn_dw, m_w_down, m_g_final, v_w_ada, v_b_ada, v_g_norm_mix, v_w_in, v_w_conv_dw, v_b_conv_dw, v_ln_conv_g, v_ln_conv_b, v_g_conv_out, v_g_attn_out, v_w_out, v_g_norm_ffn, v_w_up, v_w_ffn_dw, v_b_ffn_dw, v_w_down, v_g_final):
    args = dict(locals())
    me = 4 * lax.axis_index("x") + 2 * lax.axis_index("y") + lax.axis_index("c")

    def flat(name, prefix=""):
        a = args[prefix + name]
        return a.reshape(a.shape[-2] if a.ndim > 1 else 1, a.shape[-1])

    def flat_t(name, prefix=""):
        return args[prefix + name][0].T

    n_in, n_up, r_out, r_down = w_in.shape[2], w_up.shape[2], w_out.shape[1], w_down.shape[1]
    n_ada, n_wc, n_wf = w_ada.shape[2], w_conv_dw.shape[2], w_ffn_dw.shape[2]
    taps_c = jnp.pad(flat("w_conv_dw").reshape(1, CONV_K * n_wc), ((0, 0), (0, 2 * D_MODEL - CONV_K * n_wc)))
    taps_f = jnp.pad(flat("w_ffn_dw").reshape(1, FFN_K * n_wf), ((0, 0), (0, 3 * D_MODEL - FFN_K * n_wf)))
    first = jnp.concatenate([c, taps_c.reshape(2, D_MODEL), taps_f.reshape(3, D_MODEL), jnp.zeros((2, D_MODEL), F32)], axis=0)
    w_in_block = flat_t("w_in").astype(BF16)
    first_bits = lax.bitcast_convert_type(first, BF16).reshape(16, D_MODEL)
    first_block = all_gather(jnp.concatenate([first_bits, w_in_block], axis=0), "gather_c_taps_w_in")
    first_all = lax.bitcast_convert_type(first_block[:, :16, :].reshape(N_DEV, 8, D_MODEL, 2), F32)
    w_in_full = first_block[:, 16:, :].reshape(D_IN, D_MODEL)
    c_all = first_all[:, 0, :]
    wc_full = first_all[:, 1:3, :].reshape(N_DEV, 2 * D_MODEL)[:, :CONV_K * n_wc].reshape(N_DEV, CONV_K, n_wc)
    wc_full = wc_full.transpose(1, 0, 2).reshape(CONV_K, D_CONV)
    wf_full = first_all[:, 3:6, :].reshape(N_DEV, 3 * D_MODEL)[:, :FFN_K * n_wf].reshape(N_DEV, FFN_K, n_wf)
    wf_full = wf_full.transpose(1, 0, 2).reshape(FFN_K, 2 * D_FF)
    mod_cols = ada_fwd(c_all, flat("w_ada"), _shard(flat("b_ada"), n_ada, me), "ada_fwd")
    mod_all = all_gather(mod_cols, "gather_mod")
    mod = lax.dynamic_index_in_dim(mod_all, me, axis=1, keepdims=False).reshape(N_MOD, D_MODEL)
    mod = jnp.pad(mod, ((0, 2), (0, 0)))

    order = ("w_out", "w_up", "w_down")
    blocks = dict(w_up=flat_t("w_up").astype(BF16), w_out=flat("w_out").astype(BF16), w_down=flat("w_down").astype(BF16))
    handles, tok = exchange_start([blocks[name] for name in order], True, "gather_weights_start", mod_all)
    gathers = dict(zip(order, handles))
    mod = mod + tok

    def gathered(name, after):
        land = exchange_wait(gathers[name], after, True, f"gather_{name}_wait")
        return lax.dynamic_update_index_in_dim(land, blocks[name], me, axis=0)

    def get_w(name, after):
        if name == "w_in":
            return w_in_full
        return gathered(name, after).reshape(-1, D_MODEL)

    exchanges, own = {}, {}

    def put_grad(name, dw, after=None):
        dev_major = dw.reshape(N_DEV, -1, D_MODEL)
        own[name] = lax.dynamic_index_in_dim(dev_major, me, axis=0, keepdims=False)
        (exchanges[name],), token = exchange_start([dev_major], False, f"exchange_{name}_start", after)
        return token

    grad_x, small = local_step(
        x[0], loss_target[0], mod, get_w, put_grad, wc_full, wf_full,
        flat("g_norm_mix"), flat("b_conv_dw"), flat("ln_conv_g"), flat("ln_conv_b"), flat("g_conv_out"),
        flat("g_attn_out"), flat("g_norm_ffn"), flat("b_ffn_dw"), flat("g_final"))

    out = {}

    def finish(name, tr, after):
        parts = exchange_wait(exchanges[name], after, False, f"exchange_{name}_wait")
        if name in ("w_in", "w_up"):
            res = sum_adamw(parts, own[name], flat_t(name), flat_t(name, "m_"), flat_t(name, "v_"), tr, "adamw_" + name)
            out[name] = tuple(r.T for r in res)
        else:
            res = out[name] = sum_adamw(parts, own[name], flat(name), flat(name, "m_"), flat(name, "v_"), tr, "adamw_" + name)
        return res[0]

    after = finish("w_down", r_down, grad_x)
    after = finish("w_up", n_up // 2, after)
    after = finish("w_out", r_out, after)
    after = finish("w_in", n_in, after)

    small_all = all_gather(pack_small(small), "gather_small", after)

    def packed(prefix):
        t = {n: flat(n, prefix) for n in SMALL_REPLICATED}
        t["dmod"] = flat("b_ada", prefix)
        t["loss"] = jnp.zeros((1, 1), F32)
        t["w_conv_dw"] = _embed(flat("w_conv_dw", prefix), D_CONV, me)
        t["w_ffn_dw"] = _embed(flat("w_ffn_dw", prefix), 2 * D_FF, me)
        return pack_small(t)

    res = sum_adamw(small_all, None, packed(""), packed("m_"), packed("v_"), PACK_ROWS, "adamw_small")
    res = [unpack_small(r) for r in res]
    loss = res[0]["loss"]
    for n in SMALL_REPLICATED:
        out[n] = tuple(r[n] for r in res)
    out["b_ada"] = tuple(r["dmod"] for r in res)
    out["w_conv_dw"] = tuple(_shard(r["w_conv_dw"], n_wc, me) for r in res)
    out["w_ffn_dw"] = tuple(_shard(r["w_ffn_dw"], n_wf, me) for r in res)

    dmod_cols = _shard(small_all[:, 0, :], n_ada, me)
    out["w_ada"] = ada_bwd_adamw(c_all.T, dmod_cols, flat("w_ada"), flat("w_ada", "m_"), flat("w_ada", "v_"), "adamw_w_ada")

    result = [loss, grad_x[None]]
    for k in range(4):
        result += [out[n][k].reshape(args[n].shape) for n in WEIGHTS]
    return tuple(result)
```

```python
import functools

import jax
import jax.numpy as jnp
from jax import lax
from jax.experimental import pallas as pl
from jax.experimental.pallas import tpu as pltpu

F32 = jnp.float32
BF16 = jnp.bfloat16

N_DEV = 8
SEQ = 2048
D_MODEL = 1024
D_CONV = 512
D_ATTN = 512
HEAD_DIM = 64
CONV_K = 31
D_FF = 2816
FFN_K = 3
D_IN = 2 * D_CONV + 3 * D_ATTN
N_MOD = 6
EPS = 1e-6
ATTN_BLOCK = 128
PATTERNS = ((2048, 1), (512, 4), (128, 16))
NEG = -1e30

ADAM_LR, ADAM_B1, ADAM_B2, ADAM_EPS, ADAM_WD, ADAM_STEP = 0.001, 0.9, 0.999, 1e-08, 0.01, 10

ROWS = 256
CONV_HALO = 32
FFN_HALO = 8
FFN_TN = 1408
VMEM_LIMIT = 56 * 1024 * 1024
PACK_ROWS, PACK_W = 16, 6144


def _cp(*sem):
    return pltpu.CompilerParams(dimension_semantics=sem if sem else None, vmem_limit_bytes=VMEM_LIMIT)


def _sig(x):
    return 1.0 / (1.0 + jnp.exp(-x))


def _rsum(x):
    return jnp.sum(x, axis=0, keepdims=True)


def _mean(x):
    return jnp.mean(x, axis=-1, keepdims=True)


def _acc(ref, val, first):
    @pl.when(first)
    def _():
        ref[...] = val

    @pl.when(jnp.logical_not(first))
    def _():
        ref[...] += val


SUB = 16


def _for_chunks(fn, unroll=1, rows=ROWS):
    def step(i, carry):
        fn(pl.ds(pl.multiple_of(i * SUB, SUB), SUB))
        return carry

    lax.fori_loop(0, rows // SUB, step, 0, unroll=unroll)


def rms_mod_fwd(x, g, mod, sh_row, sc_row, name):
    def body(x_ref, g_ref, mod_ref, h_ref):
        xx = x_ref[...]
        r = lax.rsqrt(_mean(xx * xx) + EPS)
        h = xx * r * g_ref[...]
        h_ref[...] = (h * (1.0 + mod_ref[sc_row:sc_row + 1, :]) + mod_ref[sh_row:sh_row + 1, :]).astype(BF16)

    return pl.pallas_call(
        body, out_shape=jax.ShapeDtypeStruct((SEQ, D_MODEL), BF16), grid=(SEQ // ROWS,),
        in_specs=[_row_spec(D_MODEL), _vec_spec(D_MODEL), _vec_spec(D_MODEL, 8)],
        out_specs=_row_spec(D_MODEL), name=name, compiler_params=_cp("parallel"))(x, g, mod)


def resid_rms_mod_fwd(x, y, g, mod, ga_row, sh_row, sc_row, name):
    def body(x_ref, y_ref, g_ref, mod_ref, x1_ref, h_ref):
        x1 = x_ref[...] + mod_ref[ga_row:ga_row + 1, :] * y_ref[...]
        x1_ref[...] = x1
        r = lax.rsqrt(_mean(x1 * x1) + EPS)
        h = x1 * r * g_ref[...]
        h_ref[...] = (h * (1.0 + mod_ref[sc_row:sc_row + 1, :]) + mod_ref[sh_row:sh_row + 1, :]).astype(BF16)

    return pl.pallas_call(
        body, out_shape=(jax.ShapeDtypeStruct((SEQ, D_MODEL), F32), jax.ShapeDtypeStruct((SEQ, D_MODEL), BF16)),
        grid=(SEQ // ROWS,),
        in_specs=[_row_spec(D_MODEL), _row_spec(D_MODEL), _vec_spec(D_MODEL), _vec_spec(D_MODEL, 8)],
        out_specs=(_row_spec(D_MODEL), _row_spec(D_MODEL)), name=name, compiler_params=_cp("parallel"))(x, y, g, mod)


def final_loss_bwd(x1, y2, tgt, g, mod, ga_row, name):
    def body(x1_ref, y2_ref, t_ref, g_ref, mod_ref, loss_ref, dx2_ref, dy2_ref, dg_ref, dga_ref):
        first = pl.program_id(0) == 0
        ga = mod_ref[ga_row:ga_row + 1, :]
        y2 = y2_ref[...]
        x2 = x1_ref[...] + ga * y2
        r = lax.rsqrt(_mean(x2 * x2) + EPS)
        xn = x2 * r
        err = xn * g_ref[...] - t_ref[...]
        _acc(loss_ref, jnp.broadcast_to(0.5 * jnp.sum(_mean(err * err)), (8, 128)), first)
        dy = err * (1.0 / D_MODEL)
        _acc(dg_ref, _rsum(dy * xn), first)
        dxn = dy * g_ref[...]
        dx2 = r * (dxn - xn * _mean(dxn * xn))
        dx2_ref[...] = dx2
        dy2_ref[...] = (dx2 * ga).astype(BF16)
        _acc(dga_ref, _rsum(dx2 * y2), first)

    vec = jax.ShapeDtypeStruct((1, D_MODEL), F32)
    return pl.pallas_call(
        body,
        out_shape=(jax.ShapeDtypeStruct((8, 128), F32), jax.ShapeDtypeStruct((SEQ, D_MODEL), F32),
                   jax.ShapeDtypeStruct((SEQ, D_MODEL), BF16), vec, vec),
        grid=(SEQ // ROWS,),
        in_specs=[_row_spec(D_MODEL), _row_spec(D_MODEL), _row_spec(D_MODEL), _vec_spec(D_MODEL), _vec_spec(D_MODEL, 8)],
        out_specs=(pl.BlockSpec((8, 128), lambda i: (0, 0)), _row_spec(D_MODEL), _row_spec(D_MODEL),
                   _vec_spec(D_MODEL), _vec_spec(D_MODEL)),
        name=name, compiler_params=_cp("arbitrary"))(x1, y2, tgt, g, mod)


def rms_mod_bwd(x, dh, dres, g, mod, sc_row, y, ga_row, name):
    gated = y is not None

    def body(*refs):
        if gated:
            x_ref, dh_ref, dres_ref, g_ref, mod_ref, y_ref, dx_ref, dsh_ref, dsc_ref, dg_ref, dy_ref, dga_ref = refs
        else:
            x_ref, dh_ref, dres_ref, g_ref, mod_ref, dx_ref, dsh_ref, dsc_ref, dg_ref = refs
        first = pl.program_id(0) == 0
        xx = x_ref[...]
        dh = dh_ref[...]
        gg = g_ref[...]
        r = lax.rsqrt(_mean(xx * xx) + EPS)
        xn = xx * r
        _acc(dsh_ref, _rsum(dh), first)
        _acc(dsc_ref, _rsum(dh * (xn * gg)), first)
        dt = dh * (1.0 + mod_ref[sc_row:sc_row + 1, :])
        _acc(dg_ref, _rsum(dt * xn), first)
        dxn = dt * gg
        dx = dres_ref[...] + r * (dxn - xn * _mean(dxn * xn))
        dx_ref[...] = dx
        if gated:
            _acc(dga_ref, _rsum(dx * y_ref[...]), first)
            dy_ref[...] = (dx * mod_ref[ga_row:ga_row + 1, :]).astype(BF16)

    vec = jax.ShapeDtypeStruct((1, D_MODEL), F32)
    in_specs = [_row_spec(D_MODEL), _row_spec(D_MODEL), _row_spec(D_MODEL), _vec_spec(D_MODEL), _vec_spec(D_MODEL, 8)]
    out_shape = [jax.ShapeDtypeStruct((SEQ, D_MODEL), F32), vec, vec, vec]
    out_specs = [_row_spec(D_MODEL), _vec_spec(D_MODEL), _vec_spec(D_MODEL), _vec_spec(D_MODEL)]
    args = [x, dh, dres, g, mod]
    if gated:
        in_specs.append(_row_spec(D_MODEL))
        out_shape += [jax.ShapeDtypeStruct((SEQ, D_MODEL), BF16), vec]
        out_specs += [_row_spec(D_MODEL), _vec_spec(D_MODEL)]
        args.append(y)
    return pl.pallas_call(
        body, out_shape=tuple(out_shape), grid=(SEQ // ROWS,), in_specs=in_specs, out_specs=tuple(out_specs),
        name=name, compiler_params=_cp("arbitrary"))(*args)


def _prev_halo(halo, width, col):
    per = ROWS // halo
    return pl.BlockSpec((halo, width), lambda i: (jnp.maximum(i * per - 1, 0), col))


def _next_halo(halo, width, col):
    per = ROWS // halo
    last = SEQ // halo - 1
    return pl.BlockSpec((halo, width), lambda i: (jnp.minimum((i + 1) * per, last), col))


CONV_PAD = ROWS + CONV_HALO


def _shift_copies(sh):
    for b in range(1, 8):
        sh[b, 0:CONV_PAD - 8, :] = sh[0, pl.ds(b, CONV_PAD - 8), :]


def _tap(sh, rs_start, offset):
    return sh[offset % 8, pl.ds(pl.multiple_of(rs_start + (offset // 8) * 8, 8), SUB), :]


def _conv_module_forward(av_ref, ag_ref, avh_ref, agh_ref, wc_ref, bc_ref, lg_ref, lb_ref, sh, u1_s):
    i = pl.program_id(0)
    hv = avh_ref[...] * _sig(agh_ref[...])
    sh[0, 0:CONV_HALO, :] = jnp.where(i > 0, hv, 0.0)

    def glu(rs):
        sh[0, pl.ds(pl.multiple_of(rs.start + CONV_HALO, SUB), SUB), :] = av_ref[rs, :] * _sig(ag_ref[rs, :])

    _for_chunks(glu)
    _shift_copies(sh)

    def conv(rs):
        u1 = jnp.broadcast_to(bc_ref[...], (SUB, D_CONV))
        for j in range(CONV_K):
            u1 = u1 + wc_ref[j:j + 1, :] * _tap(sh, rs.start, CONV_HALO - (CONV_K - 1) + j)
        u1_s[rs, :] = u1

    _for_chunks(conv)
    u1 = u1_s[...]
    mu = _mean(u1)
    cen = u1 - mu
    rs = lax.rsqrt(_mean(cen * cen) + EPS)
    z = cen * rs
    ln = z * lg_ref[...] + lb_ref[...]
    s = _sig(ln)
    return z, rs, ln, s, ln * s


_CONV_SCRATCH = [pltpu.VMEM((8, CONV_PAD, D_CONV), F32), pltpu.VMEM((ROWS, D_CONV), F32)]


def conv_module_fwd(proj, wc, bc, lg, lb, gco, name):
    def body(av_ref, ag_ref, avh_ref, agh_ref, wc_ref, bc_ref, lg_ref, lb_ref, gco_ref, out_ref, sh, u1_s):
        _, _, _, _, u2 = _conv_module_forward(av_ref, ag_ref, avh_ref, agh_ref, wc_ref, bc_ref, lg_ref, lb_ref, sh, u1_s)
        rc = lax.rsqrt(_mean(u2 * u2) + EPS)
        out_ref[...] = (u2 * rc * gco_ref[...]).astype(BF16)

    v = _vec_spec(D_CONV)
    return pl.pallas_call(
        body, out_shape=jax.ShapeDtypeStruct((SEQ, D_CONV), BF16), grid=(SEQ // ROWS,),
        in_specs=[_row_spec(D_CONV, 0), _row_spec(D_CONV, 1), _prev_halo(CONV_HALO, D_CONV, 0),
                  _prev_halo(CONV_HALO, D_CONV, 1), _vec_spec(D_CONV, CONV_K), v, v, v, v],
        out_specs=_row_spec(D_CONV), scratch_shapes=list(_CONV_SCRATCH),
        name=name, compiler_params=_cp("parallel"))(proj, proj, proj, proj, wc, bc, lg, lb, gco)


def conv_module_bwd_a(proj, dmixed, wc, bc, lg, lb, gco, name):
    def body(av_ref, ag_ref, avh_ref, agh_ref, dm_ref, wc_ref, bc_ref, lg_ref, lb_ref, gco_ref,
             du1_ref, dgco_ref, dlg_ref, dlb_ref, dbc_ref, dwc_ref, sh, u1_s, acc):
        first = pl.program_id(0) == 0
        z, rs, ln, s, u2 = _conv_module_forward(av_ref, ag_ref, avh_ref, agh_ref, wc_ref, bc_ref, lg_ref, lb_ref, sh, u1_s)
        rc = lax.rsqrt(_mean(u2 * u2) + EPS)
        xn = u2 * rc
        dm = dm_ref[...]
        _acc(dgco_ref, _rsum(dm * xn), first)
        dyn = dm * gco_ref[...]
        du2 = rc * (dyn - xn * _mean(dyn * xn))
        dln = du2 * (s * (1.0 + ln * (1.0 - s)))
        _acc(dlg_ref, _rsum(dln * z), first)
        _acc(dlb_ref, _rsum(dln), first)
        dz = dln * lg_ref[...]
        du1 = rs * (dz - _mean(dz) - z * _mean(dz * z))
        du1_ref[...] = du1
        _acc(dbc_ref, _rsum(du1), first)
        acc[...] = jnp.zeros_like(acc)

        def taps(rs):
            d = du1_ref[rs, :]
            for j in range(CONV_K):
                acc[j] += d * _tap(sh, rs.start, CONV_HALO - (CONV_K - 1) + j)

        _for_chunks(taps)

        @pl.when(first)
        def _():
            dwc_ref[...] = jnp.zeros_like(dwc_ref)

        for j in range(CONV_K):
            dwc_ref[j:j + 1, :] += _rsum(acc[j])

    v = _vec_spec(D_CONV)
    vec = jax.ShapeDtypeStruct((1, D_CONV), F32)
    return pl.pallas_call(
        body,
        out_shape=(jax.ShapeDtypeStruct((SEQ, D_CONV), F32), vec, vec, vec, vec, jax.ShapeDtypeStruct((CONV_K, D_CONV), F32)),
        grid=(SEQ // ROWS,),
        in_specs=[_row_spec(D_CONV, 0), _row_spec(D_CONV, 1), _prev_halo(CONV_HALO, D_CONV, 0),
                  _prev_halo(CONV_HALO, D_CONV, 1), _row_spec(D_CONV, 0), _vec_spec(D_CONV, CONV_K), v, v, v, v],
        out_specs=(_row_spec(D_CONV), v, v, v, v, _vec_spec(D_CONV, CONV_K)),
        scratch_shapes=list(_CONV_SCRATCH) + [pltpu.VMEM((CONV_K, SUB, D_CONV), F32)],
        name=name, compiler_params=_cp("arbitrary"))(proj, proj, proj, proj, dmixed, wc, bc, lg, lb, gco)


def conv_module_bwd_b(proj, du1, wc, name):
    def body(av_ref, ag_ref, du1_ref, du1n_ref, wc_ref, out_ref, sh):
        i = pl.program_id(0)
        sh[0, 0:ROWS, :] = du1_ref[...]
        sh[0, ROWS:, :] = jnp.where(i < SEQ // ROWS - 1, du1n_ref[...], 0.0)
        _shift_copies(sh)

        def chunk(rs):
            du0 = jnp.zeros((SUB, D_CONV), F32)
            for j in range(CONV_K):
                du0 = du0 + wc_ref[j:j + 1, :] * _tap(sh, rs.start, CONV_K - 1 - j)
            sg = _sig(ag_ref[rs, :])
            out_ref[rs, 0:D_CONV] = (du0 * sg).astype(BF16)
            out_ref[rs, D_CONV:] = (du0 * av_ref[rs, :] * sg * (1.0 - sg)).astype(BF16)

        _for_chunks(chunk)

    return pl.pallas_call(
        body, out_shape=jax.ShapeDtypeStruct((SEQ, 2 * D_CONV), BF16), grid=(SEQ // ROWS,),
        in_specs=[_row_spec(D_CONV, 0), _row_spec(D_CONV, 1), _row_spec(D_CONV, 0), _next_halo(CONV_HALO, D_CONV, 0),
                  _vec_spec(D_CONV, CONV_K)],
        out_specs=_row_spec(2 * D_CONV), scratch_shapes=[pltpu.VMEM((8, CONV_PAD, D_CONV), F32)],
        name=name, compiler_params=_cp("parallel"))(proj, proj, du1, du1, wc)


def _attn_specs(sub_len, pairs):
    ng, width = 4 // pairs, 128 * pairs
    q = pl.BlockSpec((sub_len, width), lambda rho, g: (0, rho * 3 * ng + g))
    k = pl.BlockSpec((sub_len, width), lambda rho, g: (0, rho * 3 * ng + ng + g))
    v = pl.BlockSpec((sub_len, width), lambda rho, g: (0, rho * 3 * ng + 2 * ng + g))
    o = pl.BlockSpec((sub_len, width), lambda rho, g: (0, rho * ng + g))
    return q, k, v, o


ATTN_PAIRS = {1: 1, 4: 1, 16: 4}


def _attn_block(q_ref, k_ref, v_ref, n, hs, win):
    q0 = pl.multiple_of(n * ATTN_BLOCK, ATTN_BLOCK)
    k0 = pl.multiple_of(jnp.maximum(n - 1, 0) * ATTN_BLOCK, ATTN_BLOCK)
    qb = q_ref[pl.ds(q0, ATTN_BLOCK), hs]
    kw = k_ref[pl.ds(k0, win), hs]
    vw = v_ref[pl.ds(k0, win), hs]
    s = lax.dot_general(qb, kw, (((1,), (1,)), ((), ())), preferred_element_type=F32) * (HEAD_DIM ** -0.5)
    dist = (q0 - k0) + lax.broadcasted_iota(jnp.int32, (ATTN_BLOCK, win), 0) \
        - lax.broadcasted_iota(jnp.int32, (ATTN_BLOCK, win), 1)
    s = jnp.where((dist >= 0) & (dist <= ATTN_BLOCK), s, NEG)
    return q0, k0, qb, kw, vw, s


def attn_fwd(qkv_r, sub_len, r, name):
    nb = sub_len // ATTN_BLOCK
    win = 2 * ATTN_BLOCK if nb > 1 else ATTN_BLOCK
    pairs = ATTN_PAIRS[r]

    def body(q_ref, k_ref, v_ref, o_ref, l_ref):
        def block(n, carry):
            for h in range(2 * pairs):
                hs = slice(h * HEAD_DIM, (h + 1) * HEAD_DIM)
                q0, _, _, _, vw, s = _attn_block(q_ref, k_ref, v_ref, n, hs, win)
                m = jnp.max(s, axis=1, keepdims=True)
                p = jnp.exp(s - m)
                den = jnp.sum(p, axis=1, keepdims=True)
                o = jnp.dot(p.astype(BF16), vw, preferred_element_type=F32) / den
                o_ref[pl.ds(q0, ATTN_BLOCK), hs] = o
                l_ref[pl.ds(q0, ATTN_BLOCK), hs] = jnp.broadcast_to(m + jnp.log(den), (ATTN_BLOCK, HEAD_DIM))
            return carry

        lax.fori_loop(0, nb, block, 0, unroll=min(nb, 2))

    q, k, v, o = _attn_specs(sub_len, pairs)
    shp = jax.ShapeDtypeStruct((sub_len, r * D_ATTN), F32)
    return pl.pallas_call(
        body, out_shape=(shp, shp), grid=(r, 4 // pairs), in_specs=[q, k, v], out_specs=(o, o),
        name=name, compiler_params=_cp("parallel", "parallel"))(qkv_r, qkv_r, qkv_r)


def attn_bwd(qkv_r, do_r, lse_r, dd_r, sub_len, r, name):
    nb = sub_len // ATTN_BLOCK
    win = 2 * ATTN_BLOCK if nb > 1 else ATTN_BLOCK
    pairs = ATTN_PAIRS[r]

    def body(q_ref, k_ref, v_ref, do_ref, l_ref, dd_ref, dq_ref, dk_ref, dv_ref):
        dk_ref[...] = jnp.zeros_like(dk_ref)
        dv_ref[...] = jnp.zeros_like(dv_ref)

        def block(n, carry):
            for h in range(2 * pairs):
                hs = slice(h * HEAD_DIM, (h + 1) * HEAD_DIM)
                h1 = slice(h * HEAD_DIM, h * HEAD_DIM + 1)
                q0, k0, qb, kw, vw, s = _attn_block(q_ref, k_ref, v_ref, n, hs, win)
                dob = do_ref[pl.ds(q0, ATTN_BLOCK), hs]
                p = jnp.exp(s - l_ref[pl.ds(q0, ATTN_BLOCK), h1])
                dp = lax.dot_general(dob, vw, (((1,), (1,)), ((), ())), preferred_element_type=F32)
                ds = (p * (dp - dd_ref[pl.ds(q0, ATTN_BLOCK), h1]) * (HEAD_DIM ** -0.5)).astype(BF16)
                dq_ref[pl.ds(q0, ATTN_BLOCK), hs] = jnp.dot(ds, kw, preferred_element_type=F32)
                dk_ref[pl.ds(k0, win), hs] += lax.dot_general(ds, qb, (((0,), (0,)), ((), ())), preferred_element_type=F32)
                dv_ref[pl.ds(k0, win), hs] += lax.dot_general(p.astype(BF16), dob, (((0,), (0,)), ((), ())),
                                                              preferred_element_type=F32)
            return carry

        lax.fori_loop(0, nb, block, 0)

    q, k, v, o = _attn_specs(sub_len, pairs)
    shp = jax.ShapeDtypeStruct((sub_len, r * D_ATTN), F32)
    return pl.pallas_call(
        body, out_shape=(shp, shp, shp), grid=(r, 4 // pairs), in_specs=[q, k, v, o, o, o], out_specs=(o, o, o),
        name=name, compiler_params=_cp("parallel", "parallel"))(qkv_r, qkv_r, qkv_r, do_r, lse_r, dd_r)


def _rows(start, size, r):
    return pl.ds(start, size) if r == 1 else pl.ds(start, size, stride=r)


def _unit_rows(r, rho, n, nb):
    win = 2 * ATTN_BLOCK if nb > 1 else ATTN_BLOCK
    if isinstance(n, int):
        kb = max(n - 1, 0)
        q_rows = _rows(rho + r * ATTN_BLOCK * n, ATTN_BLOCK, r)
        k_rows = _rows(rho + r * ATTN_BLOCK * kb, win, r)
    else:
        kb = jnp.maximum(n - 1, 0)
        q_rows = pl.ds(pl.multiple_of(n * ATTN_BLOCK, ATTN_BLOCK), ATTN_BLOCK)
        k_rows = pl.ds(pl.multiple_of(kb * ATTN_BLOCK, ATTN_BLOCK), win)
    dist = (n - kb) * ATTN_BLOCK + lax.broadcasted_iota(jnp.int32, (ATTN_BLOCK, win), 0) \
        - lax.broadcasted_iota(jnp.int32, (ATTN_BLOCK, win), 1)
    return q_rows, k_rows, (dist >= 0) & (dist <= ATTN_BLOCK)


def _per_head(x):
    lane = lax.broadcasted_iota(jnp.int32, x.shape, 1)
    zero = jnp.zeros_like(x)
    return [jnp.where(lane < HEAD_DIM, x, zero), jnp.where(lane >= HEAD_DIM, x, zero)]


NT = (((1,), (1,)), ((), ()))


def _masked_scores(q2, k2, valid):
    return [jnp.where(valid, lax.dot_general(qh, k2, NT, preferred_element_type=F32) * (HEAD_DIM ** -0.5), NEG)
            for qh in _per_head(q2)]


def _attn_units(r, nb, unit):
    if r == 1:
        def four(i, carry):
            for k in range(4):
                unit(0, 4 * i + k)
            return carry
        lax.fori_loop(0, nb // 4, four, 0)
    else:
        for rho in range(r):
            for n in range(nb):
                unit(rho, n)


N_UNITS = 16


def attn_fwd_all(proj, name):
    def body(q_ref, k_ref, v_ref, att_ref, lse_ref, s_scr, p_scr, lse_scr, den_scr):
        for idx, (sub_len, r) in enumerate(PATTERNS):
            nb = sub_len // ATTN_BLOCK
            win = 2 * ATTN_BLOCK if nb > 1 else ATTN_BLOCK

            def scores(rho, n, r=r, nb=nb, win=win):
                u = rho * nb + n
                q_rows, k_rows, valid = _unit_rows(r, rho, n, nb)
                ss = _masked_scores(q_ref[q_rows, :].astype(BF16), k_ref[k_rows, :].astype(BF16), valid)
                for h in range(2):
                    s_scr[2 * u + h, :, 0:win] = ss[h]

            _attn_units(r, nb, scores)

            def softmax(u, carry, win=win):
                lses, dens = [], []
                for h in range(2):
                    sc = s_scr[2 * u + h, :, 0:win]
                    m = jnp.max(sc, axis=1, keepdims=True)
                    p = jnp.exp(sc - m)
                    den = jnp.sum(p, axis=1, keepdims=True)
                    p_scr[2 * u + h, :, 0:win] = p.astype(BF16)
                    lses.append(jnp.broadcast_to(m + jnp.log(den), (ATTN_BLOCK, HEAD_DIM)))
                    dens.append(jnp.broadcast_to(den, (ATTN_BLOCK, HEAD_DIM)))
                lse_scr[u] = jnp.concatenate(lses, axis=1)
                den_scr[u] = jnp.concatenate(dens, axis=1)
                return carry

            lax.fori_loop(0, N_UNITS, softmax, 0, unroll=2)

            def outputs(rho, n, r=r, nb=nb, win=win, idx=idx):
                u = rho * nb + n
                q_rows, k_rows, _ = _unit_rows(r, rho, n, nb)
                vs = _per_head(v_ref[k_rows, :].astype(BF16))
                o = (jnp.dot(p_scr[2 * u, :, 0:win], vs[0], preferred_element_type=F32)
                     + jnp.dot(p_scr[2 * u + 1, :, 0:win], vs[1], preferred_element_type=F32)) / den_scr[u]
                lse = lse_scr[u]
                if idx > 0:
                    old = lse_ref[q_rows, :]
                    top = jnp.maximum(old, lse)
                    new = top + jnp.log(jnp.exp(old - top) + jnp.exp(lse - top))
                    o = att_ref[q_rows, :] * jnp.exp(old - new) + o * jnp.exp(lse - new)
                    lse = new
                att_ref[q_rows, :] = o
                lse_ref[q_rows, :] = lse

            _attn_units(r, nb, outputs)

    blk = lambda first: pl.BlockSpec((SEQ, 128), lambda g: (0, first + g))
    shp = jax.ShapeDtypeStruct((SEQ, D_ATTN), F32)
    big = (2 * N_UNITS, ATTN_BLOCK, 2 * ATTN_BLOCK)
    small = pltpu.VMEM((N_UNITS, ATTN_BLOCK, 128), F32)
    return pl.pallas_call(
        body, out_shape=(shp, shp), grid=(4,), in_specs=[blk(8), blk(12), blk(16)], out_specs=(blk(0), blk(0)),
        scratch_shapes=[pltpu.VMEM(big, F32), pltpu.VMEM(big, BF16), small, small],
        name=name, compiler_params=_cp("parallel"))(proj, proj, proj)


def attn_bwd_all(proj, do, lse, dd, name):
    scale = HEAD_DIM ** -0.5

    def body(q_ref, k_ref, v_ref, do_ref, l_ref, dd_ref, out_ref, dq_s, dk_s, dv_s,
             s_scr, dp_scr, ds_scr, st_scr, dpt_scr, pt_scr, dst_scr, qb_scr, kb_scr, dob_scr):
        dq_s[...] = jnp.zeros_like(dq_s)
        dk_s[...] = jnp.zeros_like(dk_s)
        dv_s[...] = jnp.zeros_like(dv_s)
        for sub_len, r in PATTERNS:
            nb = sub_len // ATTN_BLOCK
            win = 2 * ATTN_BLOCK if nb > 1 else ATTN_BLOCK

            def scores(rho, n, r=r, nb=nb, win=win):
                u = rho * nb + n
                q_rows, k_rows, valid = _unit_rows(r, rho, n, nb)
                kb = max(n - 1, 0) if isinstance(n, int) else jnp.maximum(n - 1, 0)
                dist_t = (n - kb) * ATTN_BLOCK + lax.broadcasted_iota(jnp.int32, (win, ATTN_BLOCK), 1) \
                    - lax.broadcasted_iota(jnp.int32, (win, ATTN_BLOCK), 0)
                valid_t = (dist_t >= 0) & (dist_t <= ATTN_BLOCK)
                q2 = q_ref[q_rows, :].astype(BF16)
                k2 = k_ref[k_rows, :].astype(BF16)
                do2 = do_ref[q_rows, :].astype(BF16)
                qb_scr[u] = q2
                kb_scr[u, 0:win, :] = k2
                dob_scr[u] = do2
                l2 = l_ref[q_rows, :]
                d2 = dd_ref[q_rows, :]
                l2t = l2.T
                d2t = d2.T
                v2 = v_ref[k_rows, :].astype(BF16)
                qs, dos = _per_head(q2), _per_head(do2)
                for h in range(2):
                    c0 = h * HEAD_DIM
                    sc = lax.dot_general(qs[h], k2, NT, preferred_element_type=F32) * scale
                    s_scr[2 * u + h, :, 0:win] = jnp.where(valid, sc, NEG) - l2[:, c0:c0 + 1]
                    dp_scr[2 * u + h, :, 0:win] = lax.dot_general(dos[h], v2, NT, preferred_element_type=F32) \
                        - d2[:, c0:c0 + 1]
                    sct = lax.dot_general(k2, qs[h], NT, preferred_element_type=F32) * scale
                    st_scr[2 * u + h, 0:win, :] = jnp.where(valid_t, sct, NEG) - l2t[c0:c0 + 1, :]
                    dpt_scr[2 * u + h, 0:win, :] = lax.dot_general(v2, dos[h], NT, preferred_element_type=F32) \
                        - d2t[c0:c0 + 1, :]

            _attn_units(r, nb, scores)

            def pointwise(hu, carry, win=win):
                ds_scr[hu, :, 0:win] = (jnp.exp(s_scr[hu, :, 0:win]) * dp_scr[hu, :, 0:win] * scale).astype(BF16)
                pt = jnp.exp(st_scr[hu, 0:win, :])
                pt_scr[hu, 0:win, :] = pt.astype(BF16)
                dst_scr[hu, 0:win, :] = (pt * dpt_scr[hu, 0:win, :] * scale).astype(BF16)
                return carry

            lax.fori_loop(0, 2 * N_UNITS, pointwise, 0, unroll=4)

            def grads(rho, n, r=r, nb=nb, win=win):
                u = rho * nb + n
                q_rows, k_rows, _ = _unit_rows(r, rho, n, nb)
                qs, ks, dos = _per_head(qb_scr[u]), _per_head(kb_scr[u, 0:win, :]), _per_head(dob_scr[u])

                def both(scr, rows, rhs):
                    return (jnp.dot(scr[(2 * u,) + rows], rhs[0], preferred_element_type=F32)
                            + jnp.dot(scr[(2 * u + 1,) + rows], rhs[1], preferred_element_type=F32))

                dq_s[q_rows, :] += both(ds_scr, (slice(None), slice(0, win)), ks)
                dk_s[k_rows, :] += both(dst_scr, (slice(0, win), slice(None)), qs)
                dv_s[k_rows, :] += both(pt_scr, (slice(0, win), slice(None)), dos)

            _attn_units(r, nb, grads)
        out_ref[0] = dq_s[...].astype(BF16)
        out_ref[1] = dk_s[...].astype(BF16)
        out_ref[2] = dv_s[...].astype(BF16)

    blk = lambda first: pl.BlockSpec((SEQ, 128), lambda g: (0, first + g))
    acc = pltpu.VMEM((SEQ, 128), F32)
    big = (2 * N_UNITS, ATTN_BLOCK, 2 * ATTN_BLOCK)
    big_t = (2 * N_UNITS, 2 * ATTN_BLOCK, ATTN_BLOCK)
    return pl.pallas_call(
        body, out_shape=jax.ShapeDtypeStruct((3, SEQ, D_ATTN), BF16), grid=(4,),
        in_specs=[blk(8), blk(12), blk(16), blk(0), blk(0), blk(0)],
        out_specs=pl.BlockSpec((3, SEQ, 128), lambda g: (0, 0, g)),
        scratch_shapes=[acc, acc, acc, pltpu.VMEM(big, F32), pltpu.VMEM(big, F32), pltpu.VMEM(big, BF16),
                        pltpu.VMEM(big_t, F32), pltpu.VMEM(big_t, F32), pltpu.VMEM(big_t, BF16), pltpu.VMEM(big_t, BF16),
                        pltpu.VMEM((N_UNITS, ATTN_BLOCK, 128), BF16),
                        pltpu.VMEM((N_UNITS, 2 * ATTN_BLOCK, 128), BF16), pltpu.VMEM((N_UNITS, ATTN_BLOCK, 128), BF16)],
        name=name, compiler_params=_cp("parallel"))(proj, proj, proj, do, lse, dd)


def rms_gain_bf16(a, g, name):
    def body(a_ref, g_ref, o_ref):
        aa = a_ref[...]
        o_ref[...] = (aa * lax.rsqrt(_mean(aa * aa) + EPS) * g_ref[...]).astype(BF16)

    w = a.shape[1]
    return pl.pallas_call(
        body, out_shape=jax.ShapeDtypeStruct(a.shape, BF16), grid=(SEQ // ROWS,), in_specs=[_row_spec(w), _vec_spec(w)],
        out_specs=_row_spec(w), name=name, compiler_params=_cp("parallel"))(a, g)


def attn_combine_fwd(outs, lses, gao, name):
    def body(o1, o2, o3, l1, l2, l3, g_ref, att_ref, lse_ref, mix_ref):
        a1, a2, a3 = l1[...], l2[...], l3[...]
        m = jnp.maximum(jnp.maximum(a1, a2), a3)
        w1, w2, w3 = jnp.exp(a1 - m), jnp.exp(a2 - m), jnp.exp(a3 - m)
        den = w1 + w2 + w3
        att = (w1 * o1[...] + w2 * o2[...] + w3 * o3[...]) / den
        att_ref[...] = att
        lse_ref[...] = m + jnp.log(den)
        mix_ref[...] = (att * lax.rsqrt(_mean(att * att) + EPS) * g_ref[...]).astype(BF16)

    rs = _row_spec(D_ATTN)
    f = jax.ShapeDtypeStruct((SEQ, D_ATTN), F32)
    return pl.pallas_call(
        body, out_shape=(f, f, jax.ShapeDtypeStruct((SEQ, D_ATTN), BF16)), grid=(SEQ // ROWS,),
        in_specs=[rs] * 6 + [_vec_spec(D_ATTN)], out_specs=(rs, rs, rs),
        name=name, compiler_params=_cp("parallel"))(*outs, *lses, gao)


def attn_combine_bwd(dmixed, att, gao, name):
    def body(dm_ref, att_ref, g_ref, do_ref, dd_ref, dg_ref):
        first = pl.program_id(0) == 0
        att = att_ref[...]
        r = lax.rsqrt(_mean(att * att) + EPS)
        xn = att * r
        dm = dm_ref[...]
        _acc(dg_ref, _rsum(dm * xn), first)
        dyn = dm * g_ref[...]
        do = r * (dyn - xn * _mean(dyn * xn))
        do_ref[...] = do
        same_head = (jnp.right_shift(lax.broadcasted_iota(jnp.int32, (D_ATTN, D_ATTN), 0), 6)
                     == jnp.right_shift(lax.broadcasted_iota(jnp.int32, (D_ATTN, D_ATTN), 1), 6)).astype(F32)
        dd_ref[...] = jnp.dot(do * att, same_head, preferred_element_type=F32, precision=lax.Precision.HIGHEST)

    rs = _row_spec(D_ATTN)
    return pl.pallas_call(
        body,
        out_shape=(jax.ShapeDtypeStruct((SEQ, D_ATTN), F32), jax.ShapeDtypeStruct((SEQ, D_ATTN), F32),
                   jax.ShapeDtypeStruct((1, D_ATTN), F32)),
        grid=(SEQ // ROWS,), in_specs=[_row_spec(D_ATTN, 1), rs, _vec_spec(D_ATTN)],
        out_specs=(rs, rs, _vec_spec(D_ATTN)), name=name, compiler_params=_cp("arbitrary"))(dmixed, att, gao)


def sum3_bf16(a, b, c, name):
    def body(a_ref, b_ref, c_ref, o_ref):
        o_ref[...] = (a_ref[...] + b_ref[...] + c_ref[...]).astype(BF16)

    w = a.shape[1]
    rs = _row_spec(w)
    return pl.pallas_call(
        body, out_shape=jax.ShapeDtypeStruct(a.shape, BF16), grid=(SEQ // ROWS,), in_specs=[rs, rs, rs], out_specs=rs,
        name=name, compiler_params=_cp("parallel"))(a, b, c)


N_FT = D_FF // FFN_TN


def _ffn_specs():
    per = ROWS // FFN_HALO
    cur_g = pl.BlockSpec((ROWS, FFN_TN), lambda j, i: (i, j))
    cur_v = pl.BlockSpec((ROWS, FFN_TN), lambda j, i: (i, j + N_FT))
    halo_g = pl.BlockSpec((FFN_HALO, FFN_TN), lambda j, i: (jnp.maximum(i * per - 1, 0), j))
    halo_v = pl.BlockSpec((FFN_HALO, FFN_TN), lambda j, i: (jnp.maximum(i * per - 1, 0), j + N_FT))
    w_g = pl.BlockSpec((FFN_K, FFN_TN), lambda j, i: (0, j))
    w_v = pl.BlockSpec((FFN_K, FFN_TN), lambda j, i: (0, j + N_FT))
    b_g = pl.BlockSpec((1, FFN_TN), lambda j, i: (0, j))
    b_v = pl.BlockSpec((1, FFN_TN), lambda j, i: (0, j + N_FT))
    return [cur_g, cur_v, halo_g, halo_v, w_g, w_v, b_g, b_v]


def matmul(a, b, kind, out_dtype, tm, tn, name, b_rows=None):
    stacked = b_rows is not None
    b_shape = (N_DEV * b_rows, b.shape[2]) if stacked else b.shape
    if kind == "nn":
        (m, k), n = a.shape, b_shape[1]
        a_spec = pl.BlockSpec((tm, k), lambda j, i: (i, 0))
        b_spec = pl.BlockSpec((k, tn), lambda j, i: (0, j))
        dims = (((1,), (0,)), ((), ()))
    elif kind == "nt":
        (m, k), n = a.shape, b_shape[0]
        a_spec = pl.BlockSpec((tm, k), lambda j, i: (i, 0))
        b_spec = pl.BlockSpec((tn, k), lambda j, i: (j, 0))
        dims = (((1,), (1,)), ((), ()))
    else:
        (k, m), n = a.shape, b_shape[1]
        a_spec = pl.BlockSpec((k, tm), lambda j, i: (0, i))
        b_spec = pl.BlockSpec((k, tn), lambda j, i: (0, j))
        dims = (((0,), (0,)), ((), ()))
    assert m % tm == 0 and n % tn == 0, (name, m, n, tm, tn)
    if stacked:
        assert b_spec.block_shape[0] == b_shape[0] and kind in ("nn", "nt")
        width = b_spec.block_shape[1]
        b_spec = pl.BlockSpec((N_DEV, b_rows, width), (lambda j, i: (0, 0, j)) if kind == "nn" else (lambda j, i: (0, 0, 0)))

    def body(a_ref, b_ref, o_ref):
        bb = b_ref[...].reshape(b_shape[0], -1) if stacked else b_ref[...]
        o_ref[...] = lax.dot_general(a_ref[...], bb, dims, preferred_element_type=F32).astype(o_ref.dtype)

    return pl.pallas_call(
        body, out_shape=jax.ShapeDtypeStruct((m, n), out_dtype), grid=(n // tn, m // tm),
        in_specs=[a_spec, b_spec], out_specs=pl.BlockSpec((tm, tn), lambda j, i: (i, j)),
        name=name, compiler_params=_cp("parallel", "parallel"))(a, b)


def matmul_halves(a, b, tm, tn, name):
    _, m, k = a.shape
    n = b.shape[1]
    b3 = b.reshape(2, k, n)

    def body(a_ref, b_ref, o_ref):
        o_ref[...] = (jnp.dot(a_ref[0], b_ref[0], preferred_element_type=F32)
                      + jnp.dot(a_ref[1], b_ref[1], preferred_element_type=F32))

    return pl.pallas_call(
        body, out_shape=jax.ShapeDtypeStruct((m, n), F32), grid=(n // tn, m // tm),
        in_specs=[pl.BlockSpec((2, tm, k), lambda j, i: (0, i, 0)), pl.BlockSpec((2, k, tn), lambda j, i: (0, 0, j))],
        out_specs=pl.BlockSpec((tm, tn), lambda j, i: (i, j)), name=name, compiler_params=_cp("parallel", "parallel"))(a, b3)


def matmul_tn_halves(a, b, tm, name):
    _, k, m = a.shape
    n = b.shape[1]

    def body(a_ref, b_ref, o_ref):
        o_ref[0] = lax.dot_general(a_ref[0], b_ref[...], (((0,), (0,)), ((), ())), preferred_element_type=F32).astype(BF16)

    return pl.pallas_call(
        body, out_shape=jax.ShapeDtypeStruct((2, m, n), BF16), grid=(2, m // tm),
        in_specs=[pl.BlockSpec((1, k, tm), lambda h, i: (h, 0, i)), pl.BlockSpec((k, n), lambda h, i: (0, 0))],
        out_specs=pl.BlockSpec((1, tm, n), lambda h, i: (h, i, 0)), name=name,
        compiler_params=_cp("parallel", "parallel"))(a, b).reshape(2 * m, n)


def _row_spec(width, col=0):
    return pl.BlockSpec((ROWS, width), lambda i: (i, col))


def _vec_spec(width, rows=1):
    return pl.BlockSpec((rows, width), lambda i: (0, 0))


def _ffn_shifted(cur_ref, halo_ref, pad, s1, s2):
    i = pl.program_id(1)
    pad[0:FFN_HALO, :] = jnp.where(i > 0, halo_ref[...], 0.0)
    pad[FFN_HALO:, :] = cur_ref[0:FFN_HALO, :]
    for k, dst in ((1, s1), (2, s2)):
        dst[0:FFN_HALO, :] = pad[pl.ds(FFN_HALO - k, FFN_HALO), :]
        dst[FFN_HALO:, :] = cur_ref[pl.ds(FFN_HALO - k, ROWS - FFN_HALO), :]


def _ffn_conv(rs, cur_ref, s1, s2, w_ref, b_ref):
    return b_ref[...] + w_ref[0:1, :] * s2[rs, :] + w_ref[1:2, :] * s1[rs, :] + w_ref[2:3, :] * cur_ref[rs, :]


def ffn_act_fwd(up0, wf, bf, name):
    def body(g_ref, v_ref, gh_ref, vh_ref, wg_ref, wv_ref, bg_ref, bv_ref, act_ref, pad, g1, g2, v1, v2):
        _ffn_shifted(g_ref, gh_ref, pad, g1, g2)
        _ffn_shifted(v_ref, vh_ref, pad, v1, v2)

        def chunk(rs):
            gate = _ffn_conv(rs, g_ref, g1, g2, wg_ref, bg_ref)
            val = _ffn_conv(rs, v_ref, v1, v2, wv_ref, bv_ref)
            act_ref[rs, :] = (gate * _sig(gate) * val).astype(BF16)

        _for_chunks(chunk)

    tile = pltpu.VMEM((ROWS, FFN_TN), F32)
    return pl.pallas_call(
        body, out_shape=jax.ShapeDtypeStruct((SEQ, D_FF), BF16), grid=(N_FT, SEQ // ROWS),
        in_specs=_ffn_specs(), out_specs=pl.BlockSpec((ROWS, FFN_TN), lambda j, i: (i, j)),
        scratch_shapes=[pltpu.VMEM((2 * FFN_HALO, FFN_TN), F32), tile, tile, tile, tile],
        name=name, compiler_params=_cp("parallel", "parallel"))(up0, up0, up0, up0, wf, wf, bf, bf)


def ffn_bwd(up0, dact, wf, bf, name):
    per = ROWS // FFN_HALO
    last = SEQ // FFN_HALO - 1

    def body(g_ref, v_ref, gh_ref, vh_ref, wg_ref, wv_ref, bg_ref, bv_ref, da_ref, gn_ref, vn_ref, dan_ref,
             out_ref, dbg_ref, dbv_ref, dwg_ref, dwv_ref, pad, g1, g2, v1, v2, dgp, dvp, acc):
        i = pl.program_id(1)
        first = i == 0
        _ffn_shifted(g_ref, gh_ref, pad, g1, g2)
        _ffn_shifted(v_ref, vh_ref, pad, v1, v2)
        acc[...] = jnp.zeros_like(acc)

        def grads(gate, val, da):
            s = _sig(gate)
            return da * val * (s * (1.0 + gate * (1.0 - s))), da * (gate * s)

        def chunk(rs):
            gate = _ffn_conv(rs, g_ref, g1, g2, wg_ref, bg_ref)
            val = _ffn_conv(rs, v_ref, v1, v2, wv_ref, bv_ref)
            dgate, dval = grads(gate, val, da_ref[rs, :])
            dgp[rs, :] = dgate
            dvp[rs, :] = dval
            acc[0] += dgate
            acc[1] += dval
            for t, (sg, sv) in enumerate(((g2, v2), (g1, v1), (g_ref, v_ref))):
                acc[2 + t] += dgate * sg[rs, :]
                acc[5 + t] += dval * sv[rs, :]

        _for_chunks(chunk)
        _acc(dbg_ref, _rsum(acc[0]), first)
        _acc(dbv_ref, _rsum(acc[1]), first)
        _acc(dwg_ref, jnp.concatenate([_rsum(acc[2 + t]) for t in range(FFN_K)], axis=0), first)
        _acc(dwv_ref, jnp.concatenate([_rsum(acc[5 + t]) for t in range(FFN_K)], axis=0), first)

        def conv_next(cur_ref, nxt_ref, w_ref, b_ref):
            pad[0:FFN_HALO, :] = cur_ref[ROWS - FFN_HALO:, :]
            pad[FFN_HALO:, :] = nxt_ref[...]
            return (b_ref[...] + w_ref[0:1, :] * pad[pl.ds(FFN_HALO - 2, FFN_HALO), :]
                    + w_ref[1:2, :] * pad[pl.ds(FFN_HALO - 1, FFN_HALO), :] + w_ref[2:3, :] * nxt_ref[...])

        gate_n = conv_next(g_ref, gn_ref, wg_ref, bg_ref)
        val_n = conv_next(v_ref, vn_ref, wv_ref, bv_ref)
        dgate_n, dval_n = grads(gate_n, val_n, dan_ref[...])
        inside = i < SEQ // ROWS - 1
        dgp[ROWS:, :] = jnp.where(inside, dgate_n, 0.0)
        dvp[ROWS:, :] = jnp.where(inside, dval_n, 0.0)

        for half, (dp, s1, s2, w_ref) in enumerate(((dgp, g1, g2, wg_ref), (dvp, v1, v2, wv_ref))):
            s1[...] = dp[pl.ds(1, ROWS), :]
            s2[...] = dp[pl.ds(2, ROWS), :]

            def back(rs, dp=dp, s1=s1, s2=s2, w_ref=w_ref, half=half):
                out_ref[half, rs, :] = (w_ref[2:3, :] * dp[rs, :] + w_ref[1:2, :] * s1[rs, :]
                                        + w_ref[0:1, :] * s2[rs, :]).astype(BF16)

            _for_chunks(back)

    tile = pltpu.VMEM((ROWS, FFN_TN), F32)
    ext = pltpu.VMEM((ROWS + FFN_HALO, FFN_TN), F32)
    vec = jax.ShapeDtypeStruct((1, D_FF), F32)
    taps = jax.ShapeDtypeStruct((FFN_K, D_FF), F32)
    cur = pl.BlockSpec((ROWS, FFN_TN), lambda j, i: (i, j))
    nxt = lambda off: pl.BlockSpec((FFN_HALO, FFN_TN), lambda j, i: (jnp.minimum((i + 1) * per, last), j + off))
    vs = pl.BlockSpec((1, FFN_TN), lambda j, i: (0, j))
    ts = pl.BlockSpec((FFN_K, FFN_TN), lambda j, i: (0, j))
    return pl.pallas_call(
        body, out_shape=(jax.ShapeDtypeStruct((2, SEQ, D_FF), BF16), vec, vec, taps, taps), grid=(N_FT, SEQ // ROWS),
        in_specs=_ffn_specs() + [cur, nxt(0), nxt(N_FT), nxt(0)],
        out_specs=(pl.BlockSpec((2, ROWS, FFN_TN), lambda j, i: (0, i, j)), vs, vs, ts, ts),
        scratch_shapes=[pltpu.VMEM((2 * FFN_HALO, FFN_TN), F32), tile, tile, tile, tile, ext, ext,
                        pltpu.VMEM((2 + 2 * FFN_K, SUB, FFN_TN), F32)],
        name=name, compiler_params=_cp("parallel", "arbitrary"))(up0, up0, up0, up0, wf, wf, bf, bf, dact, up0, up0, dact)


def ada_fwd(c_all, w_ada, b_cols, name):
    def body(c_ref, w_ref, b_ref, o_ref):
        cc = c_ref[...]
        sc = (cc * _sig(cc)).astype(BF16)
        o_ref[...] = jnp.dot(sc, w_ref[...].astype(BF16), preferred_element_type=F32) + b_ref[...]

    return pl.pallas_call(body, out_shape=jax.ShapeDtypeStruct((N_DEV, w_ada.shape[1]), F32), name=name,
                          compiler_params=_cp())(c_all, w_ada, b_cols)


def _adam(w, g, m, v):
    m = ADAM_B1 * m + (1.0 - ADAM_B1) * g
    v = ADAM_B2 * v + (1.0 - ADAM_B2) * (g * g)
    m_hat = m / (1.0 - ADAM_B1 ** ADAM_STEP)
    v_hat = v / (1.0 - ADAM_B2 ** ADAM_STEP)
    delta = -ADAM_LR * (m_hat / (jnp.sqrt(v_hat) + ADAM_EPS) + ADAM_WD * w)
    return delta, m, v


def ada_bwd_adamw(c_all_t, dmod_cols, w, m, v, name):
    rows, cols = w.shape
    tr = 256

    def body(ct_ref, dm_ref, w_ref, m_ref, v_ref, g_ref, d_ref, nm_ref, nv_ref):
        def chunk(rs):
            ct = ct_ref[rs, :]
            sc = ct * _sig(ct)
            g = sc[:, 0:1] * dm_ref[0:1, :]
            for b in range(1, N_DEV):
                g = g + sc[:, b:b + 1] * dm_ref[b:b + 1, :]
            g_ref[rs, :] = g
            d_ref[rs, :], nm_ref[rs, :], nv_ref[rs, :] = _adam(w_ref[rs, :], g, m_ref[rs, :], v_ref[rs, :])

        _for_chunks(chunk, 2, tr)

    blk = pl.BlockSpec((tr, cols), lambda i: (i, 0))
    shp = jax.ShapeDtypeStruct((rows, cols), F32)
    return pl.pallas_call(
        body, out_shape=(shp, shp, shp, shp), grid=(rows // tr,),
        in_specs=[pl.BlockSpec((tr, N_DEV), lambda i: (i, 0)), pl.BlockSpec((N_DEV, cols), lambda i: (0, 0)), blk, blk, blk],
        out_specs=(blk, blk, blk, blk), name=name, compiler_params=_cp("parallel"))(c_all_t, dmod_cols, w, m, v)


def sum_adamw(parts, own, w, m, v, tr, name):
    n_parts, rows, cols = parts.shape

    def body(*refs):
        if own is None:
            p_ref, w_ref, m_ref, v_ref, g_ref, d_ref, nm_ref, nv_ref = refs
        else:
            p_ref, own_ref, w_ref, m_ref, v_ref, g_ref, d_ref, nm_ref, nv_ref = refs

        def chunk(rs):
            g = (p_ref[0, rs, :] if own is None else own_ref[rs, :]).astype(F32)
            for k in range(1, n_parts):
                g = g + p_ref[k, rs, :].astype(F32)
            g_ref[rs, :] = g
            d_ref[rs, :], nm_ref[rs, :], nv_ref[rs, :] = _adam(w_ref[rs, :], g, m_ref[rs, :], v_ref[rs, :])

        _for_chunks(chunk, 2 if tr > SUB else 1, tr)

    blk = pl.BlockSpec((tr, cols), lambda i: (i, 0))
    shp = jax.ShapeDtypeStruct((rows, cols), F32)
    args = [parts] + ([] if own is None else [own]) + [w, m, v]
    return pl.pallas_call(
        body, out_shape=(shp, shp, shp, shp), grid=(rows // tr,),
        in_specs=[pl.BlockSpec((n_parts, tr, cols), lambda i: (0, i, 0))] + [blk] * (len(args) - 1),
        out_specs=(blk, blk, blk, blk), name=name, compiler_params=_cp("parallel"))(*args)


MESH = pl.DeviceIdType.MESH
ANY = pl.BlockSpec(memory_space=pl.ANY)


def all_gather(block, name, after=None):
    extra = () if after is None else (after,)

    def body(x_ref, *refs):
        out_ref, send_sems, recv_sems, local_sem = refs[len(extra):]
        x, y, c = lax.axis_index("x"), lax.axis_index("y"), lax.axis_index("c")
        me, sibling = (x, y, c), (x, y, 1 - c)
        chips = [(1 - x, y), (x, 1 - y), (1 - x, 1 - y)]

        def slot(px, py, pc):
            return out_ref.at[4 * px + 2 * py + pc]

        def copy(k, blk, to, src=None):
            return pltpu.make_async_remote_copy(
                src_ref=slot(*blk) if src is None else src, dst_ref=slot(*blk),
                send_sem=send_sems.at[k], recv_sem=recv_sems.at[k], device_id=to, device_id_type=MESH)

        mine = pltpu.make_async_copy(x_ref, slot(*me), local_sem)
        mine.start()
        first = [copy(0, me, sibling, src=x_ref)]
        first += [copy(1 + j, me, (*chip, c), src=x_ref) for j, chip in enumerate(chips)]
        for cp in first:
            cp.start()
        passed = [copy(4 + j, (*chip, c), sibling) for j, chip in enumerate(chips)]
        for j, chip in enumerate(chips):
            copy(1 + j, (*chip, c), me).wait_recv()
            passed[j].start()
        copy(0, sibling, me).wait_recv()
        for j, chip in enumerate(chips):
            copy(4 + j, (*chip, 1 - c), me).wait_recv()
        for cp in first + passed:
            cp.wait_send()
        mine.wait()

    return pl.pallas_call(
        body, out_shape=jax.ShapeDtypeStruct((N_DEV,) + block.shape, block.dtype), in_specs=[ANY] * (1 + len(extra)), out_specs=ANY,
        scratch_shapes=[pltpu.SemaphoreType.DMA((7,)), pltpu.SemaphoreType.DMA((7,)), pltpu.SemaphoreType.DMA],
        name=name)(block, *extra)


HBM = pl.BlockSpec(memory_space=pltpu.HBM)
SEM = pl.BlockSpec(memory_space=pltpu.SEMAPHORE)
EFFECT = pltpu.SideEffectType.DATAFLOW_SIDE_EFFECTING


def _peer_copies(src_ref, land_ref, send_sems, recv_sems, gather):
    x, y, c = lax.axis_index("x"), lax.axis_index("y"), lax.axis_index("c")
    me = 4 * x + 2 * y + c
    copies = []
    for k in range(1, N_DEV):
        px = 1 - x if k & 4 else x
        py = 1 - y if k & 2 else y
        pc = 1 - c if k & 1 else c
        copies.append(pltpu.make_async_remote_copy(
            src_ref=src_ref if gather else src_ref.at[4 * px + 2 * py + pc],
            dst_ref=land_ref.at[me] if gather else land_ref.at[k],
            send_sem=send_sems.at[k - 1], recv_sem=recv_sems.at[k - 1], device_id=(px, py, pc), device_id_type=MESH))
    return copies


def exchange_start(srcs, gather, name, after=None):
    n = len(srcs)
    land_shapes = [(N_DEV,) + src.shape if gather else src.shape for src in srcs]
    extra = () if after is None else (after,)

    def body(*refs):
        src_refs, land_refs = refs[0:n], refs[n:2 * n]
        outs = refs[2 * n + len(extra):]
        for k in range(n):
            for cp in _peer_copies(src_refs[k], land_refs[k], outs[4 * k], outs[4 * k + 1], gather):
                cp.start()
        token = outs[4 * n]
        token[...] = jnp.zeros_like(token)

    out_shape, out_specs, aliases = [], [], {}
    for k, src in enumerate(srcs):
        out_shape += [pltpu.SemaphoreType.DMA((N_DEV - 1,)), pltpu.SemaphoreType.DMA((N_DEV - 1,)),
                      pltpu.HBM(src.shape, src.dtype), pltpu.HBM(land_shapes[k], src.dtype)]
        out_specs += [SEM, SEM, HBM, HBM]
        aliases[k] = 4 * k + 2
        aliases[n + k] = 4 * k + 3
    out_shape.append(jax.ShapeDtypeStruct((8, 128), F32))
    out_specs.append(pl.BlockSpec(memory_space=pltpu.VMEM))
    res = pl.pallas_call(
        body, name=name, out_shape=tuple(out_shape), in_specs=(HBM,) * (2 * n) + (ANY,) * len(extra),
        out_specs=tuple(out_specs), input_output_aliases=aliases,
        compiler_params=pltpu.CompilerParams(has_side_effects=EFFECT),
    )(*[pltpu.with_memory_space_constraint(src, pltpu.HBM) for src in srcs],
      *[pltpu.with_memory_space_constraint(lax.empty(shp, src.dtype), pltpu.HBM) for shp, src in zip(land_shapes, srcs)],
      *extra)
    return [tuple(res[4 * k:4 * k + 4]) for k in range(n)], res[4 * n][0, 0]


def exchange_wait(handles, after, gather, name):
    send_sems, recv_sems, src_thru, land_thru = handles

    def body(src_ref, land_ref, send_sems, recv_sems, after_ref, src_dead, got_ref):
        for cp in _peer_copies(src_ref, land_ref, send_sems, recv_sems, gather):
            cp.wait_send()
            cp.wait_recv()

    return pl.pallas_call(
        body, name=name,
        out_shape=(pltpu.HBM(src_thru.shape, src_thru.dtype), pltpu.HBM(land_thru.shape, land_thru.dtype)),
        in_specs=(HBM, HBM, SEM, SEM, ANY), out_specs=(HBM, HBM), input_output_aliases={0: 0, 1: 1},
        compiler_params=pltpu.CompilerParams(has_side_effects=EFFECT),
    )(src_thru, land_thru, send_sems, recv_sems, after)[1]


def _to_pattern(a, r):
    return a.reshape(SEQ // r, r * a.shape[1])


def local_step(x, tgt, mod, get_w, put_grad, wc, wf, g_mix, bc, lg, lb, gco, gao, g_ffn, bf, g_fin):
    h1 = rms_mod_fwd(x, g_mix, mod, 0, 1, "h1_fwd")
    w_in = get_w("w_in", h1)
    proj = matmul(h1, w_in, "nt", F32, 512, D_IN, "proj_fwd", b_rows=D_IN // N_DEV)
    mix_a = conv_module_fwd(proj, wc, bc, lg, lb, gco, "conv_module_fwd")
    att, lse = attn_fwd_all(proj, "attn_fwd")
    mix_b = rms_gain_bf16(att, gao, "attn_out_norm")
    mixed = jnp.concatenate([mix_a, mix_b], axis=1)
    w_out = get_w("w_out", mixed)
    y1 = matmul(mixed, w_out, "nn", F32, 512, D_MODEL, "out_proj_fwd")
    x1, h2 = resid_rms_mod_fwd(x, y1, g_ffn, mod, 2, 3, 4, "x1_h2_fwd")
    w_up = get_w("w_up", h2)
    up0 = matmul(h2, w_up, "nt", F32, 512, D_FF, "up_fwd")
    act = ffn_act_fwd(up0, wf, bf, "ffn_act_fwd")
    w_down = get_w("w_down", act)
    y2 = matmul(act, w_down, "nn", F32, 512, D_MODEL, "down_fwd")
    loss_t, dx2, dy2, d_gfin, d_gaf = final_loss_bwd(x1, y2, tgt, g_fin, mod, 5, "loss_bwd")
    dact = matmul(dy2, w_down, "nt", F32, 512, FFN_TN, "down_bwd_x")
    dw_down = matmul(act, dy2, "tn", BF16, 256, D_MODEL, "down_bwd_w")
    dup0, dbf_g, dbf_v, dwf_g, dwf_v = ffn_bwd(up0, dact, wf + put_grad("w_down", dw_down), bf, "ffn_bwd")
    dh2 = matmul_halves(dup0, w_up, 512, 512, "up_bwd_x")
    dw_up = matmul_tn_halves(dup0, h2, 256, "up_bwd_w")
    dx1, d_shf, d_scf, d_gffn, dy1, d_gam = rms_mod_bwd(x1, dh2, dx2, g_ffn + put_grad("w_up", dw_up), mod, 4, y1, 2, "h2_bwd")
    dmixed = matmul(dy1, w_out, "nt", F32, 512, D_MODEL, "out_proj_bwd_x")
    dw_out = matmul(mixed, dy1, "tn", BF16, 256, D_MODEL, "out_proj_bwd_w")
    do, dd, d_gao = attn_combine_bwd(dmixed, att, gao + put_grad("w_out", dw_out), "attn_combine_bwd")
    dqkv = attn_bwd_all(proj, do, lse, dd, "attn_bwd")
    du1, d_gco, d_lg, d_lb, d_bc, d_wc = conv_module_bwd_a(proj, dmixed, wc, bc, lg, lb, gco, "conv_module_bwd_a")
    dproj_a = conv_module_bwd_b(proj, du1, wc, "conv_module_bwd_b")
    dproj = jnp.concatenate([dproj_a, dqkv[0], dqkv[1], dqkv[2]], axis=1)
    dw_in = matmul(dproj, h1, "tn", BF16, 512, D_MODEL, "proj_bwd_w")
    dh1 = matmul(dproj, w_in, "nn", F32, 512, D_MODEL, "proj_bwd_x", b_rows=D_IN // N_DEV)
    dx, d_shm, d_scm, d_gmix = rms_mod_bwd(x, dh1, dx1, g_mix + put_grad("w_in", dw_in), mod, 1, None, 0, "h1_bwd")
    dmod = jnp.concatenate([d_shm, d_scm, d_gam, d_shf, d_scf, d_gaf], axis=1)
    small = dict(g_norm_mix=d_gmix, b_conv_dw=d_bc, ln_conv_g=d_lg, ln_conv_b=d_lb, g_conv_out=d_gco, g_attn_out=d_gao,
                 g_norm_ffn=d_gffn, b_ffn_dw=jnp.concatenate([dbf_g, dbf_v], axis=1), g_final=d_gfin,
                 w_conv_dw=d_wc, w_ffn_dw=jnp.concatenate([dwf_g, dwf_v], axis=1), dmod=dmod, loss=loss_t[0:1, 0:1])
    return dx, small


def _padw(a, width):
    return jnp.pad(a, ((0, 0), (0, width - a.shape[1])))


def pack_small(t):
    wide = jnp.concatenate([_padw(t["dmod"], PACK_W), _padw(t["b_ffn_dw"], PACK_W), _padw(t["w_ffn_dw"], PACK_W),
                            _padw(t["loss"], PACK_W), jnp.zeros((2, PACK_W), F32)], axis=0)
    z512 = jnp.zeros((1, 512), F32)
    narrow = jnp.concatenate([
        t["g_norm_mix"], t["g_norm_ffn"], t["g_final"],
        jnp.concatenate([t["b_conv_dw"], t["ln_conv_g"]], axis=1),
        jnp.concatenate([t["ln_conv_b"], t["g_conv_out"]], axis=1),
        jnp.concatenate([t["g_attn_out"], z512], axis=1),
        jnp.zeros((2, 1024), F32),
        jnp.pad(t["w_conv_dw"], ((0, 1), (0, 0))).reshape(16, 1024)], axis=0)
    return jnp.concatenate([wide, narrow.reshape(4, PACK_W), jnp.zeros((4, PACK_W), F32)], axis=0)


def unpack_small(p):
    narrow = p[8:12].reshape(24, 1024)
    return dict(
        dmod=p[0:1], b_ffn_dw=p[1:2, :2 * D_FF], w_ffn_dw=p[2:5, :2 * D_FF], loss=p[5, 0],
        g_norm_mix=narrow[0:1], g_norm_ffn=narrow[1:2], g_final=narrow[2:3],
        b_conv_dw=narrow[3:4, :512], ln_conv_g=narrow[3:4, 512:], ln_conv_b=narrow[4:5, :512], g_conv_out=narrow[4:5, 512:],
        g_attn_out=narrow[5:6, :512], w_conv_dw=narrow[8:24].reshape(32, 512)[:CONV_K])


def _embed(local, width, me):
    return lax.dynamic_update_slice(jnp.zeros((local.shape[0], width), F32), local, (0, me * local.shape[1]))


def _shard(full, n_cols, me):
    return lax.dynamic_slice(full, (0, me * n_cols), (full.shape[0], n_cols))


WEIGHTS = ["w_ada", "b_ada", "g_norm_mix", "w_in", "w_conv_dw", "b_conv_dw", "ln_conv_g", "ln_conv_b", "g_conv_out",
           "g_attn_out", "w_out", "g_norm_ffn", "w_up", "w_ffn_dw", "b_ffn_dw", "w_down", "g_final"]
SMALL_REPLICATED = ["g_norm_mix", "b_conv_dw", "ln_conv_g", "ln_conv_b", "g_conv_out", "g_attn_out", "g_norm_ffn",
                    "b_ffn_dw", "g_final"]


def kernel(x, c, w_ada, b_ada, g_norm_mix, w_in, w_conv_dw, b_conv_dw, ln_conv_g, ln_conv_b, g_conv_out, g_attn_out, w_out, g_norm_ffn, w_up, w_ffn_dw, b_ffn_dw, w_down, g_final, loss_target, m_w_ada, m_b_ada, m_g_norm_mix, m_w_in, m_w_conv_dw, m_b_conv_dw, m_ln_conv_g, m_ln_conv_b, m_g_conv_out, m_g_attn_out, m_w_out, m_g_norm_ffn, m_w_up, m_w_ffn_dw, m_b_ffn_dw, m_w_down, m_g_final, v_w_ada, v_b_ada, v_g_norm_mix, v_w_in, v_w_conv_dw, v_b_conv_dw, v_ln_conv_g, v_ln_conv_b, v_g_conv_out, v_g_attn_out, v_w_out, v_g_norm_ffn, v_w_up, v_w_ffn_dw, v_b_ffn_dw, v_w_down, v_g_final):
    args = dict(locals())
    me = 4 * lax.axis_index("x") + 2 * lax.axis_index("y") + lax.axis_index("c")

    def flat(name, prefix=""):
        a = args[prefix + name]
        return a.reshape(a.shape[-2] if a.ndim > 1 else 1, a.shape[-1])

    def flat_t(name, prefix=""):
        return args[prefix + name][0].T

    n_in, n_up, r_out, r_down = w_in.shape[2], w_up.shape[2], w_out.shape[1], w_down.shape[1]
    n_ada, n_wc, n_wf = w_ada.shape[2], w_conv_dw.shape[2], w_ffn_dw.shape[2]
    taps_c = jnp.pad(flat("w_conv_dw").reshape(1, CONV_K * n_wc), ((0, 0), (0, 2 * D_MODEL - CONV_K * n_wc)))
    taps_f = jnp.pad(flat("w_ffn_dw").reshape(1, FFN_K * n_wf), ((0, 0), (0, 3 * D_MODEL - FFN_K * n_wf)))
    first = jnp.concatenate([c, taps_c.reshape(2, D_MODEL), taps_f.reshape(3, D_MODEL), jnp.zeros((2, D_MODEL), F32)], axis=0)
    w_in_block = flat_t("w_in").astype(BF16)
    first_bits = lax.bitcast_convert_type(first, BF16).reshape(16, D_MODEL)
    first_block = all_gather(jnp.concatenate([w_in_block, first_bits], axis=0), "gather_c_taps_w_in")
    first_all = lax.bitcast_convert_type(first_block[:, n_in:, :].reshape(N_DEV, 8, D_MODEL, 2), F32)
    c_all = first_all[:, 0, :]
    wc_full = first_all[:, 1:3, :].reshape(N_DEV, 2 * D_MODEL)[:, :CONV_K * n_wc].reshape(N_DEV, CONV_K, n_wc)
    wc_full = wc_full.transpose(1, 0, 2).reshape(CONV_K, D_CONV)
    wf_full = first_all[:, 3:6, :].reshape(N_DEV, 3 * D_MODEL)[:, :FFN_K * n_wf].reshape(N_DEV, FFN_K, n_wf)
    wf_full = wf_full.transpose(1, 0, 2).reshape(FFN_K, 2 * D_FF)
    mod_cols = ada_fwd(c_all, flat("w_ada"), _shard(flat("b_ada"), n_ada, me), "ada_fwd")
    mod_all = all_gather(mod_cols, "gather_mod")
    mod = lax.dynamic_index_in_dim(mod_all, me, axis=1, keepdims=False).reshape(N_MOD, D_MODEL)
    mod = jnp.pad(mod, ((0, 2), (0, 0)))

    order = ("w_out", "w_up", "w_down")
    blocks = dict(w_up=flat_t("w_up").astype(BF16), w_out=flat("w_out").astype(BF16), w_down=flat("w_down").astype(BF16))
    handles, tok = exchange_start([blocks[name] for name in order], True, "gather_weights_start", mod_all)
    gathers = dict(zip(order, handles))
    mod = mod + tok

    def gathered(name, after):
        land = exchange_wait(gathers[name], after, True, f"gather_{name}_wait")
        return lax.dynamic_update_index_in_dim(land, blocks[name], me, axis=0)

    def get_w(name, after):
        if name == "w_in":
            return first_block
        return gathered(name, after).reshape(-1, D_MODEL)

    exchanges, own = {}, {}

    def put_grad(name, dw, after=None):
        dev_major = dw.reshape(N_DEV, -1, D_MODEL)
        own[name] = lax.dynamic_index_in_dim(dev_major, me, axis=0, keepdims=False)
        (exchanges[name],), token = exchange_start([dev_major], False, f"exchange_{name}_start", after)
        return token

    grad_x, small = local_step(
        x[0], loss_target[0], mod, get_w, put_grad, wc_full, wf_full,
        flat("g_norm_mix"), flat("b_conv_dw"), flat("ln_conv_g"), flat("ln_conv_b"), flat("g_conv_out"),
        flat("g_attn_out"), flat("g_norm_ffn"), flat("b_ffn_dw"), flat("g_final"))

    out = {}

    def finish(name, tr, after):
        parts = exchange_wait(exchanges[name], after, False, f"exchange_{name}_wait")
        if name in ("w_in", "w_up"):
            res = sum_adamw(parts, own[name], flat_t(name), flat_t(name, "m_"), flat_t(name, "v_"), tr, "adamw_" + name)
            out[name] = tuple(r.T for r in res)
        else:
            res = out[name] = sum_adamw(parts, own[name], flat(name), flat(name, "m_"), flat(name, "v_"), tr, "adamw_" + name)
        return res[0]

    after = finish("w_down", r_down, grad_x)
    after = finish("w_up", n_up // 2, after)
    after = finish("w_out", r_out, after)
    after = finish("w_in", n_in, after)

    small_all = all_gather(pack_small(small), "gather_small", after)

    def packed(prefix):
        t = {n: flat(n, prefix) for n in SMALL_REPLICATED}
        t["dmod"] = flat("b_ada", prefix)
        t["loss"] = jnp.zeros((1, 1), F32)
        t["w_conv_dw"] = _embed(flat("w_conv_dw", prefix), D_CONV, me)
        t["w_ffn_dw"] = _embed(flat("w_ffn_dw", prefix), 2 * D_FF, me)
        return pack_small(t)

    res = sum_adamw(small_all, None, packed(""), packed("m_"), packed("v_"), PACK_ROWS, "adamw_small")
    res = [unpack_small(r) for r in res]
    loss = res[0]["loss"]
    for n in SMALL_REPLICATED:
        out[n] = tuple(r[n] for r in res)
    out["b_ada"] = tuple(r["dmod"] for r in res)
    out["w_conv_dw"] = tuple(_shard(r["w_conv_dw"], n_wc, me) for r in res)
    out["w_ffn_dw"] = tuple(_shard(r["w_ffn_dw"], n_wf, me) for r in res)

    dmod_cols = _shard(small_all[:, 0, :], n_ada, me)
    out["w_ada"] = ada_bwd_adamw(c_all.T, dmod_cols, flat("w_ada"), flat("w_ada", "m_"), flat("w_ada", "v_"), "adamw_w_ada")

    result = [loss, grad_x[None]]
    for k in range(4):
        result += [out[n][k].reshape(args[n].shape) for n in WEIGHTS]
    return tuple(result)
```

```python
import functools

import jax
import jax.numpy as jnp
from jax import lax
from jax.experimental import pallas as pl
from jax.experimental.pallas import tpu as pltpu

F32 = jnp.float32
BF16 = jnp.bfloat16

N_DEV = 8
SEQ = 2048
D_MODEL = 1024
D_CONV = 512
D_ATTN = 512
HEAD_DIM = 64
CONV_K = 31
D_FF = 2816
FFN_K = 3
D_IN = 2 * D_CONV + 3 * D_ATTN
N_MOD = 6
EPS = 1e-6
ATTN_BLOCK = 128
PATTERNS = ((2048, 1), (512, 4), (128, 16))
NEG = -1e30

ADAM_LR, ADAM_B1, ADAM_B2, ADAM_EPS, ADAM_WD, ADAM_STEP = 0.001, 0.9, 0.999, 1e-08, 0.01, 10

ROWS = 256
CONV_HALO = 32
FFN_HALO = 8
FFN_TN = 1408
VMEM_LIMIT = 56 * 1024 * 1024
PACK_ROWS, PACK_W = 16, 6144


def _cp(*sem):
    return pltpu.CompilerParams(dimension_semantics=sem if sem else None, vmem_limit_bytes=VMEM_LIMIT)


def _sig(x):
    return 1.0 / (1.0 + jnp.exp(-x))


def _rsum(x):
    return jnp.sum(x, axis=0, keepdims=True)


def _mean(x):
    return jnp.mean(x, axis=-1, keepdims=True)


def _acc(ref, val, first):
    @pl.when(first)
    def _():
        ref[...] = val

    @pl.when(jnp.logical_not(first))
    def _():
        ref[...] += val


SUB = 16


def _for_chunks(fn, unroll=1, rows=ROWS):
    def step(i, carry):
        fn(pl.ds(pl.multiple_of(i * SUB, SUB), SUB))
        return carry

    lax.fori_loop(0, rows // SUB, step, 0, unroll=unroll)


def rms_mod_fwd(x, g, mod, sh_row, sc_row, name):
    def body(x_ref, g_ref, mod_ref, h_ref):
        xx = x_ref[...]
        r = lax.rsqrt(_mean(xx * xx) + EPS)
        h = xx * r * g_ref[...]
        h_ref[...] = (h * (1.0 + mod_ref[sc_row:sc_row + 1, :]) + mod_ref[sh_row:sh_row + 1, :]).astype(BF16)

    return pl.pallas_call(
        body, out_shape=jax.ShapeDtypeStruct((SEQ, D_MODEL), BF16), grid=(SEQ // ROWS,),
        in_specs=[_row_spec(D_MODEL), _vec_spec(D_MODEL), _vec_spec(D_MODEL, 8)],
        out_specs=_row_spec(D_MODEL), name=name, compiler_params=_cp("parallel"))(x, g, mod)


def resid_rms_mod_fwd(x, y, g, mod, ga_row, sh_row, sc_row, name):
    def body(x_ref, y_ref, g_ref, mod_ref, x1_ref, h_ref):
        x1 = x_ref[...] + mod_ref[ga_row:ga_row + 1, :] * y_ref[...]
        x1_ref[...] = x1
        r = lax.rsqrt(_mean(x1 * x1) + EPS)
        h = x1 * r * g_ref[...]
        h_ref[...] = (h * (1.0 + mod_ref[sc_row:sc_row + 1, :]) + mod_ref[sh_row:sh_row + 1, :]).astype(BF16)

    return pl.pallas_call(
        body, out_shape=(jax.ShapeDtypeStruct((SEQ, D_MODEL), F32), jax.ShapeDtypeStruct((SEQ, D_MODEL), BF16)),
        grid=(SEQ // ROWS,),
        in_specs=[_row_spec(D_MODEL), _row_spec(D_MODEL), _vec_spec(D_MODEL), _vec_spec(D_MODEL, 8)],
        out_specs=(_row_spec(D_MODEL), _row_spec(D_MODEL)), name=name, compiler_params=_cp("parallel"))(x, y, g, mod)


def final_loss_bwd(x1, y2, tgt, g, mod, ga_row, name):
    def body(x1_ref, y2_ref, t_ref, g_ref, mod_ref, loss_ref, dx2_ref, dy2_ref, dg_ref, dga_ref):
        first = pl.program_id(0) == 0
        ga = mod_ref[ga_row:ga_row + 1, :]
        y2 = y2_ref[...]
        x2 = x1_ref[...] + ga * y2
        r = lax.rsqrt(_mean(x2 * x2) + EPS)
        xn = x2 * r
        err = xn * g_ref[...] - t_ref[...]
        _acc(loss_ref, jnp.broadcast_to(0.5 * jnp.sum(_mean(err * err)), (8, 128)), first)
        dy = err * (1.0 / D_MODEL)
        _acc(dg_ref, _rsum(dy * xn), first)
        dxn = dy * g_ref[...]
        dx2 = r * (dxn - xn * _mean(dxn * xn))
        dx2_ref[...] = dx2
        dy2_ref[...] = (dx2 * ga).astype(BF16)
        _acc(dga_ref, _rsum(dx2 * y2), first)

    vec = jax.ShapeDtypeStruct((1, D_MODEL), F32)
    return pl.pallas_call(
        body,
        out_shape=(jax.ShapeDtypeStruct((8, 128), F32), jax.ShapeDtypeStruct((SEQ, D_MODEL), F32),
                   jax.ShapeDtypeStruct((SEQ, D_MODEL), BF16), vec, vec),
        grid=(SEQ // ROWS,),
        in_specs=[_row_spec(D_MODEL), _row_spec(D_MODEL), _row_spec(D_MODEL), _vec_spec(D_MODEL), _vec_spec(D_MODEL, 8)],
        out_specs=(pl.BlockSpec((8, 128), lambda i: (0, 0)), _row_spec(D_MODEL), _row_spec(D_MODEL),
                   _vec_spec(D_MODEL), _vec_spec(D_MODEL)),
        name=name, compiler_params=_cp("arbitrary"))(x1, y2, tgt, g, mod)


def rms_mod_bwd(x, dh, dres, g, mod, sc_row, y, ga_row, name):
    gated = y is not None

    def body(*refs):
        if gated:
            x_ref, dh_ref, dres_ref, g_ref, mod_ref, y_ref, dx_ref, dsh_ref, dsc_ref, dg_ref, dy_ref, dga_ref = refs
        else:
            x_ref, dh_ref, dres_ref, g_ref, mod_ref, dx_ref, dsh_ref, dsc_ref, dg_ref = refs
        first = pl.program_id(0) == 0
        xx = x_ref[...]
        dh = dh_ref[...]
        gg = g_ref[...]
        r = lax.rsqrt(_mean(xx * xx) + EPS)
        xn = xx * r
        _acc(dsh_ref, _rsum(dh), first)
        _acc(dsc_ref, _rsum(dh * (xn * gg)), first)
        dt = dh * (1.0 + mod_ref[sc_row:sc_row + 1, :])
        _acc(dg_ref, _rsum(dt * xn), first)
        dxn = dt * gg
        dx = dres_ref[...] + r * (dxn - xn * _mean(dxn * xn))
        dx_ref[...] = dx
        if gated:
            _acc(dga_ref, _rsum(dx * y_ref[...]), first)
            dy_ref[...] = (dx * mod_ref[ga_row:ga_row + 1, :]).astype(BF16)

    vec = jax.ShapeDtypeStruct((1, D_MODEL), F32)
    in_specs = [_row_spec(D_MODEL), _row_spec(D_MODEL), _row_spec(D_MODEL), _vec_spec(D_MODEL), _vec_spec(D_MODEL, 8)]
    out_shape = [jax.ShapeDtypeStruct((SEQ, D_MODEL), F32), vec, vec, vec]
    out_specs = [_row_spec(D_MODEL), _vec_spec(D_MODEL), _vec_spec(D_MODEL), _vec_spec(D_MODEL)]
    args = [x, dh, dres, g, mod]
    if gated:
        in_specs.append(_row_spec(D_MODEL))
        out_shape += [jax.ShapeDtypeStruct((SEQ, D_MODEL), BF16), vec]
        out_specs += [_row_spec(D_MODEL), _vec_spec(D_MODEL)]
        args.append(y)
    return pl.pallas_call(
        body, out_shape=tuple(out_shape), grid=(SEQ // ROWS,), in_specs=in_specs, out_specs=tuple(out_specs),
        name=name, compiler_params=_cp("arbitrary"))(*args)


def _prev_halo(halo, width, col):
    per = ROWS // halo
    return pl.BlockSpec((halo, width), lambda i: (jnp.maximum(i * per - 1, 0), col))


def _next_halo(halo, width, col):
    per = ROWS // halo
    last = SEQ // halo - 1
    return pl.BlockSpec((halo, width), lambda i: (jnp.minimum((i + 1) * per, last), col))


CONV_PAD = ROWS + CONV_HALO


def _shift_copies(sh):
    for b in range(1, 8):
        sh[b, 0:CONV_PAD - 8, :] = sh[0, pl.ds(b, CONV_PAD - 8), :]


def _tap(sh, rs_start, offset):
    return sh[offset % 8, pl.ds(pl.multiple_of(rs_start + (offset // 8) * 8, 8), SUB), :]


def _conv_module_forward(av_ref, ag_ref, avh_ref, agh_ref, wc_ref, bc_ref, lg_ref, lb_ref, sh, u1_s):
    i = pl.program_id(0)
    hv = avh_ref[...] * _sig(agh_ref[...])
    sh[0, 0:CONV_HALO, :] = jnp.where(i > 0, hv, 0.0)

    def glu(rs):
        sh[0, pl.ds(pl.multiple_of(rs.start + CONV_HALO, SUB), SUB), :] = av_ref[rs, :] * _sig(ag_ref[rs, :])

    _for_chunks(glu)
    _shift_copies(sh)

    def conv(rs):
        u1 = jnp.broadcast_to(bc_ref[...], (SUB, D_CONV))
        for j in range(CONV_K):
            u1 = u1 + wc_ref[j:j + 1, :] * _tap(sh, rs.start, CONV_HALO - (CONV_K - 1) + j)
        u1_s[rs, :] = u1

    _for_chunks(conv)
    u1 = u1_s[...]
    mu = _mean(u1)
    cen = u1 - mu
    rs = lax.rsqrt(_mean(cen * cen) + EPS)
    z = cen * rs
    ln = z * lg_ref[...] + lb_ref[...]
    s = _sig(ln)
    return z, rs, ln, s, ln * s


_CONV_SCRATCH = [pltpu.VMEM((8, CONV_PAD, D_CONV), F32), pltpu.VMEM((ROWS, D_CONV), F32)]


def conv_module_fwd(proj, wc, bc, lg, lb, gco, name):
    def body(av_ref, ag_ref, avh_ref, agh_ref, wc_ref, bc_ref, lg_ref, lb_ref, gco_ref, out_ref, sh, u1_s):
        _, _, _, _, u2 = _conv_module_forward(av_ref, ag_ref, avh_ref, agh_ref, wc_ref, bc_ref, lg_ref, lb_ref, sh, u1_s)
        rc = lax.rsqrt(_mean(u2 * u2) + EPS)
        out_ref[...] = (u2 * rc * gco_ref[...]).astype(BF16)

    v = _vec_spec(D_CONV)
    return pl.pallas_call(
        body, out_shape=jax.ShapeDtypeStruct((SEQ, D_CONV), BF16), grid=(SEQ // ROWS,),
        in_specs=[_row_spec(D_CONV, 0), _row_spec(D_CONV, 1), _prev_halo(CONV_HALO, D_CONV, 0),
                  _prev_halo(CONV_HALO, D_CONV, 1), _vec_spec(D_CONV, CONV_K), v, v, v, v],
        out_specs=_row_spec(D_CONV), scratch_shapes=list(_CONV_SCRATCH),
        name=name, compiler_params=_cp("parallel"))(proj, proj, proj, proj, wc, bc, lg, lb, gco)


def conv_module_bwd_a(proj, dmixed, wc, bc, lg, lb, gco, name):
    def body(av_ref, ag_ref, avh_ref, agh_ref, dm_ref, wc_ref, bc_ref, lg_ref, lb_ref, gco_ref,
             du1_ref, dgco_ref, dlg_ref, dlb_ref, dbc_ref, dwc_ref, sh, u1_s, acc):
        first = pl.program_id(0) == 0
        z, rs, ln, s, u2 = _conv_module_forward(av_ref, ag_ref, avh_ref, agh_ref, wc_ref, bc_ref, lg_ref, lb_ref, sh, u1_s)
        rc = lax.rsqrt(_mean(u2 * u2) + EPS)
        xn = u2 * rc
        dm = dm_ref[...]
        _acc(dgco_ref, _rsum(dm * xn), first)
        dyn = dm * gco_ref[...]
        du2 = rc * (dyn - xn * _mean(dyn * xn))
        dln = du2 * (s * (1.0 + ln * (1.0 - s)))
        _acc(dlg_ref, _rsum(dln * z), first)
        _acc(dlb_ref, _rsum(dln), first)
        dz = dln * lg_ref[...]
        du1 = rs * (dz - _mean(dz) - z * _mean(dz * z))
        du1_ref[...] = du1
        _acc(dbc_ref, _rsum(du1), first)
        acc[...] = jnp.zeros_like(acc)

        def taps(rs):
            d = du1_ref[rs, :]
            for j in range(CONV_K):
                acc[j] += d * _tap(sh, rs.start, CONV_HALO - (CONV_K - 1) + j)

        _for_chunks(taps)

        @pl.when(first)
        def _():
            dwc_ref[...] = jnp.zeros_like(dwc_ref)

        for j in range(CONV_K):
            dwc_ref[j:j + 1, :] += _rsum(acc[j])

    v = _vec_spec(D_CONV)
    vec = jax.ShapeDtypeStruct((1, D_CONV), F32)
    return pl.pallas_call(
        body,
        out_shape=(jax.ShapeDtypeStruct((SEQ, D_CONV), F32), vec, vec, vec, vec, jax.ShapeDtypeStruct((CONV_K, D_CONV), F32)),
        grid=(SEQ // ROWS,),
        in_specs=[_row_spec(D_CONV, 0), _row_spec(D_CONV, 1), _prev_halo(CONV_HALO, D_CONV, 0),
                  _prev_halo(CONV_HALO, D_CONV, 1), _row_spec(D_CONV, 0), _vec_spec(D_CONV, CONV_K), v, v, v, v],
        out_specs=(_row_spec(D_CONV), v, v, v, v, _vec_spec(D_CONV, CONV_K)),
        scratch_shapes=list(_CONV_SCRATCH) + [pltpu.VMEM((CONV_K, SUB, D_CONV), F32)],
        name=name, compiler_params=_cp("arbitrary"))(proj, proj, proj, proj, dmixed, wc, bc, lg, lb, gco)


def conv_module_bwd_b(proj, du1, wc, name):
    def body(av_ref, ag_ref, du1_ref, du1n_ref, wc_ref, out_ref, sh):
        i = pl.program_id(0)
        sh[0, 0:ROWS, :] = du1_ref[...]
        sh[0, ROWS:, :] = jnp.where(i < SEQ // ROWS - 1, du1n_ref[...], 0.0)
        _shift_copies(sh)

        def chunk(rs):
            du0 = jnp.zeros((SUB, D_CONV), F32)
            for j in range(CONV_K):
                du0 = du0 + wc_ref[j:j + 1, :] * _tap(sh, rs.start, CONV_K - 1 - j)
            sg = _sig(ag_ref[rs, :])
            out_ref[rs, 0:D_CONV] = (du0 * sg).astype(BF16)
            out_ref[rs, D_CONV:] = (du0 * av_ref[rs, :] * sg * (1.0 - sg)).astype(BF16)

        _for_chunks(chunk)

    return pl.pallas_call(
        body, out_shape=jax.ShapeDtypeStruct((SEQ, 2 * D_CONV), BF16), grid=(SEQ // ROWS,),
        in_specs=[_row_spec(D_CONV, 0), _row_spec(D_CONV, 1), _row_spec(D_CONV, 0), _next_halo(CONV_HALO, D_CONV, 0),
                  _vec_spec(D_CONV, CONV_K)],
        out_specs=_row_spec(2 * D_CONV), scratch_shapes=[pltpu.VMEM((8, CONV_PAD, D_CONV), F32)],
        name=name, compiler_params=_cp("parallel"))(proj, proj, du1, du1, wc)


def _attn_specs(sub_len, pairs):
    ng, width = 4 // pairs, 128 * pairs
    q = pl.BlockSpec((sub_len, width), lambda rho, g: (0, rho * 3 * ng + g))
    k = pl.BlockSpec((sub_len, width), lambda rho, g: (0, rho * 3 * ng + ng + g))
    v = pl.BlockSpec((sub_len, width), lambda rho, g: (0, rho * 3 * ng + 2 * ng + g))
    o = pl.BlockSpec((sub_len, width), lambda rho, g: (0, rho * ng + g))
    return q, k, v, o


ATTN_PAIRS = {1: 1, 4: 1, 16: 4}


def _attn_block(q_ref, k_ref, v_ref, n, hs, win):
    q0 = pl.multiple_of(n * ATTN_BLOCK, ATTN_BLOCK)
    k0 = pl.multiple_of(jnp.maximum(n - 1, 0) * ATTN_BLOCK, ATTN_BLOCK)
    qb = q_ref[pl.ds(q0, ATTN_BLOCK), hs]
    kw = k_ref[pl.ds(k0, win), hs]
    vw = v_ref[pl.ds(k0, win), hs]
    s = lax.dot_general(qb, kw, (((1,), (1,)), ((), ())), preferred_element_type=F32) * (HEAD_DIM ** -0.5)
    dist = (q0 - k0) + lax.broadcasted_iota(jnp.int32, (ATTN_BLOCK, win), 0) \
        - lax.broadcasted_iota(jnp.int32, (ATTN_BLOCK, win), 1)
    s = jnp.where((dist >= 0) & (dist <= ATTN_BLOCK), s, NEG)
    return q0, k0, qb, kw, vw, s


def attn_fwd(qkv_r, sub_len, r, name):
    nb = sub_len // ATTN_BLOCK
    win = 2 * ATTN_BLOCK if nb > 1 else ATTN_BLOCK
    pairs = ATTN_PAIRS[r]

    def body(q_ref, k_ref, v_ref, o_ref, l_ref):
        def block(n, carry):
            for h in range(2 * pairs):
                hs = slice(h * HEAD_DIM, (h + 1) * HEAD_DIM)
                q0, _, _, _, vw, s = _attn_block(q_ref, k_ref, v_ref, n, hs, win)
                m = jnp.max(s, axis=1, keepdims=True)
                p = jnp.exp(s - m)
                den = jnp.sum(p, axis=1, keepdims=True)
                o = jnp.dot(p.astype(BF16), vw, preferred_element_type=F32) / den
                o_ref[pl.ds(q0, ATTN_BLOCK), hs] = o
                l_ref[pl.ds(q0, ATTN_BLOCK), hs] = jnp.broadcast_to(m + jnp.log(den), (ATTN_BLOCK, HEAD_DIM))
            return carry

        lax.fori_loop(0, nb, block, 0, unroll=min(nb, 2))

    q, k, v, o = _attn_specs(sub_len, pairs)
    shp = jax.ShapeDtypeStruct((sub_len, r * D_ATTN), F32)
    return pl.pallas_call(
        body, out_shape=(shp, shp), grid=(r, 4 // pairs), in_specs=[q, k, v], out_specs=(o, o),
        name=name, compiler_params=_cp("parallel", "parallel"))(qkv_r, qkv_r, qkv_r)


def attn_bwd(qkv_r, do_r, lse_r, dd_r, sub_len, r, name):
    nb = sub_len // ATTN_BLOCK
    win = 2 * ATTN_BLOCK if nb > 1 else ATTN_BLOCK
    pairs = ATTN_PAIRS[r]

    def body(q_ref, k_ref, v_ref, do_ref, l_ref, dd_ref, dq_ref, dk_ref, dv_ref):
        dk_ref[...] = jnp.zeros_like(dk_ref)
        dv_ref[...] = jnp.zeros_like(dv_ref)

        def block(n, carry):
            for h in range(2 * pairs):
                hs = slice(h * HEAD_DIM, (h + 1) * HEAD_DIM)
                h1 = slice(h * HEAD_DIM, h * HEAD_DIM + 1)
                q0, k0, qb, kw, vw, s = _attn_block(q_ref, k_ref, v_ref, n, hs, win)
                dob = do_ref[pl.ds(q0, ATTN_BLOCK), hs]
                p = jnp.exp(s - l_ref[pl.ds(q0, ATTN_BLOCK), h1])
                dp = lax.dot_general(dob, vw, (((1,), (1,)), ((), ())), preferred_element_type=F32)
                ds = (p * (dp - dd_ref[pl.ds(q0, ATTN_BLOCK), h1]) * (HEAD_DIM ** -0.5)).astype(BF16)
                dq_ref[pl.ds(q0, ATTN_BLOCK), hs] = jnp.dot(ds, kw, preferred_element_type=F32)
                dk_ref[pl.ds(k0, win), hs] += lax.dot_general(ds, qb, (((0,), (0,)), ((), ())), preferred_element_type=F32)
                dv_ref[pl.ds(k0, win), hs] += lax.dot_general(p.astype(BF16), dob, (((0,), (0,)), ((), ())),
                                                              preferred_element_type=F32)
            return carry

        lax.fori_loop(0, nb, block, 0)

    q, k, v, o = _attn_specs(sub_len, pairs)
    shp = jax.ShapeDtypeStruct((sub_len, r * D_ATTN), F32)
    return pl.pallas_call(
        body, out_shape=(shp, shp, shp), grid=(r, 4 // pairs), in_specs=[q, k, v, o, o, o], out_specs=(o, o, o),
        name=name, compiler_params=_cp("parallel", "parallel"))(qkv_r, qkv_r, qkv_r, do_r, lse_r, dd_r)


def _rows(start, size, r):
    return pl.ds(start, size) if r == 1 else pl.ds(start, size, stride=r)


def _unit_rows(r, rho, n, nb):
    win = 2 * ATTN_BLOCK if nb > 1 else ATTN_BLOCK
    if isinstance(n, int):
        kb = max(n - 1, 0)
        q_rows = _rows(rho + r * ATTN_BLOCK * n, ATTN_BLOCK, r)
        k_rows = _rows(rho + r * ATTN_BLOCK * kb, win, r)
    else:
        kb = jnp.maximum(n - 1, 0)
        q_rows = pl.ds(pl.multiple_of(n * ATTN_BLOCK, ATTN_BLOCK), ATTN_BLOCK)
        k_rows = pl.ds(pl.multiple_of(kb * ATTN_BLOCK, ATTN_BLOCK), win)
    dist = (n - kb) * ATTN_BLOCK + lax.broadcasted_iota(jnp.int32, (ATTN_BLOCK, win), 0) \
        - lax.broadcasted_iota(jnp.int32, (ATTN_BLOCK, win), 1)
    return q_rows, k_rows, (dist >= 0) & (dist <= ATTN_BLOCK)


def _per_head(x):
    lane = lax.broadcasted_iota(jnp.int32, x.shape, 1)
    zero = jnp.zeros_like(x)
    return [jnp.where(lane < HEAD_DIM, x, zero), jnp.where(lane >= HEAD_DIM, x, zero)]


NT = (((1,), (1,)), ((), ()))


def _masked_scores(q2, k2, valid):
    return [jnp.where(valid, lax.dot_general(qh, k2, NT, preferred_element_type=F32) * (HEAD_DIM ** -0.5), NEG)
            for qh in _per_head(q2)]


def _attn_units(r, nb, unit):
    if r == 1:
        def four(i, carry):
            for k in range(4):
                unit(0, 4 * i + k)
            return carry
        lax.fori_loop(0, nb // 4, four, 0)
    else:
        for rho in range(r):
            for n in range(nb):
                unit(rho, n)


N_UNITS = 16


def attn_fwd_all(proj, name):
    def body(q_ref, k_ref, v_ref, att_ref, lse_ref, s_scr, p_scr, lse_scr, den_scr):
        for idx, (sub_len, r) in enumerate(PATTERNS):
            nb = sub_len // ATTN_BLOCK
            win = 2 * ATTN_BLOCK if nb > 1 else ATTN_BLOCK

            def scores(rho, n, r=r, nb=nb, win=win):
                u = rho * nb + n
                q_rows, k_rows, valid = _unit_rows(r, rho, n, nb)
                ss = _masked_scores(q_ref[q_rows, :].astype(BF16), k_ref[k_rows, :].astype(BF16), valid)
                for h in range(2):
                    s_scr[2 * u + h, :, 0:win] = ss[h]

            _attn_units(r, nb, scores)

            def softmax(u, carry, win=win):
                lses, dens = [], []
                for h in range(2):
                    sc = s_scr[2 * u + h, :, 0:win]
                    m = jnp.max(sc, axis=1, keepdims=True)
                    p = jnp.exp(sc - m)
                    den = jnp.sum(p, axis=1, keepdims=True)
                    p_scr[2 * u + h, :, 0:win] = p.astype(BF16)
                    lses.append(jnp.broadcast_to(m + jnp.log(den), (ATTN_BLOCK, HEAD_DIM)))
                    dens.append(jnp.broadcast_to(den, (ATTN_BLOCK, HEAD_DIM)))
                lse_scr[u] = jnp.concatenate(lses, axis=1)
                den_scr[u] = jnp.concatenate(dens, axis=1)
                return carry

            lax.fori_loop(0, N_UNITS, softmax, 0, unroll=2)

            def outputs(rho, n, r=r, nb=nb, win=win, idx=idx):
                u = rho * nb + n
                q_rows, k_rows, _ = _unit_rows(r, rho, n, nb)
                vs = _per_head(v_ref[k_rows, :].astype(BF16))
                o = (jnp.dot(p_scr[2 * u, :, 0:win], vs[0], preferred_element_type=F32)
                     + jnp.dot(p_scr[2 * u + 1, :, 0:win], vs[1], preferred_element_type=F32)) / den_scr[u]
                lse = lse_scr[u]
                if idx > 0:
                    old = lse_ref[q_rows, :]
                    top = jnp.maximum(old, lse)
                    new = top + jnp.log(jnp.exp(old - top) + jnp.exp(lse - top))
                    o = att_ref[q_rows, :] * jnp.exp(old - new) + o * jnp.exp(lse - new)
                    lse = new
                att_ref[q_rows, :] = o
                lse_ref[q_rows, :] = lse

            _attn_units(r, nb, outputs)

    blk = lambda first: pl.BlockSpec((SEQ, 128), lambda g: (0, first + g))
    shp = jax.ShapeDtypeStruct((SEQ, D_ATTN), F32)
    big = (2 * N_UNITS, ATTN_BLOCK, 2 * ATTN_BLOCK)
    small = pltpu.VMEM((N_UNITS, ATTN_BLOCK, 128), F32)
    return pl.pallas_call(
        body, out_shape=(shp, shp), grid=(4,), in_specs=[blk(8), blk(12), blk(16)], out_specs=(blk(0), blk(0)),
        scratch_shapes=[pltpu.VMEM(big, F32), pltpu.VMEM(big, BF16), small, small],
        name=name, compiler_params=_cp("parallel"))(proj, proj, proj)


def attn_bwd_all(proj, do, lse, dd, name):
    scale = HEAD_DIM ** -0.5

    def body(q_ref, k_ref, v_ref, do_ref, l_ref, dd_ref, out_ref, dq_s, dk_s, dv_s,
             s_scr, dp_scr, ds_scr, st_scr, dpt_scr, pt_scr, dst_scr, qb_scr, kb_scr, dob_scr):
        dq_s[...] = jnp.zeros_like(dq_s)
        dk_s[...] = jnp.zeros_like(dk_s)
        dv_s[...] = jnp.zeros_like(dv_s)
        for sub_len, r in PATTERNS:
            nb = sub_len // ATTN_BLOCK
            win = 2 * ATTN_BLOCK if nb > 1 else ATTN_BLOCK

            def scores(rho, n, r=r, nb=nb, win=win):
                u = rho * nb + n
                q_rows, k_rows, valid = _unit_rows(r, rho, n, nb)
                kb = max(n - 1, 0) if isinstance(n, int) else jnp.maximum(n - 1, 0)
                dist_t = (n - kb) * ATTN_BLOCK + lax.broadcasted_iota(jnp.int32, (win, ATTN_BLOCK), 1) \
                    - lax.broadcasted_iota(jnp.int32, (win, ATTN_BLOCK), 0)
                valid_t = (dist_t >= 0) & (dist_t <= ATTN_BLOCK)
                q2 = q_ref[q_rows, :].astype(BF16)
                k2 = k_ref[k_rows, :].astype(BF16)
                do2 = do_ref[q_rows, :].astype(BF16)
                qb_scr[u] = q2
                kb_scr[u, 0:win, :] = k2
                dob_scr[u] = do2
                l2 = l_ref[q_rows, :]
                d2 = dd_ref[q_rows, :]
                l2t = l2.T
                d2t = d2.T
                v2 = v_ref[k_rows, :].astype(BF16)
                qs, dos = _per_head(q2), _per_head(do2)
                for h in range(2):
                    c0 = h * HEAD_DIM
                    sc = lax.dot_general(qs[h], k2, NT, preferred_element_type=F32) * scale
                    s_scr[2 * u + h, :, 0:win] = jnp.where(valid, sc, NEG) - l2[:, c0:c0 + 1]
                    dp_scr[2 * u + h, :, 0:win] = lax.dot_general(dos[h], v2, NT, preferred_element_type=F32) \
                        - d2[:, c0:c0 + 1]
                    sct = lax.dot_general(k2, qs[h], NT, preferred_element_type=F32) * scale
                    st_scr[2 * u + h, 0:win, :] = jnp.where(valid_t, sct, NEG) - l2t[c0:c0 + 1, :]
                    dpt_scr[2 * u + h, 0:win, :] = lax.dot_general(v2, dos[h], NT, preferred_element_type=F32) \
                        - d2t[c0:c0 + 1, :]

            _attn_units(r, nb, scores)

            def pointwise(hu, carry, win=win):
                ds_scr[hu, :, 0:win] = (jnp.exp(s_scr[hu, :, 0:win]) * dp_scr[hu, :, 0:win] * scale).astype(BF16)
                pt = jnp.exp(st_scr[hu, 0:win, :])
                pt_scr[hu, 0:win, :] = pt.astype(BF16)
                dst_scr[hu, 0:win, :] = (pt * dpt_scr[hu, 0:win, :] * scale).astype(BF16)
                return carry

            lax.fori_loop(0, 2 * N_UNITS, pointwise, 0, unroll=4)

            def grads(rho, n, r=r, nb=nb, win=win):
                u = rho * nb + n
                q_rows, k_rows, _ = _unit_rows(r, rho, n, nb)
                qs, ks, dos = _per_head(qb_scr[u]), _per_head(kb_scr[u, 0:win, :]), _per_head(dob_scr[u])

                def both(scr, rows, rhs):
                    return (jnp.dot(scr[(2 * u,) + rows], rhs[0], preferred_element_type=F32)
                            + jnp.dot(scr[(2 * u + 1,) + rows], rhs[1], preferred_element_type=F32))

                dq_s[q_rows, :] += both(ds_scr, (slice(None), slice(0, win)), ks)
                dk_s[k_rows, :] += both(dst_scr, (slice(0, win), slice(None)), qs)
                dv_s[k_rows, :] += both(pt_scr, (slice(0, win), slice(None)), dos)

            _attn_units(r, nb, grads)
        out_ref[0] = dq_s[...].astype(BF16)
        out_ref[1] = dk_s[...].astype(BF16)
        out_ref[2] = dv_s[...].astype(BF16)

    blk = lambda first: pl.BlockSpec((SEQ, 128), lambda g: (0, first + g))
    acc = pltpu.VMEM((SEQ, 128), F32)
    big = (2 * N_UNITS, ATTN_BLOCK, 2 * ATTN_BLOCK)
    big_t = (2 * N_UNITS, 2 * ATTN_BLOCK, ATTN_BLOCK)
    return pl.pallas_call(
        body, out_shape=jax.ShapeDtypeStruct((3, SEQ, D_ATTN), BF16), grid=(4,),
        in_specs=[blk(8), blk(12), blk(16), blk(0), blk(0), blk(0)],
        out_specs=pl.BlockSpec((3, SEQ, 128), lambda g: (0, 0, g)),
        scratch_shapes=[acc, acc, acc, pltpu.VMEM(big, F32), pltpu.VMEM(big, F32), pltpu.VMEM(big, BF16),
                        pltpu.VMEM(big_t, F32), pltpu.VMEM(big_t, F32), pltpu.VMEM(big_t, BF16), pltpu.VMEM(big_t, BF16),
                        pltpu.VMEM((N_UNITS, ATTN_BLOCK, 128), BF16),
                        pltpu.VMEM((N_UNITS, 2 * ATTN_BLOCK, 128), BF16), pltpu.VMEM((N_UNITS, ATTN_BLOCK, 128), BF16)],
        name=name, compiler_params=_cp("parallel"))(proj, proj, proj, do, lse, dd)


def rms_gain_bf16(a, g, name):
    def body(a_ref, g_ref, o_ref):
        aa = a_ref[...]
        o_ref[...] = (aa * lax.rsqrt(_mean(aa * aa) + EPS) * g_ref[...]).astype(BF16)

    w = a.shape[1]
    return pl.pallas_call(
        body, out_shape=jax.ShapeDtypeStruct(a.shape, BF16), grid=(SEQ // ROWS,), in_specs=[_row_spec(w), _vec_spec(w)],
        out_specs=_row_spec(w), name=name, compiler_params=_cp("parallel"))(a, g)


def attn_combine_fwd(outs, lses, gao, name):
    def body(o1, o2, o3, l1, l2, l3, g_ref, att_ref, lse_ref, mix_ref):
        a1, a2, a3 = l1[...], l2[...], l3[...]
        m = jnp.maximum(jnp.maximum(a1, a2), a3)
        w1, w2, w3 = jnp.exp(a1 - m), jnp.exp(a2 - m), jnp.exp(a3 - m)
        den = w1 + w2 + w3
        att = (w1 * o1[...] + w2 * o2[...] + w3 * o3[...]) / den
        att_ref[...] = att
        lse_ref[...] = m + jnp.log(den)
        mix_ref[...] = (att * lax.rsqrt(_mean(att * att) + EPS) * g_ref[...]).astype(BF16)

    rs = _row_spec(D_ATTN)
    f = jax.ShapeDtypeStruct((SEQ, D_ATTN), F32)
    return pl.pallas_call(
        body, out_shape=(f, f, jax.ShapeDtypeStruct((SEQ, D_ATTN), BF16)), grid=(SEQ // ROWS,),
        in_specs=[rs] * 6 + [_vec_spec(D_ATTN)], out_specs=(rs, rs, rs),
        name=name, compiler_params=_cp("parallel"))(*outs, *lses, gao)


def attn_combine_bwd(dmixed, att, gao, name):
    def body(dm_ref, att_ref, g_ref, do_ref, dd_ref, dg_ref):
        first = pl.program_id(0) == 0
        att = att_ref[...]
        r = lax.rsqrt(_mean(att * att) + EPS)
        xn = att * r
        dm = dm_ref[...]
        _acc(dg_ref, _rsum(dm * xn), first)
        dyn = dm * g_ref[...]
        do = r * (dyn - xn * _mean(dyn * xn))
        do_ref[...] = do
        same_head = (jnp.right_shift(lax.broadcasted_iota(jnp.int32, (D_ATTN, D_ATTN), 0), 6)
                     == jnp.right_shift(lax.broadcasted_iota(jnp.int32, (D_ATTN, D_ATTN), 1), 6)).astype(F32)
        dd_ref[...] = jnp.dot(do * att, same_head, preferred_element_type=F32, precision=lax.Precision.HIGHEST)

    rs = _row_spec(D_ATTN)
    return pl.pallas_call(
        body,
        out_shape=(jax.ShapeDtypeStruct((SEQ, D_ATTN), F32), jax.ShapeDtypeStruct((SEQ, D_ATTN), F32),
                   jax.ShapeDtypeStruct((1, D_ATTN), F32)),
        grid=(SEQ // ROWS,), in_specs=[_row_spec(D_ATTN, 1), rs, _vec_spec(D_ATTN)],
        out_specs=(rs, rs, _vec_spec(D_ATTN)), name=name, compiler_params=_cp("arbitrary"))(dmixed, att, gao)


def sum3_bf16(a, b, c, name):
    def body(a_ref, b_ref, c_ref, o_ref):
        o_ref[...] = (a_ref[...] + b_ref[...] + c_ref[...]).astype(BF16)

    w = a.shape[1]
    rs = _row_spec(w)
    return pl.pallas_call(
        body, out_shape=jax.ShapeDtypeStruct(a.shape, BF16), grid=(SEQ // ROWS,), in_specs=[rs, rs, rs], out_specs=rs,
        name=name, compiler_params=_cp("parallel"))(a, b, c)


N_FT = D_FF // FFN_TN


def _ffn_specs():
    per = ROWS // FFN_HALO
    cur_g = pl.BlockSpec((ROWS, FFN_TN), lambda j, i: (i, j))
    cur_v = pl.BlockSpec((ROWS, FFN_TN), lambda j, i: (i, j + N_FT))
    halo_g = pl.BlockSpec((FFN_HALO, FFN_TN), lambda j, i: (jnp.maximum(i * per - 1, 0), j))
    halo_v = pl.BlockSpec((FFN_HALO, FFN_TN), lambda j, i: (jnp.maximum(i * per - 1, 0), j + N_FT))
    w_g = pl.BlockSpec((FFN_K, FFN_TN), lambda j, i: (0, j))
    w_v = pl.BlockSpec((FFN_K, FFN_TN), lambda j, i: (0, j + N_FT))
    b_g = pl.BlockSpec((1, FFN_TN), lambda j, i: (0, j))
    b_v = pl.BlockSpec((1, FFN_TN), lambda j, i: (0, j + N_FT))
    return [cur_g, cur_v, halo_g, halo_v, w_g, w_v, b_g, b_v]


def matmul(a, b, kind, out_dtype, tm, tn, name, b_rows=None):
    stacked = b_rows is not None
    b_shape = (N_DEV * b_rows, b.shape[2]) if stacked else b.shape
    if kind == "nn":
        (m, k), n = a.shape, b_shape[1]
        a_spec = pl.BlockSpec((tm, k), lambda j, i: (i, 0))
        b_spec = pl.BlockSpec((k, tn), lambda j, i: (0, j))
        dims = (((1,), (0,)), ((), ()))
    elif kind == "nt":
        (m, k), n = a.shape, b_shape[0]
        a_spec = pl.BlockSpec((tm, k), lambda j, i: (i, 0))
        b_spec = pl.BlockSpec((tn, k), lambda j, i: (j, 0))
        dims = (((1,), (1,)), ((), ()))
    else:
        (k, m), n = a.shape, b_shape[1]
        a_spec = pl.BlockSpec((k, tm), lambda j, i: (0, i))
        b_spec = pl.BlockSpec((k, tn), lambda j, i: (0, j))
        dims = (((0,), (0,)), ((), ()))
    assert m % tm == 0 and n % tn == 0, (name, m, n, tm, tn)
    if stacked:
        assert b_spec.block_shape[0] == b_shape[0] and kind in ("nn", "nt")
        width = b_spec.block_shape[1]
        b_spec = pl.BlockSpec((N_DEV, b_rows, width), (lambda j, i: (0, 0, j)) if kind == "nn" else (lambda j, i: (0, 0, 0)))

    def body(a_ref, b_ref, o_ref):
        bb = b_ref[...].reshape(b_shape[0], -1) if stacked else b_ref[...]
        o_ref[...] = lax.dot_general(a_ref[...], bb, dims, preferred_element_type=F32).astype(o_ref.dtype)

    return pl.pallas_call(
        body, out_shape=jax.ShapeDtypeStruct((m, n), out_dtype), grid=(n // tn, m // tm),
        in_specs=[a_spec, b_spec], out_specs=pl.BlockSpec((tm, tn), lambda j, i: (i, j)),
        name=name, compiler_params=_cp("parallel", "parallel"))(a, b)


def matmul_halves(a, b, tm, tn, name):
    _, m, k = a.shape
    n = b.shape[1]
    b3 = b.reshape(2, k, n)

    def body(a_ref, b_ref, o_ref):
        o_ref[...] = (jnp.dot(a_ref[0], b_ref[0], preferred_element_type=F32)
                      + jnp.dot(a_ref[1], b_ref[1], preferred_element_type=F32))

    return pl.pallas_call(
        body, out_shape=jax.ShapeDtypeStruct((m, n), F32), grid=(n // tn, m // tm),
        in_specs=[pl.BlockSpec((2, tm, k), lambda j, i: (0, i, 0)), pl.BlockSpec((2, k, tn), lambda j, i: (0, 0, j))],
        out_specs=pl.BlockSpec((tm, tn), lambda j, i: (i, j)), name=name, compiler_params=_cp("parallel", "parallel"))(a, b3)


def matmul_tn_halves(a, b, tm, name):
    _, k, m = a.shape
    n = b.shape[1]

    def body(a_ref, b_ref, o_ref):
        o_ref[0] = lax.dot_general(a_ref[0], b_ref[...], (((0,), (0,)), ((), ())), preferred_element_type=F32).astype(BF16)

    return pl.pallas_call(
        body, out_shape=jax.ShapeDtypeStruct((2, m, n), BF16), grid=(2, m // tm),
        in_specs=[pl.BlockSpec((1, k, tm), lambda h, i: (h, 0, i)), pl.BlockSpec((k, n), lambda h, i: (0, 0))],
        out_specs=pl.BlockSpec((1, tm, n), lambda h, i: (h, i, 0)), name=name,
        compiler_params=_cp("parallel", "parallel"))(a, b).reshape(2 * m, n)


def _row_spec(width, col=0):
    return pl.BlockSpec((ROWS, width), lambda i: (i, col))


def _vec_spec(width, rows=1):
    return pl.BlockSpec((rows, width), lambda i: (0, 0))


def _ffn_shifted(cur_ref, halo_ref, pad, s1, s2):
    i = pl.program_id(1)
    pad[0:FFN_HALO, :] = jnp.where(i > 0, halo_ref[...], 0.0)
    pad[FFN_HALO:, :] = cur_ref[0:FFN_HALO, :]
    for k, dst in ((1, s1), (2, s2)):
        dst[0:FFN_HALO, :] = pad[pl.ds(FFN_HALO - k, FFN_HALO), :]
        dst[FFN_HALO:, :] = cur_ref[pl.ds(FFN_HALO - k, ROWS - FFN_HALO), :]


def _ffn_conv(rs, cur_ref, s1, s2, w_ref, b_ref):
    return b_ref[...] + w_ref[0:1, :] * s2[rs, :] + w_ref[1:2, :] * s1[rs, :] + w_ref[2:3, :] * cur_ref[rs, :]


def ffn_act_fwd(up0, wf, bf, name):
    def body(g_ref, v_ref, gh_ref, vh_ref, wg_ref, wv_ref, bg_ref, bv_ref, act_ref, pad, g1, g2, v1, v2):
        _ffn_shifted(g_ref, gh_ref, pad, g1, g2)
        _ffn_shifted(v_ref, vh_ref, pad, v1, v2)

        def chunk(rs):
            gate = _ffn_conv(rs, g_ref, g1, g2, wg_ref, bg_ref)
            val = _ffn_conv(rs, v_ref, v1, v2, wv_ref, bv_ref)
            act_ref[rs, :] = (gate * _sig(gate) * val).astype(BF16)

        _for_chunks(chunk)

    tile = pltpu.VMEM((ROWS, FFN_TN), F32)
    return pl.pallas_call(
        body, out_shape=jax.ShapeDtypeStruct((SEQ, D_FF), BF16), grid=(N_FT, SEQ // ROWS),
        in_specs=_ffn_specs(), out_specs=pl.BlockSpec((ROWS, FFN_TN), lambda j, i: (i, j)),
        scratch_shapes=[pltpu.VMEM((2 * FFN_HALO, FFN_TN), F32), tile, tile, tile, tile],
        name=name, compiler_params=_cp("parallel", "parallel"))(up0, up0, up0, up0, wf, wf, bf, bf)


def ffn_bwd(up0, dact, wf, bf, name):
    per = ROWS // FFN_HALO
    last = SEQ // FFN_HALO - 1

    def body(g_ref, v_ref, gh_ref, vh_ref, wg_ref, wv_ref, bg_ref, bv_ref, da_ref, gn_ref, vn_ref, dan_ref,
             out_ref, dbg_ref, dbv_ref, dwg_ref, dwv_ref, pad, g1, g2, v1, v2, dgp, dvp, acc):
        i = pl.program_id(1)
        first = i == 0
        _ffn_shifted(g_ref, gh_ref, pad, g1, g2)
        _ffn_shifted(v_ref, vh_ref, pad, v1, v2)
        acc[...] = jnp.zeros_like(acc)

        def grads(gate, val, da):
            s = _sig(gate)
            return da * val * (s * (1.0 + gate * (1.0 - s))), da * (gate * s)

        def chunk(rs):
            gate = _ffn_conv(rs, g_ref, g1, g2, wg_ref, bg_ref)
            val = _ffn_conv(rs, v_ref, v1, v2, wv_ref, bv_ref)
            dgate, dval = grads(gate, val, da_ref[rs, :])
            dgp[rs, :] = dgate
            dvp[rs, :] = dval
            acc[0] += dgate
            acc[1] += dval
            for t, (sg, sv) in enumerate(((g2, v2), (g1, v1), (g_ref, v_ref))):
                acc[2 + t] += dgate * sg[rs, :]
                acc[5 + t] += dval * sv[rs, :]

        _for_chunks(chunk)
        _acc(dbg_ref, _rsum(acc[0]), first)
        _acc(dbv_ref, _rsum(acc[1]), first)
        _acc(dwg_ref, jnp.concatenate([_rsum(acc[2 + t]) for t in range(FFN_K)], axis=0), first)
        _acc(dwv_ref, jnp.concatenate([_rsum(acc[5 + t]) for t in range(FFN_K)], axis=0), first)

        def conv_next(cur_ref, nxt_ref, w_ref, b_ref):
            pad[0:FFN_HALO, :] = cur_ref[ROWS - FFN_HALO:, :]
            pad[FFN_HALO:, :] = nxt_ref[...]
            return (b_ref[...] + w_ref[0:1, :] * pad[pl.ds(FFN_HALO - 2, FFN_HALO), :]
                    + w_ref[1:2, :] * pad[pl.ds(FFN_HALO - 1, FFN_HALO), :] + w_ref[2:3, :] * nxt_ref[...])

        gate_n = conv_next(g_ref, gn_ref, wg_ref, bg_ref)
        val_n = conv_next(v_ref, vn_ref, wv_ref, bv_ref)
        dgate_n, dval_n = grads(gate_n, val_n, dan_ref[...])
        inside = i < SEQ // ROWS - 1
        dgp[ROWS:, :] = jnp.where(inside, dgate_n, 0.0)
        dvp[ROWS:, :] = jnp.where(inside, dval_n, 0.0)

        for half, (dp, s1, s2, w_ref) in enumerate(((dgp, g1, g2, wg_ref), (dvp, v1, v2, wv_ref))):
            s1[...] = dp[pl.ds(1, ROWS), :]
            s2[...] = dp[pl.ds(2, ROWS), :]

            def back(rs, dp=dp, s1=s1, s2=s2, w_ref=w_ref, half=half):
                out_ref[half, rs, :] = (w_ref[2:3, :] * dp[rs, :] + w_ref[1:2, :] * s1[rs, :]
                                        + w_ref[0:1, :] * s2[rs, :]).astype(BF16)

            _for_chunks(back)

    tile = pltpu.VMEM((ROWS, FFN_TN), F32)
    ext = pltpu.VMEM((ROWS + FFN_HALO, FFN_TN), F32)
    vec = jax.ShapeDtypeStruct((1, D_FF), F32)
    taps = jax.ShapeDtypeStruct((FFN_K, D_FF), F32)
    cur = pl.BlockSpec((ROWS, FFN_TN), lambda j, i: (i, j))
    nxt = lambda off: pl.BlockSpec((FFN_HALO, FFN_TN), lambda j, i: (jnp.minimum((i + 1) * per, last), j + off))
    vs = pl.BlockSpec((1, FFN_TN), lambda j, i: (0, j))
    ts = pl.BlockSpec((FFN_K, FFN_TN), lambda j, i: (0, j))
    return pl.pallas_call(
        body, out_shape=(jax.ShapeDtypeStruct((2, SEQ, D_FF), BF16), vec, vec, taps, taps), grid=(N_FT, SEQ // ROWS),
        in_specs=_ffn_specs() + [cur, nxt(0), nxt(N_FT), nxt(0)],
        out_specs=(pl.BlockSpec((2, ROWS, FFN_TN), lambda j, i: (0, i, j)), vs, vs, ts, ts),
        scratch_shapes=[pltpu.VMEM((2 * FFN_HALO, FFN_TN), F32), tile, tile, tile, tile, ext, ext,
                        pltpu.VMEM((2 + 2 * FFN_K, SUB, FFN_TN), F32)],
        name=name, compiler_params=_cp("parallel", "arbitrary"))(up0, up0, up0, up0, wf, wf, bf, bf, dact, up0, up0, dact)


def ada_fwd(c_all, w_ada, b_cols, name):
    def body(c_ref, w_ref, b_ref, o_ref):
        cc = c_ref[...]
        sc = (cc * _sig(cc)).astype(BF16)
        o_ref[...] = jnp.dot(sc, w_ref[...].astype(BF16), preferred_element_type=F32) + b_ref[...]

    return pl.pallas_call(body, out_shape=jax.ShapeDtypeStruct((N_DEV, w_ada.shape[1]), F32), name=name,
                          compiler_params=_cp())(c_all, w_ada, b_cols)


def _adam(w, g, m, v):
    m = ADAM_B1 * m + (1.0 - ADAM_B1) * g
    v = ADAM_B2 * v + (1.0 - ADAM_B2) * (g * g)
    m_hat = m / (1.0 - ADAM_B1 ** ADAM_STEP)
    v_hat = v / (1.0 - ADAM_B2 ** ADAM_STEP)
    delta = -ADAM_LR * (m_hat / (jnp.sqrt(v_hat) + ADAM_EPS) + ADAM_WD * w)
    return delta, m, v


def ada_bwd_adamw(c_all_t, dmod_cols, w, m, v, name):
    rows, cols = w.shape
    tr = 256

    def body(ct_ref, dm_ref, w_ref, m_ref, v_ref, g_ref, d_ref, nm_ref, nv_ref):
        def chunk(rs):
            ct = ct_ref[rs, :]
            sc = ct * _sig(ct)
            g = sc[:, 0:1] * dm_ref[0:1, :]
            for b in range(1, N_DEV):
                g = g + sc[:, b:b + 1] * dm_ref[b:b + 1, :]
            g_ref[rs, :] = g
            d_ref[rs, :], nm_ref[rs, :], nv_ref[rs, :] = _adam(w_ref[rs, :], g, m_ref[rs, :], v_ref[rs, :])

        _for_chunks(chunk, 2, tr)

    blk = pl.BlockSpec((tr, cols), lambda i: (i, 0))
    shp = jax.ShapeDtypeStruct((rows, cols), F32)
    return pl.pallas_call(
        body, out_shape=(shp, shp, shp, shp), grid=(rows // tr,),
        in_specs=[pl.BlockSpec((tr, N_DEV), lambda i: (i, 0)), pl.BlockSpec((N_DEV, cols), lambda i: (0, 0)), blk, blk, blk],
        out_specs=(blk, blk, blk, blk), name=name, compiler_params=_cp("parallel"))(c_all_t, dmod_cols, w, m, v)


def sum_adamw(parts, own, w, m, v, tr, name):
    n_parts, rows, cols = parts.shape

    def body(*refs):
        if own is None:
            p_ref, w_ref, m_ref, v_ref, g_ref, d_ref, nm_ref, nv_ref = refs
        else:
            p_ref, own_ref, w_ref, m_ref, v_ref, g_ref, d_ref, nm_ref, nv_ref = refs

        def chunk(rs):
            g = (p_ref[0, rs, :] if own is None else own_ref[rs, :]).astype(F32)
            for k in range(1, n_parts):
                g = g + p_ref[k, rs, :].astype(F32)
            g_ref[rs, :] = g
            d_ref[rs, :], nm_ref[rs, :], nv_ref[rs, :] = _adam(w_ref[rs, :], g, m_ref[rs, :], v_ref[rs, :])

        _for_chunks(chunk, 2 if tr > SUB else 1, tr)

    blk = pl.BlockSpec((tr, cols), lambda i: (i, 0))
    shp = jax.ShapeDtypeStruct((rows, cols), F32)
    args = [parts] + ([] if own is None else [own]) + [w, m, v]
    return pl.pallas_call(
        body, out_shape=(shp, shp, shp, shp), grid=(rows // tr,),
        in_specs=[pl.BlockSpec((n_parts, tr, cols), lambda i: (0, i, 0))] + [blk] * (len(args) - 1),
        out_specs=(blk, blk, blk, blk), name=name, compiler_params=_cp("parallel"))(*args)


MESH = pl.DeviceIdType.MESH
ANY = pl.BlockSpec(memory_space=pl.ANY)


def all_gather(block, name, after=None):
    extra = () if after is None else (after,)

    def body(x_ref, *refs):
        out_ref, send_sems, recv_sems, local_sem = refs[len(extra):]
        x, y, c = lax.axis_index("x"), lax.axis_index("y"), lax.axis_index("c")
        me, sibling = (x, y, c), (x, y, 1 - c)
        chips = [(1 - x, y), (x, 1 - y), (1 - x, 1 - y)]

        def slot(px, py, pc):
            return out_ref.at[4 * px + 2 * py + pc]

        def copy(k, blk, to, src=None):
            return pltpu.make_async_remote_copy(
                src_ref=slot(*blk) if src is None else src, dst_ref=slot(*blk),
                send_sem=send_sems.at[k], recv_sem=recv_sems.at[k], device_id=to, device_id_type=MESH)

        mine = pltpu.make_async_copy(x_ref, slot(*me), local_sem)
        mine.start()
        first = [copy(0, me, sibling, src=x_ref)]
        first += [copy(1 + j, me, (*chip, c), src=x_ref) for j, chip in enumerate(chips)]
        for cp in first:
            cp.start()
        passed = [copy(4 + j, (*chip, c), sibling) for j, chip in enumerate(chips)]
        for j, chip in enumerate(chips):
            copy(1 + j, (*chip, c), me).wait_recv()
            passed[j].start()
        copy(0, sibling, me).wait_recv()
        for j, chip in enumerate(chips):
            copy(4 + j, (*chip, 1 - c), me).wait_recv()
        for cp in first + passed:
            cp.wait_send()
        mine.wait()

    return pl.pallas_call(
        body, out_shape=jax.ShapeDtypeStruct((N_DEV,) + block.shape, block.dtype), in_specs=[ANY] * (1 + len(extra)), out_specs=ANY,
        scratch_shapes=[pltpu.SemaphoreType.DMA((7,)), pltpu.SemaphoreType.DMA((7,)), pltpu.SemaphoreType.DMA],
        name=name)(block, *extra)


HBM = pl.BlockSpec(memory_space=pltpu.HBM)
SEM = pl.BlockSpec(memory_space=pltpu.SEMAPHORE)
EFFECT = pltpu.SideEffectType.DATAFLOW_SIDE_EFFECTING


def _peer_copies(src_ref, land_ref, send_sems, recv_sems, gather):
    x, y, c = lax.axis_index("x"), lax.axis_index("y"), lax.axis_index("c")
    me = 4 * x + 2 * y + c
    copies = []
    for k in range(1, N_DEV):
        px = 1 - x if k & 4 else x
        py = 1 - y if k & 2 else y
        pc = 1 - c if k & 1 else c
        copies.append(pltpu.make_async_remote_copy(
            src_ref=src_ref if gather else src_ref.at[4 * px + 2 * py + pc],
            dst_ref=land_ref.at[me] if gather else land_ref.at[k],
            send_sem=send_sems.at[k - 1], recv_sem=recv_sems.at[k - 1], device_id=(px, py, pc), device_id_type=MESH))
    return copies


def exchange_start(srcs, gather, name, after=None):
    n = len(srcs)
    land_shapes = [(N_DEV,) + src.shape if gather else src.shape for src in srcs]
    extra = () if after is None else (after,)

    def body(*refs):
        src_refs, land_refs = refs[0:n], refs[n:2 * n]
        outs = refs[2 * n + len(extra):]
        for k in range(n):
            for cp in _peer_copies(src_refs[k], land_refs[k], outs[4 * k], outs[4 * k + 1], gather):
                cp.start()
        token = outs[4 * n]
        token[...] = jnp.zeros_like(token)

    out_shape, out_specs, aliases = [], [], {}
    for k, src in enumerate(srcs):
        out_shape += [pltpu.SemaphoreType.DMA((N_DEV - 1,)), pltpu.SemaphoreType.DMA((N_DEV - 1,)),
                      pltpu.HBM(src.shape, src.dtype), pltpu.HBM(land_shapes[k], src.dtype)]
        out_specs += [SEM, SEM, HBM, HBM]
        aliases[k] = 4 * k + 2
        aliases[n + k] = 4 * k + 3
    out_shape.append(jax.ShapeDtypeStruct((8, 128), F32))
    out_specs.append(pl.BlockSpec(memory_space=pltpu.VMEM))
    res = pl.pallas_call(
        body, name=name, out_shape=tuple(out_shape), in_specs=(HBM,) * (2 * n) + (ANY,) * len(extra),
        out_specs=tuple(out_specs), input_output_aliases=aliases,
        compiler_params=pltpu.CompilerParams(has_side_effects=EFFECT),
    )(*[pltpu.with_memory_space_constraint(src, pltpu.HBM) for src in srcs],
      *[pltpu.with_memory_space_constraint(lax.empty(shp, src.dtype), pltpu.HBM) for shp, src in zip(land_shapes, srcs)],
      *extra)
    return [tuple(res[4 * k:4 * k + 4]) for k in range(n)], res[4 * n][0, 0]


def exchange_wait(handles, after, gather, name):
    send_sems, recv_sems, src_thru, land_thru = handles

    def body(src_ref, land_ref, send_sems, recv_sems, after_ref, src_dead, got_ref):
        for cp in _peer_copies(src_ref, land_ref, send_sems, recv_sems, gather):
            cp.wait_send()
            cp.wait_recv()

    return pl.pallas_call(
        body, name=name,
        out_shape=(pltpu.HBM(src_thru.shape, src_thru.dtype), pltpu.HBM(land_thru.shape, land_thru.dtype)),
        in_specs=(HBM, HBM, SEM, SEM, ANY), out_specs=(HBM, HBM), input_output_aliases={0: 0, 1: 1},
        compiler_params=pltpu.CompilerParams(has_side_effects=EFFECT),
    )(src_thru, land_thru, send_sems, recv_sems, after)[1]


def _to_pattern(a, r):
    return a.reshape(SEQ // r, r * a.shape[1])


def local_step(x, tgt, mod, get_w, put_grad, wc, wf, g_mix, bc, lg, lb, gco, gao, g_ffn, bf, g_fin):
    h1 = rms_mod_fwd(x, g_mix, mod, 0, 1, "h1_fwd")
    w_in = get_w("w_in", h1)
    proj = matmul(h1, w_in, "nt", F32, 512, D_IN, "proj_fwd", b_rows=D_IN // N_DEV)
    mix_a = conv_module_fwd(proj, wc, bc, lg, lb, gco, "conv_module_fwd")
    att, lse = attn_fwd_all(proj, "attn_fwd")
    mix_b = rms_gain_bf16(att, gao, "attn_out_norm")
    mixed = jnp.concatenate([mix_a, mix_b], axis=1)
    w_out = get_w("w_out", mixed)
    y1 = matmul(mixed, w_out, "nn", F32, 512, D_MODEL, "out_proj_fwd")
    x1, h2 = resid_rms_mod_fwd(x, y1, g_ffn, mod, 2, 3, 4, "x1_h2_fwd")
    w_up = get_w("w_up", h2)
    up0 = matmul(h2, w_up, "nt", F32, 512, D_FF, "up_fwd")
    act = ffn_act_fwd(up0, wf, bf, "ffn_act_fwd")
    w_down = get_w("w_down", act)
    y2 = matmul(act, w_down, "nn", F32, 512, D_MODEL, "down_fwd")
    loss_t, dx2, dy2, d_gfin, d_gaf = final_loss_bwd(x1, y2, tgt, g_fin, mod, 5, "loss_bwd")
    dact = matmul(dy2, w_down, "nt", F32, 512, FFN_TN, "down_bwd_x")
    dw_down = matmul(act, dy2, "tn", BF16, 256, D_MODEL, "down_bwd_w")
    dup0, dbf_g, dbf_v, dwf_g, dwf_v = ffn_bwd(up0, dact, wf + put_grad("w_down", dw_down), bf, "ffn_bwd")
    dh2 = matmul_halves(dup0, w_up, 512, 512, "up_bwd_x")
    dw_up = matmul_tn_halves(dup0, h2, 256, "up_bwd_w")
    dx1, d_shf, d_scf, d_gffn, dy1, d_gam = rms_mod_bwd(x1, dh2, dx2, g_ffn + put_grad("w_up", dw_up), mod, 4, y1, 2, "h2_bwd")
    dmixed = matmul(dy1, w_out, "nt", F32, 512, D_MODEL, "out_proj_bwd_x")
    dw_out = matmul(mixed, dy1, "tn", BF16, 256, D_MODEL, "out_proj_bwd_w")
    do, dd, d_gao = attn_combine_bwd(dmixed, att, gao + put_grad("w_out", dw_out), "attn_combine_bwd")
    dqkv = attn_bwd_all(proj, do, lse, dd, "attn_bwd")
    du1, d_gco, d_lg, d_lb, d_bc, d_wc = conv_module_bwd_a(proj, dmixed, wc, bc, lg, lb, gco, "conv_module_bwd_a")
    dproj_a = conv_module_bwd_b(proj, du1, wc, "conv_module_bwd_b")
    dproj = jnp.concatenate([dproj_a, dqkv[0], dqkv[1], dqkv[2]], axis=1)
    dw_in = matmul(dproj, h1, "tn", BF16, 512, D_MODEL, "proj_bwd_w")
    dh1 = matmul(dproj, w_in, "nn", F32, 512, D_MODEL, "proj_bwd_x", b_rows=D_IN // N_DEV)
    dx, d_shm, d_scm, d_gmix = rms_mod_bwd(x, dh1, dx1, g_mix + put_grad("w_in", dw_in), mod, 1, None, 0, "h1_bwd")
    dmod = jnp.concatenate([d_shm, d_scm, d_gam, d_shf, d_scf, d_gaf], axis=1)
    small = dict(g_norm_mix=d_gmix, b_conv_dw=d_bc, ln_conv_g=d_lg, ln_conv_b=d_lb, g_conv_out=d_gco, g_attn_out=d_gao,
                 g_norm_ffn=d_gffn, b_ffn_dw=jnp.concatenate([dbf_g, dbf_v], axis=1), g_final=d_gfin,
                 w_conv_dw=d_wc, w_ffn_dw=jnp.concatenate([dwf_g, dwf_v], axis=1), dmod=dmod, loss=loss_t[0:1, 0:1])
    return dx, small


def _padw(a, width):
    return jnp.pad(a, ((0, 0), (0, width - a.shape[1])))


def pack_small(t):
    wide = jnp.concatenate([_padw(t["dmod"], PACK_W), _padw(t["b_ffn_dw"], PACK_W), _padw(t["w_ffn_dw"], PACK_W),
                            _padw(t["loss"], PACK_W), jnp.zeros((2, PACK_W), F32)], axis=0)
    z512 = jnp.zeros((1, 512), F32)
    narrow = jnp.concatenate([
        t["g_norm_mix"], t["g_norm_ffn"], t["g_final"],
        jnp.concatenate([t["b_conv_dw"], t["ln_conv_g"]], axis=1),
        jnp.concatenate([t["ln_conv_b"], t["g_conv_out"]], axis=1),
        jnp.concatenate([t["g_attn_out"], z512], axis=1),
        jnp.zeros((2, 1024), F32),
        jnp.pad(t["w_conv_dw"], ((0, 1), (0, 0))).reshape(16, 1024)], axis=0)
    return jnp.concatenate([wide, narrow.reshape(4, PACK_W), jnp.zeros((4, PACK_W), F32)], axis=0)


def unpack_small(p):
    narrow = p[8:12].reshape(24, 1024)
    return dict(
        dmod=p[0:1], b_ffn_dw=p[1:2, :2 * D_FF], w_ffn_dw=p[2:5, :2 * D_FF], loss=p[5, 0],
        g_norm_mix=narrow[0:1], g_norm_ffn=narrow[1:2], g_final=narrow[2:3],
        b_conv_dw=narrow[3:4, :512], ln_conv_g=narrow[3:4, 512:], ln_conv_b=narrow[4:5, :512], g_conv_out=narrow[4:5, 512:],
        g_attn_out=narrow[5:6, :512], w_conv_dw=narrow[8:24].reshape(32, 512)[:CONV_K])


def _embed(local, width, me):
    return lax.dynamic_update_slice(jnp.zeros((local.shape[0], width), F32), local, (0, me * local.shape[1]))


def _shard(full, n_cols, me):
    return lax.dynamic_slice(full, (0, me * n_cols), (full.shape[0], n_cols))


WEIGHTS = ["w_ada", "b_ada", "g_norm_mix", "w_in", "w_conv_dw", "b_conv_dw", "ln_conv_g", "ln_conv_b", "g_conv_out",
           "g_attn_out", "w_out", "g_norm_ffn", "w_up", "w_ffn_dw", "b_ffn_dw", "w_down", "g_final"]
SMALL_REPLICATED = ["g_norm_mix", "b_conv_dw", "ln_conv_g", "ln_conv_b", "g_conv_out", "g_attn_out", "g_norm_ffn",
                    "b_ffn_dw", "g_final"]


def kernel(x, c, w_ada, b_ada, g_norm_mix, w_in, w_conv_dw, b_conv_dw, ln_conv_g, ln_conv_b, g_conv_out, g_attn_out, w_out, g_norm_ffn, w_up, w_ffn_dw, b_ffn_dw, w_down, g_final, loss_target, m_w_ada, m_b_ada, m_g_norm_mix, m_w_in, m_w_conv_dw, m_b_conv_dw, m_ln_conv_g, m_ln_conv_b, m_g_conv_out, m_g_attn_out, m_w_out, m_g_norm_ffn, m_w_up, m_w_ffn_dw, m_b_ffn_dw, m_w_down, m_g_final, v_w_ada, v_b_ada, v_g_norm_mix, v_w_in, v_w_conv_dw, v_b_conv_dw, v_ln_conv_g, v_ln_conv_b, v_g_conv_out, v_g_attn_out, v_w_out, v_g_norm_ffn, v_w_up, v_w_ffn_dw, v_b_ffn_dw, v_w_down, v_g_final):
    args = dict(locals())
    me = 4 * lax.axis_index("x") + 2 * lax.axis_index("y") + lax.axis_index("c")

    def flat(name, prefix=""):
        a = args[prefix + name]
        return a.reshape(a.shape[-2] if a.ndim > 1 else 1, a.shape[-1])

    def flat_t(name, prefix=""):
        return args[prefix + name][0].T

    n_in, n_up, r_out, r_down = w_in.shape[2], w_up.shape[2], w_out.shape[1], w_down.shape[1]
    n_ada, n_wc, n_wf = w_ada.shape[2], w_conv_dw.shape[2], w_ffn_dw.shape[2]
    taps_c = jnp.pad(flat("w_conv_dw").reshape(1, CONV_K * n_wc), ((0, 0), (0, 2 * D_MODEL - CONV_K * n_wc)))
    taps_f = jnp.pad(flat("w_ffn_dw").reshape(1, FFN_K * n_wf), ((0, 0), (0, 3 * D_MODEL - FFN_K * n_wf)))
    first = jnp.concatenate([c, taps_c.reshape(2, D_MODEL), taps_f.reshape(3, D_MODEL), jnp.zeros((2, D_MODEL), F32)], axis=0)
    w_in_block = flat_t("w_in").astype(BF16)
    hi = lax.reduce_precision(first, 8, 7)
    mid = lax.reduce_precision(first - hi, 8, 7)
    low = lax.reduce_precision(first - hi - mid, 8, 7)
    terms = jnp.concatenate([hi, mid, low, jnp.zeros((8, D_MODEL), F32)], axis=0).astype(BF16)
    first_block = all_gather(jnp.concatenate([w_in_block, terms], axis=0), "gather_c_taps_w_in")
    terms = first_block[:, n_in:n_in + 24, :].astype(F32).reshape(N_DEV, 3, 8, D_MODEL)
    first_all = (terms[:, 0] + terms[:, 1]) + terms[:, 2]
    c_all = first_all[:, 0, :]
    wc_full = first_all[:, 1:3, :].reshape(N_DEV, 2 * D_MODEL)[:, :CONV_K * n_wc].reshape(N_DEV, CONV_K, n_wc)
    wc_full = wc_full.transpose(1, 0, 2).reshape(CONV_K, D_CONV)
    wf_full = first_all[:, 3:6, :].reshape(N_DEV, 3 * D_MODEL)[:, :FFN_K * n_wf].reshape(N_DEV, FFN_K, n_wf)
    wf_full = wf_full.transpose(1, 0, 2).reshape(FFN_K, 2 * D_FF)
    mod_cols = ada_fwd(c_all, flat("w_ada"), _shard(flat("b_ada"), n_ada, me), "ada_fwd")
    mod_all = all_gather(mod_cols, "gather_mod")
    mod = lax.dynamic_index_in_dim(mod_all, me, axis=1, keepdims=False).reshape(N_MOD, D_MODEL)
    mod = jnp.pad(mod, ((0, 2), (0, 0)))

    order = ("w_out", "w_up", "w_down")
    blocks = dict(w_up=flat_t("w_up").astype(BF16), w_out=flat("w_out").astype(BF16), w_down=flat("w_down").astype(BF16))
    handles, tok = exchange_start([blocks[name] for name in order], True, "gather_weights_start", mod_all)
    gathers = dict(zip(order, handles))
    mod = mod + tok

    def gathered(name, after):
        land = exchange_wait(gathers[name], after, True, f"gather_{name}_wait")
        return lax.dynamic_update_index_in_dim(land, blocks[name], me, axis=0)

    def get_w(name, after):
        if name == "w_in":
            return first_block
        return gathered(name, after).reshape(-1, D_MODEL)

    exchanges, own = {}, {}

    def put_grad(name, dw, after=None):
        dev_major = dw.reshape(N_DEV, -1, D_MODEL)
        own[name] = lax.dynamic_index_in_dim(dev_major, me, axis=0, keepdims=False)
        (exchanges[name],), token = exchange_start([dev_major], False, f"exchange_{name}_start", after)
        return token

    grad_x, small = local_step(
        x[0], loss_target[0], mod, get_w, put_grad, wc_full, wf_full,
        flat("g_norm_mix"), flat("b_conv_dw"), flat("ln_conv_g"), flat("ln_conv_b"), flat("g_conv_out"),
        flat("g_attn_out"), flat("g_norm_ffn"), flat("b_ffn_dw"), flat("g_final"))

    out = {}

    def finish(name, tr, after):
        parts = exchange_wait(exchanges[name], after, False, f"exchange_{name}_wait")
        if name in ("w_in", "w_up"):
            res = sum_adamw(parts, own[name], flat_t(name), flat_t(name, "m_"), flat_t(name, "v_"), tr, "adamw_" + name)
            out[name] = tuple(r.T for r in res)
        else:
            res = out[name] = sum_adamw(parts, own[name], flat(name), flat(name, "m_"), flat(name, "v_"), tr, "adamw_" + name)
        return res[0]

    after = finish("w_down", r_down, grad_x)
    after = finish("w_up", n_up // 2, after)
    after = finish("w_out", r_out, after)
    after = finish("w_in", n_in, after)

    small_all = all_gather(pack_small(small), "gather_small", after)

    def packed(prefix):
        t = {n: flat(n, prefix) for n in SMALL_REPLICATED}
        t["dmod"] = flat("b_ada", prefix)
        t["loss"] = jnp.zeros((1, 1), F32)
        t["w_conv_dw"] = _embed(flat("w_conv_dw", prefix), D_CONV, me)
        t["w_ffn_dw"] = _embed(flat("w_ffn_dw", prefix), 2 * D_FF, me)
        return pack_small(t)

    res = sum_adamw(small_all, None, packed(""), packed("m_"), packed("v_"), PACK_ROWS, "adamw_small")
    res = [unpack_small(r) for r in res]
    loss = res[0]["loss"]
    for n in SMALL_REPLICATED:
        out[n] = tuple(r[n] for r in res)
    out["b_ada"] = tuple(r["dmod"] for r in res)
    out["w_conv_dw"] = tuple(_shard(r["w_conv_dw"], n_wc, me) for r in res)
    out["w_ffn_dw"] = tuple(_shard(r["w_ffn_dw"], n_wf, me) for r in res)

    dmod_cols = _shard(small_all[:, 0, :], n_ada, me)
    out["w_ada"] = ada_bwd_adamw(c_all.T, dmod_cols, flat("w_ada"), flat("w_ada", "m_"), flat("w_ada", "v_"), "adamw_w_ada")

    result = [loss, grad_x[None]]
    for k in range(4):
        result += [out[n][k].reshape(args[n].shape) for n in WEIGHTS]
    return tuple(result)
```

```python
import functools

import jax
import jax.numpy as jnp
from jax import lax
from jax.experimental import pallas as pl
from jax.experimental.pallas import tpu as pltpu

F32 = jnp.float32
BF16 = jnp.bfloat16

N_DEV = 8
SEQ = 2048
D_MODEL = 1024
D_CONV = 512
D_ATTN = 512
HEAD_DIM = 64
CONV_K = 31
D_FF = 2816
FFN_K = 3
D_IN = 2 * D_CONV + 3 * D_ATTN
N_MOD = 6
EPS = 1e-6
ATTN_BLOCK = 128
PATTERNS = ((2048, 1), (512, 4), (128, 16))
NEG = -1e30

ADAM_LR, ADAM_B1, ADAM_B2, ADAM_EPS, ADAM_WD, ADAM_STEP = 0.001, 0.9, 0.999, 1e-08, 0.01, 10

ROWS = 256
CONV_HALO = 32
FFN_HALO = 8
FFN_TN = 1408
VMEM_LIMIT = 56 * 1024 * 1024
PACK_ROWS, PACK_W = 16, 6144


def _cp(*sem):
    return pltpu.CompilerParams(dimension_semantics=sem if sem else None, vmem_limit_bytes=VMEM_LIMIT)


def _sig(x):
    return 1.0 / (1.0 + jnp.exp(-x))


def _rsum(x):
    return jnp.sum(x, axis=0, keepdims=True)


def _mean(x):
    return jnp.mean(x, axis=-1, keepdims=True)


def _acc(ref, val, first):
    @pl.when(first)
    def _():
        ref[...] = val

    @pl.when(jnp.logical_not(first))
    def _():
        ref[...] += val


SUB = 16


def _for_chunks(fn, unroll=1, rows=ROWS):
    def step(i, carry):
        fn(pl.ds(pl.multiple_of(i * SUB, SUB), SUB))
        return carry

    lax.fori_loop(0, rows // SUB, step, 0, unroll=unroll)


def rms_mod_fwd(x, g, mod, sh_row, sc_row, name):
    def body(x_ref, g_ref, mod_ref, h_ref):
        xx = x_ref[...]
        r = lax.rsqrt(_mean(xx * xx) + EPS)
        h = xx * r * g_ref[...]
        h_ref[...] = (h * (1.0 + mod_ref[sc_row:sc_row + 1, :]) + mod_ref[sh_row:sh_row + 1, :]).astype(BF16)

    return pl.pallas_call(
        body, out_shape=jax.ShapeDtypeStruct((SEQ, D_MODEL), BF16), grid=(SEQ // ROWS,),
        in_specs=[_row_spec(D_MODEL), _vec_spec(D_MODEL), _vec_spec(D_MODEL, 8)],
        out_specs=_row_spec(D_MODEL), name=name, compiler_params=_cp("parallel"))(x, g, mod)


def resid_rms_mod_fwd(x, y, g, mod, ga_row, sh_row, sc_row, name):
    def body(x_ref, y_ref, g_ref, mod_ref, x1_ref, h_ref):
        x1 = x_ref[...] + mod_ref[ga_row:ga_row + 1, :] * y_ref[...]
        x1_ref[...] = x1
        r = lax.rsqrt(_mean(x1 * x1) + EPS)
        h = x1 * r * g_ref[...]
        h_ref[...] = (h * (1.0 + mod_ref[sc_row:sc_row + 1, :]) + mod_ref[sh_row:sh_row + 1, :]).astype(BF16)

    return pl.pallas_call(
        body, out_shape=(jax.ShapeDtypeStruct((SEQ, D_MODEL), F32), jax.ShapeDtypeStruct((SEQ, D_MODEL), BF16)),
        grid=(SEQ // ROWS,),
        in_specs=[_row_spec(D_MODEL), _row_spec(D_MODEL), _vec_spec(D_MODEL), _vec_spec(D_MODEL, 8)],
        out_specs=(_row_spec(D_MODEL), _row_spec(D_MODEL)), name=name, compiler_params=_cp("parallel"))(x, y, g, mod)


def final_loss_bwd(x1, y2, tgt, g, mod, ga_row, name):
    def body(x1_ref, y2_ref, t_ref, g_ref, mod_ref, loss_ref, dx2_ref, dy2_ref, dg_ref, dga_ref):
        first = pl.program_id(0) == 0
        ga = mod_ref[ga_row:ga_row + 1, :]
        y2 = y2_ref[...]
        x2 = x1_ref[...] + ga * y2
        r = lax.rsqrt(_mean(x2 * x2) + EPS)
        xn = x2 * r
        err = xn * g_ref[...] - t_ref[...]
        _acc(loss_ref, jnp.broadcast_to(0.5 * jnp.sum(_mean(err * err)), (8, 128)), first)
        dy = err * (1.0 / D_MODEL)
        _acc(dg_ref, _rsum(dy * xn), first)
        dxn = dy * g_ref[...]
        dx2 = r * (dxn - xn * _mean(dxn * xn))
        dx2_ref[...] = dx2
        dy2_ref[...] = (dx2 * ga).astype(BF16)
        _acc(dga_ref, _rsum(dx2 * y2), first)

    vec = jax.ShapeDtypeStruct((1, D_MODEL), F32)
    return pl.pallas_call(
        body,
        out_shape=(jax.ShapeDtypeStruct((8, 128), F32), jax.ShapeDtypeStruct((SEQ, D_MODEL), F32),
                   jax.ShapeDtypeStruct((SEQ, D_MODEL), BF16), vec, vec),
        grid=(SEQ // ROWS,),
        in_specs=[_row_spec(D_MODEL), _row_spec(D_MODEL), _row_spec(D_MODEL), _vec_spec(D_MODEL), _vec_spec(D_MODEL, 8)],
        out_specs=(pl.BlockSpec((8, 128), lambda i: (0, 0)), _row_spec(D_MODEL), _row_spec(D_MODEL),
                   _vec_spec(D_MODEL), _vec_spec(D_MODEL)),
        name=name, compiler_params=_cp("arbitrary"))(x1, y2, tgt, g, mod)


def rms_mod_bwd(x, dh, dres, g, mod, sc_row, y, ga_row, name):
    gated = y is not None

    def body(*refs):
        if gated:
            x_ref, dh_ref, dres_ref, g_ref, mod_ref, y_ref, dx_ref, dsh_ref, dsc_ref, dg_ref, dy_ref, dga_ref = refs
        else:
            x_ref, dh_ref, dres_ref, g_ref, mod_ref, dx_ref, dsh_ref, dsc_ref, dg_ref = refs
        first = pl.program_id(0) == 0
        xx = x_ref[...]
        dh = dh_ref[...]
        gg = g_ref[...]
        r = lax.rsqrt(_mean(xx * xx) + EPS)
        xn = xx * r
        _acc(dsh_ref, _rsum(dh), first)
        _acc(dsc_ref, _rsum(dh * (xn * gg)), first)
        dt = dh * (1.0 + mod_ref[sc_row:sc_row + 1, :])
        _acc(dg_ref, _rsum(dt * xn), first)
        dxn = dt * gg
        dx = dres_ref[...] + r * (dxn - xn * _mean(dxn * xn))
        dx_ref[...] = dx
        if gated:
            _acc(dga_ref, _rsum(dx * y_ref[...]), first)
            dy_ref[...] = (dx * mod_ref[ga_row:ga_row + 1, :]).astype(BF16)

    vec = jax.ShapeDtypeStruct((1, D_MODEL), F32)
    in_specs = [_row_spec(D_MODEL), _row_spec(D_MODEL), _row_spec(D_MODEL), _vec_spec(D_MODEL), _vec_spec(D_MODEL, 8)]
    out_shape = [jax.ShapeDtypeStruct((SEQ, D_MODEL), F32), vec, vec, vec]
    out_specs = [_row_spec(D_MODEL), _vec_spec(D_MODEL), _vec_spec(D_MODEL), _vec_spec(D_MODEL)]
    args = [x, dh, dres, g, mod]
    if gated:
        in_specs.append(_row_spec(D_MODEL))
        out_shape += [jax.ShapeDtypeStruct((SEQ, D_MODEL), BF16), vec]
        out_specs += [_row_spec(D_MODEL), _vec_spec(D_MODEL)]
        args.append(y)
    return pl.pallas_call(
        body, out_shape=tuple(out_shape), grid=(SEQ // ROWS,), in_specs=in_specs, out_specs=tuple(out_specs),
        name=name, compiler_params=_cp("arbitrary"))(*args)


def _prev_halo(halo, width, col):
    per = ROWS // halo
    return pl.BlockSpec((halo, width), lambda i: (jnp.maximum(i * per - 1, 0), col))


def _next_halo(halo, width, col):
    per = ROWS // halo
    last = SEQ // halo - 1
    return pl.BlockSpec((halo, width), lambda i: (jnp.minimum((i + 1) * per, last), col))


CONV_PAD = ROWS + CONV_HALO


def _shift_copies(sh):
    for b in range(1, 8):
        sh[b, 0:CONV_PAD - 8, :] = sh[0, pl.ds(b, CONV_PAD - 8), :]


def _tap(sh, rs_start, offset):
    return sh[offset % 8, pl.ds(pl.multiple_of(rs_start + (offset // 8) * 8, 8), SUB), :]


def _conv_module_forward(av_ref, ag_ref, avh_ref, agh_ref, wc_ref, bc_ref, lg_ref, lb_ref, sh, u1_s):
    i = pl.program_id(0)
    hv = avh_ref[...] * _sig(agh_ref[...])
    sh[0, 0:CONV_HALO, :] = jnp.where(i > 0, hv, 0.0)

    def glu(rs):
        sh[0, pl.ds(pl.multiple_of(rs.start + CONV_HALO, SUB), SUB), :] = av_ref[rs, :] * _sig(ag_ref[rs, :])

    _for_chunks(glu)
    _shift_copies(sh)

    def conv(rs):
        u1 = jnp.broadcast_to(bc_ref[...], (SUB, D_CONV))
        for j in range(CONV_K):
            u1 = u1 + wc_ref[j:j + 1, :] * _tap(sh, rs.start, CONV_HALO - (CONV_K - 1) + j)
        u1_s[rs, :] = u1

    _for_chunks(conv)
    u1 = u1_s[...]
    mu = _mean(u1)
    cen = u1 - mu
    rs = lax.rsqrt(_mean(cen * cen) + EPS)
    z = cen * rs
    ln = z * lg_ref[...] + lb_ref[...]
    s = _sig(ln)
    return z, rs, ln, s, ln * s


_CONV_SCRATCH = [pltpu.VMEM((8, CONV_PAD, D_CONV), F32), pltpu.VMEM((ROWS, D_CONV), F32)]


def conv_module_fwd(proj, wc, bc, lg, lb, gco, name):
    def body(av_ref, ag_ref, avh_ref, agh_ref, wc_ref, bc_ref, lg_ref, lb_ref, gco_ref, out_ref, sh, u1_s):
        _, _, _, _, u2 = _conv_module_forward(av_ref, ag_ref, avh_ref, agh_ref, wc_ref, bc_ref, lg_ref, lb_ref, sh, u1_s)
        rc = lax.rsqrt(_mean(u2 * u2) + EPS)
        out_ref[...] = (u2 * rc * gco_ref[...]).astype(BF16)

    v = _vec_spec(D_CONV)
    return pl.pallas_call(
        body, out_shape=jax.ShapeDtypeStruct((SEQ, D_CONV), BF16), grid=(SEQ // ROWS,),
        in_specs=[_row_spec(D_CONV, 0), _row_spec(D_CONV, 1), _prev_halo(CONV_HALO, D_CONV, 0),
                  _prev_halo(CONV_HALO, D_CONV, 1), _vec_spec(D_CONV, CONV_K), v, v, v, v],
        out_specs=_row_spec(D_CONV), scratch_shapes=list(_CONV_SCRATCH),
        name=name, compiler_params=_cp("parallel"))(proj, proj, proj, proj, wc, bc, lg, lb, gco)


def conv_module_bwd_a(proj, dmixed, wc, bc, lg, lb, gco, name):
    def body(av_ref, ag_ref, avh_ref, agh_ref, dm_ref, wc_ref, bc_ref, lg_ref, lb_ref, gco_ref,
             du1_ref, dgco_ref, dlg_ref, dlb_ref, dbc_ref, dwc_ref, sh, u1_s, acc):
        first = pl.program_id(0) == 0
        z, rs, ln, s, u2 = _conv_module_forward(av_ref, ag_ref, avh_ref, agh_ref, wc_ref, bc_ref, lg_ref, lb_ref, sh, u1_s)
        rc = lax.rsqrt(_mean(u2 * u2) + EPS)
        xn = u2 * rc
        dm = dm_ref[...]
        _acc(dgco_ref, _rsum(dm * xn), first)
        dyn = dm * gco_ref[...]
        du2 = rc * (dyn - xn * _mean(dyn * xn))
        dln = du2 * (s * (1.0 + ln * (1.0 - s)))
        _acc(dlg_ref, _rsum(dln * z), first)
        _acc(dlb_ref, _rsum(dln), first)
        dz = dln * lg_ref[...]
        du1 = rs * (dz - _mean(dz) - z * _mean(dz * z))
        du1_ref[...] = du1
        _acc(dbc_ref, _rsum(du1), first)
        acc[...] = jnp.zeros_like(acc)

        def taps(rs):
            d = du1_ref[rs, :]
            for j in range(CONV_K):
                acc[j] += d * _tap(sh, rs.start, CONV_HALO - (CONV_K - 1) + j)

        _for_chunks(taps)

        @pl.when(first)
        def _():
            dwc_ref[...] = jnp.zeros_like(dwc_ref)

        for j in range(CONV_K):
            dwc_ref[j:j + 1, :] += _rsum(acc[j])

    v = _vec_spec(D_CONV)
    vec = jax.ShapeDtypeStruct((1, D_CONV), F32)
    return pl.pallas_call(
        body,
        out_shape=(jax.ShapeDtypeStruct((SEQ, D_CONV), F32), vec, vec, vec, vec, jax.ShapeDtypeStruct((CONV_K, D_CONV), F32)),
        grid=(SEQ // ROWS,),
        in_specs=[_row_spec(D_CONV, 0), _row_spec(D_CONV, 1), _prev_halo(CONV_HALO, D_CONV, 0),
                  _prev_halo(CONV_HALO, D_CONV, 1), _row_spec(D_CONV, 0), _vec_spec(D_CONV, CONV_K), v, v, v, v],
        out_specs=(_row_spec(D_CONV), v, v, v, v, _vec_spec(D_CONV, CONV_K)),
        scratch_shapes=list(_CONV_SCRATCH) + [pltpu.VMEM((CONV_K, SUB, D_CONV), F32)],
        name=name, compiler_params=_cp("arbitrary"))(proj, proj, proj, proj, dmixed, wc, bc, lg, lb, gco)


def conv_module_bwd_b(proj, du1, wc, name):
    def body(av_ref, ag_ref, du1_ref, du1n_ref, wc_ref, out_ref, sh):
        i = pl.program_id(0)
        sh[0, 0:ROWS, :] = du1_ref[...]
        sh[0, ROWS:, :] = jnp.where(i < SEQ // ROWS - 1, du1n_ref[...], 0.0)
        _shift_copies(sh)

        def chunk(rs):
            du0 = jnp.zeros((SUB, D_CONV), F32)
            for j in range(CONV_K):
                du0 = du0 + wc_ref[j:j + 1, :] * _tap(sh, rs.start, CONV_K - 1 - j)
            sg = _sig(ag_ref[rs, :])
            out_ref[rs, 0:D_CONV] = (du0 * sg).astype(BF16)
            out_ref[rs, D_CONV:] = (du0 * av_ref[rs, :] * sg * (1.0 - sg)).astype(BF16)

        _for_chunks(chunk)

    return pl.pallas_call(
        body, out_shape=jax.ShapeDtypeStruct((SEQ, 2 * D_CONV), BF16), grid=(SEQ // ROWS,),
        in_specs=[_row_spec(D_CONV, 0), _row_spec(D_CONV, 1), _row_spec(D_CONV, 0), _next_halo(CONV_HALO, D_CONV, 0),
                  _vec_spec(D_CONV, CONV_K)],
        out_specs=_row_spec(2 * D_CONV), scratch_shapes=[pltpu.VMEM((8, CONV_PAD, D_CONV), F32)],
        name=name, compiler_params=_cp("parallel"))(proj, proj, du1, du1, wc)


def _attn_specs(sub_len, pairs):
    ng, width = 4 // pairs, 128 * pairs
    q = pl.BlockSpec((sub_len, width), lambda rho, g: (0, rho * 3 * ng + g))
    k = pl.BlockSpec((sub_len, width), lambda rho, g: (0, rho * 3 * ng + ng + g))
    v = pl.BlockSpec((sub_len, width), lambda rho, g: (0, rho * 3 * ng + 2 * ng + g))
    o = pl.BlockSpec((sub_len, width), lambda rho, g: (0, rho * ng + g))
    return q, k, v, o


ATTN_PAIRS = {1: 1, 4: 1, 16: 4}


def _attn_block(q_ref, k_ref, v_ref, n, hs, win):
    q0 = pl.multiple_of(n * ATTN_BLOCK, ATTN_BLOCK)
    k0 = pl.multiple_of(jnp.maximum(n - 1, 0) * ATTN_BLOCK, ATTN_BLOCK)
    qb = q_ref[pl.ds(q0, ATTN_BLOCK), hs]
    kw = k_ref[pl.ds(k0, win), hs]
    vw = v_ref[pl.ds(k0, win), hs]
    s = lax.dot_general(qb, kw, (((1,), (1,)), ((), ())), preferred_element_type=F32) * (HEAD_DIM ** -0.5)
    dist = (q0 - k0) + lax.broadcasted_iota(jnp.int32, (ATTN_BLOCK, win), 0) \
        - lax.broadcasted_iota(jnp.int32, (ATTN_BLOCK, win), 1)
    s = jnp.where((dist >= 0) & (dist <= ATTN_BLOCK), s, NEG)
    return q0, k0, qb, kw, vw, s


def attn_fwd(qkv_r, sub_len, r, name):
    nb = sub_len // ATTN_BLOCK
    win = 2 * ATTN_BLOCK if nb > 1 else ATTN_BLOCK
    pairs = ATTN_PAIRS[r]

    def body(q_ref, k_ref, v_ref, o_ref, l_ref):
        def block(n, carry):
            for h in range(2 * pairs):
                hs = slice(h * HEAD_DIM, (h + 1) * HEAD_DIM)
                q0, _, _, _, vw, s = _attn_block(q_ref, k_ref, v_ref, n, hs, win)
                m = jnp.max(s, axis=1, keepdims=True)
                p = jnp.exp(s - m)
                den = jnp.sum(p, axis=1, keepdims=True)
                o = jnp.dot(p.astype(BF16), vw, preferred_element_type=F32) / den
                o_ref[pl.ds(q0, ATTN_BLOCK), hs] = o
                l_ref[pl.ds(q0, ATTN_BLOCK), hs] = jnp.broadcast_to(m + jnp.log(den), (ATTN_BLOCK, HEAD_DIM))
            return carry

        lax.fori_loop(0, nb, block, 0, unroll=min(nb, 2))

    q, k, v, o = _attn_specs(sub_len, pairs)
    shp = jax.ShapeDtypeStruct((sub_len, r * D_ATTN), F32)
    return pl.pallas_call(
        body, out_shape=(shp, shp), grid=(r, 4 // pairs), in_specs=[q, k, v], out_specs=(o, o),
        name=name, compiler_params=_cp("parallel", "parallel"))(qkv_r, qkv_r, qkv_r)


def attn_bwd(qkv_r, do_r, lse_r, dd_r, sub_len, r, name):
    nb = sub_len // ATTN_BLOCK
    win = 2 * ATTN_BLOCK if nb > 1 else ATTN_BLOCK
    pairs = ATTN_PAIRS[r]

    def body(q_ref, k_ref, v_ref, do_ref, l_ref, dd_ref, dq_ref, dk_ref, dv_ref):
        dk_ref[...] = jnp.zeros_like(dk_ref)
        dv_ref[...] = jnp.zeros_like(dv_ref)

        def block(n, carry):
            for h in range(2 * pairs):
                hs = slice(h * HEAD_DIM, (h + 1) * HEAD_DIM)
                h1 = slice(h * HEAD_DIM, h * HEAD_DIM + 1)
                q0, k0, qb, kw, vw, s = _attn_block(q_ref, k_ref, v_ref, n, hs, win)
                dob = do_ref[pl.ds(q0, ATTN_BLOCK), hs]
                p = jnp.exp(s - l_ref[pl.ds(q0, ATTN_BLOCK), h1])
                dp = lax.dot_general(dob, vw, (((1,), (1,)), ((), ())), preferred_element_type=F32)
                ds = (p * (dp - dd_ref[pl.ds(q0, ATTN_BLOCK), h1]) * (HEAD_DIM ** -0.5)).astype(BF16)
                dq_ref[pl.ds(q0, ATTN_BLOCK), hs] = jnp.dot(ds, kw, preferred_element_type=F32)
                dk_ref[pl.ds(k0, win), hs] += lax.dot_general(ds, qb, (((0,), (0,)), ((), ())), preferred_element_type=F32)
                dv_ref[pl.ds(k0, win), hs] += lax.dot_general(p.astype(BF16), dob, (((0,), (0,)), ((), ())),
                                                              preferred_element_type=F32)
            return carry

        lax.fori_loop(0, nb, block, 0)

    q, k, v, o = _attn_specs(sub_len, pairs)
    shp = jax.ShapeDtypeStruct((sub_len, r * D_ATTN), F32)
    return pl.pallas_call(
        body, out_shape=(shp, shp, shp), grid=(r, 4 // pairs), in_specs=[q, k, v, o, o, o], out_specs=(o, o, o),
        name=name, compiler_params=_cp("parallel", "parallel"))(qkv_r, qkv_r, qkv_r, do_r, lse_r, dd_r)


def _rows(start, size, r):
    return pl.ds(start, size) if r == 1 else pl.ds(start, size, stride=r)


def _unit_rows(r, rho, n, nb):
    win = 2 * ATTN_BLOCK if nb > 1 else ATTN_BLOCK
    if isinstance(n, int):
        kb = max(n - 1, 0)
        q_rows = _rows(rho + r * ATTN_BLOCK * n, ATTN_BLOCK, r)
        k_rows = _rows(rho + r * ATTN_BLOCK * kb, win, r)
    else:
        kb = jnp.maximum(n - 1, 0)
        q_rows = pl.ds(pl.multiple_of(n * ATTN_BLOCK, ATTN_BLOCK), ATTN_BLOCK)
        k_rows = pl.ds(pl.multiple_of(kb * ATTN_BLOCK, ATTN_BLOCK), win)
    dist = (n - kb) * ATTN_BLOCK + lax.broadcasted_iota(jnp.int32, (ATTN_BLOCK, win), 0) \
        - lax.broadcasted_iota(jnp.int32, (ATTN_BLOCK, win), 1)
    return q_rows, k_rows, (dist >= 0) & (dist <= ATTN_BLOCK)


def _per_head(x):
    lane = lax.broadcasted_iota(jnp.int32, x.shape, 1)
    zero = jnp.zeros_like(x)
    return [jnp.where(lane < HEAD_DIM, x, zero), jnp.where(lane >= HEAD_DIM, x, zero)]


NT = (((1,), (1,)), ((), ()))


def _masked_scores(q2, k2, valid):
    return [jnp.where(valid, lax.dot_general(qh, k2, NT, preferred_element_type=F32) * (HEAD_DIM ** -0.5), NEG)
            for qh in _per_head(q2)]


def _attn_units(r, nb, unit):
    if r == 1:
        def four(i, carry):
            for k in range(4):
                unit(0, 4 * i + k)
            return carry
        lax.fori_loop(0, nb // 4, four, 0)
    else:
        for rho in range(r):
            for n in range(nb):
                unit(rho, n)


N_UNITS = 16


def attn_fwd_all(proj, name):
    def body(q_ref, k_ref, v_ref, att_ref, lse_ref, s_scr, p_scr, lse_scr, den_scr):
        for idx, (sub_len, r) in enumerate(PATTERNS):
            nb = sub_len // ATTN_BLOCK
            win = 2 * ATTN_BLOCK if nb > 1 else ATTN_BLOCK

            def scores(rho, n, r=r, nb=nb, win=win):
                u = rho * nb + n
                q_rows, k_rows, valid = _unit_rows(r, rho, n, nb)
                ss = _masked_scores(q_ref[q_rows, :].astype(BF16), k_ref[k_rows, :].astype(BF16), valid)
                for h in range(2):
                    s_scr[2 * u + h, :, 0:win] = ss[h]

            _attn_units(r, nb, scores)

            def softmax(u, carry, win=win):
                lses, dens = [], []
                for h in range(2):
                    sc = s_scr[2 * u + h, :, 0:win]
                    m = jnp.max(sc, axis=1, keepdims=True)
                    p = jnp.exp(sc - m)
                    den = jnp.sum(p, axis=1, keepdims=True)
                    p_scr[2 * u + h, :, 0:win] = p.astype(BF16)
                    lses.append(jnp.broadcast_to(m + jnp.log(den), (ATTN_BLOCK, HEAD_DIM)))
                    dens.append(jnp.broadcast_to(den, (ATTN_BLOCK, HEAD_DIM)))
                lse_scr[u] = jnp.concatenate(lses, axis=1)
                den_scr[u] = jnp.concatenate(dens, axis=1)
                return carry

            lax.fori_loop(0, N_UNITS, softmax, 0, unroll=2)

            def outputs(rho, n, r=r, nb=nb, win=win, idx=idx):
                u = rho * nb + n
                q_rows, k_rows, _ = _unit_rows(r, rho, n, nb)
                vs = _per_head(v_ref[k_rows, :].astype(BF16))
                o = (jnp.dot(p_scr[2 * u, :, 0:win], vs[0], preferred_element_type=F32)
                     + jnp.dot(p_scr[2 * u + 1, :, 0:win], vs[1], preferred_element_type=F32)) / den_scr[u]
                lse = lse_scr[u]
                if idx > 0:
                    old = lse_ref[q_rows, :]
                    top = jnp.maximum(old, lse)
                    new = top + jnp.log(jnp.exp(old - top) + jnp.exp(lse - top))
                    o = att_ref[q_rows, :] * jnp.exp(old - new) + o * jnp.exp(lse - new)
                    lse = new
                att_ref[q_rows, :] = o
                lse_ref[q_rows, :] = lse

            _attn_units(r, nb, outputs)

    blk = lambda first: pl.BlockSpec((SEQ, 128), lambda g: (0, first + g))
    shp = jax.ShapeDtypeStruct((SEQ, D_ATTN), F32)
    big = (2 * N_UNITS, ATTN_BLOCK, 2 * ATTN_BLOCK)
    small = pltpu.VMEM((N_UNITS, ATTN_BLOCK, 128), F32)
    return pl.pallas_call(
        body, out_shape=(shp, shp), grid=(4,), in_specs=[blk(8), blk(12), blk(16)], out_specs=(blk(0), blk(0)),
        scratch_shapes=[pltpu.VMEM(big, F32), pltpu.VMEM(big, BF16), small, small],
        name=name, compiler_params=_cp("parallel"))(proj, proj, proj)


def attn_bwd_all(proj, do, lse, dd, name):
    scale = HEAD_DIM ** -0.5

    def body(q_ref, k_ref, v_ref, do_ref, l_ref, dd_ref, out_ref, dq_s, dk_s, dv_s,
             s_scr, dp_scr, ds_scr, st_scr, dpt_scr, pt_scr, dst_scr, qb_scr, kb_scr, dob_scr):
        dq_s[...] = jnp.zeros_like(dq_s)
        dk_s[...] = jnp.zeros_like(dk_s)
        dv_s[...] = jnp.zeros_like(dv_s)
        for sub_len, r in PATTERNS:
            nb = sub_len // ATTN_BLOCK
            win = 2 * ATTN_BLOCK if nb > 1 else ATTN_BLOCK

            def scores(rho, n, r=r, nb=nb, win=win):
                u = rho * nb + n
                q_rows, k_rows, valid = _unit_rows(r, rho, n, nb)
                kb = max(n - 1, 0) if isinstance(n, int) else jnp.maximum(n - 1, 0)
                dist_t = (n - kb) * ATTN_BLOCK + lax.broadcasted_iota(jnp.int32, (win, ATTN_BLOCK), 1) \
                    - lax.broadcasted_iota(jnp.int32, (win, ATTN_BLOCK), 0)
                valid_t = (dist_t >= 0) & (dist_t <= ATTN_BLOCK)
                q2 = q_ref[q_rows, :].astype(BF16)
                k2 = k_ref[k_rows, :].astype(BF16)
                do2 = do_ref[q_rows, :].astype(BF16)
                qb_scr[u] = q2
                kb_scr[u, 0:win, :] = k2
                dob_scr[u] = do2
                l2 = l_ref[q_rows, :]
                d2 = dd_ref[q_rows, :]
                l2t = l2.T
                d2t = d2.T
                v2 = v_ref[k_rows, :].astype(BF16)
                qs, dos = _per_head(q2), _per_head(do2)
                for h in range(2):
                    c0 = h * HEAD_DIM
                    sc = lax.dot_general(qs[h], k2, NT, preferred_element_type=F32) * scale
                    s_scr[2 * u + h, :, 0:win] = jnp.where(valid, sc, NEG) - l2[:, c0:c0 + 1]
                    dp_scr[2 * u + h, :, 0:win] = lax.dot_general(dos[h], v2, NT, preferred_element_type=F32) \
                        - d2[:, c0:c0 + 1]
                    sct = lax.dot_general(k2, qs[h], NT, preferred_element_type=F32) * scale
                    st_scr[2 * u + h, 0:win, :] = jnp.where(valid_t, sct, NEG) - l2t[c0:c0 + 1, :]
                    dpt_scr[2 * u + h, 0:win, :] = lax.dot_general(v2, dos[h], NT, preferred_element_type=F32) \
                        - d2t[c0:c0 + 1, :]

            _attn_units(r, nb, scores)

            def pointwise(hu, carry, win=win):
                ds_scr[hu, :, 0:win] = (jnp.exp(s_scr[hu, :, 0:win]) * dp_scr[hu, :, 0:win] * scale).astype(BF16)
                pt = jnp.exp(st_scr[hu, 0:win, :])
                pt_scr[hu, 0:win, :] = pt.astype(BF16)
                dst_scr[hu, 0:win, :] = (pt * dpt_scr[hu, 0:win, :] * scale).astype(BF16)
                return carry

            lax.fori_loop(0, 2 * N_UNITS, pointwise, 0, unroll=4)

            def grads(rho, n, r=r, nb=nb, win=win):
                u = rho * nb + n
                q_rows, k_rows, _ = _unit_rows(r, rho, n, nb)
                qs, ks, dos = _per_head(qb_scr[u]), _per_head(kb_scr[u, 0:win, :]), _per_head(dob_scr[u])

                def both(scr, rows, rhs):
                    return (jnp.dot(scr[(2 * u,) + rows], rhs[0], preferred_element_type=F32)
                            + jnp.dot(scr[(2 * u + 1,) + rows], rhs[1], preferred_element_type=F32))

                dq_s[q_rows, :] += both(ds_scr, (slice(None), slice(0, win)), ks)
                dk_s[k_rows, :] += both(dst_scr, (slice(0, win), slice(None)), qs)
                dv_s[k_rows, :] += both(pt_scr, (slice(0, win), slice(None)), dos)

            _attn_units(r, nb, grads)
        out_ref[0] = dq_s[...].astype(BF16)
        out_ref[1] = dk_s[...].astype(BF16)
        out_ref[2] = dv_s[...].astype(BF16)

    blk = lambda first: pl.BlockSpec((SEQ, 128), lambda g: (0, first + g))
    acc = pltpu.VMEM((SEQ, 128), F32)
    big = (2 * N_UNITS, ATTN_BLOCK, 2 * ATTN_BLOCK)
    big_t = (2 * N_UNITS, 2 * ATTN_BLOCK, ATTN_BLOCK)
    return pl.pallas_call(
        body, out_shape=jax.ShapeDtypeStruct((3, SEQ, D_ATTN), BF16), grid=(4,),
        in_specs=[blk(8), blk(12), blk(16), blk(0), blk(0), blk(0)],
        out_specs=pl.BlockSpec((3, SEQ, 128), lambda g: (0, 0, g)),
        scratch_shapes=[acc, acc, acc, pltpu.VMEM(big, F32), pltpu.VMEM(big, F32), pltpu.VMEM(big, BF16),
                        pltpu.VMEM(big_t, F32), pltpu.VMEM(big_t, F32), pltpu.VMEM(big_t, BF16), pltpu.VMEM(big_t, BF16),
                        pltpu.VMEM((N_UNITS, ATTN_BLOCK, 128), BF16),
                        pltpu.VMEM((N_UNITS, 2 * ATTN_BLOCK, 128), BF16), pltpu.VMEM((N_UNITS, ATTN_BLOCK, 128), BF16)],
        name=name, compiler_params=_cp("parallel"))(proj, proj, proj, do, lse, dd)


def rms_gain_bf16(a, g, name):
    def body(a_ref, g_ref, o_ref):
        aa = a_ref[...]
        o_ref[...] = (aa * lax.rsqrt(_mean(aa * aa) + EPS) * g_ref[...]).astype(BF16)

    w = a.shape[1]
    return pl.pallas_call(
        body, out_shape=jax.ShapeDtypeStruct(a.shape, BF16), grid=(SEQ // ROWS,), in_specs=[_row_spec(w), _vec_spec(w)],
        out_specs=_row_spec(w), name=name, compiler_params=_cp("parallel"))(a, g)


def attn_combine_fwd(outs, lses, gao, name):
    def body(o1, o2, o3, l1, l2, l3, g_ref, att_ref, lse_ref, mix_ref):
        a1, a2, a3 = l1[...], l2[...], l3[...]
        m = jnp.maximum(jnp.maximum(a1, a2), a3)
        w1, w2, w3 = jnp.exp(a1 - m), jnp.exp(a2 - m), jnp.exp(a3 - m)
        den = w1 + w2 + w3
        att = (w1 * o1[...] + w2 * o2[...] + w3 * o3[...]) / den
        att_ref[...] = att
        lse_ref[...] = m + jnp.log(den)
        mix_ref[...] = (att * lax.rsqrt(_mean(att * att) + EPS) * g_ref[...]).astype(BF16)

    rs = _row_spec(D_ATTN)
    f = jax.ShapeDtypeStruct((SEQ, D_ATTN), F32)
    return pl.pallas_call(
        body, out_shape=(f, f, jax.ShapeDtypeStruct((SEQ, D_ATTN), BF16)), grid=(SEQ // ROWS,),
        in_specs=[rs] * 6 + [_vec_spec(D_ATTN)], out_specs=(rs, rs, rs),
        name=name, compiler_params=_cp("parallel"))(*outs, *lses, gao)


def attn_combine_bwd(dmixed, att, gao, name):
    def body(dm_ref, att_ref, g_ref, do_ref, dd_ref, dg_ref):
        first = pl.program_id(0) == 0
        att = att_ref[...]
        r = lax.rsqrt(_mean(att * att) + EPS)
        xn = att * r
        dm = dm_ref[...]
        _acc(dg_ref, _rsum(dm * xn), first)
        dyn = dm * g_ref[...]
        do = r * (dyn - xn * _mean(dyn * xn))
        do_ref[...] = do
        same_head = (jnp.right_shift(lax.broadcasted_iota(jnp.int32, (D_ATTN, D_ATTN), 0), 6)
                     == jnp.right_shift(lax.broadcasted_iota(jnp.int32, (D_ATTN, D_ATTN), 1), 6)).astype(F32)
        dd_ref[...] = jnp.dot(do * att, same_head, preferred_element_type=F32, precision=lax.Precision.HIGHEST)

    rs = _row_spec(D_ATTN)
    return pl.pallas_call(
        body,
        out_shape=(jax.ShapeDtypeStruct((SEQ, D_ATTN), F32), jax.ShapeDtypeStruct((SEQ, D_ATTN), F32),
                   jax.ShapeDtypeStruct((1, D_ATTN), F32)),
        grid=(SEQ // ROWS,), in_specs=[_row_spec(D_ATTN, 1), rs, _vec_spec(D_ATTN)],
        out_specs=(rs, rs, _vec_spec(D_ATTN)), name=name, compiler_params=_cp("arbitrary"))(dmixed, att, gao)


def sum3_bf16(a, b, c, name):
    def body(a_ref, b_ref, c_ref, o_ref):
        o_ref[...] = (a_ref[...] + b_ref[...] + c_ref[...]).astype(BF16)

    w = a.shape[1]
    rs = _row_spec(w)
    return pl.pallas_call(
        body, out_shape=jax.ShapeDtypeStruct(a.shape, BF16), grid=(SEQ // ROWS,), in_specs=[rs, rs, rs], out_specs=rs,
        name=name, compiler_params=_cp("parallel"))(a, b, c)


N_FT = D_FF // FFN_TN


def _ffn_specs():
    per = ROWS // FFN_HALO
    cur_g = pl.BlockSpec((ROWS, FFN_TN), lambda j, i: (i, j))
    cur_v = pl.BlockSpec((ROWS, FFN_TN), lambda j, i: (i, j + N_FT))
    halo_g = pl.BlockSpec((FFN_HALO, FFN_TN), lambda j, i: (jnp.maximum(i * per - 1, 0), j))
    halo_v = pl.BlockSpec((FFN_HALO, FFN_TN), lambda j, i: (jnp.maximum(i * per - 1, 0), j + N_FT))
    w_g = pl.BlockSpec((FFN_K, FFN_TN), lambda j, i: (0, j))
    w_v = pl.BlockSpec((FFN_K, FFN_TN), lambda j, i: (0, j + N_FT))
    b_g = pl.BlockSpec((1, FFN_TN), lambda j, i: (0, j))
    b_v = pl.BlockSpec((1, FFN_TN), lambda j, i: (0, j + N_FT))
    return [cur_g, cur_v, halo_g, halo_v, w_g, w_v, b_g, b_v]


def matmul(a, b, kind, out_dtype, tm, tn, name, b_rows=None):
    stacked = b_rows is not None
    b_shape = (N_DEV * b_rows, b.shape[2]) if stacked else b.shape
    if kind == "nn":
        (m, k), n = a.shape, b_shape[1]
        a_spec = pl.BlockSpec((tm, k), lambda j, i: (i, 0))
        b_spec = pl.BlockSpec((k, tn), lambda j, i: (0, j))
        dims = (((1,), (0,)), ((), ()))
    elif kind == "nt":
        (m, k), n = a.shape, b_shape[0]
        a_spec = pl.BlockSpec((tm, k), lambda j, i: (i, 0))
        b_spec = pl.BlockSpec((tn, k), lambda j, i: (j, 0))
        dims = (((1,), (1,)), ((), ()))
    else:
        (k, m), n = a.shape, b_shape[1]
        a_spec = pl.BlockSpec((k, tm), lambda j, i: (0, i))
        b_spec = pl.BlockSpec((k, tn), lambda j, i: (0, j))
        dims = (((0,), (0,)), ((), ()))
    assert m % tm == 0 and n % tn == 0, (name, m, n, tm, tn)
    if stacked:
        assert b_spec.block_shape[0] == b_shape[0] and kind in ("nn", "nt")
        width = b_spec.block_shape[1]
        b_spec = pl.BlockSpec((N_DEV, b_rows, width), (lambda j, i: (0, 0, j)) if kind == "nn" else (lambda j, i: (0, 0, 0)))

    def body(a_ref, b_ref, o_ref):
        bb = b_ref[...].reshape(b_shape[0], -1) if stacked else b_ref[...]
        o_ref[...] = lax.dot_general(a_ref[...], bb, dims, preferred_element_type=F32).astype(o_ref.dtype)

    return pl.pallas_call(
        body, out_shape=jax.ShapeDtypeStruct((m, n), out_dtype), grid=(n // tn, m // tm),
        in_specs=[a_spec, b_spec], out_specs=pl.BlockSpec((tm, tn), lambda j, i: (i, j)),
        name=name, compiler_params=_cp("parallel", "parallel"))(a, b)


def matmul_halves(a, b, tm, tn, name):
    _, m, k = a.shape
    n = b.shape[1]
    b3 = b.reshape(2, k, n)

    def body(a_ref, b_ref, o_ref):
        o_ref[...] = (jnp.dot(a_ref[0], b_ref[0], preferred_element_type=F32)
                      + jnp.dot(a_ref[1], b_ref[1], preferred_element_type=F32))

    return pl.pallas_call(
        body, out_shape=jax.ShapeDtypeStruct((m, n), F32), grid=(n // tn, m // tm),
        in_specs=[pl.BlockSpec((2, tm, k), lambda j, i: (0, i, 0)), pl.BlockSpec((2, k, tn), lambda j, i: (0, 0, j))],
        out_specs=pl.BlockSpec((tm, tn), lambda j, i: (i, j)), name=name, compiler_params=_cp("parallel", "parallel"))(a, b3)


def matmul_tn_halves(a, b, tm, name):
    _, k, m = a.shape
    n = b.shape[1]

    def body(a_ref, b_ref, o_ref):
        o_ref[0] = lax.dot_general(a_ref[0], b_ref[...], (((0,), (0,)), ((), ())), preferred_element_type=F32).astype(BF16)

    return pl.pallas_call(
        body, out_shape=jax.ShapeDtypeStruct((2, m, n), BF16), grid=(2, m // tm),
        in_specs=[pl.BlockSpec((1, k, tm), lambda h, i: (h, 0, i)), pl.BlockSpec((k, n), lambda h, i: (0, 0))],
        out_specs=pl.BlockSpec((1, tm, n), lambda h, i: (h, i, 0)), name=name,
        compiler_params=_cp("parallel", "parallel"))(a, b).reshape(2 * m, n)


def _row_spec(width, col=0):
    return pl.BlockSpec((ROWS, width), lambda i: (i, col))


def _vec_spec(width, rows=1):
    return pl.BlockSpec((rows, width), lambda i: (0, 0))


def _ffn_shifted(cur_ref, halo_ref, pad, s1, s2):
    i = pl.program_id(1)
    pad[0:FFN_HALO, :] = jnp.where(i > 0, halo_ref[...], 0.0)
    pad[FFN_HALO:, :] = cur_ref[0:FFN_HALO, :]
    for k, dst in ((1, s1), (2, s2)):
        dst[0:FFN_HALO, :] = pad[pl.ds(FFN_HALO - k, FFN_HALO), :]
        dst[FFN_HALO:, :] = cur_ref[pl.ds(FFN_HALO - k, ROWS - FFN_HALO), :]


def _ffn_conv(rs, cur_ref, s1, s2, w_ref, b_ref):
    return b_ref[...] + w_ref[0:1, :] * s2[rs, :] + w_ref[1:2, :] * s1[rs, :] + w_ref[2:3, :] * cur_ref[rs, :]


def ffn_act_fwd(up0, wf, bf, name):
    def body(g_ref, v_ref, gh_ref, vh_ref, wg_ref, wv_ref, bg_ref, bv_ref, act_ref, pad, g1, g2, v1, v2):
        _ffn_shifted(g_ref, gh_ref, pad, g1, g2)
        _ffn_shifted(v_ref, vh_ref, pad, v1, v2)

        def chunk(rs):
            gate = _ffn_conv(rs, g_ref, g1, g2, wg_ref, bg_ref)
            val = _ffn_conv(rs, v_ref, v1, v2, wv_ref, bv_ref)
            act_ref[rs, :] = (gate * _sig(gate) * val).astype(BF16)

        _for_chunks(chunk)

    tile = pltpu.VMEM((ROWS, FFN_TN), F32)
    return pl.pallas_call(
        body, out_shape=jax.ShapeDtypeStruct((SEQ, D_FF), BF16), grid=(N_FT, SEQ // ROWS),
        in_specs=_ffn_specs(), out_specs=pl.BlockSpec((ROWS, FFN_TN), lambda j, i: (i, j)),
        scratch_shapes=[pltpu.VMEM((2 * FFN_HALO, FFN_TN), F32), tile, tile, tile, tile],
        name=name, compiler_params=_cp("parallel", "parallel"))(up0, up0, up0, up0, wf, wf, bf, bf)


def ffn_bwd(up0, dact, wf, bf, name):
    per = ROWS // FFN_HALO
    last = SEQ // FFN_HALO - 1

    def body(g_ref, v_ref, gh_ref, vh_ref, wg_ref, wv_ref, bg_ref, bv_ref, da_ref, gn_ref, vn_ref, dan_ref,
             out_ref, dbg_ref, dbv_ref, dwg_ref, dwv_ref, pad, g1, g2, v1, v2, dgp, dvp, acc):
        i = pl.program_id(1)
        first = i == 0
        _ffn_shifted(g_ref, gh_ref, pad, g1, g2)
        _ffn_shifted(v_ref, vh_ref, pad, v1, v2)
        acc[...] = jnp.zeros_like(acc)

        def grads(gate, val, da):
            s = _sig(gate)
            return da * val * (s * (1.0 + gate * (1.0 - s))), da * (gate * s)

        def chunk(rs):
            gate = _ffn_conv(rs, g_ref, g1, g2, wg_ref, bg_ref)
            val = _ffn_conv(rs, v_ref, v1, v2, wv_ref, bv_ref)
            dgate, dval = grads(gate, val, da_ref[rs, :])
            dgp[rs, :] = dgate
            dvp[rs, :] = dval
            acc[0] += dgate
            acc[1] += dval
            for t, (sg, sv) in enumerate(((g2, v2), (g1, v1), (g_ref, v_ref))):
                acc[2 + t] += dgate * sg[rs, :]
                acc[5 + t] += dval * sv[rs, :]

        _for_chunks(chunk)
        _acc(dbg_ref, _rsum(acc[0]), first)
        _acc(dbv_ref, _rsum(acc[1]), first)
        _acc(dwg_ref, jnp.concatenate([_rsum(acc[2 + t]) for t in range(FFN_K)], axis=0), first)
        _acc(dwv_ref, jnp.concatenate([_rsum(acc[5 + t]) for t in range(FFN_K)], axis=0), first)

        def conv_next(cur_ref, nxt_ref, w_ref, b_ref):
            pad[0:FFN_HALO, :] = cur_ref[ROWS - FFN_HALO:, :]
            pad[FFN_HALO:, :] = nxt_ref[...]
            return (b_ref[...] + w_ref[0:1, :] * pad[pl.ds(FFN_HALO - 2, FFN_HALO), :]
                    + w_ref[1:2, :] * pad[pl.ds(FFN_HALO - 1, FFN_HALO), :] + w_ref[2:3, :] * nxt_ref[...])

        gate_n = conv_next(g_ref, gn_ref, wg_ref, bg_ref)
        val_n = conv_next(v_ref, vn_ref, wv_ref, bv_ref)
        dgate_n, dval_n = grads(gate_n, val_n, dan_ref[...])
        inside = i < SEQ // ROWS - 1
        dgp[ROWS:, :] = jnp.where(inside, dgate_n, 0.0)
        dvp[ROWS:, :] = jnp.where(inside, dval_n, 0.0)

        for half, (dp, s1, s2, w_ref) in enumerate(((dgp, g1, g2, wg_ref), (dvp, v1, v2, wv_ref))):
            s1[...] = dp[pl.ds(1, ROWS), :]
            s2[...] = dp[pl.ds(2, ROWS), :]

            def back(rs, dp=dp, s1=s1, s2=s2, w_ref=w_ref, half=half):
                out_ref[half, rs, :] = (w_ref[2:3, :] * dp[rs, :] + w_ref[1:2, :] * s1[rs, :]
                                        + w_ref[0:1, :] * s2[rs, :]).astype(BF16)

            _for_chunks(back)

    tile = pltpu.VMEM((ROWS, FFN_TN), F32)
    ext = pltpu.VMEM((ROWS + FFN_HALO, FFN_TN), F32)
    vec = jax.ShapeDtypeStruct((1, D_FF), F32)
    taps = jax.ShapeDtypeStruct((FFN_K, D_FF), F32)
    cur = pl.BlockSpec((ROWS, FFN_TN), lambda j, i: (i, j))
    nxt = lambda off: pl.BlockSpec((FFN_HALO, FFN_TN), lambda j, i: (jnp.minimum((i + 1) * per, last), j + off))
    vs = pl.BlockSpec((1, FFN_TN), lambda j, i: (0, j))
    ts = pl.BlockSpec((FFN_K, FFN_TN), lambda j, i: (0, j))
    return pl.pallas_call(
        body, out_shape=(jax.ShapeDtypeStruct((2, SEQ, D_FF), BF16), vec, vec, taps, taps), grid=(N_FT, SEQ // ROWS),
        in_specs=_ffn_specs() + [cur, nxt(0), nxt(N_FT), nxt(0)],
        out_specs=(pl.BlockSpec((2, ROWS, FFN_TN), lambda j, i: (0, i, j)), vs, vs, ts, ts),
        scratch_shapes=[pltpu.VMEM((2 * FFN_HALO, FFN_TN), F32), tile, tile, tile, tile, ext, ext,
                        pltpu.VMEM((2 + 2 * FFN_K, SUB, FFN_TN), F32)],
        name=name, compiler_params=_cp("parallel", "arbitrary"))(up0, up0, up0, up0, wf, wf, bf, bf, dact, up0, up0, dact)


def ada_fwd(c_all, w_ada, b_cols, name):
    def body(c_ref, w_ref, b_ref, o_ref):
        cc = c_ref[...]
        sc = (cc * _sig(cc)).astype(BF16)
        o_ref[...] = jnp.dot(sc, w_ref[...].astype(BF16), preferred_element_type=F32) + b_ref[...]

    return pl.pallas_call(body, out_shape=jax.ShapeDtypeStruct((N_DEV, w_ada.shape[1]), F32), name=name,
                          compiler_params=_cp())(c_all, w_ada, b_cols)


def _adam(w, g, m, v):
    m = ADAM_B1 * m + (1.0 - ADAM_B1) * g
    v = ADAM_B2 * v + (1.0 - ADAM_B2) * (g * g)
    m_hat = m / (1.0 - ADAM_B1 ** ADAM_STEP)
    v_hat = v / (1.0 - ADAM_B2 ** ADAM_STEP)
    delta = -ADAM_LR * (m_hat / (jnp.sqrt(v_hat) + ADAM_EPS) + ADAM_WD * w)
    return delta, m, v


def ada_bwd_adamw(c_all_t, dmod_cols, w, m, v, name):
    rows, cols = w.shape
    tr = 256

    def body(ct_ref, dm_ref, w_ref, m_ref, v_ref, g_ref, d_ref, nm_ref, nv_ref):
        def chunk(rs):
            ct = ct_ref[rs, :]
            sc = ct * _sig(ct)
            g = sc[:, 0:1] * dm_ref[0:1, :]
            for b in range(1, N_DEV):
                g = g + sc[:, b:b + 1] * dm_ref[b:b + 1, :]
            g_ref[rs, :] = g
            d_ref[rs, :], nm_ref[rs, :], nv_ref[rs, :] = _adam(w_ref[rs, :], g, m_ref[rs, :], v_ref[rs, :])

        _for_chunks(chunk, 2, tr)

    blk = pl.BlockSpec((tr, cols), lambda i: (i, 0))
    shp = jax.ShapeDtypeStruct((rows, cols), F32)
    return pl.pallas_call(
        body, out_shape=(shp, shp, shp, shp), grid=(rows // tr,),
        in_specs=[pl.BlockSpec((tr, N_DEV), lambda i: (i, 0)), pl.BlockSpec((N_DEV, cols), lambda i: (0, 0)), blk, blk, blk],
        out_specs=(blk, blk, blk, blk), name=name, compiler_params=_cp("parallel"))(c_all_t, dmod_cols, w, m, v)


def sum_adamw(parts, own, w, m, v, tr, name):
    n_parts, rows, cols = parts.shape

    def body(*refs):
        if own is None:
            p_ref, w_ref, m_ref, v_ref, g_ref, d_ref, nm_ref, nv_ref = refs
        else:
            p_ref, own_ref, w_ref, m_ref, v_ref, g_ref, d_ref, nm_ref, nv_ref = refs

        def chunk(rs):
            g = (p_ref[0, rs, :] if own is None else own_ref[rs, :]).astype(F32)
            for k in range(1, n_parts):
                g = g + p_ref[k, rs, :].astype(F32)
            g_ref[rs, :] = g
            d_ref[rs, :], nm_ref[rs, :], nv_ref[rs, :] = _adam(w_ref[rs, :], g, m_ref[rs, :], v_ref[rs, :])

        _for_chunks(chunk, 2 if tr > SUB else 1, tr)

    blk = pl.BlockSpec((tr, cols), lambda i: (i, 0))
    shp = jax.ShapeDtypeStruct((rows, cols), F32)
    args = [parts] + ([] if own is None else [own]) + [w, m, v]
    return pl.pallas_call(
        body, out_shape=(shp, shp, shp, shp), grid=(rows // tr,),
        in_specs=[pl.BlockSpec((n_parts, tr, cols), lambda i: (0, i, 0))] + [blk] * (len(args) - 1),
        out_specs=(blk, blk, blk, blk), name=name, compiler_params=_cp("parallel"))(*args)


MESH = pl.DeviceIdType.MESH
ANY = pl.BlockSpec(memory_space=pl.ANY)


def all_gather(block, name, after=None):
    extra = () if after is None else (after,)

    def body(x_ref, *refs):
        out_ref, send_sems, recv_sems, local_sem = refs[len(extra):]
        x, y, c = lax.axis_index("x"), lax.axis_index("y"), lax.axis_index("c")
        me, sibling = (x, y, c), (x, y, 1 - c)
        chips = [(1 - x, y), (x, 1 - y), (1 - x, 1 - y)]

        def slot(px, py, pc):
            return out_ref.at[4 * px + 2 * py + pc]

        def copy(k, blk, to, src=None):
            return pltpu.make_async_remote_copy(
                src_ref=slot(*blk) if src is None else src, dst_ref=slot(*blk),
                send_sem=send_sems.at[k], recv_sem=recv_sems.at[k], device_id=to, device_id_type=MESH)

        mine = pltpu.make_async_copy(x_ref, slot(*me), local_sem)
        mine.start()
        first = [copy(0, me, sibling, src=x_ref)]
        first += [copy(1 + j, me, (*chip, c), src=x_ref) for j, chip in enumerate(chips)]
        for cp in first:
            cp.start()
        passed = [copy(4 + j, (*chip, c), sibling) for j, chip in enumerate(chips)]
        for j, chip in enumerate(chips):
            copy(1 + j, (*chip, c), me).wait_recv()
            passed[j].start()
        copy(0, sibling, me).wait_recv()
        for j, chip in enumerate(chips):
            copy(4 + j, (*chip, 1 - c), me).wait_recv()
        for cp in first + passed:
            cp.wait_send()
        mine.wait()

    return pl.pallas_call(
        body, out_shape=jax.ShapeDtypeStruct((N_DEV,) + block.shape, block.dtype), in_specs=[ANY] * (1 + len(extra)), out_specs=ANY,
        scratch_shapes=[pltpu.SemaphoreType.DMA((7,)), pltpu.SemaphoreType.DMA((7,)), pltpu.SemaphoreType.DMA],
        name=name)(block, *extra)


HBM = pl.BlockSpec(memory_space=pltpu.HBM)
SEM = pl.BlockSpec(memory_space=pltpu.SEMAPHORE)
EFFECT = pltpu.SideEffectType.DATAFLOW_SIDE_EFFECTING


def _peer_copies(src_ref, land_ref, send_sems, recv_sems, gather):
    x, y, c = lax.axis_index("x"), lax.axis_index("y"), lax.axis_index("c")
    me = 4 * x + 2 * y + c
    copies = []
    for k in range(1, N_DEV):
        px = 1 - x if k & 4 else x
        py = 1 - y if k & 2 else y
        pc = 1 - c if k & 1 else c
        copies.append(pltpu.make_async_remote_copy(
            src_ref=src_ref if gather else src_ref.at[4 * px + 2 * py + pc],
            dst_ref=land_ref.at[me] if gather else land_ref.at[k],
            send_sem=send_sems.at[k - 1], recv_sem=recv_sems.at[k - 1], device_id=(px, py, pc), device_id_type=MESH))
    return copies


def exchange_start(srcs, gather, name, after=None):
    n = len(srcs)
    land_shapes = [(N_DEV,) + src.shape if gather else src.shape for src in srcs]
    extra = () if after is None else (after,)

    def body(*refs):
        src_refs, land_refs = refs[0:n], refs[n:2 * n]
        outs = refs[2 * n + len(extra):]
        for k in range(n):
            for cp in _peer_copies(src_refs[k], land_refs[k], outs[4 * k], outs[4 * k + 1], gather):
                cp.start()
        token = outs[4 * n]
        token[...] = jnp.zeros_like(token)

    out_shape, out_specs, aliases = [], [], {}
    for k, src in enumerate(srcs):
        out_shape += [pltpu.SemaphoreType.DMA((N_DEV - 1,)), pltpu.SemaphoreType.DMA((N_DEV - 1,)),
                      pltpu.HBM(src.shape, src.dtype), pltpu.HBM(land_shapes[k], src.dtype)]
        out_specs += [SEM, SEM, HBM, HBM]
        aliases[k] = 4 * k + 2
        aliases[n + k] = 4 * k + 3
    out_shape.append(jax.ShapeDtypeStruct((8, 128), F32))
    out_specs.append(pl.BlockSpec(memory_space=pltpu.VMEM))
    res = pl.pallas_call(
        body, name=name, out_shape=tuple(out_shape), in_specs=(HBM,) * (2 * n) + (ANY,) * len(extra),
        out_specs=tuple(out_specs), input_output_aliases=aliases,
        compiler_params=pltpu.CompilerParams(has_side_effects=EFFECT),
    )(*[pltpu.with_memory_space_constraint(src, pltpu.HBM) for src in srcs],
      *[pltpu.with_memory_space_constraint(lax.empty(shp, src.dtype), pltpu.HBM) for shp, src in zip(land_shapes, srcs)],
      *extra)
    return [tuple(res[4 * k:4 * k + 4]) for k in range(n)], res[4 * n][0, 0]


def exchange_wait(handles, after, gather, name):
    send_sems, recv_sems, src_thru, land_thru = handles

    def body(src_ref, land_ref, send_sems, recv_sems, after_ref, src_dead, got_ref):
        for cp in _peer_copies(src_ref, land_ref, send_sems, recv_sems, gather):
            cp.wait_send()
            cp.wait_recv()

    return pl.pallas_call(
        body, name=name,
        out_shape=(pltpu.HBM(src_thru.shape, src_thru.dtype), pltpu.HBM(land_thru.shape, land_thru.dtype)),
        in_specs=(HBM, HBM, SEM, SEM, ANY), out_specs=(HBM, HBM), input_output_aliases={0: 0, 1: 1},
        compiler_params=pltpu.CompilerParams(has_side_effects=EFFECT),
    )(src_thru, land_thru, send_sems, recv_sems, after)[1]


def _to_pattern(a, r):
    return a.reshape(SEQ // r, r * a.shape[1])


def local_step(x, tgt, mod, get_w, put_grad, wc, wf, g_mix, bc, lg, lb, gco, gao, g_ffn, bf, g_fin):
    h1 = rms_mod_fwd(x, g_mix, mod, 0, 1, "h1_fwd")
    w_in = get_w("w_in", h1)
    proj = matmul(h1, w_in, "nt", F32, 512, D_IN, "proj_fwd", b_rows=D_IN // N_DEV)
    mix_a = conv_module_fwd(proj, wc, bc, lg, lb, gco, "conv_module_fwd")
    att, lse = attn_fwd_all(proj, "attn_fwd")
    mix_b = rms_gain_bf16(att, gao, "attn_out_norm")
    mixed = jnp.concatenate([mix_a, mix_b], axis=1)
    w_out = get_w("w_out", mixed)
    y1 = matmul(mixed, w_out, "nn", F32, 512, D_MODEL, "out_proj_fwd")
    x1, h2 = resid_rms_mod_fwd(x, y1, g_ffn, mod, 2, 3, 4, "x1_h2_fwd")
    w_up = get_w("w_up", h2)
    up0 = matmul(h2, w_up, "nt", F32, 512, D_FF, "up_fwd")
    act = ffn_act_fwd(up0, wf, bf, "ffn_act_fwd")
    w_down = get_w("w_down", act)
    y2 = matmul(act, w_down, "nn", F32, 512, D_MODEL, "down_fwd")
    loss_t, dx2, dy2, d_gfin, d_gaf = final_loss_bwd(x1, y2, tgt, g_fin, mod, 5, "loss_bwd")
    dact = matmul(dy2, w_down, "nt", F32, 512, FFN_TN, "down_bwd_x")
    dw_down = matmul(act, dy2, "tn", BF16, 256, D_MODEL, "down_bwd_w")
    dup0, dbf_g, dbf_v, dwf_g, dwf_v = ffn_bwd(up0, dact, wf + put_grad("w_down", dw_down), bf, "ffn_bwd")
    dh2 = matmul_halves(dup0, w_up, 512, 512, "up_bwd_x")
    dw_up = matmul_tn_halves(dup0, h2, 256, "up_bwd_w")
    dx1, d_shf, d_scf, d_gffn, dy1, d_gam = rms_mod_bwd(x1, dh2, dx2, g_ffn + put_grad("w_up", dw_up), mod, 4, y1, 2, "h2_bwd")
    dmixed = matmul(dy1, w_out, "nt", F32, 512, D_MODEL, "out_proj_bwd_x")
    dw_out = matmul(mixed, dy1, "tn", BF16, 256, D_MODEL, "out_proj_bwd_w")
    do, dd, d_gao = attn_combine_bwd(dmixed, att, gao + put_grad("w_out", dw_out), "attn_combine_bwd")
    dqkv = attn_bwd_all(proj, do, lse, dd, "attn_bwd")
    du1, d_gco, d_lg, d_lb, d_bc, d_wc = conv_module_bwd_a(proj, dmixed, wc, bc, lg, lb, gco, "conv_module_bwd_a")
    dproj_a = conv_module_bwd_b(proj, du1, wc, "conv_module_bwd_b")
    dproj = jnp.concatenate([dproj_a, dqkv[0], dqkv[1], dqkv[2]], axis=1)
    dw_in = matmul(dproj, h1, "tn", BF16, 512, D_MODEL, "proj_bwd_w")
    dh1 = matmul(dproj, w_in, "nn", F32, 512, D_MODEL, "proj_bwd_x", b_rows=D_IN // N_DEV)
    dx, d_shm, d_scm, d_gmix = rms_mod_bwd(x, dh1, dx1, g_mix + put_grad("w_in", dw_in), mod, 1, None, 0, "h1_bwd")
    dmod = jnp.concatenate([d_shm, d_scm, d_gam, d_shf, d_scf, d_gaf], axis=1)
    small = dict(g_norm_mix=d_gmix, b_conv_dw=d_bc, ln_conv_g=d_lg, ln_conv_b=d_lb, g_conv_out=d_gco, g_attn_out=d_gao,
                 g_norm_ffn=d_gffn, b_ffn_dw=jnp.concatenate([dbf_g, dbf_v], axis=1), g_final=d_gfin,
                 w_conv_dw=d_wc, w_ffn_dw=jnp.concatenate([dwf_g, dwf_v], axis=1), dmod=dmod, loss=loss_t[0:1, 0:1])
    return dx, small


def _padw(a, width):
    return jnp.pad(a, ((0, 0), (0, width - a.shape[1])))


def pack_small(t):
    wide = jnp.concatenate([_padw(t["dmod"], PACK_W), _padw(t["b_ffn_dw"], PACK_W), _padw(t["w_ffn_dw"], PACK_W),
                            _padw(t["loss"], PACK_W), jnp.zeros((2, PACK_W), F32)], axis=0)
    z512 = jnp.zeros((1, 512), F32)
    narrow = jnp.concatenate([
        t["g_norm_mix"], t["g_norm_ffn"], t["g_final"],
        jnp.concatenate([t["b_conv_dw"], t["ln_conv_g"]], axis=1),
        jnp.concatenate([t["ln_conv_b"], t["g_conv_out"]], axis=1),
        jnp.concatenate([t["g_attn_out"], z512], axis=1),
        jnp.zeros((2, 1024), F32),
        jnp.pad(t["w_conv_dw"], ((0, 1), (0, 0))).reshape(16, 1024)], axis=0)
    return jnp.concatenate([wide, narrow.reshape(4, PACK_W), jnp.zeros((4, PACK_W), F32)], axis=0)


_NARROW = lambda k, off=0: (8 + k // 6, (k % 6) * 1024 + off)
PACKED_AT = dict(
    b_ada=(0, 0, N_MOD * D_MODEL), b_ffn_dw=(1, 0, 2 * D_FF),
    g_norm_mix=_NARROW(0) + (D_MODEL,), g_norm_ffn=_NARROW(1) + (D_MODEL,), g_final=_NARROW(2) + (D_MODEL,),
    b_conv_dw=_NARROW(3) + (D_CONV,), ln_conv_g=_NARROW(3, 512) + (D_CONV,),
    ln_conv_b=_NARROW(4) + (D_CONV,), g_conv_out=_NARROW(4, 512) + (D_CONV,), g_attn_out=_NARROW(5) + (D_ATTN,))
SMALL_ORDER = list(PACKED_AT)


def small_adamw(parts, wmv, name):
    def body(*refs):
        p_ref = refs[0]
        ins = refs[1:1 + 3 * len(SMALL_ORDER)]
        outs = refs[1 + 3 * len(SMALL_ORDER):]
        g = p_ref[0]
        for k in range(1, N_DEV):
            g = g + p_ref[k]
        for i, n in enumerate(SMALL_ORDER):
            row, lane, width = PACKED_AT[n]
            gp = g[row:row + 1, lane:lane + width]
            w_ref, m_ref, v_ref = ins[3 * i:3 * i + 3]
            g_ref, d_ref, nm_ref, nv_ref = outs[4 * i:4 * i + 4]
            g_ref[...] = gp
            d_ref[...], nm_ref[...], nv_ref[...] = _adam(w_ref[...], gp, m_ref[...], v_ref[...])
        wc_ref, wf_ref, loss_ref = outs[4 * len(SMALL_ORDER):]
        for j in range(CONV_K):
            row, lane = _NARROW(8 + j // 2, (j % 2) * 512)
            wc_ref[j:j + 1, :] = g[row:row + 1, lane:lane + D_CONV]
        wf_ref[...] = g[2:2 + FFN_K, 0:2 * D_FF]
        loss_ref[...] = jnp.broadcast_to(g[5:6, 0:1], (8, 128))

    args, out_shape = [parts], []
    for n in SMALL_ORDER:
        args += list(wmv[n])
        out_shape += [jax.ShapeDtypeStruct(wmv[n][0].shape, F32)] * 4
    out_shape += [jax.ShapeDtypeStruct((CONV_K, D_CONV), F32), jax.ShapeDtypeStruct((FFN_K, 2 * D_FF), F32),
                  jax.ShapeDtypeStruct((8, 128), F32)]
    res = pl.pallas_call(body, out_shape=tuple(out_shape), name=name, compiler_params=_cp())(*args)
    per = {n: tuple(res[4 * i:4 * i + 4]) for i, n in enumerate(SMALL_ORDER)}
    return per, res[-3], res[-2], res[-1][0, 0]


def shard_adamw(items, name):
    def body(*refs):
        ins, outs = refs[:4 * len(items)], refs[4 * len(items):]
        for i in range(len(items)):
            g_ref, w_ref, m_ref, v_ref = ins[4 * i:4 * i + 4]
            og_ref, d_ref, nm_ref, nv_ref = outs[4 * i:4 * i + 4]
            og_ref[...] = g_ref[...]
            d_ref[...], nm_ref[...], nv_ref[...] = _adam(w_ref[...], g_ref[...], m_ref[...], v_ref[...])

    args = [a for item in items for a in item]
    out_shape = tuple(jax.ShapeDtypeStruct(item[1].shape, F32) for item in items for _ in range(4))
    res = pl.pallas_call(body, out_shape=out_shape, name=name, compiler_params=_cp())(*args)
    return [tuple(res[4 * i:4 * i + 4]) for i in range(len(items))]


def _shard(full, n_cols, me):
    return lax.dynamic_slice(full, (0, me * n_cols), (full.shape[0], n_cols))


WEIGHTS = ["w_ada", "b_ada", "g_norm_mix", "w_in", "w_conv_dw", "b_conv_dw", "ln_conv_g", "ln_conv_b", "g_conv_out",
           "g_attn_out", "w_out", "g_norm_ffn", "w_up", "w_ffn_dw", "b_ffn_dw", "w_down", "g_final"]
SMALL_REPLICATED = ["g_norm_mix", "b_conv_dw", "ln_conv_g", "ln_conv_b", "g_conv_out", "g_attn_out", "g_norm_ffn",
                    "b_ffn_dw", "g_final"]


def kernel(x, c, w_ada, b_ada, g_norm_mix, w_in, w_conv_dw, b_conv_dw, ln_conv_g, ln_conv_b, g_conv_out, g_attn_out, w_out, g_norm_ffn, w_up, w_ffn_dw, b_ffn_dw, w_down, g_final, loss_target, m_w_ada, m_b_ada, m_g_norm_mix, m_w_in, m_w_conv_dw, m_b_conv_dw, m_ln_conv_g, m_ln_conv_b, m_g_conv_out, m_g_attn_out, m_w_out, m_g_norm_ffn, m_w_up, m_w_ffn_dw, m_b_ffn_dw, m_w_down, m_g_final, v_w_ada, v_b_ada, v_g_norm_mix, v_w_in, v_w_conv_dw, v_b_conv_dw, v_ln_conv_g, v_ln_conv_b, v_g_conv_out, v_g_attn_out, v_w_out, v_g_norm_ffn, v_w_up, v_w_ffn_dw, v_b_ffn_dw, v_w_down, v_g_final):
    args = dict(locals())
    me = 4 * lax.axis_index("x") + 2 * lax.axis_index("y") + lax.axis_index("c")

    def flat(name, prefix=""):
        a = args[prefix + name]
        return a.reshape(a.shape[-2] if a.ndim > 1 else 1, a.shape[-1])

    def flat_t(name, prefix=""):
        return args[prefix + name][0].T

    n_in, n_up, r_out, r_down = w_in.shape[2], w_up.shape[2], w_out.shape[1], w_down.shape[1]
    n_ada, n_wc, n_wf = w_ada.shape[2], w_conv_dw.shape[2], w_ffn_dw.shape[2]
    taps_c = jnp.pad(flat("w_conv_dw").reshape(1, CONV_K * n_wc), ((0, 0), (0, 2 * D_MODEL - CONV_K * n_wc)))
    taps_f = jnp.pad(flat("w_ffn_dw").reshape(1, FFN_K * n_wf), ((0, 0), (0, 3 * D_MODEL - FFN_K * n_wf)))
    first = jnp.concatenate([c, taps_c.reshape(2, D_MODEL), taps_f.reshape(3, D_MODEL), jnp.zeros((2, D_MODEL), F32)], axis=0)
    w_in_block = flat_t("w_in").astype(BF16)
    hi = lax.reduce_precision(first, 8, 7)
    mid = lax.reduce_precision(first - hi, 8, 7)
    low = lax.reduce_precision(first - hi - mid, 8, 7)
    terms = jnp.concatenate([hi, mid, low, jnp.zeros((8, D_MODEL), F32)], axis=0).astype(BF16)
    first_block = all_gather(jnp.concatenate([w_in_block, terms], axis=0), "gather_c_taps_w_in")
    terms = first_block[:, n_in:n_in + 24, :].astype(F32).reshape(N_DEV, 3, 8, D_MODEL)
    first_all = (terms[:, 0] + terms[:, 1]) + terms[:, 2]
    c_all = first_all[:, 0, :]
    wc_full = first_all[:, 1:3, :].reshape(N_DEV, 2 * D_MODEL)[:, :CONV_K * n_wc].reshape(N_DEV, CONV_K, n_wc)
    wc_full = wc_full.transpose(1, 0, 2).reshape(CONV_K, D_CONV)
    wf_full = first_all[:, 3:6, :].reshape(N_DEV, 3 * D_MODEL)[:, :FFN_K * n_wf].reshape(N_DEV, FFN_K, n_wf)
    wf_full = wf_full.transpose(1, 0, 2).reshape(FFN_K, 2 * D_FF)
    mod_cols = ada_fwd(c_all, flat("w_ada"), _shard(flat("b_ada"), n_ada, me), "ada_fwd")
    mod_all = all_gather(mod_cols, "gather_mod")
    mod = lax.dynamic_index_in_dim(mod_all, me, axis=1, keepdims=False).reshape(N_MOD, D_MODEL)
    mod = jnp.pad(mod, ((0, 2), (0, 0)))

    order = ("w_out", "w_up", "w_down")
    blocks = dict(w_up=flat_t("w_up").astype(BF16), w_out=flat("w_out").astype(BF16), w_down=flat("w_down").astype(BF16))
    handles, tok = exchange_start([blocks[name] for name in order], True, "gather_weights_start", mod_all)
    gathers = dict(zip(order, handles))
    mod = mod + tok

    def gathered(name, after):
        land = exchange_wait(gathers[name], after, True, f"gather_{name}_wait")
        return lax.dynamic_update_index_in_dim(land, blocks[name], me, axis=0)

    def get_w(name, after):
        if name == "w_in":
            return first_block
        return gathered(name, after).reshape(-1, D_MODEL)

    exchanges, own = {}, {}

    def put_grad(name, dw, after=None):
        dev_major = dw.reshape(N_DEV, -1, D_MODEL)
        own[name] = lax.dynamic_index_in_dim(dev_major, me, axis=0, keepdims=False)
        (exchanges[name],), token = exchange_start([dev_major], False, f"exchange_{name}_start", after)
        return token

    grad_x, small = local_step(
        x[0], loss_target[0], mod, get_w, put_grad, wc_full, wf_full,
        flat("g_norm_mix"), flat("b_conv_dw"), flat("ln_conv_g"), flat("ln_conv_b"), flat("g_conv_out"),
        flat("g_attn_out"), flat("g_norm_ffn"), flat("b_ffn_dw"), flat("g_final"))

    out = {}

    def finish(name, tr, after):
        parts = exchange_wait(exchanges[name], after, False, f"exchange_{name}_wait")
        if name in ("w_in", "w_up"):
            res = sum_adamw(parts, own[name], flat_t(name), flat_t(name, "m_"), flat_t(name, "v_"), tr, "adamw_" + name)
            out[name] = tuple(r.T for r in res)
        else:
            res = out[name] = sum_adamw(parts, own[name], flat(name), flat(name, "m_"), flat(name, "v_"), tr, "adamw_" + name)
        return res[0]

    after = finish("w_down", r_down, grad_x)
    after = finish("w_up", n_up // 2, after)
    after = finish("w_out", r_out, after)
    after = finish("w_in", n_in, after)

    small_all = all_gather(pack_small(small), "gather_small", after)

    wmv = {n: (flat(n), flat(n, "m_"), flat(n, "v_")) for n in SMALL_ORDER}
    per, g_wc, g_wf, loss = small_adamw(small_all, wmv, "adamw_small")
    out.update(per)
    taps = shard_adamw([(_shard(g_wc, n_wc, me), flat("w_conv_dw"), flat("w_conv_dw", "m_"), flat("w_conv_dw", "v_")),
                        (_shard(g_wf, n_wf, me), flat("w_ffn_dw"), flat("w_ffn_dw", "m_"), flat("w_ffn_dw", "v_"))],
                       "adamw_taps")
    out["w_conv_dw"], out["w_ffn_dw"] = taps

    dmod_cols = _shard(small_all[:, 0, :], n_ada, me)
    out["w_ada"] = ada_bwd_adamw(c_all.T, dmod_cols, flat("w_ada"), flat("w_ada", "m_"), flat("w_ada", "v_"), "adamw_w_ada")

    result = [loss, grad_x[None]]
    for k in range(4):
        result += [out[n][k].reshape(args[n].shape) for n in WEIGHTS]
    return tuple(result)
```

```python
import functools

import jax
import jax.numpy as jnp
from jax import lax
from jax.experimental import pallas as pl
from jax.experimental.pallas import tpu as pltpu

F32 = jnp.float32
BF16 = jnp.bfloat16

N_DEV = 8
SEQ = 2048
D_MODEL = 1024
D_CONV = 512
D_ATTN = 512
HEAD_DIM = 64
CONV_K = 31
D_FF = 2816
FFN_K = 3
D_IN = 2 * D_CONV + 3 * D_ATTN
N_MOD = 6
EPS = 1e-6
ATTN_BLOCK = 128
PATTERNS = ((2048, 1), (512, 4), (128, 16))
NEG = -1e30

ADAM_LR, ADAM_B1, ADAM_B2, ADAM_EPS, ADAM_WD, ADAM_STEP = 0.001, 0.9, 0.999, 1e-08, 0.01, 10

ROWS = 256
CONV_HALO = 32
FFN_HALO = 8
FFN_TN = 1408
VMEM_LIMIT = 56 * 1024 * 1024
PACK_ROWS, PACK_W = 16, 6144


def _cp(*sem):
    return pltpu.CompilerParams(dimension_semantics=sem if sem else None, vmem_limit_bytes=VMEM_LIMIT)


def _sig(x):
    return 1.0 / (1.0 + jnp.exp(-x))


def _rsum(x):
    return jnp.sum(x, axis=0, keepdims=True)


def _mean(x):
    return jnp.mean(x, axis=-1, keepdims=True)


def _acc(ref, val, first):
    @pl.when(first)
    def _():
        ref[...] = val

    @pl.when(jnp.logical_not(first))
    def _():
        ref[...] += val


SUB = 16


def _for_chunks(fn, unroll=1, rows=ROWS):
    def step(i, carry):
        fn(pl.ds(pl.multiple_of(i * SUB, SUB), SUB))
        return carry

    lax.fori_loop(0, rows // SUB, step, 0, unroll=unroll)


def rms_mod_fwd(x, g, mod, sh_row, sc_row, name):
    def body(x_ref, g_ref, mod_ref, h_ref):
        xx = x_ref[...]
        r = lax.rsqrt(_mean(xx * xx) + EPS)
        h = xx * r * g_ref[...]
        h_ref[...] = (h * (1.0 + mod_ref[sc_row:sc_row + 1, :]) + mod_ref[sh_row:sh_row + 1, :]).astype(BF16)

    return pl.pallas_call(
        body, out_shape=jax.ShapeDtypeStruct((SEQ, D_MODEL), BF16), grid=(SEQ // ROWS,),
        in_specs=[_row_spec(D_MODEL), _vec_spec(D_MODEL), _vec_spec(D_MODEL, 8)],
        out_specs=_row_spec(D_MODEL), name=name, compiler_params=_cp("parallel"))(x, g, mod)


def resid_rms_mod_fwd(x, y, g, mod, ga_row, sh_row, sc_row, name):
    def body(x_ref, y_ref, g_ref, mod_ref, x1_ref, h_ref):
        x1 = x_ref[...] + mod_ref[ga_row:ga_row + 1, :] * y_ref[...]
        x1_ref[...] = x1
        r = lax.rsqrt(_mean(x1 * x1) + EPS)
        h = x1 * r * g_ref[...]
        h_ref[...] = (h * (1.0 + mod_ref[sc_row:sc_row + 1, :]) + mod_ref[sh_row:sh_row + 1, :]).astype(BF16)

    return pl.pallas_call(
        body, out_shape=(jax.ShapeDtypeStruct((SEQ, D_MODEL), F32), jax.ShapeDtypeStruct((SEQ, D_MODEL), BF16)),
        grid=(SEQ // ROWS,),
        in_specs=[_row_spec(D_MODEL), _row_spec(D_MODEL), _vec_spec(D_MODEL), _vec_spec(D_MODEL, 8)],
        out_specs=(_row_spec(D_MODEL), _row_spec(D_MODEL)), name=name, compiler_params=_cp("parallel"))(x, y, g, mod)


def final_loss_bwd(x1, y2, tgt, g, mod, ga_row, name):
    def body(x1_ref, y2_ref, t_ref, g_ref, mod_ref, loss_ref, dx2_ref, dy2_ref, dg_ref, dga_ref):
        first = pl.program_id(0) == 0
        ga = mod_ref[ga_row:ga_row + 1, :]
        y2 = y2_ref[...]
        x2 = x1_ref[...] + ga * y2
        r = lax.rsqrt(_mean(x2 * x2) + EPS)
        xn = x2 * r
        err = xn * g_ref[...] - t_ref[...]
        _acc(loss_ref, jnp.broadcast_to(0.5 * jnp.sum(_mean(err * err)), (8, 128)), first)
        dy = err * (1.0 / D_MODEL)
        _acc(dg_ref, _rsum(dy * xn), first)
        dxn = dy * g_ref[...]
        dx2 = r * (dxn - xn * _mean(dxn * xn))
        dx2_ref[...] = dx2
        dy2_ref[...] = (dx2 * ga).astype(BF16)
        _acc(dga_ref, _rsum(dx2 * y2), first)

    vec = jax.ShapeDtypeStruct((1, D_MODEL), F32)
    return pl.pallas_call(
        body,
        out_shape=(jax.ShapeDtypeStruct((8, 128), F32), jax.ShapeDtypeStruct((SEQ, D_MODEL), F32),
                   jax.ShapeDtypeStruct((SEQ, D_MODEL), BF16), vec, vec),
        grid=(SEQ // ROWS,),
        in_specs=[_row_spec(D_MODEL), _row_spec(D_MODEL), _row_spec(D_MODEL), _vec_spec(D_MODEL), _vec_spec(D_MODEL, 8)],
        out_specs=(pl.BlockSpec((8, 128), lambda i: (0, 0)), _row_spec(D_MODEL), _row_spec(D_MODEL),
                   _vec_spec(D_MODEL), _vec_spec(D_MODEL)),
        name=name, compiler_params=_cp("arbitrary"))(x1, y2, tgt, g, mod)


def rms_mod_bwd(x, dh, dres, g, mod, sc_row, y, ga_row, name):
    gated = y is not None

    def body(*refs):
        if gated:
            x_ref, dh_ref, dres_ref, g_ref, mod_ref, y_ref, dx_ref, dsh_ref, dsc_ref, dg_ref, dy_ref, dga_ref = refs
        else:
            x_ref, dh_ref, dres_ref, g_ref, mod_ref, dx_ref, dsh_ref, dsc_ref, dg_ref = refs
        first = pl.program_id(0) == 0
        xx = x_ref[...]
        dh = dh_ref[...]
        gg = g_ref[...]
        r = lax.rsqrt(_mean(xx * xx) + EPS)
        xn = xx * r
        _acc(dsh_ref, _rsum(dh), first)
        _acc(dsc_ref, _rsum(dh * (xn * gg)), first)
        dt = dh * (1.0 + mod_ref[sc_row:sc_row + 1, :])
        _acc(dg_ref, _rsum(dt * xn), first)
        dxn = dt * gg
        dx = dres_ref[...] + r * (dxn - xn * _mean(dxn * xn))
        dx_ref[...] = dx
        if gated:
            _acc(dga_ref, _rsum(dx * y_ref[...]), first)
            dy_ref[...] = (dx * mod_ref[ga_row:ga_row + 1, :]).astype(BF16)

    vec = jax.ShapeDtypeStruct((1, D_MODEL), F32)
    in_specs = [_row_spec(D_MODEL), _row_spec(D_MODEL), _row_spec(D_MODEL), _vec_spec(D_MODEL), _vec_spec(D_MODEL, 8)]
    out_shape = [jax.ShapeDtypeStruct((SEQ, D_MODEL), F32), vec, vec, vec]
    out_specs = [_row_spec(D_MODEL), _vec_spec(D_MODEL), _vec_spec(D_MODEL), _vec_spec(D_MODEL)]
    args = [x, dh, dres, g, mod]
    if gated:
        in_specs.append(_row_spec(D_MODEL))
        out_shape += [jax.ShapeDtypeStruct((SEQ, D_MODEL), BF16), vec]
        out_specs += [_row_spec(D_MODEL), _vec_spec(D_MODEL)]
        args.append(y)
    return pl.pallas_call(
        body, out_shape=tuple(out_shape), grid=(SEQ // ROWS,), in_specs=in_specs, out_specs=tuple(out_specs),
        name=name, compiler_params=_cp("arbitrary"))(*args)


def _prev_halo(halo, width, col):
    per = ROWS // halo
    return pl.BlockSpec((halo, width), lambda i: (jnp.maximum(i * per - 1, 0), col))


def _next_halo(halo, width, col):
    per = ROWS // halo
    last = SEQ // halo - 1
    return pl.BlockSpec((halo, width), lambda i: (jnp.minimum((i + 1) * per, last), col))


CONV_PAD = ROWS + CONV_HALO


def _shift_copies(sh):
    for b in range(1, 8):
        sh[b, 0:CONV_PAD - 8, :] = sh[0, pl.ds(b, CONV_PAD - 8), :]


def _tap(sh, rs_start, offset):
    return sh[offset % 8, pl.ds(pl.multiple_of(rs_start + (offset // 8) * 8, 8), SUB), :]


def _conv_module_forward(av_ref, ag_ref, avh_ref, agh_ref, wc_ref, bc_ref, lg_ref, lb_ref, sh, u1_s):
    i = pl.program_id(0)
    hv = avh_ref[...] * _sig(agh_ref[...])
    sh[0, 0:CONV_HALO, :] = jnp.where(i > 0, hv, 0.0)

    def glu(rs):
        sh[0, pl.ds(pl.multiple_of(rs.start + CONV_HALO, SUB), SUB), :] = av_ref[rs, :] * _sig(ag_ref[rs, :])

    _for_chunks(glu)
    _shift_copies(sh)

    def conv(rs):
        u1 = jnp.broadcast_to(bc_ref[...], (SUB, D_CONV))
        for j in range(CONV_K):
            u1 = u1 + wc_ref[j:j + 1, :] * _tap(sh, rs.start, CONV_HALO - (CONV_K - 1) + j)
        u1_s[rs, :] = u1

    _for_chunks(conv)
    u1 = u1_s[...]
    mu = _mean(u1)
    cen = u1 - mu
    rs = lax.rsqrt(_mean(cen * cen) + EPS)
    z = cen * rs
    ln = z * lg_ref[...] + lb_ref[...]
    s = _sig(ln)
    return z, rs, ln, s, ln * s


_CONV_SCRATCH = [pltpu.VMEM((8, CONV_PAD, D_CONV), F32), pltpu.VMEM((ROWS, D_CONV), F32)]


def conv_module_fwd(proj, wc, bc, lg, lb, gco, name):
    def body(av_ref, ag_ref, avh_ref, agh_ref, wc_ref, bc_ref, lg_ref, lb_ref, gco_ref, out_ref, sh, u1_s):
        _, _, _, _, u2 = _conv_module_forward(av_ref, ag_ref, avh_ref, agh_ref, wc_ref, bc_ref, lg_ref, lb_ref, sh, u1_s)
        rc = lax.rsqrt(_mean(u2 * u2) + EPS)
        out_ref[...] = (u2 * rc * gco_ref[...]).astype(BF16)

    v = _vec_spec(D_CONV)
    return pl.pallas_call(
        body, out_shape=jax.ShapeDtypeStruct((SEQ, D_CONV), BF16), grid=(SEQ // ROWS,),
        in_specs=[_row_spec(D_CONV, 0), _row_spec(D_CONV, 1), _prev_halo(CONV_HALO, D_CONV, 0),
                  _prev_halo(CONV_HALO, D_CONV, 1), _vec_spec(D_CONV, CONV_K), v, v, v, v],
        out_specs=_row_spec(D_CONV), scratch_shapes=list(_CONV_SCRATCH),
        name=name, compiler_params=_cp("parallel"))(proj, proj, proj, proj, wc, bc, lg, lb, gco)


def conv_module_bwd_a(proj, dmixed, wc, bc, lg, lb, gco, name):
    def body(av_ref, ag_ref, avh_ref, agh_ref, dm_ref, wc_ref, bc_ref, lg_ref, lb_ref, gco_ref,
             du1_ref, dgco_ref, dlg_ref, dlb_ref, dbc_ref, dwc_ref, sh, u1_s, acc):
        first = pl.program_id(0) == 0
        z, rs, ln, s, u2 = _conv_module_forward(av_ref, ag_ref, avh_ref, agh_ref, wc_ref, bc_ref, lg_ref, lb_ref, sh, u1_s)
        rc = lax.rsqrt(_mean(u2 * u2) + EPS)
        xn = u2 * rc
        dm = dm_ref[...]
        _acc(dgco_ref, _rsum(dm * xn), first)
        dyn = dm * gco_ref[...]
        du2 = rc * (dyn - xn * _mean(dyn * xn))
        dln = du2 * (s * (1.0 + ln * (1.0 - s)))
        _acc(dlg_ref, _rsum(dln * z), first)
        _acc(dlb_ref, _rsum(dln), first)
        dz = dln * lg_ref[...]
        du1 = rs * (dz - _mean(dz) - z * _mean(dz * z))
        du1_ref[...] = du1
        _acc(dbc_ref, _rsum(du1), first)
        acc[...] = jnp.zeros_like(acc)

        def taps(rs):
            d = du1_ref[rs, :]
            for j in range(CONV_K):
                acc[j] += d * _tap(sh, rs.start, CONV_HALO - (CONV_K - 1) + j)

        _for_chunks(taps)

        @pl.when(first)
        def _():
            dwc_ref[...] = jnp.zeros_like(dwc_ref)

        for j in range(CONV_K):
            dwc_ref[j:j + 1, :] += _rsum(acc[j])

    v = _vec_spec(D_CONV)
    vec = jax.ShapeDtypeStruct((1, D_CONV), F32)
    return pl.pallas_call(
        body,
        out_shape=(jax.ShapeDtypeStruct((SEQ, D_CONV), F32), vec, vec, vec, vec, jax.ShapeDtypeStruct((CONV_K, D_CONV), F32)),
        grid=(SEQ // ROWS,),
        in_specs=[_row_spec(D_CONV, 0), _row_spec(D_CONV, 1), _prev_halo(CONV_HALO, D_CONV, 0),
                  _prev_halo(CONV_HALO, D_CONV, 1), _row_spec(D_CONV, 0), _vec_spec(D_CONV, CONV_K), v, v, v, v],
        out_specs=(_row_spec(D_CONV), v, v, v, v, _vec_spec(D_CONV, CONV_K)),
        scratch_shapes=list(_CONV_SCRATCH) + [pltpu.VMEM((CONV_K, SUB, D_CONV), F32)],
        name=name, compiler_params=_cp("arbitrary"))(proj, proj, proj, proj, dmixed, wc, bc, lg, lb, gco)


def conv_module_bwd_b(proj, du1, wc, name):
    def body(av_ref, ag_ref, du1_ref, du1n_ref, wc_ref, out_ref, sh):
        i = pl.program_id(0)
        sh[0, 0:ROWS, :] = du1_ref[...]
        sh[0, ROWS:, :] = jnp.where(i < SEQ // ROWS - 1, du1n_ref[...], 0.0)
        _shift_copies(sh)

        def chunk(rs):
            du0 = jnp.zeros((SUB, D_CONV), F32)
            for j in range(CONV_K):
                du0 = du0 + wc_ref[j:j + 1, :] * _tap(sh, rs.start, CONV_K - 1 - j)
            sg = _sig(ag_ref[rs, :])
            out_ref[rs, 0:D_CONV] = (du0 * sg).astype(BF16)
            out_ref[rs, D_CONV:] = (du0 * av_ref[rs, :] * sg * (1.0 - sg)).astype(BF16)

        _for_chunks(chunk)

    return pl.pallas_call(
        body, out_shape=jax.ShapeDtypeStruct((SEQ, 2 * D_CONV), BF16), grid=(SEQ // ROWS,),
        in_specs=[_row_spec(D_CONV, 0), _row_spec(D_CONV, 1), _row_spec(D_CONV, 0), _next_halo(CONV_HALO, D_CONV, 0),
                  _vec_spec(D_CONV, CONV_K)],
        out_specs=_row_spec(2 * D_CONV), scratch_shapes=[pltpu.VMEM((8, CONV_PAD, D_CONV), F32)],
        name=name, compiler_params=_cp("parallel"))(proj, proj, du1, du1, wc)


def _attn_specs(sub_len, pairs):
    ng, width = 4 // pairs, 128 * pairs
    q = pl.BlockSpec((sub_len, width), lambda rho, g: (0, rho * 3 * ng + g))
    k = pl.BlockSpec((sub_len, width), lambda rho, g: (0, rho * 3 * ng + ng + g))
    v = pl.BlockSpec((sub_len, width), lambda rho, g: (0, rho * 3 * ng + 2 * ng + g))
    o = pl.BlockSpec((sub_len, width), lambda rho, g: (0, rho * ng + g))
    return q, k, v, o


ATTN_PAIRS = {1: 1, 4: 1, 16: 4}


def _attn_block(q_ref, k_ref, v_ref, n, hs, win):
    q0 = pl.multiple_of(n * ATTN_BLOCK, ATTN_BLOCK)
    k0 = pl.multiple_of(jnp.maximum(n - 1, 0) * ATTN_BLOCK, ATTN_BLOCK)
    qb = q_ref[pl.ds(q0, ATTN_BLOCK), hs]
    kw = k_ref[pl.ds(k0, win), hs]
    vw = v_ref[pl.ds(k0, win), hs]
    s = lax.dot_general(qb, kw, (((1,), (1,)), ((), ())), preferred_element_type=F32) * (HEAD_DIM ** -0.5)
    dist = (q0 - k0) + lax.broadcasted_iota(jnp.int32, (ATTN_BLOCK, win), 0) \
        - lax.broadcasted_iota(jnp.int32, (ATTN_BLOCK, win), 1)
    s = jnp.where((dist >= 0) & (dist <= ATTN_BLOCK), s, NEG)
    return q0, k0, qb, kw, vw, s


def attn_fwd(qkv_r, sub_len, r, name):
    nb = sub_len // ATTN_BLOCK
    win = 2 * ATTN_BLOCK if nb > 1 else ATTN_BLOCK
    pairs = ATTN_PAIRS[r]

    def body(q_ref, k_ref, v_ref, o_ref, l_ref):
        def block(n, carry):
            for h in range(2 * pairs):
                hs = slice(h * HEAD_DIM, (h + 1) * HEAD_DIM)
                q0, _, _, _, vw, s = _attn_block(q_ref, k_ref, v_ref, n, hs, win)
                m = jnp.max(s, axis=1, keepdims=True)
                p = jnp.exp(s - m)
                den = jnp.sum(p, axis=1, keepdims=True)
                o = jnp.dot(p.astype(BF16), vw, preferred_element_type=F32) / den
                o_ref[pl.ds(q0, ATTN_BLOCK), hs] = o
                l_ref[pl.ds(q0, ATTN_BLOCK), hs] = jnp.broadcast_to(m + jnp.log(den), (ATTN_BLOCK, HEAD_DIM))
            return carry

        lax.fori_loop(0, nb, block, 0, unroll=min(nb, 2))

    q, k, v, o = _attn_specs(sub_len, pairs)
    shp = jax.ShapeDtypeStruct((sub_len, r * D_ATTN), F32)
    return pl.pallas_call(
        body, out_shape=(shp, shp), grid=(r, 4 // pairs), in_specs=[q, k, v], out_specs=(o, o),
        name=name, compiler_params=_cp("parallel", "parallel"))(qkv_r, qkv_r, qkv_r)


def attn_bwd(qkv_r, do_r, lse_r, dd_r, sub_len, r, name):
    nb = sub_len // ATTN_BLOCK
    win = 2 * ATTN_BLOCK if nb > 1 else ATTN_BLOCK
    pairs = ATTN_PAIRS[r]

    def body(q_ref, k_ref, v_ref, do_ref, l_ref, dd_ref, dq_ref, dk_ref, dv_ref):
        dk_ref[...] = jnp.zeros_like(dk_ref)
        dv_ref[...] = jnp.zeros_like(dv_ref)

        def block(n, carry):
            for h in range(2 * pairs):
                hs = slice(h * HEAD_DIM, (h + 1) * HEAD_DIM)
                h1 = slice(h * HEAD_DIM, h * HEAD_DIM + 1)
                q0, k0, qb, kw, vw, s = _attn_block(q_ref, k_ref, v_ref, n, hs, win)
                dob = do_ref[pl.ds(q0, ATTN_BLOCK), hs]
                p = jnp.exp(s - l_ref[pl.ds(q0, ATTN_BLOCK), h1])
                dp = lax.dot_general(dob, vw, (((1,), (1,)), ((), ())), preferred_element_type=F32)
                ds = (p * (dp - dd_ref[pl.ds(q0, ATTN_BLOCK), h1]) * (HEAD_DIM ** -0.5)).astype(BF16)
                dq_ref[pl.ds(q0, ATTN_BLOCK), hs] = jnp.dot(ds, kw, preferred_element_type=F32)
                dk_ref[pl.ds(k0, win), hs] += lax.dot_general(ds, qb, (((0,), (0,)), ((), ())), preferred_element_type=F32)
                dv_ref[pl.ds(k0, win), hs] += lax.dot_general(p.astype(BF16), dob, (((0,), (0,)), ((), ())),
                                                              preferred_element_type=F32)
            return carry

        lax.fori_loop(0, nb, block, 0)

    q, k, v, o = _attn_specs(sub_len, pairs)
    shp = jax.ShapeDtypeStruct((sub_len, r * D_ATTN), F32)
    return pl.pallas_call(
        body, out_shape=(shp, shp, shp), grid=(r, 4 // pairs), in_specs=[q, k, v, o, o, o], out_specs=(o, o, o),
        name=name, compiler_params=_cp("parallel", "parallel"))(qkv_r, qkv_r, qkv_r, do_r, lse_r, dd_r)


def _rows(start, size, r):
    return pl.ds(start, size) if r == 1 else pl.ds(start, size, stride=r)


def _unit_rows(r, rho, n, nb):
    win = 2 * ATTN_BLOCK if nb > 1 else ATTN_BLOCK
    if isinstance(n, int):
        kb = max(n - 1, 0)
        q_rows = _rows(rho + r * ATTN_BLOCK * n, ATTN_BLOCK, r)
        k_rows = _rows(rho + r * ATTN_BLOCK * kb, win, r)
    else:
        kb = jnp.maximum(n - 1, 0)
        q_rows = pl.ds(pl.multiple_of(n * ATTN_BLOCK, ATTN_BLOCK), ATTN_BLOCK)
        k_rows = pl.ds(pl.multiple_of(kb * ATTN_BLOCK, ATTN_BLOCK), win)
    dist = (n - kb) * ATTN_BLOCK + lax.broadcasted_iota(jnp.int32, (ATTN_BLOCK, win), 0) \
        - lax.broadcasted_iota(jnp.int32, (ATTN_BLOCK, win), 1)
    return q_rows, k_rows, (dist >= 0) & (dist <= ATTN_BLOCK)


def _per_head(x):
    lane = lax.broadcasted_iota(jnp.int32, x.shape, 1)
    zero = jnp.zeros_like(x)
    return [jnp.where(lane < HEAD_DIM, x, zero), jnp.where(lane >= HEAD_DIM, x, zero)]


NT = (((1,), (1,)), ((), ()))


def _masked_scores(q2, k2, valid):
    return [jnp.where(valid, lax.dot_general(qh, k2, NT, preferred_element_type=F32) * (HEAD_DIM ** -0.5), NEG)
            for qh in _per_head(q2)]


def _attn_units(r, nb, unit):
    if r == 1:
        def four(i, carry):
            for k in range(4):
                unit(0, 4 * i + k)
            return carry
        lax.fori_loop(0, nb // 4, four, 0)
    else:
        for rho in range(r):
            for n in range(nb):
                unit(rho, n)


N_UNITS = 16


def attn_fwd_all(proj, name):
    def body(q_ref, k_ref, v_ref, att_ref, lse_ref, s_scr, p_scr, lse_scr, den_scr):
        for idx, (sub_len, r) in enumerate(PATTERNS):
            nb = sub_len // ATTN_BLOCK
            win = 2 * ATTN_BLOCK if nb > 1 else ATTN_BLOCK

            def scores(rho, n, r=r, nb=nb, win=win):
                u = rho * nb + n
                q_rows, k_rows, valid = _unit_rows(r, rho, n, nb)
                ss = _masked_scores(q_ref[q_rows, :].astype(BF16), k_ref[k_rows, :].astype(BF16), valid)
                for h in range(2):
                    s_scr[2 * u + h, :, 0:win] = ss[h]

            _attn_units(r, nb, scores)

            def softmax(u, carry, win=win):
                lses, dens = [], []
                for h in range(2):
                    sc = s_scr[2 * u + h, :, 0:win]
                    m = jnp.max(sc, axis=1, keepdims=True)
                    p = jnp.exp(sc - m)
                    den = jnp.sum(p, axis=1, keepdims=True)
                    p_scr[2 * u + h, :, 0:win] = p.astype(BF16)
                    lses.append(jnp.broadcast_to(m + jnp.log(den), (ATTN_BLOCK, HEAD_DIM)))
                    dens.append(jnp.broadcast_to(den, (ATTN_BLOCK, HEAD_DIM)))
                lse_scr[u] = jnp.concatenate(lses, axis=1)
                den_scr[u] = jnp.concatenate(dens, axis=1)
                return carry

            lax.fori_loop(0, N_UNITS, softmax, 0, unroll=2)

            def outputs(rho, n, r=r, nb=nb, win=win, idx=idx):
                u = rho * nb + n
                q_rows, k_rows, _ = _unit_rows(r, rho, n, nb)
                vs = _per_head(v_ref[k_rows, :].astype(BF16))
                o = (jnp.dot(p_scr[2 * u, :, 0:win], vs[0], preferred_element_type=F32)
                     + jnp.dot(p_scr[2 * u + 1, :, 0:win], vs[1], preferred_element_type=F32)) / den_scr[u]
                lse = lse_scr[u]
                if idx > 0:
                    old = lse_ref[q_rows, :]
                    top = jnp.maximum(old, lse)
                    new = top + jnp.log(jnp.exp(old - top) + jnp.exp(lse - top))
                    o = att_ref[q_rows, :] * jnp.exp(old - new) + o * jnp.exp(lse - new)
                    lse = new
                att_ref[q_rows, :] = o
                lse_ref[q_rows, :] = lse

            _attn_units(r, nb, outputs)

    blk = lambda first: pl.BlockSpec((SEQ, 128), lambda g: (0, first + g))
    shp = jax.ShapeDtypeStruct((SEQ, D_ATTN), F32)
    big = (2 * N_UNITS, ATTN_BLOCK, 2 * ATTN_BLOCK)
    small = pltpu.VMEM((N_UNITS, ATTN_BLOCK, 128), F32)
    return pl.pallas_call(
        body, out_shape=(shp, shp), grid=(4,), in_specs=[blk(8), blk(12), blk(16)], out_specs=(blk(0), blk(0)),
        scratch_shapes=[pltpu.VMEM(big, F32), pltpu.VMEM(big, BF16), small, small],
        name=name, compiler_params=_cp("parallel"))(proj, proj, proj)


def attn_bwd_all(proj, do, lse, dd, name):
    scale = HEAD_DIM ** -0.5

    def body(q_ref, k_ref, v_ref, do_ref, l_ref, dd_ref, out_ref, dq_s, dk_s, dv_s,
             s_scr, dp_scr, ds_scr, st_scr, dpt_scr, pt_scr, dst_scr, qb_scr, kb_scr, dob_scr):
        dq_s[...] = jnp.zeros_like(dq_s)
        dk_s[...] = jnp.zeros_like(dk_s)
        dv_s[...] = jnp.zeros_like(dv_s)
        for sub_len, r in PATTERNS:
            nb = sub_len // ATTN_BLOCK
            win = 2 * ATTN_BLOCK if nb > 1 else ATTN_BLOCK

            def scores(rho, n, r=r, nb=nb, win=win):
                u = rho * nb + n
                q_rows, k_rows, valid = _unit_rows(r, rho, n, nb)
                kb = max(n - 1, 0) if isinstance(n, int) else jnp.maximum(n - 1, 0)
                dist_t = (n - kb) * ATTN_BLOCK + lax.broadcasted_iota(jnp.int32, (win, ATTN_BLOCK), 1) \
                    - lax.broadcasted_iota(jnp.int32, (win, ATTN_BLOCK), 0)
                valid_t = (dist_t >= 0) & (dist_t <= ATTN_BLOCK)
                q2 = q_ref[q_rows, :].astype(BF16)
                k2 = k_ref[k_rows, :].astype(BF16)
                do2 = do_ref[q_rows, :].astype(BF16)
                qb_scr[u] = q2
                kb_scr[u, 0:win, :] = k2
                dob_scr[u] = do2
                l2 = l_ref[q_rows, :]
                d2 = dd_ref[q_rows, :]
                l2t = l2.T
                d2t = d2.T
                v2 = v_ref[k_rows, :].astype(BF16)
                qs, dos = _per_head(q2), _per_head(do2)
                for h in range(2):
                    c0 = h * HEAD_DIM
                    sc = lax.dot_general(qs[h], k2, NT, preferred_element_type=F32) * scale
                    s_scr[2 * u + h, :, 0:win] = jnp.where(valid, sc, NEG) - l2[:, c0:c0 + 1]
                    dp_scr[2 * u + h, :, 0:win] = lax.dot_general(dos[h], v2, NT, preferred_element_type=F32) \
                        - d2[:, c0:c0 + 1]
                    sct = lax.dot_general(k2, qs[h], NT, preferred_element_type=F32) * scale
                    st_scr[2 * u + h, 0:win, :] = jnp.where(valid_t, sct, NEG) - l2t[c0:c0 + 1, :]
                    dpt_scr[2 * u + h, 0:win, :] = lax.dot_general(v2, dos[h], NT, preferred_element_type=F32) \
                        - d2t[c0:c0 + 1, :]

            _attn_units(r, nb, scores)

            def pointwise(hu, carry, win=win):
                ds_scr[hu, :, 0:win] = (jnp.exp(s_scr[hu, :, 0:win]) * dp_scr[hu, :, 0:win] * scale).astype(BF16)
                pt = jnp.exp(st_scr[hu, 0:win, :])
                pt_scr[hu, 0:win, :] = pt.astype(BF16)
                dst_scr[hu, 0:win, :] = (pt * dpt_scr[hu, 0:win, :] * scale).astype(BF16)
                return carry

            lax.fori_loop(0, 2 * N_UNITS, pointwise, 0, unroll=4)

            def grads(rho, n, r=r, nb=nb, win=win):
                u = rho * nb + n
                q_rows, k_rows, _ = _unit_rows(r, rho, n, nb)
                qs, ks, dos = _per_head(qb_scr[u]), _per_head(kb_scr[u, 0:win, :]), _per_head(dob_scr[u])

                def both(scr, rows, rhs):
                    return (jnp.dot(scr[(2 * u,) + rows], rhs[0], preferred_element_type=F32)
                            + jnp.dot(scr[(2 * u + 1,) + rows], rhs[1], preferred_element_type=F32))

                dq_s[q_rows, :] += both(ds_scr, (slice(None), slice(0, win)), ks)
                dk_s[k_rows, :] += both(dst_scr, (slice(0, win), slice(None)), qs)
                dv_s[k_rows, :] += both(pt_scr, (slice(0, win), slice(None)), dos)

            _attn_units(r, nb, grads)
        out_ref[0] = dq_s[...].astype(BF16)
        out_ref[1] = dk_s[...].astype(BF16)
        out_ref[2] = dv_s[...].astype(BF16)

    blk = lambda first: pl.BlockSpec((SEQ, 128), lambda g: (0, first + g))
    acc = pltpu.VMEM((SEQ, 128), F32)
    big = (2 * N_UNITS, ATTN_BLOCK, 2 * ATTN_BLOCK)
    big_t = (2 * N_UNITS, 2 * ATTN_BLOCK, ATTN_BLOCK)
    return pl.pallas_call(
        body, out_shape=jax.ShapeDtypeStruct((3, SEQ, D_ATTN), BF16), grid=(4,),
        in_specs=[blk(8), blk(12), blk(16), blk(0), blk(0), blk(0)],
        out_specs=pl.BlockSpec((3, SEQ, 128), lambda g: (0, 0, g)),
        scratch_shapes=[acc, acc, acc, pltpu.VMEM(big, F32), pltpu.VMEM(big, F32), pltpu.VMEM(big, BF16),
                        pltpu.VMEM(big_t, F32), pltpu.VMEM(big_t, F32), pltpu.VMEM(big_t, BF16), pltpu.VMEM(big_t, BF16),
                        pltpu.VMEM((N_UNITS, ATTN_BLOCK, 128), BF16),
                        pltpu.VMEM((N_UNITS, 2 * ATTN_BLOCK, 128), BF16), pltpu.VMEM((N_UNITS, ATTN_BLOCK, 128), BF16)],
        name=name, compiler_params=_cp("parallel"))(proj, proj, proj, do, lse, dd)


def rms_gain_bf16(a, g, name):
    def body(a_ref, g_ref, o_ref):
        aa = a_ref[...]
        o_ref[...] = (aa * lax.rsqrt(_mean(aa * aa) + EPS) * g_ref[...]).astype(BF16)

    w = a.shape[1]
    return pl.pallas_call(
        body, out_shape=jax.ShapeDtypeStruct(a.shape, BF16), grid=(SEQ // ROWS,), in_specs=[_row_spec(w), _vec_spec(w)],
        out_specs=_row_spec(w), name=name, compiler_params=_cp("parallel"))(a, g)


def attn_combine_fwd(outs, lses, gao, name):
    def body(o1, o2, o3, l1, l2, l3, g_ref, att_ref, lse_ref, mix_ref):
        a1, a2, a3 = l1[...], l2[...], l3[...]
        m = jnp.maximum(jnp.maximum(a1, a2), a3)
        w1, w2, w3 = jnp.exp(a1 - m), jnp.exp(a2 - m), jnp.exp(a3 - m)
        den = w1 + w2 + w3
        att = (w1 * o1[...] + w2 * o2[...] + w3 * o3[...]) / den
        att_ref[...] = att
        lse_ref[...] = m + jnp.log(den)
        mix_ref[...] = (att * lax.rsqrt(_mean(att * att) + EPS) * g_ref[...]).astype(BF16)

    rs = _row_spec(D_ATTN)
    f = jax.ShapeDtypeStruct((SEQ, D_ATTN), F32)
    return pl.pallas_call(
        body, out_shape=(f, f, jax.ShapeDtypeStruct((SEQ, D_ATTN), BF16)), grid=(SEQ // ROWS,),
        in_specs=[rs] * 6 + [_vec_spec(D_ATTN)], out_specs=(rs, rs, rs),
        name=name, compiler_params=_cp("parallel"))(*outs, *lses, gao)


def attn_combine_bwd(dmixed, att, gao, name):
    def body(dm_ref, att_ref, g_ref, do_ref, dd_ref, dg_ref):
        first = pl.program_id(0) == 0
        att = att_ref[...]
        r = lax.rsqrt(_mean(att * att) + EPS)
        xn = att * r
        dm = dm_ref[...]
        _acc(dg_ref, _rsum(dm * xn), first)
        dyn = dm * g_ref[...]
        do = r * (dyn - xn * _mean(dyn * xn))
        do_ref[...] = do
        same_head = (jnp.right_shift(lax.broadcasted_iota(jnp.int32, (D_ATTN, D_ATTN), 0), 6)
                     == jnp.right_shift(lax.broadcasted_iota(jnp.int32, (D_ATTN, D_ATTN), 1), 6)).astype(F32)
        dd_ref[...] = jnp.dot(do * att, same_head, preferred_element_type=F32, precision=lax.Precision.HIGHEST)

    rs = _row_spec(D_ATTN)
    return pl.pallas_call(
        body,
        out_shape=(jax.ShapeDtypeStruct((SEQ, D_ATTN), F32), jax.ShapeDtypeStruct((SEQ, D_ATTN), F32),
                   jax.ShapeDtypeStruct((1, D_ATTN), F32)),
        grid=(SEQ // ROWS,), in_specs=[_row_spec(D_ATTN, 1), rs, _vec_spec(D_ATTN)],
        out_specs=(rs, rs, _vec_spec(D_ATTN)), name=name, compiler_params=_cp("arbitrary"))(dmixed, att, gao)


def sum3_bf16(a, b, c, name):
    def body(a_ref, b_ref, c_ref, o_ref):
        o_ref[...] = (a_ref[...] + b_ref[...] + c_ref[...]).astype(BF16)

    w = a.shape[1]
    rs = _row_spec(w)
    return pl.pallas_call(
        body, out_shape=jax.ShapeDtypeStruct(a.shape, BF16), grid=(SEQ // ROWS,), in_specs=[rs, rs, rs], out_specs=rs,
        name=name, compiler_params=_cp("parallel"))(a, b, c)


N_FT = D_FF // FFN_TN


def _ffn_specs():
    per = ROWS // FFN_HALO
    cur_g = pl.BlockSpec((ROWS, FFN_TN), lambda j, i: (i, j))
    cur_v = pl.BlockSpec((ROWS, FFN_TN), lambda j, i: (i, j + N_FT))
    halo_g = pl.BlockSpec((FFN_HALO, FFN_TN), lambda j, i: (jnp.maximum(i * per - 1, 0), j))
    halo_v = pl.BlockSpec((FFN_HALO, FFN_TN), lambda j, i: (jnp.maximum(i * per - 1, 0), j + N_FT))
    w_g = pl.BlockSpec((FFN_K, FFN_TN), lambda j, i: (0, j))
    w_v = pl.BlockSpec((FFN_K, FFN_TN), lambda j, i: (0, j + N_FT))
    b_g = pl.BlockSpec((1, FFN_TN), lambda j, i: (0, j))
    b_v = pl.BlockSpec((1, FFN_TN), lambda j, i: (0, j + N_FT))
    return [cur_g, cur_v, halo_g, halo_v, w_g, w_v, b_g, b_v]


def matmul(a, b, kind, out_dtype, tm, tn, name, b_rows=None):
    stacked = b_rows is not None
    b_shape = (N_DEV * b_rows, b.shape[2]) if stacked else b.shape
    if kind == "nn":
        (m, k), n = a.shape, b_shape[1]
        a_spec = pl.BlockSpec((tm, k), lambda j, i: (i, 0))
        b_spec = pl.BlockSpec((k, tn), lambda j, i: (0, j))
        dims = (((1,), (0,)), ((), ()))
    elif kind == "nt":
        (m, k), n = a.shape, b_shape[0]
        a_spec = pl.BlockSpec((tm, k), lambda j, i: (i, 0))
        b_spec = pl.BlockSpec((tn, k), lambda j, i: (j, 0))
        dims = (((1,), (1,)), ((), ()))
    else:
        (k, m), n = a.shape, b_shape[1]
        a_spec = pl.BlockSpec((k, tm), lambda j, i: (0, i))
        b_spec = pl.BlockSpec((k, tn), lambda j, i: (0, j))
        dims = (((0,), (0,)), ((), ()))
    assert m % tm == 0 and n % tn == 0, (name, m, n, tm, tn)
    if stacked:
        assert b_spec.block_shape[0] == b_shape[0] and kind in ("nn", "nt")
        width = b_spec.block_shape[1]
        b_spec = pl.BlockSpec((N_DEV, b_rows, width), (lambda j, i: (0, 0, j)) if kind == "nn" else (lambda j, i: (0, 0, 0)))

    def body(a_ref, b_ref, o_ref):
        bb = b_ref[...].reshape(b_shape[0], -1) if stacked else b_ref[...]
        o_ref[...] = lax.dot_general(a_ref[...], bb, dims, preferred_element_type=F32).astype(o_ref.dtype)

    return pl.pallas_call(
        body, out_shape=jax.ShapeDtypeStruct((m, n), out_dtype), grid=(n // tn, m // tm),
        in_specs=[a_spec, b_spec], out_specs=pl.BlockSpec((tm, tn), lambda j, i: (i, j)),
        name=name, compiler_params=_cp("parallel", "parallel"))(a, b)


def matmul_halves(a, b, tm, tn, name):
    _, m, k = a.shape
    n = b.shape[1]
    b3 = b.reshape(2, k, n)

    def body(a_ref, b_ref, o_ref):
        o_ref[...] = (jnp.dot(a_ref[0], b_ref[0], preferred_element_type=F32)
                      + jnp.dot(a_ref[1], b_ref[1], preferred_element_type=F32))

    return pl.pallas_call(
        body, out_shape=jax.ShapeDtypeStruct((m, n), F32), grid=(n // tn, m // tm),
        in_specs=[pl.BlockSpec((2, tm, k), lambda j, i: (0, i, 0)), pl.BlockSpec((2, k, tn), lambda j, i: (0, 0, j))],
        out_specs=pl.BlockSpec((tm, tn), lambda j, i: (i, j)), name=name, compiler_params=_cp("parallel", "parallel"))(a, b3)


def matmul_tn_halves(a, b, tm, name):
    _, k, m = a.shape
    n = b.shape[1]

    def body(a_ref, b_ref, o_ref):
        o_ref[0] = lax.dot_general(a_ref[0], b_ref[...], (((0,), (0,)), ((), ())), preferred_element_type=F32).astype(BF16)

    return pl.pallas_call(
        body, out_shape=jax.ShapeDtypeStruct((2, m, n), BF16), grid=(2, m // tm),
        in_specs=[pl.BlockSpec((1, k, tm), lambda h, i: (h, 0, i)), pl.BlockSpec((k, n), lambda h, i: (0, 0))],
        out_specs=pl.BlockSpec((1, tm, n), lambda h, i: (h, i, 0)), name=name,
        compiler_params=_cp("parallel", "parallel"))(a, b).reshape(2 * m, n)


def _row_spec(width, col=0):
    return pl.BlockSpec((ROWS, width), lambda i: (i, col))


def _vec_spec(width, rows=1):
    return pl.BlockSpec((rows, width), lambda i: (0, 0))


def _ffn_shifted(cur_ref, halo_ref, pad, s1, s2):
    i = pl.program_id(1)
    pad[0:FFN_HALO, :] = jnp.where(i > 0, halo_ref[...], 0.0)
    pad[FFN_HALO:, :] = cur_ref[0:FFN_HALO, :]
    for k, dst in ((1, s1), (2, s2)):
        dst[0:FFN_HALO, :] = pad[pl.ds(FFN_HALO - k, FFN_HALO), :]
        dst[FFN_HALO:, :] = cur_ref[pl.ds(FFN_HALO - k, ROWS - FFN_HALO), :]


def _ffn_conv(rs, cur_ref, s1, s2, w_ref, b_ref):
    return b_ref[...] + w_ref[0:1, :] * s2[rs, :] + w_ref[1:2, :] * s1[rs, :] + w_ref[2:3, :] * cur_ref[rs, :]


def ffn_act_fwd(up0, wf, bf, name):
    def body(g_ref, v_ref, gh_ref, vh_ref, wg_ref, wv_ref, bg_ref, bv_ref, act_ref, pad, g1, g2, v1, v2):
        _ffn_shifted(g_ref, gh_ref, pad, g1, g2)
        _ffn_shifted(v_ref, vh_ref, pad, v1, v2)

        def chunk(rs):
            gate = _ffn_conv(rs, g_ref, g1, g2, wg_ref, bg_ref)
            val = _ffn_conv(rs, v_ref, v1, v2, wv_ref, bv_ref)
            act_ref[rs, :] = (gate * _sig(gate) * val).astype(BF16)

        _for_chunks(chunk)

    tile = pltpu.VMEM((ROWS, FFN_TN), F32)
    return pl.pallas_call(
        body, out_shape=jax.ShapeDtypeStruct((SEQ, D_FF), BF16), grid=(N_FT, SEQ // ROWS),
        in_specs=_ffn_specs(), out_specs=pl.BlockSpec((ROWS, FFN_TN), lambda j, i: (i, j)),
        scratch_shapes=[pltpu.VMEM((2 * FFN_HALO, FFN_TN), F32), tile, tile, tile, tile],
        name=name, compiler_params=_cp("parallel", "parallel"))(up0, up0, up0, up0, wf, wf, bf, bf)


def ffn_bwd(up0, dact, wf, bf, name):
    per = ROWS // FFN_HALO
    last = SEQ // FFN_HALO - 1

    def body(g_ref, v_ref, gh_ref, vh_ref, wg_ref, wv_ref, bg_ref, bv_ref, da_ref, gn_ref, vn_ref, dan_ref,
             out_ref, dbg_ref, dbv_ref, dwg_ref, dwv_ref, pad, g1, g2, v1, v2, dgp, dvp, acc):
        i = pl.program_id(1)
        first = i == 0
        _ffn_shifted(g_ref, gh_ref, pad, g1, g2)
        _ffn_shifted(v_ref, vh_ref, pad, v1, v2)
        acc[...] = jnp.zeros_like(acc)

        def grads(gate, val, da):
            s = _sig(gate)
            return da * val * (s * (1.0 + gate * (1.0 - s))), da * (gate * s)

        def chunk(rs):
            gate = _ffn_conv(rs, g_ref, g1, g2, wg_ref, bg_ref)
            val = _ffn_conv(rs, v_ref, v1, v2, wv_ref, bv_ref)
            dgate, dval = grads(gate, val, da_ref[rs, :])
            dgp[rs, :] = dgate
            dvp[rs, :] = dval
            acc[0] += dgate
            acc[1] += dval
            for t, (sg, sv) in enumerate(((g2, v2), (g1, v1), (g_ref, v_ref))):
                acc[2 + t] += dgate * sg[rs, :]
                acc[5 + t] += dval * sv[rs, :]

        _for_chunks(chunk)
        _acc(dbg_ref, _rsum(acc[0]), first)
        _acc(dbv_ref, _rsum(acc[1]), first)
        _acc(dwg_ref, jnp.concatenate([_rsum(acc[2 + t]) for t in range(FFN_K)], axis=0), first)
        _acc(dwv_ref, jnp.concatenate([_rsum(acc[5 + t]) for t in range(FFN_K)], axis=0), first)

        def conv_next(cur_ref, nxt_ref, w_ref, b_ref):
            pad[0:FFN_HALO, :] = cur_ref[ROWS - FFN_HALO:, :]
            pad[FFN_HALO:, :] = nxt_ref[...]
            return (b_ref[...] + w_ref[0:1, :] * pad[pl.ds(FFN_HALO - 2, FFN_HALO), :]
                    + w_ref[1:2, :] * pad[pl.ds(FFN_HALO - 1, FFN_HALO), :] + w_ref[2:3, :] * nxt_ref[...])

        gate_n = conv_next(g_ref, gn_ref, wg_ref, bg_ref)
        val_n = conv_next(v_ref, vn_ref, wv_ref, bv_ref)
        dgate_n, dval_n = grads(gate_n, val_n, dan_ref[...])
        inside = i < SEQ // ROWS - 1
        dgp[ROWS:, :] = jnp.where(inside, dgate_n, 0.0)
        dvp[ROWS:, :] = jnp.where(inside, dval_n, 0.0)

        for half, (dp, s1, s2, w_ref) in enumerate(((dgp, g1, g2, wg_ref), (dvp, v1, v2, wv_ref))):
            s1[...] = dp[pl.ds(1, ROWS), :]
            s2[...] = dp[pl.ds(2, ROWS), :]

            def back(rs, dp=dp, s1=s1, s2=s2, w_ref=w_ref, half=half):
                out_ref[half, rs, :] = (w_ref[2:3, :] * dp[rs, :] + w_ref[1:2, :] * s1[rs, :]
                                        + w_ref[0:1, :] * s2[rs, :]).astype(BF16)

            _for_chunks(back)

    tile = pltpu.VMEM((ROWS, FFN_TN), F32)
    ext = pltpu.VMEM((ROWS + FFN_HALO, FFN_TN), F32)
    vec = jax.ShapeDtypeStruct((1, D_FF), F32)
    taps = jax.ShapeDtypeStruct((FFN_K, D_FF), F32)
    cur = pl.BlockSpec((ROWS, FFN_TN), lambda j, i: (i, j))
    nxt = lambda off: pl.BlockSpec((FFN_HALO, FFN_TN), lambda j, i: (jnp.minimum((i + 1) * per, last), j + off))
    vs = pl.BlockSpec((1, FFN_TN), lambda j, i: (0, j))
    ts = pl.BlockSpec((FFN_K, FFN_TN), lambda j, i: (0, j))
    return pl.pallas_call(
        body, out_shape=(jax.ShapeDtypeStruct((2, SEQ, D_FF), BF16), vec, vec, taps, taps), grid=(N_FT, SEQ // ROWS),
        in_specs=_ffn_specs() + [cur, nxt(0), nxt(N_FT), nxt(0)],
        out_specs=(pl.BlockSpec((2, ROWS, FFN_TN), lambda j, i: (0, i, j)), vs, vs, ts, ts),
        scratch_shapes=[pltpu.VMEM((2 * FFN_HALO, FFN_TN), F32), tile, tile, tile, tile, ext, ext,
                        pltpu.VMEM((2 + 2 * FFN_K, SUB, FFN_TN), F32)],
        name=name, compiler_params=_cp("parallel", "arbitrary"))(up0, up0, up0, up0, wf, wf, bf, bf, dact, up0, up0, dact)


def ada_fwd(c_all, w_ada, b_cols, name):
    def body(c_ref, w_ref, b_ref, o_ref):
        cc = c_ref[...]
        sc = (cc * _sig(cc)).astype(BF16)
        o_ref[...] = jnp.dot(sc, w_ref[...].astype(BF16), preferred_element_type=F32) + b_ref[...]

    return pl.pallas_call(body, out_shape=jax.ShapeDtypeStruct((N_DEV, w_ada.shape[1]), F32), name=name,
                          compiler_params=_cp())(c_all, w_ada, b_cols)


def _adam(w, g, m, v):
    m = ADAM_B1 * m + (1.0 - ADAM_B1) * g
    v = ADAM_B2 * v + (1.0 - ADAM_B2) * (g * g)
    m_hat = m / (1.0 - ADAM_B1 ** ADAM_STEP)
    v_hat = v / (1.0 - ADAM_B2 ** ADAM_STEP)
    delta = -ADAM_LR * (m_hat / (jnp.sqrt(v_hat) + ADAM_EPS) + ADAM_WD * w)
    return delta, m, v


def ada_bwd_adamw(c_all_t, dmod_cols, w, m, v, name):
    rows, cols = w.shape
    tr = 256

    def body(ct_ref, dm_ref, w_ref, m_ref, v_ref, g_ref, d_ref, nm_ref, nv_ref):
        def chunk(rs):
            ct = ct_ref[rs, :]
            sc = ct * _sig(ct)
            g = sc[:, 0:1] * dm_ref[0:1, :]
            for b in range(1, N_DEV):
                g = g + sc[:, b:b + 1] * dm_ref[b:b + 1, :]
            g_ref[rs, :] = g
            d_ref[rs, :], nm_ref[rs, :], nv_ref[rs, :] = _adam(w_ref[rs, :], g, m_ref[rs, :], v_ref[rs, :])

        _for_chunks(chunk, 2, tr)

    blk = pl.BlockSpec((tr, cols), lambda i: (i, 0))
    shp = jax.ShapeDtypeStruct((rows, cols), F32)
    return pl.pallas_call(
        body, out_shape=(shp, shp, shp, shp), grid=(rows // tr,),
        in_specs=[pl.BlockSpec((tr, N_DEV), lambda i: (i, 0)), pl.BlockSpec((N_DEV, cols), lambda i: (0, 0)), blk, blk, blk],
        out_specs=(blk, blk, blk, blk), name=name, compiler_params=_cp("parallel"))(c_all_t, dmod_cols, w, m, v)


def sum_adamw(parts, own, w, m, v, tr, name):
    n_parts, rows, cols = parts.shape

    def body(*refs):
        if own is None:
            p_ref, w_ref, m_ref, v_ref, g_ref, d_ref, nm_ref, nv_ref = refs
        else:
            p_ref, own_ref, w_ref, m_ref, v_ref, g_ref, d_ref, nm_ref, nv_ref = refs

        def chunk(rs):
            g = (p_ref[0, rs, :] if own is None else own_ref[rs, :]).astype(F32)
            for k in range(1, n_parts):
                g = g + p_ref[k, rs, :].astype(F32)
            g_ref[rs, :] = g
            d_ref[rs, :], nm_ref[rs, :], nv_ref[rs, :] = _adam(w_ref[rs, :], g, m_ref[rs, :], v_ref[rs, :])

        _for_chunks(chunk, 2 if tr > SUB else 1, tr)

    blk = pl.BlockSpec((tr, cols), lambda i: (i, 0))
    shp = jax.ShapeDtypeStruct((rows, cols), F32)
    args = [parts] + ([] if own is None else [own]) + [w, m, v]
    return pl.pallas_call(
        body, out_shape=(shp, shp, shp, shp), grid=(rows // tr,),
        in_specs=[pl.BlockSpec((n_parts, tr, cols), lambda i: (0, i, 0))] + [blk] * (len(args) - 1),
        out_specs=(blk, blk, blk, blk), name=name, compiler_params=_cp("parallel"))(*args)


MESH = pl.DeviceIdType.MESH
ANY = pl.BlockSpec(memory_space=pl.ANY)


def all_gather(block, name, after=None):
    extra = () if after is None else (after,)

    def body(x_ref, *refs):
        out_ref, send_sems, recv_sems, local_sem = refs[len(extra):]
        x, y, c = lax.axis_index("x"), lax.axis_index("y"), lax.axis_index("c")
        me, sibling = (x, y, c), (x, y, 1 - c)
        chips = [(1 - x, y), (x, 1 - y), (1 - x, 1 - y)]

        def slot(px, py, pc):
            return out_ref.at[4 * px + 2 * py + pc]

        def copy(k, blk, to, src=None):
            return pltpu.make_async_remote_copy(
                src_ref=slot(*blk) if src is None else src, dst_ref=slot(*blk),
                send_sem=send_sems.at[k], recv_sem=recv_sems.at[k], device_id=to, device_id_type=MESH)

        mine = pltpu.make_async_copy(x_ref, slot(*me), local_sem)
        mine.start()
        first = [copy(0, me, sibling, src=x_ref)]
        first += [copy(1 + j, me, (*chip, c), src=x_ref) for j, chip in enumerate(chips)]
        for cp in first:
            cp.start()
        passed = [copy(4 + j, (*chip, c), sibling) for j, chip in enumerate(chips)]
        for j, chip in enumerate(chips):
            copy(1 + j, (*chip, c), me).wait_recv()
            passed[j].start()
        copy(0, sibling, me).wait_recv()
        for j, chip in enumerate(chips):
            copy(4 + j, (*chip, 1 - c), me).wait_recv()
        for cp in first + passed:
            cp.wait_send()
        mine.wait()

    return pl.pallas_call(
        body, out_shape=jax.ShapeDtypeStruct((N_DEV,) + block.shape, block.dtype), in_specs=[ANY] * (1 + len(extra)), out_specs=ANY,
        scratch_shapes=[pltpu.SemaphoreType.DMA((7,)), pltpu.SemaphoreType.DMA((7,)), pltpu.SemaphoreType.DMA],
        name=name)(block, *extra)


HBM = pl.BlockSpec(memory_space=pltpu.HBM)
SEM = pl.BlockSpec(memory_space=pltpu.SEMAPHORE)
EFFECT = pltpu.SideEffectType.DATAFLOW_SIDE_EFFECTING


def _peer_copies(src_ref, land_ref, send_sems, recv_sems, gather):
    x, y, c = lax.axis_index("x"), lax.axis_index("y"), lax.axis_index("c")
    me = 4 * x + 2 * y + c
    copies = []
    for k in range(1, N_DEV):
        px = 1 - x if k & 4 else x
        py = 1 - y if k & 2 else y
        pc = 1 - c if k & 1 else c
        copies.append(pltpu.make_async_remote_copy(
            src_ref=src_ref if gather else src_ref.at[4 * px + 2 * py + pc],
            dst_ref=land_ref.at[me] if gather else land_ref.at[k],
            send_sem=send_sems.at[k - 1], recv_sem=recv_sems.at[k - 1], device_id=(px, py, pc), device_id_type=MESH))
    return copies


def exchange_start(srcs, gather, name, after=None):
    n = len(srcs)
    land_shapes = [(N_DEV,) + src.shape if gather else src.shape for src in srcs]
    extra = () if after is None else (after,)

    def body(*refs):
        src_refs, land_refs = refs[0:n], refs[n:2 * n]
        outs = refs[2 * n + len(extra):]
        for k in range(n):
            for cp in _peer_copies(src_refs[k], land_refs[k], outs[4 * k], outs[4 * k + 1], gather):
                cp.start()
        token = outs[4 * n]
        token[...] = jnp.zeros_like(token)

    out_shape, out_specs, aliases = [], [], {}
    for k, src in enumerate(srcs):
        out_shape += [pltpu.SemaphoreType.DMA((N_DEV - 1,)), pltpu.SemaphoreType.DMA((N_DEV - 1,)),
                      pltpu.HBM(src.shape, src.dtype), pltpu.HBM(land_shapes[k], src.dtype)]
        out_specs += [SEM, SEM, HBM, HBM]
        aliases[k] = 4 * k + 2
        aliases[n + k] = 4 * k + 3
    out_shape.append(jax.ShapeDtypeStruct((8, 128), F32))
    out_specs.append(pl.BlockSpec(memory_space=pltpu.VMEM))
    res = pl.pallas_call(
        body, name=name, out_shape=tuple(out_shape), in_specs=(HBM,) * (2 * n) + (ANY,) * len(extra),
        out_specs=tuple(out_specs), input_output_aliases=aliases,
        compiler_params=pltpu.CompilerParams(has_side_effects=EFFECT),
    )(*[pltpu.with_memory_space_constraint(src, pltpu.HBM) for src in srcs],
      *[pltpu.with_memory_space_constraint(lax.empty(shp, src.dtype), pltpu.HBM) for shp, src in zip(land_shapes, srcs)],
      *extra)
    return [tuple(res[4 * k:4 * k + 4]) for k in range(n)], res[4 * n][0, 0]


def exchange_wait(handles, after, gather, name):
    send_sems, recv_sems, src_thru, land_thru = handles

    def body(src_ref, land_ref, send_sems, recv_sems, after_ref, src_dead, got_ref):
        for cp in _peer_copies(src_ref, land_ref, send_sems, recv_sems, gather):
            cp.wait_send()
            cp.wait_recv()

    return pl.pallas_call(
        body, name=name,
        out_shape=(pltpu.HBM(src_thru.shape, src_thru.dtype), pltpu.HBM(land_thru.shape, land_thru.dtype)),
        in_specs=(HBM, HBM, SEM, SEM, ANY), out_specs=(HBM, HBM), input_output_aliases={0: 0, 1: 1},
        compiler_params=pltpu.CompilerParams(has_side_effects=EFFECT),
    )(src_thru, land_thru, send_sems, recv_sems, after)[1]


def _to_pattern(a, r):
    return a.reshape(SEQ // r, r * a.shape[1])


def local_step(x, tgt, mod, get_w, put_grad, put_small, wc, wf, g_mix, bc, lg, lb, gco, gao, g_ffn, bf, g_fin):
    h1 = rms_mod_fwd(x, g_mix, mod, 0, 1, "h1_fwd")
    w_in = get_w("w_in", h1)
    proj = matmul(h1, w_in, "nt", F32, 512, D_IN, "proj_fwd", b_rows=D_IN // N_DEV)
    mix_a = conv_module_fwd(proj, wc, bc, lg, lb, gco, "conv_module_fwd")
    att, lse = attn_fwd_all(proj, "attn_fwd")
    mix_b = rms_gain_bf16(att, gao, "attn_out_norm")
    mixed = jnp.concatenate([mix_a, mix_b], axis=1)
    w_out = get_w("w_out", mixed)
    y1 = matmul(mixed, w_out, "nn", F32, 512, D_MODEL, "out_proj_fwd")
    x1, h2 = resid_rms_mod_fwd(x, y1, g_ffn, mod, 2, 3, 4, "x1_h2_fwd")
    w_up = get_w("w_up", h2)
    up0 = matmul(h2, w_up, "nt", F32, 512, D_FF, "up_fwd")
    act = ffn_act_fwd(up0, wf, bf, "ffn_act_fwd")
    w_down = get_w("w_down", act)
    y2 = matmul(act, w_down, "nn", F32, 512, D_MODEL, "down_fwd")
    loss_t, dx2, dy2, d_gfin, d_gaf = final_loss_bwd(x1, y2, tgt, g_fin, mod, 5, "loss_bwd")
    dact = matmul(dy2, w_down, "nt", F32, 512, FFN_TN, "down_bwd_x")
    dw_down = matmul(act, dy2, "tn", BF16, 256, D_MODEL, "down_bwd_w")
    dup0, dbf_g, dbf_v, dwf_g, dwf_v = ffn_bwd(up0, dact, wf + put_grad("w_down", dw_down), bf, "ffn_bwd")
    dh2 = matmul_halves(dup0, w_up, 512, 512, "up_bwd_x")
    dw_up = matmul_tn_halves(dup0, h2, 256, "up_bwd_w")
    dx1, d_shf, d_scf, d_gffn, dy1, d_gam = rms_mod_bwd(x1, dh2, dx2, g_ffn + put_grad("w_up", dw_up), mod, 4, y1, 2, "h2_bwd")
    dmixed = matmul(dy1, w_out, "nt", F32, 512, D_MODEL, "out_proj_bwd_x")
    dw_out = matmul(mixed, dy1, "tn", BF16, 256, D_MODEL, "out_proj_bwd_w")
    do, dd, d_gao = attn_combine_bwd(dmixed, att, gao + put_grad("w_out", dw_out), "attn_combine_bwd")
    dqkv = attn_bwd_all(proj, do, lse, dd, "attn_bwd")
    du1, d_gco, d_lg, d_lb, d_bc, d_wc = conv_module_bwd_a(proj, dmixed, wc, bc, lg, lb, gco, "conv_module_bwd_a")
    zero = jnp.zeros((1, D_MODEL), F32)
    small = dict(g_norm_mix=zero, b_conv_dw=d_bc, ln_conv_g=d_lg, ln_conv_b=d_lb, g_conv_out=d_gco, g_attn_out=d_gao,
                 g_norm_ffn=d_gffn, b_ffn_dw=jnp.concatenate([dbf_g, dbf_v], axis=1), g_final=d_gfin,
                 w_conv_dw=d_wc, w_ffn_dw=jnp.concatenate([dwf_g, dwf_v], axis=1), loss=loss_t[0:1, 0:1],
                 dmod=jnp.concatenate([zero, zero, d_gam, d_shf, d_scf, d_gaf], axis=1))
    dproj_a = conv_module_bwd_b(proj, du1, wc + put_small(small), "conv_module_bwd_b")
    dproj = jnp.concatenate([dproj_a, dqkv[0], dqkv[1], dqkv[2]], axis=1)
    dw_in = matmul(dproj, h1, "tn", BF16, 512, D_MODEL, "proj_bwd_w")
    dh1 = matmul(dproj, w_in, "nn", F32, 512, D_MODEL, "proj_bwd_x", b_rows=D_IN // N_DEV)
    dx, d_shm, d_scm, d_gmix = rms_mod_bwd(x, dh1, dx1, g_mix + put_grad("w_in", dw_in), mod, 1, None, 0, "h1_bwd")
    late = jnp.concatenate([d_gmix, d_shm, d_scm, jnp.zeros((5, D_MODEL), F32)], axis=0)
    return dx, late


def _padw(a, width):
    return jnp.pad(a, ((0, 0), (0, width - a.shape[1])))


def pack_small(t):
    wide = jnp.concatenate([_padw(t["dmod"], PACK_W), _padw(t["b_ffn_dw"], PACK_W), _padw(t["w_ffn_dw"], PACK_W),
                            _padw(t["loss"], PACK_W), jnp.zeros((2, PACK_W), F32)], axis=0)
    z512 = jnp.zeros((1, 512), F32)
    narrow = jnp.concatenate([
        t["g_norm_mix"], t["g_norm_ffn"], t["g_final"],
        jnp.concatenate([t["b_conv_dw"], t["ln_conv_g"]], axis=1),
        jnp.concatenate([t["ln_conv_b"], t["g_conv_out"]], axis=1),
        jnp.concatenate([t["g_attn_out"], z512], axis=1),
        jnp.zeros((2, 1024), F32),
        jnp.pad(t["w_conv_dw"], ((0, 1), (0, 0))).reshape(16, 1024)], axis=0)
    return jnp.concatenate([wide, narrow.reshape(4, PACK_W), jnp.zeros((4, PACK_W), F32)], axis=0)


_NARROW = lambda k, off=0: (8 + k // 6, (k % 6) * 1024 + off)
PACKED_AT = dict(
    b_ada=(0, 0, N_MOD * D_MODEL), b_ffn_dw=(1, 0, 2 * D_FF),
    g_norm_mix=_NARROW(0) + (D_MODEL,), g_norm_ffn=_NARROW(1) + (D_MODEL,), g_final=_NARROW(2) + (D_MODEL,),
    b_conv_dw=_NARROW(3) + (D_CONV,), ln_conv_g=_NARROW(3, 512) + (D_CONV,),
    ln_conv_b=_NARROW(4) + (D_CONV,), g_conv_out=_NARROW(4, 512) + (D_CONV,), g_attn_out=_NARROW(5) + (D_ATTN,))
SMALL_ORDER = list(PACKED_AT)


def small_adamw(parts, late, wmv, name):
    def body(*refs):
        p_ref, late_ref = refs[0:2]
        ins = refs[2:2 + 3 * len(SMALL_ORDER)]
        outs = refs[2 + 3 * len(SMALL_ORDER):]
        g, gl = p_ref[0], late_ref[0]
        for k in range(1, N_DEV):
            g, gl = g + p_ref[k], gl + late_ref[k]
        for i, n in enumerate(SMALL_ORDER):
            row, lane, width = PACKED_AT[n]
            gp = g[row:row + 1, lane:lane + width]
            if n == "g_norm_mix":
                gp = gp + gl[0:1, :]
            if n == "b_ada":
                gp = gp + jnp.concatenate([gl[1:2, :], gl[2:3, :], jnp.zeros((1, (N_MOD - 2) * D_MODEL), F32)], axis=1)
            w_ref, m_ref, v_ref = ins[3 * i:3 * i + 3]
            g_ref, d_ref, nm_ref, nv_ref = outs[4 * i:4 * i + 4]
            g_ref[...] = gp
            d_ref[...], nm_ref[...], nv_ref[...] = _adam(w_ref[...], gp, m_ref[...], v_ref[...])
        wc_ref, wf_ref, loss_ref = outs[4 * len(SMALL_ORDER):]
        for j in range(CONV_K):
            row, lane = _NARROW(8 + j // 2, (j % 2) * 512)
            wc_ref[j:j + 1, :] = g[row:row + 1, lane:lane + D_CONV]
        wf_ref[...] = g[2:2 + FFN_K, 0:2 * D_FF]
        loss_ref[...] = jnp.broadcast_to(g[5:6, 0:1], (8, 128))

    args, out_shape = [parts, late], []
    for n in SMALL_ORDER:
        args += list(wmv[n])
        out_shape += [jax.ShapeDtypeStruct(wmv[n][0].shape, F32)] * 4
    out_shape += [jax.ShapeDtypeStruct((CONV_K, D_CONV), F32), jax.ShapeDtypeStruct((FFN_K, 2 * D_FF), F32),
                  jax.ShapeDtypeStruct((8, 128), F32)]
    res = pl.pallas_call(body, out_shape=tuple(out_shape), name=name, compiler_params=_cp())(*args)
    per = {n: tuple(res[4 * i:4 * i + 4]) for i, n in enumerate(SMALL_ORDER)}
    return per, res[-3], res[-2], res[-1][0, 0]


def shard_adamw(items, name):
    def body(*refs):
        ins, outs = refs[:4 * len(items)], refs[4 * len(items):]
        for i in range(len(items)):
            g_ref, w_ref, m_ref, v_ref = ins[4 * i:4 * i + 4]
            og_ref, d_ref, nm_ref, nv_ref = outs[4 * i:4 * i + 4]
            og_ref[...] = g_ref[...]
            d_ref[...], nm_ref[...], nv_ref[...] = _adam(w_ref[...], g_ref[...], m_ref[...], v_ref[...])

    args = [a for item in items for a in item]
    out_shape = tuple(jax.ShapeDtypeStruct(item[1].shape, F32) for item in items for _ in range(4))
    res = pl.pallas_call(body, out_shape=out_shape, name=name, compiler_params=_cp())(*args)
    return [tuple(res[4 * i:4 * i + 4]) for i in range(len(items))]


def _shard(full, n_cols, me):
    return lax.dynamic_slice(full, (0, me * n_cols), (full.shape[0], n_cols))


WEIGHTS = ["w_ada", "b_ada", "g_norm_mix", "w_in", "w_conv_dw", "b_conv_dw", "ln_conv_g", "ln_conv_b", "g_conv_out",
           "g_attn_out", "w_out", "g_norm_ffn", "w_up", "w_ffn_dw", "b_ffn_dw", "w_down", "g_final"]
SMALL_REPLICATED = ["g_norm_mix", "b_conv_dw", "ln_conv_g", "ln_conv_b", "g_conv_out", "g_attn_out", "g_norm_ffn",
                    "b_ffn_dw", "g_final"]


def kernel(x, c, w_ada, b_ada, g_norm_mix, w_in, w_conv_dw, b_conv_dw, ln_conv_g, ln_conv_b, g_conv_out, g_attn_out, w_out, g_norm_ffn, w_up, w_ffn_dw, b_ffn_dw, w_down, g_final, loss_target, m_w_ada, m_b_ada, m_g_norm_mix, m_w_in, m_w_conv_dw, m_b_conv_dw, m_ln_conv_g, m_ln_conv_b, m_g_conv_out, m_g_attn_out, m_w_out, m_g_norm_ffn, m_w_up, m_w_ffn_dw, m_b_ffn_dw, m_w_down, m_g_final, v_w_ada, v_b_ada, v_g_norm_mix, v_w_in, v_w_conv_dw, v_b_conv_dw, v_ln_conv_g, v_ln_conv_b, v_g_conv_out, v_g_attn_out, v_w_out, v_g_norm_ffn, v_w_up, v_w_ffn_dw, v_b_ffn_dw, v_w_down, v_g_final):
    args = dict(locals())
    me = 4 * lax.axis_index("x") + 2 * lax.axis_index("y") + lax.axis_index("c")

    def flat(name, prefix=""):
        a = args[prefix + name]
        return a.reshape(a.shape[-2] if a.ndim > 1 else 1, a.shape[-1])

    def flat_t(name, prefix=""):
        return args[prefix + name][0].T

    n_in, n_up, r_out, r_down = w_in.shape[2], w_up.shape[2], w_out.shape[1], w_down.shape[1]
    n_ada, n_wc, n_wf = w_ada.shape[2], w_conv_dw.shape[2], w_ffn_dw.shape[2]
    taps_c = jnp.pad(flat("w_conv_dw").reshape(1, CONV_K * n_wc), ((0, 0), (0, 2 * D_MODEL - CONV_K * n_wc)))
    taps_f = jnp.pad(flat("w_ffn_dw").reshape(1, FFN_K * n_wf), ((0, 0), (0, 3 * D_MODEL - FFN_K * n_wf)))
    first = jnp.concatenate([c, taps_c.reshape(2, D_MODEL), taps_f.reshape(3, D_MODEL), jnp.zeros((2, D_MODEL), F32)], axis=0)
    w_in_block = flat_t("w_in").astype(BF16)
    hi = lax.reduce_precision(first, 8, 7)
    mid = lax.reduce_precision(first - hi, 8, 7)
    low = lax.reduce_precision(first - hi - mid, 8, 7)
    terms = jnp.concatenate([hi, mid, low, jnp.zeros((8, D_MODEL), F32)], axis=0).astype(BF16)
    first_block = all_gather(jnp.concatenate([w_in_block, terms], axis=0), "gather_c_taps_w_in")
    terms = first_block[:, n_in:n_in + 24, :].astype(F32).reshape(N_DEV, 3, 8, D_MODEL)
    first_all = (terms[:, 0] + terms[:, 1]) + terms[:, 2]
    c_all = first_all[:, 0, :]
    wc_full = first_all[:, 1:3, :].reshape(N_DEV, 2 * D_MODEL)[:, :CONV_K * n_wc].reshape(N_DEV, CONV_K, n_wc)
    wc_full = wc_full.transpose(1, 0, 2).reshape(CONV_K, D_CONV)
    wf_full = first_all[:, 3:6, :].reshape(N_DEV, 3 * D_MODEL)[:, :FFN_K * n_wf].reshape(N_DEV, FFN_K, n_wf)
    wf_full = wf_full.transpose(1, 0, 2).reshape(FFN_K, 2 * D_FF)
    mod_cols = ada_fwd(c_all, flat("w_ada"), _shard(flat("b_ada"), n_ada, me), "ada_fwd")
    mod_all = all_gather(mod_cols, "gather_mod")
    mod = lax.dynamic_index_in_dim(mod_all, me, axis=1, keepdims=False).reshape(N_MOD, D_MODEL)
    mod = jnp.pad(mod, ((0, 2), (0, 0)))

    order = ("w_out", "w_up", "w_down")
    blocks = dict(w_up=flat_t("w_up").astype(BF16), w_out=flat("w_out").astype(BF16), w_down=flat("w_down").astype(BF16))
    handles, tok = exchange_start([blocks[name] for name in order], True, "gather_weights_start", mod_all)
    gathers = dict(zip(order, handles))
    mod = mod + tok

    def gathered(name, after):
        land = exchange_wait(gathers[name], after, True, f"gather_{name}_wait")
        return lax.dynamic_update_index_in_dim(land, blocks[name], me, axis=0)

    def get_w(name, after):
        if name == "w_in":
            return first_block
        return gathered(name, after).reshape(-1, D_MODEL)

    exchanges, own = {}, {}

    def put_grad(name, dw, after=None):
        dev_major = dw.reshape(N_DEV, -1, D_MODEL)
        own[name] = lax.dynamic_index_in_dim(dev_major, me, axis=0, keepdims=False)
        (exchanges[name],), token = exchange_start([dev_major], False, f"exchange_{name}_start", after)
        return token

    kept = {}

    def put_small(small):
        kept["packed"] = pack_small(small)
        (kept["gather"],), token = exchange_start([kept["packed"]], True, "gather_small_start")
        return token

    grad_x, late = local_step(
        x[0], loss_target[0], mod, get_w, put_grad, put_small, wc_full, wf_full,
        flat("g_norm_mix"), flat("b_conv_dw"), flat("ln_conv_g"), flat("ln_conv_b"), flat("g_conv_out"),
        flat("g_attn_out"), flat("g_norm_ffn"), flat("b_ffn_dw"), flat("g_final"))

    out = {}

    def finish(name, tr, after):
        parts = exchange_wait(exchanges[name], after, False, f"exchange_{name}_wait")
        if name in ("w_in", "w_up"):
            res = sum_adamw(parts, own[name], flat_t(name), flat_t(name, "m_"), flat_t(name, "v_"), tr, "adamw_" + name)
            out[name] = tuple(r.T for r in res)
        else:
            res = out[name] = sum_adamw(parts, own[name], flat(name), flat(name, "m_"), flat(name, "v_"), tr, "adamw_" + name)
        return res[0]

    after = finish("w_down", r_down, grad_x)
    after = finish("w_up", n_up // 2, after)
    after = finish("w_out", r_out, after)
    after = finish("w_in", n_in, after)

    small_all = lax.dynamic_update_index_in_dim(
        exchange_wait(kept["gather"], after, True, "gather_small_wait"), kept["packed"], me, axis=0)
    late_all = all_gather(late, "gather_late", after)

    wmv = {n: (flat(n), flat(n, "m_"), flat(n, "v_")) for n in SMALL_ORDER}
    per, g_wc, g_wf, loss = small_adamw(small_all, late_all, wmv, "adamw_small")
    out.update(per)
    taps = shard_adamw([(_shard(g_wc, n_wc, me), flat("w_conv_dw"), flat("w_conv_dw", "m_"), flat("w_conv_dw", "v_")),
                        (_shard(g_wf, n_wf, me), flat("w_ffn_dw"), flat("w_ffn_dw", "m_"), flat("w_ffn_dw", "v_"))],
                       "adamw_taps")
    out["w_conv_dw"], out["w_ffn_dw"] = taps

    dmod_all = jnp.concatenate([late_all[:, 1, :], late_all[:, 2, :], small_all[:, 0, 2 * D_MODEL:]], axis=1)
    dmod_cols = _shard(dmod_all, n_ada, me)
    out["w_ada"] = ada_bwd_adamw(c_all.T, dmod_cols, flat("w_ada"), flat("w_ada", "m_"), flat("w_ada", "v_"), "adamw_w_ada")

    result = [loss, grad_x[None]]
    for k in range(4):
        result += [out[n][k].reshape(args[n].shape) for n in WEIGHTS]
    return tuple(result)
```

```python
import functools

import jax
import jax.numpy as jnp
from jax import lax
from jax.experimental import pallas as pl
from jax.experimental.pallas import tpu as pltpu

F32 = jnp.float32
BF16 = jnp.bfloat16

N_DEV = 8
SEQ = 2048
D_MODEL = 1024
D_CONV = 512
D_ATTN = 512
HEAD_DIM = 64
CONV_K = 31
D_FF = 2816
FFN_K = 3
D_IN = 2 * D_CONV + 3 * D_ATTN
N_MOD = 6
EPS = 1e-6
ATTN_BLOCK = 128
PATTERNS = ((2048, 1), (512, 4), (128, 16))
NEG = -1e30

ADAM_LR, ADAM_B1, ADAM_B2, ADAM_EPS, ADAM_WD, ADAM_STEP = 0.001, 0.9, 0.999, 1e-08, 0.01, 10

ROWS = 256
CONV_HALO = 32
FFN_HALO = 8
FFN_TN = 1408
VMEM_LIMIT = 56 * 1024 * 1024
PACK_ROWS, PACK_W = 16, 6144


NT = (((1,), (1,)), ((), ()))


def _cp(*sem):
    return pltpu.CompilerParams(dimension_semantics=sem if sem else None, vmem_limit_bytes=VMEM_LIMIT)


def _sig(x):
    return 1.0 / (1.0 + jnp.exp(-x))


def _rsum(x):
    return jnp.sum(x, axis=0, keepdims=True)


def _mean(x):
    return jnp.mean(x, axis=-1, keepdims=True)


def _acc(ref, val, first):
    @pl.when(first)
    def _():
        ref[...] = val

    @pl.when(jnp.logical_not(first))
    def _():
        ref[...] += val


SUB = 16


def _for_chunks(fn, unroll=1, rows=ROWS):
    def step(i, carry):
        fn(pl.ds(pl.multiple_of(i * SUB, SUB), SUB))
        return carry

    lax.fori_loop(0, rows // SUB, step, 0, unroll=unroll)


def rms_mod_fwd(x, g, mod, sh_row, sc_row, name):
    def body(x_ref, g_ref, mod_ref, h_ref):
        xx = x_ref[...]
        r = lax.rsqrt(_mean(xx * xx) + EPS)
        h = xx * r * g_ref[...]
        h_ref[...] = (h * (1.0 + mod_ref[sc_row:sc_row + 1, :]) + mod_ref[sh_row:sh_row + 1, :]).astype(BF16)

    return pl.pallas_call(
        body, out_shape=jax.ShapeDtypeStruct((SEQ, D_MODEL), BF16), grid=(SEQ // ROWS,),
        in_specs=[_row_spec(D_MODEL), _vec_spec(D_MODEL), _vec_spec(D_MODEL, 8)],
        out_specs=_row_spec(D_MODEL), name=name, compiler_params=_cp("parallel"))(x, g, mod)


def resid_rms_mod_fwd(x, y, g, mod, ga_row, sh_row, sc_row, name):
    def body(x_ref, y_ref, g_ref, mod_ref, x1_ref, h_ref):
        x1 = x_ref[...] + mod_ref[ga_row:ga_row + 1, :] * y_ref[...]
        x1_ref[...] = x1
        r = lax.rsqrt(_mean(x1 * x1) + EPS)
        h = x1 * r * g_ref[...]
        h_ref[...] = (h * (1.0 + mod_ref[sc_row:sc_row + 1, :]) + mod_ref[sh_row:sh_row + 1, :]).astype(BF16)

    return pl.pallas_call(
        body, out_shape=(jax.ShapeDtypeStruct((SEQ, D_MODEL), F32), jax.ShapeDtypeStruct((SEQ, D_MODEL), BF16)),
        grid=(SEQ // ROWS,),
        in_specs=[_row_spec(D_MODEL), _row_spec(D_MODEL), _vec_spec(D_MODEL), _vec_spec(D_MODEL, 8)],
        out_specs=(_row_spec(D_MODEL), _row_spec(D_MODEL)), name=name, compiler_params=_cp("parallel"))(x, y, g, mod)


def final_loss_bwd(x1, y2, tgt, g, mod, ga_row, name):
    def body(x1_ref, y2_ref, t_ref, g_ref, mod_ref, loss_ref, dx2_ref, dy2_ref, dg_ref, dga_ref):
        first = pl.program_id(0) == 0
        ga = mod_ref[ga_row:ga_row + 1, :]
        y2 = y2_ref[...]
        x2 = x1_ref[...] + ga * y2
        r = lax.rsqrt(_mean(x2 * x2) + EPS)
        xn = x2 * r
        err = xn * g_ref[...] - t_ref[...]
        _acc(loss_ref, jnp.broadcast_to(0.5 * jnp.sum(_mean(err * err)), (8, 128)), first)
        dy = err * (1.0 / D_MODEL)
        _acc(dg_ref, _rsum(dy * xn), first)
        dxn = dy * g_ref[...]
        dx2 = r * (dxn - xn * _mean(dxn * xn))
        dx2_ref[...] = dx2
        dy2_ref[...] = (dx2 * ga).astype(BF16)
        _acc(dga_ref, _rsum(dx2 * y2), first)

    vec = jax.ShapeDtypeStruct((1, D_MODEL), F32)
    return pl.pallas_call(
        body,
        out_shape=(jax.ShapeDtypeStruct((8, 128), F32), jax.ShapeDtypeStruct((SEQ, D_MODEL), F32),
                   jax.ShapeDtypeStruct((SEQ, D_MODEL), BF16), vec, vec),
        grid=(SEQ // ROWS,),
        in_specs=[_row_spec(D_MODEL), _row_spec(D_MODEL), _row_spec(D_MODEL), _vec_spec(D_MODEL), _vec_spec(D_MODEL, 8)],
        out_specs=(pl.BlockSpec((8, 128), lambda i: (0, 0)), _row_spec(D_MODEL), _row_spec(D_MODEL),
                   _vec_spec(D_MODEL), _vec_spec(D_MODEL)),
        name=name, compiler_params=_cp("arbitrary"))(x1, y2, tgt, g, mod)


def rms_mod_bwd(x, dh, dres, g, mod, sc_row, y, ga_row, name):
    gated = y is not None

    def body(*refs):
        if gated:
            x_ref, dh_ref, dres_ref, g_ref, mod_ref, y_ref, dx_ref, dsh_ref, dsc_ref, dg_ref, dy_ref, dga_ref = refs
        else:
            x_ref, dh_ref, dres_ref, g_ref, mod_ref, dx_ref, dsh_ref, dsc_ref, dg_ref = refs
        first = pl.program_id(0) == 0
        xx = x_ref[...]
        dh = dh_ref[...]
        gg = g_ref[...]
        r = lax.rsqrt(_mean(xx * xx) + EPS)
        xn = xx * r
        _acc(dsh_ref, _rsum(dh), first)
        _acc(dsc_ref, _rsum(dh * (xn * gg)), first)
        dt = dh * (1.0 + mod_ref[sc_row:sc_row + 1, :])
        _acc(dg_ref, _rsum(dt * xn), first)
        dxn = dt * gg
        dx = dres_ref[...] + r * (dxn - xn * _mean(dxn * xn))
        dx_ref[...] = dx
        if gated:
            _acc(dga_ref, _rsum(dx * y_ref[...]), first)
            dy_ref[...] = (dx * mod_ref[ga_row:ga_row + 1, :]).astype(BF16)

    vec = jax.ShapeDtypeStruct((1, D_MODEL), F32)
    in_specs = [_row_spec(D_MODEL), _row_spec(D_MODEL), _row_spec(D_MODEL), _vec_spec(D_MODEL), _vec_spec(D_MODEL, 8)]
    out_shape = [jax.ShapeDtypeStruct((SEQ, D_MODEL), F32), vec, vec, vec]
    out_specs = [_row_spec(D_MODEL), _vec_spec(D_MODEL), _vec_spec(D_MODEL), _vec_spec(D_MODEL)]
    args = [x, dh, dres, g, mod]
    if gated:
        in_specs.append(_row_spec(D_MODEL))
        out_shape += [jax.ShapeDtypeStruct((SEQ, D_MODEL), BF16), vec]
        out_specs += [_row_spec(D_MODEL), _vec_spec(D_MODEL)]
        args.append(y)
    return pl.pallas_call(
        body, out_shape=tuple(out_shape), grid=(SEQ // ROWS,), in_specs=in_specs, out_specs=tuple(out_specs),
        name=name, compiler_params=_cp("arbitrary"))(*args)


def rms_mod_matmul(x, g, mod, sh_row, sc_row, b, b_rows, name):
    n = N_DEV * b_rows

    def body(x_ref, g_ref, mod_ref, b_ref, o_ref, h_ref):
        xx = x_ref[...]
        r = lax.rsqrt(_mean(xx * xx) + EPS)
        h = (xx * r * g_ref[...] * (1.0 + mod_ref[sc_row:sc_row + 1, :]) + mod_ref[sh_row:sh_row + 1, :]).astype(BF16)
        h_ref[...] = h
        o_ref[...] = lax.dot_general(h, b_ref[...].reshape(n, D_MODEL), NT, preferred_element_type=F32)

    return pl.pallas_call(
        body, out_shape=(jax.ShapeDtypeStruct((SEQ, n), F32), jax.ShapeDtypeStruct((SEQ, D_MODEL), BF16)),
        grid=(SEQ // ROWS,),
        in_specs=[_row_spec(D_MODEL), _vec_spec(D_MODEL), _vec_spec(D_MODEL, 8),
                  pl.BlockSpec((N_DEV, b_rows, D_MODEL), lambda i: (0, 0, 0))],
        out_specs=(_row_spec(n), _row_spec(D_MODEL)), name=name, compiler_params=_cp("parallel"))(x, g, mod, b)


def norm_concat_matmul(mix_a, att, gao, w, name):
    def body(a_ref, att_ref, g_ref, w_ref, y_ref, mixed_ref):
        aa = att_ref[...]
        mixed_ref[:, 0:D_CONV] = a_ref[...]
        mixed_ref[:, D_CONV:] = (aa * lax.rsqrt(_mean(aa * aa) + EPS) * g_ref[...]).astype(BF16)
        y_ref[...] = jnp.dot(mixed_ref[...], w_ref[...], preferred_element_type=F32)

    return pl.pallas_call(
        body, out_shape=(jax.ShapeDtypeStruct((SEQ, D_MODEL), F32), jax.ShapeDtypeStruct((SEQ, D_MODEL), BF16)),
        grid=(SEQ // ROWS,),
        in_specs=[_row_spec(D_CONV), _row_spec(D_ATTN), _vec_spec(D_ATTN), pl.BlockSpec(w.shape, lambda i: (0, 0))],
        out_specs=(_row_spec(D_MODEL), _row_spec(D_MODEL)), name=name, compiler_params=_cp("parallel"))(mix_a, att, gao, w)


def matmul_loss_bwd(act, w, x1, tgt, g, mod, ga_row, name):
    def body(a_ref, w_ref, x1_ref, t_ref, g_ref, mod_ref, loss_ref, dx2_ref, dy2_ref, dg_ref, dga_ref):
        first = pl.program_id(0) == 0
        ga = mod_ref[ga_row:ga_row + 1, :]
        y2 = jnp.dot(a_ref[...], w_ref[...], preferred_element_type=F32)
        x2 = x1_ref[...] + ga * y2
        r = lax.rsqrt(_mean(x2 * x2) + EPS)
        xn = x2 * r
        err = xn * g_ref[...] - t_ref[...]
        _acc(loss_ref, jnp.broadcast_to(0.5 * jnp.sum(_mean(err * err)), (8, 128)), first)
        dy = err * (1.0 / D_MODEL)
        _acc(dg_ref, _rsum(dy * xn), first)
        dxn = dy * g_ref[...]
        dx2 = r * (dxn - xn * _mean(dxn * xn))
        dx2_ref[...] = dx2
        dy2_ref[...] = (dx2 * ga).astype(BF16)
        _acc(dga_ref, _rsum(dx2 * y2), first)

    vec = jax.ShapeDtypeStruct((1, D_MODEL), F32)
    return pl.pallas_call(
        body,
        out_shape=(jax.ShapeDtypeStruct((8, 128), F32), jax.ShapeDtypeStruct((SEQ, D_MODEL), F32),
                   jax.ShapeDtypeStruct((SEQ, D_MODEL), BF16), vec, vec),
        grid=(SEQ // ROWS,),
        in_specs=[_row_spec(act.shape[1]), pl.BlockSpec(w.shape, lambda i: (0, 0)), _row_spec(D_MODEL), _row_spec(D_MODEL),
                  _vec_spec(D_MODEL), _vec_spec(D_MODEL, 8)],
        out_specs=(pl.BlockSpec((8, 128), lambda i: (0, 0)), _row_spec(D_MODEL), _row_spec(D_MODEL),
                   _vec_spec(D_MODEL), _vec_spec(D_MODEL)),
        name=name, compiler_params=_cp("arbitrary"))(act, w, x1, tgt, g, mod)


def matmul_rms_mod_bwd(a, b, x, dres, g, mod, sc_row, y, ga_row, name, b_rows=None):
    gated = y is not None
    halves = a.ndim == 3
    if halves:
        k2 = a.shape[2]
        b_arg = b.reshape(2, k2, D_MODEL)
        a_spec = pl.BlockSpec((2, ROWS, k2), lambda i: (0, i, 0))
        b_spec = pl.BlockSpec((2, k2, D_MODEL), lambda i: (0, 0, 0))
    else:
        b_arg = b
        a_spec = _row_spec(a.shape[1])
        b_spec = pl.BlockSpec((N_DEV, b_rows, D_MODEL), lambda i: (0, 0, 0))

    def body(*refs):
        if gated:
            a_ref, b_ref, x_ref, dres_ref, g_ref, mod_ref, y_ref, dx_ref, dsh_ref, dsc_ref, dg_ref, dy_ref, dga_ref = refs
        else:
            a_ref, b_ref, x_ref, dres_ref, g_ref, mod_ref, dx_ref, dsh_ref, dsc_ref, dg_ref = refs
        first = pl.program_id(0) == 0
        if halves:
            dh = (jnp.dot(a_ref[0], b_ref[0], preferred_element_type=F32)
                  + jnp.dot(a_ref[1], b_ref[1], preferred_element_type=F32))
        else:
            dh = jnp.dot(a_ref[...], b_ref[...].reshape(N_DEV * b_rows, D_MODEL), preferred_element_type=F32)
        xx = x_ref[...]
        gg = g_ref[...]
        r = lax.rsqrt(_mean(xx * xx) + EPS)
        xn = xx * r
        _acc(dsh_ref, _rsum(dh), first)
        _acc(dsc_ref, _rsum(dh * (xn * gg)), first)
        dt = dh * (1.0 + mod_ref[sc_row:sc_row + 1, :])
        _acc(dg_ref, _rsum(dt * xn), first)
        dxn = dt * gg
        dx = dres_ref[...] + r * (dxn - xn * _mean(dxn * xn))
        dx_ref[...] = dx
        if gated:
            _acc(dga_ref, _rsum(dx * y_ref[...]), first)
            dy_ref[...] = (dx * mod_ref[ga_row:ga_row + 1, :]).astype(BF16)

    vec = jax.ShapeDtypeStruct((1, D_MODEL), F32)
    in_specs = [a_spec, b_spec, _row_spec(D_MODEL), _row_spec(D_MODEL), _vec_spec(D_MODEL), _vec_spec(D_MODEL, 8)]
    out_shape = [jax.ShapeDtypeStruct((SEQ, D_MODEL), F32), vec, vec, vec]
    out_specs = [_row_spec(D_MODEL), _vec_spec(D_MODEL), _vec_spec(D_MODEL), _vec_spec(D_MODEL)]
    args = [a, b_arg, x, dres, g, mod]
    if gated:
        in_specs.append(_row_spec(D_MODEL))
        out_shape += [jax.ShapeDtypeStruct((SEQ, D_MODEL), BF16), vec]
        out_specs += [_row_spec(D_MODEL), _vec_spec(D_MODEL)]
        args.append(y)
    return pl.pallas_call(
        body, out_shape=tuple(out_shape), grid=(SEQ // ROWS,), in_specs=in_specs, out_specs=tuple(out_specs),
        name=name, compiler_params=_cp("arbitrary"))(*args)


def _prev_halo(halo, width, col):
    per = ROWS // halo
    return pl.BlockSpec((halo, width), lambda i: (jnp.maximum(i * per - 1, 0), col))


def _next_halo(halo, width, col):
    per = ROWS // halo
    last = SEQ // halo - 1
    return pl.BlockSpec((halo, width), lambda i: (jnp.minimum((i + 1) * per, last), col))


CONV_PAD = ROWS + CONV_HALO


def _shift_copies(sh):
    for b in range(1, 8):
        sh[b, 0:CONV_PAD - 8, :] = sh[0, pl.ds(b, CONV_PAD - 8), :]


def _tap(sh, rs_start, offset):
    return sh[offset % 8, pl.ds(pl.multiple_of(rs_start + (offset // 8) * 8, 8), SUB), :]


def _conv_glu(av_ref, ag_ref, avh_ref, agh_ref, sh):
    i = pl.program_id(0)
    hv = avh_ref[...] * _sig(agh_ref[...])
    sh[0, 0:CONV_HALO, :] = jnp.where(i > 0, hv, 0.0)

    def glu(rs):
        sh[0, pl.ds(pl.multiple_of(rs.start + CONV_HALO, SUB), SUB), :] = av_ref[rs, :] * _sig(ag_ref[rs, :])

    _for_chunks(glu)
    _shift_copies(sh)


def _conv_norm(u1, lg_ref, lb_ref):
    mu = _mean(u1)
    cen = u1 - mu
    rs = lax.rsqrt(_mean(cen * cen) + EPS)
    z = cen * rs
    ln = z * lg_ref[...] + lb_ref[...]
    s = _sig(ln)
    return z, rs, ln, s, ln * s


def conv_module_fwd(proj, wc, bc, lg, lb, gco, name):
    def body(av_ref, ag_ref, avh_ref, agh_ref, wc_ref, bc_ref, lg_ref, lb_ref, gco_ref, out_ref, u1_ref, sh):
        _conv_glu(av_ref, ag_ref, avh_ref, agh_ref, sh)

        def conv(rs):
            u1 = jnp.broadcast_to(bc_ref[...], (SUB, D_CONV))
            for j in range(CONV_K):
                u1 = u1 + wc_ref[j:j + 1, :] * _tap(sh, rs.start, CONV_HALO - (CONV_K - 1) + j)
            u1_ref[rs, :] = u1

        _for_chunks(conv)
        _, _, _, _, u2 = _conv_norm(u1_ref[...], lg_ref, lb_ref)
        rc = lax.rsqrt(_mean(u2 * u2) + EPS)
        out_ref[...] = (u2 * rc * gco_ref[...]).astype(BF16)

    v = _vec_spec(D_CONV)
    return pl.pallas_call(
        body, out_shape=(jax.ShapeDtypeStruct((SEQ, D_CONV), BF16), jax.ShapeDtypeStruct((SEQ, D_CONV), F32)),
        grid=(SEQ // ROWS,),
        in_specs=[_row_spec(D_CONV, 0), _row_spec(D_CONV, 1), _prev_halo(CONV_HALO, D_CONV, 0),
                  _prev_halo(CONV_HALO, D_CONV, 1), _vec_spec(D_CONV, CONV_K), v, v, v, v],
        out_specs=(_row_spec(D_CONV), _row_spec(D_CONV)), scratch_shapes=[pltpu.VMEM((8, CONV_PAD, D_CONV), F32)],
        name=name, compiler_params=_cp("parallel"))(proj, proj, proj, proj, wc, bc, lg, lb, gco)


def conv_module_bwd_a(proj, u1, dmixed, lg, lb, gco, name):
    def body(av_ref, ag_ref, avh_ref, agh_ref, u1_ref, dm_ref, lg_ref, lb_ref, gco_ref,
             du1_ref, dgco_ref, dlg_ref, dlb_ref, dbc_ref, dwc_ref, sh, acc):
        first = pl.program_id(0) == 0
        _conv_glu(av_ref, ag_ref, avh_ref, agh_ref, sh)
        z, rs, ln, s, u2 = _conv_norm(u1_ref[...], lg_ref, lb_ref)
        rc = lax.rsqrt(_mean(u2 * u2) + EPS)
        xn = u2 * rc
        dm = dm_ref[...]
        _acc(dgco_ref, _rsum(dm * xn), first)
        dyn = dm * gco_ref[...]
        du2 = rc * (dyn - xn * _mean(dyn * xn))
        dln = du2 * (s * (1.0 + ln * (1.0 - s)))
        _acc(dlg_ref, _rsum(dln * z), first)
        _acc(dlb_ref, _rsum(dln), first)
        dz = dln * lg_ref[...]
        du1 = rs * (dz - _mean(dz) - z * _mean(dz * z))
        du1_ref[...] = du1
        _acc(dbc_ref, _rsum(du1), first)
        acc[...] = jnp.zeros_like(acc)

        def taps(rs):
            d = du1_ref[rs, :]
            for j in range(CONV_K):
                acc[j] += d * _tap(sh, rs.start, CONV_HALO - (CONV_K - 1) + j)

        _for_chunks(taps)

        @pl.when(first)
        def _():
            dwc_ref[...] = jnp.zeros_like(dwc_ref)

        for j in range(CONV_K):
            dwc_ref[j:j + 1, :] += _rsum(acc[j])

    v = _vec_spec(D_CONV)
    vec = jax.ShapeDtypeStruct((1, D_CONV), F32)
    return pl.pallas_call(
        body,
        out_shape=(jax.ShapeDtypeStruct((SEQ, D_CONV), F32), vec, vec, vec, vec, jax.ShapeDtypeStruct((CONV_K, D_CONV), F32)),
        grid=(SEQ // ROWS,),
        in_specs=[_row_spec(D_CONV, 0), _row_spec(D_CONV, 1), _prev_halo(CONV_HALO, D_CONV, 0),
                  _prev_halo(CONV_HALO, D_CONV, 1), _row_spec(D_CONV, 0), _row_spec(D_CONV, 0), v, v, v],
        out_specs=(_row_spec(D_CONV), v, v, v, v, _vec_spec(D_CONV, CONV_K)),
        scratch_shapes=[pltpu.VMEM((8, CONV_PAD, D_CONV), F32), pltpu.VMEM((CONV_K, SUB, D_CONV), F32)],
        name=name, compiler_params=_cp("arbitrary"))(proj, proj, proj, proj, u1, dmixed, lg, lb, gco)


def conv_module_bwd_b(proj, du1, wc, name):
    def body(av_ref, ag_ref, du1_ref, du1n_ref, wc_ref, out_ref, sh):
        i = pl.program_id(0)
        sh[0, 0:ROWS, :] = du1_ref[...]
        sh[0, ROWS:, :] = jnp.where(i < SEQ // ROWS - 1, du1n_ref[...], 0.0)
        _shift_copies(sh)

        def chunk(rs):
            du0 = jnp.zeros((SUB, D_CONV), F32)
            for j in range(CONV_K):
                du0 = du0 + wc_ref[j:j + 1, :] * _tap(sh, rs.start, CONV_K - 1 - j)
            sg = _sig(ag_ref[rs, :])
            out_ref[rs, 0:D_CONV] = (du0 * sg).astype(BF16)
            out_ref[rs, D_CONV:] = (du0 * av_ref[rs, :] * sg * (1.0 - sg)).astype(BF16)

        _for_chunks(chunk)

    return pl.pallas_call(
        body, out_shape=jax.ShapeDtypeStruct((SEQ, 2 * D_CONV), BF16), grid=(SEQ // ROWS,),
        in_specs=[_row_spec(D_CONV, 0), _row_spec(D_CONV, 1), _row_spec(D_CONV, 0), _next_halo(CONV_HALO, D_CONV, 0),
                  _vec_spec(D_CONV, CONV_K)],
        out_specs=_row_spec(2 * D_CONV), scratch_shapes=[pltpu.VMEM((8, CONV_PAD, D_CONV), F32)],
        name=name, compiler_params=_cp("parallel"))(proj, proj, du1, du1, wc)


def _attn_specs(sub_len, pairs):
    ng, width = 4 // pairs, 128 * pairs
    q = pl.BlockSpec((sub_len, width), lambda rho, g: (0, rho * 3 * ng + g))
    k = pl.BlockSpec((sub_len, width), lambda rho, g: (0, rho * 3 * ng + ng + g))
    v = pl.BlockSpec((sub_len, width), lambda rho, g: (0, rho * 3 * ng + 2 * ng + g))
    o = pl.BlockSpec((sub_len, width), lambda rho, g: (0, rho * ng + g))
    return q, k, v, o


ATTN_PAIRS = {1: 1, 4: 1, 16: 4}


def _attn_block(q_ref, k_ref, v_ref, n, hs, win):
    q0 = pl.multiple_of(n * ATTN_BLOCK, ATTN_BLOCK)
    k0 = pl.multiple_of(jnp.maximum(n - 1, 0) * ATTN_BLOCK, ATTN_BLOCK)
    qb = q_ref[pl.ds(q0, ATTN_BLOCK), hs]
    kw = k_ref[pl.ds(k0, win), hs]
    vw = v_ref[pl.ds(k0, win), hs]
    s = lax.dot_general(qb, kw, (((1,), (1,)), ((), ())), preferred_element_type=F32) * (HEAD_DIM ** -0.5)
    dist = (q0 - k0) + lax.broadcasted_iota(jnp.int32, (ATTN_BLOCK, win), 0) \
        - lax.broadcasted_iota(jnp.int32, (ATTN_BLOCK, win), 1)
    s = jnp.where((dist >= 0) & (dist <= ATTN_BLOCK), s, NEG)
    return q0, k0, qb, kw, vw, s


def attn_fwd(qkv_r, sub_len, r, name):
    nb = sub_len // ATTN_BLOCK
    win = 2 * ATTN_BLOCK if nb > 1 else ATTN_BLOCK
    pairs = ATTN_PAIRS[r]

    def body(q_ref, k_ref, v_ref, o_ref, l_ref):
        def block(n, carry):
            for h in range(2 * pairs):
                hs = slice(h * HEAD_DIM, (h + 1) * HEAD_DIM)
                q0, _, _, _, vw, s = _attn_block(q_ref, k_ref, v_ref, n, hs, win)
                m = jnp.max(s, axis=1, keepdims=True)
                p = jnp.exp(s - m)
                den = jnp.sum(p, axis=1, keepdims=True)
                o = jnp.dot(p.astype(BF16), vw, preferred_element_type=F32) / den
                o_ref[pl.ds(q0, ATTN_BLOCK), hs] = o
                l_ref[pl.ds(q0, ATTN_BLOCK), hs] = jnp.broadcast_to(m + jnp.log(den), (ATTN_BLOCK, HEAD_DIM))
            return carry

        lax.fori_loop(0, nb, block, 0, unroll=min(nb, 2))

    q, k, v, o = _attn_specs(sub_len, pairs)
    shp = jax.ShapeDtypeStruct((sub_len, r * D_ATTN), F32)
    return pl.pallas_call(
        body, out_shape=(shp, shp), grid=(r, 4 // pairs), in_specs=[q, k, v], out_specs=(o, o),
        name=name, compiler_params=_cp("parallel", "parallel"))(qkv_r, qkv_r, qkv_r)


def attn_bwd(qkv_r, do_r, lse_r, dd_r, sub_len, r, name):
    nb = sub_len // ATTN_BLOCK
    win = 2 * ATTN_BLOCK if nb > 1 else ATTN_BLOCK
    pairs = ATTN_PAIRS[r]

    def body(q_ref, k_ref, v_ref, do_ref, l_ref, dd_ref, dq_ref, dk_ref, dv_ref):
        dk_ref[...] = jnp.zeros_like(dk_ref)
        dv_ref[...] = jnp.zeros_like(dv_ref)

        def block(n, carry):
            for h in range(2 * pairs):
                hs = slice(h * HEAD_DIM, (h + 1) * HEAD_DIM)
                h1 = slice(h * HEAD_DIM, h * HEAD_DIM + 1)
                q0, k0, qb, kw, vw, s = _attn_block(q_ref, k_ref, v_ref, n, hs, win)
                dob = do_ref[pl.ds(q0, ATTN_BLOCK), hs]
                p = jnp.exp(s - l_ref[pl.ds(q0, ATTN_BLOCK), h1])
                dp = lax.dot_general(dob, vw, (((1,), (1,)), ((), ())), preferred_element_type=F32)
                ds = (p * (dp - dd_ref[pl.ds(q0, ATTN_BLOCK), h1]) * (HEAD_DIM ** -0.5)).astype(BF16)
                dq_ref[pl.ds(q0, ATTN_BLOCK), hs] = jnp.dot(ds, kw, preferred_element_type=F32)
                dk_ref[pl.ds(k0, win), hs] += lax.dot_general(ds, qb, (((0,), (0,)), ((), ())), preferred_element_type=F32)
                dv_ref[pl.ds(k0, win), hs] += lax.dot_general(p.astype(BF16), dob, (((0,), (0,)), ((), ())),
                                                              preferred_element_type=F32)
            return carry

        lax.fori_loop(0, nb, block, 0)

    q, k, v, o = _attn_specs(sub_len, pairs)
    shp = jax.ShapeDtypeStruct((sub_len, r * D_ATTN), F32)
    return pl.pallas_call(
        body, out_shape=(shp, shp, shp), grid=(r, 4 // pairs), in_specs=[q, k, v, o, o, o], out_specs=(o, o, o),
        name=name, compiler_params=_cp("parallel", "parallel"))(qkv_r, qkv_r, qkv_r, do_r, lse_r, dd_r)


def _rows(start, size, r):
    return pl.ds(start, size) if r == 1 else pl.ds(start, size, stride=r)


def _unit_rows(r, rho, n, nb):
    win = 2 * ATTN_BLOCK if nb > 1 else ATTN_BLOCK
    if isinstance(n, int):
        kb = max(n - 1, 0)
        q_rows = _rows(rho + r * ATTN_BLOCK * n, ATTN_BLOCK, r)
        k_rows = _rows(rho + r * ATTN_BLOCK * kb, win, r)
    else:
        kb = jnp.maximum(n - 1, 0)
        q_rows = pl.ds(pl.multiple_of(n * ATTN_BLOCK, ATTN_BLOCK), ATTN_BLOCK)
        k_rows = pl.ds(pl.multiple_of(kb * ATTN_BLOCK, ATTN_BLOCK), win)
    dist = (n - kb) * ATTN_BLOCK + lax.broadcasted_iota(jnp.int32, (ATTN_BLOCK, win), 0) \
        - lax.broadcasted_iota(jnp.int32, (ATTN_BLOCK, win), 1)
    return q_rows, k_rows, (dist >= 0) & (dist <= ATTN_BLOCK)


def _per_head(x):
    lane = lax.broadcasted_iota(jnp.int32, x.shape, 1)
    zero = jnp.zeros_like(x)
    return [jnp.where(lane < HEAD_DIM, x, zero), jnp.where(lane >= HEAD_DIM, x, zero)]


def _masked_scores(q2, k2, valid):
    return [jnp.where(valid, lax.dot_general(qh, k2, NT, preferred_element_type=F32) * (HEAD_DIM ** -0.5), NEG)
            for qh in _per_head(q2)]


def _attn_units(r, nb, unit):
    if r == 1:
        def four(i, carry):
            for k in range(4):
                unit(0, 4 * i + k)
            return carry
        lax.fori_loop(0, nb // 4, four, 0)
    else:
        for rho in range(r):
            for n in range(nb):
                unit(rho, n)


N_UNITS = 16


def attn_fwd_all(proj, name):
    def body(q_ref, k_ref, v_ref, att_ref, lse_ref, s_scr, p_scr, lse_scr, den_scr):
        for idx, (sub_len, r) in enumerate(PATTERNS):
            nb = sub_len // ATTN_BLOCK
            win = 2 * ATTN_BLOCK if nb > 1 else ATTN_BLOCK

            def scores(rho, n, r=r, nb=nb, win=win):
                u = rho * nb + n
                q_rows, k_rows, valid = _unit_rows(r, rho, n, nb)
                ss = _masked_scores(q_ref[q_rows, :].astype(BF16), k_ref[k_rows, :].astype(BF16), valid)
                for h in range(2):
                    s_scr[2 * u + h, :, 0:win] = ss[h]

            _attn_units(r, nb, scores)

            def softmax(u, carry, win=win):
                lses, dens = [], []
                for h in range(2):
                    sc = s_scr[2 * u + h, :, 0:win]
                    m = jnp.max(sc, axis=1, keepdims=True)
                    p = jnp.exp(sc - m)
                    den = jnp.sum(p, axis=1, keepdims=True)
                    p_scr[2 * u + h, :, 0:win] = p.astype(BF16)
                    lses.append(jnp.broadcast_to(m + jnp.log(den), (ATTN_BLOCK, HEAD_DIM)))
                    dens.append(jnp.broadcast_to(den, (ATTN_BLOCK, HEAD_DIM)))
                lse_scr[u] = jnp.concatenate(lses, axis=1)
                den_scr[u] = jnp.concatenate(dens, axis=1)
                return carry

            lax.fori_loop(0, N_UNITS, softmax, 0, unroll=2)

            def outputs(rho, n, r=r, nb=nb, win=win, idx=idx):
                u = rho * nb + n
                q_rows, k_rows, _ = _unit_rows(r, rho, n, nb)
                vs = _per_head(v_ref[k_rows, :].astype(BF16))
                o = (jnp.dot(p_scr[2 * u, :, 0:win], vs[0], preferred_element_type=F32)
                     + jnp.dot(p_scr[2 * u + 1, :, 0:win], vs[1], preferred_element_type=F32)) / den_scr[u]
                lse = lse_scr[u]
                if idx > 0:
                    old = lse_ref[q_rows, :]
                    top = jnp.maximum(old, lse)
                    new = top + jnp.log(jnp.exp(old - top) + jnp.exp(lse - top))
                    o = att_ref[q_rows, :] * jnp.exp(old - new) + o * jnp.exp(lse - new)
                    lse = new
                att_ref[q_rows, :] = o
                lse_ref[q_rows, :] = lse

            _attn_units(r, nb, outputs)

    blk = lambda first: pl.BlockSpec((SEQ, 128), lambda g: (0, first + g))
    shp = jax.ShapeDtypeStruct((SEQ, D_ATTN), F32)
    big = (2 * N_UNITS, ATTN_BLOCK, 2 * ATTN_BLOCK)
    small = pltpu.VMEM((N_UNITS, ATTN_BLOCK, 128), F32)
    return pl.pallas_call(
        body, out_shape=(shp, shp), grid=(4,), in_specs=[blk(8), blk(12), blk(16)], out_specs=(blk(0), blk(0)),
        scratch_shapes=[pltpu.VMEM(big, F32), pltpu.VMEM(big, BF16), small, small],
        name=name, compiler_params=_cp("parallel"))(proj, proj, proj)


def attn_bwd_all(proj, do, lse, dd, name):
    scale = HEAD_DIM ** -0.5

    def body(q_ref, k_ref, v_ref, do_ref, l_ref, dd_ref, out_ref, dq_s, dk_s, dv_s,
             s_scr, dp_scr, ds_scr, st_scr, dpt_scr, pt_scr, dst_scr, qb_scr, kb_scr, dob_scr):
        dq_s[...] = jnp.zeros_like(dq_s)
        dk_s[...] = jnp.zeros_like(dk_s)
        dv_s[...] = jnp.zeros_like(dv_s)
        for sub_len, r in PATTERNS:
            nb = sub_len // ATTN_BLOCK
            win = 2 * ATTN_BLOCK if nb > 1 else ATTN_BLOCK

            def scores(rho, n, r=r, nb=nb, win=win):
                u = rho * nb + n
                q_rows, k_rows, valid = _unit_rows(r, rho, n, nb)
                kb = max(n - 1, 0) if isinstance(n, int) else jnp.maximum(n - 1, 0)
                dist_t = (n - kb) * ATTN_BLOCK + lax.broadcasted_iota(jnp.int32, (win, ATTN_BLOCK), 1) \
                    - lax.broadcasted_iota(jnp.int32, (win, ATTN_BLOCK), 0)
                valid_t = (dist_t >= 0) & (dist_t <= ATTN_BLOCK)
                q2 = q_ref[q_rows, :].astype(BF16)
                k2 = k_ref[k_rows, :].astype(BF16)
                do2 = do_ref[q_rows, :].astype(BF16)
                qb_scr[u] = q2
                kb_scr[u, 0:win, :] = k2
                dob_scr[u] = do2
                l2 = l_ref[q_rows, :]
                d2 = dd_ref[q_rows, :]
                l2t = l2.T
                d2t = d2.T
                v2 = v_ref[k_rows, :].astype(BF16)
                qs, dos = _per_head(q2), _per_head(do2)
                for h in range(2):
                    c0 = h * HEAD_DIM
                    sc = lax.dot_general(qs[h], k2, NT, preferred_element_type=F32) * scale
                    s_scr[2 * u + h, :, 0:win] = jnp.where(valid, sc, NEG) - l2[:, c0:c0 + 1]
                    dp_scr[2 * u + h, :, 0:win] = lax.dot_general(dos[h], v2, NT, preferred_element_type=F32) \
                        - d2[:, c0:c0 + 1]
                    sct = lax.dot_general(k2, qs[h], NT, preferred_element_type=F32) * scale
                    st_scr[2 * u + h, 0:win, :] = jnp.where(valid_t, sct, NEG) - l2t[c0:c0 + 1, :]
                    dpt_scr[2 * u + h, 0:win, :] = lax.dot_general(v2, dos[h], NT, preferred_element_type=F32) \
                        - d2t[c0:c0 + 1, :]

            _attn_units(r, nb, scores)

            def pointwise(hu, carry, win=win):
                ds_scr[hu, :, 0:win] = (jnp.exp(s_scr[hu, :, 0:win]) * dp_scr[hu, :, 0:win] * scale).astype(BF16)
                pt = jnp.exp(st_scr[hu, 0:win, :])
                pt_scr[hu, 0:win, :] = pt.astype(BF16)
                dst_scr[hu, 0:win, :] = (pt * dpt_scr[hu, 0:win, :] * scale).astype(BF16)
                return carry

            lax.fori_loop(0, 2 * N_UNITS, pointwise, 0, unroll=4)

            def grads(rho, n, r=r, nb=nb, win=win):
                u = rho * nb + n
                q_rows, k_rows, _ = _unit_rows(r, rho, n, nb)
                qs, ks, dos = _per_head(qb_scr[u]), _per_head(kb_scr[u, 0:win, :]), _per_head(dob_scr[u])

                def both(scr, rows, rhs):
                    return (jnp.dot(scr[(2 * u,) + rows], rhs[0], preferred_element_type=F32)
                            + jnp.dot(scr[(2 * u + 1,) + rows], rhs[1], preferred_element_type=F32))

                dq_s[q_rows, :] += both(ds_scr, (slice(None), slice(0, win)), ks)
                dk_s[k_rows, :] += both(dst_scr, (slice(0, win), slice(None)), qs)
                dv_s[k_rows, :] += both(pt_scr, (slice(0, win), slice(None)), dos)

            _attn_units(r, nb, grads)
        out_ref[0] = dq_s[...].astype(BF16)
        out_ref[1] = dk_s[...].astype(BF16)
        out_ref[2] = dv_s[...].astype(BF16)

    blk = lambda first: pl.BlockSpec((SEQ, 128), lambda g: (0, first + g))
    acc = pltpu.VMEM((SEQ, 128), F32)
    big = (2 * N_UNITS, ATTN_BLOCK, 2 * ATTN_BLOCK)
    big_t = (2 * N_UNITS, 2 * ATTN_BLOCK, ATTN_BLOCK)
    return pl.pallas_call(
        body, out_shape=jax.ShapeDtypeStruct((3, SEQ, D_ATTN), BF16), grid=(4,),
        in_specs=[blk(8), blk(12), blk(16), blk(0), blk(0), blk(0)],
        out_specs=pl.BlockSpec((3, SEQ, 128), lambda g: (0, 0, g)),
        scratch_shapes=[acc, acc, acc, pltpu.VMEM(big, F32), pltpu.VMEM(big, F32), pltpu.VMEM(big, BF16),
                        pltpu.VMEM(big_t, F32), pltpu.VMEM(big_t, F32), pltpu.VMEM(big_t, BF16), pltpu.VMEM(big_t, BF16),
                        pltpu.VMEM((N_UNITS, ATTN_BLOCK, 128), BF16),
                        pltpu.VMEM((N_UNITS, 2 * ATTN_BLOCK, 128), BF16), pltpu.VMEM((N_UNITS, ATTN_BLOCK, 128), BF16)],
        name=name, compiler_params=_cp("parallel"))(proj, proj, proj, do, lse, dd)


def rms_gain_bf16(a, g, name):
    def body(a_ref, g_ref, o_ref):
        aa = a_ref[...]
        o_ref[...] = (aa * lax.rsqrt(_mean(aa * aa) + EPS) * g_ref[...]).astype(BF16)

    w = a.shape[1]
    return pl.pallas_call(
        body, out_shape=jax.ShapeDtypeStruct(a.shape, BF16), grid=(SEQ // ROWS,), in_specs=[_row_spec(w), _vec_spec(w)],
        out_specs=_row_spec(w), name=name, compiler_params=_cp("parallel"))(a, g)


def attn_combine_fwd(outs, lses, gao, name):
    def body(o1, o2, o3, l1, l2, l3, g_ref, att_ref, lse_ref, mix_ref):
        a1, a2, a3 = l1[...], l2[...], l3[...]
        m = jnp.maximum(jnp.maximum(a1, a2), a3)
        w1, w2, w3 = jnp.exp(a1 - m), jnp.exp(a2 - m), jnp.exp(a3 - m)
        den = w1 + w2 + w3
        att = (w1 * o1[...] + w2 * o2[...] + w3 * o3[...]) / den
        att_ref[...] = att
        lse_ref[...] = m + jnp.log(den)
        mix_ref[...] = (att * lax.rsqrt(_mean(att * att) + EPS) * g_ref[...]).astype(BF16)

    rs = _row_spec(D_ATTN)
    f = jax.ShapeDtypeStruct((SEQ, D_ATTN), F32)
    return pl.pallas_call(
        body, out_shape=(f, f, jax.ShapeDtypeStruct((SEQ, D_ATTN), BF16)), grid=(SEQ // ROWS,),
        in_specs=[rs] * 6 + [_vec_spec(D_ATTN)], out_specs=(rs, rs, rs),
        name=name, compiler_params=_cp("parallel"))(*outs, *lses, gao)


def attn_combine_bwd(dmixed, att, gao, name):
    def body(dm_ref, att_ref, g_ref, do_ref, dd_ref, dg_ref):
        first = pl.program_id(0) == 0
        att = att_ref[...]
        r = lax.rsqrt(_mean(att * att) + EPS)
        xn = att * r
        dm = dm_ref[...]
        _acc(dg_ref, _rsum(dm * xn), first)
        dyn = dm * g_ref[...]
        do = r * (dyn - xn * _mean(dyn * xn))
        do_ref[...] = do
        same_head = (jnp.right_shift(lax.broadcasted_iota(jnp.int32, (D_ATTN, D_ATTN), 0), 6)
                     == jnp.right_shift(lax.broadcasted_iota(jnp.int32, (D_ATTN, D_ATTN), 1), 6)).astype(F32)
        dd_ref[...] = jnp.dot(do * att, same_head, preferred_element_type=F32, precision=lax.Precision.HIGHEST)

    rs = _row_spec(D_ATTN)
    return pl.pallas_call(
        body,
        out_shape=(jax.ShapeDtypeStruct((SEQ, D_ATTN), F32), jax.ShapeDtypeStruct((SEQ, D_ATTN), F32),
                   jax.ShapeDtypeStruct((1, D_ATTN), F32)),
        grid=(SEQ // ROWS,), in_specs=[_row_spec(D_ATTN, 1), rs, _vec_spec(D_ATTN)],
        out_specs=(rs, rs, _vec_spec(D_ATTN)), name=name, compiler_params=_cp("arbitrary"))(dmixed, att, gao)


def sum3_bf16(a, b, c, name):
    def body(a_ref, b_ref, c_ref, o_ref):
        o_ref[...] = (a_ref[...] + b_ref[...] + c_ref[...]).astype(BF16)

    w = a.shape[1]
    rs = _row_spec(w)
    return pl.pallas_call(
        body, out_shape=jax.ShapeDtypeStruct(a.shape, BF16), grid=(SEQ // ROWS,), in_specs=[rs, rs, rs], out_specs=rs,
        name=name, compiler_params=_cp("parallel"))(a, b, c)


N_FT = D_FF // FFN_TN


def _ffn_specs():
    per = ROWS // FFN_HALO
    cur_g = pl.BlockSpec((ROWS, FFN_TN), lambda j, i: (i, j))
    cur_v = pl.BlockSpec((ROWS, FFN_TN), lambda j, i: (i, j + N_FT))
    halo_g = pl.BlockSpec((FFN_HALO, FFN_TN), lambda j, i: (jnp.maximum(i * per - 1, 0), j))
    halo_v = pl.BlockSpec((FFN_HALO, FFN_TN), lambda j, i: (jnp.maximum(i * per - 1, 0), j + N_FT))
    w_g = pl.BlockSpec((FFN_K, FFN_TN), lambda j, i: (0, j))
    w_v = pl.BlockSpec((FFN_K, FFN_TN), lambda j, i: (0, j + N_FT))
    b_g = pl.BlockSpec((1, FFN_TN), lambda j, i: (0, j))
    b_v = pl.BlockSpec((1, FFN_TN), lambda j, i: (0, j + N_FT))
    return [cur_g, cur_v, halo_g, halo_v, w_g, w_v, b_g, b_v]


def matmul(a, b, kind, out_dtype, tm, tn, name, b_rows=None):
    stacked = b_rows is not None
    b_shape = (N_DEV * b_rows, b.shape[2]) if stacked else b.shape
    if kind == "nn":
        (m, k), n = a.shape, b_shape[1]
        a_spec = pl.BlockSpec((tm, k), lambda j, i: (i, 0))
        b_spec = pl.BlockSpec((k, tn), lambda j, i: (0, j))
        dims = (((1,), (0,)), ((), ()))
    elif kind == "nt":
        (m, k), n = a.shape, b_shape[0]
        a_spec = pl.BlockSpec((tm, k), lambda j, i: (i, 0))
        b_spec = pl.BlockSpec((tn, k), lambda j, i: (j, 0))
        dims = (((1,), (1,)), ((), ()))
    else:
        (k, m), n = a.shape, b_shape[1]
        a_spec = pl.BlockSpec((k, tm), lambda j, i: (0, i))
        b_spec = pl.BlockSpec((k, tn), lambda j, i: (0, j))
        dims = (((0,), (0,)), ((), ()))
    assert m % tm == 0 and n % tn == 0, (name, m, n, tm, tn)
    if stacked:
        assert b_spec.block_shape[0] == b_shape[0] and kind in ("nn", "nt")
        width = b_spec.block_shape[1]
        b_spec = pl.BlockSpec((N_DEV, b_rows, width), (lambda j, i: (0, 0, j)) if kind == "nn" else (lambda j, i: (0, 0, 0)))

    def body(a_ref, b_ref, o_ref):
        bb = b_ref[...].reshape(b_shape[0], -1) if stacked else b_ref[...]
        o_ref[...] = lax.dot_general(a_ref[...], bb, dims, preferred_element_type=F32).astype(o_ref.dtype)

    return pl.pallas_call(
        body, out_shape=jax.ShapeDtypeStruct((m, n), out_dtype), grid=(n // tn, m // tm),
        in_specs=[a_spec, b_spec], out_specs=pl.BlockSpec((tm, tn), lambda j, i: (i, j)),
        name=name, compiler_params=_cp("parallel", "parallel"))(a, b)


def matmul_halves(a, b, tm, tn, name):
    _, m, k = a.shape
    n = b.shape[1]
    b3 = b.reshape(2, k, n)

    def body(a_ref, b_ref, o_ref):
        o_ref[...] = (jnp.dot(a_ref[0], b_ref[0], preferred_element_type=F32)
                      + jnp.dot(a_ref[1], b_ref[1], preferred_element_type=F32))

    return pl.pallas_call(
        body, out_shape=jax.ShapeDtypeStruct((m, n), F32), grid=(n // tn, m // tm),
        in_specs=[pl.BlockSpec((2, tm, k), lambda j, i: (0, i, 0)), pl.BlockSpec((2, k, tn), lambda j, i: (0, 0, j))],
        out_specs=pl.BlockSpec((tm, tn), lambda j, i: (i, j)), name=name, compiler_params=_cp("parallel", "parallel"))(a, b3)


def matmul_tn_halves(a, b, tm, name):
    _, k, m = a.shape
    n = b.shape[1]

    def body(a_ref, b_ref, o_ref):
        o_ref[0] = lax.dot_general(a_ref[0], b_ref[...], (((0,), (0,)), ((), ())), preferred_element_type=F32).astype(BF16)

    return pl.pallas_call(
        body, out_shape=jax.ShapeDtypeStruct((2, m, n), BF16), grid=(2, m // tm),
        in_specs=[pl.BlockSpec((1, k, tm), lambda h, i: (h, 0, i)), pl.BlockSpec((k, n), lambda h, i: (0, 0))],
        out_specs=pl.BlockSpec((1, tm, n), lambda h, i: (h, i, 0)), name=name,
        compiler_params=_cp("parallel", "parallel"))(a, b).reshape(2 * m, n)


def _row_spec(width, col=0):
    return pl.BlockSpec((ROWS, width), lambda i: (i, col))


def _vec_spec(width, rows=1):
    return pl.BlockSpec((rows, width), lambda i: (0, 0))


def _ffn_shifted(cur_ref, halo_ref, pad, s1, s2):
    i = pl.program_id(1)
    pad[0:FFN_HALO, :] = jnp.where(i > 0, halo_ref[...], 0.0)
    pad[FFN_HALO:, :] = cur_ref[0:FFN_HALO, :]
    for k, dst in ((1, s1), (2, s2)):
        dst[0:FFN_HALO, :] = pad[pl.ds(FFN_HALO - k, FFN_HALO), :]
        dst[FFN_HALO:, :] = cur_ref[pl.ds(FFN_HALO - k, ROWS - FFN_HALO), :]


def _ffn_conv(rs, cur_ref, s1, s2, w_ref, b_ref):
    return b_ref[...] + w_ref[0:1, :] * s2[rs, :] + w_ref[1:2, :] * s1[rs, :] + w_ref[2:3, :] * cur_ref[rs, :]


def ffn_act_fwd(up0, wf, bf, name):
    def body(g_ref, v_ref, gh_ref, vh_ref, wg_ref, wv_ref, bg_ref, bv_ref, act_ref, pad, g1, g2, v1, v2):
        _ffn_shifted(g_ref, gh_ref, pad, g1, g2)
        _ffn_shifted(v_ref, vh_ref, pad, v1, v2)

        def chunk(rs):
            gate = _ffn_conv(rs, g_ref, g1, g2, wg_ref, bg_ref)
            val = _ffn_conv(rs, v_ref, v1, v2, wv_ref, bv_ref)
            act_ref[rs, :] = (gate * _sig(gate) * val).astype(BF16)

        _for_chunks(chunk)

    tile = pltpu.VMEM((ROWS, FFN_TN), F32)
    return pl.pallas_call(
        body, out_shape=jax.ShapeDtypeStruct((SEQ, D_FF), BF16), grid=(N_FT, SEQ // ROWS),
        in_specs=_ffn_specs(), out_specs=pl.BlockSpec((ROWS, FFN_TN), lambda j, i: (i, j)),
        scratch_shapes=[pltpu.VMEM((2 * FFN_HALO, FFN_TN), F32), tile, tile, tile, tile],
        name=name, compiler_params=_cp("parallel", "parallel"))(up0, up0, up0, up0, wf, wf, bf, bf)


def ffn_bwd(up0, dact, wf, bf, name):
    per = ROWS // FFN_HALO
    last = SEQ // FFN_HALO - 1

    def body(g_ref, v_ref, gh_ref, vh_ref, wg_ref, wv_ref, bg_ref, bv_ref, da_ref, gn_ref, vn_ref, dan_ref,
             out_ref, dbg_ref, dbv_ref, dwg_ref, dwv_ref, pad, g1, g2, v1, v2, dgp, dvp, acc):
        i = pl.program_id(1)
        first = i == 0
        _ffn_shifted(g_ref, gh_ref, pad, g1, g2)
        _ffn_shifted(v_ref, vh_ref, pad, v1, v2)
        acc[...] = jnp.zeros_like(acc)

        def grads(gate, val, da):
            s = _sig(gate)
            return da * val * (s * (1.0 + gate * (1.0 - s))), da * (gate * s)

        def chunk(rs):
            gate = _ffn_conv(rs, g_ref, g1, g2, wg_ref, bg_ref)
            val = _ffn_conv(rs, v_ref, v1, v2, wv_ref, bv_ref)
            dgate, dval = grads(gate, val, da_ref[rs, :])
            dgp[rs, :] = dgate
            dvp[rs, :] = dval
            acc[0] += dgate
            acc[1] += dval
            for t, (sg, sv) in enumerate(((g2, v2), (g1, v1), (g_ref, v_ref))):
                acc[2 + t] += dgate * sg[rs, :]
                acc[5 + t] += dval * sv[rs, :]

        _for_chunks(chunk)
        _acc(dbg_ref, _rsum(acc[0]), first)
        _acc(dbv_ref, _rsum(acc[1]), first)
        _acc(dwg_ref, jnp.concatenate([_rsum(acc[2 + t]) for t in range(FFN_K)], axis=0), first)
        _acc(dwv_ref, jnp.concatenate([_rsum(acc[5 + t]) for t in range(FFN_K)], axis=0), first)

        def conv_next(cur_ref, nxt_ref, w_ref, b_ref):
            pad[0:FFN_HALO, :] = cur_ref[ROWS - FFN_HALO:, :]
            pad[FFN_HALO:, :] = nxt_ref[...]
            return (b_ref[...] + w_ref[0:1, :] * pad[pl.ds(FFN_HALO - 2, FFN_HALO), :]
                    + w_ref[1:2, :] * pad[pl.ds(FFN_HALO - 1, FFN_HALO), :] + w_ref[2:3, :] * nxt_ref[...])

        gate_n = conv_next(g_ref, gn_ref, wg_ref, bg_ref)
        val_n = conv_next(v_ref, vn_ref, wv_ref, bv_ref)
        dgate_n, dval_n = grads(gate_n, val_n, dan_ref[...])
        inside = i < SEQ // ROWS - 1
        dgp[ROWS:, :] = jnp.where(inside, dgate_n, 0.0)
        dvp[ROWS:, :] = jnp.where(inside, dval_n, 0.0)

        for half, (dp, s1, s2, w_ref) in enumerate(((dgp, g1, g2, wg_ref), (dvp, v1, v2, wv_ref))):
            s1[...] = dp[pl.ds(1, ROWS), :]
            s2[...] = dp[pl.ds(2, ROWS), :]

            def back(rs, dp=dp, s1=s1, s2=s2, w_ref=w_ref, half=half):
                out_ref[half, rs, :] = (w_ref[2:3, :] * dp[rs, :] + w_ref[1:2, :] * s1[rs, :]
                                        + w_ref[0:1, :] * s2[rs, :]).astype(BF16)

            _for_chunks(back)

    tile = pltpu.VMEM((ROWS, FFN_TN), F32)
    ext = pltpu.VMEM((ROWS + FFN_HALO, FFN_TN), F32)
    vec = jax.ShapeDtypeStruct((1, D_FF), F32)
    taps = jax.ShapeDtypeStruct((FFN_K, D_FF), F32)
    cur = pl.BlockSpec((ROWS, FFN_TN), lambda j, i: (i, j))
    nxt = lambda off: pl.BlockSpec((FFN_HALO, FFN_TN), lambda j, i: (jnp.minimum((i + 1) * per, last), j + off))
    vs = pl.BlockSpec((1, FFN_TN), lambda j, i: (0, j))
    ts = pl.BlockSpec((FFN_K, FFN_TN), lambda j, i: (0, j))
    return pl.pallas_call(
        body, out_shape=(jax.ShapeDtypeStruct((2, SEQ, D_FF), BF16), vec, vec, taps, taps), grid=(N_FT, SEQ // ROWS),
        in_specs=_ffn_specs() + [cur, nxt(0), nxt(N_FT), nxt(0)],
        out_specs=(pl.BlockSpec((2, ROWS, FFN_TN), lambda j, i: (0, i, j)), vs, vs, ts, ts),
        scratch_shapes=[pltpu.VMEM((2 * FFN_HALO, FFN_TN), F32), tile, tile, tile, tile, ext, ext,
                        pltpu.VMEM((2 + 2 * FFN_K, SUB, FFN_TN), F32)],
        name=name, compiler_params=_cp("parallel", "arbitrary"))(up0, up0, up0, up0, wf, wf, bf, bf, dact, up0, up0, dact)


def ada_fwd(c_all, w_ada, b_cols, name):
    def body(c_ref, w_ref, b_ref, o_ref):
        cc = c_ref[...]
        sc = (cc * _sig(cc)).astype(BF16)
        o_ref[...] = jnp.dot(sc, w_ref[...].astype(BF16), preferred_element_type=F32) + b_ref[...]

    return pl.pallas_call(body, out_shape=jax.ShapeDtypeStruct((N_DEV, w_ada.shape[1]), F32), name=name,
                          compiler_params=_cp())(c_all, w_ada, b_cols)


def _adam(w, g, m, v):
    m = ADAM_B1 * m + (1.0 - ADAM_B1) * g
    v = ADAM_B2 * v + (1.0 - ADAM_B2) * (g * g)
    m_hat = m / (1.0 - ADAM_B1 ** ADAM_STEP)
    v_hat = v / (1.0 - ADAM_B2 ** ADAM_STEP)
    delta = -ADAM_LR * (m_hat / (jnp.sqrt(v_hat) + ADAM_EPS) + ADAM_WD * w)
    return delta, m, v


def ada_bwd_adamw(c_all_t, dmod_cols, w, m, v, name):
    rows, cols = w.shape
    tr = 256

    def body(ct_ref, dm_ref, w_ref, m_ref, v_ref, g_ref, d_ref, nm_ref, nv_ref):
        def chunk(rs):
            ct = ct_ref[rs, :]
            sc = ct * _sig(ct)
            g = sc[:, 0:1] * dm_ref[0:1, :]
            for b in range(1, N_DEV):
                g = g + sc[:, b:b + 1] * dm_ref[b:b + 1, :]
            g_ref[rs, :] = g
            d_ref[rs, :], nm_ref[rs, :], nv_ref[rs, :] = _adam(w_ref[rs, :], g, m_ref[rs, :], v_ref[rs, :])

        _for_chunks(chunk, 2, tr)

    blk = pl.BlockSpec((tr, cols), lambda i: (i, 0))
    shp = jax.ShapeDtypeStruct((rows, cols), F32)
    return pl.pallas_call(
        body, out_shape=(shp, shp, shp, shp), grid=(rows // tr,),
        in_specs=[pl.BlockSpec((tr, N_DEV), lambda i: (i, 0)), pl.BlockSpec((N_DEV, cols), lambda i: (0, 0)), blk, blk, blk],
        out_specs=(blk, blk, blk, blk), name=name, compiler_params=_cp("parallel"))(c_all_t, dmod_cols, w, m, v)


def sum_adamw(parts, own, w, m, v, tr, name):
    n_parts, rows, cols = parts.shape

    def body(*refs):
        if own is None:
            p_ref, w_ref, m_ref, v_ref, g_ref, d_ref, nm_ref, nv_ref = refs
        else:
            p_ref, own_ref, w_ref, m_ref, v_ref, g_ref, d_ref, nm_ref, nv_ref = refs

        def chunk(rs):
            g = (p_ref[0, rs, :] if own is None else own_ref[rs, :]).astype(F32)
            for k in range(1, n_parts):
                g = g + p_ref[k, rs, :].astype(F32)
            g_ref[rs, :] = g
            d_ref[rs, :], nm_ref[rs, :], nv_ref[rs, :] = _adam(w_ref[rs, :], g, m_ref[rs, :], v_ref[rs, :])

        _for_chunks(chunk, 2 if tr > SUB else 1, tr)

    blk = pl.BlockSpec((tr, cols), lambda i: (i, 0))
    shp = jax.ShapeDtypeStruct((rows, cols), F32)
    args = [parts] + ([] if own is None else [own]) + [w, m, v]
    return pl.pallas_call(
        body, out_shape=(shp, shp, shp, shp), grid=(rows // tr,),
        in_specs=[pl.BlockSpec((n_parts, tr, cols), lambda i: (0, i, 0))] + [blk] * (len(args) - 1),
        out_specs=(blk, blk, blk, blk), name=name, compiler_params=_cp("parallel"))(*args)


MESH = pl.DeviceIdType.MESH
ANY = pl.BlockSpec(memory_space=pl.ANY)


def all_gather(block, name, after=None):
    extra = () if after is None else (after,)

    def body(x_ref, *refs):
        out_ref, send_sems, recv_sems, local_sem = refs[len(extra):]
        x, y, c = lax.axis_index("x"), lax.axis_index("y"), lax.axis_index("c")
        me, sibling = (x, y, c), (x, y, 1 - c)
        chips = [(1 - x, y), (x, 1 - y), (1 - x, 1 - y)]

        def slot(px, py, pc):
            return out_ref.at[4 * px + 2 * py + pc]

        def copy(k, blk, to, src=None):
            return pltpu.make_async_remote_copy(
                src_ref=slot(*blk) if src is None else src, dst_ref=slot(*blk),
                send_sem=send_sems.at[k], recv_sem=recv_sems.at[k], device_id=to, device_id_type=MESH)

        mine = pltpu.make_async_copy(x_ref, slot(*me), local_sem)
        mine.start()
        first = [copy(0, me, sibling, src=x_ref)]
        first += [copy(1 + j, me, (*chip, c), src=x_ref) for j, chip in enumerate(chips)]
        for cp in first:
            cp.start()
        passed = [copy(4 + j, (*chip, c), sibling) for j, chip in enumerate(chips)]
        for j, chip in enumerate(chips):
            copy(1 + j, (*chip, c), me).wait_recv()
            passed[j].start()
        copy(0, sibling, me).wait_recv()
        for j, chip in enumerate(chips):
            copy(4 + j, (*chip, 1 - c), me).wait_recv()
        for cp in first + passed:
            cp.wait_send()
        mine.wait()

    return pl.pallas_call(
        body, out_shape=jax.ShapeDtypeStruct((N_DEV,) + block.shape, block.dtype), in_specs=[ANY] * (1 + len(extra)), out_specs=ANY,
        scratch_shapes=[pltpu.SemaphoreType.DMA((7,)), pltpu.SemaphoreType.DMA((7,)), pltpu.SemaphoreType.DMA],
        name=name)(block, *extra)


HBM = pl.BlockSpec(memory_space=pltpu.HBM)
SEM = pl.BlockSpec(memory_space=pltpu.SEMAPHORE)
EFFECT = pltpu.SideEffectType.DATAFLOW_SIDE_EFFECTING


def _peer_copies(src_ref, land_ref, send_sems, recv_sems, gather):
    x, y, c = lax.axis_index("x"), lax.axis_index("y"), lax.axis_index("c")
    me = 4 * x + 2 * y + c
    copies = []
    for k in range(1, N_DEV):
        px = 1 - x if k & 4 else x
        py = 1 - y if k & 2 else y
        pc = 1 - c if k & 1 else c
        copies.append(pltpu.make_async_remote_copy(
            src_ref=src_ref if gather else src_ref.at[4 * px + 2 * py + pc],
            dst_ref=land_ref.at[me] if gather else land_ref.at[k],
            send_sem=send_sems.at[k - 1], recv_sem=recv_sems.at[k - 1], device_id=(px, py, pc), device_id_type=MESH))
    return copies


def exchange_start(srcs, gather, name, after=None):
    n = len(srcs)
    land_shapes = [(N_DEV,) + src.shape if gather else src.shape for src in srcs]
    extra = () if after is None else (after,)

    def body(*refs):
        src_refs, land_refs = refs[0:n], refs[n:2 * n]
        outs = refs[2 * n + len(extra):]
        for k in range(n):
            for cp in _peer_copies(src_refs[k], land_refs[k], outs[4 * k], outs[4 * k + 1], gather):
                cp.start()
        token = outs[4 * n]
        token[...] = jnp.zeros_like(token)

    out_shape, out_specs, aliases = [], [], {}
    for k, src in enumerate(srcs):
        out_shape += [pltpu.SemaphoreType.DMA((N_DEV - 1,)), pltpu.SemaphoreType.DMA((N_DEV - 1,)),
                      pltpu.HBM(src.shape, src.dtype), pltpu.HBM(land_shapes[k], src.dtype)]
        out_specs += [SEM, SEM, HBM, HBM]
        aliases[k] = 4 * k + 2
        aliases[n + k] = 4 * k + 3
    out_shape.append(jax.ShapeDtypeStruct((8, 128), F32))
    out_specs.append(pl.BlockSpec(memory_space=pltpu.VMEM))
    res = pl.pallas_call(
        body, name=name, out_shape=tuple(out_shape), in_specs=(HBM,) * (2 * n) + (ANY,) * len(extra),
        out_specs=tuple(out_specs), input_output_aliases=aliases,
        compiler_params=pltpu.CompilerParams(has_side_effects=EFFECT),
    )(*[pltpu.with_memory_space_constraint(src, pltpu.HBM) for src in srcs],
      *[pltpu.with_memory_space_constraint(lax.empty(shp, src.dtype), pltpu.HBM) for shp, src in zip(land_shapes, srcs)],
      *extra)
    return [tuple(res[4 * k:4 * k + 4]) for k in range(n)], res[4 * n][0, 0]


def exchange_wait(handles, after, gather, name):
    send_sems, recv_sems, src_thru, land_thru = handles

    def body(src_ref, land_ref, send_sems, recv_sems, after_ref, src_dead, got_ref):
        for cp in _peer_copies(src_ref, land_ref, send_sems, recv_sems, gather):
            cp.wait_send()
            cp.wait_recv()

    return pl.pallas_call(
        body, name=name,
        out_shape=(pltpu.HBM(src_thru.shape, src_thru.dtype), pltpu.HBM(land_thru.shape, land_thru.dtype)),
        in_specs=(HBM, HBM, SEM, SEM, ANY), out_specs=(HBM, HBM), input_output_aliases={0: 0, 1: 1},
        compiler_params=pltpu.CompilerParams(has_side_effects=EFFECT),
    )(src_thru, land_thru, send_sems, recv_sems, after)[1]


def _to_pattern(a, r):
    return a.reshape(SEQ // r, r * a.shape[1])


def local_step(x, tgt, mod, get_w, put_grad, wc, wf, g_mix, bc, lg, lb, gco, gao, g_ffn, bf, g_fin):
    w_in = get_w("w_in", mod)
    proj, h1 = rms_mod_matmul(x, g_mix, mod, 0, 1, w_in, D_IN // N_DEV, "proj_fwd")
    mix_a, u1 = conv_module_fwd(proj, wc, bc, lg, lb, gco, "conv_module_fwd")
    att, lse = attn_fwd_all(proj, "attn_fwd")
    w_out = get_w("w_out", att)
    y1, mixed = norm_concat_matmul(mix_a, att, gao, w_out, "out_proj_fwd")
    x1, h2 = resid_rms_mod_fwd(x, y1, g_ffn, mod, 2, 3, 4, "x1_h2_fwd")
    w_up = get_w("w_up", h2)
    up0 = matmul(h2, w_up, "nt", F32, 512, D_FF, "up_fwd")
    act = ffn_act_fwd(up0, wf, bf, "ffn_act_fwd")
    w_down = get_w("w_down", act)
    loss_t, dx2, dy2, d_gfin, d_gaf = matmul_loss_bwd(act, w_down, x1, tgt, g_fin, mod, 5, "down_fwd_loss")
    dact = matmul(dy2, w_down, "nt", F32, 512, FFN_TN, "down_bwd_x")
    dw_down = matmul(act, dy2, "tn", BF16, 256, D_MODEL, "down_bwd_w")
    dup0, dbf_g, dbf_v, dwf_g, dwf_v = ffn_bwd(up0, dact, wf + put_grad("w_down", dw_down), bf, "ffn_bwd")
    dw_up = matmul_tn_halves(dup0, h2, 256, "up_bwd_w")
    dx1, d_shf, d_scf, d_gffn, dy1, d_gam = matmul_rms_mod_bwd(
        dup0, w_up, x1, dx2, g_ffn + put_grad("w_up", dw_up), mod, 4, y1, 2, "up_bwd_x")
    dmixed = matmul(dy1, w_out, "nt", F32, 512, D_MODEL, "out_proj_bwd_x")
    dw_out = matmul(mixed, dy1, "tn", BF16, 256, D_MODEL, "out_proj_bwd_w")
    do, dd, d_gao = attn_combine_bwd(dmixed, att, gao + put_grad("w_out", dw_out), "attn_combine_bwd")
    dqkv = attn_bwd_all(proj, do, lse, dd, "attn_bwd")
    du1, d_gco, d_lg, d_lb, d_bc, d_wc = conv_module_bwd_a(proj, u1, dmixed, lg, lb, gco, "conv_module_bwd_a")
    dproj_a = conv_module_bwd_b(proj, du1, wc, "conv_module_bwd_b")
    dproj = jnp.concatenate([dproj_a, dqkv[0], dqkv[1], dqkv[2]], axis=1)
    dw_in = matmul(dproj, h1, "tn", BF16, 512, D_MODEL, "proj_bwd_w")
    dx, d_shm, d_scm, d_gmix = matmul_rms_mod_bwd(
        dproj, w_in, x, dx1, g_mix + put_grad("w_in", dw_in), mod, 1, None, 0, "proj_bwd_x", b_rows=D_IN // N_DEV)
    dmod = jnp.concatenate([d_shm, d_scm, d_gam, d_shf, d_scf, d_gaf], axis=1)
    small = dict(g_norm_mix=d_gmix, b_conv_dw=d_bc, ln_conv_g=d_lg, ln_conv_b=d_lb, g_conv_out=d_gco, g_attn_out=d_gao,
                 g_norm_ffn=d_gffn, b_ffn_dw=jnp.concatenate([dbf_g, dbf_v], axis=1), g_final=d_gfin,
                 w_conv_dw=d_wc, w_ffn_dw=jnp.concatenate([dwf_g, dwf_v], axis=1), dmod=dmod, loss=loss_t[0:1, 0:1])
    return dx, small


def _padw(a, width):
    return jnp.pad(a, ((0, 0), (0, width - a.shape[1])))


def pack_small(t):
    wide = jnp.concatenate([_padw(t["dmod"], PACK_W), _padw(t["b_ffn_dw"], PACK_W), _padw(t["w_ffn_dw"], PACK_W),
                            _padw(t["loss"], PACK_W), jnp.zeros((2, PACK_W), F32)], axis=0)
    z512 = jnp.zeros((1, 512), F32)
    narrow = jnp.concatenate([
        t["g_norm_mix"], t["g_norm_ffn"], t["g_final"],
        jnp.concatenate([t["b_conv_dw"], t["ln_conv_g"]], axis=1),
        jnp.concatenate([t["ln_conv_b"], t["g_conv_out"]], axis=1),
        jnp.concatenate([t["g_attn_out"], z512], axis=1),
        jnp.zeros((2, 1024), F32),
        jnp.pad(t["w_conv_dw"], ((0, 1), (0, 0))).reshape(16, 1024)], axis=0)
    return jnp.concatenate([wide, narrow.reshape(4, PACK_W), jnp.zeros((4, PACK_W), F32)], axis=0)


_NARROW = lambda k, off=0: (8 + k // 6, (k % 6) * 1024 + off)
PACKED_AT = dict(
    b_ada=(0, 0, N_MOD * D_MODEL), b_ffn_dw=(1, 0, 2 * D_FF),
    g_norm_mix=_NARROW(0) + (D_MODEL,), g_norm_ffn=_NARROW(1) + (D_MODEL,), g_final=_NARROW(2) + (D_MODEL,),
    b_conv_dw=_NARROW(3) + (D_CONV,), ln_conv_g=_NARROW(3, 512) + (D_CONV,),
    ln_conv_b=_NARROW(4) + (D_CONV,), g_conv_out=_NARROW(4, 512) + (D_CONV,), g_attn_out=_NARROW(5) + (D_ATTN,))
SMALL_ORDER = list(PACKED_AT)


def small_adamw(parts, wmv, name):
    def body(*refs):
        p_ref = refs[0]
        ins = refs[1:1 + 3 * len(SMALL_ORDER)]
        outs = refs[1 + 3 * len(SMALL_ORDER):]
        g = p_ref[0]
        for k in range(1, N_DEV):
            g = g + p_ref[k]
        for i, n in enumerate(SMALL_ORDER):
            row, lane, width = PACKED_AT[n]
            gp = g[row:row + 1, lane:lane + width]
            w_ref, m_ref, v_ref = ins[3 * i:3 * i + 3]
            g_ref, d_ref, nm_ref, nv_ref = outs[4 * i:4 * i + 4]
            g_ref[...] = gp
            d_ref[...], nm_ref[...], nv_ref[...] = _adam(w_ref[...], gp, m_ref[...], v_ref[...])
        wc_ref, wf_ref, loss_ref = outs[4 * len(SMALL_ORDER):]
        for j in range(CONV_K):
            row, lane = _NARROW(8 + j // 2, (j % 2) * 512)
            wc_ref[j:j + 1, :] = g[row:row + 1, lane:lane + D_CONV]
        wf_ref[...] = g[2:2 + FFN_K, 0:2 * D_FF]
        loss_ref[...] = jnp.broadcast_to(g[5:6, 0:1], (8, 128))

    args, out_shape = [parts], []
    for n in SMALL_ORDER:
        args += list(wmv[n])
        out_shape += [jax.ShapeDtypeStruct(wmv[n][0].shape, F32)] * 4
    out_shape += [jax.ShapeDtypeStruct((CONV_K, D_CONV), F32), jax.ShapeDtypeStruct((FFN_K, 2 * D_FF), F32),
                  jax.ShapeDtypeStruct((8, 128), F32)]
    res = pl.pallas_call(body, out_shape=tuple(out_shape), name=name, compiler_params=_cp())(*args)
    per = {n: tuple(res[4 * i:4 * i + 4]) for i, n in enumerate(SMALL_ORDER)}
    return per, res[-3], res[-2], res[-1][0, 0]


def shard_adamw(items, name):
    def body(*refs):
        ins, outs = refs[:4 * len(items)], refs[4 * len(items):]
        for i in range(len(items)):
            g_ref, w_ref, m_ref, v_ref = ins[4 * i:4 * i + 4]
            og_ref, d_ref, nm_ref, nv_ref = outs[4 * i:4 * i + 4]
            og_ref[...] = g_ref[...]
            d_ref[...], nm_ref[...], nv_ref[...] = _adam(w_ref[...], g_ref[...], m_ref[...], v_ref[...])

    args = [a for item in items for a in item]
    out_shape = tuple(jax.ShapeDtypeStruct(item[1].shape, F32) for item in items for _ in range(4))
    res = pl.pallas_call(body, out_shape=out_shape, name=name, compiler_params=_cp())(*args)
    return [tuple(res[4 * i:4 * i + 4]) for i in range(len(items))]


def _shard(full, n_cols, me):
    return lax.dynamic_slice(full, (0, me * n_cols), (full.shape[0], n_cols))


WEIGHTS = ["w_ada", "b_ada", "g_norm_mix", "w_in", "w_conv_dw", "b_conv_dw", "ln_conv_g", "ln_conv_b", "g_conv_out",
           "g_attn_out", "w_out", "g_norm_ffn", "w_up", "w_ffn_dw", "b_ffn_dw", "w_down", "g_final"]
SMALL_REPLICATED = ["g_norm_mix", "b_conv_dw", "ln_conv_g", "ln_conv_b", "g_conv_out", "g_attn_out", "g_norm_ffn",
                    "b_ffn_dw", "g_final"]


def kernel(x, c, w_ada, b_ada, g_norm_mix, w_in, w_conv_dw, b_conv_dw, ln_conv_g, ln_conv_b, g_conv_out, g_attn_out, w_out, g_norm_ffn, w_up, w_ffn_dw, b_ffn_dw, w_down, g_final, loss_target, m_w_ada, m_b_ada, m_g_norm_mix, m_w_in, m_w_conv_dw, m_b_conv_dw, m_ln_conv_g, m_ln_conv_b, m_g_conv_out, m_g_attn_out, m_w_out, m_g_norm_ffn, m_w_up, m_w_ffn_dw, m_b_ffn_dw, m_w_down, m_g_final, v_w_ada, v_b_ada, v_g_norm_mix, v_w_in, v_w_conv_dw, v_b_conv_dw, v_ln_conv_g, v_ln_conv_b, v_g_conv_out, v_g_attn_out, v_w_out, v_g_norm_ffn, v_w_up, v_w_ffn_dw, v_b_ffn_dw, v_w_down, v_g_final):
    args = dict(locals())
    me = 4 * lax.axis_index("x") + 2 * lax.axis_index("y") + lax.axis_index("c")

    def flat(name, prefix=""):
        a = args[prefix + name]
        return a.reshape(a.shape[-2] if a.ndim > 1 else 1, a.shape[-1])

    def flat_t(name, prefix=""):
        return args[prefix + name][0].T

    n_in, n_up, r_out, r_down = w_in.shape[2], w_up.shape[2], w_out.shape[1], w_down.shape[1]
    n_ada, n_wc, n_wf = w_ada.shape[2], w_conv_dw.shape[2], w_ffn_dw.shape[2]
    taps_c = jnp.pad(flat("w_conv_dw").reshape(1, CONV_K * n_wc), ((0, 0), (0, 2 * D_MODEL - CONV_K * n_wc)))
    taps_f = jnp.pad(flat("w_ffn_dw").reshape(1, FFN_K * n_wf), ((0, 0), (0, 3 * D_MODEL - FFN_K * n_wf)))
    first = jnp.concatenate([c, taps_c.reshape(2, D_MODEL), taps_f.reshape(3, D_MODEL), jnp.zeros((2, D_MODEL), F32)], axis=0)
    w_in_block = flat_t("w_in").astype(BF16)
    hi = lax.reduce_precision(first, 8, 7)
    mid = lax.reduce_precision(first - hi, 8, 7)
    low = lax.reduce_precision(first - hi - mid, 8, 7)
    terms = jnp.concatenate([hi, mid, low, jnp.zeros((8, D_MODEL), F32)], axis=0).astype(BF16)
    first_block = all_gather(jnp.concatenate([w_in_block, terms], axis=0), "gather_c_taps_w_in")
    terms = first_block[:, n_in:n_in + 24, :].astype(F32).reshape(N_DEV, 3, 8, D_MODEL)
    first_all = (terms[:, 0] + terms[:, 1]) + terms[:, 2]
    c_all = first_all[:, 0, :]
    wc_full = first_all[:, 1:3, :].reshape(N_DEV, 2 * D_MODEL)[:, :CONV_K * n_wc].reshape(N_DEV, CONV_K, n_wc)
    wc_full = wc_full.transpose(1, 0, 2).reshape(CONV_K, D_CONV)
    wf_full = first_all[:, 3:6, :].reshape(N_DEV, 3 * D_MODEL)[:, :FFN_K * n_wf].reshape(N_DEV, FFN_K, n_wf)
    wf_full = wf_full.transpose(1, 0, 2).reshape(FFN_K, 2 * D_FF)
    mod_cols = ada_fwd(c_all, flat("w_ada"), _shard(flat("b_ada"), n_ada, me), "ada_fwd")
    mod_all = all_gather(mod_cols, "gather_mod")
    mod = lax.dynamic_index_in_dim(mod_all, me, axis=1, keepdims=False).reshape(N_MOD, D_MODEL)
    mod = jnp.pad(mod, ((0, 2), (0, 0)))

    order = ("w_out", "w_up", "w_down")
    blocks = dict(w_up=flat_t("w_up").astype(BF16), w_out=flat("w_out").astype(BF16), w_down=flat("w_down").astype(BF16))
    handles, tok = exchange_start([blocks[name] for name in order], True, "gather_weights_start", mod_all)
    gathers = dict(zip(order, handles))
    mod = mod + tok

    def gathered(name, after):
        land = exchange_wait(gathers[name], after, True, f"gather_{name}_wait")
        return lax.dynamic_update_index_in_dim(land, blocks[name], me, axis=0)

    def get_w(name, after):
        if name == "w_in":
            return first_block
        return gathered(name, after).reshape(-1, D_MODEL)

    exchanges, own = {}, {}

    def put_grad(name, dw, after=None):
        dev_major = dw.reshape(N_DEV, -1, D_MODEL)
        own[name] = lax.dynamic_index_in_dim(dev_major, me, axis=0, keepdims=False)
        (exchanges[name],), token = exchange_start([dev_major], False, f"exchange_{name}_start", after)
        return token

    grad_x, small = local_step(
        x[0], loss_target[0], mod, get_w, put_grad, wc_full, wf_full,
        flat("g_norm_mix"), flat("b_conv_dw"), flat("ln_conv_g"), flat("ln_conv_b"), flat("g_conv_out"),
        flat("g_attn_out"), flat("g_norm_ffn"), flat("b_ffn_dw"), flat("g_final"))

    out = {}

    def finish(name, tr, after):
        parts = exchange_wait(exchanges[name], after, False, f"exchange_{name}_wait")
        if name in ("w_in", "w_up"):
            res = sum_adamw(parts, own[name], flat_t(name), flat_t(name, "m_"), flat_t(name, "v_"), tr, "adamw_" + name)
            out[name] = tuple(r.T for r in res)
        else:
            res = out[name] = sum_adamw(parts, own[name], flat(name), flat(name, "m_"), flat(name, "v_"), tr, "adamw_" + name)
        return res[0]

    after = finish("w_down", r_down, grad_x)
    after = finish("w_up", n_up // 2, after)
    after = finish("w_out", r_out, after)
    after = finish("w_in", n_in, after)

    small_all = all_gather(pack_small(small), "gather_small", after)

    wmv = {n: (flat(n), flat(n, "m_"), flat(n, "v_")) for n in SMALL_ORDER}
    per, g_wc, g_wf, loss = small_adamw(small_all, wmv, "adamw_small")
    out.update(per)
    taps = shard_adamw([(_shard(g_wc, n_wc, me), flat("w_conv_dw"), flat("w_conv_dw", "m_"), flat("w_conv_dw", "v_")),
                        (_shard(g_wf, n_wf, me), flat("w_ffn_dw"), flat("w_ffn_dw", "m_"), flat("w_ffn_dw", "v_"))],
                       "adamw_taps")
    out["w_conv_dw"], out["w_ffn_dw"] = taps

    dmod_cols = _shard(small_all[:, 0, :], n_ada, me)
    out["w_ada"] = ada_bwd_adamw(c_all.T, dmod_cols, flat("w_ada"), flat("w_ada", "m_"), flat("w_ada", "v_"), "adamw_w_ada")

    result = [loss, grad_x[None]]
    for k in range(4):
        result += [out[n][k].reshape(args[n].shape) for n in WEIGHTS]
    return tuple(result)
```

```python
import functools

import jax
import jax.numpy as jnp
from jax import lax
from jax.experimental import pallas as pl
from jax.experimental.pallas import tpu as pltpu

F32 = jnp.float32
BF16 = jnp.bfloat16

N_DEV = 8
SEQ = 2048
D_MODEL = 1024
D_CONV = 512
D_ATTN = 512
HEAD_DIM = 64
CONV_K = 31
D_FF = 2816
FFN_K = 3
D_IN = 2 * D_CONV + 3 * D_ATTN
N_MOD = 6
EPS = 1e-6
ATTN_BLOCK = 128
PATTERNS = ((2048, 1), (512, 4), (128, 16))
NEG = -1e30

ADAM_LR, ADAM_B1, ADAM_B2, ADAM_EPS, ADAM_WD, ADAM_STEP = 0.001, 0.9, 0.999, 1e-08, 0.01, 10

ROWS = 256
CONV_HALO = 32
FFN_HALO = 8
FFN_TN = 1408
VMEM_LIMIT = 56 * 1024 * 1024
PACK_ROWS, PACK_W = 16, 6144


NT = (((1,), (1,)), ((), ()))


def _cp(*sem):
    return pltpu.CompilerParams(dimension_semantics=sem if sem else None, vmem_limit_bytes=VMEM_LIMIT)


def _sig(x):
    return 1.0 / (1.0 + jnp.exp(-x))


def _rsum(x):
    return jnp.sum(x, axis=0, keepdims=True)


def _mean(x):
    return jnp.mean(x, axis=-1, keepdims=True)


def _acc(ref, val, first):
    @pl.when(first)
    def _():
        ref[...] = val

    @pl.when(jnp.logical_not(first))
    def _():
        ref[...] += val


SUB = 16


def _for_chunks(fn, unroll=1, rows=ROWS):
    def step(i, carry):
        fn(pl.ds(pl.multiple_of(i * SUB, SUB), SUB))
        return carry

    lax.fori_loop(0, rows // SUB, step, 0, unroll=unroll)


def rms_mod_fwd(x, g, mod, sh_row, sc_row, name):
    def body(x_ref, g_ref, mod_ref, h_ref):
        xx = x_ref[...]
        r = lax.rsqrt(_mean(xx * xx) + EPS)
        h = xx * r * g_ref[...]
        h_ref[...] = (h * (1.0 + mod_ref[sc_row:sc_row + 1, :]) + mod_ref[sh_row:sh_row + 1, :]).astype(BF16)

    return pl.pallas_call(
        body, out_shape=jax.ShapeDtypeStruct((SEQ, D_MODEL), BF16), grid=(SEQ // ROWS,),
        in_specs=[_row_spec(D_MODEL), _vec_spec(D_MODEL), _vec_spec(D_MODEL, 8)],
        out_specs=_row_spec(D_MODEL), name=name, compiler_params=_cp("parallel"))(x, g, mod)


def resid_rms_mod_fwd(x, y, g, mod, ga_row, sh_row, sc_row, name):
    def body(x_ref, y_ref, g_ref, mod_ref, x1_ref, h_ref):
        x1 = x_ref[...] + mod_ref[ga_row:ga_row + 1, :] * y_ref[...]
        x1_ref[...] = x1
        r = lax.rsqrt(_mean(x1 * x1) + EPS)
        h = x1 * r * g_ref[...]
        h_ref[...] = (h * (1.0 + mod_ref[sc_row:sc_row + 1, :]) + mod_ref[sh_row:sh_row + 1, :]).astype(BF16)

    return pl.pallas_call(
        body, out_shape=(jax.ShapeDtypeStruct((SEQ, D_MODEL), F32), jax.ShapeDtypeStruct((SEQ, D_MODEL), BF16)),
        grid=(SEQ // ROWS,),
        in_specs=[_row_spec(D_MODEL), _row_spec(D_MODEL), _vec_spec(D_MODEL), _vec_spec(D_MODEL, 8)],
        out_specs=(_row_spec(D_MODEL), _row_spec(D_MODEL)), name=name, compiler_params=_cp("parallel"))(x, y, g, mod)


def final_loss_bwd(x1, y2, tgt, g, mod, ga_row, name):
    def body(x1_ref, y2_ref, t_ref, g_ref, mod_ref, loss_ref, dx2_ref, dy2_ref, dg_ref, dga_ref):
        first = pl.program_id(0) == 0
        ga = mod_ref[ga_row:ga_row + 1, :]
        y2 = y2_ref[...]
        x2 = x1_ref[...] + ga * y2
        r = lax.rsqrt(_mean(x2 * x2) + EPS)
        xn = x2 * r
        err = xn * g_ref[...] - t_ref[...]
        _acc(loss_ref, jnp.broadcast_to(0.5 * jnp.sum(_mean(err * err)), (8, 128)), first)
        dy = err * (1.0 / D_MODEL)
        _acc(dg_ref, _rsum(dy * xn), first)
        dxn = dy * g_ref[...]
        dx2 = r * (dxn - xn * _mean(dxn * xn))
        dx2_ref[...] = dx2
        dy2_ref[...] = (dx2 * ga).astype(BF16)
        _acc(dga_ref, _rsum(dx2 * y2), first)

    vec = jax.ShapeDtypeStruct((1, D_MODEL), F32)
    return pl.pallas_call(
        body,
        out_shape=(jax.ShapeDtypeStruct((8, 128), F32), jax.ShapeDtypeStruct((SEQ, D_MODEL), F32),
                   jax.ShapeDtypeStruct((SEQ, D_MODEL), BF16), vec, vec),
        grid=(SEQ // ROWS,),
        in_specs=[_row_spec(D_MODEL), _row_spec(D_MODEL), _row_spec(D_MODEL), _vec_spec(D_MODEL), _vec_spec(D_MODEL, 8)],
        out_specs=(pl.BlockSpec((8, 128), lambda i: (0, 0)), _row_spec(D_MODEL), _row_spec(D_MODEL),
                   _vec_spec(D_MODEL), _vec_spec(D_MODEL)),
        name=name, compiler_params=_cp("arbitrary"))(x1, y2, tgt, g, mod)


def rms_mod_bwd(x, dh, dres, g, mod, sc_row, y, ga_row, name):
    gated = y is not None

    def body(*refs):
        if gated:
            x_ref, dh_ref, dres_ref, g_ref, mod_ref, y_ref, dx_ref, dsh_ref, dsc_ref, dg_ref, dy_ref, dga_ref = refs
        else:
            x_ref, dh_ref, dres_ref, g_ref, mod_ref, dx_ref, dsh_ref, dsc_ref, dg_ref = refs
        first = pl.program_id(0) == 0
        xx = x_ref[...]
        dh = dh_ref[...]
        gg = g_ref[...]
        r = lax.rsqrt(_mean(xx * xx) + EPS)
        xn = xx * r
        _acc(dsh_ref, _rsum(dh), first)
        _acc(dsc_ref, _rsum(dh * (xn * gg)), first)
        dt = dh * (1.0 + mod_ref[sc_row:sc_row + 1, :])
        _acc(dg_ref, _rsum(dt * xn), first)
        dxn = dt * gg
        dx = dres_ref[...] + r * (dxn - xn * _mean(dxn * xn))
        dx_ref[...] = dx
        if gated:
            _acc(dga_ref, _rsum(dx * y_ref[...]), first)
            dy_ref[...] = (dx * mod_ref[ga_row:ga_row + 1, :]).astype(BF16)

    vec = jax.ShapeDtypeStruct((1, D_MODEL), F32)
    in_specs = [_row_spec(D_MODEL), _row_spec(D_MODEL), _row_spec(D_MODEL), _vec_spec(D_MODEL), _vec_spec(D_MODEL, 8)]
    out_shape = [jax.ShapeDtypeStruct((SEQ, D_MODEL), F32), vec, vec, vec]
    out_specs = [_row_spec(D_MODEL), _vec_spec(D_MODEL), _vec_spec(D_MODEL), _vec_spec(D_MODEL)]
    args = [x, dh, dres, g, mod]
    if gated:
        in_specs.append(_row_spec(D_MODEL))
        out_shape += [jax.ShapeDtypeStruct((SEQ, D_MODEL), BF16), vec]
        out_specs += [_row_spec(D_MODEL), _vec_spec(D_MODEL)]
        args.append(y)
    return pl.pallas_call(
        body, out_shape=tuple(out_shape), grid=(SEQ // ROWS,), in_specs=in_specs, out_specs=tuple(out_specs),
        name=name, compiler_params=_cp("arbitrary"))(*args)


def rms_mod_matmul(x, g, mod, sh_row, sc_row, b, b_rows, name):
    n = N_DEV * b_rows

    def body(x_ref, g_ref, mod_ref, b_ref, o_ref, h_ref):
        xx = x_ref[...]
        r = lax.rsqrt(_mean(xx * xx) + EPS)
        h = (xx * r * g_ref[...] * (1.0 + mod_ref[sc_row:sc_row + 1, :]) + mod_ref[sh_row:sh_row + 1, :]).astype(BF16)
        h_ref[...] = h
        o_ref[...] = lax.dot_general(h, b_ref[...].reshape(n, D_MODEL), NT, preferred_element_type=F32)

    return pl.pallas_call(
        body, out_shape=(jax.ShapeDtypeStruct((SEQ, n), F32), jax.ShapeDtypeStruct((SEQ, D_MODEL), BF16)),
        grid=(SEQ // ROWS,),
        in_specs=[_row_spec(D_MODEL), _vec_spec(D_MODEL), _vec_spec(D_MODEL, 8),
                  pl.BlockSpec((N_DEV, b_rows, D_MODEL), lambda i: (0, 0, 0))],
        out_specs=(_row_spec(n), _row_spec(D_MODEL)), name=name, compiler_params=_cp("parallel"))(x, g, mod, b)


def norm_concat_matmul(mix_a, att, gao, w, name):
    def body(a_ref, att_ref, g_ref, w_ref, y_ref, mixed_ref):
        aa = att_ref[...]
        mixed_ref[:, 0:D_CONV] = a_ref[...]
        mixed_ref[:, D_CONV:] = (aa * lax.rsqrt(_mean(aa * aa) + EPS) * g_ref[...]).astype(BF16)
        y_ref[...] = jnp.dot(mixed_ref[...], w_ref[...], preferred_element_type=F32)

    return pl.pallas_call(
        body, out_shape=(jax.ShapeDtypeStruct((SEQ, D_MODEL), F32), jax.ShapeDtypeStruct((SEQ, D_MODEL), BF16)),
        grid=(SEQ // ROWS,),
        in_specs=[_row_spec(D_CONV), _row_spec(D_ATTN), _vec_spec(D_ATTN), pl.BlockSpec(w.shape, lambda i: (0, 0))],
        out_specs=(_row_spec(D_MODEL), _row_spec(D_MODEL)), name=name, compiler_params=_cp("parallel"))(mix_a, att, gao, w)


def resid_rms_mod_matmul(x, y, g, mod, ga_row, sh_row, sc_row, w, name):
    n = w.shape[0]

    def body(x_ref, y_ref, g_ref, mod_ref, w_ref, o_ref, x1_ref, h_ref):
        x1 = x_ref[...] + mod_ref[ga_row:ga_row + 1, :] * y_ref[...]
        x1_ref[...] = x1
        r = lax.rsqrt(_mean(x1 * x1) + EPS)
        h = (x1 * r * g_ref[...] * (1.0 + mod_ref[sc_row:sc_row + 1, :]) + mod_ref[sh_row:sh_row + 1, :]).astype(BF16)
        h_ref[...] = h
        o_ref[...] = lax.dot_general(h, w_ref[...], NT, preferred_element_type=F32)

    return pl.pallas_call(
        body,
        out_shape=(jax.ShapeDtypeStruct((SEQ, n), F32), jax.ShapeDtypeStruct((SEQ, D_MODEL), F32),
                   jax.ShapeDtypeStruct((SEQ, D_MODEL), BF16)),
        grid=(SEQ // ROWS,),
        in_specs=[_row_spec(D_MODEL), _row_spec(D_MODEL), _vec_spec(D_MODEL), _vec_spec(D_MODEL, 8),
                  pl.BlockSpec(w.shape, lambda i: (0, 0))],
        out_specs=(_row_spec(n), _row_spec(D_MODEL), _row_spec(D_MODEL)),
        name=name, compiler_params=_cp("parallel"))(x, y, g, mod, w)


def matmul_combine_bwd(dy, w, att, gao, name):
    def body(dy_ref, w_ref, att_ref, g_ref, dm_ref, do_ref, dd_ref, dg_ref):
        first = pl.program_id(0) == 0
        dm_ref[...] = lax.dot_general(dy_ref[...], w_ref[...], NT, preferred_element_type=F32)
        att = att_ref[...]
        r = lax.rsqrt(_mean(att * att) + EPS)
        xn = att * r
        dm = dm_ref[:, D_CONV:]
        _acc(dg_ref, _rsum(dm * xn), first)
        dyn = dm * g_ref[...]
        do = r * (dyn - xn * _mean(dyn * xn))
        do_ref[...] = do
        same_head = (jnp.right_shift(lax.broadcasted_iota(jnp.int32, (D_ATTN, D_ATTN), 0), 6)
                     == jnp.right_shift(lax.broadcasted_iota(jnp.int32, (D_ATTN, D_ATTN), 1), 6)).astype(F32)
        dd_ref[...] = jnp.dot(do * att, same_head, preferred_element_type=F32, precision=lax.Precision.HIGHEST)

    rs = _row_spec(D_ATTN)
    f = jax.ShapeDtypeStruct((SEQ, D_ATTN), F32)
    return pl.pallas_call(
        body, out_shape=(jax.ShapeDtypeStruct((SEQ, D_MODEL), F32), f, f, jax.ShapeDtypeStruct((1, D_ATTN), F32)),
        grid=(SEQ // ROWS,),
        in_specs=[_row_spec(D_MODEL), pl.BlockSpec(w.shape, lambda i: (0, 0)), rs, _vec_spec(D_ATTN)],
        out_specs=(_row_spec(D_MODEL), rs, rs, _vec_spec(D_ATTN)),
        name=name, compiler_params=_cp("arbitrary"))(dy, w, att, gao)


def matmul_loss_bwd(act, w, x1, tgt, g, mod, ga_row, name):
    def body(a_ref, w_ref, x1_ref, t_ref, g_ref, mod_ref, loss_ref, dx2_ref, dy2_ref, dg_ref, dga_ref):
        first = pl.program_id(0) == 0
        ga = mod_ref[ga_row:ga_row + 1, :]
        y2 = jnp.dot(a_ref[...], w_ref[...], preferred_element_type=F32)
        x2 = x1_ref[...] + ga * y2
        r = lax.rsqrt(_mean(x2 * x2) + EPS)
        xn = x2 * r
        err = xn * g_ref[...] - t_ref[...]
        _acc(loss_ref, jnp.broadcast_to(0.5 * jnp.sum(_mean(err * err)), (8, 128)), first)
        dy = err * (1.0 / D_MODEL)
        _acc(dg_ref, _rsum(dy * xn), first)
        dxn = dy * g_ref[...]
        dx2 = r * (dxn - xn * _mean(dxn * xn))
        dx2_ref[...] = dx2
        dy2_ref[...] = (dx2 * ga).astype(BF16)
        _acc(dga_ref, _rsum(dx2 * y2), first)

    vec = jax.ShapeDtypeStruct((1, D_MODEL), F32)
    return pl.pallas_call(
        body,
        out_shape=(jax.ShapeDtypeStruct((8, 128), F32), jax.ShapeDtypeStruct((SEQ, D_MODEL), F32),
                   jax.ShapeDtypeStruct((SEQ, D_MODEL), BF16), vec, vec),
        grid=(SEQ // ROWS,),
        in_specs=[_row_spec(act.shape[1]), pl.BlockSpec(w.shape, lambda i: (0, 0)), _row_spec(D_MODEL), _row_spec(D_MODEL),
                  _vec_spec(D_MODEL), _vec_spec(D_MODEL, 8)],
        out_specs=(pl.BlockSpec((8, 128), lambda i: (0, 0)), _row_spec(D_MODEL), _row_spec(D_MODEL),
                   _vec_spec(D_MODEL), _vec_spec(D_MODEL)),
        name=name, compiler_params=_cp("arbitrary"))(act, w, x1, tgt, g, mod)


def matmul_rms_mod_bwd(a, b, x, dres, g, mod, sc_row, y, ga_row, name, b_rows=None):
    gated = y is not None
    halves = a.ndim == 3
    if halves:
        k2 = a.shape[2]
        b_arg = b.reshape(2, k2, D_MODEL)
        a_spec = pl.BlockSpec((2, ROWS, k2), lambda i: (0, i, 0))
        b_spec = pl.BlockSpec((2, k2, D_MODEL), lambda i: (0, 0, 0))
    else:
        b_arg = b
        a_spec = _row_spec(a.shape[1])
        b_spec = pl.BlockSpec((N_DEV, b_rows, D_MODEL), lambda i: (0, 0, 0))

    def body(*refs):
        if gated:
            a_ref, b_ref, x_ref, dres_ref, g_ref, mod_ref, y_ref, dx_ref, dsh_ref, dsc_ref, dg_ref, dy_ref, dga_ref = refs
        else:
            a_ref, b_ref, x_ref, dres_ref, g_ref, mod_ref, dx_ref, dsh_ref, dsc_ref, dg_ref = refs
        first = pl.program_id(0) == 0
        if halves:
            dh = (jnp.dot(a_ref[0], b_ref[0], preferred_element_type=F32)
                  + jnp.dot(a_ref[1], b_ref[1], preferred_element_type=F32))
        else:
            dh = jnp.dot(a_ref[...], b_ref[...].reshape(N_DEV * b_rows, D_MODEL), preferred_element_type=F32)
        xx = x_ref[...]
        gg = g_ref[...]
        r = lax.rsqrt(_mean(xx * xx) + EPS)
        xn = xx * r
        _acc(dsh_ref, _rsum(dh), first)
        _acc(dsc_ref, _rsum(dh * (xn * gg)), first)
        dt = dh * (1.0 + mod_ref[sc_row:sc_row + 1, :])
        _acc(dg_ref, _rsum(dt * xn), first)
        dxn = dt * gg
        dx = dres_ref[...] + r * (dxn - xn * _mean(dxn * xn))
        dx_ref[...] = dx
        if gated:
            _acc(dga_ref, _rsum(dx * y_ref[...]), first)
            dy_ref[...] = (dx * mod_ref[ga_row:ga_row + 1, :]).astype(BF16)

    vec = jax.ShapeDtypeStruct((1, D_MODEL), F32)
    in_specs = [a_spec, b_spec, _row_spec(D_MODEL), _row_spec(D_MODEL), _vec_spec(D_MODEL), _vec_spec(D_MODEL, 8)]
    out_shape = [jax.ShapeDtypeStruct((SEQ, D_MODEL), F32), vec, vec, vec]
    out_specs = [_row_spec(D_MODEL), _vec_spec(D_MODEL), _vec_spec(D_MODEL), _vec_spec(D_MODEL)]
    args = [a, b_arg, x, dres, g, mod]
    if gated:
        in_specs.append(_row_spec(D_MODEL))
        out_shape += [jax.ShapeDtypeStruct((SEQ, D_MODEL), BF16), vec]
        out_specs += [_row_spec(D_MODEL), _vec_spec(D_MODEL)]
        args.append(y)
    return pl.pallas_call(
        body, out_shape=tuple(out_shape), grid=(SEQ // ROWS,), in_specs=in_specs, out_specs=tuple(out_specs),
        name=name, compiler_params=_cp("arbitrary"))(*args)


def _prev_halo(halo, width, col):
    per = ROWS // halo
    return pl.BlockSpec((halo, width), lambda i: (jnp.maximum(i * per - 1, 0), col))


def _next_halo(halo, width, col):
    per = ROWS // halo
    last = SEQ // halo - 1
    return pl.BlockSpec((halo, width), lambda i: (jnp.minimum((i + 1) * per, last), col))


CONV_PAD = ROWS + CONV_HALO


def _shift_copies(sh):
    for b in range(1, 8):
        sh[b, 0:CONV_PAD - 8, :] = sh[0, pl.ds(b, CONV_PAD - 8), :]


def _tap(sh, rs_start, offset):
    return sh[offset % 8, pl.ds(pl.multiple_of(rs_start + (offset // 8) * 8, 8), SUB), :]


def _conv_glu(av_ref, ag_ref, avh_ref, agh_ref, sh):
    i = pl.program_id(0)
    hv = avh_ref[...] * _sig(agh_ref[...])
    sh[0, 0:CONV_HALO, :] = jnp.where(i > 0, hv, 0.0)

    def glu(rs):
        sh[0, pl.ds(pl.multiple_of(rs.start + CONV_HALO, SUB), SUB), :] = av_ref[rs, :] * _sig(ag_ref[rs, :])

    _for_chunks(glu)
    _shift_copies(sh)


def _conv_norm(u1, lg_ref, lb_ref):
    mu = _mean(u1)
    cen = u1 - mu
    rs = lax.rsqrt(_mean(cen * cen) + EPS)
    z = cen * rs
    ln = z * lg_ref[...] + lb_ref[...]
    s = _sig(ln)
    return z, rs, ln, s, ln * s


def conv_module_fwd(proj, wc, bc, lg, lb, gco, name):
    def body(av_ref, ag_ref, avh_ref, agh_ref, wc_ref, bc_ref, lg_ref, lb_ref, gco_ref, out_ref, u1_ref, sh):
        _conv_glu(av_ref, ag_ref, avh_ref, agh_ref, sh)

        def conv(rs):
            u1 = jnp.broadcast_to(bc_ref[...], (SUB, D_CONV))
            for j in range(CONV_K):
                u1 = u1 + wc_ref[j:j + 1, :] * _tap(sh, rs.start, CONV_HALO - (CONV_K - 1) + j)
            u1_ref[rs, :] = u1

        _for_chunks(conv)
        _, _, _, _, u2 = _conv_norm(u1_ref[...], lg_ref, lb_ref)
        rc = lax.rsqrt(_mean(u2 * u2) + EPS)
        out_ref[...] = (u2 * rc * gco_ref[...]).astype(BF16)

    v = _vec_spec(D_CONV)
    return pl.pallas_call(
        body, out_shape=(jax.ShapeDtypeStruct((SEQ, D_CONV), BF16), jax.ShapeDtypeStruct((SEQ, D_CONV), F32)),
        grid=(SEQ // ROWS,),
        in_specs=[_row_spec(D_CONV, 0), _row_spec(D_CONV, 1), _prev_halo(CONV_HALO, D_CONV, 0),
                  _prev_halo(CONV_HALO, D_CONV, 1), _vec_spec(D_CONV, CONV_K), v, v, v, v],
        out_specs=(_row_spec(D_CONV), _row_spec(D_CONV)), scratch_shapes=[pltpu.VMEM((8, CONV_PAD, D_CONV), F32)],
        name=name, compiler_params=_cp("parallel"))(proj, proj, proj, proj, wc, bc, lg, lb, gco)


def conv_module_bwd_a(proj, u1, dmixed, lg, lb, gco, name):
    def body(av_ref, ag_ref, avh_ref, agh_ref, u1_ref, dm_ref, lg_ref, lb_ref, gco_ref,
             du1_ref, dgco_ref, dlg_ref, dlb_ref, dbc_ref, dwc_ref, sh, acc):
        first = pl.program_id(0) == 0
        _conv_glu(av_ref, ag_ref, avh_ref, agh_ref, sh)
        z, rs, ln, s, u2 = _conv_norm(u1_ref[...], lg_ref, lb_ref)
        rc = lax.rsqrt(_mean(u2 * u2) + EPS)
        xn = u2 * rc
        dm = dm_ref[...]
        _acc(dgco_ref, _rsum(dm * xn), first)
        dyn = dm * gco_ref[...]
        du2 = rc * (dyn - xn * _mean(dyn * xn))
        dln = du2 * (s * (1.0 + ln * (1.0 - s)))
        _acc(dlg_ref, _rsum(dln * z), first)
        _acc(dlb_ref, _rsum(dln), first)
        dz = dln * lg_ref[...]
        du1 = rs * (dz - _mean(dz) - z * _mean(dz * z))
        du1_ref[...] = du1
        _acc(dbc_ref, _rsum(du1), first)
        acc[...] = jnp.zeros_like(acc)

        def taps(rs):
            d = du1_ref[rs, :]
            for j in range(CONV_K):
                acc[j] += d * _tap(sh, rs.start, CONV_HALO - (CONV_K - 1) + j)

        _for_chunks(taps)

        @pl.when(first)
        def _():
            dwc_ref[...] = jnp.zeros_like(dwc_ref)

        for j in range(CONV_K):
            dwc_ref[j:j + 1, :] += _rsum(acc[j])

    v = _vec_spec(D_CONV)
    vec = jax.ShapeDtypeStruct((1, D_CONV), F32)
    return pl.pallas_call(
        body,
        out_shape=(jax.ShapeDtypeStruct((SEQ, D_CONV), F32), vec, vec, vec, vec, jax.ShapeDtypeStruct((CONV_K, D_CONV), F32)),
        grid=(SEQ // ROWS,),
        in_specs=[_row_spec(D_CONV, 0), _row_spec(D_CONV, 1), _prev_halo(CONV_HALO, D_CONV, 0),
                  _prev_halo(CONV_HALO, D_CONV, 1), _row_spec(D_CONV, 0), _row_spec(D_CONV, 0), v, v, v],
        out_specs=(_row_spec(D_CONV), v, v, v, v, _vec_spec(D_CONV, CONV_K)),
        scratch_shapes=[pltpu.VMEM((8, CONV_PAD, D_CONV), F32), pltpu.VMEM((CONV_K, SUB, D_CONV), F32)],
        name=name, compiler_params=_cp("arbitrary"))(proj, proj, proj, proj, u1, dmixed, lg, lb, gco)


def conv_module_bwd_b(proj, du1, wc, name):
    def body(av_ref, ag_ref, du1_ref, du1n_ref, wc_ref, out_ref, sh):
        i = pl.program_id(0)
        sh[0, 0:ROWS, :] = du1_ref[...]
        sh[0, ROWS:, :] = jnp.where(i < SEQ // ROWS - 1, du1n_ref[...], 0.0)
        _shift_copies(sh)

        def chunk(rs):
            du0 = jnp.zeros((SUB, D_CONV), F32)
            for j in range(CONV_K):
                du0 = du0 + wc_ref[j:j + 1, :] * _tap(sh, rs.start, CONV_K - 1 - j)
            sg = _sig(ag_ref[rs, :])
            out_ref[rs, 0:D_CONV] = (du0 * sg).astype(BF16)
            out_ref[rs, D_CONV:] = (du0 * av_ref[rs, :] * sg * (1.0 - sg)).astype(BF16)

        _for_chunks(chunk)

    return pl.pallas_call(
        body, out_shape=jax.ShapeDtypeStruct((SEQ, 2 * D_CONV), BF16), grid=(SEQ // ROWS,),
        in_specs=[_row_spec(D_CONV, 0), _row_spec(D_CONV, 1), _row_spec(D_CONV, 0), _next_halo(CONV_HALO, D_CONV, 0),
                  _vec_spec(D_CONV, CONV_K)],
        out_specs=_row_spec(2 * D_CONV), scratch_shapes=[pltpu.VMEM((8, CONV_PAD, D_CONV), F32)],
        name=name, compiler_params=_cp("parallel"))(proj, proj, du1, du1, wc)


def _attn_specs(sub_len, pairs):
    ng, width = 4 // pairs, 128 * pairs
    q = pl.BlockSpec((sub_len, width), lambda rho, g: (0, rho * 3 * ng + g))
    k = pl.BlockSpec((sub_len, width), lambda rho, g: (0, rho * 3 * ng + ng + g))
    v = pl.BlockSpec((sub_len, width), lambda rho, g: (0, rho * 3 * ng + 2 * ng + g))
    o = pl.BlockSpec((sub_len, width), lambda rho, g: (0, rho * ng + g))
    return q, k, v, o


ATTN_PAIRS = {1: 1, 4: 1, 16: 4}


def _attn_block(q_ref, k_ref, v_ref, n, hs, win):
    q0 = pl.multiple_of(n * ATTN_BLOCK, ATTN_BLOCK)
    k0 = pl.multiple_of(jnp.maximum(n - 1, 0) * ATTN_BLOCK, ATTN_BLOCK)
    qb = q_ref[pl.ds(q0, ATTN_BLOCK), hs]
    kw = k_ref[pl.ds(k0, win), hs]
    vw = v_ref[pl.ds(k0, win), hs]
    s = lax.dot_general(qb, kw, (((1,), (1,)), ((), ())), preferred_element_type=F32) * (HEAD_DIM ** -0.5)
    dist = (q0 - k0) + lax.broadcasted_iota(jnp.int32, (ATTN_BLOCK, win), 0) \
        - lax.broadcasted_iota(jnp.int32, (ATTN_BLOCK, win), 1)
    s = jnp.where((dist >= 0) & (dist <= ATTN_BLOCK), s, NEG)
    return q0, k0, qb, kw, vw, s


def attn_fwd(qkv_r, sub_len, r, name):
    nb = sub_len // ATTN_BLOCK
    win = 2 * ATTN_BLOCK if nb > 1 else ATTN_BLOCK
    pairs = ATTN_PAIRS[r]

    def body(q_ref, k_ref, v_ref, o_ref, l_ref):
        def block(n, carry):
            for h in range(2 * pairs):
                hs = slice(h * HEAD_DIM, (h + 1) * HEAD_DIM)
                q0, _, _, _, vw, s = _attn_block(q_ref, k_ref, v_ref, n, hs, win)
                m = jnp.max(s, axis=1, keepdims=True)
                p = jnp.exp(s - m)
                den = jnp.sum(p, axis=1, keepdims=True)
                o = jnp.dot(p.astype(BF16), vw, preferred_element_type=F32) / den
                o_ref[pl.ds(q0, ATTN_BLOCK), hs] = o
                l_ref[pl.ds(q0, ATTN_BLOCK), hs] = jnp.broadcast_to(m + jnp.log(den), (ATTN_BLOCK, HEAD_DIM))
            return carry

        lax.fori_loop(0, nb, block, 0, unroll=min(nb, 2))

    q, k, v, o = _attn_specs(sub_len, pairs)
    shp = jax.ShapeDtypeStruct((sub_len, r * D_ATTN), F32)
    return pl.pallas_call(
        body, out_shape=(shp, shp), grid=(r, 4 // pairs), in_specs=[q, k, v], out_specs=(o, o),
        name=name, compiler_params=_cp("parallel", "parallel"))(qkv_r, qkv_r, qkv_r)


def attn_bwd(qkv_r, do_r, lse_r, dd_r, sub_len, r, name):
    nb = sub_len // ATTN_BLOCK
    win = 2 * ATTN_BLOCK if nb > 1 else ATTN_BLOCK
    pairs = ATTN_PAIRS[r]

    def body(q_ref, k_ref, v_ref, do_ref, l_ref, dd_ref, dq_ref, dk_ref, dv_ref):
        dk_ref[...] = jnp.zeros_like(dk_ref)
        dv_ref[...] = jnp.zeros_like(dv_ref)

        def block(n, carry):
            for h in range(2 * pairs):
                hs = slice(h * HEAD_DIM, (h + 1) * HEAD_DIM)
                h1 = slice(h * HEAD_DIM, h * HEAD_DIM + 1)
                q0, k0, qb, kw, vw, s = _attn_block(q_ref, k_ref, v_ref, n, hs, win)
                dob = do_ref[pl.ds(q0, ATTN_BLOCK), hs]
                p = jnp.exp(s - l_ref[pl.ds(q0, ATTN_BLOCK), h1])
                dp = lax.dot_general(dob, vw, (((1,), (1,)), ((), ())), preferred_element_type=F32)
                ds = (p * (dp - dd_ref[pl.ds(q0, ATTN_BLOCK), h1]) * (HEAD_DIM ** -0.5)).astype(BF16)
                dq_ref[pl.ds(q0, ATTN_BLOCK), hs] = jnp.dot(ds, kw, preferred_element_type=F32)
                dk_ref[pl.ds(k0, win), hs] += lax.dot_general(ds, qb, (((0,), (0,)), ((), ())), preferred_element_type=F32)
                dv_ref[pl.ds(k0, win), hs] += lax.dot_general(p.astype(BF16), dob, (((0,), (0,)), ((), ())),
                                                              preferred_element_type=F32)
            return carry

        lax.fori_loop(0, nb, block, 0)

    q, k, v, o = _attn_specs(sub_len, pairs)
    shp = jax.ShapeDtypeStruct((sub_len, r * D_ATTN), F32)
    return pl.pallas_call(
        body, out_shape=(shp, shp, shp), grid=(r, 4 // pairs), in_specs=[q, k, v, o, o, o], out_specs=(o, o, o),
        name=name, compiler_params=_cp("parallel", "parallel"))(qkv_r, qkv_r, qkv_r, do_r, lse_r, dd_r)


def _rows(start, size, r):
    return pl.ds(start, size) if r == 1 else pl.ds(start, size, stride=r)


def _unit_rows(r, rho, n, nb):
    win = 2 * ATTN_BLOCK if nb > 1 else ATTN_BLOCK
    if isinstance(n, int):
        kb = max(n - 1, 0)
        q_rows = _rows(rho + r * ATTN_BLOCK * n, ATTN_BLOCK, r)
        k_rows = _rows(rho + r * ATTN_BLOCK * kb, win, r)
    else:
        kb = jnp.maximum(n - 1, 0)
        q_rows = pl.ds(pl.multiple_of(n * ATTN_BLOCK, ATTN_BLOCK), ATTN_BLOCK)
        k_rows = pl.ds(pl.multiple_of(kb * ATTN_BLOCK, ATTN_BLOCK), win)
    dist = (n - kb) * ATTN_BLOCK + lax.broadcasted_iota(jnp.int32, (ATTN_BLOCK, win), 0) \
        - lax.broadcasted_iota(jnp.int32, (ATTN_BLOCK, win), 1)
    return q_rows, k_rows, (dist >= 0) & (dist <= ATTN_BLOCK)


def _per_head(x):
    lane = lax.broadcasted_iota(jnp.int32, x.shape, 1)
    zero = jnp.zeros_like(x)
    return [jnp.where(lane < HEAD_DIM, x, zero), jnp.where(lane >= HEAD_DIM, x, zero)]


def _masked_scores(q2, k2, valid):
    return [jnp.where(valid, lax.dot_general(qh, k2, NT, preferred_element_type=F32) * (HEAD_DIM ** -0.5), NEG)
            for qh in _per_head(q2)]


def _attn_units(r, nb, unit):
    if r == 1:
        def four(i, carry):
            for k in range(4):
                unit(0, 4 * i + k)
            return carry
        lax.fori_loop(0, nb // 4, four, 0)
    else:
        for rho in range(r):
            for n in range(nb):
                unit(rho, n)


N_UNITS = 16


def attn_fwd_all(proj, name):
    def body(q_ref, k_ref, v_ref, att_ref, lse_ref, s_scr, p_scr, lse_scr, den_scr):
        for idx, (sub_len, r) in enumerate(PATTERNS):
            nb = sub_len // ATTN_BLOCK
            win = 2 * ATTN_BLOCK if nb > 1 else ATTN_BLOCK

            def scores(rho, n, r=r, nb=nb, win=win):
                u = rho * nb + n
                q_rows, k_rows, valid = _unit_rows(r, rho, n, nb)
                ss = _masked_scores(q_ref[q_rows, :].astype(BF16), k_ref[k_rows, :].astype(BF16), valid)
                for h in range(2):
                    s_scr[2 * u + h, :, 0:win] = ss[h]

            _attn_units(r, nb, scores)

            def softmax(u, carry, win=win):
                lses, dens = [], []
                for h in range(2):
                    sc = s_scr[2 * u + h, :, 0:win]
                    m = jnp.max(sc, axis=1, keepdims=True)
                    p = jnp.exp(sc - m)
                    den = jnp.sum(p, axis=1, keepdims=True)
                    p_scr[2 * u + h, :, 0:win] = p.astype(BF16)
                    lses.append(jnp.broadcast_to(m + jnp.log(den), (ATTN_BLOCK, HEAD_DIM)))
                    dens.append(jnp.broadcast_to(den, (ATTN_BLOCK, HEAD_DIM)))
                lse_scr[u] = jnp.concatenate(lses, axis=1)
                den_scr[u] = jnp.concatenate(dens, axis=1)
                return carry

            lax.fori_loop(0, N_UNITS, softmax, 0, unroll=2)

            def outputs(rho, n, r=r, nb=nb, win=win, idx=idx):
                u = rho * nb + n
                q_rows, k_rows, _ = _unit_rows(r, rho, n, nb)
                vs = _per_head(v_ref[k_rows, :].astype(BF16))
                o = (jnp.dot(p_scr[2 * u, :, 0:win], vs[0], preferred_element_type=F32)
                     + jnp.dot(p_scr[2 * u + 1, :, 0:win], vs[1], preferred_element_type=F32)) / den_scr[u]
                lse = lse_scr[u]
                if idx > 0:
                    old = lse_ref[q_rows, :]
                    top = jnp.maximum(old, lse)
                    new = top + jnp.log(jnp.exp(old - top) + jnp.exp(lse - top))
                    o = att_ref[q_rows, :] * jnp.exp(old - new) + o * jnp.exp(lse - new)
                    lse = new
                att_ref[q_rows, :] = o
                lse_ref[q_rows, :] = lse

            _attn_units(r, nb, outputs)

    blk = lambda first: pl.BlockSpec((SEQ, 128), lambda g: (0, first + g))
    shp = jax.ShapeDtypeStruct((SEQ, D_ATTN), F32)
    big = (2 * N_UNITS, ATTN_BLOCK, 2 * ATTN_BLOCK)
    small = pltpu.VMEM((N_UNITS, ATTN_BLOCK, 128), F32)
    return pl.pallas_call(
        body, out_shape=(shp, shp), grid=(4,), in_specs=[blk(8), blk(12), blk(16)], out_specs=(blk(0), blk(0)),
        scratch_shapes=[pltpu.VMEM(big, F32), pltpu.VMEM(big, BF16), small, small],
        name=name, compiler_params=_cp("parallel"))(proj, proj, proj)


def attn_bwd_all(proj, do, lse, dd, name):
    scale = HEAD_DIM ** -0.5

    def body(q_ref, k_ref, v_ref, do_ref, l_ref, dd_ref, out_ref, dq_s, dk_s, dv_s,
             s_scr, dp_scr, ds_scr, st_scr, dpt_scr, pt_scr, dst_scr, qb_scr, kb_scr, dob_scr):
        dq_s[...] = jnp.zeros_like(dq_s)
        dk_s[...] = jnp.zeros_like(dk_s)
        dv_s[...] = jnp.zeros_like(dv_s)
        for sub_len, r in PATTERNS:
            nb = sub_len // ATTN_BLOCK
            win = 2 * ATTN_BLOCK if nb > 1 else ATTN_BLOCK

            def scores(rho, n, r=r, nb=nb, win=win):
                u = rho * nb + n
                q_rows, k_rows, valid = _unit_rows(r, rho, n, nb)
                kb = max(n - 1, 0) if isinstance(n, int) else jnp.maximum(n - 1, 0)
                dist_t = (n - kb) * ATTN_BLOCK + lax.broadcasted_iota(jnp.int32, (win, ATTN_BLOCK), 1) \
                    - lax.broadcasted_iota(jnp.int32, (win, ATTN_BLOCK), 0)
                valid_t = (dist_t >= 0) & (dist_t <= ATTN_BLOCK)
                q2 = q_ref[q_rows, :].astype(BF16)
                k2 = k_ref[k_rows, :].astype(BF16)
                do2 = do_ref[q_rows, :].astype(BF16)
                qb_scr[u] = q2
                kb_scr[u, 0:win, :] = k2
                dob_scr[u] = do2
                l2 = l_ref[q_rows, :]
                d2 = dd_ref[q_rows, :]
                l2t = l2.T
                d2t = d2.T
                v2 = v_ref[k_rows, :].astype(BF16)
                qs, dos = _per_head(q2), _per_head(do2)
                for h in range(2):
                    c0 = h * HEAD_DIM
                    sc = lax.dot_general(qs[h], k2, NT, preferred_element_type=F32) * scale
                    s_scr[2 * u + h, :, 0:win] = jnp.where(valid, sc, NEG) - l2[:, c0:c0 + 1]
                    dp_scr[2 * u + h, :, 0:win] = lax.dot_general(dos[h], v2, NT, preferred_element_type=F32) \
                        - d2[:, c0:c0 + 1]
                    sct = lax.dot_general(k2, qs[h], NT, preferred_element_type=F32) * scale
                    st_scr[2 * u + h, 0:win, :] = jnp.where(valid_t, sct, NEG) - l2t[c0:c0 + 1, :]
                    dpt_scr[2 * u + h, 0:win, :] = lax.dot_general(v2, dos[h], NT, preferred_element_type=F32) \
                        - d2t[c0:c0 + 1, :]

            _attn_units(r, nb, scores)

            def pointwise(hu, carry, win=win):
                ds_scr[hu, :, 0:win] = (jnp.exp(s_scr[hu, :, 0:win]) * dp_scr[hu, :, 0:win] * scale).astype(BF16)
                pt = jnp.exp(st_scr[hu, 0:win, :])
                pt_scr[hu, 0:win, :] = pt.astype(BF16)
                dst_scr[hu, 0:win, :] = (pt * dpt_scr[hu, 0:win, :] * scale).astype(BF16)
                return carry

            lax.fori_loop(0, 2 * N_UNITS, pointwise, 0, unroll=4)

            def grads(rho, n, r=r, nb=nb, win=win):
                u = rho * nb + n
                q_rows, k_rows, _ = _unit_rows(r, rho, n, nb)
                qs, ks, dos = _per_head(qb_scr[u]), _per_head(kb_scr[u, 0:win, :]), _per_head(dob_scr[u])

                def both(scr, rows, rhs):
                    return (jnp.dot(scr[(2 * u,) + rows], rhs[0], preferred_element_type=F32)
                            + jnp.dot(scr[(2 * u + 1,) + rows], rhs[1], preferred_element_type=F32))

                dq_s[q_rows, :] += both(ds_scr, (slice(None), slice(0, win)), ks)
                dk_s[k_rows, :] += both(dst_scr, (slice(0, win), slice(None)), qs)
                dv_s[k_rows, :] += both(pt_scr, (slice(0, win), slice(None)), dos)

            _attn_units(r, nb, grads)
        out_ref[0] = dq_s[...].astype(BF16)
        out_ref[1] = dk_s[...].astype(BF16)
        out_ref[2] = dv_s[...].astype(BF16)

    blk = lambda first: pl.BlockSpec((SEQ, 128), lambda g: (0, first + g))
    acc = pltpu.VMEM((SEQ, 128), F32)
    big = (2 * N_UNITS, ATTN_BLOCK, 2 * ATTN_BLOCK)
    big_t = (2 * N_UNITS, 2 * ATTN_BLOCK, ATTN_BLOCK)
    return pl.pallas_call(
        body, out_shape=jax.ShapeDtypeStruct((3, SEQ, D_ATTN), BF16), grid=(4,),
        in_specs=[blk(8), blk(12), blk(16), blk(0), blk(0), blk(0)],
        out_specs=pl.BlockSpec((3, SEQ, 128), lambda g: (0, 0, g)),
        scratch_shapes=[acc, acc, acc, pltpu.VMEM(big, F32), pltpu.VMEM(big, F32), pltpu.VMEM(big, BF16),
                        pltpu.VMEM(big_t, F32), pltpu.VMEM(big_t, F32), pltpu.VMEM(big_t, BF16), pltpu.VMEM(big_t, BF16),
                        pltpu.VMEM((N_UNITS, ATTN_BLOCK, 128), BF16),
                        pltpu.VMEM((N_UNITS, 2 * ATTN_BLOCK, 128), BF16), pltpu.VMEM((N_UNITS, ATTN_BLOCK, 128), BF16)],
        name=name, compiler_params=_cp("parallel"))(proj, proj, proj, do, lse, dd)


def rms_gain_bf16(a, g, name):
    def body(a_ref, g_ref, o_ref):
        aa = a_ref[...]
        o_ref[...] = (aa * lax.rsqrt(_mean(aa * aa) + EPS) * g_ref[...]).astype(BF16)

    w = a.shape[1]
    return pl.pallas_call(
        body, out_shape=jax.ShapeDtypeStruct(a.shape, BF16), grid=(SEQ // ROWS,), in_specs=[_row_spec(w), _vec_spec(w)],
        out_specs=_row_spec(w), name=name, compiler_params=_cp("parallel"))(a, g)


def attn_combine_fwd(outs, lses, gao, name):
    def body(o1, o2, o3, l1, l2, l3, g_ref, att_ref, lse_ref, mix_ref):
        a1, a2, a3 = l1[...], l2[...], l3[...]
        m = jnp.maximum(jnp.maximum(a1, a2), a3)
        w1, w2, w3 = jnp.exp(a1 - m), jnp.exp(a2 - m), jnp.exp(a3 - m)
        den = w1 + w2 + w3
        att = (w1 * o1[...] + w2 * o2[...] + w3 * o3[...]) / den
        att_ref[...] = att
        lse_ref[...] = m + jnp.log(den)
        mix_ref[...] = (att * lax.rsqrt(_mean(att * att) + EPS) * g_ref[...]).astype(BF16)

    rs = _row_spec(D_ATTN)
    f = jax.ShapeDtypeStruct((SEQ, D_ATTN), F32)
    return pl.pallas_call(
        body, out_shape=(f, f, jax.ShapeDtypeStruct((SEQ, D_ATTN), BF16)), grid=(SEQ // ROWS,),
        in_specs=[rs] * 6 + [_vec_spec(D_ATTN)], out_specs=(rs, rs, rs),
        name=name, compiler_params=_cp("parallel"))(*outs, *lses, gao)


def attn_combine_bwd(dmixed, att, gao, name):
    def body(dm_ref, att_ref, g_ref, do_ref, dd_ref, dg_ref):
        first = pl.program_id(0) == 0
        att = att_ref[...]
        r = lax.rsqrt(_mean(att * att) + EPS)
        xn = att * r
        dm = dm_ref[...]
        _acc(dg_ref, _rsum(dm * xn), first)
        dyn = dm * g_ref[...]
        do = r * (dyn - xn * _mean(dyn * xn))
        do_ref[...] = do
        same_head = (jnp.right_shift(lax.broadcasted_iota(jnp.int32, (D_ATTN, D_ATTN), 0), 6)
                     == jnp.right_shift(lax.broadcasted_iota(jnp.int32, (D_ATTN, D_ATTN), 1), 6)).astype(F32)
        dd_ref[...] = jnp.dot(do * att, same_head, preferred_element_type=F32, precision=lax.Precision.HIGHEST)

    rs = _row_spec(D_ATTN)
    return pl.pallas_call(
        body,
        out_shape=(jax.ShapeDtypeStruct((SEQ, D_ATTN), F32), jax.ShapeDtypeStruct((SEQ, D_ATTN), F32),
                   jax.ShapeDtypeStruct((1, D_ATTN), F32)),
        grid=(SEQ // ROWS,), in_specs=[_row_spec(D_ATTN, 1), rs, _vec_spec(D_ATTN)],
        out_specs=(rs, rs, _vec_spec(D_ATTN)), name=name, compiler_params=_cp("arbitrary"))(dmixed, att, gao)


def sum3_bf16(a, b, c, name):
    def body(a_ref, b_ref, c_ref, o_ref):
        o_ref[...] = (a_ref[...] + b_ref[...] + c_ref[...]).astype(BF16)

    w = a.shape[1]
    rs = _row_spec(w)
    return pl.pallas_call(
        body, out_shape=jax.ShapeDtypeStruct(a.shape, BF16), grid=(SEQ // ROWS,), in_specs=[rs, rs, rs], out_specs=rs,
        name=name, compiler_params=_cp("parallel"))(a, b, c)


N_FT = D_FF // FFN_TN


def _ffn_specs():
    per = ROWS // FFN_HALO
    cur_g = pl.BlockSpec((ROWS, FFN_TN), lambda j, i: (i, j))
    cur_v = pl.BlockSpec((ROWS, FFN_TN), lambda j, i: (i, j + N_FT))
    halo_g = pl.BlockSpec((FFN_HALO, FFN_TN), lambda j, i: (jnp.maximum(i * per - 1, 0), j))
    halo_v = pl.BlockSpec((FFN_HALO, FFN_TN), lambda j, i: (jnp.maximum(i * per - 1, 0), j + N_FT))
    w_g = pl.BlockSpec((FFN_K, FFN_TN), lambda j, i: (0, j))
    w_v = pl.BlockSpec((FFN_K, FFN_TN), lambda j, i: (0, j + N_FT))
    b_g = pl.BlockSpec((1, FFN_TN), lambda j, i: (0, j))
    b_v = pl.BlockSpec((1, FFN_TN), lambda j, i: (0, j + N_FT))
    return [cur_g, cur_v, halo_g, halo_v, w_g, w_v, b_g, b_v]


def matmul(a, b, kind, out_dtype, tm, tn, name, b_rows=None):
    stacked = b_rows is not None
    b_shape = (N_DEV * b_rows, b.shape[2]) if stacked else b.shape
    if kind == "nn":
        (m, k), n = a.shape, b_shape[1]
        a_spec = pl.BlockSpec((tm, k), lambda j, i: (i, 0))
        b_spec = pl.BlockSpec((k, tn), lambda j, i: (0, j))
        dims = (((1,), (0,)), ((), ()))
    elif kind == "nt":
        (m, k), n = a.shape, b_shape[0]
        a_spec = pl.BlockSpec((tm, k), lambda j, i: (i, 0))
        b_spec = pl.BlockSpec((tn, k), lambda j, i: (j, 0))
        dims = (((1,), (1,)), ((), ()))
    else:
        (k, m), n = a.shape, b_shape[1]
        a_spec = pl.BlockSpec((k, tm), lambda j, i: (0, i))
        b_spec = pl.BlockSpec((k, tn), lambda j, i: (0, j))
        dims = (((0,), (0,)), ((), ()))
    assert m % tm == 0 and n % tn == 0, (name, m, n, tm, tn)
    if stacked:
        assert b_spec.block_shape[0] == b_shape[0] and kind in ("nn", "nt")
        width = b_spec.block_shape[1]
        b_spec = pl.BlockSpec((N_DEV, b_rows, width), (lambda j, i: (0, 0, j)) if kind == "nn" else (lambda j, i: (0, 0, 0)))

    def body(a_ref, b_ref, o_ref):
        bb = b_ref[...].reshape(b_shape[0], -1) if stacked else b_ref[...]
        o_ref[...] = lax.dot_general(a_ref[...], bb, dims, preferred_element_type=F32).astype(o_ref.dtype)

    return pl.pallas_call(
        body, out_shape=jax.ShapeDtypeStruct((m, n), out_dtype), grid=(n // tn, m // tm),
        in_specs=[a_spec, b_spec], out_specs=pl.BlockSpec((tm, tn), lambda j, i: (i, j)),
        name=name, compiler_params=_cp("parallel", "parallel"))(a, b)


def matmul_halves(a, b, tm, tn, name):
    _, m, k = a.shape
    n = b.shape[1]
    b3 = b.reshape(2, k, n)

    def body(a_ref, b_ref, o_ref):
        o_ref[...] = (jnp.dot(a_ref[0], b_ref[0], preferred_element_type=F32)
                      + jnp.dot(a_ref[1], b_ref[1], preferred_element_type=F32))

    return pl.pallas_call(
        body, out_shape=jax.ShapeDtypeStruct((m, n), F32), grid=(n // tn, m // tm),
        in_specs=[pl.BlockSpec((2, tm, k), lambda j, i: (0, i, 0)), pl.BlockSpec((2, k, tn), lambda j, i: (0, 0, j))],
        out_specs=pl.BlockSpec((tm, tn), lambda j, i: (i, j)), name=name, compiler_params=_cp("parallel", "parallel"))(a, b3)


def matmul_tn_halves(a, b, tm, name):
    _, k, m = a.shape
    n = b.shape[1]

    def body(a_ref, b_ref, o_ref):
        o_ref[0] = lax.dot_general(a_ref[0], b_ref[...], (((0,), (0,)), ((), ())), preferred_element_type=F32).astype(BF16)

    return pl.pallas_call(
        body, out_shape=jax.ShapeDtypeStruct((2, m, n), BF16), grid=(2, m // tm),
        in_specs=[pl.BlockSpec((1, k, tm), lambda h, i: (h, 0, i)), pl.BlockSpec((k, n), lambda h, i: (0, 0))],
        out_specs=pl.BlockSpec((1, tm, n), lambda h, i: (h, i, 0)), name=name,
        compiler_params=_cp("parallel", "parallel"))(a, b).reshape(2 * m, n)


def _row_spec(width, col=0):
    return pl.BlockSpec((ROWS, width), lambda i: (i, col))


def _vec_spec(width, rows=1):
    return pl.BlockSpec((rows, width), lambda i: (0, 0))


def _ffn_shifted(cur_ref, halo_ref, pad, s1, s2):
    i = pl.program_id(1)
    pad[0:FFN_HALO, :] = jnp.where(i > 0, halo_ref[...], 0.0)
    pad[FFN_HALO:, :] = cur_ref[0:FFN_HALO, :]
    for k, dst in ((1, s1), (2, s2)):
        dst[0:FFN_HALO, :] = pad[pl.ds(FFN_HALO - k, FFN_HALO), :]
        dst[FFN_HALO:, :] = cur_ref[pl.ds(FFN_HALO - k, ROWS - FFN_HALO), :]


def _ffn_conv(rs, cur_ref, s1, s2, w_ref, b_ref):
    return b_ref[...] + w_ref[0:1, :] * s2[rs, :] + w_ref[1:2, :] * s1[rs, :] + w_ref[2:3, :] * cur_ref[rs, :]


def ffn_act_fwd(up0, wf, bf, name):
    def body(g_ref, v_ref, gh_ref, vh_ref, wg_ref, wv_ref, bg_ref, bv_ref, act_ref, pad, g1, g2, v1, v2):
        _ffn_shifted(g_ref, gh_ref, pad, g1, g2)
        _ffn_shifted(v_ref, vh_ref, pad, v1, v2)

        def chunk(rs):
            gate = _ffn_conv(rs, g_ref, g1, g2, wg_ref, bg_ref)
            val = _ffn_conv(rs, v_ref, v1, v2, wv_ref, bv_ref)
            act_ref[rs, :] = (gate * _sig(gate) * val).astype(BF16)

        _for_chunks(chunk)

    tile = pltpu.VMEM((ROWS, FFN_TN), F32)
    return pl.pallas_call(
        body, out_shape=jax.ShapeDtypeStruct((SEQ, D_FF), BF16), grid=(N_FT, SEQ // ROWS),
        in_specs=_ffn_specs(), out_specs=pl.BlockSpec((ROWS, FFN_TN), lambda j, i: (i, j)),
        scratch_shapes=[pltpu.VMEM((2 * FFN_HALO, FFN_TN), F32), tile, tile, tile, tile],
        name=name, compiler_params=_cp("parallel", "parallel"))(up0, up0, up0, up0, wf, wf, bf, bf)


def ffn_bwd(up0, dact, wf, bf, name):
    per = ROWS // FFN_HALO
    last = SEQ // FFN_HALO - 1

    def body(g_ref, v_ref, gh_ref, vh_ref, wg_ref, wv_ref, bg_ref, bv_ref, da_ref, gn_ref, vn_ref, dan_ref,
             out_ref, dbg_ref, dbv_ref, dwg_ref, dwv_ref, pad, g1, g2, v1, v2, dgp, dvp, acc):
        i = pl.program_id(1)
        first = i == 0
        _ffn_shifted(g_ref, gh_ref, pad, g1, g2)
        _ffn_shifted(v_ref, vh_ref, pad, v1, v2)
        acc[...] = jnp.zeros_like(acc)

        def grads(gate, val, da):
            s = _sig(gate)
            return da * val * (s * (1.0 + gate * (1.0 - s))), da * (gate * s)

        def chunk(rs):
            gate = _ffn_conv(rs, g_ref, g1, g2, wg_ref, bg_ref)
            val = _ffn_conv(rs, v_ref, v1, v2, wv_ref, bv_ref)
            dgate, dval = grads(gate, val, da_ref[rs, :])
            dgp[rs, :] = dgate
            dvp[rs, :] = dval
            acc[0] += dgate
            acc[1] += dval
            for t, (sg, sv) in enumerate(((g2, v2), (g1, v1), (g_ref, v_ref))):
                acc[2 + t] += dgate * sg[rs, :]
                acc[5 + t] += dval * sv[rs, :]

        _for_chunks(chunk)
        _acc(dbg_ref, _rsum(acc[0]), first)
        _acc(dbv_ref, _rsum(acc[1]), first)
        _acc(dwg_ref, jnp.concatenate([_rsum(acc[2 + t]) for t in range(FFN_K)], axis=0), first)
        _acc(dwv_ref, jnp.concatenate([_rsum(acc[5 + t]) for t in range(FFN_K)], axis=0), first)

        def conv_next(cur_ref, nxt_ref, w_ref, b_ref):
            pad[0:FFN_HALO, :] = cur_ref[ROWS - FFN_HALO:, :]
            pad[FFN_HALO:, :] = nxt_ref[...]
            return (b_ref[...] + w_ref[0:1, :] * pad[pl.ds(FFN_HALO - 2, FFN_HALO), :]
                    + w_ref[1:2, :] * pad[pl.ds(FFN_HALO - 1, FFN_HALO), :] + w_ref[2:3, :] * nxt_ref[...])

        gate_n = conv_next(g_ref, gn_ref, wg_ref, bg_ref)
        val_n = conv_next(v_ref, vn_ref, wv_ref, bv_ref)
        dgate_n, dval_n = grads(gate_n, val_n, dan_ref[...])
        inside = i < SEQ // ROWS - 1
        dgp[ROWS:, :] = jnp.where(inside, dgate_n, 0.0)
        dvp[ROWS:, :] = jnp.where(inside, dval_n, 0.0)

        for half, (dp, s1, s2, w_ref) in enumerate(((dgp, g1, g2, wg_ref), (dvp, v1, v2, wv_ref))):
            s1[...] = dp[pl.ds(1, ROWS), :]
            s2[...] = dp[pl.ds(2, ROWS), :]

            def back(rs, dp=dp, s1=s1, s2=s2, w_ref=w_ref, half=half):
                out_ref[half, rs, :] = (w_ref[2:3, :] * dp[rs, :] + w_ref[1:2, :] * s1[rs, :]
                                        + w_ref[0:1, :] * s2[rs, :]).astype(BF16)

            _for_chunks(back)

    tile = pltpu.VMEM((ROWS, FFN_TN), F32)
    ext = pltpu.VMEM((ROWS + FFN_HALO, FFN_TN), F32)
    vec = jax.ShapeDtypeStruct((1, D_FF), F32)
    taps = jax.ShapeDtypeStruct((FFN_K, D_FF), F32)
    cur = pl.BlockSpec((ROWS, FFN_TN), lambda j, i: (i, j))
    nxt = lambda off: pl.BlockSpec((FFN_HALO, FFN_TN), lambda j, i: (jnp.minimum((i + 1) * per, last), j + off))
    vs = pl.BlockSpec((1, FFN_TN), lambda j, i: (0, j))
    ts = pl.BlockSpec((FFN_K, FFN_TN), lambda j, i: (0, j))
    return pl.pallas_call(
        body, out_shape=(jax.ShapeDtypeStruct((2, SEQ, D_FF), BF16), vec, vec, taps, taps), grid=(N_FT, SEQ // ROWS),
        in_specs=_ffn_specs() + [cur, nxt(0), nxt(N_FT), nxt(0)],
        out_specs=(pl.BlockSpec((2, ROWS, FFN_TN), lambda j, i: (0, i, j)), vs, vs, ts, ts),
        scratch_shapes=[pltpu.VMEM((2 * FFN_HALO, FFN_TN), F32), tile, tile, tile, tile, ext, ext,
                        pltpu.VMEM((2 + 2 * FFN_K, SUB, FFN_TN), F32)],
        name=name, compiler_params=_cp("parallel", "arbitrary"))(up0, up0, up0, up0, wf, wf, bf, bf, dact, up0, up0, dact)


def ada_fwd(c_all, w_ada, b_cols, name):
    def body(c_ref, w_ref, b_ref, o_ref):
        cc = c_ref[...]
        sc = (cc * _sig(cc)).astype(BF16)
        o_ref[...] = jnp.dot(sc, w_ref[...].astype(BF16), preferred_element_type=F32) + b_ref[...]

    return pl.pallas_call(body, out_shape=jax.ShapeDtypeStruct((N_DEV, w_ada.shape[1]), F32), name=name,
                          compiler_params=_cp())(c_all, w_ada, b_cols)


def _adam(w, g, m, v):
    m = ADAM_B1 * m + (1.0 - ADAM_B1) * g
    v = ADAM_B2 * v + (1.0 - ADAM_B2) * (g * g)
    m_hat = m / (1.0 - ADAM_B1 ** ADAM_STEP)
    v_hat = v / (1.0 - ADAM_B2 ** ADAM_STEP)
    delta = -ADAM_LR * (m_hat / (jnp.sqrt(v_hat) + ADAM_EPS) + ADAM_WD * w)
    return delta, m, v


def ada_bwd_adamw(c_all_t, dmod_cols, w, m, v, name):
    rows, cols = w.shape
    tr = 256

    def body(ct_ref, dm_ref, w_ref, m_ref, v_ref, g_ref, d_ref, nm_ref, nv_ref):
        def chunk(rs):
            ct = ct_ref[rs, :]
            sc = ct * _sig(ct)
            g = sc[:, 0:1] * dm_ref[0:1, :]
            for b in range(1, N_DEV):
                g = g + sc[:, b:b + 1] * dm_ref[b:b + 1, :]
            g_ref[rs, :] = g
            d_ref[rs, :], nm_ref[rs, :], nv_ref[rs, :] = _adam(w_ref[rs, :], g, m_ref[rs, :], v_ref[rs, :])

        _for_chunks(chunk, 2, tr)

    blk = pl.BlockSpec((tr, cols), lambda i: (i, 0))
    shp = jax.ShapeDtypeStruct((rows, cols), F32)
    return pl.pallas_call(
        body, out_shape=(shp, shp, shp, shp), grid=(rows // tr,),
        in_specs=[pl.BlockSpec((tr, N_DEV), lambda i: (i, 0)), pl.BlockSpec((N_DEV, cols), lambda i: (0, 0)), blk, blk, blk],
        out_specs=(blk, blk, blk, blk), name=name, compiler_params=_cp("parallel"))(c_all_t, dmod_cols, w, m, v)


def sum_adamw(parts, own, w, m, v, tr, name):
    n_parts, rows, cols = parts.shape

    def body(*refs):
        if own is None:
            p_ref, w_ref, m_ref, v_ref, g_ref, d_ref, nm_ref, nv_ref = refs
        else:
            p_ref, own_ref, w_ref, m_ref, v_ref, g_ref, d_ref, nm_ref, nv_ref = refs

        def chunk(rs):
            g = (p_ref[0, rs, :] if own is None else own_ref[rs, :]).astype(F32)
            for k in range(1, n_parts):
                g = g + p_ref[k, rs, :].astype(F32)
            g_ref[rs, :] = g
            d_ref[rs, :], nm_ref[rs, :], nv_ref[rs, :] = _adam(w_ref[rs, :], g, m_ref[rs, :], v_ref[rs, :])

        _for_chunks(chunk, 2 if tr > SUB else 1, tr)

    blk = pl.BlockSpec((tr, cols), lambda i: (i, 0))
    shp = jax.ShapeDtypeStruct((rows, cols), F32)
    args = [parts] + ([] if own is None else [own]) + [w, m, v]
    return pl.pallas_call(
        body, out_shape=(shp, shp, shp, shp), grid=(rows // tr,),
        in_specs=[pl.BlockSpec((n_parts, tr, cols), lambda i: (0, i, 0))] + [blk] * (len(args) - 1),
        out_specs=(blk, blk, blk, blk), name=name, compiler_params=_cp("parallel"))(*args)


MESH = pl.DeviceIdType.MESH
ANY = pl.BlockSpec(memory_space=pl.ANY)


def all_gather(block, name, after=None):
    extra = () if after is None else (after,)

    def body(x_ref, *refs):
        out_ref, send_sems, recv_sems, local_sem = refs[len(extra):]
        x, y, c = lax.axis_index("x"), lax.axis_index("y"), lax.axis_index("c")
        me, sibling = (x, y, c), (x, y, 1 - c)
        chips = [(1 - x, y), (x, 1 - y), (1 - x, 1 - y)]

        def slot(px, py, pc):
            return out_ref.at[4 * px + 2 * py + pc]

        def copy(k, blk, to, src=None):
            return pltpu.make_async_remote_copy(
                src_ref=slot(*blk) if src is None else src, dst_ref=slot(*blk),
                send_sem=send_sems.at[k], recv_sem=recv_sems.at[k], device_id=to, device_id_type=MESH)

        mine = pltpu.make_async_copy(x_ref, slot(*me), local_sem)
        mine.start()
        first = [copy(0, me, sibling, src=x_ref)]
        first += [copy(1 + j, me, (*chip, c), src=x_ref) for j, chip in enumerate(chips)]
        for cp in first:
            cp.start()
        passed = [copy(4 + j, (*chip, c), sibling) for j, chip in enumerate(chips)]
        for j, chip in enumerate(chips):
            copy(1 + j, (*chip, c), me).wait_recv()
            passed[j].start()
        copy(0, sibling, me).wait_recv()
        for j, chip in enumerate(chips):
            copy(4 + j, (*chip, 1 - c), me).wait_recv()
        for cp in first + passed:
            cp.wait_send()
        mine.wait()

    return pl.pallas_call(
        body, out_shape=jax.ShapeDtypeStruct((N_DEV,) + block.shape, block.dtype), in_specs=[ANY] * (1 + len(extra)), out_specs=ANY,
        scratch_shapes=[pltpu.SemaphoreType.DMA((7,)), pltpu.SemaphoreType.DMA((7,)), pltpu.SemaphoreType.DMA],
        name=name)(block, *extra)


HBM = pl.BlockSpec(memory_space=pltpu.HBM)
SEM = pl.BlockSpec(memory_space=pltpu.SEMAPHORE)
EFFECT = pltpu.SideEffectType.DATAFLOW_SIDE_EFFECTING


def _peer_copies(src_ref, land_ref, send_sems, recv_sems, gather):
    x, y, c = lax.axis_index("x"), lax.axis_index("y"), lax.axis_index("c")
    me = 4 * x + 2 * y + c
    copies = []
    for k in range(1, N_DEV):
        px = 1 - x if k & 4 else x
        py = 1 - y if k & 2 else y
        pc = 1 - c if k & 1 else c
        copies.append(pltpu.make_async_remote_copy(
            src_ref=src_ref if gather else src_ref.at[4 * px + 2 * py + pc],
            dst_ref=land_ref.at[me] if gather else land_ref.at[k],
            send_sem=send_sems.at[k - 1], recv_sem=recv_sems.at[k - 1], device_id=(px, py, pc), device_id_type=MESH))
    return copies


def exchange_start(srcs, gather, name, after=None):
    n = len(srcs)
    land_shapes = [(N_DEV,) + src.shape if gather else src.shape for src in srcs]
    extra = () if after is None else (after,)

    def body(*refs):
        src_refs, land_refs = refs[0:n], refs[n:2 * n]
        outs = refs[2 * n + len(extra):]
        for k in range(n):
            for cp in _peer_copies(src_refs[k], land_refs[k], outs[4 * k], outs[4 * k + 1], gather):
                cp.start()
        token = outs[4 * n]
        token[...] = jnp.zeros_like(token)

    out_shape, out_specs, aliases = [], [], {}
    for k, src in enumerate(srcs):
        out_shape += [pltpu.SemaphoreType.DMA((N_DEV - 1,)), pltpu.SemaphoreType.DMA((N_DEV - 1,)),
                      pltpu.HBM(src.shape, src.dtype), pltpu.HBM(land_shapes[k], src.dtype)]
        out_specs += [SEM, SEM, HBM, HBM]
        aliases[k] = 4 * k + 2
        aliases[n + k] = 4 * k + 3
    out_shape.append(jax.ShapeDtypeStruct((8, 128), F32))
    out_specs.append(pl.BlockSpec(memory_space=pltpu.VMEM))
    res = pl.pallas_call(
        body, name=name, out_shape=tuple(out_shape), in_specs=(HBM,) * (2 * n) + (ANY,) * len(extra),
        out_specs=tuple(out_specs), input_output_aliases=aliases,
        compiler_params=pltpu.CompilerParams(has_side_effects=EFFECT),
    )(*[pltpu.with_memory_space_constraint(src, pltpu.HBM) for src in srcs],
      *[pltpu.with_memory_space_constraint(lax.empty(shp, src.dtype), pltpu.HBM) for shp, src in zip(land_shapes, srcs)],
      *extra)
    return [tuple(res[4 * k:4 * k + 4]) for k in range(n)], res[4 * n][0, 0]


def exchange_wait(handles, after, gather, name):
    send_sems, recv_sems, src_thru, land_thru = handles

    def body(src_ref, land_ref, send_sems, recv_sems, after_ref, src_dead, got_ref):
        for cp in _peer_copies(src_ref, land_ref, send_sems, recv_sems, gather):
            cp.wait_send()
            cp.wait_recv()

    return pl.pallas_call(
        body, name=name,
        out_shape=(pltpu.HBM(src_thru.shape, src_thru.dtype), pltpu.HBM(land_thru.shape, land_thru.dtype)),
        in_specs=(HBM, HBM, SEM, SEM, ANY), out_specs=(HBM, HBM), input_output_aliases={0: 0, 1: 1},
        compiler_params=pltpu.CompilerParams(has_side_effects=EFFECT),
    )(src_thru, land_thru, send_sems, recv_sems, after)[1]


def _to_pattern(a, r):
    return a.reshape(SEQ // r, r * a.shape[1])


def local_step(x, tgt, mod, get_w, put_grad, wc, wf, g_mix, bc, lg, lb, gco, gao, g_ffn, bf, g_fin):
    w_in = get_w("w_in", mod)
    proj, h1 = rms_mod_matmul(x, g_mix, mod, 0, 1, w_in, D_IN // N_DEV, "proj_fwd")
    mix_a, u1 = conv_module_fwd(proj, wc, bc, lg, lb, gco, "conv_module_fwd")
    att, lse = attn_fwd_all(proj, "attn_fwd")
    w_out = get_w("w_out", att)
    y1, mixed = norm_concat_matmul(mix_a, att, gao, w_out, "out_proj_fwd")
    w_up = get_w("w_up", y1)
    up0, x1, h2 = resid_rms_mod_matmul(x, y1, g_ffn, mod, 2, 3, 4, w_up, "up_fwd")
    act = ffn_act_fwd(up0, wf, bf, "ffn_act_fwd")
    w_down = get_w("w_down", act)
    loss_t, dx2, dy2, d_gfin, d_gaf = matmul_loss_bwd(act, w_down, x1, tgt, g_fin, mod, 5, "down_fwd_loss")
    dact = matmul(dy2, w_down, "nt", F32, 512, FFN_TN, "down_bwd_x")
    dw_down = matmul(act, dy2, "tn", BF16, 256, D_MODEL, "down_bwd_w")
    dup0, dbf_g, dbf_v, dwf_g, dwf_v = ffn_bwd(up0, dact, wf + put_grad("w_down", dw_down), bf, "ffn_bwd")
    dw_up = matmul_tn_halves(dup0, h2, 256, "up_bwd_w")
    dx1, d_shf, d_scf, d_gffn, dy1, d_gam = matmul_rms_mod_bwd(
        dup0, w_up, x1, dx2, g_ffn + put_grad("w_up", dw_up), mod, 4, y1, 2, "up_bwd_x")
    dw_out = matmul(mixed, dy1, "tn", BF16, 256, D_MODEL, "out_proj_bwd_w")
    dmixed, do, dd, d_gao = matmul_combine_bwd(dy1, w_out, att, gao + put_grad("w_out", dw_out), "out_proj_bwd_x")
    dqkv = attn_bwd_all(proj, do, lse, dd, "attn_bwd")
    du1, d_gco, d_lg, d_lb, d_bc, d_wc = conv_module_bwd_a(proj, u1, dmixed, lg, lb, gco, "conv_module_bwd_a")
    dproj_a = conv_module_bwd_b(proj, du1, wc, "conv_module_bwd_b")
    dproj = jnp.concatenate([dproj_a, dqkv[0], dqkv[1], dqkv[2]], axis=1)
    dw_in = matmul(dproj, h1, "tn", BF16, 512, D_MODEL, "proj_bwd_w")
    dx, d_shm, d_scm, d_gmix = matmul_rms_mod_bwd(
        dproj, w_in, x, dx1, g_mix + put_grad("w_in", dw_in), mod, 1, None, 0, "proj_bwd_x", b_rows=D_IN // N_DEV)
    dmod = jnp.concatenate([d_shm, d_scm, d_gam, d_shf, d_scf, d_gaf], axis=1)
    small = dict(g_norm_mix=d_gmix, b_conv_dw=d_bc, ln_conv_g=d_lg, ln_conv_b=d_lb, g_conv_out=d_gco, g_attn_out=d_gao,
                 g_norm_ffn=d_gffn, b_ffn_dw=jnp.concatenate([dbf_g, dbf_v], axis=1), g_final=d_gfin,
                 w_conv_dw=d_wc, w_ffn_dw=jnp.concatenate([dwf_g, dwf_v], axis=1), dmod=dmod, loss=loss_t[0:1, 0:1])
    return dx, small


def _padw(a, width):
    return jnp.pad(a, ((0, 0), (0, width - a.shape[1])))


def pack_small(t):
    wide = jnp.concatenate([_padw(t["dmod"], PACK_W), _padw(t["b_ffn_dw"], PACK_W), _padw(t["w_ffn_dw"], PACK_W),
                            _padw(t["loss"], PACK_W), jnp.zeros((2, PACK_W), F32)], axis=0)
    z512 = jnp.zeros((1, 512), F32)
    narrow = jnp.concatenate([
        t["g_norm_mix"], t["g_norm_ffn"], t["g_final"],
        jnp.concatenate([t["b_conv_dw"], t["ln_conv_g"]], axis=1),
        jnp.concatenate([t["ln_conv_b"], t["g_conv_out"]], axis=1),
        jnp.concatenate([t["g_attn_out"], z512], axis=1),
        jnp.zeros((2, 1024), F32),
        jnp.pad(t["w_conv_dw"], ((0, 1), (0, 0))).reshape(16, 1024)], axis=0)
    return jnp.concatenate([wide, narrow.reshape(4, PACK_W), jnp.zeros((4, PACK_W), F32)], axis=0)


_NARROW = lambda k, off=0: (8 + k // 6, (k % 6) * 1024 + off)
PACKED_AT = dict(
    b_ada=(0, 0, N_MOD * D_MODEL), b_ffn_dw=(1, 0, 2 * D_FF),
    g_norm_mix=_NARROW(0) + (D_MODEL,), g_norm_ffn=_NARROW(1) + (D_MODEL,), g_final=_NARROW(2) + (D_MODEL,),
    b_conv_dw=_NARROW(3) + (D_CONV,), ln_conv_g=_NARROW(3, 512) + (D_CONV,),
    ln_conv_b=_NARROW(4) + (D_CONV,), g_conv_out=_NARROW(4, 512) + (D_CONV,), g_attn_out=_NARROW(5) + (D_ATTN,))
SMALL_ORDER = list(PACKED_AT)


def small_adamw(parts, wmv, name):
    def body(*refs):
        p_ref = refs[0]
        ins = refs[1:1 + 3 * len(SMALL_ORDER)]
        outs = refs[1 + 3 * len(SMALL_ORDER):]
        g = p_ref[0]
        for k in range(1, N_DEV):
            g = g + p_ref[k]
        for i, n in enumerate(SMALL_ORDER):
            row, lane, width = PACKED_AT[n]
            gp = g[row:row + 1, lane:lane + width]
            w_ref, m_ref, v_ref = ins[3 * i:3 * i + 3]
            g_ref, d_ref, nm_ref, nv_ref = outs[4 * i:4 * i + 4]
            g_ref[...] = gp
            d_ref[...], nm_ref[...], nv_ref[...] = _adam(w_ref[...], gp, m_ref[...], v_ref[...])
        wc_ref, wf_ref, loss_ref = outs[4 * len(SMALL_ORDER):]
        for j in range(CONV_K):
            row, lane = _NARROW(8 + j // 2, (j % 2) * 512)
            wc_ref[j:j + 1, :] = g[row:row + 1, lane:lane + D_CONV]
        wf_ref[...] = g[2:2 + FFN_K, 0:2 * D_FF]
        loss_ref[...] = jnp.broadcast_to(g[5:6, 0:1], (8, 128))

    args, out_shape = [parts], []
    for n in SMALL_ORDER:
        args += list(wmv[n])
        out_shape += [jax.ShapeDtypeStruct(wmv[n][0].shape, F32)] * 4
    out_shape += [jax.ShapeDtypeStruct((CONV_K, D_CONV), F32), jax.ShapeDtypeStruct((FFN_K, 2 * D_FF), F32),
                  jax.ShapeDtypeStruct((8, 128), F32)]
    res = pl.pallas_call(body, out_shape=tuple(out_shape), name=name, compiler_params=_cp())(*args)
    per = {n: tuple(res[4 * i:4 * i + 4]) for i, n in enumerate(SMALL_ORDER)}
    return per, res[-3], res[-2], res[-1][0, 0]


def shard_adamw(items, name):
    def body(*refs):
        ins, outs = refs[:4 * len(items)], refs[4 * len(items):]
        for i in range(len(items)):
            g_ref, w_ref, m_ref, v_ref = ins[4 * i:4 * i + 4]
            og_ref, d_ref, nm_ref, nv_ref = outs[4 * i:4 * i + 4]
            og_ref[...] = g_ref[...]
            d_ref[...], nm_ref[...], nv_ref[...] = _adam(w_ref[...], g_ref[...], m_ref[...], v_ref[...])

    args = [a for item in items for a in item]
    out_shape = tuple(jax.ShapeDtypeStruct(item[1].shape, F32) for item in items for _ in range(4))
    res = pl.pallas_call(body, out_shape=out_shape, name=name, compiler_params=_cp())(*args)
    return [tuple(res[4 * i:4 * i + 4]) for i in range(len(items))]


def _shard(full, n_cols, me):
    return lax.dynamic_slice(full, (0, me * n_cols), (full.shape[0], n_cols))


WEIGHTS = ["w_ada", "b_ada", "g_norm_mix", "w_in", "w_conv_dw", "b_conv_dw", "ln_conv_g", "ln_conv_b", "g_conv_out",
           "g_attn_out", "w_out", "g_norm_ffn", "w_up", "w_ffn_dw", "b_ffn_dw", "w_down", "g_final"]
SMALL_REPLICATED = ["g_norm_mix", "b_conv_dw", "ln_conv_g", "ln_conv_b", "g_conv_out", "g_attn_out", "g_norm_ffn",
                    "b_ffn_dw", "g_final"]


def kernel(x, c, w_ada, b_ada, g_norm_mix, w_in, w_conv_dw, b_conv_dw, ln_conv_g, ln_conv_b, g_conv_out, g_attn_out, w_out, g_norm_ffn, w_up, w_ffn_dw, b_ffn_dw, w_down, g_final, loss_target, m_w_ada, m_b_ada, m_g_norm_mix, m_w_in, m_w_conv_dw, m_b_conv_dw, m_ln_conv_g, m_ln_conv_b, m_g_conv_out, m_g_attn_out, m_w_out, m_g_norm_ffn, m_w_up, m_w_ffn_dw, m_b_ffn_dw, m_w_down, m_g_final, v_w_ada, v_b_ada, v_g_norm_mix, v_w_in, v_w_conv_dw, v_b_conv_dw, v_ln_conv_g, v_ln_conv_b, v_g_conv_out, v_g_attn_out, v_w_out, v_g_norm_ffn, v_w_up, v_w_ffn_dw, v_b_ffn_dw, v_w_down, v_g_final):
    args = dict(locals())
    me = 4 * lax.axis_index("x") + 2 * lax.axis_index("y") + lax.axis_index("c")

    def flat(name, prefix=""):
        a = args[prefix + name]
        return a.reshape(a.shape[-2] if a.ndim > 1 else 1, a.shape[-1])

    def flat_t(name, prefix=""):
        return args[prefix + name][0].T

    n_in, n_up, r_out, r_down = w_in.shape[2], w_up.shape[2], w_out.shape[1], w_down.shape[1]
    n_ada, n_wc, n_wf = w_ada.shape[2], w_conv_dw.shape[2], w_ffn_dw.shape[2]
    taps_c = jnp.pad(flat("w_conv_dw").reshape(1, CONV_K * n_wc), ((0, 0), (0, 2 * D_MODEL - CONV_K * n_wc)))
    taps_f = jnp.pad(flat("w_ffn_dw").reshape(1, FFN_K * n_wf), ((0, 0), (0, 3 * D_MODEL - FFN_K * n_wf)))
    first = jnp.concatenate([c, taps_c.reshape(2, D_MODEL), taps_f.reshape(3, D_MODEL), jnp.zeros((2, D_MODEL), F32)], axis=0)
    w_in_block = flat_t("w_in").astype(BF16)
    hi = lax.reduce_precision(first, 8, 7)
    mid = lax.reduce_precision(first - hi, 8, 7)
    low = lax.reduce_precision(first - hi - mid, 8, 7)
    terms = jnp.concatenate([hi, mid, low, jnp.zeros((8, D_MODEL), F32)], axis=0).astype(BF16)
    first_block = all_gather(jnp.concatenate([w_in_block, terms], axis=0), "gather_c_taps_w_in")
    terms = first_block[:, n_in:n_in + 24, :].astype(F32).reshape(N_DEV, 3, 8, D_MODEL)
    first_all = (terms[:, 0] + terms[:, 1]) + terms[:, 2]
    c_all = first_all[:, 0, :]
    wc_full = first_all[:, 1:3, :].reshape(N_DEV, 2 * D_MODEL)[:, :CONV_K * n_wc].reshape(N_DEV, CONV_K, n_wc)
    wc_full = wc_full.transpose(1, 0, 2).reshape(CONV_K, D_CONV)
    wf_full = first_all[:, 3:6, :].reshape(N_DEV, 3 * D_MODEL)[:, :FFN_K * n_wf].reshape(N_DEV, FFN_K, n_wf)
    wf_full = wf_full.transpose(1, 0, 2).reshape(FFN_K, 2 * D_FF)
    mod_cols = ada_fwd(c_all, flat("w_ada"), _shard(flat("b_ada"), n_ada, me), "ada_fwd")
    mod_all = all_gather(mod_cols, "gather_mod")
    mod = lax.dynamic_index_in_dim(mod_all, me, axis=1, keepdims=False).reshape(N_MOD, D_MODEL)
    mod = jnp.pad(mod, ((0, 2), (0, 0)))

    order = ("w_out", "w_up", "w_down")
    blocks = dict(w_up=flat_t("w_up").astype(BF16), w_out=flat("w_out").astype(BF16), w_down=flat("w_down").astype(BF16))
    handles, tok = exchange_start([blocks[name] for name in order], True, "gather_weights_start", mod_all)
    gathers = dict(zip(order, handles))
    mod = mod + tok

    def gathered(name, after):
        land = exchange_wait(gathers[name], after, True, f"gather_{name}_wait")
        return lax.dynamic_update_index_in_dim(land, blocks[name], me, axis=0)

    def get_w(name, after):
        if name == "w_in":
            return first_block
        return gathered(name, after).reshape(-1, D_MODEL)

    exchanges, own = {}, {}

    def put_grad(name, dw, after=None):
        dev_major = dw.reshape(N_DEV, -1, D_MODEL)
        own[name] = lax.dynamic_index_in_dim(dev_major, me, axis=0, keepdims=False)
        (exchanges[name],), token = exchange_start([dev_major], False, f"exchange_{name}_start", after)
        return token

    grad_x, small = local_step(
        x[0], loss_target[0], mod, get_w, put_grad, wc_full, wf_full,
        flat("g_norm_mix"), flat("b_conv_dw"), flat("ln_conv_g"), flat("ln_conv_b"), flat("g_conv_out"),
        flat("g_attn_out"), flat("g_norm_ffn"), flat("b_ffn_dw"), flat("g_final"))

    out = {}

    def finish(name, tr, after):
        parts = exchange_wait(exchanges[name], after, False, f"exchange_{name}_wait")
        if name in ("w_in", "w_up"):
            res = sum_adamw(parts, own[name], flat_t(name), flat_t(name, "m_"), flat_t(name, "v_"), tr, "adamw_" + name)
            out[name] = tuple(r.T for r in res)
        else:
            res = out[name] = sum_adamw(parts, own[name], flat(name), flat(name, "m_"), flat(name, "v_"), tr, "adamw_" + name)
        return res[0]

    after = finish("w_down", r_down, grad_x)
    after = finish("w_up", n_up // 2, after)
    after = finish("w_out", r_out, after)
    after = finish("w_in", n_in, after)

    small_all = all_gather(pack_small(small), "gather_small", after)

    wmv = {n: (flat(n), flat(n, "m_"), flat(n, "v_")) for n in SMALL_ORDER}
    per, g_wc, g_wf, loss = small_adamw(small_all, wmv, "adamw_small")
    out.update(per)
    taps = shard_adamw([(_shard(g_wc, n_wc, me), flat("w_conv_dw"), flat("w_conv_dw", "m_"), flat("w_conv_dw", "v_")),
                        (_shard(g_wf, n_wf, me), flat("w_ffn_dw"), flat("w_ffn_dw", "m_"), flat("w_ffn_dw", "v_"))],
                       "adamw_taps")
    out["w_conv_dw"], out["w_ffn_dw"] = taps

    dmod_cols = _shard(small_all[:, 0, :], n_ada, me)
    out["w_ada"] = ada_bwd_adamw(c_all.T, dmod_cols, flat("w_ada"), flat("w_ada", "m_"), flat("w_ada", "v_"), "adamw_w_ada")

    result = [loss, grad_x[None]]
    for k in range(4):
        result += [out[n][k].reshape(args[n].shape) for n in WEIGHTS]
    return tuple(result)
```

```python
import functools

import jax
import jax.numpy as jnp
from jax import lax
from jax.experimental import pallas as pl
from jax.experimental.pallas import tpu as pltpu

F32 = jnp.float32
BF16 = jnp.bfloat16

N_DEV = 8
SEQ = 2048
D_MODEL = 1024
D_CONV = 512
D_ATTN = 512
HEAD_DIM = 64
CONV_K = 31
D_FF = 2816
FFN_K = 3
D_IN = 2 * D_CONV + 3 * D_ATTN
N_MOD = 6
EPS = 1e-6
ATTN_BLOCK = 128
PATTERNS = ((2048, 1), (512, 4), (128, 16))
NEG = -1e30

ADAM_LR, ADAM_B1, ADAM_B2, ADAM_EPS, ADAM_WD, ADAM_STEP = 0.001, 0.9, 0.999, 1e-08, 0.01, 10

ROWS = 256
CONV_HALO = 32
FFN_HALO = 8
FFN_TN = 1408
VMEM_LIMIT = 56 * 1024 * 1024
PACK_ROWS, PACK_W = 16, 6144


NT = (((1,), (1,)), ((), ()))


def _cp(*sem):
    return pltpu.CompilerParams(dimension_semantics=sem if sem else None, vmem_limit_bytes=VMEM_LIMIT)


def _sig(x):
    return 1.0 / (1.0 + jnp.exp(-x))


def _rsum(x):
    return jnp.sum(x, axis=0, keepdims=True)


def _mean(x):
    return jnp.mean(x, axis=-1, keepdims=True)


def _acc(ref, val, first):
    @pl.when(first)
    def _():
        ref[...] = val

    @pl.when(jnp.logical_not(first))
    def _():
        ref[...] += val


SUB = 16


def _for_chunks(fn, unroll=1, rows=ROWS):
    def step(i, carry):
        fn(pl.ds(pl.multiple_of(i * SUB, SUB), SUB))
        return carry

    lax.fori_loop(0, rows // SUB, step, 0, unroll=unroll)


def rms_mod_fwd(x, g, mod, sh_row, sc_row, name):
    def body(x_ref, g_ref, mod_ref, h_ref):
        xx = x_ref[...]
        r = lax.rsqrt(_mean(xx * xx) + EPS)
        h = xx * r * g_ref[...]
        h_ref[...] = (h * (1.0 + mod_ref[sc_row:sc_row + 1, :]) + mod_ref[sh_row:sh_row + 1, :]).astype(BF16)

    return pl.pallas_call(
        body, out_shape=jax.ShapeDtypeStruct((SEQ, D_MODEL), BF16), grid=(SEQ // ROWS,),
        in_specs=[_row_spec(D_MODEL), _vec_spec(D_MODEL), _vec_spec(D_MODEL, 8)],
        out_specs=_row_spec(D_MODEL), name=name, compiler_params=_cp("parallel"))(x, g, mod)


def resid_rms_mod_fwd(x, y, g, mod, ga_row, sh_row, sc_row, name):
    def body(x_ref, y_ref, g_ref, mod_ref, x1_ref, h_ref):
        x1 = x_ref[...] + mod_ref[ga_row:ga_row + 1, :] * y_ref[...]
        x1_ref[...] = x1
        r = lax.rsqrt(_mean(x1 * x1) + EPS)
        h = x1 * r * g_ref[...]
        h_ref[...] = (h * (1.0 + mod_ref[sc_row:sc_row + 1, :]) + mod_ref[sh_row:sh_row + 1, :]).astype(BF16)

    return pl.pallas_call(
        body, out_shape=(jax.ShapeDtypeStruct((SEQ, D_MODEL), F32), jax.ShapeDtypeStruct((SEQ, D_MODEL), BF16)),
        grid=(SEQ // ROWS,),
        in_specs=[_row_spec(D_MODEL), _row_spec(D_MODEL), _vec_spec(D_MODEL), _vec_spec(D_MODEL, 8)],
        out_specs=(_row_spec(D_MODEL), _row_spec(D_MODEL)), name=name, compiler_params=_cp("parallel"))(x, y, g, mod)


def final_loss_bwd(x1, y2, tgt, g, mod, ga_row, name):
    def body(x1_ref, y2_ref, t_ref, g_ref, mod_ref, loss_ref, dx2_ref, dy2_ref, dg_ref, dga_ref):
        first = pl.program_id(0) == 0
        ga = mod_ref[ga_row:ga_row + 1, :]
        y2 = y2_ref[...]
        x2 = x1_ref[...] + ga * y2
        r = lax.rsqrt(_mean(x2 * x2) + EPS)
        xn = x2 * r
        err = xn * g_ref[...] - t_ref[...]
        _acc(loss_ref, jnp.broadcast_to(0.5 * jnp.sum(_mean(err * err)), (8, 128)), first)
        dy = err * (1.0 / D_MODEL)
        _acc(dg_ref, _rsum(dy * xn), first)
        dxn = dy * g_ref[...]
        dx2 = r * (dxn - xn * _mean(dxn * xn))
        dx2_ref[...] = dx2
        dy2_ref[...] = (dx2 * ga).astype(BF16)
        _acc(dga_ref, _rsum(dx2 * y2), first)

    vec = jax.ShapeDtypeStruct((1, D_MODEL), F32)
    return pl.pallas_call(
        body,
        out_shape=(jax.ShapeDtypeStruct((8, 128), F32), jax.ShapeDtypeStruct((SEQ, D_MODEL), F32),
                   jax.ShapeDtypeStruct((SEQ, D_MODEL), BF16), vec, vec),
        grid=(SEQ // ROWS,),
        in_specs=[_row_spec(D_MODEL), _row_spec(D_MODEL), _row_spec(D_MODEL), _vec_spec(D_MODEL), _vec_spec(D_MODEL, 8)],
        out_specs=(pl.BlockSpec((8, 128), lambda i: (0, 0)), _row_spec(D_MODEL), _row_spec(D_MODEL),
                   _vec_spec(D_MODEL), _vec_spec(D_MODEL)),
        name=name, compiler_params=_cp("arbitrary"))(x1, y2, tgt, g, mod)


def rms_mod_bwd(x, dh, dres, g, mod, sc_row, y, ga_row, name):
    gated = y is not None

    def body(*refs):
        if gated:
            x_ref, dh_ref, dres_ref, g_ref, mod_ref, y_ref, dx_ref, dsh_ref, dsc_ref, dg_ref, dy_ref, dga_ref = refs
        else:
            x_ref, dh_ref, dres_ref, g_ref, mod_ref, dx_ref, dsh_ref, dsc_ref, dg_ref = refs
        first = pl.program_id(0) == 0
        xx = x_ref[...]
        dh = dh_ref[...]
        gg = g_ref[...]
        r = lax.rsqrt(_mean(xx * xx) + EPS)
        xn = xx * r
        _acc(dsh_ref, _rsum(dh), first)
        _acc(dsc_ref, _rsum(dh * (xn * gg)), first)
        dt = dh * (1.0 + mod_ref[sc_row:sc_row + 1, :])
        _acc(dg_ref, _rsum(dt * xn), first)
        dxn = dt * gg
        dx = dres_ref[...] + r * (dxn - xn * _mean(dxn * xn))
        dx_ref[...] = dx
        if gated:
            _acc(dga_ref, _rsum(dx * y_ref[...]), first)
            dy_ref[...] = (dx * mod_ref[ga_row:ga_row + 1, :]).astype(BF16)

    vec = jax.ShapeDtypeStruct((1, D_MODEL), F32)
    in_specs = [_row_spec(D_MODEL), _row_spec(D_MODEL), _row_spec(D_MODEL), _vec_spec(D_MODEL), _vec_spec(D_MODEL, 8)]
    out_shape = [jax.ShapeDtypeStruct((SEQ, D_MODEL), F32), vec, vec, vec]
    out_specs = [_row_spec(D_MODEL), _vec_spec(D_MODEL), _vec_spec(D_MODEL), _vec_spec(D_MODEL)]
    args = [x, dh, dres, g, mod]
    if gated:
        in_specs.append(_row_spec(D_MODEL))
        out_shape += [jax.ShapeDtypeStruct((SEQ, D_MODEL), BF16), vec]
        out_specs += [_row_spec(D_MODEL), _vec_spec(D_MODEL)]
        args.append(y)
    return pl.pallas_call(
        body, out_shape=tuple(out_shape), grid=(SEQ // ROWS,), in_specs=in_specs, out_specs=tuple(out_specs),
        name=name, compiler_params=_cp("arbitrary"))(*args)


def rms_mod_matmul(x, g, mod, sh_row, sc_row, b, b_rows, name):
    n = N_DEV * b_rows

    def body(x_ref, g_ref, mod_ref, b_ref, o_ref, h_ref):
        xx = x_ref[...]
        r = lax.rsqrt(_mean(xx * xx) + EPS)
        h = (xx * r * g_ref[...] * (1.0 + mod_ref[sc_row:sc_row + 1, :]) + mod_ref[sh_row:sh_row + 1, :]).astype(BF16)
        h_ref[...] = h
        o_ref[...] = lax.dot_general(h, b_ref[...].reshape(n, D_MODEL), NT, preferred_element_type=F32)

    return pl.pallas_call(
        body, out_shape=(jax.ShapeDtypeStruct((SEQ, n), F32), jax.ShapeDtypeStruct((SEQ, D_MODEL), BF16)),
        grid=(SEQ // ROWS,),
        in_specs=[_row_spec(D_MODEL), _vec_spec(D_MODEL), _vec_spec(D_MODEL, 8),
                  pl.BlockSpec((N_DEV, b_rows, D_MODEL), lambda i: (0, 0, 0))],
        out_specs=(_row_spec(n), _row_spec(D_MODEL)), name=name, compiler_params=_cp("parallel"))(x, g, mod, b)


def norm_concat_matmul(mix_a, att, gao, w, name):
    def body(a_ref, att_ref, g_ref, w_ref, y_ref, mixed_ref):
        aa = att_ref[...]
        mixed_ref[:, 0:D_CONV] = a_ref[...]
        mixed_ref[:, D_CONV:] = (aa * lax.rsqrt(_mean(aa * aa) + EPS) * g_ref[...]).astype(BF16)
        y_ref[...] = jnp.dot(mixed_ref[...], w_ref[...], preferred_element_type=F32)

    return pl.pallas_call(
        body, out_shape=(jax.ShapeDtypeStruct((SEQ, D_MODEL), F32), jax.ShapeDtypeStruct((SEQ, D_MODEL), BF16)),
        grid=(SEQ // ROWS,),
        in_specs=[_row_spec(D_CONV), _row_spec(D_ATTN), _vec_spec(D_ATTN), pl.BlockSpec(w.shape, lambda i: (0, 0))],
        out_specs=(_row_spec(D_MODEL), _row_spec(D_MODEL)), name=name, compiler_params=_cp("parallel"))(mix_a, att, gao, w)


def resid_rms_mod_matmul(x, y, g, mod, ga_row, sh_row, sc_row, w, name):
    n = w.shape[0]

    def body(x_ref, y_ref, g_ref, mod_ref, w_ref, o_ref, x1_ref, h_ref):
        x1 = x_ref[...] + mod_ref[ga_row:ga_row + 1, :] * y_ref[...]
        x1_ref[...] = x1
        r = lax.rsqrt(_mean(x1 * x1) + EPS)
        h = (x1 * r * g_ref[...] * (1.0 + mod_ref[sc_row:sc_row + 1, :]) + mod_ref[sh_row:sh_row + 1, :]).astype(BF16)
        h_ref[...] = h
        o_ref[...] = lax.dot_general(h, w_ref[...], NT, preferred_element_type=F32)

    return pl.pallas_call(
        body,
        out_shape=(jax.ShapeDtypeStruct((SEQ, n), F32), jax.ShapeDtypeStruct((SEQ, D_MODEL), F32),
                   jax.ShapeDtypeStruct((SEQ, D_MODEL), BF16)),
        grid=(SEQ // ROWS,),
        in_specs=[_row_spec(D_MODEL), _row_spec(D_MODEL), _vec_spec(D_MODEL), _vec_spec(D_MODEL, 8),
                  pl.BlockSpec(w.shape, lambda i: (0, 0))],
        out_specs=(_row_spec(n), _row_spec(D_MODEL), _row_spec(D_MODEL)),
        name=name, compiler_params=_cp("parallel"))(x, y, g, mod, w)


def matmul_combine_bwd(dy, w, att, gao, name):
    def body(dy_ref, w_ref, att_ref, g_ref, dm_ref, do_ref, dd_ref, dg_ref):
        first = pl.program_id(0) == 0
        dm_ref[...] = lax.dot_general(dy_ref[...], w_ref[...], NT, preferred_element_type=F32)
        att = att_ref[...]
        r = lax.rsqrt(_mean(att * att) + EPS)
        xn = att * r
        dm = dm_ref[:, D_CONV:]
        _acc(dg_ref, _rsum(dm * xn), first)
        dyn = dm * g_ref[...]
        do = r * (dyn - xn * _mean(dyn * xn))
        do_ref[...] = do
        same_head = (jnp.right_shift(lax.broadcasted_iota(jnp.int32, (D_ATTN, D_ATTN), 0), 6)
                     == jnp.right_shift(lax.broadcasted_iota(jnp.int32, (D_ATTN, D_ATTN), 1), 6)).astype(F32)
        dd_ref[...] = jnp.dot(do * att, same_head, preferred_element_type=F32, precision=lax.Precision.HIGHEST)

    rs = _row_spec(D_ATTN)
    f = jax.ShapeDtypeStruct((SEQ, D_ATTN), F32)
    return pl.pallas_call(
        body, out_shape=(jax.ShapeDtypeStruct((SEQ, D_MODEL), F32), f, f, jax.ShapeDtypeStruct((1, D_ATTN), F32)),
        grid=(SEQ // ROWS,),
        in_specs=[_row_spec(D_MODEL), pl.BlockSpec(w.shape, lambda i: (0, 0)), rs, _vec_spec(D_ATTN)],
        out_specs=(_row_spec(D_MODEL), rs, rs, _vec_spec(D_ATTN)),
        name=name, compiler_params=_cp("arbitrary"))(dy, w, att, gao)


def matmul_loss_bwd(act, w, x1, tgt, g, mod, ga_row, name):
    def body(a_ref, w_ref, x1_ref, t_ref, g_ref, mod_ref, loss_ref, dx2_ref, dy2_ref, dg_ref, dga_ref):
        first = pl.program_id(0) == 0
        ga = mod_ref[ga_row:ga_row + 1, :]
        y2 = jnp.dot(a_ref[...], w_ref[...], preferred_element_type=F32)
        x2 = x1_ref[...] + ga * y2
        r = lax.rsqrt(_mean(x2 * x2) + EPS)
        xn = x2 * r
        err = xn * g_ref[...] - t_ref[...]
        _acc(loss_ref, jnp.broadcast_to(0.5 * jnp.sum(_mean(err * err)), (8, 128)), first)
        dy = err * (1.0 / D_MODEL)
        _acc(dg_ref, _rsum(dy * xn), first)
        dxn = dy * g_ref[...]
        dx2 = r * (dxn - xn * _mean(dxn * xn))
        dx2_ref[...] = dx2
        dy2_ref[...] = (dx2 * ga).astype(BF16)
        _acc(dga_ref, _rsum(dx2 * y2), first)

    vec = jax.ShapeDtypeStruct((1, D_MODEL), F32)
    return pl.pallas_call(
        body,
        out_shape=(jax.ShapeDtypeStruct((8, 128), F32), jax.ShapeDtypeStruct((SEQ, D_MODEL), F32),
                   jax.ShapeDtypeStruct((SEQ, D_MODEL), BF16), vec, vec),
        grid=(SEQ // ROWS,),
        in_specs=[_row_spec(act.shape[1]), pl.BlockSpec(w.shape, lambda i: (0, 0)), _row_spec(D_MODEL), _row_spec(D_MODEL),
                  _vec_spec(D_MODEL), _vec_spec(D_MODEL, 8)],
        out_specs=(pl.BlockSpec((8, 128), lambda i: (0, 0)), _row_spec(D_MODEL), _row_spec(D_MODEL),
                   _vec_spec(D_MODEL), _vec_spec(D_MODEL)),
        name=name, compiler_params=_cp("arbitrary"))(act, w, x1, tgt, g, mod)


def matmul_rms_mod_bwd(a, b, x, dres, g, mod, sc_row, y, ga_row, name, b_rows=None):
    gated = y is not None
    halves = a.ndim == 3
    if halves:
        k2 = a.shape[2]
        b_arg = b.reshape(2, k2, D_MODEL)
        a_spec = pl.BlockSpec((2, ROWS, k2), lambda i: (0, i, 0))
        b_spec = pl.BlockSpec((2, k2, D_MODEL), lambda i: (0, 0, 0))
    else:
        b_arg = b
        a_spec = _row_spec(a.shape[1])
        b_spec = pl.BlockSpec((N_DEV, b_rows, D_MODEL), lambda i: (0, 0, 0))

    def body(*refs):
        if gated:
            a_ref, b_ref, x_ref, dres_ref, g_ref, mod_ref, y_ref, dx_ref, dsh_ref, dsc_ref, dg_ref, dy_ref, dga_ref = refs
        else:
            a_ref, b_ref, x_ref, dres_ref, g_ref, mod_ref, dx_ref, dsh_ref, dsc_ref, dg_ref = refs
        first = pl.program_id(0) == 0
        if halves:
            dh = (jnp.dot(a_ref[0], b_ref[0], preferred_element_type=F32)
                  + jnp.dot(a_ref[1], b_ref[1], preferred_element_type=F32))
        else:
            dh = jnp.dot(a_ref[...], b_ref[...].reshape(N_DEV * b_rows, D_MODEL), preferred_element_type=F32)
        xx = x_ref[...]
        gg = g_ref[...]
        r = lax.rsqrt(_mean(xx * xx) + EPS)
        xn = xx * r
        _acc(dsh_ref, _rsum(dh), first)
        _acc(dsc_ref, _rsum(dh * (xn * gg)), first)
        dt = dh * (1.0 + mod_ref[sc_row:sc_row + 1, :])
        _acc(dg_ref, _rsum(dt * xn), first)
        dxn = dt * gg
        dx = dres_ref[...] + r * (dxn - xn * _mean(dxn * xn))
        dx_ref[...] = dx
        if gated:
            _acc(dga_ref, _rsum(dx * y_ref[...]), first)
            dy_ref[...] = (dx * mod_ref[ga_row:ga_row + 1, :]).astype(BF16)

    vec = jax.ShapeDtypeStruct((1, D_MODEL), F32)
    in_specs = [a_spec, b_spec, _row_spec(D_MODEL), _row_spec(D_MODEL), _vec_spec(D_MODEL), _vec_spec(D_MODEL, 8)]
    out_shape = [jax.ShapeDtypeStruct((SEQ, D_MODEL), F32), vec, vec, vec]
    out_specs = [_row_spec(D_MODEL), _vec_spec(D_MODEL), _vec_spec(D_MODEL), _vec_spec(D_MODEL)]
    args = [a, b_arg, x, dres, g, mod]
    if gated:
        in_specs.append(_row_spec(D_MODEL))
        out_shape += [jax.ShapeDtypeStruct((SEQ, D_MODEL), BF16), vec]
        out_specs += [_row_spec(D_MODEL), _vec_spec(D_MODEL)]
        args.append(y)
    return pl.pallas_call(
        body, out_shape=tuple(out_shape), grid=(SEQ // ROWS,), in_specs=in_specs, out_specs=tuple(out_specs),
        name=name, compiler_params=_cp("arbitrary"))(*args)


def _prev_halo(halo, width, col):
    per = ROWS // halo
    return pl.BlockSpec((halo, width), lambda i: (jnp.maximum(i * per - 1, 0), col))


def _next_halo(halo, width, col):
    per = ROWS // halo
    last = SEQ // halo - 1
    return pl.BlockSpec((halo, width), lambda i: (jnp.minimum((i + 1) * per, last), col))


CONV_PAD = ROWS + CONV_HALO


def _shift_copies(sh):
    for b in range(1, 8):
        sh[b, 0:CONV_PAD - 8, :] = sh[0, pl.ds(b, CONV_PAD - 8), :]


def _tap(sh, rs_start, offset):
    return sh[offset % 8, pl.ds(pl.multiple_of(rs_start + (offset // 8) * 8, 8), SUB), :]


def _conv_glu(av_ref, ag_ref, avh_ref, agh_ref, sh):
    i = pl.program_id(0)
    hv = avh_ref[...] * _sig(agh_ref[...])
    sh[0, 0:CONV_HALO, :] = jnp.where(i > 0, hv, 0.0)

    def glu(rs):
        sh[0, pl.ds(pl.multiple_of(rs.start + CONV_HALO, SUB), SUB), :] = av_ref[rs, :] * _sig(ag_ref[rs, :])

    _for_chunks(glu)
    _shift_copies(sh)


def _conv_norm(u1, lg_ref, lb_ref):
    mu = _mean(u1)
    cen = u1 - mu
    rs = lax.rsqrt(_mean(cen * cen) + EPS)
    z = cen * rs
    ln = z * lg_ref[...] + lb_ref[...]
    s = _sig(ln)
    return z, rs, ln, s, ln * s


def conv_module_fwd(proj, wc, bc, lg, lb, gco, name):
    def body(av_ref, ag_ref, avh_ref, agh_ref, wc_ref, bc_ref, lg_ref, lb_ref, gco_ref, out_ref, u1_ref, sh):
        _conv_glu(av_ref, ag_ref, avh_ref, agh_ref, sh)

        def conv(rs):
            u1 = jnp.broadcast_to(bc_ref[...], (SUB, D_CONV))
            for j in range(CONV_K):
                u1 = u1 + wc_ref[j:j + 1, :] * _tap(sh, rs.start, CONV_HALO - (CONV_K - 1) + j)
            u1_ref[rs, :] = u1

        _for_chunks(conv)
        _, _, _, _, u2 = _conv_norm(u1_ref[...], lg_ref, lb_ref)
        rc = lax.rsqrt(_mean(u2 * u2) + EPS)
        out_ref[...] = (u2 * rc * gco_ref[...]).astype(BF16)

    v = _vec_spec(D_CONV)
    return pl.pallas_call(
        body, out_shape=(jax.ShapeDtypeStruct((SEQ, D_CONV), BF16), jax.ShapeDtypeStruct((SEQ, D_CONV), F32)),
        grid=(SEQ // ROWS,),
        in_specs=[_row_spec(D_CONV, 0), _row_spec(D_CONV, 1), _prev_halo(CONV_HALO, D_CONV, 0),
                  _prev_halo(CONV_HALO, D_CONV, 1), _vec_spec(D_CONV, CONV_K), v, v, v, v],
        out_specs=(_row_spec(D_CONV), _row_spec(D_CONV)), scratch_shapes=[pltpu.VMEM((8, CONV_PAD, D_CONV), F32)],
        name=name, compiler_params=_cp("parallel"))(proj, proj, proj, proj, wc, bc, lg, lb, gco)


def conv_module_bwd_a(proj, u1, dmixed, lg, lb, gco, name):
    def body(av_ref, ag_ref, avh_ref, agh_ref, u1_ref, dm_ref, lg_ref, lb_ref, gco_ref,
             du1_ref, dgco_ref, dlg_ref, dlb_ref, dbc_ref, dwc_ref, sh, acc):
        first = pl.program_id(0) == 0
        _conv_glu(av_ref, ag_ref, avh_ref, agh_ref, sh)
        z, rs, ln, s, u2 = _conv_norm(u1_ref[...], lg_ref, lb_ref)
        rc = lax.rsqrt(_mean(u2 * u2) + EPS)
        xn = u2 * rc
        dm = dm_ref[...]
        _acc(dgco_ref, _rsum(dm * xn), first)
        dyn = dm * gco_ref[...]
        du2 = rc * (dyn - xn * _mean(dyn * xn))
        dln = du2 * (s * (1.0 + ln * (1.0 - s)))
        _acc(dlg_ref, _rsum(dln * z), first)
        _acc(dlb_ref, _rsum(dln), first)
        dz = dln * lg_ref[...]
        du1 = rs * (dz - _mean(dz) - z * _mean(dz * z))
        du1_ref[...] = du1
        _acc(dbc_ref, _rsum(du1), first)
        acc[...] = jnp.zeros_like(acc)

        def taps(rs):
            d = du1_ref[rs, :]
            for j in range(CONV_K):
                acc[j] += d * _tap(sh, rs.start, CONV_HALO - (CONV_K - 1) + j)

        _for_chunks(taps)

        @pl.when(first)
        def _():
            dwc_ref[...] = jnp.zeros_like(dwc_ref)

        for j in range(CONV_K):
            dwc_ref[j:j + 1, :] += _rsum(acc[j])

    v = _vec_spec(D_CONV)
    vec = jax.ShapeDtypeStruct((1, D_CONV), F32)
    return pl.pallas_call(
        body,
        out_shape=(jax.ShapeDtypeStruct((SEQ, D_CONV), F32), vec, vec, vec, vec, jax.ShapeDtypeStruct((CONV_K, D_CONV), F32)),
        grid=(SEQ // ROWS,),
        in_specs=[_row_spec(D_CONV, 0), _row_spec(D_CONV, 1), _prev_halo(CONV_HALO, D_CONV, 0),
                  _prev_halo(CONV_HALO, D_CONV, 1), _row_spec(D_CONV, 0), _row_spec(D_CONV, 0), v, v, v],
        out_specs=(_row_spec(D_CONV), v, v, v, v, _vec_spec(D_CONV, CONV_K)),
        scratch_shapes=[pltpu.VMEM((8, CONV_PAD, D_CONV), F32), pltpu.VMEM((CONV_K, SUB, D_CONV), F32)],
        name=name, compiler_params=_cp("arbitrary"))(proj, proj, proj, proj, u1, dmixed, lg, lb, gco)


def conv_module_bwd_b(proj, du1, wc, name):
    def body(av_ref, ag_ref, du1_ref, du1n_ref, wc_ref, out_ref, sh):
        i = pl.program_id(0)
        sh[0, 0:ROWS, :] = du1_ref[...]
        sh[0, ROWS:, :] = jnp.where(i < SEQ // ROWS - 1, du1n_ref[...], 0.0)
        _shift_copies(sh)

        def chunk(rs):
            du0 = jnp.zeros((SUB, D_CONV), F32)
            for j in range(CONV_K):
                du0 = du0 + wc_ref[j:j + 1, :] * _tap(sh, rs.start, CONV_K - 1 - j)
            sg = _sig(ag_ref[rs, :])
            out_ref[rs, 0:D_CONV] = (du0 * sg).astype(BF16)
            out_ref[rs, D_CONV:] = (du0 * av_ref[rs, :] * sg * (1.0 - sg)).astype(BF16)

        _for_chunks(chunk)

    return pl.pallas_call(
        body, out_shape=jax.ShapeDtypeStruct((SEQ, 2 * D_CONV), BF16), grid=(SEQ // ROWS,),
        in_specs=[_row_spec(D_CONV, 0), _row_spec(D_CONV, 1), _row_spec(D_CONV, 0), _next_halo(CONV_HALO, D_CONV, 0),
                  _vec_spec(D_CONV, CONV_K)],
        out_specs=_row_spec(2 * D_CONV), scratch_shapes=[pltpu.VMEM((8, CONV_PAD, D_CONV), F32)],
        name=name, compiler_params=_cp("parallel"))(proj, proj, du1, du1, wc)


def _attn_specs(sub_len, pairs):
    ng, width = 4 // pairs, 128 * pairs
    q = pl.BlockSpec((sub_len, width), lambda rho, g: (0, rho * 3 * ng + g))
    k = pl.BlockSpec((sub_len, width), lambda rho, g: (0, rho * 3 * ng + ng + g))
    v = pl.BlockSpec((sub_len, width), lambda rho, g: (0, rho * 3 * ng + 2 * ng + g))
    o = pl.BlockSpec((sub_len, width), lambda rho, g: (0, rho * ng + g))
    return q, k, v, o


ATTN_PAIRS = {1: 1, 4: 1, 16: 4}


def _attn_block(q_ref, k_ref, v_ref, n, hs, win):
    q0 = pl.multiple_of(n * ATTN_BLOCK, ATTN_BLOCK)
    k0 = pl.multiple_of(jnp.maximum(n - 1, 0) * ATTN_BLOCK, ATTN_BLOCK)
    qb = q_ref[pl.ds(q0, ATTN_BLOCK), hs]
    kw = k_ref[pl.ds(k0, win), hs]
    vw = v_ref[pl.ds(k0, win), hs]
    s = lax.dot_general(qb, kw, (((1,), (1,)), ((), ())), preferred_element_type=F32) * (HEAD_DIM ** -0.5)
    dist = (q0 - k0) + lax.broadcasted_iota(jnp.int32, (ATTN_BLOCK, win), 0) \
        - lax.broadcasted_iota(jnp.int32, (ATTN_BLOCK, win), 1)
    s = jnp.where((dist >= 0) & (dist <= ATTN_BLOCK), s, NEG)
    return q0, k0, qb, kw, vw, s


def attn_fwd(qkv_r, sub_len, r, name):
    nb = sub_len // ATTN_BLOCK
    win = 2 * ATTN_BLOCK if nb > 1 else ATTN_BLOCK
    pairs = ATTN_PAIRS[r]

    def body(q_ref, k_ref, v_ref, o_ref, l_ref):
        def block(n, carry):
            for h in range(2 * pairs):
                hs = slice(h * HEAD_DIM, (h + 1) * HEAD_DIM)
                q0, _, _, _, vw, s = _attn_block(q_ref, k_ref, v_ref, n, hs, win)
                m = jnp.max(s, axis=1, keepdims=True)
                p = jnp.exp(s - m)
                den = jnp.sum(p, axis=1, keepdims=True)
                o = jnp.dot(p.astype(BF16), vw, preferred_element_type=F32) / den
                o_ref[pl.ds(q0, ATTN_BLOCK), hs] = o
                l_ref[pl.ds(q0, ATTN_BLOCK), hs] = jnp.broadcast_to(m + jnp.log(den), (ATTN_BLOCK, HEAD_DIM))
            return carry

        lax.fori_loop(0, nb, block, 0, unroll=min(nb, 2))

    q, k, v, o = _attn_specs(sub_len, pairs)
    shp = jax.ShapeDtypeStruct((sub_len, r * D_ATTN), F32)
    return pl.pallas_call(
        body, out_shape=(shp, shp), grid=(r, 4 // pairs), in_specs=[q, k, v], out_specs=(o, o),
        name=name, compiler_params=_cp("parallel", "parallel"))(qkv_r, qkv_r, qkv_r)


def attn_bwd(qkv_r, do_r, lse_r, dd_r, sub_len, r, name):
    nb = sub_len // ATTN_BLOCK
    win = 2 * ATTN_BLOCK if nb > 1 else ATTN_BLOCK
    pairs = ATTN_PAIRS[r]

    def body(q_ref, k_ref, v_ref, do_ref, l_ref, dd_ref, dq_ref, dk_ref, dv_ref):
        dk_ref[...] = jnp.zeros_like(dk_ref)
        dv_ref[...] = jnp.zeros_like(dv_ref)

        def block(n, carry):
            for h in range(2 * pairs):
                hs = slice(h * HEAD_DIM, (h + 1) * HEAD_DIM)
                h1 = slice(h * HEAD_DIM, h * HEAD_DIM + 1)
                q0, k0, qb, kw, vw, s = _attn_block(q_ref, k_ref, v_ref, n, hs, win)
                dob = do_ref[pl.ds(q0, ATTN_BLOCK), hs]
                p = jnp.exp(s - l_ref[pl.ds(q0, ATTN_BLOCK), h1])
                dp = lax.dot_general(dob, vw, (((1,), (1,)), ((), ())), preferred_element_type=F32)
                ds = (p * (dp - dd_ref[pl.ds(q0, ATTN_BLOCK), h1]) * (HEAD_DIM ** -0.5)).astype(BF16)
                dq_ref[pl.ds(q0, ATTN_BLOCK), hs] = jnp.dot(ds, kw, preferred_element_type=F32)
                dk_ref[pl.ds(k0, win), hs] += lax.dot_general(ds, qb, (((0,), (0,)), ((), ())), preferred_element_type=F32)
                dv_ref[pl.ds(k0, win), hs] += lax.dot_general(p.astype(BF16), dob, (((0,), (0,)), ((), ())),
                                                              preferred_element_type=F32)
            return carry

        lax.fori_loop(0, nb, block, 0)

    q, k, v, o = _attn_specs(sub_len, pairs)
    shp = jax.ShapeDtypeStruct((sub_len, r * D_ATTN), F32)
    return pl.pallas_call(
        body, out_shape=(shp, shp, shp), grid=(r, 4 // pairs), in_specs=[q, k, v, o, o, o], out_specs=(o, o, o),
        name=name, compiler_params=_cp("parallel", "parallel"))(qkv_r, qkv_r, qkv_r, do_r, lse_r, dd_r)


def _rows(start, size, r):
    return pl.ds(start, size) if r == 1 else pl.ds(start, size, stride=r)


def _unit_rows(r, rho, n, nb):
    win = 2 * ATTN_BLOCK if nb > 1 else ATTN_BLOCK
    if isinstance(n, int):
        kb = max(n - 1, 0)
        q_rows = _rows(rho + r * ATTN_BLOCK * n, ATTN_BLOCK, r)
        k_rows = _rows(rho + r * ATTN_BLOCK * kb, win, r)
    else:
        kb = jnp.maximum(n - 1, 0)
        q_rows = pl.ds(pl.multiple_of(n * ATTN_BLOCK, ATTN_BLOCK), ATTN_BLOCK)
        k_rows = pl.ds(pl.multiple_of(kb * ATTN_BLOCK, ATTN_BLOCK), win)
    dist = (n - kb) * ATTN_BLOCK + lax.broadcasted_iota(jnp.int32, (ATTN_BLOCK, win), 0) \
        - lax.broadcasted_iota(jnp.int32, (ATTN_BLOCK, win), 1)
    return q_rows, k_rows, (dist >= 0) & (dist <= ATTN_BLOCK)


def _per_head(x):
    lane = lax.broadcasted_iota(jnp.int32, x.shape, 1)
    zero = jnp.zeros_like(x)
    return [jnp.where(lane < HEAD_DIM, x, zero), jnp.where(lane >= HEAD_DIM, x, zero)]


def _masked_scores(q2, k2, valid):
    return [jnp.where(valid, lax.dot_general(qh, k2, NT, preferred_element_type=F32) * (HEAD_DIM ** -0.5), NEG)
            for qh in _per_head(q2)]


def _attn_units(r, nb, unit):
    if r == 1:
        def four(i, carry):
            for k in range(4):
                unit(0, 4 * i + k)
            return carry
        lax.fori_loop(0, nb // 4, four, 0)
    else:
        for rho in range(r):
            for n in range(nb):
                unit(rho, n)


N_UNITS = 16


def attn_fwd_all(proj, name):
    def body(q_ref, k_ref, v_ref, att_ref, lse_ref, s_scr, p_scr, lse_scr, den_scr):
        for idx, (sub_len, r) in enumerate(PATTERNS):
            nb = sub_len // ATTN_BLOCK
            win = 2 * ATTN_BLOCK if nb > 1 else ATTN_BLOCK

            def scores(rho, n, r=r, nb=nb, win=win):
                u = rho * nb + n
                q_rows, k_rows, valid = _unit_rows(r, rho, n, nb)
                ss = _masked_scores(q_ref[q_rows, :].astype(BF16), k_ref[k_rows, :].astype(BF16), valid)
                for h in range(2):
                    s_scr[2 * u + h, :, 0:win] = ss[h]

            _attn_units(r, nb, scores)

            def softmax(u, carry, win=win):
                lses, dens = [], []
                for h in range(2):
                    sc = s_scr[2 * u + h, :, 0:win]
                    m = jnp.max(sc, axis=1, keepdims=True)
                    p = jnp.exp(sc - m)
                    den = jnp.sum(p, axis=1, keepdims=True)
                    p_scr[2 * u + h, :, 0:win] = p.astype(BF16)
                    lses.append(jnp.broadcast_to(m + jnp.log(den), (ATTN_BLOCK, HEAD_DIM)))
                    dens.append(jnp.broadcast_to(den, (ATTN_BLOCK, HEAD_DIM)))
                lse_scr[u] = jnp.concatenate(lses, axis=1)
                den_scr[u] = jnp.concatenate(dens, axis=1)
                return carry

            lax.fori_loop(0, N_UNITS, softmax, 0, unroll=2)

            def outputs(rho, n, r=r, nb=nb, win=win, idx=idx):
                u = rho * nb + n
                q_rows, k_rows, _ = _unit_rows(r, rho, n, nb)
                vs = _per_head(v_ref[k_rows, :].astype(BF16))
                o = (jnp.dot(p_scr[2 * u, :, 0:win], vs[0], preferred_element_type=F32)
                     + jnp.dot(p_scr[2 * u + 1, :, 0:win], vs[1], preferred_element_type=F32)) / den_scr[u]
                lse = lse_scr[u]
                if idx > 0:
                    old = lse_ref[q_rows, :]
                    top = jnp.maximum(old, lse)
                    new = top + jnp.log(jnp.exp(old - top) + jnp.exp(lse - top))
                    o = att_ref[q_rows, :] * jnp.exp(old - new) + o * jnp.exp(lse - new)
                    lse = new
                att_ref[q_rows, :] = o
                lse_ref[q_rows, :] = lse

            _attn_units(r, nb, outputs)

    blk = lambda first: pl.BlockSpec((SEQ, 128), lambda g: (0, first + g))
    shp = jax.ShapeDtypeStruct((SEQ, D_ATTN), F32)
    big = (2 * N_UNITS, ATTN_BLOCK, 2 * ATTN_BLOCK)
    small = pltpu.VMEM((N_UNITS, ATTN_BLOCK, 128), F32)
    return pl.pallas_call(
        body, out_shape=(shp, shp), grid=(4,), in_specs=[blk(8), blk(12), blk(16)], out_specs=(blk(0), blk(0)),
        scratch_shapes=[pltpu.VMEM(big, F32), pltpu.VMEM(big, BF16), small, small],
        name=name, compiler_params=_cp("parallel"))(proj, proj, proj)


def attn_bwd_all(proj, do, lse, dd, name):
    scale = HEAD_DIM ** -0.5

    def body(q_ref, k_ref, v_ref, do_ref, l_ref, dd_ref, out_ref, dq_s, dk_s, dv_s,
             s_scr, dp_scr, ds_scr, st_scr, dpt_scr, pt_scr, dst_scr, qb_scr, kb_scr, dob_scr):
        dq_s[...] = jnp.zeros_like(dq_s)
        dk_s[...] = jnp.zeros_like(dk_s)
        dv_s[...] = jnp.zeros_like(dv_s)
        for sub_len, r in PATTERNS:
            nb = sub_len // ATTN_BLOCK
            win = 2 * ATTN_BLOCK if nb > 1 else ATTN_BLOCK

            def scores(rho, n, r=r, nb=nb, win=win):
                u = rho * nb + n
                q_rows, k_rows, valid = _unit_rows(r, rho, n, nb)
                kb = max(n - 1, 0) if isinstance(n, int) else jnp.maximum(n - 1, 0)
                dist_t = (n - kb) * ATTN_BLOCK + lax.broadcasted_iota(jnp.int32, (win, ATTN_BLOCK), 1) \
                    - lax.broadcasted_iota(jnp.int32, (win, ATTN_BLOCK), 0)
                valid_t = (dist_t >= 0) & (dist_t <= ATTN_BLOCK)
                q2 = q_ref[q_rows, :].astype(BF16)
                k2 = k_ref[k_rows, :].astype(BF16)
                do2 = do_ref[q_rows, :].astype(BF16)
                qb_scr[u] = q2
                kb_scr[u, 0:win, :] = k2
                dob_scr[u] = do2
                l2 = l_ref[q_rows, :]
                d2 = dd_ref[q_rows, :]
                l2t = l2.T
                d2t = d2.T
                v2 = v_ref[k_rows, :].astype(BF16)
                qs, dos = _per_head(q2), _per_head(do2)
                for h in range(2):
                    c0 = h * HEAD_DIM
                    sc = lax.dot_general(qs[h], k2, NT, preferred_element_type=F32) * scale
                    s_scr[2 * u + h, :, 0:win] = jnp.where(valid, sc, NEG) - l2[:, c0:c0 + 1]
                    dp_scr[2 * u + h, :, 0:win] = lax.dot_general(dos[h], v2, NT, preferred_element_type=F32) \
                        - d2[:, c0:c0 + 1]
                    sct = lax.dot_general(k2, qs[h], NT, preferred_element_type=F32) * scale
                    st_scr[2 * u + h, 0:win, :] = jnp.where(valid_t, sct, NEG) - l2t[c0:c0 + 1, :]
                    dpt_scr[2 * u + h, 0:win, :] = lax.dot_general(v2, dos[h], NT, preferred_element_type=F32) \
                        - d2t[c0:c0 + 1, :]

            _attn_units(r, nb, scores)

            def pointwise(hu, carry, win=win):
                ds_scr[hu, :, 0:win] = (jnp.exp(s_scr[hu, :, 0:win]) * dp_scr[hu, :, 0:win] * scale).astype(BF16)
                pt = jnp.exp(st_scr[hu, 0:win, :])
                pt_scr[hu, 0:win, :] = pt.astype(BF16)
                dst_scr[hu, 0:win, :] = (pt * dpt_scr[hu, 0:win, :] * scale).astype(BF16)
                return carry

            lax.fori_loop(0, 2 * N_UNITS, pointwise, 0, unroll=4)

            def grads(rho, n, r=r, nb=nb, win=win):
                u = rho * nb + n
                q_rows, k_rows, _ = _unit_rows(r, rho, n, nb)
                qs, ks, dos = _per_head(qb_scr[u]), _per_head(kb_scr[u, 0:win, :]), _per_head(dob_scr[u])

                def both(scr, rows, rhs):
                    return (jnp.dot(scr[(2 * u,) + rows], rhs[0], preferred_element_type=F32)
                            + jnp.dot(scr[(2 * u + 1,) + rows], rhs[1], preferred_element_type=F32))

                dq_s[q_rows, :] += both(ds_scr, (slice(None), slice(0, win)), ks)
                dk_s[k_rows, :] += both(dst_scr, (slice(0, win), slice(None)), qs)
                dv_s[k_rows, :] += both(pt_scr, (slice(0, win), slice(None)), dos)

            _attn_units(r, nb, grads)
        out_ref[0] = dq_s[...].astype(BF16)
        out_ref[1] = dk_s[...].astype(BF16)
        out_ref[2] = dv_s[...].astype(BF16)

    blk = lambda first: pl.BlockSpec((SEQ, 128), lambda g: (0, first + g))
    acc = pltpu.VMEM((SEQ, 128), F32)
    big = (2 * N_UNITS, ATTN_BLOCK, 2 * ATTN_BLOCK)
    big_t = (2 * N_UNITS, 2 * ATTN_BLOCK, ATTN_BLOCK)
    return pl.pallas_call(
        body, out_shape=jax.ShapeDtypeStruct((3, SEQ, D_ATTN), BF16), grid=(4,),
        in_specs=[blk(8), blk(12), blk(16), blk(0), blk(0), blk(0)],
        out_specs=pl.BlockSpec((3, SEQ, 128), lambda g: (0, 0, g)),
        scratch_shapes=[acc, acc, acc, pltpu.VMEM(big, F32), pltpu.VMEM(big, F32), pltpu.VMEM(big, BF16),
                        pltpu.VMEM(big_t, F32), pltpu.VMEM(big_t, F32), pltpu.VMEM(big_t, BF16), pltpu.VMEM(big_t, BF16),
                        pltpu.VMEM((N_UNITS, ATTN_BLOCK, 128), BF16),
                        pltpu.VMEM((N_UNITS, 2 * ATTN_BLOCK, 128), BF16), pltpu.VMEM((N_UNITS, ATTN_BLOCK, 128), BF16)],
        name=name, compiler_params=_cp("parallel"))(proj, proj, proj, do, lse, dd)


def rms_gain_bf16(a, g, name):
    def body(a_ref, g_ref, o_ref):
        aa = a_ref[...]
        o_ref[...] = (aa * lax.rsqrt(_mean(aa * aa) + EPS) * g_ref[...]).astype(BF16)

    w = a.shape[1]
    return pl.pallas_call(
        body, out_shape=jax.ShapeDtypeStruct(a.shape, BF16), grid=(SEQ // ROWS,), in_specs=[_row_spec(w), _vec_spec(w)],
        out_specs=_row_spec(w), name=name, compiler_params=_cp("parallel"))(a, g)


def attn_combine_fwd(outs, lses, gao, name):
    def body(o1, o2, o3, l1, l2, l3, g_ref, att_ref, lse_ref, mix_ref):
        a1, a2, a3 = l1[...], l2[...], l3[...]
        m = jnp.maximum(jnp.maximum(a1, a2), a3)
        w1, w2, w3 = jnp.exp(a1 - m), jnp.exp(a2 - m), jnp.exp(a3 - m)
        den = w1 + w2 + w3
        att = (w1 * o1[...] + w2 * o2[...] + w3 * o3[...]) / den
        att_ref[...] = att
        lse_ref[...] = m + jnp.log(den)
        mix_ref[...] = (att * lax.rsqrt(_mean(att * att) + EPS) * g_ref[...]).astype(BF16)

    rs = _row_spec(D_ATTN)
    f = jax.ShapeDtypeStruct((SEQ, D_ATTN), F32)
    return pl.pallas_call(
        body, out_shape=(f, f, jax.ShapeDtypeStruct((SEQ, D_ATTN), BF16)), grid=(SEQ // ROWS,),
        in_specs=[rs] * 6 + [_vec_spec(D_ATTN)], out_specs=(rs, rs, rs),
        name=name, compiler_params=_cp("parallel"))(*outs, *lses, gao)


def attn_combine_bwd(dmixed, att, gao, name):
    def body(dm_ref, att_ref, g_ref, do_ref, dd_ref, dg_ref):
        first = pl.program_id(0) == 0
        att = att_ref[...]
        r = lax.rsqrt(_mean(att * att) + EPS)
        xn = att * r
        dm = dm_ref[...]
        _acc(dg_ref, _rsum(dm * xn), first)
        dyn = dm * g_ref[...]
        do = r * (dyn - xn * _mean(dyn * xn))
        do_ref[...] = do
        same_head = (jnp.right_shift(lax.broadcasted_iota(jnp.int32, (D_ATTN, D_ATTN), 0), 6)
                     == jnp.right_shift(lax.broadcasted_iota(jnp.int32, (D_ATTN, D_ATTN), 1), 6)).astype(F32)
        dd_ref[...] = jnp.dot(do * att, same_head, preferred_element_type=F32, precision=lax.Precision.HIGHEST)

    rs = _row_spec(D_ATTN)
    return pl.pallas_call(
        body,
        out_shape=(jax.ShapeDtypeStruct((SEQ, D_ATTN), F32), jax.ShapeDtypeStruct((SEQ, D_ATTN), F32),
                   jax.ShapeDtypeStruct((1, D_ATTN), F32)),
        grid=(SEQ // ROWS,), in_specs=[_row_spec(D_ATTN, 1), rs, _vec_spec(D_ATTN)],
        out_specs=(rs, rs, _vec_spec(D_ATTN)), name=name, compiler_params=_cp("arbitrary"))(dmixed, att, gao)


def sum3_bf16(a, b, c, name):
    def body(a_ref, b_ref, c_ref, o_ref):
        o_ref[...] = (a_ref[...] + b_ref[...] + c_ref[...]).astype(BF16)

    w = a.shape[1]
    rs = _row_spec(w)
    return pl.pallas_call(
        body, out_shape=jax.ShapeDtypeStruct(a.shape, BF16), grid=(SEQ // ROWS,), in_specs=[rs, rs, rs], out_specs=rs,
        name=name, compiler_params=_cp("parallel"))(a, b, c)


N_FT = D_FF // FFN_TN


def _ffn_specs():
    per = ROWS // FFN_HALO
    cur_g = pl.BlockSpec((ROWS, FFN_TN), lambda j, i: (i, j))
    cur_v = pl.BlockSpec((ROWS, FFN_TN), lambda j, i: (i, j + N_FT))
    halo_g = pl.BlockSpec((FFN_HALO, FFN_TN), lambda j, i: (jnp.maximum(i * per - 1, 0), j))
    halo_v = pl.BlockSpec((FFN_HALO, FFN_TN), lambda j, i: (jnp.maximum(i * per - 1, 0), j + N_FT))
    w_g = pl.BlockSpec((FFN_K, FFN_TN), lambda j, i: (0, j))
    w_v = pl.BlockSpec((FFN_K, FFN_TN), lambda j, i: (0, j + N_FT))
    b_g = pl.BlockSpec((1, FFN_TN), lambda j, i: (0, j))
    b_v = pl.BlockSpec((1, FFN_TN), lambda j, i: (0, j + N_FT))
    return [cur_g, cur_v, halo_g, halo_v, w_g, w_v, b_g, b_v]


def matmul(a, b, kind, out_dtype, tm, tn, name, b_rows=None):
    stacked = b_rows is not None
    b_shape = (N_DEV * b_rows, b.shape[2]) if stacked else b.shape
    if kind == "nn":
        (m, k), n = a.shape, b_shape[1]
        a_spec = pl.BlockSpec((tm, k), lambda j, i: (i, 0))
        b_spec = pl.BlockSpec((k, tn), lambda j, i: (0, j))
        dims = (((1,), (0,)), ((), ()))
    elif kind == "nt":
        (m, k), n = a.shape, b_shape[0]
        a_spec = pl.BlockSpec((tm, k), lambda j, i: (i, 0))
        b_spec = pl.BlockSpec((tn, k), lambda j, i: (j, 0))
        dims = (((1,), (1,)), ((), ()))
    else:
        (k, m), n = a.shape, b_shape[1]
        a_spec = pl.BlockSpec((k, tm), lambda j, i: (0, i))
        b_spec = pl.BlockSpec((k, tn), lambda j, i: (0, j))
        dims = (((0,), (0,)), ((), ()))
    assert m % tm == 0 and n % tn == 0, (name, m, n, tm, tn)
    if stacked:
        assert b_spec.block_shape[0] == b_shape[0] and kind in ("nn", "nt")
        width = b_spec.block_shape[1]
        b_spec = pl.BlockSpec((N_DEV, b_rows, width), (lambda j, i: (0, 0, j)) if kind == "nn" else (lambda j, i: (0, 0, 0)))

    def body(a_ref, b_ref, o_ref):
        bb = b_ref[...].reshape(b_shape[0], -1) if stacked else b_ref[...]
        o_ref[...] = lax.dot_general(a_ref[...], bb, dims, preferred_element_type=F32).astype(o_ref.dtype)

    return pl.pallas_call(
        body, out_shape=jax.ShapeDtypeStruct((m, n), out_dtype), grid=(n // tn, m // tm),
        in_specs=[a_spec, b_spec], out_specs=pl.BlockSpec((tm, tn), lambda j, i: (i, j)),
        name=name, compiler_params=_cp("parallel", "parallel"))(a, b)


def matmul_halves(a, b, tm, tn, name):
    _, m, k = a.shape
    n = b.shape[1]
    b3 = b.reshape(2, k, n)

    def body(a_ref, b_ref, o_ref):
        o_ref[...] = (jnp.dot(a_ref[0], b_ref[0], preferred_element_type=F32)
                      + jnp.dot(a_ref[1], b_ref[1], preferred_element_type=F32))

    return pl.pallas_call(
        body, out_shape=jax.ShapeDtypeStruct((m, n), F32), grid=(n // tn, m // tm),
        in_specs=[pl.BlockSpec((2, tm, k), lambda j, i: (0, i, 0)), pl.BlockSpec((2, k, tn), lambda j, i: (0, 0, j))],
        out_specs=pl.BlockSpec((tm, tn), lambda j, i: (i, j)), name=name, compiler_params=_cp("parallel", "parallel"))(a, b3)


def matmul_tn_halves(a, b, tm, name):
    _, k, m = a.shape
    n = b.shape[1]

    def body(a_ref, b_ref, o_ref):
        o_ref[0] = lax.dot_general(a_ref[0], b_ref[...], (((0,), (0,)), ((), ())), preferred_element_type=F32).astype(BF16)

    return pl.pallas_call(
        body, out_shape=jax.ShapeDtypeStruct((2, m, n), BF16), grid=(2, m // tm),
        in_specs=[pl.BlockSpec((1, k, tm), lambda h, i: (h, 0, i)), pl.BlockSpec((k, n), lambda h, i: (0, 0))],
        out_specs=pl.BlockSpec((1, tm, n), lambda h, i: (h, i, 0)), name=name,
        compiler_params=_cp("parallel", "parallel"))(a, b).reshape(2 * m, n)


def _row_spec(width, col=0):
    return pl.BlockSpec((ROWS, width), lambda i: (i, col))


def _vec_spec(width, rows=1):
    return pl.BlockSpec((rows, width), lambda i: (0, 0))


def _ffn_shifted(cur_ref, halo_ref, pad, s1, s2):
    i = pl.program_id(1)
    pad[0:FFN_HALO, :] = jnp.where(i > 0, halo_ref[...], 0.0)
    pad[FFN_HALO:, :] = cur_ref[0:FFN_HALO, :]
    for k, dst in ((1, s1), (2, s2)):
        dst[0:FFN_HALO, :] = pad[pl.ds(FFN_HALO - k, FFN_HALO), :]
        dst[FFN_HALO:, :] = cur_ref[pl.ds(FFN_HALO - k, ROWS - FFN_HALO), :]


def _ffn_conv(rs, cur_ref, s1, s2, w_ref, b_ref):
    return b_ref[...] + w_ref[0:1, :] * s2[rs, :] + w_ref[1:2, :] * s1[rs, :] + w_ref[2:3, :] * cur_ref[rs, :]


def ffn_act_fwd(up0, wf, bf, name):
    def body(g_ref, v_ref, gh_ref, vh_ref, wg_ref, wv_ref, bg_ref, bv_ref, act_ref, pad, g1, g2, v1, v2):
        _ffn_shifted(g_ref, gh_ref, pad, g1, g2)
        _ffn_shifted(v_ref, vh_ref, pad, v1, v2)

        def chunk(rs):
            gate = _ffn_conv(rs, g_ref, g1, g2, wg_ref, bg_ref)
            val = _ffn_conv(rs, v_ref, v1, v2, wv_ref, bv_ref)
            act_ref[rs, :] = (gate * _sig(gate) * val).astype(BF16)

        _for_chunks(chunk)

    tile = pltpu.VMEM((ROWS, FFN_TN), F32)
    return pl.pallas_call(
        body, out_shape=jax.ShapeDtypeStruct((SEQ, D_FF), BF16), grid=(N_FT, SEQ // ROWS),
        in_specs=_ffn_specs(), out_specs=pl.BlockSpec((ROWS, FFN_TN), lambda j, i: (i, j)),
        scratch_shapes=[pltpu.VMEM((2 * FFN_HALO, FFN_TN), F32), tile, tile, tile, tile],
        name=name, compiler_params=_cp("parallel", "parallel"))(up0, up0, up0, up0, wf, wf, bf, bf)


def ffn_bwd(up0, dact, wf, bf, name):
    per = ROWS // FFN_HALO
    last = SEQ // FFN_HALO - 1

    def body(g_ref, v_ref, gh_ref, vh_ref, wg_ref, wv_ref, bg_ref, bv_ref, da_ref, gn_ref, vn_ref, dan_ref,
             out_ref, dbg_ref, dbv_ref, dwg_ref, dwv_ref, pad, g1, g2, v1, v2, dgp, dvp, acc):
        i = pl.program_id(1)
        first = i == 0
        _ffn_shifted(g_ref, gh_ref, pad, g1, g2)
        _ffn_shifted(v_ref, vh_ref, pad, v1, v2)
        acc[...] = jnp.zeros_like(acc)

        def grads(gate, val, da):
            s = _sig(gate)
            return da * val * (s * (1.0 + gate * (1.0 - s))), da * (gate * s)

        def chunk(rs):
            gate = _ffn_conv(rs, g_ref, g1, g2, wg_ref, bg_ref)
            val = _ffn_conv(rs, v_ref, v1, v2, wv_ref, bv_ref)
            dgate, dval = grads(gate, val, da_ref[rs, :])
            dgp[rs, :] = dgate
            dvp[rs, :] = dval
            acc[0] += dgate
            acc[1] += dval
            for t, (sg, sv) in enumerate(((g2, v2), (g1, v1), (g_ref, v_ref))):
                acc[2 + t] += dgate * sg[rs, :]
                acc[5 + t] += dval * sv[rs, :]

        _for_chunks(chunk)
        _acc(dbg_ref, _rsum(acc[0]), first)
        _acc(dbv_ref, _rsum(acc[1]), first)
        _acc(dwg_ref, jnp.concatenate([_rsum(acc[2 + t]) for t in range(FFN_K)], axis=0), first)
        _acc(dwv_ref, jnp.concatenate([_rsum(acc[5 + t]) for t in range(FFN_K)], axis=0), first)

        def conv_next(cur_ref, nxt_ref, w_ref, b_ref):
            pad[0:FFN_HALO, :] = cur_ref[ROWS - FFN_HALO:, :]
            pad[FFN_HALO:, :] = nxt_ref[...]
            return (b_ref[...] + w_ref[0:1, :] * pad[pl.ds(FFN_HALO - 2, FFN_HALO), :]
                    + w_ref[1:2, :] * pad[pl.ds(FFN_HALO - 1, FFN_HALO), :] + w_ref[2:3, :] * nxt_ref[...])

        gate_n = conv_next(g_ref, gn_ref, wg_ref, bg_ref)
        val_n = conv_next(v_ref, vn_ref, wv_ref, bv_ref)
        dgate_n, dval_n = grads(gate_n, val_n, dan_ref[...])
        inside = i < SEQ // ROWS - 1
        dgp[ROWS:, :] = jnp.where(inside, dgate_n, 0.0)
        dvp[ROWS:, :] = jnp.where(inside, dval_n, 0.0)

        for half, (dp, s1, s2, w_ref) in enumerate(((dgp, g1, g2, wg_ref), (dvp, v1, v2, wv_ref))):
            s1[...] = dp[pl.ds(1, ROWS), :]
            s2[...] = dp[pl.ds(2, ROWS), :]

            def back(rs, dp=dp, s1=s1, s2=s2, w_ref=w_ref, half=half):
                out_ref[half, rs, :] = (w_ref[2:3, :] * dp[rs, :] + w_ref[1:2, :] * s1[rs, :]
                                        + w_ref[0:1, :] * s2[rs, :]).astype(BF16)

            _for_chunks(back)

    tile = pltpu.VMEM((ROWS, FFN_TN), F32)
    ext = pltpu.VMEM((ROWS + FFN_HALO, FFN_TN), F32)
    vec = jax.ShapeDtypeStruct((1, D_FF), F32)
    taps = jax.ShapeDtypeStruct((FFN_K, D_FF), F32)
    cur = pl.BlockSpec((ROWS, FFN_TN), lambda j, i: (i, j))
    nxt = lambda off: pl.BlockSpec((FFN_HALO, FFN_TN), lambda j, i: (jnp.minimum((i + 1) * per, last), j + off))
    vs = pl.BlockSpec((1, FFN_TN), lambda j, i: (0, j))
    ts = pl.BlockSpec((FFN_K, FFN_TN), lambda j, i: (0, j))
    return pl.pallas_call(
        body, out_shape=(jax.ShapeDtypeStruct((2, SEQ, D_FF), BF16), vec, vec, taps, taps), grid=(N_FT, SEQ // ROWS),
        in_specs=_ffn_specs() + [cur, nxt(0), nxt(N_FT), nxt(0)],
        out_specs=(pl.BlockSpec((2, ROWS, FFN_TN), lambda j, i: (0, i, j)), vs, vs, ts, ts),
        scratch_shapes=[pltpu.VMEM((2 * FFN_HALO, FFN_TN), F32), tile, tile, tile, tile, ext, ext,
                        pltpu.VMEM((2 + 2 * FFN_K, SUB, FFN_TN), F32)],
        name=name, compiler_params=_cp("parallel", "arbitrary"))(up0, up0, up0, up0, wf, wf, bf, bf, dact, up0, up0, dact)


def ada_fwd(c_all, w_ada, b_cols, name):
    def body(c_ref, w_ref, b_ref, o_ref):
        cc = c_ref[...]
        sc = (cc * _sig(cc)).astype(BF16)
        o_ref[...] = jnp.dot(sc, w_ref[...].astype(BF16), preferred_element_type=F32) + b_ref[...]

    return pl.pallas_call(body, out_shape=jax.ShapeDtypeStruct((N_DEV, w_ada.shape[1]), F32), name=name,
                          compiler_params=_cp())(c_all, w_ada, b_cols)


def _adam(w, g, m, v):
    m = ADAM_B1 * m + (1.0 - ADAM_B1) * g
    v = ADAM_B2 * v + (1.0 - ADAM_B2) * (g * g)
    m_hat = m / (1.0 - ADAM_B1 ** ADAM_STEP)
    v_hat = v / (1.0 - ADAM_B2 ** ADAM_STEP)
    delta = -ADAM_LR * (m_hat / (jnp.sqrt(v_hat) + ADAM_EPS) + ADAM_WD * w)
    return delta, m, v


def ada_bwd_adamw(c_all_t, dmod_cols, w, m, v, name):
    rows, cols = w.shape
    tr = 256

    def body(ct_ref, dm_ref, w_ref, m_ref, v_ref, g_ref, d_ref, nm_ref, nv_ref):
        def chunk(rs):
            ct = ct_ref[rs, :]
            sc = ct * _sig(ct)
            g = sc[:, 0:1] * dm_ref[0:1, :]
            for b in range(1, N_DEV):
                g = g + sc[:, b:b + 1] * dm_ref[b:b + 1, :]
            g_ref[rs, :] = g
            d_ref[rs, :], nm_ref[rs, :], nv_ref[rs, :] = _adam(w_ref[rs, :], g, m_ref[rs, :], v_ref[rs, :])

        _for_chunks(chunk, 2, tr)

    blk = pl.BlockSpec((tr, cols), lambda i: (i, 0))
    shp = jax.ShapeDtypeStruct((rows, cols), F32)
    return pl.pallas_call(
        body, out_shape=(shp, shp, shp, shp), grid=(rows // tr,),
        in_specs=[pl.BlockSpec((tr, N_DEV), lambda i: (i, 0)), pl.BlockSpec((N_DEV, cols), lambda i: (0, 0)), blk, blk, blk],
        out_specs=(blk, blk, blk, blk), name=name, compiler_params=_cp("parallel"))(c_all_t, dmod_cols, w, m, v)


def sum_adamw(parts, own, w, m, v, tr, name):
    n_parts, rows, cols = parts.shape

    def body(*refs):
        if own is None:
            p_ref, w_ref, m_ref, v_ref, g_ref, d_ref, nm_ref, nv_ref = refs
        else:
            p_ref, own_ref, w_ref, m_ref, v_ref, g_ref, d_ref, nm_ref, nv_ref = refs

        def chunk(rs):
            g = (p_ref[0, rs, :] if own is None else own_ref[rs, :]).astype(F32)
            for k in range(1, n_parts):
                g = g + p_ref[k, rs, :].astype(F32)
            g_ref[rs, :] = g
            d_ref[rs, :], nm_ref[rs, :], nv_ref[rs, :] = _adam(w_ref[rs, :], g, m_ref[rs, :], v_ref[rs, :])

        _for_chunks(chunk, 2 if tr > SUB else 1, tr)

    blk = pl.BlockSpec((tr, cols), lambda i: (i, 0))
    shp = jax.ShapeDtypeStruct((rows, cols), F32)
    args = [parts] + ([] if own is None else [own]) + [w, m, v]
    return pl.pallas_call(
        body, out_shape=(shp, shp, shp, shp), grid=(rows // tr,),
        in_specs=[pl.BlockSpec((n_parts, tr, cols), lambda i: (0, i, 0))] + [blk] * (len(args) - 1),
        out_specs=(blk, blk, blk, blk), name=name, compiler_params=_cp("parallel"))(*args)


MESH = pl.DeviceIdType.MESH
ANY = pl.BlockSpec(memory_space=pl.ANY)


def all_gather(block, name, after=None):
    extra = () if after is None else (after,)

    def body(x_ref, *refs):
        out_ref, send_sems, recv_sems, local_sem = refs[len(extra):]
        x, y, c = lax.axis_index("x"), lax.axis_index("y"), lax.axis_index("c")
        me, sibling = (x, y, c), (x, y, 1 - c)
        chips = [(1 - x, y), (x, 1 - y), (1 - x, 1 - y)]

        def slot(px, py, pc):
            return out_ref.at[4 * px + 2 * py + pc]

        def copy(k, blk, to, src=None):
            return pltpu.make_async_remote_copy(
                src_ref=slot(*blk) if src is None else src, dst_ref=slot(*blk),
                send_sem=send_sems.at[k], recv_sem=recv_sems.at[k], device_id=to, device_id_type=MESH)

        mine = pltpu.make_async_copy(x_ref, slot(*me), local_sem)
        mine.start()
        first = [copy(0, me, sibling, src=x_ref)]
        first += [copy(1 + j, me, (*chip, c), src=x_ref) for j, chip in enumerate(chips)]
        for cp in first:
            cp.start()
        passed = [copy(4 + j, (*chip, c), sibling) for j, chip in enumerate(chips)]
        for j, chip in enumerate(chips):
            copy(1 + j, (*chip, c), me).wait_recv()
            passed[j].start()
        copy(0, sibling, me).wait_recv()
        for j, chip in enumerate(chips):
            copy(4 + j, (*chip, 1 - c), me).wait_recv()
        for cp in first + passed:
            cp.wait_send()
        mine.wait()

    return pl.pallas_call(
        body, out_shape=jax.ShapeDtypeStruct((N_DEV,) + block.shape, block.dtype), in_specs=[ANY] * (1 + len(extra)), out_specs=ANY,
        scratch_shapes=[pltpu.SemaphoreType.DMA((7,)), pltpu.SemaphoreType.DMA((7,)), pltpu.SemaphoreType.DMA],
        name=name)(block, *extra)


HBM = pl.BlockSpec(memory_space=pltpu.HBM)
SEM = pl.BlockSpec(memory_space=pltpu.SEMAPHORE)
EFFECT = pltpu.SideEffectType.DATAFLOW_SIDE_EFFECTING


def _peer_copies(src_ref, land_ref, send_sems, recv_sems, gather):
    x, y, c = lax.axis_index("x"), lax.axis_index("y"), lax.axis_index("c")
    me = 4 * x + 2 * y + c
    copies = []
    for k in range(1, N_DEV):
        px = 1 - x if k & 4 else x
        py = 1 - y if k & 2 else y
        pc = 1 - c if k & 1 else c
        copies.append(pltpu.make_async_remote_copy(
            src_ref=src_ref if gather else src_ref.at[4 * px + 2 * py + pc],
            dst_ref=land_ref.at[me] if gather else land_ref.at[k],
            send_sem=send_sems.at[k - 1], recv_sem=recv_sems.at[k - 1], device_id=(px, py, pc), device_id_type=MESH))
    return copies


def exchange_start(srcs, gather, name, after=None):
    n = len(srcs)
    land_shapes = [(N_DEV,) + src.shape if gather else src.shape for src in srcs]
    extra = () if after is None else (after,)

    def body(*refs):
        src_refs, land_refs = refs[0:n], refs[n:2 * n]
        outs = refs[2 * n + len(extra):]
        for k in range(n):
            for cp in _peer_copies(src_refs[k], land_refs[k], outs[4 * k], outs[4 * k + 1], gather):
                cp.start()
        token = outs[4 * n]
        token[...] = jnp.zeros_like(token)

    out_shape, out_specs, aliases = [], [], {}
    for k, src in enumerate(srcs):
        out_shape += [pltpu.SemaphoreType.DMA((N_DEV - 1,)), pltpu.SemaphoreType.DMA((N_DEV - 1,)),
                      pltpu.HBM(src.shape, src.dtype), pltpu.HBM(land_shapes[k], src.dtype)]
        out_specs += [SEM, SEM, HBM, HBM]
        aliases[k] = 4 * k + 2
        aliases[n + k] = 4 * k + 3
    out_shape.append(jax.ShapeDtypeStruct((8, 128), F32))
    out_specs.append(pl.BlockSpec(memory_space=pltpu.VMEM))
    res = pl.pallas_call(
        body, name=name, out_shape=tuple(out_shape), in_specs=(HBM,) * (2 * n) + (ANY,) * len(extra),
        out_specs=tuple(out_specs), input_output_aliases=aliases,
        compiler_params=pltpu.CompilerParams(has_side_effects=EFFECT),
    )(*[pltpu.with_memory_space_constraint(src, pltpu.HBM) for src in srcs],
      *[pltpu.with_memory_space_constraint(lax.empty(shp, src.dtype), pltpu.HBM) for shp, src in zip(land_shapes, srcs)],
      *extra)
    return [tuple(res[4 * k:4 * k + 4]) for k in range(n)], res[4 * n][0, 0]


def _stage1_peer(i):
    x, y, c = lax.axis_index("x"), lax.axis_index("y"), lax.axis_index("c")
    if i == 0:
        return (x, y, 1 - c)
    return (1 - x if i & 1 else x, 1 - y if i & 2 else y, c)


def _slot_of(peer):
    return 4 * peer[0] + 2 * peer[1] + peer[2]


def _stage1_copy(i, src_ref, land_ref, send_sems, recv_sems):
    me = _slot_of((lax.axis_index("x"), lax.axis_index("y"), lax.axis_index("c")))
    return pltpu.make_async_remote_copy(src_ref=src_ref, dst_ref=land_ref.at[me], send_sem=send_sems.at[i],
                                        recv_sem=recv_sems.at[i], device_id=_stage1_peer(i), device_id_type=MESH)


def _stage2_copy(j, land_ref, send_sems, recv_sems):
    slot = _slot_of(_stage1_peer(j + 1))
    return pltpu.make_async_remote_copy(src_ref=land_ref.at[slot], dst_ref=land_ref.at[slot], send_sem=send_sems.at[j],
                                        recv_sem=recv_sems.at[j], device_id=_stage1_peer(0), device_id_type=MESH)


def gather2_start(srcs, name, after=None):
    n = len(srcs)
    extra = () if after is None else (after,)

    def body(*refs):
        src_refs, land_refs = refs[0:n], refs[n:2 * n]
        outs = refs[2 * n + len(extra):]
        for k in range(n):
            for i in range(4):
                _stage1_copy(i, src_refs[k], land_refs[k], outs[4 * k], outs[4 * k + 1]).start()
        outs[4 * n][...] = jnp.zeros((8, 128), F32)

    out_shape, out_specs, aliases = [], [], {}
    for k, src in enumerate(srcs):
        out_shape += [pltpu.SemaphoreType.DMA((4,)), pltpu.SemaphoreType.DMA((4,)),
                      pltpu.HBM(src.shape, src.dtype), pltpu.HBM((N_DEV,) + src.shape, src.dtype)]
        out_specs += [SEM, SEM, HBM, HBM]
        aliases[k] = 4 * k + 2
        aliases[n + k] = 4 * k + 3
    out_shape.append(jax.ShapeDtypeStruct((8, 128), F32))
    out_specs.append(pl.BlockSpec(memory_space=pltpu.VMEM))
    res = pl.pallas_call(
        body, name=name, out_shape=tuple(out_shape), in_specs=(HBM,) * (2 * n) + (ANY,) * len(extra),
        out_specs=tuple(out_specs), input_output_aliases=aliases,
        compiler_params=pltpu.CompilerParams(has_side_effects=EFFECT),
    )(*[pltpu.with_memory_space_constraint(src, pltpu.HBM) for src in srcs],
      *[pltpu.with_memory_space_constraint(lax.empty((N_DEV,) + src.shape, src.dtype), pltpu.HBM) for src in srcs], *extra)
    return [dict(send1=res[4 * k], recv1=res[4 * k + 1], src=res[4 * k + 2], land=res[4 * k + 3]) for k in range(n)], \
        res[4 * n][0, 0]


def gather2_pass(handles, after, name):
    n = len(handles)

    def body(*refs):
        src_refs, land_refs, recv1 = refs[0:n], refs[n:2 * n], refs[2 * n:3 * n]
        outs = refs[3 * n + 1:]
        for k in range(n):
            for j in range(3):
                _stage1_copy(j + 1, src_refs[k], land_refs[k], recv1[k], recv1[k]).wait_recv()
                _stage2_copy(j, land_refs[k], outs[3 * k], outs[3 * k + 1]).start()

    out_shape, out_specs, aliases = [], [], {}
    for k, h in enumerate(handles):
        out_shape += [pltpu.SemaphoreType.DMA((3,)), pltpu.SemaphoreType.DMA((3,)), pltpu.HBM(h["land"].shape, h["land"].dtype)]
        out_specs += [SEM, SEM, HBM]
        aliases[n + k] = 3 * k + 2
    res = pl.pallas_call(
        body, name=name, out_shape=tuple(out_shape), in_specs=(HBM,) * (2 * n) + (SEM,) * n + (ANY,),
        out_specs=tuple(out_specs), input_output_aliases=aliases,
        compiler_params=pltpu.CompilerParams(has_side_effects=EFFECT),
    )(*[h["src"] for h in handles], *[h["land"] for h in handles], *[h["recv1"] for h in handles], after)
    return [dict(h, send2=res[3 * k], recv2=res[3 * k + 1], land=res[3 * k + 2]) for k, h in enumerate(handles)]


def gather2_wait(h, after, name):
    def body(src_ref, land_ref, send1, recv1, send2, recv2, after_ref, src_dead, got_ref):
        for i in range(4):
            _stage1_copy(i, src_ref, land_ref, send1, recv1).wait_send()
        _stage1_copy(0, src_ref, land_ref, send1, recv1).wait_recv()
        for j in range(3):
            cp = _stage2_copy(j, land_ref, send2, recv2)
            cp.wait_send()
            cp.wait_recv()

    return pl.pallas_call(
        body, name=name,
        out_shape=(pltpu.HBM(h["src"].shape, h["src"].dtype), pltpu.HBM(h["land"].shape, h["land"].dtype)),
        in_specs=(HBM, HBM, SEM, SEM, SEM, SEM, ANY), out_specs=(HBM, HBM), input_output_aliases={0: 0, 1: 1},
        compiler_params=pltpu.CompilerParams(has_side_effects=EFFECT),
    )(h["src"], h["land"], h["send1"], h["recv1"], h["send2"], h["recv2"], after)[1]


def exchange_wait(handles, after, gather, name):
    send_sems, recv_sems, src_thru, land_thru = handles

    def body(src_ref, land_ref, send_sems, recv_sems, after_ref, src_dead, got_ref):
        for cp in _peer_copies(src_ref, land_ref, send_sems, recv_sems, gather):
            cp.wait_send()
            cp.wait_recv()

    return pl.pallas_call(
        body, name=name,
        out_shape=(pltpu.HBM(src_thru.shape, src_thru.dtype), pltpu.HBM(land_thru.shape, land_thru.dtype)),
        in_specs=(HBM, HBM, SEM, SEM, ANY), out_specs=(HBM, HBM), input_output_aliases={0: 0, 1: 1},
        compiler_params=pltpu.CompilerParams(has_side_effects=EFFECT),
    )(src_thru, land_thru, send_sems, recv_sems, after)[1]


def _to_pattern(a, r):
    return a.reshape(SEQ // r, r * a.shape[1])


def local_step(x, tgt, mod, get_w, put_grad, wc, wf, g_mix, bc, lg, lb, gco, gao, g_ffn, bf, g_fin):
    w_in = get_w("w_in", mod)
    proj, h1 = rms_mod_matmul(x, g_mix, mod, 0, 1, w_in, D_IN // N_DEV, "proj_fwd")
    mix_a, u1 = conv_module_fwd(proj, wc, bc, lg, lb, gco, "conv_module_fwd")
    att, lse = attn_fwd_all(proj, "attn_fwd")
    w_out = get_w("w_out", att)
    y1, mixed = norm_concat_matmul(mix_a, att, gao, w_out, "out_proj_fwd")
    w_up = get_w("w_up", y1)
    up0, x1, h2 = resid_rms_mod_matmul(x, y1, g_ffn, mod, 2, 3, 4, w_up, "up_fwd")
    act = ffn_act_fwd(up0, wf, bf, "ffn_act_fwd")
    w_down = get_w("w_down", act)
    loss_t, dx2, dy2, d_gfin, d_gaf = matmul_loss_bwd(act, w_down, x1, tgt, g_fin, mod, 5, "down_fwd_loss")
    dact = matmul(dy2, w_down, "nt", F32, 512, FFN_TN, "down_bwd_x")
    dw_down = matmul(act, dy2, "tn", BF16, 256, D_MODEL, "down_bwd_w")
    dup0, dbf_g, dbf_v, dwf_g, dwf_v = ffn_bwd(up0, dact, wf + put_grad("w_down", dw_down), bf, "ffn_bwd")
    dw_up = matmul_tn_halves(dup0, h2, 256, "up_bwd_w")
    dx1, d_shf, d_scf, d_gffn, dy1, d_gam = matmul_rms_mod_bwd(
        dup0, w_up, x1, dx2, g_ffn + put_grad("w_up", dw_up), mod, 4, y1, 2, "up_bwd_x")
    dw_out = matmul(mixed, dy1, "tn", BF16, 256, D_MODEL, "out_proj_bwd_w")
    dmixed, do, dd, d_gao = matmul_combine_bwd(dy1, w_out, att, gao + put_grad("w_out", dw_out), "out_proj_bwd_x")
    dqkv = attn_bwd_all(proj, do, lse, dd, "attn_bwd")
    du1, d_gco, d_lg, d_lb, d_bc, d_wc = conv_module_bwd_a(proj, u1, dmixed, lg, lb, gco, "conv_module_bwd_a")
    dproj_a = conv_module_bwd_b(proj, du1, wc, "conv_module_bwd_b")
    dproj = jnp.concatenate([dproj_a, dqkv[0], dqkv[1], dqkv[2]], axis=1)
    dw_in = matmul(dproj, h1, "tn", BF16, 512, D_MODEL, "proj_bwd_w")
    dx, d_shm, d_scm, d_gmix = matmul_rms_mod_bwd(
        dproj, w_in, x, dx1, g_mix + put_grad("w_in", dw_in), mod, 1, None, 0, "proj_bwd_x", b_rows=D_IN // N_DEV)
    dmod = jnp.concatenate([d_shm, d_scm, d_gam, d_shf, d_scf, d_gaf], axis=1)
    small = dict(g_norm_mix=d_gmix, b_conv_dw=d_bc, ln_conv_g=d_lg, ln_conv_b=d_lb, g_conv_out=d_gco, g_attn_out=d_gao,
                 g_norm_ffn=d_gffn, b_ffn_dw=jnp.concatenate([dbf_g, dbf_v], axis=1), g_final=d_gfin,
                 w_conv_dw=d_wc, w_ffn_dw=jnp.concatenate([dwf_g, dwf_v], axis=1), dmod=dmod, loss=loss_t[0:1, 0:1])
    return dx, small


def _padw(a, width):
    return jnp.pad(a, ((0, 0), (0, width - a.shape[1])))


def pack_small(t):
    wide = jnp.concatenate([_padw(t["dmod"], PACK_W), _padw(t["b_ffn_dw"], PACK_W), _padw(t["w_ffn_dw"], PACK_W),
                            _padw(t["loss"], PACK_W), jnp.zeros((2, PACK_W), F32)], axis=0)
    z512 = jnp.zeros((1, 512), F32)
    narrow = jnp.concatenate([
        t["g_norm_mix"], t["g_norm_ffn"], t["g_final"],
        jnp.concatenate([t["b_conv_dw"], t["ln_conv_g"]], axis=1),
        jnp.concatenate([t["ln_conv_b"], t["g_conv_out"]], axis=1),
        jnp.concatenate([t["g_attn_out"], z512], axis=1),
        jnp.zeros((2, 1024), F32),
        jnp.pad(t["w_conv_dw"], ((0, 1), (0, 0))).reshape(16, 1024)], axis=0)
    return jnp.concatenate([wide, narrow.reshape(4, PACK_W), jnp.zeros((4, PACK_W), F32)], axis=0)


_NARROW = lambda k, off=0: (8 + k // 6, (k % 6) * 1024 + off)
PACKED_AT = dict(
    b_ada=(0, 0, N_MOD * D_MODEL), b_ffn_dw=(1, 0, 2 * D_FF),
    g_norm_mix=_NARROW(0) + (D_MODEL,), g_norm_ffn=_NARROW(1) + (D_MODEL,), g_final=_NARROW(2) + (D_MODEL,),
    b_conv_dw=_NARROW(3) + (D_CONV,), ln_conv_g=_NARROW(3, 512) + (D_CONV,),
    ln_conv_b=_NARROW(4) + (D_CONV,), g_conv_out=_NARROW(4, 512) + (D_CONV,), g_attn_out=_NARROW(5) + (D_ATTN,))
SMALL_ORDER = list(PACKED_AT)


def small_adamw(parts, wmv, name):
    def body(*refs):
        p_ref = refs[0]
        ins = refs[1:1 + 3 * len(SMALL_ORDER)]
        outs = refs[1 + 3 * len(SMALL_ORDER):]
        g = p_ref[0]
        for k in range(1, N_DEV):
            g = g + p_ref[k]
        for i, n in enumerate(SMALL_ORDER):
            row, lane, width = PACKED_AT[n]
            gp = g[row:row + 1, lane:lane + width]
            w_ref, m_ref, v_ref = ins[3 * i:3 * i + 3]
            g_ref, d_ref, nm_ref, nv_ref = outs[4 * i:4 * i + 4]
            g_ref[...] = gp
            d_ref[...], nm_ref[...], nv_ref[...] = _adam(w_ref[...], gp, m_ref[...], v_ref[...])
        wc_ref, wf_ref, loss_ref = outs[4 * len(SMALL_ORDER):]
        for j in range(CONV_K):
            row, lane = _NARROW(8 + j // 2, (j % 2) * 512)
            wc_ref[j:j + 1, :] = g[row:row + 1, lane:lane + D_CONV]
        wf_ref[...] = g[2:2 + FFN_K, 0:2 * D_FF]
        loss_ref[...] = jnp.broadcast_to(g[5:6, 0:1], (8, 128))

    args, out_shape = [parts], []
    for n in SMALL_ORDER:
        args += list(wmv[n])
        out_shape += [jax.ShapeDtypeStruct(wmv[n][0].shape, F32)] * 4
    out_shape += [jax.ShapeDtypeStruct((CONV_K, D_CONV), F32), jax.ShapeDtypeStruct((FFN_K, 2 * D_FF), F32),
                  jax.ShapeDtypeStruct((8, 128), F32)]
    res = pl.pallas_call(body, out_shape=tuple(out_shape), name=name, compiler_params=_cp())(*args)
    per = {n: tuple(res[4 * i:4 * i + 4]) for i, n in enumerate(SMALL_ORDER)}
    return per, res[-3], res[-2], res[-1][0, 0]


def shard_adamw(items, name):
    def body(*refs):
        ins, outs = refs[:4 * len(items)], refs[4 * len(items):]
        for i in range(len(items)):
            g_ref, w_ref, m_ref, v_ref = ins[4 * i:4 * i + 4]
            og_ref, d_ref, nm_ref, nv_ref = outs[4 * i:4 * i + 4]
            og_ref[...] = g_ref[...]
            d_ref[...], nm_ref[...], nv_ref[...] = _adam(w_ref[...], g_ref[...], m_ref[...], v_ref[...])

    args = [a for item in items for a in item]
    out_shape = tuple(jax.ShapeDtypeStruct(item[1].shape, F32) for item in items for _ in range(4))
    res = pl.pallas_call(body, out_shape=out_shape, name=name, compiler_params=_cp())(*args)
    return [tuple(res[4 * i:4 * i + 4]) for i in range(len(items))]


def _shard(full, n_cols, me):
    return lax.dynamic_slice(full, (0, me * n_cols), (full.shape[0], n_cols))


WEIGHTS = ["w_ada", "b_ada", "g_norm_mix", "w_in", "w_conv_dw", "b_conv_dw", "ln_conv_g", "ln_conv_b", "g_conv_out",
           "g_attn_out", "w_out", "g_norm_ffn", "w_up", "w_ffn_dw", "b_ffn_dw", "w_down", "g_final"]
SMALL_REPLICATED = ["g_norm_mix", "b_conv_dw", "ln_conv_g", "ln_conv_b", "g_conv_out", "g_attn_out", "g_norm_ffn",
                    "b_ffn_dw", "g_final"]


def kernel(x, c, w_ada, b_ada, g_norm_mix, w_in, w_conv_dw, b_conv_dw, ln_conv_g, ln_conv_b, g_conv_out, g_attn_out, w_out, g_norm_ffn, w_up, w_ffn_dw, b_ffn_dw, w_down, g_final, loss_target, m_w_ada, m_b_ada, m_g_norm_mix, m_w_in, m_w_conv_dw, m_b_conv_dw, m_ln_conv_g, m_ln_conv_b, m_g_conv_out, m_g_attn_out, m_w_out, m_g_norm_ffn, m_w_up, m_w_ffn_dw, m_b_ffn_dw, m_w_down, m_g_final, v_w_ada, v_b_ada, v_g_norm_mix, v_w_in, v_w_conv_dw, v_b_conv_dw, v_ln_conv_g, v_ln_conv_b, v_g_conv_out, v_g_attn_out, v_w_out, v_g_norm_ffn, v_w_up, v_w_ffn_dw, v_b_ffn_dw, v_w_down, v_g_final):
    args = dict(locals())
    me = 4 * lax.axis_index("x") + 2 * lax.axis_index("y") + lax.axis_index("c")

    def flat(name, prefix=""):
        a = args[prefix + name]
        return a.reshape(a.shape[-2] if a.ndim > 1 else 1, a.shape[-1])

    def flat_t(name, prefix=""):
        return args[prefix + name][0].T

    n_in, n_up, r_out, r_down = w_in.shape[2], w_up.shape[2], w_out.shape[1], w_down.shape[1]
    n_ada, n_wc, n_wf = w_ada.shape[2], w_conv_dw.shape[2], w_ffn_dw.shape[2]
    taps_c = jnp.pad(flat("w_conv_dw").reshape(1, CONV_K * n_wc), ((0, 0), (0, 2 * D_MODEL - CONV_K * n_wc)))
    taps_f = jnp.pad(flat("w_ffn_dw").reshape(1, FFN_K * n_wf), ((0, 0), (0, 3 * D_MODEL - FFN_K * n_wf)))
    first = jnp.concatenate([c, taps_c.reshape(2, D_MODEL), taps_f.reshape(3, D_MODEL), jnp.zeros((2, D_MODEL), F32)], axis=0)
    w_in_block = flat_t("w_in").astype(BF16)
    hi = lax.reduce_precision(first, 8, 7)
    mid = lax.reduce_precision(first - hi, 8, 7)
    low = lax.reduce_precision(first - hi - mid, 8, 7)
    terms = jnp.concatenate([hi, mid, low, jnp.zeros((8, D_MODEL), F32)], axis=0).astype(BF16)
    first_block = all_gather(jnp.concatenate([w_in_block, terms], axis=0), "gather_c_taps_w_in")
    terms = first_block[:, n_in:n_in + 24, :].astype(F32).reshape(N_DEV, 3, 8, D_MODEL)
    first_all = (terms[:, 0] + terms[:, 1]) + terms[:, 2]
    c_all = first_all[:, 0, :]
    wc_full = first_all[:, 1:3, :].reshape(N_DEV, 2 * D_MODEL)[:, :CONV_K * n_wc].reshape(N_DEV, CONV_K, n_wc)
    wc_full = wc_full.transpose(1, 0, 2).reshape(CONV_K, D_CONV)
    wf_full = first_all[:, 3:6, :].reshape(N_DEV, 3 * D_MODEL)[:, :FFN_K * n_wf].reshape(N_DEV, FFN_K, n_wf)
    wf_full = wf_full.transpose(1, 0, 2).reshape(FFN_K, 2 * D_FF)
    mod_cols = ada_fwd(c_all, flat("w_ada"), _shard(flat("b_ada"), n_ada, me), "ada_fwd")
    mod_all = all_gather(mod_cols, "gather_mod")
    mod = lax.dynamic_index_in_dim(mod_all, me, axis=1, keepdims=False).reshape(N_MOD, D_MODEL)
    mod = jnp.pad(mod, ((0, 2), (0, 0)))

    order = ("w_out", "w_up", "w_down")
    blocks = dict(w_up=flat_t("w_up").astype(BF16), w_out=flat("w_out").astype(BF16), w_down=flat("w_down").astype(BF16))
    handles, tok = gather2_start([blocks[name] for name in order], "gather_weights_start", mod_all)
    gathers = dict(zip(order, handles))
    mod = mod + tok

    def gathered(name, after):
        if "send2" not in gathers[name]:
            gathers.update(zip(order, gather2_pass([gathers[w] for w in order], after, "gather_weights_pass")))
        land = gather2_wait(gathers[name], after, f"gather_{name}_wait")
        return lax.dynamic_update_index_in_dim(land, blocks[name], me, axis=0)

    def get_w(name, after):
        if name == "w_in":
            return first_block
        return gathered(name, after).reshape(-1, D_MODEL)

    exchanges, own = {}, {}

    def put_grad(name, dw, after=None):
        dev_major = dw.reshape(N_DEV, -1, D_MODEL)
        own[name] = lax.dynamic_index_in_dim(dev_major, me, axis=0, keepdims=False)
        (exchanges[name],), token = exchange_start([dev_major], False, f"exchange_{name}_start", after)
        return token

    grad_x, small = local_step(
        x[0], loss_target[0], mod, get_w, put_grad, wc_full, wf_full,
        flat("g_norm_mix"), flat("b_conv_dw"), flat("ln_conv_g"), flat("ln_conv_b"), flat("g_conv_out"),
        flat("g_attn_out"), flat("g_norm_ffn"), flat("b_ffn_dw"), flat("g_final"))

    out = {}

    def finish(name, tr, after):
        parts = exchange_wait(exchanges[name], after, False, f"exchange_{name}_wait")
        if name in ("w_in", "w_up"):
            res = sum_adamw(parts, own[name], flat_t(name), flat_t(name, "m_"), flat_t(name, "v_"), tr, "adamw_" + name)
            out[name] = tuple(r.T for r in res)
        else:
            res = out[name] = sum_adamw(parts, own[name], flat(name), flat(name, "m_"), flat(name, "v_"), tr, "adamw_" + name)
        return res[0]

    after = finish("w_down", r_down, grad_x)
    after = finish("w_up", n_up // 2, after)
    after = finish("w_out", r_out, after)
    after = finish("w_in", n_in, after)

    small_all = all_gather(pack_small(small), "gather_small", after)

    wmv = {n: (flat(n), flat(n, "m_"), flat(n, "v_")) for n in SMALL_ORDER}
    per, g_wc, g_wf, loss = small_adamw(small_all, wmv, "adamw_small")
    out.update(per)
    taps = shard_adamw([(_shard(g_wc, n_wc, me), flat("w_conv_dw"), flat("w_conv_dw", "m_"), flat("w_conv_dw", "v_")),
                        (_shard(g_wf, n_wf, me), flat("w_ffn_dw"), flat("w_ffn_dw", "m_"), flat("w_ffn_dw", "v_"))],
                       "adamw_taps")
    out["w_conv_dw"], out["w_ffn_dw"] = taps

    dmod_cols = _shard(small_all[:, 0, :], n_ada, me)
    out["w_ada"] = ada_bwd_adamw(c_all.T, dmod_cols, flat("w_ada"), flat("w_ada", "m_"), flat("w_ada", "v_"), "adamw_w_ada")

    result = [loss, grad_x[None]]
    for k in range(4):
        result += [out[n][k].reshape(args[n].shape) for n in WEIGHTS]
    return tuple(result)
```

```python
import functools

import jax
import jax.numpy as jnp
from jax import lax
from jax.experimental import pallas as pl
from jax.experimental.pallas import tpu as pltpu

F32 = jnp.float32
BF16 = jnp.bfloat16

N_DEV = 8
SEQ = 2048
D_MODEL = 1024
D_CONV = 512
D_ATTN = 512
HEAD_DIM = 64
CONV_K = 31
D_FF = 2816
FFN_K = 3
D_IN = 2 * D_CONV + 3 * D_ATTN
N_MOD = 6
EPS = 1e-6
ATTN_BLOCK = 128
PATTERNS = ((2048, 1), (512, 4), (128, 16))
NEG = -1e30

ADAM_LR, ADAM_B1, ADAM_B2, ADAM_EPS, ADAM_WD, ADAM_STEP = 0.001, 0.9, 0.999, 1e-08, 0.01, 10

ROWS = 256
CONV_HALO = 32
FFN_HALO = 8
FFN_TN = 1408
VMEM_LIMIT = 56 * 1024 * 1024
PACK_ROWS, PACK_W = 16, 6144


NT = (((1,), (1,)), ((), ()))


def _cp(*sem):
    return pltpu.CompilerParams(dimension_semantics=sem if sem else None, vmem_limit_bytes=VMEM_LIMIT)


def _sig(x):
    return 1.0 / (1.0 + jnp.exp(-x))


def _rsum(x):
    return jnp.sum(x, axis=0, keepdims=True)


def _mean(x):
    return jnp.mean(x, axis=-1, keepdims=True)


def _acc(ref, val, first):
    @pl.when(first)
    def _():
        ref[...] = val

    @pl.when(jnp.logical_not(first))
    def _():
        ref[...] += val


SUB = 16


def _for_chunks(fn, unroll=1, rows=ROWS):
    def step(i, carry):
        fn(pl.ds(pl.multiple_of(i * SUB, SUB), SUB))
        return carry

    lax.fori_loop(0, rows // SUB, step, 0, unroll=unroll)


def rms_mod_fwd(x, g, mod, sh_row, sc_row, name):
    def body(x_ref, g_ref, mod_ref, h_ref):
        xx = x_ref[...]
        r = lax.rsqrt(_mean(xx * xx) + EPS)
        h = xx * r * g_ref[...]
        h_ref[...] = (h * (1.0 + mod_ref[sc_row:sc_row + 1, :]) + mod_ref[sh_row:sh_row + 1, :]).astype(BF16)

    return pl.pallas_call(
        body, out_shape=jax.ShapeDtypeStruct((SEQ, D_MODEL), BF16), grid=(SEQ // ROWS,),
        in_specs=[_row_spec(D_MODEL), _vec_spec(D_MODEL), _vec_spec(D_MODEL, 8)],
        out_specs=_row_spec(D_MODEL), name=name, compiler_params=_cp("parallel"))(x, g, mod)


def resid_rms_mod_fwd(x, y, g, mod, ga_row, sh_row, sc_row, name):
    def body(x_ref, y_ref, g_ref, mod_ref, x1_ref, h_ref):
        x1 = x_ref[...] + mod_ref[ga_row:ga_row + 1, :] * y_ref[...]
        x1_ref[...] = x1
        r = lax.rsqrt(_mean(x1 * x1) + EPS)
        h = x1 * r * g_ref[...]
        h_ref[...] = (h * (1.0 + mod_ref[sc_row:sc_row + 1, :]) + mod_ref[sh_row:sh_row + 1, :]).astype(BF16)

    return pl.pallas_call(
        body, out_shape=(jax.ShapeDtypeStruct((SEQ, D_MODEL), F32), jax.ShapeDtypeStruct((SEQ, D_MODEL), BF16)),
        grid=(SEQ // ROWS,),
        in_specs=[_row_spec(D_MODEL), _row_spec(D_MODEL), _vec_spec(D_MODEL), _vec_spec(D_MODEL, 8)],
        out_specs=(_row_spec(D_MODEL), _row_spec(D_MODEL)), name=name, compiler_params=_cp("parallel"))(x, y, g, mod)


def final_loss_bwd(x1, y2, tgt, g, mod, ga_row, name):
    def body(x1_ref, y2_ref, t_ref, g_ref, mod_ref, loss_ref, dx2_ref, dy2_ref, dg_ref, dga_ref):
        first = pl.program_id(0) == 0
        ga = mod_ref[ga_row:ga_row + 1, :]
        y2 = y2_ref[...]
        x2 = x1_ref[...] + ga * y2
        r = lax.rsqrt(_mean(x2 * x2) + EPS)
        xn = x2 * r
        err = xn * g_ref[...] - t_ref[...]
        _acc(loss_ref, jnp.broadcast_to(0.5 * jnp.sum(_mean(err * err)), (8, 128)), first)
        dy = err * (1.0 / D_MODEL)
        _acc(dg_ref, _rsum(dy * xn), first)
        dxn = dy * g_ref[...]
        dx2 = r * (dxn - xn * _mean(dxn * xn))
        dx2_ref[...] = dx2
        dy2_ref[...] = (dx2 * ga).astype(BF16)
        _acc(dga_ref, _rsum(dx2 * y2), first)

    vec = jax.ShapeDtypeStruct((1, D_MODEL), F32)
    return pl.pallas_call(
        body,
        out_shape=(jax.ShapeDtypeStruct((8, 128), F32), jax.ShapeDtypeStruct((SEQ, D_MODEL), F32),
                   jax.ShapeDtypeStruct((SEQ, D_MODEL), BF16), vec, vec),
        grid=(SEQ // ROWS,),
        in_specs=[_row_spec(D_MODEL), _row_spec(D_MODEL), _row_spec(D_MODEL), _vec_spec(D_MODEL), _vec_spec(D_MODEL, 8)],
        out_specs=(pl.BlockSpec((8, 128), lambda i: (0, 0)), _row_spec(D_MODEL), _row_spec(D_MODEL),
                   _vec_spec(D_MODEL), _vec_spec(D_MODEL)),
        name=name, compiler_params=_cp("arbitrary"))(x1, y2, tgt, g, mod)


def rms_mod_bwd(x, dh, dres, g, mod, sc_row, y, ga_row, name):
    gated = y is not None

    def body(*refs):
        if gated:
            x_ref, dh_ref, dres_ref, g_ref, mod_ref, y_ref, dx_ref, dsh_ref, dsc_ref, dg_ref, dy_ref, dga_ref = refs
        else:
            x_ref, dh_ref, dres_ref, g_ref, mod_ref, dx_ref, dsh_ref, dsc_ref, dg_ref = refs
        first = pl.program_id(0) == 0
        xx = x_ref[...]
        dh = dh_ref[...]
        gg = g_ref[...]
        r = lax.rsqrt(_mean(xx * xx) + EPS)
        xn = xx * r
        _acc(dsh_ref, _rsum(dh), first)
        _acc(dsc_ref, _rsum(dh * (xn * gg)), first)
        dt = dh * (1.0 + mod_ref[sc_row:sc_row + 1, :])
        _acc(dg_ref, _rsum(dt * xn), first)
        dxn = dt * gg
        dx = dres_ref[...] + r * (dxn - xn * _mean(dxn * xn))
        dx_ref[...] = dx
        if gated:
            _acc(dga_ref, _rsum(dx * y_ref[...]), first)
            dy_ref[...] = (dx * mod_ref[ga_row:ga_row + 1, :]).astype(BF16)

    vec = jax.ShapeDtypeStruct((1, D_MODEL), F32)
    in_specs = [_row_spec(D_MODEL), _row_spec(D_MODEL), _row_spec(D_MODEL), _vec_spec(D_MODEL), _vec_spec(D_MODEL, 8)]
    out_shape = [jax.ShapeDtypeStruct((SEQ, D_MODEL), F32), vec, vec, vec]
    out_specs = [_row_spec(D_MODEL), _vec_spec(D_MODEL), _vec_spec(D_MODEL), _vec_spec(D_MODEL)]
    args = [x, dh, dres, g, mod]
    if gated:
        in_specs.append(_row_spec(D_MODEL))
        out_shape += [jax.ShapeDtypeStruct((SEQ, D_MODEL), BF16), vec]
        out_specs += [_row_spec(D_MODEL), _vec_spec(D_MODEL)]
        args.append(y)
    return pl.pallas_call(
        body, out_shape=tuple(out_shape), grid=(SEQ // ROWS,), in_specs=in_specs, out_specs=tuple(out_specs),
        name=name, compiler_params=_cp("arbitrary"))(*args)


def rms_mod_matmul(x, g, mod, sh_row, sc_row, b, b_rows, name):
    n = N_DEV * b_rows

    def body(x_ref, g_ref, mod_ref, b_ref, o_ref, h_ref):
        xx = x_ref[...]
        r = lax.rsqrt(_mean(xx * xx) + EPS)
        h = (xx * r * g_ref[...] * (1.0 + mod_ref[sc_row:sc_row + 1, :]) + mod_ref[sh_row:sh_row + 1, :]).astype(BF16)
        h_ref[...] = h
        o_ref[...] = lax.dot_general(h, b_ref[...].reshape(n, D_MODEL), NT, preferred_element_type=F32)

    return pl.pallas_call(
        body, out_shape=(jax.ShapeDtypeStruct((SEQ, n), F32), jax.ShapeDtypeStruct((SEQ, D_MODEL), BF16)),
        grid=(SEQ // ROWS,),
        in_specs=[_row_spec(D_MODEL), _vec_spec(D_MODEL), _vec_spec(D_MODEL, 8),
                  pl.BlockSpec((N_DEV, b_rows, D_MODEL), lambda i: (0, 0, 0))],
        out_specs=(_row_spec(n), _row_spec(D_MODEL)), name=name, compiler_params=_cp("parallel"))(x, g, mod, b)


def norm_concat_matmul(mix_a, att, gao, w, name):
    def body(a_ref, att_ref, g_ref, w_ref, y_ref, mixed_ref):
        aa = att_ref[...]
        mixed_ref[:, 0:D_CONV] = a_ref[...]
        mixed_ref[:, D_CONV:] = (aa * lax.rsqrt(_mean(aa * aa) + EPS) * g_ref[...]).astype(BF16)
        y_ref[...] = jnp.dot(mixed_ref[...], w_ref[...], preferred_element_type=F32)

    return pl.pallas_call(
        body, out_shape=(jax.ShapeDtypeStruct((SEQ, D_MODEL), F32), jax.ShapeDtypeStruct((SEQ, D_MODEL), BF16)),
        grid=(SEQ // ROWS,),
        in_specs=[_row_spec(D_CONV), _row_spec(D_ATTN), _vec_spec(D_ATTN), pl.BlockSpec(w.shape, lambda i: (0, 0))],
        out_specs=(_row_spec(D_MODEL), _row_spec(D_MODEL)), name=name, compiler_params=_cp("parallel"))(mix_a, att, gao, w)


def resid_rms_mod_matmul(x, y, g, mod, ga_row, sh_row, sc_row, w, name):
    n = w.shape[0]

    def body(x_ref, y_ref, g_ref, mod_ref, w_ref, o_ref, x1_ref, h_ref):
        x1 = x_ref[...] + mod_ref[ga_row:ga_row + 1, :] * y_ref[...]
        x1_ref[...] = x1
        r = lax.rsqrt(_mean(x1 * x1) + EPS)
        h = (x1 * r * g_ref[...] * (1.0 + mod_ref[sc_row:sc_row + 1, :]) + mod_ref[sh_row:sh_row + 1, :]).astype(BF16)
        h_ref[...] = h
        o_ref[...] = lax.dot_general(h, w_ref[...], NT, preferred_element_type=F32)

    return pl.pallas_call(
        body,
        out_shape=(jax.ShapeDtypeStruct((SEQ, n), F32), jax.ShapeDtypeStruct((SEQ, D_MODEL), F32),
                   jax.ShapeDtypeStruct((SEQ, D_MODEL), BF16)),
        grid=(SEQ // ROWS,),
        in_specs=[_row_spec(D_MODEL), _row_spec(D_MODEL), _vec_spec(D_MODEL), _vec_spec(D_MODEL, 8),
                  pl.BlockSpec(w.shape, lambda i: (0, 0))],
        out_specs=(_row_spec(n), _row_spec(D_MODEL), _row_spec(D_MODEL)),
        name=name, compiler_params=_cp("parallel"))(x, y, g, mod, w)


def matmul_combine_bwd(dy, w, att, gao, name):
    def body(dy_ref, w_ref, att_ref, g_ref, dm_ref, do_ref, dd_ref, dg_ref):
        first = pl.program_id(0) == 0
        dm_ref[...] = lax.dot_general(dy_ref[...], w_ref[...], NT, preferred_element_type=F32)
        att = att_ref[...]
        r = lax.rsqrt(_mean(att * att) + EPS)
        xn = att * r
        dm = dm_ref[:, D_CONV:]
        _acc(dg_ref, _rsum(dm * xn), first)
        dyn = dm * g_ref[...]
        do = r * (dyn - xn * _mean(dyn * xn))
        do_ref[...] = do
        same_head = (jnp.right_shift(lax.broadcasted_iota(jnp.int32, (D_ATTN, D_ATTN), 0), 6)
                     == jnp.right_shift(lax.broadcasted_iota(jnp.int32, (D_ATTN, D_ATTN), 1), 6)).astype(F32)
        dd_ref[...] = jnp.dot(do * att, same_head, preferred_element_type=F32, precision=lax.Precision.HIGHEST)

    rs = _row_spec(D_ATTN)
    f = jax.ShapeDtypeStruct((SEQ, D_ATTN), F32)
    return pl.pallas_call(
        body, out_shape=(jax.ShapeDtypeStruct((SEQ, D_MODEL), F32), f, f, jax.ShapeDtypeStruct((1, D_ATTN), F32)),
        grid=(SEQ // ROWS,),
        in_specs=[_row_spec(D_MODEL), pl.BlockSpec(w.shape, lambda i: (0, 0)), rs, _vec_spec(D_ATTN)],
        out_specs=(_row_spec(D_MODEL), rs, rs, _vec_spec(D_ATTN)),
        name=name, compiler_params=_cp("arbitrary"))(dy, w, att, gao)


def matmul_loss_bwd(act, w, x1, tgt, g, mod, ga_row, name):
    def body(a_ref, w_ref, x1_ref, t_ref, g_ref, mod_ref, loss_ref, dx2_ref, dy2_ref, dg_ref, dga_ref):
        first = pl.program_id(0) == 0
        ga = mod_ref[ga_row:ga_row + 1, :]
        y2 = jnp.dot(a_ref[...], w_ref[...], preferred_element_type=F32)
        x2 = x1_ref[...] + ga * y2
        r = lax.rsqrt(_mean(x2 * x2) + EPS)
        xn = x2 * r
        err = xn * g_ref[...] - t_ref[...]
        _acc(loss_ref, jnp.broadcast_to(0.5 * jnp.sum(_mean(err * err)), (8, 128)), first)
        dy = err * (1.0 / D_MODEL)
        _acc(dg_ref, _rsum(dy * xn), first)
        dxn = dy * g_ref[...]
        dx2 = r * (dxn - xn * _mean(dxn * xn))
        dx2_ref[...] = dx2
        dy2_ref[...] = (dx2 * ga).astype(BF16)
        _acc(dga_ref, _rsum(dx2 * y2), first)

    vec = jax.ShapeDtypeStruct((1, D_MODEL), F32)
    return pl.pallas_call(
        body,
        out_shape=(jax.ShapeDtypeStruct((8, 128), F32), jax.ShapeDtypeStruct((SEQ, D_MODEL), F32),
                   jax.ShapeDtypeStruct((SEQ, D_MODEL), BF16), vec, vec),
        grid=(SEQ // ROWS,),
        in_specs=[_row_spec(act.shape[1]), pl.BlockSpec(w.shape, lambda i: (0, 0)), _row_spec(D_MODEL), _row_spec(D_MODEL),
                  _vec_spec(D_MODEL), _vec_spec(D_MODEL, 8)],
        out_specs=(pl.BlockSpec((8, 128), lambda i: (0, 0)), _row_spec(D_MODEL), _row_spec(D_MODEL),
                   _vec_spec(D_MODEL), _vec_spec(D_MODEL)),
        name=name, compiler_params=_cp("arbitrary"))(act, w, x1, tgt, g, mod)


def matmul_rms_mod_bwd(a, b, x, dres, g, mod, sc_row, y, ga_row, name, b_rows=None):
    gated = y is not None
    halves = a.ndim == 3
    if halves:
        k2 = a.shape[2]
        b_arg = b.reshape(2, k2, D_MODEL)
        a_spec = pl.BlockSpec((2, ROWS, k2), lambda i: (0, i, 0))
        b_spec = pl.BlockSpec((2, k2, D_MODEL), lambda i: (0, 0, 0))
    else:
        b_arg = b
        a_spec = _row_spec(a.shape[1])
        b_spec = pl.BlockSpec((N_DEV, b_rows, D_MODEL), lambda i: (0, 0, 0))

    def body(*refs):
        if gated:
            a_ref, b_ref, x_ref, dres_ref, g_ref, mod_ref, y_ref, dx_ref, dsh_ref, dsc_ref, dg_ref, dy_ref, dga_ref = refs
        else:
            a_ref, b_ref, x_ref, dres_ref, g_ref, mod_ref, dx_ref, dsh_ref, dsc_ref, dg_ref = refs
        first = pl.program_id(0) == 0
        if halves:
            dh = (jnp.dot(a_ref[0], b_ref[0], preferred_element_type=F32)
                  + jnp.dot(a_ref[1], b_ref[1], preferred_element_type=F32))
        else:
            dh = jnp.dot(a_ref[...], b_ref[...].reshape(N_DEV * b_rows, D_MODEL), preferred_element_type=F32)
        xx = x_ref[...]
        gg = g_ref[...]
        r = lax.rsqrt(_mean(xx * xx) + EPS)
        xn = xx * r
        _acc(dsh_ref, _rsum(dh), first)
        _acc(dsc_ref, _rsum(dh * (xn * gg)), first)
        dt = dh * (1.0 + mod_ref[sc_row:sc_row + 1, :])
        _acc(dg_ref, _rsum(dt * xn), first)
        dxn = dt * gg
        dx = dres_ref[...] + r * (dxn - xn * _mean(dxn * xn))
        dx_ref[...] = dx
        if gated:
            _acc(dga_ref, _rsum(dx * y_ref[...]), first)
            dy_ref[...] = (dx * mod_ref[ga_row:ga_row + 1, :]).astype(BF16)

    vec = jax.ShapeDtypeStruct((1, D_MODEL), F32)
    in_specs = [a_spec, b_spec, _row_spec(D_MODEL), _row_spec(D_MODEL), _vec_spec(D_MODEL), _vec_spec(D_MODEL, 8)]
    out_shape = [jax.ShapeDtypeStruct((SEQ, D_MODEL), F32), vec, vec, vec]
    out_specs = [_row_spec(D_MODEL), _vec_spec(D_MODEL), _vec_spec(D_MODEL), _vec_spec(D_MODEL)]
    args = [a, b_arg, x, dres, g, mod]
    if gated:
        in_specs.append(_row_spec(D_MODEL))
        out_shape += [jax.ShapeDtypeStruct((SEQ, D_MODEL), BF16), vec]
        out_specs += [_row_spec(D_MODEL), _vec_spec(D_MODEL)]
        args.append(y)
    return pl.pallas_call(
        body, out_shape=tuple(out_shape), grid=(SEQ // ROWS,), in_specs=in_specs, out_specs=tuple(out_specs),
        name=name, compiler_params=_cp("arbitrary"))(*args)


def _prev_halo(halo, width, col):
    per = ROWS // halo
    return pl.BlockSpec((halo, width), lambda i: (jnp.maximum(i * per - 1, 0), col))


def _next_halo(halo, width, col):
    per = ROWS // halo
    last = SEQ // halo - 1
    return pl.BlockSpec((halo, width), lambda i: (jnp.minimum((i + 1) * per, last), col))


CONV_PAD = ROWS + CONV_HALO


def _shift_copies(sh):
    for b in range(1, 8):
        sh[b, 0:CONV_PAD - 8, :] = sh[0, pl.ds(b, CONV_PAD - 8), :]


def _tap(sh, rs_start, offset):
    return sh[offset % 8, pl.ds(pl.multiple_of(rs_start + (offset // 8) * 8, 8), SUB), :]


def _conv_glu(av_ref, ag_ref, avh_ref, agh_ref, sh):
    i = pl.program_id(0)
    hv = avh_ref[...] * _sig(agh_ref[...])
    sh[0, 0:CONV_HALO, :] = jnp.where(i > 0, hv, 0.0)

    def glu(rs):
        sh[0, pl.ds(pl.multiple_of(rs.start + CONV_HALO, SUB), SUB), :] = av_ref[rs, :] * _sig(ag_ref[rs, :])

    _for_chunks(glu)
    _shift_copies(sh)


def _conv_norm(u1, lg_ref, lb_ref):
    mu = _mean(u1)
    cen = u1 - mu
    rs = lax.rsqrt(_mean(cen * cen) + EPS)
    z = cen * rs
    ln = z * lg_ref[...] + lb_ref[...]
    s = _sig(ln)
    return z, rs, ln, s, ln * s


def conv_module_fwd(proj, wc, bc, lg, lb, gco, name):
    def body(av_ref, ag_ref, avh_ref, agh_ref, wc_ref, bc_ref, lg_ref, lb_ref, gco_ref, out_ref, u1_ref, sh):
        _conv_glu(av_ref, ag_ref, avh_ref, agh_ref, sh)

        def conv(rs):
            u1 = jnp.broadcast_to(bc_ref[...], (SUB, D_CONV))
            for j in range(CONV_K):
                u1 = u1 + wc_ref[j:j + 1, :] * _tap(sh, rs.start, CONV_HALO - (CONV_K - 1) + j)
            u1_ref[rs, :] = u1

        _for_chunks(conv)
        _, _, _, _, u2 = _conv_norm(u1_ref[...], lg_ref, lb_ref)
        rc = lax.rsqrt(_mean(u2 * u2) + EPS)
        out_ref[...] = (u2 * rc * gco_ref[...]).astype(BF16)

    v = _vec_spec(D_CONV)
    return pl.pallas_call(
        body, out_shape=(jax.ShapeDtypeStruct((SEQ, D_CONV), BF16), jax.ShapeDtypeStruct((SEQ, D_CONV), F32)),
        grid=(SEQ // ROWS,),
        in_specs=[_row_spec(D_CONV, 0), _row_spec(D_CONV, 1), _prev_halo(CONV_HALO, D_CONV, 0),
                  _prev_halo(CONV_HALO, D_CONV, 1), _vec_spec(D_CONV, CONV_K), v, v, v, v],
        out_specs=(_row_spec(D_CONV), _row_spec(D_CONV)), scratch_shapes=[pltpu.VMEM((8, CONV_PAD, D_CONV), F32)],
        name=name, compiler_params=_cp("parallel"))(proj, proj, proj, proj, wc, bc, lg, lb, gco)


def conv_module_bwd_a(proj, u1, dmixed, lg, lb, gco, name):
    def body(av_ref, ag_ref, avh_ref, agh_ref, u1_ref, dm_ref, lg_ref, lb_ref, gco_ref,
             du1_ref, dgco_ref, dlg_ref, dlb_ref, dbc_ref, dwc_ref, sh, acc):
        first = pl.program_id(0) == 0
        _conv_glu(av_ref, ag_ref, avh_ref, agh_ref, sh)
        z, rs, ln, s, u2 = _conv_norm(u1_ref[...], lg_ref, lb_ref)
        rc = lax.rsqrt(_mean(u2 * u2) + EPS)
        xn = u2 * rc
        dm = dm_ref[...]
        _acc(dgco_ref, _rsum(dm * xn), first)
        dyn = dm * gco_ref[...]
        du2 = rc * (dyn - xn * _mean(dyn * xn))
        dln = du2 * (s * (1.0 + ln * (1.0 - s)))
        _acc(dlg_ref, _rsum(dln * z), first)
        _acc(dlb_ref, _rsum(dln), first)
        dz = dln * lg_ref[...]
        du1 = rs * (dz - _mean(dz) - z * _mean(dz * z))
        du1_ref[...] = du1
        _acc(dbc_ref, _rsum(du1), first)
        acc[...] = jnp.zeros_like(acc)

        def taps(rs):
            d = du1_ref[rs, :]
            for j in range(CONV_K):
                acc[j] += d * _tap(sh, rs.start, CONV_HALO - (CONV_K - 1) + j)

        _for_chunks(taps)

        @pl.when(first)
        def _():
            dwc_ref[...] = jnp.zeros_like(dwc_ref)

        for j in range(CONV_K):
            dwc_ref[j:j + 1, :] += _rsum(acc[j])

    v = _vec_spec(D_CONV)
    vec = jax.ShapeDtypeStruct((1, D_CONV), F32)
    return pl.pallas_call(
        body,
        out_shape=(jax.ShapeDtypeStruct((SEQ, D_CONV), F32), vec, vec, vec, vec, jax.ShapeDtypeStruct((CONV_K, D_CONV), F32)),
        grid=(SEQ // ROWS,),
        in_specs=[_row_spec(D_CONV, 0), _row_spec(D_CONV, 1), _prev_halo(CONV_HALO, D_CONV, 0),
                  _prev_halo(CONV_HALO, D_CONV, 1), _row_spec(D_CONV, 0), _row_spec(D_CONV, 0), v, v, v],
        out_specs=(_row_spec(D_CONV), v, v, v, v, _vec_spec(D_CONV, CONV_K)),
        scratch_shapes=[pltpu.VMEM((8, CONV_PAD, D_CONV), F32), pltpu.VMEM((CONV_K, SUB, D_CONV), F32)],
        name=name, compiler_params=_cp("arbitrary"))(proj, proj, proj, proj, u1, dmixed, lg, lb, gco)


def conv_module_bwd_b(proj, du1, wc, name):
    def body(av_ref, ag_ref, du1_ref, du1n_ref, wc_ref, out_ref, sh):
        i = pl.program_id(0)
        sh[0, 0:ROWS, :] = du1_ref[...]
        sh[0, ROWS:, :] = jnp.where(i < SEQ // ROWS - 1, du1n_ref[...], 0.0)
        _shift_copies(sh)

        def chunk(rs):
            du0 = jnp.zeros((SUB, D_CONV), F32)
            for j in range(CONV_K):
                du0 = du0 + wc_ref[j:j + 1, :] * _tap(sh, rs.start, CONV_K - 1 - j)
            sg = _sig(ag_ref[rs, :])
            out_ref[rs, 0:D_CONV] = (du0 * sg).astype(BF16)
            out_ref[rs, D_CONV:] = (du0 * av_ref[rs, :] * sg * (1.0 - sg)).astype(BF16)

        _for_chunks(chunk)

    return pl.pallas_call(
        body, out_shape=jax.ShapeDtypeStruct((SEQ, 2 * D_CONV), BF16), grid=(SEQ // ROWS,),
        in_specs=[_row_spec(D_CONV, 0), _row_spec(D_CONV, 1), _row_spec(D_CONV, 0), _next_halo(CONV_HALO, D_CONV, 0),
                  _vec_spec(D_CONV, CONV_K)],
        out_specs=_row_spec(2 * D_CONV), scratch_shapes=[pltpu.VMEM((8, CONV_PAD, D_CONV), F32)],
        name=name, compiler_params=_cp("parallel"))(proj, proj, du1, du1, wc)


def _attn_specs(sub_len, pairs):
    ng, width = 4 // pairs, 128 * pairs
    q = pl.BlockSpec((sub_len, width), lambda rho, g: (0, rho * 3 * ng + g))
    k = pl.BlockSpec((sub_len, width), lambda rho, g: (0, rho * 3 * ng + ng + g))
    v = pl.BlockSpec((sub_len, width), lambda rho, g: (0, rho * 3 * ng + 2 * ng + g))
    o = pl.BlockSpec((sub_len, width), lambda rho, g: (0, rho * ng + g))
    return q, k, v, o


ATTN_PAIRS = {1: 1, 4: 1, 16: 4}


def _attn_block(q_ref, k_ref, v_ref, n, hs, win):
    q0 = pl.multiple_of(n * ATTN_BLOCK, ATTN_BLOCK)
    k0 = pl.multiple_of(jnp.maximum(n - 1, 0) * ATTN_BLOCK, ATTN_BLOCK)
    qb = q_ref[pl.ds(q0, ATTN_BLOCK), hs]
    kw = k_ref[pl.ds(k0, win), hs]
    vw = v_ref[pl.ds(k0, win), hs]
    s = lax.dot_general(qb, kw, (((1,), (1,)), ((), ())), preferred_element_type=F32) * (HEAD_DIM ** -0.5)
    dist = (q0 - k0) + lax.broadcasted_iota(jnp.int32, (ATTN_BLOCK, win), 0) \
        - lax.broadcasted_iota(jnp.int32, (ATTN_BLOCK, win), 1)
    s = jnp.where((dist >= 0) & (dist <= ATTN_BLOCK), s, NEG)
    return q0, k0, qb, kw, vw, s


def attn_fwd(qkv_r, sub_len, r, name):
    nb = sub_len // ATTN_BLOCK
    win = 2 * ATTN_BLOCK if nb > 1 else ATTN_BLOCK
    pairs = ATTN_PAIRS[r]

    def body(q_ref, k_ref, v_ref, o_ref, l_ref):
        def block(n, carry):
            for h in range(2 * pairs):
                hs = slice(h * HEAD_DIM, (h + 1) * HEAD_DIM)
                q0, _, _, _, vw, s = _attn_block(q_ref, k_ref, v_ref, n, hs, win)
                m = jnp.max(s, axis=1, keepdims=True)
                p = jnp.exp(s - m)
                den = jnp.sum(p, axis=1, keepdims=True)
                o = jnp.dot(p.astype(BF16), vw, preferred_element_type=F32) / den
                o_ref[pl.ds(q0, ATTN_BLOCK), hs] = o
                l_ref[pl.ds(q0, ATTN_BLOCK), hs] = jnp.broadcast_to(m + jnp.log(den), (ATTN_BLOCK, HEAD_DIM))
            return carry

        lax.fori_loop(0, nb, block, 0, unroll=min(nb, 2))

    q, k, v, o = _attn_specs(sub_len, pairs)
    shp = jax.ShapeDtypeStruct((sub_len, r * D_ATTN), F32)
    return pl.pallas_call(
        body, out_shape=(shp, shp), grid=(r, 4 // pairs), in_specs=[q, k, v], out_specs=(o, o),
        name=name, compiler_params=_cp("parallel", "parallel"))(qkv_r, qkv_r, qkv_r)


def attn_bwd(qkv_r, do_r, lse_r, dd_r, sub_len, r, name):
    nb = sub_len // ATTN_BLOCK
    win = 2 * ATTN_BLOCK if nb > 1 else ATTN_BLOCK
    pairs = ATTN_PAIRS[r]

    def body(q_ref, k_ref, v_ref, do_ref, l_ref, dd_ref, dq_ref, dk_ref, dv_ref):
        dk_ref[...] = jnp.zeros_like(dk_ref)
        dv_ref[...] = jnp.zeros_like(dv_ref)

        def block(n, carry):
            for h in range(2 * pairs):
                hs = slice(h * HEAD_DIM, (h + 1) * HEAD_DIM)
                h1 = slice(h * HEAD_DIM, h * HEAD_DIM + 1)
                q0, k0, qb, kw, vw, s = _attn_block(q_ref, k_ref, v_ref, n, hs, win)
                dob = do_ref[pl.ds(q0, ATTN_BLOCK), hs]
                p = jnp.exp(s - l_ref[pl.ds(q0, ATTN_BLOCK), h1])
                dp = lax.dot_general(dob, vw, (((1,), (1,)), ((), ())), preferred_element_type=F32)
                ds = (p * (dp - dd_ref[pl.ds(q0, ATTN_BLOCK), h1]) * (HEAD_DIM ** -0.5)).astype(BF16)
                dq_ref[pl.ds(q0, ATTN_BLOCK), hs] = jnp.dot(ds, kw, preferred_element_type=F32)
                dk_ref[pl.ds(k0, win), hs] += lax.dot_general(ds, qb, (((0,), (0,)), ((), ())), preferred_element_type=F32)
                dv_ref[pl.ds(k0, win), hs] += lax.dot_general(p.astype(BF16), dob, (((0,), (0,)), ((), ())),
                                                              preferred_element_type=F32)
            return carry

        lax.fori_loop(0, nb, block, 0)

    q, k, v, o = _attn_specs(sub_len, pairs)
    shp = jax.ShapeDtypeStruct((sub_len, r * D_ATTN), F32)
    return pl.pallas_call(
        body, out_shape=(shp, shp, shp), grid=(r, 4 // pairs), in_specs=[q, k, v, o, o, o], out_specs=(o, o, o),
        name=name, compiler_params=_cp("parallel", "parallel"))(qkv_r, qkv_r, qkv_r, do_r, lse_r, dd_r)


def _rows(start, size, r):
    return pl.ds(start, size) if r == 1 else pl.ds(start, size, stride=r)


def _unit_rows(r, rho, n, nb):
    win = 2 * ATTN_BLOCK if nb > 1 else ATTN_BLOCK
    if isinstance(n, int):
        kb = max(n - 1, 0)
        q_rows = _rows(rho + r * ATTN_BLOCK * n, ATTN_BLOCK, r)
        k_rows = _rows(rho + r * ATTN_BLOCK * kb, win, r)
    else:
        kb = jnp.maximum(n - 1, 0)
        q_rows = pl.ds(pl.multiple_of(n * ATTN_BLOCK, ATTN_BLOCK), ATTN_BLOCK)
        k_rows = pl.ds(pl.multiple_of(kb * ATTN_BLOCK, ATTN_BLOCK), win)
    dist = (n - kb) * ATTN_BLOCK + lax.broadcasted_iota(jnp.int32, (ATTN_BLOCK, win), 0) \
        - lax.broadcasted_iota(jnp.int32, (ATTN_BLOCK, win), 1)
    return q_rows, k_rows, (dist >= 0) & (dist <= ATTN_BLOCK)


def _per_head(x):
    lane = lax.broadcasted_iota(jnp.int32, x.shape, 1)
    zero = jnp.zeros_like(x)
    return [jnp.where(lane < HEAD_DIM, x, zero), jnp.where(lane >= HEAD_DIM, x, zero)]


def _masked_scores(q2, k2, valid):
    return [jnp.where(valid, lax.dot_general(qh, k2, NT, preferred_element_type=F32) * (HEAD_DIM ** -0.5), NEG)
            for qh in _per_head(q2)]


def _attn_units(r, nb, unit):
    if r == 1:
        def four(i, carry):
            for k in range(4):
                unit(0, 4 * i + k)
            return carry
        lax.fori_loop(0, nb // 4, four, 0)
    else:
        for rho in range(r):
            for n in range(nb):
                unit(rho, n)


N_UNITS = 16


def attn_fwd_all(proj, name):
    def body(q_ref, k_ref, v_ref, att_ref, lse_ref, s_scr, p_scr, lse_scr, den_scr):
        for idx, (sub_len, r) in enumerate(PATTERNS):
            nb = sub_len // ATTN_BLOCK
            win = 2 * ATTN_BLOCK if nb > 1 else ATTN_BLOCK

            def scores(rho, n, r=r, nb=nb, win=win):
                u = rho * nb + n
                q_rows, k_rows, valid = _unit_rows(r, rho, n, nb)
                ss = _masked_scores(q_ref[q_rows, :].astype(BF16), k_ref[k_rows, :].astype(BF16), valid)
                for h in range(2):
                    s_scr[2 * u + h, :, 0:win] = ss[h]

            _attn_units(r, nb, scores)

            def softmax(u, carry, win=win):
                lses, dens = [], []
                for h in range(2):
                    sc = s_scr[2 * u + h, :, 0:win]
                    m = jnp.max(sc, axis=1, keepdims=True)
                    p = jnp.exp(sc - m)
                    den = jnp.sum(p, axis=1, keepdims=True)
                    p_scr[2 * u + h, :, 0:win] = p.astype(BF16)
                    lses.append(jnp.broadcast_to(m + jnp.log(den), (ATTN_BLOCK, HEAD_DIM)))
                    dens.append(jnp.broadcast_to(den, (ATTN_BLOCK, HEAD_DIM)))
                lse_scr[u] = jnp.concatenate(lses, axis=1)
                den_scr[u] = jnp.concatenate(dens, axis=1)
                return carry

            lax.fori_loop(0, N_UNITS, softmax, 0, unroll=2)

            def outputs(rho, n, r=r, nb=nb, win=win, idx=idx):
                u = rho * nb + n
                q_rows, k_rows, _ = _unit_rows(r, rho, n, nb)
                vs = _per_head(v_ref[k_rows, :].astype(BF16))
                o = (jnp.dot(p_scr[2 * u, :, 0:win], vs[0], preferred_element_type=F32)
                     + jnp.dot(p_scr[2 * u + 1, :, 0:win], vs[1], preferred_element_type=F32)) / den_scr[u]
                lse = lse_scr[u]
                if idx > 0:
                    old = lse_ref[q_rows, :]
                    top = jnp.maximum(old, lse)
                    new = top + jnp.log(jnp.exp(old - top) + jnp.exp(lse - top))
                    o = att_ref[q_rows, :] * jnp.exp(old - new) + o * jnp.exp(lse - new)
                    lse = new
                att_ref[q_rows, :] = o
                lse_ref[q_rows, :] = lse

            _attn_units(r, nb, outputs)

    blk = lambda first: pl.BlockSpec((SEQ, 128), lambda g: (0, first + g))
    shp = jax.ShapeDtypeStruct((SEQ, D_ATTN), F32)
    big = (2 * N_UNITS, ATTN_BLOCK, 2 * ATTN_BLOCK)
    small = pltpu.VMEM((N_UNITS, ATTN_BLOCK, 128), F32)
    return pl.pallas_call(
        body, out_shape=(shp, shp), grid=(4,), in_specs=[blk(8), blk(12), blk(16)], out_specs=(blk(0), blk(0)),
        scratch_shapes=[pltpu.VMEM(big, F32), pltpu.VMEM(big, BF16), small, small],
        name=name, compiler_params=_cp("parallel"))(proj, proj, proj)


def attn_bwd_all(proj, do, lse, dd, name):
    scale = HEAD_DIM ** -0.5

    def body(q_ref, k_ref, v_ref, do_ref, l_ref, dd_ref, out_ref, dq_s, dk_s, dv_s,
             s_scr, dp_scr, ds_scr, st_scr, dpt_scr, pt_scr, dst_scr, qb_scr, kb_scr, dob_scr):
        dq_s[...] = jnp.zeros_like(dq_s)
        dk_s[...] = jnp.zeros_like(dk_s)
        dv_s[...] = jnp.zeros_like(dv_s)
        for sub_len, r in PATTERNS:
            nb = sub_len // ATTN_BLOCK
            win = 2 * ATTN_BLOCK if nb > 1 else ATTN_BLOCK

            def scores(rho, n, r=r, nb=nb, win=win):
                u = rho * nb + n
                q_rows, k_rows, valid = _unit_rows(r, rho, n, nb)
                kb = max(n - 1, 0) if isinstance(n, int) else jnp.maximum(n - 1, 0)
                dist_t = (n - kb) * ATTN_BLOCK + lax.broadcasted_iota(jnp.int32, (win, ATTN_BLOCK), 1) \
                    - lax.broadcasted_iota(jnp.int32, (win, ATTN_BLOCK), 0)
                valid_t = (dist_t >= 0) & (dist_t <= ATTN_BLOCK)
                q2 = q_ref[q_rows, :].astype(BF16)
                k2 = k_ref[k_rows, :].astype(BF16)
                do2 = do_ref[q_rows, :].astype(BF16)
                qb_scr[u] = q2
                kb_scr[u, 0:win, :] = k2
                dob_scr[u] = do2
                l2 = l_ref[q_rows, :]
                d2 = dd_ref[q_rows, :]
                l2t = l2.T
                d2t = d2.T
                v2 = v_ref[k_rows, :].astype(BF16)
                qs, dos = _per_head(q2), _per_head(do2)
                for h in range(2):
                    c0 = h * HEAD_DIM
                    sc = lax.dot_general(qs[h], k2, NT, preferred_element_type=F32) * scale
                    s_scr[2 * u + h, :, 0:win] = jnp.where(valid, sc, NEG) - l2[:, c0:c0 + 1]
                    dp_scr[2 * u + h, :, 0:win] = lax.dot_general(dos[h], v2, NT, preferred_element_type=F32) \
                        - d2[:, c0:c0 + 1]
                    sct = lax.dot_general(k2, qs[h], NT, preferred_element_type=F32) * scale
                    st_scr[2 * u + h, 0:win, :] = jnp.where(valid_t, sct, NEG) - l2t[c0:c0 + 1, :]
                    dpt_scr[2 * u + h, 0:win, :] = lax.dot_general(v2, dos[h], NT, preferred_element_type=F32) \
                        - d2t[c0:c0 + 1, :]

            _attn_units(r, nb, scores)

            def pointwise(hu, carry, win=win):
                ds_scr[hu, :, 0:win] = (jnp.exp(s_scr[hu, :, 0:win]) * dp_scr[hu, :, 0:win] * scale).astype(BF16)
                pt = jnp.exp(st_scr[hu, 0:win, :])
                pt_scr[hu, 0:win, :] = pt.astype(BF16)
                dst_scr[hu, 0:win, :] = (pt * dpt_scr[hu, 0:win, :] * scale).astype(BF16)
                return carry

            lax.fori_loop(0, 2 * N_UNITS, pointwise, 0, unroll=4)

            def grads(rho, n, r=r, nb=nb, win=win):
                u = rho * nb + n
                q_rows, k_rows, _ = _unit_rows(r, rho, n, nb)
                qs, ks, dos = _per_head(qb_scr[u]), _per_head(kb_scr[u, 0:win, :]), _per_head(dob_scr[u])

                def both(scr, rows, rhs):
                    return (jnp.dot(scr[(2 * u,) + rows], rhs[0], preferred_element_type=F32)
                            + jnp.dot(scr[(2 * u + 1,) + rows], rhs[1], preferred_element_type=F32))

                dq_s[q_rows, :] += both(ds_scr, (slice(None), slice(0, win)), ks)
                dk_s[k_rows, :] += both(dst_scr, (slice(0, win), slice(None)), qs)
                dv_s[k_rows, :] += both(pt_scr, (slice(0, win), slice(None)), dos)

            _attn_units(r, nb, grads)
        out_ref[0] = dq_s[...].astype(BF16)
        out_ref[1] = dk_s[...].astype(BF16)
        out_ref[2] = dv_s[...].astype(BF16)

    blk = lambda first: pl.BlockSpec((SEQ, 128), lambda g: (0, first + g))
    acc = pltpu.VMEM((SEQ, 128), F32)
    big = (2 * N_UNITS, ATTN_BLOCK, 2 * ATTN_BLOCK)
    big_t = (2 * N_UNITS, 2 * ATTN_BLOCK, ATTN_BLOCK)
    return pl.pallas_call(
        body, out_shape=jax.ShapeDtypeStruct((3, SEQ, D_ATTN), BF16), grid=(4,),
        in_specs=[blk(8), blk(12), blk(16), blk(0), blk(0), blk(0)],
        out_specs=pl.BlockSpec((3, SEQ, 128), lambda g: (0, 0, g)),
        scratch_shapes=[acc, acc, acc, pltpu.VMEM(big, F32), pltpu.VMEM(big, F32), pltpu.VMEM(big, BF16),
                        pltpu.VMEM(big_t, F32), pltpu.VMEM(big_t, F32), pltpu.VMEM(big_t, BF16), pltpu.VMEM(big_t, BF16),
                        pltpu.VMEM((N_UNITS, ATTN_BLOCK, 128), BF16),
                        pltpu.VMEM((N_UNITS, 2 * ATTN_BLOCK, 128), BF16), pltpu.VMEM((N_UNITS, ATTN_BLOCK, 128), BF16)],
        name=name, compiler_params=_cp("parallel"))(proj, proj, proj, do, lse, dd)


def rms_gain_bf16(a, g, name):
    def body(a_ref, g_ref, o_ref):
        aa = a_ref[...]
        o_ref[...] = (aa * lax.rsqrt(_mean(aa * aa) + EPS) * g_ref[...]).astype(BF16)

    w = a.shape[1]
    return pl.pallas_call(
        body, out_shape=jax.ShapeDtypeStruct(a.shape, BF16), grid=(SEQ // ROWS,), in_specs=[_row_spec(w), _vec_spec(w)],
        out_specs=_row_spec(w), name=name, compiler_params=_cp("parallel"))(a, g)


def attn_combine_fwd(outs, lses, gao, name):
    def body(o1, o2, o3, l1, l2, l3, g_ref, att_ref, lse_ref, mix_ref):
        a1, a2, a3 = l1[...], l2[...], l3[...]
        m = jnp.maximum(jnp.maximum(a1, a2), a3)
        w1, w2, w3 = jnp.exp(a1 - m), jnp.exp(a2 - m), jnp.exp(a3 - m)
        den = w1 + w2 + w3
        att = (w1 * o1[...] + w2 * o2[...] + w3 * o3[...]) / den
        att_ref[...] = att
        lse_ref[...] = m + jnp.log(den)
        mix_ref[...] = (att * lax.rsqrt(_mean(att * att) + EPS) * g_ref[...]).astype(BF16)

    rs = _row_spec(D_ATTN)
    f = jax.ShapeDtypeStruct((SEQ, D_ATTN), F32)
    return pl.pallas_call(
        body, out_shape=(f, f, jax.ShapeDtypeStruct((SEQ, D_ATTN), BF16)), grid=(SEQ // ROWS,),
        in_specs=[rs] * 6 + [_vec_spec(D_ATTN)], out_specs=(rs, rs, rs),
        name=name, compiler_params=_cp("parallel"))(*outs, *lses, gao)


def attn_combine_bwd(dmixed, att, gao, name):
    def body(dm_ref, att_ref, g_ref, do_ref, dd_ref, dg_ref):
        first = pl.program_id(0) == 0
        att = att_ref[...]
        r = lax.rsqrt(_mean(att * att) + EPS)
        xn = att * r
        dm = dm_ref[...]
        _acc(dg_ref, _rsum(dm * xn), first)
        dyn = dm * g_ref[...]
        do = r * (dyn - xn * _mean(dyn * xn))
        do_ref[...] = do
        same_head = (jnp.right_shift(lax.broadcasted_iota(jnp.int32, (D_ATTN, D_ATTN), 0), 6)
                     == jnp.right_shift(lax.broadcasted_iota(jnp.int32, (D_ATTN, D_ATTN), 1), 6)).astype(F32)
        dd_ref[...] = jnp.dot(do * att, same_head, preferred_element_type=F32, precision=lax.Precision.HIGHEST)

    rs = _row_spec(D_ATTN)
    return pl.pallas_call(
        body,
        out_shape=(jax.ShapeDtypeStruct((SEQ, D_ATTN), F32), jax.ShapeDtypeStruct((SEQ, D_ATTN), F32),
                   jax.ShapeDtypeStruct((1, D_ATTN), F32)),
        grid=(SEQ // ROWS,), in_specs=[_row_spec(D_ATTN, 1), rs, _vec_spec(D_ATTN)],
        out_specs=(rs, rs, _vec_spec(D_ATTN)), name=name, compiler_params=_cp("arbitrary"))(dmixed, att, gao)


def sum3_bf16(a, b, c, name):
    def body(a_ref, b_ref, c_ref, o_ref):
        o_ref[...] = (a_ref[...] + b_ref[...] + c_ref[...]).astype(BF16)

    w = a.shape[1]
    rs = _row_spec(w)
    return pl.pallas_call(
        body, out_shape=jax.ShapeDtypeStruct(a.shape, BF16), grid=(SEQ // ROWS,), in_specs=[rs, rs, rs], out_specs=rs,
        name=name, compiler_params=_cp("parallel"))(a, b, c)


N_FT = D_FF // FFN_TN


def _ffn_specs():
    per = ROWS // FFN_HALO
    cur_g = pl.BlockSpec((ROWS, FFN_TN), lambda j, i: (i, j))
    cur_v = pl.BlockSpec((ROWS, FFN_TN), lambda j, i: (i, j + N_FT))
    halo_g = pl.BlockSpec((FFN_HALO, FFN_TN), lambda j, i: (jnp.maximum(i * per - 1, 0), j))
    halo_v = pl.BlockSpec((FFN_HALO, FFN_TN), lambda j, i: (jnp.maximum(i * per - 1, 0), j + N_FT))
    w_g = pl.BlockSpec((FFN_K, FFN_TN), lambda j, i: (0, j))
    w_v = pl.BlockSpec((FFN_K, FFN_TN), lambda j, i: (0, j + N_FT))
    b_g = pl.BlockSpec((1, FFN_TN), lambda j, i: (0, j))
    b_v = pl.BlockSpec((1, FFN_TN), lambda j, i: (0, j + N_FT))
    return [cur_g, cur_v, halo_g, halo_v, w_g, w_v, b_g, b_v]


def matmul(a, b, kind, out_dtype, tm, tn, name, b_rows=None):
    stacked = b_rows is not None
    b_shape = (N_DEV * b_rows, b.shape[2]) if stacked else b.shape
    if kind == "nn":
        (m, k), n = a.shape, b_shape[1]
        a_spec = pl.BlockSpec((tm, k), lambda j, i: (i, 0))
        b_spec = pl.BlockSpec((k, tn), lambda j, i: (0, j))
        dims = (((1,), (0,)), ((), ()))
    elif kind == "nt":
        (m, k), n = a.shape, b_shape[0]
        a_spec = pl.BlockSpec((tm, k), lambda j, i: (i, 0))
        b_spec = pl.BlockSpec((tn, k), lambda j, i: (j, 0))
        dims = (((1,), (1,)), ((), ()))
    else:
        (k, m), n = a.shape, b_shape[1]
        a_spec = pl.BlockSpec((k, tm), lambda j, i: (0, i))
        b_spec = pl.BlockSpec((k, tn), lambda j, i: (0, j))
        dims = (((0,), (0,)), ((), ()))
    assert m % tm == 0 and n % tn == 0, (name, m, n, tm, tn)
    if stacked:
        assert b_spec.block_shape[0] == b_shape[0] and kind in ("nn", "nt")
        width = b_spec.block_shape[1]
        b_spec = pl.BlockSpec((N_DEV, b_rows, width), (lambda j, i: (0, 0, j)) if kind == "nn" else (lambda j, i: (0, 0, 0)))

    def body(a_ref, b_ref, o_ref):
        bb = b_ref[...].reshape(b_shape[0], -1) if stacked else b_ref[...]
        o_ref[...] = lax.dot_general(a_ref[...], bb, dims, preferred_element_type=F32).astype(o_ref.dtype)

    return pl.pallas_call(
        body, out_shape=jax.ShapeDtypeStruct((m, n), out_dtype), grid=(n // tn, m // tm),
        in_specs=[a_spec, b_spec], out_specs=pl.BlockSpec((tm, tn), lambda j, i: (i, j)),
        name=name, compiler_params=_cp("parallel", "parallel"))(a, b)


def matmul_halves(a, b, tm, tn, name):
    _, m, k = a.shape
    n = b.shape[1]
    b3 = b.reshape(2, k, n)

    def body(a_ref, b_ref, o_ref):
        o_ref[...] = (jnp.dot(a_ref[0], b_ref[0], preferred_element_type=F32)
                      + jnp.dot(a_ref[1], b_ref[1], preferred_element_type=F32))

    return pl.pallas_call(
        body, out_shape=jax.ShapeDtypeStruct((m, n), F32), grid=(n // tn, m // tm),
        in_specs=[pl.BlockSpec((2, tm, k), lambda j, i: (0, i, 0)), pl.BlockSpec((2, k, tn), lambda j, i: (0, 0, j))],
        out_specs=pl.BlockSpec((tm, tn), lambda j, i: (i, j)), name=name, compiler_params=_cp("parallel", "parallel"))(a, b3)


def matmul_tn_halves(a, b, tm, name):
    _, k, m = a.shape
    n = b.shape[1]

    def body(a_ref, b_ref, o_ref):
        o_ref[0] = lax.dot_general(a_ref[0], b_ref[...], (((0,), (0,)), ((), ())), preferred_element_type=F32).astype(BF16)

    return pl.pallas_call(
        body, out_shape=jax.ShapeDtypeStruct((2, m, n), BF16), grid=(2, m // tm),
        in_specs=[pl.BlockSpec((1, k, tm), lambda h, i: (h, 0, i)), pl.BlockSpec((k, n), lambda h, i: (0, 0))],
        out_specs=pl.BlockSpec((1, tm, n), lambda h, i: (h, i, 0)), name=name,
        compiler_params=_cp("parallel", "parallel"))(a, b).reshape(2 * m, n)


def _row_spec(width, col=0):
    return pl.BlockSpec((ROWS, width), lambda i: (i, col))


def _vec_spec(width, rows=1):
    return pl.BlockSpec((rows, width), lambda i: (0, 0))


def _ffn_shifted(cur_ref, halo_ref, pad, s1, s2):
    i = pl.program_id(1)
    pad[0:FFN_HALO, :] = jnp.where(i > 0, halo_ref[...], 0.0)
    pad[FFN_HALO:, :] = cur_ref[0:FFN_HALO, :]
    for k, dst in ((1, s1), (2, s2)):
        dst[0:FFN_HALO, :] = pad[pl.ds(FFN_HALO - k, FFN_HALO), :]
        dst[FFN_HALO:, :] = cur_ref[pl.ds(FFN_HALO - k, ROWS - FFN_HALO), :]


def _ffn_conv(rs, cur_ref, s1, s2, w_ref, b_ref):
    return b_ref[...] + w_ref[0:1, :] * s2[rs, :] + w_ref[1:2, :] * s1[rs, :] + w_ref[2:3, :] * cur_ref[rs, :]


def ffn_act_fwd(up0, wf, bf, name):
    def body(g_ref, v_ref, gh_ref, vh_ref, wg_ref, wv_ref, bg_ref, bv_ref, act_ref, pad, g1, g2, v1, v2):
        _ffn_shifted(g_ref, gh_ref, pad, g1, g2)
        _ffn_shifted(v_ref, vh_ref, pad, v1, v2)

        def chunk(rs):
            gate = _ffn_conv(rs, g_ref, g1, g2, wg_ref, bg_ref)
            val = _ffn_conv(rs, v_ref, v1, v2, wv_ref, bv_ref)
            act_ref[rs, :] = (gate * _sig(gate) * val).astype(BF16)

        _for_chunks(chunk)

    tile = pltpu.VMEM((ROWS, FFN_TN), F32)
    return pl.pallas_call(
        body, out_shape=jax.ShapeDtypeStruct((SEQ, D_FF), BF16), grid=(N_FT, SEQ // ROWS),
        in_specs=_ffn_specs(), out_specs=pl.BlockSpec((ROWS, FFN_TN), lambda j, i: (i, j)),
        scratch_shapes=[pltpu.VMEM((2 * FFN_HALO, FFN_TN), F32), tile, tile, tile, tile],
        name=name, compiler_params=_cp("parallel", "parallel"))(up0, up0, up0, up0, wf, wf, bf, bf)


def ffn_bwd(up0, dact, wf, bf, name):
    per = ROWS // FFN_HALO
    last = SEQ // FFN_HALO - 1

    def body(g_ref, v_ref, gh_ref, vh_ref, wg_ref, wv_ref, bg_ref, bv_ref, da_ref, gn_ref, vn_ref, dan_ref,
             out_ref, dbg_ref, dbv_ref, dwg_ref, dwv_ref, pad, g1, g2, v1, v2, dgp, dvp, acc):
        i = pl.program_id(1)
        first = i == 0
        _ffn_shifted(g_ref, gh_ref, pad, g1, g2)
        _ffn_shifted(v_ref, vh_ref, pad, v1, v2)
        acc[...] = jnp.zeros_like(acc)

        def grads(gate, val, da):
            s = _sig(gate)
            return da * val * (s * (1.0 + gate * (1.0 - s))), da * (gate * s)

        def chunk(rs):
            gate = _ffn_conv(rs, g_ref, g1, g2, wg_ref, bg_ref)
            val = _ffn_conv(rs, v_ref, v1, v2, wv_ref, bv_ref)
            dgate, dval = grads(gate, val, da_ref[rs, :])
            dgp[rs, :] = dgate
            dvp[rs, :] = dval
            acc[0] += dgate
            acc[1] += dval
            for t, (sg, sv) in enumerate(((g2, v2), (g1, v1), (g_ref, v_ref))):
                acc[2 + t] += dgate * sg[rs, :]
                acc[5 + t] += dval * sv[rs, :]

        _for_chunks(chunk)
        _acc(dbg_ref, _rsum(acc[0]), first)
        _acc(dbv_ref, _rsum(acc[1]), first)
        _acc(dwg_ref, jnp.concatenate([_rsum(acc[2 + t]) for t in range(FFN_K)], axis=0), first)
        _acc(dwv_ref, jnp.concatenate([_rsum(acc[5 + t]) for t in range(FFN_K)], axis=0), first)

        def conv_next(cur_ref, nxt_ref, w_ref, b_ref):
            pad[0:FFN_HALO, :] = cur_ref[ROWS - FFN_HALO:, :]
            pad[FFN_HALO:, :] = nxt_ref[...]
            return (b_ref[...] + w_ref[0:1, :] * pad[pl.ds(FFN_HALO - 2, FFN_HALO), :]
                    + w_ref[1:2, :] * pad[pl.ds(FFN_HALO - 1, FFN_HALO), :] + w_ref[2:3, :] * nxt_ref[...])

        gate_n = conv_next(g_ref, gn_ref, wg_ref, bg_ref)
        val_n = conv_next(v_ref, vn_ref, wv_ref, bv_ref)
        dgate_n, dval_n = grads(gate_n, val_n, dan_ref[...])
        inside = i < SEQ // ROWS - 1
        dgp[ROWS:, :] = jnp.where(inside, dgate_n, 0.0)
        dvp[ROWS:, :] = jnp.where(inside, dval_n, 0.0)

        for half, (dp, s1, s2, w_ref) in enumerate(((dgp, g1, g2, wg_ref), (dvp, v1, v2, wv_ref))):
            s1[...] = dp[pl.ds(1, ROWS), :]
            s2[...] = dp[pl.ds(2, ROWS), :]

            def back(rs, dp=dp, s1=s1, s2=s2, w_ref=w_ref, half=half):
                out_ref[half, rs, :] = (w_ref[2:3, :] * dp[rs, :] + w_ref[1:2, :] * s1[rs, :]
                                        + w_ref[0:1, :] * s2[rs, :]).astype(BF16)

            _for_chunks(back)

    tile = pltpu.VMEM((ROWS, FFN_TN), F32)
    ext = pltpu.VMEM((ROWS + FFN_HALO, FFN_TN), F32)
    vec = jax.ShapeDtypeStruct((1, D_FF), F32)
    taps = jax.ShapeDtypeStruct((FFN_K, D_FF), F32)
    cur = pl.BlockSpec((ROWS, FFN_TN), lambda j, i: (i, j))
    nxt = lambda off: pl.BlockSpec((FFN_HALO, FFN_TN), lambda j, i: (jnp.minimum((i + 1) * per, last), j + off))
    vs = pl.BlockSpec((1, FFN_TN), lambda j, i: (0, j))
    ts = pl.BlockSpec((FFN_K, FFN_TN), lambda j, i: (0, j))
    return pl.pallas_call(
        body, out_shape=(jax.ShapeDtypeStruct((2, SEQ, D_FF), BF16), vec, vec, taps, taps), grid=(N_FT, SEQ // ROWS),
        in_specs=_ffn_specs() + [cur, nxt(0), nxt(N_FT), nxt(0)],
        out_specs=(pl.BlockSpec((2, ROWS, FFN_TN), lambda j, i: (0, i, j)), vs, vs, ts, ts),
        scratch_shapes=[pltpu.VMEM((2 * FFN_HALO, FFN_TN), F32), tile, tile, tile, tile, ext, ext,
                        pltpu.VMEM((2 + 2 * FFN_K, SUB, FFN_TN), F32)],
        name=name, compiler_params=_cp("parallel", "arbitrary"))(up0, up0, up0, up0, wf, wf, bf, bf, dact, up0, up0, dact)


def ada_fwd(c_all, w_ada, b_cols, name):
    def body(c_ref, w_ref, b_ref, o_ref):
        cc = c_ref[...]
        sc = (cc * _sig(cc)).astype(BF16)
        o_ref[...] = jnp.dot(sc, w_ref[...].astype(BF16), preferred_element_type=F32) + b_ref[...]

    return pl.pallas_call(body, out_shape=jax.ShapeDtypeStruct((N_DEV, w_ada.shape[1]), F32), name=name,
                          compiler_params=_cp())(c_all, w_ada, b_cols)


def _adam(w, g, m, v):
    m = ADAM_B1 * m + (1.0 - ADAM_B1) * g
    v = ADAM_B2 * v + (1.0 - ADAM_B2) * (g * g)
    m_hat = m / (1.0 - ADAM_B1 ** ADAM_STEP)
    v_hat = v / (1.0 - ADAM_B2 ** ADAM_STEP)
    delta = -ADAM_LR * (m_hat / (jnp.sqrt(v_hat) + ADAM_EPS) + ADAM_WD * w)
    return delta, m, v


def ada_bwd_adamw(c_all_t, dmod_cols, w, m, v, name):
    rows, cols = w.shape
    tr = 256

    def body(ct_ref, dm_ref, w_ref, m_ref, v_ref, g_ref, d_ref, nm_ref, nv_ref):
        def chunk(rs):
            ct = ct_ref[rs, :]
            sc = ct * _sig(ct)
            g = sc[:, 0:1] * dm_ref[0:1, :]
            for b in range(1, N_DEV):
                g = g + sc[:, b:b + 1] * dm_ref[b:b + 1, :]
            g_ref[rs, :] = g
            d_ref[rs, :], nm_ref[rs, :], nv_ref[rs, :] = _adam(w_ref[rs, :], g, m_ref[rs, :], v_ref[rs, :])

        _for_chunks(chunk, 2, tr)

    blk = pl.BlockSpec((tr, cols), lambda i: (i, 0))
    shp = jax.ShapeDtypeStruct((rows, cols), F32)
    return pl.pallas_call(
        body, out_shape=(shp, shp, shp, shp), grid=(rows // tr,),
        in_specs=[pl.BlockSpec((tr, N_DEV), lambda i: (i, 0)), pl.BlockSpec((N_DEV, cols), lambda i: (0, 0)), blk, blk, blk],
        out_specs=(blk, blk, blk, blk), name=name, compiler_params=_cp("parallel"))(c_all_t, dmod_cols, w, m, v)


def sum_adamw(parts, own, w, m, v, tr, name):
    n_parts, rows, cols = parts.shape

    def body(*refs):
        if own is None:
            p_ref, w_ref, m_ref, v_ref, g_ref, d_ref, nm_ref, nv_ref = refs
        else:
            p_ref, own_ref, w_ref, m_ref, v_ref, g_ref, d_ref, nm_ref, nv_ref = refs

        def chunk(rs):
            g = (p_ref[0, rs, :] if own is None else own_ref[rs, :]).astype(F32)
            for k in range(1, n_parts):
                g = g + p_ref[k, rs, :].astype(F32)
            g_ref[rs, :] = g
            d_ref[rs, :], nm_ref[rs, :], nv_ref[rs, :] = _adam(w_ref[rs, :], g, m_ref[rs, :], v_ref[rs, :])

        _for_chunks(chunk, 2 if tr > SUB else 1, tr)

    blk = pl.BlockSpec((tr, cols), lambda i: (i, 0))
    shp = jax.ShapeDtypeStruct((rows, cols), F32)
    args = [parts] + ([] if own is None else [own]) + [w, m, v]
    return pl.pallas_call(
        body, out_shape=(shp, shp, shp, shp), grid=(rows // tr,),
        in_specs=[pl.BlockSpec((n_parts, tr, cols), lambda i: (0, i, 0))] + [blk] * (len(args) - 1),
        out_specs=(blk, blk, blk, blk), name=name, compiler_params=_cp("parallel"))(*args)


MESH = pl.DeviceIdType.MESH
ANY = pl.BlockSpec(memory_space=pl.ANY)


def all_gather(block, name, after=None):
    extra = () if after is None else (after,)

    def body(x_ref, *refs):
        out_ref, send_sems, recv_sems, local_sem = refs[len(extra):]
        x, y, c = lax.axis_index("x"), lax.axis_index("y"), lax.axis_index("c")
        me, sibling = (x, y, c), (x, y, 1 - c)
        chips = [(1 - x, y), (x, 1 - y), (1 - x, 1 - y)]

        def slot(px, py, pc):
            return out_ref.at[4 * px + 2 * py + pc]

        def copy(k, blk, to, src=None):
            return pltpu.make_async_remote_copy(
                src_ref=slot(*blk) if src is None else src, dst_ref=slot(*blk),
                send_sem=send_sems.at[k], recv_sem=recv_sems.at[k], device_id=to, device_id_type=MESH)

        mine = pltpu.make_async_copy(x_ref, slot(*me), local_sem)
        mine.start()
        first = [copy(0, me, sibling, src=x_ref)]
        first += [copy(1 + j, me, (*chip, c), src=x_ref) for j, chip in enumerate(chips)]
        for cp in first:
            cp.start()
        passed = [copy(4 + j, (*chip, c), sibling) for j, chip in enumerate(chips)]
        for j, chip in enumerate(chips):
            copy(1 + j, (*chip, c), me).wait_recv()
            passed[j].start()
        copy(0, sibling, me).wait_recv()
        for j, chip in enumerate(chips):
            copy(4 + j, (*chip, 1 - c), me).wait_recv()
        for cp in first + passed:
            cp.wait_send()
        mine.wait()

    return pl.pallas_call(
        body, out_shape=jax.ShapeDtypeStruct((N_DEV,) + block.shape, block.dtype), in_specs=[ANY] * (1 + len(extra)), out_specs=ANY,
        scratch_shapes=[pltpu.SemaphoreType.DMA((7,)), pltpu.SemaphoreType.DMA((7,)), pltpu.SemaphoreType.DMA],
        name=name)(block, *extra)


HBM = pl.BlockSpec(memory_space=pltpu.HBM)
SEM = pl.BlockSpec(memory_space=pltpu.SEMAPHORE)
EFFECT = pltpu.SideEffectType.DATAFLOW_SIDE_EFFECTING


def _peer_copies(src_ref, land_ref, send_sems, recv_sems, gather):
    x, y, c = lax.axis_index("x"), lax.axis_index("y"), lax.axis_index("c")
    me = 4 * x + 2 * y + c
    copies = []
    for k in range(1, N_DEV):
        px = 1 - x if k & 4 else x
        py = 1 - y if k & 2 else y
        pc = 1 - c if k & 1 else c
        copies.append(pltpu.make_async_remote_copy(
            src_ref=src_ref if gather else src_ref.at[4 * px + 2 * py + pc],
            dst_ref=land_ref.at[me] if gather else land_ref.at[k],
            send_sem=send_sems.at[k - 1], recv_sem=recv_sems.at[k - 1], device_id=(px, py, pc), device_id_type=MESH))
    return copies


def exchange_start(srcs, gather, name, after=None):
    n = len(srcs)
    land_shapes = [(N_DEV,) + src.shape if gather else src.shape for src in srcs]
    extra = () if after is None else (after,)

    def body(*refs):
        src_refs, land_refs = refs[0:n], refs[n:2 * n]
        outs = refs[2 * n + len(extra):]
        for k in range(n):
            for cp in _peer_copies(src_refs[k], land_refs[k], outs[4 * k], outs[4 * k + 1], gather):
                cp.start()
        token = outs[4 * n]
        token[...] = jnp.zeros_like(token)

    out_shape, out_specs, aliases = [], [], {}
    for k, src in enumerate(srcs):
        out_shape += [pltpu.SemaphoreType.DMA((N_DEV - 1,)), pltpu.SemaphoreType.DMA((N_DEV - 1,)),
                      pltpu.HBM(src.shape, src.dtype), pltpu.HBM(land_shapes[k], src.dtype)]
        out_specs += [SEM, SEM, HBM, HBM]
        aliases[k] = 4 * k + 2
        aliases[n + k] = 4 * k + 3
    out_shape.append(jax.ShapeDtypeStruct((8, 128), F32))
    out_specs.append(pl.BlockSpec(memory_space=pltpu.VMEM))
    res = pl.pallas_call(
        body, name=name, out_shape=tuple(out_shape), in_specs=(HBM,) * (2 * n) + (ANY,) * len(extra),
        out_specs=tuple(out_specs), input_output_aliases=aliases,
        compiler_params=pltpu.CompilerParams(has_side_effects=EFFECT),
    )(*[pltpu.with_memory_space_constraint(src, pltpu.HBM) for src in srcs],
      *[pltpu.with_memory_space_constraint(lax.empty(shp, src.dtype), pltpu.HBM) for shp, src in zip(land_shapes, srcs)],
      *extra)
    return [tuple(res[4 * k:4 * k + 4]) for k in range(n)], res[4 * n][0, 0]


def _stage1_peer(i):
    x, y, c = lax.axis_index("x"), lax.axis_index("y"), lax.axis_index("c")
    if i == 0:
        return (x, y, 1 - c)
    return (1 - x if i & 1 else x, 1 - y if i & 2 else y, c)


def _slot_of(peer):
    return 4 * peer[0] + 2 * peer[1] + peer[2]


def _stage1_copy(i, src_ref, land_ref, send_sems, recv_sems):
    me = _slot_of((lax.axis_index("x"), lax.axis_index("y"), lax.axis_index("c")))
    return pltpu.make_async_remote_copy(src_ref=src_ref, dst_ref=land_ref.at[me], send_sem=send_sems.at[i],
                                        recv_sem=recv_sems.at[i], device_id=_stage1_peer(i), device_id_type=MESH)


def _stage2_copy(j, land_ref, send_sems, recv_sems):
    slot = _slot_of(_stage1_peer(j + 1))
    return pltpu.make_async_remote_copy(src_ref=land_ref.at[slot], dst_ref=land_ref.at[slot], send_sem=send_sems.at[j],
                                        recv_sem=recv_sems.at[j], device_id=_stage1_peer(0), device_id_type=MESH)


def gather2_start(srcs, name, after=None):
    n = len(srcs)
    extra = () if after is None else (after,)

    def body(*refs):
        src_refs, land_refs = refs[0:n], refs[n:2 * n]
        outs = refs[2 * n + len(extra):]
        for k in range(n):
            for i in range(4):
                _stage1_copy(i, src_refs[k], land_refs[k], outs[4 * k], outs[4 * k + 1]).start()
        outs[4 * n][...] = jnp.zeros((8, 128), F32)

    out_shape, out_specs, aliases = [], [], {}
    for k, src in enumerate(srcs):
        out_shape += [pltpu.SemaphoreType.DMA((4,)), pltpu.SemaphoreType.DMA((4,)),
                      pltpu.HBM(src.shape, src.dtype), pltpu.HBM((N_DEV,) + src.shape, src.dtype)]
        out_specs += [SEM, SEM, HBM, HBM]
        aliases[k] = 4 * k + 2
        aliases[n + k] = 4 * k + 3
    out_shape.append(jax.ShapeDtypeStruct((8, 128), F32))
    out_specs.append(pl.BlockSpec(memory_space=pltpu.VMEM))
    res = pl.pallas_call(
        body, name=name, out_shape=tuple(out_shape), in_specs=(HBM,) * (2 * n) + (ANY,) * len(extra),
        out_specs=tuple(out_specs), input_output_aliases=aliases,
        compiler_params=pltpu.CompilerParams(has_side_effects=EFFECT),
    )(*[pltpu.with_memory_space_constraint(src, pltpu.HBM) for src in srcs],
      *[pltpu.with_memory_space_constraint(lax.empty((N_DEV,) + src.shape, src.dtype), pltpu.HBM) for src in srcs], *extra)
    return [dict(send1=res[4 * k], recv1=res[4 * k + 1], src=res[4 * k + 2], land=res[4 * k + 3]) for k in range(n)], \
        res[4 * n][0, 0]


def gather2_pass(handles, after, name):
    n = len(handles)

    def body(*refs):
        src_refs, land_refs, recv1 = refs[0:n], refs[n:2 * n], refs[2 * n:3 * n]
        outs = refs[3 * n + 1:]
        for k in range(n):
            for j in range(3):
                _stage1_copy(j + 1, src_refs[k], land_refs[k], recv1[k], recv1[k]).wait_recv()
                _stage2_copy(j, land_refs[k], outs[3 * k], outs[3 * k + 1]).start()

    out_shape, out_specs, aliases = [], [], {}
    for k, h in enumerate(handles):
        out_shape += [pltpu.SemaphoreType.DMA((3,)), pltpu.SemaphoreType.DMA((3,)), pltpu.HBM(h["land"].shape, h["land"].dtype)]
        out_specs += [SEM, SEM, HBM]
        aliases[n + k] = 3 * k + 2
    res = pl.pallas_call(
        body, name=name, out_shape=tuple(out_shape), in_specs=(HBM,) * (2 * n) + (SEM,) * n + (ANY,),
        out_specs=tuple(out_specs), input_output_aliases=aliases,
        compiler_params=pltpu.CompilerParams(has_side_effects=EFFECT),
    )(*[h["src"] for h in handles], *[h["land"] for h in handles], *[h["recv1"] for h in handles], after)
    return [dict(h, send2=res[3 * k], recv2=res[3 * k + 1], land=res[3 * k + 2]) for k, h in enumerate(handles)]


def gather2_wait(h, after, name):
    def body(src_ref, land_ref, send1, recv1, send2, recv2, after_ref, src_dead, got_ref):
        for i in range(4):
            _stage1_copy(i, src_ref, land_ref, send1, recv1).wait_send()
        _stage1_copy(0, src_ref, land_ref, send1, recv1).wait_recv()
        for j in range(3):
            cp = _stage2_copy(j, land_ref, send2, recv2)
            cp.wait_send()
            cp.wait_recv()

    return pl.pallas_call(
        body, name=name,
        out_shape=(pltpu.HBM(h["src"].shape, h["src"].dtype), pltpu.HBM(h["land"].shape, h["land"].dtype)),
        in_specs=(HBM, HBM, SEM, SEM, SEM, SEM, ANY), out_specs=(HBM, HBM), input_output_aliases={0: 0, 1: 1},
        compiler_params=pltpu.CompilerParams(has_side_effects=EFFECT),
    )(h["src"], h["land"], h["send1"], h["recv1"], h["send2"], h["recv2"], after)[1]


def exchange_wait(handles, after, gather, name):
    send_sems, recv_sems, src_thru, land_thru = handles

    def body(src_ref, land_ref, send_sems, recv_sems, after_ref, src_dead, got_ref):
        for cp in _peer_copies(src_ref, land_ref, send_sems, recv_sems, gather):
            cp.wait_send()
            cp.wait_recv()

    return pl.pallas_call(
        body, name=name,
        out_shape=(pltpu.HBM(src_thru.shape, src_thru.dtype), pltpu.HBM(land_thru.shape, land_thru.dtype)),
        in_specs=(HBM, HBM, SEM, SEM, ANY), out_specs=(HBM, HBM), input_output_aliases={0: 0, 1: 1},
        compiler_params=pltpu.CompilerParams(has_side_effects=EFFECT),
    )(src_thru, land_thru, send_sems, recv_sems, after)[1]


def _to_pattern(a, r):
    return a.reshape(SEQ // r, r * a.shape[1])


def local_step(x, tgt, mod, get_w, put_grad, wc, wf, g_mix, bc, lg, lb, gco, gao, g_ffn, bf, g_fin):
    w_in = get_w("w_in", mod)
    proj, h1 = rms_mod_matmul(x, g_mix, mod, 0, 1, w_in, D_IN // N_DEV, "proj_fwd")
    mix_a, u1 = conv_module_fwd(proj, wc, bc, lg, lb, gco, "conv_module_fwd")
    att, lse = attn_fwd_all(proj, "attn_fwd")
    w_out = get_w("w_out", att)
    y1, mixed = norm_concat_matmul(mix_a, att, gao, w_out, "out_proj_fwd")
    w_up = get_w("w_up", y1)
    up0, x1, h2 = resid_rms_mod_matmul(x, y1, g_ffn, mod, 2, 3, 4, w_up, "up_fwd")
    act = ffn_act_fwd(up0, wf, bf, "ffn_act_fwd")
    w_down = get_w("w_down", act)
    loss_t, dx2, dy2, d_gfin, d_gaf = matmul_loss_bwd(act, w_down, x1, tgt, g_fin, mod, 5, "down_fwd_loss")
    dact = matmul(dy2, w_down, "nt", F32, 512, FFN_TN, "down_bwd_x")
    dw_down = matmul(act, dy2, "tn", BF16, 256, D_MODEL, "down_bwd_w")
    dup0, dbf_g, dbf_v, dwf_g, dwf_v = ffn_bwd(up0, dact, wf + put_grad("w_down", dw_down), bf, "ffn_bwd")
    dw_up = matmul_tn_halves(dup0, h2, 256, "up_bwd_w")
    dx1, d_shf, d_scf, d_gffn, dy1, d_gam = matmul_rms_mod_bwd(
        dup0, w_up, x1, dx2, g_ffn + put_grad("w_up", dw_up), mod, 4, y1, 2, "up_bwd_x")
    dw_out = matmul(mixed, dy1, "tn", BF16, 256, D_MODEL, "out_proj_bwd_w")
    dmixed, do, dd, d_gao = matmul_combine_bwd(dy1, w_out, att, gao + put_grad("w_out", dw_out), "out_proj_bwd_x")
    dqkv = attn_bwd_all(proj, do, lse, dd, "attn_bwd")
    du1, d_gco, d_lg, d_lb, d_bc, d_wc = conv_module_bwd_a(proj, u1, dmixed, lg, lb, gco, "conv_module_bwd_a")
    dproj_a = conv_module_bwd_b(proj, du1, wc, "conv_module_bwd_b")
    dproj = jnp.concatenate([dproj_a, dqkv[0], dqkv[1], dqkv[2]], axis=1)
    dw_in = matmul(dproj, h1, "tn", BF16, 512, D_MODEL, "proj_bwd_w")
    dx, d_shm, d_scm, d_gmix = matmul_rms_mod_bwd(
        dproj, w_in, x, dx1, g_mix + put_grad("w_in", dw_in), mod, 1, None, 0, "proj_bwd_x", b_rows=D_IN // N_DEV)
    dmod = jnp.concatenate([d_shm, d_scm, d_gam, d_shf, d_scf, d_gaf], axis=1)
    small = dict(g_norm_mix=d_gmix, b_conv_dw=d_bc, ln_conv_g=d_lg, ln_conv_b=d_lb, g_conv_out=d_gco, g_attn_out=d_gao,
                 g_norm_ffn=d_gffn, b_ffn_dw=jnp.concatenate([dbf_g, dbf_v], axis=1), g_final=d_gfin,
                 w_conv_dw=d_wc, w_ffn_dw=jnp.concatenate([dwf_g, dwf_v], axis=1), dmod=dmod, loss=loss_t[0:1, 0:1])
    return dx, small


def _padw(a, width):
    return jnp.pad(a, ((0, 0), (0, width - a.shape[1])))


def pack_small(t):
    wide = jnp.concatenate([_padw(t["dmod"], PACK_W), _padw(t["b_ffn_dw"], PACK_W), _padw(t["w_ffn_dw"], PACK_W),
                            _padw(t["loss"], PACK_W), jnp.zeros((2, PACK_W), F32)], axis=0)
    z512 = jnp.zeros((1, 512), F32)
    narrow = jnp.concatenate([
        t["g_norm_mix"], t["g_norm_ffn"], t["g_final"],
        jnp.concatenate([t["b_conv_dw"], t["ln_conv_g"]], axis=1),
        jnp.concatenate([t["ln_conv_b"], t["g_conv_out"]], axis=1),
        jnp.concatenate([t["g_attn_out"], z512], axis=1),
        jnp.zeros((2, 1024), F32),
        jnp.pad(t["w_conv_dw"], ((0, 1), (0, 0))).reshape(16, 1024)], axis=0)
    return jnp.concatenate([wide, narrow.reshape(4, PACK_W), jnp.zeros((4, PACK_W), F32)], axis=0)


_NARROW = lambda k, off=0: (8 + k // 6, (k % 6) * 1024 + off)
PACKED_AT = dict(
    b_ada=(0, 0, N_MOD * D_MODEL), b_ffn_dw=(1, 0, 2 * D_FF),
    g_norm_mix=_NARROW(0) + (D_MODEL,), g_norm_ffn=_NARROW(1) + (D_MODEL,), g_final=_NARROW(2) + (D_MODEL,),
    b_conv_dw=_NARROW(3) + (D_CONV,), ln_conv_g=_NARROW(3, 512) + (D_CONV,),
    ln_conv_b=_NARROW(4) + (D_CONV,), g_conv_out=_NARROW(4, 512) + (D_CONV,), g_attn_out=_NARROW(5) + (D_ATTN,))
SMALL_ORDER = list(PACKED_AT)


def small_adamw(parts, wmv, name):
    def body(*refs):
        p_ref = refs[0]
        ins = refs[1:1 + 3 * len(SMALL_ORDER)]
        outs = refs[1 + 3 * len(SMALL_ORDER):]
        g = p_ref[0]
        for k in range(1, N_DEV):
            g = g + p_ref[k]
        for i, n in enumerate(SMALL_ORDER):
            row, lane, width = PACKED_AT[n]
            gp = g[row:row + 1, lane:lane + width]
            w_ref, m_ref, v_ref = ins[3 * i:3 * i + 3]
            g_ref, d_ref, nm_ref, nv_ref = outs[4 * i:4 * i + 4]
            g_ref[...] = gp
            d_ref[...], nm_ref[...], nv_ref[...] = _adam(w_ref[...], gp, m_ref[...], v_ref[...])
        wc_ref, wf_ref, loss_ref = outs[4 * len(SMALL_ORDER):]
        for j in range(CONV_K):
            row, lane = _NARROW(8 + j // 2, (j % 2) * 512)
            wc_ref[j:j + 1, :] = g[row:row + 1, lane:lane + D_CONV]
        wf_ref[...] = g[2:2 + FFN_K, 0:2 * D_FF]
        loss_ref[...] = jnp.broadcast_to(g[5:6, 0:1], (8, 128))

    args, out_shape = [parts], []
    for n in SMALL_ORDER:
        args += list(wmv[n])
        out_shape += [jax.ShapeDtypeStruct(wmv[n][0].shape, F32)] * 4
    out_shape += [jax.ShapeDtypeStruct((CONV_K, D_CONV), F32), jax.ShapeDtypeStruct((FFN_K, 2 * D_FF), F32),
                  jax.ShapeDtypeStruct((8, 128), F32)]
    res = pl.pallas_call(body, out_shape=tuple(out_shape), name=name, compiler_params=_cp())(*args)
    per = {n: tuple(res[4 * i:4 * i + 4]) for i, n in enumerate(SMALL_ORDER)}
    return per, res[-3], res[-2], res[-1][0, 0]


def shard_adamw(items, name):
    def body(*refs):
        ins, outs = refs[:4 * len(items)], refs[4 * len(items):]
        for i in range(len(items)):
            g_ref, w_ref, m_ref, v_ref = ins[4 * i:4 * i + 4]
            og_ref, d_ref, nm_ref, nv_ref = outs[4 * i:4 * i + 4]
            og_ref[...] = g_ref[...]
            d_ref[...], nm_ref[...], nv_ref[...] = _adam(w_ref[...], g_ref[...], m_ref[...], v_ref[...])

    args = [a for item in items for a in item]
    out_shape = tuple(jax.ShapeDtypeStruct(item[1].shape, F32) for item in items for _ in range(4))
    res = pl.pallas_call(body, out_shape=out_shape, name=name, compiler_params=_cp())(*args)
    return [tuple(res[4 * i:4 * i + 4]) for i in range(len(items))]


def _shard(full, n_cols, me):
    return lax.dynamic_slice(full, (0, me * n_cols), (full.shape[0], n_cols))


WEIGHTS = ["w_ada", "b_ada", "g_norm_mix", "w_in", "w_conv_dw", "b_conv_dw", "ln_conv_g", "ln_conv_b", "g_conv_out",
           "g_attn_out", "w_out", "g_norm_ffn", "w_up", "w_ffn_dw", "b_ffn_dw", "w_down", "g_final"]
SMALL_REPLICATED = ["g_norm_mix", "b_conv_dw", "ln_conv_g", "ln_conv_b", "g_conv_out", "g_attn_out", "g_norm_ffn",
                    "b_ffn_dw", "g_final"]


def kernel(x, c, w_ada, b_ada, g_norm_mix, w_in, w_conv_dw, b_conv_dw, ln_conv_g, ln_conv_b, g_conv_out, g_attn_out, w_out, g_norm_ffn, w_up, w_ffn_dw, b_ffn_dw, w_down, g_final, loss_target, m_w_ada, m_b_ada, m_g_norm_mix, m_w_in, m_w_conv_dw, m_b_conv_dw, m_ln_conv_g, m_ln_conv_b, m_g_conv_out, m_g_attn_out, m_w_out, m_g_norm_ffn, m_w_up, m_w_ffn_dw, m_b_ffn_dw, m_w_down, m_g_final, v_w_ada, v_b_ada, v_g_norm_mix, v_w_in, v_w_conv_dw, v_b_conv_dw, v_ln_conv_g, v_ln_conv_b, v_g_conv_out, v_g_attn_out, v_w_out, v_g_norm_ffn, v_w_up, v_w_ffn_dw, v_b_ffn_dw, v_w_down, v_g_final):
    args = dict(locals())
    me = 4 * lax.axis_index("x") + 2 * lax.axis_index("y") + lax.axis_index("c")

    def flat(name, prefix=""):
        a = args[prefix + name]
        return a.reshape(a.shape[-2] if a.ndim > 1 else 1, a.shape[-1])

    def flat_t(name, prefix=""):
        return args[prefix + name][0].T

    n_in, n_up, r_out, r_down = w_in.shape[2], w_up.shape[2], w_out.shape[1], w_down.shape[1]
    n_ada, n_wc, n_wf = w_ada.shape[2], w_conv_dw.shape[2], w_ffn_dw.shape[2]
    taps_c = jnp.pad(flat("w_conv_dw").reshape(1, CONV_K * n_wc), ((0, 0), (0, 2 * D_MODEL - CONV_K * n_wc)))
    taps_f = jnp.pad(flat("w_ffn_dw").reshape(1, FFN_K * n_wf), ((0, 0), (0, 3 * D_MODEL - FFN_K * n_wf)))
    first = jnp.concatenate([c, taps_c.reshape(2, D_MODEL), taps_f.reshape(3, D_MODEL), jnp.zeros((2, D_MODEL), F32)], axis=0)
    w_in_block = flat_t("w_in").astype(BF16)
    hi = lax.reduce_precision(first, 8, 7)
    mid = lax.reduce_precision(first - hi, 8, 7)
    low = lax.reduce_precision(first - hi - mid, 8, 7)
    terms = jnp.concatenate([hi, mid, low, jnp.zeros((8, D_MODEL), F32)], axis=0).astype(BF16)
    first_block = all_gather(jnp.concatenate([w_in_block, terms], axis=0), "gather_c_taps_w_in")
    terms = first_block[:, n_in:n_in + 24, :].astype(F32).reshape(N_DEV, 3, 8, D_MODEL)
    first_all = (terms[:, 0] + terms[:, 1]) + terms[:, 2]
    c_all = first_all[:, 0, :]
    wc_full = first_all[:, 1:3, :].reshape(N_DEV, 2 * D_MODEL)[:, :CONV_K * n_wc].reshape(N_DEV, CONV_K, n_wc)
    wc_full = wc_full.transpose(1, 0, 2).reshape(CONV_K, D_CONV)
    wf_full = first_all[:, 3:6, :].reshape(N_DEV, 3 * D_MODEL)[:, :FFN_K * n_wf].reshape(N_DEV, FFN_K, n_wf)
    wf_full = wf_full.transpose(1, 0, 2).reshape(FFN_K, 2 * D_FF)
    mod_cols = ada_fwd(c_all, flat("w_ada"), _shard(flat("b_ada"), n_ada, me), "ada_fwd")
    mod_all = all_gather(mod_cols, "gather_mod")
    mod = lax.dynamic_index_in_dim(mod_all, me, axis=1, keepdims=False).reshape(N_MOD, D_MODEL)
    mod = jnp.pad(mod, ((0, 2), (0, 0)))

    order = ("w_out", "w_up", "w_down")
    blocks = dict(w_up=flat_t("w_up").astype(BF16), w_out=flat("w_out").astype(BF16), w_down=flat("w_down").astype(BF16))
    handles, tok = gather2_start([blocks[name] for name in order], "gather_weights_start", mod_all)
    gathers = dict(zip(order, handles))
    mod = mod + tok

    def gathered(name, after):
        if "send2" not in gathers[name]:
            group = ("w_out", "w_up") if name != "w_down" else ("w_down",)
            gathers.update(zip(group, gather2_pass([gathers[w] for w in group], after, f"gather_{name}_pass")))
        land = gather2_wait(gathers[name], after, f"gather_{name}_wait")
        return lax.dynamic_update_index_in_dim(land, blocks[name], me, axis=0)

    def get_w(name, after):
        if name == "w_in":
            return first_block
        return gathered(name, after).reshape(-1, D_MODEL)

    exchanges, own = {}, {}

    def put_grad(name, dw, after=None):
        dev_major = dw.reshape(N_DEV, -1, D_MODEL)
        own[name] = lax.dynamic_index_in_dim(dev_major, me, axis=0, keepdims=False)
        (exchanges[name],), token = exchange_start([dev_major], False, f"exchange_{name}_start", after)
        return token

    grad_x, small = local_step(
        x[0], loss_target[0], mod, get_w, put_grad, wc_full, wf_full,
        flat("g_norm_mix"), flat("b_conv_dw"), flat("ln_conv_g"), flat("ln_conv_b"), flat("g_conv_out"),
        flat("g_attn_out"), flat("g_norm_ffn"), flat("b_ffn_dw"), flat("g_final"))

    out = {}

    def finish(name, tr, after):
        parts = exchange_wait(exchanges[name], after, False, f"exchange_{name}_wait")
        if name in ("w_in", "w_up"):
            res = sum_adamw(parts, own[name], flat_t(name), flat_t(name, "m_"), flat_t(name, "v_"), tr, "adamw_" + name)
            out[name] = tuple(r.T for r in res)
        else:
            res = out[name] = sum_adamw(parts, own[name], flat(name), flat(name, "m_"), flat(name, "v_"), tr, "adamw_" + name)
        return res[0]

    after = finish("w_down", r_down, grad_x)
    after = finish("w_up", n_up // 2, after)
    after = finish("w_out", r_out, after)
    after = finish("w_in", n_in, after)

    small_all = all_gather(pack_small(small), "gather_small", after)

    wmv = {n: (flat(n), flat(n, "m_"), flat(n, "v_")) for n in SMALL_ORDER}
    per, g_wc, g_wf, loss = small_adamw(small_all, wmv, "adamw_small")
    out.update(per)
    taps = shard_adamw([(_shard(g_wc, n_wc, me), flat("w_conv_dw"), flat("w_conv_dw", "m_"), flat("w_conv_dw", "v_")),
                        (_shard(g_wf, n_wf, me), flat("w_ffn_dw"), flat("w_ffn_dw", "m_"), flat("w_ffn_dw", "v_"))],
                       "adamw_taps")
    out["w_conv_dw"], out["w_ffn_dw"] = taps

    dmod_cols = _shard(small_all[:, 0, :], n_ada, me)
    out["w_ada"] = ada_bwd_adamw(c_all.T, dmod_cols, flat("w_ada"), flat("w_ada", "m_"), flat("w_ada", "v_"), "adamw_w_ada")

    result = [loss, grad_x[None]]
    for k in range(4):
        result += [out[n][k].reshape(args[n].shape) for n in WEIGHTS]
    return tuple(result)
```

```python
import functools

import jax
import jax.numpy as jnp
from jax import lax
from jax.experimental import pallas as pl
from jax.experimental.pallas import tpu as pltpu

F32 = jnp.float32
BF16 = jnp.bfloat16

N_DEV = 8
SEQ = 2048
D_MODEL = 1024
D_CONV = 512
D_ATTN = 512
HEAD_DIM = 64
CONV_K = 31
D_FF = 2816
FFN_K = 3
D_IN = 2 * D_CONV + 3 * D_ATTN
N_MOD = 6
EPS = 1e-6
ATTN_BLOCK = 128
PATTERNS = ((2048, 1), (512, 4), (128, 16))
NEG = -1e30

ADAM_LR, ADAM_B1, ADAM_B2, ADAM_EPS, ADAM_WD, ADAM_STEP = 0.001, 0.9, 0.999, 1e-08, 0.01, 10

ROWS = 256
CONV_HALO = 32
FFN_HALO = 8
FFN_TN = 1408
VMEM_LIMIT = 56 * 1024 * 1024
PACK_W = 6144


NT = (((1,), (1,)), ((), ()))
ANY = pl.BlockSpec(memory_space=pl.ANY)


def _cp(*sem):
    return pltpu.CompilerParams(dimension_semantics=sem if sem else None, vmem_limit_bytes=VMEM_LIMIT)


def _with_after(body, n_in, after):
    if after is None:
        return body, [], []
    return (lambda *refs: body(*refs[:n_in], *refs[n_in + 1:])), [ANY], [after]


def _sig(x):
    return 1.0 / (1.0 + jnp.exp(-x))


def _rsum(x):
    return jnp.sum(x, axis=0, keepdims=True)


def _mean(x):
    return jnp.mean(x, axis=-1, keepdims=True)


def _acc(ref, val, first):
    @pl.when(first)
    def _():
        ref[...] = val

    @pl.when(jnp.logical_not(first))
    def _():
        ref[...] += val


SUB = 16


def _for_chunks(fn, unroll=1, rows=ROWS):
    def step(i, carry):
        fn(pl.ds(pl.multiple_of(i * SUB, SUB), SUB))
        return carry

    lax.fori_loop(0, rows // SUB, step, 0, unroll=unroll)


def rms_mod_matmul(x, g, mod, sh_row, sc_row, b, b_rows, name, after=None):
    n = N_DEV * b_rows

    def body(x_ref, g_ref, mod_ref, b_ref, o_ref, h_ref):
        xx = x_ref[...]
        r = lax.rsqrt(_mean(xx * xx) + EPS)
        h = (xx * r * g_ref[...] * (1.0 + mod_ref[sc_row:sc_row + 1, :]) + mod_ref[sh_row:sh_row + 1, :]).astype(BF16)
        h_ref[...] = h
        o_ref[...] = lax.dot_general(h, b_ref[...].reshape(n, D_MODEL), NT, preferred_element_type=F32)

    body, more_specs, more = _with_after(body, 4, after)
    return pl.pallas_call(
        body, out_shape=(jax.ShapeDtypeStruct((SEQ, n), F32), jax.ShapeDtypeStruct((SEQ, D_MODEL), BF16)),
        grid=(SEQ // ROWS,),
        in_specs=[_row_spec(D_MODEL), _vec_spec(D_MODEL), _vec_spec(D_MODEL, 8),
                  pl.BlockSpec((N_DEV, b_rows, D_MODEL), lambda i: (0, 0, 0))] + more_specs,
        out_specs=(_row_spec(n), _row_spec(D_MODEL)), name=name, compiler_params=_cp("parallel"))(x, g, mod, b, *more)


def norm_concat_matmul(mix_a, att, gao, w, name):
    def body(a_ref, att_ref, g_ref, w_ref, y_ref, mixed_ref):
        aa = att_ref[...]
        mixed_ref[:, 0:D_CONV] = a_ref[...]
        mixed_ref[:, D_CONV:] = (aa * lax.rsqrt(_mean(aa * aa) + EPS) * g_ref[...]).astype(BF16)
        y_ref[...] = jnp.dot(mixed_ref[...], w_ref[...], preferred_element_type=F32)

    return pl.pallas_call(
        body, out_shape=(jax.ShapeDtypeStruct((SEQ, D_MODEL), F32), jax.ShapeDtypeStruct((SEQ, D_MODEL), BF16)),
        grid=(SEQ // ROWS,),
        in_specs=[_row_spec(D_CONV), _row_spec(D_ATTN), _vec_spec(D_ATTN), pl.BlockSpec(w.shape, lambda i: (0, 0))],
        out_specs=(_row_spec(D_MODEL), _row_spec(D_MODEL)), name=name, compiler_params=_cp("parallel"))(mix_a, att, gao, w)


def resid_rms_mod_matmul(x, y, g, mod, ga_row, sh_row, sc_row, w, name):
    n = w.shape[0]

    def body(x_ref, y_ref, g_ref, mod_ref, w_ref, o_ref, x1_ref, h_ref):
        x1 = x_ref[...] + mod_ref[ga_row:ga_row + 1, :] * y_ref[...]
        x1_ref[...] = x1
        r = lax.rsqrt(_mean(x1 * x1) + EPS)
        h = (x1 * r * g_ref[...] * (1.0 + mod_ref[sc_row:sc_row + 1, :]) + mod_ref[sh_row:sh_row + 1, :]).astype(BF16)
        h_ref[...] = h
        o_ref[...] = lax.dot_general(h, w_ref[...], NT, preferred_element_type=F32)

    return pl.pallas_call(
        body,
        out_shape=(jax.ShapeDtypeStruct((SEQ, n), F32), jax.ShapeDtypeStruct((SEQ, D_MODEL), F32),
                   jax.ShapeDtypeStruct((SEQ, D_MODEL), BF16)),
        grid=(SEQ // ROWS,),
        in_specs=[_row_spec(D_MODEL), _row_spec(D_MODEL), _vec_spec(D_MODEL), _vec_spec(D_MODEL, 8),
                  pl.BlockSpec(w.shape, lambda i: (0, 0))],
        out_specs=(_row_spec(n), _row_spec(D_MODEL), _row_spec(D_MODEL)),
        name=name, compiler_params=_cp("parallel"))(x, y, g, mod, w)


def matmul_combine_bwd(dy, w, att, gao, name, after=None):
    def body(dy_ref, w_ref, att_ref, g_ref, dm_ref, do_ref, dd_ref, dg_ref):
        first = pl.program_id(0) == 0
        dm_ref[...] = lax.dot_general(dy_ref[...], w_ref[...], NT, preferred_element_type=F32)
        att = att_ref[...]
        r = lax.rsqrt(_mean(att * att) + EPS)
        xn = att * r
        dm = dm_ref[:, D_CONV:]
        _acc(dg_ref, _rsum(dm * xn), first)
        dyn = dm * g_ref[...]
        do = r * (dyn - xn * _mean(dyn * xn))
        do_ref[...] = do
        same_head = (jnp.right_shift(lax.broadcasted_iota(jnp.int32, (D_ATTN, D_ATTN), 0), 6)
                     == jnp.right_shift(lax.broadcasted_iota(jnp.int32, (D_ATTN, D_ATTN), 1), 6)).astype(F32)
        dd_ref[...] = jnp.dot(do * att, same_head, preferred_element_type=F32, precision=lax.Precision.HIGHEST)

    rs = _row_spec(D_ATTN)
    f = jax.ShapeDtypeStruct((SEQ, D_ATTN), F32)
    body, more_specs, more = _with_after(body, 4, after)
    return pl.pallas_call(
        body, out_shape=(jax.ShapeDtypeStruct((SEQ, D_MODEL), F32), f, f, jax.ShapeDtypeStruct((1, D_ATTN), F32)),
        grid=(SEQ // ROWS,),
        in_specs=[_row_spec(D_MODEL), pl.BlockSpec(w.shape, lambda i: (0, 0)), rs, _vec_spec(D_ATTN)] + more_specs,
        out_specs=(_row_spec(D_MODEL), rs, rs, _vec_spec(D_ATTN)),
        name=name, compiler_params=_cp("arbitrary"))(dy, w, att, gao, *more)


def matmul_loss_bwd(act, w, x1, tgt, g, mod, ga_row, name):
    def body(a_ref, w_ref, x1_ref, t_ref, g_ref, mod_ref, loss_ref, dx2_ref, dy2_ref, dg_ref, dga_ref):
        first = pl.program_id(0) == 0
        ga = mod_ref[ga_row:ga_row + 1, :]
        y2 = jnp.dot(a_ref[...], w_ref[...], preferred_element_type=F32)
        x2 = x1_ref[...] + ga * y2
        r = lax.rsqrt(_mean(x2 * x2) + EPS)
        xn = x2 * r
        err = xn * g_ref[...] - t_ref[...]
        _acc(loss_ref, jnp.broadcast_to(0.5 * jnp.sum(_mean(err * err)), (8, 128)), first)
        dy = err * (1.0 / D_MODEL)
        _acc(dg_ref, _rsum(dy * xn), first)
        dxn = dy * g_ref[...]
        dx2 = r * (dxn - xn * _mean(dxn * xn))
        dx2_ref[...] = dx2
        dy2_ref[...] = (dx2 * ga).astype(BF16)
        _acc(dga_ref, _rsum(dx2 * y2), first)

    vec = jax.ShapeDtypeStruct((1, D_MODEL), F32)
    return pl.pallas_call(
        body,
        out_shape=(jax.ShapeDtypeStruct((8, 128), F32), jax.ShapeDtypeStruct((SEQ, D_MODEL), F32),
                   jax.ShapeDtypeStruct((SEQ, D_MODEL), BF16), vec, vec),
        grid=(SEQ // ROWS,),
        in_specs=[_row_spec(act.shape[1]), pl.BlockSpec(w.shape, lambda i: (0, 0)), _row_spec(D_MODEL), _row_spec(D_MODEL),
                  _vec_spec(D_MODEL), _vec_spec(D_MODEL, 8)],
        out_specs=(pl.BlockSpec((8, 128), lambda i: (0, 0)), _row_spec(D_MODEL), _row_spec(D_MODEL),
                   _vec_spec(D_MODEL), _vec_spec(D_MODEL)),
        name=name, compiler_params=_cp("arbitrary"))(act, w, x1, tgt, g, mod)


def matmul_rms_mod_bwd(a, b, x, dres, g, mod, sc_row, y, ga_row, name, b_rows=None, after=None):
    gated = y is not None
    halves = a.ndim == 3
    if halves:
        k2 = a.shape[2]
        b_arg = b.reshape(2, k2, D_MODEL)
        a_spec = pl.BlockSpec((2, ROWS, k2), lambda i: (0, i, 0))
        b_spec = pl.BlockSpec((2, k2, D_MODEL), lambda i: (0, 0, 0))
    else:
        b_arg = b
        a_spec = _row_spec(a.shape[1])
        b_spec = pl.BlockSpec((N_DEV, b_rows, D_MODEL), lambda i: (0, 0, 0))

    def body(*refs):
        if gated:
            a_ref, b_ref, x_ref, dres_ref, g_ref, mod_ref, y_ref, dx_ref, dsh_ref, dsc_ref, dg_ref, dy_ref, dga_ref = refs
        else:
            a_ref, b_ref, x_ref, dres_ref, g_ref, mod_ref, dx_ref, dsh_ref, dsc_ref, dg_ref = refs
        first = pl.program_id(0) == 0
        if halves:
            dh = (jnp.dot(a_ref[0], b_ref[0], preferred_element_type=F32)
                  + jnp.dot(a_ref[1], b_ref[1], preferred_element_type=F32))
        else:
            dh = jnp.dot(a_ref[...], b_ref[...].reshape(N_DEV * b_rows, D_MODEL), preferred_element_type=F32)
        xx = x_ref[...]
        gg = g_ref[...]
        r = lax.rsqrt(_mean(xx * xx) + EPS)
        xn = xx * r
        _acc(dsh_ref, _rsum(dh), first)
        _acc(dsc_ref, _rsum(dh * (xn * gg)), first)
        dt = dh * (1.0 + mod_ref[sc_row:sc_row + 1, :])
        _acc(dg_ref, _rsum(dt * xn), first)
        dxn = dt * gg
        dx = dres_ref[...] + r * (dxn - xn * _mean(dxn * xn))
        dx_ref[...] = dx
        if gated:
            _acc(dga_ref, _rsum(dx * y_ref[...]), first)
            dy_ref[...] = (dx * mod_ref[ga_row:ga_row + 1, :]).astype(BF16)

    vec = jax.ShapeDtypeStruct((1, D_MODEL), F32)
    in_specs = [a_spec, b_spec, _row_spec(D_MODEL), _row_spec(D_MODEL), _vec_spec(D_MODEL), _vec_spec(D_MODEL, 8)]
    out_shape = [jax.ShapeDtypeStruct((SEQ, D_MODEL), F32), vec, vec, vec]
    out_specs = [_row_spec(D_MODEL), _vec_spec(D_MODEL), _vec_spec(D_MODEL), _vec_spec(D_MODEL)]
    args = [a, b_arg, x, dres, g, mod]
    if gated:
        in_specs.append(_row_spec(D_MODEL))
        out_shape += [jax.ShapeDtypeStruct((SEQ, D_MODEL), BF16), vec]
        out_specs += [_row_spec(D_MODEL), _vec_spec(D_MODEL)]
        args.append(y)
    body, more_specs, more = _with_after(body, len(args), after)
    return pl.pallas_call(
        body, out_shape=tuple(out_shape), grid=(SEQ // ROWS,), in_specs=in_specs + more_specs, out_specs=tuple(out_specs),
        name=name, compiler_params=_cp("arbitrary"))(*args, *more)


def _prev_halo(halo, width, col):
    per = ROWS // halo
    return pl.BlockSpec((halo, width), lambda i: (jnp.maximum(i * per - 1, 0), col))


def _next_halo(halo, width, col):
    per = ROWS // halo
    last = SEQ // halo - 1
    return pl.BlockSpec((halo, width), lambda i: (jnp.minimum((i + 1) * per, last), col))


CONV_PAD = ROWS + CONV_HALO


def _shift_copies(sh):
    for b in range(1, 8):
        sh[b, 0:CONV_PAD - 8, :] = sh[0, pl.ds(b, CONV_PAD - 8), :]


def _tap(sh, rs_start, offset):
    return sh[offset % 8, pl.ds(pl.multiple_of(rs_start + (offset // 8) * 8, 8), SUB), :]


def _conv_glu(av_ref, ag_ref, avh_ref, agh_ref, sh):
    i = pl.program_id(0)
    hv = avh_ref[...] * _sig(agh_ref[...])
    sh[0, 0:CONV_HALO, :] = jnp.where(i > 0, hv, 0.0)

    def glu(rs):
        sh[0, pl.ds(pl.multiple_of(rs.start + CONV_HALO, SUB), SUB), :] = av_ref[rs, :] * _sig(ag_ref[rs, :])

    _for_chunks(glu)
    _shift_copies(sh)


def _conv_norm(u1, lg_ref, lb_ref):
    mu = _mean(u1)
    cen = u1 - mu
    rs = lax.rsqrt(_mean(cen * cen) + EPS)
    z = cen * rs
    ln = z * lg_ref[...] + lb_ref[...]
    s = _sig(ln)
    return z, rs, ln, s, ln * s


def conv_module_fwd(proj, wc, bc, lg, lb, gco, name):
    def body(av_ref, ag_ref, avh_ref, agh_ref, wc_ref, bc_ref, lg_ref, lb_ref, gco_ref, out_ref, u1_ref, sh):
        _conv_glu(av_ref, ag_ref, avh_ref, agh_ref, sh)

        def conv(rs):
            u1 = jnp.broadcast_to(bc_ref[...], (SUB, D_CONV))
            for j in range(CONV_K):
                u1 = u1 + wc_ref[j:j + 1, :] * _tap(sh, rs.start, CONV_HALO - (CONV_K - 1) + j)
            u1_ref[rs, :] = u1

        _for_chunks(conv)
        _, _, _, _, u2 = _conv_norm(u1_ref[...], lg_ref, lb_ref)
        rc = lax.rsqrt(_mean(u2 * u2) + EPS)
        out_ref[...] = (u2 * rc * gco_ref[...]).astype(BF16)

    v = _vec_spec(D_CONV)
    return pl.pallas_call(
        body, out_shape=(jax.ShapeDtypeStruct((SEQ, D_CONV), BF16), jax.ShapeDtypeStruct((SEQ, D_CONV), F32)),
        grid=(SEQ // ROWS,),
        in_specs=[_row_spec(D_CONV, 0), _row_spec(D_CONV, 1), _prev_halo(CONV_HALO, D_CONV, 0),
                  _prev_halo(CONV_HALO, D_CONV, 1), _vec_spec(D_CONV, CONV_K), v, v, v, v],
        out_specs=(_row_spec(D_CONV), _row_spec(D_CONV)), scratch_shapes=[pltpu.VMEM((8, CONV_PAD, D_CONV), F32)],
        name=name, compiler_params=_cp("parallel"))(proj, proj, proj, proj, wc, bc, lg, lb, gco)


def conv_module_bwd_a(proj, u1, dmixed, lg, lb, gco, name):
    def body(av_ref, ag_ref, avh_ref, agh_ref, u1_ref, dm_ref, lg_ref, lb_ref, gco_ref,
             du1_ref, dgco_ref, dlg_ref, dlb_ref, dbc_ref, dwc_ref, sh, acc):
        first = pl.program_id(0) == 0
        _conv_glu(av_ref, ag_ref, avh_ref, agh_ref, sh)
        z, rs, ln, s, u2 = _conv_norm(u1_ref[...], lg_ref, lb_ref)
        rc = lax.rsqrt(_mean(u2 * u2) + EPS)
        xn = u2 * rc
        dm = dm_ref[...]
        _acc(dgco_ref, _rsum(dm * xn), first)
        dyn = dm * gco_ref[...]
        du2 = rc * (dyn - xn * _mean(dyn * xn))
        dln = du2 * (s * (1.0 + ln * (1.0 - s)))
        _acc(dlg_ref, _rsum(dln * z), first)
        _acc(dlb_ref, _rsum(dln), first)
        dz = dln * lg_ref[...]
        du1 = rs * (dz - _mean(dz) - z * _mean(dz * z))
        du1_ref[...] = du1
        _acc(dbc_ref, _rsum(du1), first)
        acc[...] = jnp.zeros_like(acc)

        def taps(rs):
            d = du1_ref[rs, :]
            for j in range(CONV_K):
                acc[j] += d * _tap(sh, rs.start, CONV_HALO - (CONV_K - 1) + j)

        _for_chunks(taps)

        @pl.when(first)
        def _():
            dwc_ref[...] = jnp.zeros_like(dwc_ref)

        for j in range(CONV_K):
            dwc_ref[j:j + 1, :] += _rsum(acc[j])

    v = _vec_spec(D_CONV)
    vec = jax.ShapeDtypeStruct((1, D_CONV), F32)
    return pl.pallas_call(
        body,
        out_shape=(jax.ShapeDtypeStruct((SEQ, D_CONV), F32), vec, vec, vec, vec, jax.ShapeDtypeStruct((CONV_K, D_CONV), F32)),
        grid=(SEQ // ROWS,),
        in_specs=[_row_spec(D_CONV, 0), _row_spec(D_CONV, 1), _prev_halo(CONV_HALO, D_CONV, 0),
                  _prev_halo(CONV_HALO, D_CONV, 1), _row_spec(D_CONV, 0), _row_spec(D_CONV, 0), v, v, v],
        out_specs=(_row_spec(D_CONV), v, v, v, v, _vec_spec(D_CONV, CONV_K)),
        scratch_shapes=[pltpu.VMEM((8, CONV_PAD, D_CONV), F32), pltpu.VMEM((CONV_K, SUB, D_CONV), F32)],
        name=name, compiler_params=_cp("arbitrary"))(proj, proj, proj, proj, u1, dmixed, lg, lb, gco)


def conv_module_bwd_b(proj, du1, wc, name):
    def body(av_ref, ag_ref, du1_ref, du1n_ref, wc_ref, out_ref, sh):
        i = pl.program_id(0)
        sh[0, 0:ROWS, :] = du1_ref[...]
        sh[0, ROWS:, :] = jnp.where(i < SEQ // ROWS - 1, du1n_ref[...], 0.0)
        _shift_copies(sh)

        def chunk(rs):
            du0 = jnp.zeros((SUB, D_CONV), F32)
            for j in range(CONV_K):
                du0 = du0 + wc_ref[j:j + 1, :] * _tap(sh, rs.start, CONV_K - 1 - j)
            sg = _sig(ag_ref[rs, :])
            out_ref[rs, 0:D_CONV] = (du0 * sg).astype(BF16)
            out_ref[rs, D_CONV:] = (du0 * av_ref[rs, :] * sg * (1.0 - sg)).astype(BF16)

        _for_chunks(chunk)

    return pl.pallas_call(
        body, out_shape=jax.ShapeDtypeStruct((SEQ, 2 * D_CONV), BF16), grid=(SEQ // ROWS,),
        in_specs=[_row_spec(D_CONV, 0), _row_spec(D_CONV, 1), _row_spec(D_CONV, 0), _next_halo(CONV_HALO, D_CONV, 0),
                  _vec_spec(D_CONV, CONV_K)],
        out_specs=_row_spec(2 * D_CONV), scratch_shapes=[pltpu.VMEM((8, CONV_PAD, D_CONV), F32)],
        name=name, compiler_params=_cp("parallel"))(proj, proj, du1, du1, wc)


def _rows(start, size, r):
    return pl.ds(start, size) if r == 1 else pl.ds(start, size, stride=r)


def _unit_rows(r, rho, n, nb):
    win = 2 * ATTN_BLOCK if nb > 1 else ATTN_BLOCK
    if isinstance(n, int):
        kb = max(n - 1, 0)
        q_rows = _rows(rho + r * ATTN_BLOCK * n, ATTN_BLOCK, r)
        k_rows = _rows(rho + r * ATTN_BLOCK * kb, win, r)
    else:
        kb = jnp.maximum(n - 1, 0)
        q_rows = pl.ds(pl.multiple_of(n * ATTN_BLOCK, ATTN_BLOCK), ATTN_BLOCK)
        k_rows = pl.ds(pl.multiple_of(kb * ATTN_BLOCK, ATTN_BLOCK), win)
    dist = (n - kb) * ATTN_BLOCK + lax.broadcasted_iota(jnp.int32, (ATTN_BLOCK, win), 0) \
        - lax.broadcasted_iota(jnp.int32, (ATTN_BLOCK, win), 1)
    return q_rows, k_rows, (dist >= 0) & (dist <= ATTN_BLOCK)


def _per_head(x):
    lane = lax.broadcasted_iota(jnp.int32, x.shape, 1)
    zero = jnp.zeros_like(x)
    return [jnp.where(lane < HEAD_DIM, x, zero), jnp.where(lane >= HEAD_DIM, x, zero)]


def _masked_scores(q2, k2, valid):
    return [jnp.where(valid, lax.dot_general(qh, k2, NT, preferred_element_type=F32) * (HEAD_DIM ** -0.5), NEG)
            for qh in _per_head(q2)]


def _attn_units(r, nb, unit):
    if r == 1:
        def four(i, carry):
            for k in range(4):
                unit(0, 4 * i + k)
            return carry
        lax.fori_loop(0, nb // 4, four, 0)
    else:
        for rho in range(r):
            for n in range(nb):
                unit(rho, n)


N_UNITS = 16


def attn_fwd_all(proj, name):
    def body(q_ref, k_ref, v_ref, att_ref, lse_ref, s_scr, p_scr, lse_scr, den_scr):
        for idx, (sub_len, r) in enumerate(PATTERNS):
            nb = sub_len // ATTN_BLOCK
            win = 2 * ATTN_BLOCK if nb > 1 else ATTN_BLOCK

            def scores(rho, n, r=r, nb=nb, win=win):
                u = rho * nb + n
                q_rows, k_rows, valid = _unit_rows(r, rho, n, nb)
                ss = _masked_scores(q_ref[q_rows, :].astype(BF16), k_ref[k_rows, :].astype(BF16), valid)
                for h in range(2):
                    s_scr[2 * u + h, :, 0:win] = ss[h]

            _attn_units(r, nb, scores)

            def softmax(u, carry, win=win):
                lses, dens = [], []
                for h in range(2):
                    sc = s_scr[2 * u + h, :, 0:win]
                    m = jnp.max(sc, axis=1, keepdims=True)
                    p = jnp.exp(sc - m)
                    den = jnp.sum(p, axis=1, keepdims=True)
                    p_scr[2 * u + h, :, 0:win] = p.astype(BF16)
                    lses.append(jnp.broadcast_to(m + jnp.log(den), (ATTN_BLOCK, HEAD_DIM)))
                    dens.append(jnp.broadcast_to(den, (ATTN_BLOCK, HEAD_DIM)))
                lse_scr[u] = jnp.concatenate(lses, axis=1)
                den_scr[u] = jnp.concatenate(dens, axis=1)
                return carry

            lax.fori_loop(0, N_UNITS, softmax, 0, unroll=2)

            def outputs(rho, n, r=r, nb=nb, win=win, idx=idx):
                u = rho * nb + n
                q_rows, k_rows, _ = _unit_rows(r, rho, n, nb)
                vs = _per_head(v_ref[k_rows, :].astype(BF16))
                o = (jnp.dot(p_scr[2 * u, :, 0:win], vs[0], preferred_element_type=F32)
                     + jnp.dot(p_scr[2 * u + 1, :, 0:win], vs[1], preferred_element_type=F32)) / den_scr[u]
                lse = lse_scr[u]
                if idx > 0:
                    old = lse_ref[q_rows, :]
                    top = jnp.maximum(old, lse)
                    new = top + jnp.log(jnp.exp(old - top) + jnp.exp(lse - top))
                    o = att_ref[q_rows, :] * jnp.exp(old - new) + o * jnp.exp(lse - new)
                    lse = new
                att_ref[q_rows, :] = o
                lse_ref[q_rows, :] = lse

            _attn_units(r, nb, outputs)

    blk = lambda first: pl.BlockSpec((SEQ, 128), lambda g: (0, first + g))
    shp = jax.ShapeDtypeStruct((SEQ, D_ATTN), F32)
    big = (2 * N_UNITS, ATTN_BLOCK, 2 * ATTN_BLOCK)
    small = pltpu.VMEM((N_UNITS, ATTN_BLOCK, 128), F32)
    return pl.pallas_call(
        body, out_shape=(shp, shp), grid=(4,), in_specs=[blk(8), blk(12), blk(16)], out_specs=(blk(0), blk(0)),
        scratch_shapes=[pltpu.VMEM(big, F32), pltpu.VMEM(big, BF16), small, small],
        name=name, compiler_params=_cp("parallel"))(proj, proj, proj)


def attn_bwd_all(proj, do, lse, dd, name):
    scale = HEAD_DIM ** -0.5

    def body(q_ref, k_ref, v_ref, do_ref, l_ref, dd_ref, out_ref, dq_s, dk_s, dv_s,
             s_scr, dp_scr, ds_scr, st_scr, dpt_scr, pt_scr, dst_scr, qb_scr, kb_scr, dob_scr):
        dq_s[...] = jnp.zeros_like(dq_s)
        dk_s[...] = jnp.zeros_like(dk_s)
        dv_s[...] = jnp.zeros_like(dv_s)
        for sub_len, r in PATTERNS:
            nb = sub_len // ATTN_BLOCK
            win = 2 * ATTN_BLOCK if nb > 1 else ATTN_BLOCK

            def scores(rho, n, r=r, nb=nb, win=win):
                u = rho * nb + n
                q_rows, k_rows, valid = _unit_rows(r, rho, n, nb)
                kb = max(n - 1, 0) if isinstance(n, int) else jnp.maximum(n - 1, 0)
                dist_t = (n - kb) * ATTN_BLOCK + lax.broadcasted_iota(jnp.int32, (win, ATTN_BLOCK), 1) \
                    - lax.broadcasted_iota(jnp.int32, (win, ATTN_BLOCK), 0)
                valid_t = (dist_t >= 0) & (dist_t <= ATTN_BLOCK)
                q2 = q_ref[q_rows, :].astype(BF16)
                k2 = k_ref[k_rows, :].astype(BF16)
                do2 = do_ref[q_rows, :].astype(BF16)
                qb_scr[u] = q2
                kb_scr[u, 0:win, :] = k2
                dob_scr[u] = do2
                l2 = l_ref[q_rows, :]
                d2 = dd_ref[q_rows, :]
                l2t = l2.T
                d2t = d2.T
                v2 = v_ref[k_rows, :].astype(BF16)
                qs, dos = _per_head(q2), _per_head(do2)
                for h in range(2):
                    c0 = h * HEAD_DIM
                    sc = lax.dot_general(qs[h], k2, NT, preferred_element_type=F32) * scale
                    s_scr[2 * u + h, :, 0:win] = jnp.where(valid, sc, NEG) - l2[:, c0:c0 + 1]
                    dp_scr[2 * u + h, :, 0:win] = lax.dot_general(dos[h], v2, NT, preferred_element_type=F32) \
                        - d2[:, c0:c0 + 1]
                    sct = lax.dot_general(k2, qs[h], NT, preferred_element_type=F32) * scale
                    st_scr[2 * u + h, 0:win, :] = jnp.where(valid_t, sct, NEG) - l2t[c0:c0 + 1, :]
                    dpt_scr[2 * u + h, 0:win, :] = lax.dot_general(v2, dos[h], NT, preferred_element_type=F32) \
                        - d2t[c0:c0 + 1, :]

            _attn_units(r, nb, scores)

            def pointwise(hu, carry, win=win):
                ds_scr[hu, :, 0:win] = (jnp.exp(s_scr[hu, :, 0:win]) * dp_scr[hu, :, 0:win] * scale).astype(BF16)
                pt = jnp.exp(st_scr[hu, 0:win, :])
                pt_scr[hu, 0:win, :] = pt.astype(BF16)
                dst_scr[hu, 0:win, :] = (pt * dpt_scr[hu, 0:win, :] * scale).astype(BF16)
                return carry

            lax.fori_loop(0, 2 * N_UNITS, pointwise, 0, unroll=4)

            def grads(rho, n, r=r, nb=nb, win=win):
                u = rho * nb + n
                q_rows, k_rows, _ = _unit_rows(r, rho, n, nb)
                qs, ks, dos = _per_head(qb_scr[u]), _per_head(kb_scr[u, 0:win, :]), _per_head(dob_scr[u])

                def both(scr, rows, rhs):
                    return (jnp.dot(scr[(2 * u,) + rows], rhs[0], preferred_element_type=F32)
                            + jnp.dot(scr[(2 * u + 1,) + rows], rhs[1], preferred_element_type=F32))

                dq_s[q_rows, :] += both(ds_scr, (slice(None), slice(0, win)), ks)
                dk_s[k_rows, :] += both(dst_scr, (slice(0, win), slice(None)), qs)
                dv_s[k_rows, :] += both(pt_scr, (slice(0, win), slice(None)), dos)

            _attn_units(r, nb, grads)
        out_ref[0] = dq_s[...].astype(BF16)
        out_ref[1] = dk_s[...].astype(BF16)
        out_ref[2] = dv_s[...].astype(BF16)

    blk = lambda first: pl.BlockSpec((SEQ, 128), lambda g: (0, first + g))
    acc = pltpu.VMEM((SEQ, 128), F32)
    big = (2 * N_UNITS, ATTN_BLOCK, 2 * ATTN_BLOCK)
    big_t = (2 * N_UNITS, 2 * ATTN_BLOCK, ATTN_BLOCK)
    return pl.pallas_call(
        body, out_shape=jax.ShapeDtypeStruct((3, SEQ, D_ATTN), BF16), grid=(4,),
        in_specs=[blk(8), blk(12), blk(16), blk(0), blk(0), blk(0)],
        out_specs=pl.BlockSpec((3, SEQ, 128), lambda g: (0, 0, g)),
        scratch_shapes=[acc, acc, acc, pltpu.VMEM(big, F32), pltpu.VMEM(big, F32), pltpu.VMEM(big, BF16),
                        pltpu.VMEM(big_t, F32), pltpu.VMEM(big_t, F32), pltpu.VMEM(big_t, BF16), pltpu.VMEM(big_t, BF16),
                        pltpu.VMEM((N_UNITS, ATTN_BLOCK, 128), BF16),
                        pltpu.VMEM((N_UNITS, 2 * ATTN_BLOCK, 128), BF16), pltpu.VMEM((N_UNITS, ATTN_BLOCK, 128), BF16)],
        name=name, compiler_params=_cp("parallel"))(proj, proj, proj, do, lse, dd)


N_FT = D_FF // FFN_TN


def _ffn_specs():
    per = ROWS // FFN_HALO
    cur_g = pl.BlockSpec((ROWS, FFN_TN), lambda j, i: (i, j))
    cur_v = pl.BlockSpec((ROWS, FFN_TN), lambda j, i: (i, j + N_FT))
    halo_g = pl.BlockSpec((FFN_HALO, FFN_TN), lambda j, i: (jnp.maximum(i * per - 1, 0), j))
    halo_v = pl.BlockSpec((FFN_HALO, FFN_TN), lambda j, i: (jnp.maximum(i * per - 1, 0), j + N_FT))
    w_g = pl.BlockSpec((FFN_K, FFN_TN), lambda j, i: (0, j))
    w_v = pl.BlockSpec((FFN_K, FFN_TN), lambda j, i: (0, j + N_FT))
    b_g = pl.BlockSpec((1, FFN_TN), lambda j, i: (0, j))
    b_v = pl.BlockSpec((1, FFN_TN), lambda j, i: (0, j + N_FT))
    return [cur_g, cur_v, halo_g, halo_v, w_g, w_v, b_g, b_v]


def matmul(a, b, kind, out_dtype, tm, tn, name, b_rows=None):
    stacked = b_rows is not None
    b_shape = (N_DEV * b_rows, b.shape[2]) if stacked else b.shape
    if kind == "nn":
        (m, k), n = a.shape, b_shape[1]
        a_spec = pl.BlockSpec((tm, k), lambda j, i: (i, 0))
        b_spec = pl.BlockSpec((k, tn), lambda j, i: (0, j))
        dims = (((1,), (0,)), ((), ()))
    elif kind == "nt":
        (m, k), n = a.shape, b_shape[0]
        a_spec = pl.BlockSpec((tm, k), lambda j, i: (i, 0))
        b_spec = pl.BlockSpec((tn, k), lambda j, i: (j, 0))
        dims = (((1,), (1,)), ((), ()))
    else:
        (k, m), n = a.shape, b_shape[1]
        a_spec = pl.BlockSpec((k, tm), lambda j, i: (0, i))
        b_spec = pl.BlockSpec((k, tn), lambda j, i: (0, j))
        dims = (((0,), (0,)), ((), ()))
    assert m % tm == 0 and n % tn == 0, (name, m, n, tm, tn)
    if stacked:
        assert b_spec.block_shape[0] == b_shape[0] and kind in ("nn", "nt")
        width = b_spec.block_shape[1]
        b_spec = pl.BlockSpec((N_DEV, b_rows, width), (lambda j, i: (0, 0, j)) if kind == "nn" else (lambda j, i: (0, 0, 0)))

    def body(a_ref, b_ref, o_ref):
        bb = b_ref[...].reshape(b_shape[0], -1) if stacked else b_ref[...]
        o_ref[...] = lax.dot_general(a_ref[...], bb, dims, preferred_element_type=F32).astype(o_ref.dtype)

    return pl.pallas_call(
        body, out_shape=jax.ShapeDtypeStruct((m, n), out_dtype), grid=(n // tn, m // tm),
        in_specs=[a_spec, b_spec], out_specs=pl.BlockSpec((tm, tn), lambda j, i: (i, j)),
        name=name, compiler_params=_cp("parallel", "parallel"))(a, b)


def matmul_tn_halves(a, b, tm, name):
    _, k, m = a.shape
    n = b.shape[1]

    def body(a_ref, b_ref, o_ref):
        o_ref[0] = lax.dot_general(a_ref[0], b_ref[...], (((0,), (0,)), ((), ())), preferred_element_type=F32).astype(BF16)

    return pl.pallas_call(
        body, out_shape=jax.ShapeDtypeStruct((2, m, n), BF16), grid=(2, m // tm),
        in_specs=[pl.BlockSpec((1, k, tm), lambda h, i: (h, 0, i)), pl.BlockSpec((k, n), lambda h, i: (0, 0))],
        out_specs=pl.BlockSpec((1, tm, n), lambda h, i: (h, i, 0)), name=name,
        compiler_params=_cp("parallel", "parallel"))(a, b).reshape(2 * m, n)


def _row_spec(width, col=0):
    return pl.BlockSpec((ROWS, width), lambda i: (i, col))


def _vec_spec(width, rows=1):
    return pl.BlockSpec((rows, width), lambda i: (0, 0))


def _ffn_shifted(cur_ref, halo_ref, pad, s1, s2):
    i = pl.program_id(1)
    pad[0:FFN_HALO, :] = jnp.where(i > 0, halo_ref[...], 0.0)
    pad[FFN_HALO:, :] = cur_ref[0:FFN_HALO, :]
    for k, dst in ((1, s1), (2, s2)):
        dst[0:FFN_HALO, :] = pad[pl.ds(FFN_HALO - k, FFN_HALO), :]
        dst[FFN_HALO:, :] = cur_ref[pl.ds(FFN_HALO - k, ROWS - FFN_HALO), :]


def _ffn_conv(rs, cur_ref, s1, s2, w_ref, b_ref):
    return b_ref[...] + w_ref[0:1, :] * s2[rs, :] + w_ref[1:2, :] * s1[rs, :] + w_ref[2:3, :] * cur_ref[rs, :]


def ffn_act_fwd(up0, wf, bf, name):
    def body(g_ref, v_ref, gh_ref, vh_ref, wg_ref, wv_ref, bg_ref, bv_ref, act_ref, pad, g1, g2, v1, v2):
        _ffn_shifted(g_ref, gh_ref, pad, g1, g2)
        _ffn_shifted(v_ref, vh_ref, pad, v1, v2)

        def chunk(rs):
            gate = _ffn_conv(rs, g_ref, g1, g2, wg_ref, bg_ref)
            val = _ffn_conv(rs, v_ref, v1, v2, wv_ref, bv_ref)
            act_ref[rs, :] = (gate * _sig(gate) * val).astype(BF16)

        _for_chunks(chunk)

    tile = pltpu.VMEM((ROWS, FFN_TN), F32)
    return pl.pallas_call(
        body, out_shape=jax.ShapeDtypeStruct((SEQ, D_FF), BF16), grid=(N_FT, SEQ // ROWS),
        in_specs=_ffn_specs(), out_specs=pl.BlockSpec((ROWS, FFN_TN), lambda j, i: (i, j)),
        scratch_shapes=[pltpu.VMEM((2 * FFN_HALO, FFN_TN), F32), tile, tile, tile, tile],
        name=name, compiler_params=_cp("parallel", "parallel"))(up0, up0, up0, up0, wf, wf, bf, bf)


def ffn_bwd(up0, dact, wf, bf, name, after=None):
    per = ROWS // FFN_HALO
    last = SEQ // FFN_HALO - 1

    def body(g_ref, v_ref, gh_ref, vh_ref, wg_ref, wv_ref, bg_ref, bv_ref, da_ref, gn_ref, vn_ref, dan_ref,
             out_ref, dbg_ref, dbv_ref, dwg_ref, dwv_ref, pad, g1, g2, v1, v2, dgp, dvp, acc):
        i = pl.program_id(1)
        first = i == 0
        _ffn_shifted(g_ref, gh_ref, pad, g1, g2)
        _ffn_shifted(v_ref, vh_ref, pad, v1, v2)
        acc[...] = jnp.zeros_like(acc)

        def grads(gate, val, da):
            s = _sig(gate)
            return da * val * (s * (1.0 + gate * (1.0 - s))), da * (gate * s)

        def chunk(rs):
            gate = _ffn_conv(rs, g_ref, g1, g2, wg_ref, bg_ref)
            val = _ffn_conv(rs, v_ref, v1, v2, wv_ref, bv_ref)
            dgate, dval = grads(gate, val, da_ref[rs, :])
            dgp[rs, :] = dgate
            dvp[rs, :] = dval
            acc[0] += dgate
            acc[1] += dval
            for t, (sg, sv) in enumerate(((g2, v2), (g1, v1), (g_ref, v_ref))):
                acc[2 + t] += dgate * sg[rs, :]
                acc[5 + t] += dval * sv[rs, :]

        _for_chunks(chunk)
        _acc(dbg_ref, _rsum(acc[0]), first)
        _acc(dbv_ref, _rsum(acc[1]), first)
        _acc(dwg_ref, jnp.concatenate([_rsum(acc[2 + t]) for t in range(FFN_K)], axis=0), first)
        _acc(dwv_ref, jnp.concatenate([_rsum(acc[5 + t]) for t in range(FFN_K)], axis=0), first)

        def conv_next(cur_ref, nxt_ref, w_ref, b_ref):
            pad[0:FFN_HALO, :] = cur_ref[ROWS - FFN_HALO:, :]
            pad[FFN_HALO:, :] = nxt_ref[...]
            return (b_ref[...] + w_ref[0:1, :] * pad[pl.ds(FFN_HALO - 2, FFN_HALO), :]
                    + w_ref[1:2, :] * pad[pl.ds(FFN_HALO - 1, FFN_HALO), :] + w_ref[2:3, :] * nxt_ref[...])

        gate_n = conv_next(g_ref, gn_ref, wg_ref, bg_ref)
        val_n = conv_next(v_ref, vn_ref, wv_ref, bv_ref)
        dgate_n, dval_n = grads(gate_n, val_n, dan_ref[...])
        inside = i < SEQ // ROWS - 1
        dgp[ROWS:, :] = jnp.where(inside, dgate_n, 0.0)
        dvp[ROWS:, :] = jnp.where(inside, dval_n, 0.0)

        for half, (dp, s1, s2, w_ref) in enumerate(((dgp, g1, g2, wg_ref), (dvp, v1, v2, wv_ref))):
            s1[...] = dp[pl.ds(1, ROWS), :]
            s2[...] = dp[pl.ds(2, ROWS), :]

            def back(rs, dp=dp, s1=s1, s2=s2, w_ref=w_ref, half=half):
                out_ref[half, rs, :] = (w_ref[2:3, :] * dp[rs, :] + w_ref[1:2, :] * s1[rs, :]
                                        + w_ref[0:1, :] * s2[rs, :]).astype(BF16)

            _for_chunks(back)

    body, more_specs, more = _with_after(body, 12, after)
    tile = pltpu.VMEM((ROWS, FFN_TN), F32)
    ext = pltpu.VMEM((ROWS + FFN_HALO, FFN_TN), F32)
    vec = jax.ShapeDtypeStruct((1, D_FF), F32)
    taps = jax.ShapeDtypeStruct((FFN_K, D_FF), F32)
    cur = pl.BlockSpec((ROWS, FFN_TN), lambda j, i: (i, j))
    nxt = lambda off: pl.BlockSpec((FFN_HALO, FFN_TN), lambda j, i: (jnp.minimum((i + 1) * per, last), j + off))
    vs = pl.BlockSpec((1, FFN_TN), lambda j, i: (0, j))
    ts = pl.BlockSpec((FFN_K, FFN_TN), lambda j, i: (0, j))
    return pl.pallas_call(
        body, out_shape=(jax.ShapeDtypeStruct((2, SEQ, D_FF), BF16), vec, vec, taps, taps), grid=(N_FT, SEQ // ROWS),
        in_specs=_ffn_specs() + [cur, nxt(0), nxt(N_FT), nxt(0)] + more_specs,
        out_specs=(pl.BlockSpec((2, ROWS, FFN_TN), lambda j, i: (0, i, j)), vs, vs, ts, ts),
        scratch_shapes=[pltpu.VMEM((2 * FFN_HALO, FFN_TN), F32), tile, tile, tile, tile, ext, ext,
                        pltpu.VMEM((2 + 2 * FFN_K, SUB, FFN_TN), F32)],
        name=name, compiler_params=_cp("parallel", "arbitrary"))(up0, up0, up0, up0, wf, wf, bf, bf, dact, up0, up0, dact,
                                                                 *more)


def ada_fwd(c_all, w_ada, b_cols, name):
    def body(c_ref, w_ref, b_ref, o_ref):
        cc = c_ref[...]
        sc = (cc * _sig(cc)).astype(BF16)
        o_ref[...] = jnp.dot(sc, w_ref[...].astype(BF16), preferred_element_type=F32) + b_ref[...]

    return pl.pallas_call(body, out_shape=jax.ShapeDtypeStruct((N_DEV, w_ada.shape[1]), F32), name=name,
                          compiler_params=_cp())(c_all, w_ada, b_cols)


def _adam(w, g, m, v):
    m = ADAM_B1 * m + (1.0 - ADAM_B1) * g
    v = ADAM_B2 * v + (1.0 - ADAM_B2) * (g * g)
    m_hat = m / (1.0 - ADAM_B1 ** ADAM_STEP)
    v_hat = v / (1.0 - ADAM_B2 ** ADAM_STEP)
    delta = -ADAM_LR * (m_hat / (jnp.sqrt(v_hat) + ADAM_EPS) + ADAM_WD * w)
    return delta, m, v


def ada_bwd_adamw(c_all_t, dmod_cols, w, m, v, name):
    rows, cols = w.shape
    tr = 256

    def body(ct_ref, dm_ref, w_ref, m_ref, v_ref, g_ref, d_ref, nm_ref, nv_ref):
        def chunk(rs):
            ct = ct_ref[rs, :]
            sc = ct * _sig(ct)
            g = sc[:, 0:1] * dm_ref[0:1, :]
            for b in range(1, N_DEV):
                g = g + sc[:, b:b + 1] * dm_ref[b:b + 1, :]
            g_ref[rs, :] = g
            d_ref[rs, :], nm_ref[rs, :], nv_ref[rs, :] = _adam(w_ref[rs, :], g, m_ref[rs, :], v_ref[rs, :])

        _for_chunks(chunk, 2, tr)

    blk = pl.BlockSpec((tr, cols), lambda i: (i, 0))
    shp = jax.ShapeDtypeStruct((rows, cols), F32)
    return pl.pallas_call(
        body, out_shape=(shp, shp, shp, shp), grid=(rows // tr,),
        in_specs=[pl.BlockSpec((tr, N_DEV), lambda i: (i, 0)), pl.BlockSpec((N_DEV, cols), lambda i: (0, 0)), blk, blk, blk],
        out_specs=(blk, blk, blk, blk), name=name, compiler_params=_cp("parallel"))(c_all_t, dmod_cols, w, m, v)


def sum_adamw(parts, mine, me, w, m, v, tr, name):
    n_parts, rows, cols = parts.shape

    def body(me_ref, p_ref, own_ref, w_ref, m_ref, v_ref, g_ref, d_ref, nm_ref, nv_ref):
        def chunk(rs):
            g = own_ref[0, rs, :].astype(F32)
            for k in range(1, n_parts):
                g = g + p_ref[k, rs, :].astype(F32)
            g_ref[rs, :] = g
            d_ref[rs, :], nm_ref[rs, :], nv_ref[rs, :] = _adam(w_ref[rs, :], g, m_ref[rs, :], v_ref[rs, :])

        _for_chunks(chunk, 2, tr)

    blk = pl.BlockSpec((tr, cols), lambda i, me_ref: (i, 0))
    shp = jax.ShapeDtypeStruct((rows, cols), F32)
    grid_spec = pltpu.PrefetchScalarGridSpec(
        num_scalar_prefetch=1, grid=(rows // tr,),
        in_specs=[pl.BlockSpec((n_parts, tr, cols), lambda i, me_ref: (0, i, 0)),
                  pl.BlockSpec((1, tr, cols), lambda i, me_ref: (me_ref[0], i, 0)), blk, blk, blk],
        out_specs=(blk, blk, blk, blk))
    return pl.pallas_call(body, out_shape=(shp, shp, shp, shp), grid_spec=grid_spec, name=name,
                          compiler_params=_cp("parallel"))(me, parts, mine, w, m, v)


MESH = pl.DeviceIdType.MESH


def all_gather(block, name, after=None):
    extra = () if after is None else (after,)

    def body(x_ref, *refs):
        out_ref, send_sems, recv_sems, local_sem = refs[len(extra):]
        x, y, c = lax.axis_index("x"), lax.axis_index("y"), lax.axis_index("c")
        me, sibling = (x, y, c), (x, y, 1 - c)
        chips = [(1 - x, y), (x, 1 - y), (1 - x, 1 - y)]

        def slot(px, py, pc):
            return out_ref.at[4 * px + 2 * py + pc]

        def copy(k, blk, to, src=None):
            return pltpu.make_async_remote_copy(
                src_ref=slot(*blk) if src is None else src, dst_ref=slot(*blk),
                send_sem=send_sems.at[k], recv_sem=recv_sems.at[k], device_id=to, device_id_type=MESH)

        mine = pltpu.make_async_copy(x_ref, slot(*me), local_sem)
        mine.start()
        first = [copy(0, me, sibling, src=x_ref)]
        first += [copy(1 + j, me, (*chip, c), src=x_ref) for j, chip in enumerate(chips)]
        for cp in first:
            cp.start()
        passed = [copy(4 + j, (*chip, c), sibling) for j, chip in enumerate(chips)]
        for j, chip in enumerate(chips):
            copy(1 + j, (*chip, c), me).wait_recv()
            passed[j].start()
        copy(0, sibling, me).wait_recv()
        for j, chip in enumerate(chips):
            copy(4 + j, (*chip, 1 - c), me).wait_recv()
        for cp in first + passed:
            cp.wait_send()
        mine.wait()

    return pl.pallas_call(
        body, out_shape=jax.ShapeDtypeStruct((N_DEV,) + block.shape, block.dtype), in_specs=[ANY] * (1 + len(extra)), out_specs=ANY,
        scratch_shapes=[pltpu.SemaphoreType.DMA((7,)), pltpu.SemaphoreType.DMA((7,)), pltpu.SemaphoreType.DMA],
        name=name)(block, *extra)


HBM = pl.BlockSpec(memory_space=pltpu.HBM)
SEM = pl.BlockSpec(memory_space=pltpu.SEMAPHORE)
EFFECT = pltpu.SideEffectType.DATAFLOW_SIDE_EFFECTING


def _peer_copies(src_ref, land_ref, send_sems, recv_sems, gather):
    x, y, c = lax.axis_index("x"), lax.axis_index("y"), lax.axis_index("c")
    me = 4 * x + 2 * y + c
    copies = []
    for k in range(1, N_DEV):
        px = 1 - x if k & 4 else x
        py = 1 - y if k & 2 else y
        pc = 1 - c if k & 1 else c
        copies.append(pltpu.make_async_remote_copy(
            src_ref=src_ref if gather else src_ref.at[4 * px + 2 * py + pc],
            dst_ref=land_ref.at[me] if gather else land_ref.at[k],
            send_sem=send_sems.at[k - 1], recv_sem=recv_sems.at[k - 1], device_id=(px, py, pc), device_id_type=MESH))
    return copies


def exchange_start(srcs, gather, name, after=None):
    n = len(srcs)
    land_shapes = [(N_DEV,) + src.shape if gather else src.shape for src in srcs]
    extra = () if after is None else (after,)

    def body(*refs):
        src_refs, land_refs = refs[0:n], refs[n:2 * n]
        outs = refs[2 * n + len(extra):]
        for k in range(n):
            for cp in _peer_copies(src_refs[k], land_refs[k], outs[4 * k], outs[4 * k + 1], gather):
                cp.start()
        token = outs[4 * n]
        token[...] = jnp.zeros_like(token)

    out_shape, out_specs, aliases = [], [], {}
    for k, src in enumerate(srcs):
        out_shape += [pltpu.SemaphoreType.DMA((N_DEV - 1,)), pltpu.SemaphoreType.DMA((N_DEV - 1,)),
                      pltpu.HBM(src.shape, src.dtype), pltpu.HBM(land_shapes[k], src.dtype)]
        out_specs += [SEM, SEM, HBM, HBM]
        aliases[k] = 4 * k + 2
        aliases[n + k] = 4 * k + 3
    out_shape.append(jax.ShapeDtypeStruct((8, 128), F32))
    out_specs.append(pl.BlockSpec(memory_space=pltpu.VMEM))
    res = pl.pallas_call(
        body, name=name, out_shape=tuple(out_shape), in_specs=(HBM,) * (2 * n) + (ANY,) * len(extra),
        out_specs=tuple(out_specs), input_output_aliases=aliases,
        compiler_params=pltpu.CompilerParams(has_side_effects=EFFECT),
    )(*[pltpu.with_memory_space_constraint(src, pltpu.HBM) for src in srcs],
      *[pltpu.with_memory_space_constraint(lax.empty(shp, src.dtype), pltpu.HBM) for shp, src in zip(land_shapes, srcs)],
      *extra)
    return [tuple(res[4 * k:4 * k + 4]) for k in range(n)], res[4 * n]


def _stage1_peer(i):
    x, y, c = lax.axis_index("x"), lax.axis_index("y"), lax.axis_index("c")
    if i == 0:
        return (x, y, 1 - c)
    return (1 - x if i & 1 else x, 1 - y if i & 2 else y, c)


def _slot_of(peer):
    return 4 * peer[0] + 2 * peer[1] + peer[2]


def _stage1_copy(i, src_ref, land_ref, send_sems, recv_sems):
    me = _slot_of((lax.axis_index("x"), lax.axis_index("y"), lax.axis_index("c")))
    return pltpu.make_async_remote_copy(src_ref=src_ref, dst_ref=land_ref.at[me], send_sem=send_sems.at[i],
                                        recv_sem=recv_sems.at[i], device_id=_stage1_peer(i), device_id_type=MESH)


def _stage2_copy(j, land_ref, send_sems, recv_sems):
    slot = _slot_of(_stage1_peer(j + 1))
    return pltpu.make_async_remote_copy(src_ref=land_ref.at[slot], dst_ref=land_ref.at[slot], send_sem=send_sems.at[j],
                                        recv_sem=recv_sems.at[j], device_id=_stage1_peer(0), device_id_type=MESH)


def gather2_start(srcs, name, after=None):
    n = len(srcs)
    extra = () if after is None else (after,)

    def body(*refs):
        src_refs, land_refs = refs[0:n], refs[n:2 * n]
        outs = refs[2 * n + len(extra):]
        for k in range(n):
            for i in range(4):
                _stage1_copy(i, src_refs[k], land_refs[k], outs[4 * k], outs[4 * k + 1]).start()
        outs[4 * n][...] = jnp.zeros((8, 128), F32)

    out_shape, out_specs, aliases = [], [], {}
    for k, src in enumerate(srcs):
        out_shape += [pltpu.SemaphoreType.DMA((4,)), pltpu.SemaphoreType.DMA((4,)),
                      pltpu.HBM(src.shape, src.dtype), pltpu.HBM((N_DEV,) + src.shape, src.dtype)]
        out_specs += [SEM, SEM, HBM, HBM]
        aliases[k] = 4 * k + 2
        aliases[n + k] = 4 * k + 3
    out_shape.append(jax.ShapeDtypeStruct((8, 128), F32))
    out_specs.append(pl.BlockSpec(memory_space=pltpu.VMEM))
    res = pl.pallas_call(
        body, name=name, out_shape=tuple(out_shape), in_specs=(HBM,) * (2 * n) + (ANY,) * len(extra),
        out_specs=tuple(out_specs), input_output_aliases=aliases,
        compiler_params=pltpu.CompilerParams(has_side_effects=EFFECT),
    )(*[pltpu.with_memory_space_constraint(src, pltpu.HBM) for src in srcs],
      *[pltpu.with_memory_space_constraint(lax.empty((N_DEV,) + src.shape, src.dtype), pltpu.HBM) for src in srcs], *extra)
    return [dict(send1=res[4 * k], recv1=res[4 * k + 1], src=res[4 * k + 2], land=res[4 * k + 3]) for k in range(n)], \
        res[4 * n]


def gather2_pass(handles, after, name):
    n = len(handles)

    def body(*refs):
        src_refs, land_refs, recv1 = refs[0:n], refs[n:2 * n], refs[2 * n:3 * n]
        outs = refs[3 * n + 1:]
        for k in range(n):
            for j in range(3):
                _stage1_copy(j + 1, src_refs[k], land_refs[k], recv1[k], recv1[k]).wait_recv()
                _stage2_copy(j, land_refs[k], outs[3 * k], outs[3 * k + 1]).start()

    out_shape, out_specs, aliases = [], [], {}
    for k, h in enumerate(handles):
        out_shape += [pltpu.SemaphoreType.DMA((3,)), pltpu.SemaphoreType.DMA((3,)), pltpu.HBM(h["land"].shape, h["land"].dtype)]
        out_specs += [SEM, SEM, HBM]
        aliases[n + k] = 3 * k + 2
    res = pl.pallas_call(
        body, name=name, out_shape=tuple(out_shape), in_specs=(HBM,) * (2 * n) + (SEM,) * n + (ANY,),
        out_specs=tuple(out_specs), input_output_aliases=aliases,
        compiler_params=pltpu.CompilerParams(has_side_effects=EFFECT),
    )(*[h["src"] for h in handles], *[h["land"] for h in handles], *[h["recv1"] for h in handles], after)
    return [dict(h, send2=res[3 * k], recv2=res[3 * k + 1], land=res[3 * k + 2]) for k, h in enumerate(handles)]


def gather2_wait(h, after, name):
    def body(src_ref, land_ref, send1, recv1, send2, recv2, after_ref, src_dead, got_ref):
        for i in range(4):
            _stage1_copy(i, src_ref, land_ref, send1, recv1).wait_send()
        _stage1_copy(0, src_ref, land_ref, send1, recv1).wait_recv()
        for j in range(3):
            cp = _stage2_copy(j, land_ref, send2, recv2)
            cp.wait_send()
            cp.wait_recv()

    return pl.pallas_call(
        body, name=name,
        out_shape=(pltpu.HBM(h["src"].shape, h["src"].dtype), pltpu.HBM(h["land"].shape, h["land"].dtype)),
        in_specs=(HBM, HBM, SEM, SEM, SEM, SEM, ANY), out_specs=(HBM, HBM), input_output_aliases={0: 0, 1: 1},
        compiler_params=pltpu.CompilerParams(has_side_effects=EFFECT),
    )(h["src"], h["land"], h["send1"], h["recv1"], h["send2"], h["recv2"], after)[1]


def exchange_wait(handles, after, gather, name):
    send_sems, recv_sems, src_thru, land_thru = handles

    def body(src_ref, land_ref, send_sems, recv_sems, after_ref, src_dead, got_ref):
        for cp in _peer_copies(src_ref, land_ref, send_sems, recv_sems, gather):
            cp.wait_send()
            cp.wait_recv()

    return pl.pallas_call(
        body, name=name,
        out_shape=(pltpu.HBM(src_thru.shape, src_thru.dtype), pltpu.HBM(land_thru.shape, land_thru.dtype)),
        in_specs=(HBM, HBM, SEM, SEM, ANY), out_specs=(HBM, HBM), input_output_aliases={0: 0, 1: 1},
        compiler_params=pltpu.CompilerParams(has_side_effects=EFFECT),
    )(src_thru, land_thru, send_sems, recv_sems, after)[1]


def local_step(x, tgt, mod, started, get_w, put_grad, wc, wf, g_mix, bc, lg, lb, gco, gao, g_ffn, bf, g_fin):
    w_in = get_w("w_in", mod)
    proj, h1 = rms_mod_matmul(x, g_mix, mod, 0, 1, w_in, D_IN // N_DEV, "proj_fwd", after=started)
    mix_a, u1 = conv_module_fwd(proj, wc, bc, lg, lb, gco, "conv_module_fwd")
    att, lse = attn_fwd_all(proj, "attn_fwd")
    w_out = get_w("w_out", att)
    y1, mixed = norm_concat_matmul(mix_a, att, gao, w_out, "out_proj_fwd")
    w_up = get_w("w_up", y1)
    up0, x1, h2 = resid_rms_mod_matmul(x, y1, g_ffn, mod, 2, 3, 4, w_up, "up_fwd")
    act = ffn_act_fwd(up0, wf, bf, "ffn_act_fwd")
    w_down = get_w("w_down", act)
    loss_t, dx2, dy2, d_gfin, d_gaf = matmul_loss_bwd(act, w_down, x1, tgt, g_fin, mod, 5, "down_fwd_loss")
    dact = matmul(dy2, w_down, "nt", F32, 512, FFN_TN, "down_bwd_x")
    dw_down = matmul(act, dy2, "tn", BF16, 256, D_MODEL, "down_bwd_w")
    dup0, dbf_g, dbf_v, dwf_g, dwf_v = ffn_bwd(up0, dact, wf, bf, "ffn_bwd", after=put_grad("w_down", dw_down))
    dw_up = matmul_tn_halves(dup0, h2, 256, "up_bwd_w")
    dx1, d_shf, d_scf, d_gffn, dy1, d_gam = matmul_rms_mod_bwd(
        dup0, w_up, x1, dx2, g_ffn, mod, 4, y1, 2, "up_bwd_x", after=put_grad("w_up", dw_up))
    dw_out = matmul(mixed, dy1, "tn", BF16, 256, D_MODEL, "out_proj_bwd_w")
    dmixed, do, dd, d_gao = matmul_combine_bwd(dy1, w_out, att, gao, "out_proj_bwd_x", after=put_grad("w_out", dw_out))
    dqkv = attn_bwd_all(proj, do, lse, dd, "attn_bwd")
    du1, d_gco, d_lg, d_lb, d_bc, d_wc = conv_module_bwd_a(proj, u1, dmixed, lg, lb, gco, "conv_module_bwd_a")
    dproj_a = conv_module_bwd_b(proj, du1, wc, "conv_module_bwd_b")
    dproj = jnp.concatenate([dproj_a, dqkv[0], dqkv[1], dqkv[2]], axis=1)
    dw_in = matmul(dproj, h1, "tn", BF16, 512, D_MODEL, "proj_bwd_w")
    dx, d_shm, d_scm, d_gmix = matmul_rms_mod_bwd(
        dproj, w_in, x, dx1, g_mix, mod, 1, None, 0, "proj_bwd_x", b_rows=D_IN // N_DEV, after=put_grad("w_in", dw_in))
    dmod = jnp.concatenate([d_shm, d_scm, d_gam, d_shf, d_scf, d_gaf], axis=1)
    small = dict(g_norm_mix=d_gmix, b_conv_dw=d_bc, ln_conv_g=d_lg, ln_conv_b=d_lb, g_conv_out=d_gco, g_attn_out=d_gao,
                 g_norm_ffn=d_gffn, b_ffn_dw=jnp.concatenate([dbf_g, dbf_v], axis=1), g_final=d_gfin,
                 w_conv_dw=d_wc, w_ffn_dw=jnp.concatenate([dwf_g, dwf_v], axis=1), dmod=dmod, loss=loss_t[0:1, 0:1])
    return dx, small


def _padw(a, width):
    return jnp.pad(a, ((0, 0), (0, width - a.shape[1])))


def pack_small(t):
    wide = jnp.concatenate([_padw(t["dmod"], PACK_W), _padw(t["b_ffn_dw"], PACK_W), _padw(t["w_ffn_dw"], PACK_W),
                            _padw(t["loss"], PACK_W), jnp.zeros((2, PACK_W), F32)], axis=0)
    z512 = jnp.zeros((1, 512), F32)
    narrow = jnp.concatenate([
        t["g_norm_mix"], t["g_norm_ffn"], t["g_final"],
        jnp.concatenate([t["b_conv_dw"], t["ln_conv_g"]], axis=1),
        jnp.concatenate([t["ln_conv_b"], t["g_conv_out"]], axis=1),
        jnp.concatenate([t["g_attn_out"], z512], axis=1),
        jnp.zeros((2, 1024), F32),
        jnp.pad(t["w_conv_dw"], ((0, 1), (0, 0))).reshape(16, 1024)], axis=0)
    return jnp.concatenate([wide, narrow.reshape(4, PACK_W), jnp.zeros((4, PACK_W), F32)], axis=0)


_NARROW = lambda k, off=0: (8 + k // 6, (k % 6) * 1024 + off)
PACKED_AT = dict(
    b_ada=(0, 0, N_MOD * D_MODEL), b_ffn_dw=(1, 0, 2 * D_FF),
    g_norm_mix=_NARROW(0) + (D_MODEL,), g_norm_ffn=_NARROW(1) + (D_MODEL,), g_final=_NARROW(2) + (D_MODEL,),
    b_conv_dw=_NARROW(3) + (D_CONV,), ln_conv_g=_NARROW(3, 512) + (D_CONV,),
    ln_conv_b=_NARROW(4) + (D_CONV,), g_conv_out=_NARROW(4, 512) + (D_CONV,), g_attn_out=_NARROW(5) + (D_ATTN,))
SMALL_ORDER = list(PACKED_AT)


def small_adamw(parts, wmv, name):
    def body(*refs):
        p_ref = refs[0]
        ins = refs[1:1 + 3 * len(SMALL_ORDER)]
        outs = refs[1 + 3 * len(SMALL_ORDER):]
        g = p_ref[0]
        for k in range(1, N_DEV):
            g = g + p_ref[k]
        for i, n in enumerate(SMALL_ORDER):
            row, lane, width = PACKED_AT[n]
            gp = g[row:row + 1, lane:lane + width]
            w_ref, m_ref, v_ref = ins[3 * i:3 * i + 3]
            g_ref, d_ref, nm_ref, nv_ref = outs[4 * i:4 * i + 4]
            g_ref[...] = gp
            d_ref[...], nm_ref[...], nv_ref[...] = _adam(w_ref[...], gp, m_ref[...], v_ref[...])
        wc_ref, wf_ref, loss_ref = outs[4 * len(SMALL_ORDER):]
        for j in range(CONV_K):
            row, lane = _NARROW(8 + j // 2, (j % 2) * 512)
            wc_ref[j:j + 1, :] = g[row:row + 1, lane:lane + D_CONV]
        wf_ref[...] = g[2:2 + FFN_K, 0:2 * D_FF]
        loss_ref[...] = jnp.broadcast_to(g[5:6, 0:1], (8, 128))

    args, out_shape = [parts], []
    for n in SMALL_ORDER:
        args += list(wmv[n])
        out_shape += [jax.ShapeDtypeStruct(wmv[n][0].shape, F32)] * 4
    out_shape += [jax.ShapeDtypeStruct((CONV_K, D_CONV), F32), jax.ShapeDtypeStruct((FFN_K, 2 * D_FF), F32),
                  jax.ShapeDtypeStruct((8, 128), F32)]
    res = pl.pallas_call(body, out_shape=tuple(out_shape), name=name, compiler_params=_cp())(*args)
    per = {n: tuple(res[4 * i:4 * i + 4]) for i, n in enumerate(SMALL_ORDER)}
    return per, res[-3], res[-2], res[-1][0, 0]


def shard_adamw(items, name):
    def body(*refs):
        ins, outs = refs[:4 * len(items)], refs[4 * len(items):]
        for i in range(len(items)):
            g_ref, w_ref, m_ref, v_ref = ins[4 * i:4 * i + 4]
            og_ref, d_ref, nm_ref, nv_ref = outs[4 * i:4 * i + 4]
            og_ref[...] = g_ref[...]
            d_ref[...], nm_ref[...], nv_ref[...] = _adam(w_ref[...], g_ref[...], m_ref[...], v_ref[...])

    args = [a for item in items for a in item]
    out_shape = tuple(jax.ShapeDtypeStruct(item[1].shape, F32) for item in items for _ in range(4))
    res = pl.pallas_call(body, out_shape=out_shape, name=name, compiler_params=_cp())(*args)
    return [tuple(res[4 * i:4 * i + 4]) for i in range(len(items))]


def _shard(full, n_cols, me):
    return lax.dynamic_slice(full, (0, me * n_cols), (full.shape[0], n_cols))


WEIGHTS = ["w_ada", "b_ada", "g_norm_mix", "w_in", "w_conv_dw", "b_conv_dw", "ln_conv_g", "ln_conv_b", "g_conv_out",
           "g_attn_out", "w_out", "g_norm_ffn", "w_up", "w_ffn_dw", "b_ffn_dw", "w_down", "g_final"]


def kernel(x, c, w_ada, b_ada, g_norm_mix, w_in, w_conv_dw, b_conv_dw, ln_conv_g, ln_conv_b, g_conv_out, g_attn_out, w_out, g_norm_ffn, w_up, w_ffn_dw, b_ffn_dw, w_down, g_final, loss_target, m_w_ada, m_b_ada, m_g_norm_mix, m_w_in, m_w_conv_dw, m_b_conv_dw, m_ln_conv_g, m_ln_conv_b, m_g_conv_out, m_g_attn_out, m_w_out, m_g_norm_ffn, m_w_up, m_w_ffn_dw, m_b_ffn_dw, m_w_down, m_g_final, v_w_ada, v_b_ada, v_g_norm_mix, v_w_in, v_w_conv_dw, v_b_conv_dw, v_ln_conv_g, v_ln_conv_b, v_g_conv_out, v_g_attn_out, v_w_out, v_g_norm_ffn, v_w_up, v_w_ffn_dw, v_b_ffn_dw, v_w_down, v_g_final):
    args = dict(locals())
    me = 4 * lax.axis_index("x") + 2 * lax.axis_index("y") + lax.axis_index("c")
    me1 = me.astype(jnp.int32).reshape(1)

    def flat(name, prefix=""):
        a = args[prefix + name]
        return a.reshape(a.shape[-2] if a.ndim > 1 else 1, a.shape[-1])

    def flat_t(name, prefix=""):
        return args[prefix + name][0].T

    n_in, n_up, r_out, r_down = w_in.shape[2], w_up.shape[2], w_out.shape[1], w_down.shape[1]
    n_ada, n_wc, n_wf = w_ada.shape[2], w_conv_dw.shape[2], w_ffn_dw.shape[2]
    taps_c = jnp.pad(flat("w_conv_dw").reshape(1, CONV_K * n_wc), ((0, 0), (0, 2 * D_MODEL - CONV_K * n_wc)))
    taps_f = jnp.pad(flat("w_ffn_dw").reshape(1, FFN_K * n_wf), ((0, 0), (0, 3 * D_MODEL - FFN_K * n_wf)))
    first = jnp.concatenate([c, taps_c.reshape(2, D_MODEL), taps_f.reshape(3, D_MODEL), jnp.zeros((2, D_MODEL), F32)], axis=0)
    w_in_block = flat_t("w_in").astype(BF16)
    hi = lax.reduce_precision(first, 8, 7)
    mid = lax.reduce_precision(first - hi, 8, 7)
    low = lax.reduce_precision(first - hi - mid, 8, 7)
    terms = jnp.concatenate([hi, mid, low, jnp.zeros((8, D_MODEL), F32)], axis=0).astype(BF16)
    first_block = all_gather(jnp.concatenate([w_in_block, terms], axis=0), "gather_c_taps_w_in")
    terms = first_block[:, n_in:n_in + 24, :].astype(F32).reshape(N_DEV, 3, 8, D_MODEL)
    first_all = (terms[:, 0] + terms[:, 1]) + terms[:, 2]
    c_all = first_all[:, 0, :]
    wc_full = first_all[:, 1:3, :].reshape(N_DEV, 2 * D_MODEL)[:, :CONV_K * n_wc].reshape(N_DEV, CONV_K, n_wc)
    wc_full = wc_full.transpose(1, 0, 2).reshape(CONV_K, D_CONV)
    wf_full = first_all[:, 3:6, :].reshape(N_DEV, 3 * D_MODEL)[:, :FFN_K * n_wf].reshape(N_DEV, FFN_K, n_wf)
    wf_full = wf_full.transpose(1, 0, 2).reshape(FFN_K, 2 * D_FF)
    mod_cols = ada_fwd(c_all, flat("w_ada"), _shard(flat("b_ada"), n_ada, me), "ada_fwd")
    mod_all = all_gather(mod_cols, "gather_mod")
    mod = lax.dynamic_index_in_dim(mod_all, me, axis=1, keepdims=False).reshape(N_MOD, D_MODEL)
    mod = jnp.pad(mod, ((0, 2), (0, 0)))

    order = ("w_out", "w_up", "w_down")
    blocks = dict(w_up=flat_t("w_up").astype(BF16), w_out=flat("w_out").astype(BF16), w_down=flat("w_down").astype(BF16))
    handles, tok = gather2_start([blocks[name] for name in order], "gather_weights_start", mod_all)
    gathers = dict(zip(order, handles))

    def gathered(name, after):
        if "send2" not in gathers[name]:
            group = ("w_out", "w_up") if name != "w_down" else ("w_down",)
            gathers.update(zip(group, gather2_pass([gathers[w] for w in group], after, f"gather_{name}_pass")))
        land = gather2_wait(gathers[name], after, f"gather_{name}_wait")
        return lax.dynamic_update_index_in_dim(land, blocks[name], me, axis=0)

    def get_w(name, after):
        if name == "w_in":
            return first_block
        return gathered(name, after).reshape(-1, D_MODEL)

    exchanges, own = {}, {}

    def put_grad(name, dw, after=None):
        dev_major = dw.reshape(N_DEV, -1, D_MODEL)
        own[name] = dev_major
        (exchanges[name],), token = exchange_start([dev_major], False, f"exchange_{name}_start", after)
        return token

    grad_x, small = local_step(
        x[0], loss_target[0], mod, tok, get_w, put_grad, wc_full, wf_full,
        flat("g_norm_mix"), flat("b_conv_dw"), flat("ln_conv_g"), flat("ln_conv_b"), flat("g_conv_out"),
        flat("g_attn_out"), flat("g_norm_ffn"), flat("b_ffn_dw"), flat("g_final"))

    out = {}

    def finish(name, tr, after):
        parts = exchange_wait(exchanges[name], after, False, f"exchange_{name}_wait")
        if name in ("w_in", "w_up"):
            res = sum_adamw(parts, own[name], me1, flat_t(name), flat_t(name, "m_"), flat_t(name, "v_"), tr, "adamw_" + name)
            out[name] = tuple(r.T for r in res)
        else:
            res = out[name] = sum_adamw(parts, own[name], me1, flat(name), flat(name, "m_"), flat(name, "v_"), tr,
                                        "adamw_" + name)
        return res[0]

    after = finish("w_down", r_down, grad_x)
    after = finish("w_up", n_up // 2, after)
    after = finish("w_out", r_out, after)
    after = finish("w_in", n_in, after)

    small_all = all_gather(pack_small(small), "gather_small", after)

    wmv = {n: (flat(n), flat(n, "m_"), flat(n, "v_")) for n in SMALL_ORDER}
    per, g_wc, g_wf, loss = small_adamw(small_all, wmv, "adamw_small")
    out.update(per)
    taps = shard_adamw([(_shard(g_wc, n_wc, me), flat("w_conv_dw"), flat("w_conv_dw", "m_"), flat("w_conv_dw", "v_")),
                        (_shard(g_wf, n_wf, me), flat("w_ffn_dw"), flat("w_ffn_dw", "m_"), flat("w_ffn_dw", "v_"))],
                       "adamw_taps")
    out["w_conv_dw"], out["w_ffn_dw"] = taps

    dmod_cols = _shard(small_all[:, 0, :], n_ada, me)
    out["w_ada"] = ada_bwd_adamw(c_all.T, dmod_cols, flat("w_ada"), flat("w_ada", "m_"), flat("w_ada", "v_"), "adamw_w_ada")

    result = [loss, grad_x[None]]
    for k in range(4):
        result += [out[n][k].reshape(args[n].shape) for n in WEIGHTS]
    return tuple(result)
```

```python
import functools

import jax
import jax.numpy as jnp
from jax import lax
from jax.experimental import pallas as pl
from jax.experimental.pallas import tpu as pltpu

F32 = jnp.float32
BF16 = jnp.bfloat16

N_DEV = 8
SEQ = 2048
D_MODEL = 1024
D_CONV = 512
D_ATTN = 512
HEAD_DIM = 64
CONV_K = 31
D_FF = 2816
FFN_K = 3
D_IN = 2 * D_CONV + 3 * D_ATTN
N_MOD = 6
EPS = 1e-6
ATTN_BLOCK = 128
PATTERNS = ((2048, 1), (512, 4), (128, 16))
NEG = -1e30

ADAM_LR, ADAM_B1, ADAM_B2, ADAM_EPS, ADAM_WD, ADAM_STEP = 0.001, 0.9, 0.999, 1e-08, 0.01, 10

ROWS = 256
CONV_HALO = 32
FFN_HALO = 8
FFN_TN = 1408
VMEM_LIMIT = 56 * 1024 * 1024
PACK_W = 6144


NT = (((1,), (1,)), ((), ()))
ANY = pl.BlockSpec(memory_space=pl.ANY)


def _cp(*sem):
    return pltpu.CompilerParams(dimension_semantics=sem if sem else None, vmem_limit_bytes=VMEM_LIMIT)


def _with_after(body, n_in, after):
    if after is None:
        return body, [], []
    return (lambda *refs: body(*refs[:n_in], *refs[n_in + 1:])), [ANY], [after]


def _sig(x):
    return 1.0 / (1.0 + jnp.exp(-x))


def _rsum(x):
    return jnp.sum(x, axis=0, keepdims=True)


def _mean(x):
    return jnp.mean(x, axis=-1, keepdims=True)


def _acc(ref, val, first):
    @pl.when(first)
    def _():
        ref[...] = val

    @pl.when(jnp.logical_not(first))
    def _():
        ref[...] += val


SUB = 16


def _for_chunks(fn, unroll=1, rows=ROWS):
    def step(i, carry):
        fn(pl.ds(pl.multiple_of(i * SUB, SUB), SUB))
        return carry

    lax.fori_loop(0, rows // SUB, step, 0, unroll=unroll)


def rms_mod_matmul(x, g, mod, sh_row, sc_row, b, b_rows, name, after=None):
    n = N_DEV * b_rows

    def body(x_ref, g_ref, mod_ref, b_ref, o_ref, h_ref):
        xx = x_ref[...]
        r = lax.rsqrt(_mean(xx * xx) + EPS)
        h = (xx * r * g_ref[...] * (1.0 + mod_ref[sc_row:sc_row + 1, :]) + mod_ref[sh_row:sh_row + 1, :]).astype(BF16)
        h_ref[...] = h
        o_ref[...] = lax.dot_general(h, b_ref[...].reshape(n, D_MODEL), NT, preferred_element_type=F32)

    body, more_specs, more = _with_after(body, 4, after)
    return pl.pallas_call(
        body, out_shape=(jax.ShapeDtypeStruct((SEQ, n), F32), jax.ShapeDtypeStruct((SEQ, D_MODEL), BF16)),
        grid=(SEQ // ROWS,),
        in_specs=[_row_spec(D_MODEL), _vec_spec(D_MODEL), _vec_spec(D_MODEL, 8),
                  pl.BlockSpec((N_DEV, b_rows, D_MODEL), lambda i: (0, 0, 0))] + more_specs,
        out_specs=(_row_spec(n), _row_spec(D_MODEL)), name=name, compiler_params=_cp("parallel"))(x, g, mod, b, *more)


def norm_concat_matmul(mix_a, att, gao, w, name):
    def body(a_ref, att_ref, g_ref, w_ref, y_ref, mixed_ref):
        aa = att_ref[...]
        mixed_ref[:, 0:D_CONV] = a_ref[...]
        mixed_ref[:, D_CONV:] = (aa * lax.rsqrt(_mean(aa * aa) + EPS) * g_ref[...]).astype(BF16)
        y_ref[...] = jnp.dot(mixed_ref[...], w_ref[...], preferred_element_type=F32)

    return pl.pallas_call(
        body, out_shape=(jax.ShapeDtypeStruct((SEQ, D_MODEL), F32), jax.ShapeDtypeStruct((SEQ, D_MODEL), BF16)),
        grid=(SEQ // ROWS,),
        in_specs=[_row_spec(D_CONV), _row_spec(D_ATTN), _vec_spec(D_ATTN), pl.BlockSpec(w.shape, lambda i: (0, 0))],
        out_specs=(_row_spec(D_MODEL), _row_spec(D_MODEL)), name=name, compiler_params=_cp("parallel"))(mix_a, att, gao, w)


def resid_rms_mod_matmul(x, y, g, mod, ga_row, sh_row, sc_row, w, name):
    n = w.shape[0]

    def body(x_ref, y_ref, g_ref, mod_ref, w_ref, o_ref, x1_ref, h_ref):
        x1 = x_ref[...] + mod_ref[ga_row:ga_row + 1, :] * y_ref[...]
        x1_ref[...] = x1
        r = lax.rsqrt(_mean(x1 * x1) + EPS)
        h = (x1 * r * g_ref[...] * (1.0 + mod_ref[sc_row:sc_row + 1, :]) + mod_ref[sh_row:sh_row + 1, :]).astype(BF16)
        h_ref[...] = h
        o_ref[...] = lax.dot_general(h, w_ref[...], NT, preferred_element_type=F32)

    return pl.pallas_call(
        body,
        out_shape=(jax.ShapeDtypeStruct((SEQ, n), F32), jax.ShapeDtypeStruct((SEQ, D_MODEL), F32),
                   jax.ShapeDtypeStruct((SEQ, D_MODEL), BF16)),
        grid=(SEQ // ROWS,),
        in_specs=[_row_spec(D_MODEL), _row_spec(D_MODEL), _vec_spec(D_MODEL), _vec_spec(D_MODEL, 8),
                  pl.BlockSpec(w.shape, lambda i: (0, 0))],
        out_specs=(_row_spec(n), _row_spec(D_MODEL), _row_spec(D_MODEL)),
        name=name, compiler_params=_cp("parallel"))(x, y, g, mod, w)


def matmul_combine_bwd(dy, w, att, gao, name, after=None):
    def body(dy_ref, w_ref, att_ref, g_ref, dm_ref, do_ref, dd_ref, dg_ref):
        first = pl.program_id(0) == 0
        dm_ref[...] = lax.dot_general(dy_ref[...], w_ref[...], NT, preferred_element_type=F32)
        att = att_ref[...]
        r = lax.rsqrt(_mean(att * att) + EPS)
        xn = att * r
        dm = dm_ref[:, D_CONV:]
        _acc(dg_ref, _rsum(dm * xn), first)
        dyn = dm * g_ref[...]
        do = r * (dyn - xn * _mean(dyn * xn))
        do_ref[...] = do
        same_head = (jnp.right_shift(lax.broadcasted_iota(jnp.int32, (D_ATTN, D_ATTN), 0), 6)
                     == jnp.right_shift(lax.broadcasted_iota(jnp.int32, (D_ATTN, D_ATTN), 1), 6)).astype(F32)
        dd_ref[...] = jnp.dot(do * att, same_head, preferred_element_type=F32, precision=lax.Precision.HIGHEST)

    rs = _row_spec(D_ATTN)
    f = jax.ShapeDtypeStruct((SEQ, D_ATTN), F32)
    body, more_specs, more = _with_after(body, 4, after)
    return pl.pallas_call(
        body, out_shape=(jax.ShapeDtypeStruct((SEQ, D_MODEL), F32), f, f, jax.ShapeDtypeStruct((1, D_ATTN), F32)),
        grid=(SEQ // ROWS,),
        in_specs=[_row_spec(D_MODEL), pl.BlockSpec(w.shape, lambda i: (0, 0)), rs, _vec_spec(D_ATTN)] + more_specs,
        out_specs=(_row_spec(D_MODEL), rs, rs, _vec_spec(D_ATTN)),
        name=name, compiler_params=_cp("arbitrary"))(dy, w, att, gao, *more)


def matmul_loss_bwd(act, w, x1, tgt, g, mod, ga_row, name):
    def body(a_ref, w_ref, x1_ref, t_ref, g_ref, mod_ref, loss_ref, dx2_ref, dy2_ref, dg_ref, dga_ref):
        first = pl.program_id(0) == 0
        ga = mod_ref[ga_row:ga_row + 1, :]
        y2 = jnp.dot(a_ref[...], w_ref[...], preferred_element_type=F32)
        x2 = x1_ref[...] + ga * y2
        r = lax.rsqrt(_mean(x2 * x2) + EPS)
        xn = x2 * r
        err = xn * g_ref[...] - t_ref[...]
        _acc(loss_ref, jnp.broadcast_to(0.5 * jnp.sum(_mean(err * err)), (8, 128)), first)
        dy = err * (1.0 / D_MODEL)
        _acc(dg_ref, _rsum(dy * xn), first)
        dxn = dy * g_ref[...]
        dx2 = r * (dxn - xn * _mean(dxn * xn))
        dx2_ref[...] = dx2
        dy2_ref[...] = (dx2 * ga).astype(BF16)
        _acc(dga_ref, _rsum(dx2 * y2), first)

    vec = jax.ShapeDtypeStruct((1, D_MODEL), F32)
    return pl.pallas_call(
        body,
        out_shape=(jax.ShapeDtypeStruct((8, 128), F32), jax.ShapeDtypeStruct((SEQ, D_MODEL), F32),
                   jax.ShapeDtypeStruct((SEQ, D_MODEL), BF16), vec, vec),
        grid=(SEQ // ROWS,),
        in_specs=[_row_spec(act.shape[1]), pl.BlockSpec(w.shape, lambda i: (0, 0)), _row_spec(D_MODEL), _row_spec(D_MODEL),
                  _vec_spec(D_MODEL), _vec_spec(D_MODEL, 8)],
        out_specs=(pl.BlockSpec((8, 128), lambda i: (0, 0)), _row_spec(D_MODEL), _row_spec(D_MODEL),
                   _vec_spec(D_MODEL), _vec_spec(D_MODEL)),
        name=name, compiler_params=_cp("arbitrary"))(act, w, x1, tgt, g, mod)


def matmul_rms_mod_bwd(a, b, x, dres, g, mod, sc_row, y, ga_row, name, b_rows=None, after=None):
    gated = y is not None
    halves = a.ndim == 3
    if halves:
        k2 = a.shape[2]
        b_arg = b.reshape(2, k2, D_MODEL)
        a_spec = pl.BlockSpec((2, ROWS, k2), lambda i: (0, i, 0))
        b_spec = pl.BlockSpec((2, k2, D_MODEL), lambda i: (0, 0, 0))
    else:
        b_arg = b
        a_spec = _row_spec(a.shape[1])
        b_spec = pl.BlockSpec((N_DEV, b_rows, D_MODEL), lambda i: (0, 0, 0))

    def body(*refs):
        if gated:
            a_ref, b_ref, x_ref, dres_ref, g_ref, mod_ref, y_ref, dx_ref, dsh_ref, dsc_ref, dg_ref, dy_ref, dga_ref = refs
        else:
            a_ref, b_ref, x_ref, dres_ref, g_ref, mod_ref, dx_ref, dsh_ref, dsc_ref, dg_ref = refs
        first = pl.program_id(0) == 0
        if halves:
            dh = (jnp.dot(a_ref[0], b_ref[0], preferred_element_type=F32)
                  + jnp.dot(a_ref[1], b_ref[1], preferred_element_type=F32))
        else:
            dh = jnp.dot(a_ref[...], b_ref[...].reshape(N_DEV * b_rows, D_MODEL), preferred_element_type=F32)
        xx = x_ref[...]
        gg = g_ref[...]
        r = lax.rsqrt(_mean(xx * xx) + EPS)
        xn = xx * r
        _acc(dsh_ref, _rsum(dh), first)
        _acc(dsc_ref, _rsum(dh * (xn * gg)), first)
        dt = dh * (1.0 + mod_ref[sc_row:sc_row + 1, :])
        _acc(dg_ref, _rsum(dt * xn), first)
        dxn = dt * gg
        dx = dres_ref[...] + r * (dxn - xn * _mean(dxn * xn))
        dx_ref[...] = dx
        if gated:
            _acc(dga_ref, _rsum(dx * y_ref[...]), first)
            dy_ref[...] = (dx * mod_ref[ga_row:ga_row + 1, :]).astype(BF16)

    vec = jax.ShapeDtypeStruct((1, D_MODEL), F32)
    in_specs = [a_spec, b_spec, _row_spec(D_MODEL), _row_spec(D_MODEL), _vec_spec(D_MODEL), _vec_spec(D_MODEL, 8)]
    out_shape = [jax.ShapeDtypeStruct((SEQ, D_MODEL), F32), vec, vec, vec]
    out_specs = [_row_spec(D_MODEL), _vec_spec(D_MODEL), _vec_spec(D_MODEL), _vec_spec(D_MODEL)]
    args = [a, b_arg, x, dres, g, mod]
    if gated:
        in_specs.append(_row_spec(D_MODEL))
        out_shape += [jax.ShapeDtypeStruct((SEQ, D_MODEL), BF16), vec]
        out_specs += [_row_spec(D_MODEL), _vec_spec(D_MODEL)]
        args.append(y)
    body, more_specs, more = _with_after(body, len(args), after)
    return pl.pallas_call(
        body, out_shape=tuple(out_shape), grid=(SEQ // ROWS,), in_specs=in_specs + more_specs, out_specs=tuple(out_specs),
        name=name, compiler_params=_cp("arbitrary"))(*args, *more)


def _prev_halo(halo, width, col):
    per = ROWS // halo
    return pl.BlockSpec((halo, width), lambda i: (jnp.maximum(i * per - 1, 0), col))


def _next_halo(halo, width, col):
    per = ROWS // halo
    last = SEQ // halo - 1
    return pl.BlockSpec((halo, width), lambda i: (jnp.minimum((i + 1) * per, last), col))


CONV_PAD = ROWS + CONV_HALO


def _shift_copies(sh):
    for b in range(1, 8):
        sh[b, 0:CONV_PAD - 8, :] = sh[0, pl.ds(b, CONV_PAD - 8), :]


def _tap(sh, rs_start, offset):
    return sh[offset % 8, pl.ds(pl.multiple_of(rs_start + (offset // 8) * 8, 8), SUB), :]


def _conv_glu(av_ref, ag_ref, avh_ref, agh_ref, sh):
    i = pl.program_id(0)
    hv = avh_ref[...] * _sig(agh_ref[...])
    sh[0, 0:CONV_HALO, :] = jnp.where(i > 0, hv, 0.0)

    def glu(rs):
        sh[0, pl.ds(pl.multiple_of(rs.start + CONV_HALO, SUB), SUB), :] = av_ref[rs, :] * _sig(ag_ref[rs, :])

    _for_chunks(glu)
    _shift_copies(sh)


def _conv_norm(u1, lg_ref, lb_ref):
    mu = _mean(u1)
    cen = u1 - mu
    rs = lax.rsqrt(_mean(cen * cen) + EPS)
    z = cen * rs
    ln = z * lg_ref[...] + lb_ref[...]
    s = _sig(ln)
    return z, rs, ln, s, ln * s


def conv_module_fwd(proj, wc, bc, lg, lb, gco, name):
    def body(av_ref, ag_ref, avh_ref, agh_ref, wc_ref, bc_ref, lg_ref, lb_ref, gco_ref, out_ref, u1_ref, sh):
        _conv_glu(av_ref, ag_ref, avh_ref, agh_ref, sh)

        def conv(rs):
            u1 = jnp.broadcast_to(bc_ref[...], (SUB, D_CONV))
            for j in range(CONV_K):
                u1 = u1 + wc_ref[j:j + 1, :] * _tap(sh, rs.start, CONV_HALO - (CONV_K - 1) + j)
            u1_ref[rs, :] = u1

        _for_chunks(conv)
        _, _, _, _, u2 = _conv_norm(u1_ref[...], lg_ref, lb_ref)
        rc = lax.rsqrt(_mean(u2 * u2) + EPS)
        out_ref[...] = (u2 * rc * gco_ref[...]).astype(BF16)

    v = _vec_spec(D_CONV)
    return pl.pallas_call(
        body, out_shape=(jax.ShapeDtypeStruct((SEQ, D_CONV), BF16), jax.ShapeDtypeStruct((SEQ, D_CONV), F32)),
        grid=(SEQ // ROWS,),
        in_specs=[_row_spec(D_CONV, 0), _row_spec(D_CONV, 1), _prev_halo(CONV_HALO, D_CONV, 0),
                  _prev_halo(CONV_HALO, D_CONV, 1), _vec_spec(D_CONV, CONV_K), v, v, v, v],
        out_specs=(_row_spec(D_CONV), _row_spec(D_CONV)), scratch_shapes=[pltpu.VMEM((8, CONV_PAD, D_CONV), F32)],
        name=name, compiler_params=_cp("parallel"))(proj, proj, proj, proj, wc, bc, lg, lb, gco)


def conv_module_bwd_a(proj, u1, dmixed, lg, lb, gco, name):
    def body(av_ref, ag_ref, avh_ref, agh_ref, u1_ref, dm_ref, lg_ref, lb_ref, gco_ref,
             du1_ref, dgco_ref, dlg_ref, dlb_ref, dbc_ref, dwc_ref, sh, acc):
        first = pl.program_id(0) == 0
        _conv_glu(av_ref, ag_ref, avh_ref, agh_ref, sh)
        z, rs, ln, s, u2 = _conv_norm(u1_ref[...], lg_ref, lb_ref)
        rc = lax.rsqrt(_mean(u2 * u2) + EPS)
        xn = u2 * rc
        dm = dm_ref[...]
        _acc(dgco_ref, _rsum(dm * xn), first)
        dyn = dm * gco_ref[...]
        du2 = rc * (dyn - xn * _mean(dyn * xn))
        dln = du2 * (s * (1.0 + ln * (1.0 - s)))
        _acc(dlg_ref, _rsum(dln * z), first)
        _acc(dlb_ref, _rsum(dln), first)
        dz = dln * lg_ref[...]
        du1 = rs * (dz - _mean(dz) - z * _mean(dz * z))
        du1_ref[...] = du1
        _acc(dbc_ref, _rsum(du1), first)
        acc[...] = jnp.zeros_like(acc)

        def taps(rs):
            d = du1_ref[rs, :]
            for j in range(CONV_K):
                acc[j] += d * _tap(sh, rs.start, CONV_HALO - (CONV_K - 1) + j)

        _for_chunks(taps)

        @pl.when(first)
        def _():
            dwc_ref[...] = jnp.zeros_like(dwc_ref)

        for j in range(CONV_K):
            dwc_ref[j:j + 1, :] += _rsum(acc[j])

    v = _vec_spec(D_CONV)
    vec = jax.ShapeDtypeStruct((1, D_CONV), F32)
    return pl.pallas_call(
        body,
        out_shape=(jax.ShapeDtypeStruct((SEQ, D_CONV), F32), vec, vec, vec, vec, jax.ShapeDtypeStruct((CONV_K, D_CONV), F32)),
        grid=(SEQ // ROWS,),
        in_specs=[_row_spec(D_CONV, 0), _row_spec(D_CONV, 1), _prev_halo(CONV_HALO, D_CONV, 0),
                  _prev_halo(CONV_HALO, D_CONV, 1), _row_spec(D_CONV, 0), _row_spec(D_CONV, 0), v, v, v],
        out_specs=(_row_spec(D_CONV), v, v, v, v, _vec_spec(D_CONV, CONV_K)),
        scratch_shapes=[pltpu.VMEM((8, CONV_PAD, D_CONV), F32), pltpu.VMEM((CONV_K, SUB, D_CONV), F32)],
        name=name, compiler_params=_cp("arbitrary"))(proj, proj, proj, proj, u1, dmixed, lg, lb, gco)


def conv_module_bwd_b(proj, du1, wc, name):
    def body(av_ref, ag_ref, du1_ref, du1n_ref, wc_ref, out_ref, sh):
        i = pl.program_id(0)
        sh[0, 0:ROWS, :] = du1_ref[...]
        sh[0, ROWS:, :] = jnp.where(i < SEQ // ROWS - 1, du1n_ref[...], 0.0)
        _shift_copies(sh)

        def chunk(rs):
            du0 = jnp.zeros((SUB, D_CONV), F32)
            for j in range(CONV_K):
                du0 = du0 + wc_ref[j:j + 1, :] * _tap(sh, rs.start, CONV_K - 1 - j)
            sg = _sig(ag_ref[rs, :])
            out_ref[rs, 0:D_CONV] = (du0 * sg).astype(BF16)
            out_ref[rs, D_CONV:] = (du0 * av_ref[rs, :] * sg * (1.0 - sg)).astype(BF16)

        _for_chunks(chunk)

    return pl.pallas_call(
        body, out_shape=jax.ShapeDtypeStruct((SEQ, 2 * D_CONV), BF16), grid=(SEQ // ROWS,),
        in_specs=[_row_spec(D_CONV, 0), _row_spec(D_CONV, 1), _row_spec(D_CONV, 0), _next_halo(CONV_HALO, D_CONV, 0),
                  _vec_spec(D_CONV, CONV_K)],
        out_specs=_row_spec(2 * D_CONV), scratch_shapes=[pltpu.VMEM((8, CONV_PAD, D_CONV), F32)],
        name=name, compiler_params=_cp("parallel"))(proj, proj, du1, du1, wc)


def _rows(start, size, r):
    return pl.ds(start, size) if r == 1 else pl.ds(start, size, stride=r)


def _unit_rows(r, rho, n, nb):
    win = 2 * ATTN_BLOCK if nb > 1 else ATTN_BLOCK
    if isinstance(n, int):
        kb = max(n - 1, 0)
        q_rows = _rows(rho + r * ATTN_BLOCK * n, ATTN_BLOCK, r)
        k_rows = _rows(rho + r * ATTN_BLOCK * kb, win, r)
    else:
        kb = jnp.maximum(n - 1, 0)
        q_rows = pl.ds(pl.multiple_of(n * ATTN_BLOCK, ATTN_BLOCK), ATTN_BLOCK)
        k_rows = pl.ds(pl.multiple_of(kb * ATTN_BLOCK, ATTN_BLOCK), win)
    dist = (n - kb) * ATTN_BLOCK + lax.broadcasted_iota(jnp.int32, (ATTN_BLOCK, win), 0) \
        - lax.broadcasted_iota(jnp.int32, (ATTN_BLOCK, win), 1)
    return q_rows, k_rows, (dist >= 0) & (dist <= ATTN_BLOCK)


def _per_head(x):
    lane = lax.broadcasted_iota(jnp.int32, x.shape, 1)
    zero = jnp.zeros_like(x)
    return [jnp.where(lane < HEAD_DIM, x, zero), jnp.where(lane >= HEAD_DIM, x, zero)]


def _masked_scores(q2, k2, valid):
    return [jnp.where(valid, lax.dot_general(qh, k2, NT, preferred_element_type=F32) * (HEAD_DIM ** -0.5), NEG)
            for qh in _per_head(q2)]


def _attn_units(r, nb, unit):
    if r == 1:
        def four(i, carry):
            for k in range(4):
                unit(0, 4 * i + k)
            return carry
        lax.fori_loop(0, nb // 4, four, 0)
    else:
        for rho in range(r):
            for n in range(nb):
                unit(rho, n)


N_UNITS = 16


def attn_fwd_all(proj, name):
    def body(q_ref, k_ref, v_ref, att_ref, lse_ref, s_scr, p_scr, lse_scr, den_scr):
        for idx, (sub_len, r) in enumerate(PATTERNS):
            nb = sub_len // ATTN_BLOCK
            win = 2 * ATTN_BLOCK if nb > 1 else ATTN_BLOCK

            def scores(rho, n, r=r, nb=nb, win=win):
                u = rho * nb + n
                q_rows, k_rows, valid = _unit_rows(r, rho, n, nb)
                ss = _masked_scores(q_ref[q_rows, :].astype(BF16), k_ref[k_rows, :].astype(BF16), valid)
                for h in range(2):
                    s_scr[2 * u + h, :, 0:win] = ss[h]

            _attn_units(r, nb, scores)

            def softmax(u, carry, win=win):
                lses, dens = [], []
                for h in range(2):
                    sc = s_scr[2 * u + h, :, 0:win]
                    m = jnp.max(sc, axis=1, keepdims=True)
                    p = jnp.exp(sc - m)
                    den = jnp.sum(p, axis=1, keepdims=True)
                    p_scr[2 * u + h, :, 0:win] = p.astype(BF16)
                    lses.append(jnp.broadcast_to(m + jnp.log(den), (ATTN_BLOCK, HEAD_DIM)))
                    dens.append(jnp.broadcast_to(den, (ATTN_BLOCK, HEAD_DIM)))
                lse_scr[u] = jnp.concatenate(lses, axis=1)
                den_scr[u] = jnp.concatenate(dens, axis=1)
                return carry

            lax.fori_loop(0, N_UNITS, softmax, 0, unroll=2)

            def outputs(rho, n, r=r, nb=nb, win=win, idx=idx):
                u = rho * nb + n
                q_rows, k_rows, _ = _unit_rows(r, rho, n, nb)
                vs = _per_head(v_ref[k_rows, :].astype(BF16))
                o = (jnp.dot(p_scr[2 * u, :, 0:win], vs[0], preferred_element_type=F32)
                     + jnp.dot(p_scr[2 * u + 1, :, 0:win], vs[1], preferred_element_type=F32)) / den_scr[u]
                lse = lse_scr[u]
                if idx > 0:
                    old = lse_ref[q_rows, :]
                    top = jnp.maximum(old, lse)
                    new = top + jnp.log(jnp.exp(old - top) + jnp.exp(lse - top))
                    o = att_ref[q_rows, :] * jnp.exp(old - new) + o * jnp.exp(lse - new)
                    lse = new
                att_ref[q_rows, :] = o
                lse_ref[q_rows, :] = lse

            _attn_units(r, nb, outputs)

    blk = lambda first: pl.BlockSpec((SEQ, 128), lambda g: (0, first + g))
    shp = jax.ShapeDtypeStruct((SEQ, D_ATTN), F32)
    big = (2 * N_UNITS, ATTN_BLOCK, 2 * ATTN_BLOCK)
    small = pltpu.VMEM((N_UNITS, ATTN_BLOCK, 128), F32)
    return pl.pallas_call(
        body, out_shape=(shp, shp), grid=(4,), in_specs=[blk(8), blk(12), blk(16)], out_specs=(blk(0), blk(0)),
        scratch_shapes=[pltpu.VMEM(big, F32), pltpu.VMEM(big, BF16), small, small],
        name=name, compiler_params=_cp("parallel"))(proj, proj, proj)


def attn_bwd_all(proj, do, lse, dd, name):
    scale = HEAD_DIM ** -0.5

    def body(q_ref, k_ref, v_ref, do_ref, l_ref, dd_ref, out_ref, dq_s, dk_s, dv_s,
             s_scr, dp_scr, ds_scr, st_scr, dpt_scr, pt_scr, dst_scr, qb_scr, kb_scr, dob_scr):
        dq_s[...] = jnp.zeros_like(dq_s)
        dk_s[...] = jnp.zeros_like(dk_s)
        dv_s[...] = jnp.zeros_like(dv_s)
        for sub_len, r in PATTERNS:
            nb = sub_len // ATTN_BLOCK
            win = 2 * ATTN_BLOCK if nb > 1 else ATTN_BLOCK

            def scores(rho, n, r=r, nb=nb, win=win):
                u = rho * nb + n
                q_rows, k_rows, valid = _unit_rows(r, rho, n, nb)
                kb = max(n - 1, 0) if isinstance(n, int) else jnp.maximum(n - 1, 0)
                dist_t = (n - kb) * ATTN_BLOCK + lax.broadcasted_iota(jnp.int32, (win, ATTN_BLOCK), 1) \
                    - lax.broadcasted_iota(jnp.int32, (win, ATTN_BLOCK), 0)
                valid_t = (dist_t >= 0) & (dist_t <= ATTN_BLOCK)
                q2 = q_ref[q_rows, :].astype(BF16)
                k2 = k_ref[k_rows, :].astype(BF16)
                do2 = do_ref[q_rows, :].astype(BF16)
                qb_scr[u] = q2
                kb_scr[u, 0:win, :] = k2
                dob_scr[u] = do2
                l2 = l_ref[q_rows, :]
                d2 = dd_ref[q_rows, :]
                l2t = l2.T
                d2t = d2.T
                v2 = v_ref[k_rows, :].astype(BF16)
                qs, dos = _per_head(q2), _per_head(do2)
                for h in range(2):
                    c0 = h * HEAD_DIM
                    sc = lax.dot_general(qs[h], k2, NT, preferred_element_type=F32) * scale
                    s_scr[2 * u + h, :, 0:win] = jnp.where(valid, sc, NEG) - l2[:, c0:c0 + 1]
                    dp_scr[2 * u + h, :, 0:win] = lax.dot_general(dos[h], v2, NT, preferred_element_type=F32) \
                        - d2[:, c0:c0 + 1]
                    sct = lax.dot_general(k2, qs[h], NT, preferred_element_type=F32) * scale
                    st_scr[2 * u + h, 0:win, :] = jnp.where(valid_t, sct, NEG) - l2t[c0:c0 + 1, :]
                    dpt_scr[2 * u + h, 0:win, :] = lax.dot_general(v2, dos[h], NT, preferred_element_type=F32) \
                        - d2t[c0:c0 + 1, :]

            _attn_units(r, nb, scores)

            def pointwise(hu, carry, win=win):
                ds_scr[hu, :, 0:win] = (jnp.exp(s_scr[hu, :, 0:win]) * dp_scr[hu, :, 0:win] * scale).astype(BF16)
                pt = jnp.exp(st_scr[hu, 0:win, :])
                pt_scr[hu, 0:win, :] = pt.astype(BF16)
                dst_scr[hu, 0:win, :] = (pt * dpt_scr[hu, 0:win, :] * scale).astype(BF16)
                return carry

            lax.fori_loop(0, 2 * N_UNITS, pointwise, 0, unroll=4)

            def grads(rho, n, r=r, nb=nb, win=win):
                u = rho * nb + n
                q_rows, k_rows, _ = _unit_rows(r, rho, n, nb)
                qs, ks, dos = _per_head(qb_scr[u]), _per_head(kb_scr[u, 0:win, :]), _per_head(dob_scr[u])

                def both(scr, rows, rhs):
                    return (jnp.dot(scr[(2 * u,) + rows], rhs[0], preferred_element_type=F32)
                            + jnp.dot(scr[(2 * u + 1,) + rows], rhs[1], preferred_element_type=F32))

                dq_s[q_rows, :] += both(ds_scr, (slice(None), slice(0, win)), ks)
                dk_s[k_rows, :] += both(dst_scr, (slice(0, win), slice(None)), qs)
                dv_s[k_rows, :] += both(pt_scr, (slice(0, win), slice(None)), dos)

            _attn_units(r, nb, grads)
        out_ref[0] = dq_s[...].astype(BF16)
        out_ref[1] = dk_s[...].astype(BF16)
        out_ref[2] = dv_s[...].astype(BF16)

    blk = lambda first: pl.BlockSpec((SEQ, 128), lambda g: (0, first + g))
    acc = pltpu.VMEM((SEQ, 128), F32)
    big = (2 * N_UNITS, ATTN_BLOCK, 2 * ATTN_BLOCK)
    big_t = (2 * N_UNITS, 2 * ATTN_BLOCK, ATTN_BLOCK)
    return pl.pallas_call(
        body, out_shape=jax.ShapeDtypeStruct((3, SEQ, D_ATTN), BF16), grid=(4,),
        in_specs=[blk(8), blk(12), blk(16), blk(0), blk(0), blk(0)],
        out_specs=pl.BlockSpec((3, SEQ, 128), lambda g: (0, 0, g)),
        scratch_shapes=[acc, acc, acc, pltpu.VMEM(big, F32), pltpu.VMEM(big, F32), pltpu.VMEM(big, BF16),
                        pltpu.VMEM(big_t, F32), pltpu.VMEM(big_t, F32), pltpu.VMEM(big_t, BF16), pltpu.VMEM(big_t, BF16),
                        pltpu.VMEM((N_UNITS, ATTN_BLOCK, 128), BF16),
                        pltpu.VMEM((N_UNITS, 2 * ATTN_BLOCK, 128), BF16), pltpu.VMEM((N_UNITS, ATTN_BLOCK, 128), BF16)],
        name=name, compiler_params=_cp("parallel"))(proj, proj, proj, do, lse, dd)


N_FT = D_FF // FFN_TN


def _ffn_specs():
    per = ROWS // FFN_HALO
    cur_g = pl.BlockSpec((ROWS, FFN_TN), lambda j, i: (i, j))
    cur_v = pl.BlockSpec((ROWS, FFN_TN), lambda j, i: (i, j + N_FT))
    halo_g = pl.BlockSpec((FFN_HALO, FFN_TN), lambda j, i: (jnp.maximum(i * per - 1, 0), j))
    halo_v = pl.BlockSpec((FFN_HALO, FFN_TN), lambda j, i: (jnp.maximum(i * per - 1, 0), j + N_FT))
    w_g = pl.BlockSpec((FFN_K, FFN_TN), lambda j, i: (0, j))
    w_v = pl.BlockSpec((FFN_K, FFN_TN), lambda j, i: (0, j + N_FT))
    b_g = pl.BlockSpec((1, FFN_TN), lambda j, i: (0, j))
    b_v = pl.BlockSpec((1, FFN_TN), lambda j, i: (0, j + N_FT))
    return [cur_g, cur_v, halo_g, halo_v, w_g, w_v, b_g, b_v]


def matmul(a, b, kind, out_dtype, tm, tn, name, b_rows=None):
    stacked = b_rows is not None
    b_shape = (N_DEV * b_rows, b.shape[2]) if stacked else b.shape
    if kind == "nn":
        (m, k), n = a.shape, b_shape[1]
        a_spec = pl.BlockSpec((tm, k), lambda j, i: (i, 0))
        b_spec = pl.BlockSpec((k, tn), lambda j, i: (0, j))
        dims = (((1,), (0,)), ((), ()))
    elif kind == "nt":
        (m, k), n = a.shape, b_shape[0]
        a_spec = pl.BlockSpec((tm, k), lambda j, i: (i, 0))
        b_spec = pl.BlockSpec((tn, k), lambda j, i: (j, 0))
        dims = (((1,), (1,)), ((), ()))
    else:
        (k, m), n = a.shape, b_shape[1]
        a_spec = pl.BlockSpec((k, tm), lambda j, i: (0, i))
        b_spec = pl.BlockSpec((k, tn), lambda j, i: (0, j))
        dims = (((0,), (0,)), ((), ()))
    assert m % tm == 0 and n % tn == 0, (name, m, n, tm, tn)
    if stacked:
        assert b_spec.block_shape[0] == b_shape[0] and kind in ("nn", "nt")
        width = b_spec.block_shape[1]
        b_spec = pl.BlockSpec((N_DEV, b_rows, width), (lambda j, i: (0, 0, j)) if kind == "nn" else (lambda j, i: (0, 0, 0)))

    def body(a_ref, b_ref, o_ref):
        bb = b_ref[...].reshape(b_shape[0], -1) if stacked else b_ref[...]
        o_ref[...] = lax.dot_general(a_ref[...], bb, dims, preferred_element_type=F32).astype(o_ref.dtype)

    return pl.pallas_call(
        body, out_shape=jax.ShapeDtypeStruct((m, n), out_dtype), grid=(n // tn, m // tm),
        in_specs=[a_spec, b_spec], out_specs=pl.BlockSpec((tm, tn), lambda j, i: (i, j)),
        name=name, compiler_params=_cp("parallel", "parallel"))(a, b)


def matmul_tn_halves(a, b, tm, name):
    _, k, m = a.shape
    n = b.shape[1]

    def body(a_ref, b_ref, o_ref):
        o_ref[0] = lax.dot_general(a_ref[0], b_ref[...], (((0,), (0,)), ((), ())), preferred_element_type=F32).astype(BF16)

    return pl.pallas_call(
        body, out_shape=jax.ShapeDtypeStruct((2, m, n), BF16), grid=(2, m // tm),
        in_specs=[pl.BlockSpec((1, k, tm), lambda h, i: (h, 0, i)), pl.BlockSpec((k, n), lambda h, i: (0, 0))],
        out_specs=pl.BlockSpec((1, tm, n), lambda h, i: (h, i, 0)), name=name,
        compiler_params=_cp("parallel", "parallel"))(a, b).reshape(2 * m, n)


def _row_spec(width, col=0):
    return pl.BlockSpec((ROWS, width), lambda i: (i, col))


def _vec_spec(width, rows=1):
    return pl.BlockSpec((rows, width), lambda i: (0, 0))


def _ffn_shifted(cur_ref, halo_ref, pad, s1, s2):
    i = pl.program_id(1)
    pad[0:FFN_HALO, :] = jnp.where(i > 0, halo_ref[...], 0.0)
    pad[FFN_HALO:, :] = cur_ref[0:FFN_HALO, :]
    for k, dst in ((1, s1), (2, s2)):
        dst[0:FFN_HALO, :] = pad[pl.ds(FFN_HALO - k, FFN_HALO), :]
        dst[FFN_HALO:, :] = cur_ref[pl.ds(FFN_HALO - k, ROWS - FFN_HALO), :]


def _ffn_conv(rs, cur_ref, s1, s2, w_ref, b_ref):
    return b_ref[...] + w_ref[0:1, :] * s2[rs, :] + w_ref[1:2, :] * s1[rs, :] + w_ref[2:3, :] * cur_ref[rs, :]


def ffn_act_fwd(up0, wf, bf, name):
    def body(g_ref, v_ref, gh_ref, vh_ref, wg_ref, wv_ref, bg_ref, bv_ref, act_ref, pad, g1, g2, v1, v2):
        _ffn_shifted(g_ref, gh_ref, pad, g1, g2)
        _ffn_shifted(v_ref, vh_ref, pad, v1, v2)

        def chunk(rs):
            gate = _ffn_conv(rs, g_ref, g1, g2, wg_ref, bg_ref)
            val = _ffn_conv(rs, v_ref, v1, v2, wv_ref, bv_ref)
            act_ref[rs, :] = (gate * _sig(gate) * val).astype(BF16)

        _for_chunks(chunk)

    tile = pltpu.VMEM((ROWS, FFN_TN), F32)
    return pl.pallas_call(
        body, out_shape=jax.ShapeDtypeStruct((SEQ, D_FF), BF16), grid=(N_FT, SEQ // ROWS),
        in_specs=_ffn_specs(), out_specs=pl.BlockSpec((ROWS, FFN_TN), lambda j, i: (i, j)),
        scratch_shapes=[pltpu.VMEM((2 * FFN_HALO, FFN_TN), F32), tile, tile, tile, tile],
        name=name, compiler_params=_cp("parallel", "parallel"))(up0, up0, up0, up0, wf, wf, bf, bf)


def ffn_bwd(up0, dact, wf, bf, name, after=None):
    per = ROWS // FFN_HALO
    last = SEQ // FFN_HALO - 1

    def body(g_ref, v_ref, gh_ref, vh_ref, wg_ref, wv_ref, bg_ref, bv_ref, da_ref, gn_ref, vn_ref, dan_ref,
             out_ref, dbg_ref, dbv_ref, dwg_ref, dwv_ref, pad, g1, g2, v1, v2, dgp, dvp, acc):
        i = pl.program_id(1)
        first = i == 0
        _ffn_shifted(g_ref, gh_ref, pad, g1, g2)
        _ffn_shifted(v_ref, vh_ref, pad, v1, v2)
        acc[...] = jnp.zeros_like(acc)

        def grads(gate, val, da):
            s = _sig(gate)
            return da * val * (s * (1.0 + gate * (1.0 - s))), da * (gate * s)

        def chunk(rs):
            gate = _ffn_conv(rs, g_ref, g1, g2, wg_ref, bg_ref)
            val = _ffn_conv(rs, v_ref, v1, v2, wv_ref, bv_ref)
            dgate, dval = grads(gate, val, da_ref[rs, :])
            dgp[rs, :] = dgate
            dvp[rs, :] = dval
            acc[0] += dgate
            acc[1] += dval
            for t, (sg, sv) in enumerate(((g2, v2), (g1, v1), (g_ref, v_ref))):
                acc[2 + t] += dgate * sg[rs, :]
                acc[5 + t] += dval * sv[rs, :]

        _for_chunks(chunk)
        _acc(dbg_ref, _rsum(acc[0]), first)
        _acc(dbv_ref, _rsum(acc[1]), first)
        _acc(dwg_ref, jnp.concatenate([_rsum(acc[2 + t]) for t in range(FFN_K)], axis=0), first)
        _acc(dwv_ref, jnp.concatenate([_rsum(acc[5 + t]) for t in range(FFN_K)], axis=0), first)

        def conv_next(cur_ref, nxt_ref, w_ref, b_ref):
            pad[0:FFN_HALO, :] = cur_ref[ROWS - FFN_HALO:, :]
            pad[FFN_HALO:, :] = nxt_ref[...]
            return (b_ref[...] + w_ref[0:1, :] * pad[pl.ds(FFN_HALO - 2, FFN_HALO), :]
                    + w_ref[1:2, :] * pad[pl.ds(FFN_HALO - 1, FFN_HALO), :] + w_ref[2:3, :] * nxt_ref[...])

        gate_n = conv_next(g_ref, gn_ref, wg_ref, bg_ref)
        val_n = conv_next(v_ref, vn_ref, wv_ref, bv_ref)
        dgate_n, dval_n = grads(gate_n, val_n, dan_ref[...])
        inside = i < SEQ // ROWS - 1
        dgp[ROWS:, :] = jnp.where(inside, dgate_n, 0.0)
        dvp[ROWS:, :] = jnp.where(inside, dval_n, 0.0)

        for half, (dp, s1, s2, w_ref) in enumerate(((dgp, g1, g2, wg_ref), (dvp, v1, v2, wv_ref))):
            s1[...] = dp[pl.ds(1, ROWS), :]
            s2[...] = dp[pl.ds(2, ROWS), :]

            def back(rs, dp=dp, s1=s1, s2=s2, w_ref=w_ref, half=half):
                out_ref[half, rs, :] = (w_ref[2:3, :] * dp[rs, :] + w_ref[1:2, :] * s1[rs, :]
                                        + w_ref[0:1, :] * s2[rs, :]).astype(BF16)

            _for_chunks(back)

    body, more_specs, more = _with_after(body, 12, after)
    tile = pltpu.VMEM((ROWS, FFN_TN), F32)
    ext = pltpu.VMEM((ROWS + FFN_HALO, FFN_TN), F32)
    vec = jax.ShapeDtypeStruct((1, D_FF), F32)
    taps = jax.ShapeDtypeStruct((FFN_K, D_FF), F32)
    cur = pl.BlockSpec((ROWS, FFN_TN), lambda j, i: (i, j))
    nxt = lambda off: pl.BlockSpec((FFN_HALO, FFN_TN), lambda j, i: (jnp.minimum((i + 1) * per, last), j + off))
    vs = pl.BlockSpec((1, FFN_TN), lambda j, i: (0, j))
    ts = pl.BlockSpec((FFN_K, FFN_TN), lambda j, i: (0, j))
    return pl.pallas_call(
        body, out_shape=(jax.ShapeDtypeStruct((2, SEQ, D_FF), BF16), vec, vec, taps, taps), grid=(N_FT, SEQ // ROWS),
        in_specs=_ffn_specs() + [cur, nxt(0), nxt(N_FT), nxt(0)] + more_specs,
        out_specs=(pl.BlockSpec((2, ROWS, FFN_TN), lambda j, i: (0, i, j)), vs, vs, ts, ts),
        scratch_shapes=[pltpu.VMEM((2 * FFN_HALO, FFN_TN), F32), tile, tile, tile, tile, ext, ext,
                        pltpu.VMEM((2 + 2 * FFN_K, SUB, FFN_TN), F32)],
        name=name, compiler_params=_cp("parallel", "arbitrary"))(up0, up0, up0, up0, wf, wf, bf, bf, dact, up0, up0, dact,
                                                                 *more)


def ada_fwd(c_all, w_ada, b_cols, name):
    def body(c_ref, w_ref, b_ref, o_ref):
        cc = c_ref[...]
        sc = (cc * _sig(cc)).astype(BF16)
        o_ref[...] = jnp.dot(sc, w_ref[...].astype(BF16), preferred_element_type=F32) + b_ref[...]

    return pl.pallas_call(body, out_shape=jax.ShapeDtypeStruct((N_DEV, w_ada.shape[1]), F32), name=name,
                          compiler_params=_cp())(c_all, w_ada, b_cols)


def _adam(w, g, m, v):
    m = ADAM_B1 * m + (1.0 - ADAM_B1) * g
    v = ADAM_B2 * v + (1.0 - ADAM_B2) * (g * g)
    m_hat = m / (1.0 - ADAM_B1 ** ADAM_STEP)
    v_hat = v / (1.0 - ADAM_B2 ** ADAM_STEP)
    delta = -ADAM_LR * (m_hat / (jnp.sqrt(v_hat) + ADAM_EPS) + ADAM_WD * w)
    return delta, m, v


def ada_bwd_adamw(c_all_t, dmod_cols, w, m, v, name):
    rows, cols = w.shape
    tr = 256

    def body(ct_ref, dm_ref, w_ref, m_ref, v_ref, g_ref, d_ref, nm_ref, nv_ref):
        def chunk(rs):
            ct = ct_ref[rs, :]
            sc = ct * _sig(ct)
            g = sc[:, 0:1] * dm_ref[0:1, :]
            for b in range(1, N_DEV):
                g = g + sc[:, b:b + 1] * dm_ref[b:b + 1, :]
            g_ref[rs, :] = g
            d_ref[rs, :], nm_ref[rs, :], nv_ref[rs, :] = _adam(w_ref[rs, :], g, m_ref[rs, :], v_ref[rs, :])

        _for_chunks(chunk, 2, tr)

    blk = pl.BlockSpec((tr, cols), lambda i: (i, 0))
    shp = jax.ShapeDtypeStruct((rows, cols), F32)
    return pl.pallas_call(
        body, out_shape=(shp, shp, shp, shp), grid=(rows // tr,),
        in_specs=[pl.BlockSpec((tr, N_DEV), lambda i: (i, 0)), pl.BlockSpec((N_DEV, cols), lambda i: (0, 0)), blk, blk, blk],
        out_specs=(blk, blk, blk, blk), name=name, compiler_params=_cp("parallel"))(c_all_t, dmod_cols, w, m, v)


def sum_adamw(parts, mine, me, w, m, v, tr, name):
    n_parts, rows, cols = parts.shape

    def body(me_ref, p_ref, own_ref, w_ref, m_ref, v_ref, g_ref, d_ref, nm_ref, nv_ref):
        def chunk(rs):
            g = own_ref[0, rs, :].astype(F32)
            for k in range(1, n_parts):
                g = g + p_ref[k, rs, :].astype(F32)
            g_ref[rs, :] = g
            d_ref[rs, :], nm_ref[rs, :], nv_ref[rs, :] = _adam(w_ref[rs, :], g, m_ref[rs, :], v_ref[rs, :])

        _for_chunks(chunk, 2, tr)

    blk = pl.BlockSpec((tr, cols), lambda i, me_ref: (i, 0))
    shp = jax.ShapeDtypeStruct((rows, cols), F32)
    grid_spec = pltpu.PrefetchScalarGridSpec(
        num_scalar_prefetch=1, grid=(rows // tr,),
        in_specs=[pl.BlockSpec((n_parts, tr, cols), lambda i, me_ref: (0, i, 0)),
                  pl.BlockSpec((1, tr, cols), lambda i, me_ref: (me_ref[0], i, 0)), blk, blk, blk],
        out_specs=(blk, blk, blk, blk))
    return pl.pallas_call(body, out_shape=(shp, shp, shp, shp), grid_spec=grid_spec, name=name,
                          compiler_params=_cp("parallel"))(me, parts, mine, w, m, v)


MESH = pl.DeviceIdType.MESH


def all_gather(block, name, after=None):
    extra = () if after is None else (after,)

    def body(x_ref, *refs):
        out_ref, send_sems, recv_sems, local_sem = refs[len(extra):]
        x, y, c = lax.axis_index("x"), lax.axis_index("y"), lax.axis_index("c")
        me, sibling = (x, y, c), (x, y, 1 - c)
        chips = [(1 - x, y), (x, 1 - y), (1 - x, 1 - y)]

        def slot(px, py, pc):
            return out_ref.at[4 * px + 2 * py + pc]

        def copy(k, blk, to, src=None):
            return pltpu.make_async_remote_copy(
                src_ref=slot(*blk) if src is None else src, dst_ref=slot(*blk),
                send_sem=send_sems.at[k], recv_sem=recv_sems.at[k], device_id=to, device_id_type=MESH)

        mine = pltpu.make_async_copy(x_ref, slot(*me), local_sem)
        mine.start()
        first = [copy(0, me, sibling, src=x_ref)]
        first += [copy(1 + j, me, (*chip, c), src=x_ref) for j, chip in enumerate(chips)]
        for cp in first:
            cp.start()
        passed = [copy(4 + j, (*chip, c), sibling) for j, chip in enumerate(chips)]
        for j, chip in enumerate(chips):
            copy(1 + j, (*chip, c), me).wait_recv()
            passed[j].start()
        copy(0, sibling, me).wait_recv()
        for j, chip in enumerate(chips):
            copy(4 + j, (*chip, 1 - c), me).wait_recv()
        for cp in first + passed:
            cp.wait_send()
        mine.wait()

    return pl.pallas_call(
        body, out_shape=jax.ShapeDtypeStruct((N_DEV,) + block.shape, block.dtype), in_specs=[ANY] * (1 + len(extra)), out_specs=ANY,
        scratch_shapes=[pltpu.SemaphoreType.DMA((7,)), pltpu.SemaphoreType.DMA((7,)), pltpu.SemaphoreType.DMA],
        name=name)(block, *extra)


HBM = pl.BlockSpec(memory_space=pltpu.HBM)
SEM = pl.BlockSpec(memory_space=pltpu.SEMAPHORE)
EFFECT = pltpu.SideEffectType.DATAFLOW_SIDE_EFFECTING


def _peer_copies(src_ref, land_ref, send_sems, recv_sems, gather):
    x, y, c = lax.axis_index("x"), lax.axis_index("y"), lax.axis_index("c")
    me = 4 * x + 2 * y + c
    copies = []
    for k in range(1, N_DEV):
        px = 1 - x if k & 4 else x
        py = 1 - y if k & 2 else y
        pc = 1 - c if k & 1 else c
        copies.append(pltpu.make_async_remote_copy(
            src_ref=src_ref if gather else src_ref.at[4 * px + 2 * py + pc],
            dst_ref=land_ref.at[me] if gather else land_ref.at[k],
            send_sem=send_sems.at[k - 1], recv_sem=recv_sems.at[k - 1], device_id=(px, py, pc), device_id_type=MESH))
    return copies


def exchange_start(srcs, gather, name, after=None):
    n = len(srcs)
    land_shapes = [(N_DEV,) + src.shape if gather else src.shape for src in srcs]
    extra = () if after is None else (after,)

    def body(*refs):
        src_refs, land_refs = refs[0:n], refs[n:2 * n]
        outs = refs[2 * n + len(extra):]
        for k in range(n):
            for cp in _peer_copies(src_refs[k], land_refs[k], outs[4 * k], outs[4 * k + 1], gather):
                cp.start()
        token = outs[4 * n]
        token[...] = jnp.zeros_like(token)

    out_shape, out_specs, aliases = [], [], {}
    for k, src in enumerate(srcs):
        out_shape += [pltpu.SemaphoreType.DMA((N_DEV - 1,)), pltpu.SemaphoreType.DMA((N_DEV - 1,)),
                      pltpu.HBM(src.shape, src.dtype), pltpu.HBM(land_shapes[k], src.dtype)]
        out_specs += [SEM, SEM, HBM, HBM]
        aliases[k] = 4 * k + 2
        aliases[n + k] = 4 * k + 3
    out_shape.append(jax.ShapeDtypeStruct((8, 128), F32))
    out_specs.append(pl.BlockSpec(memory_space=pltpu.VMEM))
    res = pl.pallas_call(
        body, name=name, out_shape=tuple(out_shape), in_specs=(HBM,) * (2 * n) + (ANY,) * len(extra),
        out_specs=tuple(out_specs), input_output_aliases=aliases,
        compiler_params=pltpu.CompilerParams(has_side_effects=EFFECT),
    )(*[pltpu.with_memory_space_constraint(src, pltpu.HBM) for src in srcs],
      *[pltpu.with_memory_space_constraint(lax.empty(shp, src.dtype), pltpu.HBM) for shp, src in zip(land_shapes, srcs)],
      *extra)
    return [tuple(res[4 * k:4 * k + 4]) for k in range(n)], res[4 * n]


def _stage1_peer(i):
    x, y, c = lax.axis_index("x"), lax.axis_index("y"), lax.axis_index("c")
    if i == 0:
        return (x, y, 1 - c)
    return (1 - x if i & 1 else x, 1 - y if i & 2 else y, c)


def _slot_of(peer):
    return 4 * peer[0] + 2 * peer[1] + peer[2]


def _stage1_copy(i, src_ref, land_ref, send_sems, recv_sems):
    me = _slot_of((lax.axis_index("x"), lax.axis_index("y"), lax.axis_index("c")))
    return pltpu.make_async_remote_copy(src_ref=src_ref, dst_ref=land_ref.at[me], send_sem=send_sems.at[i],
                                        recv_sem=recv_sems.at[i], device_id=_stage1_peer(i), device_id_type=MESH)


def _stage2_copy(j, land_ref, send_sems, recv_sems):
    slot = _slot_of(_stage1_peer(j + 1))
    return pltpu.make_async_remote_copy(src_ref=land_ref.at[slot], dst_ref=land_ref.at[slot], send_sem=send_sems.at[j],
                                        recv_sem=recv_sems.at[j], device_id=_stage1_peer(0), device_id_type=MESH)


def gather2_start(srcs, name, after=None):
    n = len(srcs)
    extra = () if after is None else (after,)

    def body(*refs):
        src_refs, land_refs = refs[0:n], refs[n:2 * n]
        outs = refs[2 * n + len(extra):]
        for k in range(n):
            for i in range(4):
                _stage1_copy(i, src_refs[k], land_refs[k], outs[4 * k], outs[4 * k + 1]).start()
        outs[4 * n][...] = jnp.zeros((8, 128), F32)

    out_shape, out_specs, aliases = [], [], {}
    for k, src in enumerate(srcs):
        out_shape += [pltpu.SemaphoreType.DMA((4,)), pltpu.SemaphoreType.DMA((4,)),
                      pltpu.HBM(src.shape, src.dtype), pltpu.HBM((N_DEV,) + src.shape, src.dtype)]
        out_specs += [SEM, SEM, HBM, HBM]
        aliases[k] = 4 * k + 2
        aliases[n + k] = 4 * k + 3
    out_shape.append(jax.ShapeDtypeStruct((8, 128), F32))
    out_specs.append(pl.BlockSpec(memory_space=pltpu.VMEM))
    res = pl.pallas_call(
        body, name=name, out_shape=tuple(out_shape), in_specs=(HBM,) * (2 * n) + (ANY,) * len(extra),
        out_specs=tuple(out_specs), input_output_aliases=aliases,
        compiler_params=pltpu.CompilerParams(has_side_effects=EFFECT),
    )(*[pltpu.with_memory_space_constraint(src, pltpu.HBM) for src in srcs],
      *[pltpu.with_memory_space_constraint(lax.empty((N_DEV,) + src.shape, src.dtype), pltpu.HBM) for src in srcs], *extra)
    return [dict(send1=res[4 * k], recv1=res[4 * k + 1], src=res[4 * k + 2], land=res[4 * k + 3]) for k in range(n)], \
        res[4 * n]


def gather2_pass(handles, after, name):
    n = len(handles)

    def body(*refs):
        src_refs, land_refs, recv1 = refs[0:n], refs[n:2 * n], refs[2 * n:3 * n]
        outs = refs[3 * n + 1:]
        for k in range(n):
            for j in range(3):
                _stage1_copy(j + 1, src_refs[k], land_refs[k], recv1[k], recv1[k]).wait_recv()
                _stage2_copy(j, land_refs[k], outs[3 * k], outs[3 * k + 1]).start()

    out_shape, out_specs, aliases = [], [], {}
    for k, h in enumerate(handles):
        out_shape += [pltpu.SemaphoreType.DMA((3,)), pltpu.SemaphoreType.DMA((3,)), pltpu.HBM(h["land"].shape, h["land"].dtype)]
        out_specs += [SEM, SEM, HBM]
        aliases[n + k] = 3 * k + 2
    res = pl.pallas_call(
        body, name=name, out_shape=tuple(out_shape), in_specs=(HBM,) * (2 * n) + (SEM,) * n + (ANY,),
        out_specs=tuple(out_specs), input_output_aliases=aliases,
        compiler_params=pltpu.CompilerParams(has_side_effects=EFFECT),
    )(*[h["src"] for h in handles], *[h["land"] for h in handles], *[h["recv1"] for h in handles], after)
    return [dict(h, send2=res[3 * k], recv2=res[3 * k + 1], land=res[3 * k + 2]) for k, h in enumerate(handles)]


def gather2_wait(h, after, name):
    def body(src_ref, land_ref, send1, recv1, send2, recv2, after_ref, src_dead, got_ref):
        for i in range(4):
            _stage1_copy(i, src_ref, land_ref, send1, recv1).wait_send()
        _stage1_copy(0, src_ref, land_ref, send1, recv1).wait_recv()
        for j in range(3):
            cp = _stage2_copy(j, land_ref, send2, recv2)
            cp.wait_send()
            cp.wait_recv()

    return pl.pallas_call(
        body, name=name,
        out_shape=(pltpu.HBM(h["src"].shape, h["src"].dtype), pltpu.HBM(h["land"].shape, h["land"].dtype)),
        in_specs=(HBM, HBM, SEM, SEM, SEM, SEM, ANY), out_specs=(HBM, HBM), input_output_aliases={0: 0, 1: 1},
        compiler_params=pltpu.CompilerParams(has_side_effects=EFFECT),
    )(h["src"], h["land"], h["send1"], h["recv1"], h["send2"], h["recv2"], after)[1]


def exchange_wait(handles, after, gather, name):
    send_sems, recv_sems, src_thru, land_thru = handles

    def body(src_ref, land_ref, send_sems, recv_sems, after_ref, src_dead, got_ref):
        for cp in _peer_copies(src_ref, land_ref, send_sems, recv_sems, gather):
            cp.wait_send()
            cp.wait_recv()

    return pl.pallas_call(
        body, name=name,
        out_shape=(pltpu.HBM(src_thru.shape, src_thru.dtype), pltpu.HBM(land_thru.shape, land_thru.dtype)),
        in_specs=(HBM, HBM, SEM, SEM, ANY), out_specs=(HBM, HBM), input_output_aliases={0: 0, 1: 1},
        compiler_params=pltpu.CompilerParams(has_side_effects=EFFECT),
    )(src_thru, land_thru, send_sems, recv_sems, after)


def local_step(x, tgt, mod, started, get_w, put_grad, wc, wf, g_mix, bc, lg, lb, gco, gao, g_ffn, bf, g_fin):
    w_in = get_w("w_in", mod)
    proj, h1 = rms_mod_matmul(x, g_mix, mod, 0, 1, w_in, D_IN // N_DEV, "proj_fwd", after=started)
    mix_a, u1 = conv_module_fwd(proj, wc, bc, lg, lb, gco, "conv_module_fwd")
    att, lse = attn_fwd_all(proj, "attn_fwd")
    w_out = get_w("w_out", att)
    y1, mixed = norm_concat_matmul(mix_a, att, gao, w_out, "out_proj_fwd")
    w_up = get_w("w_up", y1)
    up0, x1, h2 = resid_rms_mod_matmul(x, y1, g_ffn, mod, 2, 3, 4, w_up, "up_fwd")
    act = ffn_act_fwd(up0, wf, bf, "ffn_act_fwd")
    w_down = get_w("w_down", act)
    loss_t, dx2, dy2, d_gfin, d_gaf = matmul_loss_bwd(act, w_down, x1, tgt, g_fin, mod, 5, "down_fwd_loss")
    dact = matmul(dy2, w_down, "nt", F32, 512, FFN_TN, "down_bwd_x")
    dw_down = matmul(act, dy2, "tn", BF16, 256, D_MODEL, "down_bwd_w")
    dup0, dbf_g, dbf_v, dwf_g, dwf_v = ffn_bwd(up0, dact, wf, bf, "ffn_bwd", after=put_grad("w_down", dw_down))
    dw_up = matmul_tn_halves(dup0, h2, 256, "up_bwd_w")
    dx1, d_shf, d_scf, d_gffn, dy1, d_gam = matmul_rms_mod_bwd(
        dup0, w_up, x1, dx2, g_ffn, mod, 4, y1, 2, "up_bwd_x", after=put_grad("w_up", dw_up))
    dw_out = matmul(mixed, dy1, "tn", BF16, 256, D_MODEL, "out_proj_bwd_w")
    dmixed, do, dd, d_gao = matmul_combine_bwd(dy1, w_out, att, gao, "out_proj_bwd_x", after=put_grad("w_out", dw_out))
    dqkv = attn_bwd_all(proj, do, lse, dd, "attn_bwd")
    du1, d_gco, d_lg, d_lb, d_bc, d_wc = conv_module_bwd_a(proj, u1, dmixed, lg, lb, gco, "conv_module_bwd_a")
    dproj_a = conv_module_bwd_b(proj, du1, wc, "conv_module_bwd_b")
    dproj = jnp.concatenate([dproj_a, dqkv[0], dqkv[1], dqkv[2]], axis=1)
    dw_in = matmul(dproj, h1, "tn", BF16, 512, D_MODEL, "proj_bwd_w")
    dx, d_shm, d_scm, d_gmix = matmul_rms_mod_bwd(
        dproj, w_in, x, dx1, g_mix, mod, 1, None, 0, "proj_bwd_x", b_rows=D_IN // N_DEV, after=put_grad("w_in", dw_in))
    dmod = jnp.concatenate([d_shm, d_scm, d_gam, d_shf, d_scf, d_gaf], axis=1)
    small = dict(g_norm_mix=d_gmix, b_conv_dw=d_bc, ln_conv_g=d_lg, ln_conv_b=d_lb, g_conv_out=d_gco, g_attn_out=d_gao,
                 g_norm_ffn=d_gffn, b_ffn_dw=jnp.concatenate([dbf_g, dbf_v], axis=1), g_final=d_gfin,
                 w_conv_dw=d_wc, w_ffn_dw=jnp.concatenate([dwf_g, dwf_v], axis=1), dmod=dmod, loss=loss_t[0:1, 0:1])
    return dx, small


def _padw(a, width):
    return jnp.pad(a, ((0, 0), (0, width - a.shape[1])))


def pack_small(t):
    wide = jnp.concatenate([_padw(t["dmod"], PACK_W), _padw(t["b_ffn_dw"], PACK_W), _padw(t["w_ffn_dw"], PACK_W),
                            _padw(t["loss"], PACK_W), jnp.zeros((2, PACK_W), F32)], axis=0)
    z512 = jnp.zeros((1, 512), F32)
    narrow = jnp.concatenate([
        t["g_norm_mix"], t["g_norm_ffn"], t["g_final"],
        jnp.concatenate([t["b_conv_dw"], t["ln_conv_g"]], axis=1),
        jnp.concatenate([t["ln_conv_b"], t["g_conv_out"]], axis=1),
        jnp.concatenate([t["g_attn_out"], z512], axis=1),
        jnp.zeros((2, 1024), F32),
        jnp.pad(t["w_conv_dw"], ((0, 1), (0, 0))).reshape(16, 1024)], axis=0)
    return jnp.concatenate([wide, narrow.reshape(4, PACK_W), jnp.zeros((4, PACK_W), F32)], axis=0)


_NARROW = lambda k, off=0: (8 + k // 6, (k % 6) * 1024 + off)
PACKED_AT = dict(
    b_ada=(0, 0, N_MOD * D_MODEL), b_ffn_dw=(1, 0, 2 * D_FF),
    g_norm_mix=_NARROW(0) + (D_MODEL,), g_norm_ffn=_NARROW(1) + (D_MODEL,), g_final=_NARROW(2) + (D_MODEL,),
    b_conv_dw=_NARROW(3) + (D_CONV,), ln_conv_g=_NARROW(3, 512) + (D_CONV,),
    ln_conv_b=_NARROW(4) + (D_CONV,), g_conv_out=_NARROW(4, 512) + (D_CONV,), g_attn_out=_NARROW(5) + (D_ATTN,))
SMALL_ORDER = list(PACKED_AT)


def small_adamw(parts, wmv, name):
    def body(*refs):
        p_ref = refs[0]
        ins = refs[1:1 + 3 * len(SMALL_ORDER)]
        outs = refs[1 + 3 * len(SMALL_ORDER):]
        g = p_ref[0]
        for k in range(1, N_DEV):
            g = g + p_ref[k]
        for i, n in enumerate(SMALL_ORDER):
            row, lane, width = PACKED_AT[n]
            gp = g[row:row + 1, lane:lane + width]
            w_ref, m_ref, v_ref = ins[3 * i:3 * i + 3]
            g_ref, d_ref, nm_ref, nv_ref = outs[4 * i:4 * i + 4]
            g_ref[...] = gp
            d_ref[...], nm_ref[...], nv_ref[...] = _adam(w_ref[...], gp, m_ref[...], v_ref[...])
        wc_ref, wf_ref, loss_ref = outs[4 * len(SMALL_ORDER):]
        for j in range(CONV_K):
            row, lane = _NARROW(8 + j // 2, (j % 2) * 512)
            wc_ref[j:j + 1, :] = g[row:row + 1, lane:lane + D_CONV]
        wf_ref[...] = g[2:2 + FFN_K, 0:2 * D_FF]
        loss_ref[...] = jnp.broadcast_to(g[5:6, 0:1], (8, 128))

    args, out_shape = [parts], []
    for n in SMALL_ORDER:
        args += list(wmv[n])
        out_shape += [jax.ShapeDtypeStruct(wmv[n][0].shape, F32)] * 4
    out_shape += [jax.ShapeDtypeStruct((CONV_K, D_CONV), F32), jax.ShapeDtypeStruct((FFN_K, 2 * D_FF), F32),
                  jax.ShapeDtypeStruct((8, 128), F32)]
    res = pl.pallas_call(body, out_shape=tuple(out_shape), name=name, compiler_params=_cp())(*args)
    per = {n: tuple(res[4 * i:4 * i + 4]) for i, n in enumerate(SMALL_ORDER)}
    return per, res[-3], res[-2], res[-1][0, 0]


def shard_adamw(items, name):
    def body(*refs):
        ins, outs = refs[:4 * len(items)], refs[4 * len(items):]
        for i in range(len(items)):
            g_ref, w_ref, m_ref, v_ref = ins[4 * i:4 * i + 4]
            og_ref, d_ref, nm_ref, nv_ref = outs[4 * i:4 * i + 4]
            og_ref[...] = g_ref[...]
            d_ref[...], nm_ref[...], nv_ref[...] = _adam(w_ref[...], g_ref[...], m_ref[...], v_ref[...])

    args = [a for item in items for a in item]
    out_shape = tuple(jax.ShapeDtypeStruct(item[1].shape, F32) for item in items for _ in range(4))
    res = pl.pallas_call(body, out_shape=out_shape, name=name, compiler_params=_cp())(*args)
    return [tuple(res[4 * i:4 * i + 4]) for i in range(len(items))]


def _shard(full, n_cols, me):
    return lax.dynamic_slice(full, (0, me * n_cols), (full.shape[0], n_cols))


WEIGHTS = ["w_ada", "b_ada", "g_norm_mix", "w_in", "w_conv_dw", "b_conv_dw", "ln_conv_g", "ln_conv_b", "g_conv_out",
           "g_attn_out", "w_out", "g_norm_ffn", "w_up", "w_ffn_dw", "b_ffn_dw", "w_down", "g_final"]


def kernel(x, c, w_ada, b_ada, g_norm_mix, w_in, w_conv_dw, b_conv_dw, ln_conv_g, ln_conv_b, g_conv_out, g_attn_out, w_out, g_norm_ffn, w_up, w_ffn_dw, b_ffn_dw, w_down, g_final, loss_target, m_w_ada, m_b_ada, m_g_norm_mix, m_w_in, m_w_conv_dw, m_b_conv_dw, m_ln_conv_g, m_ln_conv_b, m_g_conv_out, m_g_attn_out, m_w_out, m_g_norm_ffn, m_w_up, m_w_ffn_dw, m_b_ffn_dw, m_w_down, m_g_final, v_w_ada, v_b_ada, v_g_norm_mix, v_w_in, v_w_conv_dw, v_b_conv_dw, v_ln_conv_g, v_ln_conv_b, v_g_conv_out, v_g_attn_out, v_w_out, v_g_norm_ffn, v_w_up, v_w_ffn_dw, v_b_ffn_dw, v_w_down, v_g_final):
    args = dict(locals())
    me = 4 * lax.axis_index("x") + 2 * lax.axis_index("y") + lax.axis_index("c")
    me1 = me.astype(jnp.int32).reshape(1)

    def flat(name, prefix=""):
        a = args[prefix + name]
        return a.reshape(a.shape[-2] if a.ndim > 1 else 1, a.shape[-1])

    def flat_t(name, prefix=""):
        return args[prefix + name][0].T

    n_in, n_up, r_out, r_down = w_in.shape[2], w_up.shape[2], w_out.shape[1], w_down.shape[1]
    n_ada, n_wc, n_wf = w_ada.shape[2], w_conv_dw.shape[2], w_ffn_dw.shape[2]
    taps_c = jnp.pad(flat("w_conv_dw").reshape(1, CONV_K * n_wc), ((0, 0), (0, 2 * D_MODEL - CONV_K * n_wc)))
    taps_f = jnp.pad(flat("w_ffn_dw").reshape(1, FFN_K * n_wf), ((0, 0), (0, 3 * D_MODEL - FFN_K * n_wf)))
    first = jnp.concatenate([c, taps_c.reshape(2, D_MODEL), taps_f.reshape(3, D_MODEL), jnp.zeros((2, D_MODEL), F32)], axis=0)
    w_in_block = flat_t("w_in").astype(BF16)
    hi = lax.reduce_precision(first, 8, 7)
    mid = lax.reduce_precision(first - hi, 8, 7)
    low = lax.reduce_precision(first - hi - mid, 8, 7)
    terms = jnp.concatenate([hi, mid, low, jnp.zeros((8, D_MODEL), F32)], axis=0).astype(BF16)
    first_block = all_gather(jnp.concatenate([w_in_block, terms], axis=0), "gather_c_taps_w_in")
    terms = first_block[:, n_in:n_in + 24, :].astype(F32).reshape(N_DEV, 3, 8, D_MODEL)
    first_all = (terms[:, 0] + terms[:, 1]) + terms[:, 2]
    c_all = first_all[:, 0, :]
    wc_full = first_all[:, 1:3, :].reshape(N_DEV, 2 * D_MODEL)[:, :CONV_K * n_wc].reshape(N_DEV, CONV_K, n_wc)
    wc_full = wc_full.transpose(1, 0, 2).reshape(CONV_K, D_CONV)
    wf_full = first_all[:, 3:6, :].reshape(N_DEV, 3 * D_MODEL)[:, :FFN_K * n_wf].reshape(N_DEV, FFN_K, n_wf)
    wf_full = wf_full.transpose(1, 0, 2).reshape(FFN_K, 2 * D_FF)
    mod_cols = ada_fwd(c_all, flat("w_ada"), _shard(flat("b_ada"), n_ada, me), "ada_fwd")
    mod_all = all_gather(mod_cols, "gather_mod")
    mod = lax.dynamic_index_in_dim(mod_all, me, axis=1, keepdims=False).reshape(N_MOD, D_MODEL)
    mod = jnp.pad(mod, ((0, 2), (0, 0)))

    order = ("w_out", "w_up", "w_down")
    blocks = dict(w_up=flat_t("w_up").astype(BF16), w_out=flat("w_out").astype(BF16), w_down=flat("w_down").astype(BF16))
    handles, tok = gather2_start([blocks[name] for name in order], "gather_weights_start", mod_all)
    gathers = dict(zip(order, handles))

    def gathered(name, after):
        if "send2" not in gathers[name]:
            group = ("w_out", "w_up") if name != "w_down" else ("w_down",)
            gathers.update(zip(group, gather2_pass([gathers[w] for w in group], after, f"gather_{name}_pass")))
        land = gather2_wait(gathers[name], after, f"gather_{name}_wait")
        return lax.dynamic_update_index_in_dim(land, blocks[name], me, axis=0)

    def get_w(name, after):
        if name == "w_in":
            return first_block
        return gathered(name, after).reshape(-1, D_MODEL)

    exchanges = {}

    def put_grad(name, dw, after=None):
        dev_major = dw.reshape(N_DEV, -1, D_MODEL)
        (exchanges[name],), token = exchange_start([dev_major], False, f"exchange_{name}_start", after)
        return token

    grad_x, small = local_step(
        x[0], loss_target[0], mod, tok, get_w, put_grad, wc_full, wf_full,
        flat("g_norm_mix"), flat("b_conv_dw"), flat("ln_conv_g"), flat("ln_conv_b"), flat("g_conv_out"),
        flat("g_attn_out"), flat("g_norm_ffn"), flat("b_ffn_dw"), flat("g_final"))

    out = {}

    def finish(name, tr, after):
        mine, parts = exchange_wait(exchanges[name], after, False, f"exchange_{name}_wait")
        if name in ("w_in", "w_up"):
            res = sum_adamw(parts, mine, me1, flat_t(name), flat_t(name, "m_"), flat_t(name, "v_"), tr, "adamw_" + name)
            out[name] = tuple(r.T for r in res)
        else:
            res = out[name] = sum_adamw(parts, mine, me1, flat(name), flat(name, "m_"), flat(name, "v_"), tr,
                                        "adamw_" + name)
        return res[0]

    after = finish("w_down", r_down, grad_x)
    after = finish("w_up", n_up // 2, after)
    after = finish("w_out", r_out, after)
    after = finish("w_in", n_in, after)

    small_all = all_gather(pack_small(small), "gather_small", after)

    wmv = {n: (flat(n), flat(n, "m_"), flat(n, "v_")) for n in SMALL_ORDER}
    per, g_wc, g_wf, loss = small_adamw(small_all, wmv, "adamw_small")
    out.update(per)
    taps = shard_adamw([(_shard(g_wc, n_wc, me), flat("w_conv_dw"), flat("w_conv_dw", "m_"), flat("w_conv_dw", "v_")),
                        (_shard(g_wf, n_wf, me), flat("w_ffn_dw"), flat("w_ffn_dw", "m_"), flat("w_ffn_dw", "v_"))],
                       "adamw_taps")
    out["w_conv_dw"], out["w_ffn_dw"] = taps

    dmod_cols = _shard(small_all[:, 0, :], n_ada, me)
    out["w_ada"] = ada_bwd_adamw(c_all.T, dmod_cols, flat("w_ada"), flat("w_ada", "m_"), flat("w_ada", "v_"), "adamw_w_ada")

    result = [loss, grad_x[None]]
    for k in range(4):
        result += [out[n][k].reshape(args[n].shape) for n in WEIGHTS]
    return tuple(result)
```

```python
import functools

import jax
import jax.numpy as jnp
from jax import lax
from jax.experimental import pallas as pl
from jax.experimental.pallas import tpu as pltpu

F32 = jnp.float32
BF16 = jnp.bfloat16

N_DEV = 8
SEQ = 2048
D_MODEL = 1024
D_CONV = 512
D_ATTN = 512
HEAD_DIM = 64
CONV_K = 31
D_FF = 2816
FFN_K = 3
D_IN = 2 * D_CONV + 3 * D_ATTN
N_MOD = 6
EPS = 1e-6
ATTN_BLOCK = 128
PATTERNS = ((2048, 1), (512, 4), (128, 16))
NEG = -1e30

ADAM_LR, ADAM_B1, ADAM_B2, ADAM_EPS, ADAM_WD, ADAM_STEP = 0.001, 0.9, 0.999, 1e-08, 0.01, 10

ROWS = 256
CONV_HALO = 32
FFN_HALO = 8
FFN_TN = 1408
VMEM_LIMIT = 56 * 1024 * 1024
PACK_W = 6144


NT = (((1,), (1,)), ((), ()))
ANY = pl.BlockSpec(memory_space=pl.ANY)


def _cp(*sem):
    return pltpu.CompilerParams(dimension_semantics=sem if sem else None, vmem_limit_bytes=VMEM_LIMIT)


def _with_after(body, n_in, after):
    if after is None:
        return body, [], []
    return (lambda *refs: body(*refs[:n_in], *refs[n_in + 1:])), [ANY], [after]


def _sig(x):
    return 1.0 / (1.0 + jnp.exp(-x))


def _rsum(x):
    return jnp.sum(x, axis=0, keepdims=True)


def _mean(x):
    return jnp.mean(x, axis=-1, keepdims=True)


def _acc(ref, val, first):
    @pl.when(first)
    def _():
        ref[...] = val

    @pl.when(jnp.logical_not(first))
    def _():
        ref[...] += val


SUB = 16


def _for_chunks(fn, unroll=1, rows=ROWS):
    def step(i, carry):
        fn(pl.ds(pl.multiple_of(i * SUB, SUB), SUB))
        return carry

    lax.fori_loop(0, rows // SUB, step, 0, unroll=unroll)


def rms_mod_matmul(x, g, mod, sh_row, sc_row, b, b_rows, name, after=None):
    n = N_DEV * b_rows

    def body(x_ref, g_ref, mod_ref, b_ref, o_ref, h_ref):
        xx = x_ref[...]
        r = lax.rsqrt(_mean(xx * xx) + EPS)
        h = (xx * r * g_ref[...] * (1.0 + mod_ref[sc_row:sc_row + 1, :]) + mod_ref[sh_row:sh_row + 1, :]).astype(BF16)
        h_ref[...] = h
        o_ref[...] = lax.dot_general(h, b_ref[...].reshape(n, D_MODEL), NT, preferred_element_type=F32)

    body, more_specs, more = _with_after(body, 4, after)
    return pl.pallas_call(
        body, out_shape=(jax.ShapeDtypeStruct((SEQ, n), F32), jax.ShapeDtypeStruct((SEQ, D_MODEL), BF16)),
        grid=(SEQ // ROWS,),
        in_specs=[_row_spec(D_MODEL), _vec_spec(D_MODEL), _vec_spec(D_MODEL, 8),
                  pl.BlockSpec((N_DEV, b_rows, D_MODEL), lambda i: (0, 0, 0))] + more_specs,
        out_specs=(_row_spec(n), _row_spec(D_MODEL)), name=name, compiler_params=_cp("parallel"))(x, g, mod, b, *more)


def norm_concat_matmul(mix_a, att, gao, w, name):
    def body(a_ref, att_ref, g_ref, w_ref, y_ref, mixed_ref):
        aa = att_ref[...]
        mixed_ref[:, 0:D_CONV] = a_ref[...]
        mixed_ref[:, D_CONV:] = (aa * lax.rsqrt(_mean(aa * aa) + EPS) * g_ref[...]).astype(BF16)
        y_ref[...] = jnp.dot(mixed_ref[...], w_ref[...], preferred_element_type=F32)

    return pl.pallas_call(
        body, out_shape=(jax.ShapeDtypeStruct((SEQ, D_MODEL), F32), jax.ShapeDtypeStruct((SEQ, D_MODEL), BF16)),
        grid=(SEQ // ROWS,),
        in_specs=[_row_spec(D_CONV), _row_spec(D_ATTN), _vec_spec(D_ATTN), pl.BlockSpec(w.shape, lambda i: (0, 0))],
        out_specs=(_row_spec(D_MODEL), _row_spec(D_MODEL)), name=name, compiler_params=_cp("parallel"))(mix_a, att, gao, w)


def resid_rms_mod_matmul(x, y, g, mod, ga_row, sh_row, sc_row, w, name):
    n = w.shape[0]

    def body(x_ref, y_ref, g_ref, mod_ref, w_ref, o_ref, x1_ref, h_ref):
        x1 = x_ref[...] + mod_ref[ga_row:ga_row + 1, :] * y_ref[...]
        x1_ref[...] = x1
        r = lax.rsqrt(_mean(x1 * x1) + EPS)
        h = (x1 * r * g_ref[...] * (1.0 + mod_ref[sc_row:sc_row + 1, :]) + mod_ref[sh_row:sh_row + 1, :]).astype(BF16)
        h_ref[...] = h
        o_ref[...] = lax.dot_general(h, w_ref[...], NT, preferred_element_type=F32)

    return pl.pallas_call(
        body,
        out_shape=(jax.ShapeDtypeStruct((SEQ, n), F32), jax.ShapeDtypeStruct((SEQ, D_MODEL), F32),
                   jax.ShapeDtypeStruct((SEQ, D_MODEL), BF16)),
        grid=(SEQ // ROWS,),
        in_specs=[_row_spec(D_MODEL), _row_spec(D_MODEL), _vec_spec(D_MODEL), _vec_spec(D_MODEL, 8),
                  pl.BlockSpec(w.shape, lambda i: (0, 0))],
        out_specs=(_row_spec(n), _row_spec(D_MODEL), _row_spec(D_MODEL)),
        name=name, compiler_params=_cp("parallel"))(x, y, g, mod, w)


def matmul_combine_bwd(dy, w, att, gao, name, after=None):
    def body(dy_ref, w_ref, att_ref, g_ref, dm_ref, do_ref, dd_ref, dg_ref):
        first = pl.program_id(0) == 0
        dm_ref[...] = lax.dot_general(dy_ref[...], w_ref[...], NT, preferred_element_type=F32)
        att = att_ref[...]
        r = lax.rsqrt(_mean(att * att) + EPS)
        xn = att * r
        dm = dm_ref[:, D_CONV:]
        _acc(dg_ref, _rsum(dm * xn), first)
        dyn = dm * g_ref[...]
        do = r * (dyn - xn * _mean(dyn * xn))
        do_ref[...] = do
        same_head = (jnp.right_shift(lax.broadcasted_iota(jnp.int32, (D_ATTN, D_ATTN), 0), 6)
                     == jnp.right_shift(lax.broadcasted_iota(jnp.int32, (D_ATTN, D_ATTN), 1), 6)).astype(F32)
        dd_ref[...] = jnp.dot(do * att, same_head, preferred_element_type=F32, precision=lax.Precision.HIGHEST)

    rs = _row_spec(D_ATTN)
    f = jax.ShapeDtypeStruct((SEQ, D_ATTN), F32)
    body, more_specs, more = _with_after(body, 4, after)
    return pl.pallas_call(
        body, out_shape=(jax.ShapeDtypeStruct((SEQ, D_MODEL), F32), f, f, jax.ShapeDtypeStruct((1, D_ATTN), F32)),
        grid=(SEQ // ROWS,),
        in_specs=[_row_spec(D_MODEL), pl.BlockSpec(w.shape, lambda i: (0, 0)), rs, _vec_spec(D_ATTN)] + more_specs,
        out_specs=(_row_spec(D_MODEL), rs, rs, _vec_spec(D_ATTN)),
        name=name, compiler_params=_cp("arbitrary"))(dy, w, att, gao, *more)


def matmul_loss_bwd(act, w, x1, tgt, g, mod, ga_row, name):
    def body(a_ref, w_ref, x1_ref, t_ref, g_ref, mod_ref, loss_ref, dx2_ref, dy2_ref, dg_ref, dga_ref):
        first = pl.program_id(0) == 0
        ga = mod_ref[ga_row:ga_row + 1, :]
        y2 = jnp.dot(a_ref[...], w_ref[...], preferred_element_type=F32)
        x2 = x1_ref[...] + ga * y2
        r = lax.rsqrt(_mean(x2 * x2) + EPS)
        xn = x2 * r
        err = xn * g_ref[...] - t_ref[...]
        _acc(loss_ref, jnp.broadcast_to(0.5 * jnp.sum(_mean(err * err)), (8, 128)), first)
        dy = err * (1.0 / D_MODEL)
        _acc(dg_ref, _rsum(dy * xn), first)
        dxn = dy * g_ref[...]
        dx2 = r * (dxn - xn * _mean(dxn * xn))
        dx2_ref[...] = dx2
        dy2_ref[...] = (dx2 * ga).astype(BF16)
        _acc(dga_ref, _rsum(dx2 * y2), first)

    vec = jax.ShapeDtypeStruct((1, D_MODEL), F32)
    return pl.pallas_call(
        body,
        out_shape=(jax.ShapeDtypeStruct((8, 128), F32), jax.ShapeDtypeStruct((SEQ, D_MODEL), F32),
                   jax.ShapeDtypeStruct((SEQ, D_MODEL), BF16), vec, vec),
        grid=(SEQ // ROWS,),
        in_specs=[_row_spec(act.shape[1]), pl.BlockSpec(w.shape, lambda i: (0, 0)), _row_spec(D_MODEL), _row_spec(D_MODEL),
                  _vec_spec(D_MODEL), _vec_spec(D_MODEL, 8)],
        out_specs=(pl.BlockSpec((8, 128), lambda i: (0, 0)), _row_spec(D_MODEL), _row_spec(D_MODEL),
                   _vec_spec(D_MODEL), _vec_spec(D_MODEL)),
        name=name, compiler_params=_cp("arbitrary"))(act, w, x1, tgt, g, mod)


def matmul_rms_mod_bwd(a, b, x, dres, g, mod, sc_row, y, ga_row, name, b_rows=None, after=None):
    gated = y is not None
    pieces = isinstance(a, tuple)
    halves = not pieces and a.ndim == 3
    if pieces:
        a1, a3 = a
        b_arg = b
        a_spec = _row_spec(a1.shape[1])
        b_spec = pl.BlockSpec((N_DEV, b_rows, D_MODEL), lambda i: (0, 0, 0))
    elif halves:
        k2 = a.shape[2]
        b_arg = b.reshape(2, k2, D_MODEL)
        a_spec = pl.BlockSpec((2, ROWS, k2), lambda i: (0, i, 0))
        b_spec = pl.BlockSpec((2, k2, D_MODEL), lambda i: (0, 0, 0))
    else:
        b_arg = b
        a_spec = _row_spec(a.shape[1])
        b_spec = pl.BlockSpec((N_DEV, b_rows, D_MODEL), lambda i: (0, 0, 0))

    def body(*refs):
        if pieces:
            a3_ref, refs = refs[1], refs[:1] + refs[2:]
        if gated:
            a_ref, b_ref, x_ref, dres_ref, g_ref, mod_ref, y_ref, dx_ref, dsh_ref, dsc_ref, dg_ref, dy_ref, dga_ref = refs
        else:
            a_ref, b_ref, x_ref, dres_ref, g_ref, mod_ref, dx_ref, dsh_ref, dsc_ref, dg_ref = refs
        first = pl.program_id(0) == 0
        if pieces:
            wv = b_ref[...].reshape(N_DEV * b_rows, D_MODEL)
            k1, k3 = a_ref.shape[1], a3_ref.shape[2]
            dh = jnp.dot(a_ref[...], wv[0:k1], preferred_element_type=F32)
            for t in range(3):
                dh = dh + jnp.dot(a3_ref[t], wv[k1 + t * k3:k1 + (t + 1) * k3], preferred_element_type=F32)
        elif halves:
            dh = (jnp.dot(a_ref[0], b_ref[0], preferred_element_type=F32)
                  + jnp.dot(a_ref[1], b_ref[1], preferred_element_type=F32))
        else:
            dh = jnp.dot(a_ref[...], b_ref[...].reshape(N_DEV * b_rows, D_MODEL), preferred_element_type=F32)
        xx = x_ref[...]
        gg = g_ref[...]
        r = lax.rsqrt(_mean(xx * xx) + EPS)
        xn = xx * r
        _acc(dsh_ref, _rsum(dh), first)
        _acc(dsc_ref, _rsum(dh * (xn * gg)), first)
        dt = dh * (1.0 + mod_ref[sc_row:sc_row + 1, :])
        _acc(dg_ref, _rsum(dt * xn), first)
        dxn = dt * gg
        dx = dres_ref[...] + r * (dxn - xn * _mean(dxn * xn))
        dx_ref[...] = dx
        if gated:
            _acc(dga_ref, _rsum(dx * y_ref[...]), first)
            dy_ref[...] = (dx * mod_ref[ga_row:ga_row + 1, :]).astype(BF16)

    vec = jax.ShapeDtypeStruct((1, D_MODEL), F32)
    in_specs = [a_spec, b_spec, _row_spec(D_MODEL), _row_spec(D_MODEL), _vec_spec(D_MODEL), _vec_spec(D_MODEL, 8)]
    out_shape = [jax.ShapeDtypeStruct((SEQ, D_MODEL), F32), vec, vec, vec]
    out_specs = [_row_spec(D_MODEL), _vec_spec(D_MODEL), _vec_spec(D_MODEL), _vec_spec(D_MODEL)]
    args = [a, b_arg, x, dres, g, mod]
    if pieces:
        in_specs.insert(1, pl.BlockSpec((3, ROWS, a3.shape[2]), lambda i: (0, i, 0)))
        args[0:1] = [a1, a3]
    if gated:
        in_specs.append(_row_spec(D_MODEL))
        out_shape += [jax.ShapeDtypeStruct((SEQ, D_MODEL), BF16), vec]
        out_specs += [_row_spec(D_MODEL), _vec_spec(D_MODEL)]
        args.append(y)
    body, more_specs, more = _with_after(body, len(args), after)
    return pl.pallas_call(
        body, out_shape=tuple(out_shape), grid=(SEQ // ROWS,), in_specs=in_specs + more_specs, out_specs=tuple(out_specs),
        name=name, compiler_params=_cp("arbitrary"))(*args, *more)


def _prev_halo(halo, width, col):
    per = ROWS // halo
    return pl.BlockSpec((halo, width), lambda i: (jnp.maximum(i * per - 1, 0), col))


def _next_halo(halo, width, col):
    per = ROWS // halo
    last = SEQ // halo - 1
    return pl.BlockSpec((halo, width), lambda i: (jnp.minimum((i + 1) * per, last), col))


CONV_PAD = ROWS + CONV_HALO


def _shift_copies(sh):
    for b in range(1, 8):
        sh[b, 0:CONV_PAD - 8, :] = sh[0, pl.ds(b, CONV_PAD - 8), :]


def _tap(sh, rs_start, offset):
    return sh[offset % 8, pl.ds(pl.multiple_of(rs_start + (offset // 8) * 8, 8), SUB), :]


def _conv_glu(av_ref, ag_ref, avh_ref, agh_ref, sh):
    i = pl.program_id(0)
    hv = avh_ref[...] * _sig(agh_ref[...])
    sh[0, 0:CONV_HALO, :] = jnp.where(i > 0, hv, 0.0)

    def glu(rs):
        sh[0, pl.ds(pl.multiple_of(rs.start + CONV_HALO, SUB), SUB), :] = av_ref[rs, :] * _sig(ag_ref[rs, :])

    _for_chunks(glu)
    _shift_copies(sh)


def _conv_norm(u1, lg_ref, lb_ref):
    mu = _mean(u1)
    cen = u1 - mu
    rs = lax.rsqrt(_mean(cen * cen) + EPS)
    z = cen * rs
    ln = z * lg_ref[...] + lb_ref[...]
    s = _sig(ln)
    return z, rs, ln, s, ln * s


def conv_module_fwd(proj, wc, bc, lg, lb, gco, name):
    def body(av_ref, ag_ref, avh_ref, agh_ref, wc_ref, bc_ref, lg_ref, lb_ref, gco_ref, out_ref, u1_ref, sh):
        _conv_glu(av_ref, ag_ref, avh_ref, agh_ref, sh)

        def conv(rs):
            u1 = jnp.broadcast_to(bc_ref[...], (SUB, D_CONV))
            for j in range(CONV_K):
                u1 = u1 + wc_ref[j:j + 1, :] * _tap(sh, rs.start, CONV_HALO - (CONV_K - 1) + j)
            u1_ref[rs, :] = u1

        _for_chunks(conv)
        _, _, _, _, u2 = _conv_norm(u1_ref[...], lg_ref, lb_ref)
        rc = lax.rsqrt(_mean(u2 * u2) + EPS)
        out_ref[...] = (u2 * rc * gco_ref[...]).astype(BF16)

    v = _vec_spec(D_CONV)
    return pl.pallas_call(
        body, out_shape=(jax.ShapeDtypeStruct((SEQ, D_CONV), BF16), jax.ShapeDtypeStruct((SEQ, D_CONV), F32)),
        grid=(SEQ // ROWS,),
        in_specs=[_row_spec(D_CONV, 0), _row_spec(D_CONV, 1), _prev_halo(CONV_HALO, D_CONV, 0),
                  _prev_halo(CONV_HALO, D_CONV, 1), _vec_spec(D_CONV, CONV_K), v, v, v, v],
        out_specs=(_row_spec(D_CONV), _row_spec(D_CONV)), scratch_shapes=[pltpu.VMEM((8, CONV_PAD, D_CONV), F32)],
        name=name, compiler_params=_cp("parallel"))(proj, proj, proj, proj, wc, bc, lg, lb, gco)


def conv_module_bwd_a(proj, u1, dmixed, lg, lb, gco, name):
    def body(av_ref, ag_ref, avh_ref, agh_ref, u1_ref, dm_ref, lg_ref, lb_ref, gco_ref,
             du1_ref, dgco_ref, dlg_ref, dlb_ref, dbc_ref, dwc_ref, sh, acc):
        first = pl.program_id(0) == 0
        _conv_glu(av_ref, ag_ref, avh_ref, agh_ref, sh)
        z, rs, ln, s, u2 = _conv_norm(u1_ref[...], lg_ref, lb_ref)
        rc = lax.rsqrt(_mean(u2 * u2) + EPS)
        xn = u2 * rc
        dm = dm_ref[...]
        _acc(dgco_ref, _rsum(dm * xn), first)
        dyn = dm * gco_ref[...]
        du2 = rc * (dyn - xn * _mean(dyn * xn))
        dln = du2 * (s * (1.0 + ln * (1.0 - s)))
        _acc(dlg_ref, _rsum(dln * z), first)
        _acc(dlb_ref, _rsum(dln), first)
        dz = dln * lg_ref[...]
        du1 = rs * (dz - _mean(dz) - z * _mean(dz * z))
        du1_ref[...] = du1
        _acc(dbc_ref, _rsum(du1), first)
        acc[...] = jnp.zeros_like(acc)

        def taps(rs):
            d = du1_ref[rs, :]
            for j in range(CONV_K):
                acc[j] += d * _tap(sh, rs.start, CONV_HALO - (CONV_K - 1) + j)

        _for_chunks(taps)

        @pl.when(first)
        def _():
            dwc_ref[...] = jnp.zeros_like(dwc_ref)

        for j in range(CONV_K):
            dwc_ref[j:j + 1, :] += _rsum(acc[j])

    v = _vec_spec(D_CONV)
    vec = jax.ShapeDtypeStruct((1, D_CONV), F32)
    return pl.pallas_call(
        body,
        out_shape=(jax.ShapeDtypeStruct((SEQ, D_CONV), F32), vec, vec, vec, vec, jax.ShapeDtypeStruct((CONV_K, D_CONV), F32)),
        grid=(SEQ // ROWS,),
        in_specs=[_row_spec(D_CONV, 0), _row_spec(D_CONV, 1), _prev_halo(CONV_HALO, D_CONV, 0),
                  _prev_halo(CONV_HALO, D_CONV, 1), _row_spec(D_CONV, 0), _row_spec(D_CONV, 0), v, v, v],
        out_specs=(_row_spec(D_CONV), v, v, v, v, _vec_spec(D_CONV, CONV_K)),
        scratch_shapes=[pltpu.VMEM((8, CONV_PAD, D_CONV), F32), pltpu.VMEM((CONV_K, SUB, D_CONV), F32)],
        name=name, compiler_params=_cp("arbitrary"))(proj, proj, proj, proj, u1, dmixed, lg, lb, gco)


def conv_module_bwd_b(proj, du1, wc, name):
    def body(av_ref, ag_ref, du1_ref, du1n_ref, wc_ref, out_ref, sh):
        i = pl.program_id(0)
        sh[0, 0:ROWS, :] = du1_ref[...]
        sh[0, ROWS:, :] = jnp.where(i < SEQ // ROWS - 1, du1n_ref[...], 0.0)
        _shift_copies(sh)

        def chunk(rs):
            du0 = jnp.zeros((SUB, D_CONV), F32)
            for j in range(CONV_K):
                du0 = du0 + wc_ref[j:j + 1, :] * _tap(sh, rs.start, CONV_K - 1 - j)
            sg = _sig(ag_ref[rs, :])
            out_ref[rs, 0:D_CONV] = (du0 * sg).astype(BF16)
            out_ref[rs, D_CONV:] = (du0 * av_ref[rs, :] * sg * (1.0 - sg)).astype(BF16)

        _for_chunks(chunk)

    return pl.pallas_call(
        body, out_shape=jax.ShapeDtypeStruct((SEQ, 2 * D_CONV), BF16), grid=(SEQ // ROWS,),
        in_specs=[_row_spec(D_CONV, 0), _row_spec(D_CONV, 1), _row_spec(D_CONV, 0), _next_halo(CONV_HALO, D_CONV, 0),
                  _vec_spec(D_CONV, CONV_K)],
        out_specs=_row_spec(2 * D_CONV), scratch_shapes=[pltpu.VMEM((8, CONV_PAD, D_CONV), F32)],
        name=name, compiler_params=_cp("parallel"))(proj, proj, du1, du1, wc)


def _rows(start, size, r):
    return pl.ds(start, size) if r == 1 else pl.ds(start, size, stride=r)


def _unit_rows(r, rho, n, nb):
    win = 2 * ATTN_BLOCK if nb > 1 else ATTN_BLOCK
    if isinstance(n, int):
        kb = max(n - 1, 0)
        q_rows = _rows(rho + r * ATTN_BLOCK * n, ATTN_BLOCK, r)
        k_rows = _rows(rho + r * ATTN_BLOCK * kb, win, r)
    else:
        kb = jnp.maximum(n - 1, 0)
        q_rows = pl.ds(pl.multiple_of(n * ATTN_BLOCK, ATTN_BLOCK), ATTN_BLOCK)
        k_rows = pl.ds(pl.multiple_of(kb * ATTN_BLOCK, ATTN_BLOCK), win)
    dist = (n - kb) * ATTN_BLOCK + lax.broadcasted_iota(jnp.int32, (ATTN_BLOCK, win), 0) \
        - lax.broadcasted_iota(jnp.int32, (ATTN_BLOCK, win), 1)
    return q_rows, k_rows, (dist >= 0) & (dist <= ATTN_BLOCK)


def _per_head(x):
    lane = lax.broadcasted_iota(jnp.int32, x.shape, 1)
    zero = jnp.zeros_like(x)
    return [jnp.where(lane < HEAD_DIM, x, zero), jnp.where(lane >= HEAD_DIM, x, zero)]


def _masked_scores(q2, k2, valid):
    return [jnp.where(valid, lax.dot_general(qh, k2, NT, preferred_element_type=F32) * (HEAD_DIM ** -0.5), NEG)
            for qh in _per_head(q2)]


def _attn_units(r, nb, unit):
    if r == 1:
        def four(i, carry):
            for k in range(4):
                unit(0, 4 * i + k)
            return carry
        lax.fori_loop(0, nb // 4, four, 0)
    else:
        for rho in range(r):
            for n in range(nb):
                unit(rho, n)


N_UNITS = 16


def attn_fwd_all(proj, name):
    def body(q_ref, k_ref, v_ref, att_ref, lse_ref, s_scr, p_scr, lse_scr, den_scr):
        for idx, (sub_len, r) in enumerate(PATTERNS):
            nb = sub_len // ATTN_BLOCK
            win = 2 * ATTN_BLOCK if nb > 1 else ATTN_BLOCK

            def scores(rho, n, r=r, nb=nb, win=win):
                u = rho * nb + n
                q_rows, k_rows, valid = _unit_rows(r, rho, n, nb)
                ss = _masked_scores(q_ref[q_rows, :].astype(BF16), k_ref[k_rows, :].astype(BF16), valid)
                for h in range(2):
                    s_scr[2 * u + h, :, 0:win] = ss[h]

            _attn_units(r, nb, scores)

            def softmax(u, carry, win=win):
                lses, dens = [], []
                for h in range(2):
                    sc = s_scr[2 * u + h, :, 0:win]
                    m = jnp.max(sc, axis=1, keepdims=True)
                    p = jnp.exp(sc - m)
                    den = jnp.sum(p, axis=1, keepdims=True)
                    p_scr[2 * u + h, :, 0:win] = p.astype(BF16)
                    lses.append(jnp.broadcast_to(m + jnp.log(den), (ATTN_BLOCK, HEAD_DIM)))
                    dens.append(jnp.broadcast_to(den, (ATTN_BLOCK, HEAD_DIM)))
                lse_scr[u] = jnp.concatenate(lses, axis=1)
                den_scr[u] = jnp.concatenate(dens, axis=1)
                return carry

            lax.fori_loop(0, N_UNITS, softmax, 0, unroll=2)

            def outputs(rho, n, r=r, nb=nb, win=win, idx=idx):
                u = rho * nb + n
                q_rows, k_rows, _ = _unit_rows(r, rho, n, nb)
                vs = _per_head(v_ref[k_rows, :].astype(BF16))
                o = (jnp.dot(p_scr[2 * u, :, 0:win], vs[0], preferred_element_type=F32)
                     + jnp.dot(p_scr[2 * u + 1, :, 0:win], vs[1], preferred_element_type=F32)) / den_scr[u]
                lse = lse_scr[u]
                if idx > 0:
                    old = lse_ref[q_rows, :]
                    top = jnp.maximum(old, lse)
                    new = top + jnp.log(jnp.exp(old - top) + jnp.exp(lse - top))
                    o = att_ref[q_rows, :] * jnp.exp(old - new) + o * jnp.exp(lse - new)
                    lse = new
                att_ref[q_rows, :] = o
                lse_ref[q_rows, :] = lse

            _attn_units(r, nb, outputs)

    blk = lambda first: pl.BlockSpec((SEQ, 128), lambda g: (0, first + g))
    shp = jax.ShapeDtypeStruct((SEQ, D_ATTN), F32)
    big = (2 * N_UNITS, ATTN_BLOCK, 2 * ATTN_BLOCK)
    small = pltpu.VMEM((N_UNITS, ATTN_BLOCK, 128), F32)
    return pl.pallas_call(
        body, out_shape=(shp, shp), grid=(4,), in_specs=[blk(8), blk(12), blk(16)], out_specs=(blk(0), blk(0)),
        scratch_shapes=[pltpu.VMEM(big, F32), pltpu.VMEM(big, BF16), small, small],
        name=name, compiler_params=_cp("parallel"))(proj, proj, proj)


def attn_bwd_all(proj, do, lse, dd, name):
    scale = HEAD_DIM ** -0.5

    def body(q_ref, k_ref, v_ref, do_ref, l_ref, dd_ref, out_ref, dq_s, dk_s, dv_s,
             s_scr, dp_scr, ds_scr, st_scr, dpt_scr, pt_scr, dst_scr, qb_scr, kb_scr, dob_scr):
        dq_s[...] = jnp.zeros_like(dq_s)
        dk_s[...] = jnp.zeros_like(dk_s)
        dv_s[...] = jnp.zeros_like(dv_s)
        for sub_len, r in PATTERNS:
            nb = sub_len // ATTN_BLOCK
            win = 2 * ATTN_BLOCK if nb > 1 else ATTN_BLOCK

            def scores(rho, n, r=r, nb=nb, win=win):
                u = rho * nb + n
                q_rows, k_rows, valid = _unit_rows(r, rho, n, nb)
                kb = max(n - 1, 0) if isinstance(n, int) else jnp.maximum(n - 1, 0)
                dist_t = (n - kb) * ATTN_BLOCK + lax.broadcasted_iota(jnp.int32, (win, ATTN_BLOCK), 1) \
                    - lax.broadcasted_iota(jnp.int32, (win, ATTN_BLOCK), 0)
                valid_t = (dist_t >= 0) & (dist_t <= ATTN_BLOCK)
                q2 = q_ref[q_rows, :].astype(BF16)
                k2 = k_ref[k_rows, :].astype(BF16)
                do2 = do_ref[q_rows, :].astype(BF16)
                qb_scr[u] = q2
                kb_scr[u, 0:win, :] = k2
                dob_scr[u] = do2
                l2 = l_ref[q_rows, :]
                d2 = dd_ref[q_rows, :]
                l2t = l2.T
                d2t = d2.T
                v2 = v_ref[k_rows, :].astype(BF16)
                qs, dos = _per_head(q2), _per_head(do2)
                for h in range(2):
                    c0 = h * HEAD_DIM
                    sc = lax.dot_general(qs[h], k2, NT, preferred_element_type=F32) * scale
                    s_scr[2 * u + h, :, 0:win] = jnp.where(valid, sc, NEG) - l2[:, c0:c0 + 1]
                    dp_scr[2 * u + h, :, 0:win] = lax.dot_general(dos[h], v2, NT, preferred_element_type=F32) \
                        - d2[:, c0:c0 + 1]
                    sct = lax.dot_general(k2, qs[h], NT, preferred_element_type=F32) * scale
                    st_scr[2 * u + h, 0:win, :] = jnp.where(valid_t, sct, NEG) - l2t[c0:c0 + 1, :]
                    dpt_scr[2 * u + h, 0:win, :] = lax.dot_general(v2, dos[h], NT, preferred_element_type=F32) \
                        - d2t[c0:c0 + 1, :]

            _attn_units(r, nb, scores)

            def pointwise(hu, carry, win=win):
                ds_scr[hu, :, 0:win] = (jnp.exp(s_scr[hu, :, 0:win]) * dp_scr[hu, :, 0:win] * scale).astype(BF16)
                pt = jnp.exp(st_scr[hu, 0:win, :])
                pt_scr[hu, 0:win, :] = pt.astype(BF16)
                dst_scr[hu, 0:win, :] = (pt * dpt_scr[hu, 0:win, :] * scale).astype(BF16)
                return carry

            lax.fori_loop(0, 2 * N_UNITS, pointwise, 0, unroll=4)

            def grads(rho, n, r=r, nb=nb, win=win):
                u = rho * nb + n
                q_rows, k_rows, _ = _unit_rows(r, rho, n, nb)
                qs, ks, dos = _per_head(qb_scr[u]), _per_head(kb_scr[u, 0:win, :]), _per_head(dob_scr[u])

                def both(scr, rows, rhs):
                    return (jnp.dot(scr[(2 * u,) + rows], rhs[0], preferred_element_type=F32)
                            + jnp.dot(scr[(2 * u + 1,) + rows], rhs[1], preferred_element_type=F32))

                dq_s[q_rows, :] += both(ds_scr, (slice(None), slice(0, win)), ks)
                dk_s[k_rows, :] += both(dst_scr, (slice(0, win), slice(None)), qs)
                dv_s[k_rows, :] += both(pt_scr, (slice(0, win), slice(None)), dos)

            _attn_units(r, nb, grads)
        out_ref[0] = dq_s[...].astype(BF16)
        out_ref[1] = dk_s[...].astype(BF16)
        out_ref[2] = dv_s[...].astype(BF16)

    blk = lambda first: pl.BlockSpec((SEQ, 128), lambda g: (0, first + g))
    acc = pltpu.VMEM((SEQ, 128), F32)
    big = (2 * N_UNITS, ATTN_BLOCK, 2 * ATTN_BLOCK)
    big_t = (2 * N_UNITS, 2 * ATTN_BLOCK, ATTN_BLOCK)
    return pl.pallas_call(
        body, out_shape=jax.ShapeDtypeStruct((3, SEQ, D_ATTN), BF16), grid=(4,),
        in_specs=[blk(8), blk(12), blk(16), blk(0), blk(0), blk(0)],
        out_specs=pl.BlockSpec((3, SEQ, 128), lambda g: (0, 0, g)),
        scratch_shapes=[acc, acc, acc, pltpu.VMEM(big, F32), pltpu.VMEM(big, F32), pltpu.VMEM(big, BF16),
                        pltpu.VMEM(big_t, F32), pltpu.VMEM(big_t, F32), pltpu.VMEM(big_t, BF16), pltpu.VMEM(big_t, BF16),
                        pltpu.VMEM((N_UNITS, ATTN_BLOCK, 128), BF16),
                        pltpu.VMEM((N_UNITS, 2 * ATTN_BLOCK, 128), BF16), pltpu.VMEM((N_UNITS, ATTN_BLOCK, 128), BF16)],
        name=name, compiler_params=_cp("parallel"))(proj, proj, proj, do, lse, dd)


N_FT = D_FF // FFN_TN


def _ffn_specs():
    per = ROWS // FFN_HALO
    cur_g = pl.BlockSpec((ROWS, FFN_TN), lambda j, i: (i, j))
    cur_v = pl.BlockSpec((ROWS, FFN_TN), lambda j, i: (i, j + N_FT))
    halo_g = pl.BlockSpec((FFN_HALO, FFN_TN), lambda j, i: (jnp.maximum(i * per - 1, 0), j))
    halo_v = pl.BlockSpec((FFN_HALO, FFN_TN), lambda j, i: (jnp.maximum(i * per - 1, 0), j + N_FT))
    w_g = pl.BlockSpec((FFN_K, FFN_TN), lambda j, i: (0, j))
    w_v = pl.BlockSpec((FFN_K, FFN_TN), lambda j, i: (0, j + N_FT))
    b_g = pl.BlockSpec((1, FFN_TN), lambda j, i: (0, j))
    b_v = pl.BlockSpec((1, FFN_TN), lambda j, i: (0, j + N_FT))
    return [cur_g, cur_v, halo_g, halo_v, w_g, w_v, b_g, b_v]


def matmul(a, b, kind, out_dtype, tm, tn, name, b_rows=None):
    stacked = b_rows is not None
    b_shape = (N_DEV * b_rows, b.shape[2]) if stacked else b.shape
    if kind == "nn":
        (m, k), n = a.shape, b_shape[1]
        a_spec = pl.BlockSpec((tm, k), lambda j, i: (i, 0))
        b_spec = pl.BlockSpec((k, tn), lambda j, i: (0, j))
        dims = (((1,), (0,)), ((), ()))
    elif kind == "nt":
        (m, k), n = a.shape, b_shape[0]
        a_spec = pl.BlockSpec((tm, k), lambda j, i: (i, 0))
        b_spec = pl.BlockSpec((tn, k), lambda j, i: (j, 0))
        dims = (((1,), (1,)), ((), ()))
    else:
        (k, m), n = a.shape, b_shape[1]
        a_spec = pl.BlockSpec((k, tm), lambda j, i: (0, i))
        b_spec = pl.BlockSpec((k, tn), lambda j, i: (0, j))
        dims = (((0,), (0,)), ((), ()))
    assert m % tm == 0 and n % tn == 0, (name, m, n, tm, tn)
    if stacked:
        assert b_spec.block_shape[0] == b_shape[0] and kind in ("nn", "nt")
        width = b_spec.block_shape[1]
        b_spec = pl.BlockSpec((N_DEV, b_rows, width), (lambda j, i: (0, 0, j)) if kind == "nn" else (lambda j, i: (0, 0, 0)))

    def body(a_ref, b_ref, o_ref):
        bb = b_ref[...].reshape(b_shape[0], -1) if stacked else b_ref[...]
        o_ref[...] = lax.dot_general(a_ref[...], bb, dims, preferred_element_type=F32).astype(o_ref.dtype)

    return pl.pallas_call(
        body, out_shape=jax.ShapeDtypeStruct((m, n), out_dtype), grid=(n // tn, m // tm),
        in_specs=[a_spec, b_spec], out_specs=pl.BlockSpec((tm, tn), lambda j, i: (i, j)),
        name=name, compiler_params=_cp("parallel", "parallel"))(a, b)


def matmul_tn_pieces(a1, a3, b, name):
    k, n = b.shape
    tm = a3.shape[2]
    n1 = a1.shape[1] // tm

    def body(a1_ref, a3_ref, b_ref, o_ref):
        i = pl.program_id(0)
        tn_dims = (((0,), (0,)), ((), ()))

        @pl.when(i < n1)
        def _():
            o_ref[...] = lax.dot_general(a1_ref[...], b_ref[...], tn_dims, preferred_element_type=F32).astype(BF16)

        @pl.when(i >= n1)
        def _():
            o_ref[...] = lax.dot_general(a3_ref[0], b_ref[...], tn_dims, preferred_element_type=F32).astype(BF16)

    return pl.pallas_call(
        body, out_shape=jax.ShapeDtypeStruct((a1.shape[1] + 3 * tm, n), BF16), grid=(n1 + 3,),
        in_specs=[pl.BlockSpec((k, tm), lambda i: (0, jnp.minimum(i, n1 - 1))),
                  pl.BlockSpec((1, k, tm), lambda i: (jnp.maximum(i - n1, 0), 0, 0)),
                  pl.BlockSpec((k, n), lambda i: (0, 0))],
        out_specs=pl.BlockSpec((tm, n), lambda i: (i, 0)), name=name, compiler_params=_cp("parallel"))(a1, a3, b)


def matmul_tn_halves(a, b, tm, name):
    _, k, m = a.shape
    n = b.shape[1]

    def body(a_ref, b_ref, o_ref):
        o_ref[0] = lax.dot_general(a_ref[0], b_ref[...], (((0,), (0,)), ((), ())), preferred_element_type=F32).astype(BF16)

    return pl.pallas_call(
        body, out_shape=jax.ShapeDtypeStruct((2, m, n), BF16), grid=(2, m // tm),
        in_specs=[pl.BlockSpec((1, k, tm), lambda h, i: (h, 0, i)), pl.BlockSpec((k, n), lambda h, i: (0, 0))],
        out_specs=pl.BlockSpec((1, tm, n), lambda h, i: (h, i, 0)), name=name,
        compiler_params=_cp("parallel", "parallel"))(a, b).reshape(2 * m, n)


def _row_spec(width, col=0):
    return pl.BlockSpec((ROWS, width), lambda i: (i, col))


def _vec_spec(width, rows=1):
    return pl.BlockSpec((rows, width), lambda i: (0, 0))


def _ffn_shifted(cur_ref, halo_ref, pad, s1, s2):
    i = pl.program_id(1)
    pad[0:FFN_HALO, :] = jnp.where(i > 0, halo_ref[...], 0.0)
    pad[FFN_HALO:, :] = cur_ref[0:FFN_HALO, :]
    for k, dst in ((1, s1), (2, s2)):
        dst[0:FFN_HALO, :] = pad[pl.ds(FFN_HALO - k, FFN_HALO), :]
        dst[FFN_HALO:, :] = cur_ref[pl.ds(FFN_HALO - k, ROWS - FFN_HALO), :]


def _ffn_conv(rs, cur_ref, s1, s2, w_ref, b_ref):
    return b_ref[...] + w_ref[0:1, :] * s2[rs, :] + w_ref[1:2, :] * s1[rs, :] + w_ref[2:3, :] * cur_ref[rs, :]


def ffn_act_fwd(up0, wf, bf, name):
    def body(g_ref, v_ref, gh_ref, vh_ref, wg_ref, wv_ref, bg_ref, bv_ref, act_ref, pad, g1, g2, v1, v2):
        _ffn_shifted(g_ref, gh_ref, pad, g1, g2)
        _ffn_shifted(v_ref, vh_ref, pad, v1, v2)

        def chunk(rs):
            gate = _ffn_conv(rs, g_ref, g1, g2, wg_ref, bg_ref)
            val = _ffn_conv(rs, v_ref, v1, v2, wv_ref, bv_ref)
            act_ref[rs, :] = (gate * _sig(gate) * val).astype(BF16)

        _for_chunks(chunk)

    tile = pltpu.VMEM((ROWS, FFN_TN), F32)
    return pl.pallas_call(
        body, out_shape=jax.ShapeDtypeStruct((SEQ, D_FF), BF16), grid=(N_FT, SEQ // ROWS),
        in_specs=_ffn_specs(), out_specs=pl.BlockSpec((ROWS, FFN_TN), lambda j, i: (i, j)),
        scratch_shapes=[pltpu.VMEM((2 * FFN_HALO, FFN_TN), F32), tile, tile, tile, tile],
        name=name, compiler_params=_cp("parallel", "parallel"))(up0, up0, up0, up0, wf, wf, bf, bf)


def ffn_bwd(up0, dact, wf, bf, name, after=None):
    per = ROWS // FFN_HALO
    last = SEQ // FFN_HALO - 1

    def body(g_ref, v_ref, gh_ref, vh_ref, wg_ref, wv_ref, bg_ref, bv_ref, da_ref, gn_ref, vn_ref, dan_ref,
             out_ref, dbg_ref, dbv_ref, dwg_ref, dwv_ref, pad, g1, g2, v1, v2, dgp, dvp, acc):
        i = pl.program_id(1)
        first = i == 0
        _ffn_shifted(g_ref, gh_ref, pad, g1, g2)
        _ffn_shifted(v_ref, vh_ref, pad, v1, v2)
        acc[...] = jnp.zeros_like(acc)

        def grads(gate, val, da):
            s = _sig(gate)
            return da * val * (s * (1.0 + gate * (1.0 - s))), da * (gate * s)

        def chunk(rs):
            gate = _ffn_conv(rs, g_ref, g1, g2, wg_ref, bg_ref)
            val = _ffn_conv(rs, v_ref, v1, v2, wv_ref, bv_ref)
            dgate, dval = grads(gate, val, da_ref[rs, :])
            dgp[rs, :] = dgate
            dvp[rs, :] = dval
            acc[0] += dgate
            acc[1] += dval
            for t, (sg, sv) in enumerate(((g2, v2), (g1, v1), (g_ref, v_ref))):
                acc[2 + t] += dgate * sg[rs, :]
                acc[5 + t] += dval * sv[rs, :]

        _for_chunks(chunk)
        _acc(dbg_ref, _rsum(acc[0]), first)
        _acc(dbv_ref, _rsum(acc[1]), first)
        _acc(dwg_ref, jnp.concatenate([_rsum(acc[2 + t]) for t in range(FFN_K)], axis=0), first)
        _acc(dwv_ref, jnp.concatenate([_rsum(acc[5 + t]) for t in range(FFN_K)], axis=0), first)

        def conv_next(cur_ref, nxt_ref, w_ref, b_ref):
            pad[0:FFN_HALO, :] = cur_ref[ROWS - FFN_HALO:, :]
            pad[FFN_HALO:, :] = nxt_ref[...]
            return (b_ref[...] + w_ref[0:1, :] * pad[pl.ds(FFN_HALO - 2, FFN_HALO), :]
                    + w_ref[1:2, :] * pad[pl.ds(FFN_HALO - 1, FFN_HALO), :] + w_ref[2:3, :] * nxt_ref[...])

        gate_n = conv_next(g_ref, gn_ref, wg_ref, bg_ref)
        val_n = conv_next(v_ref, vn_ref, wv_ref, bv_ref)
        dgate_n, dval_n = grads(gate_n, val_n, dan_ref[...])
        inside = i < SEQ // ROWS - 1
        dgp[ROWS:, :] = jnp.where(inside, dgate_n, 0.0)
        dvp[ROWS:, :] = jnp.where(inside, dval_n, 0.0)

        for half, (dp, s1, s2, w_ref) in enumerate(((dgp, g1, g2, wg_ref), (dvp, v1, v2, wv_ref))):
            s1[...] = dp[pl.ds(1, ROWS), :]
            s2[...] = dp[pl.ds(2, ROWS), :]

            def back(rs, dp=dp, s1=s1, s2=s2, w_ref=w_ref, half=half):
                out_ref[half, rs, :] = (w_ref[2:3, :] * dp[rs, :] + w_ref[1:2, :] * s1[rs, :]
                                        + w_ref[0:1, :] * s2[rs, :]).astype(BF16)

            _for_chunks(back)

    body, more_specs, more = _with_after(body, 12, after)
    tile = pltpu.VMEM((ROWS, FFN_TN), F32)
    ext = pltpu.VMEM((ROWS + FFN_HALO, FFN_TN), F32)
    vec = jax.ShapeDtypeStruct((1, D_FF), F32)
    taps = jax.ShapeDtypeStruct((FFN_K, D_FF), F32)
    cur = pl.BlockSpec((ROWS, FFN_TN), lambda j, i: (i, j))
    nxt = lambda off: pl.BlockSpec((FFN_HALO, FFN_TN), lambda j, i: (jnp.minimum((i + 1) * per, last), j + off))
    vs = pl.BlockSpec((1, FFN_TN), lambda j, i: (0, j))
    ts = pl.BlockSpec((FFN_K, FFN_TN), lambda j, i: (0, j))
    return pl.pallas_call(
        body, out_shape=(jax.ShapeDtypeStruct((2, SEQ, D_FF), BF16), vec, vec, taps, taps), grid=(N_FT, SEQ // ROWS),
        in_specs=_ffn_specs() + [cur, nxt(0), nxt(N_FT), nxt(0)] + more_specs,
        out_specs=(pl.BlockSpec((2, ROWS, FFN_TN), lambda j, i: (0, i, j)), vs, vs, ts, ts),
        scratch_shapes=[pltpu.VMEM((2 * FFN_HALO, FFN_TN), F32), tile, tile, tile, tile, ext, ext,
                        pltpu.VMEM((2 + 2 * FFN_K, SUB, FFN_TN), F32)],
        name=name, compiler_params=_cp("parallel", "arbitrary"))(up0, up0, up0, up0, wf, wf, bf, bf, dact, up0, up0, dact,
                                                                 *more)


def ada_fwd(c_all, w_ada, b_cols, name):
    def body(c_ref, w_ref, b_ref, o_ref):
        cc = c_ref[...]
        sc = (cc * _sig(cc)).astype(BF16)
        o_ref[...] = jnp.dot(sc, w_ref[...].astype(BF16), preferred_element_type=F32) + b_ref[...]

    return pl.pallas_call(body, out_shape=jax.ShapeDtypeStruct((N_DEV, w_ada.shape[1]), F32), name=name,
                          compiler_params=_cp())(c_all, w_ada, b_cols)


def _adam(w, g, m, v):
    m = ADAM_B1 * m + (1.0 - ADAM_B1) * g
    v = ADAM_B2 * v + (1.0 - ADAM_B2) * (g * g)
    m_hat = m / (1.0 - ADAM_B1 ** ADAM_STEP)
    v_hat = v / (1.0 - ADAM_B2 ** ADAM_STEP)
    delta = -ADAM_LR * (m_hat / (jnp.sqrt(v_hat) + ADAM_EPS) + ADAM_WD * w)
    return delta, m, v


def ada_bwd_adamw(c_all, dmod_cols, w, m, v, name):
    rows, cols = w.shape
    tr = 256

    def body(c_ref, dm_ref, w_ref, m_ref, v_ref, g_ref, d_ref, nm_ref, nv_ref):
        cc = c_ref[...]
        sc = (cc * _sig(cc)).T
        g = sc[:, 0:1] * dm_ref[0:1, :]
        for b in range(1, N_DEV):
            g = g + sc[:, b:b + 1] * dm_ref[b:b + 1, :]
        g_ref[...] = g
        d_ref[...], nm_ref[...], nv_ref[...] = _adam(w_ref[...], g, m_ref[...], v_ref[...])

    blk = pl.BlockSpec((tr, cols), lambda i: (i, 0))
    shp = jax.ShapeDtypeStruct((rows, cols), F32)
    return pl.pallas_call(
        body, out_shape=(shp, shp, shp, shp), grid=(rows // tr,),
        in_specs=[pl.BlockSpec((N_DEV, tr), lambda i: (0, i)), pl.BlockSpec((N_DEV, cols), lambda i: (0, 0)), blk, blk, blk],
        out_specs=(blk, blk, blk, blk), name=name, compiler_params=_cp("parallel"))(c_all, dmod_cols, w, m, v)


def sum_adamw(parts, mine, me, w, m, v, tr, name):
    n_parts, rows, cols = parts.shape

    def body(me_ref, p_ref, own_ref, w_ref, m_ref, v_ref, g_ref, d_ref, nm_ref, nv_ref):
        def chunk(rs):
            g = own_ref[0, rs, :].astype(F32)
            for k in range(1, n_parts):
                g = g + p_ref[k, rs, :].astype(F32)
            g_ref[rs, :] = g
            d_ref[rs, :], nm_ref[rs, :], nv_ref[rs, :] = _adam(w_ref[rs, :], g, m_ref[rs, :], v_ref[rs, :])

        _for_chunks(chunk, 2, tr)

    blk = pl.BlockSpec((tr, cols), lambda i, me_ref: (i, 0))
    shp = jax.ShapeDtypeStruct((rows, cols), F32)
    grid_spec = pltpu.PrefetchScalarGridSpec(
        num_scalar_prefetch=1, grid=(rows // tr,),
        in_specs=[pl.BlockSpec((n_parts, tr, cols), lambda i, me_ref: (0, i, 0)),
                  pl.BlockSpec((1, tr, cols), lambda i, me_ref: (me_ref[0], i, 0)), blk, blk, blk],
        out_specs=(blk, blk, blk, blk))
    return pl.pallas_call(body, out_shape=(shp, shp, shp, shp), grid_spec=grid_spec, name=name,
                          compiler_params=_cp("parallel"))(me, parts, mine, w, m, v)


MESH = pl.DeviceIdType.MESH


def all_gather(block, name, after=None):
    extra = () if after is None else (after,)

    def body(x_ref, *refs):
        out_ref, send_sems, recv_sems, local_sem = refs[len(extra):]
        x, y, c = lax.axis_index("x"), lax.axis_index("y"), lax.axis_index("c")
        me, sibling = (x, y, c), (x, y, 1 - c)
        chips = [(1 - x, y), (x, 1 - y), (1 - x, 1 - y)]

        def slot(px, py, pc):
            return out_ref.at[4 * px + 2 * py + pc]

        def copy(k, blk, to, src=None):
            return pltpu.make_async_remote_copy(
                src_ref=slot(*blk) if src is None else src, dst_ref=slot(*blk),
                send_sem=send_sems.at[k], recv_sem=recv_sems.at[k], device_id=to, device_id_type=MESH)

        mine = pltpu.make_async_copy(x_ref, slot(*me), local_sem)
        mine.start()
        first = [copy(0, me, sibling, src=x_ref)]
        first += [copy(1 + j, me, (*chip, c), src=x_ref) for j, chip in enumerate(chips)]
        for cp in first:
            cp.start()
        passed = [copy(4 + j, (*chip, c), sibling) for j, chip in enumerate(chips)]
        for j, chip in enumerate(chips):
            copy(1 + j, (*chip, c), me).wait_recv()
            passed[j].start()
        copy(0, sibling, me).wait_recv()
        for j, chip in enumerate(chips):
            copy(4 + j, (*chip, 1 - c), me).wait_recv()
        for cp in first + passed:
            cp.wait_send()
        mine.wait()

    return pl.pallas_call(
        body, out_shape=jax.ShapeDtypeStruct((N_DEV,) + block.shape, block.dtype), in_specs=[ANY] * (1 + len(extra)), out_specs=ANY,
        scratch_shapes=[pltpu.SemaphoreType.DMA((7,)), pltpu.SemaphoreType.DMA((7,)), pltpu.SemaphoreType.DMA],
        name=name)(block, *extra)


HBM = pl.BlockSpec(memory_space=pltpu.HBM)
SEM = pl.BlockSpec(memory_space=pltpu.SEMAPHORE)
EFFECT = pltpu.SideEffectType.DATAFLOW_SIDE_EFFECTING


def _peer_copies(src_ref, land_ref, send_sems, recv_sems, gather):
    x, y, c = lax.axis_index("x"), lax.axis_index("y"), lax.axis_index("c")
    me = 4 * x + 2 * y + c
    copies = []
    for k in range(1, N_DEV):
        px = 1 - x if k & 4 else x
        py = 1 - y if k & 2 else y
        pc = 1 - c if k & 1 else c
        copies.append(pltpu.make_async_remote_copy(
            src_ref=src_ref if gather else src_ref.at[4 * px + 2 * py + pc],
            dst_ref=land_ref.at[me] if gather else land_ref.at[k],
            send_sem=send_sems.at[k - 1], recv_sem=recv_sems.at[k - 1], device_id=(px, py, pc), device_id_type=MESH))
    return copies


def exchange_start(srcs, gather, name, after=None):
    n = len(srcs)
    land_shapes = [(N_DEV,) + src.shape if gather else src.shape for src in srcs]
    extra = () if after is None else (after,)

    def body(*refs):
        src_refs, land_refs = refs[0:n], refs[n:2 * n]
        outs = refs[2 * n + len(extra):]
        for k in range(n):
            for cp in _peer_copies(src_refs[k], land_refs[k], outs[4 * k], outs[4 * k + 1], gather):
                cp.start()
        token = outs[4 * n]
        token[...] = jnp.zeros_like(token)

    out_shape, out_specs, aliases = [], [], {}
    for k, src in enumerate(srcs):
        out_shape += [pltpu.SemaphoreType.DMA((N_DEV - 1,)), pltpu.SemaphoreType.DMA((N_DEV - 1,)),
                      pltpu.HBM(src.shape, src.dtype), pltpu.HBM(land_shapes[k], src.dtype)]
        out_specs += [SEM, SEM, HBM, HBM]
        aliases[k] = 4 * k + 2
        aliases[n + k] = 4 * k + 3
    out_shape.append(jax.ShapeDtypeStruct((8, 128), F32))
    out_specs.append(pl.BlockSpec(memory_space=pltpu.VMEM))
    res = pl.pallas_call(
        body, name=name, out_shape=tuple(out_shape), in_specs=(HBM,) * (2 * n) + (ANY,) * len(extra),
        out_specs=tuple(out_specs), input_output_aliases=aliases,
        compiler_params=pltpu.CompilerParams(has_side_effects=EFFECT),
    )(*[pltpu.with_memory_space_constraint(src, pltpu.HBM) for src in srcs],
      *[pltpu.with_memory_space_constraint(lax.empty(shp, src.dtype), pltpu.HBM) for shp, src in zip(land_shapes, srcs)],
      *extra)
    return [tuple(res[4 * k:4 * k + 4]) for k in range(n)], res[4 * n]


def _stage1_peer(i):
    x, y, c = lax.axis_index("x"), lax.axis_index("y"), lax.axis_index("c")
    if i == 0:
        return (x, y, 1 - c)
    return (1 - x if i & 1 else x, 1 - y if i & 2 else y, c)


def _slot_of(peer):
    return 4 * peer[0] + 2 * peer[1] + peer[2]


def _stage1_copy(i, src_ref, land_ref, send_sems, recv_sems):
    me = _slot_of((lax.axis_index("x"), lax.axis_index("y"), lax.axis_index("c")))
    return pltpu.make_async_remote_copy(src_ref=src_ref, dst_ref=land_ref.at[me], send_sem=send_sems.at[i],
                                        recv_sem=recv_sems.at[i], device_id=_stage1_peer(i), device_id_type=MESH)


def _stage2_copy(j, land_ref, send_sems, recv_sems):
    slot = _slot_of(_stage1_peer(j + 1))
    return pltpu.make_async_remote_copy(src_ref=land_ref.at[slot], dst_ref=land_ref.at[slot], send_sem=send_sems.at[j],
                                        recv_sem=recv_sems.at[j], device_id=_stage1_peer(0), device_id_type=MESH)


def gather2_start(srcs, name, after=None):
    n = len(srcs)
    extra = () if after is None else (after,)

    def body(*refs):
        src_refs, land_refs = refs[0:n], refs[n:2 * n]
        outs = refs[2 * n + len(extra):]
        for k in range(n):
            for i in range(4):
                _stage1_copy(i, src_refs[k], land_refs[k], outs[4 * k], outs[4 * k + 1]).start()
        outs[4 * n][...] = jnp.zeros((8, 128), F32)

    out_shape, out_specs, aliases = [], [], {}
    for k, src in enumerate(srcs):
        out_shape += [pltpu.SemaphoreType.DMA((4,)), pltpu.SemaphoreType.DMA((4,)),
                      pltpu.HBM(src.shape, src.dtype), pltpu.HBM((N_DEV,) + src.shape, src.dtype)]
        out_specs += [SEM, SEM, HBM, HBM]
        aliases[k] = 4 * k + 2
        aliases[n + k] = 4 * k + 3
    out_shape.append(jax.ShapeDtypeStruct((8, 128), F32))
    out_specs.append(pl.BlockSpec(memory_space=pltpu.VMEM))
    res = pl.pallas_call(
        body, name=name, out_shape=tuple(out_shape), in_specs=(HBM,) * (2 * n) + (ANY,) * len(extra),
        out_specs=tuple(out_specs), input_output_aliases=aliases,
        compiler_params=pltpu.CompilerParams(has_side_effects=EFFECT),
    )(*[pltpu.with_memory_space_constraint(src, pltpu.HBM) for src in srcs],
      *[pltpu.with_memory_space_constraint(lax.empty((N_DEV,) + src.shape, src.dtype), pltpu.HBM) for src in srcs], *extra)
    return [dict(send1=res[4 * k], recv1=res[4 * k + 1], src=res[4 * k + 2], land=res[4 * k + 3]) for k in range(n)], \
        res[4 * n]


def gather2_pass(handles, after, name):
    n = len(handles)

    def body(*refs):
        src_refs, land_refs, recv1 = refs[0:n], refs[n:2 * n], refs[2 * n:3 * n]
        outs = refs[3 * n + 1:]
        for k in range(n):
            for j in range(3):
                _stage1_copy(j + 1, src_refs[k], land_refs[k], recv1[k], recv1[k]).wait_recv()
                _stage2_copy(j, land_refs[k], outs[3 * k], outs[3 * k + 1]).start()

    out_shape, out_specs, aliases = [], [], {}
    for k, h in enumerate(handles):
        out_shape += [pltpu.SemaphoreType.DMA((3,)), pltpu.SemaphoreType.DMA((3,)), pltpu.HBM(h["land"].shape, h["land"].dtype)]
        out_specs += [SEM, SEM, HBM]
        aliases[n + k] = 3 * k + 2
    res = pl.pallas_call(
        body, name=name, out_shape=tuple(out_shape), in_specs=(HBM,) * (2 * n) + (SEM,) * n + (ANY,),
        out_specs=tuple(out_specs), input_output_aliases=aliases,
        compiler_params=pltpu.CompilerParams(has_side_effects=EFFECT),
    )(*[h["src"] for h in handles], *[h["land"] for h in handles], *[h["recv1"] for h in handles], after)
    return [dict(h, send2=res[3 * k], recv2=res[3 * k + 1], land=res[3 * k + 2]) for k, h in enumerate(handles)]


def gather2_wait(h, after, name):
    def body(src_ref, land_ref, send1, recv1, send2, recv2, after_ref, src_dead, got_ref):
        for i in range(4):
            _stage1_copy(i, src_ref, land_ref, send1, recv1).wait_send()
        _stage1_copy(0, src_ref, land_ref, send1, recv1).wait_recv()
        for j in range(3):
            cp = _stage2_copy(j, land_ref, send2, recv2)
            cp.wait_send()
            cp.wait_recv()

    return pl.pallas_call(
        body, name=name,
        out_shape=(pltpu.HBM(h["src"].shape, h["src"].dtype), pltpu.HBM(h["land"].shape, h["land"].dtype)),
        in_specs=(HBM, HBM, SEM, SEM, SEM, SEM, ANY), out_specs=(HBM, HBM), input_output_aliases={0: 0, 1: 1},
        compiler_params=pltpu.CompilerParams(has_side_effects=EFFECT),
    )(h["src"], h["land"], h["send1"], h["recv1"], h["send2"], h["recv2"], after)[1]


def exchange_wait(handles, after, gather, name):
    send_sems, recv_sems, src_thru, land_thru = handles

    def body(src_ref, land_ref, send_sems, recv_sems, after_ref, src_dead, got_ref):
        for cp in _peer_copies(src_ref, land_ref, send_sems, recv_sems, gather):
            cp.wait_send()
            cp.wait_recv()

    return pl.pallas_call(
        body, name=name,
        out_shape=(pltpu.HBM(src_thru.shape, src_thru.dtype), pltpu.HBM(land_thru.shape, land_thru.dtype)),
        in_specs=(HBM, HBM, SEM, SEM, ANY), out_specs=(HBM, HBM), input_output_aliases={0: 0, 1: 1},
        compiler_params=pltpu.CompilerParams(has_side_effects=EFFECT),
    )(src_thru, land_thru, send_sems, recv_sems, after)


def local_step(x, tgt, mod, started, get_w, put_grad, wc, wf, g_mix, bc, lg, lb, gco, gao, g_ffn, bf, g_fin):
    w_in = get_w("w_in", mod)
    proj, h1 = rms_mod_matmul(x, g_mix, mod, 0, 1, w_in, D_IN // N_DEV, "proj_fwd", after=started)
    mix_a, u1 = conv_module_fwd(proj, wc, bc, lg, lb, gco, "conv_module_fwd")
    att, lse = attn_fwd_all(proj, "attn_fwd")
    w_out = get_w("w_out", att)
    y1, mixed = norm_concat_matmul(mix_a, att, gao, w_out, "out_proj_fwd")
    w_up = get_w("w_up", y1)
    up0, x1, h2 = resid_rms_mod_matmul(x, y1, g_ffn, mod, 2, 3, 4, w_up, "up_fwd")
    act = ffn_act_fwd(up0, wf, bf, "ffn_act_fwd")
    w_down = get_w("w_down", act)
    loss_t, dx2, dy2, d_gfin, d_gaf = matmul_loss_bwd(act, w_down, x1, tgt, g_fin, mod, 5, "down_fwd_loss")
    dact = matmul(dy2, w_down, "nt", F32, 512, FFN_TN, "down_bwd_x")
    dw_down = matmul(act, dy2, "tn", BF16, 256, D_MODEL, "down_bwd_w")
    dup0, dbf_g, dbf_v, dwf_g, dwf_v = ffn_bwd(up0, dact, wf, bf, "ffn_bwd", after=put_grad("w_down", dw_down))
    dw_up = matmul_tn_halves(dup0, h2, 256, "up_bwd_w")
    dx1, d_shf, d_scf, d_gffn, dy1, d_gam = matmul_rms_mod_bwd(
        dup0, w_up, x1, dx2, g_ffn, mod, 4, y1, 2, "up_bwd_x", after=put_grad("w_up", dw_up))
    dw_out = matmul(mixed, dy1, "tn", BF16, 256, D_MODEL, "out_proj_bwd_w")
    dmixed, do, dd, d_gao = matmul_combine_bwd(dy1, w_out, att, gao, "out_proj_bwd_x", after=put_grad("w_out", dw_out))
    dqkv = attn_bwd_all(proj, do, lse, dd, "attn_bwd")
    du1, d_gco, d_lg, d_lb, d_bc, d_wc = conv_module_bwd_a(proj, u1, dmixed, lg, lb, gco, "conv_module_bwd_a")
    dproj_a = conv_module_bwd_b(proj, du1, wc, "conv_module_bwd_b")
    dw_in = matmul_tn_pieces(dproj_a, dqkv, h1, "proj_bwd_w")
    dx, d_shm, d_scm, d_gmix = matmul_rms_mod_bwd(
        (dproj_a, dqkv), w_in, x, dx1, g_mix, mod, 1, None, 0, "proj_bwd_x", b_rows=D_IN // N_DEV,
        after=put_grad("w_in", dw_in))
    dmod = jnp.concatenate([d_shm, d_scm, d_gam, d_shf, d_scf, d_gaf], axis=1)
    small = dict(g_norm_mix=d_gmix, b_conv_dw=d_bc, ln_conv_g=d_lg, ln_conv_b=d_lb, g_conv_out=d_gco, g_attn_out=d_gao,
                 g_norm_ffn=d_gffn, b_ffn_dw=jnp.concatenate([dbf_g, dbf_v], axis=1), g_final=d_gfin,
                 w_conv_dw=d_wc, w_ffn_dw=jnp.concatenate([dwf_g, dwf_v], axis=1), dmod=dmod, loss=loss_t[0:1, 0:1])
    return dx, small


def _padw(a, width):
    return jnp.pad(a, ((0, 0), (0, width - a.shape[1])))


def pack_small(t):
    wide = jnp.concatenate([_padw(t["dmod"], PACK_W), _padw(t["b_ffn_dw"], PACK_W), _padw(t["w_ffn_dw"], PACK_W),
                            _padw(t["loss"], PACK_W), jnp.zeros((2, PACK_W), F32)], axis=0)
    z512 = jnp.zeros((1, 512), F32)
    narrow = jnp.concatenate([
        t["g_norm_mix"], t["g_norm_ffn"], t["g_final"],
        jnp.concatenate([t["b_conv_dw"], t["ln_conv_g"]], axis=1),
        jnp.concatenate([t["ln_conv_b"], t["g_conv_out"]], axis=1),
        jnp.concatenate([t["g_attn_out"], z512], axis=1),
        jnp.zeros((2, 1024), F32),
        jnp.pad(t["w_conv_dw"], ((0, 1), (0, 0))).reshape(16, 1024)], axis=0)
    return jnp.concatenate([wide, narrow.reshape(4, PACK_W), jnp.zeros((4, PACK_W), F32)], axis=0)


_NARROW = lambda k, off=0: (8 + k // 6, (k % 6) * 1024 + off)
PACKED_AT = dict(
    b_ada=(0, 0, N_MOD * D_MODEL), b_ffn_dw=(1, 0, 2 * D_FF),
    g_norm_mix=_NARROW(0) + (D_MODEL,), g_norm_ffn=_NARROW(1) + (D_MODEL,), g_final=_NARROW(2) + (D_MODEL,),
    b_conv_dw=_NARROW(3) + (D_CONV,), ln_conv_g=_NARROW(3, 512) + (D_CONV,),
    ln_conv_b=_NARROW(4) + (D_CONV,), g_conv_out=_NARROW(4, 512) + (D_CONV,), g_attn_out=_NARROW(5) + (D_ATTN,))
SMALL_ORDER = list(PACKED_AT)


def small_adamw(parts, wmv, name):
    def body(*refs):
        p_ref = refs[0]
        ins = refs[1:1 + 3 * len(SMALL_ORDER)]
        outs = refs[1 + 3 * len(SMALL_ORDER):]
        g = p_ref[0]
        for k in range(1, N_DEV):
            g = g + p_ref[k]
        for i, n in enumerate(SMALL_ORDER):
            row, lane, width = PACKED_AT[n]
            gp = g[row:row + 1, lane:lane + width]
            w_ref, m_ref, v_ref = ins[3 * i:3 * i + 3]
            g_ref, d_ref, nm_ref, nv_ref = outs[4 * i:4 * i + 4]
            g_ref[...] = gp
            d_ref[...], nm_ref[...], nv_ref[...] = _adam(w_ref[...], gp, m_ref[...], v_ref[...])
        wc_ref, wf_ref, loss_ref = outs[4 * len(SMALL_ORDER):]
        for j in range(CONV_K):
            row, lane = _NARROW(8 + j // 2, (j % 2) * 512)
            wc_ref[j:j + 1, :] = g[row:row + 1, lane:lane + D_CONV]
        wf_ref[...] = g[2:2 + FFN_K, 0:2 * D_FF]
        loss_ref[...] = jnp.broadcast_to(g[5:6, 0:1], (8, 128))

    args, out_shape = [parts], []
    for n in SMALL_ORDER:
        args += list(wmv[n])
        out_shape += [jax.ShapeDtypeStruct(wmv[n][0].shape, F32)] * 4
    out_shape += [jax.ShapeDtypeStruct((CONV_K, D_CONV), F32), jax.ShapeDtypeStruct((FFN_K, 2 * D_FF), F32),
                  jax.ShapeDtypeStruct((8, 128), F32)]
    res = pl.pallas_call(body, out_shape=tuple(out_shape), name=name, compiler_params=_cp())(*args)
    per = {n: tuple(res[4 * i:4 * i + 4]) for i, n in enumerate(SMALL_ORDER)}
    return per, res[-3], res[-2], res[-1][0, 0]


def shard_adamw(items, name):
    def body(*refs):
        ins, outs = refs[:4 * len(items)], refs[4 * len(items):]
        for i in range(len(items)):
            g_ref, w_ref, m_ref, v_ref = ins[4 * i:4 * i + 4]
            og_ref, d_ref, nm_ref, nv_ref = outs[4 * i:4 * i + 4]
            og_ref[...] = g_ref[...]
            d_ref[...], nm_ref[...], nv_ref[...] = _adam(w_ref[...], g_ref[...], m_ref[...], v_ref[...])

    args = [a for item in items for a in item]
    out_shape = tuple(jax.ShapeDtypeStruct(item[1].shape, F32) for item in items for _ in range(4))
    res = pl.pallas_call(body, out_shape=out_shape, name=name, compiler_params=_cp())(*args)
    return [tuple(res[4 * i:4 * i + 4]) for i in range(len(items))]


def _shard(full, n_cols, me):
    return lax.dynamic_slice(full, (0, me * n_cols), (full.shape[0], n_cols))


WEIGHTS = ["w_ada", "b_ada", "g_norm_mix", "w_in", "w_conv_dw", "b_conv_dw", "ln_conv_g", "ln_conv_b", "g_conv_out",
           "g_attn_out", "w_out", "g_norm_ffn", "w_up", "w_ffn_dw", "b_ffn_dw", "w_down", "g_final"]


def kernel(x, c, w_ada, b_ada, g_norm_mix, w_in, w_conv_dw, b_conv_dw, ln_conv_g, ln_conv_b, g_conv_out, g_attn_out, w_out, g_norm_ffn, w_up, w_ffn_dw, b_ffn_dw, w_down, g_final, loss_target, m_w_ada, m_b_ada, m_g_norm_mix, m_w_in, m_w_conv_dw, m_b_conv_dw, m_ln_conv_g, m_ln_conv_b, m_g_conv_out, m_g_attn_out, m_w_out, m_g_norm_ffn, m_w_up, m_w_ffn_dw, m_b_ffn_dw, m_w_down, m_g_final, v_w_ada, v_b_ada, v_g_norm_mix, v_w_in, v_w_conv_dw, v_b_conv_dw, v_ln_conv_g, v_ln_conv_b, v_g_conv_out, v_g_attn_out, v_w_out, v_g_norm_ffn, v_w_up, v_w_ffn_dw, v_b_ffn_dw, v_w_down, v_g_final):
    args = dict(locals())
    me = 4 * lax.axis_index("x") + 2 * lax.axis_index("y") + lax.axis_index("c")
    me1 = me.astype(jnp.int32).reshape(1)

    def flat(name, prefix=""):
        a = args[prefix + name]
        return a.reshape(a.shape[-2] if a.ndim > 1 else 1, a.shape[-1])

    def flat_t(name, prefix=""):
        return args[prefix + name][0].T

    n_in, n_up, r_out, r_down = w_in.shape[2], w_up.shape[2], w_out.shape[1], w_down.shape[1]
    n_ada, n_wc, n_wf = w_ada.shape[2], w_conv_dw.shape[2], w_ffn_dw.shape[2]
    taps_c = jnp.pad(flat("w_conv_dw").reshape(1, CONV_K * n_wc), ((0, 0), (0, 2 * D_MODEL - CONV_K * n_wc)))
    taps_f = jnp.pad(flat("w_ffn_dw").reshape(1, FFN_K * n_wf), ((0, 0), (0, 3 * D_MODEL - FFN_K * n_wf)))
    first = jnp.concatenate([c, taps_c.reshape(2, D_MODEL), taps_f.reshape(3, D_MODEL), jnp.zeros((2, D_MODEL), F32)], axis=0)
    w_in_block = flat_t("w_in").astype(BF16)
    hi = lax.reduce_precision(first, 8, 7)
    mid = lax.reduce_precision(first - hi, 8, 7)
    low = lax.reduce_precision(first - hi - mid, 8, 7)
    terms = jnp.concatenate([hi, mid, low, jnp.zeros((8, D_MODEL), F32)], axis=0).astype(BF16)
    first_block = all_gather(jnp.concatenate([w_in_block, terms], axis=0), "gather_c_taps_w_in")
    terms = first_block[:, n_in:n_in + 24, :].astype(F32)
    first_all = (terms[:, 0:8] + terms[:, 8:16]) + terms[:, 16:24]
    c_all = first_all[:, 0, :]
    wc_full = first_all[:, 1:3, :].reshape(N_DEV, 2 * D_MODEL)[:, :CONV_K * n_wc].reshape(N_DEV, CONV_K, n_wc)
    wc_full = wc_full.transpose(1, 0, 2).reshape(CONV_K, D_CONV)
    wf_full = first_all[:, 3:6, :].reshape(N_DEV, 3 * D_MODEL)[:, :FFN_K * n_wf].reshape(N_DEV, FFN_K, n_wf)
    wf_full = wf_full.transpose(1, 0, 2).reshape(FFN_K, 2 * D_FF)
    mod_cols = ada_fwd(c_all, flat("w_ada"), _shard(flat("b_ada"), n_ada, me), "ada_fwd")
    mod_all = all_gather(mod_cols, "gather_mod")
    mod = lax.dynamic_index_in_dim(mod_all, me, axis=1, keepdims=False).reshape(N_MOD, D_MODEL)
    mod = jnp.pad(mod, ((0, 2), (0, 0)))

    order = ("w_out", "w_up", "w_down")
    blocks = dict(w_up=flat_t("w_up").astype(BF16), w_out=flat("w_out").astype(BF16), w_down=flat("w_down").astype(BF16))
    handles, tok = gather2_start([blocks[name] for name in order], "gather_weights_start", mod_all)
    gathers = dict(zip(order, handles))

    def gathered(name, after):
        if "send2" not in gathers[name]:
            group = ("w_out", "w_up") if name != "w_down" else ("w_down",)
            gathers.update(zip(group, gather2_pass([gathers[w] for w in group], after, f"gather_{name}_pass")))
        land = gather2_wait(gathers[name], after, f"gather_{name}_wait")
        return lax.dynamic_update_index_in_dim(land, blocks[name], me, axis=0)

    def get_w(name, after):
        if name == "w_in":
            return first_block
        return gathered(name, after).reshape(-1, D_MODEL)

    exchanges = {}

    def put_grad(name, dw, after=None):
        dev_major = dw.reshape(N_DEV, -1, D_MODEL)
        (exchanges[name],), token = exchange_start([dev_major], False, f"exchange_{name}_start", after)
        return token

    grad_x, small = local_step(
        x[0], loss_target[0], mod, tok, get_w, put_grad, wc_full, wf_full,
        flat("g_norm_mix"), flat("b_conv_dw"), flat("ln_conv_g"), flat("ln_conv_b"), flat("g_conv_out"),
        flat("g_attn_out"), flat("g_norm_ffn"), flat("b_ffn_dw"), flat("g_final"))

    out = {}

    def finish(name, tr, after):
        mine, parts = exchange_wait(exchanges[name], after, False, f"exchange_{name}_wait")
        if name in ("w_in", "w_up"):
            res = sum_adamw(parts, mine, me1, flat_t(name), flat_t(name, "m_"), flat_t(name, "v_"), tr, "adamw_" + name)
            out[name] = tuple(r.T for r in res)
        else:
            res = out[name] = sum_adamw(parts, mine, me1, flat(name), flat(name, "m_"), flat(name, "v_"), tr,
                                        "adamw_" + name)
        return res[0]

    after = finish("w_down", r_down, grad_x)
    after = finish("w_up", n_up // 2, after)
    after = finish("w_out", r_out, after)
    after = finish("w_in", n_in, after)

    small_all = all_gather(pack_small(small), "gather_small", after)

    wmv = {n: (flat(n), flat(n, "m_"), flat(n, "v_")) for n in SMALL_ORDER}
    per, g_wc, g_wf, loss = small_adamw(small_all, wmv, "adamw_small")
    out.update(per)
    taps = shard_adamw([(_shard(g_wc, n_wc, me), flat("w_conv_dw"), flat("w_conv_dw", "m_"), flat("w_conv_dw", "v_")),
                        (_shard(g_wf, n_wf, me), flat("w_ffn_dw"), flat("w_ffn_dw", "m_"), flat("w_ffn_dw", "v_"))],
                       "adamw_taps")
    out["w_conv_dw"], out["w_ffn_dw"] = taps

    dmod_cols = _shard(small_all[:, 0, :], n_ada, me)
    out["w_ada"] = ada_bwd_adamw(c_all, dmod_cols, flat("w_ada"), flat("w_ada", "m_"), flat("w_ada", "v_"), "adamw_w_ada")

    result = [loss, grad_x[None]]
    for k in range(4):
        result += [out[n][k].reshape(args[n].shape) for n in WEIGHTS]
    return tuple(result)
```

```python
import functools

import jax
import jax.numpy as jnp
from jax import lax
from jax.experimental import pallas as pl
from jax.experimental.pallas import tpu as pltpu

F32 = jnp.float32
BF16 = jnp.bfloat16

N_DEV = 8
SEQ = 2048
D_MODEL = 1024
D_CONV = 512
D_ATTN = 512
HEAD_DIM = 64
CONV_K = 31
D_FF = 2816
FFN_K = 3
D_IN = 2 * D_CONV + 3 * D_ATTN
N_MOD = 6
EPS = 1e-6
ATTN_BLOCK = 128
PATTERNS = ((2048, 1), (512, 4), (128, 16))
NEG = -1e30

ADAM_LR, ADAM_B1, ADAM_B2, ADAM_EPS, ADAM_WD, ADAM_STEP = 0.001, 0.9, 0.999, 1e-08, 0.01, 10

ROWS = 256
CONV_HALO = 32
FFN_HALO = 8
FFN_TN = 1408
VMEM_LIMIT = 56 * 1024 * 1024
PACK_W = 6144


NT = (((1,), (1,)), ((), ()))
ANY = pl.BlockSpec(memory_space=pl.ANY)


def _cp(*sem):
    return pltpu.CompilerParams(dimension_semantics=sem if sem else None, vmem_limit_bytes=VMEM_LIMIT)


def _with_after(body, n_in, after):
    if after is None:
        return body, [], []
    return (lambda *refs: body(*refs[:n_in], *refs[n_in + 1:])), [ANY], [after]


def _sig(x):
    return 1.0 / (1.0 + jnp.exp(-x))


def _rsum(x):
    return jnp.sum(x, axis=0, keepdims=True)


def _mean(x):
    return jnp.mean(x, axis=-1, keepdims=True)


def _acc(ref, val, first):
    @pl.when(first)
    def _():
        ref[...] = val

    @pl.when(jnp.logical_not(first))
    def _():
        ref[...] += val


SUB = 16


def _for_chunks(fn, unroll=1, rows=ROWS):
    def step(i, carry):
        fn(pl.ds(pl.multiple_of(i * SUB, SUB), SUB))
        return carry

    lax.fori_loop(0, rows // SUB, step, 0, unroll=unroll)


PAIR = 2 * ROWS


def _pair_spec(width, col=0):
    return pl.BlockSpec((PAIR, width), lambda i: (i, col))


def _halves():
    return [slice(h * ROWS, (h + 1) * ROWS) for h in range(2)]


def rms_mod_matmul(x, g, mod, sh_row, sc_row, b, b_rows, name, after=None):
    n = N_DEV * b_rows

    def body(x_ref, g_ref, mod_ref, b_ref, o_ref, h_ref):
        for rs in _halves():
            xx = x_ref[rs, :]
            r = lax.rsqrt(_mean(xx * xx) + EPS)
            h = (xx * r * g_ref[...] * (1.0 + mod_ref[sc_row:sc_row + 1, :]) + mod_ref[sh_row:sh_row + 1, :]).astype(BF16)
            h_ref[rs, :] = h
            o_ref[rs, :] = lax.dot_general(h, b_ref[...].reshape(n, D_MODEL), NT, preferred_element_type=F32)

    body, more_specs, more = _with_after(body, 4, after)
    return pl.pallas_call(
        body, out_shape=(jax.ShapeDtypeStruct((SEQ, n), F32), jax.ShapeDtypeStruct((SEQ, D_MODEL), BF16)),
        grid=(SEQ // PAIR,),
        in_specs=[_pair_spec(D_MODEL), _vec_spec(D_MODEL), _vec_spec(D_MODEL, 8),
                  pl.BlockSpec((N_DEV, b_rows, D_MODEL), lambda i: (0, 0, 0))] + more_specs,
        out_specs=(_pair_spec(n), _pair_spec(D_MODEL)), name=name, compiler_params=_cp("parallel"))(x, g, mod, b, *more)


def norm_concat_matmul(mix_a, att, gao, w, name):
    def body(a_ref, att_ref, g_ref, w_ref, y_ref, mixed_ref):
        for rs in _halves():
            aa = att_ref[rs, :]
            mixed_ref[rs, 0:D_CONV] = a_ref[rs, :]
            mixed_ref[rs, D_CONV:] = (aa * lax.rsqrt(_mean(aa * aa) + EPS) * g_ref[...]).astype(BF16)
            y_ref[rs, :] = jnp.dot(mixed_ref[rs, :], w_ref[...], preferred_element_type=F32)

    return pl.pallas_call(
        body, out_shape=(jax.ShapeDtypeStruct((SEQ, D_MODEL), F32), jax.ShapeDtypeStruct((SEQ, D_MODEL), BF16)),
        grid=(SEQ // PAIR,),
        in_specs=[_pair_spec(D_CONV), _pair_spec(D_ATTN), _vec_spec(D_ATTN), pl.BlockSpec(w.shape, lambda i: (0, 0))],
        out_specs=(_pair_spec(D_MODEL), _pair_spec(D_MODEL)), name=name, compiler_params=_cp("parallel"))(mix_a, att, gao, w)


def resid_rms_mod_matmul(x, y, g, mod, ga_row, sh_row, sc_row, w, name):
    n = w.shape[0]

    def body(x_ref, y_ref, g_ref, mod_ref, w_ref, o_ref, x1_ref, h_ref):
        x1 = x_ref[...] + mod_ref[ga_row:ga_row + 1, :] * y_ref[...]
        x1_ref[...] = x1
        r = lax.rsqrt(_mean(x1 * x1) + EPS)
        h = (x1 * r * g_ref[...] * (1.0 + mod_ref[sc_row:sc_row + 1, :]) + mod_ref[sh_row:sh_row + 1, :]).astype(BF16)
        h_ref[...] = h
        o_ref[...] = lax.dot_general(h, w_ref[...], NT, preferred_element_type=F32)

    return pl.pallas_call(
        body,
        out_shape=(jax.ShapeDtypeStruct((SEQ, n), F32), jax.ShapeDtypeStruct((SEQ, D_MODEL), F32),
                   jax.ShapeDtypeStruct((SEQ, D_MODEL), BF16)),
        grid=(SEQ // ROWS,),
        in_specs=[_row_spec(D_MODEL), _row_spec(D_MODEL), _vec_spec(D_MODEL), _vec_spec(D_MODEL, 8),
                  pl.BlockSpec(w.shape, lambda i: (0, 0))],
        out_specs=(_row_spec(n), _row_spec(D_MODEL), _row_spec(D_MODEL)),
        name=name, compiler_params=_cp("parallel"))(x, y, g, mod, w)


def matmul_combine_bwd(dy, w, att, gao, name, after=None):
    def body(dy_ref, w_ref, att_ref, g_ref, dm_ref, do_ref, dd_ref, dg_ref):
        @pl.when(pl.program_id(0) == 0)
        def _():
            dg_ref[...] = jnp.zeros_like(dg_ref)

        same_head = (jnp.right_shift(lax.broadcasted_iota(jnp.int32, (D_ATTN, D_ATTN), 0), 6)
                     == jnp.right_shift(lax.broadcasted_iota(jnp.int32, (D_ATTN, D_ATTN), 1), 6)).astype(F32)
        for rs in _halves():
            dmixed = lax.dot_general(dy_ref[rs, :], w_ref[...], NT, preferred_element_type=F32)
            dm_ref[rs, :] = dmixed
            att = att_ref[rs, :]
            r = lax.rsqrt(_mean(att * att) + EPS)
            xn = att * r
            dm = dmixed[:, D_CONV:]
            dg_ref[...] += _rsum(dm * xn)
            dyn = dm * g_ref[...]
            do = r * (dyn - xn * _mean(dyn * xn))
            do_ref[rs, :] = do
            dd_ref[rs, :] = jnp.dot(do * att, same_head, preferred_element_type=F32, precision=lax.Precision.HIGHEST)

    rs = _pair_spec(D_ATTN)
    f = jax.ShapeDtypeStruct((SEQ, D_ATTN), F32)
    body, more_specs, more = _with_after(body, 4, after)
    return pl.pallas_call(
        body, out_shape=(jax.ShapeDtypeStruct((SEQ, D_MODEL), F32), f, f, jax.ShapeDtypeStruct((1, D_ATTN), F32)),
        grid=(SEQ // PAIR,),
        in_specs=[_pair_spec(D_MODEL), pl.BlockSpec(w.shape, lambda i: (0, 0)), rs, _vec_spec(D_ATTN)] + more_specs,
        out_specs=(_pair_spec(D_MODEL), rs, rs, _vec_spec(D_ATTN)),
        name=name, compiler_params=_cp("arbitrary"))(dy, w, att, gao, *more)


def matmul_loss_bwd(act, w, x1, tgt, g, mod, ga_row, name):
    def body(a_ref, w_ref, x1_ref, t_ref, g_ref, mod_ref, loss_ref, dx2_ref, dy2_ref, dg_ref, dga_ref):
        @pl.when(pl.program_id(0) == 0)
        def _():
            loss_ref[...] = jnp.zeros_like(loss_ref)
            dg_ref[...] = jnp.zeros_like(dg_ref)
            dga_ref[...] = jnp.zeros_like(dga_ref)

        ga = mod_ref[ga_row:ga_row + 1, :]
        for rs in _halves():
            y2 = jnp.dot(a_ref[rs, :], w_ref[...], preferred_element_type=F32)
            x2 = x1_ref[rs, :] + ga * y2
            r = lax.rsqrt(_mean(x2 * x2) + EPS)
            xn = x2 * r
            err = xn * g_ref[...] - t_ref[rs, :]
            loss_ref[...] += jnp.broadcast_to(0.5 * jnp.sum(_mean(err * err)), (8, 128))
            dy = err * (1.0 / D_MODEL)
            dg_ref[...] += _rsum(dy * xn)
            dxn = dy * g_ref[...]
            dx2 = r * (dxn - xn * _mean(dxn * xn))
            dx2_ref[rs, :] = dx2
            dy2_ref[rs, :] = (dx2 * ga).astype(BF16)
            dga_ref[...] += _rsum(dx2 * y2)

    vec = jax.ShapeDtypeStruct((1, D_MODEL), F32)
    rows = _pair_spec
    return pl.pallas_call(
        body,
        out_shape=(jax.ShapeDtypeStruct((8, 128), F32), jax.ShapeDtypeStruct((SEQ, D_MODEL), F32),
                   jax.ShapeDtypeStruct((SEQ, D_MODEL), BF16), vec, vec),
        grid=(SEQ // PAIR,),
        in_specs=[rows(act.shape[1]), pl.BlockSpec(w.shape, lambda i: (0, 0)), rows(D_MODEL), rows(D_MODEL),
                  _vec_spec(D_MODEL), _vec_spec(D_MODEL, 8)],
        out_specs=(pl.BlockSpec((8, 128), lambda i: (0, 0)), rows(D_MODEL), rows(D_MODEL),
                   _vec_spec(D_MODEL), _vec_spec(D_MODEL)),
        name=name, compiler_params=_cp("arbitrary"))(act, w, x1, tgt, g, mod)


def matmul_rms_mod_bwd(a, b, x, dres, g, mod, sc_row, y, ga_row, name, b_rows=None, after=None):
    gated = y is not None
    pieces = isinstance(a, tuple)
    tm = PAIR if pieces else ROWS
    blocks = [slice(h * ROWS, (h + 1) * ROWS) for h in range(tm // ROWS)]
    rows = lambda width: pl.BlockSpec((tm, width), lambda i: (i, 0))
    if pieces:
        a1, a3 = a
        a_args = [a1, a3]
        a_specs = [rows(a1.shape[1]), pl.BlockSpec((3, tm, a3.shape[2]), lambda i: (0, i, 0))]
        b_arg, b_spec = b, pl.BlockSpec((N_DEV, b_rows, D_MODEL), lambda i: (0, 0, 0))
    else:
        k2 = a.shape[2]
        a_args = [a]
        a_specs = [pl.BlockSpec((2, tm, k2), lambda i: (0, i, 0))]
        b_arg, b_spec = b.reshape(2, k2, D_MODEL), pl.BlockSpec((2, k2, D_MODEL), lambda i: (0, 0, 0))
    n_a = len(a_args)

    def body(*refs):
        a_refs, (b_ref, x_ref, dres_ref, g_ref, mod_ref) = refs[:n_a], refs[n_a:n_a + 5]
        if gated:
            y_ref, dx_ref, dsh_ref, dsc_ref, dg_ref, dy_ref, dga_ref = refs[n_a + 5:]
        else:
            dx_ref, dsh_ref, dsc_ref, dg_ref = refs[n_a + 5:]

        @pl.when(pl.program_id(0) == 0)
        def _():
            for ref in (dsh_ref, dsc_ref, dg_ref) + ((dga_ref,) if gated else ()):
                ref[...] = jnp.zeros_like(ref)

        gg = g_ref[...]
        for rs in blocks:
            if pieces:
                wv = b_ref[...].reshape(N_DEV * b_rows, D_MODEL)
                k1, k3 = a_refs[0].shape[1], a_refs[1].shape[2]
                dh = jnp.dot(a_refs[0][rs, :], wv[0:k1], preferred_element_type=F32)
                for t in range(3):
                    dh = dh + jnp.dot(a_refs[1][t, rs, :], wv[k1 + t * k3:k1 + (t + 1) * k3], preferred_element_type=F32)
            else:
                dh = (jnp.dot(a_refs[0][0, rs, :], b_ref[0], preferred_element_type=F32)
                      + jnp.dot(a_refs[0][1, rs, :], b_ref[1], preferred_element_type=F32))
            xx = x_ref[rs, :]
            r = lax.rsqrt(_mean(xx * xx) + EPS)
            xn = xx * r
            dsh_ref[...] += _rsum(dh)
            dsc_ref[...] += _rsum(dh * (xn * gg))
            dt = dh * (1.0 + mod_ref[sc_row:sc_row + 1, :])
            dg_ref[...] += _rsum(dt * xn)
            dxn = dt * gg
            dx = dres_ref[rs, :] + r * (dxn - xn * _mean(dxn * xn))
            dx_ref[rs, :] = dx
            if gated:
                dga_ref[...] += _rsum(dx * y_ref[rs, :])
                dy_ref[rs, :] = (dx * mod_ref[ga_row:ga_row + 1, :]).astype(BF16)

    vec = jax.ShapeDtypeStruct((1, D_MODEL), F32)
    in_specs = a_specs + [b_spec, rows(D_MODEL), rows(D_MODEL), _vec_spec(D_MODEL), _vec_spec(D_MODEL, 8)]
    out_shape = [jax.ShapeDtypeStruct((SEQ, D_MODEL), F32), vec, vec, vec]
    out_specs = [rows(D_MODEL), _vec_spec(D_MODEL), _vec_spec(D_MODEL), _vec_spec(D_MODEL)]
    args = a_args + [b_arg, x, dres, g, mod]
    if gated:
        in_specs.append(rows(D_MODEL))
        out_shape += [jax.ShapeDtypeStruct((SEQ, D_MODEL), BF16), vec]
        out_specs += [rows(D_MODEL), _vec_spec(D_MODEL)]
        args.append(y)
    body, more_specs, more = _with_after(body, len(args), after)
    return pl.pallas_call(
        body, out_shape=tuple(out_shape), grid=(SEQ // tm,), in_specs=in_specs + more_specs, out_specs=tuple(out_specs),
        name=name, compiler_params=_cp("arbitrary"))(*args, *more)


def _prev_halo(halo, width, col):
    per = ROWS // halo
    return pl.BlockSpec((halo, width), lambda i: (jnp.maximum(i * per - 1, 0), col))


def _next_halo(halo, width, col):
    per = ROWS // halo
    last = SEQ // halo - 1
    return pl.BlockSpec((halo, width), lambda i: (jnp.minimum((i + 1) * per, last), col))


CONV_PAD = ROWS + CONV_HALO


def _shift_copies(sh):
    for b in range(1, 8):
        sh[b, 0:CONV_PAD - 8, :] = sh[0, pl.ds(b, CONV_PAD - 8), :]


def _tap(sh, rs_start, offset):
    return sh[offset % 8, pl.ds(pl.multiple_of(rs_start + (offset // 8) * 8, 8), SUB), :]


def _conv_glu(av_ref, ag_ref, avh_ref, agh_ref, sh):
    i = pl.program_id(0)
    hv = avh_ref[...] * _sig(agh_ref[...])
    sh[0, 0:CONV_HALO, :] = jnp.where(i > 0, hv, 0.0)

    def glu(rs):
        sh[0, pl.ds(pl.multiple_of(rs.start + CONV_HALO, SUB), SUB), :] = av_ref[rs, :] * _sig(ag_ref[rs, :])

    _for_chunks(glu)
    _shift_copies(sh)


def _conv_norm(u1, lg_ref, lb_ref):
    mu = _mean(u1)
    cen = u1 - mu
    rs = lax.rsqrt(_mean(cen * cen) + EPS)
    z = cen * rs
    ln = z * lg_ref[...] + lb_ref[...]
    s = _sig(ln)
    return z, rs, ln, s, ln * s


def conv_module_fwd(proj, wc, bc, lg, lb, gco, name):
    def body(av_ref, ag_ref, avh_ref, agh_ref, wc_ref, bc_ref, lg_ref, lb_ref, gco_ref, out_ref, u1_ref, sh):
        _conv_glu(av_ref, ag_ref, avh_ref, agh_ref, sh)

        def conv(rs):
            u1 = jnp.broadcast_to(bc_ref[...], (SUB, D_CONV))
            for j in range(CONV_K):
                u1 = u1 + wc_ref[j:j + 1, :] * _tap(sh, rs.start, CONV_HALO - (CONV_K - 1) + j)
            u1_ref[rs, :] = u1

        _for_chunks(conv)
        _, _, _, _, u2 = _conv_norm(u1_ref[...], lg_ref, lb_ref)
        rc = lax.rsqrt(_mean(u2 * u2) + EPS)
        out_ref[...] = (u2 * rc * gco_ref[...]).astype(BF16)

    v = _vec_spec(D_CONV)
    return pl.pallas_call(
        body, out_shape=(jax.ShapeDtypeStruct((SEQ, D_CONV), BF16), jax.ShapeDtypeStruct((SEQ, D_CONV), F32)),
        grid=(SEQ // ROWS,),
        in_specs=[_row_spec(D_CONV, 0), _row_spec(D_CONV, 1), _prev_halo(CONV_HALO, D_CONV, 0),
                  _prev_halo(CONV_HALO, D_CONV, 1), _vec_spec(D_CONV, CONV_K), v, v, v, v],
        out_specs=(_row_spec(D_CONV), _row_spec(D_CONV)), scratch_shapes=[pltpu.VMEM((8, CONV_PAD, D_CONV), F32)],
        name=name, compiler_params=_cp("parallel"))(proj, proj, proj, proj, wc, bc, lg, lb, gco)


def conv_module_bwd_a(proj, u1, dmixed, lg, lb, gco, name):
    def body(av_ref, ag_ref, avh_ref, agh_ref, u1_ref, dm_ref, lg_ref, lb_ref, gco_ref,
             du1_ref, dgco_ref, dlg_ref, dlb_ref, dbc_ref, dwc_ref, sh, acc):
        first = pl.program_id(0) == 0
        _conv_glu(av_ref, ag_ref, avh_ref, agh_ref, sh)
        z, rs, ln, s, u2 = _conv_norm(u1_ref[...], lg_ref, lb_ref)
        rc = lax.rsqrt(_mean(u2 * u2) + EPS)
        xn = u2 * rc
        dm = dm_ref[...]
        _acc(dgco_ref, _rsum(dm * xn), first)
        dyn = dm * gco_ref[...]
        du2 = rc * (dyn - xn * _mean(dyn * xn))
        dln = du2 * (s * (1.0 + ln * (1.0 - s)))
        _acc(dlg_ref, _rsum(dln * z), first)
        _acc(dlb_ref, _rsum(dln), first)
        dz = dln * lg_ref[...]
        du1 = rs * (dz - _mean(dz) - z * _mean(dz * z))
        du1_ref[...] = du1
        _acc(dbc_ref, _rsum(du1), first)
        acc[...] = jnp.zeros_like(acc)

        def taps(rs):
            d = du1_ref[rs, :]
            for j in range(CONV_K):
                acc[j] += d * _tap(sh, rs.start, CONV_HALO - (CONV_K - 1) + j)

        _for_chunks(taps)

        @pl.when(first)
        def _():
            dwc_ref[...] = jnp.zeros_like(dwc_ref)

        for j in range(CONV_K):
            dwc_ref[j:j + 1, :] += _rsum(acc[j])

    v = _vec_spec(D_CONV)
    vec = jax.ShapeDtypeStruct((1, D_CONV), F32)
    return pl.pallas_call(
        body,
        out_shape=(jax.ShapeDtypeStruct((SEQ, D_CONV), F32), vec, vec, vec, vec, jax.ShapeDtypeStruct((CONV_K, D_CONV), F32)),
        grid=(SEQ // ROWS,),
        in_specs=[_row_spec(D_CONV, 0), _row_spec(D_CONV, 1), _prev_halo(CONV_HALO, D_CONV, 0),
                  _prev_halo(CONV_HALO, D_CONV, 1), _row_spec(D_CONV, 0), _row_spec(D_CONV, 0), v, v, v],
        out_specs=(_row_spec(D_CONV), v, v, v, v, _vec_spec(D_CONV, CONV_K)),
        scratch_shapes=[pltpu.VMEM((8, CONV_PAD, D_CONV), F32), pltpu.VMEM((CONV_K, SUB, D_CONV), F32)],
        name=name, compiler_params=_cp("arbitrary"))(proj, proj, proj, proj, u1, dmixed, lg, lb, gco)


def conv_module_bwd_b(proj, du1, wc, name):
    def body(av_ref, ag_ref, du1_ref, du1n_ref, wc_ref, out_ref, sh):
        i = pl.program_id(0)
        sh[0, 0:ROWS, :] = du1_ref[...]
        sh[0, ROWS:, :] = jnp.where(i < SEQ // ROWS - 1, du1n_ref[...], 0.0)
        _shift_copies(sh)

        def chunk(rs):
            du0 = jnp.zeros((SUB, D_CONV), F32)
            for j in range(CONV_K):
                du0 = du0 + wc_ref[j:j + 1, :] * _tap(sh, rs.start, CONV_K - 1 - j)
            sg = _sig(ag_ref[rs, :])
            out_ref[rs, 0:D_CONV] = (du0 * sg).astype(BF16)
            out_ref[rs, D_CONV:] = (du0 * av_ref[rs, :] * sg * (1.0 - sg)).astype(BF16)

        _for_chunks(chunk)

    return pl.pallas_call(
        body, out_shape=jax.ShapeDtypeStruct((SEQ, 2 * D_CONV), BF16), grid=(SEQ // ROWS,),
        in_specs=[_row_spec(D_CONV, 0), _row_spec(D_CONV, 1), _row_spec(D_CONV, 0), _next_halo(CONV_HALO, D_CONV, 0),
                  _vec_spec(D_CONV, CONV_K)],
        out_specs=_row_spec(2 * D_CONV), scratch_shapes=[pltpu.VMEM((8, CONV_PAD, D_CONV), F32)],
        name=name, compiler_params=_cp("parallel"))(proj, proj, du1, du1, wc)


def _rows(start, size, r):
    return pl.ds(start, size) if r == 1 else pl.ds(start, size, stride=r)


def _unit_rows(r, rho, n, nb):
    win = 2 * ATTN_BLOCK if nb > 1 else ATTN_BLOCK
    if isinstance(n, int):
        kb = max(n - 1, 0)
        q_rows = _rows(rho + r * ATTN_BLOCK * n, ATTN_BLOCK, r)
        k_rows = _rows(rho + r * ATTN_BLOCK * kb, win, r)
    else:
        kb = jnp.maximum(n - 1, 0)
        q_rows = pl.ds(pl.multiple_of(n * ATTN_BLOCK, ATTN_BLOCK), ATTN_BLOCK)
        k_rows = pl.ds(pl.multiple_of(kb * ATTN_BLOCK, ATTN_BLOCK), win)
    dist = (n - kb) * ATTN_BLOCK + lax.broadcasted_iota(jnp.int32, (ATTN_BLOCK, win), 0) \
        - lax.broadcasted_iota(jnp.int32, (ATTN_BLOCK, win), 1)
    return q_rows, k_rows, (dist >= 0) & (dist <= ATTN_BLOCK)


def _per_head(x):
    lane = lax.broadcasted_iota(jnp.int32, x.shape, 1)
    zero = jnp.zeros_like(x)
    return [jnp.where(lane < HEAD_DIM, x, zero), jnp.where(lane >= HEAD_DIM, x, zero)]


def _masked_scores(q2, k2, valid):
    return [jnp.where(valid, lax.dot_general(qh, k2, NT, preferred_element_type=F32) * (HEAD_DIM ** -0.5), NEG)
            for qh in _per_head(q2)]


def _attn_units(r, nb, unit):
    if r == 1:
        def four(i, carry):
            for k in range(4):
                unit(0, 4 * i + k)
            return carry
        lax.fori_loop(0, nb // 4, four, 0)
    else:
        for rho in range(r):
            for n in range(nb):
                unit(rho, n)


N_UNITS = 16


def attn_fwd_all(proj, name):
    def body(q_ref, k_ref, v_ref, att_ref, lse_ref, s_scr, p_scr, lse_scr, den_scr):
        for idx, (sub_len, r) in enumerate(PATTERNS):
            nb = sub_len // ATTN_BLOCK
            win = 2 * ATTN_BLOCK if nb > 1 else ATTN_BLOCK

            def scores(rho, n, r=r, nb=nb, win=win):
                u = rho * nb + n
                q_rows, k_rows, valid = _unit_rows(r, rho, n, nb)
                ss = _masked_scores(q_ref[q_rows, :].astype(BF16), k_ref[k_rows, :].astype(BF16), valid)
                for h in range(2):
                    s_scr[2 * u + h, :, 0:win] = ss[h]

            _attn_units(r, nb, scores)

            def softmax(u, carry, win=win):
                lses, dens = [], []
                for h in range(2):
                    sc = s_scr[2 * u + h, :, 0:win]
                    m = jnp.max(sc, axis=1, keepdims=True)
                    p = jnp.exp(sc - m)
                    den = jnp.sum(p, axis=1, keepdims=True)
                    p_scr[2 * u + h, :, 0:win] = p.astype(BF16)
                    lses.append(jnp.broadcast_to(m + jnp.log(den), (ATTN_BLOCK, HEAD_DIM)))
                    dens.append(jnp.broadcast_to(den, (ATTN_BLOCK, HEAD_DIM)))
                lse_scr[u] = jnp.concatenate(lses, axis=1)
                den_scr[u] = jnp.concatenate(dens, axis=1)
                return carry

            lax.fori_loop(0, N_UNITS, softmax, 0, unroll=2)

            def outputs(rho, n, r=r, nb=nb, win=win, idx=idx):
                u = rho * nb + n
                q_rows, k_rows, _ = _unit_rows(r, rho, n, nb)
                vs = _per_head(v_ref[k_rows, :].astype(BF16))
                o = (jnp.dot(p_scr[2 * u, :, 0:win], vs[0], preferred_element_type=F32)
                     + jnp.dot(p_scr[2 * u + 1, :, 0:win], vs[1], preferred_element_type=F32)) / den_scr[u]
                lse = lse_scr[u]
                if idx > 0:
                    old = lse_ref[q_rows, :]
                    top = jnp.maximum(old, lse)
                    new = top + jnp.log(jnp.exp(old - top) + jnp.exp(lse - top))
                    o = att_ref[q_rows, :] * jnp.exp(old - new) + o * jnp.exp(lse - new)
                    lse = new
                att_ref[q_rows, :] = o
                lse_ref[q_rows, :] = lse

            _attn_units(r, nb, outputs)

    blk = lambda first: pl.BlockSpec((SEQ, 128), lambda g: (0, first + g))
    shp = jax.ShapeDtypeStruct((SEQ, D_ATTN), F32)
    big = (2 * N_UNITS, ATTN_BLOCK, 2 * ATTN_BLOCK)
    small = pltpu.VMEM((N_UNITS, ATTN_BLOCK, 128), F32)
    return pl.pallas_call(
        body, out_shape=(shp, shp), grid=(4,), in_specs=[blk(8), blk(12), blk(16)], out_specs=(blk(0), blk(0)),
        scratch_shapes=[pltpu.VMEM(big, F32), pltpu.VMEM(big, BF16), small, small],
        name=name, compiler_params=_cp("parallel"))(proj, proj, proj)


def attn_bwd_all(proj, do, lse, dd, name):
    scale = HEAD_DIM ** -0.5

    def body(q_ref, k_ref, v_ref, do_ref, l_ref, dd_ref, out_ref, dq_s, dk_s, dv_s,
             s_scr, dp_scr, ds_scr, st_scr, dpt_scr, pt_scr, dst_scr, qb_scr, kb_scr, dob_scr):
        dq_s[...] = jnp.zeros_like(dq_s)
        dk_s[...] = jnp.zeros_like(dk_s)
        dv_s[...] = jnp.zeros_like(dv_s)
        for sub_len, r in PATTERNS:
            nb = sub_len // ATTN_BLOCK
            win = 2 * ATTN_BLOCK if nb > 1 else ATTN_BLOCK

            def scores(rho, n, r=r, nb=nb, win=win):
                u = rho * nb + n
                q_rows, k_rows, valid = _unit_rows(r, rho, n, nb)
                kb = max(n - 1, 0) if isinstance(n, int) else jnp.maximum(n - 1, 0)
                dist_t = (n - kb) * ATTN_BLOCK + lax.broadcasted_iota(jnp.int32, (win, ATTN_BLOCK), 1) \
                    - lax.broadcasted_iota(jnp.int32, (win, ATTN_BLOCK), 0)
                valid_t = (dist_t >= 0) & (dist_t <= ATTN_BLOCK)
                q2 = q_ref[q_rows, :].astype(BF16)
                k2 = k_ref[k_rows, :].astype(BF16)
                do2 = do_ref[q_rows, :].astype(BF16)
                qb_scr[u] = q2
                kb_scr[u, 0:win, :] = k2
                dob_scr[u] = do2
                l2 = l_ref[q_rows, :]
                d2 = dd_ref[q_rows, :]
                l2t = l2.T
                d2t = d2.T
                v2 = v_ref[k_rows, :].astype(BF16)
                qs, dos = _per_head(q2), _per_head(do2)
                for h in range(2):
                    c0 = h * HEAD_DIM
                    sc = lax.dot_general(qs[h], k2, NT, preferred_element_type=F32) * scale
                    s_scr[2 * u + h, :, 0:win] = jnp.where(valid, sc, NEG) - l2[:, c0:c0 + 1]
                    dp_scr[2 * u + h, :, 0:win] = lax.dot_general(dos[h], v2, NT, preferred_element_type=F32) \
                        - d2[:, c0:c0 + 1]
                    sct = lax.dot_general(k2, qs[h], NT, preferred_element_type=F32) * scale
                    st_scr[2 * u + h, 0:win, :] = jnp.where(valid_t, sct, NEG) - l2t[c0:c0 + 1, :]
                    dpt_scr[2 * u + h, 0:win, :] = lax.dot_general(v2, dos[h], NT, preferred_element_type=F32) \
                        - d2t[c0:c0 + 1, :]

            _attn_units(r, nb, scores)

            def pointwise(hu, carry, win=win):
                ds_scr[hu, :, 0:win] = (jnp.exp(s_scr[hu, :, 0:win]) * dp_scr[hu, :, 0:win] * scale).astype(BF16)
                pt = jnp.exp(st_scr[hu, 0:win, :])
                pt_scr[hu, 0:win, :] = pt.astype(BF16)
                dst_scr[hu, 0:win, :] = (pt * dpt_scr[hu, 0:win, :] * scale).astype(BF16)
                return carry

            lax.fori_loop(0, 2 * N_UNITS, pointwise, 0, unroll=4)

            def grads(rho, n, r=r, nb=nb, win=win):
                u = rho * nb + n
                q_rows, k_rows, _ = _unit_rows(r, rho, n, nb)
                qs, ks, dos = _per_head(qb_scr[u]), _per_head(kb_scr[u, 0:win, :]), _per_head(dob_scr[u])

                def both(scr, rows, rhs):
                    return (jnp.dot(scr[(2 * u,) + rows], rhs[0], preferred_element_type=F32)
                            + jnp.dot(scr[(2 * u + 1,) + rows], rhs[1], preferred_element_type=F32))

                dq_s[q_rows, :] += both(ds_scr, (slice(None), slice(0, win)), ks)
                dk_s[k_rows, :] += both(dst_scr, (slice(0, win), slice(None)), qs)
                dv_s[k_rows, :] += both(pt_scr, (slice(0, win), slice(None)), dos)

            _attn_units(r, nb, grads)
        out_ref[0] = dq_s[...].astype(BF16)
        out_ref[1] = dk_s[...].astype(BF16)
        out_ref[2] = dv_s[...].astype(BF16)

    blk = lambda first: pl.BlockSpec((SEQ, 128), lambda g: (0, first + g))
    acc = pltpu.VMEM((SEQ, 128), F32)
    big = (2 * N_UNITS, ATTN_BLOCK, 2 * ATTN_BLOCK)
    big_t = (2 * N_UNITS, 2 * ATTN_BLOCK, ATTN_BLOCK)
    return pl.pallas_call(
        body, out_shape=jax.ShapeDtypeStruct((3, SEQ, D_ATTN), BF16), grid=(4,),
        in_specs=[blk(8), blk(12), blk(16), blk(0), blk(0), blk(0)],
        out_specs=pl.BlockSpec((3, SEQ, 128), lambda g: (0, 0, g)),
        scratch_shapes=[acc, acc, acc, pltpu.VMEM(big, F32), pltpu.VMEM(big, F32), pltpu.VMEM(big, BF16),
                        pltpu.VMEM(big_t, F32), pltpu.VMEM(big_t, F32), pltpu.VMEM(big_t, BF16), pltpu.VMEM(big_t, BF16),
                        pltpu.VMEM((N_UNITS, ATTN_BLOCK, 128), BF16),
                        pltpu.VMEM((N_UNITS, 2 * ATTN_BLOCK, 128), BF16), pltpu.VMEM((N_UNITS, ATTN_BLOCK, 128), BF16)],
        name=name, compiler_params=_cp("parallel"))(proj, proj, proj, do, lse, dd)


N_FT = D_FF // FFN_TN


def _ffn_specs():
    per = ROWS // FFN_HALO
    cur_g = pl.BlockSpec((ROWS, FFN_TN), lambda j, i: (i, j))
    cur_v = pl.BlockSpec((ROWS, FFN_TN), lambda j, i: (i, j + N_FT))
    halo_g = pl.BlockSpec((FFN_HALO, FFN_TN), lambda j, i: (jnp.maximum(i * per - 1, 0), j))
    halo_v = pl.BlockSpec((FFN_HALO, FFN_TN), lambda j, i: (jnp.maximum(i * per - 1, 0), j + N_FT))
    w_g = pl.BlockSpec((FFN_K, FFN_TN), lambda j, i: (0, j))
    w_v = pl.BlockSpec((FFN_K, FFN_TN), lambda j, i: (0, j + N_FT))
    b_g = pl.BlockSpec((1, FFN_TN), lambda j, i: (0, j))
    b_v = pl.BlockSpec((1, FFN_TN), lambda j, i: (0, j + N_FT))
    return [cur_g, cur_v, halo_g, halo_v, w_g, w_v, b_g, b_v]


def matmul(a, b, kind, out_dtype, tm, tn, name, b_rows=None):
    stacked = b_rows is not None
    b_shape = (N_DEV * b_rows, b.shape[2]) if stacked else b.shape
    if kind == "nn":
        (m, k), n = a.shape, b_shape[1]
        a_spec = pl.BlockSpec((tm, k), lambda j, i: (i, 0))
        b_spec = pl.BlockSpec((k, tn), lambda j, i: (0, j))
        dims = (((1,), (0,)), ((), ()))
    elif kind == "nt":
        (m, k), n = a.shape, b_shape[0]
        a_spec = pl.BlockSpec((tm, k), lambda j, i: (i, 0))
        b_spec = pl.BlockSpec((tn, k), lambda j, i: (j, 0))
        dims = (((1,), (1,)), ((), ()))
    else:
        (k, m), n = a.shape, b_shape[1]
        a_spec = pl.BlockSpec((k, tm), lambda j, i: (0, i))
        b_spec = pl.BlockSpec((k, tn), lambda j, i: (0, j))
        dims = (((0,), (0,)), ((), ()))
    assert m % tm == 0 and n % tn == 0, (name, m, n, tm, tn)
    if stacked:
        assert b_spec.block_shape[0] == b_shape[0] and kind in ("nn", "nt")
        width = b_spec.block_shape[1]
        b_spec = pl.BlockSpec((N_DEV, b_rows, width), (lambda j, i: (0, 0, j)) if kind == "nn" else (lambda j, i: (0, 0, 0)))

    def body(a_ref, b_ref, o_ref):
        bb = b_ref[...].reshape(b_shape[0], -1) if stacked else b_ref[...]
        o_ref[...] = lax.dot_general(a_ref[...], bb, dims, preferred_element_type=F32).astype(o_ref.dtype)

    return pl.pallas_call(
        body, out_shape=jax.ShapeDtypeStruct((m, n), out_dtype), grid=(n // tn, m // tm),
        in_specs=[a_spec, b_spec], out_specs=pl.BlockSpec((tm, tn), lambda j, i: (i, j)),
        name=name, compiler_params=_cp("parallel", "parallel"))(a, b)


def matmul_tn_pieces(a1, a3, b, name):
    k, n = b.shape
    tm = a3.shape[2]
    n1 = a1.shape[1] // tm

    def body(a1_ref, a3_ref, b_ref, o_ref):
        i = pl.program_id(0)
        tn_dims = (((0,), (0,)), ((), ()))

        @pl.when(i < n1)
        def _():
            o_ref[...] = lax.dot_general(a1_ref[...], b_ref[...], tn_dims, preferred_element_type=F32).astype(BF16)

        @pl.when(i >= n1)
        def _():
            o_ref[...] = lax.dot_general(a3_ref[0], b_ref[...], tn_dims, preferred_element_type=F32).astype(BF16)

    return pl.pallas_call(
        body, out_shape=jax.ShapeDtypeStruct((a1.shape[1] + 3 * tm, n), BF16), grid=(n1 + 3,),
        in_specs=[pl.BlockSpec((k, tm), lambda i: (0, jnp.minimum(i, n1 - 1))),
                  pl.BlockSpec((1, k, tm), lambda i: (jnp.maximum(i - n1, 0), 0, 0)),
                  pl.BlockSpec((k, n), lambda i: (0, 0))],
        out_specs=pl.BlockSpec((tm, n), lambda i: (i, 0)), name=name, compiler_params=_cp("parallel"))(a1, a3, b)


def matmul_tn_halves(a, b, tm, name):
    _, k, m = a.shape
    n = b.shape[1]

    def body(a_ref, b_ref, o_ref):
        o_ref[0] = lax.dot_general(a_ref[0], b_ref[...], (((0,), (0,)), ((), ())), preferred_element_type=F32).astype(BF16)

    return pl.pallas_call(
        body, out_shape=jax.ShapeDtypeStruct((2, m, n), BF16), grid=(2, m // tm),
        in_specs=[pl.BlockSpec((1, k, tm), lambda h, i: (h, 0, i)), pl.BlockSpec((k, n), lambda h, i: (0, 0))],
        out_specs=pl.BlockSpec((1, tm, n), lambda h, i: (h, i, 0)), name=name,
        compiler_params=_cp("parallel", "parallel"))(a, b).reshape(2 * m, n)


def _row_spec(width, col=0):
    return pl.BlockSpec((ROWS, width), lambda i: (i, col))


def _vec_spec(width, rows=1):
    return pl.BlockSpec((rows, width), lambda i: (0, 0))


def _ffn_shifted(cur_ref, halo_ref, pad, s1, s2):
    i = pl.program_id(1)
    pad[0:FFN_HALO, :] = jnp.where(i > 0, halo_ref[...], 0.0)
    pad[FFN_HALO:, :] = cur_ref[0:FFN_HALO, :]
    for k, dst in ((1, s1), (2, s2)):
        dst[0:FFN_HALO, :] = pad[pl.ds(FFN_HALO - k, FFN_HALO), :]
        dst[FFN_HALO:, :] = cur_ref[pl.ds(FFN_HALO - k, ROWS - FFN_HALO), :]


def _ffn_conv(rs, cur_ref, s1, s2, w_ref, b_ref):
    return b_ref[...] + w_ref[0:1, :] * s2[rs, :] + w_ref[1:2, :] * s1[rs, :] + w_ref[2:3, :] * cur_ref[rs, :]


def ffn_act_fwd(up0, wf, bf, name):
    def body(g_ref, v_ref, gh_ref, vh_ref, wg_ref, wv_ref, bg_ref, bv_ref, act_ref, pad, g1, g2, v1, v2):
        _ffn_shifted(g_ref, gh_ref, pad, g1, g2)
        _ffn_shifted(v_ref, vh_ref, pad, v1, v2)

        def chunk(rs):
            gate = _ffn_conv(rs, g_ref, g1, g2, wg_ref, bg_ref)
            val = _ffn_conv(rs, v_ref, v1, v2, wv_ref, bv_ref)
            act_ref[rs, :] = (gate * _sig(gate) * val).astype(BF16)

        _for_chunks(chunk)

    tile = pltpu.VMEM((ROWS, FFN_TN), F32)
    return pl.pallas_call(
        body, out_shape=jax.ShapeDtypeStruct((SEQ, D_FF), BF16), grid=(N_FT, SEQ // ROWS),
        in_specs=_ffn_specs(), out_specs=pl.BlockSpec((ROWS, FFN_TN), lambda j, i: (i, j)),
        scratch_shapes=[pltpu.VMEM((2 * FFN_HALO, FFN_TN), F32), tile, tile, tile, tile],
        name=name, compiler_params=_cp("parallel", "parallel"))(up0, up0, up0, up0, wf, wf, bf, bf)


def ffn_bwd(up0, dact, wf, bf, name, after=None):
    per = ROWS // FFN_HALO
    last = SEQ // FFN_HALO - 1

    def body(g_ref, v_ref, gh_ref, vh_ref, wg_ref, wv_ref, bg_ref, bv_ref, da_ref, gn_ref, vn_ref, dan_ref,
             out_ref, dbg_ref, dbv_ref, dwg_ref, dwv_ref, pad, g1, g2, v1, v2, dgp, dvp, acc):
        i = pl.program_id(1)
        first = i == 0
        _ffn_shifted(g_ref, gh_ref, pad, g1, g2)
        _ffn_shifted(v_ref, vh_ref, pad, v1, v2)
        acc[...] = jnp.zeros_like(acc)

        def grads(gate, val, da):
            s = _sig(gate)
            return da * val * (s * (1.0 + gate * (1.0 - s))), da * (gate * s)

        def chunk(rs):
            gate = _ffn_conv(rs, g_ref, g1, g2, wg_ref, bg_ref)
            val = _ffn_conv(rs, v_ref, v1, v2, wv_ref, bv_ref)
            dgate, dval = grads(gate, val, da_ref[rs, :])
            dgp[rs, :] = dgate
            dvp[rs, :] = dval
            acc[0] += dgate
            acc[1] += dval
            for t, (sg, sv) in enumerate(((g2, v2), (g1, v1), (g_ref, v_ref))):
                acc[2 + t] += dgate * sg[rs, :]
                acc[5 + t] += dval * sv[rs, :]

        _for_chunks(chunk)
        _acc(dbg_ref, _rsum(acc[0]), first)
        _acc(dbv_ref, _rsum(acc[1]), first)
        _acc(dwg_ref, jnp.concatenate([_rsum(acc[2 + t]) for t in range(FFN_K)], axis=0), first)
        _acc(dwv_ref, jnp.concatenate([_rsum(acc[5 + t]) for t in range(FFN_K)], axis=0), first)

        def conv_next(cur_ref, nxt_ref, w_ref, b_ref):
            pad[0:FFN_HALO, :] = cur_ref[ROWS - FFN_HALO:, :]
            pad[FFN_HALO:, :] = nxt_ref[...]
            return (b_ref[...] + w_ref[0:1, :] * pad[pl.ds(FFN_HALO - 2, FFN_HALO), :]
                    + w_ref[1:2, :] * pad[pl.ds(FFN_HALO - 1, FFN_HALO), :] + w_ref[2:3, :] * nxt_ref[...])

        gate_n = conv_next(g_ref, gn_ref, wg_ref, bg_ref)
        val_n = conv_next(v_ref, vn_ref, wv_ref, bv_ref)
        dgate_n, dval_n = grads(gate_n, val_n, dan_ref[...])
        inside = i < SEQ // ROWS - 1
        dgp[ROWS:, :] = jnp.where(inside, dgate_n, 0.0)
        dvp[ROWS:, :] = jnp.where(inside, dval_n, 0.0)

        for half, (dp, s1, s2, w_ref) in enumerate(((dgp, g1, g2, wg_ref), (dvp, v1, v2, wv_ref))):
            s1[...] = dp[pl.ds(1, ROWS), :]
            s2[...] = dp[pl.ds(2, ROWS), :]

            def back(rs, dp=dp, s1=s1, s2=s2, w_ref=w_ref, half=half):
                out_ref[half, rs, :] = (w_ref[2:3, :] * dp[rs, :] + w_ref[1:2, :] * s1[rs, :]
                                        + w_ref[0:1, :] * s2[rs, :]).astype(BF16)

            _for_chunks(back)

    body, more_specs, more = _with_after(body, 12, after)
    tile = pltpu.VMEM((ROWS, FFN_TN), F32)
    ext = pltpu.VMEM((ROWS + FFN_HALO, FFN_TN), F32)
    vec = jax.ShapeDtypeStruct((1, D_FF), F32)
    taps = jax.ShapeDtypeStruct((FFN_K, D_FF), F32)
    cur = pl.BlockSpec((ROWS, FFN_TN), lambda j, i: (i, j))
    nxt = lambda off: pl.BlockSpec((FFN_HALO, FFN_TN), lambda j, i: (jnp.minimum((i + 1) * per, last), j + off))
    vs = pl.BlockSpec((1, FFN_TN), lambda j, i: (0, j))
    ts = pl.BlockSpec((FFN_K, FFN_TN), lambda j, i: (0, j))
    return pl.pallas_call(
        body, out_shape=(jax.ShapeDtypeStruct((2, SEQ, D_FF), BF16), vec, vec, taps, taps), grid=(N_FT, SEQ // ROWS),
        in_specs=_ffn_specs() + [cur, nxt(0), nxt(N_FT), nxt(0)] + more_specs,
        out_specs=(pl.BlockSpec((2, ROWS, FFN_TN), lambda j, i: (0, i, j)), vs, vs, ts, ts),
        scratch_shapes=[pltpu.VMEM((2 * FFN_HALO, FFN_TN), F32), tile, tile, tile, tile, ext, ext,
                        pltpu.VMEM((2 + 2 * FFN_K, SUB, FFN_TN), F32)],
        name=name, compiler_params=_cp("parallel", "arbitrary"))(up0, up0, up0, up0, wf, wf, bf, bf, dact, up0, up0, dact,
                                                                 *more)


def ada_fwd(c_all, w_ada, b_cols, name):
    def body(c_ref, w_ref, b_ref, o_ref):
        cc = c_ref[...]
        sc = (cc * _sig(cc)).astype(BF16)
        o_ref[...] = jnp.dot(sc, w_ref[...].astype(BF16), preferred_element_type=F32) + b_ref[...]

    return pl.pallas_call(body, out_shape=jax.ShapeDtypeStruct((N_DEV, w_ada.shape[1]), F32), name=name,
                          compiler_params=_cp())(c_all, w_ada, b_cols)


def _adam(w, g, m, v):
    m = ADAM_B1 * m + (1.0 - ADAM_B1) * g
    v = ADAM_B2 * v + (1.0 - ADAM_B2) * (g * g)
    m_hat = m / (1.0 - ADAM_B1 ** ADAM_STEP)
    v_hat = v / (1.0 - ADAM_B2 ** ADAM_STEP)
    delta = -ADAM_LR * (m_hat / (jnp.sqrt(v_hat) + ADAM_EPS) + ADAM_WD * w)
    return delta, m, v


def ada_bwd_adamw(c_all, dmod_cols, w, m, v, name):
    rows, cols = w.shape
    tr = 256

    def body(c_ref, dm_ref, w_ref, m_ref, v_ref, g_ref, d_ref, nm_ref, nv_ref):
        cc = c_ref[...]
        sc = (cc * _sig(cc)).T
        g = sc[:, 0:1] * dm_ref[0:1, :]
        for b in range(1, N_DEV):
            g = g + sc[:, b:b + 1] * dm_ref[b:b + 1, :]
        g_ref[...] = g
        d_ref[...], nm_ref[...], nv_ref[...] = _adam(w_ref[...], g, m_ref[...], v_ref[...])

    blk = pl.BlockSpec((tr, cols), lambda i: (i, 0))
    shp = jax.ShapeDtypeStruct((rows, cols), F32)
    return pl.pallas_call(
        body, out_shape=(shp, shp, shp, shp), grid=(rows // tr,),
        in_specs=[pl.BlockSpec((N_DEV, tr), lambda i: (0, i)), pl.BlockSpec((N_DEV, cols), lambda i: (0, 0)), blk, blk, blk],
        out_specs=(blk, blk, blk, blk), name=name, compiler_params=_cp("parallel"))(c_all, dmod_cols, w, m, v)


def sum_adamw(parts, mine, me, w, m, v, tr, name):
    n_parts, rows, cols = parts.shape

    def body(me_ref, p_ref, own_ref, w_ref, m_ref, v_ref, g_ref, d_ref, nm_ref, nv_ref):
        def chunk(rs):
            g = own_ref[0, rs, :].astype(F32)
            for k in range(1, n_parts):
                g = g + p_ref[k, rs, :].astype(F32)
            g_ref[rs, :] = g
            d_ref[rs, :], nm_ref[rs, :], nv_ref[rs, :] = _adam(w_ref[rs, :], g, m_ref[rs, :], v_ref[rs, :])

        _for_chunks(chunk, 2, tr)

    blk = pl.BlockSpec((tr, cols), lambda i, me_ref: (i, 0))
    shp = jax.ShapeDtypeStruct((rows, cols), F32)
    grid_spec = pltpu.PrefetchScalarGridSpec(
        num_scalar_prefetch=1, grid=(rows // tr,),
        in_specs=[pl.BlockSpec((n_parts, tr, cols), lambda i, me_ref: (0, i, 0)),
                  pl.BlockSpec((1, tr, cols), lambda i, me_ref: (me_ref[0], i, 0)), blk, blk, blk],
        out_specs=(blk, blk, blk, blk))
    return pl.pallas_call(body, out_shape=(shp, shp, shp, shp), grid_spec=grid_spec, name=name,
                          compiler_params=_cp("parallel"))(me, parts, mine, w, m, v)


MESH = pl.DeviceIdType.MESH


def all_gather(block, name, after=None):
    extra = () if after is None else (after,)

    def body(x_ref, *refs):
        out_ref, send_sems, recv_sems, local_sem = refs[len(extra):]
        x, y, c = lax.axis_index("x"), lax.axis_index("y"), lax.axis_index("c")
        me, sibling = (x, y, c), (x, y, 1 - c)
        chips = [(1 - x, y), (x, 1 - y), (1 - x, 1 - y)]

        def slot(px, py, pc):
            return out_ref.at[4 * px + 2 * py + pc]

        def copy(k, blk, to, src=None):
            return pltpu.make_async_remote_copy(
                src_ref=slot(*blk) if src is None else src, dst_ref=slot(*blk),
                send_sem=send_sems.at[k], recv_sem=recv_sems.at[k], device_id=to, device_id_type=MESH)

        mine = pltpu.make_async_copy(x_ref, slot(*me), local_sem)
        mine.start()
        first = [copy(0, me, sibling, src=x_ref)]
        first += [copy(1 + j, me, (*chip, c), src=x_ref) for j, chip in enumerate(chips)]
        for cp in first:
            cp.start()
        passed = [copy(4 + j, (*chip, c), sibling) for j, chip in enumerate(chips)]
        for j, chip in enumerate(chips):
            copy(1 + j, (*chip, c), me).wait_recv()
            passed[j].start()
        copy(0, sibling, me).wait_recv()
        for j, chip in enumerate(chips):
            copy(4 + j, (*chip, 1 - c), me).wait_recv()
        for cp in first + passed:
            cp.wait_send()
        mine.wait()

    return pl.pallas_call(
        body, out_shape=jax.ShapeDtypeStruct((N_DEV,) + block.shape, block.dtype), in_specs=[ANY] * (1 + len(extra)), out_specs=ANY,
        scratch_shapes=[pltpu.SemaphoreType.DMA((7,)), pltpu.SemaphoreType.DMA((7,)), pltpu.SemaphoreType.DMA],
        name=name)(block, *extra)


HBM = pl.BlockSpec(memory_space=pltpu.HBM)
SEM = pl.BlockSpec(memory_space=pltpu.SEMAPHORE)
EFFECT = pltpu.SideEffectType.DATAFLOW_SIDE_EFFECTING


def _peer_copies(src_ref, land_ref, send_sems, recv_sems, gather):
    x, y, c = lax.axis_index("x"), lax.axis_index("y"), lax.axis_index("c")
    me = 4 * x + 2 * y + c
    copies = []
    for k in range(1, N_DEV):
        px = 1 - x if k & 4 else x
        py = 1 - y if k & 2 else y
        pc = 1 - c if k & 1 else c
        copies.append(pltpu.make_async_remote_copy(
            src_ref=src_ref if gather else src_ref.at[4 * px + 2 * py + pc],
            dst_ref=land_ref.at[me] if gather else land_ref.at[k],
            send_sem=send_sems.at[k - 1], recv_sem=recv_sems.at[k - 1], device_id=(px, py, pc), device_id_type=MESH))
    return copies


def exchange_start(srcs, gather, name, after=None):
    n = len(srcs)
    land_shapes = [(N_DEV,) + src.shape if gather else src.shape for src in srcs]
    extra = () if after is None else (after,)

    def body(*refs):
        src_refs, land_refs = refs[0:n], refs[n:2 * n]
        outs = refs[2 * n + len(extra):]
        for k in range(n):
            for cp in _peer_copies(src_refs[k], land_refs[k], outs[4 * k], outs[4 * k + 1], gather):
                cp.start()
        token = outs[4 * n]
        token[...] = jnp.zeros_like(token)

    out_shape, out_specs, aliases = [], [], {}
    for k, src in enumerate(srcs):
        out_shape += [pltpu.SemaphoreType.DMA((N_DEV - 1,)), pltpu.SemaphoreType.DMA((N_DEV - 1,)),
                      pltpu.HBM(src.shape, src.dtype), pltpu.HBM(land_shapes[k], src.dtype)]
        out_specs += [SEM, SEM, HBM, HBM]
        aliases[k] = 4 * k + 2
        aliases[n + k] = 4 * k + 3
    out_shape.append(jax.ShapeDtypeStruct((8, 128), F32))
    out_specs.append(pl.BlockSpec(memory_space=pltpu.VMEM))
    res = pl.pallas_call(
        body, name=name, out_shape=tuple(out_shape), in_specs=(HBM,) * (2 * n) + (ANY,) * len(extra),
        out_specs=tuple(out_specs), input_output_aliases=aliases,
        compiler_params=pltpu.CompilerParams(has_side_effects=EFFECT),
    )(*[pltpu.with_memory_space_constraint(src, pltpu.HBM) for src in srcs],
      *[pltpu.with_memory_space_constraint(lax.empty(shp, src.dtype), pltpu.HBM) for shp, src in zip(land_shapes, srcs)],
      *extra)
    return [tuple(res[4 * k:4 * k + 4]) for k in range(n)], res[4 * n]


def _stage1_peer(i):
    x, y, c = lax.axis_index("x"), lax.axis_index("y"), lax.axis_index("c")
    if i == 0:
        return (x, y, 1 - c)
    return (1 - x if i & 1 else x, 1 - y if i & 2 else y, c)


def _slot_of(peer):
    return 4 * peer[0] + 2 * peer[1] + peer[2]


def _stage1_copy(i, src_ref, land_ref, send_sems, recv_sems):
    me = _slot_of((lax.axis_index("x"), lax.axis_index("y"), lax.axis_index("c")))
    return pltpu.make_async_remote_copy(src_ref=src_ref, dst_ref=land_ref.at[me], send_sem=send_sems.at[i],
                                        recv_sem=recv_sems.at[i], device_id=_stage1_peer(i), device_id_type=MESH)


def _stage2_copy(j, land_ref, send_sems, recv_sems):
    slot = _slot_of(_stage1_peer(j + 1))
    return pltpu.make_async_remote_copy(src_ref=land_ref.at[slot], dst_ref=land_ref.at[slot], send_sem=send_sems.at[j],
                                        recv_sem=recv_sems.at[j], device_id=_stage1_peer(0), device_id_type=MESH)


def gather2_start(srcs, name, after=None):
    n = len(srcs)
    extra = () if after is None else (after,)

    def body(*refs):
        src_refs, land_refs = refs[0:n], refs[n:2 * n]
        outs = refs[2 * n + len(extra):]
        for k in range(n):
            for i in range(4):
                _stage1_copy(i, src_refs[k], land_refs[k], outs[4 * k], outs[4 * k + 1]).start()
        outs[4 * n][...] = jnp.zeros((8, 128), F32)

    out_shape, out_specs, aliases = [], [], {}
    for k, src in enumerate(srcs):
        out_shape += [pltpu.SemaphoreType.DMA((4,)), pltpu.SemaphoreType.DMA((4,)),
                      pltpu.HBM(src.shape, src.dtype), pltpu.HBM((N_DEV,) + src.shape, src.dtype)]
        out_specs += [SEM, SEM, HBM, HBM]
        aliases[k] = 4 * k + 2
        aliases[n + k] = 4 * k + 3
    out_shape.append(jax.ShapeDtypeStruct((8, 128), F32))
    out_specs.append(pl.BlockSpec(memory_space=pltpu.VMEM))
    res = pl.pallas_call(
        body, name=name, out_shape=tuple(out_shape), in_specs=(HBM,) * (2 * n) + (ANY,) * len(extra),
        out_specs=tuple(out_specs), input_output_aliases=aliases,
        compiler_params=pltpu.CompilerParams(has_side_effects=EFFECT),
    )(*[pltpu.with_memory_space_constraint(src, pltpu.HBM) for src in srcs],
      *[pltpu.with_memory_space_constraint(lax.empty((N_DEV,) + src.shape, src.dtype), pltpu.HBM) for src in srcs], *extra)
    return [dict(send1=res[4 * k], recv1=res[4 * k + 1], src=res[4 * k + 2], land=res[4 * k + 3]) for k in range(n)], \
        res[4 * n]


def gather2_pass(handles, after, name):
    n = len(handles)

    def body(*refs):
        src_refs, land_refs, recv1 = refs[0:n], refs[n:2 * n], refs[2 * n:3 * n]
        outs = refs[3 * n + 1:]
        for k in range(n):
            for j in range(3):
                _stage1_copy(j + 1, src_refs[k], land_refs[k], recv1[k], recv1[k]).wait_recv()
                _stage2_copy(j, land_refs[k], outs[3 * k], outs[3 * k + 1]).start()

    out_shape, out_specs, aliases = [], [], {}
    for k, h in enumerate(handles):
        out_shape += [pltpu.SemaphoreType.DMA((3,)), pltpu.SemaphoreType.DMA((3,)), pltpu.HBM(h["land"].shape, h["land"].dtype)]
        out_specs += [SEM, SEM, HBM]
        aliases[n + k] = 3 * k + 2
    res = pl.pallas_call(
        body, name=name, out_shape=tuple(out_shape), in_specs=(HBM,) * (2 * n) + (SEM,) * n + (ANY,),
        out_specs=tuple(out_specs), input_output_aliases=aliases,
        compiler_params=pltpu.CompilerParams(has_side_effects=EFFECT),
    )(*[h["src"] for h in handles], *[h["land"] for h in handles], *[h["recv1"] for h in handles], after)
    return [dict(h, send2=res[3 * k], recv2=res[3 * k + 1], land=res[3 * k + 2]) for k, h in enumerate(handles)]


def gather2_wait(h, after, name):
    def body(src_ref, land_ref, send1, recv1, send2, recv2, after_ref, src_dead, got_ref):
        for i in range(4):
            _stage1_copy(i, src_ref, land_ref, send1, recv1).wait_send()
        _stage1_copy(0, src_ref, land_ref, send1, recv1).wait_recv()
        for j in range(3):
            cp = _stage2_copy(j, land_ref, send2, recv2)
            cp.wait_send()
            cp.wait_recv()

    return pl.pallas_call(
        body, name=name,
        out_shape=(pltpu.HBM(h["src"].shape, h["src"].dtype), pltpu.HBM(h["land"].shape, h["land"].dtype)),
        in_specs=(HBM, HBM, SEM, SEM, SEM, SEM, ANY), out_specs=(HBM, HBM), input_output_aliases={0: 0, 1: 1},
        compiler_params=pltpu.CompilerParams(has_side_effects=EFFECT),
    )(h["src"], h["land"], h["send1"], h["recv1"], h["send2"], h["recv2"], after)[1]


def exchange_wait(handles, after, gather, name):
    send_sems, recv_sems, src_thru, land_thru = handles

    def body(src_ref, land_ref, send_sems, recv_sems, after_ref, src_dead, got_ref):
        for cp in _peer_copies(src_ref, land_ref, send_sems, recv_sems, gather):
            cp.wait_send()
            cp.wait_recv()

    return pl.pallas_call(
        body, name=name,
        out_shape=(pltpu.HBM(src_thru.shape, src_thru.dtype), pltpu.HBM(land_thru.shape, land_thru.dtype)),
        in_specs=(HBM, HBM, SEM, SEM, ANY), out_specs=(HBM, HBM), input_output_aliases={0: 0, 1: 1},
        compiler_params=pltpu.CompilerParams(has_side_effects=EFFECT),
    )(src_thru, land_thru, send_sems, recv_sems, after)


def local_step(x, tgt, mod, started, get_w, put_grad, wc, wf, g_mix, bc, lg, lb, gco, gao, g_ffn, bf, g_fin):
    w_in = get_w("w_in", mod)
    proj, h1 = rms_mod_matmul(x, g_mix, mod, 0, 1, w_in, D_IN // N_DEV, "proj_fwd", after=started)
    mix_a, u1 = conv_module_fwd(proj, wc, bc, lg, lb, gco, "conv_module_fwd")
    att, lse = attn_fwd_all(proj, "attn_fwd")
    w_out = get_w("w_out", att)
    y1, mixed = norm_concat_matmul(mix_a, att, gao, w_out, "out_proj_fwd")
    w_up = get_w("w_up", y1)
    up0, x1, h2 = resid_rms_mod_matmul(x, y1, g_ffn, mod, 2, 3, 4, w_up, "up_fwd")
    act = ffn_act_fwd(up0, wf, bf, "ffn_act_fwd")
    w_down = get_w("w_down", act)
    loss_t, dx2, dy2, d_gfin, d_gaf = matmul_loss_bwd(act, w_down, x1, tgt, g_fin, mod, 5, "down_fwd_loss")
    dact = matmul(dy2, w_down, "nt", F32, 512, FFN_TN, "down_bwd_x")
    dw_down = matmul(act, dy2, "tn", BF16, 256, D_MODEL, "down_bwd_w")
    dup0, dbf_g, dbf_v, dwf_g, dwf_v = ffn_bwd(up0, dact, wf, bf, "ffn_bwd", after=put_grad("w_down", dw_down))
    dw_up = matmul_tn_halves(dup0, h2, 256, "up_bwd_w")
    dx1, d_shf, d_scf, d_gffn, dy1, d_gam = matmul_rms_mod_bwd(
        dup0, w_up, x1, dx2, g_ffn, mod, 4, y1, 2, "up_bwd_x", after=put_grad("w_up", dw_up))
    dw_out = matmul(mixed, dy1, "tn", BF16, 256, D_MODEL, "out_proj_bwd_w")
    dmixed, do, dd, d_gao = matmul_combine_bwd(dy1, w_out, att, gao, "out_proj_bwd_x", after=put_grad("w_out", dw_out))
    dqkv = attn_bwd_all(proj, do, lse, dd, "attn_bwd")
    du1, d_gco, d_lg, d_lb, d_bc, d_wc = conv_module_bwd_a(proj, u1, dmixed, lg, lb, gco, "conv_module_bwd_a")
    dproj_a = conv_module_bwd_b(proj, du1, wc, "conv_module_bwd_b")
    dw_in = matmul_tn_pieces(dproj_a, dqkv, h1, "proj_bwd_w")
    dx, d_shm, d_scm, d_gmix = matmul_rms_mod_bwd(
        (dproj_a, dqkv), w_in, x, dx1, g_mix, mod, 1, None, 0, "proj_bwd_x", b_rows=D_IN // N_DEV,
        after=put_grad("w_in", dw_in))
    dmod = jnp.concatenate([d_shm, d_scm, d_gam, d_shf, d_scf, d_gaf], axis=1)
    small = dict(g_norm_mix=d_gmix, b_conv_dw=d_bc, ln_conv_g=d_lg, ln_conv_b=d_lb, g_conv_out=d_gco, g_attn_out=d_gao,
                 g_norm_ffn=d_gffn, b_ffn_dw=jnp.concatenate([dbf_g, dbf_v], axis=1), g_final=d_gfin,
                 w_conv_dw=d_wc, w_ffn_dw=jnp.concatenate([dwf_g, dwf_v], axis=1), dmod=dmod, loss=loss_t[0:1, 0:1])
    return dx, small


def _padw(a, width):
    return jnp.pad(a, ((0, 0), (0, width - a.shape[1])))


def pack_small(t):
    wide = jnp.concatenate([_padw(t["dmod"], PACK_W), _padw(t["b_ffn_dw"], PACK_W), _padw(t["w_ffn_dw"], PACK_W),
                            _padw(t["loss"], PACK_W), jnp.zeros((2, PACK_W), F32)], axis=0)
    z512 = jnp.zeros((1, 512), F32)
    narrow = jnp.concatenate([
        t["g_norm_mix"], t["g_norm_ffn"], t["g_final"],
        jnp.concatenate([t["b_conv_dw"], t["ln_conv_g"]], axis=1),
        jnp.concatenate([t["ln_conv_b"], t["g_conv_out"]], axis=1),
        jnp.concatenate([t["g_attn_out"], z512], axis=1),
        jnp.zeros((2, 1024), F32),
        jnp.pad(t["w_conv_dw"], ((0, 1), (0, 0))).reshape(16, 1024)], axis=0)
    return jnp.concatenate([wide, narrow.reshape(4, PACK_W), jnp.zeros((4, PACK_W), F32)], axis=0)


_NARROW = lambda k, off=0: (8 + k // 6, (k % 6) * 1024 + off)
PACKED_AT = dict(
    b_ada=(0, 0, N_MOD * D_MODEL), b_ffn_dw=(1, 0, 2 * D_FF),
    g_norm_mix=_NARROW(0) + (D_MODEL,), g_norm_ffn=_NARROW(1) + (D_MODEL,), g_final=_NARROW(2) + (D_MODEL,),
    b_conv_dw=_NARROW(3) + (D_CONV,), ln_conv_g=_NARROW(3, 512) + (D_CONV,),
    ln_conv_b=_NARROW(4) + (D_CONV,), g_conv_out=_NARROW(4, 512) + (D_CONV,), g_attn_out=_NARROW(5) + (D_ATTN,))
SMALL_ORDER = list(PACKED_AT)


def small_adamw(parts, wmv, name):
    def body(*refs):
        p_ref = refs[0]
        ins = refs[1:1 + 3 * len(SMALL_ORDER)]
        outs = refs[1 + 3 * len(SMALL_ORDER):]
        g = p_ref[0]
        for k in range(1, N_DEV):
            g = g + p_ref[k]
        for i, n in enumerate(SMALL_ORDER):
            row, lane, width = PACKED_AT[n]
            gp = g[row:row + 1, lane:lane + width]
            w_ref, m_ref, v_ref = ins[3 * i:3 * i + 3]
            g_ref, d_ref, nm_ref, nv_ref = outs[4 * i:4 * i + 4]
            g_ref[...] = gp
            d_ref[...], nm_ref[...], nv_ref[...] = _adam(w_ref[...], gp, m_ref[...], v_ref[...])
        wc_ref, wf_ref, loss_ref = outs[4 * len(SMALL_ORDER):]
        for j in range(CONV_K):
            row, lane = _NARROW(8 + j // 2, (j % 2) * 512)
            wc_ref[j:j + 1, :] = g[row:row + 1, lane:lane + D_CONV]
        wf_ref[...] = g[2:2 + FFN_K, 0:2 * D_FF]
        loss_ref[...] = jnp.broadcast_to(g[5:6, 0:1], (8, 128))

    args, out_shape = [parts], []
    for n in SMALL_ORDER:
        args += list(wmv[n])
        out_shape += [jax.ShapeDtypeStruct(wmv[n][0].shape, F32)] * 4
    out_shape += [jax.ShapeDtypeStruct((CONV_K, D_CONV), F32), jax.ShapeDtypeStruct((FFN_K, 2 * D_FF), F32),
                  jax.ShapeDtypeStruct((8, 128), F32)]
    res = pl.pallas_call(body, out_shape=tuple(out_shape), name=name, compiler_params=_cp())(*args)
    per = {n: tuple(res[4 * i:4 * i + 4]) for i, n in enumerate(SMALL_ORDER)}
    return per, res[-3], res[-2], res[-1][0, 0]


def shard_adamw(items, name):
    def body(*refs):
        ins, outs = refs[:4 * len(items)], refs[4 * len(items):]
        for i in range(len(items)):
            g_ref, w_ref, m_ref, v_ref = ins[4 * i:4 * i + 4]
            og_ref, d_ref, nm_ref, nv_ref = outs[4 * i:4 * i + 4]
            og_ref[...] = g_ref[...]
            d_ref[...], nm_ref[...], nv_ref[...] = _adam(w_ref[...], g_ref[...], m_ref[...], v_ref[...])

    args = [a for item in items for a in item]
    out_shape = tuple(jax.ShapeDtypeStruct(item[1].shape, F32) for item in items for _ in range(4))
    res = pl.pallas_call(body, out_shape=out_shape, name=name, compiler_params=_cp())(*args)
    return [tuple(res[4 * i:4 * i + 4]) for i in range(len(items))]


def _shard(full, n_cols, me):
    return lax.dynamic_slice(full, (0, me * n_cols), (full.shape[0], n_cols))


WEIGHTS = ["w_ada", "b_ada", "g_norm_mix", "w_in", "w_conv_dw", "b_conv_dw", "ln_conv_g", "ln_conv_b", "g_conv_out",
           "g_attn_out", "w_out", "g_norm_ffn", "w_up", "w_ffn_dw", "b_ffn_dw", "w_down", "g_final"]


def kernel(x, c, w_ada, b_ada, g_norm_mix, w_in, w_conv_dw, b_conv_dw, ln_conv_g, ln_conv_b, g_conv_out, g_attn_out, w_out, g_norm_ffn, w_up, w_ffn_dw, b_ffn_dw, w_down, g_final, loss_target, m_w_ada, m_b_ada, m_g_norm_mix, m_w_in, m_w_conv_dw, m_b_conv_dw, m_ln_conv_g, m_ln_conv_b, m_g_conv_out, m_g_attn_out, m_w_out, m_g_norm_ffn, m_w_up, m_w_ffn_dw, m_b_ffn_dw, m_w_down, m_g_final, v_w_ada, v_b_ada, v_g_norm_mix, v_w_in, v_w_conv_dw, v_b_conv_dw, v_ln_conv_g, v_ln_conv_b, v_g_conv_out, v_g_attn_out, v_w_out, v_g_norm_ffn, v_w_up, v_w_ffn_dw, v_b_ffn_dw, v_w_down, v_g_final):
    args = dict(locals())
    me = 4 * lax.axis_index("x") + 2 * lax.axis_index("y") + lax.axis_index("c")
    me1 = me.astype(jnp.int32).reshape(1)

    def flat(name, prefix=""):
        a = args[prefix + name]
        return a.reshape(a.shape[-2] if a.ndim > 1 else 1, a.shape[-1])

    def flat_t(name, prefix=""):
        return args[prefix + name][0].T

    n_in, n_up, r_out, r_down = w_in.shape[2], w_up.shape[2], w_out.shape[1], w_down.shape[1]
    n_ada, n_wc, n_wf = w_ada.shape[2], w_conv_dw.shape[2], w_ffn_dw.shape[2]
    taps_c = jnp.pad(flat("w_conv_dw").reshape(1, CONV_K * n_wc), ((0, 0), (0, 2 * D_MODEL - CONV_K * n_wc)))
    taps_f = jnp.pad(flat("w_ffn_dw").reshape(1, FFN_K * n_wf), ((0, 0), (0, 3 * D_MODEL - FFN_K * n_wf)))
    first = jnp.concatenate([c, taps_c.reshape(2, D_MODEL), taps_f.reshape(3, D_MODEL), jnp.zeros((2, D_MODEL), F32)], axis=0)
    w_in_block = flat_t("w_in").astype(BF16)
    hi = lax.reduce_precision(first, 8, 7)
    mid = lax.reduce_precision(first - hi, 8, 7)
    low = lax.reduce_precision(first - hi - mid, 8, 7)
    terms = jnp.concatenate([hi, mid, low, jnp.zeros((8, D_MODEL), F32)], axis=0).astype(BF16)
    first_block = all_gather(jnp.concatenate([w_in_block, terms], axis=0), "gather_c_taps_w_in")
    terms = first_block[:, n_in:n_in + 24, :].astype(F32)
    first_all = (terms[:, 0:8] + terms[:, 8:16]) + terms[:, 16:24]
    c_all = first_all[:, 0, :]
    wc_full = first_all[:, 1:3, :].reshape(N_DEV, 2 * D_MODEL)[:, :CONV_K * n_wc].reshape(N_DEV, CONV_K, n_wc)
    wc_full = wc_full.transpose(1, 0, 2).reshape(CONV_K, D_CONV)
    wf_full = first_all[:, 3:6, :].reshape(N_DEV, 3 * D_MODEL)[:, :FFN_K * n_wf].reshape(N_DEV, FFN_K, n_wf)
    wf_full = wf_full.transpose(1, 0, 2).reshape(FFN_K, 2 * D_FF)
    mod_cols = ada_fwd(c_all, flat("w_ada"), _shard(flat("b_ada"), n_ada, me), "ada_fwd")
    mod_all = all_gather(mod_cols, "gather_mod")
    mod = lax.dynamic_index_in_dim(mod_all, me, axis=1, keepdims=False).reshape(N_MOD, D_MODEL)
    mod = jnp.pad(mod, ((0, 2), (0, 0)))

    order = ("w_out", "w_up", "w_down")
    blocks = dict(w_up=flat_t("w_up").astype(BF16), w_out=flat("w_out").astype(BF16), w_down=flat("w_down").astype(BF16))
    handles, tok = gather2_start([blocks[name] for name in order], "gather_weights_start", mod_all)
    gathers = dict(zip(order, handles))

    def gathered(name, after):
        if "send2" not in gathers[name]:
            group = ("w_out", "w_up") if name != "w_down" else ("w_down",)
            gathers.update(zip(group, gather2_pass([gathers[w] for w in group], after, f"gather_{name}_pass")))
        land = gather2_wait(gathers[name], after, f"gather_{name}_wait")
        return lax.dynamic_update_index_in_dim(land, blocks[name], me, axis=0)

    def get_w(name, after):
        if name == "w_in":
            return first_block
        return gathered(name, after).reshape(-1, D_MODEL)

    exchanges = {}

    def put_grad(name, dw, after=None):
        dev_major = dw.reshape(N_DEV, -1, D_MODEL)
        (exchanges[name],), token = exchange_start([dev_major], False, f"exchange_{name}_start", after)
        return token

    grad_x, small = local_step(
        x[0], loss_target[0], mod, tok, get_w, put_grad, wc_full, wf_full,
        flat("g_norm_mix"), flat("b_conv_dw"), flat("ln_conv_g"), flat("ln_conv_b"), flat("g_conv_out"),
        flat("g_attn_out"), flat("g_norm_ffn"), flat("b_ffn_dw"), flat("g_final"))

    out = {}

    def finish(name, tr, after):
        mine, parts = exchange_wait(exchanges[name], after, False, f"exchange_{name}_wait")
        if name in ("w_in", "w_up"):
            res = sum_adamw(parts, mine, me1, flat_t(name), flat_t(name, "m_"), flat_t(name, "v_"), tr, "adamw_" + name)
            out[name] = tuple(r.T for r in res)
        else:
            res = out[name] = sum_adamw(parts, mine, me1, flat(name), flat(name, "m_"), flat(name, "v_"), tr,
                                        "adamw_" + name)
        return res[0]

    after = finish("w_down", r_down, grad_x)
    after = finish("w_up", n_up // 2, after)
    after = finish("w_out", r_out, after)
    after = finish("w_in", n_in, after)

    small_all = all_gather(pack_small(small), "gather_small", after)

    wmv = {n: (flat(n), flat(n, "m_"), flat(n, "v_")) for n in SMALL_ORDER}
    per, g_wc, g_wf, loss = small_adamw(small_all, wmv, "adamw_small")
    out.update(per)
    taps = shard_adamw([(_shard(g_wc, n_wc, me), flat("w_conv_dw"), flat("w_conv_dw", "m_"), flat("w_conv_dw", "v_")),
                        (_shard(g_wf, n_wf, me), flat("w_ffn_dw"), flat("w_ffn_dw", "m_"), flat("w_ffn_dw", "v_"))],
                       "adamw_taps")
    out["w_conv_dw"], out["w_ffn_dw"] = taps

    dmod_cols = _shard(small_all[:, 0, :], n_ada, me)
    out["w_ada"] = ada_bwd_adamw(c_all, dmod_cols, flat("w_ada"), flat("w_ada", "m_"), flat("w_ada", "v_"), "adamw_w_ada")

    result = [loss, grad_x[None]]
    for k in range(4):
        result += [out[n][k].reshape(args[n].shape) for n in WEIGHTS]
    return tuple(result)
```

```python
import functools

import jax
import jax.numpy as jnp
from jax import lax
from jax.experimental import pallas as pl
from jax.experimental.pallas import tpu as pltpu

F32 = jnp.float32
BF16 = jnp.bfloat16

N_DEV = 8
SEQ = 2048
D_MODEL = 1024
D_CONV = 512
D_ATTN = 512
HEAD_DIM = 64
CONV_K = 31
D_FF = 2816
FFN_K = 3
D_IN = 2 * D_CONV + 3 * D_ATTN
N_MOD = 6
EPS = 1e-6
ATTN_BLOCK = 128
PATTERNS = ((2048, 1), (512, 4), (128, 16))
NEG = -1e30

ADAM_LR, ADAM_B1, ADAM_B2, ADAM_EPS, ADAM_WD, ADAM_STEP = 0.001, 0.9, 0.999, 1e-08, 0.01, 10

ROWS = 256
CONV_HALO = 32
FFN_HALO = 8
FFN_TN = 1408
VMEM_LIMIT = 56 * 1024 * 1024
PACK_W = 6144


NT = (((1,), (1,)), ((), ()))
ANY = pl.BlockSpec(memory_space=pl.ANY)


def _cp(*sem):
    return pltpu.CompilerParams(dimension_semantics=sem if sem else None, vmem_limit_bytes=VMEM_LIMIT)


def _with_after(body, n_in, after):
    if after is None:
        return body, [], []
    return (lambda *refs: body(*refs[:n_in], *refs[n_in + 1:])), [ANY], [after]


def _sig(x):
    return 1.0 / (1.0 + jnp.exp(-x))


def _rsum(x):
    return jnp.sum(x, axis=0, keepdims=True)


def _mean(x):
    return jnp.mean(x, axis=-1, keepdims=True)


def _acc(ref, val, first):
    @pl.when(first)
    def _():
        ref[...] = val

    @pl.when(jnp.logical_not(first))
    def _():
        ref[...] += val


SUB = 16


def _for_chunks(fn, unroll=1, rows=ROWS):
    def step(i, carry):
        fn(pl.ds(pl.multiple_of(i * SUB, SUB), SUB))
        return carry

    lax.fori_loop(0, rows // SUB, step, 0, unroll=unroll)


PAIR = 2 * ROWS


def _pair_spec(width, col=0):
    return pl.BlockSpec((PAIR, width), lambda i: (i, col))


def _halves():
    return [slice(h * ROWS, (h + 1) * ROWS) for h in range(2)]


def rms_mod_matmul(x, g, mod, sh_row, sc_row, b, b_rows, name, after=None):
    n = N_DEV * b_rows

    def body(x_ref, g_ref, mod_ref, b_ref, o_ref, h_ref):
        for rs in _halves():
            xx = x_ref[rs, :]
            r = lax.rsqrt(_mean(xx * xx) + EPS)
            h = (xx * r * g_ref[...] * (1.0 + mod_ref[sc_row:sc_row + 1, :]) + mod_ref[sh_row:sh_row + 1, :]).astype(BF16)
            h_ref[rs, :] = h
            o_ref[rs, :] = lax.dot_general(h, b_ref[...].reshape(n, D_MODEL), NT, preferred_element_type=F32)

    body, more_specs, more = _with_after(body, 4, after)
    return pl.pallas_call(
        body, out_shape=(jax.ShapeDtypeStruct((SEQ, n), F32), jax.ShapeDtypeStruct((SEQ, D_MODEL), BF16)),
        grid=(SEQ // PAIR,),
        in_specs=[_pair_spec(D_MODEL), _vec_spec(D_MODEL), _vec_spec(D_MODEL, 8),
                  pl.BlockSpec((N_DEV, b_rows, D_MODEL), lambda i: (0, 0, 0))] + more_specs,
        out_specs=(_pair_spec(n), _pair_spec(D_MODEL)), name=name, compiler_params=_cp("parallel"))(x, g, mod, b, *more)


def norm_concat_matmul(mix_a, att, gao, w, name):
    def body(a_ref, att_ref, g_ref, w_ref, y_ref, mixed_ref):
        for rs in _halves():
            aa = att_ref[rs, :]
            mixed_ref[rs, 0:D_CONV] = a_ref[rs, :]
            mixed_ref[rs, D_CONV:] = (aa * lax.rsqrt(_mean(aa * aa) + EPS) * g_ref[...]).astype(BF16)
            y_ref[rs, :] = jnp.dot(mixed_ref[rs, :], w_ref[...], preferred_element_type=F32)

    return pl.pallas_call(
        body, out_shape=(jax.ShapeDtypeStruct((SEQ, D_MODEL), F32), jax.ShapeDtypeStruct((SEQ, D_MODEL), BF16)),
        grid=(SEQ // PAIR,),
        in_specs=[_pair_spec(D_CONV), _pair_spec(D_ATTN), _vec_spec(D_ATTN), pl.BlockSpec(w.shape, lambda i: (0, 0))],
        out_specs=(_pair_spec(D_MODEL), _pair_spec(D_MODEL)), name=name, compiler_params=_cp("parallel"))(mix_a, att, gao, w)


def resid_rms_mod_matmul(x, y, g, mod, ga_row, sh_row, sc_row, w, name):
    n = w.shape[0]

    def body(x_ref, y_ref, g_ref, mod_ref, w_ref, o_ref, x1_ref, h_ref):
        x1 = x_ref[...] + mod_ref[ga_row:ga_row + 1, :] * y_ref[...]
        x1_ref[...] = x1
        r = lax.rsqrt(_mean(x1 * x1) + EPS)
        h = (x1 * r * g_ref[...] * (1.0 + mod_ref[sc_row:sc_row + 1, :]) + mod_ref[sh_row:sh_row + 1, :]).astype(BF16)
        h_ref[...] = h
        o_ref[...] = lax.dot_general(h, w_ref[...], NT, preferred_element_type=F32)

    return pl.pallas_call(
        body,
        out_shape=(jax.ShapeDtypeStruct((SEQ, n), F32), jax.ShapeDtypeStruct((SEQ, D_MODEL), F32),
                   jax.ShapeDtypeStruct((SEQ, D_MODEL), BF16)),
        grid=(SEQ // ROWS,),
        in_specs=[_row_spec(D_MODEL), _row_spec(D_MODEL), _vec_spec(D_MODEL), _vec_spec(D_MODEL, 8),
                  pl.BlockSpec(w.shape, lambda i: (0, 0))],
        out_specs=(_row_spec(n), _row_spec(D_MODEL), _row_spec(D_MODEL)),
        name=name, compiler_params=_cp("parallel"))(x, y, g, mod, w)


def matmul_combine_bwd(dy, w, att, gao, name, after=None):
    def body(dy_ref, w_ref, att_ref, g_ref, dm_ref, do_ref, dd_ref, dg_ref):
        @pl.when(pl.program_id(0) == 0)
        def _():
            dg_ref[...] = jnp.zeros_like(dg_ref)

        same_head = (jnp.right_shift(lax.broadcasted_iota(jnp.int32, (D_ATTN, D_ATTN), 0), 6)
                     == jnp.right_shift(lax.broadcasted_iota(jnp.int32, (D_ATTN, D_ATTN), 1), 6)).astype(F32)
        for rs in _halves():
            dmixed = lax.dot_general(dy_ref[rs, :], w_ref[...], NT, preferred_element_type=F32)
            dm_ref[rs, :] = dmixed
            att = att_ref[rs, :]
            r = lax.rsqrt(_mean(att * att) + EPS)
            xn = att * r
            dm = dmixed[:, D_CONV:]
            dg_ref[...] += _rsum(dm * xn)
            dyn = dm * g_ref[...]
            do = r * (dyn - xn * _mean(dyn * xn))
            do_ref[rs, :] = do
            dd_ref[rs, :] = jnp.dot(do * att, same_head, preferred_element_type=F32, precision=lax.Precision.HIGHEST)

    rs = _pair_spec(D_ATTN)
    f = jax.ShapeDtypeStruct((SEQ, D_ATTN), F32)
    body, more_specs, more = _with_after(body, 4, after)
    return pl.pallas_call(
        body, out_shape=(jax.ShapeDtypeStruct((SEQ, D_MODEL), F32), f, f, jax.ShapeDtypeStruct((1, D_ATTN), F32)),
        grid=(SEQ // PAIR,),
        in_specs=[_pair_spec(D_MODEL), pl.BlockSpec(w.shape, lambda i: (0, 0)), rs, _vec_spec(D_ATTN)] + more_specs,
        out_specs=(_pair_spec(D_MODEL), rs, rs, _vec_spec(D_ATTN)),
        name=name, compiler_params=_cp("arbitrary"))(dy, w, att, gao, *more)


def matmul_loss_bwd(act, w, x1, tgt, g, mod, ga_row, name):
    def body(a_ref, w_ref, x1_ref, t_ref, g_ref, mod_ref, loss_ref, dx2_ref, dy2_ref, dg_ref, dga_ref):
        @pl.when(pl.program_id(0) == 0)
        def _():
            loss_ref[...] = jnp.zeros_like(loss_ref)
            dg_ref[...] = jnp.zeros_like(dg_ref)
            dga_ref[...] = jnp.zeros_like(dga_ref)

        ga = mod_ref[ga_row:ga_row + 1, :]
        for rs in _halves():
            y2 = jnp.dot(a_ref[rs, :], w_ref[...], preferred_element_type=F32)
            x2 = x1_ref[rs, :] + ga * y2
            r = lax.rsqrt(_mean(x2 * x2) + EPS)
            xn = x2 * r
            err = xn * g_ref[...] - t_ref[rs, :]
            loss_ref[...] += jnp.broadcast_to(0.5 * jnp.sum(_mean(err * err)), (8, 128))
            dy = err * (1.0 / D_MODEL)
            dg_ref[...] += _rsum(dy * xn)
            dxn = dy * g_ref[...]
            dx2 = r * (dxn - xn * _mean(dxn * xn))
            dx2_ref[rs, :] = dx2
            dy2_ref[rs, :] = (dx2 * ga).astype(BF16)
            dga_ref[...] += _rsum(dx2 * y2)

    vec = jax.ShapeDtypeStruct((1, D_MODEL), F32)
    rows = _pair_spec
    return pl.pallas_call(
        body,
        out_shape=(jax.ShapeDtypeStruct((8, 128), F32), jax.ShapeDtypeStruct((SEQ, D_MODEL), F32),
                   jax.ShapeDtypeStruct((SEQ, D_MODEL), BF16), vec, vec),
        grid=(SEQ // PAIR,),
        in_specs=[rows(act.shape[1]), pl.BlockSpec(w.shape, lambda i: (0, 0)), rows(D_MODEL), rows(D_MODEL),
                  _vec_spec(D_MODEL), _vec_spec(D_MODEL, 8)],
        out_specs=(pl.BlockSpec((8, 128), lambda i: (0, 0)), rows(D_MODEL), rows(D_MODEL),
                   _vec_spec(D_MODEL), _vec_spec(D_MODEL)),
        name=name, compiler_params=_cp("arbitrary"))(act, w, x1, tgt, g, mod)


def matmul_rms_mod_bwd(a, b, x, dres, g, mod, sc_row, y, ga_row, name, b_rows=None, after=None):
    gated = y is not None
    pieces = isinstance(a, tuple)
    tm = PAIR if pieces else ROWS
    blocks = [slice(h * ROWS, (h + 1) * ROWS) for h in range(tm // ROWS)]
    rows = lambda width: pl.BlockSpec((tm, width), lambda i: (i, 0))
    if pieces:
        a1, a3 = a
        a_args = [a1, a3]
        a_specs = [rows(a1.shape[1]), pl.BlockSpec((3, tm, a3.shape[2]), lambda i: (0, i, 0))]
        b_arg, b_spec = b, pl.BlockSpec((N_DEV, b_rows, D_MODEL), lambda i: (0, 0, 0))
    else:
        k2 = a.shape[2]
        a_args = [a]
        a_specs = [pl.BlockSpec((2, tm, k2), lambda i: (0, i, 0))]
        b_arg, b_spec = b.reshape(2, k2, D_MODEL), pl.BlockSpec((2, k2, D_MODEL), lambda i: (0, 0, 0))
    n_a = len(a_args)

    def body(*refs):
        a_refs, (b_ref, x_ref, dres_ref, g_ref, mod_ref) = refs[:n_a], refs[n_a:n_a + 5]
        if gated:
            y_ref, dx_ref, dsh_ref, dsc_ref, dg_ref, dy_ref, dga_ref = refs[n_a + 5:]
        else:
            dx_ref, dsh_ref, dsc_ref, dg_ref = refs[n_a + 5:]

        @pl.when(pl.program_id(0) == 0)
        def _():
            for ref in (dsh_ref, dsc_ref, dg_ref) + ((dga_ref,) if gated else ()):
                ref[...] = jnp.zeros_like(ref)

        gg = g_ref[...]
        for rs in blocks:
            if pieces:
                wv = b_ref[...].reshape(N_DEV * b_rows, D_MODEL)
                k1, k3 = a_refs[0].shape[1], a_refs[1].shape[2]
                dh = jnp.dot(a_refs[0][rs, :], wv[0:k1], preferred_element_type=F32)
                for t in range(3):
                    dh = dh + jnp.dot(a_refs[1][t, rs, :], wv[k1 + t * k3:k1 + (t + 1) * k3], preferred_element_type=F32)
            else:
                dh = (jnp.dot(a_refs[0][0, rs, :], b_ref[0], preferred_element_type=F32)
                      + jnp.dot(a_refs[0][1, rs, :], b_ref[1], preferred_element_type=F32))
            xx = x_ref[rs, :]
            r = lax.rsqrt(_mean(xx * xx) + EPS)
            xn = xx * r
            dsh_ref[...] += _rsum(dh)
            dsc_ref[...] += _rsum(dh * (xn * gg))
            dt = dh * (1.0 + mod_ref[sc_row:sc_row + 1, :])
            dg_ref[...] += _rsum(dt * xn)
            dxn = dt * gg
            dx = dres_ref[rs, :] + r * (dxn - xn * _mean(dxn * xn))
            dx_ref[rs, :] = dx
            if gated:
                dga_ref[...] += _rsum(dx * y_ref[rs, :])
                dy_ref[rs, :] = (dx * mod_ref[ga_row:ga_row + 1, :]).astype(BF16)

    vec = jax.ShapeDtypeStruct((1, D_MODEL), F32)
    in_specs = a_specs + [b_spec, rows(D_MODEL), rows(D_MODEL), _vec_spec(D_MODEL), _vec_spec(D_MODEL, 8)]
    out_shape = [jax.ShapeDtypeStruct((SEQ, D_MODEL), F32), vec, vec, vec]
    out_specs = [rows(D_MODEL), _vec_spec(D_MODEL), _vec_spec(D_MODEL), _vec_spec(D_MODEL)]
    args = a_args + [b_arg, x, dres, g, mod]
    if gated:
        in_specs.append(rows(D_MODEL))
        out_shape += [jax.ShapeDtypeStruct((SEQ, D_MODEL), BF16), vec]
        out_specs += [rows(D_MODEL), _vec_spec(D_MODEL)]
        args.append(y)
    body, more_specs, more = _with_after(body, len(args), after)
    return pl.pallas_call(
        body, out_shape=tuple(out_shape), grid=(SEQ // tm,), in_specs=in_specs + more_specs, out_specs=tuple(out_specs),
        name=name, compiler_params=_cp("arbitrary"))(*args, *more)


def _prev_halo(halo, width, col):
    per = ROWS // halo
    return pl.BlockSpec((halo, width), lambda i: (jnp.maximum(i * per - 1, 0), col))


def _next_halo(halo, width, col):
    per = ROWS // halo
    last = SEQ // halo - 1
    return pl.BlockSpec((halo, width), lambda i: (jnp.minimum((i + 1) * per, last), col))


CONV_PAD = ROWS + CONV_HALO


def _shift_copies(sh):
    for b in range(1, 8):
        sh[b, 0:CONV_PAD - 8, :] = sh[0, pl.ds(b, CONV_PAD - 8), :]


def _tap(sh, rs_start, offset):
    return sh[offset % 8, pl.ds(pl.multiple_of(rs_start + (offset // 8) * 8, 8), SUB), :]


def _conv_glu(av_ref, ag_ref, avh_ref, agh_ref, sh):
    i = pl.program_id(0)
    hv = avh_ref[...] * _sig(agh_ref[...])
    sh[0, 0:CONV_HALO, :] = jnp.where(i > 0, hv, 0.0)

    def glu(rs):
        sh[0, pl.ds(pl.multiple_of(rs.start + CONV_HALO, SUB), SUB), :] = av_ref[rs, :] * _sig(ag_ref[rs, :])

    _for_chunks(glu)
    _shift_copies(sh)


def _conv_norm(u1, lg_ref, lb_ref):
    mu = _mean(u1)
    cen = u1 - mu
    rs = lax.rsqrt(_mean(cen * cen) + EPS)
    z = cen * rs
    ln = z * lg_ref[...] + lb_ref[...]
    s = _sig(ln)
    return z, rs, ln, s, ln * s


def conv_module_fwd(proj, wc, bc, lg, lb, gco, name):
    def body(av_ref, ag_ref, avh_ref, agh_ref, wc_ref, bc_ref, lg_ref, lb_ref, gco_ref, out_ref, u1_ref, sh):
        _conv_glu(av_ref, ag_ref, avh_ref, agh_ref, sh)

        def conv(rs):
            u1 = jnp.broadcast_to(bc_ref[...], (SUB, D_CONV))
            for j in range(CONV_K):
                u1 = u1 + wc_ref[j:j + 1, :] * _tap(sh, rs.start, CONV_HALO - (CONV_K - 1) + j)
            u1_ref[rs, :] = u1

        _for_chunks(conv)
        _, _, _, _, u2 = _conv_norm(u1_ref[...], lg_ref, lb_ref)
        rc = lax.rsqrt(_mean(u2 * u2) + EPS)
        out_ref[...] = (u2 * rc * gco_ref[...]).astype(BF16)

    v = _vec_spec(D_CONV)
    return pl.pallas_call(
        body, out_shape=(jax.ShapeDtypeStruct((SEQ, D_CONV), BF16), jax.ShapeDtypeStruct((SEQ, D_CONV), F32)),
        grid=(SEQ // ROWS,),
        in_specs=[_row_spec(D_CONV, 0), _row_spec(D_CONV, 1), _prev_halo(CONV_HALO, D_CONV, 0),
                  _prev_halo(CONV_HALO, D_CONV, 1), _vec_spec(D_CONV, CONV_K), v, v, v, v],
        out_specs=(_row_spec(D_CONV), _row_spec(D_CONV)), scratch_shapes=[pltpu.VMEM((8, CONV_PAD, D_CONV), F32)],
        name=name, compiler_params=_cp("parallel"))(proj, proj, proj, proj, wc, bc, lg, lb, gco)


def conv_module_bwd_a(proj, u1, dmixed, lg, lb, gco, name):
    def body(av_ref, ag_ref, avh_ref, agh_ref, u1_ref, dm_ref, lg_ref, lb_ref, gco_ref,
             du1_ref, dgco_ref, dlg_ref, dlb_ref, dbc_ref, dwc_ref, sh, acc):
        first = pl.program_id(0) == 0
        _conv_glu(av_ref, ag_ref, avh_ref, agh_ref, sh)
        z, rs, ln, s, u2 = _conv_norm(u1_ref[...], lg_ref, lb_ref)
        rc = lax.rsqrt(_mean(u2 * u2) + EPS)
        xn = u2 * rc
        dm = dm_ref[...]
        _acc(dgco_ref, _rsum(dm * xn), first)
        dyn = dm * gco_ref[...]
        du2 = rc * (dyn - xn * _mean(dyn * xn))
        dln = du2 * (s * (1.0 + ln * (1.0 - s)))
        _acc(dlg_ref, _rsum(dln * z), first)
        _acc(dlb_ref, _rsum(dln), first)
        dz = dln * lg_ref[...]
        du1 = rs * (dz - _mean(dz) - z * _mean(dz * z))
        du1_ref[...] = du1
        _acc(dbc_ref, _rsum(du1), first)
        acc[...] = jnp.zeros_like(acc)

        def taps(rs):
            d = du1_ref[rs, :]
            for j in range(CONV_K):
                acc[j] += d * _tap(sh, rs.start, CONV_HALO - (CONV_K - 1) + j)

        _for_chunks(taps)

        @pl.when(first)
        def _():
            dwc_ref[...] = jnp.zeros_like(dwc_ref)

        for j in range(CONV_K):
            dwc_ref[j:j + 1, :] += _rsum(acc[j])

    v = _vec_spec(D_CONV)
    vec = jax.ShapeDtypeStruct((1, D_CONV), F32)
    return pl.pallas_call(
        body,
        out_shape=(jax.ShapeDtypeStruct((SEQ, D_CONV), F32), vec, vec, vec, vec, jax.ShapeDtypeStruct((CONV_K, D_CONV), F32)),
        grid=(SEQ // ROWS,),
        in_specs=[_row_spec(D_CONV, 0), _row_spec(D_CONV, 1), _prev_halo(CONV_HALO, D_CONV, 0),
                  _prev_halo(CONV_HALO, D_CONV, 1), _row_spec(D_CONV, 0), _row_spec(D_CONV, 0), v, v, v],
        out_specs=(_row_spec(D_CONV), v, v, v, v, _vec_spec(D_CONV, CONV_K)),
        scratch_shapes=[pltpu.VMEM((8, CONV_PAD, D_CONV), F32), pltpu.VMEM((CONV_K, SUB, D_CONV), F32)],
        name=name, compiler_params=_cp("arbitrary"))(proj, proj, proj, proj, u1, dmixed, lg, lb, gco)


def conv_module_bwd_b(proj, du1, wc, name):
    def body(av_ref, ag_ref, du1_ref, du1n_ref, wc_ref, out_ref, sh):
        i = pl.program_id(0)
        sh[0, 0:ROWS, :] = du1_ref[...]
        sh[0, ROWS:, :] = jnp.where(i < SEQ // ROWS - 1, du1n_ref[...], 0.0)
        _shift_copies(sh)

        def chunk(rs):
            du0 = jnp.zeros((SUB, D_CONV), F32)
            for j in range(CONV_K):
                du0 = du0 + wc_ref[j:j + 1, :] * _tap(sh, rs.start, CONV_K - 1 - j)
            sg = _sig(ag_ref[rs, :])
            out_ref[rs, 0:D_CONV] = (du0 * sg).astype(BF16)
            out_ref[rs, D_CONV:] = (du0 * av_ref[rs, :] * sg * (1.0 - sg)).astype(BF16)

        _for_chunks(chunk)

    return pl.pallas_call(
        body, out_shape=jax.ShapeDtypeStruct((SEQ, 2 * D_CONV), BF16), grid=(SEQ // ROWS,),
        in_specs=[_row_spec(D_CONV, 0), _row_spec(D_CONV, 1), _row_spec(D_CONV, 0), _next_halo(CONV_HALO, D_CONV, 0),
                  _vec_spec(D_CONV, CONV_K)],
        out_specs=_row_spec(2 * D_CONV), scratch_shapes=[pltpu.VMEM((8, CONV_PAD, D_CONV), F32)],
        name=name, compiler_params=_cp("parallel"))(proj, proj, du1, du1, wc)


def _rows(start, size, r):
    return pl.ds(start, size) if r == 1 else pl.ds(start, size, stride=r)


def _unit_rows(r, rho, n, nb):
    win = 2 * ATTN_BLOCK if nb > 1 else ATTN_BLOCK
    if isinstance(n, int):
        kb = max(n - 1, 0)
        q_rows = _rows(rho + r * ATTN_BLOCK * n, ATTN_BLOCK, r)
        k_rows = _rows(rho + r * ATTN_BLOCK * kb, win, r)
    else:
        kb = jnp.maximum(n - 1, 0)
        q_rows = pl.ds(pl.multiple_of(n * ATTN_BLOCK, ATTN_BLOCK), ATTN_BLOCK)
        k_rows = pl.ds(pl.multiple_of(kb * ATTN_BLOCK, ATTN_BLOCK), win)
    dist = (n - kb) * ATTN_BLOCK + lax.broadcasted_iota(jnp.int32, (ATTN_BLOCK, win), 0) \
        - lax.broadcasted_iota(jnp.int32, (ATTN_BLOCK, win), 1)
    return q_rows, k_rows, (dist >= 0) & (dist <= ATTN_BLOCK)


def _per_head(x):
    lane = lax.broadcasted_iota(jnp.int32, x.shape, 1)
    zero = jnp.zeros_like(x)
    return [jnp.where(lane < HEAD_DIM, x, zero), jnp.where(lane >= HEAD_DIM, x, zero)]


def _masked_scores(q2, k2, valid):
    return [jnp.where(valid, lax.dot_general(qh, k2, NT, preferred_element_type=F32) * (HEAD_DIM ** -0.5), NEG)
            for qh in _per_head(q2)]


def _attn_units(r, nb, unit):
    if r == 1:
        def four(i, carry):
            for k in range(4):
                unit(0, 4 * i + k)
            return carry
        lax.fori_loop(0, nb // 4, four, 0)
    else:
        for rho in range(r):
            for n in range(nb):
                unit(rho, n)


N_UNITS = 16


def attn_fwd_all(proj, name):
    def body(q_ref, k_ref, v_ref, att_ref, lse_ref, s_scr, p_scr, lse_scr, den_scr):
        for idx, (sub_len, r) in enumerate(PATTERNS):
            nb = sub_len // ATTN_BLOCK
            win = 2 * ATTN_BLOCK if nb > 1 else ATTN_BLOCK

            def scores(rho, n, r=r, nb=nb, win=win):
                u = rho * nb + n
                q_rows, k_rows, valid = _unit_rows(r, rho, n, nb)
                ss = _masked_scores(q_ref[q_rows, :].astype(BF16), k_ref[k_rows, :].astype(BF16), valid)
                for h in range(2):
                    s_scr[2 * u + h, :, 0:win] = ss[h]

            _attn_units(r, nb, scores)

            def softmax(u, carry, win=win):
                lses, dens = [], []
                for h in range(2):
                    sc = s_scr[2 * u + h, :, 0:win]
                    m = jnp.max(sc, axis=1, keepdims=True)
                    p = jnp.exp(sc - m)
                    den = jnp.sum(p, axis=1, keepdims=True)
                    p_scr[2 * u + h, :, 0:win] = p.astype(BF16)
                    lses.append(jnp.broadcast_to(m + jnp.log(den), (ATTN_BLOCK, HEAD_DIM)))
                    dens.append(jnp.broadcast_to(den, (ATTN_BLOCK, HEAD_DIM)))
                lse_scr[u] = jnp.concatenate(lses, axis=1)
                den_scr[u] = jnp.concatenate(dens, axis=1)
                return carry

            lax.fori_loop(0, N_UNITS, softmax, 0, unroll=2)

            def outputs(rho, n, r=r, nb=nb, win=win, idx=idx):
                u = rho * nb + n
                q_rows, k_rows, _ = _unit_rows(r, rho, n, nb)
                vs = _per_head(v_ref[k_rows, :].astype(BF16))
                o = (jnp.dot(p_scr[2 * u, :, 0:win], vs[0], preferred_element_type=F32)
                     + jnp.dot(p_scr[2 * u + 1, :, 0:win], vs[1], preferred_element_type=F32)) / den_scr[u]
                lse = lse_scr[u]
                if idx > 0:
                    old = lse_ref[q_rows, :]
                    top = jnp.maximum(old, lse)
                    new = top + jnp.log(jnp.exp(old - top) + jnp.exp(lse - top))
                    o = att_ref[q_rows, :] * jnp.exp(old - new) + o * jnp.exp(lse - new)
                    lse = new
                att_ref[q_rows, :] = o
                lse_ref[q_rows, :] = lse

            _attn_units(r, nb, outputs)

    blk = lambda first: pl.BlockSpec((SEQ, 128), lambda g: (0, first + g))
    shp = jax.ShapeDtypeStruct((SEQ, D_ATTN), F32)
    big = (2 * N_UNITS, ATTN_BLOCK, 2 * ATTN_BLOCK)
    small = pltpu.VMEM((N_UNITS, ATTN_BLOCK, 128), F32)
    return pl.pallas_call(
        body, out_shape=(shp, shp), grid=(4,), in_specs=[blk(8), blk(12), blk(16)], out_specs=(blk(0), blk(0)),
        scratch_shapes=[pltpu.VMEM(big, F32), pltpu.VMEM(big, BF16), small, small],
        name=name, compiler_params=_cp("parallel"))(proj, proj, proj)


def attn_bwd_all(proj, do, lse, dd, name):
    scale = HEAD_DIM ** -0.5

    def body(q_ref, k_ref, v_ref, do_ref, l_ref, dd_ref, out_ref, dq_s, dk_s, dv_s,
             s_scr, dp_scr, ds_scr, st_scr, dpt_scr, pt_scr, dst_scr, qb_scr, kb_scr, dob_scr):
        dq_s[...] = jnp.zeros_like(dq_s)
        dk_s[...] = jnp.zeros_like(dk_s)
        dv_s[...] = jnp.zeros_like(dv_s)
        for sub_len, r in PATTERNS:
            nb = sub_len // ATTN_BLOCK
            win = 2 * ATTN_BLOCK if nb > 1 else ATTN_BLOCK

            def scores(rho, n, r=r, nb=nb, win=win):
                u = rho * nb + n
                q_rows, k_rows, valid = _unit_rows(r, rho, n, nb)
                kb = max(n - 1, 0) if isinstance(n, int) else jnp.maximum(n - 1, 0)
                dist_t = (n - kb) * ATTN_BLOCK + lax.broadcasted_iota(jnp.int32, (win, ATTN_BLOCK), 1) \
                    - lax.broadcasted_iota(jnp.int32, (win, ATTN_BLOCK), 0)
                valid_t = (dist_t >= 0) & (dist_t <= ATTN_BLOCK)
                q2 = q_ref[q_rows, :].astype(BF16)
                k2 = k_ref[k_rows, :].astype(BF16)
                do2 = do_ref[q_rows, :].astype(BF16)
                qb_scr[u] = q2
                kb_scr[u, 0:win, :] = k2
                dob_scr[u] = do2
                l2 = l_ref[q_rows, :]
                d2 = dd_ref[q_rows, :]
                l2t = l2.T
                d2t = d2.T
                v2 = v_ref[k_rows, :].astype(BF16)
                qs, dos = _per_head(q2), _per_head(do2)
                for h in range(2):
                    c0 = h * HEAD_DIM
                    sc = lax.dot_general(qs[h], k2, NT, preferred_element_type=F32) * scale
                    s_scr[2 * u + h, :, 0:win] = jnp.where(valid, sc, NEG) - l2[:, c0:c0 + 1]
                    dp_scr[2 * u + h, :, 0:win] = lax.dot_general(dos[h], v2, NT, preferred_element_type=F32) \
                        - d2[:, c0:c0 + 1]
                    sct = lax.dot_general(k2, qs[h], NT, preferred_element_type=F32) * scale
                    st_scr[2 * u + h, 0:win, :] = jnp.where(valid_t, sct, NEG) - l2t[c0:c0 + 1, :]
                    dpt_scr[2 * u + h, 0:win, :] = lax.dot_general(v2, dos[h], NT, preferred_element_type=F32) \
                        - d2t[c0:c0 + 1, :]

            _attn_units(r, nb, scores)

            def pointwise(hu, carry, win=win):
                ds_scr[hu, :, 0:win] = (jnp.exp(s_scr[hu, :, 0:win]) * dp_scr[hu, :, 0:win] * scale).astype(BF16)
                pt = jnp.exp(st_scr[hu, 0:win, :])
                pt_scr[hu, 0:win, :] = pt.astype(BF16)
                dst_scr[hu, 0:win, :] = (pt * dpt_scr[hu, 0:win, :] * scale).astype(BF16)
                return carry

            lax.fori_loop(0, 2 * N_UNITS, pointwise, 0, unroll=4)

            def grads(rho, n, r=r, nb=nb, win=win):
                u = rho * nb + n
                q_rows, k_rows, _ = _unit_rows(r, rho, n, nb)
                qs, ks, dos = _per_head(qb_scr[u]), _per_head(kb_scr[u, 0:win, :]), _per_head(dob_scr[u])

                def both(scr, rows, rhs):
                    return (jnp.dot(scr[(2 * u,) + rows], rhs[0], preferred_element_type=F32)
                            + jnp.dot(scr[(2 * u + 1,) + rows], rhs[1], preferred_element_type=F32))

                dq_s[q_rows, :] += both(ds_scr, (slice(None), slice(0, win)), ks)
                dk_s[k_rows, :] += both(dst_scr, (slice(0, win), slice(None)), qs)
                dv_s[k_rows, :] += both(pt_scr, (slice(0, win), slice(None)), dos)

            _attn_units(r, nb, grads)
        out_ref[0] = dq_s[...].astype(BF16)
        out_ref[1] = dk_s[...].astype(BF16)
        out_ref[2] = dv_s[...].astype(BF16)

    blk = lambda first: pl.BlockSpec((SEQ, 128), lambda g: (0, first + g))
    acc = pltpu.VMEM((SEQ, 128), F32)
    big = (2 * N_UNITS, ATTN_BLOCK, 2 * ATTN_BLOCK)
    big_t = (2 * N_UNITS, 2 * ATTN_BLOCK, ATTN_BLOCK)
    return pl.pallas_call(
        body, out_shape=jax.ShapeDtypeStruct((3, SEQ, D_ATTN), BF16), grid=(4,),
        in_specs=[blk(8), blk(12), blk(16), blk(0), blk(0), blk(0)],
        out_specs=pl.BlockSpec((3, SEQ, 128), lambda g: (0, 0, g)),
        scratch_shapes=[acc, acc, acc, pltpu.VMEM(big, F32), pltpu.VMEM(big, F32), pltpu.VMEM(big, BF16),
                        pltpu.VMEM(big_t, F32), pltpu.VMEM(big_t, F32), pltpu.VMEM(big_t, BF16), pltpu.VMEM(big_t, BF16),
                        pltpu.VMEM((N_UNITS, ATTN_BLOCK, 128), BF16),
                        pltpu.VMEM((N_UNITS, 2 * ATTN_BLOCK, 128), BF16), pltpu.VMEM((N_UNITS, ATTN_BLOCK, 128), BF16)],
        name=name, compiler_params=_cp("parallel"))(proj, proj, proj, do, lse, dd)


N_FT = D_FF // FFN_TN


def _ffn_specs():
    per = ROWS // FFN_HALO
    cur_g = pl.BlockSpec((ROWS, FFN_TN), lambda j, i: (i, j))
    cur_v = pl.BlockSpec((ROWS, FFN_TN), lambda j, i: (i, j + N_FT))
    halo_g = pl.BlockSpec((FFN_HALO, FFN_TN), lambda j, i: (jnp.maximum(i * per - 1, 0), j))
    halo_v = pl.BlockSpec((FFN_HALO, FFN_TN), lambda j, i: (jnp.maximum(i * per - 1, 0), j + N_FT))
    w_g = pl.BlockSpec((FFN_K, FFN_TN), lambda j, i: (0, j))
    w_v = pl.BlockSpec((FFN_K, FFN_TN), lambda j, i: (0, j + N_FT))
    b_g = pl.BlockSpec((1, FFN_TN), lambda j, i: (0, j))
    b_v = pl.BlockSpec((1, FFN_TN), lambda j, i: (0, j + N_FT))
    return [cur_g, cur_v, halo_g, halo_v, w_g, w_v, b_g, b_v]


def matmul(a, b, kind, out_dtype, tm, tn, name, b_rows=None):
    stacked = b_rows is not None
    b_shape = (N_DEV * b_rows, b.shape[2]) if stacked else b.shape
    if kind == "nn":
        (m, k), n = a.shape, b_shape[1]
        a_spec = pl.BlockSpec((tm, k), lambda j, i: (i, 0))
        b_spec = pl.BlockSpec((k, tn), lambda j, i: (0, j))
        dims = (((1,), (0,)), ((), ()))
    elif kind == "nt":
        (m, k), n = a.shape, b_shape[0]
        a_spec = pl.BlockSpec((tm, k), lambda j, i: (i, 0))
        b_spec = pl.BlockSpec((tn, k), lambda j, i: (j, 0))
        dims = (((1,), (1,)), ((), ()))
    else:
        (k, m), n = a.shape, b_shape[1]
        a_spec = pl.BlockSpec((k, tm), lambda j, i: (0, i))
        b_spec = pl.BlockSpec((k, tn), lambda j, i: (0, j))
        dims = (((0,), (0,)), ((), ()))
    assert m % tm == 0 and n % tn == 0, (name, m, n, tm, tn)
    if stacked:
        assert b_spec.block_shape[0] == b_shape[0] and kind in ("nn", "nt")
        width = b_spec.block_shape[1]
        b_spec = pl.BlockSpec((N_DEV, b_rows, width), (lambda j, i: (0, 0, j)) if kind == "nn" else (lambda j, i: (0, 0, 0)))

    def body(a_ref, b_ref, o_ref):
        bb = b_ref[...].reshape(b_shape[0], -1) if stacked else b_ref[...]
        o_ref[...] = lax.dot_general(a_ref[...], bb, dims, preferred_element_type=F32).astype(o_ref.dtype)

    return pl.pallas_call(
        body, out_shape=jax.ShapeDtypeStruct((m, n), out_dtype), grid=(n // tn, m // tm),
        in_specs=[a_spec, b_spec], out_specs=pl.BlockSpec((tm, tn), lambda j, i: (i, j)),
        name=name, compiler_params=_cp("parallel", "parallel"))(a, b)


def matmul_tn_pieces(a1, a3, b, name):
    k, n = b.shape
    tm = a3.shape[2]
    n1 = a1.shape[1] // tm

    def body(a1_ref, a3_ref, b_ref, o_ref):
        i = pl.program_id(0)
        tn_dims = (((0,), (0,)), ((), ()))

        @pl.when(i < n1)
        def _():
            o_ref[...] = lax.dot_general(a1_ref[...], b_ref[...], tn_dims, preferred_element_type=F32).astype(BF16)

        @pl.when(i >= n1)
        def _():
            o_ref[...] = lax.dot_general(a3_ref[0], b_ref[...], tn_dims, preferred_element_type=F32).astype(BF16)

    return pl.pallas_call(
        body, out_shape=jax.ShapeDtypeStruct((a1.shape[1] + 3 * tm, n), BF16), grid=(n1 + 3,),
        in_specs=[pl.BlockSpec((k, tm), lambda i: (0, jnp.minimum(i, n1 - 1))),
                  pl.BlockSpec((1, k, tm), lambda i: (jnp.maximum(i - n1, 0), 0, 0)),
                  pl.BlockSpec((k, n), lambda i: (0, 0))],
        out_specs=pl.BlockSpec((tm, n), lambda i: (i, 0)), name=name, compiler_params=_cp("parallel"))(a1, a3, b)


def matmul_tn_halves(a, b, tm, name):
    _, k, m = a.shape
    n = b.shape[1]

    def body(a_ref, b_ref, o_ref):
        o_ref[0] = lax.dot_general(a_ref[0], b_ref[...], (((0,), (0,)), ((), ())), preferred_element_type=F32).astype(BF16)

    return pl.pallas_call(
        body, out_shape=jax.ShapeDtypeStruct((2, m, n), BF16), grid=(2, m // tm),
        in_specs=[pl.BlockSpec((1, k, tm), lambda h, i: (h, 0, i)), pl.BlockSpec((k, n), lambda h, i: (0, 0))],
        out_specs=pl.BlockSpec((1, tm, n), lambda h, i: (h, i, 0)), name=name,
        compiler_params=_cp("parallel", "parallel"))(a, b).reshape(2 * m, n)


def _row_spec(width, col=0):
    return pl.BlockSpec((ROWS, width), lambda i: (i, col))


def _vec_spec(width, rows=1):
    return pl.BlockSpec((rows, width), lambda i: (0, 0))


def _ffn_shifted(cur_ref, halo_ref, pad, s1, s2):
    i = pl.program_id(1)
    pad[0:FFN_HALO, :] = jnp.where(i > 0, halo_ref[...], 0.0)
    pad[FFN_HALO:, :] = cur_ref[0:FFN_HALO, :]
    for k, dst in ((1, s1), (2, s2)):
        dst[0:FFN_HALO, :] = pad[pl.ds(FFN_HALO - k, FFN_HALO), :]
        dst[FFN_HALO:, :] = cur_ref[pl.ds(FFN_HALO - k, ROWS - FFN_HALO), :]


def _ffn_conv(rs, cur_ref, s1, s2, w_ref, b_ref):
    return b_ref[...] + w_ref[0:1, :] * s2[rs, :] + w_ref[1:2, :] * s1[rs, :] + w_ref[2:3, :] * cur_ref[rs, :]


def ffn_act_fwd(up0, wf, bf, name):
    def body(g_ref, v_ref, gh_ref, vh_ref, wg_ref, wv_ref, bg_ref, bv_ref, act_ref, pad, g1, g2, v1, v2):
        _ffn_shifted(g_ref, gh_ref, pad, g1, g2)
        _ffn_shifted(v_ref, vh_ref, pad, v1, v2)

        def chunk(rs):
            gate = _ffn_conv(rs, g_ref, g1, g2, wg_ref, bg_ref)
            val = _ffn_conv(rs, v_ref, v1, v2, wv_ref, bv_ref)
            act_ref[rs, :] = (gate * _sig(gate) * val).astype(BF16)

        _for_chunks(chunk)

    tile = pltpu.VMEM((ROWS, FFN_TN), F32)
    return pl.pallas_call(
        body, out_shape=jax.ShapeDtypeStruct((SEQ, D_FF), BF16), grid=(N_FT, SEQ // ROWS),
        in_specs=_ffn_specs(), out_specs=pl.BlockSpec((ROWS, FFN_TN), lambda j, i: (i, j)),
        scratch_shapes=[pltpu.VMEM((2 * FFN_HALO, FFN_TN), F32), tile, tile, tile, tile],
        name=name, compiler_params=_cp("parallel", "parallel"))(up0, up0, up0, up0, wf, wf, bf, bf)


def ffn_bwd(up0, dact, wf, bf, name, after=None):
    per = ROWS // FFN_HALO
    last = SEQ // FFN_HALO - 1

    def body(g_ref, v_ref, gh_ref, vh_ref, wg_ref, wv_ref, bg_ref, bv_ref, da_ref, gn_ref, vn_ref, dan_ref,
             out_ref, dbg_ref, dbv_ref, dwg_ref, dwv_ref, pad, g1, g2, v1, v2, dgp, dvp, acc):
        i = pl.program_id(1)
        first = i == 0
        _ffn_shifted(g_ref, gh_ref, pad, g1, g2)
        _ffn_shifted(v_ref, vh_ref, pad, v1, v2)
        acc[...] = jnp.zeros_like(acc)

        def grads(gate, val, da):
            s = _sig(gate)
            return da * val * (s * (1.0 + gate * (1.0 - s))), da * (gate * s)

        for c in range(FFN_TN // 128):
            ls = slice(c * 128, (c + 1) * 128)
            wg = [wg_ref[t:t + 1, ls] for t in range(FFN_K)]
            wv = [wv_ref[t:t + 1, ls] for t in range(FFN_K)]
            bg, bv = bg_ref[:, ls], bv_ref[:, ls]
            fold = lambda q: jnp.sum(q.reshape(8, 8, 128), axis=0)

            def block(i, sums, ls=ls, wg=wg, wv=wv, bg=bg, bv=bv):
                rs = pl.ds(pl.multiple_of(i * 64, 64), 64)
                gs = (g2[rs, ls], g1[rs, ls], g_ref[rs, ls])
                vs = (v2[rs, ls], v1[rs, ls], v_ref[rs, ls])
                gate = bg + wg[0] * gs[0] + wg[1] * gs[1] + wg[2] * gs[2]
                val = bv + wv[0] * vs[0] + wv[1] * vs[1] + wv[2] * vs[2]
                dgate, dval = grads(gate, val, da_ref[rs, ls])
                dgp[rs, ls] = dgate
                dvp[rs, ls] = dval
                new = [dgate, dval] + [dgate * gs[t] for t in range(FFN_K)] + [dval * vs[t] for t in range(FFN_K)]
                return tuple(a + fold(q) for a, q in zip(sums, new))

            sums = lax.fori_loop(0, ROWS // 64, block, (jnp.zeros((8, 128), F32),) * (2 + 2 * FFN_K))
            for k in range(2 + 2 * FFN_K):
                acc[k, 0:8, ls] = sums[k]
        _acc(dbg_ref, _rsum(acc[0]), first)
        _acc(dbv_ref, _rsum(acc[1]), first)
        _acc(dwg_ref, jnp.concatenate([_rsum(acc[2 + t]) for t in range(FFN_K)], axis=0), first)
        _acc(dwv_ref, jnp.concatenate([_rsum(acc[5 + t]) for t in range(FFN_K)], axis=0), first)

        def conv_next(cur_ref, nxt_ref, w_ref, b_ref):
            pad[0:FFN_HALO, :] = cur_ref[ROWS - FFN_HALO:, :]
            pad[FFN_HALO:, :] = nxt_ref[...]
            return (b_ref[...] + w_ref[0:1, :] * pad[pl.ds(FFN_HALO - 2, FFN_HALO), :]
                    + w_ref[1:2, :] * pad[pl.ds(FFN_HALO - 1, FFN_HALO), :] + w_ref[2:3, :] * nxt_ref[...])

        gate_n = conv_next(g_ref, gn_ref, wg_ref, bg_ref)
        val_n = conv_next(v_ref, vn_ref, wv_ref, bv_ref)
        dgate_n, dval_n = grads(gate_n, val_n, dan_ref[...])
        inside = i < SEQ // ROWS - 1
        dgp[ROWS:, :] = jnp.where(inside, dgate_n, 0.0)
        dvp[ROWS:, :] = jnp.where(inside, dval_n, 0.0)

        for half, (dp, s1, s2, w_ref) in enumerate(((dgp, g1, g2, wg_ref), (dvp, v1, v2, wv_ref))):
            s1[...] = dp[pl.ds(1, ROWS), :]
            s2[...] = dp[pl.ds(2, ROWS), :]

            def back(rs, dp=dp, s1=s1, s2=s2, w_ref=w_ref, half=half):
                out_ref[half, rs, :] = (w_ref[2:3, :] * dp[rs, :] + w_ref[1:2, :] * s1[rs, :]
                                        + w_ref[0:1, :] * s2[rs, :]).astype(BF16)

            _for_chunks(back)

    body, more_specs, more = _with_after(body, 12, after)
    tile = pltpu.VMEM((ROWS, FFN_TN), F32)
    ext = pltpu.VMEM((ROWS + FFN_HALO, FFN_TN), F32)
    vec = jax.ShapeDtypeStruct((1, D_FF), F32)
    taps = jax.ShapeDtypeStruct((FFN_K, D_FF), F32)
    cur = pl.BlockSpec((ROWS, FFN_TN), lambda j, i: (i, j))
    nxt = lambda off: pl.BlockSpec((FFN_HALO, FFN_TN), lambda j, i: (jnp.minimum((i + 1) * per, last), j + off))
    vs = pl.BlockSpec((1, FFN_TN), lambda j, i: (0, j))
    ts = pl.BlockSpec((FFN_K, FFN_TN), lambda j, i: (0, j))
    return pl.pallas_call(
        body, out_shape=(jax.ShapeDtypeStruct((2, SEQ, D_FF), BF16), vec, vec, taps, taps), grid=(N_FT, SEQ // ROWS),
        in_specs=_ffn_specs() + [cur, nxt(0), nxt(N_FT), nxt(0)] + more_specs,
        out_specs=(pl.BlockSpec((2, ROWS, FFN_TN), lambda j, i: (0, i, j)), vs, vs, ts, ts),
        scratch_shapes=[pltpu.VMEM((2 * FFN_HALO, FFN_TN), F32), tile, tile, tile, tile, ext, ext,
                        pltpu.VMEM((2 + 2 * FFN_K, SUB, FFN_TN), F32)],
        name=name, compiler_params=_cp("parallel", "arbitrary"))(up0, up0, up0, up0, wf, wf, bf, bf, dact, up0, up0, dact,
                                                                 *more)


def ada_fwd(c_all, w_ada, b_cols, name):
    def body(c_ref, w_ref, b_ref, o_ref):
        cc = c_ref[...]
        sc = (cc * _sig(cc)).astype(BF16)
        o_ref[...] = jnp.dot(sc, w_ref[...].astype(BF16), preferred_element_type=F32) + b_ref[...]

    return pl.pallas_call(body, out_shape=jax.ShapeDtypeStruct((N_DEV, w_ada.shape[1]), F32), name=name,
                          compiler_params=_cp())(c_all, w_ada, b_cols)


def _adam(w, g, m, v):
    m = ADAM_B1 * m + (1.0 - ADAM_B1) * g
    v = ADAM_B2 * v + (1.0 - ADAM_B2) * (g * g)
    m_hat = m / (1.0 - ADAM_B1 ** ADAM_STEP)
    v_hat = v / (1.0 - ADAM_B2 ** ADAM_STEP)
    delta = -ADAM_LR * (m_hat / (jnp.sqrt(v_hat) + ADAM_EPS) + ADAM_WD * w)
    return delta, m, v


def ada_bwd_adamw(c_all, dmod_cols, w, m, v, name):
    rows, cols = w.shape
    tr = 256

    def body(c_ref, dm_ref, w_ref, m_ref, v_ref, g_ref, d_ref, nm_ref, nv_ref):
        cc = c_ref[...]
        sc = (cc * _sig(cc)).T
        g = sc[:, 0:1] * dm_ref[0:1, :]
        for b in range(1, N_DEV):
            g = g + sc[:, b:b + 1] * dm_ref[b:b + 1, :]
        g_ref[...] = g
        d_ref[...], nm_ref[...], nv_ref[...] = _adam(w_ref[...], g, m_ref[...], v_ref[...])

    blk = pl.BlockSpec((tr, cols), lambda i: (i, 0))
    shp = jax.ShapeDtypeStruct((rows, cols), F32)
    return pl.pallas_call(
        body, out_shape=(shp, shp, shp, shp), grid=(rows // tr,),
        in_specs=[pl.BlockSpec((N_DEV, tr), lambda i: (0, i)), pl.BlockSpec((N_DEV, cols), lambda i: (0, 0)), blk, blk, blk],
        out_specs=(blk, blk, blk, blk), name=name, compiler_params=_cp("parallel"))(c_all, dmod_cols, w, m, v)


def sum_adamw(parts, mine, me, w, m, v, tr, name):
    n_parts, rows, cols = parts.shape

    def body(me_ref, p_ref, own_ref, w_ref, m_ref, v_ref, g_ref, d_ref, nm_ref, nv_ref):
        def chunk(rs):
            g = own_ref[0, rs, :].astype(F32)
            for k in range(1, n_parts):
                g = g + p_ref[k, rs, :].astype(F32)
            g_ref[rs, :] = g
            d_ref[rs, :], nm_ref[rs, :], nv_ref[rs, :] = _adam(w_ref[rs, :], g, m_ref[rs, :], v_ref[rs, :])

        _for_chunks(chunk, 2, tr)

    blk = pl.BlockSpec((tr, cols), lambda i, me_ref: (i, 0))
    shp = jax.ShapeDtypeStruct((rows, cols), F32)
    grid_spec = pltpu.PrefetchScalarGridSpec(
        num_scalar_prefetch=1, grid=(rows // tr,),
        in_specs=[pl.BlockSpec((n_parts, tr, cols), lambda i, me_ref: (0, i, 0)),
                  pl.BlockSpec((1, tr, cols), lambda i, me_ref: (me_ref[0], i, 0)), blk, blk, blk],
        out_specs=(blk, blk, blk, blk))
    return pl.pallas_call(body, out_shape=(shp, shp, shp, shp), grid_spec=grid_spec, name=name,
                          compiler_params=_cp("parallel"))(me, parts, mine, w, m, v)


MESH = pl.DeviceIdType.MESH


def all_gather(block, name, after=None):
    extra = () if after is None else (after,)

    def body(x_ref, *refs):
        out_ref, send_sems, recv_sems, local_sem = refs[len(extra):]
        x, y, c = lax.axis_index("x"), lax.axis_index("y"), lax.axis_index("c")
        me, sibling = (x, y, c), (x, y, 1 - c)
        chips = [(1 - x, y), (x, 1 - y), (1 - x, 1 - y)]

        def slot(px, py, pc):
            return out_ref.at[4 * px + 2 * py + pc]

        def copy(k, blk, to, src=None):
            return pltpu.make_async_remote_copy(
                src_ref=slot(*blk) if src is None else src, dst_ref=slot(*blk),
                send_sem=send_sems.at[k], recv_sem=recv_sems.at[k], device_id=to, device_id_type=MESH)

        mine = pltpu.make_async_copy(x_ref, slot(*me), local_sem)
        mine.start()
        first = [copy(0, me, sibling, src=x_ref)]
        first += [copy(1 + j, me, (*chip, c), src=x_ref) for j, chip in enumerate(chips)]
        for cp in first:
            cp.start()
        passed = [copy(4 + j, (*chip, c), sibling) for j, chip in enumerate(chips)]
        for j, chip in enumerate(chips):
            copy(1 + j, (*chip, c), me).wait_recv()
            passed[j].start()
        copy(0, sibling, me).wait_recv()
        for j, chip in enumerate(chips):
            copy(4 + j, (*chip, 1 - c), me).wait_recv()
        for cp in first + passed:
            cp.wait_send()
        mine.wait()

    return pl.pallas_call(
        body, out_shape=jax.ShapeDtypeStruct((N_DEV,) + block.shape, block.dtype), in_specs=[ANY] * (1 + len(extra)), out_specs=ANY,
        scratch_shapes=[pltpu.SemaphoreType.DMA((7,)), pltpu.SemaphoreType.DMA((7,)), pltpu.SemaphoreType.DMA],
        name=name)(block, *extra)


HBM = pl.BlockSpec(memory_space=pltpu.HBM)
SEM = pl.BlockSpec(memory_space=pltpu.SEMAPHORE)
EFFECT = pltpu.SideEffectType.DATAFLOW_SIDE_EFFECTING


def _peer_copies(src_ref, land_ref, send_sems, recv_sems, gather):
    x, y, c = lax.axis_index("x"), lax.axis_index("y"), lax.axis_index("c")
    me = 4 * x + 2 * y + c
    copies = []
    for k in range(1, N_DEV):
        px = 1 - x if k & 4 else x
        py = 1 - y if k & 2 else y
        pc = 1 - c if k & 1 else c
        copies.append(pltpu.make_async_remote_copy(
            src_ref=src_ref if gather else src_ref.at[4 * px + 2 * py + pc],
            dst_ref=land_ref.at[me] if gather else land_ref.at[k],
            send_sem=send_sems.at[k - 1], recv_sem=recv_sems.at[k - 1], device_id=(px, py, pc), device_id_type=MESH))
    return copies


def exchange_start(srcs, gather, name, after=None):
    n = len(srcs)
    land_shapes = [(N_DEV,) + src.shape if gather else src.shape for src in srcs]
    extra = () if after is None else (after,)

    def body(*refs):
        src_refs, land_refs = refs[0:n], refs[n:2 * n]
        outs = refs[2 * n + len(extra):]
        for k in range(n):
            for cp in _peer_copies(src_refs[k], land_refs[k], outs[4 * k], outs[4 * k + 1], gather):
                cp.start()
        token = outs[4 * n]
        token[...] = jnp.zeros_like(token)

    out_shape, out_specs, aliases = [], [], {}
    for k, src in enumerate(srcs):
        out_shape += [pltpu.SemaphoreType.DMA((N_DEV - 1,)), pltpu.SemaphoreType.DMA((N_DEV - 1,)),
                      pltpu.HBM(src.shape, src.dtype), pltpu.HBM(land_shapes[k], src.dtype)]
        out_specs += [SEM, SEM, HBM, HBM]
        aliases[k] = 4 * k + 2
        aliases[n + k] = 4 * k + 3
    out_shape.append(jax.ShapeDtypeStruct((8, 128), F32))
    out_specs.append(pl.BlockSpec(memory_space=pltpu.VMEM))
    res = pl.pallas_call(
        body, name=name, out_shape=tuple(out_shape), in_specs=(HBM,) * (2 * n) + (ANY,) * len(extra),
        out_specs=tuple(out_specs), input_output_aliases=aliases,
        compiler_params=pltpu.CompilerParams(has_side_effects=EFFECT),
    )(*[pltpu.with_memory_space_constraint(src, pltpu.HBM) for src in srcs],
      *[pltpu.with_memory_space_constraint(lax.empty(shp, src.dtype), pltpu.HBM) for shp, src in zip(land_shapes, srcs)],
      *extra)
    return [tuple(res[4 * k:4 * k + 4]) for k in range(n)], res[4 * n]


def _stage1_peer(i):
    x, y, c = lax.axis_index("x"), lax.axis_index("y"), lax.axis_index("c")
    if i == 0:
        return (x, y, 1 - c)
    return (1 - x if i & 1 else x, 1 - y if i & 2 else y, c)


def _slot_of(peer):
    return 4 * peer[0] + 2 * peer[1] + peer[2]


def _stage1_copy(i, src_ref, land_ref, send_sems, recv_sems):
    me = _slot_of((lax.axis_index("x"), lax.axis_index("y"), lax.axis_index("c")))
    return pltpu.make_async_remote_copy(src_ref=src_ref, dst_ref=land_ref.at[me], send_sem=send_sems.at[i],
                                        recv_sem=recv_sems.at[i], device_id=_stage1_peer(i), device_id_type=MESH)


def _stage2_copy(j, land_ref, send_sems, recv_sems):
    slot = _slot_of(_stage1_peer(j + 1))
    return pltpu.make_async_remote_copy(src_ref=land_ref.at[slot], dst_ref=land_ref.at[slot], send_sem=send_sems.at[j],
                                        recv_sem=recv_sems.at[j], device_id=_stage1_peer(0), device_id_type=MESH)


def gather2_start(srcs, name, after=None):
    n = len(srcs)
    extra = () if after is None else (after,)

    def body(*refs):
        src_refs, land_refs = refs[0:n], refs[n:2 * n]
        outs = refs[2 * n + len(extra):]
        for k in range(n):
            for i in range(4):
                _stage1_copy(i, src_refs[k], land_refs[k], outs[4 * k], outs[4 * k + 1]).start()
        outs[4 * n][...] = jnp.zeros((8, 128), F32)

    out_shape, out_specs, aliases = [], [], {}
    for k, src in enumerate(srcs):
        out_shape += [pltpu.SemaphoreType.DMA((4,)), pltpu.SemaphoreType.DMA((4,)),
                      pltpu.HBM(src.shape, src.dtype), pltpu.HBM((N_DEV,) + src.shape, src.dtype)]
        out_specs += [SEM, SEM, HBM, HBM]
        aliases[k] = 4 * k + 2
        aliases[n + k] = 4 * k + 3
    out_shape.append(jax.ShapeDtypeStruct((8, 128), F32))
    out_specs.append(pl.BlockSpec(memory_space=pltpu.VMEM))
    res = pl.pallas_call(
        body, name=name, out_shape=tuple(out_shape), in_specs=(HBM,) * (2 * n) + (ANY,) * len(extra),
        out_specs=tuple(out_specs), input_output_aliases=aliases,
        compiler_params=pltpu.CompilerParams(has_side_effects=EFFECT),
    )(*[pltpu.with_memory_space_constraint(src, pltpu.HBM) for src in srcs],
      *[pltpu.with_memory_space_constraint(lax.empty((N_DEV,) + src.shape, src.dtype), pltpu.HBM) for src in srcs], *extra)
    return [dict(send1=res[4 * k], recv1=res[4 * k + 1], src=res[4 * k + 2], land=res[4 * k + 3]) for k in range(n)], \
        res[4 * n]


def gather2_pass(handles, after, name):
    n = len(handles)

    def body(*refs):
        src_refs, land_refs, recv1 = refs[0:n], refs[n:2 * n], refs[2 * n:3 * n]
        outs = refs[3 * n + 1:]
        for k in range(n):
            for j in range(3):
                _stage1_copy(j + 1, src_refs[k], land_refs[k], recv1[k], recv1[k]).wait_recv()
                _stage2_copy(j, land_refs[k], outs[3 * k], outs[3 * k + 1]).start()

    out_shape, out_specs, aliases = [], [], {}
    for k, h in enumerate(handles):
        out_shape += [pltpu.SemaphoreType.DMA((3,)), pltpu.SemaphoreType.DMA((3,)), pltpu.HBM(h["land"].shape, h["land"].dtype)]
        out_specs += [SEM, SEM, HBM]
        aliases[n + k] = 3 * k + 2
    res = pl.pallas_call(
        body, name=name, out_shape=tuple(out_shape), in_specs=(HBM,) * (2 * n) + (SEM,) * n + (ANY,),
        out_specs=tuple(out_specs), input_output_aliases=aliases,
        compiler_params=pltpu.CompilerParams(has_side_effects=EFFECT),
    )(*[h["src"] for h in handles], *[h["land"] for h in handles], *[h["recv1"] for h in handles], after)
    return [dict(h, send2=res[3 * k], recv2=res[3 * k + 1], land=res[3 * k + 2]) for k, h in enumerate(handles)]


def gather2_wait(h, after, name):
    def body(src_ref, land_ref, send1, recv1, send2, recv2, after_ref, src_dead, got_ref):
        for i in range(4):
            _stage1_copy(i, src_ref, land_ref, send1, recv1).wait_send()
        _stage1_copy(0, src_ref, land_ref, send1, recv1).wait_recv()
        for j in range(3):
            cp = _stage2_copy(j, land_ref, send2, recv2)
            cp.wait_send()
            cp.wait_recv()

    return pl.pallas_call(
        body, name=name,
        out_shape=(pltpu.HBM(h["src"].shape, h["src"].dtype), pltpu.HBM(h["land"].shape, h["land"].dtype)),
        in_specs=(HBM, HBM, SEM, SEM, SEM, SEM, ANY), out_specs=(HBM, HBM), input_output_aliases={0: 0, 1: 1},
        compiler_params=pltpu.CompilerParams(has_side_effects=EFFECT),
    )(h["src"], h["land"], h["send1"], h["recv1"], h["send2"], h["recv2"], after)[1]


def exchange_wait(handles, after, gather, name):
    send_sems, recv_sems, src_thru, land_thru = handles

    def body(src_ref, land_ref, send_sems, recv_sems, after_ref, src_dead, got_ref):
        for cp in _peer_copies(src_ref, land_ref, send_sems, recv_sems, gather):
            cp.wait_send()
            cp.wait_recv()

    return pl.pallas_call(
        body, name=name,
        out_shape=(pltpu.HBM(src_thru.shape, src_thru.dtype), pltpu.HBM(land_thru.shape, land_thru.dtype)),
        in_specs=(HBM, HBM, SEM, SEM, ANY), out_specs=(HBM, HBM), input_output_aliases={0: 0, 1: 1},
        compiler_params=pltpu.CompilerParams(has_side_effects=EFFECT),
    )(src_thru, land_thru, send_sems, recv_sems, after)


def local_step(x, tgt, mod, started, get_w, put_grad, wc, wf, g_mix, bc, lg, lb, gco, gao, g_ffn, bf, g_fin):
    w_in = get_w("w_in", mod)
    proj, h1 = rms_mod_matmul(x, g_mix, mod, 0, 1, w_in, D_IN // N_DEV, "proj_fwd", after=started)
    mix_a, u1 = conv_module_fwd(proj, wc, bc, lg, lb, gco, "conv_module_fwd")
    att, lse = attn_fwd_all(proj, "attn_fwd")
    w_out = get_w("w_out", att)
    y1, mixed = norm_concat_matmul(mix_a, att, gao, w_out, "out_proj_fwd")
    w_up = get_w("w_up", y1)
    up0, x1, h2 = resid_rms_mod_matmul(x, y1, g_ffn, mod, 2, 3, 4, w_up, "up_fwd")
    act = ffn_act_fwd(up0, wf, bf, "ffn_act_fwd")
    w_down = get_w("w_down", act)
    loss_t, dx2, dy2, d_gfin, d_gaf = matmul_loss_bwd(act, w_down, x1, tgt, g_fin, mod, 5, "down_fwd_loss")
    dact = matmul(dy2, w_down, "nt", F32, 512, FFN_TN, "down_bwd_x")
    dw_down = matmul(act, dy2, "tn", BF16, 256, D_MODEL, "down_bwd_w")
    dup0, dbf_g, dbf_v, dwf_g, dwf_v = ffn_bwd(up0, dact, wf, bf, "ffn_bwd", after=put_grad("w_down", dw_down))
    dw_up = matmul_tn_halves(dup0, h2, 256, "up_bwd_w")
    dx1, d_shf, d_scf, d_gffn, dy1, d_gam = matmul_rms_mod_bwd(
        dup0, w_up, x1, dx2, g_ffn, mod, 4, y1, 2, "up_bwd_x", after=put_grad("w_up", dw_up))
    dw_out = matmul(mixed, dy1, "tn", BF16, 256, D_MODEL, "out_proj_bwd_w")
    dmixed, do, dd, d_gao = matmul_combine_bwd(dy1, w_out, att, gao, "out_proj_bwd_x", after=put_grad("w_out", dw_out))
    dqkv = attn_bwd_all(proj, do, lse, dd, "attn_bwd")
    du1, d_gco, d_lg, d_lb, d_bc, d_wc = conv_module_bwd_a(proj, u1, dmixed, lg, lb, gco, "conv_module_bwd_a")
    dproj_a = conv_module_bwd_b(proj, du1, wc, "conv_module_bwd_b")
    dw_in = matmul_tn_pieces(dproj_a, dqkv, h1, "proj_bwd_w")
    dx, d_shm, d_scm, d_gmix = matmul_rms_mod_bwd(
        (dproj_a, dqkv), w_in, x, dx1, g_mix, mod, 1, None, 0, "proj_bwd_x", b_rows=D_IN // N_DEV,
        after=put_grad("w_in", dw_in))
    dmod = jnp.concatenate([d_shm, d_scm, d_gam, d_shf, d_scf, d_gaf], axis=1)
    small = dict(g_norm_mix=d_gmix, b_conv_dw=d_bc, ln_conv_g=d_lg, ln_conv_b=d_lb, g_conv_out=d_gco, g_attn_out=d_gao,
                 g_norm_ffn=d_gffn, b_ffn_dw=jnp.concatenate([dbf_g, dbf_v], axis=1), g_final=d_gfin,
                 w_conv_dw=d_wc, w_ffn_dw=jnp.concatenate([dwf_g, dwf_v], axis=1), dmod=dmod, loss=loss_t[0:1, 0:1])
    return dx, small


def _padw(a, width):
    return jnp.pad(a, ((0, 0), (0, width - a.shape[1])))


def pack_small(t):
    wide = jnp.concatenate([_padw(t["dmod"], PACK_W), _padw(t["b_ffn_dw"], PACK_W), _padw(t["w_ffn_dw"], PACK_W),
                            _padw(t["loss"], PACK_W), jnp.zeros((2, PACK_W), F32)], axis=0)
    z512 = jnp.zeros((1, 512), F32)
    narrow = jnp.concatenate([
        t["g_norm_mix"], t["g_norm_ffn"], t["g_final"],
        jnp.concatenate([t["b_conv_dw"], t["ln_conv_g"]], axis=1),
        jnp.concatenate([t["ln_conv_b"], t["g_conv_out"]], axis=1),
        jnp.concatenate([t["g_attn_out"], z512], axis=1),
        jnp.zeros((2, 1024), F32),
        jnp.pad(t["w_conv_dw"], ((0, 1), (0, 0))).reshape(16, 1024)], axis=0)
    return jnp.concatenate([wide, narrow.reshape(4, PACK_W), jnp.zeros((4, PACK_W), F32)], axis=0)


_NARROW = lambda k, off=0: (8 + k // 6, (k % 6) * 1024 + off)
PACKED_AT = dict(
    b_ada=(0, 0, N_MOD * D_MODEL), b_ffn_dw=(1, 0, 2 * D_FF),
    g_norm_mix=_NARROW(0) + (D_MODEL,), g_norm_ffn=_NARROW(1) + (D_MODEL,), g_final=_NARROW(2) + (D_MODEL,),
    b_conv_dw=_NARROW(3) + (D_CONV,), ln_conv_g=_NARROW(3, 512) + (D_CONV,),
    ln_conv_b=_NARROW(4) + (D_CONV,), g_conv_out=_NARROW(4, 512) + (D_CONV,), g_attn_out=_NARROW(5) + (D_ATTN,))
SMALL_ORDER = list(PACKED_AT)


def small_adamw(parts, wmv, name):
    def body(*refs):
        p_ref = refs[0]
        ins = refs[1:1 + 3 * len(SMALL_ORDER)]
        outs = refs[1 + 3 * len(SMALL_ORDER):]
        g = p_ref[0]
        for k in range(1, N_DEV):
            g = g + p_ref[k]
        for i, n in enumerate(SMALL_ORDER):
            row, lane, width = PACKED_AT[n]
            gp = g[row:row + 1, lane:lane + width]
            w_ref, m_ref, v_ref = ins[3 * i:3 * i + 3]
            g_ref, d_ref, nm_ref, nv_ref = outs[4 * i:4 * i + 4]
            g_ref[...] = gp
            d_ref[...], nm_ref[...], nv_ref[...] = _adam(w_ref[...], gp, m_ref[...], v_ref[...])
        wc_ref, wf_ref, loss_ref = outs[4 * len(SMALL_ORDER):]
        for j in range(CONV_K):
            row, lane = _NARROW(8 + j // 2, (j % 2) * 512)
            wc_ref[j:j + 1, :] = g[row:row + 1, lane:lane + D_CONV]
        wf_ref[...] = g[2:2 + FFN_K, 0:2 * D_FF]
        loss_ref[...] = jnp.broadcast_to(g[5:6, 0:1], (8, 128))

    args, out_shape = [parts], []
    for n in SMALL_ORDER:
        args += list(wmv[n])
        out_shape += [jax.ShapeDtypeStruct(wmv[n][0].shape, F32)] * 4
    out_shape += [jax.ShapeDtypeStruct((CONV_K, D_CONV), F32), jax.ShapeDtypeStruct((FFN_K, 2 * D_FF), F32),
                  jax.ShapeDtypeStruct((8, 128), F32)]
    res = pl.pallas_call(body, out_shape=tuple(out_shape), name=name, compiler_params=_cp())(*args)
    per = {n: tuple(res[4 * i:4 * i + 4]) for i, n in enumerate(SMALL_ORDER)}
    return per, res[-3], res[-2], res[-1][0, 0]


def shard_adamw(items, name):
    def body(*refs):
        ins, outs = refs[:4 * len(items)], refs[4 * len(items):]
        for i in range(len(items)):
            g_ref, w_ref, m_ref, v_ref = ins[4 * i:4 * i + 4]
            og_ref, d_ref, nm_ref, nv_ref = outs[4 * i:4 * i + 4]
            og_ref[...] = g_ref[...]
            d_ref[...], nm_ref[...], nv_ref[...] = _adam(w_ref[...], g_ref[...], m_ref[...], v_ref[...])

    args = [a for item in items for a in item]
    out_shape = tuple(jax.ShapeDtypeStruct(item[1].shape, F32) for item in items for _ in range(4))
    res = pl.pallas_call(body, out_shape=out_shape, name=name, compiler_params=_cp())(*args)
    return [tuple(res[4 * i:4 * i + 4]) for i in range(len(items))]


def _shard(full, n_cols, me):
    return lax.dynamic_slice(full, (0, me * n_cols), (full.shape[0], n_cols))


WEIGHTS = ["w_ada", "b_ada", "g_norm_mix", "w_in", "w_conv_dw", "b_conv_dw", "ln_conv_g", "ln_conv_b", "g_conv_out",
           "g_attn_out", "w_out", "g_norm_ffn", "w_up", "w_ffn_dw", "b_ffn_dw", "w_down", "g_final"]


def kernel(x, c, w_ada, b_ada, g_norm_mix, w_in, w_conv_dw, b_conv_dw, ln_conv_g, ln_conv_b, g_conv_out, g_attn_out, w_out, g_norm_ffn, w_up, w_ffn_dw, b_ffn_dw, w_down, g_final, loss_target, m_w_ada, m_b_ada, m_g_norm_mix, m_w_in, m_w_conv_dw, m_b_conv_dw, m_ln_conv_g, m_ln_conv_b, m_g_conv_out, m_g_attn_out, m_w_out, m_g_norm_ffn, m_w_up, m_w_ffn_dw, m_b_ffn_dw, m_w_down, m_g_final, v_w_ada, v_b_ada, v_g_norm_mix, v_w_in, v_w_conv_dw, v_b_conv_dw, v_ln_conv_g, v_ln_conv_b, v_g_conv_out, v_g_attn_out, v_w_out, v_g_norm_ffn, v_w_up, v_w_ffn_dw, v_b_ffn_dw, v_w_down, v_g_final):
    args = dict(locals())
    me = 4 * lax.axis_index("x") + 2 * lax.axis_index("y") + lax.axis_index("c")
    me1 = me.astype(jnp.int32).reshape(1)

    def flat(name, prefix=""):
        a = args[prefix + name]
        return a.reshape(a.shape[-2] if a.ndim > 1 else 1, a.shape[-1])

    def flat_t(name, prefix=""):
        return args[prefix + name][0].T

    n_in, n_up, r_out, r_down = w_in.shape[2], w_up.shape[2], w_out.shape[1], w_down.shape[1]
    n_ada, n_wc, n_wf = w_ada.shape[2], w_conv_dw.shape[2], w_ffn_dw.shape[2]
    taps_c = jnp.pad(flat("w_conv_dw").reshape(1, CONV_K * n_wc), ((0, 0), (0, 2 * D_MODEL - CONV_K * n_wc)))
    taps_f = jnp.pad(flat("w_ffn_dw").reshape(1, FFN_K * n_wf), ((0, 0), (0, 3 * D_MODEL - FFN_K * n_wf)))
    first = jnp.concatenate([c, taps_c.reshape(2, D_MODEL), taps_f.reshape(3, D_MODEL), jnp.zeros((2, D_MODEL), F32)], axis=0)
    w_in_block = flat_t("w_in").astype(BF16)
    hi = lax.reduce_precision(first, 8, 7)
    mid = lax.reduce_precision(first - hi, 8, 7)
    low = lax.reduce_precision(first - hi - mid, 8, 7)
    terms = jnp.concatenate([hi, mid, low, jnp.zeros((8, D_MODEL), F32)], axis=0).astype(BF16)
    first_block = all_gather(jnp.concatenate([w_in_block, terms], axis=0), "gather_c_taps_w_in")
    terms = first_block[:, n_in:n_in + 24, :].astype(F32)
    first_all = (terms[:, 0:8] + terms[:, 8:16]) + terms[:, 16:24]
    c_all = first_all[:, 0, :]
    wc_full = first_all[:, 1:3, :].reshape(N_DEV, 2 * D_MODEL)[:, :CONV_K * n_wc].reshape(N_DEV, CONV_K, n_wc)
    wc_full = wc_full.transpose(1, 0, 2).reshape(CONV_K, D_CONV)
    wf_full = first_all[:, 3:6, :].reshape(N_DEV, 3 * D_MODEL)[:, :FFN_K * n_wf].reshape(N_DEV, FFN_K, n_wf)
    wf_full = wf_full.transpose(1, 0, 2).reshape(FFN_K, 2 * D_FF)
    mod_cols = ada_fwd(c_all, flat("w_ada"), _shard(flat("b_ada"), n_ada, me), "ada_fwd")
    mod_all = all_gather(mod_cols, "gather_mod")
    mod = lax.dynamic_index_in_dim(mod_all, me, axis=1, keepdims=False).reshape(N_MOD, D_MODEL)
    mod = jnp.pad(mod, ((0, 2), (0, 0)))

    order = ("w_out", "w_up", "w_down")
    blocks = dict(w_up=flat_t("w_up").astype(BF16), w_out=flat("w_out").astype(BF16), w_down=flat("w_down").astype(BF16))
    handles, tok = gather2_start([blocks[name] for name in order], "gather_weights_start", mod_all)
    gathers = dict(zip(order, handles))

    def gathered(name, after):
        if "send2" not in gathers[name]:
            group = ("w_out", "w_up") if name != "w_down" else ("w_down",)
            gathers.update(zip(group, gather2_pass([gathers[w] for w in group], after, f"gather_{name}_pass")))
        land = gather2_wait(gathers[name], after, f"gather_{name}_wait")
        return lax.dynamic_update_index_in_dim(land, blocks[name], me, axis=0)

    def get_w(name, after):
        if name == "w_in":
            return first_block
        return gathered(name, after).reshape(-1, D_MODEL)

    exchanges = {}

    def put_grad(name, dw, after=None):
        dev_major = dw.reshape(N_DEV, -1, D_MODEL)
        (exchanges[name],), token = exchange_start([dev_major], False, f"exchange_{name}_start", after)
        return token

    grad_x, small = local_step(
        x[0], loss_target[0], mod, tok, get_w, put_grad, wc_full, wf_full,
        flat("g_norm_mix"), flat("b_conv_dw"), flat("ln_conv_g"), flat("ln_conv_b"), flat("g_conv_out"),
        flat("g_attn_out"), flat("g_norm_ffn"), flat("b_ffn_dw"), flat("g_final"))

    out = {}

    def finish(name, tr, after):
        mine, parts = exchange_wait(exchanges[name], after, False, f"exchange_{name}_wait")
        if name in ("w_in", "w_up"):
            res = sum_adamw(parts, mine, me1, flat_t(name), flat_t(name, "m_"), flat_t(name, "v_"), tr, "adamw_" + name)
            out[name] = tuple(r.T for r in res)
        else:
            res = out[name] = sum_adamw(parts, mine, me1, flat(name), flat(name, "m_"), flat(name, "v_"), tr,
                                        "adamw_" + name)
        return res[0]

    after = finish("w_down", r_down, grad_x)
    after = finish("w_up", n_up // 2, after)
    after = finish("w_out", r_out, after)
    after = finish("w_in", n_in, after)

    small_all = all_gather(pack_small(small), "gather_small", after)

    wmv = {n: (flat(n), flat(n, "m_"), flat(n, "v_")) for n in SMALL_ORDER}
    per, g_wc, g_wf, loss = small_adamw(small_all, wmv, "adamw_small")
    out.update(per)
    taps = shard_adamw([(_shard(g_wc, n_wc, me), flat("w_conv_dw"), flat("w_conv_dw", "m_"), flat("w_conv_dw", "v_")),
                        (_shard(g_wf, n_wf, me), flat("w_ffn_dw"), flat("w_ffn_dw", "m_"), flat("w_ffn_dw", "v_"))],
                       "adamw_taps")
    out["w_conv_dw"], out["w_ffn_dw"] = taps

    dmod_cols = _shard(small_all[:, 0, :], n_ada, me)
    out["w_ada"] = ada_bwd_adamw(c_all, dmod_cols, flat("w_ada"), flat("w_ada", "m_"), flat("w_ada", "v_"), "adamw_w_ada")

    result = [loss, grad_x[None]]
    for k in range(4):
        result += [out[n][k].reshape(args[n].shape) for n in WEIGHTS]
    return tuple(result)
```

```python
import functools

import jax
import jax.numpy as jnp
from jax import lax
from jax.experimental import pallas as pl
from jax.experimental.pallas import tpu as pltpu

F32 = jnp.float32
BF16 = jnp.bfloat16

N_DEV = 8
SEQ = 2048
D_MODEL = 1024
D_CONV = 512
D_ATTN = 512
HEAD_DIM = 64
CONV_K = 31
D_FF = 2816
FFN_K = 3
D_IN = 2 * D_CONV + 3 * D_ATTN
N_MOD = 6
EPS = 1e-6
ATTN_BLOCK = 128
PATTERNS = ((2048, 1), (512, 4), (128, 16))
NEG = -1e30

ADAM_LR, ADAM_B1, ADAM_B2, ADAM_EPS, ADAM_WD, ADAM_STEP = 0.001, 0.9, 0.999, 1e-08, 0.01, 10

ROWS = 256
CONV_HALO = 32
FFN_HALO = 8
FFN_TN = 1408
VMEM_LIMIT = 56 * 1024 * 1024
PACK_W = 6144


NT = (((1,), (1,)), ((), ()))
ANY = pl.BlockSpec(memory_space=pl.ANY)


def _cp(*sem):
    return pltpu.CompilerParams(dimension_semantics=sem if sem else None, vmem_limit_bytes=VMEM_LIMIT)


def _with_after(body, n_in, after):
    if after is None:
        return body, [], []
    return (lambda *refs: body(*refs[:n_in], *refs[n_in + 1:])), [ANY], [after]


def _sig(x):
    return 1.0 / (1.0 + jnp.exp(-x))


def _rsum(x):
    return jnp.sum(x, axis=0, keepdims=True)


def _mean(x):
    return jnp.mean(x, axis=-1, keepdims=True)


def _acc(ref, val, first):
    @pl.when(first)
    def _():
        ref[...] = val

    @pl.when(jnp.logical_not(first))
    def _():
        ref[...] += val


SUB = 16


def _for_chunks(fn, unroll=1, rows=ROWS):
    def step(i, carry):
        fn(pl.ds(pl.multiple_of(i * SUB, SUB), SUB))
        return carry

    lax.fori_loop(0, rows // SUB, step, 0, unroll=unroll)


PAIR = 2 * ROWS


def _pair_spec(width, col=0):
    return pl.BlockSpec((PAIR, width), lambda i: (i, col))


def _halves():
    return [slice(h * ROWS, (h + 1) * ROWS) for h in range(2)]


def rms_mod_matmul(x, g, mod, sh_row, sc_row, b, b_rows, name, after=None):
    n = N_DEV * b_rows

    def body(x_ref, g_ref, mod_ref, b_ref, o_ref, h_ref):
        for rs in _halves():
            xx = x_ref[rs, :]
            r = lax.rsqrt(_mean(xx * xx) + EPS)
            h = (xx * r * g_ref[...] * (1.0 + mod_ref[sc_row:sc_row + 1, :]) + mod_ref[sh_row:sh_row + 1, :]).astype(BF16)
            h_ref[rs, :] = h
            o_ref[rs, :] = lax.dot_general(h, b_ref[...].reshape(n, D_MODEL), NT, preferred_element_type=F32)

    body, more_specs, more = _with_after(body, 4, after)
    return pl.pallas_call(
        body, out_shape=(jax.ShapeDtypeStruct((SEQ, n), F32), jax.ShapeDtypeStruct((SEQ, D_MODEL), BF16)),
        grid=(SEQ // PAIR,),
        in_specs=[_pair_spec(D_MODEL), _vec_spec(D_MODEL), _vec_spec(D_MODEL, 8),
                  pl.BlockSpec((N_DEV, b_rows, D_MODEL), lambda i: (0, 0, 0))] + more_specs,
        out_specs=(_pair_spec(n), _pair_spec(D_MODEL)), name=name, compiler_params=_cp("parallel"))(x, g, mod, b, *more)


def norm_concat_matmul(mix_a, att, gao, w, name):
    def body(a_ref, att_ref, g_ref, w_ref, y_ref, mixed_ref):
        for rs in _halves():
            aa = att_ref[rs, :]
            mixed_ref[rs, 0:D_CONV] = a_ref[rs, :]
            mixed_ref[rs, D_CONV:] = (aa * lax.rsqrt(_mean(aa * aa) + EPS) * g_ref[...]).astype(BF16)
            y_ref[rs, :] = jnp.dot(mixed_ref[rs, :], w_ref[...], preferred_element_type=F32)

    return pl.pallas_call(
        body, out_shape=(jax.ShapeDtypeStruct((SEQ, D_MODEL), F32), jax.ShapeDtypeStruct((SEQ, D_MODEL), BF16)),
        grid=(SEQ // PAIR,),
        in_specs=[_pair_spec(D_CONV), _pair_spec(D_ATTN), _vec_spec(D_ATTN), pl.BlockSpec(w.shape, lambda i: (0, 0))],
        out_specs=(_pair_spec(D_MODEL), _pair_spec(D_MODEL)), name=name, compiler_params=_cp("parallel"))(mix_a, att, gao, w)


def resid_rms_mod_matmul(x, y, g, mod, ga_row, sh_row, sc_row, w, name):
    n = w.shape[0]

    def body(x_ref, y_ref, g_ref, mod_ref, w_ref, o_ref, x1_ref, h_ref):
        x1 = x_ref[...] + mod_ref[ga_row:ga_row + 1, :] * y_ref[...]
        x1_ref[...] = x1
        r = lax.rsqrt(_mean(x1 * x1) + EPS)
        h = (x1 * r * g_ref[...] * (1.0 + mod_ref[sc_row:sc_row + 1, :]) + mod_ref[sh_row:sh_row + 1, :]).astype(BF16)
        h_ref[...] = h
        o_ref[...] = lax.dot_general(h, w_ref[...], NT, preferred_element_type=F32)

    return pl.pallas_call(
        body,
        out_shape=(jax.ShapeDtypeStruct((SEQ, n), F32), jax.ShapeDtypeStruct((SEQ, D_MODEL), F32),
                   jax.ShapeDtypeStruct((SEQ, D_MODEL), BF16)),
        grid=(SEQ // ROWS,),
        in_specs=[_row_spec(D_MODEL), _row_spec(D_MODEL), _vec_spec(D_MODEL), _vec_spec(D_MODEL, 8),
                  pl.BlockSpec(w.shape, lambda i: (0, 0))],
        out_specs=(_row_spec(n), _row_spec(D_MODEL), _row_spec(D_MODEL)),
        name=name, compiler_params=_cp("parallel"))(x, y, g, mod, w)


def matmul_combine_bwd(dy, w, att, gao, name, after=None):
    def body(dy_ref, w_ref, att_ref, g_ref, dm_ref, do_ref, dd_ref, dg_ref):
        @pl.when(pl.program_id(0) == 0)
        def _():
            dg_ref[...] = jnp.zeros_like(dg_ref)

        same_head = (jnp.right_shift(lax.broadcasted_iota(jnp.int32, (D_ATTN, D_ATTN), 0), 6)
                     == jnp.right_shift(lax.broadcasted_iota(jnp.int32, (D_ATTN, D_ATTN), 1), 6)).astype(F32)
        for rs in _halves():
            dmixed = lax.dot_general(dy_ref[rs, :], w_ref[...], NT, preferred_element_type=F32)
            dm_ref[rs, :] = dmixed
            att = att_ref[rs, :]
            r = lax.rsqrt(_mean(att * att) + EPS)
            xn = att * r
            dm = dmixed[:, D_CONV:]
            dg_ref[...] += _rsum(dm * xn)
            dyn = dm * g_ref[...]
            do = r * (dyn - xn * _mean(dyn * xn))
            do_ref[rs, :] = do
            dd_ref[rs, :] = jnp.dot(do * att, same_head, preferred_element_type=F32, precision=lax.Precision.HIGHEST)

    rs = _pair_spec(D_ATTN)
    f = jax.ShapeDtypeStruct((SEQ, D_ATTN), F32)
    body, more_specs, more = _with_after(body, 4, after)
    return pl.pallas_call(
        body, out_shape=(jax.ShapeDtypeStruct((SEQ, D_MODEL), F32), f, f, jax.ShapeDtypeStruct((1, D_ATTN), F32)),
        grid=(SEQ // PAIR,),
        in_specs=[_pair_spec(D_MODEL), pl.BlockSpec(w.shape, lambda i: (0, 0)), rs, _vec_spec(D_ATTN)] + more_specs,
        out_specs=(_pair_spec(D_MODEL), rs, rs, _vec_spec(D_ATTN)),
        name=name, compiler_params=_cp("arbitrary"))(dy, w, att, gao, *more)


def matmul_loss_bwd(act, w, x1, tgt, g, mod, ga_row, name):
    def body(a_ref, w_ref, x1_ref, t_ref, g_ref, mod_ref, loss_ref, dx2_ref, dy2_ref, dg_ref, dga_ref):
        @pl.when(pl.program_id(0) == 0)
        def _():
            loss_ref[...] = jnp.zeros_like(loss_ref)
            dg_ref[...] = jnp.zeros_like(dg_ref)
            dga_ref[...] = jnp.zeros_like(dga_ref)

        ga = mod_ref[ga_row:ga_row + 1, :]
        for rs in _halves():
            y2 = jnp.dot(a_ref[rs, :], w_ref[...], preferred_element_type=F32)
            x2 = x1_ref[rs, :] + ga * y2
            r = lax.rsqrt(_mean(x2 * x2) + EPS)
            xn = x2 * r
            err = xn * g_ref[...] - t_ref[rs, :]
            loss_ref[...] += jnp.broadcast_to(0.5 * jnp.sum(_mean(err * err)), (8, 128))
            dy = err * (1.0 / D_MODEL)
            dg_ref[...] += _rsum(dy * xn)
            dxn = dy * g_ref[...]
            dx2 = r * (dxn - xn * _mean(dxn * xn))
            dx2_ref[rs, :] = dx2
            dy2_ref[rs, :] = (dx2 * ga).astype(BF16)
            dga_ref[...] += _rsum(dx2 * y2)

    vec = jax.ShapeDtypeStruct((1, D_MODEL), F32)
    rows = _pair_spec
    return pl.pallas_call(
        body,
        out_shape=(jax.ShapeDtypeStruct((8, 128), F32), jax.ShapeDtypeStruct((SEQ, D_MODEL), F32),
                   jax.ShapeDtypeStruct((SEQ, D_MODEL), BF16), vec, vec),
        grid=(SEQ // PAIR,),
        in_specs=[rows(act.shape[1]), pl.BlockSpec(w.shape, lambda i: (0, 0)), rows(D_MODEL), rows(D_MODEL),
                  _vec_spec(D_MODEL), _vec_spec(D_MODEL, 8)],
        out_specs=(pl.BlockSpec((8, 128), lambda i: (0, 0)), rows(D_MODEL), rows(D_MODEL),
                   _vec_spec(D_MODEL), _vec_spec(D_MODEL)),
        name=name, compiler_params=_cp("arbitrary"))(act, w, x1, tgt, g, mod)


def matmul_rms_mod_bwd(a, b, x, dres, g, mod, sc_row, y, ga_row, name, b_rows=None, after=None):
    gated = y is not None
    pieces = isinstance(a, tuple)
    tm = PAIR if pieces else ROWS
    blocks = [slice(h * ROWS, (h + 1) * ROWS) for h in range(tm // ROWS)]
    rows = lambda width: pl.BlockSpec((tm, width), lambda i: (i, 0))
    if pieces:
        a1, a3 = a
        a_args = [a1, a3]
        a_specs = [rows(a1.shape[1]), pl.BlockSpec((3, tm, a3.shape[2]), lambda i: (0, i, 0))]
        b_arg, b_spec = b, pl.BlockSpec((N_DEV, b_rows, D_MODEL), lambda i: (0, 0, 0))
    else:
        k2 = a.shape[2]
        a_args = [a]
        a_specs = [pl.BlockSpec((2, tm, k2), lambda i: (0, i, 0))]
        b_arg, b_spec = b.reshape(2, k2, D_MODEL), pl.BlockSpec((2, k2, D_MODEL), lambda i: (0, 0, 0))
    n_a = len(a_args)

    def body(*refs):
        a_refs, (b_ref, x_ref, dres_ref, g_ref, mod_ref) = refs[:n_a], refs[n_a:n_a + 5]
        if gated:
            y_ref, dx_ref, dsh_ref, dsc_ref, dg_ref, dy_ref, dga_ref = refs[n_a + 5:]
        else:
            dx_ref, dsh_ref, dsc_ref, dg_ref = refs[n_a + 5:]

        @pl.when(pl.program_id(0) == 0)
        def _():
            for ref in (dsh_ref, dsc_ref, dg_ref) + ((dga_ref,) if gated else ()):
                ref[...] = jnp.zeros_like(ref)

        gg = g_ref[...]
        for rs in blocks:
            if pieces:
                wv = b_ref[...].reshape(N_DEV * b_rows, D_MODEL)
                k1, k3 = a_refs[0].shape[1], a_refs[1].shape[2]
                dh = jnp.dot(a_refs[0][rs, :], wv[0:k1], preferred_element_type=F32)
                for t in range(3):
                    dh = dh + jnp.dot(a_refs[1][t, rs, :], wv[k1 + t * k3:k1 + (t + 1) * k3], preferred_element_type=F32)
            else:
                dh = (jnp.dot(a_refs[0][0, rs, :], b_ref[0], preferred_element_type=F32)
                      + jnp.dot(a_refs[0][1, rs, :], b_ref[1], preferred_element_type=F32))
            xx = x_ref[rs, :]
            r = lax.rsqrt(_mean(xx * xx) + EPS)
            xn = xx * r
            dsh_ref[...] += _rsum(dh)
            dsc_ref[...] += _rsum(dh * (xn * gg))
            dt = dh * (1.0 + mod_ref[sc_row:sc_row + 1, :])
            dg_ref[...] += _rsum(dt * xn)
            dxn = dt * gg
            dx = dres_ref[rs, :] + r * (dxn - xn * _mean(dxn * xn))
            dx_ref[rs, :] = dx
            if gated:
                dga_ref[...] += _rsum(dx * y_ref[rs, :])
                dy_ref[rs, :] = (dx * mod_ref[ga_row:ga_row + 1, :]).astype(BF16)

    vec = jax.ShapeDtypeStruct((1, D_MODEL), F32)
    in_specs = a_specs + [b_spec, rows(D_MODEL), rows(D_MODEL), _vec_spec(D_MODEL), _vec_spec(D_MODEL, 8)]
    out_shape = [jax.ShapeDtypeStruct((SEQ, D_MODEL), F32), vec, vec, vec]
    out_specs = [rows(D_MODEL), _vec_spec(D_MODEL), _vec_spec(D_MODEL), _vec_spec(D_MODEL)]
    args = a_args + [b_arg, x, dres, g, mod]
    if gated:
        in_specs.append(rows(D_MODEL))
        out_shape += [jax.ShapeDtypeStruct((SEQ, D_MODEL), BF16), vec]
        out_specs += [rows(D_MODEL), _vec_spec(D_MODEL)]
        args.append(y)
    body, more_specs, more = _with_after(body, len(args), after)
    return pl.pallas_call(
        body, out_shape=tuple(out_shape), grid=(SEQ // tm,), in_specs=in_specs + more_specs, out_specs=tuple(out_specs),
        name=name, compiler_params=_cp("arbitrary"))(*args, *more)


def _prev_halo(halo, width, col):
    per = ROWS // halo
    return pl.BlockSpec((halo, width), lambda i: (jnp.maximum(i * per - 1, 0), col))


def _next_halo(halo, width, col):
    per = ROWS // halo
    last = SEQ // halo - 1
    return pl.BlockSpec((halo, width), lambda i: (jnp.minimum((i + 1) * per, last), col))


CONV_PAD = ROWS + CONV_HALO


def _shift_copies(sh):
    for b in range(1, 8):
        sh[b, 0:CONV_PAD - 8, :] = sh[0, pl.ds(b, CONV_PAD - 8), :]


def _tap(sh, rs_start, offset):
    return sh[offset % 8, pl.ds(pl.multiple_of(rs_start + (offset // 8) * 8, 8), SUB), :]


def _conv_glu(av_ref, ag_ref, avh_ref, agh_ref, sh):
    i = pl.program_id(0)
    hv = avh_ref[...] * _sig(agh_ref[...])
    sh[0, 0:CONV_HALO, :] = jnp.where(i > 0, hv, 0.0)

    def glu(rs):
        sh[0, pl.ds(pl.multiple_of(rs.start + CONV_HALO, SUB), SUB), :] = av_ref[rs, :] * _sig(ag_ref[rs, :])

    _for_chunks(glu)
    _shift_copies(sh)


def _conv_norm(u1, lg_ref, lb_ref):
    mu = _mean(u1)
    cen = u1 - mu
    rs = lax.rsqrt(_mean(cen * cen) + EPS)
    z = cen * rs
    ln = z * lg_ref[...] + lb_ref[...]
    s = _sig(ln)
    return z, rs, ln, s, ln * s


def conv_module_fwd(proj, wc, bc, lg, lb, gco, name):
    def body(av_ref, ag_ref, avh_ref, agh_ref, wc_ref, bc_ref, lg_ref, lb_ref, gco_ref, out_ref, u1_ref, sh):
        _conv_glu(av_ref, ag_ref, avh_ref, agh_ref, sh)

        def conv(rs):
            u1 = jnp.broadcast_to(bc_ref[...], (SUB, D_CONV))
            for j in range(CONV_K):
                u1 = u1 + wc_ref[j:j + 1, :] * _tap(sh, rs.start, CONV_HALO - (CONV_K - 1) + j)
            u1_ref[rs, :] = u1

        _for_chunks(conv)
        _, _, _, _, u2 = _conv_norm(u1_ref[...], lg_ref, lb_ref)
        rc = lax.rsqrt(_mean(u2 * u2) + EPS)
        out_ref[...] = (u2 * rc * gco_ref[...]).astype(BF16)

    v = _vec_spec(D_CONV)
    return pl.pallas_call(
        body, out_shape=(jax.ShapeDtypeStruct((SEQ, D_CONV), BF16), jax.ShapeDtypeStruct((SEQ, D_CONV), F32)),
        grid=(SEQ // ROWS,),
        in_specs=[_row_spec(D_CONV, 0), _row_spec(D_CONV, 1), _prev_halo(CONV_HALO, D_CONV, 0),
                  _prev_halo(CONV_HALO, D_CONV, 1), _vec_spec(D_CONV, CONV_K), v, v, v, v],
        out_specs=(_row_spec(D_CONV), _row_spec(D_CONV)), scratch_shapes=[pltpu.VMEM((8, CONV_PAD, D_CONV), F32)],
        name=name, compiler_params=_cp("parallel"))(proj, proj, proj, proj, wc, bc, lg, lb, gco)


def conv_module_bwd_a(proj, u1, dmixed, lg, lb, gco, name):
    def body(av_ref, ag_ref, avh_ref, agh_ref, u1_ref, dm_ref, lg_ref, lb_ref, gco_ref,
             du1_ref, dgco_ref, dlg_ref, dlb_ref, dbc_ref, dwc_ref, sh, acc):
        first = pl.program_id(0) == 0
        _conv_glu(av_ref, ag_ref, avh_ref, agh_ref, sh)
        z, rs, ln, s, u2 = _conv_norm(u1_ref[...], lg_ref, lb_ref)
        rc = lax.rsqrt(_mean(u2 * u2) + EPS)
        xn = u2 * rc
        dm = dm_ref[...]
        _acc(dgco_ref, _rsum(dm * xn), first)
        dyn = dm * gco_ref[...]
        du2 = rc * (dyn - xn * _mean(dyn * xn))
        dln = du2 * (s * (1.0 + ln * (1.0 - s)))
        _acc(dlg_ref, _rsum(dln * z), first)
        _acc(dlb_ref, _rsum(dln), first)
        dz = dln * lg_ref[...]
        du1 = rs * (dz - _mean(dz) - z * _mean(dz * z))
        du1_ref[...] = du1
        _acc(dbc_ref, _rsum(du1), first)
        acc[...] = jnp.zeros_like(acc)

        def taps(rs):
            d = du1_ref[rs, :]
            for j in range(CONV_K):
                acc[j] += d * _tap(sh, rs.start, CONV_HALO - (CONV_K - 1) + j)

        _for_chunks(taps)

        @pl.when(first)
        def _():
            dwc_ref[...] = jnp.zeros_like(dwc_ref)

        for j in range(CONV_K):
            dwc_ref[j:j + 1, :] += _rsum(acc[j])

    v = _vec_spec(D_CONV)
    vec = jax.ShapeDtypeStruct((1, D_CONV), F32)
    return pl.pallas_call(
        body,
        out_shape=(jax.ShapeDtypeStruct((SEQ, D_CONV), F32), vec, vec, vec, vec, jax.ShapeDtypeStruct((CONV_K, D_CONV), F32)),
        grid=(SEQ // ROWS,),
        in_specs=[_row_spec(D_CONV, 0), _row_spec(D_CONV, 1), _prev_halo(CONV_HALO, D_CONV, 0),
                  _prev_halo(CONV_HALO, D_CONV, 1), _row_spec(D_CONV, 0), _row_spec(D_CONV, 0), v, v, v],
        out_specs=(_row_spec(D_CONV), v, v, v, v, _vec_spec(D_CONV, CONV_K)),
        scratch_shapes=[pltpu.VMEM((8, CONV_PAD, D_CONV), F32), pltpu.VMEM((CONV_K, SUB, D_CONV), F32)],
        name=name, compiler_params=_cp("arbitrary"))(proj, proj, proj, proj, u1, dmixed, lg, lb, gco)


def conv_module_bwd_b(proj, du1, wc, name):
    def body(av_ref, ag_ref, du1_ref, du1n_ref, wc_ref, out_ref, sh):
        i = pl.program_id(0)
        sh[0, 0:ROWS, :] = du1_ref[...]
        sh[0, ROWS:, :] = jnp.where(i < SEQ // ROWS - 1, du1n_ref[...], 0.0)
        _shift_copies(sh)

        def chunk(rs):
            du0 = jnp.zeros((SUB, D_CONV), F32)
            for j in range(CONV_K):
                du0 = du0 + wc_ref[j:j + 1, :] * _tap(sh, rs.start, CONV_K - 1 - j)
            sg = _sig(ag_ref[rs, :])
            out_ref[rs, 0:D_CONV] = (du0 * sg).astype(BF16)
            out_ref[rs, D_CONV:] = (du0 * av_ref[rs, :] * sg * (1.0 - sg)).astype(BF16)

        _for_chunks(chunk)

    return pl.pallas_call(
        body, out_shape=jax.ShapeDtypeStruct((SEQ, 2 * D_CONV), BF16), grid=(SEQ // ROWS,),
        in_specs=[_row_spec(D_CONV, 0), _row_spec(D_CONV, 1), _row_spec(D_CONV, 0), _next_halo(CONV_HALO, D_CONV, 0),
                  _vec_spec(D_CONV, CONV_K)],
        out_specs=_row_spec(2 * D_CONV), scratch_shapes=[pltpu.VMEM((8, CONV_PAD, D_CONV), F32)],
        name=name, compiler_params=_cp("parallel"))(proj, proj, du1, du1, wc)


def _rows(start, size, r):
    return pl.ds(start, size) if r == 1 else pl.ds(start, size, stride=r)


def _unit_rows(r, rho, n, nb):
    win = 2 * ATTN_BLOCK if nb > 1 else ATTN_BLOCK
    if isinstance(n, int):
        kb = max(n - 1, 0)
        q_rows = _rows(rho + r * ATTN_BLOCK * n, ATTN_BLOCK, r)
        k_rows = _rows(rho + r * ATTN_BLOCK * kb, win, r)
    else:
        kb = jnp.maximum(n - 1, 0)
        q_rows = pl.ds(pl.multiple_of(n * ATTN_BLOCK, ATTN_BLOCK), ATTN_BLOCK)
        k_rows = pl.ds(pl.multiple_of(kb * ATTN_BLOCK, ATTN_BLOCK), win)
    dist = (n - kb) * ATTN_BLOCK + lax.broadcasted_iota(jnp.int32, (ATTN_BLOCK, win), 0) \
        - lax.broadcasted_iota(jnp.int32, (ATTN_BLOCK, win), 1)
    return q_rows, k_rows, (dist >= 0) & (dist <= ATTN_BLOCK)


def _per_head(x):
    lane = lax.broadcasted_iota(jnp.int32, x.shape, 1)
    zero = jnp.zeros_like(x)
    return [jnp.where(lane < HEAD_DIM, x, zero), jnp.where(lane >= HEAD_DIM, x, zero)]


SCALE = HEAD_DIM ** -0.5


def _masked_scores(q2, k2, valid):
    return [jnp.where(valid, lax.dot_general(qh, k2, NT, preferred_element_type=F32), NEG) for qh in _per_head(q2)]


def _attn_units(r, nb, unit):
    if r == 1:
        def four(i, carry):
            for k in range(4):
                unit(0, 4 * i + k)
            return carry
        lax.fori_loop(0, nb // 4, four, 0)
    else:
        for rho in range(r):
            for n in range(nb):
                unit(rho, n)


N_UNITS = 16


def attn_fwd_all(proj, name):
    def body(q_ref, k_ref, v_ref, att_ref, lse_ref, s_scr, p_scr, lse_scr, den_scr):
        for idx, (sub_len, r) in enumerate(PATTERNS):
            nb = sub_len // ATTN_BLOCK
            win = 2 * ATTN_BLOCK if nb > 1 else ATTN_BLOCK

            def scores(rho, n, r=r, nb=nb, win=win):
                u = rho * nb + n
                q_rows, k_rows, valid = _unit_rows(r, rho, n, nb)
                ss = _masked_scores((q_ref[q_rows, :] * SCALE).astype(BF16), k_ref[k_rows, :].astype(BF16), valid)
                for h in range(2):
                    s_scr[2 * u + h, :, 0:win] = ss[h]

            _attn_units(r, nb, scores)

            def softmax(u, carry, win=win):
                lses, dens = [], []
                for h in range(2):
                    sc = s_scr[2 * u + h, :, 0:win]
                    m = jnp.max(sc, axis=1, keepdims=True)
                    p = jnp.exp(sc - m)
                    den = jnp.sum(p, axis=1, keepdims=True)
                    p_scr[2 * u + h, :, 0:win] = p.astype(BF16)
                    lses.append(jnp.broadcast_to(m + jnp.log(den), (ATTN_BLOCK, HEAD_DIM)))
                    dens.append(jnp.broadcast_to(den, (ATTN_BLOCK, HEAD_DIM)))
                lse_scr[u] = jnp.concatenate(lses, axis=1)
                den_scr[u] = jnp.concatenate(dens, axis=1)
                return carry

            lax.fori_loop(0, N_UNITS, softmax, 0, unroll=2)

            def outputs(rho, n, r=r, nb=nb, win=win, idx=idx):
                u = rho * nb + n
                q_rows, k_rows, _ = _unit_rows(r, rho, n, nb)
                vs = _per_head(v_ref[k_rows, :].astype(BF16))
                o = (jnp.dot(p_scr[2 * u, :, 0:win], vs[0], preferred_element_type=F32)
                     + jnp.dot(p_scr[2 * u + 1, :, 0:win], vs[1], preferred_element_type=F32)) / den_scr[u]
                lse = lse_scr[u]
                if idx > 0:
                    old = lse_ref[q_rows, :]
                    top = jnp.maximum(old, lse)
                    new = top + jnp.log(jnp.exp(old - top) + jnp.exp(lse - top))
                    o = att_ref[q_rows, :] * jnp.exp(old - new) + o * jnp.exp(lse - new)
                    lse = new
                att_ref[q_rows, :] = o
                lse_ref[q_rows, :] = lse

            _attn_units(r, nb, outputs)

    blk = lambda first: pl.BlockSpec((SEQ, 128), lambda g: (0, first + g))
    shp = jax.ShapeDtypeStruct((SEQ, D_ATTN), F32)
    big = (2 * N_UNITS, ATTN_BLOCK, 2 * ATTN_BLOCK)
    small = pltpu.VMEM((N_UNITS, ATTN_BLOCK, 128), F32)
    return pl.pallas_call(
        body, out_shape=(shp, shp), grid=(4,), in_specs=[blk(8), blk(12), blk(16)], out_specs=(blk(0), blk(0)),
        scratch_shapes=[pltpu.VMEM(big, F32), pltpu.VMEM(big, BF16), small, small],
        name=name, compiler_params=_cp("parallel"))(proj, proj, proj)


def attn_bwd_all(proj, do, lse, dd, name):
    def body(q_ref, k_ref, v_ref, do_ref, l_ref, dd_ref, out_ref, dq_s, dk_s, dv_s,
             s_scr, dp_scr, ds_scr, st_scr, dpt_scr, pt_scr, dst_scr, qb_scr, kb_scr, dob_scr):
        dq_s[...] = jnp.zeros_like(dq_s)
        dk_s[...] = jnp.zeros_like(dk_s)
        dv_s[...] = jnp.zeros_like(dv_s)
        for sub_len, r in PATTERNS:
            nb = sub_len // ATTN_BLOCK
            win = 2 * ATTN_BLOCK if nb > 1 else ATTN_BLOCK

            def scores(rho, n, r=r, nb=nb, win=win):
                u = rho * nb + n
                q_rows, k_rows, valid = _unit_rows(r, rho, n, nb)
                kb = max(n - 1, 0) if isinstance(n, int) else jnp.maximum(n - 1, 0)
                dist_t = (n - kb) * ATTN_BLOCK + lax.broadcasted_iota(jnp.int32, (win, ATTN_BLOCK), 1) \
                    - lax.broadcasted_iota(jnp.int32, (win, ATTN_BLOCK), 0)
                valid_t = (dist_t >= 0) & (dist_t <= ATTN_BLOCK)
                q2 = (q_ref[q_rows, :] * SCALE).astype(BF16)
                kf = k_ref[k_rows, :]
                k2 = kf.astype(BF16)
                do2 = do_ref[q_rows, :].astype(BF16)
                qb_scr[u] = q2
                kb_scr[u, 0:win, :] = (kf * SCALE).astype(BF16)
                dob_scr[u] = do2
                l2 = l_ref[q_rows, :]
                d2 = dd_ref[q_rows, :]
                l2t = l2.T
                d2t = d2.T
                v2 = v_ref[k_rows, :].astype(BF16)
                qs, dos = _per_head(q2), _per_head(do2)
                for h in range(2):
                    c0 = h * HEAD_DIM
                    sc = lax.dot_general(qs[h], k2, NT, preferred_element_type=F32)
                    s_scr[2 * u + h, :, 0:win] = jnp.where(valid, sc, NEG) - l2[:, c0:c0 + 1]
                    dp_scr[2 * u + h, :, 0:win] = lax.dot_general(dos[h], v2, NT, preferred_element_type=F32) \
                        - d2[:, c0:c0 + 1]
                    sct = lax.dot_general(k2, qs[h], NT, preferred_element_type=F32)
                    st_scr[2 * u + h, 0:win, :] = jnp.where(valid_t, sct, NEG) - l2t[c0:c0 + 1, :]
                    dpt_scr[2 * u + h, 0:win, :] = lax.dot_general(v2, dos[h], NT, preferred_element_type=F32) \
                        - d2t[c0:c0 + 1, :]

            _attn_units(r, nb, scores)

            def pointwise(hu, carry, win=win):
                ds_scr[hu, :, 0:win] = (jnp.exp(s_scr[hu, :, 0:win]) * dp_scr[hu, :, 0:win]).astype(BF16)
                pt = jnp.exp(st_scr[hu, 0:win, :])
                pt_scr[hu, 0:win, :] = pt.astype(BF16)
                dst_scr[hu, 0:win, :] = (pt * dpt_scr[hu, 0:win, :]).astype(BF16)
                return carry

            lax.fori_loop(0, 2 * N_UNITS, pointwise, 0, unroll=4)

            def grads(rho, n, r=r, nb=nb, win=win):
                u = rho * nb + n
                q_rows, k_rows, _ = _unit_rows(r, rho, n, nb)
                qs, ks, dos = _per_head(qb_scr[u]), _per_head(kb_scr[u, 0:win, :]), _per_head(dob_scr[u])

                def both(scr, rows, rhs):
                    return (jnp.dot(scr[(2 * u,) + rows], rhs[0], preferred_element_type=F32)
                            + jnp.dot(scr[(2 * u + 1,) + rows], rhs[1], preferred_element_type=F32))

                dq_s[q_rows, :] += both(ds_scr, (slice(None), slice(0, win)), ks)
                dk_s[k_rows, :] += both(dst_scr, (slice(0, win), slice(None)), qs)
                dv_s[k_rows, :] += both(pt_scr, (slice(0, win), slice(None)), dos)

            _attn_units(r, nb, grads)
        out_ref[0] = dq_s[...].astype(BF16)
        out_ref[1] = dk_s[...].astype(BF16)
        out_ref[2] = dv_s[...].astype(BF16)

    blk = lambda first: pl.BlockSpec((SEQ, 128), lambda g: (0, first + g))
    acc = pltpu.VMEM((SEQ, 128), F32)
    big = (2 * N_UNITS, ATTN_BLOCK, 2 * ATTN_BLOCK)
    big_t = (2 * N_UNITS, 2 * ATTN_BLOCK, ATTN_BLOCK)
    return pl.pallas_call(
        body, out_shape=jax.ShapeDtypeStruct((3, SEQ, D_ATTN), BF16), grid=(4,),
        in_specs=[blk(8), blk(12), blk(16), blk(0), blk(0), blk(0)],
        out_specs=pl.BlockSpec((3, SEQ, 128), lambda g: (0, 0, g)),
        scratch_shapes=[acc, acc, acc, pltpu.VMEM(big, F32), pltpu.VMEM(big, F32), pltpu.VMEM(big, BF16),
                        pltpu.VMEM(big_t, F32), pltpu.VMEM(big_t, F32), pltpu.VMEM(big_t, BF16), pltpu.VMEM(big_t, BF16),
                        pltpu.VMEM((N_UNITS, ATTN_BLOCK, 128), BF16),
                        pltpu.VMEM((N_UNITS, 2 * ATTN_BLOCK, 128), BF16), pltpu.VMEM((N_UNITS, ATTN_BLOCK, 128), BF16)],
        name=name, compiler_params=_cp("parallel"))(proj, proj, proj, do, lse, dd)


N_FT = D_FF // FFN_TN


def _ffn_specs():
    per = ROWS // FFN_HALO
    cur_g = pl.BlockSpec((ROWS, FFN_TN), lambda j, i: (i, j))
    cur_v = pl.BlockSpec((ROWS, FFN_TN), lambda j, i: (i, j + N_FT))
    halo_g = pl.BlockSpec((FFN_HALO, FFN_TN), lambda j, i: (jnp.maximum(i * per - 1, 0), j))
    halo_v = pl.BlockSpec((FFN_HALO, FFN_TN), lambda j, i: (jnp.maximum(i * per - 1, 0), j + N_FT))
    w_g = pl.BlockSpec((FFN_K, FFN_TN), lambda j, i: (0, j))
    w_v = pl.BlockSpec((FFN_K, FFN_TN), lambda j, i: (0, j + N_FT))
    b_g = pl.BlockSpec((1, FFN_TN), lambda j, i: (0, j))
    b_v = pl.BlockSpec((1, FFN_TN), lambda j, i: (0, j + N_FT))
    return [cur_g, cur_v, halo_g, halo_v, w_g, w_v, b_g, b_v]


def matmul(a, b, kind, out_dtype, tm, tn, name, b_rows=None):
    stacked = b_rows is not None
    b_shape = (N_DEV * b_rows, b.shape[2]) if stacked else b.shape
    if kind == "nn":
        (m, k), n = a.shape, b_shape[1]
        a_spec = pl.BlockSpec((tm, k), lambda j, i: (i, 0))
        b_spec = pl.BlockSpec((k, tn), lambda j, i: (0, j))
        dims = (((1,), (0,)), ((), ()))
    elif kind == "nt":
        (m, k), n = a.shape, b_shape[0]
        a_spec = pl.BlockSpec((tm, k), lambda j, i: (i, 0))
        b_spec = pl.BlockSpec((tn, k), lambda j, i: (j, 0))
        dims = (((1,), (1,)), ((), ()))
    else:
        (k, m), n = a.shape, b_shape[1]
        a_spec = pl.BlockSpec((k, tm), lambda j, i: (0, i))
        b_spec = pl.BlockSpec((k, tn), lambda j, i: (0, j))
        dims = (((0,), (0,)), ((), ()))
    assert m % tm == 0 and n % tn == 0, (name, m, n, tm, tn)
    if stacked:
        assert b_spec.block_shape[0] == b_shape[0] and kind in ("nn", "nt")
        width = b_spec.block_shape[1]
        b_spec = pl.BlockSpec((N_DEV, b_rows, width), (lambda j, i: (0, 0, j)) if kind == "nn" else (lambda j, i: (0, 0, 0)))

    def body(a_ref, b_ref, o_ref):
        bb = b_ref[...].reshape(b_shape[0], -1) if stacked else b_ref[...]
        o_ref[...] = lax.dot_general(a_ref[...], bb, dims, preferred_element_type=F32).astype(o_ref.dtype)

    return pl.pallas_call(
        body, out_shape=jax.ShapeDtypeStruct((m, n), out_dtype), grid=(n // tn, m // tm),
        in_specs=[a_spec, b_spec], out_specs=pl.BlockSpec((tm, tn), lambda j, i: (i, j)),
        name=name, compiler_params=_cp("parallel", "parallel"))(a, b)


def matmul_tn_pieces(a1, a3, b, name):
    k, n = b.shape
    tm = a3.shape[2]
    n1 = a1.shape[1] // tm

    def body(a1_ref, a3_ref, b_ref, o_ref):
        i = pl.program_id(0)
        tn_dims = (((0,), (0,)), ((), ()))

        @pl.when(i < n1)
        def _():
            o_ref[...] = lax.dot_general(a1_ref[...], b_ref[...], tn_dims, preferred_element_type=F32).astype(BF16)

        @pl.when(i >= n1)
        def _():
            o_ref[...] = lax.dot_general(a3_ref[0], b_ref[...], tn_dims, preferred_element_type=F32).astype(BF16)

    return pl.pallas_call(
        body, out_shape=jax.ShapeDtypeStruct((a1.shape[1] + 3 * tm, n), BF16), grid=(n1 + 3,),
        in_specs=[pl.BlockSpec((k, tm), lambda i: (0, jnp.minimum(i, n1 - 1))),
                  pl.BlockSpec((1, k, tm), lambda i: (jnp.maximum(i - n1, 0), 0, 0)),
                  pl.BlockSpec((k, n), lambda i: (0, 0))],
        out_specs=pl.BlockSpec((tm, n), lambda i: (i, 0)), name=name, compiler_params=_cp("parallel"))(a1, a3, b)


def matmul_tn_halves(a, b, tm, name):
    _, k, m = a.shape
    n = b.shape[1]

    def body(a_ref, b_ref, o_ref):
        o_ref[0] = lax.dot_general(a_ref[0], b_ref[...], (((0,), (0,)), ((), ())), preferred_element_type=F32).astype(BF16)

    return pl.pallas_call(
        body, out_shape=jax.ShapeDtypeStruct((2, m, n), BF16), grid=(2, m // tm),
        in_specs=[pl.BlockSpec((1, k, tm), lambda h, i: (h, 0, i)), pl.BlockSpec((k, n), lambda h, i: (0, 0))],
        out_specs=pl.BlockSpec((1, tm, n), lambda h, i: (h, i, 0)), name=name,
        compiler_params=_cp("parallel", "parallel"))(a, b).reshape(2 * m, n)


def _row_spec(width, col=0):
    return pl.BlockSpec((ROWS, width), lambda i: (i, col))


def _vec_spec(width, rows=1):
    return pl.BlockSpec((rows, width), lambda i: (0, 0))


def _ffn_shifted(cur_ref, halo_ref, pad, s1, s2):
    i = pl.program_id(1)
    pad[0:FFN_HALO, :] = jnp.where(i > 0, halo_ref[...], 0.0)
    pad[FFN_HALO:, :] = cur_ref[0:FFN_HALO, :]
    for k, dst in ((1, s1), (2, s2)):
        dst[0:FFN_HALO, :] = pad[pl.ds(FFN_HALO - k, FFN_HALO), :]
        dst[FFN_HALO:, :] = cur_ref[pl.ds(FFN_HALO - k, ROWS - FFN_HALO), :]


def _ffn_conv(rs, cur_ref, s1, s2, w_ref, b_ref):
    return b_ref[...] + w_ref[0:1, :] * s2[rs, :] + w_ref[1:2, :] * s1[rs, :] + w_ref[2:3, :] * cur_ref[rs, :]


def ffn_act_fwd(up0, wf, bf, name):
    def body(g_ref, v_ref, gh_ref, vh_ref, wg_ref, wv_ref, bg_ref, bv_ref, act_ref, pad, g1, g2, v1, v2):
        _ffn_shifted(g_ref, gh_ref, pad, g1, g2)
        _ffn_shifted(v_ref, vh_ref, pad, v1, v2)

        def chunk(rs):
            gate = _ffn_conv(rs, g_ref, g1, g2, wg_ref, bg_ref)
            val = _ffn_conv(rs, v_ref, v1, v2, wv_ref, bv_ref)
            act_ref[rs, :] = (gate * _sig(gate) * val).astype(BF16)

        _for_chunks(chunk)

    tile = pltpu.VMEM((ROWS, FFN_TN), F32)
    return pl.pallas_call(
        body, out_shape=jax.ShapeDtypeStruct((SEQ, D_FF), BF16), grid=(N_FT, SEQ // ROWS),
        in_specs=_ffn_specs(), out_specs=pl.BlockSpec((ROWS, FFN_TN), lambda j, i: (i, j)),
        scratch_shapes=[pltpu.VMEM((2 * FFN_HALO, FFN_TN), F32), tile, tile, tile, tile],
        name=name, compiler_params=_cp("parallel", "parallel"))(up0, up0, up0, up0, wf, wf, bf, bf)


def ffn_bwd(up0, dact, wf, bf, name, after=None):
    per = ROWS // FFN_HALO
    last = SEQ // FFN_HALO - 1

    def body(g_ref, v_ref, gh_ref, vh_ref, wg_ref, wv_ref, bg_ref, bv_ref, da_ref, gn_ref, vn_ref, dan_ref,
             out_ref, dbg_ref, dbv_ref, dwg_ref, dwv_ref, pad, g1, g2, v1, v2, dgp, dvp, acc):
        i = pl.program_id(1)
        first = i == 0
        _ffn_shifted(g_ref, gh_ref, pad, g1, g2)
        _ffn_shifted(v_ref, vh_ref, pad, v1, v2)
        acc[...] = jnp.zeros_like(acc)

        def grads(gate, val, da):
            s = _sig(gate)
            return da * val * (s * (1.0 + gate * (1.0 - s))), da * (gate * s)

        for c in range(FFN_TN // 128):
            ls = slice(c * 128, (c + 1) * 128)
            wg = [wg_ref[t:t + 1, ls] for t in range(FFN_K)]
            wv = [wv_ref[t:t + 1, ls] for t in range(FFN_K)]
            bg, bv = bg_ref[:, ls], bv_ref[:, ls]
            fold = lambda q: jnp.sum(q.reshape(8, 8, 128), axis=0)

            def block(i, sums, ls=ls, wg=wg, wv=wv, bg=bg, bv=bv):
                rs = pl.ds(pl.multiple_of(i * 64, 64), 64)
                gs = (g2[rs, ls], g1[rs, ls], g_ref[rs, ls])
                vs = (v2[rs, ls], v1[rs, ls], v_ref[rs, ls])
                gate = bg + wg[0] * gs[0] + wg[1] * gs[1] + wg[2] * gs[2]
                val = bv + wv[0] * vs[0] + wv[1] * vs[1] + wv[2] * vs[2]
                dgate, dval = grads(gate, val, da_ref[rs, ls])
                dgp[rs, ls] = dgate
                dvp[rs, ls] = dval
                new = [dgate, dval] + [dgate * gs[t] for t in range(FFN_K)] + [dval * vs[t] for t in range(FFN_K)]
                return tuple(a + fold(q) for a, q in zip(sums, new))

            sums = lax.fori_loop(0, ROWS // 64, block, (jnp.zeros((8, 128), F32),) * (2 + 2 * FFN_K))
            for k in range(2 + 2 * FFN_K):
                acc[k, 0:8, ls] = sums[k]
        _acc(dbg_ref, _rsum(acc[0]), first)
        _acc(dbv_ref, _rsum(acc[1]), first)
        _acc(dwg_ref, jnp.concatenate([_rsum(acc[2 + t]) for t in range(FFN_K)], axis=0), first)
        _acc(dwv_ref, jnp.concatenate([_rsum(acc[5 + t]) for t in range(FFN_K)], axis=0), first)

        def conv_next(cur_ref, nxt_ref, w_ref, b_ref):
            pad[0:FFN_HALO, :] = cur_ref[ROWS - FFN_HALO:, :]
            pad[FFN_HALO:, :] = nxt_ref[...]
            return (b_ref[...] + w_ref[0:1, :] * pad[pl.ds(FFN_HALO - 2, FFN_HALO), :]
                    + w_ref[1:2, :] * pad[pl.ds(FFN_HALO - 1, FFN_HALO), :] + w_ref[2:3, :] * nxt_ref[...])

        gate_n = conv_next(g_ref, gn_ref, wg_ref, bg_ref)
        val_n = conv_next(v_ref, vn_ref, wv_ref, bv_ref)
        dgate_n, dval_n = grads(gate_n, val_n, dan_ref[...])
        inside = i < SEQ // ROWS - 1
        dgp[ROWS:, :] = jnp.where(inside, dgate_n, 0.0)
        dvp[ROWS:, :] = jnp.where(inside, dval_n, 0.0)

        for half, (dp, s1, s2, w_ref) in enumerate(((dgp, g1, g2, wg_ref), (dvp, v1, v2, wv_ref))):
            s1[...] = dp[pl.ds(1, ROWS), :]
            s2[...] = dp[pl.ds(2, ROWS), :]

            def back(rs, dp=dp, s1=s1, s2=s2, w_ref=w_ref, half=half):
                out_ref[half, rs, :] = (w_ref[2:3, :] * dp[rs, :] + w_ref[1:2, :] * s1[rs, :]
                                        + w_ref[0:1, :] * s2[rs, :]).astype(BF16)

            _for_chunks(back)

    body, more_specs, more = _with_after(body, 12, after)
    tile = pltpu.VMEM((ROWS, FFN_TN), F32)
    ext = pltpu.VMEM((ROWS + FFN_HALO, FFN_TN), F32)
    vec = jax.ShapeDtypeStruct((1, D_FF), F32)
    taps = jax.ShapeDtypeStruct((FFN_K, D_FF), F32)
    cur = pl.BlockSpec((ROWS, FFN_TN), lambda j, i: (i, j))
    nxt = lambda off: pl.BlockSpec((FFN_HALO, FFN_TN), lambda j, i: (jnp.minimum((i + 1) * per, last), j + off))
    vs = pl.BlockSpec((1, FFN_TN), lambda j, i: (0, j))
    ts = pl.BlockSpec((FFN_K, FFN_TN), lambda j, i: (0, j))
    return pl.pallas_call(
        body, out_shape=(jax.ShapeDtypeStruct((2, SEQ, D_FF), BF16), vec, vec, taps, taps), grid=(N_FT, SEQ // ROWS),
        in_specs=_ffn_specs() + [cur, nxt(0), nxt(N_FT), nxt(0)] + more_specs,
        out_specs=(pl.BlockSpec((2, ROWS, FFN_TN), lambda j, i: (0, i, j)), vs, vs, ts, ts),
        scratch_shapes=[pltpu.VMEM((2 * FFN_HALO, FFN_TN), F32), tile, tile, tile, tile, ext, ext,
                        pltpu.VMEM((2 + 2 * FFN_K, SUB, FFN_TN), F32)],
        name=name, compiler_params=_cp("parallel", "arbitrary"))(up0, up0, up0, up0, wf, wf, bf, bf, dact, up0, up0, dact,
                                                                 *more)


def ada_fwd(c_all, w_ada, b_cols, name):
    def body(c_ref, w_ref, b_ref, o_ref):
        cc = c_ref[...]
        sc = (cc * _sig(cc)).astype(BF16)
        o_ref[...] = jnp.dot(sc, w_ref[...].astype(BF16), preferred_element_type=F32) + b_ref[...]

    return pl.pallas_call(body, out_shape=jax.ShapeDtypeStruct((N_DEV, w_ada.shape[1]), F32), name=name,
                          compiler_params=_cp())(c_all, w_ada, b_cols)


def _adam(w, g, m, v):
    m = ADAM_B1 * m + (1.0 - ADAM_B1) * g
    v = ADAM_B2 * v + (1.0 - ADAM_B2) * (g * g)
    m_hat = m / (1.0 - ADAM_B1 ** ADAM_STEP)
    v_hat = v / (1.0 - ADAM_B2 ** ADAM_STEP)
    delta = -ADAM_LR * (m_hat / (jnp.sqrt(v_hat) + ADAM_EPS) + ADAM_WD * w)
    return delta, m, v


def ada_bwd_adamw(c_all, dmod_cols, w, m, v, name):
    rows, cols = w.shape
    tr = 256

    def body(c_ref, dm_ref, w_ref, m_ref, v_ref, g_ref, d_ref, nm_ref, nv_ref):
        cc = c_ref[...]
        sc = (cc * _sig(cc)).T
        g = sc[:, 0:1] * dm_ref[0:1, :]
        for b in range(1, N_DEV):
            g = g + sc[:, b:b + 1] * dm_ref[b:b + 1, :]
        g_ref[...] = g
        d_ref[...], nm_ref[...], nv_ref[...] = _adam(w_ref[...], g, m_ref[...], v_ref[...])

    blk = pl.BlockSpec((tr, cols), lambda i: (i, 0))
    shp = jax.ShapeDtypeStruct((rows, cols), F32)
    return pl.pallas_call(
        body, out_shape=(shp, shp, shp, shp), grid=(rows // tr,),
        in_specs=[pl.BlockSpec((N_DEV, tr), lambda i: (0, i)), pl.BlockSpec((N_DEV, cols), lambda i: (0, 0)), blk, blk, blk],
        out_specs=(blk, blk, blk, blk), name=name, compiler_params=_cp("parallel"))(c_all, dmod_cols, w, m, v)


def sum_adamw(parts, mine, me, w, m, v, tr, name):
    n_parts, rows, cols = parts.shape

    def body(me_ref, p_ref, own_ref, w_ref, m_ref, v_ref, g_ref, d_ref, nm_ref, nv_ref):
        def chunk(rs):
            g = own_ref[0, rs, :].astype(F32)
            for k in range(1, n_parts):
                g = g + p_ref[k, rs, :].astype(F32)
            g_ref[rs, :] = g
            d_ref[rs, :], nm_ref[rs, :], nv_ref[rs, :] = _adam(w_ref[rs, :], g, m_ref[rs, :], v_ref[rs, :])

        _for_chunks(chunk, 2, tr)

    blk = pl.BlockSpec((tr, cols), lambda i, me_ref: (i, 0))
    shp = jax.ShapeDtypeStruct((rows, cols), F32)
    grid_spec = pltpu.PrefetchScalarGridSpec(
        num_scalar_prefetch=1, grid=(rows // tr,),
        in_specs=[pl.BlockSpec((n_parts, tr, cols), lambda i, me_ref: (0, i, 0)),
                  pl.BlockSpec((1, tr, cols), lambda i, me_ref: (me_ref[0], i, 0)), blk, blk, blk],
        out_specs=(blk, blk, blk, blk))
    return pl.pallas_call(body, out_shape=(shp, shp, shp, shp), grid_spec=grid_spec, name=name,
                          compiler_params=_cp("parallel"))(me, parts, mine, w, m, v)


MESH = pl.DeviceIdType.MESH


def all_gather(block, name, after=None):
    extra = () if after is None else (after,)

    def body(x_ref, *refs):
        out_ref, send_sems, recv_sems, local_sem = refs[len(extra):]
        x, y, c = lax.axis_index("x"), lax.axis_index("y"), lax.axis_index("c")
        me, sibling = (x, y, c), (x, y, 1 - c)
        chips = [(1 - x, y), (x, 1 - y), (1 - x, 1 - y)]

        def slot(px, py, pc):
            return out_ref.at[4 * px + 2 * py + pc]

        def copy(k, blk, to, src=None):
            return pltpu.make_async_remote_copy(
                src_ref=slot(*blk) if src is None else src, dst_ref=slot(*blk),
                send_sem=send_sems.at[k], recv_sem=recv_sems.at[k], device_id=to, device_id_type=MESH)

        mine = pltpu.make_async_copy(x_ref, slot(*me), local_sem)
        mine.start()
        first = [copy(0, me, sibling, src=x_ref)]
        first += [copy(1 + j, me, (*chip, c), src=x_ref) for j, chip in enumerate(chips)]
        for cp in first:
            cp.start()
        passed = [copy(4 + j, (*chip, c), sibling) for j, chip in enumerate(chips)]
        for j, chip in enumerate(chips):
            copy(1 + j, (*chip, c), me).wait_recv()
            passed[j].start()
        copy(0, sibling, me).wait_recv()
        for j, chip in enumerate(chips):
            copy(4 + j, (*chip, 1 - c), me).wait_recv()
        for cp in first + passed:
            cp.wait_send()
        mine.wait()

    return pl.pallas_call(
        body, out_shape=jax.ShapeDtypeStruct((N_DEV,) + block.shape, block.dtype), in_specs=[ANY] * (1 + len(extra)), out_specs=ANY,
        scratch_shapes=[pltpu.SemaphoreType.DMA((7,)), pltpu.SemaphoreType.DMA((7,)), pltpu.SemaphoreType.DMA],
        name=name)(block, *extra)


HBM = pl.BlockSpec(memory_space=pltpu.HBM)
SEM = pl.BlockSpec(memory_space=pltpu.SEMAPHORE)
EFFECT = pltpu.SideEffectType.DATAFLOW_SIDE_EFFECTING


def _peer_copies(src_ref, land_ref, send_sems, recv_sems, gather):
    x, y, c = lax.axis_index("x"), lax.axis_index("y"), lax.axis_index("c")
    me = 4 * x + 2 * y + c
    copies = []
    for k in range(1, N_DEV):
        px = 1 - x if k & 4 else x
        py = 1 - y if k & 2 else y
        pc = 1 - c if k & 1 else c
        copies.append(pltpu.make_async_remote_copy(
            src_ref=src_ref if gather else src_ref.at[4 * px + 2 * py + pc],
            dst_ref=land_ref.at[me] if gather else land_ref.at[k],
            send_sem=send_sems.at[k - 1], recv_sem=recv_sems.at[k - 1], device_id=(px, py, pc), device_id_type=MESH))
    return copies


def exchange_start(srcs, gather, name, after=None):
    n = len(srcs)
    land_shapes = [(N_DEV,) + src.shape if gather else src.shape for src in srcs]
    extra = () if after is None else (after,)

    def body(*refs):
        src_refs, land_refs = refs[0:n], refs[n:2 * n]
        outs = refs[2 * n + len(extra):]
        for k in range(n):
            for cp in _peer_copies(src_refs[k], land_refs[k], outs[4 * k], outs[4 * k + 1], gather):
                cp.start()
        token = outs[4 * n]
        token[...] = jnp.zeros_like(token)

    out_shape, out_specs, aliases = [], [], {}
    for k, src in enumerate(srcs):
        out_shape += [pltpu.SemaphoreType.DMA((N_DEV - 1,)), pltpu.SemaphoreType.DMA((N_DEV - 1,)),
                      pltpu.HBM(src.shape, src.dtype), pltpu.HBM(land_shapes[k], src.dtype)]
        out_specs += [SEM, SEM, HBM, HBM]
        aliases[k] = 4 * k + 2
        aliases[n + k] = 4 * k + 3
    out_shape.append(jax.ShapeDtypeStruct((8, 128), F32))
    out_specs.append(pl.BlockSpec(memory_space=pltpu.VMEM))
    res = pl.pallas_call(
        body, name=name, out_shape=tuple(out_shape), in_specs=(HBM,) * (2 * n) + (ANY,) * len(extra),
        out_specs=tuple(out_specs), input_output_aliases=aliases,
        compiler_params=pltpu.CompilerParams(has_side_effects=EFFECT),
    )(*[pltpu.with_memory_space_constraint(src, pltpu.HBM) for src in srcs],
      *[pltpu.with_memory_space_constraint(lax.empty(shp, src.dtype), pltpu.HBM) for shp, src in zip(land_shapes, srcs)],
      *extra)
    return [tuple(res[4 * k:4 * k + 4]) for k in range(n)], res[4 * n]


def _stage1_peer(i):
    x, y, c = lax.axis_index("x"), lax.axis_index("y"), lax.axis_index("c")
    if i == 0:
        return (x, y, 1 - c)
    return (1 - x if i & 1 else x, 1 - y if i & 2 else y, c)


def _slot_of(peer):
    return 4 * peer[0] + 2 * peer[1] + peer[2]


def _stage1_copy(i, src_ref, land_ref, send_sems, recv_sems):
    me = _slot_of((lax.axis_index("x"), lax.axis_index("y"), lax.axis_index("c")))
    return pltpu.make_async_remote_copy(src_ref=src_ref, dst_ref=land_ref.at[me], send_sem=send_sems.at[i],
                                        recv_sem=recv_sems.at[i], device_id=_stage1_peer(i), device_id_type=MESH)


def _stage2_copy(j, land_ref, send_sems, recv_sems):
    slot = _slot_of(_stage1_peer(j + 1))
    return pltpu.make_async_remote_copy(src_ref=land_ref.at[slot], dst_ref=land_ref.at[slot], send_sem=send_sems.at[j],
                                        recv_sem=recv_sems.at[j], device_id=_stage1_peer(0), device_id_type=MESH)


def gather2_start(srcs, name, after=None):
    n = len(srcs)
    extra = () if after is None else (after,)

    def body(*refs):
        src_refs, land_refs = refs[0:n], refs[n:2 * n]
        outs = refs[2 * n + len(extra):]
        for k in range(n):
            for i in range(4):
                _stage1_copy(i, src_refs[k], land_refs[k], outs[4 * k], outs[4 * k + 1]).start()
        outs[4 * n][...] = jnp.zeros((8, 128), F32)

    out_shape, out_specs, aliases = [], [], {}
    for k, src in enumerate(srcs):
        out_shape += [pltpu.SemaphoreType.DMA((4,)), pltpu.SemaphoreType.DMA((4,)),
                      pltpu.HBM(src.shape, src.dtype), pltpu.HBM((N_DEV,) + src.shape, src.dtype)]
        out_specs += [SEM, SEM, HBM, HBM]
        aliases[k] = 4 * k + 2
        aliases[n + k] = 4 * k + 3
    out_shape.append(jax.ShapeDtypeStruct((8, 128), F32))
    out_specs.append(pl.BlockSpec(memory_space=pltpu.VMEM))
    res = pl.pallas_call(
        body, name=name, out_shape=tuple(out_shape), in_specs=(HBM,) * (2 * n) + (ANY,) * len(extra),
        out_specs=tuple(out_specs), input_output_aliases=aliases,
        compiler_params=pltpu.CompilerParams(has_side_effects=EFFECT),
    )(*[pltpu.with_memory_space_constraint(src, pltpu.HBM) for src in srcs],
      *[pltpu.with_memory_space_constraint(lax.empty((N_DEV,) + src.shape, src.dtype), pltpu.HBM) for src in srcs], *extra)
    return [dict(send1=res[4 * k], recv1=res[4 * k + 1], src=res[4 * k + 2], land=res[4 * k + 3]) for k in range(n)], \
        res[4 * n]


def gather2_pass(handles, after, name):
    n = len(handles)

    def body(*refs):
        src_refs, land_refs, recv1 = refs[0:n], refs[n:2 * n], refs[2 * n:3 * n]
        outs = refs[3 * n + 1:]
        for k in range(n):
            for j in range(3):
                _stage1_copy(j + 1, src_refs[k], land_refs[k], recv1[k], recv1[k]).wait_recv()
                _stage2_copy(j, land_refs[k], outs[3 * k], outs[3 * k + 1]).start()

    out_shape, out_specs, aliases = [], [], {}
    for k, h in enumerate(handles):
        out_shape += [pltpu.SemaphoreType.DMA((3,)), pltpu.SemaphoreType.DMA((3,)), pltpu.HBM(h["land"].shape, h["land"].dtype)]
        out_specs += [SEM, SEM, HBM]
        aliases[n + k] = 3 * k + 2
    res = pl.pallas_call(
        body, name=name, out_shape=tuple(out_shape), in_specs=(HBM,) * (2 * n) + (SEM,) * n + (ANY,),
        out_specs=tuple(out_specs), input_output_aliases=aliases,
        compiler_params=pltpu.CompilerParams(has_side_effects=EFFECT),
    )(*[h["src"] for h in handles], *[h["land"] for h in handles], *[h["recv1"] for h in handles], after)
    return [dict(h, send2=res[3 * k], recv2=res[3 * k + 1], land=res[3 * k + 2]) for k, h in enumerate(handles)]


def gather2_wait(h, after, name):
    def body(src_ref, land_ref, send1, recv1, send2, recv2, after_ref, src_dead, got_ref):
        for i in range(4):
            _stage1_copy(i, src_ref, land_ref, send1, recv1).wait_send()
        _stage1_copy(0, src_ref, land_ref, send1, recv1).wait_recv()
        for j in range(3):
            cp = _stage2_copy(j, land_ref, send2, recv2)
            cp.wait_send()
            cp.wait_recv()

    return pl.pallas_call(
        body, name=name,
        out_shape=(pltpu.HBM(h["src"].shape, h["src"].dtype), pltpu.HBM(h["land"].shape, h["land"].dtype)),
        in_specs=(HBM, HBM, SEM, SEM, SEM, SEM, ANY), out_specs=(HBM, HBM), input_output_aliases={0: 0, 1: 1},
        compiler_params=pltpu.CompilerParams(has_side_effects=EFFECT),
    )(h["src"], h["land"], h["send1"], h["recv1"], h["send2"], h["recv2"], after)[1]


def exchange_wait(handles, after, gather, name):
    send_sems, recv_sems, src_thru, land_thru = handles

    def body(src_ref, land_ref, send_sems, recv_sems, after_ref, src_dead, got_ref):
        for cp in _peer_copies(src_ref, land_ref, send_sems, recv_sems, gather):
            cp.wait_send()
            cp.wait_recv()

    return pl.pallas_call(
        body, name=name,
        out_shape=(pltpu.HBM(src_thru.shape, src_thru.dtype), pltpu.HBM(land_thru.shape, land_thru.dtype)),
        in_specs=(HBM, HBM, SEM, SEM, ANY), out_specs=(HBM, HBM), input_output_aliases={0: 0, 1: 1},
        compiler_params=pltpu.CompilerParams(has_side_effects=EFFECT),
    )(src_thru, land_thru, send_sems, recv_sems, after)


def local_step(x, tgt, mod, started, get_w, put_grad, wc, wf, g_mix, bc, lg, lb, gco, gao, g_ffn, bf, g_fin):
    w_in = get_w("w_in", mod)
    proj, h1 = rms_mod_matmul(x, g_mix, mod, 0, 1, w_in, D_IN // N_DEV, "proj_fwd", after=started)
    mix_a, u1 = conv_module_fwd(proj, wc, bc, lg, lb, gco, "conv_module_fwd")
    att, lse = attn_fwd_all(proj, "attn_fwd")
    w_out = get_w("w_out", att)
    y1, mixed = norm_concat_matmul(mix_a, att, gao, w_out, "out_proj_fwd")
    w_up = get_w("w_up", y1)
    up0, x1, h2 = resid_rms_mod_matmul(x, y1, g_ffn, mod, 2, 3, 4, w_up, "up_fwd")
    act = ffn_act_fwd(up0, wf, bf, "ffn_act_fwd")
    w_down = get_w("w_down", act)
    loss_t, dx2, dy2, d_gfin, d_gaf = matmul_loss_bwd(act, w_down, x1, tgt, g_fin, mod, 5, "down_fwd_loss")
    dact = matmul(dy2, w_down, "nt", F32, 512, FFN_TN, "down_bwd_x")
    dw_down = matmul(act, dy2, "tn", BF16, 256, D_MODEL, "down_bwd_w")
    dup0, dbf_g, dbf_v, dwf_g, dwf_v = ffn_bwd(up0, dact, wf, bf, "ffn_bwd", after=put_grad("w_down", dw_down))
    dw_up = matmul_tn_halves(dup0, h2, 256, "up_bwd_w")
    dx1, d_shf, d_scf, d_gffn, dy1, d_gam = matmul_rms_mod_bwd(
        dup0, w_up, x1, dx2, g_ffn, mod, 4, y1, 2, "up_bwd_x", after=put_grad("w_up", dw_up))
    dw_out = matmul(mixed, dy1, "tn", BF16, 256, D_MODEL, "out_proj_bwd_w")
    dmixed, do, dd, d_gao = matmul_combine_bwd(dy1, w_out, att, gao, "out_proj_bwd_x", after=put_grad("w_out", dw_out))
    dqkv = attn_bwd_all(proj, do, lse, dd, "attn_bwd")
    du1, d_gco, d_lg, d_lb, d_bc, d_wc = conv_module_bwd_a(proj, u1, dmixed, lg, lb, gco, "conv_module_bwd_a")
    dproj_a = conv_module_bwd_b(proj, du1, wc, "conv_module_bwd_b")
    dw_in = matmul_tn_pieces(dproj_a, dqkv, h1, "proj_bwd_w")
    dx, d_shm, d_scm, d_gmix = matmul_rms_mod_bwd(
        (dproj_a, dqkv), w_in, x, dx1, g_mix, mod, 1, None, 0, "proj_bwd_x", b_rows=D_IN // N_DEV,
        after=put_grad("w_in", dw_in))
    dmod = jnp.concatenate([d_shm, d_scm, d_gam, d_shf, d_scf, d_gaf], axis=1)
    small = dict(g_norm_mix=d_gmix, b_conv_dw=d_bc, ln_conv_g=d_lg, ln_conv_b=d_lb, g_conv_out=d_gco, g_attn_out=d_gao,
                 g_norm_ffn=d_gffn, b_ffn_dw=jnp.concatenate([dbf_g, dbf_v], axis=1), g_final=d_gfin,
                 w_conv_dw=d_wc, w_ffn_dw=jnp.concatenate([dwf_g, dwf_v], axis=1), dmod=dmod, loss=loss_t[0:1, 0:1])
    return dx, small


def _padw(a, width):
    return jnp.pad(a, ((0, 0), (0, width - a.shape[1])))


def pack_small(t):
    wide = jnp.concatenate([_padw(t["dmod"], PACK_W), _padw(t["b_ffn_dw"], PACK_W), _padw(t["w_ffn_dw"], PACK_W),
                            _padw(t["loss"], PACK_W), jnp.zeros((2, PACK_W), F32)], axis=0)
    z512 = jnp.zeros((1, 512), F32)
    narrow = jnp.concatenate([
        t["g_norm_mix"], t["g_norm_ffn"], t["g_final"],
        jnp.concatenate([t["b_conv_dw"], t["ln_conv_g"]], axis=1),
        jnp.concatenate([t["ln_conv_b"], t["g_conv_out"]], axis=1),
        jnp.concatenate([t["g_attn_out"], z512], axis=1),
        jnp.zeros((2, 1024), F32),
        jnp.pad(t["w_conv_dw"], ((0, 1), (0, 0))).reshape(16, 1024)], axis=0)
    return jnp.concatenate([wide, narrow.reshape(4, PACK_W), jnp.zeros((4, PACK_W), F32)], axis=0)


_NARROW = lambda k, off=0: (8 + k // 6, (k % 6) * 1024 + off)
PACKED_AT = dict(
    b_ada=(0, 0, N_MOD * D_MODEL), b_ffn_dw=(1, 0, 2 * D_FF),
    g_norm_mix=_NARROW(0) + (D_MODEL,), g_norm_ffn=_NARROW(1) + (D_MODEL,), g_final=_NARROW(2) + (D_MODEL,),
    b_conv_dw=_NARROW(3) + (D_CONV,), ln_conv_g=_NARROW(3, 512) + (D_CONV,),
    ln_conv_b=_NARROW(4) + (D_CONV,), g_conv_out=_NARROW(4, 512) + (D_CONV,), g_attn_out=_NARROW(5) + (D_ATTN,))
SMALL_ORDER = list(PACKED_AT)


def small_adamw(parts, wmv, name):
    def body(*refs):
        p_ref = refs[0]
        ins = refs[1:1 + 3 * len(SMALL_ORDER)]
        outs = refs[1 + 3 * len(SMALL_ORDER):]
        g = p_ref[0]
        for k in range(1, N_DEV):
            g = g + p_ref[k]
        for i, n in enumerate(SMALL_ORDER):
            row, lane, width = PACKED_AT[n]
            gp = g[row:row + 1, lane:lane + width]
            w_ref, m_ref, v_ref = ins[3 * i:3 * i + 3]
            g_ref, d_ref, nm_ref, nv_ref = outs[4 * i:4 * i + 4]
            g_ref[...] = gp
            d_ref[...], nm_ref[...], nv_ref[...] = _adam(w_ref[...], gp, m_ref[...], v_ref[...])
        wc_ref, wf_ref, loss_ref = outs[4 * len(SMALL_ORDER):]
        for j in range(CONV_K):
            row, lane = _NARROW(8 + j // 2, (j % 2) * 512)
            wc_ref[j:j + 1, :] = g[row:row + 1, lane:lane + D_CONV]
        wf_ref[...] = g[2:2 + FFN_K, 0:2 * D_FF]
        loss_ref[...] = jnp.broadcast_to(g[5:6, 0:1], (8, 128))

    args, out_shape = [parts], []
    for n in SMALL_ORDER:
        args += list(wmv[n])
        out_shape += [jax.ShapeDtypeStruct(wmv[n][0].shape, F32)] * 4
    out_shape += [jax.ShapeDtypeStruct((CONV_K, D_CONV), F32), jax.ShapeDtypeStruct((FFN_K, 2 * D_FF), F32),
                  jax.ShapeDtypeStruct((8, 128), F32)]
    res = pl.pallas_call(body, out_shape=tuple(out_shape), name=name, compiler_params=_cp())(*args)
    per = {n: tuple(res[4 * i:4 * i + 4]) for i, n in enumerate(SMALL_ORDER)}
    return per, res[-3], res[-2], res[-1][0, 0]


def shard_adamw(items, name):
    def body(*refs):
        ins, outs = refs[:4 * len(items)], refs[4 * len(items):]
        for i in range(len(items)):
            g_ref, w_ref, m_ref, v_ref = ins[4 * i:4 * i + 4]
            og_ref, d_ref, nm_ref, nv_ref = outs[4 * i:4 * i + 4]
            og_ref[...] = g_ref[...]
            d_ref[...], nm_ref[...], nv_ref[...] = _adam(w_ref[...], g_ref[...], m_ref[...], v_ref[...])

    args = [a for item in items for a in item]
    out_shape = tuple(jax.ShapeDtypeStruct(item[1].shape, F32) for item in items for _ in range(4))
    res = pl.pallas_call(body, out_shape=out_shape, name=name, compiler_params=_cp())(*args)
    return [tuple(res[4 * i:4 * i + 4]) for i in range(len(items))]


def _shard(full, n_cols, me):
    return lax.dynamic_slice(full, (0, me * n_cols), (full.shape[0], n_cols))


WEIGHTS = ["w_ada", "b_ada", "g_norm_mix", "w_in", "w_conv_dw", "b_conv_dw", "ln_conv_g", "ln_conv_b", "g_conv_out",
           "g_attn_out", "w_out", "g_norm_ffn", "w_up", "w_ffn_dw", "b_ffn_dw", "w_down", "g_final"]


def kernel(x, c, w_ada, b_ada, g_norm_mix, w_in, w_conv_dw, b_conv_dw, ln_conv_g, ln_conv_b, g_conv_out, g_attn_out, w_out, g_norm_ffn, w_up, w_ffn_dw, b_ffn_dw, w_down, g_final, loss_target, m_w_ada, m_b_ada, m_g_norm_mix, m_w_in, m_w_conv_dw, m_b_conv_dw, m_ln_conv_g, m_ln_conv_b, m_g_conv_out, m_g_attn_out, m_w_out, m_g_norm_ffn, m_w_up, m_w_ffn_dw, m_b_ffn_dw, m_w_down, m_g_final, v_w_ada, v_b_ada, v_g_norm_mix, v_w_in, v_w_conv_dw, v_b_conv_dw, v_ln_conv_g, v_ln_conv_b, v_g_conv_out, v_g_attn_out, v_w_out, v_g_norm_ffn, v_w_up, v_w_ffn_dw, v_b_ffn_dw, v_w_down, v_g_final):
    args = dict(locals())
    me = 4 * lax.axis_index("x") + 2 * lax.axis_index("y") + lax.axis_index("c")
    me1 = me.astype(jnp.int32).reshape(1)

    def flat(name, prefix=""):
        a = args[prefix + name]
        return a.reshape(a.shape[-2] if a.ndim > 1 else 1, a.shape[-1])

    def flat_t(name, prefix=""):
        return args[prefix + name][0].T

    n_in, n_up, r_out, r_down = w_in.shape[2], w_up.shape[2], w_out.shape[1], w_down.shape[1]
    n_ada, n_wc, n_wf = w_ada.shape[2], w_conv_dw.shape[2], w_ffn_dw.shape[2]
    taps_c = jnp.pad(flat("w_conv_dw").reshape(1, CONV_K * n_wc), ((0, 0), (0, 2 * D_MODEL - CONV_K * n_wc)))
    taps_f = jnp.pad(flat("w_ffn_dw").reshape(1, FFN_K * n_wf), ((0, 0), (0, 3 * D_MODEL - FFN_K * n_wf)))
    first = jnp.concatenate([c, taps_c.reshape(2, D_MODEL), taps_f.reshape(3, D_MODEL), jnp.zeros((2, D_MODEL), F32)], axis=0)
    w_in_block = flat_t("w_in").astype(BF16)
    hi = lax.reduce_precision(first, 8, 7)
    mid = lax.reduce_precision(first - hi, 8, 7)
    low = lax.reduce_precision(first - hi - mid, 8, 7)
    terms = jnp.concatenate([hi, mid, low, jnp.zeros((8, D_MODEL), F32)], axis=0).astype(BF16)
    first_block = all_gather(jnp.concatenate([w_in_block, terms], axis=0), "gather_c_taps_w_in")
    terms = first_block[:, n_in:n_in + 24, :].astype(F32)
    first_all = (terms[:, 0:8] + terms[:, 8:16]) + terms[:, 16:24]
    c_all = first_all[:, 0, :]
    wc_full = first_all[:, 1:3, :].reshape(N_DEV, 2 * D_MODEL)[:, :CONV_K * n_wc].reshape(N_DEV, CONV_K, n_wc)
    wc_full = wc_full.transpose(1, 0, 2).reshape(CONV_K, D_CONV)
    wf_full = first_all[:, 3:6, :].reshape(N_DEV, 3 * D_MODEL)[:, :FFN_K * n_wf].reshape(N_DEV, FFN_K, n_wf)
    wf_full = wf_full.transpose(1, 0, 2).reshape(FFN_K, 2 * D_FF)
    mod_cols = ada_fwd(c_all, flat("w_ada"), _shard(flat("b_ada"), n_ada, me), "ada_fwd")
    mod_all = all_gather(mod_cols, "gather_mod")
    mod = lax.dynamic_index_in_dim(mod_all, me, axis=1, keepdims=False).reshape(N_MOD, D_MODEL)
    mod = jnp.pad(mod, ((0, 2), (0, 0)))

    order = ("w_out", "w_up", "w_down")
    blocks = dict(w_up=flat_t("w_up").astype(BF16), w_out=flat("w_out").astype(BF16), w_down=flat("w_down").astype(BF16))
    handles, tok = gather2_start([blocks[name] for name in order], "gather_weights_start", mod_all)
    gathers = dict(zip(order, handles))

    def gathered(name, after):
        if "send2" not in gathers[name]:
            group = ("w_out", "w_up") if name != "w_down" else ("w_down",)
            gathers.update(zip(group, gather2_pass([gathers[w] for w in group], after, f"gather_{name}_pass")))
        land = gather2_wait(gathers[name], after, f"gather_{name}_wait")
        return lax.dynamic_update_index_in_dim(land, blocks[name], me, axis=0)

    def get_w(name, after):
        if name == "w_in":
            return first_block
        return gathered(name, after).reshape(-1, D_MODEL)

    exchanges = {}

    def put_grad(name, dw, after=None):
        dev_major = dw.reshape(N_DEV, -1, D_MODEL)
        (exchanges[name],), token = exchange_start([dev_major], False, f"exchange_{name}_start", after)
        return token

    grad_x, small = local_step(
        x[0], loss_target[0], mod, tok, get_w, put_grad, wc_full, wf_full,
        flat("g_norm_mix"), flat("b_conv_dw"), flat("ln_conv_g"), flat("ln_conv_b"), flat("g_conv_out"),
        flat("g_attn_out"), flat("g_norm_ffn"), flat("b_ffn_dw"), flat("g_final"))

    out = {}

    def finish(name, tr, after):
        mine, parts = exchange_wait(exchanges[name], after, False, f"exchange_{name}_wait")
        if name in ("w_in", "w_up"):
            res = sum_adamw(parts, mine, me1, flat_t(name), flat_t(name, "m_"), flat_t(name, "v_"), tr, "adamw_" + name)
            out[name] = tuple(r.T for r in res)
        else:
            res = out[name] = sum_adamw(parts, mine, me1, flat(name), flat(name, "m_"), flat(name, "v_"), tr,
                                        "adamw_" + name)
        return res[0]

    after = finish("w_down", r_down, grad_x)
    after = finish("w_up", n_up // 2, after)
    after = finish("w_out", r_out, after)
    after = finish("w_in", n_in, after)

    small_all = all_gather(pack_small(small), "gather_small", after)

    wmv = {n: (flat(n), flat(n, "m_"), flat(n, "v_")) for n in SMALL_ORDER}
    per, g_wc, g_wf, loss = small_adamw(small_all, wmv, "adamw_small")
    out.update(per)
    taps = shard_adamw([(_shard(g_wc, n_wc, me), flat("w_conv_dw"), flat("w_conv_dw", "m_"), flat("w_conv_dw", "v_")),
                        (_shard(g_wf, n_wf, me), flat("w_ffn_dw"), flat("w_ffn_dw", "m_"), flat("w_ffn_dw", "v_"))],
                       "adamw_taps")
    out["w_conv_dw"], out["w_ffn_dw"] = taps

    dmod_cols = _shard(small_all[:, 0, :], n_ada, me)
    out["w_ada"] = ada_bwd_adamw(c_all, dmod_cols, flat("w_ada"), flat("w_ada", "m_"), flat("w_ada", "v_"), "adamw_w_ada")

    result = [loss, grad_x[None]]
    for k in range(4):
        result += [out[n][k].reshape(args[n].shape) for n in WEIGHTS]
    return tuple(result)
```

```python
import functools

import jax
import jax.numpy as jnp
from jax import lax
from jax.experimental import pallas as pl
from jax.experimental.pallas import tpu as pltpu

F32 = jnp.float32
BF16 = jnp.bfloat16

N_DEV = 8
SEQ = 2048
D_MODEL = 1024
D_CONV = 512
D_ATTN = 512
HEAD_DIM = 64
CONV_K = 31
D_FF = 2816
FFN_K = 3
D_IN = 2 * D_CONV + 3 * D_ATTN
N_MOD = 6
EPS = 1e-6
ATTN_BLOCK = 128
PATTERNS = ((2048, 1), (512, 4), (128, 16))
NEG = -1e30

ADAM_LR, ADAM_B1, ADAM_B2, ADAM_EPS, ADAM_WD, ADAM_STEP = 0.001, 0.9, 0.999, 1e-08, 0.01, 10

ROWS = 256
CONV_HALO = 32
FFN_HALO = 8
FFN_TN = 1408
VMEM_LIMIT = 56 * 1024 * 1024
PACK_W = 6144


NT = (((1,), (1,)), ((), ()))
ANY = pl.BlockSpec(memory_space=pl.ANY)


def _cp(*sem):
    return pltpu.CompilerParams(dimension_semantics=sem if sem else None, vmem_limit_bytes=VMEM_LIMIT)


def _with_after(body, n_in, after):
    if after is None:
        return body, [], []
    return (lambda *refs: body(*refs[:n_in], *refs[n_in + 1:])), [ANY], [after]


def _sig(x):
    return 1.0 / (1.0 + jnp.exp(-x))


def _rsum(x):
    return jnp.sum(x, axis=0, keepdims=True)


def _mean(x):
    return jnp.mean(x, axis=-1, keepdims=True)


def _acc(ref, val, first):
    @pl.when(first)
    def _():
        ref[...] = val

    @pl.when(jnp.logical_not(first))
    def _():
        ref[...] += val


SUB = 16


def _for_chunks(fn, unroll=1, rows=ROWS):
    def step(i, carry):
        fn(pl.ds(pl.multiple_of(i * SUB, SUB), SUB))
        return carry

    lax.fori_loop(0, rows // SUB, step, 0, unroll=unroll)


PAIR = 2 * ROWS


def _pair_spec(width, col=0):
    return pl.BlockSpec((PAIR, width), lambda i: (i, col))


def _halves():
    return [slice(h * ROWS, (h + 1) * ROWS) for h in range(2)]


def rms_mod_matmul(x, g, mod, sh_row, sc_row, b, b_rows, name, after=None):
    n = N_DEV * b_rows

    def body(x_ref, g_ref, mod_ref, b_ref, o_ref, h_ref):
        for rs in _halves():
            xx = x_ref[rs, :]
            r = lax.rsqrt(_mean(xx * xx) + EPS)
            h = (xx * r * g_ref[...] * (1.0 + mod_ref[sc_row:sc_row + 1, :]) + mod_ref[sh_row:sh_row + 1, :]).astype(BF16)
            h_ref[rs, :] = h
            o_ref[rs, :] = lax.dot_general(h, b_ref[...].reshape(n, D_MODEL), NT, preferred_element_type=F32)

    body, more_specs, more = _with_after(body, 4, after)
    return pl.pallas_call(
        body, out_shape=(jax.ShapeDtypeStruct((SEQ, n), F32), jax.ShapeDtypeStruct((SEQ, D_MODEL), BF16)),
        grid=(SEQ // PAIR,),
        in_specs=[_pair_spec(D_MODEL), _vec_spec(D_MODEL), _vec_spec(D_MODEL, 8),
                  pl.BlockSpec((N_DEV, b_rows, D_MODEL), lambda i: (0, 0, 0))] + more_specs,
        out_specs=(_pair_spec(n), _pair_spec(D_MODEL)), name=name, compiler_params=_cp("parallel"))(x, g, mod, b, *more)


def norm_concat_matmul(mix_a, att, gao, w, name):
    def body(a_ref, att_ref, g_ref, w_ref, y_ref, mixed_ref):
        for rs in _halves():
            aa = att_ref[rs, :]
            mixed_ref[rs, 0:D_CONV] = a_ref[rs, :]
            mixed_ref[rs, D_CONV:] = (aa * lax.rsqrt(_mean(aa * aa) + EPS) * g_ref[...]).astype(BF16)
            y_ref[rs, :] = jnp.dot(mixed_ref[rs, :], w_ref[...], preferred_element_type=F32)

    return pl.pallas_call(
        body, out_shape=(jax.ShapeDtypeStruct((SEQ, D_MODEL), F32), jax.ShapeDtypeStruct((SEQ, D_MODEL), BF16)),
        grid=(SEQ // PAIR,),
        in_specs=[_pair_spec(D_CONV), _pair_spec(D_ATTN), _vec_spec(D_ATTN), pl.BlockSpec(w.shape, lambda i: (0, 0))],
        out_specs=(_pair_spec(D_MODEL), _pair_spec(D_MODEL)), name=name, compiler_params=_cp("parallel"))(mix_a, att, gao, w)


def resid_rms_mod_matmul(x, y, g, mod, ga_row, sh_row, sc_row, w, name):
    n = w.shape[0]

    def body(x_ref, y_ref, g_ref, mod_ref, w_ref, o_ref, x1_ref, h_ref):
        x1 = x_ref[...] + mod_ref[ga_row:ga_row + 1, :] * y_ref[...]
        x1_ref[...] = x1
        r = lax.rsqrt(_mean(x1 * x1) + EPS)
        h = (x1 * r * g_ref[...] * (1.0 + mod_ref[sc_row:sc_row + 1, :]) + mod_ref[sh_row:sh_row + 1, :]).astype(BF16)
        h_ref[...] = h
        o_ref[...] = lax.dot_general(h, w_ref[...], NT, preferred_element_type=F32)

    return pl.pallas_call(
        body,
        out_shape=(jax.ShapeDtypeStruct((SEQ, n), F32), jax.ShapeDtypeStruct((SEQ, D_MODEL), F32),
                   jax.ShapeDtypeStruct((SEQ, D_MODEL), BF16)),
        grid=(SEQ // ROWS,),
        in_specs=[_row_spec(D_MODEL), _row_spec(D_MODEL), _vec_spec(D_MODEL), _vec_spec(D_MODEL, 8),
                  pl.BlockSpec(w.shape, lambda i: (0, 0))],
        out_specs=(_row_spec(n), _row_spec(D_MODEL), _row_spec(D_MODEL)),
        name=name, compiler_params=_cp("parallel"))(x, y, g, mod, w)


def matmul_combine_bwd(dy, w, att, gao, name, after=None):
    def body(dy_ref, w_ref, att_ref, g_ref, dm_ref, do_ref, dd_ref, dg_ref):
        @pl.when(pl.program_id(0) == 0)
        def _():
            dg_ref[...] = jnp.zeros_like(dg_ref)

        same_head = (jnp.right_shift(lax.broadcasted_iota(jnp.int32, (D_ATTN, D_ATTN), 0), 6)
                     == jnp.right_shift(lax.broadcasted_iota(jnp.int32, (D_ATTN, D_ATTN), 1), 6)).astype(F32)
        for rs in _halves():
            dmixed = lax.dot_general(dy_ref[rs, :], w_ref[...], NT, preferred_element_type=F32)
            dm_ref[rs, :] = dmixed
            att = att_ref[rs, :]
            r = lax.rsqrt(_mean(att * att) + EPS)
            xn = att * r
            dm = dmixed[:, D_CONV:]
            dg_ref[...] += _rsum(dm * xn)
            dyn = dm * g_ref[...]
            do = r * (dyn - xn * _mean(dyn * xn))
            do_ref[rs, :] = do
            dd_ref[rs, :] = jnp.dot(do * att, same_head, preferred_element_type=F32, precision=lax.Precision.HIGHEST)

    rs = _pair_spec(D_ATTN)
    f = jax.ShapeDtypeStruct((SEQ, D_ATTN), F32)
    body, more_specs, more = _with_after(body, 4, after)
    return pl.pallas_call(
        body, out_shape=(jax.ShapeDtypeStruct((SEQ, D_MODEL), F32), f, f, jax.ShapeDtypeStruct((1, D_ATTN), F32)),
        grid=(SEQ // PAIR,),
        in_specs=[_pair_spec(D_MODEL), pl.BlockSpec(w.shape, lambda i: (0, 0)), rs, _vec_spec(D_ATTN)] + more_specs,
        out_specs=(_pair_spec(D_MODEL), rs, rs, _vec_spec(D_ATTN)),
        name=name, compiler_params=_cp("arbitrary"))(dy, w, att, gao, *more)


def matmul_loss_bwd(act, w, x1, tgt, g, mod, ga_row, name):
    def body(a_ref, w_ref, x1_ref, t_ref, g_ref, mod_ref, loss_ref, dx2_ref, dy2_ref, dg_ref, dga_ref):
        @pl.when(pl.program_id(0) == 0)
        def _():
            loss_ref[...] = jnp.zeros_like(loss_ref)
            dg_ref[...] = jnp.zeros_like(dg_ref)
            dga_ref[...] = jnp.zeros_like(dga_ref)

        ga = mod_ref[ga_row:ga_row + 1, :]
        for rs in _halves():
            y2 = jnp.dot(a_ref[rs, :], w_ref[...], preferred_element_type=F32)
            x2 = x1_ref[rs, :] + ga * y2
            r = lax.rsqrt(_mean(x2 * x2) + EPS)
            xn = x2 * r
            err = xn * g_ref[...] - t_ref[rs, :]
            loss_ref[...] += jnp.broadcast_to(0.5 * jnp.sum(_mean(err * err)), (8, 128))
            dy = err * (1.0 / D_MODEL)
            dg_ref[...] += _rsum(dy * xn)
            dxn = dy * g_ref[...]
            dx2 = r * (dxn - xn * _mean(dxn * xn))
            dx2_ref[rs, :] = dx2
            dy2_ref[rs, :] = (dx2 * ga).astype(BF16)
            dga_ref[...] += _rsum(dx2 * y2)

    vec = jax.ShapeDtypeStruct((1, D_MODEL), F32)
    rows = _pair_spec
    return pl.pallas_call(
        body,
        out_shape=(jax.ShapeDtypeStruct((8, 128), F32), jax.ShapeDtypeStruct((SEQ, D_MODEL), F32),
                   jax.ShapeDtypeStruct((SEQ, D_MODEL), BF16), vec, vec),
        grid=(SEQ // PAIR,),
        in_specs=[rows(act.shape[1]), pl.BlockSpec(w.shape, lambda i: (0, 0)), rows(D_MODEL), rows(D_MODEL),
                  _vec_spec(D_MODEL), _vec_spec(D_MODEL, 8)],
        out_specs=(pl.BlockSpec((8, 128), lambda i: (0, 0)), rows(D_MODEL), rows(D_MODEL),
                   _vec_spec(D_MODEL), _vec_spec(D_MODEL)),
        name=name, compiler_params=_cp("arbitrary"))(act, w, x1, tgt, g, mod)


def matmul_rms_mod_bwd(a, b, x, dres, g, mod, sc_row, y, ga_row, name, b_rows=None, after=None):
    gated = y is not None
    pieces = isinstance(a, tuple)
    tm = PAIR if pieces else ROWS
    blocks = [slice(h * ROWS, (h + 1) * ROWS) for h in range(tm // ROWS)]
    rows = lambda width: pl.BlockSpec((tm, width), lambda i: (i, 0))
    if pieces:
        a1, a3 = a
        a_args = [a1, a3]
        a_specs = [rows(a1.shape[1]), pl.BlockSpec((3, tm, a3.shape[2]), lambda i: (0, i, 0))]
        b_arg, b_spec = b, pl.BlockSpec((N_DEV, b_rows, D_MODEL), lambda i: (0, 0, 0))
    else:
        k2 = a.shape[2]
        a_args = [a]
        a_specs = [pl.BlockSpec((2, tm, k2), lambda i: (0, i, 0))]
        b_arg, b_spec = b.reshape(2, k2, D_MODEL), pl.BlockSpec((2, k2, D_MODEL), lambda i: (0, 0, 0))
    n_a = len(a_args)

    def body(*refs):
        a_refs, (b_ref, x_ref, dres_ref, g_ref, mod_ref) = refs[:n_a], refs[n_a:n_a + 5]
        if gated:
            y_ref, dx_ref, dsh_ref, dsc_ref, dg_ref, dy_ref, dga_ref = refs[n_a + 5:]
        else:
            dx_ref, dsh_ref, dsc_ref, dg_ref = refs[n_a + 5:]

        @pl.when(pl.program_id(0) == 0)
        def _():
            for ref in (dsh_ref, dsc_ref, dg_ref) + ((dga_ref,) if gated else ()):
                ref[...] = jnp.zeros_like(ref)

        gg = g_ref[...]
        for rs in blocks:
            if pieces:
                wv = b_ref[...].reshape(N_DEV * b_rows, D_MODEL)
                k1, k3 = a_refs[0].shape[1], a_refs[1].shape[2]
                dh = jnp.dot(a_refs[0][rs, :], wv[0:k1], preferred_element_type=F32)
                for t in range(3):
                    dh = dh + jnp.dot(a_refs[1][t, rs, :], wv[k1 + t * k3:k1 + (t + 1) * k3], preferred_element_type=F32)
            else:
                dh = (jnp.dot(a_refs[0][0, rs, :], b_ref[0], preferred_element_type=F32)
                      + jnp.dot(a_refs[0][1, rs, :], b_ref[1], preferred_element_type=F32))
            xx = x_ref[rs, :]
            r = lax.rsqrt(_mean(xx * xx) + EPS)
            xn = xx * r
            dsh_ref[...] += _rsum(dh)
            dsc_ref[...] += _rsum(dh * (xn * gg))
            dt = dh * (1.0 + mod_ref[sc_row:sc_row + 1, :])
            dg_ref[...] += _rsum(dt * xn)
            dxn = dt * gg
            dx = dres_ref[rs, :] + r * (dxn - xn * _mean(dxn * xn))
            dx_ref[rs, :] = dx
            if gated:
                dga_ref[...] += _rsum(dx * y_ref[rs, :])
                dy_ref[rs, :] = (dx * mod_ref[ga_row:ga_row + 1, :]).astype(BF16)

    vec = jax.ShapeDtypeStruct((1, D_MODEL), F32)
    in_specs = a_specs + [b_spec, rows(D_MODEL), rows(D_MODEL), _vec_spec(D_MODEL), _vec_spec(D_MODEL, 8)]
    out_shape = [jax.ShapeDtypeStruct((SEQ, D_MODEL), F32), vec, vec, vec]
    out_specs = [rows(D_MODEL), _vec_spec(D_MODEL), _vec_spec(D_MODEL), _vec_spec(D_MODEL)]
    args = a_args + [b_arg, x, dres, g, mod]
    if gated:
        in_specs.append(rows(D_MODEL))
        out_shape += [jax.ShapeDtypeStruct((SEQ, D_MODEL), BF16), vec]
        out_specs += [rows(D_MODEL), _vec_spec(D_MODEL)]
        args.append(y)
    body, more_specs, more = _with_after(body, len(args), after)
    return pl.pallas_call(
        body, out_shape=tuple(out_shape), grid=(SEQ // tm,), in_specs=in_specs + more_specs, out_specs=tuple(out_specs),
        name=name, compiler_params=_cp("arbitrary"))(*args, *more)


def _prev_halo(halo, width, col):
    per = ROWS // halo
    return pl.BlockSpec((halo, width), lambda i: (jnp.maximum(i * per - 1, 0), col))


def _next_halo(halo, width, col):
    per = ROWS // halo
    last = SEQ // halo - 1
    return pl.BlockSpec((halo, width), lambda i: (jnp.minimum((i + 1) * per, last), col))


CONV_PAD = ROWS + CONV_HALO


def _shift_copies(sh):
    for b in range(1, 8):
        sh[b, 0:CONV_PAD - 8, :] = sh[0, pl.ds(b, CONV_PAD - 8), :]


def _tap(sh, rs_start, offset):
    return sh[offset % 8, pl.ds(pl.multiple_of(rs_start + (offset // 8) * 8, 8), SUB), :]


def _conv_glu(av_ref, ag_ref, avh_ref, agh_ref, sh):
    i = pl.program_id(0)
    hv = avh_ref[...] * _sig(agh_ref[...])
    sh[0, 0:CONV_HALO, :] = jnp.where(i > 0, hv, 0.0)

    def glu(rs):
        sh[0, pl.ds(pl.multiple_of(rs.start + CONV_HALO, SUB), SUB), :] = av_ref[rs, :] * _sig(ag_ref[rs, :])

    _for_chunks(glu)
    _shift_copies(sh)


def _conv_norm(u1, lg_ref, lb_ref):
    mu = _mean(u1)
    cen = u1 - mu
    rs = lax.rsqrt(_mean(cen * cen) + EPS)
    z = cen * rs
    ln = z * lg_ref[...] + lb_ref[...]
    s = _sig(ln)
    return z, rs, ln, s, ln * s


def conv_module_fwd(proj, wc, bc, lg, lb, gco, name):
    def body(av_ref, ag_ref, avh_ref, agh_ref, wc_ref, bc_ref, lg_ref, lb_ref, gco_ref, out_ref, u1_ref, sh):
        _conv_glu(av_ref, ag_ref, avh_ref, agh_ref, sh)

        def conv(rs):
            u1 = jnp.broadcast_to(bc_ref[...], (SUB, D_CONV))
            for j in range(CONV_K):
                u1 = u1 + wc_ref[j:j + 1, :] * _tap(sh, rs.start, CONV_HALO - (CONV_K - 1) + j)
            u1_ref[rs, :] = u1

        _for_chunks(conv)
        _, _, _, _, u2 = _conv_norm(u1_ref[...], lg_ref, lb_ref)
        rc = lax.rsqrt(_mean(u2 * u2) + EPS)
        out_ref[...] = (u2 * rc * gco_ref[...]).astype(BF16)

    v = _vec_spec(D_CONV)
    return pl.pallas_call(
        body, out_shape=(jax.ShapeDtypeStruct((SEQ, D_CONV), BF16), jax.ShapeDtypeStruct((SEQ, D_CONV), F32)),
        grid=(SEQ // ROWS,),
        in_specs=[_row_spec(D_CONV, 0), _row_spec(D_CONV, 1), _prev_halo(CONV_HALO, D_CONV, 0),
                  _prev_halo(CONV_HALO, D_CONV, 1), _vec_spec(D_CONV, CONV_K), v, v, v, v],
        out_specs=(_row_spec(D_CONV), _row_spec(D_CONV)), scratch_shapes=[pltpu.VMEM((8, CONV_PAD, D_CONV), F32)],
        name=name, compiler_params=_cp("parallel"))(proj, proj, proj, proj, wc, bc, lg, lb, gco)


def conv_module_bwd_a(proj, u1, dmixed, lg, lb, gco, name):
    def body(av_ref, ag_ref, avh_ref, agh_ref, u1_ref, dm_ref, lg_ref, lb_ref, gco_ref,
             du1_ref, dgco_ref, dlg_ref, dlb_ref, dbc_ref, dwc_ref, sh, acc):
        first = pl.program_id(0) == 0
        _conv_glu(av_ref, ag_ref, avh_ref, agh_ref, sh)
        z, rs, ln, s, u2 = _conv_norm(u1_ref[...], lg_ref, lb_ref)
        rc = lax.rsqrt(_mean(u2 * u2) + EPS)
        xn = u2 * rc
        dm = dm_ref[...]
        _acc(dgco_ref, _rsum(dm * xn), first)
        dyn = dm * gco_ref[...]
        du2 = rc * (dyn - xn * _mean(dyn * xn))
        dln = du2 * (s * (1.0 + ln * (1.0 - s)))
        _acc(dlg_ref, _rsum(dln * z), first)
        _acc(dlb_ref, _rsum(dln), first)
        dz = dln * lg_ref[...]
        du1 = rs * (dz - _mean(dz) - z * _mean(dz * z))
        du1_ref[...] = du1
        _acc(dbc_ref, _rsum(du1), first)
        acc[...] = jnp.zeros_like(acc)

        def taps(rs):
            d = du1_ref[rs, :]
            for j in range(CONV_K):
                acc[j] += d * _tap(sh, rs.start, CONV_HALO - (CONV_K - 1) + j)

        _for_chunks(taps)

        @pl.when(first)
        def _():
            dwc_ref[...] = jnp.zeros_like(dwc_ref)

        for j in range(CONV_K):
            dwc_ref[j:j + 1, :] += _rsum(acc[j])

    v = _vec_spec(D_CONV)
    vec = jax.ShapeDtypeStruct((1, D_CONV), F32)
    return pl.pallas_call(
        body,
        out_shape=(jax.ShapeDtypeStruct((SEQ, D_CONV), F32), vec, vec, vec, vec, jax.ShapeDtypeStruct((CONV_K, D_CONV), F32)),
        grid=(SEQ // ROWS,),
        in_specs=[_row_spec(D_CONV, 0), _row_spec(D_CONV, 1), _prev_halo(CONV_HALO, D_CONV, 0),
                  _prev_halo(CONV_HALO, D_CONV, 1), _row_spec(D_CONV, 0), _row_spec(D_CONV, 0), v, v, v],
        out_specs=(_row_spec(D_CONV), v, v, v, v, _vec_spec(D_CONV, CONV_K)),
        scratch_shapes=[pltpu.VMEM((8, CONV_PAD, D_CONV), F32), pltpu.VMEM((CONV_K, SUB, D_CONV), F32)],
        name=name, compiler_params=_cp("arbitrary"))(proj, proj, proj, proj, u1, dmixed, lg, lb, gco)


def conv_module_bwd_b(proj, du1, wc, name):
    def body(av_ref, ag_ref, du1_ref, du1n_ref, wc_ref, out_ref, sh):
        i = pl.program_id(0)
        sh[0, 0:ROWS, :] = du1_ref[...]
        sh[0, ROWS:, :] = jnp.where(i < SEQ // ROWS - 1, du1n_ref[...], 0.0)
        _shift_copies(sh)

        def chunk(rs):
            du0 = jnp.zeros((SUB, D_CONV), F32)
            for j in range(CONV_K):
                du0 = du0 + wc_ref[j:j + 1, :] * _tap(sh, rs.start, CONV_K - 1 - j)
            sg = _sig(ag_ref[rs, :])
            out_ref[rs, 0:D_CONV] = (du0 * sg).astype(BF16)
            out_ref[rs, D_CONV:] = (du0 * av_ref[rs, :] * sg * (1.0 - sg)).astype(BF16)

        _for_chunks(chunk)

    return pl.pallas_call(
        body, out_shape=jax.ShapeDtypeStruct((SEQ, 2 * D_CONV), BF16), grid=(SEQ // ROWS,),
        in_specs=[_row_spec(D_CONV, 0), _row_spec(D_CONV, 1), _row_spec(D_CONV, 0), _next_halo(CONV_HALO, D_CONV, 0),
                  _vec_spec(D_CONV, CONV_K)],
        out_specs=_row_spec(2 * D_CONV), scratch_shapes=[pltpu.VMEM((8, CONV_PAD, D_CONV), F32)],
        name=name, compiler_params=_cp("parallel"))(proj, proj, du1, du1, wc)


def _rows(start, size, r):
    return pl.ds(start, size) if r == 1 else pl.ds(start, size, stride=r)


def _unit_rows(r, rho, n, nb):
    win = 2 * ATTN_BLOCK if nb > 1 else ATTN_BLOCK
    if isinstance(n, int):
        kb = max(n - 1, 0)
        q_rows = _rows(rho + r * ATTN_BLOCK * n, ATTN_BLOCK, r)
        k_rows = _rows(rho + r * ATTN_BLOCK * kb, win, r)
    else:
        kb = jnp.maximum(n - 1, 0)
        q_rows = pl.ds(pl.multiple_of(n * ATTN_BLOCK, ATTN_BLOCK), ATTN_BLOCK)
        k_rows = pl.ds(pl.multiple_of(kb * ATTN_BLOCK, ATTN_BLOCK), win)
    dist = (n - kb) * ATTN_BLOCK + lax.broadcasted_iota(jnp.int32, (ATTN_BLOCK, win), 0) \
        - lax.broadcasted_iota(jnp.int32, (ATTN_BLOCK, win), 1)
    return q_rows, k_rows, (dist >= 0) & (dist <= ATTN_BLOCK)


def _per_head(x):
    lane = lax.broadcasted_iota(jnp.int32, x.shape, 1)
    zero = jnp.zeros_like(x)
    return [jnp.where(lane < HEAD_DIM, x, zero), jnp.where(lane >= HEAD_DIM, x, zero)]


SCALE = HEAD_DIM ** -0.5


def _masked_scores(q2, k2, valid):
    return [jnp.where(valid, lax.dot_general(qh, k2, NT, preferred_element_type=F32), NEG) for qh in _per_head(q2)]


def _attn_units(r, nb, unit):
    if r == 1:
        def four(i, carry):
            for k in range(4):
                unit(0, 4 * i + k)
            return carry
        lax.fori_loop(0, nb // 4, four, 0)
    else:
        for rho in range(r):
            for n in range(nb):
                unit(rho, n)


N_UNITS = 16


def attn_fwd_all(proj, name):
    def body(q_ref, k_ref, v_ref, att_ref, lse_ref, s_scr, p_scr, lse_scr, den_scr):
        for idx, (sub_len, r) in enumerate(PATTERNS):
            nb = sub_len // ATTN_BLOCK
            win = 2 * ATTN_BLOCK if nb > 1 else ATTN_BLOCK

            def scores(rho, n, r=r, nb=nb, win=win):
                u = rho * nb + n
                q_rows, k_rows, valid = _unit_rows(r, rho, n, nb)
                ss = _masked_scores((q_ref[q_rows, :] * SCALE).astype(BF16), k_ref[k_rows, :].astype(BF16), valid)
                for h in range(2):
                    s_scr[2 * u + h, :, 0:win] = ss[h]

            _attn_units(r, nb, scores)

            def softmax(u, carry, win=win):
                lses, dens = [], []
                for h in range(2):
                    sc = s_scr[2 * u + h, :, 0:win]
                    m = jnp.max(sc, axis=1, keepdims=True)
                    p = jnp.exp(sc - m)
                    den = jnp.sum(p, axis=1, keepdims=True)
                    p_scr[2 * u + h, :, 0:win] = p.astype(BF16)
                    lses.append(jnp.broadcast_to(m + jnp.log(den), (ATTN_BLOCK, HEAD_DIM)))
                    dens.append(jnp.broadcast_to(den, (ATTN_BLOCK, HEAD_DIM)))
                lse_scr[u] = jnp.concatenate(lses, axis=1)
                den_scr[u] = jnp.concatenate(dens, axis=1)
                return carry

            lax.fori_loop(0, N_UNITS, softmax, 0, unroll=2)

            def outputs(rho, n, r=r, nb=nb, win=win, idx=idx):
                u = rho * nb + n
                q_rows, k_rows, _ = _unit_rows(r, rho, n, nb)
                vs = _per_head(v_ref[k_rows, :].astype(BF16))
                o = (jnp.dot(p_scr[2 * u, :, 0:win], vs[0], preferred_element_type=F32)
                     + jnp.dot(p_scr[2 * u + 1, :, 0:win], vs[1], preferred_element_type=F32)) / den_scr[u]
                lse = lse_scr[u]
                if idx > 0:
                    old = lse_ref[q_rows, :]
                    top = jnp.maximum(old, lse)
                    new = top + jnp.log(jnp.exp(old - top) + jnp.exp(lse - top))
                    o = att_ref[q_rows, :] * jnp.exp(old - new) + o * jnp.exp(lse - new)
                    lse = new
                att_ref[q_rows, :] = o
                lse_ref[q_rows, :] = lse

            _attn_units(r, nb, outputs)

    blk = lambda first: pl.BlockSpec((SEQ, 128), lambda g: (0, first + g))
    shp = jax.ShapeDtypeStruct((SEQ, D_ATTN), F32)
    big = (2 * N_UNITS, ATTN_BLOCK, 2 * ATTN_BLOCK)
    small = pltpu.VMEM((N_UNITS, ATTN_BLOCK, 128), F32)
    return pl.pallas_call(
        body, out_shape=(shp, shp), grid=(4,), in_specs=[blk(8), blk(12), blk(16)], out_specs=(blk(0), blk(0)),
        scratch_shapes=[pltpu.VMEM(big, F32), pltpu.VMEM(big, BF16), small, small],
        name=name, compiler_params=_cp("parallel"))(proj, proj, proj)


def attn_bwd_all(proj, do, lse, dd, name):
    def body(q_ref, k_ref, v_ref, do_ref, l_ref, dd_ref, out_ref, dq_s, dk_s, dv_s,
             s_scr, dp_scr, ds_scr, st_scr, dpt_scr, pt_scr, dst_scr, qb_scr, kb_scr, dob_scr):
        dq_s[...] = jnp.zeros_like(dq_s)
        dk_s[...] = jnp.zeros_like(dk_s)
        dv_s[...] = jnp.zeros_like(dv_s)
        for sub_len, r in PATTERNS:
            nb = sub_len // ATTN_BLOCK
            win = 2 * ATTN_BLOCK if nb > 1 else ATTN_BLOCK

            def scores(rho, n, r=r, nb=nb, win=win):
                u = rho * nb + n
                q_rows, k_rows, valid = _unit_rows(r, rho, n, nb)
                kb = max(n - 1, 0) if isinstance(n, int) else jnp.maximum(n - 1, 0)
                dist_t = (n - kb) * ATTN_BLOCK + lax.broadcasted_iota(jnp.int32, (win, ATTN_BLOCK), 1) \
                    - lax.broadcasted_iota(jnp.int32, (win, ATTN_BLOCK), 0)
                valid_t = (dist_t >= 0) & (dist_t <= ATTN_BLOCK)
                q2 = (q_ref[q_rows, :] * SCALE).astype(BF16)
                kf = k_ref[k_rows, :]
                k2 = kf.astype(BF16)
                do2 = do_ref[q_rows, :].astype(BF16)
                qb_scr[u] = q2
                kb_scr[u, 0:win, :] = (kf * SCALE).astype(BF16)
                dob_scr[u] = do2
                l2 = l_ref[q_rows, :]
                d2 = dd_ref[q_rows, :]
                l2t = l2.T
                d2t = d2.T
                v2 = v_ref[k_rows, :].astype(BF16)
                qs, dos = _per_head(q2), _per_head(do2)
                for h in range(2):
                    c0 = h * HEAD_DIM
                    sc = lax.dot_general(qs[h], k2, NT, preferred_element_type=F32)
                    s_scr[2 * u + h, :, 0:win] = jnp.where(valid, sc, NEG) - l2[:, c0:c0 + 1]
                    dp_scr[2 * u + h, :, 0:win] = lax.dot_general(dos[h], v2, NT, preferred_element_type=F32) \
                        - d2[:, c0:c0 + 1]
                    sct = lax.dot_general(k2, qs[h], NT, preferred_element_type=F32)
                    st_scr[2 * u + h, 0:win, :] = jnp.where(valid_t, sct, NEG) - l2t[c0:c0 + 1, :]
                    dpt_scr[2 * u + h, 0:win, :] = lax.dot_general(v2, dos[h], NT, preferred_element_type=F32) \
                        - d2t[c0:c0 + 1, :]

            _attn_units(r, nb, scores)

            def pointwise(hu, carry, win=win):
                ds_scr[hu, :, 0:win] = (jnp.exp(s_scr[hu, :, 0:win]) * dp_scr[hu, :, 0:win]).astype(BF16)
                pt = jnp.exp(st_scr[hu, 0:win, :])
                pt_scr[hu, 0:win, :] = pt.astype(BF16)
                dst_scr[hu, 0:win, :] = (pt * dpt_scr[hu, 0:win, :]).astype(BF16)
                return carry

            lax.fori_loop(0, 2 * N_UNITS, pointwise, 0, unroll=4)

            def grads(rho, n, r=r, nb=nb, win=win):
                u = rho * nb + n
                q_rows, k_rows, _ = _unit_rows(r, rho, n, nb)
                qs, ks, dos = _per_head(qb_scr[u]), _per_head(kb_scr[u, 0:win, :]), _per_head(dob_scr[u])

                def both(scr, rows, rhs):
                    return (jnp.dot(scr[(2 * u,) + rows], rhs[0], preferred_element_type=F32)
                            + jnp.dot(scr[(2 * u + 1,) + rows], rhs[1], preferred_element_type=F32))

                dq_s[q_rows, :] += both(ds_scr, (slice(None), slice(0, win)), ks)
                dk_s[k_rows, :] += both(dst_scr, (slice(0, win), slice(None)), qs)
                dv_s[k_rows, :] += both(pt_scr, (slice(0, win), slice(None)), dos)

            _attn_units(r, nb, grads)
        out_ref[0] = dq_s[...].astype(BF16)
        out_ref[1] = dk_s[...].astype(BF16)
        out_ref[2] = dv_s[...].astype(BF16)

    blk = lambda first: pl.BlockSpec((SEQ, 128), lambda g: (0, first + g))
    acc = pltpu.VMEM((SEQ, 128), F32)
    big = (2 * N_UNITS, ATTN_BLOCK, 2 * ATTN_BLOCK)
    big_t = (2 * N_UNITS, 2 * ATTN_BLOCK, ATTN_BLOCK)
    return pl.pallas_call(
        body, out_shape=jax.ShapeDtypeStruct((3, SEQ, D_ATTN), BF16), grid=(4,),
        in_specs=[blk(8), blk(12), blk(16), blk(0), blk(0), blk(0)],
        out_specs=pl.BlockSpec((3, SEQ, 128), lambda g: (0, 0, g)),
        scratch_shapes=[acc, acc, acc, pltpu.VMEM(big, F32), pltpu.VMEM(big, F32), pltpu.VMEM(big, BF16),
                        pltpu.VMEM(big_t, F32), pltpu.VMEM(big_t, F32), pltpu.VMEM(big_t, BF16), pltpu.VMEM(big_t, BF16),
                        pltpu.VMEM((N_UNITS, ATTN_BLOCK, 128), BF16),
                        pltpu.VMEM((N_UNITS, 2 * ATTN_BLOCK, 128), BF16), pltpu.VMEM((N_UNITS, ATTN_BLOCK, 128), BF16)],
        name=name, compiler_params=_cp("parallel"))(proj, proj, proj, do, lse, dd)


N_FT = D_FF // FFN_TN


def _ffn_specs():
    per = ROWS // FFN_HALO
    cur_g = pl.BlockSpec((ROWS, FFN_TN), lambda j, i: (i, j))
    cur_v = pl.BlockSpec((ROWS, FFN_TN), lambda j, i: (i, j + N_FT))
    halo_g = pl.BlockSpec((FFN_HALO, FFN_TN), lambda j, i: (jnp.maximum(i * per - 1, 0), j))
    halo_v = pl.BlockSpec((FFN_HALO, FFN_TN), lambda j, i: (jnp.maximum(i * per - 1, 0), j + N_FT))
    w_g = pl.BlockSpec((FFN_K, FFN_TN), lambda j, i: (0, j))
    w_v = pl.BlockSpec((FFN_K, FFN_TN), lambda j, i: (0, j + N_FT))
    b_g = pl.BlockSpec((1, FFN_TN), lambda j, i: (0, j))
    b_v = pl.BlockSpec((1, FFN_TN), lambda j, i: (0, j + N_FT))
    return [cur_g, cur_v, halo_g, halo_v, w_g, w_v, b_g, b_v]


def matmul(a, b, kind, out_dtype, tm, tn, name, b_rows=None):
    stacked = b_rows is not None
    b_shape = (N_DEV * b_rows, b.shape[2]) if stacked else b.shape
    if kind == "nn":
        (m, k), n = a.shape, b_shape[1]
        a_spec = pl.BlockSpec((tm, k), lambda j, i: (i, 0))
        b_spec = pl.BlockSpec((k, tn), lambda j, i: (0, j))
        dims = (((1,), (0,)), ((), ()))
    elif kind == "nt":
        (m, k), n = a.shape, b_shape[0]
        a_spec = pl.BlockSpec((tm, k), lambda j, i: (i, 0))
        b_spec = pl.BlockSpec((tn, k), lambda j, i: (j, 0))
        dims = (((1,), (1,)), ((), ()))
    else:
        (k, m), n = a.shape, b_shape[1]
        a_spec = pl.BlockSpec((k, tm), lambda j, i: (0, i))
        b_spec = pl.BlockSpec((k, tn), lambda j, i: (0, j))
        dims = (((0,), (0,)), ((), ()))
    assert m % tm == 0 and n % tn == 0, (name, m, n, tm, tn)
    if stacked:
        assert b_spec.block_shape[0] == b_shape[0] and kind in ("nn", "nt")
        width = b_spec.block_shape[1]
        b_spec = pl.BlockSpec((N_DEV, b_rows, width), (lambda j, i: (0, 0, j)) if kind == "nn" else (lambda j, i: (0, 0, 0)))

    def body(a_ref, b_ref, o_ref):
        bb = b_ref[...].reshape(b_shape[0], -1) if stacked else b_ref[...]
        o_ref[...] = lax.dot_general(a_ref[...], bb, dims, preferred_element_type=F32).astype(o_ref.dtype)

    return pl.pallas_call(
        body, out_shape=jax.ShapeDtypeStruct((m, n), out_dtype), grid=(n // tn, m // tm),
        in_specs=[a_spec, b_spec], out_specs=pl.BlockSpec((tm, tn), lambda j, i: (i, j)),
        name=name, compiler_params=_cp("parallel", "parallel"))(a, b)


def matmul_tn_pieces(a1, a3, b, name):
    k, n = b.shape
    tm = a3.shape[2]
    n1 = a1.shape[1] // tm

    def body(a1_ref, a3_ref, b_ref, o_ref):
        i = pl.program_id(0)
        tn_dims = (((0,), (0,)), ((), ()))

        @pl.when(i < n1)
        def _():
            o_ref[...] = lax.dot_general(a1_ref[...], b_ref[...], tn_dims, preferred_element_type=F32).astype(BF16)

        @pl.when(i >= n1)
        def _():
            o_ref[...] = lax.dot_general(a3_ref[0], b_ref[...], tn_dims, preferred_element_type=F32).astype(BF16)

    return pl.pallas_call(
        body, out_shape=jax.ShapeDtypeStruct((a1.shape[1] + 3 * tm, n), BF16), grid=(n1 + 3,),
        in_specs=[pl.BlockSpec((k, tm), lambda i: (0, jnp.minimum(i, n1 - 1))),
                  pl.BlockSpec((1, k, tm), lambda i: (jnp.maximum(i - n1, 0), 0, 0)),
                  pl.BlockSpec((k, n), lambda i: (0, 0))],
        out_specs=pl.BlockSpec((tm, n), lambda i: (i, 0)), name=name, compiler_params=_cp("parallel"))(a1, a3, b)


def matmul_tn_halves(a, b, tm, name):
    _, k, m = a.shape
    n = b.shape[1]

    def body(a_ref, b_ref, o_ref):
        o_ref[0] = lax.dot_general(a_ref[0], b_ref[...], (((0,), (0,)), ((), ())), preferred_element_type=F32).astype(BF16)

    return pl.pallas_call(
        body, out_shape=jax.ShapeDtypeStruct((2, m, n), BF16), grid=(2, m // tm),
        in_specs=[pl.BlockSpec((1, k, tm), lambda h, i: (h, 0, i)), pl.BlockSpec((k, n), lambda h, i: (0, 0))],
        out_specs=pl.BlockSpec((1, tm, n), lambda h, i: (h, i, 0)), name=name,
        compiler_params=_cp("parallel", "parallel"))(a, b).reshape(2 * m, n)


def _row_spec(width, col=0):
    return pl.BlockSpec((ROWS, width), lambda i: (i, col))


def _vec_spec(width, rows=1):
    return pl.BlockSpec((rows, width), lambda i: (0, 0))


FFN_RB = 64


def _ffn_lane_blocks(fn):
    for c in range(FFN_TN // 128):
        fn(slice(c * 128, (c + 1) * 128))


def _ffn_taps(cur_ref, halo_ref, head, ls, r0):
    if r0 == 0:
        head[0:FFN_HALO, :] = jnp.where(pl.program_id(1) > 0, halo_ref[:, ls], 0.0)
        head[FFN_HALO:, :] = cur_ref[0:FFN_RB, ls]
        return tuple(head[pl.ds(FFN_HALO - k, FFN_RB), :] for k in (2, 1, 0))
    return tuple(cur_ref[pl.ds(r0 - k, FFN_RB), ls] for k in (2, 1, 0))


def _ffn_conv(taps, w, b):
    return b + w[0] * taps[0] + w[1] * taps[1] + w[2] * taps[2]


def ffn_act_fwd(up0, wf, bf, name):
    def body(g_ref, v_ref, gh_ref, vh_ref, wg_ref, wv_ref, bg_ref, bv_ref, act_ref, head_g, head_v):
        def lanes(ls):
            wg = [wg_ref[t:t + 1, ls] for t in range(FFN_K)]
            wv = [wv_ref[t:t + 1, ls] for t in range(FFN_K)]
            for r0 in range(0, ROWS, FFN_RB):
                gate = _ffn_conv(_ffn_taps(g_ref, gh_ref, head_g, ls, r0), wg, bg_ref[:, ls])
                val = _ffn_conv(_ffn_taps(v_ref, vh_ref, head_v, ls, r0), wv, bv_ref[:, ls])
                act_ref[r0:r0 + FFN_RB, ls] = (gate * _sig(gate) * val).astype(BF16)

        _ffn_lane_blocks(lanes)

    head = pltpu.VMEM((FFN_HALO + FFN_RB, 128), F32)
    return pl.pallas_call(
        body, out_shape=jax.ShapeDtypeStruct((SEQ, D_FF), BF16), grid=(N_FT, SEQ // ROWS),
        in_specs=_ffn_specs(), out_specs=pl.BlockSpec((ROWS, FFN_TN), lambda j, i: (i, j)),
        scratch_shapes=[head, head], name=name, compiler_params=_cp("parallel", "parallel"))(up0, up0, up0, up0, wf, wf, bf, bf)


def ffn_bwd(up0, dact, wf, bf, name, after=None):
    per = ROWS // FFN_HALO
    last = SEQ // FFN_HALO - 1

    def body(g_ref, v_ref, gh_ref, vh_ref, wg_ref, wv_ref, bg_ref, bv_ref, da_ref, gn_ref, vn_ref, dan_ref,
             out_ref, dbg_ref, dbv_ref, dwg_ref, dwv_ref, pad, head_g, head_v, dgp, dvp, acc):
        i = pl.program_id(1)
        first = i == 0
        acc[...] = jnp.zeros_like(acc)

        def grads(gate, val, da):
            s = _sig(gate)
            return da * val * (s * (1.0 + gate * (1.0 - s))), da * (gate * s)

        fold = lambda q: jnp.sum(q.reshape(FFN_RB // 8, 8, 128), axis=0)

        def lanes(ls):
            wg = [wg_ref[t:t + 1, ls] for t in range(FFN_K)]
            wv = [wv_ref[t:t + 1, ls] for t in range(FFN_K)]
            sums = [jnp.zeros((8, 128), F32)] * (2 + 2 * FFN_K)
            for r0 in range(0, ROWS, FFN_RB):
                gs = _ffn_taps(g_ref, gh_ref, head_g, ls, r0)
                vs = _ffn_taps(v_ref, vh_ref, head_v, ls, r0)
                dgate, dval = grads(_ffn_conv(gs, wg, bg_ref[:, ls]), _ffn_conv(vs, wv, bv_ref[:, ls]),
                                    da_ref[r0:r0 + FFN_RB, ls])
                dgp[r0:r0 + FFN_RB, ls] = dgate
                dvp[r0:r0 + FFN_RB, ls] = dval
                new = [dgate, dval] + [dgate * gs[t] for t in range(FFN_K)] + [dval * vs[t] for t in range(FFN_K)]
                sums = [a + fold(q) for a, q in zip(sums, new)]
            for k in range(2 + 2 * FFN_K):
                acc[k, 0:8, ls] = sums[k]

        _ffn_lane_blocks(lanes)
        _acc(dbg_ref, _rsum(acc[0]), first)
        _acc(dbv_ref, _rsum(acc[1]), first)
        _acc(dwg_ref, jnp.concatenate([_rsum(acc[2 + t]) for t in range(FFN_K)], axis=0), first)
        _acc(dwv_ref, jnp.concatenate([_rsum(acc[5 + t]) for t in range(FFN_K)], axis=0), first)

        def conv_next(cur_ref, nxt_ref, w_ref, b_ref):
            pad[0:FFN_HALO, :] = cur_ref[ROWS - FFN_HALO:, :]
            pad[FFN_HALO:, :] = nxt_ref[...]
            return (b_ref[...] + w_ref[0:1, :] * pad[pl.ds(FFN_HALO - 2, FFN_HALO), :]
                    + w_ref[1:2, :] * pad[pl.ds(FFN_HALO - 1, FFN_HALO), :] + w_ref[2:3, :] * nxt_ref[...])

        gate_n = conv_next(g_ref, gn_ref, wg_ref, bg_ref)
        val_n = conv_next(v_ref, vn_ref, wv_ref, bv_ref)
        dgate_n, dval_n = grads(gate_n, val_n, dan_ref[...])
        inside = i < SEQ // ROWS - 1
        dgp[ROWS:, :] = jnp.where(inside, dgate_n, 0.0)
        dvp[ROWS:, :] = jnp.where(inside, dval_n, 0.0)

        def back(ls):
            for half, (dp, w_ref) in enumerate(((dgp, wg_ref), (dvp, wv_ref))):
                w = [w_ref[t:t + 1, ls] for t in range(FFN_K)]
                for r0 in range(0, ROWS, FFN_RB):
                    out_ref[half, r0:r0 + FFN_RB, ls] = (
                        w[2] * dp[r0:r0 + FFN_RB, ls] + w[1] * dp[pl.ds(r0 + 1, FFN_RB), ls]
                        + w[0] * dp[pl.ds(r0 + 2, FFN_RB), ls]).astype(BF16)

        _ffn_lane_blocks(back)

    body, more_specs, more = _with_after(body, 12, after)
    head = pltpu.VMEM((FFN_HALO + FFN_RB, 128), F32)
    ext = pltpu.VMEM((ROWS + FFN_HALO, FFN_TN), F32)
    vec = jax.ShapeDtypeStruct((1, D_FF), F32)
    taps = jax.ShapeDtypeStruct((FFN_K, D_FF), F32)
    cur = pl.BlockSpec((ROWS, FFN_TN), lambda j, i: (i, j))
    nxt = lambda off: pl.BlockSpec((FFN_HALO, FFN_TN), lambda j, i: (jnp.minimum((i + 1) * per, last), j + off))
    vs = pl.BlockSpec((1, FFN_TN), lambda j, i: (0, j))
    ts = pl.BlockSpec((FFN_K, FFN_TN), lambda j, i: (0, j))
    return pl.pallas_call(
        body, out_shape=(jax.ShapeDtypeStruct((2, SEQ, D_FF), BF16), vec, vec, taps, taps), grid=(N_FT, SEQ // ROWS),
        in_specs=_ffn_specs() + [cur, nxt(0), nxt(N_FT), nxt(0)] + more_specs,
        out_specs=(pl.BlockSpec((2, ROWS, FFN_TN), lambda j, i: (0, i, j)), vs, vs, ts, ts),
        scratch_shapes=[pltpu.VMEM((2 * FFN_HALO, FFN_TN), F32), head, head, ext, ext,
                        pltpu.VMEM((2 + 2 * FFN_K, SUB, FFN_TN), F32)],
        name=name, compiler_params=_cp("parallel", "arbitrary"))(up0, up0, up0, up0, wf, wf, bf, bf, dact, up0, up0, dact,
                                                                 *more)


def ada_fwd(c_all, w_ada, b_cols, name):
    def body(c_ref, w_ref, b_ref, o_ref):
        cc = c_ref[...]
        sc = (cc * _sig(cc)).astype(BF16)
        o_ref[...] = jnp.dot(sc, w_ref[...].astype(BF16), preferred_element_type=F32) + b_ref[...]

    return pl.pallas_call(body, out_shape=jax.ShapeDtypeStruct((N_DEV, w_ada.shape[1]), F32), name=name,
                          compiler_params=_cp())(c_all, w_ada, b_cols)


def _adam(w, g, m, v):
    m = ADAM_B1 * m + (1.0 - ADAM_B1) * g
    v = ADAM_B2 * v + (1.0 - ADAM_B2) * (g * g)
    m_hat = m / (1.0 - ADAM_B1 ** ADAM_STEP)
    v_hat = v / (1.0 - ADAM_B2 ** ADAM_STEP)
    delta = -ADAM_LR * (m_hat / (jnp.sqrt(v_hat) + ADAM_EPS) + ADAM_WD * w)
    return delta, m, v


def ada_bwd_adamw(c_all, dmod_cols, w, m, v, name):
    rows, cols = w.shape
    tr = 256

    def body(c_ref, dm_ref, w_ref, m_ref, v_ref, g_ref, d_ref, nm_ref, nv_ref):
        cc = c_ref[...]
        sc = (cc * _sig(cc)).T
        g = sc[:, 0:1] * dm_ref[0:1, :]
        for b in range(1, N_DEV):
            g = g + sc[:, b:b + 1] * dm_ref[b:b + 1, :]
        g_ref[...] = g
        d_ref[...], nm_ref[...], nv_ref[...] = _adam(w_ref[...], g, m_ref[...], v_ref[...])

    blk = pl.BlockSpec((tr, cols), lambda i: (i, 0))
    shp = jax.ShapeDtypeStruct((rows, cols), F32)
    return pl.pallas_call(
        body, out_shape=(shp, shp, shp, shp), grid=(rows // tr,),
        in_specs=[pl.BlockSpec((N_DEV, tr), lambda i: (0, i)), pl.BlockSpec((N_DEV, cols), lambda i: (0, 0)), blk, blk, blk],
        out_specs=(blk, blk, blk, blk), name=name, compiler_params=_cp("parallel"))(c_all, dmod_cols, w, m, v)


def sum_adamw(parts, mine, me, w, m, v, tr, name):
    n_parts, rows, cols = parts.shape

    def body(me_ref, p_ref, own_ref, w_ref, m_ref, v_ref, g_ref, d_ref, nm_ref, nv_ref):
        def chunk(rs):
            g = own_ref[0, rs, :].astype(F32)
            for k in range(1, n_parts):
                g = g + p_ref[k, rs, :].astype(F32)
            g_ref[rs, :] = g
            d_ref[rs, :], nm_ref[rs, :], nv_ref[rs, :] = _adam(w_ref[rs, :], g, m_ref[rs, :], v_ref[rs, :])

        _for_chunks(chunk, 2, tr)

    blk = pl.BlockSpec((tr, cols), lambda i, me_ref: (i, 0))
    shp = jax.ShapeDtypeStruct((rows, cols), F32)
    grid_spec = pltpu.PrefetchScalarGridSpec(
        num_scalar_prefetch=1, grid=(rows // tr,),
        in_specs=[pl.BlockSpec((n_parts, tr, cols), lambda i, me_ref: (0, i, 0)),
                  pl.BlockSpec((1, tr, cols), lambda i, me_ref: (me_ref[0], i, 0)), blk, blk, blk],
        out_specs=(blk, blk, blk, blk))
    return pl.pallas_call(body, out_shape=(shp, shp, shp, shp), grid_spec=grid_spec, name=name,
                          compiler_params=_cp("parallel"))(me, parts, mine, w, m, v)


MESH = pl.DeviceIdType.MESH


def all_gather(block, name, after=None):
    extra = () if after is None else (after,)

    def body(x_ref, *refs):
        out_ref, send_sems, recv_sems, local_sem = refs[len(extra):]
        x, y, c = lax.axis_index("x"), lax.axis_index("y"), lax.axis_index("c")
        me, sibling = (x, y, c), (x, y, 1 - c)
        chips = [(1 - x, y), (x, 1 - y), (1 - x, 1 - y)]

        def slot(px, py, pc):
            return out_ref.at[4 * px + 2 * py + pc]

        def copy(k, blk, to, src=None):
            return pltpu.make_async_remote_copy(
                src_ref=slot(*blk) if src is None else src, dst_ref=slot(*blk),
                send_sem=send_sems.at[k], recv_sem=recv_sems.at[k], device_id=to, device_id_type=MESH)

        mine = pltpu.make_async_copy(x_ref, slot(*me), local_sem)
        mine.start()
        first = [copy(0, me, sibling, src=x_ref)]
        first += [copy(1 + j, me, (*chip, c), src=x_ref) for j, chip in enumerate(chips)]
        for cp in first:
            cp.start()
        passed = [copy(4 + j, (*chip, c), sibling) for j, chip in enumerate(chips)]
        for j, chip in enumerate(chips):
            copy(1 + j, (*chip, c), me).wait_recv()
            passed[j].start()
        copy(0, sibling, me).wait_recv()
        for j, chip in enumerate(chips):
            copy(4 + j, (*chip, 1 - c), me).wait_recv()
        for cp in first + passed:
            cp.wait_send()
        mine.wait()

    return pl.pallas_call(
        body, out_shape=jax.ShapeDtypeStruct((N_DEV,) + block.shape, block.dtype), in_specs=[ANY] * (1 + len(extra)), out_specs=ANY,
        scratch_shapes=[pltpu.SemaphoreType.DMA((7,)), pltpu.SemaphoreType.DMA((7,)), pltpu.SemaphoreType.DMA],
        name=name)(block, *extra)


HBM = pl.BlockSpec(memory_space=pltpu.HBM)
SEM = pl.BlockSpec(memory_space=pltpu.SEMAPHORE)
EFFECT = pltpu.SideEffectType.DATAFLOW_SIDE_EFFECTING


def _peer_copies(src_ref, land_ref, send_sems, recv_sems, gather):
    x, y, c = lax.axis_index("x"), lax.axis_index("y"), lax.axis_index("c")
    me = 4 * x + 2 * y + c
    copies = []
    for k in range(1, N_DEV):
        px = 1 - x if k & 4 else x
        py = 1 - y if k & 2 else y
        pc = 1 - c if k & 1 else c
        copies.append(pltpu.make_async_remote_copy(
            src_ref=src_ref if gather else src_ref.at[4 * px + 2 * py + pc],
            dst_ref=land_ref.at[me] if gather else land_ref.at[k],
            send_sem=send_sems.at[k - 1], recv_sem=recv_sems.at[k - 1], device_id=(px, py, pc), device_id_type=MESH))
    return copies


def exchange_start(srcs, gather, name, after=None):
    n = len(srcs)
    land_shapes = [(N_DEV,) + src.shape if gather else src.shape for src in srcs]
    extra = () if after is None else (after,)

    def body(*refs):
        src_refs, land_refs = refs[0:n], refs[n:2 * n]
        outs = refs[2 * n + len(extra):]
        for k in range(n):
            for cp in _peer_copies(src_refs[k], land_refs[k], outs[4 * k], outs[4 * k + 1], gather):
                cp.start()
        token = outs[4 * n]
        token[...] = jnp.zeros_like(token)

    out_shape, out_specs, aliases = [], [], {}
    for k, src in enumerate(srcs):
        out_shape += [pltpu.SemaphoreType.DMA((N_DEV - 1,)), pltpu.SemaphoreType.DMA((N_DEV - 1,)),
                      pltpu.HBM(src.shape, src.dtype), pltpu.HBM(land_shapes[k], src.dtype)]
        out_specs += [SEM, SEM, HBM, HBM]
        aliases[k] = 4 * k + 2
        aliases[n + k] = 4 * k + 3
    out_shape.append(jax.ShapeDtypeStruct((8, 128), F32))
    out_specs.append(pl.BlockSpec(memory_space=pltpu.VMEM))
    res = pl.pallas_call(
        body, name=name, out_shape=tuple(out_shape), in_specs=(HBM,) * (2 * n) + (ANY,) * len(extra),
        out_specs=tuple(out_specs), input_output_aliases=aliases,
        compiler_params=pltpu.CompilerParams(has_side_effects=EFFECT),
    )(*[pltpu.with_memory_space_constraint(src, pltpu.HBM) for src in srcs],
      *[pltpu.with_memory_space_constraint(lax.empty(shp, src.dtype), pltpu.HBM) for shp, src in zip(land_shapes, srcs)],
      *extra)
    return [tuple(res[4 * k:4 * k + 4]) for k in range(n)], res[4 * n]


def _stage1_peer(i):
    x, y, c = lax.axis_index("x"), lax.axis_index("y"), lax.axis_index("c")
    if i == 0:
        return (x, y, 1 - c)
    return (1 - x if i & 1 else x, 1 - y if i & 2 else y, c)


def _slot_of(peer):
    return 4 * peer[0] + 2 * peer[1] + peer[2]


def _stage1_copy(i, src_ref, land_ref, send_sems, recv_sems):
    me = _slot_of((lax.axis_index("x"), lax.axis_index("y"), lax.axis_index("c")))
    return pltpu.make_async_remote_copy(src_ref=src_ref, dst_ref=land_ref.at[me], send_sem=send_sems.at[i],
                                        recv_sem=recv_sems.at[i], device_id=_stage1_peer(i), device_id_type=MESH)


def _stage2_copy(j, land_ref, send_sems, recv_sems):
    slot = _slot_of(_stage1_peer(j + 1))
    return pltpu.make_async_remote_copy(src_ref=land_ref.at[slot], dst_ref=land_ref.at[slot], send_sem=send_sems.at[j],
                                        recv_sem=recv_sems.at[j], device_id=_stage1_peer(0), device_id_type=MESH)


def gather2_start(srcs, name, after=None):
    n = len(srcs)
    extra = () if after is None else (after,)

    def body(*refs):
        src_refs, land_refs = refs[0:n], refs[n:2 * n]
        outs = refs[2 * n + len(extra):]
        for k in range(n):
            for i in range(4):
                _stage1_copy(i, src_refs[k], land_refs[k], outs[4 * k], outs[4 * k + 1]).start()
        outs[4 * n][...] = jnp.zeros((8, 128), F32)

    out_shape, out_specs, aliases = [], [], {}
    for k, src in enumerate(srcs):
        out_shape += [pltpu.SemaphoreType.DMA((4,)), pltpu.SemaphoreType.DMA((4,)),
                      pltpu.HBM(src.shape, src.dtype), pltpu.HBM((N_DEV,) + src.shape, src.dtype)]
        out_specs += [SEM, SEM, HBM, HBM]
        aliases[k] = 4 * k + 2
        aliases[n + k] = 4 * k + 3
    out_shape.append(jax.ShapeDtypeStruct((8, 128), F32))
    out_specs.append(pl.BlockSpec(memory_space=pltpu.VMEM))
    res = pl.pallas_call(
        body, name=name, out_shape=tuple(out_shape), in_specs=(HBM,) * (2 * n) + (ANY,) * len(extra),
        out_specs=tuple(out_specs), input_output_aliases=aliases,
        compiler_params=pltpu.CompilerParams(has_side_effects=EFFECT),
    )(*[pltpu.with_memory_space_constraint(src, pltpu.HBM) for src in srcs],
      *[pltpu.with_memory_space_constraint(lax.empty((N_DEV,) + src.shape, src.dtype), pltpu.HBM) for src in srcs], *extra)
    return [dict(send1=res[4 * k], recv1=res[4 * k + 1], src=res[4 * k + 2], land=res[4 * k + 3]) for k in range(n)], \
        res[4 * n]


def gather2_pass(handles, after, name):
    n = len(handles)

    def body(*refs):
        src_refs, land_refs, recv1 = refs[0:n], refs[n:2 * n], refs[2 * n:3 * n]
        outs = refs[3 * n + 1:]
        for k in range(n):
            for j in range(3):
                _stage1_copy(j + 1, src_refs[k], land_refs[k], recv1[k], recv1[k]).wait_recv()
                _stage2_copy(j, land_refs[k], outs[3 * k], outs[3 * k + 1]).start()

    out_shape, out_specs, aliases = [], [], {}
    for k, h in enumerate(handles):
        out_shape += [pltpu.SemaphoreType.DMA((3,)), pltpu.SemaphoreType.DMA((3,)), pltpu.HBM(h["land"].shape, h["land"].dtype)]
        out_specs += [SEM, SEM, HBM]
        aliases[n + k] = 3 * k + 2
    res = pl.pallas_call(
        body, name=name, out_shape=tuple(out_shape), in_specs=(HBM,) * (2 * n) + (SEM,) * n + (ANY,),
        out_specs=tuple(out_specs), input_output_aliases=aliases,
        compiler_params=pltpu.CompilerParams(has_side_effects=EFFECT),
    )(*[h["src"] for h in handles], *[h["land"] for h in handles], *[h["recv1"] for h in handles], after)
    return [dict(h, send2=res[3 * k], recv2=res[3 * k + 1], land=res[3 * k + 2]) for k, h in enumerate(handles)]


def gather2_wait(h, after, name):
    def body(src_ref, land_ref, send1, recv1, send2, recv2, after_ref, src_dead, got_ref):
        for i in range(4):
            _stage1_copy(i, src_ref, land_ref, send1, recv1).wait_send()
        _stage1_copy(0, src_ref, land_ref, send1, recv1).wait_recv()
        for j in range(3):
            cp = _stage2_copy(j, land_ref, send2, recv2)
            cp.wait_send()
            cp.wait_recv()

    return pl.pallas_call(
        body, name=name,
        out_shape=(pltpu.HBM(h["src"].shape, h["src"].dtype), pltpu.HBM(h["land"].shape, h["land"].dtype)),
        in_specs=(HBM, HBM, SEM, SEM, SEM, SEM, ANY), out_specs=(HBM, HBM), input_output_aliases={0: 0, 1: 1},
        compiler_params=pltpu.CompilerParams(has_side_effects=EFFECT),
    )(h["src"], h["land"], h["send1"], h["recv1"], h["send2"], h["recv2"], after)[1]


def exchange_wait(handles, after, gather, name):
    send_sems, recv_sems, src_thru, land_thru = handles

    def body(src_ref, land_ref, send_sems, recv_sems, after_ref, src_dead, got_ref):
        for cp in _peer_copies(src_ref, land_ref, send_sems, recv_sems, gather):
            cp.wait_send()
            cp.wait_recv()

    return pl.pallas_call(
        body, name=name,
        out_shape=(pltpu.HBM(src_thru.shape, src_thru.dtype), pltpu.HBM(land_thru.shape, land_thru.dtype)),
        in_specs=(HBM, HBM, SEM, SEM, ANY), out_specs=(HBM, HBM), input_output_aliases={0: 0, 1: 1},
        compiler_params=pltpu.CompilerParams(has_side_effects=EFFECT),
    )(src_thru, land_thru, send_sems, recv_sems, after)


def local_step(x, tgt, mod, started, get_w, put_grad, wc, wf, g_mix, bc, lg, lb, gco, gao, g_ffn, bf, g_fin):
    w_in = get_w("w_in", mod)
    proj, h1 = rms_mod_matmul(x, g_mix, mod, 0, 1, w_in, D_IN // N_DEV, "proj_fwd", after=started)
    mix_a, u1 = conv_module_fwd(proj, wc, bc, lg, lb, gco, "conv_module_fwd")
    att, lse = attn_fwd_all(proj, "attn_fwd")
    w_out = get_w("w_out", att)
    y1, mixed = norm_concat_matmul(mix_a, att, gao, w_out, "out_proj_fwd")
    w_up = get_w("w_up", y1)
    up0, x1, h2 = resid_rms_mod_matmul(x, y1, g_ffn, mod, 2, 3, 4, w_up, "up_fwd")
    act = ffn_act_fwd(up0, wf, bf, "ffn_act_fwd")
    w_down = get_w("w_down", act)
    loss_t, dx2, dy2, d_gfin, d_gaf = matmul_loss_bwd(act, w_down, x1, tgt, g_fin, mod, 5, "down_fwd_loss")
    dact = matmul(dy2, w_down, "nt", F32, 512, FFN_TN, "down_bwd_x")
    dw_down = matmul(act, dy2, "tn", BF16, 256, D_MODEL, "down_bwd_w")
    dup0, dbf_g, dbf_v, dwf_g, dwf_v = ffn_bwd(up0, dact, wf, bf, "ffn_bwd", after=put_grad("w_down", dw_down))
    dw_up = matmul_tn_halves(dup0, h2, 256, "up_bwd_w")
    dx1, d_shf, d_scf, d_gffn, dy1, d_gam = matmul_rms_mod_bwd(
        dup0, w_up, x1, dx2, g_ffn, mod, 4, y1, 2, "up_bwd_x", after=put_grad("w_up", dw_up))
    dw_out = matmul(mixed, dy1, "tn", BF16, 256, D_MODEL, "out_proj_bwd_w")
    dmixed, do, dd, d_gao = matmul_combine_bwd(dy1, w_out, att, gao, "out_proj_bwd_x", after=put_grad("w_out", dw_out))
    dqkv = attn_bwd_all(proj, do, lse, dd, "attn_bwd")
    du1, d_gco, d_lg, d_lb, d_bc, d_wc = conv_module_bwd_a(proj, u1, dmixed, lg, lb, gco, "conv_module_bwd_a")
    dproj_a = conv_module_bwd_b(proj, du1, wc, "conv_module_bwd_b")
    dw_in = matmul_tn_pieces(dproj_a, dqkv, h1, "proj_bwd_w")
    dx, d_shm, d_scm, d_gmix = matmul_rms_mod_bwd(
        (dproj_a, dqkv), w_in, x, dx1, g_mix, mod, 1, None, 0, "proj_bwd_x", b_rows=D_IN // N_DEV,
        after=put_grad("w_in", dw_in))
    dmod = jnp.concatenate([d_shm, d_scm, d_gam, d_shf, d_scf, d_gaf], axis=1)
    small = dict(g_norm_mix=d_gmix, b_conv_dw=d_bc, ln_conv_g=d_lg, ln_conv_b=d_lb, g_conv_out=d_gco, g_attn_out=d_gao,
                 g_norm_ffn=d_gffn, b_ffn_dw=jnp.concatenate([dbf_g, dbf_v], axis=1), g_final=d_gfin,
                 w_conv_dw=d_wc, w_ffn_dw=jnp.concatenate([dwf_g, dwf_v], axis=1), dmod=dmod, loss=loss_t[0:1, 0:1])
    return dx, small


def _padw(a, width):
    return jnp.pad(a, ((0, 0), (0, width - a.shape[1])))


def pack_small(t):
    wide = jnp.concatenate([_padw(t["dmod"], PACK_W), _padw(t["b_ffn_dw"], PACK_W), _padw(t["w_ffn_dw"], PACK_W),
                            _padw(t["loss"], PACK_W), jnp.zeros((2, PACK_W), F32)], axis=0)
    z512 = jnp.zeros((1, 512), F32)
    narrow = jnp.concatenate([
        t["g_norm_mix"], t["g_norm_ffn"], t["g_final"],
        jnp.concatenate([t["b_conv_dw"], t["ln_conv_g"]], axis=1),
        jnp.concatenate([t["ln_conv_b"], t["g_conv_out"]], axis=1),
        jnp.concatenate([t["g_attn_out"], z512], axis=1),
        jnp.zeros((2, 1024), F32),
        jnp.pad(t["w_conv_dw"], ((0, 1), (0, 0))).reshape(16, 1024)], axis=0)
    return jnp.concatenate([wide, narrow.reshape(4, PACK_W), jnp.zeros((4, PACK_W), F32)], axis=0)


_NARROW = lambda k, off=0: (8 + k // 6, (k % 6) * 1024 + off)
PACKED_AT = dict(
    b_ada=(0, 0, N_MOD * D_MODEL), b_ffn_dw=(1, 0, 2 * D_FF),
    g_norm_mix=_NARROW(0) + (D_MODEL,), g_norm_ffn=_NARROW(1) + (D_MODEL,), g_final=_NARROW(2) + (D_MODEL,),
    b_conv_dw=_NARROW(3) + (D_CONV,), ln_conv_g=_NARROW(3, 512) + (D_CONV,),
    ln_conv_b=_NARROW(4) + (D_CONV,), g_conv_out=_NARROW(4, 512) + (D_CONV,), g_attn_out=_NARROW(5) + (D_ATTN,))
SMALL_ORDER = list(PACKED_AT)


def small_adamw(parts, wmv, name):
    def body(*refs):
        p_ref = refs[0]
        ins = refs[1:1 + 3 * len(SMALL_ORDER)]
        outs = refs[1 + 3 * len(SMALL_ORDER):]
        g = p_ref[0]
        for k in range(1, N_DEV):
            g = g + p_ref[k]
        for i, n in enumerate(SMALL_ORDER):
            row, lane, width = PACKED_AT[n]
            gp = g[row:row + 1, lane:lane + width]
            w_ref, m_ref, v_ref = ins[3 * i:3 * i + 3]
            g_ref, d_ref, nm_ref, nv_ref = outs[4 * i:4 * i + 4]
            g_ref[...] = gp
            d_ref[...], nm_ref[...], nv_ref[...] = _adam(w_ref[...], gp, m_ref[...], v_ref[...])
        wc_ref, wf_ref, loss_ref = outs[4 * len(SMALL_ORDER):]
        for j in range(CONV_K):
            row, lane = _NARROW(8 + j // 2, (j % 2) * 512)
            wc_ref[j:j + 1, :] = g[row:row + 1, lane:lane + D_CONV]
        wf_ref[...] = g[2:2 + FFN_K, 0:2 * D_FF]
        loss_ref[...] = jnp.broadcast_to(g[5:6, 0:1], (8, 128))

    args, out_shape = [parts], []
    for n in SMALL_ORDER:
        args += list(wmv[n])
        out_shape += [jax.ShapeDtypeStruct(wmv[n][0].shape, F32)] * 4
    out_shape += [jax.ShapeDtypeStruct((CONV_K, D_CONV), F32), jax.ShapeDtypeStruct((FFN_K, 2 * D_FF), F32),
                  jax.ShapeDtypeStruct((8, 128), F32)]
    res = pl.pallas_call(body, out_shape=tuple(out_shape), name=name, compiler_params=_cp())(*args)
    per = {n: tuple(res[4 * i:4 * i + 4]) for i, n in enumerate(SMALL_ORDER)}
    return per, res[-3], res[-2], res[-1][0, 0]


def shard_adamw(items, name):
    def body(*refs):
        ins, outs = refs[:4 * len(items)], refs[4 * len(items):]
        for i in range(len(items)):
            g_ref, w_ref, m_ref, v_ref = ins[4 * i:4 * i + 4]
            og_ref, d_ref, nm_ref, nv_ref = outs[4 * i:4 * i + 4]
            og_ref[...] = g_ref[...]
            d_ref[...], nm_ref[...], nv_ref[...] = _adam(w_ref[...], g_ref[...], m_ref[...], v_ref[...])

    args = [a for item in items for a in item]
    out_shape = tuple(jax.ShapeDtypeStruct(item[1].shape, F32) for item in items for _ in range(4))
    res = pl.pallas_call(body, out_shape=out_shape, name=name, compiler_params=_cp())(*args)
    return [tuple(res[4 * i:4 * i + 4]) for i in range(len(items))]


def _shard(full, n_cols, me):
    return lax.dynamic_slice(full, (0, me * n_cols), (full.shape[0], n_cols))


WEIGHTS = ["w_ada", "b_ada", "g_norm_mix", "w_in", "w_conv_dw", "b_conv_dw", "ln_conv_g", "ln_conv_b", "g_conv_out",
           "g_attn_out", "w_out", "g_norm_ffn", "w_up", "w_ffn_dw", "b_ffn_dw", "w_down", "g_final"]


def kernel(x, c, w_ada, b_ada, g_norm_mix, w_in, w_conv_dw, b_conv_dw, ln_conv_g, ln_conv_b, g_conv_out, g_attn_out, w_out, g_norm_ffn, w_up, w_ffn_dw, b_ffn_dw, w_down, g_final, loss_target, m_w_ada, m_b_ada, m_g_norm_mix, m_w_in, m_w_conv_dw, m_b_conv_dw, m_ln_conv_g, m_ln_conv_b, m_g_conv_out, m_g_attn_out, m_w_out, m_g_norm_ffn, m_w_up, m_w_ffn_dw, m_b_ffn_dw, m_w_down, m_g_final, v_w_ada, v_b_ada, v_g_norm_mix, v_w_in, v_w_conv_dw, v_b_conv_dw, v_ln_conv_g, v_ln_conv_b, v_g_conv_out, v_g_attn_out, v_w_out, v_g_norm_ffn, v_w_up, v_w_ffn_dw, v_b_ffn_dw, v_w_down, v_g_final):
    args = dict(locals())
    me = 4 * lax.axis_index("x") + 2 * lax.axis_index("y") + lax.axis_index("c")
    me1 = me.astype(jnp.int32).reshape(1)

    def flat(name, prefix=""):
        a = args[prefix + name]
        return a.reshape(a.shape[-2] if a.ndim > 1 else 1, a.shape[-1])

    def flat_t(name, prefix=""):
        return args[prefix + name][0].T

    n_in, n_up, r_out, r_down = w_in.shape[2], w_up.shape[2], w_out.shape[1], w_down.shape[1]
    n_ada, n_wc, n_wf = w_ada.shape[2], w_conv_dw.shape[2], w_ffn_dw.shape[2]
    taps_c = jnp.pad(flat("w_conv_dw").reshape(1, CONV_K * n_wc), ((0, 0), (0, 2 * D_MODEL - CONV_K * n_wc)))
    taps_f = jnp.pad(flat("w_ffn_dw").reshape(1, FFN_K * n_wf), ((0, 0), (0, 3 * D_MODEL - FFN_K * n_wf)))
    first = jnp.concatenate([c, taps_c.reshape(2, D_MODEL), taps_f.reshape(3, D_MODEL), jnp.zeros((2, D_MODEL), F32)], axis=0)
    w_in_block = flat_t("w_in").astype(BF16)
    hi = lax.reduce_precision(first, 8, 7)
    mid = lax.reduce_precision(first - hi, 8, 7)
    low = lax.reduce_precision(first - hi - mid, 8, 7)
    terms = jnp.concatenate([hi, mid, low, jnp.zeros((8, D_MODEL), F32)], axis=0).astype(BF16)
    first_block = all_gather(jnp.concatenate([w_in_block, terms], axis=0), "gather_c_taps_w_in")
    terms = first_block[:, n_in:n_in + 24, :].astype(F32)
    first_all = (terms[:, 0:8] + terms[:, 8:16]) + terms[:, 16:24]
    c_all = first_all[:, 0, :]
    wc_full = first_all[:, 1:3, :].reshape(N_DEV, 2 * D_MODEL)[:, :CONV_K * n_wc].reshape(N_DEV, CONV_K, n_wc)
    wc_full = wc_full.transpose(1, 0, 2).reshape(CONV_K, D_CONV)
    wf_full = first_all[:, 3:6, :].reshape(N_DEV, 3 * D_MODEL)[:, :FFN_K * n_wf].reshape(N_DEV, FFN_K, n_wf)
    wf_full = wf_full.transpose(1, 0, 2).reshape(FFN_K, 2 * D_FF)
    mod_cols = ada_fwd(c_all, flat("w_ada"), _shard(flat("b_ada"), n_ada, me), "ada_fwd")
    mod_all = all_gather(mod_cols, "gather_mod")
    mod = lax.dynamic_index_in_dim(mod_all, me, axis=1, keepdims=False).reshape(N_MOD, D_MODEL)
    mod = jnp.pad(mod, ((0, 2), (0, 0)))

    order = ("w_out", "w_up", "w_down")
    blocks = dict(w_up=flat_t("w_up").astype(BF16), w_out=flat("w_out").astype(BF16), w_down=flat("w_down").astype(BF16))
    handles, tok = gather2_start([blocks[name] for name in order], "gather_weights_start", mod_all)
    gathers = dict(zip(order, handles))

    def gathered(name, after):
        if "send2" not in gathers[name]:
            group = ("w_out", "w_up") if name != "w_down" else ("w_down",)
            gathers.update(zip(group, gather2_pass([gathers[w] for w in group], after, f"gather_{name}_pass")))
        land = gather2_wait(gathers[name], after, f"gather_{name}_wait")
        return lax.dynamic_update_index_in_dim(land, blocks[name], me, axis=0)

    def get_w(name, after):
        if name == "w_in":
            return first_block
        return gathered(name, after).reshape(-1, D_MODEL)

    exchanges = {}

    def put_grad(name, dw, after=None):
        dev_major = dw.reshape(N_DEV, -1, D_MODEL)
        (exchanges[name],), token = exchange_start([dev_major], False, f"exchange_{name}_start", after)
        return token

    grad_x, small = local_step(
        x[0], loss_target[0], mod, tok, get_w, put_grad, wc_full, wf_full,
        flat("g_norm_mix"), flat("b_conv_dw"), flat("ln_conv_g"), flat("ln_conv_b"), flat("g_conv_out"),
        flat("g_attn_out"), flat("g_norm_ffn"), flat("b_ffn_dw"), flat("g_final"))

    out = {}

    def finish(name, tr, after):
        mine, parts = exchange_wait(exchanges[name], after, False, f"exchange_{name}_wait")
        if name in ("w_in", "w_up"):
            res = sum_adamw(parts, mine, me1, flat_t(name), flat_t(name, "m_"), flat_t(name, "v_"), tr, "adamw_" + name)
            out[name] = tuple(r.T for r in res)
        else:
            res = out[name] = sum_adamw(parts, mine, me1, flat(name), flat(name, "m_"), flat(name, "v_"), tr,
                                        "adamw_" + name)
        return res[0]

    after = finish("w_down", r_down, grad_x)
    after = finish("w_up", n_up // 2, after)
    after = finish("w_out", r_out, after)
    after = finish("w_in", n_in, after)

    small_all = all_gather(pack_small(small), "gather_small", after)

    wmv = {n: (flat(n), flat(n, "m_"), flat(n, "v_")) for n in SMALL_ORDER}
    per, g_wc, g_wf, loss = small_adamw(small_all, wmv, "adamw_small")
    out.update(per)
    taps = shard_adamw([(_shard(g_wc, n_wc, me), flat("w_conv_dw"), flat("w_conv_dw", "m_"), flat("w_conv_dw", "v_")),
                        (_shard(g_wf, n_wf, me), flat("w_ffn_dw"), flat("w_ffn_dw", "m_"), flat("w_ffn_dw", "v_"))],
                       "adamw_taps")
    out["w_conv_dw"], out["w_ffn_dw"] = taps

    dmod_cols = _shard(small_all[:, 0, :], n_ada, me)
    out["w_ada"] = ada_bwd_adamw(c_all, dmod_cols, flat("w_ada"), flat("w_ada", "m_"), flat("w_ada", "v_"), "adamw_w_ada")

    result = [loss, grad_x[None]]
    for k in range(4):
        result += [out[n][k].reshape(args[n].shape) for n in WEIGHTS]
    return tuple(result)
```

```python
import functools

import jax
import jax.numpy as jnp
from jax import lax
from jax.experimental import pallas as pl
from jax.experimental.pallas import tpu as pltpu

F32 = jnp.float32
BF16 = jnp.bfloat16

N_DEV = 8
SEQ = 2048
D_MODEL = 1024
D_CONV = 512
D_ATTN = 512
HEAD_DIM = 64
CONV_K = 31
D_FF = 2816
FFN_K = 3
D_IN = 2 * D_CONV + 3 * D_ATTN
N_MOD = 6
EPS = 1e-6
ATTN_BLOCK = 128
PATTERNS = ((2048, 1), (512, 4), (128, 16))
NEG = -1e30

ADAM_LR, ADAM_B1, ADAM_B2, ADAM_EPS, ADAM_WD, ADAM_STEP = 0.001, 0.9, 0.999, 1e-08, 0.01, 10

ROWS = 256
CONV_HALO = 32
FFN_HALO = 8
FFN_TN = 1408
VMEM_LIMIT = 56 * 1024 * 1024
PACK_W = 6144


NT = (((1,), (1,)), ((), ()))
ANY = pl.BlockSpec(memory_space=pl.ANY)


def _cp(*sem):
    return pltpu.CompilerParams(dimension_semantics=sem if sem else None, vmem_limit_bytes=VMEM_LIMIT)


def _with_after(body, n_in, after):
    if after is None:
        return body, [], []
    return (lambda *refs: body(*refs[:n_in], *refs[n_in + 1:])), [ANY], [after]


def _sig(x):
    return 1.0 / (1.0 + jnp.exp(-x))


def _rsum(x):
    return jnp.sum(x, axis=0, keepdims=True)


def _mean(x):
    return jnp.mean(x, axis=-1, keepdims=True)


def _acc(ref, val, first):
    @pl.when(first)
    def _():
        ref[...] = val

    @pl.when(jnp.logical_not(first))
    def _():
        ref[...] += val


SUB = 16


def _for_chunks(fn, unroll=1, rows=ROWS):
    def step(i, carry):
        fn(pl.ds(pl.multiple_of(i * SUB, SUB), SUB))
        return carry

    lax.fori_loop(0, rows // SUB, step, 0, unroll=unroll)


PAIR = 2 * ROWS


def _pair_spec(width, col=0):
    return pl.BlockSpec((PAIR, width), lambda i: (i, col))


def _halves():
    return [slice(h * ROWS, (h + 1) * ROWS) for h in range(2)]


def rms_mod_matmul(x, g, mod, sh_row, sc_row, b, b_rows, name, after=None):
    n = N_DEV * b_rows

    def body(x_ref, g_ref, mod_ref, b_ref, o_ref, h_ref):
        for rs in _halves():
            xx = x_ref[rs, :]
            r = lax.rsqrt(_mean(xx * xx) + EPS)
            h = (xx * r * g_ref[...] * (1.0 + mod_ref[sc_row:sc_row + 1, :]) + mod_ref[sh_row:sh_row + 1, :]).astype(BF16)
            h_ref[rs, :] = h
            o_ref[rs, :] = lax.dot_general(h, b_ref[...].reshape(n, D_MODEL), NT, preferred_element_type=F32)

    body, more_specs, more = _with_after(body, 4, after)
    return pl.pallas_call(
        body, out_shape=(jax.ShapeDtypeStruct((SEQ, n), F32), jax.ShapeDtypeStruct((SEQ, D_MODEL), BF16)),
        grid=(SEQ // PAIR,),
        in_specs=[_pair_spec(D_MODEL), _vec_spec(D_MODEL), _vec_spec(D_MODEL, 8),
                  pl.BlockSpec((N_DEV, b_rows, D_MODEL), lambda i: (0, 0, 0))] + more_specs,
        out_specs=(_pair_spec(n), _pair_spec(D_MODEL)), name=name, compiler_params=_cp("parallel"))(x, g, mod, b, *more)


def norm_concat_matmul(mix_a, att, gao, w, name):
    def body(a_ref, att_ref, g_ref, w_ref, y_ref, mixed_ref):
        for rs in _halves():
            aa = att_ref[rs, :]
            mixed_ref[rs, 0:D_CONV] = a_ref[rs, :]
            mixed_ref[rs, D_CONV:] = (aa * lax.rsqrt(_mean(aa * aa) + EPS) * g_ref[...]).astype(BF16)
            y_ref[rs, :] = jnp.dot(mixed_ref[rs, :], w_ref[...], preferred_element_type=F32)

    return pl.pallas_call(
        body, out_shape=(jax.ShapeDtypeStruct((SEQ, D_MODEL), F32), jax.ShapeDtypeStruct((SEQ, D_MODEL), BF16)),
        grid=(SEQ // PAIR,),
        in_specs=[_pair_spec(D_CONV), _pair_spec(D_ATTN), _vec_spec(D_ATTN), pl.BlockSpec(w.shape, lambda i: (0, 0))],
        out_specs=(_pair_spec(D_MODEL), _pair_spec(D_MODEL)), name=name, compiler_params=_cp("parallel"))(mix_a, att, gao, w)


def resid_rms_mod_matmul(x, y, g, mod, ga_row, sh_row, sc_row, w, name):
    n = w.shape[0]

    def body(x_ref, y_ref, g_ref, mod_ref, w_ref, o_ref, x1_ref, h_ref):
        x1 = x_ref[...] + mod_ref[ga_row:ga_row + 1, :] * y_ref[...]
        x1_ref[...] = x1
        r = lax.rsqrt(_mean(x1 * x1) + EPS)
        h = (x1 * r * g_ref[...] * (1.0 + mod_ref[sc_row:sc_row + 1, :]) + mod_ref[sh_row:sh_row + 1, :]).astype(BF16)
        h_ref[...] = h
        o_ref[...] = lax.dot_general(h, w_ref[...], NT, preferred_element_type=F32)

    return pl.pallas_call(
        body,
        out_shape=(jax.ShapeDtypeStruct((SEQ, n), F32), jax.ShapeDtypeStruct((SEQ, D_MODEL), F32),
                   jax.ShapeDtypeStruct((SEQ, D_MODEL), BF16)),
        grid=(SEQ // ROWS,),
        in_specs=[_row_spec(D_MODEL), _row_spec(D_MODEL), _vec_spec(D_MODEL), _vec_spec(D_MODEL, 8),
                  pl.BlockSpec(w.shape, lambda i: (0, 0))],
        out_specs=(_row_spec(n), _row_spec(D_MODEL), _row_spec(D_MODEL)),
        name=name, compiler_params=_cp("parallel"))(x, y, g, mod, w)


def matmul_combine_bwd(dy, w, att, gao, name, after=None):
    def body(dy_ref, w_ref, att_ref, g_ref, dm_ref, do_ref, dd_ref, dg_ref):
        @pl.when(pl.program_id(0) == 0)
        def _():
            dg_ref[...] = jnp.zeros_like(dg_ref)

        same_head = (jnp.right_shift(lax.broadcasted_iota(jnp.int32, (D_ATTN, D_ATTN), 0), 6)
                     == jnp.right_shift(lax.broadcasted_iota(jnp.int32, (D_ATTN, D_ATTN), 1), 6)).astype(F32)
        for rs in _halves():
            dmixed = lax.dot_general(dy_ref[rs, :], w_ref[...], NT, preferred_element_type=F32)
            dm_ref[rs, :] = dmixed
            att = att_ref[rs, :]
            r = lax.rsqrt(_mean(att * att) + EPS)
            xn = att * r
            dm = dmixed[:, D_CONV:]
            dg_ref[...] += _rsum(dm * xn)
            dyn = dm * g_ref[...]
            do = r * (dyn - xn * _mean(dyn * xn))
            do_ref[rs, :] = do
            dd_ref[rs, :] = jnp.dot(do * att, same_head, preferred_element_type=F32, precision=lax.Precision.HIGHEST)

    rs = _pair_spec(D_ATTN)
    f = jax.ShapeDtypeStruct((SEQ, D_ATTN), F32)
    body, more_specs, more = _with_after(body, 4, after)
    return pl.pallas_call(
        body, out_shape=(jax.ShapeDtypeStruct((SEQ, D_MODEL), F32), f, f, jax.ShapeDtypeStruct((1, D_ATTN), F32)),
        grid=(SEQ // PAIR,),
        in_specs=[_pair_spec(D_MODEL), pl.BlockSpec(w.shape, lambda i: (0, 0)), rs, _vec_spec(D_ATTN)] + more_specs,
        out_specs=(_pair_spec(D_MODEL), rs, rs, _vec_spec(D_ATTN)),
        name=name, compiler_params=_cp("arbitrary"))(dy, w, att, gao, *more)


def matmul_loss_bwd(act, w, x1, tgt, g, mod, ga_row, name):
    def body(a_ref, w_ref, x1_ref, t_ref, g_ref, mod_ref, loss_ref, dx2_ref, dy2_ref, dg_ref, dga_ref):
        @pl.when(pl.program_id(0) == 0)
        def _():
            loss_ref[...] = jnp.zeros_like(loss_ref)
            dg_ref[...] = jnp.zeros_like(dg_ref)
            dga_ref[...] = jnp.zeros_like(dga_ref)

        ga = mod_ref[ga_row:ga_row + 1, :]
        for rs in _halves():
            y2 = jnp.dot(a_ref[rs, :], w_ref[...], preferred_element_type=F32)
            x2 = x1_ref[rs, :] + ga * y2
            r = lax.rsqrt(_mean(x2 * x2) + EPS)
            xn = x2 * r
            err = xn * g_ref[...] - t_ref[rs, :]
            loss_ref[...] += jnp.broadcast_to(0.5 * jnp.sum(_mean(err * err)), (8, 128))
            dy = err * (1.0 / D_MODEL)
            dg_ref[...] += _rsum(dy * xn)
            dxn = dy * g_ref[...]
            dx2 = r * (dxn - xn * _mean(dxn * xn))
            dx2_ref[rs, :] = dx2
            dy2_ref[rs, :] = (dx2 * ga).astype(BF16)
            dga_ref[...] += _rsum(dx2 * y2)

    vec = jax.ShapeDtypeStruct((1, D_MODEL), F32)
    rows = _pair_spec
    return pl.pallas_call(
        body,
        out_shape=(jax.ShapeDtypeStruct((8, 128), F32), jax.ShapeDtypeStruct((SEQ, D_MODEL), F32),
                   jax.ShapeDtypeStruct((SEQ, D_MODEL), BF16), vec, vec),
        grid=(SEQ // PAIR,),
        in_specs=[rows(act.shape[1]), pl.BlockSpec(w.shape, lambda i: (0, 0)), rows(D_MODEL), rows(D_MODEL),
                  _vec_spec(D_MODEL), _vec_spec(D_MODEL, 8)],
        out_specs=(pl.BlockSpec((8, 128), lambda i: (0, 0)), rows(D_MODEL), rows(D_MODEL),
                   _vec_spec(D_MODEL), _vec_spec(D_MODEL)),
        name=name, compiler_params=_cp("arbitrary"))(act, w, x1, tgt, g, mod)


def matmul_rms_mod_bwd(a, b, x, dres, g, mod, sc_row, y, ga_row, name, b_rows=None, after=None):
    gated = y is not None
    pieces = isinstance(a, tuple)
    tm = PAIR if pieces else ROWS
    blocks = [slice(h * ROWS, (h + 1) * ROWS) for h in range(tm // ROWS)]
    rows = lambda width: pl.BlockSpec((tm, width), lambda i: (i, 0))
    if pieces:
        a1, a3 = a
        a_args = [a1, a3]
        a_specs = [rows(a1.shape[1]), pl.BlockSpec((3, tm, a3.shape[2]), lambda i: (0, i, 0))]
        b_arg, b_spec = b, pl.BlockSpec((N_DEV, b_rows, D_MODEL), lambda i: (0, 0, 0))
    else:
        k2 = a.shape[2]
        a_args = [a]
        a_specs = [pl.BlockSpec((2, tm, k2), lambda i: (0, i, 0))]
        b_arg, b_spec = b.reshape(2, k2, D_MODEL), pl.BlockSpec((2, k2, D_MODEL), lambda i: (0, 0, 0))
    n_a = len(a_args)

    def body(*refs):
        a_refs, (b_ref, x_ref, dres_ref, g_ref, mod_ref) = refs[:n_a], refs[n_a:n_a + 5]
        if gated:
            y_ref, dx_ref, dsh_ref, dsc_ref, dg_ref, dy_ref, dga_ref = refs[n_a + 5:]
        else:
            dx_ref, dsh_ref, dsc_ref, dg_ref = refs[n_a + 5:]

        @pl.when(pl.program_id(0) == 0)
        def _():
            for ref in (dsh_ref, dsc_ref, dg_ref) + ((dga_ref,) if gated else ()):
                ref[...] = jnp.zeros_like(ref)

        gg = g_ref[...]
        for rs in blocks:
            if pieces:
                wv = b_ref[...].reshape(N_DEV * b_rows, D_MODEL)
                k1, k3 = a_refs[0].shape[1], a_refs[1].shape[2]
                dh = jnp.dot(a_refs[0][rs, :], wv[0:k1], preferred_element_type=F32)
                for t in range(3):
                    dh = dh + jnp.dot(a_refs[1][t, rs, :], wv[k1 + t * k3:k1 + (t + 1) * k3], preferred_element_type=F32)
            else:
                dh = (jnp.dot(a_refs[0][0, rs, :], b_ref[0], preferred_element_type=F32)
                      + jnp.dot(a_refs[0][1, rs, :], b_ref[1], preferred_element_type=F32))
            xx = x_ref[rs, :]
            r = lax.rsqrt(_mean(xx * xx) + EPS)
            xn = xx * r
            dsh_ref[...] += _rsum(dh)
            dsc_ref[...] += _rsum(dh * (xn * gg))
            dt = dh * (1.0 + mod_ref[sc_row:sc_row + 1, :])
            dg_ref[...] += _rsum(dt * xn)
            dxn = dt * gg
            dx = dres_ref[rs, :] + r * (dxn - xn * _mean(dxn * xn))
            dx_ref[rs, :] = dx
            if gated:
                dga_ref[...] += _rsum(dx * y_ref[rs, :])
                dy_ref[rs, :] = (dx * mod_ref[ga_row:ga_row + 1, :]).astype(BF16)

    vec = jax.ShapeDtypeStruct((1, D_MODEL), F32)
    in_specs = a_specs + [b_spec, rows(D_MODEL), rows(D_MODEL), _vec_spec(D_MODEL), _vec_spec(D_MODEL, 8)]
    out_shape = [jax.ShapeDtypeStruct((SEQ, D_MODEL), F32), vec, vec, vec]
    out_specs = [rows(D_MODEL), _vec_spec(D_MODEL), _vec_spec(D_MODEL), _vec_spec(D_MODEL)]
    args = a_args + [b_arg, x, dres, g, mod]
    if gated:
        in_specs.append(rows(D_MODEL))
        out_shape += [jax.ShapeDtypeStruct((SEQ, D_MODEL), BF16), vec]
        out_specs += [rows(D_MODEL), _vec_spec(D_MODEL)]
        args.append(y)
    body, more_specs, more = _with_after(body, len(args), after)
    return pl.pallas_call(
        body, out_shape=tuple(out_shape), grid=(SEQ // tm,), in_specs=in_specs + more_specs, out_specs=tuple(out_specs),
        name=name, compiler_params=_cp("arbitrary"))(*args, *more)


def _prev_halo(halo, width, col):
    per = ROWS // halo
    return pl.BlockSpec((halo, width), lambda i: (jnp.maximum(i * per - 1, 0), col))


def _next_halo(halo, width, col):
    per = ROWS // halo
    last = SEQ // halo - 1
    return pl.BlockSpec((halo, width), lambda i: (jnp.minimum((i + 1) * per, last), col))


CONV_PAD = ROWS + CONV_HALO


def _shift_copies(sh):
    for b in range(1, 8):
        sh[b, 0:CONV_PAD - 8, :] = sh[0, pl.ds(b, CONV_PAD - 8), :]


def _tap(sh, rs_start, offset):
    return sh[offset % 8, pl.ds(pl.multiple_of(rs_start + (offset // 8) * 8, 8), SUB), :]


def _conv_glu(av_ref, ag_ref, avh_ref, agh_ref, sh):
    i = pl.program_id(0)
    hv = avh_ref[...] * _sig(agh_ref[...])
    sh[0, 0:CONV_HALO, :] = jnp.where(i > 0, hv, 0.0)

    def glu(rs):
        sh[0, pl.ds(pl.multiple_of(rs.start + CONV_HALO, SUB), SUB), :] = av_ref[rs, :] * _sig(ag_ref[rs, :])

    _for_chunks(glu)
    _shift_copies(sh)


def _conv_norm(u1, lg_ref, lb_ref):
    mu = _mean(u1)
    cen = u1 - mu
    rs = lax.rsqrt(_mean(cen * cen) + EPS)
    z = cen * rs
    ln = z * lg_ref[...] + lb_ref[...]
    s = _sig(ln)
    return z, rs, ln, s, ln * s


def conv_module_fwd(proj, wc, bc, lg, lb, gco, name):
    def body(av_ref, ag_ref, avh_ref, agh_ref, wc_ref, bc_ref, lg_ref, lb_ref, gco_ref, out_ref, u1_ref, sh):
        _conv_glu(av_ref, ag_ref, avh_ref, agh_ref, sh)

        def conv(rs):
            u1 = jnp.broadcast_to(bc_ref[...], (SUB, D_CONV))
            for j in range(CONV_K):
                u1 = u1 + wc_ref[j:j + 1, :] * _tap(sh, rs.start, CONV_HALO - (CONV_K - 1) + j)
            u1_ref[rs, :] = u1

        _for_chunks(conv)
        _, _, _, _, u2 = _conv_norm(u1_ref[...], lg_ref, lb_ref)
        rc = lax.rsqrt(_mean(u2 * u2) + EPS)
        out_ref[...] = (u2 * rc * gco_ref[...]).astype(BF16)

    v = _vec_spec(D_CONV)
    return pl.pallas_call(
        body, out_shape=(jax.ShapeDtypeStruct((SEQ, D_CONV), BF16), jax.ShapeDtypeStruct((SEQ, D_CONV), F32)),
        grid=(SEQ // ROWS,),
        in_specs=[_row_spec(D_CONV, 0), _row_spec(D_CONV, 1), _prev_halo(CONV_HALO, D_CONV, 0),
                  _prev_halo(CONV_HALO, D_CONV, 1), _vec_spec(D_CONV, CONV_K), v, v, v, v],
        out_specs=(_row_spec(D_CONV), _row_spec(D_CONV)), scratch_shapes=[pltpu.VMEM((8, CONV_PAD, D_CONV), F32)],
        name=name, compiler_params=_cp("parallel"))(proj, proj, proj, proj, wc, bc, lg, lb, gco)


def conv_module_bwd_a(proj, u1, dmixed, lg, lb, gco, name):
    def body(av_ref, ag_ref, avh_ref, agh_ref, u1_ref, dm_ref, lg_ref, lb_ref, gco_ref,
             du1_ref, dgco_ref, dlg_ref, dlb_ref, dbc_ref, dwc_ref, sh, acc):
        first = pl.program_id(0) == 0
        _conv_glu(av_ref, ag_ref, avh_ref, agh_ref, sh)
        z, rs, ln, s, u2 = _conv_norm(u1_ref[...], lg_ref, lb_ref)
        rc = lax.rsqrt(_mean(u2 * u2) + EPS)
        xn = u2 * rc
        dm = dm_ref[...]
        _acc(dgco_ref, _rsum(dm * xn), first)
        dyn = dm * gco_ref[...]
        du2 = rc * (dyn - xn * _mean(dyn * xn))
        dln = du2 * (s * (1.0 + ln * (1.0 - s)))
        _acc(dlg_ref, _rsum(dln * z), first)
        _acc(dlb_ref, _rsum(dln), first)
        dz = dln * lg_ref[...]
        du1 = rs * (dz - _mean(dz) - z * _mean(dz * z))
        du1_ref[...] = du1
        _acc(dbc_ref, _rsum(du1), first)
        acc[...] = jnp.zeros_like(acc)

        def taps(rs):
            d = du1_ref[rs, :]
            for j in range(CONV_K):
                acc[j] += d * _tap(sh, rs.start, CONV_HALO - (CONV_K - 1) + j)

        _for_chunks(taps)

        @pl.when(first)
        def _():
            dwc_ref[...] = jnp.zeros_like(dwc_ref)

        for j in range(CONV_K):
            dwc_ref[j:j + 1, :] += _rsum(acc[j])

    v = _vec_spec(D_CONV)
    vec = jax.ShapeDtypeStruct((1, D_CONV), F32)
    return pl.pallas_call(
        body,
        out_shape=(jax.ShapeDtypeStruct((SEQ, D_CONV), F32), vec, vec, vec, vec, jax.ShapeDtypeStruct((CONV_K, D_CONV), F32)),
        grid=(SEQ // ROWS,),
        in_specs=[_row_spec(D_CONV, 0), _row_spec(D_CONV, 1), _prev_halo(CONV_HALO, D_CONV, 0),
                  _prev_halo(CONV_HALO, D_CONV, 1), _row_spec(D_CONV, 0), _row_spec(D_CONV, 0), v, v, v],
        out_specs=(_row_spec(D_CONV), v, v, v, v, _vec_spec(D_CONV, CONV_K)),
        scratch_shapes=[pltpu.VMEM((8, CONV_PAD, D_CONV), F32), pltpu.VMEM((CONV_K, SUB, D_CONV), F32)],
        name=name, compiler_params=_cp("arbitrary"))(proj, proj, proj, proj, u1, dmixed, lg, lb, gco)


def conv_module_bwd_b(proj, du1, wc, name):
    def body(av_ref, ag_ref, du1_ref, du1n_ref, wc_ref, out_ref, sh):
        i = pl.program_id(0)
        sh[0, 0:ROWS, :] = du1_ref[...]
        sh[0, ROWS:, :] = jnp.where(i < SEQ // ROWS - 1, du1n_ref[...], 0.0)
        _shift_copies(sh)

        def chunk(rs):
            du0 = jnp.zeros((SUB, D_CONV), F32)
            for j in range(CONV_K):
                du0 = du0 + wc_ref[j:j + 1, :] * _tap(sh, rs.start, CONV_K - 1 - j)
            sg = _sig(ag_ref[rs, :])
            out_ref[rs, 0:D_CONV] = (du0 * sg).astype(BF16)
            out_ref[rs, D_CONV:] = (du0 * av_ref[rs, :] * sg * (1.0 - sg)).astype(BF16)

        _for_chunks(chunk)

    return pl.pallas_call(
        body, out_shape=jax.ShapeDtypeStruct((SEQ, 2 * D_CONV), BF16), grid=(SEQ // ROWS,),
        in_specs=[_row_spec(D_CONV, 0), _row_spec(D_CONV, 1), _row_spec(D_CONV, 0), _next_halo(CONV_HALO, D_CONV, 0),
                  _vec_spec(D_CONV, CONV_K)],
        out_specs=_row_spec(2 * D_CONV), scratch_shapes=[pltpu.VMEM((8, CONV_PAD, D_CONV), F32)],
        name=name, compiler_params=_cp("parallel"))(proj, proj, du1, du1, wc)


def _rows(start, size, r):
    return pl.ds(start, size) if r == 1 else pl.ds(start, size, stride=r)


def _unit_rows(r, rho, n, nb):
    win = 2 * ATTN_BLOCK if nb > 1 else ATTN_BLOCK
    if isinstance(n, int):
        kb = max(n - 1, 0)
        q_rows = _rows(rho + r * ATTN_BLOCK * n, ATTN_BLOCK, r)
        k_rows = _rows(rho + r * ATTN_BLOCK * kb, win, r)
    else:
        kb = jnp.maximum(n - 1, 0)
        q_rows = pl.ds(pl.multiple_of(n * ATTN_BLOCK, ATTN_BLOCK), ATTN_BLOCK)
        k_rows = pl.ds(pl.multiple_of(kb * ATTN_BLOCK, ATTN_BLOCK), win)
    return q_rows, k_rows, n - kb


def _band_bias(first_block, transposed):
    shape = (2 * ATTN_BLOCK, ATTN_BLOCK) if transposed else (ATTN_BLOCK, 2 * ATTN_BLOCK)
    q_axis = 1 if transposed else 0
    dist = (0 if first_block else ATTN_BLOCK) + lax.broadcasted_iota(jnp.int32, shape, q_axis) \
        - lax.broadcasted_iota(jnp.int32, shape, 1 - q_axis)
    return jnp.where((dist >= 0) & (dist <= ATTN_BLOCK), 0.0, NEG)


def _per_head(x):
    lane = lax.broadcasted_iota(jnp.int32, x.shape, 1)
    zero = jnp.zeros_like(x)
    return [jnp.where(lane < HEAD_DIM, x, zero), jnp.where(lane >= HEAD_DIM, x, zero)]


SCALE = HEAD_DIM ** -0.5


def _masked_scores(q2, k2, bias):
    return [lax.dot_general(qh, k2, NT, preferred_element_type=F32) + bias for qh in _per_head(q2)]


def _attn_units(r, nb, unit):
    if r == 1:
        def four(i, carry):
            for k in range(4):
                unit(0, 4 * i + k)
            return carry
        lax.fori_loop(0, nb // 4, four, 0)
    else:
        for rho in range(r):
            for n in range(nb):
                unit(rho, n)


N_UNITS = 16


def attn_fwd_all(proj, name):
    def body(q_ref, k_ref, v_ref, att_ref, lse_ref, s_scr, p_scr, lse_scr, den_scr, bias_scr):
        bias_scr[0] = _band_bias(True, False)
        bias_scr[1] = _band_bias(False, False)
        for idx, (sub_len, r) in enumerate(PATTERNS):
            nb = sub_len // ATTN_BLOCK
            win = 2 * ATTN_BLOCK if nb > 1 else ATTN_BLOCK

            def scores(rho, n, r=r, nb=nb, win=win):
                u = rho * nb + n
                q_rows, k_rows, variant = _unit_rows(r, rho, n, nb)
                ss = _masked_scores((q_ref[q_rows, :] * SCALE).astype(BF16), k_ref[k_rows, :].astype(BF16),
                                    bias_scr[variant, :, 0:win])
                for h in range(2):
                    s_scr[2 * u + h, :, 0:win] = ss[h]

            _attn_units(r, nb, scores)

            def softmax(u, carry, win=win):
                lses, dens = [], []
                for h in range(2):
                    sc = s_scr[2 * u + h, :, 0:win]
                    m = jnp.max(sc, axis=1, keepdims=True)
                    p = jnp.exp(sc - m)
                    den = jnp.sum(p, axis=1, keepdims=True)
                    p_scr[2 * u + h, :, 0:win] = p.astype(BF16)
                    lses.append(jnp.broadcast_to(m + jnp.log(den), (ATTN_BLOCK, HEAD_DIM)))
                    dens.append(jnp.broadcast_to(den, (ATTN_BLOCK, HEAD_DIM)))
                lse_scr[u] = jnp.concatenate(lses, axis=1)
                den_scr[u] = jnp.concatenate(dens, axis=1)
                return carry

            lax.fori_loop(0, N_UNITS, softmax, 0, unroll=2)

            def outputs(rho, n, r=r, nb=nb, win=win, idx=idx):
                u = rho * nb + n
                q_rows, k_rows, _ = _unit_rows(r, rho, n, nb)
                vs = _per_head(v_ref[k_rows, :].astype(BF16))
                o = (jnp.dot(p_scr[2 * u, :, 0:win], vs[0], preferred_element_type=F32)
                     + jnp.dot(p_scr[2 * u + 1, :, 0:win], vs[1], preferred_element_type=F32)) / den_scr[u]
                lse = lse_scr[u]
                if idx > 0:
                    old = lse_ref[q_rows, :]
                    top = jnp.maximum(old, lse)
                    new = top + jnp.log(jnp.exp(old - top) + jnp.exp(lse - top))
                    o = att_ref[q_rows, :] * jnp.exp(old - new) + o * jnp.exp(lse - new)
                    lse = new
                att_ref[q_rows, :] = o
                lse_ref[q_rows, :] = lse

            _attn_units(r, nb, outputs)

    blk = lambda first: pl.BlockSpec((SEQ, 128), lambda g: (0, first + g))
    shp = jax.ShapeDtypeStruct((SEQ, D_ATTN), F32)
    big = (2 * N_UNITS, ATTN_BLOCK, 2 * ATTN_BLOCK)
    small = pltpu.VMEM((N_UNITS, ATTN_BLOCK, 128), F32)
    return pl.pallas_call(
        body, out_shape=(shp, shp), grid=(4,), in_specs=[blk(8), blk(12), blk(16)], out_specs=(blk(0), blk(0)),
        scratch_shapes=[pltpu.VMEM(big, F32), pltpu.VMEM(big, BF16), small, small,
                        pltpu.VMEM((2, ATTN_BLOCK, 2 * ATTN_BLOCK), F32)],
        name=name, compiler_params=_cp("parallel"))(proj, proj, proj)


def attn_bwd_all(proj, do, lse, dd, name):
    def body(q_ref, k_ref, v_ref, do_ref, l_ref, dd_ref, out_ref, dq_s, dk_s, dv_s,
             s_scr, dp_scr, ds_scr, st_scr, dpt_scr, pt_scr, dst_scr, qb_scr, kb_scr, dob_scr, bias_scr, bias_t_scr):
        for first_block in (True, False):
            bias_scr[1 - int(first_block)] = _band_bias(first_block, False)
            bias_t_scr[1 - int(first_block)] = _band_bias(first_block, True)
        dq_s[...] = jnp.zeros_like(dq_s)
        dk_s[...] = jnp.zeros_like(dk_s)
        dv_s[...] = jnp.zeros_like(dv_s)
        for sub_len, r in PATTERNS:
            nb = sub_len // ATTN_BLOCK
            win = 2 * ATTN_BLOCK if nb > 1 else ATTN_BLOCK

            def scores(rho, n, r=r, nb=nb, win=win):
                u = rho * nb + n
                q_rows, k_rows, variant = _unit_rows(r, rho, n, nb)
                bias, bias_t = bias_scr[variant, :, 0:win], bias_t_scr[variant, 0:win, :]
                q2 = (q_ref[q_rows, :] * SCALE).astype(BF16)
                kf = k_ref[k_rows, :]
                k2 = kf.astype(BF16)
                do2 = do_ref[q_rows, :].astype(BF16)
                qb_scr[u] = q2
                kb_scr[u, 0:win, :] = (kf * SCALE).astype(BF16)
                dob_scr[u] = do2
                l2 = l_ref[q_rows, :]
                d2 = dd_ref[q_rows, :]
                l2t = l2.T
                d2t = d2.T
                v2 = v_ref[k_rows, :].astype(BF16)
                qs, dos = _per_head(q2), _per_head(do2)
                for h in range(2):
                    c0 = h * HEAD_DIM
                    sc = lax.dot_general(qs[h], k2, NT, preferred_element_type=F32)
                    s_scr[2 * u + h, :, 0:win] = sc + bias - l2[:, c0:c0 + 1]
                    dp_scr[2 * u + h, :, 0:win] = lax.dot_general(dos[h], v2, NT, preferred_element_type=F32) \
                        - d2[:, c0:c0 + 1]
                    sct = lax.dot_general(k2, qs[h], NT, preferred_element_type=F32)
                    st_scr[2 * u + h, 0:win, :] = sct + bias_t - l2t[c0:c0 + 1, :]
                    dpt_scr[2 * u + h, 0:win, :] = lax.dot_general(v2, dos[h], NT, preferred_element_type=F32) \
                        - d2t[c0:c0 + 1, :]

            _attn_units(r, nb, scores)

            def pointwise(hu, carry, win=win):
                ds_scr[hu, :, 0:win] = (jnp.exp(s_scr[hu, :, 0:win]) * dp_scr[hu, :, 0:win]).astype(BF16)
                pt = jnp.exp(st_scr[hu, 0:win, :])
                pt_scr[hu, 0:win, :] = pt.astype(BF16)
                dst_scr[hu, 0:win, :] = (pt * dpt_scr[hu, 0:win, :]).astype(BF16)
                return carry

            lax.fori_loop(0, 2 * N_UNITS, pointwise, 0, unroll=4)

            def grads(rho, n, r=r, nb=nb, win=win):
                u = rho * nb + n
                q_rows, k_rows, _ = _unit_rows(r, rho, n, nb)
                qs, ks, dos = _per_head(qb_scr[u]), _per_head(kb_scr[u, 0:win, :]), _per_head(dob_scr[u])

                def both(scr, rows, rhs):
                    return (jnp.dot(scr[(2 * u,) + rows], rhs[0], preferred_element_type=F32)
                            + jnp.dot(scr[(2 * u + 1,) + rows], rhs[1], preferred_element_type=F32))

                dq_s[q_rows, :] += both(ds_scr, (slice(None), slice(0, win)), ks)
                dk_s[k_rows, :] += both(dst_scr, (slice(0, win), slice(None)), qs)
                dv_s[k_rows, :] += both(pt_scr, (slice(0, win), slice(None)), dos)

            _attn_units(r, nb, grads)
        out_ref[0] = dq_s[...].astype(BF16)
        out_ref[1] = dk_s[...].astype(BF16)
        out_ref[2] = dv_s[...].astype(BF16)

    blk = lambda first: pl.BlockSpec((SEQ, 128), lambda g: (0, first + g))
    acc = pltpu.VMEM((SEQ, 128), F32)
    big = (2 * N_UNITS, ATTN_BLOCK, 2 * ATTN_BLOCK)
    big_t = (2 * N_UNITS, 2 * ATTN_BLOCK, ATTN_BLOCK)
    return pl.pallas_call(
        body, out_shape=jax.ShapeDtypeStruct((3, SEQ, D_ATTN), BF16), grid=(4,),
        in_specs=[blk(8), blk(12), blk(16), blk(0), blk(0), blk(0)],
        out_specs=pl.BlockSpec((3, SEQ, 128), lambda g: (0, 0, g)),
        scratch_shapes=[acc, acc, acc, pltpu.VMEM(big, F32), pltpu.VMEM(big, F32), pltpu.VMEM(big, BF16),
                        pltpu.VMEM(big_t, F32), pltpu.VMEM(big_t, F32), pltpu.VMEM(big_t, BF16), pltpu.VMEM(big_t, BF16),
                        pltpu.VMEM((N_UNITS, ATTN_BLOCK, 128), BF16),
                        pltpu.VMEM((N_UNITS, 2 * ATTN_BLOCK, 128), BF16), pltpu.VMEM((N_UNITS, ATTN_BLOCK, 128), BF16),
                        pltpu.VMEM((2, ATTN_BLOCK, 2 * ATTN_BLOCK), F32), pltpu.VMEM((2, 2 * ATTN_BLOCK, ATTN_BLOCK), F32)],
        name=name, compiler_params=_cp("parallel"))(proj, proj, proj, do, lse, dd)


N_FT = D_FF // FFN_TN


def _ffn_specs():
    per = ROWS // FFN_HALO
    cur_g = pl.BlockSpec((ROWS, FFN_TN), lambda j, i: (i, j))
    cur_v = pl.BlockSpec((ROWS, FFN_TN), lambda j, i: (i, j + N_FT))
    halo_g = pl.BlockSpec((FFN_HALO, FFN_TN), lambda j, i: (jnp.maximum(i * per - 1, 0), j))
    halo_v = pl.BlockSpec((FFN_HALO, FFN_TN), lambda j, i: (jnp.maximum(i * per - 1, 0), j + N_FT))
    w_g = pl.BlockSpec((FFN_K, FFN_TN), lambda j, i: (0, j))
    w_v = pl.BlockSpec((FFN_K, FFN_TN), lambda j, i: (0, j + N_FT))
    b_g = pl.BlockSpec((1, FFN_TN), lambda j, i: (0, j))
    b_v = pl.BlockSpec((1, FFN_TN), lambda j, i: (0, j + N_FT))
    return [cur_g, cur_v, halo_g, halo_v, w_g, w_v, b_g, b_v]


def matmul(a, b, kind, out_dtype, tm, tn, name, b_rows=None):
    stacked = b_rows is not None
    b_shape = (N_DEV * b_rows, b.shape[2]) if stacked else b.shape
    if kind == "nn":
        (m, k), n = a.shape, b_shape[1]
        a_spec = pl.BlockSpec((tm, k), lambda j, i: (i, 0))
        b_spec = pl.BlockSpec((k, tn), lambda j, i: (0, j))
        dims = (((1,), (0,)), ((), ()))
    elif kind == "nt":
        (m, k), n = a.shape, b_shape[0]
        a_spec = pl.BlockSpec((tm, k), lambda j, i: (i, 0))
        b_spec = pl.BlockSpec((tn, k), lambda j, i: (j, 0))
        dims = (((1,), (1,)), ((), ()))
    else:
        (k, m), n = a.shape, b_shape[1]
        a_spec = pl.BlockSpec((k, tm), lambda j, i: (0, i))
        b_spec = pl.BlockSpec((k, tn), lambda j, i: (0, j))
        dims = (((0,), (0,)), ((), ()))
    assert m % tm == 0 and n % tn == 0, (name, m, n, tm, tn)
    if stacked:
        assert b_spec.block_shape[0] == b_shape[0] and kind in ("nn", "nt")
        width = b_spec.block_shape[1]
        b_spec = pl.BlockSpec((N_DEV, b_rows, width), (lambda j, i: (0, 0, j)) if kind == "nn" else (lambda j, i: (0, 0, 0)))

    def body(a_ref, b_ref, o_ref):
        bb = b_ref[...].reshape(b_shape[0], -1) if stacked else b_ref[...]
        o_ref[...] = lax.dot_general(a_ref[...], bb, dims, preferred_element_type=F32).astype(o_ref.dtype)

    return pl.pallas_call(
        body, out_shape=jax.ShapeDtypeStruct((m, n), out_dtype), grid=(n // tn, m // tm),
        in_specs=[a_spec, b_spec], out_specs=pl.BlockSpec((tm, tn), lambda j, i: (i, j)),
        name=name, compiler_params=_cp("parallel", "parallel"))(a, b)


def matmul_tn_pieces(a1, a3, b, name):
    k, n = b.shape
    tm = a3.shape[2]
    n1 = a1.shape[1] // tm

    def body(a1_ref, a3_ref, b_ref, o_ref):
        i = pl.program_id(0)
        tn_dims = (((0,), (0,)), ((), ()))

        @pl.when(i < n1)
        def _():
            o_ref[...] = lax.dot_general(a1_ref[...], b_ref[...], tn_dims, preferred_element_type=F32).astype(BF16)

        @pl.when(i >= n1)
        def _():
            o_ref[...] = lax.dot_general(a3_ref[0], b_ref[...], tn_dims, preferred_element_type=F32).astype(BF16)

    return pl.pallas_call(
        body, out_shape=jax.ShapeDtypeStruct((a1.shape[1] + 3 * tm, n), BF16), grid=(n1 + 3,),
        in_specs=[pl.BlockSpec((k, tm), lambda i: (0, jnp.minimum(i, n1 - 1))),
                  pl.BlockSpec((1, k, tm), lambda i: (jnp.maximum(i - n1, 0), 0, 0)),
                  pl.BlockSpec((k, n), lambda i: (0, 0))],
        out_specs=pl.BlockSpec((tm, n), lambda i: (i, 0)), name=name, compiler_params=_cp("parallel"))(a1, a3, b)


def matmul_tn_halves(a, b, tm, name):
    _, k, m = a.shape
    n = b.shape[1]

    def body(a_ref, b_ref, o_ref):
        o_ref[0] = lax.dot_general(a_ref[0], b_ref[...], (((0,), (0,)), ((), ())), preferred_element_type=F32).astype(BF16)

    return pl.pallas_call(
        body, out_shape=jax.ShapeDtypeStruct((2, m, n), BF16), grid=(2, m // tm),
        in_specs=[pl.BlockSpec((1, k, tm), lambda h, i: (h, 0, i)), pl.BlockSpec((k, n), lambda h, i: (0, 0))],
        out_specs=pl.BlockSpec((1, tm, n), lambda h, i: (h, i, 0)), name=name,
        compiler_params=_cp("parallel", "parallel"))(a, b).reshape(2 * m, n)


def _row_spec(width, col=0):
    return pl.BlockSpec((ROWS, width), lambda i: (i, col))


def _vec_spec(width, rows=1):
    return pl.BlockSpec((rows, width), lambda i: (0, 0))


FFN_RB = 64


def _ffn_lane_blocks(fn):
    for c in range(FFN_TN // 128):
        fn(slice(c * 128, (c + 1) * 128))


def _ffn_taps(cur_ref, halo_ref, head, ls, r0):
    if r0 == 0:
        head[0:FFN_HALO, :] = jnp.where(pl.program_id(1) > 0, halo_ref[:, ls], 0.0)
        head[FFN_HALO:, :] = cur_ref[0:FFN_RB, ls]
        return tuple(head[pl.ds(FFN_HALO - k, FFN_RB), :] for k in (2, 1, 0))
    return tuple(cur_ref[pl.ds(r0 - k, FFN_RB), ls] for k in (2, 1, 0))


def _ffn_conv(taps, w, b):
    return b + w[0] * taps[0] + w[1] * taps[1] + w[2] * taps[2]


def ffn_act_fwd(up0, wf, bf, name):
    def body(g_ref, v_ref, gh_ref, vh_ref, wg_ref, wv_ref, bg_ref, bv_ref, act_ref, head_g, head_v):
        def lanes(ls):
            wg = [wg_ref[t:t + 1, ls] for t in range(FFN_K)]
            wv = [wv_ref[t:t + 1, ls] for t in range(FFN_K)]
            for r0 in range(0, ROWS, FFN_RB):
                gate = _ffn_conv(_ffn_taps(g_ref, gh_ref, head_g, ls, r0), wg, bg_ref[:, ls])
                val = _ffn_conv(_ffn_taps(v_ref, vh_ref, head_v, ls, r0), wv, bv_ref[:, ls])
                act_ref[r0:r0 + FFN_RB, ls] = (gate * _sig(gate) * val).astype(BF16)

        _ffn_lane_blocks(lanes)

    head = pltpu.VMEM((FFN_HALO + FFN_RB, 128), F32)
    return pl.pallas_call(
        body, out_shape=jax.ShapeDtypeStruct((SEQ, D_FF), BF16), grid=(N_FT, SEQ // ROWS),
        in_specs=_ffn_specs(), out_specs=pl.BlockSpec((ROWS, FFN_TN), lambda j, i: (i, j)),
        scratch_shapes=[head, head], name=name, compiler_params=_cp("parallel", "parallel"))(up0, up0, up0, up0, wf, wf, bf, bf)


def ffn_bwd(up0, dact, wf, bf, name, after=None):
    per = ROWS // FFN_HALO
    last = SEQ // FFN_HALO - 1

    def body(g_ref, v_ref, gh_ref, vh_ref, wg_ref, wv_ref, bg_ref, bv_ref, da_ref, gn_ref, vn_ref, dan_ref,
             out_ref, dbg_ref, dbv_ref, dwg_ref, dwv_ref, pad, head_g, head_v, dgp, dvp, acc):
        i = pl.program_id(1)
        first = i == 0
        acc[...] = jnp.zeros_like(acc)

        def grads(gate, val, da):
            s = _sig(gate)
            return da * val * (s * (1.0 + gate * (1.0 - s))), da * (gate * s)

        fold = lambda q: jnp.sum(q.reshape(FFN_RB // 8, 8, 128), axis=0)

        def lanes(ls):
            wg = [wg_ref[t:t + 1, ls] for t in range(FFN_K)]
            wv = [wv_ref[t:t + 1, ls] for t in range(FFN_K)]
            sums = [jnp.zeros((8, 128), F32)] * (2 + 2 * FFN_K)
            for r0 in range(0, ROWS, FFN_RB):
                gs = _ffn_taps(g_ref, gh_ref, head_g, ls, r0)
                vs = _ffn_taps(v_ref, vh_ref, head_v, ls, r0)
                dgate, dval = grads(_ffn_conv(gs, wg, bg_ref[:, ls]), _ffn_conv(vs, wv, bv_ref[:, ls]),
                                    da_ref[r0:r0 + FFN_RB, ls])
                dgp[r0:r0 + FFN_RB, ls] = dgate
                dvp[r0:r0 + FFN_RB, ls] = dval
                new = [dgate, dval] + [dgate * gs[t] for t in range(FFN_K)] + [dval * vs[t] for t in range(FFN_K)]
                sums = [a + fold(q) for a, q in zip(sums, new)]
            for k in range(2 + 2 * FFN_K):
                acc[k, 0:8, ls] = sums[k]

        _ffn_lane_blocks(lanes)
        _acc(dbg_ref, _rsum(acc[0]), first)
        _acc(dbv_ref, _rsum(acc[1]), first)
        _acc(dwg_ref, jnp.concatenate([_rsum(acc[2 + t]) for t in range(FFN_K)], axis=0), first)
        _acc(dwv_ref, jnp.concatenate([_rsum(acc[5 + t]) for t in range(FFN_K)], axis=0), first)

        def conv_next(cur_ref, nxt_ref, w_ref, b_ref):
            pad[0:FFN_HALO, :] = cur_ref[ROWS - FFN_HALO:, :]
            pad[FFN_HALO:, :] = nxt_ref[...]
            return (b_ref[...] + w_ref[0:1, :] * pad[pl.ds(FFN_HALO - 2, FFN_HALO), :]
                    + w_ref[1:2, :] * pad[pl.ds(FFN_HALO - 1, FFN_HALO), :] + w_ref[2:3, :] * nxt_ref[...])

        gate_n = conv_next(g_ref, gn_ref, wg_ref, bg_ref)
        val_n = conv_next(v_ref, vn_ref, wv_ref, bv_ref)
        dgate_n, dval_n = grads(gate_n, val_n, dan_ref[...])
        inside = i < SEQ // ROWS - 1
        dgp[ROWS:, :] = jnp.where(inside, dgate_n, 0.0)
        dvp[ROWS:, :] = jnp.where(inside, dval_n, 0.0)

        def back(ls):
            for half, (dp, w_ref) in enumerate(((dgp, wg_ref), (dvp, wv_ref))):
                w = [w_ref[t:t + 1, ls] for t in range(FFN_K)]
                for r0 in range(0, ROWS, FFN_RB):
                    out_ref[half, r0:r0 + FFN_RB, ls] = (
                        w[2] * dp[r0:r0 + FFN_RB, ls] + w[1] * dp[pl.ds(r0 + 1, FFN_RB), ls]
                        + w[0] * dp[pl.ds(r0 + 2, FFN_RB), ls]).astype(BF16)

        _ffn_lane_blocks(back)

    body, more_specs, more = _with_after(body, 12, after)
    head = pltpu.VMEM((FFN_HALO + FFN_RB, 128), F32)
    ext = pltpu.VMEM((ROWS + FFN_HALO, FFN_TN), F32)
    vec = jax.ShapeDtypeStruct((1, D_FF), F32)
    taps = jax.ShapeDtypeStruct((FFN_K, D_FF), F32)
    cur = pl.BlockSpec((ROWS, FFN_TN), lambda j, i: (i, j))
    nxt = lambda off: pl.BlockSpec((FFN_HALO, FFN_TN), lambda j, i: (jnp.minimum((i + 1) * per, last), j + off))
    vs = pl.BlockSpec((1, FFN_TN), lambda j, i: (0, j))
    ts = pl.BlockSpec((FFN_K, FFN_TN), lambda j, i: (0, j))
    return pl.pallas_call(
        body, out_shape=(jax.ShapeDtypeStruct((2, SEQ, D_FF), BF16), vec, vec, taps, taps), grid=(N_FT, SEQ // ROWS),
        in_specs=_ffn_specs() + [cur, nxt(0), nxt(N_FT), nxt(0)] + more_specs,
        out_specs=(pl.BlockSpec((2, ROWS, FFN_TN), lambda j, i: (0, i, j)), vs, vs, ts, ts),
        scratch_shapes=[pltpu.VMEM((2 * FFN_HALO, FFN_TN), F32), head, head, ext, ext,
                        pltpu.VMEM((2 + 2 * FFN_K, SUB, FFN_TN), F32)],
        name=name, compiler_params=_cp("parallel", "arbitrary"))(up0, up0, up0, up0, wf, wf, bf, bf, dact, up0, up0, dact,
                                                                 *more)


def ada_fwd(c_all, w_ada, b_cols, name):
    def body(c_ref, w_ref, b_ref, o_ref):
        cc = c_ref[...]
        sc = (cc * _sig(cc)).astype(BF16)
        o_ref[...] = jnp.dot(sc, w_ref[...].astype(BF16), preferred_element_type=F32) + b_ref[...]

    return pl.pallas_call(body, out_shape=jax.ShapeDtypeStruct((N_DEV, w_ada.shape[1]), F32), name=name,
                          compiler_params=_cp())(c_all, w_ada, b_cols)


def _adam(w, g, m, v):
    m = ADAM_B1 * m + (1.0 - ADAM_B1) * g
    v = ADAM_B2 * v + (1.0 - ADAM_B2) * (g * g)
    m_hat = m / (1.0 - ADAM_B1 ** ADAM_STEP)
    v_hat = v / (1.0 - ADAM_B2 ** ADAM_STEP)
    delta = -ADAM_LR * (m_hat / (jnp.sqrt(v_hat) + ADAM_EPS) + ADAM_WD * w)
    return delta, m, v


def ada_bwd_adamw(c_all, dmod_cols, w, m, v, name):
    rows, cols = w.shape
    tr = 256

    def body(c_ref, dm_ref, w_ref, m_ref, v_ref, g_ref, d_ref, nm_ref, nv_ref):
        cc = c_ref[...]
        sc = (cc * _sig(cc)).T
        g = sc[:, 0:1] * dm_ref[0:1, :]
        for b in range(1, N_DEV):
            g = g + sc[:, b:b + 1] * dm_ref[b:b + 1, :]
        g_ref[...] = g
        d_ref[...], nm_ref[...], nv_ref[...] = _adam(w_ref[...], g, m_ref[...], v_ref[...])

    blk = pl.BlockSpec((tr, cols), lambda i: (i, 0))
    shp = jax.ShapeDtypeStruct((rows, cols), F32)
    return pl.pallas_call(
        body, out_shape=(shp, shp, shp, shp), grid=(rows // tr,),
        in_specs=[pl.BlockSpec((N_DEV, tr), lambda i: (0, i)), pl.BlockSpec((N_DEV, cols), lambda i: (0, 0)), blk, blk, blk],
        out_specs=(blk, blk, blk, blk), name=name, compiler_params=_cp("parallel"))(c_all, dmod_cols, w, m, v)


def sum_adamw(parts, mine, me, w, m, v, tr, name):
    n_parts, rows, cols = parts.shape

    def body(me_ref, p_ref, own_ref, w_ref, m_ref, v_ref, g_ref, d_ref, nm_ref, nv_ref):
        def chunk(rs):
            g = own_ref[0, rs, :].astype(F32)
            for k in range(1, n_parts):
                g = g + p_ref[k, rs, :].astype(F32)
            g_ref[rs, :] = g
            d_ref[rs, :], nm_ref[rs, :], nv_ref[rs, :] = _adam(w_ref[rs, :], g, m_ref[rs, :], v_ref[rs, :])

        _for_chunks(chunk, 2, tr)

    blk = pl.BlockSpec((tr, cols), lambda i, me_ref: (i, 0))
    shp = jax.ShapeDtypeStruct((rows, cols), F32)
    grid_spec = pltpu.PrefetchScalarGridSpec(
        num_scalar_prefetch=1, grid=(rows // tr,),
        in_specs=[pl.BlockSpec((n_parts, tr, cols), lambda i, me_ref: (0, i, 0)),
                  pl.BlockSpec((1, tr, cols), lambda i, me_ref: (me_ref[0], i, 0)), blk, blk, blk],
        out_specs=(blk, blk, blk, blk))
    return pl.pallas_call(body, out_shape=(shp, shp, shp, shp), grid_spec=grid_spec, name=name,
                          compiler_params=_cp("parallel"))(me, parts, mine, w, m, v)


MESH = pl.DeviceIdType.MESH


def all_gather(block, name, after=None):
    extra = () if after is None else (after,)

    def body(x_ref, *refs):
        out_ref, send_sems, recv_sems, local_sem = refs[len(extra):]
        x, y, c = lax.axis_index("x"), lax.axis_index("y"), lax.axis_index("c")
        me, sibling = (x, y, c), (x, y, 1 - c)
        chips = [(1 - x, y), (x, 1 - y), (1 - x, 1 - y)]

        def slot(px, py, pc):
            return out_ref.at[4 * px + 2 * py + pc]

        def copy(k, blk, to, src=None):
            return pltpu.make_async_remote_copy(
                src_ref=slot(*blk) if src is None else src, dst_ref=slot(*blk),
                send_sem=send_sems.at[k], recv_sem=recv_sems.at[k], device_id=to, device_id_type=MESH)

        mine = pltpu.make_async_copy(x_ref, slot(*me), local_sem)
        mine.start()
        first = [copy(0, me, sibling, src=x_ref)]
        first += [copy(1 + j, me, (*chip, c), src=x_ref) for j, chip in enumerate(chips)]
        for cp in first:
            cp.start()
        passed = [copy(4 + j, (*chip, c), sibling) for j, chip in enumerate(chips)]
        for j, chip in enumerate(chips):
            copy(1 + j, (*chip, c), me).wait_recv()
            passed[j].start()
        copy(0, sibling, me).wait_recv()
        for j, chip in enumerate(chips):
            copy(4 + j, (*chip, 1 - c), me).wait_recv()
        for cp in first + passed:
            cp.wait_send()
        mine.wait()

    return pl.pallas_call(
        body, out_shape=jax.ShapeDtypeStruct((N_DEV,) + block.shape, block.dtype), in_specs=[ANY] * (1 + len(extra)), out_specs=ANY,
        scratch_shapes=[pltpu.SemaphoreType.DMA((7,)), pltpu.SemaphoreType.DMA((7,)), pltpu.SemaphoreType.DMA],
        name=name)(block, *extra)


HBM = pl.BlockSpec(memory_space=pltpu.HBM)
SEM = pl.BlockSpec(memory_space=pltpu.SEMAPHORE)
EFFECT = pltpu.SideEffectType.DATAFLOW_SIDE_EFFECTING


def _peer_copies(src_ref, land_ref, send_sems, recv_sems, gather):
    x, y, c = lax.axis_index("x"), lax.axis_index("y"), lax.axis_index("c")
    me = 4 * x + 2 * y + c
    copies = []
    for k in range(1, N_DEV):
        px = 1 - x if k & 4 else x
        py = 1 - y if k & 2 else y
        pc = 1 - c if k & 1 else c
        copies.append(pltpu.make_async_remote_copy(
            src_ref=src_ref if gather else src_ref.at[4 * px + 2 * py + pc],
            dst_ref=land_ref.at[me] if gather else land_ref.at[k],
            send_sem=send_sems.at[k - 1], recv_sem=recv_sems.at[k - 1], device_id=(px, py, pc), device_id_type=MESH))
    return copies


def exchange_start(srcs, gather, name, after=None):
    n = len(srcs)
    land_shapes = [(N_DEV,) + src.shape if gather else src.shape for src in srcs]
    extra = () if after is None else (after,)

    def body(*refs):
        src_refs, land_refs = refs[0:n], refs[n:2 * n]
        outs = refs[2 * n + len(extra):]
        for k in range(n):
            for cp in _peer_copies(src_refs[k], land_refs[k], outs[4 * k], outs[4 * k + 1], gather):
                cp.start()
        token = outs[4 * n]
        token[...] = jnp.zeros_like(token)

    out_shape, out_specs, aliases = [], [], {}
    for k, src in enumerate(srcs):
        out_shape += [pltpu.SemaphoreType.DMA((N_DEV - 1,)), pltpu.SemaphoreType.DMA((N_DEV - 1,)),
                      pltpu.HBM(src.shape, src.dtype), pltpu.HBM(land_shapes[k], src.dtype)]
        out_specs += [SEM, SEM, HBM, HBM]
        aliases[k] = 4 * k + 2
        aliases[n + k] = 4 * k + 3
    out_shape.append(jax.ShapeDtypeStruct((8, 128), F32))
    out_specs.append(pl.BlockSpec(memory_space=pltpu.VMEM))
    res = pl.pallas_call(
        body, name=name, out_shape=tuple(out_shape), in_specs=(HBM,) * (2 * n) + (ANY,) * len(extra),
        out_specs=tuple(out_specs), input_output_aliases=aliases,
        compiler_params=pltpu.CompilerParams(has_side_effects=EFFECT),
    )(*[pltpu.with_memory_space_constraint(src, pltpu.HBM) for src in srcs],
      *[pltpu.with_memory_space_constraint(lax.empty(shp, src.dtype), pltpu.HBM) for shp, src in zip(land_shapes, srcs)],
      *extra)
    return [tuple(res[4 * k:4 * k + 4]) for k in range(n)], res[4 * n]


def _stage1_peer(i):
    x, y, c = lax.axis_index("x"), lax.axis_index("y"), lax.axis_index("c")
    if i == 0:
        return (x, y, 1 - c)
    return (1 - x if i & 1 else x, 1 - y if i & 2 else y, c)


def _slot_of(peer):
    return 4 * peer[0] + 2 * peer[1] + peer[2]


def _stage1_copy(i, src_ref, land_ref, send_sems, recv_sems):
    me = _slot_of((lax.axis_index("x"), lax.axis_index("y"), lax.axis_index("c")))
    return pltpu.make_async_remote_copy(src_ref=src_ref, dst_ref=land_ref.at[me], send_sem=send_sems.at[i],
                                        recv_sem=recv_sems.at[i], device_id=_stage1_peer(i), device_id_type=MESH)


def _stage2_copy(j, land_ref, send_sems, recv_sems):
    slot = _slot_of(_stage1_peer(j + 1))
    return pltpu.make_async_remote_copy(src_ref=land_ref.at[slot], dst_ref=land_ref.at[slot], send_sem=send_sems.at[j],
                                        recv_sem=recv_sems.at[j], device_id=_stage1_peer(0), device_id_type=MESH)


def gather2_start(srcs, name, after=None):
    n = len(srcs)
    extra = () if after is None else (after,)

    def body(*refs):
        src_refs, land_refs = refs[0:n], refs[n:2 * n]
        outs = refs[2 * n + len(extra):]
        for k in range(n):
            for i in range(4):
                _stage1_copy(i, src_refs[k], land_refs[k], outs[4 * k], outs[4 * k + 1]).start()
        outs[4 * n][...] = jnp.zeros((8, 128), F32)

    out_shape, out_specs, aliases = [], [], {}
    for k, src in enumerate(srcs):
        out_shape += [pltpu.SemaphoreType.DMA((4,)), pltpu.SemaphoreType.DMA((4,)),
                      pltpu.HBM(src.shape, src.dtype), pltpu.HBM((N_DEV,) + src.shape, src.dtype)]
        out_specs += [SEM, SEM, HBM, HBM]
        aliases[k] = 4 * k + 2
        aliases[n + k] = 4 * k + 3
    out_shape.append(jax.ShapeDtypeStruct((8, 128), F32))
    out_specs.append(pl.BlockSpec(memory_space=pltpu.VMEM))
    res = pl.pallas_call(
        body, name=name, out_shape=tuple(out_shape), in_specs=(HBM,) * (2 * n) + (ANY,) * len(extra),
        out_specs=tuple(out_specs), input_output_aliases=aliases,
        compiler_params=pltpu.CompilerParams(has_side_effects=EFFECT),
    )(*[pltpu.with_memory_space_constraint(src, pltpu.HBM) for src in srcs],
      *[pltpu.with_memory_space_constraint(lax.empty((N_DEV,) + src.shape, src.dtype), pltpu.HBM) for src in srcs], *extra)
    return [dict(send1=res[4 * k], recv1=res[4 * k + 1], src=res[4 * k + 2], land=res[4 * k + 3]) for k in range(n)], \
        res[4 * n]


def gather2_pass(handles, after, name):
    n = len(handles)

    def body(*refs):
        src_refs, land_refs, recv1 = refs[0:n], refs[n:2 * n], refs[2 * n:3 * n]
        outs = refs[3 * n + 1:]
        for k in range(n):
            for j in range(3):
                _stage1_copy(j + 1, src_refs[k], land_refs[k], recv1[k], recv1[k]).wait_recv()
                _stage2_copy(j, land_refs[k], outs[3 * k], outs[3 * k + 1]).start()

    out_shape, out_specs, aliases = [], [], {}
    for k, h in enumerate(handles):
        out_shape += [pltpu.SemaphoreType.DMA((3,)), pltpu.SemaphoreType.DMA((3,)), pltpu.HBM(h["land"].shape, h["land"].dtype)]
        out_specs += [SEM, SEM, HBM]
        aliases[n + k] = 3 * k + 2
    res = pl.pallas_call(
        body, name=name, out_shape=tuple(out_shape), in_specs=(HBM,) * (2 * n) + (SEM,) * n + (ANY,),
        out_specs=tuple(out_specs), input_output_aliases=aliases,
        compiler_params=pltpu.CompilerParams(has_side_effects=EFFECT),
    )(*[h["src"] for h in handles], *[h["land"] for h in handles], *[h["recv1"] for h in handles], after)
    return [dict(h, send2=res[3 * k], recv2=res[3 * k + 1], land=res[3 * k + 2]) for k, h in enumerate(handles)]


def gather2_wait(h, after, name):
    def body(src_ref, land_ref, send1, recv1, send2, recv2, after_ref, src_dead, got_ref):
        for i in range(4):
            _stage1_copy(i, src_ref, land_ref, send1, recv1).wait_send()
        _stage1_copy(0, src_ref, land_ref, send1, recv1).wait_recv()
        for j in range(3):
            cp = _stage2_copy(j, land_ref, send2, recv2)
            cp.wait_send()
            cp.wait_recv()

    return pl.pallas_call(
        body, name=name,
        out_shape=(pltpu.HBM(h["src"].shape, h["src"].dtype), pltpu.HBM(h["land"].shape, h["land"].dtype)),
        in_specs=(HBM, HBM, SEM, SEM, SEM, SEM, ANY), out_specs=(HBM, HBM), input_output_aliases={0: 0, 1: 1},
        compiler_params=pltpu.CompilerParams(has_side_effects=EFFECT),
    )(h["src"], h["land"], h["send1"], h["recv1"], h["send2"], h["recv2"], after)[1]


def exchange_wait(handles, after, gather, name):
    send_sems, recv_sems, src_thru, land_thru = handles

    def body(src_ref, land_ref, send_sems, recv_sems, after_ref, src_dead, got_ref):
        for cp in _peer_copies(src_ref, land_ref, send_sems, recv_sems, gather):
            cp.wait_send()
            cp.wait_recv()

    return pl.pallas_call(
        body, name=name,
        out_shape=(pltpu.HBM(src_thru.shape, src_thru.dtype), pltpu.HBM(land_thru.shape, land_thru.dtype)),
        in_specs=(HBM, HBM, SEM, SEM, ANY), out_specs=(HBM, HBM), input_output_aliases={0: 0, 1: 1},
        compiler_params=pltpu.CompilerParams(has_side_effects=EFFECT),
    )(src_thru, land_thru, send_sems, recv_sems, after)


def local_step(x, tgt, mod, started, get_w, put_grad, wc, wf, g_mix, bc, lg, lb, gco, gao, g_ffn, bf, g_fin):
    w_in = get_w("w_in", mod)
    proj, h1 = rms_mod_matmul(x, g_mix, mod, 0, 1, w_in, D_IN // N_DEV, "proj_fwd", after=started)
    mix_a, u1 = conv_module_fwd(proj, wc, bc, lg, lb, gco, "conv_module_fwd")
    att, lse = attn_fwd_all(proj, "attn_fwd")
    w_out = get_w("w_out", att)
    y1, mixed = norm_concat_matmul(mix_a, att, gao, w_out, "out_proj_fwd")
    w_up = get_w("w_up", y1)
    up0, x1, h2 = resid_rms_mod_matmul(x, y1, g_ffn, mod, 2, 3, 4, w_up, "up_fwd")
    act = ffn_act_fwd(up0, wf, bf, "ffn_act_fwd")
    w_down = get_w("w_down", act)
    loss_t, dx2, dy2, d_gfin, d_gaf = matmul_loss_bwd(act, w_down, x1, tgt, g_fin, mod, 5, "down_fwd_loss")
    dact = matmul(dy2, w_down, "nt", F32, 512, FFN_TN, "down_bwd_x")
    dw_down = matmul(act, dy2, "tn", BF16, 256, D_MODEL, "down_bwd_w")
    dup0, dbf_g, dbf_v, dwf_g, dwf_v = ffn_bwd(up0, dact, wf, bf, "ffn_bwd", after=put_grad("w_down", dw_down))
    dw_up = matmul_tn_halves(dup0, h2, 256, "up_bwd_w")
    dx1, d_shf, d_scf, d_gffn, dy1, d_gam = matmul_rms_mod_bwd(
        dup0, w_up, x1, dx2, g_ffn, mod, 4, y1, 2, "up_bwd_x", after=put_grad("w_up", dw_up))
    dw_out = matmul(mixed, dy1, "tn", BF16, 256, D_MODEL, "out_proj_bwd_w")
    dmixed, do, dd, d_gao = matmul_combine_bwd(dy1, w_out, att, gao, "out_proj_bwd_x", after=put_grad("w_out", dw_out))
    dqkv = attn_bwd_all(proj, do, lse, dd, "attn_bwd")
    du1, d_gco, d_lg, d_lb, d_bc, d_wc = conv_module_bwd_a(proj, u1, dmixed, lg, lb, gco, "conv_module_bwd_a")
    dproj_a = conv_module_bwd_b(proj, du1, wc, "conv_module_bwd_b")
    dw_in = matmul_tn_pieces(dproj_a, dqkv, h1, "proj_bwd_w")
    dx, d_shm, d_scm, d_gmix = matmul_rms_mod_bwd(
        (dproj_a, dqkv), w_in, x, dx1, g_mix, mod, 1, None, 0, "proj_bwd_x", b_rows=D_IN // N_DEV,
        after=put_grad("w_in", dw_in))
    dmod = jnp.concatenate([d_shm, d_scm, d_gam, d_shf, d_scf, d_gaf], axis=1)
    small = dict(g_norm_mix=d_gmix, b_conv_dw=d_bc, ln_conv_g=d_lg, ln_conv_b=d_lb, g_conv_out=d_gco, g_attn_out=d_gao,
                 g_norm_ffn=d_gffn, b_ffn_dw=jnp.concatenate([dbf_g, dbf_v], axis=1), g_final=d_gfin,
                 w_conv_dw=d_wc, w_ffn_dw=jnp.concatenate([dwf_g, dwf_v], axis=1), dmod=dmod, loss=loss_t[0:1, 0:1])
    return dx, small


def _padw(a, width):
    return jnp.pad(a, ((0, 0), (0, width - a.shape[1])))


def pack_small(t):
    wide = jnp.concatenate([_padw(t["dmod"], PACK_W), _padw(t["b_ffn_dw"], PACK_W), _padw(t["w_ffn_dw"], PACK_W),
                            _padw(t["loss"], PACK_W), jnp.zeros((2, PACK_W), F32)], axis=0)
    z512 = jnp.zeros((1, 512), F32)
    narrow = jnp.concatenate([
        t["g_norm_mix"], t["g_norm_ffn"], t["g_final"],
        jnp.concatenate([t["b_conv_dw"], t["ln_conv_g"]], axis=1),
        jnp.concatenate([t["ln_conv_b"], t["g_conv_out"]], axis=1),
        jnp.concatenate([t["g_attn_out"], z512], axis=1),
        jnp.zeros((2, 1024), F32),
        jnp.pad(t["w_conv_dw"], ((0, 1), (0, 0))).reshape(16, 1024)], axis=0)
    return jnp.concatenate([wide, narrow.reshape(4, PACK_W), jnp.zeros((4, PACK_W), F32)], axis=0)


_NARROW = lambda k, off=0: (8 + k // 6, (k % 6) * 1024 + off)
PACKED_AT = dict(
    b_ada=(0, 0, N_MOD * D_MODEL), b_ffn_dw=(1, 0, 2 * D_FF),
    g_norm_mix=_NARROW(0) + (D_MODEL,), g_norm_ffn=_NARROW(1) + (D_MODEL,), g_final=_NARROW(2) + (D_MODEL,),
    b_conv_dw=_NARROW(3) + (D_CONV,), ln_conv_g=_NARROW(3, 512) + (D_CONV,),
    ln_conv_b=_NARROW(4) + (D_CONV,), g_conv_out=_NARROW(4, 512) + (D_CONV,), g_attn_out=_NARROW(5) + (D_ATTN,))
SMALL_ORDER = list(PACKED_AT)


def small_adamw(parts, wmv, name):
    def body(*refs):
        p_ref = refs[0]
        ins = refs[1:1 + 3 * len(SMALL_ORDER)]
        outs = refs[1 + 3 * len(SMALL_ORDER):]
        g = p_ref[0]
        for k in range(1, N_DEV):
            g = g + p_ref[k]
        for i, n in enumerate(SMALL_ORDER):
            row, lane, width = PACKED_AT[n]
            gp = g[row:row + 1, lane:lane + width]
            w_ref, m_ref, v_ref = ins[3 * i:3 * i + 3]
            g_ref, d_ref, nm_ref, nv_ref = outs[4 * i:4 * i + 4]
            g_ref[...] = gp
            d_ref[...], nm_ref[...], nv_ref[...] = _adam(w_ref[...], gp, m_ref[...], v_ref[...])
        wc_ref, wf_ref, loss_ref = outs[4 * len(SMALL_ORDER):]
        for j in range(CONV_K):
            row, lane = _NARROW(8 + j // 2, (j % 2) * 512)
            wc_ref[j:j + 1, :] = g[row:row + 1, lane:lane + D_CONV]
        wf_ref[...] = g[2:2 + FFN_K, 0:2 * D_FF]
        loss_ref[...] = jnp.broadcast_to(g[5:6, 0:1], (8, 128))

    args, out_shape = [parts], []
    for n in SMALL_ORDER:
        args += list(wmv[n])
        out_shape += [jax.ShapeDtypeStruct(wmv[n][0].shape, F32)] * 4
    out_shape += [jax.ShapeDtypeStruct((CONV_K, D_CONV), F32), jax.ShapeDtypeStruct((FFN_K, 2 * D_FF), F32),
                  jax.ShapeDtypeStruct((8, 128), F32)]
    res = pl.pallas_call(body, out_shape=tuple(out_shape), name=name, compiler_params=_cp())(*args)
    per = {n: tuple(res[4 * i:4 * i + 4]) for i, n in enumerate(SMALL_ORDER)}
    return per, res[-3], res[-2], res[-1][0, 0]


def shard_adamw(items, name):
    def body(*refs):
        ins, outs = refs[:4 * len(items)], refs[4 * len(items):]
        for i in range(len(items)):
            g_ref, w_ref, m_ref, v_ref = ins[4 * i:4 * i + 4]
            og_ref, d_ref, nm_ref, nv_ref = outs[4 * i:4 * i + 4]
            og_ref[...] = g_ref[...]
            d_ref[...], nm_ref[...], nv_ref[...] = _adam(w_ref[...], g_ref[...], m_ref[...], v_ref[...])

    args = [a for item in items for a in item]
    out_shape = tuple(jax.ShapeDtypeStruct(item[1].shape, F32) for item in items for _ in range(4))
    res = pl.pallas_call(body, out_shape=out_shape, name=name, compiler_params=_cp())(*args)
    return [tuple(res[4 * i:4 * i + 4]) for i in range(len(items))]


def _shard(full, n_cols, me):
    return lax.dynamic_slice(full, (0, me * n_cols), (full.shape[0], n_cols))


WEIGHTS = ["w_ada", "b_ada", "g_norm_mix", "w_in", "w_conv_dw", "b_conv_dw", "ln_conv_g", "ln_conv_b", "g_conv_out",
           "g_attn_out", "w_out", "g_norm_ffn", "w_up", "w_ffn_dw", "b_ffn_dw", "w_down", "g_final"]


def kernel(x, c, w_ada, b_ada, g_norm_mix, w_in, w_conv_dw, b_conv_dw, ln_conv_g, ln_conv_b, g_conv_out, g_attn_out, w_out, g_norm_ffn, w_up, w_ffn_dw, b_ffn_dw, w_down, g_final, loss_target, m_w_ada, m_b_ada, m_g_norm_mix, m_w_in, m_w_conv_dw, m_b_conv_dw, m_ln_conv_g, m_ln_conv_b, m_g_conv_out, m_g_attn_out, m_w_out, m_g_norm_ffn, m_w_up, m_w_ffn_dw, m_b_ffn_dw, m_w_down, m_g_final, v_w_ada, v_b_ada, v_g_norm_mix, v_w_in, v_w_conv_dw, v_b_conv_dw, v_ln_conv_g, v_ln_conv_b, v_g_conv_out, v_g_attn_out, v_w_out, v_g_norm_ffn, v_w_up, v_w_ffn_dw, v_b_ffn_dw, v_w_down, v_g_final):
    args = dict(locals())
    me = 4 * lax.axis_index("x") + 2 * lax.axis_index("y") + lax.axis_index("c")
    me1 = me.astype(jnp.int32).reshape(1)

    def flat(name, prefix=""):
        a = args[prefix + name]
        return a.reshape(a.shape[-2] if a.ndim > 1 else 1, a.shape[-1])

    def flat_t(name, prefix=""):
        return args[prefix + name][0].T

    n_in, n_up, r_out, r_down = w_in.shape[2], w_up.shape[2], w_out.shape[1], w_down.shape[1]
    n_ada, n_wc, n_wf = w_ada.shape[2], w_conv_dw.shape[2], w_ffn_dw.shape[2]
    taps_c = jnp.pad(flat("w_conv_dw").reshape(1, CONV_K * n_wc), ((0, 0), (0, 2 * D_MODEL - CONV_K * n_wc)))
    taps_f = jnp.pad(flat("w_ffn_dw").reshape(1, FFN_K * n_wf), ((0, 0), (0, 3 * D_MODEL - FFN_K * n_wf)))
    first = jnp.concatenate([c, taps_c.reshape(2, D_MODEL), taps_f.reshape(3, D_MODEL), jnp.zeros((2, D_MODEL), F32)], axis=0)
    w_in_block = flat_t("w_in").astype(BF16)
    hi = lax.reduce_precision(first, 8, 7)
    mid = lax.reduce_precision(first - hi, 8, 7)
    low = lax.reduce_precision(first - hi - mid, 8, 7)
    terms = jnp.concatenate([hi, mid, low, jnp.zeros((8, D_MODEL), F32)], axis=0).astype(BF16)
    first_block = all_gather(jnp.concatenate([w_in_block, terms], axis=0), "gather_c_taps_w_in")
    terms = first_block[:, n_in:n_in + 24, :].astype(F32)
    first_all = (terms[:, 0:8] + terms[:, 8:16]) + terms[:, 16:24]
    c_all = first_all[:, 0, :]
    wc_full = first_all[:, 1:3, :].reshape(N_DEV, 2 * D_MODEL)[:, :CONV_K * n_wc].reshape(N_DEV, CONV_K, n_wc)
    wc_full = wc_full.transpose(1, 0, 2).reshape(CONV_K, D_CONV)
    wf_full = first_all[:, 3:6, :].reshape(N_DEV, 3 * D_MODEL)[:, :FFN_K * n_wf].reshape(N_DEV, FFN_K, n_wf)
    wf_full = wf_full.transpose(1, 0, 2).reshape(FFN_K, 2 * D_FF)
    mod_cols = ada_fwd(c_all, flat("w_ada"), _shard(flat("b_ada"), n_ada, me), "ada_fwd")
    mod_all = all_gather(mod_cols, "gather_mod")
    mod = lax.dynamic_index_in_dim(mod_all, me, axis=1, keepdims=False).reshape(N_MOD, D_MODEL)
    mod = jnp.pad(mod, ((0, 2), (0, 0)))

    order = ("w_out", "w_up", "w_down")
    blocks = dict(w_up=flat_t("w_up").astype(BF16), w_out=flat("w_out").astype(BF16), w_down=flat("w_down").astype(BF16))
    handles, tok = gather2_start([blocks[name] for name in order], "gather_weights_start", mod_all)
    gathers = dict(zip(order, handles))

    def gathered(name, after):
        if "send2" not in gathers[name]:
            group = ("w_out", "w_up") if name != "w_down" else ("w_down",)
            gathers.update(zip(group, gather2_pass([gathers[w] for w in group], after, f"gather_{name}_pass")))
        land = gather2_wait(gathers[name], after, f"gather_{name}_wait")
        return lax.dynamic_update_index_in_dim(land, blocks[name], me, axis=0)

    def get_w(name, after):
        if name == "w_in":
            return first_block
        return gathered(name, after).reshape(-1, D_MODEL)

    exchanges = {}

    def put_grad(name, dw, after=None):
        dev_major = dw.reshape(N_DEV, -1, D_MODEL)
        (exchanges[name],), token = exchange_start([dev_major], False, f"exchange_{name}_start", after)
        return token

    grad_x, small = local_step(
        x[0], loss_target[0], mod, tok, get_w, put_grad, wc_full, wf_full,
        flat("g_norm_mix"), flat("b_conv_dw"), flat("ln_conv_g"), flat("ln_conv_b"), flat("g_conv_out"),
        flat("g_attn_out"), flat("g_norm_ffn"), flat("b_ffn_dw"), flat("g_final"))

    out = {}

    def finish(name, tr, after):
        mine, parts = exchange_wait(exchanges[name], after, False, f"exchange_{name}_wait")
        if name in ("w_in", "w_up"):
            res = sum_adamw(parts, mine, me1, flat_t(name), flat_t(name, "m_"), flat_t(name, "v_"), tr, "adamw_" + name)
            out[name] = tuple(r.T for r in res)
        else:
            res = out[name] = sum_adamw(parts, mine, me1, flat(name), flat(name, "m_"), flat(name, "v_"), tr,
                                        "adamw_" + name)
        return res[0]

    after = finish("w_down", r_down, grad_x)
    after = finish("w_up", n_up // 2, after)
    after = finish("w_out", r_out, after)
    after = finish("w_in", n_in, after)

    small_all = all_gather(pack_small(small), "gather_small", after)

    wmv = {n: (flat(n), flat(n, "m_"), flat(n, "v_")) for n in SMALL_ORDER}
    per, g_wc, g_wf, loss = small_adamw(small_all, wmv, "adamw_small")
    out.update(per)
    taps = shard_adamw([(_shard(g_wc, n_wc, me), flat("w_conv_dw"), flat("w_conv_dw", "m_"), flat("w_conv_dw", "v_")),
                        (_shard(g_wf, n_wf, me), flat("w_ffn_dw"), flat("w_ffn_dw", "m_"), flat("w_ffn_dw", "v_"))],
                       "adamw_taps")
    out["w_conv_dw"], out["w_ffn_dw"] = taps

    dmod_cols = _shard(small_all[:, 0, :], n_ada, me)
    out["w_ada"] = ada_bwd_adamw(c_all, dmod_cols, flat("w_ada"), flat("w_ada", "m_"), flat("w_ada", "v_"), "adamw_w_ada")

    result = [loss, grad_x[None]]
    for k in range(4):
        result += [out[n][k].reshape(args[n].shape) for n in WEIGHTS]
    return tuple(result)
```

```python
import jax
import jax.numpy as jnp
from jax import lax
from jax.experimental import pallas as pl
from jax.experimental.pallas import tpu as pltpu

F32 = jnp.float32
BF16 = jnp.bfloat16

N_DEV = 8
SEQ = 2048
D_MODEL = 1024
D_CONV = 512
D_ATTN = 512
HEAD_DIM = 64
CONV_K = 31
D_FF = 2816
FFN_K = 3
D_IN = 2 * D_CONV + 3 * D_ATTN
N_MOD = 6
EPS = 1e-6
ATTN_BLOCK = 128
PATTERNS = ((2048, 1), (512, 4), (128, 16))
NEG = -1e30

ADAM_LR, ADAM_B1, ADAM_B2, ADAM_EPS, ADAM_WD, ADAM_STEP = 0.001, 0.9, 0.999, 1e-08, 0.01, 10

ROWS = 256
CONV_HALO = 32
FFN_HALO = 8
FFN_TN = 1408
VMEM_LIMIT = 56 * 1024 * 1024


NT = (((1,), (1,)), ((), ()))
ANY = pl.BlockSpec(memory_space=pl.ANY)


def _cp(*sem):
    return pltpu.CompilerParams(dimension_semantics=sem if sem else None, vmem_limit_bytes=VMEM_LIMIT)


def _with_after(body, n_in, after):
    if after is None:
        return body, [], []
    return (lambda *refs: body(*refs[:n_in], *refs[n_in + 1:])), [ANY], [after]


def _sig(x):
    return 1.0 / (1.0 + jnp.exp(-x))


def _rsum(x):
    return jnp.sum(x, axis=0, keepdims=True)


def _mean(x):
    return jnp.mean(x, axis=-1, keepdims=True)


def _acc(ref, val, first):
    @pl.when(first)
    def _():
        ref[...] = val

    @pl.when(jnp.logical_not(first))
    def _():
        ref[...] += val


SUB = 16


def _for_chunks(fn, unroll=1, rows=ROWS):
    def step(i, carry):
        fn(pl.ds(pl.multiple_of(i * SUB, SUB), SUB))
        return carry

    lax.fori_loop(0, rows // SUB, step, 0, unroll=unroll)


PAIR = 2 * ROWS


def _pair_spec(width, col=0):
    return pl.BlockSpec((PAIR, width), lambda i: (i, col))


def _halves():
    return [slice(h * ROWS, (h + 1) * ROWS) for h in range(2)]


def rms_mod_matmul(x, g, mod, sh_row, sc_row, b, b_rows, name, after=None):
    n = N_DEV * b_rows

    def body(x_ref, g_ref, mod_ref, b_ref, o_ref, h_ref):
        for rs in _halves():
            xx = x_ref[rs, :]
            r = lax.rsqrt(_mean(xx * xx) + EPS)
            h = (xx * r * g_ref[...] * (1.0 + mod_ref[sc_row:sc_row + 1, :]) + mod_ref[sh_row:sh_row + 1, :]).astype(BF16)
            h_ref[rs, :] = h
            o_ref[rs, :] = lax.dot_general(h, b_ref[...].reshape(n, D_MODEL), NT, preferred_element_type=F32)

    body, more_specs, more = _with_after(body, 4, after)
    return pl.pallas_call(
        body, out_shape=(jax.ShapeDtypeStruct((SEQ, n), F32), jax.ShapeDtypeStruct((SEQ, D_MODEL), BF16)),
        grid=(SEQ // PAIR,),
        in_specs=[_pair_spec(D_MODEL), _vec_spec(D_MODEL), _vec_spec(D_MODEL, 8),
                  pl.BlockSpec((N_DEV, b_rows, D_MODEL), lambda i: (0, 0, 0))] + more_specs,
        out_specs=(_pair_spec(n), _pair_spec(D_MODEL)), name=name, compiler_params=_cp("parallel"))(x, g, mod, b, *more)


def norm_concat_matmul(mix_a, att, gao, w, name):
    def body(a_ref, att_ref, g_ref, w_ref, y_ref, mixed_ref):
        for rs in _halves():
            aa = att_ref[rs, :]
            mixed_ref[rs, 0:D_CONV] = a_ref[rs, :]
            mixed_ref[rs, D_CONV:] = (aa * lax.rsqrt(_mean(aa * aa) + EPS) * g_ref[...]).astype(BF16)
            y_ref[rs, :] = jnp.dot(mixed_ref[rs, :], w_ref[...], preferred_element_type=F32)

    return pl.pallas_call(
        body, out_shape=(jax.ShapeDtypeStruct((SEQ, D_MODEL), F32), jax.ShapeDtypeStruct((SEQ, D_MODEL), BF16)),
        grid=(SEQ // PAIR,),
        in_specs=[_pair_spec(D_CONV), _pair_spec(D_ATTN), _vec_spec(D_ATTN), pl.BlockSpec(w.shape, lambda i: (0, 0))],
        out_specs=(_pair_spec(D_MODEL), _pair_spec(D_MODEL)), name=name, compiler_params=_cp("parallel"))(mix_a, att, gao, w)


def resid_rms_mod_matmul(x, y, g, mod, ga_row, sh_row, sc_row, w, name):
    n = w.shape[0]

    def body(x_ref, y_ref, g_ref, mod_ref, w_ref, o_ref, x1_ref, h_ref):
        x1 = x_ref[...] + mod_ref[ga_row:ga_row + 1, :] * y_ref[...]
        x1_ref[...] = x1
        r = lax.rsqrt(_mean(x1 * x1) + EPS)
        h = (x1 * r * g_ref[...] * (1.0 + mod_ref[sc_row:sc_row + 1, :]) + mod_ref[sh_row:sh_row + 1, :]).astype(BF16)
        h_ref[...] = h
        o_ref[...] = lax.dot_general(h, w_ref[...], NT, preferred_element_type=F32)

    return pl.pallas_call(
        body,
        out_shape=(jax.ShapeDtypeStruct((SEQ, n), F32), jax.ShapeDtypeStruct((SEQ, D_MODEL), F32),
                   jax.ShapeDtypeStruct((SEQ, D_MODEL), BF16)),
        grid=(SEQ // ROWS,),
        in_specs=[_row_spec(D_MODEL), _row_spec(D_MODEL), _vec_spec(D_MODEL), _vec_spec(D_MODEL, 8),
                  pl.BlockSpec(w.shape, lambda i: (0, 0))],
        out_specs=(_row_spec(n), _row_spec(D_MODEL), _row_spec(D_MODEL)),
        name=name, compiler_params=_cp("parallel"))(x, y, g, mod, w)


def matmul_combine_bwd(dy, w, att, gao, name, after=None):
    def body(dy_ref, w_ref, att_ref, g_ref, dm_ref, do_ref, dd_ref, dg_ref):
        @pl.when(pl.program_id(0) == 0)
        def _():
            dg_ref[...] = jnp.zeros_like(dg_ref)

        same_head = (jnp.right_shift(lax.broadcasted_iota(jnp.int32, (D_ATTN, D_ATTN), 0), 6)
                     == jnp.right_shift(lax.broadcasted_iota(jnp.int32, (D_ATTN, D_ATTN), 1), 6)).astype(F32)
        for rs in _halves():
            dmixed = lax.dot_general(dy_ref[rs, :], w_ref[...], NT, preferred_element_type=F32)
            dm_ref[rs, :] = dmixed
            att = att_ref[rs, :]
            r = lax.rsqrt(_mean(att * att) + EPS)
            xn = att * r
            dm = dmixed[:, D_CONV:]
            dg_ref[...] += _rsum(dm * xn)
            dyn = dm * g_ref[...]
            do = r * (dyn - xn * _mean(dyn * xn))
            do_ref[rs, :] = do
            dd_ref[rs, :] = jnp.dot(do * att, same_head, preferred_element_type=F32, precision=lax.Precision.HIGHEST)

    rs = _pair_spec(D_ATTN)
    f = jax.ShapeDtypeStruct((SEQ, D_ATTN), F32)
    body, more_specs, more = _with_after(body, 4, after)
    return pl.pallas_call(
        body, out_shape=(jax.ShapeDtypeStruct((SEQ, D_MODEL), F32), f, f, jax.ShapeDtypeStruct((1, D_ATTN), F32)),
        grid=(SEQ // PAIR,),
        in_specs=[_pair_spec(D_MODEL), pl.BlockSpec(w.shape, lambda i: (0, 0)), rs, _vec_spec(D_ATTN)] + more_specs,
        out_specs=(_pair_spec(D_MODEL), rs, rs, _vec_spec(D_ATTN)),
        name=name, compiler_params=_cp("arbitrary"))(dy, w, att, gao, *more)


def matmul_loss_bwd(act, w, x1, tgt, g, mod, ga_row, name):
    def body(a_ref, w_ref, x1_ref, t_ref, g_ref, mod_ref, loss_ref, dx2_ref, dy2_ref, dg_ref, dga_ref):
        @pl.when(pl.program_id(0) == 0)
        def _():
            loss_ref[...] = jnp.zeros_like(loss_ref)
            dg_ref[...] = jnp.zeros_like(dg_ref)
            dga_ref[...] = jnp.zeros_like(dga_ref)

        ga = mod_ref[ga_row:ga_row + 1, :]
        for rs in _halves():
            y2 = jnp.dot(a_ref[rs, :], w_ref[...], preferred_element_type=F32)
            x2 = x1_ref[rs, :] + ga * y2
            r = lax.rsqrt(_mean(x2 * x2) + EPS)
            xn = x2 * r
            err = xn * g_ref[...] - t_ref[rs, :]
            loss_ref[...] += jnp.broadcast_to(0.5 * jnp.sum(_mean(err * err)), (8, 128))
            dy = err * (1.0 / D_MODEL)
            dg_ref[...] += _rsum(dy * xn)
            dxn = dy * g_ref[...]
            dx2 = r * (dxn - xn * _mean(dxn * xn))
            dx2_ref[rs, :] = dx2
            dy2_ref[rs, :] = (dx2 * ga).astype(BF16)
            dga_ref[...] += _rsum(dx2 * y2)

    vec = jax.ShapeDtypeStruct((1, D_MODEL), F32)
    rows = _pair_spec
    return pl.pallas_call(
        body,
        out_shape=(jax.ShapeDtypeStruct((8, 128), F32), jax.ShapeDtypeStruct((SEQ, D_MODEL), F32),
                   jax.ShapeDtypeStruct((SEQ, D_MODEL), BF16), vec, vec),
        grid=(SEQ // PAIR,),
        in_specs=[rows(act.shape[1]), pl.BlockSpec(w.shape, lambda i: (0, 0)), rows(D_MODEL), rows(D_MODEL),
                  _vec_spec(D_MODEL), _vec_spec(D_MODEL, 8)],
        out_specs=(pl.BlockSpec((8, 128), lambda i: (0, 0)), rows(D_MODEL), rows(D_MODEL),
                   _vec_spec(D_MODEL), _vec_spec(D_MODEL)),
        name=name, compiler_params=_cp("arbitrary"))(act, w, x1, tgt, g, mod)


def matmul_rms_mod_bwd(a, b, x, dres, g, mod, sc_row, y, ga_row, name, b_rows=None, after=None):
    gated = y is not None
    pieces = isinstance(a, tuple)
    tm = PAIR if pieces else ROWS
    blocks = [slice(h * ROWS, (h + 1) * ROWS) for h in range(tm // ROWS)]
    rows = lambda width: pl.BlockSpec((tm, width), lambda i: (i, 0))
    if pieces:
        a1, a3 = a
        a_args = [a1, a3]
        a_specs = [rows(a1.shape[1]), pl.BlockSpec((3, tm, a3.shape[2]), lambda i: (0, i, 0))]
        b_arg, b_spec = b, pl.BlockSpec((N_DEV, b_rows, D_MODEL), lambda i: (0, 0, 0))
    else:
        k2 = a.shape[2]
        a_args = [a]
        a_specs = [pl.BlockSpec((2, tm, k2), lambda i: (0, i, 0))]
        b_arg, b_spec = b.reshape(2, k2, D_MODEL), pl.BlockSpec((2, k2, D_MODEL), lambda i: (0, 0, 0))
    n_a = len(a_args)

    def body(*refs):
        a_refs, (b_ref, x_ref, dres_ref, g_ref, mod_ref) = refs[:n_a], refs[n_a:n_a + 5]
        if gated:
            y_ref, dx_ref, dsh_ref, dsc_ref, dg_ref, dy_ref, dga_ref = refs[n_a + 5:]
        else:
            dx_ref, dsh_ref, dsc_ref, dg_ref = refs[n_a + 5:]

        @pl.when(pl.program_id(0) == 0)
        def _():
            for ref in (dsh_ref, dsc_ref, dg_ref) + ((dga_ref,) if gated else ()):
                ref[...] = jnp.zeros_like(ref)

        gg = g_ref[...]
        for rs in blocks:
            if pieces:
                wv = b_ref[...].reshape(N_DEV * b_rows, D_MODEL)
                k1, k3 = a_refs[0].shape[1], a_refs[1].shape[2]
                dh = jnp.dot(a_refs[0][rs, :], wv[0:k1], preferred_element_type=F32)
                for t in range(3):
                    dh = dh + jnp.dot(a_refs[1][t, rs, :], wv[k1 + t * k3:k1 + (t + 1) * k3], preferred_element_type=F32)
            else:
                dh = (jnp.dot(a_refs[0][0, rs, :], b_ref[0], preferred_element_type=F32)
                      + jnp.dot(a_refs[0][1, rs, :], b_ref[1], preferred_element_type=F32))
            xx = x_ref[rs, :]
            r = lax.rsqrt(_mean(xx * xx) + EPS)
            xn = xx * r
            dsh_ref[...] += _rsum(dh)
            dsc_ref[...] += _rsum(dh * (xn * gg))
            dt = dh * (1.0 + mod_ref[sc_row:sc_row + 1, :])
            dg_ref[...] += _rsum(dt * xn)
            dxn = dt * gg
            dx = dres_ref[rs, :] + r * (dxn - xn * _mean(dxn * xn))
            dx_ref[rs, :] = dx
            if gated:
                dga_ref[...] += _rsum(dx * y_ref[rs, :])
                dy_ref[rs, :] = (dx * mod_ref[ga_row:ga_row + 1, :]).astype(BF16)

    vec = jax.ShapeDtypeStruct((1, D_MODEL), F32)
    in_specs = a_specs + [b_spec, rows(D_MODEL), rows(D_MODEL), _vec_spec(D_MODEL), _vec_spec(D_MODEL, 8)]
    out_shape = [jax.ShapeDtypeStruct((SEQ, D_MODEL), F32), vec, vec, vec]
    out_specs = [rows(D_MODEL), _vec_spec(D_MODEL), _vec_spec(D_MODEL), _vec_spec(D_MODEL)]
    args = a_args + [b_arg, x, dres, g, mod]
    if gated:
        in_specs.append(rows(D_MODEL))
        out_shape += [jax.ShapeDtypeStruct((SEQ, D_MODEL), BF16), vec]
        out_specs += [rows(D_MODEL), _vec_spec(D_MODEL)]
        args.append(y)
    body, more_specs, more = _with_after(body, len(args), after)
    return pl.pallas_call(
        body, out_shape=tuple(out_shape), grid=(SEQ // tm,), in_specs=in_specs + more_specs, out_specs=tuple(out_specs),
        name=name, compiler_params=_cp("arbitrary"))(*args, *more)


def _prev_halo(halo, width, col):
    per = ROWS // halo
    return pl.BlockSpec((halo, width), lambda i: (jnp.maximum(i * per - 1, 0), col))


def _next_halo(halo, width, col):
    per = ROWS // halo
    last = SEQ // halo - 1
    return pl.BlockSpec((halo, width), lambda i: (jnp.minimum((i + 1) * per, last), col))


CONV_PAD = ROWS + CONV_HALO


def _shift_copies(sh):
    for b in range(1, 8):
        sh[b, 0:CONV_PAD - 8, :] = sh[0, pl.ds(b, CONV_PAD - 8), :]


def _tap(sh, rs_start, offset):
    return sh[offset % 8, pl.ds(pl.multiple_of(rs_start + (offset // 8) * 8, 8), SUB), :]


def _conv_glu(av_ref, ag_ref, avh_ref, agh_ref, sh):
    i = pl.program_id(0)
    hv = avh_ref[...] * _sig(agh_ref[...])
    sh[0, 0:CONV_HALO, :] = jnp.where(i > 0, hv, 0.0)

    def glu(rs):
        sh[0, pl.ds(pl.multiple_of(rs.start + CONV_HALO, SUB), SUB), :] = av_ref[rs, :] * _sig(ag_ref[rs, :])

    _for_chunks(glu)
    _shift_copies(sh)


def _conv_norm(u1, lg_ref, lb_ref):
    mu = _mean(u1)
    cen = u1 - mu
    rs = lax.rsqrt(_mean(cen * cen) + EPS)
    z = cen * rs
    ln = z * lg_ref[...] + lb_ref[...]
    s = _sig(ln)
    return z, rs, ln, s, ln * s


def conv_module_fwd(proj, wc, bc, lg, lb, gco, name):
    def body(av_ref, ag_ref, avh_ref, agh_ref, wc_ref, bc_ref, lg_ref, lb_ref, gco_ref, out_ref, u1_ref, sh):
        _conv_glu(av_ref, ag_ref, avh_ref, agh_ref, sh)

        def conv(rs):
            u1 = jnp.broadcast_to(bc_ref[...], (SUB, D_CONV))
            for j in range(CONV_K):
                u1 = u1 + wc_ref[j:j + 1, :] * _tap(sh, rs.start, CONV_HALO - (CONV_K - 1) + j)
            u1_ref[rs, :] = u1

        _for_chunks(conv)
        _, _, _, _, u2 = _conv_norm(u1_ref[...], lg_ref, lb_ref)
        rc = lax.rsqrt(_mean(u2 * u2) + EPS)
        out_ref[...] = (u2 * rc * gco_ref[...]).astype(BF16)

    v = _vec_spec(D_CONV)
    return pl.pallas_call(
        body, out_shape=(jax.ShapeDtypeStruct((SEQ, D_CONV), BF16), jax.ShapeDtypeStruct((SEQ, D_CONV), F32)),
        grid=(SEQ // ROWS,),
        in_specs=[_row_spec(D_CONV, 0), _row_spec(D_CONV, 1), _prev_halo(CONV_HALO, D_CONV, 0),
                  _prev_halo(CONV_HALO, D_CONV, 1), _vec_spec(D_CONV, CONV_K), v, v, v, v],
        out_specs=(_row_spec(D_CONV), _row_spec(D_CONV)), scratch_shapes=[pltpu.VMEM((8, CONV_PAD, D_CONV), F32)],
        name=name, compiler_params=_cp("parallel"))(proj, proj, proj, proj, wc, bc, lg, lb, gco)


def conv_module_bwd_a(proj, u1, dmixed, lg, lb, gco, name):
    def body(av_ref, ag_ref, avh_ref, agh_ref, u1_ref, dm_ref, lg_ref, lb_ref, gco_ref,
             du1_ref, dgco_ref, dlg_ref, dlb_ref, dbc_ref, dwc_ref, sh, acc):
        first = pl.program_id(0) == 0
        _conv_glu(av_ref, ag_ref, avh_ref, agh_ref, sh)
        z, rs, ln, s, u2 = _conv_norm(u1_ref[...], lg_ref, lb_ref)
        rc = lax.rsqrt(_mean(u2 * u2) + EPS)
        xn = u2 * rc
        dm = dm_ref[...]
        _acc(dgco_ref, _rsum(dm * xn), first)
        dyn = dm * gco_ref[...]
        du2 = rc * (dyn - xn * _mean(dyn * xn))
        dln = du2 * (s * (1.0 + ln * (1.0 - s)))
        _acc(dlg_ref, _rsum(dln * z), first)
        _acc(dlb_ref, _rsum(dln), first)
        dz = dln * lg_ref[...]
        du1 = rs * (dz - _mean(dz) - z * _mean(dz * z))
        du1_ref[...] = du1
        _acc(dbc_ref, _rsum(du1), first)
        acc[...] = jnp.zeros_like(acc)

        def taps(rs):
            d = du1_ref[rs, :]
            for j in range(CONV_K):
                acc[j] += d * _tap(sh, rs.start, CONV_HALO - (CONV_K - 1) + j)

        _for_chunks(taps)

        @pl.when(first)
        def _():
            dwc_ref[...] = jnp.zeros_like(dwc_ref)

        for j in range(CONV_K):
            dwc_ref[j:j + 1, :] += _rsum(acc[j])

    v = _vec_spec(D_CONV)
    vec = jax.ShapeDtypeStruct((1, D_CONV), F32)
    return pl.pallas_call(
        body,
        out_shape=(jax.ShapeDtypeStruct((SEQ, D_CONV), F32), vec, vec, vec, vec, jax.ShapeDtypeStruct((CONV_K, D_CONV), F32)),
        grid=(SEQ // ROWS,),
        in_specs=[_row_spec(D_CONV, 0), _row_spec(D_CONV, 1), _prev_halo(CONV_HALO, D_CONV, 0),
                  _prev_halo(CONV_HALO, D_CONV, 1), _row_spec(D_CONV, 0), _row_spec(D_CONV, 0), v, v, v],
        out_specs=(_row_spec(D_CONV), v, v, v, v, _vec_spec(D_CONV, CONV_K)),
        scratch_shapes=[pltpu.VMEM((8, CONV_PAD, D_CONV), F32), pltpu.VMEM((CONV_K, SUB, D_CONV), F32)],
        name=name, compiler_params=_cp("arbitrary"))(proj, proj, proj, proj, u1, dmixed, lg, lb, gco)


def conv_module_bwd_b(proj, du1, wc, name):
    def body(av_ref, ag_ref, du1_ref, du1n_ref, wc_ref, out_ref, sh):
        i = pl.program_id(0)
        sh[0, 0:ROWS, :] = du1_ref[...]
        sh[0, ROWS:, :] = jnp.where(i < SEQ // ROWS - 1, du1n_ref[...], 0.0)
        _shift_copies(sh)

        def chunk(rs):
            du0 = jnp.zeros((SUB, D_CONV), F32)
            for j in range(CONV_K):
                du0 = du0 + wc_ref[j:j + 1, :] * _tap(sh, rs.start, CONV_K - 1 - j)
            sg = _sig(ag_ref[rs, :])
            out_ref[rs, 0:D_CONV] = (du0 * sg).astype(BF16)
            out_ref[rs, D_CONV:] = (du0 * av_ref[rs, :] * sg * (1.0 - sg)).astype(BF16)

        _for_chunks(chunk)

    return pl.pallas_call(
        body, out_shape=jax.ShapeDtypeStruct((SEQ, 2 * D_CONV), BF16), grid=(SEQ // ROWS,),
        in_specs=[_row_spec(D_CONV, 0), _row_spec(D_CONV, 1), _row_spec(D_CONV, 0), _next_halo(CONV_HALO, D_CONV, 0),
                  _vec_spec(D_CONV, CONV_K)],
        out_specs=_row_spec(2 * D_CONV), scratch_shapes=[pltpu.VMEM((8, CONV_PAD, D_CONV), F32)],
        name=name, compiler_params=_cp("parallel"))(proj, proj, du1, du1, wc)


def _rows(start, size, r):
    return pl.ds(start, size) if r == 1 else pl.ds(start, size, stride=r)


def _unit_rows(r, rho, n, nb):
    win = 2 * ATTN_BLOCK if nb > 1 else ATTN_BLOCK
    if isinstance(n, int):
        kb = max(n - 1, 0)
        q_rows = _rows(rho + r * ATTN_BLOCK * n, ATTN_BLOCK, r)
        k_rows = _rows(rho + r * ATTN_BLOCK * kb, win, r)
    else:
        kb = jnp.maximum(n - 1, 0)
        q_rows = pl.ds(pl.multiple_of(n * ATTN_BLOCK, ATTN_BLOCK), ATTN_BLOCK)
        k_rows = pl.ds(pl.multiple_of(kb * ATTN_BLOCK, ATTN_BLOCK), win)
    return q_rows, k_rows, n - kb


def _band_bias(first_block, transposed):
    shape = (2 * ATTN_BLOCK, ATTN_BLOCK) if transposed else (ATTN_BLOCK, 2 * ATTN_BLOCK)
    q_axis = 1 if transposed else 0
    dist = (0 if first_block else ATTN_BLOCK) + lax.broadcasted_iota(jnp.int32, shape, q_axis) \
        - lax.broadcasted_iota(jnp.int32, shape, 1 - q_axis)
    return jnp.where((dist >= 0) & (dist <= ATTN_BLOCK), 0.0, NEG)


def _per_head(x):
    lane = lax.broadcasted_iota(jnp.int32, x.shape, 1)
    zero = jnp.zeros_like(x)
    return [jnp.where(lane < HEAD_DIM, x, zero), jnp.where(lane >= HEAD_DIM, x, zero)]


SCALE = HEAD_DIM ** -0.5


def _masked_scores(q2, k2, bias):
    return [lax.dot_general(qh, k2, NT, preferred_element_type=F32) + bias for qh in _per_head(q2)]


def _attn_units(r, nb, unit):
    if r == 1:
        def four(i, carry):
            for k in range(4):
                unit(0, 4 * i + k)
            return carry
        lax.fori_loop(0, nb // 4, four, 0)
    else:
        for rho in range(r):
            for n in range(nb):
                unit(rho, n)


N_UNITS = 16


def attn_fwd_all(proj, name):
    def body(q_ref, k_ref, v_ref, att_ref, lse_ref, s_scr, p_scr, lse_scr, den_scr, bias_scr):
        bias_scr[0] = _band_bias(True, False)
        bias_scr[1] = _band_bias(False, False)
        for idx, (sub_len, r) in enumerate(PATTERNS):
            nb = sub_len // ATTN_BLOCK
            win = 2 * ATTN_BLOCK if nb > 1 else ATTN_BLOCK

            def scores(rho, n, r=r, nb=nb, win=win):
                u = rho * nb + n
                q_rows, k_rows, variant = _unit_rows(r, rho, n, nb)
                ss = _masked_scores((q_ref[q_rows, :] * SCALE).astype(BF16), k_ref[k_rows, :].astype(BF16),
                                    bias_scr[variant, :, 0:win])
                for h in range(2):
                    s_scr[2 * u + h, :, 0:win] = ss[h]

            _attn_units(r, nb, scores)

            def softmax(u, carry, win=win):
                lses, dens = [], []
                for h in range(2):
                    sc = s_scr[2 * u + h, :, 0:win]
                    m = jnp.max(sc, axis=1, keepdims=True)
                    p = jnp.exp(sc - m)
                    den = jnp.sum(p, axis=1, keepdims=True)
                    p_scr[2 * u + h, :, 0:win] = p.astype(BF16)
                    lses.append(jnp.broadcast_to(m + jnp.log(den), (ATTN_BLOCK, HEAD_DIM)))
                    dens.append(jnp.broadcast_to(den, (ATTN_BLOCK, HEAD_DIM)))
                lse_scr[u] = jnp.concatenate(lses, axis=1)
                den_scr[u] = jnp.concatenate(dens, axis=1)
                return carry

            lax.fori_loop(0, N_UNITS, softmax, 0, unroll=2)

            def outputs(rho, n, r=r, nb=nb, win=win, idx=idx):
                u = rho * nb + n
                q_rows, k_rows, _ = _unit_rows(r, rho, n, nb)
                vs = _per_head(v_ref[k_rows, :].astype(BF16))
                o = (jnp.dot(p_scr[2 * u, :, 0:win], vs[0], preferred_element_type=F32)
                     + jnp.dot(p_scr[2 * u + 1, :, 0:win], vs[1], preferred_element_type=F32)) / den_scr[u]
                lse = lse_scr[u]
                if idx > 0:
                    old = lse_ref[q_rows, :]
                    top = jnp.maximum(old, lse)
                    new = top + jnp.log(jnp.exp(old - top) + jnp.exp(lse - top))
                    o = att_ref[q_rows, :] * jnp.exp(old - new) + o * jnp.exp(lse - new)
                    lse = new
                att_ref[q_rows, :] = o
                lse_ref[q_rows, :] = lse

            _attn_units(r, nb, outputs)

    blk = lambda first: pl.BlockSpec((SEQ, 128), lambda g: (0, first + g))
    shp = jax.ShapeDtypeStruct((SEQ, D_ATTN), F32)
    big = (2 * N_UNITS, ATTN_BLOCK, 2 * ATTN_BLOCK)
    small = pltpu.VMEM((N_UNITS, ATTN_BLOCK, 128), F32)
    return pl.pallas_call(
        body, out_shape=(shp, shp), grid=(4,), in_specs=[blk(8), blk(12), blk(16)], out_specs=(blk(0), blk(0)),
        scratch_shapes=[pltpu.VMEM(big, F32), pltpu.VMEM(big, BF16), small, small,
                        pltpu.VMEM((2, ATTN_BLOCK, 2 * ATTN_BLOCK), F32)],
        name=name, compiler_params=_cp("parallel"))(proj, proj, proj)


def attn_bwd_all(proj, do, lse, dd, name):
    def body(q_ref, k_ref, v_ref, do_ref, l_ref, dd_ref, out_ref, dq_s, dk_s, dv_s,
             s_scr, dp_scr, ds_scr, st_scr, dpt_scr, pt_scr, dst_scr, qb_scr, kb_scr, dob_scr, bias_scr, bias_t_scr):
        for first_block in (True, False):
            bias_scr[1 - int(first_block)] = _band_bias(first_block, False)
            bias_t_scr[1 - int(first_block)] = _band_bias(first_block, True)
        dq_s[...] = jnp.zeros_like(dq_s)
        dk_s[...] = jnp.zeros_like(dk_s)
        dv_s[...] = jnp.zeros_like(dv_s)
        for sub_len, r in PATTERNS:
            nb = sub_len // ATTN_BLOCK
            win = 2 * ATTN_BLOCK if nb > 1 else ATTN_BLOCK

            def scores(rho, n, r=r, nb=nb, win=win):
                u = rho * nb + n
                q_rows, k_rows, variant = _unit_rows(r, rho, n, nb)
                bias, bias_t = bias_scr[variant, :, 0:win], bias_t_scr[variant, 0:win, :]
                q2 = (q_ref[q_rows, :] * SCALE).astype(BF16)
                kf = k_ref[k_rows, :]
                k2 = kf.astype(BF16)
                do2 = do_ref[q_rows, :].astype(BF16)
                qb_scr[u] = q2
                kb_scr[u, 0:win, :] = (kf * SCALE).astype(BF16)
                dob_scr[u] = do2
                l2 = l_ref[q_rows, :]
                d2 = dd_ref[q_rows, :]
                l2t = l2.T
                d2t = d2.T
                v2 = v_ref[k_rows, :].astype(BF16)
                qs, dos = _per_head(q2), _per_head(do2)
                for h in range(2):
                    c0 = h * HEAD_DIM
                    sc = lax.dot_general(qs[h], k2, NT, preferred_element_type=F32)
                    s_scr[2 * u + h, :, 0:win] = sc + bias - l2[:, c0:c0 + 1]
                    dp_scr[2 * u + h, :, 0:win] = lax.dot_general(dos[h], v2, NT, preferred_element_type=F32) \
                        - d2[:, c0:c0 + 1]
                    sct = lax.dot_general(k2, qs[h], NT, preferred_element_type=F32)
                    st_scr[2 * u + h, 0:win, :] = sct + bias_t - l2t[c0:c0 + 1, :]
                    dpt_scr[2 * u + h, 0:win, :] = lax.dot_general(v2, dos[h], NT, preferred_element_type=F32) \
                        - d2t[c0:c0 + 1, :]

            _attn_units(r, nb, scores)

            def pointwise(hu, carry, win=win):
                ds_scr[hu, :, 0:win] = (jnp.exp(s_scr[hu, :, 0:win]) * dp_scr[hu, :, 0:win]).astype(BF16)
                pt = jnp.exp(st_scr[hu, 0:win, :])
                pt_scr[hu, 0:win, :] = pt.astype(BF16)
                dst_scr[hu, 0:win, :] = (pt * dpt_scr[hu, 0:win, :]).astype(BF16)
                return carry

            lax.fori_loop(0, 2 * N_UNITS, pointwise, 0, unroll=4)

            def grads(rho, n, r=r, nb=nb, win=win):
                u = rho * nb + n
                q_rows, k_rows, _ = _unit_rows(r, rho, n, nb)
                qs, ks, dos = _per_head(qb_scr[u]), _per_head(kb_scr[u, 0:win, :]), _per_head(dob_scr[u])

                def both(scr, rows, rhs):
                    return (jnp.dot(scr[(2 * u,) + rows], rhs[0], preferred_element_type=F32)
                            + jnp.dot(scr[(2 * u + 1,) + rows], rhs[1], preferred_element_type=F32))

                dq_s[q_rows, :] += both(ds_scr, (slice(None), slice(0, win)), ks)
                dk_s[k_rows, :] += both(dst_scr, (slice(0, win), slice(None)), qs)
                dv_s[k_rows, :] += both(pt_scr, (slice(0, win), slice(None)), dos)

            _attn_units(r, nb, grads)
        out_ref[0] = dq_s[...].astype(BF16)
        out_ref[1] = dk_s[...].astype(BF16)
        out_ref[2] = dv_s[...].astype(BF16)

    blk = lambda first: pl.BlockSpec((SEQ, 128), lambda g: (0, first + g))
    acc = pltpu.VMEM((SEQ, 128), F32)
    big = (2 * N_UNITS, ATTN_BLOCK, 2 * ATTN_BLOCK)
    big_t = (2 * N_UNITS, 2 * ATTN_BLOCK, ATTN_BLOCK)
    return pl.pallas_call(
        body, out_shape=jax.ShapeDtypeStruct((3, SEQ, D_ATTN), BF16), grid=(4,),
        in_specs=[blk(8), blk(12), blk(16), blk(0), blk(0), blk(0)],
        out_specs=pl.BlockSpec((3, SEQ, 128), lambda g: (0, 0, g)),
        scratch_shapes=[acc, acc, acc, pltpu.VMEM(big, F32), pltpu.VMEM(big, F32), pltpu.VMEM(big, BF16),
                        pltpu.VMEM(big_t, F32), pltpu.VMEM(big_t, F32), pltpu.VMEM(big_t, BF16), pltpu.VMEM(big_t, BF16),
                        pltpu.VMEM((N_UNITS, ATTN_BLOCK, 128), BF16),
                        pltpu.VMEM((N_UNITS, 2 * ATTN_BLOCK, 128), BF16), pltpu.VMEM((N_UNITS, ATTN_BLOCK, 128), BF16),
                        pltpu.VMEM((2, ATTN_BLOCK, 2 * ATTN_BLOCK), F32), pltpu.VMEM((2, 2 * ATTN_BLOCK, ATTN_BLOCK), F32)],
        name=name, compiler_params=_cp("parallel"))(proj, proj, proj, do, lse, dd)


N_FT = D_FF // FFN_TN


def _ffn_specs():
    per = ROWS // FFN_HALO
    cur_g = pl.BlockSpec((ROWS, FFN_TN), lambda j, i: (i, j))
    cur_v = pl.BlockSpec((ROWS, FFN_TN), lambda j, i: (i, j + N_FT))
    halo_g = pl.BlockSpec((FFN_HALO, FFN_TN), lambda j, i: (jnp.maximum(i * per - 1, 0), j))
    halo_v = pl.BlockSpec((FFN_HALO, FFN_TN), lambda j, i: (jnp.maximum(i * per - 1, 0), j + N_FT))
    w_g = pl.BlockSpec((FFN_K, FFN_TN), lambda j, i: (0, j))
    w_v = pl.BlockSpec((FFN_K, FFN_TN), lambda j, i: (0, j + N_FT))
    b_g = pl.BlockSpec((1, FFN_TN), lambda j, i: (0, j))
    b_v = pl.BlockSpec((1, FFN_TN), lambda j, i: (0, j + N_FT))
    return [cur_g, cur_v, halo_g, halo_v, w_g, w_v, b_g, b_v]


def matmul(a, b, kind, out_dtype, tm, tn, name, b_rows=None):
    stacked = b_rows is not None
    b_shape = (N_DEV * b_rows, b.shape[2]) if stacked else b.shape
    if kind == "nn":
        (m, k), n = a.shape, b_shape[1]
        a_spec = pl.BlockSpec((tm, k), lambda j, i: (i, 0))
        b_spec = pl.BlockSpec((k, tn), lambda j, i: (0, j))
        dims = (((1,), (0,)), ((), ()))
    elif kind == "nt":
        (m, k), n = a.shape, b_shape[0]
        a_spec = pl.BlockSpec((tm, k), lambda j, i: (i, 0))
        b_spec = pl.BlockSpec((tn, k), lambda j, i: (j, 0))
        dims = (((1,), (1,)), ((), ()))
    else:
        (k, m), n = a.shape, b_shape[1]
        a_spec = pl.BlockSpec((k, tm), lambda j, i: (0, i))
        b_spec = pl.BlockSpec((k, tn), lambda j, i: (0, j))
        dims = (((0,), (0,)), ((), ()))
    assert m % tm == 0 and n % tn == 0, (name, m, n, tm, tn)
    if stacked:
        assert b_spec.block_shape[0] == b_shape[0] and kind in ("nn", "nt")
        width = b_spec.block_shape[1]
        b_spec = pl.BlockSpec((N_DEV, b_rows, width), (lambda j, i: (0, 0, j)) if kind == "nn" else (lambda j, i: (0, 0, 0)))

    def body(a_ref, b_ref, o_ref):
        bb = b_ref[...].reshape(b_shape[0], -1) if stacked else b_ref[...]
        o_ref[...] = lax.dot_general(a_ref[...], bb, dims, preferred_element_type=F32).astype(o_ref.dtype)

    return pl.pallas_call(
        body, out_shape=jax.ShapeDtypeStruct((m, n), out_dtype), grid=(n // tn, m // tm),
        in_specs=[a_spec, b_spec], out_specs=pl.BlockSpec((tm, tn), lambda j, i: (i, j)),
        name=name, compiler_params=_cp("parallel", "parallel"))(a, b)


def matmul_tn_pieces(a1, a3, b, name):
    k, n = b.shape
    tm = a3.shape[2]
    n1 = a1.shape[1] // tm

    def body(a1_ref, a3_ref, b_ref, o_ref):
        i = pl.program_id(0)
        tn_dims = (((0,), (0,)), ((), ()))

        @pl.when(i < n1)
        def _():
            o_ref[...] = lax.dot_general(a1_ref[...], b_ref[...], tn_dims, preferred_element_type=F32).astype(BF16)

        @pl.when(i >= n1)
        def _():
            o_ref[...] = lax.dot_general(a3_ref[0], b_ref[...], tn_dims, preferred_element_type=F32).astype(BF16)

    return pl.pallas_call(
        body, out_shape=jax.ShapeDtypeStruct((a1.shape[1] + 3 * tm, n), BF16), grid=(n1 + 3,),
        in_specs=[pl.BlockSpec((k, tm), lambda i: (0, jnp.minimum(i, n1 - 1))),
                  pl.BlockSpec((1, k, tm), lambda i: (jnp.maximum(i - n1, 0), 0, 0)),
                  pl.BlockSpec((k, n), lambda i: (0, 0))],
        out_specs=pl.BlockSpec((tm, n), lambda i: (i, 0)), name=name, compiler_params=_cp("parallel"))(a1, a3, b)


def matmul_tn_halves(a, b, tm, name):
    _, k, m = a.shape
    n = b.shape[1]

    def body(a_ref, b_ref, o_ref):
        o_ref[0] = lax.dot_general(a_ref[0], b_ref[...], (((0,), (0,)), ((), ())), preferred_element_type=F32).astype(BF16)

    return pl.pallas_call(
        body, out_shape=jax.ShapeDtypeStruct((2, m, n), BF16), grid=(2, m // tm),
        in_specs=[pl.BlockSpec((1, k, tm), lambda h, i: (h, 0, i)), pl.BlockSpec((k, n), lambda h, i: (0, 0))],
        out_specs=pl.BlockSpec((1, tm, n), lambda h, i: (h, i, 0)), name=name,
        compiler_params=_cp("parallel", "parallel"))(a, b).reshape(2 * m, n)


def _row_spec(width, col=0):
    return pl.BlockSpec((ROWS, width), lambda i: (i, col))


def _vec_spec(width, rows=1):
    return pl.BlockSpec((rows, width), lambda i: (0, 0))


FFN_RB = 64


def _ffn_lane_blocks(fn):
    for c in range(FFN_TN // 128):
        fn(slice(c * 128, (c + 1) * 128))


def _ffn_taps(cur_ref, halo_ref, head, ls, r0):
    if r0 == 0:
        head[0:FFN_HALO, :] = jnp.where(pl.program_id(1) > 0, halo_ref[:, ls], 0.0)
        head[FFN_HALO:, :] = cur_ref[0:FFN_RB, ls]
        return tuple(head[pl.ds(FFN_HALO - k, FFN_RB), :] for k in (2, 1, 0))
    return tuple(cur_ref[pl.ds(r0 - k, FFN_RB), ls] for k in (2, 1, 0))


def _ffn_conv(taps, w, b):
    return b + w[0] * taps[0] + w[1] * taps[1] + w[2] * taps[2]


def ffn_act_fwd(up0, wf, bf, name):
    def body(g_ref, v_ref, gh_ref, vh_ref, wg_ref, wv_ref, bg_ref, bv_ref, act_ref, head_g, head_v):
        def lanes(ls):
            wg = [wg_ref[t:t + 1, ls] for t in range(FFN_K)]
            wv = [wv_ref[t:t + 1, ls] for t in range(FFN_K)]
            for r0 in range(0, ROWS, FFN_RB):
                gate = _ffn_conv(_ffn_taps(g_ref, gh_ref, head_g, ls, r0), wg, bg_ref[:, ls])
                val = _ffn_conv(_ffn_taps(v_ref, vh_ref, head_v, ls, r0), wv, bv_ref[:, ls])
                act_ref[r0:r0 + FFN_RB, ls] = (gate * _sig(gate) * val).astype(BF16)

        _ffn_lane_blocks(lanes)

    head = pltpu.VMEM((FFN_HALO + FFN_RB, 128), F32)
    return pl.pallas_call(
        body, out_shape=jax.ShapeDtypeStruct((SEQ, D_FF), BF16), grid=(N_FT, SEQ // ROWS),
        in_specs=_ffn_specs(), out_specs=pl.BlockSpec((ROWS, FFN_TN), lambda j, i: (i, j)),
        scratch_shapes=[head, head], name=name, compiler_params=_cp("parallel", "parallel"))(up0, up0, up0, up0, wf, wf, bf, bf)


def ffn_bwd(up0, dact, wf, bf, name, after=None):
    per = ROWS // FFN_HALO
    last = SEQ // FFN_HALO - 1

    def body(g_ref, v_ref, gh_ref, vh_ref, wg_ref, wv_ref, bg_ref, bv_ref, da_ref, gn_ref, vn_ref, dan_ref,
             out_ref, dbg_ref, dbv_ref, dwg_ref, dwv_ref, pad, head_g, head_v, dgp, dvp, acc):
        i = pl.program_id(1)
        first = i == 0
        acc[...] = jnp.zeros_like(acc)

        def grads(gate, val, da):
            s = _sig(gate)
            return da * val * (s * (1.0 + gate * (1.0 - s))), da * (gate * s)

        fold = lambda q: jnp.sum(q.reshape(FFN_RB // 8, 8, 128), axis=0)

        def lanes(ls):
            wg = [wg_ref[t:t + 1, ls] for t in range(FFN_K)]
            wv = [wv_ref[t:t + 1, ls] for t in range(FFN_K)]
            sums = [jnp.zeros((8, 128), F32)] * (2 + 2 * FFN_K)
            for r0 in range(0, ROWS, FFN_RB):
                gs = _ffn_taps(g_ref, gh_ref, head_g, ls, r0)
                vs = _ffn_taps(v_ref, vh_ref, head_v, ls, r0)
                dgate, dval = grads(_ffn_conv(gs, wg, bg_ref[:, ls]), _ffn_conv(vs, wv, bv_ref[:, ls]),
                                    da_ref[r0:r0 + FFN_RB, ls])
                dgp[r0:r0 + FFN_RB, ls] = dgate
                dvp[r0:r0 + FFN_RB, ls] = dval
                new = [dgate, dval] + [dgate * gs[t] for t in range(FFN_K)] + [dval * vs[t] for t in range(FFN_K)]
                sums = [a + fold(q) for a, q in zip(sums, new)]
            for k in range(2 + 2 * FFN_K):
                acc[k, 0:8, ls] = sums[k]

        _ffn_lane_blocks(lanes)
        _acc(dbg_ref, _rsum(acc[0]), first)
        _acc(dbv_ref, _rsum(acc[1]), first)
        _acc(dwg_ref, jnp.concatenate([_rsum(acc[2 + t]) for t in range(FFN_K)], axis=0), first)
        _acc(dwv_ref, jnp.concatenate([_rsum(acc[5 + t]) for t in range(FFN_K)], axis=0), first)

        def conv_next(cur_ref, nxt_ref, w_ref, b_ref):
            pad[0:FFN_HALO, :] = cur_ref[ROWS - FFN_HALO:, :]
            pad[FFN_HALO:, :] = nxt_ref[...]
            return (b_ref[...] + w_ref[0:1, :] * pad[pl.ds(FFN_HALO - 2, FFN_HALO), :]
                    + w_ref[1:2, :] * pad[pl.ds(FFN_HALO - 1, FFN_HALO), :] + w_ref[2:3, :] * nxt_ref[...])

        gate_n = conv_next(g_ref, gn_ref, wg_ref, bg_ref)
        val_n = conv_next(v_ref, vn_ref, wv_ref, bv_ref)
        dgate_n, dval_n = grads(gate_n, val_n, dan_ref[...])
        inside = i < SEQ // ROWS - 1
        dgp[ROWS:, :] = jnp.where(inside, dgate_n, 0.0)
        dvp[ROWS:, :] = jnp.where(inside, dval_n, 0.0)

        def back(ls):
            for half, (dp, w_ref) in enumerate(((dgp, wg_ref), (dvp, wv_ref))):
                w = [w_ref[t:t + 1, ls] for t in range(FFN_K)]
                for r0 in range(0, ROWS, FFN_RB):
                    out_ref[half, r0:r0 + FFN_RB, ls] = (
                        w[2] * dp[r0:r0 + FFN_RB, ls] + w[1] * dp[pl.ds(r0 + 1, FFN_RB), ls]
                        + w[0] * dp[pl.ds(r0 + 2, FFN_RB), ls]).astype(BF16)

        _ffn_lane_blocks(back)

    body, more_specs, more = _with_after(body, 12, after)
    head = pltpu.VMEM((FFN_HALO + FFN_RB, 128), F32)
    ext = pltpu.VMEM((ROWS + FFN_HALO, FFN_TN), F32)
    vec = jax.ShapeDtypeStruct((1, D_FF), F32)
    taps = jax.ShapeDtypeStruct((FFN_K, D_FF), F32)
    cur = pl.BlockSpec((ROWS, FFN_TN), lambda j, i: (i, j))
    nxt = lambda off: pl.BlockSpec((FFN_HALO, FFN_TN), lambda j, i: (jnp.minimum((i + 1) * per, last), j + off))
    vs = pl.BlockSpec((1, FFN_TN), lambda j, i: (0, j))
    ts = pl.BlockSpec((FFN_K, FFN_TN), lambda j, i: (0, j))
    return pl.pallas_call(
        body, out_shape=(jax.ShapeDtypeStruct((2, SEQ, D_FF), BF16), vec, vec, taps, taps), grid=(N_FT, SEQ // ROWS),
        in_specs=_ffn_specs() + [cur, nxt(0), nxt(N_FT), nxt(0)] + more_specs,
        out_specs=(pl.BlockSpec((2, ROWS, FFN_TN), lambda j, i: (0, i, j)), vs, vs, ts, ts),
        scratch_shapes=[pltpu.VMEM((2 * FFN_HALO, FFN_TN), F32), head, head, ext, ext,
                        pltpu.VMEM((2 + 2 * FFN_K, SUB, FFN_TN), F32)],
        name=name, compiler_params=_cp("parallel", "arbitrary"))(up0, up0, up0, up0, wf, wf, bf, bf, dact, up0, up0, dact,
                                                                 *more)


def ada_fwd(c_all, w_ada, b_cols, name):
    def body(c_ref, w_ref, b_ref, o_ref):
        cc = c_ref[...]
        sc = (cc * _sig(cc)).astype(BF16)
        o_ref[...] = jnp.dot(sc, w_ref[...].astype(BF16), preferred_element_type=F32) + b_ref[...]

    return pl.pallas_call(body, out_shape=jax.ShapeDtypeStruct((N_DEV, w_ada.shape[1]), F32), name=name,
                          compiler_params=_cp())(c_all, w_ada, b_cols)


def _adam(w, g, m, v):
    m = ADAM_B1 * m + (1.0 - ADAM_B1) * g
    v = ADAM_B2 * v + (1.0 - ADAM_B2) * (g * g)
    m_hat = m / (1.0 - ADAM_B1 ** ADAM_STEP)
    v_hat = v / (1.0 - ADAM_B2 ** ADAM_STEP)
    delta = -ADAM_LR * (m_hat / (jnp.sqrt(v_hat) + ADAM_EPS) + ADAM_WD * w)
    return delta, m, v


def ada_bwd_adamw(c_all, dmod_cols, w, m, v, name):
    rows, cols = w.shape
    tr = 256

    def body(c_ref, dm_ref, w_ref, m_ref, v_ref, g_ref, d_ref, nm_ref, nv_ref):
        cc = c_ref[...]
        sc = (cc * _sig(cc)).T
        g = sc[:, 0:1] * dm_ref[0:1, :]
        for b in range(1, N_DEV):
            g = g + sc[:, b:b + 1] * dm_ref[b:b + 1, :]
        g_ref[...] = g
        d_ref[...], nm_ref[...], nv_ref[...] = _adam(w_ref[...], g, m_ref[...], v_ref[...])

    blk = pl.BlockSpec((tr, cols), lambda i: (i, 0))
    shp = jax.ShapeDtypeStruct((rows, cols), F32)
    return pl.pallas_call(
        body, out_shape=(shp, shp, shp, shp), grid=(rows // tr,),
        in_specs=[pl.BlockSpec((N_DEV, tr), lambda i: (0, i)), pl.BlockSpec((N_DEV, cols), lambda i: (0, 0)), blk, blk, blk],
        out_specs=(blk, blk, blk, blk), name=name, compiler_params=_cp("parallel"))(c_all, dmod_cols, w, m, v)


def sum_adamw(parts, mine, me, w, m, v, tr, name):
    n_parts, rows, cols = parts.shape

    def body(me_ref, p_ref, own_ref, w_ref, m_ref, v_ref, g_ref, d_ref, nm_ref, nv_ref):
        def chunk(rs):
            g = own_ref[0, rs, :].astype(F32)
            for k in range(1, n_parts):
                g = g + p_ref[k, rs, :].astype(F32)
            g_ref[rs, :] = g
            d_ref[rs, :], nm_ref[rs, :], nv_ref[rs, :] = _adam(w_ref[rs, :], g, m_ref[rs, :], v_ref[rs, :])

        _for_chunks(chunk, 2, tr)

    blk = pl.BlockSpec((tr, cols), lambda i, me_ref: (i, 0))
    shp = jax.ShapeDtypeStruct((rows, cols), F32)
    grid_spec = pltpu.PrefetchScalarGridSpec(
        num_scalar_prefetch=1, grid=(rows // tr,),
        in_specs=[pl.BlockSpec((n_parts, tr, cols), lambda i, me_ref: (0, i, 0)),
                  pl.BlockSpec((1, tr, cols), lambda i, me_ref: (me_ref[0], i, 0)), blk, blk, blk],
        out_specs=(blk, blk, blk, blk))
    return pl.pallas_call(body, out_shape=(shp, shp, shp, shp), grid_spec=grid_spec, name=name,
                          compiler_params=_cp("parallel"))(me, parts, mine, w, m, v)


MESH = pl.DeviceIdType.MESH


def all_gather(block, name, after=None):
    extra = () if after is None else (after,)

    def body(x_ref, *refs):
        out_ref, send_sems, recv_sems, local_sem = refs[len(extra):]
        x, y, c = lax.axis_index("x"), lax.axis_index("y"), lax.axis_index("c")
        me, sibling = (x, y, c), (x, y, 1 - c)
        chips = [(1 - x, y), (x, 1 - y), (1 - x, 1 - y)]

        def slot(px, py, pc):
            return out_ref.at[4 * px + 2 * py + pc]

        def copy(k, blk, to, src=None):
            return pltpu.make_async_remote_copy(
                src_ref=slot(*blk) if src is None else src, dst_ref=slot(*blk),
                send_sem=send_sems.at[k], recv_sem=recv_sems.at[k], device_id=to, device_id_type=MESH)

        mine = pltpu.make_async_copy(x_ref, slot(*me), local_sem)
        mine.start()
        first = [copy(0, me, sibling, src=x_ref)]
        first += [copy(1 + j, me, (*chip, c), src=x_ref) for j, chip in enumerate(chips)]
        for cp in first:
            cp.start()
        passed = [copy(4 + j, (*chip, c), sibling) for j, chip in enumerate(chips)]
        for j, chip in enumerate(chips):
            copy(1 + j, (*chip, c), me).wait_recv()
            passed[j].start()
        copy(0, sibling, me).wait_recv()
        for j, chip in enumerate(chips):
            copy(4 + j, (*chip, 1 - c), me).wait_recv()
        for cp in first + passed:
            cp.wait_send()
        mine.wait()

    return pl.pallas_call(
        body, out_shape=jax.ShapeDtypeStruct((N_DEV,) + block.shape, block.dtype), in_specs=[ANY] * (1 + len(extra)), out_specs=ANY,
        scratch_shapes=[pltpu.SemaphoreType.DMA((7,)), pltpu.SemaphoreType.DMA((7,)), pltpu.SemaphoreType.DMA],
        name=name)(block, *extra)


HBM = pl.BlockSpec(memory_space=pltpu.HBM)
SEM = pl.BlockSpec(memory_space=pltpu.SEMAPHORE)
EFFECT = pltpu.SideEffectType.DATAFLOW_SIDE_EFFECTING


def _peer_copies(src_ref, land_ref, send_sems, recv_sems, gather):
    x, y, c = lax.axis_index("x"), lax.axis_index("y"), lax.axis_index("c")
    me = 4 * x + 2 * y + c
    copies = []
    for k in range(1, N_DEV):
        px = 1 - x if k & 4 else x
        py = 1 - y if k & 2 else y
        pc = 1 - c if k & 1 else c
        copies.append(pltpu.make_async_remote_copy(
            src_ref=src_ref if gather else src_ref.at[4 * px + 2 * py + pc],
            dst_ref=land_ref.at[me] if gather else land_ref.at[k],
            send_sem=send_sems.at[k - 1], recv_sem=recv_sems.at[k - 1], device_id=(px, py, pc), device_id_type=MESH))
    return copies


def exchange_start(srcs, gather, name, after=None):
    n = len(srcs)
    land_shapes = [(N_DEV,) + src.shape if gather else src.shape for src in srcs]
    extra = () if after is None else (after,)

    def body(*refs):
        src_refs, land_refs = refs[0:n], refs[n:2 * n]
        outs = refs[2 * n + len(extra):]
        for k in range(n):
            for cp in _peer_copies(src_refs[k], land_refs[k], outs[4 * k], outs[4 * k + 1], gather):
                cp.start()
        token = outs[4 * n]
        token[...] = jnp.zeros_like(token)

    out_shape, out_specs, aliases = [], [], {}
    for k, src in enumerate(srcs):
        out_shape += [pltpu.SemaphoreType.DMA((N_DEV - 1,)), pltpu.SemaphoreType.DMA((N_DEV - 1,)),
                      pltpu.HBM(src.shape, src.dtype), pltpu.HBM(land_shapes[k], src.dtype)]
        out_specs += [SEM, SEM, HBM, HBM]
        aliases[k] = 4 * k + 2
        aliases[n + k] = 4 * k + 3
    out_shape.append(jax.ShapeDtypeStruct((8, 128), F32))
    out_specs.append(pl.BlockSpec(memory_space=pltpu.VMEM))
    res = pl.pallas_call(
        body, name=name, out_shape=tuple(out_shape), in_specs=(HBM,) * (2 * n) + (ANY,) * len(extra),
        out_specs=tuple(out_specs), input_output_aliases=aliases,
        compiler_params=pltpu.CompilerParams(has_side_effects=EFFECT),
    )(*[pltpu.with_memory_space_constraint(src, pltpu.HBM) for src in srcs],
      *[pltpu.with_memory_space_constraint(lax.empty(shp, src.dtype), pltpu.HBM) for shp, src in zip(land_shapes, srcs)],
      *extra)
    return [tuple(res[4 * k:4 * k + 4]) for k in range(n)], res[4 * n]


def _stage1_peer(i):
    x, y, c = lax.axis_index("x"), lax.axis_index("y"), lax.axis_index("c")
    if i == 0:
        return (x, y, 1 - c)
    return (1 - x if i & 1 else x, 1 - y if i & 2 else y, c)


def _slot_of(peer):
    return 4 * peer[0] + 2 * peer[1] + peer[2]


def _stage1_copy(i, src_ref, land_ref, send_sems, recv_sems):
    me = _slot_of((lax.axis_index("x"), lax.axis_index("y"), lax.axis_index("c")))
    return pltpu.make_async_remote_copy(src_ref=src_ref, dst_ref=land_ref.at[me], send_sem=send_sems.at[i],
                                        recv_sem=recv_sems.at[i], device_id=_stage1_peer(i), device_id_type=MESH)


def _stage2_copy(j, land_ref, send_sems, recv_sems):
    slot = _slot_of(_stage1_peer(j + 1))
    return pltpu.make_async_remote_copy(src_ref=land_ref.at[slot], dst_ref=land_ref.at[slot], send_sem=send_sems.at[j],
                                        recv_sem=recv_sems.at[j], device_id=_stage1_peer(0), device_id_type=MESH)


def gather2_start(srcs, name, after=None):
    n = len(srcs)
    extra = () if after is None else (after,)

    def body(*refs):
        src_refs, land_refs = refs[0:n], refs[n:2 * n]
        outs = refs[2 * n + len(extra):]
        for k in range(n):
            for i in range(4):
                _stage1_copy(i, src_refs[k], land_refs[k], outs[4 * k], outs[4 * k + 1]).start()
        outs[4 * n][...] = jnp.zeros((8, 128), F32)

    out_shape, out_specs, aliases = [], [], {}
    for k, src in enumerate(srcs):
        out_shape += [pltpu.SemaphoreType.DMA((4,)), pltpu.SemaphoreType.DMA((4,)),
                      pltpu.HBM(src.shape, src.dtype), pltpu.HBM((N_DEV,) + src.shape, src.dtype)]
        out_specs += [SEM, SEM, HBM, HBM]
        aliases[k] = 4 * k + 2
        aliases[n + k] = 4 * k + 3
    out_shape.append(jax.ShapeDtypeStruct((8, 128), F32))
    out_specs.append(pl.BlockSpec(memory_space=pltpu.VMEM))
    res = pl.pallas_call(
        body, name=name, out_shape=tuple(out_shape), in_specs=(HBM,) * (2 * n) + (ANY,) * len(extra),
        out_specs=tuple(out_specs), input_output_aliases=aliases,
        compiler_params=pltpu.CompilerParams(has_side_effects=EFFECT),
    )(*[pltpu.with_memory_space_constraint(src, pltpu.HBM) for src in srcs],
      *[pltpu.with_memory_space_constraint(lax.empty((N_DEV,) + src.shape, src.dtype), pltpu.HBM) for src in srcs], *extra)
    return [dict(send1=res[4 * k], recv1=res[4 * k + 1], src=res[4 * k + 2], land=res[4 * k + 3]) for k in range(n)], \
        res[4 * n]


def gather2_pass(handles, after, name):
    n = len(handles)

    def body(*refs):
        src_refs, land_refs, recv1 = refs[0:n], refs[n:2 * n], refs[2 * n:3 * n]
        outs = refs[3 * n + 1:]
        for k in range(n):
            for j in range(3):
                _stage1_copy(j + 1, src_refs[k], land_refs[k], recv1[k], recv1[k]).wait_recv()
                _stage2_copy(j, land_refs[k], outs[3 * k], outs[3 * k + 1]).start()

    out_shape, out_specs, aliases = [], [], {}
    for k, h in enumerate(handles):
        out_shape += [pltpu.SemaphoreType.DMA((3,)), pltpu.SemaphoreType.DMA((3,)), pltpu.HBM(h["land"].shape, h["land"].dtype)]
        out_specs += [SEM, SEM, HBM]
        aliases[n + k] = 3 * k + 2
    res = pl.pallas_call(
        body, name=name, out_shape=tuple(out_shape), in_specs=(HBM,) * (2 * n) + (SEM,) * n + (ANY,),
        out_specs=tuple(out_specs), input_output_aliases=aliases,
        compiler_params=pltpu.CompilerParams(has_side_effects=EFFECT),
    )(*[h["src"] for h in handles], *[h["land"] for h in handles], *[h["recv1"] for h in handles], after)
    return [dict(h, send2=res[3 * k], recv2=res[3 * k + 1], land=res[3 * k + 2]) for k, h in enumerate(handles)]


def gather2_wait(h, after, name):
    def body(src_ref, land_ref, send1, recv1, send2, recv2, after_ref, src_dead, got_ref):
        for i in range(4):
            _stage1_copy(i, src_ref, land_ref, send1, recv1).wait_send()
        _stage1_copy(0, src_ref, land_ref, send1, recv1).wait_recv()
        for j in range(3):
            cp = _stage2_copy(j, land_ref, send2, recv2)
            cp.wait_send()
            cp.wait_recv()

    return pl.pallas_call(
        body, name=name,
        out_shape=(pltpu.HBM(h["src"].shape, h["src"].dtype), pltpu.HBM(h["land"].shape, h["land"].dtype)),
        in_specs=(HBM, HBM, SEM, SEM, SEM, SEM, ANY), out_specs=(HBM, HBM), input_output_aliases={0: 0, 1: 1},
        compiler_params=pltpu.CompilerParams(has_side_effects=EFFECT),
    )(h["src"], h["land"], h["send1"], h["recv1"], h["send2"], h["recv2"], after)[1]


def exchange_wait(handles, after, gather, name):
    send_sems, recv_sems, src_thru, land_thru = handles

    def body(src_ref, land_ref, send_sems, recv_sems, after_ref, src_dead, got_ref):
        for cp in _peer_copies(src_ref, land_ref, send_sems, recv_sems, gather):
            cp.wait_send()
            cp.wait_recv()

    return pl.pallas_call(
        body, name=name,
        out_shape=(pltpu.HBM(src_thru.shape, src_thru.dtype), pltpu.HBM(land_thru.shape, land_thru.dtype)),
        in_specs=(HBM, HBM, SEM, SEM, ANY), out_specs=(HBM, HBM), input_output_aliases={0: 0, 1: 1},
        compiler_params=pltpu.CompilerParams(has_side_effects=EFFECT),
    )(src_thru, land_thru, send_sems, recv_sems, after)


def local_step(x, tgt, mod, started, get_w, put_grad, wc, wf, g_mix, bc, lg, lb, gco, gao, g_ffn, bf, g_fin):
    w_in = get_w("w_in", mod)
    proj, h1 = rms_mod_matmul(x, g_mix, mod, 0, 1, w_in, D_IN // N_DEV, "proj_fwd", after=started)
    mix_a, u1 = conv_module_fwd(proj, wc, bc, lg, lb, gco, "conv_module_fwd")
    att, lse = attn_fwd_all(proj, "attn_fwd")
    w_out = get_w("w_out", att)
    y1, mixed = norm_concat_matmul(mix_a, att, gao, w_out, "out_proj_fwd")
    w_up = get_w("w_up", y1)
    up0, x1, h2 = resid_rms_mod_matmul(x, y1, g_ffn, mod, 2, 3, 4, w_up, "up_fwd")
    act = ffn_act_fwd(up0, wf, bf, "ffn_act_fwd")
    w_down = get_w("w_down", act)
    loss_t, dx2, dy2, d_gfin, d_gaf = matmul_loss_bwd(act, w_down, x1, tgt, g_fin, mod, 5, "down_fwd_loss")
    dact = matmul(dy2, w_down, "nt", F32, 512, FFN_TN, "down_bwd_x")
    dw_down = matmul(act, dy2, "tn", BF16, 256, D_MODEL, "down_bwd_w")
    dup0, dbf_g, dbf_v, dwf_g, dwf_v = ffn_bwd(up0, dact, wf, bf, "ffn_bwd", after=put_grad("w_down", dw_down))
    dw_up = matmul_tn_halves(dup0, h2, 256, "up_bwd_w")
    dx1, d_shf, d_scf, d_gffn, dy1, d_gam = matmul_rms_mod_bwd(
        dup0, w_up, x1, dx2, g_ffn, mod, 4, y1, 2, "up_bwd_x", after=put_grad("w_up", dw_up))
    dw_out = matmul(mixed, dy1, "tn", BF16, 256, D_MODEL, "out_proj_bwd_w")
    dmixed, do, dd, d_gao = matmul_combine_bwd(dy1, w_out, att, gao, "out_proj_bwd_x", after=put_grad("w_out", dw_out))
    dqkv = attn_bwd_all(proj, do, lse, dd, "attn_bwd")
    du1, d_gco, d_lg, d_lb, d_bc, d_wc = conv_module_bwd_a(proj, u1, dmixed, lg, lb, gco, "conv_module_bwd_a")
    dproj_a = conv_module_bwd_b(proj, du1, wc, "conv_module_bwd_b")
    dw_in = matmul_tn_pieces(dproj_a, dqkv, h1, "proj_bwd_w")
    dx, d_shm, d_scm, d_gmix = matmul_rms_mod_bwd(
        (dproj_a, dqkv), w_in, x, dx1, g_mix, mod, 1, None, 0, "proj_bwd_x", b_rows=D_IN // N_DEV,
        after=put_grad("w_in", dw_in))
    dmod = jnp.concatenate([d_shm, d_scm, d_gam, d_shf, d_scf, d_gaf], axis=1)
    small = dict(g_norm_mix=d_gmix, b_conv_dw=d_bc, ln_conv_g=d_lg, ln_conv_b=d_lb, g_conv_out=d_gco, g_attn_out=d_gao,
                 g_norm_ffn=d_gffn, b_ffn_dw=jnp.concatenate([dbf_g, dbf_v], axis=1), g_final=d_gfin,
                 w_conv_dw=d_wc, w_ffn_dw=jnp.concatenate([dwf_g, dwf_v], axis=1), dmod=dmod, loss=loss_t[0:1, 0:1])
    return dx, small


PACK_W = 7168
TAPS_PER_ROW = PACK_W // D_CONV
PACKED_AT = dict(
    b_ada=(0, 0, N_MOD * D_MODEL), g_norm_mix=(0, 6144, D_MODEL),
    b_ffn_dw=(1, 0, 2 * D_FF), g_norm_ffn=(1, 5632, D_MODEL), b_conv_dw=(1, 6656, D_CONV),
    g_final=(2, 5632, D_MODEL), ln_conv_g=(2, 6656, D_CONV),
    ln_conv_b=(3, 5632, D_CONV), g_conv_out=(3, 6144, D_CONV), g_attn_out=(3, 6656, D_ATTN))
SMALL_ORDER = list(PACKED_AT)
LOSS_AT = (4, 2 * D_FF)


def pack_small(t):
    cat = lambda *parts: jnp.concatenate(parts, axis=1)
    wf = t["w_ffn_dw"]
    taps = jnp.pad(t["w_conv_dw"].reshape(1, CONV_K * D_CONV), ((0, 0), (0, 3 * PACK_W - CONV_K * D_CONV)))
    return jnp.concatenate([
        cat(t["dmod"], t["g_norm_mix"]),
        cat(t["b_ffn_dw"], t["g_norm_ffn"], t["b_conv_dw"]),
        cat(wf[0:1], t["g_final"], t["ln_conv_g"]),
        cat(wf[1:2], t["ln_conv_b"], t["g_conv_out"], t["g_attn_out"]),
        cat(wf[2:3], jnp.pad(t["loss"], ((0, 0), (0, PACK_W - 2 * D_FF - 1)))),
        taps.reshape(3, PACK_W)], axis=0)


def small_adamw(parts, wmv, name):
    def body(*refs):
        p_ref = refs[0]
        ins = refs[1:1 + 3 * len(SMALL_ORDER)]
        outs = refs[1 + 3 * len(SMALL_ORDER):]
        g = p_ref[0]
        for k in range(1, N_DEV):
            g = g + p_ref[k]
        for i, n in enumerate(SMALL_ORDER):
            row, lane, width = PACKED_AT[n]
            gp = g[row:row + 1, lane:lane + width]
            w_ref, m_ref, v_ref = ins[3 * i:3 * i + 3]
            g_ref, d_ref, nm_ref, nv_ref = outs[4 * i:4 * i + 4]
            g_ref[...] = gp
            d_ref[...], nm_ref[...], nv_ref[...] = _adam(w_ref[...], gp, m_ref[...], v_ref[...])
        wc_ref, wf_ref, loss_ref = outs[4 * len(SMALL_ORDER):]
        for j in range(CONV_K):
            row, lane = 5 + j // TAPS_PER_ROW, (j % TAPS_PER_ROW) * D_CONV
            wc_ref[j:j + 1, :] = g[row:row + 1, lane:lane + D_CONV]
        wf_ref[...] = g[2:2 + FFN_K, 0:2 * D_FF]
        loss_ref[...] = jnp.broadcast_to(g[LOSS_AT[0]:LOSS_AT[0] + 1, LOSS_AT[1]:LOSS_AT[1] + 1], (8, 128))

    args, out_shape = [parts], []
    for n in SMALL_ORDER:
        args += list(wmv[n])
        out_shape += [jax.ShapeDtypeStruct(wmv[n][0].shape, F32)] * 4
    out_shape += [jax.ShapeDtypeStruct((CONV_K, D_CONV), F32), jax.ShapeDtypeStruct((FFN_K, 2 * D_FF), F32),
                  jax.ShapeDtypeStruct((8, 128), F32)]
    res = pl.pallas_call(body, out_shape=tuple(out_shape), name=name, compiler_params=_cp())(*args)
    per = {n: tuple(res[4 * i:4 * i + 4]) for i, n in enumerate(SMALL_ORDER)}
    return per, res[-3], res[-2], res[-1][0, 0]


def shard_adamw(items, name):
    def body(*refs):
        ins, outs = refs[:4 * len(items)], refs[4 * len(items):]
        for i in range(len(items)):
            g_ref, w_ref, m_ref, v_ref = ins[4 * i:4 * i + 4]
            og_ref, d_ref, nm_ref, nv_ref = outs[4 * i:4 * i + 4]
            og_ref[...] = g_ref[...]
            d_ref[...], nm_ref[...], nv_ref[...] = _adam(w_ref[...], g_ref[...], m_ref[...], v_ref[...])

    args = [a for item in items for a in item]
    out_shape = tuple(jax.ShapeDtypeStruct(item[1].shape, F32) for item in items for _ in range(4))
    res = pl.pallas_call(body, out_shape=out_shape, name=name, compiler_params=_cp())(*args)
    return [tuple(res[4 * i:4 * i + 4]) for i in range(len(items))]


def _shard(full, n_cols, me):
    return lax.dynamic_slice(full, (0, me * n_cols), (full.shape[0], n_cols))


WEIGHTS = ["w_ada", "b_ada", "g_norm_mix", "w_in", "w_conv_dw", "b_conv_dw", "ln_conv_g", "ln_conv_b", "g_conv_out",
           "g_attn_out", "w_out", "g_norm_ffn", "w_up", "w_ffn_dw", "b_ffn_dw", "w_down", "g_final"]


def kernel(x, c, w_ada, b_ada, g_norm_mix, w_in, w_conv_dw, b_conv_dw, ln_conv_g, ln_conv_b, g_conv_out, g_attn_out, w_out, g_norm_ffn, w_up, w_ffn_dw, b_ffn_dw, w_down, g_final, loss_target, m_w_ada, m_b_ada, m_g_norm_mix, m_w_in, m_w_conv_dw, m_b_conv_dw, m_ln_conv_g, m_ln_conv_b, m_g_conv_out, m_g_attn_out, m_w_out, m_g_norm_ffn, m_w_up, m_w_ffn_dw, m_b_ffn_dw, m_w_down, m_g_final, v_w_ada, v_b_ada, v_g_norm_mix, v_w_in, v_w_conv_dw, v_b_conv_dw, v_ln_conv_g, v_ln_conv_b, v_g_conv_out, v_g_attn_out, v_w_out, v_g_norm_ffn, v_w_up, v_w_ffn_dw, v_b_ffn_dw, v_w_down, v_g_final):
    args = dict(locals())
    me = 4 * lax.axis_index("x") + 2 * lax.axis_index("y") + lax.axis_index("c")
    me1 = me.astype(jnp.int32).reshape(1)

    def flat(name, prefix=""):
        a = args[prefix + name]
        return a.reshape(a.shape[-2] if a.ndim > 1 else 1, a.shape[-1])

    def flat_t(name, prefix=""):
        return args[prefix + name][0].T

    n_in, n_up, r_out, r_down = w_in.shape[2], w_up.shape[2], w_out.shape[1], w_down.shape[1]
    n_ada, n_wc, n_wf = w_ada.shape[2], w_conv_dw.shape[2], w_ffn_dw.shape[2]
    taps_c = jnp.pad(flat("w_conv_dw").reshape(1, CONV_K * n_wc), ((0, 0), (0, 2 * D_MODEL - CONV_K * n_wc)))
    taps_f = jnp.pad(flat("w_ffn_dw").reshape(1, FFN_K * n_wf), ((0, 0), (0, 3 * D_MODEL - FFN_K * n_wf)))
    first = jnp.concatenate([c, taps_c.reshape(2, D_MODEL), taps_f.reshape(3, D_MODEL), jnp.zeros((2, D_MODEL), F32)], axis=0)
    w_in_block = flat_t("w_in").astype(BF16)
    hi = lax.reduce_precision(first, 8, 7)
    mid = lax.reduce_precision(first - hi, 8, 7)
    low = lax.reduce_precision(first - hi - mid, 8, 7)
    terms = jnp.concatenate([hi, mid, low, jnp.zeros((8, D_MODEL), F32)], axis=0).astype(BF16)
    first_block = all_gather(jnp.concatenate([w_in_block, terms], axis=0), "gather_c_taps_w_in")
    terms = first_block[:, n_in:n_in + 24, :].astype(F32)
    first_all = (terms[:, 0:8] + terms[:, 8:16]) + terms[:, 16:24]
    c_all = first_all[:, 0, :]
    wc_full = first_all[:, 1:3, :].reshape(N_DEV, 2 * D_MODEL)[:, :CONV_K * n_wc].reshape(N_DEV, CONV_K, n_wc)
    wc_full = wc_full.transpose(1, 0, 2).reshape(CONV_K, D_CONV)
    wf_full = first_all[:, 3:6, :].reshape(N_DEV, 3 * D_MODEL)[:, :FFN_K * n_wf].reshape(N_DEV, FFN_K, n_wf)
    wf_full = wf_full.transpose(1, 0, 2).reshape(FFN_K, 2 * D_FF)
    mod_cols = ada_fwd(c_all, flat("w_ada"), _shard(flat("b_ada"), n_ada, me), "ada_fwd")
    mod_all = all_gather(mod_cols, "gather_mod")
    mod = lax.dynamic_index_in_dim(mod_all, me, axis=1, keepdims=False).reshape(N_MOD, D_MODEL)
    mod = jnp.pad(mod, ((0, 2), (0, 0)))

    order = ("w_out", "w_up", "w_down")
    blocks = dict(w_up=flat_t("w_up").astype(BF16), w_out=flat("w_out").astype(BF16), w_down=flat("w_down").astype(BF16))
    handles, tok = gather2_start([blocks[name] for name in order], "gather_weights_start", mod_all)
    gathers = dict(zip(order, handles))

    def gathered(name, after):
        if "send2" not in gathers[name]:
            group = ("w_out", "w_up") if name != "w_down" else ("w_down",)
            gathers.update(zip(group, gather2_pass([gathers[w] for w in group], after, f"gather_{name}_pass")))
        land = gather2_wait(gathers[name], after, f"gather_{name}_wait")
        return lax.dynamic_update_index_in_dim(land, blocks[name], me, axis=0)

    def get_w(name, after):
        if name == "w_in":
            return first_block
        return gathered(name, after).reshape(-1, D_MODEL)

    exchanges = {}

    def put_grad(name, dw, after=None):
        dev_major = dw.reshape(N_DEV, -1, D_MODEL)
        (exchanges[name],), token = exchange_start([dev_major], False, f"exchange_{name}_start", after)
        return token

    grad_x, small = local_step(
        x[0], loss_target[0], mod, tok, get_w, put_grad, wc_full, wf_full,
        flat("g_norm_mix"), flat("b_conv_dw"), flat("ln_conv_g"), flat("ln_conv_b"), flat("g_conv_out"),
        flat("g_attn_out"), flat("g_norm_ffn"), flat("b_ffn_dw"), flat("g_final"))

    out = {}

    def finish(name, tr, after):
        mine, parts = exchange_wait(exchanges[name], after, False, f"exchange_{name}_wait")
        if name in ("w_in", "w_up"):
            res = sum_adamw(parts, mine, me1, flat_t(name), flat_t(name, "m_"), flat_t(name, "v_"), tr, "adamw_" + name)
            out[name] = tuple(r.T for r in res)
        else:
            res = out[name] = sum_adamw(parts, mine, me1, flat(name), flat(name, "m_"), flat(name, "v_"), tr,
                                        "adamw_" + name)
        return res[0]

    after = finish("w_down", r_down, grad_x)
    after = finish("w_up", n_up // 2, after)
    after = finish("w_out", r_out, after)
    after = finish("w_in", n_in, after)

    small_all = all_gather(pack_small(small), "gather_small", after)

    wmv = {n: (flat(n), flat(n, "m_"), flat(n, "v_")) for n in SMALL_ORDER}
    per, g_wc, g_wf, loss = small_adamw(small_all, wmv, "adamw_small")
    out.update(per)
    taps = shard_adamw([(_shard(g_wc, n_wc, me), flat("w_conv_dw"), flat("w_conv_dw", "m_"), flat("w_conv_dw", "v_")),
                        (_shard(g_wf, n_wf, me), flat("w_ffn_dw"), flat("w_ffn_dw", "m_"), flat("w_ffn_dw", "v_"))],
                       "adamw_taps")
    out["w_conv_dw"], out["w_ffn_dw"] = taps

    dmod_cols = _shard(small_all[:, 0, :], n_ada, me)
    out["w_ada"] = ada_bwd_adamw(c_all, dmod_cols, flat("w_ada"), flat("w_ada", "m_"), flat("w_ada", "v_"), "adamw_w_ada")

    result = [loss, grad_x[None]]
    for k in range(4):
        result += [out[n][k].reshape(args[n].shape) for n in WEIGHTS]
    return tuple(result)
```

```python
import jax
import jax.numpy as jnp
from jax import lax
from jax.experimental import pallas as pl
from jax.experimental.pallas import tpu as pltpu

F32 = jnp.float32
BF16 = jnp.bfloat16

N_DEV = 8
SEQ = 2048
D_MODEL = 1024
D_CONV = 512
D_ATTN = 512
HEAD_DIM = 64
CONV_K = 31
D_FF = 2816
FFN_K = 3
D_IN = 2 * D_CONV + 3 * D_ATTN
N_MOD = 6
EPS = 1e-6
ATTN_BLOCK = 128
PATTERNS = ((2048, 1), (512, 4), (128, 16))
NEG = -1e30

ADAM_LR, ADAM_B1, ADAM_B2, ADAM_EPS, ADAM_WD, ADAM_STEP = 0.001, 0.9, 0.999, 1e-08, 0.01, 10

ROWS = 256
CONV_HALO = 32
FFN_HALO = 8
FFN_TN = 1408
VMEM_LIMIT = 56 * 1024 * 1024


NT = (((1,), (1,)), ((), ()))
ANY = pl.BlockSpec(memory_space=pl.ANY)


def _cp(*sem):
    return pltpu.CompilerParams(dimension_semantics=sem if sem else None, vmem_limit_bytes=VMEM_LIMIT)


def _with_after(body, n_in, after):
    if after is None:
        return body, [], []
    return (lambda *refs: body(*refs[:n_in], *refs[n_in + 1:])), [ANY], [after]


def _sig(x):
    return 1.0 / (1.0 + jnp.exp(-x))


def _rsum(x):
    return jnp.sum(x, axis=0, keepdims=True)


def _mean(x):
    return jnp.mean(x, axis=-1, keepdims=True)


def _acc(ref, val, first):
    @pl.when(first)
    def _():
        ref[...] = val

    @pl.when(jnp.logical_not(first))
    def _():
        ref[...] += val


SUB = 16


def _for_chunks(fn, unroll=1, rows=ROWS):
    def step(i, carry):
        fn(pl.ds(pl.multiple_of(i * SUB, SUB), SUB))
        return carry

    lax.fori_loop(0, rows // SUB, step, 0, unroll=unroll)


PAIR = 2 * ROWS


def _pair_spec(width, col=0):
    return pl.BlockSpec((PAIR, width), lambda i: (i, col))


def _halves():
    return [slice(h * ROWS, (h + 1) * ROWS) for h in range(2)]


def rms_mod_matmul(x, g, mod, sh_row, sc_row, b, b_rows, name, after=None):
    n = N_DEV * b_rows

    def body(x_ref, g_ref, mod_ref, b_ref, o_ref, h_ref):
        for rs in _halves():
            xx = x_ref[rs, :]
            r = lax.rsqrt(_mean(xx * xx) + EPS)
            h = (xx * r * g_ref[...] * (1.0 + mod_ref[sc_row:sc_row + 1, :]) + mod_ref[sh_row:sh_row + 1, :]).astype(BF16)
            h_ref[rs, :] = h
            o_ref[rs, :] = lax.dot_general(h, b_ref[...].reshape(n, D_MODEL), NT, preferred_element_type=F32)

    body, more_specs, more = _with_after(body, 4, after)
    return pl.pallas_call(
        body, out_shape=(jax.ShapeDtypeStruct((SEQ, n), F32), jax.ShapeDtypeStruct((SEQ, D_MODEL), BF16)),
        grid=(SEQ // PAIR,),
        in_specs=[_pair_spec(D_MODEL), _vec_spec(D_MODEL), _vec_spec(D_MODEL, 8),
                  pl.BlockSpec((N_DEV, b_rows, D_MODEL), lambda i: (0, 0, 0))] + more_specs,
        out_specs=(_pair_spec(n), _pair_spec(D_MODEL)), name=name, compiler_params=_cp("parallel"))(x, g, mod, b, *more)


def norm_concat_matmul(mix_a, att, gao, w, name):
    def body(a_ref, att_ref, g_ref, w_ref, y_ref, mixed_ref):
        for rs in _halves():
            aa = att_ref[rs, :]
            mixed_ref[rs, 0:D_CONV] = a_ref[rs, :]
            mixed_ref[rs, D_CONV:] = (aa * lax.rsqrt(_mean(aa * aa) + EPS) * g_ref[...]).astype(BF16)
            y_ref[rs, :] = jnp.dot(mixed_ref[rs, :], w_ref[...], preferred_element_type=F32)

    return pl.pallas_call(
        body, out_shape=(jax.ShapeDtypeStruct((SEQ, D_MODEL), F32), jax.ShapeDtypeStruct((SEQ, D_MODEL), BF16)),
        grid=(SEQ // PAIR,),
        in_specs=[_pair_spec(D_CONV), _pair_spec(D_ATTN), _vec_spec(D_ATTN), pl.BlockSpec(w.shape, lambda i: (0, 0))],
        out_specs=(_pair_spec(D_MODEL), _pair_spec(D_MODEL)), name=name, compiler_params=_cp("parallel"))(mix_a, att, gao, w)


def resid_rms_mod_matmul(x, y, g, mod, ga_row, sh_row, sc_row, w, name):
    n = w.shape[0]

    def body(x_ref, y_ref, g_ref, mod_ref, w_ref, o_ref, x1_ref, h_ref):
        x1 = x_ref[...] + mod_ref[ga_row:ga_row + 1, :] * y_ref[...]
        x1_ref[...] = x1
        r = lax.rsqrt(_mean(x1 * x1) + EPS)
        h = (x1 * r * g_ref[...] * (1.0 + mod_ref[sc_row:sc_row + 1, :]) + mod_ref[sh_row:sh_row + 1, :]).astype(BF16)
        h_ref[...] = h
        o_ref[...] = lax.dot_general(h, w_ref[...], NT, preferred_element_type=F32)

    return pl.pallas_call(
        body,
        out_shape=(jax.ShapeDtypeStruct((SEQ, n), F32), jax.ShapeDtypeStruct((SEQ, D_MODEL), F32),
                   jax.ShapeDtypeStruct((SEQ, D_MODEL), BF16)),
        grid=(SEQ // ROWS,),
        in_specs=[_row_spec(D_MODEL), _row_spec(D_MODEL), _vec_spec(D_MODEL), _vec_spec(D_MODEL, 8),
                  pl.BlockSpec(w.shape, lambda i: (0, 0))],
        out_specs=(_row_spec(n), _row_spec(D_MODEL), _row_spec(D_MODEL)),
        name=name, compiler_params=_cp("parallel"))(x, y, g, mod, w)


def matmul_combine_bwd(dy, w, att, gao, name, after=None):
    def body(dy_ref, w_ref, att_ref, g_ref, dm_ref, do_ref, dd_ref, dg_ref):
        @pl.when(pl.program_id(0) == 0)
        def _():
            dg_ref[...] = jnp.zeros_like(dg_ref)

        same_head = (jnp.right_shift(lax.broadcasted_iota(jnp.int32, (D_ATTN, D_ATTN), 0), 6)
                     == jnp.right_shift(lax.broadcasted_iota(jnp.int32, (D_ATTN, D_ATTN), 1), 6)).astype(F32)
        for rs in _halves():
            dmixed = lax.dot_general(dy_ref[rs, :], w_ref[...], NT, preferred_element_type=F32)
            dm_ref[rs, :] = dmixed
            att = att_ref[rs, :]
            r = lax.rsqrt(_mean(att * att) + EPS)
            xn = att * r
            dm = dmixed[:, D_CONV:]
            dg_ref[...] += _rsum(dm * xn)
            dyn = dm * g_ref[...]
            do = r * (dyn - xn * _mean(dyn * xn))
            do_ref[rs, :] = do
            dd_ref[rs, :] = jnp.dot(do * att, same_head, preferred_element_type=F32, precision=lax.Precision.HIGHEST)

    rs = _pair_spec(D_ATTN)
    f = jax.ShapeDtypeStruct((SEQ, D_ATTN), F32)
    body, more_specs, more = _with_after(body, 4, after)
    return pl.pallas_call(
        body, out_shape=(jax.ShapeDtypeStruct((SEQ, D_MODEL), F32), f, f, jax.ShapeDtypeStruct((1, D_ATTN), F32)),
        grid=(SEQ // PAIR,),
        in_specs=[_pair_spec(D_MODEL), pl.BlockSpec(w.shape, lambda i: (0, 0)), rs, _vec_spec(D_ATTN)] + more_specs,
        out_specs=(_pair_spec(D_MODEL), rs, rs, _vec_spec(D_ATTN)),
        name=name, compiler_params=_cp("arbitrary"))(dy, w, att, gao, *more)


def matmul_loss_bwd(act, w, x1, tgt, g, mod, ga_row, name):
    def body(a_ref, w_ref, x1_ref, t_ref, g_ref, mod_ref, loss_ref, dx2_ref, dy2_ref, dg_ref, dga_ref):
        @pl.when(pl.program_id(0) == 0)
        def _():
            loss_ref[...] = jnp.zeros_like(loss_ref)
            dg_ref[...] = jnp.zeros_like(dg_ref)
            dga_ref[...] = jnp.zeros_like(dga_ref)

        ga = mod_ref[ga_row:ga_row + 1, :]
        for rs in _halves():
            y2 = jnp.dot(a_ref[rs, :], w_ref[...], preferred_element_type=F32)
            x2 = x1_ref[rs, :] + ga * y2
            r = lax.rsqrt(_mean(x2 * x2) + EPS)
            xn = x2 * r
            err = xn * g_ref[...] - t_ref[rs, :]
            loss_ref[...] += jnp.broadcast_to(0.5 * jnp.sum(_mean(err * err)), (8, 128))
            dy = err * (1.0 / D_MODEL)
            dg_ref[...] += _rsum(dy * xn)
            dxn = dy * g_ref[...]
            dx2 = r * (dxn - xn * _mean(dxn * xn))
            dx2_ref[rs, :] = dx2
            dy2_ref[rs, :] = (dx2 * ga).astype(BF16)
            dga_ref[...] += _rsum(dx2 * y2)

    vec = jax.ShapeDtypeStruct((1, D_MODEL), F32)
    rows = _pair_spec
    return pl.pallas_call(
        body,
        out_shape=(jax.ShapeDtypeStruct((8, 128), F32), jax.ShapeDtypeStruct((SEQ, D_MODEL), F32),
                   jax.ShapeDtypeStruct((SEQ, D_MODEL), BF16), vec, vec),
        grid=(SEQ // PAIR,),
        in_specs=[rows(act.shape[1]), pl.BlockSpec(w.shape, lambda i: (0, 0)), rows(D_MODEL), rows(D_MODEL),
                  _vec_spec(D_MODEL), _vec_spec(D_MODEL, 8)],
        out_specs=(pl.BlockSpec((8, 128), lambda i: (0, 0)), rows(D_MODEL), rows(D_MODEL),
                   _vec_spec(D_MODEL), _vec_spec(D_MODEL)),
        name=name, compiler_params=_cp("arbitrary"))(act, w, x1, tgt, g, mod)


def matmul_rms_mod_bwd(a, b, x, dres, g, mod, sc_row, y, ga_row, name, b_rows=None, after=None):
    gated = y is not None
    pieces = isinstance(a, tuple)
    tm = PAIR if pieces else ROWS
    blocks = [slice(h * ROWS, (h + 1) * ROWS) for h in range(tm // ROWS)]
    rows = lambda width: pl.BlockSpec((tm, width), lambda i: (i, 0))
    if pieces:
        a1, a3 = a
        a_args = [a1, a3]
        a_specs = [rows(a1.shape[1]), pl.BlockSpec((3, tm, a3.shape[2]), lambda i: (0, i, 0))]
        b_arg, b_spec = b, pl.BlockSpec((N_DEV, b_rows, D_MODEL), lambda i: (0, 0, 0))
    else:
        k2 = a.shape[2]
        a_args = [a]
        a_specs = [pl.BlockSpec((2, tm, k2), lambda i: (0, i, 0))]
        b_arg, b_spec = b.reshape(2, k2, D_MODEL), pl.BlockSpec((2, k2, D_MODEL), lambda i: (0, 0, 0))
    n_a = len(a_args)

    def body(*refs):
        a_refs, (b_ref, x_ref, dres_ref, g_ref, mod_ref) = refs[:n_a], refs[n_a:n_a + 5]
        if gated:
            y_ref, dx_ref, dsh_ref, dsc_ref, dg_ref, dy_ref, dga_ref = refs[n_a + 5:]
        else:
            dx_ref, dsh_ref, dsc_ref, dg_ref = refs[n_a + 5:]

        @pl.when(pl.program_id(0) == 0)
        def _():
            for ref in (dsh_ref, dsc_ref, dg_ref) + ((dga_ref,) if gated else ()):
                ref[...] = jnp.zeros_like(ref)

        gg = g_ref[...]
        for rs in blocks:
            if pieces:
                wv = b_ref[...].reshape(N_DEV * b_rows, D_MODEL)
                k1, k3 = a_refs[0].shape[1], a_refs[1].shape[2]
                dh = jnp.dot(a_refs[0][rs, :], wv[0:k1], preferred_element_type=F32)
                for t in range(3):
                    dh = dh + jnp.dot(a_refs[1][t, rs, :], wv[k1 + t * k3:k1 + (t + 1) * k3], preferred_element_type=F32)
            else:
                dh = (jnp.dot(a_refs[0][0, rs, :], b_ref[0], preferred_element_type=F32)
                      + jnp.dot(a_refs[0][1, rs, :], b_ref[1], preferred_element_type=F32))
            xx = x_ref[rs, :]
            r = lax.rsqrt(_mean(xx * xx) + EPS)
            xn = xx * r
            dsh_ref[...] += _rsum(dh)
            dsc_ref[...] += _rsum(dh * (xn * gg))
            dt = dh * (1.0 + mod_ref[sc_row:sc_row + 1, :])
            dg_ref[...] += _rsum(dt * xn)
            dxn = dt * gg
            dx = dres_ref[rs, :] + r * (dxn - xn * _mean(dxn * xn))
            dx_ref[rs, :] = dx
            if gated:
                dga_ref[...] += _rsum(dx * y_ref[rs, :])
                dy_ref[rs, :] = (dx * mod_ref[ga_row:ga_row + 1, :]).astype(BF16)

    vec = jax.ShapeDtypeStruct((1, D_MODEL), F32)
    in_specs = a_specs + [b_spec, rows(D_MODEL), rows(D_MODEL), _vec_spec(D_MODEL), _vec_spec(D_MODEL, 8)]
    out_shape = [jax.ShapeDtypeStruct((SEQ, D_MODEL), F32), vec, vec, vec]
    out_specs = [rows(D_MODEL), _vec_spec(D_MODEL), _vec_spec(D_MODEL), _vec_spec(D_MODEL)]
    args = a_args + [b_arg, x, dres, g, mod]
    if gated:
        in_specs.append(rows(D_MODEL))
        out_shape += [jax.ShapeDtypeStruct((SEQ, D_MODEL), BF16), vec]
        out_specs += [rows(D_MODEL), _vec_spec(D_MODEL)]
        args.append(y)
    body, more_specs, more = _with_after(body, len(args), after)
    return pl.pallas_call(
        body, out_shape=tuple(out_shape), grid=(SEQ // tm,), in_specs=in_specs + more_specs, out_specs=tuple(out_specs),
        name=name, compiler_params=_cp("arbitrary"))(*args, *more)


def _prev_halo(halo, width, col):
    per = ROWS // halo
    return pl.BlockSpec((halo, width), lambda i: (jnp.maximum(i * per - 1, 0), col))


def _next_halo(halo, width, col):
    per = ROWS // halo
    last = SEQ // halo - 1
    return pl.BlockSpec((halo, width), lambda i: (jnp.minimum((i + 1) * per, last), col))


CONV_PAD = ROWS + CONV_HALO


def _shift_copies(sh):
    for b in range(1, 8):
        sh[b, 0:CONV_PAD - 8, :] = sh[0, pl.ds(b, CONV_PAD - 8), :]


def _tap(sh, rs_start, offset):
    return sh[offset % 8, pl.ds(pl.multiple_of(rs_start + (offset // 8) * 8, 8), SUB), :]


def _conv_glu(av_ref, ag_ref, avh_ref, agh_ref, sh):
    i = pl.program_id(0)
    hv = avh_ref[...] * _sig(agh_ref[...])
    sh[0, 0:CONV_HALO, :] = jnp.where(i > 0, hv, 0.0)

    def glu(rs):
        sh[0, pl.ds(pl.multiple_of(rs.start + CONV_HALO, SUB), SUB), :] = av_ref[rs, :] * _sig(ag_ref[rs, :])

    _for_chunks(glu)
    _shift_copies(sh)


def _conv_norm(u1, lg_ref, lb_ref):
    mu = _mean(u1)
    cen = u1 - mu
    rs = lax.rsqrt(_mean(cen * cen) + EPS)
    z = cen * rs
    ln = z * lg_ref[...] + lb_ref[...]
    s = _sig(ln)
    return z, rs, ln, s, ln * s


def conv_module_fwd(proj, wc, bc, lg, lb, gco, name):
    def body(av_ref, ag_ref, avh_ref, agh_ref, wc_ref, bc_ref, lg_ref, lb_ref, gco_ref, out_ref, u1_ref, sh):
        _conv_glu(av_ref, ag_ref, avh_ref, agh_ref, sh)

        def conv(rs):
            u1 = jnp.broadcast_to(bc_ref[...], (SUB, D_CONV))
            for j in range(CONV_K):
                u1 = u1 + wc_ref[j:j + 1, :] * _tap(sh, rs.start, CONV_HALO - (CONV_K - 1) + j)
            u1_ref[rs, :] = u1

        _for_chunks(conv)
        _, _, _, _, u2 = _conv_norm(u1_ref[...], lg_ref, lb_ref)
        rc = lax.rsqrt(_mean(u2 * u2) + EPS)
        out_ref[...] = (u2 * rc * gco_ref[...]).astype(BF16)

    v = _vec_spec(D_CONV)
    return pl.pallas_call(
        body, out_shape=(jax.ShapeDtypeStruct((SEQ, D_CONV), BF16), jax.ShapeDtypeStruct((SEQ, D_CONV), F32)),
        grid=(SEQ // ROWS,),
        in_specs=[_row_spec(D_CONV, 0), _row_spec(D_CONV, 1), _prev_halo(CONV_HALO, D_CONV, 0),
                  _prev_halo(CONV_HALO, D_CONV, 1), _vec_spec(D_CONV, CONV_K), v, v, v, v],
        out_specs=(_row_spec(D_CONV), _row_spec(D_CONV)), scratch_shapes=[pltpu.VMEM((8, CONV_PAD, D_CONV), F32)],
        name=name, compiler_params=_cp("parallel"))(proj, proj, proj, proj, wc, bc, lg, lb, gco)


def conv_module_bwd_a(proj, u1, dmixed, lg, lb, gco, name):
    def body(av_ref, ag_ref, avh_ref, agh_ref, u1_ref, dm_ref, lg_ref, lb_ref, gco_ref,
             du1_ref, dgco_ref, dlg_ref, dlb_ref, dbc_ref, dwc_ref, sh, acc):
        first = pl.program_id(0) == 0
        _conv_glu(av_ref, ag_ref, avh_ref, agh_ref, sh)
        z, rs, ln, s, u2 = _conv_norm(u1_ref[...], lg_ref, lb_ref)
        rc = lax.rsqrt(_mean(u2 * u2) + EPS)
        xn = u2 * rc
        dm = dm_ref[...]
        _acc(dgco_ref, _rsum(dm * xn), first)
        dyn = dm * gco_ref[...]
        du2 = rc * (dyn - xn * _mean(dyn * xn))
        dln = du2 * (s * (1.0 + ln * (1.0 - s)))
        _acc(dlg_ref, _rsum(dln * z), first)
        _acc(dlb_ref, _rsum(dln), first)
        dz = dln * lg_ref[...]
        du1 = rs * (dz - _mean(dz) - z * _mean(dz * z))
        du1_ref[...] = du1
        _acc(dbc_ref, _rsum(du1), first)
        acc[...] = jnp.zeros_like(acc)

        def taps(rs):
            d = du1_ref[rs, :]
            for j in range(CONV_K):
                acc[j] += d * _tap(sh, rs.start, CONV_HALO - (CONV_K - 1) + j)

        _for_chunks(taps)

        @pl.when(first)
        def _():
            dwc_ref[...] = jnp.zeros_like(dwc_ref)

        for j in range(CONV_K):
            dwc_ref[j:j + 1, :] += _rsum(acc[j])

    v = _vec_spec(D_CONV)
    vec = jax.ShapeDtypeStruct((1, D_CONV), F32)
    return pl.pallas_call(
        body,
        out_shape=(jax.ShapeDtypeStruct((SEQ, D_CONV), F32), vec, vec, vec, vec, jax.ShapeDtypeStruct((CONV_K, D_CONV), F32)),
        grid=(SEQ // ROWS,),
        in_specs=[_row_spec(D_CONV, 0), _row_spec(D_CONV, 1), _prev_halo(CONV_HALO, D_CONV, 0),
                  _prev_halo(CONV_HALO, D_CONV, 1), _row_spec(D_CONV, 0), _row_spec(D_CONV, 0), v, v, v],
        out_specs=(_row_spec(D_CONV), v, v, v, v, _vec_spec(D_CONV, CONV_K)),
        scratch_shapes=[pltpu.VMEM((8, CONV_PAD, D_CONV), F32), pltpu.VMEM((CONV_K, SUB, D_CONV), F32)],
        name=name, compiler_params=_cp("arbitrary"))(proj, proj, proj, proj, u1, dmixed, lg, lb, gco)


def conv_module_bwd_b(proj, du1, wc, name):
    def body(av_ref, ag_ref, du1_ref, du1n_ref, wc_ref, out_ref, sh):
        i = pl.program_id(0)
        sh[0, 0:ROWS, :] = du1_ref[...]
        sh[0, ROWS:, :] = jnp.where(i < SEQ // ROWS - 1, du1n_ref[...], 0.0)
        _shift_copies(sh)

        def chunk(rs):
            du0 = jnp.zeros((SUB, D_CONV), F32)
            for j in range(CONV_K):
                du0 = du0 + wc_ref[j:j + 1, :] * _tap(sh, rs.start, CONV_K - 1 - j)
            sg = _sig(ag_ref[rs, :])
            out_ref[rs, 0:D_CONV] = (du0 * sg).astype(BF16)
            out_ref[rs, D_CONV:] = (du0 * av_ref[rs, :] * sg * (1.0 - sg)).astype(BF16)

        _for_chunks(chunk)

    return pl.pallas_call(
        body, out_shape=jax.ShapeDtypeStruct((SEQ, 2 * D_CONV), BF16), grid=(SEQ // ROWS,),
        in_specs=[_row_spec(D_CONV, 0), _row_spec(D_CONV, 1), _row_spec(D_CONV, 0), _next_halo(CONV_HALO, D_CONV, 0),
                  _vec_spec(D_CONV, CONV_K)],
        out_specs=_row_spec(2 * D_CONV), scratch_shapes=[pltpu.VMEM((8, CONV_PAD, D_CONV), F32)],
        name=name, compiler_params=_cp("parallel"))(proj, proj, du1, du1, wc)


def _rows(start, size, r):
    return pl.ds(start, size) if r == 1 else pl.ds(start, size, stride=r)


def _unit_rows(r, rho, n, nb):
    win = 2 * ATTN_BLOCK if nb > 1 else ATTN_BLOCK
    if isinstance(n, int):
        kb = max(n - 1, 0)
        q_rows = _rows(rho + r * ATTN_BLOCK * n, ATTN_BLOCK, r)
        k_rows = _rows(rho + r * ATTN_BLOCK * kb, win, r)
    else:
        kb = jnp.maximum(n - 1, 0)
        q_rows = pl.ds(pl.multiple_of(n * ATTN_BLOCK, ATTN_BLOCK), ATTN_BLOCK)
        k_rows = pl.ds(pl.multiple_of(kb * ATTN_BLOCK, ATTN_BLOCK), win)
    return q_rows, k_rows, n - kb


def _band_bias(first_block, transposed):
    shape = (2 * ATTN_BLOCK, ATTN_BLOCK) if transposed else (ATTN_BLOCK, 2 * ATTN_BLOCK)
    q_axis = 1 if transposed else 0
    dist = (0 if first_block else ATTN_BLOCK) + lax.broadcasted_iota(jnp.int32, shape, q_axis) \
        - lax.broadcasted_iota(jnp.int32, shape, 1 - q_axis)
    return jnp.where((dist >= 0) & (dist <= ATTN_BLOCK), 0.0, NEG)


def _per_head(x):
    lane = lax.broadcasted_iota(jnp.int32, x.shape, 1)
    zero = jnp.zeros_like(x)
    return [jnp.where(lane < HEAD_DIM, x, zero), jnp.where(lane >= HEAD_DIM, x, zero)]


SCALE = HEAD_DIM ** -0.5


def _masked_scores(q2, k2, bias):
    return [lax.dot_general(qh, k2, NT, preferred_element_type=F32) + bias for qh in _per_head(q2)]


def _attn_units(r, nb, unit):
    if r == 1:
        def four(i, carry):
            for k in range(4):
                unit(0, 4 * i + k)
            return carry
        lax.fori_loop(0, nb // 4, four, 0)
    else:
        for rho in range(r):
            for n in range(nb):
                unit(rho, n)


N_UNITS = 16


def attn_fwd_all(proj, name):
    def body(q_ref, k_ref, v_ref, att_ref, lse_ref, s_scr, p_scr, lse_scr, den_scr, bias_scr):
        bias_scr[0] = _band_bias(True, False)
        bias_scr[1] = _band_bias(False, False)
        for idx, (sub_len, r) in enumerate(PATTERNS):
            nb = sub_len // ATTN_BLOCK
            win = 2 * ATTN_BLOCK if nb > 1 else ATTN_BLOCK

            def scores(rho, n, r=r, nb=nb, win=win):
                u = rho * nb + n
                q_rows, k_rows, variant = _unit_rows(r, rho, n, nb)
                ss = _masked_scores((q_ref[q_rows, :] * SCALE).astype(BF16), k_ref[k_rows, :].astype(BF16),
                                    bias_scr[variant, :, 0:win])
                for h in range(2):
                    s_scr[2 * u + h, :, 0:win] = ss[h]

            _attn_units(r, nb, scores)

            def softmax(u, carry, win=win):
                lses, dens = [], []
                for h in range(2):
                    sc = s_scr[2 * u + h, :, 0:win]
                    m = jnp.max(sc, axis=1, keepdims=True)
                    p = jnp.exp(sc - m)
                    den = jnp.sum(p, axis=1, keepdims=True)
                    p_scr[2 * u + h, :, 0:win] = p.astype(BF16)
                    lses.append(jnp.broadcast_to(m + jnp.log(den), (ATTN_BLOCK, HEAD_DIM)))
                    dens.append(jnp.broadcast_to(den, (ATTN_BLOCK, HEAD_DIM)))
                lse_scr[u] = jnp.concatenate(lses, axis=1)
                den_scr[u] = jnp.concatenate(dens, axis=1)
                return carry

            lax.fori_loop(0, N_UNITS, softmax, 0, unroll=2)

            def outputs(rho, n, r=r, nb=nb, win=win, idx=idx):
                u = rho * nb + n
                q_rows, k_rows, _ = _unit_rows(r, rho, n, nb)
                vs = _per_head(v_ref[k_rows, :].astype(BF16))
                o = (jnp.dot(p_scr[2 * u, :, 0:win], vs[0], preferred_element_type=F32)
                     + jnp.dot(p_scr[2 * u + 1, :, 0:win], vs[1], preferred_element_type=F32)) / den_scr[u]
                lse = lse_scr[u]
                if idx > 0:
                    old = lse_ref[q_rows, :]
                    top = jnp.maximum(old, lse)
                    new = top + jnp.log(jnp.exp(old - top) + jnp.exp(lse - top))
                    o = att_ref[q_rows, :] * jnp.exp(old - new) + o * jnp.exp(lse - new)
                    lse = new
                att_ref[q_rows, :] = o
                lse_ref[q_rows, :] = lse

            _attn_units(r, nb, outputs)

    blk = lambda first: pl.BlockSpec((SEQ, 128), lambda g: (0, first + g))
    shp = jax.ShapeDtypeStruct((SEQ, D_ATTN), F32)
    big = (2 * N_UNITS, ATTN_BLOCK, 2 * ATTN_BLOCK)
    small = pltpu.VMEM((N_UNITS, ATTN_BLOCK, 128), F32)
    return pl.pallas_call(
        body, out_shape=(shp, shp), grid=(4,), in_specs=[blk(8), blk(12), blk(16)], out_specs=(blk(0), blk(0)),
        scratch_shapes=[pltpu.VMEM(big, F32), pltpu.VMEM(big, BF16), small, small,
                        pltpu.VMEM((2, ATTN_BLOCK, 2 * ATTN_BLOCK), F32)],
        name=name, compiler_params=_cp("parallel"))(proj, proj, proj)


def attn_bwd_all(proj, do, lse, dd, name):
    def body(q_ref, k_ref, v_ref, do_ref, l_ref, dd_ref, out_ref, dq_s, dk_s, dv_s,
             s_scr, dp_scr, ds_scr, st_scr, dpt_scr, pt_scr, dst_scr, qb_scr, kb_scr, dob_scr, bias_scr, bias_t_scr):
        for first_block in (True, False):
            bias_scr[1 - int(first_block)] = _band_bias(first_block, False)
            bias_t_scr[1 - int(first_block)] = _band_bias(first_block, True)
        dq_s[...] = jnp.zeros_like(dq_s)
        dk_s[...] = jnp.zeros_like(dk_s)
        dv_s[...] = jnp.zeros_like(dv_s)
        for sub_len, r in PATTERNS:
            nb = sub_len // ATTN_BLOCK
            win = 2 * ATTN_BLOCK if nb > 1 else ATTN_BLOCK

            def scores(rho, n, r=r, nb=nb, win=win):
                u = rho * nb + n
                q_rows, k_rows, variant = _unit_rows(r, rho, n, nb)
                bias, bias_t = bias_scr[variant, :, 0:win], bias_t_scr[variant, 0:win, :]
                q2 = (q_ref[q_rows, :] * SCALE).astype(BF16)
                kf = k_ref[k_rows, :]
                k2 = kf.astype(BF16)
                do2 = do_ref[q_rows, :].astype(BF16)
                qb_scr[u] = q2
                kb_scr[u, 0:win, :] = (kf * SCALE).astype(BF16)
                dob_scr[u] = do2
                l2 = l_ref[q_rows, :]
                d2 = dd_ref[q_rows, :]
                l2t = l2.T
                d2t = d2.T
                v2 = v_ref[k_rows, :].astype(BF16)
                qs, dos = _per_head(q2), _per_head(do2)
                for h in range(2):
                    c0 = h * HEAD_DIM
                    sc = lax.dot_general(qs[h], k2, NT, preferred_element_type=F32)
                    s_scr[2 * u + h, :, 0:win] = sc + bias - l2[:, c0:c0 + 1]
                    dp_scr[2 * u + h, :, 0:win] = lax.dot_general(dos[h], v2, NT, preferred_element_type=F32) \
                        - d2[:, c0:c0 + 1]
                    sct = lax.dot_general(k2, qs[h], NT, preferred_element_type=F32)
                    st_scr[2 * u + h, 0:win, :] = sct + bias_t - l2t[c0:c0 + 1, :]
                    dpt_scr[2 * u + h, 0:win, :] = lax.dot_general(v2, dos[h], NT, preferred_element_type=F32) \
                        - d2t[c0:c0 + 1, :]

            _attn_units(r, nb, scores)

            def pointwise(hu, carry, win=win):
                ds_scr[hu, :, 0:win] = (jnp.exp(s_scr[hu, :, 0:win]) * dp_scr[hu, :, 0:win]).astype(BF16)
                pt = jnp.exp(st_scr[hu, 0:win, :])
                pt_scr[hu, 0:win, :] = pt.astype(BF16)
                dst_scr[hu, 0:win, :] = (pt * dpt_scr[hu, 0:win, :]).astype(BF16)
                return carry

            lax.fori_loop(0, 2 * N_UNITS, pointwise, 0, unroll=4)

            def grads(rho, n, r=r, nb=nb, win=win):
                u = rho * nb + n
                q_rows, k_rows, _ = _unit_rows(r, rho, n, nb)
                qs, ks, dos = _per_head(qb_scr[u]), _per_head(kb_scr[u, 0:win, :]), _per_head(dob_scr[u])

                def both(scr, rows, rhs):
                    return (jnp.dot(scr[(2 * u,) + rows], rhs[0], preferred_element_type=F32)
                            + jnp.dot(scr[(2 * u + 1,) + rows], rhs[1], preferred_element_type=F32))

                dq_s[q_rows, :] += both(ds_scr, (slice(None), slice(0, win)), ks)
                dk_s[k_rows, :] += both(dst_scr, (slice(0, win), slice(None)), qs)
                dv_s[k_rows, :] += both(pt_scr, (slice(0, win), slice(None)), dos)

            _attn_units(r, nb, grads)
        out_ref[0] = dq_s[...].astype(BF16)
        out_ref[1] = dk_s[...].astype(BF16)
        out_ref[2] = dv_s[...].astype(BF16)

    blk = lambda first: pl.BlockSpec((SEQ, 128), lambda g: (0, first + g))
    acc = pltpu.VMEM((SEQ, 128), F32)
    big = (2 * N_UNITS, ATTN_BLOCK, 2 * ATTN_BLOCK)
    big_t = (2 * N_UNITS, 2 * ATTN_BLOCK, ATTN_BLOCK)
    return pl.pallas_call(
        body, out_shape=jax.ShapeDtypeStruct((3, SEQ, D_ATTN), BF16), grid=(4,),
        in_specs=[blk(8), blk(12), blk(16), blk(0), blk(0), blk(0)],
        out_specs=pl.BlockSpec((3, SEQ, 128), lambda g: (0, 0, g)),
        scratch_shapes=[acc, acc, acc, pltpu.VMEM(big, F32), pltpu.VMEM(big, F32), pltpu.VMEM(big, BF16),
                        pltpu.VMEM(big_t, F32), pltpu.VMEM(big_t, F32), pltpu.VMEM(big_t, BF16), pltpu.VMEM(big_t, BF16),
                        pltpu.VMEM((N_UNITS, ATTN_BLOCK, 128), BF16),
                        pltpu.VMEM((N_UNITS, 2 * ATTN_BLOCK, 128), BF16), pltpu.VMEM((N_UNITS, ATTN_BLOCK, 128), BF16),
                        pltpu.VMEM((2, ATTN_BLOCK, 2 * ATTN_BLOCK), F32), pltpu.VMEM((2, 2 * ATTN_BLOCK, ATTN_BLOCK), F32)],
        name=name, compiler_params=_cp("parallel"))(proj, proj, proj, do, lse, dd)


N_FT = D_FF // FFN_TN


def _ffn_specs():
    per = ROWS // FFN_HALO
    cur_g = pl.BlockSpec((ROWS, FFN_TN), lambda j, i: (i, j))
    cur_v = pl.BlockSpec((ROWS, FFN_TN), lambda j, i: (i, j + N_FT))
    halo_g = pl.BlockSpec((FFN_HALO, FFN_TN), lambda j, i: (jnp.maximum(i * per - 1, 0), j))
    halo_v = pl.BlockSpec((FFN_HALO, FFN_TN), lambda j, i: (jnp.maximum(i * per - 1, 0), j + N_FT))
    w_g = pl.BlockSpec((FFN_K, FFN_TN), lambda j, i: (0, j))
    w_v = pl.BlockSpec((FFN_K, FFN_TN), lambda j, i: (0, j + N_FT))
    b_g = pl.BlockSpec((1, FFN_TN), lambda j, i: (0, j))
    b_v = pl.BlockSpec((1, FFN_TN), lambda j, i: (0, j + N_FT))
    return [cur_g, cur_v, halo_g, halo_v, w_g, w_v, b_g, b_v]


def matmul(a, b, kind, out_dtype, tm, tn, name, b_rows=None):
    stacked = b_rows is not None
    b_shape = (N_DEV * b_rows, b.shape[2]) if stacked else b.shape
    if kind == "nn":
        (m, k), n = a.shape, b_shape[1]
        a_spec = pl.BlockSpec((tm, k), lambda j, i: (i, 0))
        b_spec = pl.BlockSpec((k, tn), lambda j, i: (0, j))
        dims = (((1,), (0,)), ((), ()))
    elif kind == "nt":
        (m, k), n = a.shape, b_shape[0]
        a_spec = pl.BlockSpec((tm, k), lambda j, i: (i, 0))
        b_spec = pl.BlockSpec((tn, k), lambda j, i: (j, 0))
        dims = (((1,), (1,)), ((), ()))
    else:
        (k, m), n = a.shape, b_shape[1]
        a_spec = pl.BlockSpec((k, tm), lambda j, i: (0, i))
        b_spec = pl.BlockSpec((k, tn), lambda j, i: (0, j))
        dims = (((0,), (0,)), ((), ()))
    assert m % tm == 0 and n % tn == 0, (name, m, n, tm, tn)
    if stacked:
        assert b_spec.block_shape[0] == b_shape[0] and kind in ("nn", "nt")
        width = b_spec.block_shape[1]
        b_spec = pl.BlockSpec((N_DEV, b_rows, width), (lambda j, i: (0, 0, j)) if kind == "nn" else (lambda j, i: (0, 0, 0)))

    def body(a_ref, b_ref, o_ref):
        bb = b_ref[...].reshape(b_shape[0], -1) if stacked else b_ref[...]
        o_ref[...] = lax.dot_general(a_ref[...], bb, dims, preferred_element_type=F32).astype(o_ref.dtype)

    return pl.pallas_call(
        body, out_shape=jax.ShapeDtypeStruct((m, n), out_dtype), grid=(n // tn, m // tm),
        in_specs=[a_spec, b_spec], out_specs=pl.BlockSpec((tm, tn), lambda j, i: (i, j)),
        name=name, compiler_params=_cp("parallel", "parallel"))(a, b)


def matmul_tn_pieces(a1, a3, b, name):
    k, n = b.shape
    tm = a3.shape[2]
    n1 = a1.shape[1] // tm

    def body(a1_ref, a3_ref, b_ref, o_ref):
        i = pl.program_id(0)
        tn_dims = (((0,), (0,)), ((), ()))

        @pl.when(i < n1)
        def _():
            o_ref[...] = lax.dot_general(a1_ref[...], b_ref[...], tn_dims, preferred_element_type=F32).astype(BF16)

        @pl.when(i >= n1)
        def _():
            o_ref[...] = lax.dot_general(a3_ref[0], b_ref[...], tn_dims, preferred_element_type=F32).astype(BF16)

    return pl.pallas_call(
        body, out_shape=jax.ShapeDtypeStruct((a1.shape[1] + 3 * tm, n), BF16), grid=(n1 + 3,),
        in_specs=[pl.BlockSpec((k, tm), lambda i: (0, jnp.minimum(i, n1 - 1))),
                  pl.BlockSpec((1, k, tm), lambda i: (jnp.maximum(i - n1, 0), 0, 0)),
                  pl.BlockSpec((k, n), lambda i: (0, 0))],
        out_specs=pl.BlockSpec((tm, n), lambda i: (i, 0)), name=name, compiler_params=_cp("parallel"))(a1, a3, b)


def matmul_tn_halves(a, b, tm, name):
    _, k, m = a.shape
    n = b.shape[1]

    def body(a_ref, b_ref, o_ref):
        o_ref[0] = lax.dot_general(a_ref[0], b_ref[...], (((0,), (0,)), ((), ())), preferred_element_type=F32).astype(BF16)

    return pl.pallas_call(
        body, out_shape=jax.ShapeDtypeStruct((2, m, n), BF16), grid=(2, m // tm),
        in_specs=[pl.BlockSpec((1, k, tm), lambda h, i: (h, 0, i)), pl.BlockSpec((k, n), lambda h, i: (0, 0))],
        out_specs=pl.BlockSpec((1, tm, n), lambda h, i: (h, i, 0)), name=name,
        compiler_params=_cp("parallel", "parallel"))(a, b).reshape(2 * m, n)


def _row_spec(width, col=0):
    return pl.BlockSpec((ROWS, width), lambda i: (i, col))


def _vec_spec(width, rows=1):
    return pl.BlockSpec((rows, width), lambda i: (0, 0))


FFN_RB = 64


def _ffn_lane_blocks(fn):
    for c in range(FFN_TN // 128):
        fn(slice(c * 128, (c + 1) * 128))


def _ffn_taps(cur_ref, halo_ref, head, ls, r0):
    if r0 == 0:
        head[0:FFN_HALO, :] = jnp.where(pl.program_id(1) > 0, halo_ref[:, ls], 0.0)
        head[FFN_HALO:, :] = cur_ref[0:FFN_RB, ls]
        return tuple(head[pl.ds(FFN_HALO - k, FFN_RB), :] for k in (2, 1, 0))
    return tuple(cur_ref[pl.ds(r0 - k, FFN_RB), ls] for k in (2, 1, 0))


def _ffn_conv(taps, w, b):
    return b + w[0] * taps[0] + w[1] * taps[1] + w[2] * taps[2]


def ffn_act_fwd(up0, wf, bf, name):
    def body(g_ref, v_ref, gh_ref, vh_ref, wg_ref, wv_ref, bg_ref, bv_ref, act_ref, head_g, head_v):
        def lanes(ls):
            wg = [wg_ref[t:t + 1, ls] for t in range(FFN_K)]
            wv = [wv_ref[t:t + 1, ls] for t in range(FFN_K)]
            for r0 in range(0, ROWS, FFN_RB):
                gate = _ffn_conv(_ffn_taps(g_ref, gh_ref, head_g, ls, r0), wg, bg_ref[:, ls])
                val = _ffn_conv(_ffn_taps(v_ref, vh_ref, head_v, ls, r0), wv, bv_ref[:, ls])
                act_ref[r0:r0 + FFN_RB, ls] = (gate * _sig(gate) * val).astype(BF16)

        _ffn_lane_blocks(lanes)

    head = pltpu.VMEM((FFN_HALO + FFN_RB, 128), F32)
    return pl.pallas_call(
        body, out_shape=jax.ShapeDtypeStruct((SEQ, D_FF), BF16), grid=(N_FT, SEQ // ROWS),
        in_specs=_ffn_specs(), out_specs=pl.BlockSpec((ROWS, FFN_TN), lambda j, i: (i, j)),
        scratch_shapes=[head, head], name=name, compiler_params=_cp("parallel", "parallel"))(up0, up0, up0, up0, wf, wf, bf, bf)


def ffn_bwd(up0, dact, wf, bf, name, after=None):
    per = ROWS // FFN_HALO
    last = SEQ // FFN_HALO - 1

    def body(g_ref, v_ref, gh_ref, vh_ref, wg_ref, wv_ref, bg_ref, bv_ref, da_ref, gn_ref, vn_ref, dan_ref,
             out_ref, dbg_ref, dbv_ref, dwg_ref, dwv_ref, pad, head_g, head_v, dgp, dvp, acc):
        i = pl.program_id(1)
        first = i == 0
        acc[...] = jnp.zeros_like(acc)

        def grads(gate, val, da):
            s = _sig(gate)
            return da * val * (s * (1.0 + gate * (1.0 - s))), da * (gate * s)

        fold = lambda q: jnp.sum(q.reshape(FFN_RB // 8, 8, 128), axis=0)

        def lanes(ls):
            wg = [wg_ref[t:t + 1, ls] for t in range(FFN_K)]
            wv = [wv_ref[t:t + 1, ls] for t in range(FFN_K)]
            sums = [jnp.zeros((8, 128), F32)] * (2 + 2 * FFN_K)
            for r0 in range(0, ROWS, FFN_RB):
                gs = _ffn_taps(g_ref, gh_ref, head_g, ls, r0)
                vs = _ffn_taps(v_ref, vh_ref, head_v, ls, r0)
                dgate, dval = grads(_ffn_conv(gs, wg, bg_ref[:, ls]), _ffn_conv(vs, wv, bv_ref[:, ls]),
                                    da_ref[r0:r0 + FFN_RB, ls])
                dgp[r0:r0 + FFN_RB, ls] = dgate
                dvp[r0:r0 + FFN_RB, ls] = dval
                new = [dgate, dval] + [dgate * gs[t] for t in range(FFN_K)] + [dval * vs[t] for t in range(FFN_K)]
                sums = [a + fold(q) for a, q in zip(sums, new)]
            for k in range(2 + 2 * FFN_K):
                acc[k, 0:8, ls] = sums[k]

        _ffn_lane_blocks(lanes)
        _acc(dbg_ref, _rsum(acc[0]), first)
        _acc(dbv_ref, _rsum(acc[1]), first)
        _acc(dwg_ref, jnp.concatenate([_rsum(acc[2 + t]) for t in range(FFN_K)], axis=0), first)
        _acc(dwv_ref, jnp.concatenate([_rsum(acc[5 + t]) for t in range(FFN_K)], axis=0), first)

        def conv_next(cur_ref, nxt_ref, w_ref, b_ref):
            pad[0:FFN_HALO, :] = cur_ref[ROWS - FFN_HALO:, :]
            pad[FFN_HALO:, :] = nxt_ref[...]
            return (b_ref[...] + w_ref[0:1, :] * pad[pl.ds(FFN_HALO - 2, FFN_HALO), :]
                    + w_ref[1:2, :] * pad[pl.ds(FFN_HALO - 1, FFN_HALO), :] + w_ref[2:3, :] * nxt_ref[...])

        gate_n = conv_next(g_ref, gn_ref, wg_ref, bg_ref)
        val_n = conv_next(v_ref, vn_ref, wv_ref, bv_ref)
        dgate_n, dval_n = grads(gate_n, val_n, dan_ref[...])
        inside = i < SEQ // ROWS - 1
        dgp[ROWS:, :] = jnp.where(inside, dgate_n, 0.0)
        dvp[ROWS:, :] = jnp.where(inside, dval_n, 0.0)

        def back(ls):
            for half, (dp, w_ref) in enumerate(((dgp, wg_ref), (dvp, wv_ref))):
                w = [w_ref[t:t + 1, ls] for t in range(FFN_K)]
                for r0 in range(0, ROWS, FFN_RB):
                    out_ref[half, r0:r0 + FFN_RB, ls] = (
                        w[2] * dp[r0:r0 + FFN_RB, ls] + w[1] * dp[pl.ds(r0 + 1, FFN_RB), ls]
                        + w[0] * dp[pl.ds(r0 + 2, FFN_RB), ls]).astype(BF16)

        _ffn_lane_blocks(back)

    body, more_specs, more = _with_after(body, 12, after)
    head = pltpu.VMEM((FFN_HALO + FFN_RB, 128), F32)
    ext = pltpu.VMEM((ROWS + FFN_HALO, FFN_TN), F32)
    vec = jax.ShapeDtypeStruct((1, D_FF), F32)
    taps = jax.ShapeDtypeStruct((FFN_K, D_FF), F32)
    cur = pl.BlockSpec((ROWS, FFN_TN), lambda j, i: (i, j))
    nxt = lambda off: pl.BlockSpec((FFN_HALO, FFN_TN), lambda j, i: (jnp.minimum((i + 1) * per, last), j + off))
    vs = pl.BlockSpec((1, FFN_TN), lambda j, i: (0, j))
    ts = pl.BlockSpec((FFN_K, FFN_TN), lambda j, i: (0, j))
    return pl.pallas_call(
        body, out_shape=(jax.ShapeDtypeStruct((2, SEQ, D_FF), BF16), vec, vec, taps, taps), grid=(N_FT, SEQ // ROWS),
        in_specs=_ffn_specs() + [cur, nxt(0), nxt(N_FT), nxt(0)] + more_specs,
        out_specs=(pl.BlockSpec((2, ROWS, FFN_TN), lambda j, i: (0, i, j)), vs, vs, ts, ts),
        scratch_shapes=[pltpu.VMEM((2 * FFN_HALO, FFN_TN), F32), head, head, ext, ext,
                        pltpu.VMEM((2 + 2 * FFN_K, SUB, FFN_TN), F32)],
        name=name, compiler_params=_cp("parallel", "arbitrary"))(up0, up0, up0, up0, wf, wf, bf, bf, dact, up0, up0, dact,
                                                                 *more)


def ada_fwd(c_all, w_ada, b_cols, name):
    def body(c_ref, w_ref, b_ref, o_ref):
        cc = c_ref[...]
        sc = (cc * _sig(cc)).astype(BF16)
        o_ref[...] = jnp.dot(sc, w_ref[...].astype(BF16), preferred_element_type=F32) + b_ref[...]

    return pl.pallas_call(body, out_shape=jax.ShapeDtypeStruct((N_DEV, w_ada.shape[1]), F32), name=name,
                          compiler_params=_cp())(c_all, w_ada, b_cols)


def _adam(w, g, m, v):
    m = ADAM_B1 * m + (1.0 - ADAM_B1) * g
    v = ADAM_B2 * v + (1.0 - ADAM_B2) * (g * g)
    m_hat = m / (1.0 - ADAM_B1 ** ADAM_STEP)
    v_hat = v / (1.0 - ADAM_B2 ** ADAM_STEP)
    delta = -ADAM_LR * (m_hat / (jnp.sqrt(v_hat) + ADAM_EPS) + ADAM_WD * w)
    return delta, m, v


def ada_bwd_adamw(c_all, dmod_cols, w, m, v, name):
    rows, cols = w.shape
    tr = 256

    def body(c_ref, dm_ref, w_ref, m_ref, v_ref, g_ref, d_ref, nm_ref, nv_ref):
        cc = c_ref[...]
        sc = (cc * _sig(cc)).T
        g = sc[:, 0:1] * dm_ref[0:1, :]
        for b in range(1, N_DEV):
            g = g + sc[:, b:b + 1] * dm_ref[b:b + 1, :]
        g_ref[...] = g
        d_ref[...], nm_ref[...], nv_ref[...] = _adam(w_ref[...], g, m_ref[...], v_ref[...])

    blk = pl.BlockSpec((tr, cols), lambda i: (i, 0))
    shp = jax.ShapeDtypeStruct((rows, cols), F32)
    return pl.pallas_call(
        body, out_shape=(shp, shp, shp, shp), grid=(rows // tr,),
        in_specs=[pl.BlockSpec((N_DEV, tr), lambda i: (0, i)), pl.BlockSpec((N_DEV, cols), lambda i: (0, 0)), blk, blk, blk],
        out_specs=(blk, blk, blk, blk), name=name, compiler_params=_cp("parallel"))(c_all, dmod_cols, w, m, v)


def sum_adamw(parts, mine, me, w, m, v, tr, name):
    n_parts, rows, cols = parts.shape

    def body(me_ref, p_ref, own_ref, w_ref, m_ref, v_ref, g_ref, d_ref, nm_ref, nv_ref):
        def chunk(rs):
            g = own_ref[0, rs, :].astype(F32)
            for k in range(1, n_parts):
                g = g + p_ref[k, rs, :].astype(F32)
            g_ref[rs, :] = g
            d_ref[rs, :], nm_ref[rs, :], nv_ref[rs, :] = _adam(w_ref[rs, :], g, m_ref[rs, :], v_ref[rs, :])

        _for_chunks(chunk, 2, tr)

    blk = pl.BlockSpec((tr, cols), lambda i, me_ref: (i, 0))
    shp = jax.ShapeDtypeStruct((rows, cols), F32)
    grid_spec = pltpu.PrefetchScalarGridSpec(
        num_scalar_prefetch=1, grid=(rows // tr,),
        in_specs=[pl.BlockSpec((n_parts, tr, cols), lambda i, me_ref: (0, i, 0)),
                  pl.BlockSpec((1, tr, cols), lambda i, me_ref: (me_ref[0], i, 0)), blk, blk, blk],
        out_specs=(blk, blk, blk, blk))
    return pl.pallas_call(body, out_shape=(shp, shp, shp, shp), grid_spec=grid_spec, name=name,
                          compiler_params=_cp("parallel"))(me, parts, mine, w, m, v)


MESH = pl.DeviceIdType.MESH


def all_gather(block, name, after=None):
    extra = () if after is None else (after,)

    def body(x_ref, *refs):
        out_ref, send_sems, recv_sems, local_sem = refs[len(extra):]
        x, y, c = lax.axis_index("x"), lax.axis_index("y"), lax.axis_index("c")
        me, sibling = (x, y, c), (x, y, 1 - c)
        chips = [(1 - x, y), (x, 1 - y), (1 - x, 1 - y)]

        def slot(px, py, pc):
            return out_ref.at[4 * px + 2 * py + pc]

        def copy(k, blk, to, src=None):
            return pltpu.make_async_remote_copy(
                src_ref=slot(*blk) if src is None else src, dst_ref=slot(*blk),
                send_sem=send_sems.at[k], recv_sem=recv_sems.at[k], device_id=to, device_id_type=MESH)

        mine = pltpu.make_async_copy(x_ref, slot(*me), local_sem)
        mine.start()
        first = [copy(0, me, sibling, src=x_ref)]
        first += [copy(1 + j, me, (*chip, c), src=x_ref) for j, chip in enumerate(chips)]
        for cp in first:
            cp.start()
        passed = [copy(4 + j, (*chip, c), sibling) for j, chip in enumerate(chips)]
        for j, chip in enumerate(chips):
            copy(1 + j, (*chip, c), me).wait_recv()
            passed[j].start()
        copy(0, sibling, me).wait_recv()
        for j, chip in enumerate(chips):
            copy(4 + j, (*chip, 1 - c), me).wait_recv()
        for cp in first + passed:
            cp.wait_send()
        mine.wait()

    return pl.pallas_call(
        body, out_shape=jax.ShapeDtypeStruct((N_DEV,) + block.shape, block.dtype), in_specs=[ANY] * (1 + len(extra)), out_specs=ANY,
        scratch_shapes=[pltpu.SemaphoreType.DMA((7,)), pltpu.SemaphoreType.DMA((7,)), pltpu.SemaphoreType.DMA],
        name=name)(block, *extra)


HBM = pl.BlockSpec(memory_space=pltpu.HBM)
SEM = pl.BlockSpec(memory_space=pltpu.SEMAPHORE)
EFFECT = pltpu.SideEffectType.DATAFLOW_SIDE_EFFECTING


def _peer_copies(src_ref, land_ref, send_sems, recv_sems, gather):
    x, y, c = lax.axis_index("x"), lax.axis_index("y"), lax.axis_index("c")
    me = 4 * x + 2 * y + c
    copies = []
    for k in range(1, N_DEV):
        px = 1 - x if k & 4 else x
        py = 1 - y if k & 2 else y
        pc = 1 - c if k & 1 else c
        copies.append(pltpu.make_async_remote_copy(
            src_ref=src_ref if gather else src_ref.at[4 * px + 2 * py + pc],
            dst_ref=land_ref.at[me] if gather else land_ref.at[k],
            send_sem=send_sems.at[k - 1], recv_sem=recv_sems.at[k - 1], device_id=(px, py, pc), device_id_type=MESH))
    return copies


def exchange_start(srcs, gather, name, after=None):
    n = len(srcs)
    land_shapes = [(N_DEV,) + src.shape if gather else src.shape for src in srcs]
    extra = () if after is None else (after,)

    def body(*refs):
        src_refs, land_refs = refs[0:n], refs[n:2 * n]
        outs = refs[2 * n + len(extra):]
        for k in range(n):
            for cp in _peer_copies(src_refs[k], land_refs[k], outs[4 * k], outs[4 * k + 1], gather):
                cp.start()
        token = outs[4 * n]
        token[...] = jnp.zeros_like(token)

    out_shape, out_specs, aliases = [], [], {}
    for k, src in enumerate(srcs):
        out_shape += [pltpu.SemaphoreType.DMA((N_DEV - 1,)), pltpu.SemaphoreType.DMA((N_DEV - 1,)),
                      pltpu.HBM(src.shape, src.dtype), pltpu.HBM(land_shapes[k], src.dtype)]
        out_specs += [SEM, SEM, HBM, HBM]
        aliases[k] = 4 * k + 2
        aliases[n + k] = 4 * k + 3
    out_shape.append(jax.ShapeDtypeStruct((8, 128), F32))
    out_specs.append(pl.BlockSpec(memory_space=pltpu.VMEM))
    res = pl.pallas_call(
        body, name=name, out_shape=tuple(out_shape), in_specs=(HBM,) * (2 * n) + (ANY,) * len(extra),
        out_specs=tuple(out_specs), input_output_aliases=aliases,
        compiler_params=pltpu.CompilerParams(has_side_effects=EFFECT),
    )(*[pltpu.with_memory_space_constraint(src, pltpu.HBM) for src in srcs],
      *[pltpu.with_memory_space_constraint(lax.empty(shp, src.dtype), pltpu.HBM) for shp, src in zip(land_shapes, srcs)],
      *extra)
    return [tuple(res[4 * k:4 * k + 4]) for k in range(n)], res[4 * n]


def _stage1_peer(i):
    x, y, c = lax.axis_index("x"), lax.axis_index("y"), lax.axis_index("c")
    if i == 0:
        return (x, y, 1 - c)
    return (1 - x if i & 1 else x, 1 - y if i & 2 else y, c)


def _slot_of(peer):
    return 4 * peer[0] + 2 * peer[1] + peer[2]


def _stage1_copy(i, src_ref, land_ref, send_sems, recv_sems):
    me = _slot_of((lax.axis_index("x"), lax.axis_index("y"), lax.axis_index("c")))
    return pltpu.make_async_remote_copy(src_ref=src_ref, dst_ref=land_ref.at[me], send_sem=send_sems.at[i],
                                        recv_sem=recv_sems.at[i], device_id=_stage1_peer(i), device_id_type=MESH)


def _stage2_copy(j, land_ref, send_sems, recv_sems):
    slot = _slot_of(_stage1_peer(j + 1))
    return pltpu.make_async_remote_copy(src_ref=land_ref.at[slot], dst_ref=land_ref.at[slot], send_sem=send_sems.at[j],
                                        recv_sem=recv_sems.at[j], device_id=_stage1_peer(0), device_id_type=MESH)


def gather2_start(srcs, name, after=None):
    n = len(srcs)
    extra = () if after is None else (after,)

    def body(*refs):
        src_refs, land_refs = refs[0:n], refs[n:2 * n]
        outs = refs[2 * n + len(extra):]
        for k in range(n):
            for i in range(4):
                _stage1_copy(i, src_refs[k], land_refs[k], outs[4 * k], outs[4 * k + 1]).start()
        outs[4 * n][...] = jnp.zeros((8, 128), F32)

    out_shape, out_specs, aliases = [], [], {}
    for k, src in enumerate(srcs):
        out_shape += [pltpu.SemaphoreType.DMA((4,)), pltpu.SemaphoreType.DMA((4,)),
                      pltpu.HBM(src.shape, src.dtype), pltpu.HBM((N_DEV,) + src.shape, src.dtype)]
        out_specs += [SEM, SEM, HBM, HBM]
        aliases[k] = 4 * k + 2
        aliases[n + k] = 4 * k + 3
    out_shape.append(jax.ShapeDtypeStruct((8, 128), F32))
    out_specs.append(pl.BlockSpec(memory_space=pltpu.VMEM))
    res = pl.pallas_call(
        body, name=name, out_shape=tuple(out_shape), in_specs=(HBM,) * (2 * n) + (ANY,) * len(extra),
        out_specs=tuple(out_specs), input_output_aliases=aliases,
        compiler_params=pltpu.CompilerParams(has_side_effects=EFFECT),
    )(*[pltpu.with_memory_space_constraint(src, pltpu.HBM) for src in srcs],
      *[pltpu.with_memory_space_constraint(lax.empty((N_DEV,) + src.shape, src.dtype), pltpu.HBM) for src in srcs], *extra)
    return [dict(send1=res[4 * k], recv1=res[4 * k + 1], src=res[4 * k + 2], land=res[4 * k + 3]) for k in range(n)], \
        res[4 * n]


def gather2_pass(handles, after, name):
    n = len(handles)

    def body(*refs):
        src_refs, land_refs, recv1 = refs[0:n], refs[n:2 * n], refs[2 * n:3 * n]
        outs = refs[3 * n + 1:]
        for k in range(n):
            for j in range(3):
                _stage1_copy(j + 1, src_refs[k], land_refs[k], recv1[k], recv1[k]).wait_recv()
                _stage2_copy(j, land_refs[k], outs[3 * k], outs[3 * k + 1]).start()

    out_shape, out_specs, aliases = [], [], {}
    for k, h in enumerate(handles):
        out_shape += [pltpu.SemaphoreType.DMA((3,)), pltpu.SemaphoreType.DMA((3,)), pltpu.HBM(h["land"].shape, h["land"].dtype)]
        out_specs += [SEM, SEM, HBM]
        aliases[n + k] = 3 * k + 2
    res = pl.pallas_call(
        body, name=name, out_shape=tuple(out_shape), in_specs=(HBM,) * (2 * n) + (SEM,) * n + (ANY,),
        out_specs=tuple(out_specs), input_output_aliases=aliases,
        compiler_params=pltpu.CompilerParams(has_side_effects=EFFECT),
    )(*[h["src"] for h in handles], *[h["land"] for h in handles], *[h["recv1"] for h in handles], after)
    return [dict(h, send2=res[3 * k], recv2=res[3 * k + 1], land=res[3 * k + 2]) for k, h in enumerate(handles)]


def gather2_wait(h, after, name):
    def body(src_ref, land_ref, send1, recv1, send2, recv2, after_ref, src_dead, got_ref):
        for i in range(4):
            _stage1_copy(i, src_ref, land_ref, send1, recv1).wait_send()
        _stage1_copy(0, src_ref, land_ref, send1, recv1).wait_recv()
        for j in range(3):
            cp = _stage2_copy(j, land_ref, send2, recv2)
            cp.wait_send()
            cp.wait_recv()

    return pl.pallas_call(
        body, name=name,
        out_shape=(pltpu.HBM(h["src"].shape, h["src"].dtype), pltpu.HBM(h["land"].shape, h["land"].dtype)),
        in_specs=(HBM, HBM, SEM, SEM, SEM, SEM, ANY), out_specs=(HBM, HBM), input_output_aliases={0: 0, 1: 1},
        compiler_params=pltpu.CompilerParams(has_side_effects=EFFECT),
    )(h["src"], h["land"], h["send1"], h["recv1"], h["send2"], h["recv2"], after)[1]


def exchange_wait(handles, after, gather, name):
    send_sems, recv_sems, src_thru, land_thru = handles

    def body(src_ref, land_ref, send_sems, recv_sems, after_ref, src_dead, got_ref):
        for cp in _peer_copies(src_ref, land_ref, send_sems, recv_sems, gather):
            cp.wait_send()
            cp.wait_recv()

    return pl.pallas_call(
        body, name=name,
        out_shape=(pltpu.HBM(src_thru.shape, src_thru.dtype), pltpu.HBM(land_thru.shape, land_thru.dtype)),
        in_specs=(HBM, HBM, SEM, SEM, ANY), out_specs=(HBM, HBM), input_output_aliases={0: 0, 1: 1},
        compiler_params=pltpu.CompilerParams(has_side_effects=EFFECT),
    )(src_thru, land_thru, send_sems, recv_sems, after)


def local_step(x, tgt, mod, started, get_w, put_grad, wc, wf, g_mix, bc, lg, lb, gco, gao, g_ffn, bf, g_fin):
    w_in = get_w("w_in", mod)
    proj, h1 = rms_mod_matmul(x, g_mix, mod, 0, 1, w_in, D_IN // N_DEV, "proj_fwd", after=started)
    mix_a, u1 = conv_module_fwd(proj, wc, bc, lg, lb, gco, "conv_module_fwd")
    att, lse = attn_fwd_all(proj, "attn_fwd")
    w_out = get_w("w_out", att)
    y1, mixed = norm_concat_matmul(mix_a, att, gao, w_out, "out_proj_fwd")
    w_up = get_w("w_up", y1)
    up0, x1, h2 = resid_rms_mod_matmul(x, y1, g_ffn, mod, 2, 3, 4, w_up, "up_fwd")
    get_w("w_down", up0, only_pass_on=True)
    act = ffn_act_fwd(up0, wf, bf, "ffn_act_fwd")
    w_down = get_w("w_down", act)
    loss_t, dx2, dy2, d_gfin, d_gaf = matmul_loss_bwd(act, w_down, x1, tgt, g_fin, mod, 5, "down_fwd_loss")
    dact = matmul(dy2, w_down, "nt", F32, 512, FFN_TN, "down_bwd_x")
    dw_down = matmul(act, dy2, "tn", BF16, 256, D_MODEL, "down_bwd_w")
    dup0, dbf_g, dbf_v, dwf_g, dwf_v = ffn_bwd(up0, dact, wf, bf, "ffn_bwd", after=put_grad("w_down", dw_down))
    dw_up = matmul_tn_halves(dup0, h2, 256, "up_bwd_w")
    dx1, d_shf, d_scf, d_gffn, dy1, d_gam = matmul_rms_mod_bwd(
        dup0, w_up, x1, dx2, g_ffn, mod, 4, y1, 2, "up_bwd_x", after=put_grad("w_up", dw_up))
    dw_out = matmul(mixed, dy1, "tn", BF16, 256, D_MODEL, "out_proj_bwd_w")
    dmixed, do, dd, d_gao = matmul_combine_bwd(dy1, w_out, att, gao, "out_proj_bwd_x", after=put_grad("w_out", dw_out))
    dqkv = attn_bwd_all(proj, do, lse, dd, "attn_bwd")
    du1, d_gco, d_lg, d_lb, d_bc, d_wc = conv_module_bwd_a(proj, u1, dmixed, lg, lb, gco, "conv_module_bwd_a")
    dproj_a = conv_module_bwd_b(proj, du1, wc, "conv_module_bwd_b")
    dw_in = matmul_tn_pieces(dproj_a, dqkv, h1, "proj_bwd_w")
    dx, d_shm, d_scm, d_gmix = matmul_rms_mod_bwd(
        (dproj_a, dqkv), w_in, x, dx1, g_mix, mod, 1, None, 0, "proj_bwd_x", b_rows=D_IN // N_DEV,
        after=put_grad("w_in", dw_in))
    dmod = jnp.concatenate([d_shm, d_scm, d_gam, d_shf, d_scf, d_gaf], axis=1)
    small = dict(g_norm_mix=d_gmix, b_conv_dw=d_bc, ln_conv_g=d_lg, ln_conv_b=d_lb, g_conv_out=d_gco, g_attn_out=d_gao,
                 g_norm_ffn=d_gffn, b_ffn_dw=jnp.concatenate([dbf_g, dbf_v], axis=1), g_final=d_gfin,
                 w_conv_dw=d_wc, w_ffn_dw=jnp.concatenate([dwf_g, dwf_v], axis=1), dmod=dmod, loss=loss_t[0:1, 0:1])
    return dx, small


PACK_W = 7168
TAPS_PER_ROW = PACK_W // D_CONV
PACKED_AT = dict(
    b_ada=(0, 0, N_MOD * D_MODEL), g_norm_mix=(0, 6144, D_MODEL),
    b_ffn_dw=(1, 0, 2 * D_FF), g_norm_ffn=(1, 5632, D_MODEL), b_conv_dw=(1, 6656, D_CONV),
    g_final=(2, 5632, D_MODEL), ln_conv_g=(2, 6656, D_CONV),
    ln_conv_b=(3, 5632, D_CONV), g_conv_out=(3, 6144, D_CONV), g_attn_out=(3, 6656, D_ATTN))
SMALL_ORDER = list(PACKED_AT)
LOSS_AT = (4, 2 * D_FF)


def pack_small(t):
    cat = lambda *parts: jnp.concatenate(parts, axis=1)
    wf = t["w_ffn_dw"]
    taps = jnp.pad(t["w_conv_dw"].reshape(1, CONV_K * D_CONV), ((0, 0), (0, 3 * PACK_W - CONV_K * D_CONV)))
    return jnp.concatenate([
        cat(t["dmod"], t["g_norm_mix"]),
        cat(t["b_ffn_dw"], t["g_norm_ffn"], t["b_conv_dw"]),
        cat(wf[0:1], t["g_final"], t["ln_conv_g"]),
        cat(wf[1:2], t["ln_conv_b"], t["g_conv_out"], t["g_attn_out"]),
        cat(wf[2:3], jnp.pad(t["loss"], ((0, 0), (0, PACK_W - 2 * D_FF - 1)))),
        taps.reshape(3, PACK_W)], axis=0)


def small_adamw(parts, wmv, name):
    def body(*refs):
        p_ref = refs[0]
        ins = refs[1:1 + 3 * len(SMALL_ORDER)]
        outs = refs[1 + 3 * len(SMALL_ORDER):]
        g = p_ref[0]
        for k in range(1, N_DEV):
            g = g + p_ref[k]
        for i, n in enumerate(SMALL_ORDER):
            row, lane, width = PACKED_AT[n]
            gp = g[row:row + 1, lane:lane + width]
            w_ref, m_ref, v_ref = ins[3 * i:3 * i + 3]
            g_ref, d_ref, nm_ref, nv_ref = outs[4 * i:4 * i + 4]
            g_ref[...] = gp
            d_ref[...], nm_ref[...], nv_ref[...] = _adam(w_ref[...], gp, m_ref[...], v_ref[...])
        wc_ref, wf_ref, loss_ref = outs[4 * len(SMALL_ORDER):]
        for j in range(CONV_K):
            row, lane = 5 + j // TAPS_PER_ROW, (j % TAPS_PER_ROW) * D_CONV
            wc_ref[j:j + 1, :] = g[row:row + 1, lane:lane + D_CONV]
        wf_ref[...] = g[2:2 + FFN_K, 0:2 * D_FF]
        loss_ref[...] = jnp.broadcast_to(g[LOSS_AT[0]:LOSS_AT[0] + 1, LOSS_AT[1]:LOSS_AT[1] + 1], (8, 128))

    args, out_shape = [parts], []
    for n in SMALL_ORDER:
        args += list(wmv[n])
        out_shape += [jax.ShapeDtypeStruct(wmv[n][0].shape, F32)] * 4
    out_shape += [jax.ShapeDtypeStruct((CONV_K, D_CONV), F32), jax.ShapeDtypeStruct((FFN_K, 2 * D_FF), F32),
                  jax.ShapeDtypeStruct((8, 128), F32)]
    res = pl.pallas_call(body, out_shape=tuple(out_shape), name=name, compiler_params=_cp())(*args)
    per = {n: tuple(res[4 * i:4 * i + 4]) for i, n in enumerate(SMALL_ORDER)}
    return per, res[-3], res[-2], res[-1][0, 0]


def shard_adamw(items, name):
    def body(*refs):
        ins, outs = refs[:4 * len(items)], refs[4 * len(items):]
        for i in range(len(items)):
            g_ref, w_ref, m_ref, v_ref = ins[4 * i:4 * i + 4]
            og_ref, d_ref, nm_ref, nv_ref = outs[4 * i:4 * i + 4]
            og_ref[...] = g_ref[...]
            d_ref[...], nm_ref[...], nv_ref[...] = _adam(w_ref[...], g_ref[...], m_ref[...], v_ref[...])

    args = [a for item in items for a in item]
    out_shape = tuple(jax.ShapeDtypeStruct(item[1].shape, F32) for item in items for _ in range(4))
    res = pl.pallas_call(body, out_shape=out_shape, name=name, compiler_params=_cp())(*args)
    return [tuple(res[4 * i:4 * i + 4]) for i in range(len(items))]


def _shard(full, n_cols, me):
    return lax.dynamic_slice(full, (0, me * n_cols), (full.shape[0], n_cols))


WEIGHTS = ["w_ada", "b_ada", "g_norm_mix", "w_in", "w_conv_dw", "b_conv_dw", "ln_conv_g", "ln_conv_b", "g_conv_out",
           "g_attn_out", "w_out", "g_norm_ffn", "w_up", "w_ffn_dw", "b_ffn_dw", "w_down", "g_final"]


def kernel(x, c, w_ada, b_ada, g_norm_mix, w_in, w_conv_dw, b_conv_dw, ln_conv_g, ln_conv_b, g_conv_out, g_attn_out, w_out, g_norm_ffn, w_up, w_ffn_dw, b_ffn_dw, w_down, g_final, loss_target, m_w_ada, m_b_ada, m_g_norm_mix, m_w_in, m_w_conv_dw, m_b_conv_dw, m_ln_conv_g, m_ln_conv_b, m_g_conv_out, m_g_attn_out, m_w_out, m_g_norm_ffn, m_w_up, m_w_ffn_dw, m_b_ffn_dw, m_w_down, m_g_final, v_w_ada, v_b_ada, v_g_norm_mix, v_w_in, v_w_conv_dw, v_b_conv_dw, v_ln_conv_g, v_ln_conv_b, v_g_conv_out, v_g_attn_out, v_w_out, v_g_norm_ffn, v_w_up, v_w_ffn_dw, v_b_ffn_dw, v_w_down, v_g_final):
    args = dict(locals())
    me = 4 * lax.axis_index("x") + 2 * lax.axis_index("y") + lax.axis_index("c")
    me1 = me.astype(jnp.int32).reshape(1)

    def flat(name, prefix=""):
        a = args[prefix + name]
        return a.reshape(a.shape[-2] if a.ndim > 1 else 1, a.shape[-1])

    def flat_t(name, prefix=""):
        return args[prefix + name][0].T

    n_in, n_up, r_out, r_down = w_in.shape[2], w_up.shape[2], w_out.shape[1], w_down.shape[1]
    n_ada, n_wc, n_wf = w_ada.shape[2], w_conv_dw.shape[2], w_ffn_dw.shape[2]
    taps_c = jnp.pad(flat("w_conv_dw").reshape(1, CONV_K * n_wc), ((0, 0), (0, 2 * D_MODEL - CONV_K * n_wc)))
    taps_f = jnp.pad(flat("w_ffn_dw").reshape(1, FFN_K * n_wf), ((0, 0), (0, 3 * D_MODEL - FFN_K * n_wf)))
    first = jnp.concatenate([c, taps_c.reshape(2, D_MODEL), taps_f.reshape(3, D_MODEL), jnp.zeros((2, D_MODEL), F32)], axis=0)
    w_in_block = flat_t("w_in").astype(BF16)
    hi = lax.reduce_precision(first, 8, 7)
    mid = lax.reduce_precision(first - hi, 8, 7)
    low = lax.reduce_precision(first - hi - mid, 8, 7)
    terms = jnp.concatenate([hi, mid, low, jnp.zeros((8, D_MODEL), F32)], axis=0).astype(BF16)
    first_block = all_gather(jnp.concatenate([w_in_block, terms], axis=0), "gather_c_taps_w_in")
    terms = first_block[:, n_in:n_in + 24, :].astype(F32)
    first_all = (terms[:, 0:8] + terms[:, 8:16]) + terms[:, 16:24]
    c_all = first_all[:, 0, :]
    wc_full = first_all[:, 1:3, :].reshape(N_DEV, 2 * D_MODEL)[:, :CONV_K * n_wc].reshape(N_DEV, CONV_K, n_wc)
    wc_full = wc_full.transpose(1, 0, 2).reshape(CONV_K, D_CONV)
    wf_full = first_all[:, 3:6, :].reshape(N_DEV, 3 * D_MODEL)[:, :FFN_K * n_wf].reshape(N_DEV, FFN_K, n_wf)
    wf_full = wf_full.transpose(1, 0, 2).reshape(FFN_K, 2 * D_FF)
    mod_cols = ada_fwd(c_all, flat("w_ada"), _shard(flat("b_ada"), n_ada, me), "ada_fwd")
    mod_all = all_gather(mod_cols, "gather_mod")
    mod = lax.dynamic_index_in_dim(mod_all, me, axis=1, keepdims=False).reshape(N_MOD, D_MODEL)
    mod = jnp.pad(mod, ((0, 2), (0, 0)))

    order = ("w_out", "w_up", "w_down")
    blocks = dict(w_up=flat_t("w_up").astype(BF16), w_out=flat("w_out").astype(BF16), w_down=flat("w_down").astype(BF16))
    handles, tok = gather2_start([blocks[name] for name in order], "gather_weights_start", mod_all)
    gathers = dict(zip(order, handles))

    def pass_on(name, after):
        if "send2" not in gathers[name]:
            group = ("w_out", "w_up") if name != "w_down" else ("w_down",)
            gathers.update(zip(group, gather2_pass([gathers[w] for w in group], after, f"gather_{name}_pass")))

    def gathered(name, after):
        pass_on(name, after)
        land = gather2_wait(gathers[name], after, f"gather_{name}_wait")
        return lax.dynamic_update_index_in_dim(land, blocks[name], me, axis=0)

    def get_w(name, after, only_pass_on=False):
        if name == "w_in":
            return first_block
        if only_pass_on:
            return pass_on(name, after)
        return gathered(name, after).reshape(-1, D_MODEL)

    exchanges = {}

    def put_grad(name, dw, after=None):
        dev_major = dw.reshape(N_DEV, -1, D_MODEL)
        (exchanges[name],), token = exchange_start([dev_major], False, f"exchange_{name}_start", after)
        return token

    grad_x, small = local_step(
        x[0], loss_target[0], mod, tok, get_w, put_grad, wc_full, wf_full,
        flat("g_norm_mix"), flat("b_conv_dw"), flat("ln_conv_g"), flat("ln_conv_b"), flat("g_conv_out"),
        flat("g_attn_out"), flat("g_norm_ffn"), flat("b_ffn_dw"), flat("g_final"))

    out = {}

    def finish(name, tr, after):
        mine, parts = exchange_wait(exchanges[name], after, False, f"exchange_{name}_wait")
        if name in ("w_in", "w_up"):
            res = sum_adamw(parts, mine, me1, flat_t(name), flat_t(name, "m_"), flat_t(name, "v_"), tr, "adamw_" + name)
            out[name] = tuple(r.T for r in res)
        else:
            res = out[name] = sum_adamw(parts, mine, me1, flat(name), flat(name, "m_"), flat(name, "v_"), tr,
                                        "adamw_" + name)
        return res[0]

    after = finish("w_down", r_down, grad_x)
    after = finish("w_up", n_up // 2, after)
    after = finish("w_out", r_out, after)
    after = finish("w_in", n_in, after)

    small_all = all_gather(pack_small(small), "gather_small", after)

    wmv = {n: (flat(n), flat(n, "m_"), flat(n, "v_")) for n in SMALL_ORDER}
    per, g_wc, g_wf, loss = small_adamw(small_all, wmv, "adamw_small")
    out.update(per)
    taps = shard_adamw([(_shard(g_wc, n_wc, me), flat("w_conv_dw"), flat("w_conv_dw", "m_"), flat("w_conv_dw", "v_")),
                        (_shard(g_wf, n_wf, me), flat("w_ffn_dw"), flat("w_ffn_dw", "m_"), flat("w_ffn_dw", "v_"))],
                       "adamw_taps")
    out["w_conv_dw"], out["w_ffn_dw"] = taps

    dmod_cols = _shard(small_all[:, 0, :], n_ada, me)
    out["w_ada"] = ada_bwd_adamw(c_all, dmod_cols, flat("w_ada"), flat("w_ada", "m_"), flat("w_ada", "v_"), "adamw_w_ada")

    result = [loss, grad_x[None]]
    for k in range(4):
        result += [out[n][k].reshape(args[n].shape) for n in WEIGHTS]
    return tuple(result)
```

```python
import jax
import jax.numpy as jnp
from jax import lax
from jax.experimental import pallas as pl
from jax.experimental.pallas import tpu as pltpu

F32 = jnp.float32
BF16 = jnp.bfloat16

N_DEV = 8
SEQ = 2048
D_MODEL = 1024
D_CONV = 512
D_ATTN = 512
HEAD_DIM = 64
CONV_K = 31
D_FF = 2816
FFN_K = 3
D_IN = 2 * D_CONV + 3 * D_ATTN
N_MOD = 6
EPS = 1e-6
ATTN_BLOCK = 128
PATTERNS = ((2048, 1), (512, 4), (128, 16))
NEG = -1e30

ADAM_LR, ADAM_B1, ADAM_B2, ADAM_EPS, ADAM_WD, ADAM_STEP = 0.001, 0.9, 0.999, 1e-08, 0.01, 10

ROWS = 256
CONV_HALO = 32
FFN_HALO = 8
FFN_TN = 1408
VMEM_LIMIT = 56 * 1024 * 1024


NT = (((1,), (1,)), ((), ()))
ANY = pl.BlockSpec(memory_space=pl.ANY)


def _cp(*sem):
    return pltpu.CompilerParams(dimension_semantics=sem if sem else None, vmem_limit_bytes=VMEM_LIMIT)


def _with_after(body, n_in, after):
    if after is None:
        return body, [], []
    return (lambda *refs: body(*refs[:n_in], *refs[n_in + 1:])), [ANY], [after]


def _sig(x):
    return 1.0 / (1.0 + jnp.exp(-x))


def _rsum(x):
    return jnp.sum(x, axis=0, keepdims=True)


def _mean(x):
    return jnp.mean(x, axis=-1, keepdims=True)


def _acc(ref, val, first):
    @pl.when(first)
    def _():
        ref[...] = val

    @pl.when(jnp.logical_not(first))
    def _():
        ref[...] += val


SUB = 16


def _for_chunks(fn, unroll=1, rows=ROWS):
    def step(i, carry):
        fn(pl.ds(pl.multiple_of(i * SUB, SUB), SUB))
        return carry

    lax.fori_loop(0, rows // SUB, step, 0, unroll=unroll)


PAIR = 2 * ROWS


def _pair_spec(width, col=0):
    return pl.BlockSpec((PAIR, width), lambda i: (i, col))


def _halves():
    return [slice(h * ROWS, (h + 1) * ROWS) for h in range(2)]


def rms_mod_matmul(x, g, mod, sh_row, sc_row, b, b_rows, name, after=None):
    n = N_DEV * b_rows

    def body(x_ref, g_ref, mod_ref, b_ref, o_ref, h_ref):
        for rs in _halves():
            xx = x_ref[rs, :]
            r = lax.rsqrt(_mean(xx * xx) + EPS)
            h = (xx * r * g_ref[...] * (1.0 + mod_ref[sc_row:sc_row + 1, :]) + mod_ref[sh_row:sh_row + 1, :]).astype(BF16)
            h_ref[rs, :] = h
            o_ref[rs, :] = lax.dot_general(h, b_ref[...].reshape(n, D_MODEL), NT, preferred_element_type=F32)

    body, more_specs, more = _with_after(body, 4, after)
    return pl.pallas_call(
        body, out_shape=(jax.ShapeDtypeStruct((SEQ, n), F32), jax.ShapeDtypeStruct((SEQ, D_MODEL), BF16)),
        grid=(SEQ // PAIR,),
        in_specs=[_pair_spec(D_MODEL), _vec_spec(D_MODEL), _vec_spec(D_MODEL, 8),
                  pl.BlockSpec((N_DEV, b_rows, D_MODEL), lambda i: (0, 0, 0))] + more_specs,
        out_specs=(_pair_spec(n), _pair_spec(D_MODEL)), name=name, compiler_params=_cp("parallel"))(x, g, mod, b, *more)


def norm_concat_matmul(mix_a, att, gao, w, name):
    def body(a_ref, att_ref, g_ref, w_ref, y_ref, mixed_ref):
        for rs in _halves():
            aa = att_ref[rs, :]
            mixed_ref[rs, 0:D_CONV] = a_ref[rs, :]
            mixed_ref[rs, D_CONV:] = (aa * lax.rsqrt(_mean(aa * aa) + EPS) * g_ref[...]).astype(BF16)
            y_ref[rs, :] = jnp.dot(mixed_ref[rs, :], w_ref[...], preferred_element_type=F32)

    return pl.pallas_call(
        body, out_shape=(jax.ShapeDtypeStruct((SEQ, D_MODEL), F32), jax.ShapeDtypeStruct((SEQ, D_MODEL), BF16)),
        grid=(SEQ // PAIR,),
        in_specs=[_pair_spec(D_CONV), _pair_spec(D_ATTN), _vec_spec(D_ATTN), pl.BlockSpec(w.shape, lambda i: (0, 0))],
        out_specs=(_pair_spec(D_MODEL), _pair_spec(D_MODEL)), name=name, compiler_params=_cp("parallel"))(mix_a, att, gao, w)


def resid_rms_mod_matmul(x, y, g, mod, ga_row, sh_row, sc_row, w, name):
    n = w.shape[0]

    def body(x_ref, y_ref, g_ref, mod_ref, w_ref, o_ref, x1_ref, h_ref):
        x1 = x_ref[...] + mod_ref[ga_row:ga_row + 1, :] * y_ref[...]
        x1_ref[...] = x1
        r = lax.rsqrt(_mean(x1 * x1) + EPS)
        h = (x1 * r * g_ref[...] * (1.0 + mod_ref[sc_row:sc_row + 1, :]) + mod_ref[sh_row:sh_row + 1, :]).astype(BF16)
        h_ref[...] = h
        o_ref[...] = lax.dot_general(h, w_ref[...], NT, preferred_element_type=F32)

    return pl.pallas_call(
        body,
        out_shape=(jax.ShapeDtypeStruct((SEQ, n), F32), jax.ShapeDtypeStruct((SEQ, D_MODEL), F32),
                   jax.ShapeDtypeStruct((SEQ, D_MODEL), BF16)),
        grid=(SEQ // ROWS,),
        in_specs=[_row_spec(D_MODEL), _row_spec(D_MODEL), _vec_spec(D_MODEL), _vec_spec(D_MODEL, 8),
                  pl.BlockSpec(w.shape, lambda i: (0, 0))],
        out_specs=(_row_spec(n), _row_spec(D_MODEL), _row_spec(D_MODEL)),
        name=name, compiler_params=_cp("parallel"))(x, y, g, mod, w)


def matmul_combine_bwd(dy, w, att, gao, name, after=None):
    def body(dy_ref, w_ref, att_ref, g_ref, dm_ref, do_ref, dd_ref, dg_ref):
        @pl.when(pl.program_id(0) == 0)
        def _():
            dg_ref[...] = jnp.zeros_like(dg_ref)

        same_head = (jnp.right_shift(lax.broadcasted_iota(jnp.int32, (D_ATTN, D_ATTN), 0), 6)
                     == jnp.right_shift(lax.broadcasted_iota(jnp.int32, (D_ATTN, D_ATTN), 1), 6)).astype(F32)
        for rs in _halves():
            dmixed = lax.dot_general(dy_ref[rs, :], w_ref[...], NT, preferred_element_type=F32)
            dm_ref[rs, :] = dmixed
            att = att_ref[rs, :]
            r = lax.rsqrt(_mean(att * att) + EPS)
            xn = att * r
            dm = dmixed[:, D_CONV:]
            dg_ref[...] += _rsum(dm * xn)
            dyn = dm * g_ref[...]
            do = r * (dyn - xn * _mean(dyn * xn))
            do_ref[rs, :] = do
            dd_ref[rs, :] = jnp.dot(do * att, same_head, preferred_element_type=F32, precision=lax.Precision.HIGHEST)

    rs = _pair_spec(D_ATTN)
    f = jax.ShapeDtypeStruct((SEQ, D_ATTN), F32)
    body, more_specs, more = _with_after(body, 4, after)
    return pl.pallas_call(
        body, out_shape=(jax.ShapeDtypeStruct((SEQ, D_MODEL), F32), f, f, jax.ShapeDtypeStruct((1, D_ATTN), F32)),
        grid=(SEQ // PAIR,),
        in_specs=[_pair_spec(D_MODEL), pl.BlockSpec(w.shape, lambda i: (0, 0)), rs, _vec_spec(D_ATTN)] + more_specs,
        out_specs=(_pair_spec(D_MODEL), rs, rs, _vec_spec(D_ATTN)),
        name=name, compiler_params=_cp("arbitrary"))(dy, w, att, gao, *more)


def matmul_loss_bwd(act, w, x1, tgt, g, mod, ga_row, name):
    def body(a_ref, w_ref, x1_ref, t_ref, g_ref, mod_ref, loss_ref, dx2_ref, dy2_ref, dg_ref, dga_ref):
        @pl.when(pl.program_id(0) == 0)
        def _():
            loss_ref[...] = jnp.zeros_like(loss_ref)
            dg_ref[...] = jnp.zeros_like(dg_ref)
            dga_ref[...] = jnp.zeros_like(dga_ref)

        ga = mod_ref[ga_row:ga_row + 1, :]
        for rs in _halves():
            y2 = jnp.dot(a_ref[rs, :], w_ref[...], preferred_element_type=F32)
            x2 = x1_ref[rs, :] + ga * y2
            r = lax.rsqrt(_mean(x2 * x2) + EPS)
            xn = x2 * r
            err = xn * g_ref[...] - t_ref[rs, :]
            loss_ref[...] += jnp.broadcast_to(0.5 * jnp.sum(_mean(err * err)), (8, 128))
            dy = err * (1.0 / D_MODEL)
            dg_ref[...] += _rsum(dy * xn)
            dxn = dy * g_ref[...]
            dx2 = r * (dxn - xn * _mean(dxn * xn))
            dx2_ref[rs, :] = dx2
            dy2_ref[rs, :] = (dx2 * ga).astype(BF16)
            dga_ref[...] += _rsum(dx2 * y2)

    vec = jax.ShapeDtypeStruct((1, D_MODEL), F32)
    rows = _pair_spec
    return pl.pallas_call(
        body,
        out_shape=(jax.ShapeDtypeStruct((8, 128), F32), jax.ShapeDtypeStruct((SEQ, D_MODEL), F32),
                   jax.ShapeDtypeStruct((SEQ, D_MODEL), BF16), vec, vec),
        grid=(SEQ // PAIR,),
        in_specs=[rows(act.shape[1]), pl.BlockSpec(w.shape, lambda i: (0, 0)), rows(D_MODEL), rows(D_MODEL),
                  _vec_spec(D_MODEL), _vec_spec(D_MODEL, 8)],
        out_specs=(pl.BlockSpec((8, 128), lambda i: (0, 0)), rows(D_MODEL), rows(D_MODEL),
                   _vec_spec(D_MODEL), _vec_spec(D_MODEL)),
        name=name, compiler_params=_cp("arbitrary"))(act, w, x1, tgt, g, mod)


def matmul_rms_mod_bwd(a, b, x, dres, g, mod, sc_row, y, ga_row, name, b_rows=None, after=None):
    gated = y is not None
    pieces = isinstance(a, tuple)
    tm = PAIR if pieces else ROWS
    blocks = [slice(h * ROWS, (h + 1) * ROWS) for h in range(tm // ROWS)]
    rows = lambda width: pl.BlockSpec((tm, width), lambda i: (i, 0))
    if pieces:
        a1, a3 = a
        a_args = [a1, a3]
        a_specs = [rows(a1.shape[1]), pl.BlockSpec((3, tm, a3.shape[2]), lambda i: (0, i, 0))]
        b_arg, b_spec = b, pl.BlockSpec((N_DEV, b_rows, D_MODEL), lambda i: (0, 0, 0))
    else:
        k2 = a.shape[2]
        a_args = [a]
        a_specs = [pl.BlockSpec((2, tm, k2), lambda i: (0, i, 0))]
        b_arg, b_spec = b.reshape(2, k2, D_MODEL), pl.BlockSpec((2, k2, D_MODEL), lambda i: (0, 0, 0))
    n_a = len(a_args)

    def body(*refs):
        a_refs, (b_ref, x_ref, dres_ref, g_ref, mod_ref) = refs[:n_a], refs[n_a:n_a + 5]
        if gated:
            y_ref, dx_ref, dsh_ref, dsc_ref, dg_ref, dy_ref, dga_ref = refs[n_a + 5:]
        else:
            dx_ref, dsh_ref, dsc_ref, dg_ref = refs[n_a + 5:]

        @pl.when(pl.program_id(0) == 0)
        def _():
            for ref in (dsh_ref, dsc_ref, dg_ref) + ((dga_ref,) if gated else ()):
                ref[...] = jnp.zeros_like(ref)

        gg = g_ref[...]
        for rs in blocks:
            if pieces:
                wv = b_ref[...].reshape(N_DEV * b_rows, D_MODEL)
                k1, k3 = a_refs[0].shape[1], a_refs[1].shape[2]
                dh = jnp.dot(a_refs[0][rs, :], wv[0:k1], preferred_element_type=F32)
                for t in range(3):
                    dh = dh + jnp.dot(a_refs[1][t, rs, :], wv[k1 + t * k3:k1 + (t + 1) * k3], preferred_element_type=F32)
            else:
                dh = (jnp.dot(a_refs[0][0, rs, :], b_ref[0], preferred_element_type=F32)
                      + jnp.dot(a_refs[0][1, rs, :], b_ref[1], preferred_element_type=F32))
            xx = x_ref[rs, :]
            r = lax.rsqrt(_mean(xx * xx) + EPS)
            xn = xx * r
            dsh_ref[...] += _rsum(dh)
            dsc_ref[...] += _rsum(dh * (xn * gg))
            dt = dh * (1.0 + mod_ref[sc_row:sc_row + 1, :])
            dg_ref[...] += _rsum(dt * xn)
            dxn = dt * gg
            dx = dres_ref[rs, :] + r * (dxn - xn * _mean(dxn * xn))
            dx_ref[rs, :] = dx
            if gated:
                dga_ref[...] += _rsum(dx * y_ref[rs, :])
                dy_ref[rs, :] = (dx * mod_ref[ga_row:ga_row + 1, :]).astype(BF16)

    vec = jax.ShapeDtypeStruct((1, D_MODEL), F32)
    in_specs = a_specs + [b_spec, rows(D_MODEL), rows(D_MODEL), _vec_spec(D_MODEL), _vec_spec(D_MODEL, 8)]
    out_shape = [jax.ShapeDtypeStruct((SEQ, D_MODEL), F32), vec, vec, vec]
    out_specs = [rows(D_MODEL), _vec_spec(D_MODEL), _vec_spec(D_MODEL), _vec_spec(D_MODEL)]
    args = a_args + [b_arg, x, dres, g, mod]
    if gated:
        in_specs.append(rows(D_MODEL))
        out_shape += [jax.ShapeDtypeStruct((SEQ, D_MODEL), BF16), vec]
        out_specs += [rows(D_MODEL), _vec_spec(D_MODEL)]
        args.append(y)
    body, more_specs, more = _with_after(body, len(args), after)
    return pl.pallas_call(
        body, out_shape=tuple(out_shape), grid=(SEQ // tm,), in_specs=in_specs + more_specs, out_specs=tuple(out_specs),
        name=name, compiler_params=_cp("arbitrary"))(*args, *more)


def _prev_halo(halo, width, col):
    per = ROWS // halo
    return pl.BlockSpec((halo, width), lambda i: (jnp.maximum(i * per - 1, 0), col))


def _next_halo(halo, width, col):
    per = ROWS // halo
    last = SEQ // halo - 1
    return pl.BlockSpec((halo, width), lambda i: (jnp.minimum((i + 1) * per, last), col))


CONV_PAD = ROWS + CONV_HALO


def _shift_copies(sh):
    for b in range(1, 8):
        sh[b, 0:CONV_PAD - 8, :] = sh[0, pl.ds(b, CONV_PAD - 8), :]


def _tap(sh, rs_start, offset):
    return sh[offset % 8, pl.ds(pl.multiple_of(rs_start + (offset // 8) * 8, 8), SUB), :]


def _conv_glu(av_ref, ag_ref, avh_ref, agh_ref, sh):
    i = pl.program_id(0)
    hv = avh_ref[...] * _sig(agh_ref[...])
    sh[0, 0:CONV_HALO, :] = jnp.where(i > 0, hv, 0.0)

    def glu(rs):
        sh[0, pl.ds(pl.multiple_of(rs.start + CONV_HALO, SUB), SUB), :] = av_ref[rs, :] * _sig(ag_ref[rs, :])

    _for_chunks(glu)
    _shift_copies(sh)


def _conv_norm(u1, lg_ref, lb_ref):
    mu = _mean(u1)
    cen = u1 - mu
    rs = lax.rsqrt(_mean(cen * cen) + EPS)
    z = cen * rs
    ln = z * lg_ref[...] + lb_ref[...]
    s = _sig(ln)
    return z, rs, ln, s, ln * s


def conv_module_fwd(proj, wc, bc, lg, lb, gco, name):
    def body(av_ref, ag_ref, avh_ref, agh_ref, wc_ref, bc_ref, lg_ref, lb_ref, gco_ref, out_ref, u1_ref, sh):
        _conv_glu(av_ref, ag_ref, avh_ref, agh_ref, sh)

        def conv(rs):
            u1 = jnp.broadcast_to(bc_ref[...], (SUB, D_CONV))
            for j in range(CONV_K):
                u1 = u1 + wc_ref[j:j + 1, :] * _tap(sh, rs.start, CONV_HALO - (CONV_K - 1) + j)
            u1_ref[rs, :] = u1

        _for_chunks(conv)
        _, _, _, _, u2 = _conv_norm(u1_ref[...], lg_ref, lb_ref)
        rc = lax.rsqrt(_mean(u2 * u2) + EPS)
        out_ref[...] = (u2 * rc * gco_ref[...]).astype(BF16)

    v = _vec_spec(D_CONV)
    return pl.pallas_call(
        body, out_shape=(jax.ShapeDtypeStruct((SEQ, D_CONV), BF16), jax.ShapeDtypeStruct((SEQ, D_CONV), F32)),
        grid=(SEQ // ROWS,),
        in_specs=[_row_spec(D_CONV, 0), _row_spec(D_CONV, 1), _prev_halo(CONV_HALO, D_CONV, 0),
                  _prev_halo(CONV_HALO, D_CONV, 1), _vec_spec(D_CONV, CONV_K), v, v, v, v],
        out_specs=(_row_spec(D_CONV), _row_spec(D_CONV)), scratch_shapes=[pltpu.VMEM((8, CONV_PAD, D_CONV), F32)],
        name=name, compiler_params=_cp("parallel"))(proj, proj, proj, proj, wc, bc, lg, lb, gco)


def conv_module_bwd_a(proj, u1, dmixed, lg, lb, gco, name):
    def body(av_ref, ag_ref, avh_ref, agh_ref, u1_ref, dm_ref, lg_ref, lb_ref, gco_ref,
             du1_ref, dgco_ref, dlg_ref, dlb_ref, dbc_ref, dwc_ref, sh, acc):
        first = pl.program_id(0) == 0
        _conv_glu(av_ref, ag_ref, avh_ref, agh_ref, sh)
        z, rs, ln, s, u2 = _conv_norm(u1_ref[...], lg_ref, lb_ref)
        rc = lax.rsqrt(_mean(u2 * u2) + EPS)
        xn = u2 * rc
        dm = dm_ref[...]
        _acc(dgco_ref, _rsum(dm * xn), first)
        dyn = dm * gco_ref[...]
        du2 = rc * (dyn - xn * _mean(dyn * xn))
        dln = du2 * (s * (1.0 + ln * (1.0 - s)))
        _acc(dlg_ref, _rsum(dln * z), first)
        _acc(dlb_ref, _rsum(dln), first)
        dz = dln * lg_ref[...]
        du1 = rs * (dz - _mean(dz) - z * _mean(dz * z))
        du1_ref[...] = du1
        _acc(dbc_ref, _rsum(du1), first)
        acc[...] = jnp.zeros_like(acc)

        def taps(rs):
            d = du1_ref[rs, :]
            for j in range(CONV_K):
                acc[j] += d * _tap(sh, rs.start, CONV_HALO - (CONV_K - 1) + j)

        _for_chunks(taps)

        @pl.when(first)
        def _():
            dwc_ref[...] = jnp.zeros_like(dwc_ref)

        for j in range(CONV_K):
            dwc_ref[j:j + 1, :] += _rsum(acc[j])

    v = _vec_spec(D_CONV)
    vec = jax.ShapeDtypeStruct((1, D_CONV), F32)
    return pl.pallas_call(
        body,
        out_shape=(jax.ShapeDtypeStruct((SEQ, D_CONV), F32), vec, vec, vec, vec, jax.ShapeDtypeStruct((CONV_K, D_CONV), F32)),
        grid=(SEQ // ROWS,),
        in_specs=[_row_spec(D_CONV, 0), _row_spec(D_CONV, 1), _prev_halo(CONV_HALO, D_CONV, 0),
                  _prev_halo(CONV_HALO, D_CONV, 1), _row_spec(D_CONV, 0), _row_spec(D_CONV, 0), v, v, v],
        out_specs=(_row_spec(D_CONV), v, v, v, v, _vec_spec(D_CONV, CONV_K)),
        scratch_shapes=[pltpu.VMEM((8, CONV_PAD, D_CONV), F32), pltpu.VMEM((CONV_K, SUB, D_CONV), F32)],
        name=name, compiler_params=_cp("arbitrary"))(proj, proj, proj, proj, u1, dmixed, lg, lb, gco)


def conv_module_bwd_b(proj, du1, wc, name):
    def body(av_ref, ag_ref, du1_ref, du1n_ref, wc_ref, out_ref, sh):
        i = pl.program_id(0)
        sh[0, 0:ROWS, :] = du1_ref[...]
        sh[0, ROWS:, :] = jnp.where(i < SEQ // ROWS - 1, du1n_ref[...], 0.0)
        _shift_copies(sh)

        def chunk(rs):
            du0 = jnp.zeros((SUB, D_CONV), F32)
            for j in range(CONV_K):
                du0 = du0 + wc_ref[j:j + 1, :] * _tap(sh, rs.start, CONV_K - 1 - j)
            sg = _sig(ag_ref[rs, :])
            out_ref[rs, 0:D_CONV] = (du0 * sg).astype(BF16)
            out_ref[rs, D_CONV:] = (du0 * av_ref[rs, :] * sg * (1.0 - sg)).astype(BF16)

        _for_chunks(chunk)

    return pl.pallas_call(
        body, out_shape=jax.ShapeDtypeStruct((SEQ, 2 * D_CONV), BF16), grid=(SEQ // ROWS,),
        in_specs=[_row_spec(D_CONV, 0), _row_spec(D_CONV, 1), _row_spec(D_CONV, 0), _next_halo(CONV_HALO, D_CONV, 0),
                  _vec_spec(D_CONV, CONV_K)],
        out_specs=_row_spec(2 * D_CONV), scratch_shapes=[pltpu.VMEM((8, CONV_PAD, D_CONV), F32)],
        name=name, compiler_params=_cp("parallel"))(proj, proj, du1, du1, wc)


def _rows(start, size, r):
    return pl.ds(start, size) if r == 1 else pl.ds(start, size, stride=r)


def _unit_rows(r, rho, n, nb):
    win = 2 * ATTN_BLOCK if nb > 1 else ATTN_BLOCK
    if isinstance(n, int):
        kb = max(n - 1, 0)
        q_rows = _rows(rho + r * ATTN_BLOCK * n, ATTN_BLOCK, r)
        k_rows = _rows(rho + r * ATTN_BLOCK * kb, win, r)
    else:
        kb = jnp.maximum(n - 1, 0)
        q_rows = pl.ds(pl.multiple_of(n * ATTN_BLOCK, ATTN_BLOCK), ATTN_BLOCK)
        k_rows = pl.ds(pl.multiple_of(kb * ATTN_BLOCK, ATTN_BLOCK), win)
    return q_rows, k_rows, n - kb


def _band_bias(first_block, transposed):
    shape = (2 * ATTN_BLOCK, ATTN_BLOCK) if transposed else (ATTN_BLOCK, 2 * ATTN_BLOCK)
    q_axis = 1 if transposed else 0
    dist = (0 if first_block else ATTN_BLOCK) + lax.broadcasted_iota(jnp.int32, shape, q_axis) \
        - lax.broadcasted_iota(jnp.int32, shape, 1 - q_axis)
    return jnp.where((dist >= 0) & (dist <= ATTN_BLOCK), 0.0, NEG)


def _per_head(x):
    lane = lax.broadcasted_iota(jnp.int32, x.shape, 1)
    zero = jnp.zeros_like(x)
    return [jnp.where(lane < HEAD_DIM, x, zero), jnp.where(lane >= HEAD_DIM, x, zero)]


SCALE = HEAD_DIM ** -0.5


def _masked_scores(q2, k2, bias):
    return [lax.dot_general(qh, k2, NT, preferred_element_type=F32) + bias for qh in _per_head(q2)]


def _attn_units(r, nb, unit):
    if r == 1:
        def four(i, carry):
            for k in range(4):
                unit(0, 4 * i + k)
            return carry
        lax.fori_loop(0, nb // 4, four, 0)
    else:
        for rho in range(r):
            for n in range(nb):
                unit(rho, n)


N_UNITS = 16


def attn_fwd_all(proj, name):
    def body(q_ref, k_ref, v_ref, att_ref, lse_ref, s_scr, p_scr, lse_scr, den_scr, bias_scr):
        bias_scr[0] = _band_bias(True, False)
        bias_scr[1] = _band_bias(False, False)
        for idx, (sub_len, r) in enumerate(PATTERNS):
            nb = sub_len // ATTN_BLOCK
            win = 2 * ATTN_BLOCK if nb > 1 else ATTN_BLOCK

            def scores(rho, n, r=r, nb=nb, win=win):
                u = rho * nb + n
                q_rows, k_rows, variant = _unit_rows(r, rho, n, nb)
                ss = _masked_scores((q_ref[q_rows, :] * SCALE).astype(BF16), k_ref[k_rows, :].astype(BF16),
                                    bias_scr[variant, :, 0:win])
                for h in range(2):
                    s_scr[2 * u + h, :, 0:win] = ss[h]

            _attn_units(r, nb, scores)

            def softmax(u, carry, win=win):
                lses, dens = [], []
                for h in range(2):
                    sc = s_scr[2 * u + h, :, 0:win]
                    m = jnp.max(sc, axis=1, keepdims=True)
                    p = jnp.exp(sc - m)
                    den = jnp.sum(p, axis=1, keepdims=True)
                    p_scr[2 * u + h, :, 0:win] = p.astype(BF16)
                    lses.append(jnp.broadcast_to(m + jnp.log(den), (ATTN_BLOCK, HEAD_DIM)))
                    dens.append(jnp.broadcast_to(den, (ATTN_BLOCK, HEAD_DIM)))
                lse_scr[u] = jnp.concatenate(lses, axis=1)
                den_scr[u] = jnp.concatenate(dens, axis=1)
                return carry

            lax.fori_loop(0, N_UNITS, softmax, 0, unroll=2)

            def outputs(rho, n, r=r, nb=nb, win=win, idx=idx):
                u = rho * nb + n
                q_rows, k_rows, _ = _unit_rows(r, rho, n, nb)
                vs = _per_head(v_ref[k_rows, :].astype(BF16))
                o = (jnp.dot(p_scr[2 * u, :, 0:win], vs[0], preferred_element_type=F32)
                     + jnp.dot(p_scr[2 * u + 1, :, 0:win], vs[1], preferred_element_type=F32)) / den_scr[u]
                lse = lse_scr[u]
                if idx > 0:
                    old = lse_ref[q_rows, :]
                    top = jnp.maximum(old, lse)
                    new = top + jnp.log(jnp.exp(old - top) + jnp.exp(lse - top))
                    o = att_ref[q_rows, :] * jnp.exp(old - new) + o * jnp.exp(lse - new)
                    lse = new
                att_ref[q_rows, :] = o
                lse_ref[q_rows, :] = lse

            _attn_units(r, nb, outputs)

    blk = lambda first: pl.BlockSpec((SEQ, 128), lambda g: (0, first + g))
    shp = jax.ShapeDtypeStruct((SEQ, D_ATTN), F32)
    big = (2 * N_UNITS, ATTN_BLOCK, 2 * ATTN_BLOCK)
    small = pltpu.VMEM((N_UNITS, ATTN_BLOCK, 128), F32)
    return pl.pallas_call(
        body, out_shape=(shp, shp), grid=(4,), in_specs=[blk(8), blk(12), blk(16)], out_specs=(blk(0), blk(0)),
        scratch_shapes=[pltpu.VMEM(big, F32), pltpu.VMEM(big, BF16), small, small,
                        pltpu.VMEM((2, ATTN_BLOCK, 2 * ATTN_BLOCK), F32)],
        name=name, compiler_params=_cp("parallel"))(proj, proj, proj)


def attn_bwd_all(proj, do, lse, dd, name):
    def body(q_ref, k_ref, v_ref, do_ref, l_ref, dd_ref, out_ref, dq_s, dk_s, dv_s,
             s_scr, dp_scr, ds_scr, st_scr, dpt_scr, pt_scr, dst_scr, qb_scr, kb_scr, dob_scr, bias_scr, bias_t_scr):
        for first_block in (True, False):
            bias_scr[1 - int(first_block)] = _band_bias(first_block, False)
            bias_t_scr[1 - int(first_block)] = _band_bias(first_block, True)
        dq_s[...] = jnp.zeros_like(dq_s)
        dk_s[...] = jnp.zeros_like(dk_s)
        dv_s[...] = jnp.zeros_like(dv_s)
        for sub_len, r in PATTERNS:
            nb = sub_len // ATTN_BLOCK
            win = 2 * ATTN_BLOCK if nb > 1 else ATTN_BLOCK

            def scores(rho, n, r=r, nb=nb, win=win):
                u = rho * nb + n
                q_rows, k_rows, variant = _unit_rows(r, rho, n, nb)
                bias, bias_t = bias_scr[variant, :, 0:win], bias_t_scr[variant, 0:win, :]
                q2 = (q_ref[q_rows, :] * SCALE).astype(BF16)
                kf = k_ref[k_rows, :]
                k2 = kf.astype(BF16)
                do2 = do_ref[q_rows, :].astype(BF16)
                qb_scr[u] = q2
                kb_scr[u, 0:win, :] = (kf * SCALE).astype(BF16)
                dob_scr[u] = do2
                l2 = l_ref[q_rows, :]
                d2 = dd_ref[q_rows, :]
                l2t = l2.T
                d2t = d2.T
                v2 = v_ref[k_rows, :].astype(BF16)
                qs, dos = _per_head(q2), _per_head(do2)
                for h in range(2):
                    c0 = h * HEAD_DIM
                    sc = lax.dot_general(qs[h], k2, NT, preferred_element_type=F32)
                    s_scr[2 * u + h, :, 0:win] = sc + bias - l2[:, c0:c0 + 1]
                    dp_scr[2 * u + h, :, 0:win] = lax.dot_general(dos[h], v2, NT, preferred_element_type=F32) \
                        - d2[:, c0:c0 + 1]
                    sct = lax.dot_general(k2, qs[h], NT, preferred_element_type=F32)
                    st_scr[2 * u + h, 0:win, :] = sct + bias_t - l2t[c0:c0 + 1, :]
                    dpt_scr[2 * u + h, 0:win, :] = lax.dot_general(v2, dos[h], NT, preferred_element_type=F32) \
                        - d2t[c0:c0 + 1, :]

            _attn_units(r, nb, scores)

            def pointwise(hu, carry, win=win):
                ds_scr[hu, :, 0:win] = (jnp.exp(s_scr[hu, :, 0:win]) * dp_scr[hu, :, 0:win]).astype(BF16)
                pt = jnp.exp(st_scr[hu, 0:win, :])
                pt_scr[hu, 0:win, :] = pt.astype(BF16)
                dst_scr[hu, 0:win, :] = (pt * dpt_scr[hu, 0:win, :]).astype(BF16)
                return carry

            lax.fori_loop(0, 2 * N_UNITS, pointwise, 0, unroll=4)

            def grads(rho, n, r=r, nb=nb, win=win):
                u = rho * nb + n
                q_rows, k_rows, _ = _unit_rows(r, rho, n, nb)
                qs, ks, dos = _per_head(qb_scr[u]), _per_head(kb_scr[u, 0:win, :]), _per_head(dob_scr[u])

                def both(scr, rows, rhs):
                    return (jnp.dot(scr[(2 * u,) + rows], rhs[0], preferred_element_type=F32)
                            + jnp.dot(scr[(2 * u + 1,) + rows], rhs[1], preferred_element_type=F32))

                dq_s[q_rows, :] += both(ds_scr, (slice(None), slice(0, win)), ks)
                dk_s[k_rows, :] += both(dst_scr, (slice(0, win), slice(None)), qs)
                dv_s[k_rows, :] += both(pt_scr, (slice(0, win), slice(None)), dos)

            _attn_units(r, nb, grads)
        out_ref[0] = dq_s[...].astype(BF16)
        out_ref[1] = dk_s[...].astype(BF16)
        out_ref[2] = dv_s[...].astype(BF16)

    blk = lambda first: pl.BlockSpec((SEQ, 128), lambda g: (0, first + g))
    acc = pltpu.VMEM((SEQ, 128), F32)
    big = (2 * N_UNITS, ATTN_BLOCK, 2 * ATTN_BLOCK)
    big_t = (2 * N_UNITS, 2 * ATTN_BLOCK, ATTN_BLOCK)
    return pl.pallas_call(
        body, out_shape=jax.ShapeDtypeStruct((3, SEQ, D_ATTN), BF16), grid=(4,),
        in_specs=[blk(8), blk(12), blk(16), blk(0), blk(0), blk(0)],
        out_specs=pl.BlockSpec((3, SEQ, 128), lambda g: (0, 0, g)),
        scratch_shapes=[acc, acc, acc, pltpu.VMEM(big, F32), pltpu.VMEM(big, F32), pltpu.VMEM(big, BF16),
                        pltpu.VMEM(big_t, F32), pltpu.VMEM(big_t, F32), pltpu.VMEM(big_t, BF16), pltpu.VMEM(big_t, BF16),
                        pltpu.VMEM((N_UNITS, ATTN_BLOCK, 128), BF16),
                        pltpu.VMEM((N_UNITS, 2 * ATTN_BLOCK, 128), BF16), pltpu.VMEM((N_UNITS, ATTN_BLOCK, 128), BF16),
                        pltpu.VMEM((2, ATTN_BLOCK, 2 * ATTN_BLOCK), F32), pltpu.VMEM((2, 2 * ATTN_BLOCK, ATTN_BLOCK), F32)],
        name=name, compiler_params=_cp("parallel"))(proj, proj, proj, do, lse, dd)


N_FT = D_FF // FFN_TN


def _ffn_specs():
    per = ROWS // FFN_HALO
    cur_g = pl.BlockSpec((ROWS, FFN_TN), lambda j, i: (i, j))
    cur_v = pl.BlockSpec((ROWS, FFN_TN), lambda j, i: (i, j + N_FT))
    halo_g = pl.BlockSpec((FFN_HALO, FFN_TN), lambda j, i: (jnp.maximum(i * per - 1, 0), j))
    halo_v = pl.BlockSpec((FFN_HALO, FFN_TN), lambda j, i: (jnp.maximum(i * per - 1, 0), j + N_FT))
    w_g = pl.BlockSpec((FFN_K, FFN_TN), lambda j, i: (0, j))
    w_v = pl.BlockSpec((FFN_K, FFN_TN), lambda j, i: (0, j + N_FT))
    b_g = pl.BlockSpec((1, FFN_TN), lambda j, i: (0, j))
    b_v = pl.BlockSpec((1, FFN_TN), lambda j, i: (0, j + N_FT))
    return [cur_g, cur_v, halo_g, halo_v, w_g, w_v, b_g, b_v]


def matmul(a, b, kind, out_dtype, tm, tn, name, b_rows=None):
    stacked = b_rows is not None
    b_shape = (N_DEV * b_rows, b.shape[2]) if stacked else b.shape
    if kind == "nn":
        (m, k), n = a.shape, b_shape[1]
        a_spec = pl.BlockSpec((tm, k), lambda j, i: (i, 0))
        b_spec = pl.BlockSpec((k, tn), lambda j, i: (0, j))
        dims = (((1,), (0,)), ((), ()))
    elif kind == "nt":
        (m, k), n = a.shape, b_shape[0]
        a_spec = pl.BlockSpec((tm, k), lambda j, i: (i, 0))
        b_spec = pl.BlockSpec((tn, k), lambda j, i: (j, 0))
        dims = (((1,), (1,)), ((), ()))
    else:
        (k, m), n = a.shape, b_shape[1]
        a_spec = pl.BlockSpec((k, tm), lambda j, i: (0, i))
        b_spec = pl.BlockSpec((k, tn), lambda j, i: (0, j))
        dims = (((0,), (0,)), ((), ()))
    assert m % tm == 0 and n % tn == 0, (name, m, n, tm, tn)
    if stacked:
        assert b_spec.block_shape[0] == b_shape[0] and kind in ("nn", "nt")
        width = b_spec.block_shape[1]
        b_spec = pl.BlockSpec((N_DEV, b_rows, width), (lambda j, i: (0, 0, j)) if kind == "nn" else (lambda j, i: (0, 0, 0)))

    def body(a_ref, b_ref, o_ref):
        bb = b_ref[...].reshape(b_shape[0], -1) if stacked else b_ref[...]
        o_ref[...] = lax.dot_general(a_ref[...], bb, dims, preferred_element_type=F32).astype(o_ref.dtype)

    return pl.pallas_call(
        body, out_shape=jax.ShapeDtypeStruct((m, n), out_dtype), grid=(n // tn, m // tm),
        in_specs=[a_spec, b_spec], out_specs=pl.BlockSpec((tm, tn), lambda j, i: (i, j)),
        name=name, compiler_params=_cp("parallel", "parallel"))(a, b)


def matmul_tn_pieces(a1, a3, b, name):
    k, n = b.shape
    tm = a3.shape[2]
    n1 = a1.shape[1] // tm

    def body(a1_ref, a3_ref, b_ref, o_ref):
        i = pl.program_id(0)
        tn_dims = (((0,), (0,)), ((), ()))

        @pl.when(i < n1)
        def _():
            o_ref[...] = lax.dot_general(a1_ref[...], b_ref[...], tn_dims, preferred_element_type=F32).astype(BF16)

        @pl.when(i >= n1)
        def _():
            o_ref[...] = lax.dot_general(a3_ref[0], b_ref[...], tn_dims, preferred_element_type=F32).astype(BF16)

    return pl.pallas_call(
        body, out_shape=jax.ShapeDtypeStruct((a1.shape[1] + 3 * tm, n), BF16), grid=(n1 + 3,),
        in_specs=[pl.BlockSpec((k, tm), lambda i: (0, jnp.minimum(i, n1 - 1))),
                  pl.BlockSpec((1, k, tm), lambda i: (jnp.maximum(i - n1, 0), 0, 0)),
                  pl.BlockSpec((k, n), lambda i: (0, 0))],
        out_specs=pl.BlockSpec((tm, n), lambda i: (i, 0)), name=name, compiler_params=_cp("parallel"))(a1, a3, b)


def matmul_tn_halves(a, b, tm, name):
    _, k, m = a.shape
    n = b.shape[1]

    def body(a_ref, b_ref, o_ref):
        o_ref[0] = lax.dot_general(a_ref[0], b_ref[...], (((0,), (0,)), ((), ())), preferred_element_type=F32).astype(BF16)

    return pl.pallas_call(
        body, out_shape=jax.ShapeDtypeStruct((2, m, n), BF16), grid=(2, m // tm),
        in_specs=[pl.BlockSpec((1, k, tm), lambda h, i: (h, 0, i)), pl.BlockSpec((k, n), lambda h, i: (0, 0))],
        out_specs=pl.BlockSpec((1, tm, n), lambda h, i: (h, i, 0)), name=name,
        compiler_params=_cp("parallel", "parallel"))(a, b).reshape(2 * m, n)


def _row_spec(width, col=0):
    return pl.BlockSpec((ROWS, width), lambda i: (i, col))


def _vec_spec(width, rows=1):
    return pl.BlockSpec((rows, width), lambda i: (0, 0))


FFN_RB = 64


def _ffn_lane_blocks(fn):
    for c in range(FFN_TN // 128):
        fn(slice(c * 128, (c + 1) * 128))


def _ffn_taps(cur_ref, halo_ref, head, ls, r0):
    if r0 == 0:
        head[0:FFN_HALO, :] = jnp.where(pl.program_id(1) > 0, halo_ref[:, ls], 0.0)
        head[FFN_HALO:, :] = cur_ref[0:FFN_RB, ls]
        return tuple(head[pl.ds(FFN_HALO - k, FFN_RB), :] for k in (2, 1, 0))
    return tuple(cur_ref[pl.ds(r0 - k, FFN_RB), ls] for k in (2, 1, 0))


def _ffn_conv(taps, w, b):
    return b + w[0] * taps[0] + w[1] * taps[1] + w[2] * taps[2]


def ffn_act_fwd(up0, wf, bf, name):
    def body(g_ref, v_ref, gh_ref, vh_ref, wg_ref, wv_ref, bg_ref, bv_ref, act_ref, head_g, head_v):
        def lanes(ls):
            wg = [wg_ref[t:t + 1, ls] for t in range(FFN_K)]
            wv = [wv_ref[t:t + 1, ls] for t in range(FFN_K)]
            for r0 in range(0, ROWS, FFN_RB):
                gate = _ffn_conv(_ffn_taps(g_ref, gh_ref, head_g, ls, r0), wg, bg_ref[:, ls])
                val = _ffn_conv(_ffn_taps(v_ref, vh_ref, head_v, ls, r0), wv, bv_ref[:, ls])
                act_ref[r0:r0 + FFN_RB, ls] = (gate * _sig(gate) * val).astype(BF16)

        _ffn_lane_blocks(lanes)

    head = pltpu.VMEM((FFN_HALO + FFN_RB, 128), F32)
    return pl.pallas_call(
        body, out_shape=jax.ShapeDtypeStruct((SEQ, D_FF), BF16), grid=(N_FT, SEQ // ROWS),
        in_specs=_ffn_specs(), out_specs=pl.BlockSpec((ROWS, FFN_TN), lambda j, i: (i, j)),
        scratch_shapes=[head, head], name=name, compiler_params=_cp("parallel", "parallel"))(up0, up0, up0, up0, wf, wf, bf, bf)


def ffn_bwd(up0, dact, wf, bf, name, after=None):
    per = ROWS // FFN_HALO
    last = SEQ // FFN_HALO - 1

    def body(g_ref, v_ref, gh_ref, vh_ref, wg_ref, wv_ref, bg_ref, bv_ref, da_ref, gn_ref, vn_ref, dan_ref,
             out_ref, dbg_ref, dbv_ref, dwg_ref, dwv_ref, pad, head_g, head_v, dgp, dvp, acc):
        i = pl.program_id(1)
        first = i == 0
        acc[...] = jnp.zeros_like(acc)

        def grads(gate, val, da):
            s = _sig(gate)
            return da * val * (s * (1.0 + gate * (1.0 - s))), da * (gate * s)

        fold = lambda q: jnp.sum(q.reshape(FFN_RB // 8, 8, 128), axis=0)

        def lanes(ls):
            wg = [wg_ref[t:t + 1, ls] for t in range(FFN_K)]
            wv = [wv_ref[t:t + 1, ls] for t in range(FFN_K)]
            sums = [jnp.zeros((8, 128), F32)] * (2 + 2 * FFN_K)
            for r0 in range(0, ROWS, FFN_RB):
                gs = _ffn_taps(g_ref, gh_ref, head_g, ls, r0)
                vs = _ffn_taps(v_ref, vh_ref, head_v, ls, r0)
                dgate, dval = grads(_ffn_conv(gs, wg, bg_ref[:, ls]), _ffn_conv(vs, wv, bv_ref[:, ls]),
                                    da_ref[r0:r0 + FFN_RB, ls])
                dgp[r0:r0 + FFN_RB, ls] = dgate
                dvp[r0:r0 + FFN_RB, ls] = dval
                new = [dgate, dval] + [dgate * gs[t] for t in range(FFN_K)] + [dval * vs[t] for t in range(FFN_K)]
                sums = [a + fold(q) for a, q in zip(sums, new)]
            for k in range(2 + 2 * FFN_K):
                acc[k, 0:8, ls] = sums[k]

        _ffn_lane_blocks(lanes)
        _acc(dbg_ref, _rsum(acc[0]), first)
        _acc(dbv_ref, _rsum(acc[1]), first)
        _acc(dwg_ref, jnp.concatenate([_rsum(acc[2 + t]) for t in range(FFN_K)], axis=0), first)
        _acc(dwv_ref, jnp.concatenate([_rsum(acc[5 + t]) for t in range(FFN_K)], axis=0), first)

        def conv_next(cur_ref, nxt_ref, w_ref, b_ref):
            pad[0:FFN_HALO, :] = cur_ref[ROWS - FFN_HALO:, :]
            pad[FFN_HALO:, :] = nxt_ref[...]
            return (b_ref[...] + w_ref[0:1, :] * pad[pl.ds(FFN_HALO - 2, FFN_HALO), :]
                    + w_ref[1:2, :] * pad[pl.ds(FFN_HALO - 1, FFN_HALO), :] + w_ref[2:3, :] * nxt_ref[...])

        gate_n = conv_next(g_ref, gn_ref, wg_ref, bg_ref)
        val_n = conv_next(v_ref, vn_ref, wv_ref, bv_ref)
        dgate_n, dval_n = grads(gate_n, val_n, dan_ref[...])
        inside = i < SEQ // ROWS - 1
        dgp[ROWS:, :] = jnp.where(inside, dgate_n, 0.0)
        dvp[ROWS:, :] = jnp.where(inside, dval_n, 0.0)

        def back(ls):
            for half, (dp, w_ref) in enumerate(((dgp, wg_ref), (dvp, wv_ref))):
                w = [w_ref[t:t + 1, ls] for t in range(FFN_K)]
                for r0 in range(0, ROWS, FFN_RB):
                    out_ref[half, r0:r0 + FFN_RB, ls] = (
                        w[2] * dp[r0:r0 + FFN_RB, ls] + w[1] * dp[pl.ds(r0 + 1, FFN_RB), ls]
                        + w[0] * dp[pl.ds(r0 + 2, FFN_RB), ls]).astype(BF16)

        _ffn_lane_blocks(back)

    body, more_specs, more = _with_after(body, 12, after)
    head = pltpu.VMEM((FFN_HALO + FFN_RB, 128), F32)
    ext = pltpu.VMEM((ROWS + FFN_HALO, FFN_TN), F32)
    vec = jax.ShapeDtypeStruct((1, D_FF), F32)
    taps = jax.ShapeDtypeStruct((FFN_K, D_FF), F32)
    cur = pl.BlockSpec((ROWS, FFN_TN), lambda j, i: (i, j))
    nxt = lambda off: pl.BlockSpec((FFN_HALO, FFN_TN), lambda j, i: (jnp.minimum((i + 1) * per, last), j + off))
    vs = pl.BlockSpec((1, FFN_TN), lambda j, i: (0, j))
    ts = pl.BlockSpec((FFN_K, FFN_TN), lambda j, i: (0, j))
    return pl.pallas_call(
        body, out_shape=(jax.ShapeDtypeStruct((2, SEQ, D_FF), BF16), vec, vec, taps, taps), grid=(N_FT, SEQ // ROWS),
        in_specs=_ffn_specs() + [cur, nxt(0), nxt(N_FT), nxt(0)] + more_specs,
        out_specs=(pl.BlockSpec((2, ROWS, FFN_TN), lambda j, i: (0, i, j)), vs, vs, ts, ts),
        scratch_shapes=[pltpu.VMEM((2 * FFN_HALO, FFN_TN), F32), head, head, ext, ext,
                        pltpu.VMEM((2 + 2 * FFN_K, SUB, FFN_TN), F32)],
        name=name, compiler_params=_cp("parallel", "arbitrary"))(up0, up0, up0, up0, wf, wf, bf, bf, dact, up0, up0, dact,
                                                                 *more)


def ada_fwd(c_all, w_ada, b_cols, name):
    def body(c_ref, w_ref, b_ref, o_ref):
        cc = c_ref[...]
        sc = (cc * _sig(cc)).astype(BF16)
        o_ref[...] = jnp.dot(sc, w_ref[...].astype(BF16), preferred_element_type=F32) + b_ref[...]

    return pl.pallas_call(body, out_shape=jax.ShapeDtypeStruct((N_DEV, w_ada.shape[1]), F32), name=name,
                          compiler_params=_cp())(c_all, w_ada, b_cols)


def _adam(w, g, m, v):
    m = ADAM_B1 * m + (1.0 - ADAM_B1) * g
    v = ADAM_B2 * v + (1.0 - ADAM_B2) * (g * g)
    m_hat = m / (1.0 - ADAM_B1 ** ADAM_STEP)
    v_hat = v / (1.0 - ADAM_B2 ** ADAM_STEP)
    delta = -ADAM_LR * (m_hat / (jnp.sqrt(v_hat) + ADAM_EPS) + ADAM_WD * w)
    return delta, m, v


def ada_bwd_adamw(c_all, dmod_cols, w, m, v, name):
    rows, cols = w.shape
    tr = 256

    def body(c_ref, dm_ref, w_ref, m_ref, v_ref, g_ref, d_ref, nm_ref, nv_ref):
        cc = c_ref[...]
        sc = (cc * _sig(cc)).T
        g = sc[:, 0:1] * dm_ref[0:1, :]
        for b in range(1, N_DEV):
            g = g + sc[:, b:b + 1] * dm_ref[b:b + 1, :]
        g_ref[...] = g
        d_ref[...], nm_ref[...], nv_ref[...] = _adam(w_ref[...], g, m_ref[...], v_ref[...])

    blk = pl.BlockSpec((tr, cols), lambda i: (i, 0))
    shp = jax.ShapeDtypeStruct((rows, cols), F32)
    return pl.pallas_call(
        body, out_shape=(shp, shp, shp, shp), grid=(rows // tr,),
        in_specs=[pl.BlockSpec((N_DEV, tr), lambda i: (0, i)), pl.BlockSpec((N_DEV, cols), lambda i: (0, 0)), blk, blk, blk],
        out_specs=(blk, blk, blk, blk), name=name, compiler_params=_cp("parallel"))(c_all, dmod_cols, w, m, v)


def sum_adamw(parts, mine, me, w, m, v, tr, name):
    n_parts, rows, cols = parts.shape

    def body(me_ref, p_ref, own_ref, w_ref, m_ref, v_ref, g_ref, d_ref, nm_ref, nv_ref):
        def chunk(rs):
            g = own_ref[0, rs, :].astype(F32)
            for k in range(1, n_parts):
                g = g + p_ref[k, rs, :].astype(F32)
            g_ref[rs, :] = g
            d_ref[rs, :], nm_ref[rs, :], nv_ref[rs, :] = _adam(w_ref[rs, :], g, m_ref[rs, :], v_ref[rs, :])

        _for_chunks(chunk, 2, tr)

    blk = pl.BlockSpec((tr, cols), lambda i, me_ref: (i, 0))
    shp = jax.ShapeDtypeStruct((rows, cols), F32)
    grid_spec = pltpu.PrefetchScalarGridSpec(
        num_scalar_prefetch=1, grid=(rows // tr,),
        in_specs=[pl.BlockSpec((n_parts, tr, cols), lambda i, me_ref: (0, i, 0)),
                  pl.BlockSpec((1, tr, cols), lambda i, me_ref: (me_ref[0], i, 0)), blk, blk, blk],
        out_specs=(blk, blk, blk, blk))
    return pl.pallas_call(body, out_shape=(shp, shp, shp, shp), grid_spec=grid_spec, name=name,
                          compiler_params=_cp("parallel"))(me, parts, mine, w, m, v)


MESH = pl.DeviceIdType.MESH


def all_gather(block, name, after=None):
    extra = () if after is None else (after,)

    def body(x_ref, *refs):
        out_ref, send_sems, recv_sems, local_sem = refs[len(extra):]
        x, y, c = lax.axis_index("x"), lax.axis_index("y"), lax.axis_index("c")
        me, sibling = (x, y, c), (x, y, 1 - c)
        chips = [(1 - x, y), (x, 1 - y), (1 - x, 1 - y)]

        def slot(px, py, pc):
            return out_ref.at[4 * px + 2 * py + pc]

        def copy(k, blk, to, src=None):
            return pltpu.make_async_remote_copy(
                src_ref=slot(*blk) if src is None else src, dst_ref=slot(*blk),
                send_sem=send_sems.at[k], recv_sem=recv_sems.at[k], device_id=to, device_id_type=MESH)

        mine = pltpu.make_async_copy(x_ref, slot(*me), local_sem)
        mine.start()
        first = [copy(0, me, sibling, src=x_ref)]
        first += [copy(1 + j, me, (*chip, c), src=x_ref) for j, chip in enumerate(chips)]
        for cp in first:
            cp.start()
        passed = [copy(4 + j, (*chip, c), sibling) for j, chip in enumerate(chips)]
        for j, chip in enumerate(chips):
            copy(1 + j, (*chip, c), me).wait_recv()
            passed[j].start()
        copy(0, sibling, me).wait_recv()
        for j, chip in enumerate(chips):
            copy(4 + j, (*chip, 1 - c), me).wait_recv()
        for cp in first + passed:
            cp.wait_send()
        mine.wait()

    return pl.pallas_call(
        body, out_shape=jax.ShapeDtypeStruct((N_DEV,) + block.shape, block.dtype), in_specs=[ANY] * (1 + len(extra)), out_specs=ANY,
        scratch_shapes=[pltpu.SemaphoreType.DMA((7,)), pltpu.SemaphoreType.DMA((7,)), pltpu.SemaphoreType.DMA],
        name=name)(block, *extra)


HBM = pl.BlockSpec(memory_space=pltpu.HBM)
SEM = pl.BlockSpec(memory_space=pltpu.SEMAPHORE)
EFFECT = pltpu.SideEffectType.DATAFLOW_SIDE_EFFECTING


def _peer_copies(src_ref, land_ref, send_sems, recv_sems, gather):
    x, y, c = lax.axis_index("x"), lax.axis_index("y"), lax.axis_index("c")
    me = 4 * x + 2 * y + c
    copies = []
    for k in range(1, N_DEV):
        px = 1 - x if k & 4 else x
        py = 1 - y if k & 2 else y
        pc = 1 - c if k & 1 else c
        copies.append(pltpu.make_async_remote_copy(
            src_ref=src_ref if gather else src_ref.at[4 * px + 2 * py + pc],
            dst_ref=land_ref.at[me] if gather else land_ref.at[k],
            send_sem=send_sems.at[k - 1], recv_sem=recv_sems.at[k - 1], device_id=(px, py, pc), device_id_type=MESH))
    return copies


def exchange_start(srcs, gather, name, after=None):
    n = len(srcs)
    land_shapes = [(N_DEV,) + src.shape if gather else src.shape for src in srcs]
    extra = () if after is None else (after,)

    def body(*refs):
        src_refs, land_refs = refs[0:n], refs[n:2 * n]
        outs = refs[2 * n + len(extra):]
        for k in range(n):
            for cp in _peer_copies(src_refs[k], land_refs[k], outs[4 * k], outs[4 * k + 1], gather):
                cp.start()
        token = outs[4 * n]
        token[...] = jnp.zeros_like(token)

    out_shape, out_specs, aliases = [], [], {}
    for k, src in enumerate(srcs):
        out_shape += [pltpu.SemaphoreType.DMA((N_DEV - 1,)), pltpu.SemaphoreType.DMA((N_DEV - 1,)),
                      pltpu.HBM(src.shape, src.dtype), pltpu.HBM(land_shapes[k], src.dtype)]
        out_specs += [SEM, SEM, HBM, HBM]
        aliases[k] = 4 * k + 2
        aliases[n + k] = 4 * k + 3
    out_shape.append(jax.ShapeDtypeStruct((8, 128), F32))
    out_specs.append(pl.BlockSpec(memory_space=pltpu.VMEM))
    res = pl.pallas_call(
        body, name=name, out_shape=tuple(out_shape), in_specs=(HBM,) * (2 * n) + (ANY,) * len(extra),
        out_specs=tuple(out_specs), input_output_aliases=aliases,
        compiler_params=pltpu.CompilerParams(has_side_effects=EFFECT),
    )(*[pltpu.with_memory_space_constraint(src, pltpu.HBM) for src in srcs],
      *[pltpu.with_memory_space_constraint(lax.empty(shp, src.dtype), pltpu.HBM) for shp, src in zip(land_shapes, srcs)],
      *extra)
    return [tuple(res[4 * k:4 * k + 4]) for k in range(n)], res[4 * n]


def _stage1_peer(i):
    x, y, c = lax.axis_index("x"), lax.axis_index("y"), lax.axis_index("c")
    if i == 0:
        return (x, y, 1 - c)
    return (1 - x if i & 1 else x, 1 - y if i & 2 else y, c)


def _slot_of(peer):
    return 4 * peer[0] + 2 * peer[1] + peer[2]


def _stage1_copy(i, src_ref, land_ref, send_sems, recv_sems):
    me = _slot_of((lax.axis_index("x"), lax.axis_index("y"), lax.axis_index("c")))
    return pltpu.make_async_remote_copy(src_ref=src_ref, dst_ref=land_ref.at[me], send_sem=send_sems.at[i],
                                        recv_sem=recv_sems.at[i], device_id=_stage1_peer(i), device_id_type=MESH)


def _stage2_copy(j, land_ref, send_sems, recv_sems):
    slot = _slot_of(_stage1_peer(j + 1))
    return pltpu.make_async_remote_copy(src_ref=land_ref.at[slot], dst_ref=land_ref.at[slot], send_sem=send_sems.at[j],
                                        recv_sem=recv_sems.at[j], device_id=_stage1_peer(0), device_id_type=MESH)


def gather2_start(srcs, name, after=None):
    n = len(srcs)
    extra = () if after is None else (after,)

    def body(*refs):
        src_refs, land_refs = refs[0:n], refs[n:2 * n]
        outs = refs[2 * n + len(extra):]
        for k in range(n):
            for i in range(4):
                _stage1_copy(i, src_refs[k], land_refs[k], outs[4 * k], outs[4 * k + 1]).start()
        outs[4 * n][...] = jnp.zeros((8, 128), F32)

    out_shape, out_specs, aliases = [], [], {}
    for k, src in enumerate(srcs):
        out_shape += [pltpu.SemaphoreType.DMA((4,)), pltpu.SemaphoreType.DMA((4,)),
                      pltpu.HBM(src.shape, src.dtype), pltpu.HBM((N_DEV,) + src.shape, src.dtype)]
        out_specs += [SEM, SEM, HBM, HBM]
        aliases[k] = 4 * k + 2
        aliases[n + k] = 4 * k + 3
    out_shape.append(jax.ShapeDtypeStruct((8, 128), F32))
    out_specs.append(pl.BlockSpec(memory_space=pltpu.VMEM))
    res = pl.pallas_call(
        body, name=name, out_shape=tuple(out_shape), in_specs=(HBM,) * (2 * n) + (ANY,) * len(extra),
        out_specs=tuple(out_specs), input_output_aliases=aliases,
        compiler_params=pltpu.CompilerParams(has_side_effects=EFFECT),
    )(*[pltpu.with_memory_space_constraint(src, pltpu.HBM) for src in srcs],
      *[pltpu.with_memory_space_constraint(lax.empty((N_DEV,) + src.shape, src.dtype), pltpu.HBM) for src in srcs], *extra)
    return [dict(send1=res[4 * k], recv1=res[4 * k + 1], src=res[4 * k + 2], land=res[4 * k + 3]) for k in range(n)], \
        res[4 * n]


def gather2_pass(handles, after, name):
    n = len(handles)

    def body(*refs):
        src_refs, land_refs, recv1 = refs[0:n], refs[n:2 * n], refs[2 * n:3 * n]
        outs = refs[3 * n + 1:]
        for k in range(n):
            for j in range(3):
                _stage1_copy(j + 1, src_refs[k], land_refs[k], recv1[k], recv1[k]).wait_recv()
                _stage2_copy(j, land_refs[k], outs[3 * k], outs[3 * k + 1]).start()

    out_shape, out_specs, aliases = [], [], {}
    for k, h in enumerate(handles):
        out_shape += [pltpu.SemaphoreType.DMA((3,)), pltpu.SemaphoreType.DMA((3,)), pltpu.HBM(h["land"].shape, h["land"].dtype)]
        out_specs += [SEM, SEM, HBM]
        aliases[n + k] = 3 * k + 2
    res = pl.pallas_call(
        body, name=name, out_shape=tuple(out_shape), in_specs=(HBM,) * (2 * n) + (SEM,) * n + (ANY,),
        out_specs=tuple(out_specs), input_output_aliases=aliases,
        compiler_params=pltpu.CompilerParams(has_side_effects=EFFECT),
    )(*[h["src"] for h in handles], *[h["land"] for h in handles], *[h["recv1"] for h in handles], after)
    return [dict(h, send2=res[3 * k], recv2=res[3 * k + 1], land=res[3 * k + 2]) for k, h in enumerate(handles)]


def gather2_wait(h, after, name):
    def body(src_ref, land_ref, send1, recv1, send2, recv2, after_ref, src_dead, got_ref):
        for i in range(4):
            _stage1_copy(i, src_ref, land_ref, send1, recv1).wait_send()
        _stage1_copy(0, src_ref, land_ref, send1, recv1).wait_recv()
        for j in range(3):
            cp = _stage2_copy(j, land_ref, send2, recv2)
            cp.wait_send()
            cp.wait_recv()

    return pl.pallas_call(
        body, name=name,
        out_shape=(pltpu.HBM(h["src"].shape, h["src"].dtype), pltpu.HBM(h["land"].shape, h["land"].dtype)),
        in_specs=(HBM, HBM, SEM, SEM, SEM, SEM, ANY), out_specs=(HBM, HBM), input_output_aliases={0: 0, 1: 1},
        compiler_params=pltpu.CompilerParams(has_side_effects=EFFECT),
    )(h["src"], h["land"], h["send1"], h["recv1"], h["send2"], h["recv2"], after)[1]


def exchange_wait(handles, after, gather, name):
    send_sems, recv_sems, src_thru, land_thru = handles

    def body(src_ref, land_ref, send_sems, recv_sems, after_ref, src_dead, got_ref):
        for cp in _peer_copies(src_ref, land_ref, send_sems, recv_sems, gather):
            cp.wait_send()
            cp.wait_recv()

    return pl.pallas_call(
        body, name=name,
        out_shape=(pltpu.HBM(src_thru.shape, src_thru.dtype), pltpu.HBM(land_thru.shape, land_thru.dtype)),
        in_specs=(HBM, HBM, SEM, SEM, ANY), out_specs=(HBM, HBM), input_output_aliases={0: 0, 1: 1},
        compiler_params=pltpu.CompilerParams(has_side_effects=EFFECT),
    )(src_thru, land_thru, send_sems, recv_sems, after)


def local_step(x, tgt, mod, started, get_w, put_grad, wc, wf, g_mix, bc, lg, lb, gco, gao, g_ffn, bf, g_fin):
    w_in = get_w("w_in", mod)
    proj, h1 = rms_mod_matmul(x, g_mix, mod, 0, 1, w_in, D_IN // N_DEV, "proj_fwd", after=started)
    mix_a, u1 = conv_module_fwd(proj, wc, bc, lg, lb, gco, "conv_module_fwd")
    att, lse = attn_fwd_all(proj, "attn_fwd")
    w_out = get_w("w_out", att)
    y1, mixed = norm_concat_matmul(mix_a, att, gao, w_out, "out_proj_fwd")
    w_up = get_w("w_up", y1)
    up0, x1, h2 = resid_rms_mod_matmul(x, y1, g_ffn, mod, 2, 3, 4, w_up, "up_fwd")
    get_w("w_down", up0, only_pass_on=True)
    act = ffn_act_fwd(up0, wf, bf, "ffn_act_fwd")
    w_down = get_w("w_down", act)
    loss_t, dx2, dy2, d_gfin, d_gaf = matmul_loss_bwd(act, w_down, x1, tgt, g_fin, mod, 5, "down_fwd_loss")
    dact = matmul(dy2, w_down, "nt", F32, 512, FFN_TN, "down_bwd_x")
    dw_down = matmul(act, dy2, "tn", BF16, 256, D_MODEL, "down_bwd_w")
    dup0, dbf_g, dbf_v, dwf_g, dwf_v = ffn_bwd(up0, dact, wf, bf, "ffn_bwd", after=put_grad("w_down", dw_down))
    dw_up = matmul_tn_halves(dup0, h2, 256, "up_bwd_w")
    dx1, d_shf, d_scf, d_gffn, dy1, d_gam = matmul_rms_mod_bwd(
        dup0, w_up, x1, dx2, g_ffn, mod, 4, y1, 2, "up_bwd_x", after=put_grad("w_up", dw_up))
    dw_out = matmul(mixed, dy1, "tn", BF16, 256, D_MODEL, "out_proj_bwd_w")
    dmixed, do, dd, d_gao = matmul_combine_bwd(dy1, w_out, att, gao, "out_proj_bwd_x", after=put_grad("w_out", dw_out))
    dqkv = attn_bwd_all(proj, do, lse, dd, "attn_bwd")
    du1, d_gco, d_lg, d_lb, d_bc, d_wc = conv_module_bwd_a(proj, u1, dmixed, lg, lb, gco, "conv_module_bwd_a")
    dproj_a = conv_module_bwd_b(proj, du1, wc, "conv_module_bwd_b")
    dw_in = matmul_tn_pieces(dproj_a, dqkv, h1, "proj_bwd_w")
    dx, d_shm, d_scm, d_gmix = matmul_rms_mod_bwd(
        (dproj_a, dqkv), w_in, x, dx1, g_mix, mod, 1, None, 0, "proj_bwd_x", b_rows=D_IN // N_DEV,
        after=put_grad("w_in", dw_in))
    dmod = jnp.concatenate([d_shm, d_scm, d_gam, d_shf, d_scf, d_gaf], axis=1)
    small = dict(g_norm_mix=d_gmix, b_conv_dw=d_bc, ln_conv_g=d_lg, ln_conv_b=d_lb, g_conv_out=d_gco, g_attn_out=d_gao,
                 g_norm_ffn=d_gffn, b_ffn_dw=jnp.concatenate([dbf_g, dbf_v], axis=1), g_final=d_gfin,
                 w_conv_dw=d_wc, w_ffn_dw=jnp.concatenate([dwf_g, dwf_v], axis=1), dmod=dmod, loss=loss_t[0:1, 0:1])
    return dx, small


PACK_W = 7168
TAPS_PER_ROW = PACK_W // D_CONV
PACKED_AT = dict(
    b_ada=(0, 0, N_MOD * D_MODEL), g_norm_mix=(0, 6144, D_MODEL),
    b_ffn_dw=(1, 0, 2 * D_FF), g_norm_ffn=(1, 5632, D_MODEL), b_conv_dw=(1, 6656, D_CONV),
    g_final=(2, 5632, D_MODEL), ln_conv_g=(2, 6656, D_CONV),
    ln_conv_b=(3, 5632, D_CONV), g_conv_out=(3, 6144, D_CONV), g_attn_out=(3, 6656, D_ATTN))
SMALL_ORDER = list(PACKED_AT)
LOSS_AT = (4, 2 * D_FF)


def pack_small(t):
    cat = lambda *parts: jnp.concatenate(parts, axis=1)
    wf = t["w_ffn_dw"]
    taps = jnp.pad(t["w_conv_dw"].reshape(1, CONV_K * D_CONV), ((0, 0), (0, 3 * PACK_W - CONV_K * D_CONV)))
    return jnp.concatenate([
        cat(t["dmod"], t["g_norm_mix"]),
        cat(t["b_ffn_dw"], t["g_norm_ffn"], t["b_conv_dw"]),
        cat(wf[0:1], t["g_final"], t["ln_conv_g"]),
        cat(wf[1:2], t["ln_conv_b"], t["g_conv_out"], t["g_attn_out"]),
        cat(wf[2:3], jnp.pad(t["loss"], ((0, 0), (0, PACK_W - 2 * D_FF - 1)))),
        taps.reshape(3, PACK_W)], axis=0)


def small_adamw(parts, wmv, name):
    def body(*refs):
        p_ref = refs[0]
        ins = refs[1:1 + 3 * len(SMALL_ORDER)]
        outs = refs[1 + 3 * len(SMALL_ORDER):]
        g = p_ref[0]
        for k in range(1, N_DEV):
            g = g + p_ref[k]
        for i, n in enumerate(SMALL_ORDER):
            row, lane, width = PACKED_AT[n]
            gp = g[row:row + 1, lane:lane + width]
            w_ref, m_ref, v_ref = ins[3 * i:3 * i + 3]
            g_ref, d_ref, nm_ref, nv_ref = outs[4 * i:4 * i + 4]
            g_ref[...] = gp
            d_ref[...], nm_ref[...], nv_ref[...] = _adam(w_ref[...], gp, m_ref[...], v_ref[...])
        wc_ref, wf_ref, loss_ref = outs[4 * len(SMALL_ORDER):]
        for j in range(CONV_K):
            row, lane = 5 + j // TAPS_PER_ROW, (j % TAPS_PER_ROW) * D_CONV
            wc_ref[j:j + 1, :] = g[row:row + 1, lane:lane + D_CONV]
        wf_ref[...] = g[2:2 + FFN_K, 0:2 * D_FF]
        loss_ref[...] = jnp.broadcast_to(g[LOSS_AT[0]:LOSS_AT[0] + 1, LOSS_AT[1]:LOSS_AT[1] + 1], (8, 128))

    args, out_shape = [parts], []
    for n in SMALL_ORDER:
        args += list(wmv[n])
        out_shape += [jax.ShapeDtypeStruct(wmv[n][0].shape, F32)] * 4
    out_shape += [jax.ShapeDtypeStruct((CONV_K, D_CONV), F32), jax.ShapeDtypeStruct((FFN_K, 2 * D_FF), F32),
                  jax.ShapeDtypeStruct((8, 128), F32)]
    res = pl.pallas_call(body, out_shape=tuple(out_shape), name=name, compiler_params=_cp())(*args)
    per = {n: tuple(res[4 * i:4 * i + 4]) for i, n in enumerate(SMALL_ORDER)}
    return per, res[-3], res[-2], res[-1][0, 0]


def shard_adamw(items, name):
    def body(*refs):
        ins, outs = refs[:4 * len(items)], refs[4 * len(items):]
        for i in range(len(items)):
            g_ref, w_ref, m_ref, v_ref = ins[4 * i:4 * i + 4]
            og_ref, d_ref, nm_ref, nv_ref = outs[4 * i:4 * i + 4]
            og_ref[...] = g_ref[...]
            d_ref[...], nm_ref[...], nv_ref[...] = _adam(w_ref[...], g_ref[...], m_ref[...], v_ref[...])

    args = [a for item in items for a in item]
    out_shape = tuple(jax.ShapeDtypeStruct(item[1].shape, F32) for item in items for _ in range(4))
    res = pl.pallas_call(body, out_shape=out_shape, name=name, compiler_params=_cp())(*args)
    return [tuple(res[4 * i:4 * i + 4]) for i in range(len(items))]


def _shard(full, n_cols, me):
    return lax.dynamic_slice(full, (0, me * n_cols), (full.shape[0], n_cols))


WEIGHTS = ["w_ada", "b_ada", "g_norm_mix", "w_in", "w_conv_dw", "b_conv_dw", "ln_conv_g", "ln_conv_b", "g_conv_out",
           "g_attn_out", "w_out", "g_norm_ffn", "w_up", "w_ffn_dw", "b_ffn_dw", "w_down", "g_final"]


def kernel(x, c, w_ada, b_ada, g_norm_mix, w_in, w_conv_dw, b_conv_dw, ln_conv_g, ln_conv_b, g_conv_out, g_attn_out, w_out, g_norm_ffn, w_up, w_ffn_dw, b_ffn_dw, w_down, g_final, loss_target, m_w_ada, m_b_ada, m_g_norm_mix, m_w_in, m_w_conv_dw, m_b_conv_dw, m_ln_conv_g, m_ln_conv_b, m_g_conv_out, m_g_attn_out, m_w_out, m_g_norm_ffn, m_w_up, m_w_ffn_dw, m_b_ffn_dw, m_w_down, m_g_final, v_w_ada, v_b_ada, v_g_norm_mix, v_w_in, v_w_conv_dw, v_b_conv_dw, v_ln_conv_g, v_ln_conv_b, v_g_conv_out, v_g_attn_out, v_w_out, v_g_norm_ffn, v_w_up, v_w_ffn_dw, v_b_ffn_dw, v_w_down, v_g_final):
    args = dict(locals())
    me = 4 * lax.axis_index("x") + 2 * lax.axis_index("y") + lax.axis_index("c")
    me1 = me.astype(jnp.int32).reshape(1)

    def flat(name, prefix=""):
        a = args[prefix + name]
        return a.reshape(a.shape[-2] if a.ndim > 1 else 1, a.shape[-1])

    def flat_t(name, prefix=""):
        return args[prefix + name][0].T

    n_in, n_up, r_out, r_down = w_in.shape[2], w_up.shape[2], w_out.shape[1], w_down.shape[1]
    n_ada, n_wc, n_wf = w_ada.shape[2], w_conv_dw.shape[2], w_ffn_dw.shape[2]
    taps_c = jnp.pad(flat("w_conv_dw").reshape(1, CONV_K * n_wc), ((0, 0), (0, 2 * D_MODEL - CONV_K * n_wc)))
    taps_f = jnp.pad(flat("w_ffn_dw").reshape(1, FFN_K * n_wf), ((0, 0), (0, 3 * D_MODEL - FFN_K * n_wf)))
    first = jnp.concatenate([c, taps_c.reshape(2, D_MODEL), taps_f.reshape(3, D_MODEL), jnp.zeros((2, D_MODEL), F32)], axis=0)
    w_in_block = flat_t("w_in").astype(BF16)
    hi = lax.reduce_precision(first, 8, 7)
    mid = lax.reduce_precision(first - hi, 8, 7)
    low = lax.reduce_precision(first - hi - mid, 8, 7)
    terms = jnp.concatenate([hi, mid, low, jnp.zeros((8, D_MODEL), F32)], axis=0).astype(BF16)
    first_block = all_gather(jnp.concatenate([w_in_block, terms], axis=0), "gather_c_taps_w_in")
    terms = first_block[:, n_in:n_in + 24, :].astype(F32)
    first_all = (terms[:, 0:8] + terms[:, 8:16]) + terms[:, 16:24]
    c_all = first_all[:, 0, :]
    wc_full = first_all[:, 1:3, :].reshape(N_DEV, 2 * D_MODEL)[:, :CONV_K * n_wc].reshape(N_DEV, CONV_K, n_wc)
    wc_full = wc_full.transpose(1, 0, 2).reshape(CONV_K, D_CONV)
    wf_full = first_all[:, 3:6, :].reshape(N_DEV, 3 * D_MODEL)[:, :FFN_K * n_wf].reshape(N_DEV, FFN_K, n_wf)
    wf_full = wf_full.transpose(1, 0, 2).reshape(FFN_K, 2 * D_FF)
    mod_cols = ada_fwd(c_all, flat("w_ada"), _shard(flat("b_ada"), n_ada, me), "ada_fwd")
    mod_all = all_gather(mod_cols, "gather_mod")
    mod = lax.dynamic_index_in_dim(mod_all, me, axis=1, keepdims=False).reshape(N_MOD, D_MODEL)
    mod = jnp.pad(mod, ((0, 2), (0, 0)))

    order = ("w_out", "w_up", "w_down")
    blocks = dict(w_up=flat_t("w_up").astype(BF16), w_out=flat("w_out").astype(BF16), w_down=flat("w_down").astype(BF16))
    handles, tok = gather2_start([blocks[name] for name in order], "gather_weights_start", mod_all)
    gathers = dict(zip(order, handles))

    def pass_on(name, after):
        if "send2" not in gathers[name]:
            group = ("w_out", "w_up") if name != "w_down" else ("w_down",)
            gathers.update(zip(group, gather2_pass([gathers[w] for w in group], after, f"gather_{name}_pass")))

    def gathered(name, after):
        pass_on(name, after)
        land = gather2_wait(gathers[name], after, f"gather_{name}_wait")
        return lax.dynamic_update_index_in_dim(land, blocks[name], me, axis=0)

    def get_w(name, after, only_pass_on=False):
        if name == "w_in":
            return first_block
        if only_pass_on:
            return pass_on(name, after)
        return gathered(name, after).reshape(-1, D_MODEL)

    exchanges = {}

    def put_grad(name, dw, after=None):
        dev_major = dw.reshape(N_DEV, -1, D_MODEL)
        (exchanges[name],), token = exchange_start([dev_major], False, f"exchange_{name}_start", after)
        return token

    grad_x, small = local_step(
        x[0], loss_target[0], mod, tok, get_w, put_grad, wc_full, wf_full,
        flat("g_norm_mix"), flat("b_conv_dw"), flat("ln_conv_g"), flat("ln_conv_b"), flat("g_conv_out"),
        flat("g_attn_out"), flat("g_norm_ffn"), flat("b_ffn_dw"), flat("g_final"))

    out = {}

    def finish(name, tr, after, then=None):
        mine, parts = exchange_wait(exchanges[name], after, False, f"exchange_{name}_wait")
        if then is not None:
            then(parts)
        if name in ("w_in", "w_up"):
            res = sum_adamw(parts, mine, me1, flat_t(name), flat_t(name, "m_"), flat_t(name, "v_"), tr, "adamw_" + name)
            out[name] = tuple(r.T for r in res)
        else:
            res = out[name] = sum_adamw(parts, mine, me1, flat(name), flat(name, "m_"), flat(name, "v_"), tr,
                                        "adamw_" + name)
        return res[0]

    after = finish("w_down", r_down, grad_x)
    after = finish("w_up", n_up // 2, after)
    after = finish("w_out", r_out, after)
    packed = pack_small(small)
    small_gather = []
    after = finish("w_in", n_in, after, then=lambda parts: small_gather.append(
        gather2_start([packed], "gather_small_start", parts)[0][0]))
    (handle,) = gather2_pass(small_gather, after, "gather_small_pass")
    small_all = lax.dynamic_update_index_in_dim(
        gather2_wait(handle, after, "gather_small_wait"), packed, me, axis=0)

    wmv = {n: (flat(n), flat(n, "m_"), flat(n, "v_")) for n in SMALL_ORDER}
    per, g_wc, g_wf, loss = small_adamw(small_all, wmv, "adamw_small")
    out.update(per)
    taps = shard_adamw([(_shard(g_wc, n_wc, me), flat("w_conv_dw"), flat("w_conv_dw", "m_"), flat("w_conv_dw", "v_")),
                        (_shard(g_wf, n_wf, me), flat("w_ffn_dw"), flat("w_ffn_dw", "m_"), flat("w_ffn_dw", "v_"))],
                       "adamw_taps")
    out["w_conv_dw"], out["w_ffn_dw"] = taps

    dmod_cols = _shard(small_all[:, 0, :], n_ada, me)
    out["w_ada"] = ada_bwd_adamw(c_all, dmod_cols, flat("w_ada"), flat("w_ada", "m_"), flat("w_ada", "v_"), "adamw_w_ada")

    result = [loss, grad_x[None]]
    for k in range(4):
        result += [out[n][k].reshape(args[n].shape) for n in WEIGHTS]
    return tuple(result)
```

```python
import jax
import jax.numpy as jnp
from jax import lax
from jax.experimental import pallas as pl
from jax.experimental.pallas import tpu as pltpu

F32 = jnp.float32
BF16 = jnp.bfloat16

N_DEV = 8
SEQ = 2048
D_MODEL = 1024
D_CONV = 512
D_ATTN = 512
HEAD_DIM = 64
CONV_K = 31
D_FF = 2816
FFN_K = 3
D_IN = 2 * D_CONV + 3 * D_ATTN
N_MOD = 6
EPS = 1e-6
ATTN_BLOCK = 128
PATTERNS = ((2048, 1), (512, 4), (128, 16))
NEG = -1e30

ADAM_LR, ADAM_B1, ADAM_B2, ADAM_EPS, ADAM_WD, ADAM_STEP = 0.001, 0.9, 0.999, 1e-08, 0.01, 10

ROWS = 256
CONV_HALO = 32
FFN_HALO = 8
FFN_TN = 1408
VMEM_LIMIT = 56 * 1024 * 1024


NT = (((1,), (1,)), ((), ()))
ANY = pl.BlockSpec(memory_space=pl.ANY)


def _cp(*sem):
    return pltpu.CompilerParams(dimension_semantics=sem if sem else None, vmem_limit_bytes=VMEM_LIMIT)


def _with_after(body, n_in, after):
    if after is None:
        return body, [], []
    return (lambda *refs: body(*refs[:n_in], *refs[n_in + 1:])), [ANY], [after]


def _sig(x):
    return 1.0 / (1.0 + jnp.exp(-x))


def _rsum(x):
    return jnp.sum(x, axis=0, keepdims=True)


def _mean(x):
    return jnp.mean(x, axis=-1, keepdims=True)


def _acc(ref, val, first):
    @pl.when(first)
    def _():
        ref[...] = val

    @pl.when(jnp.logical_not(first))
    def _():
        ref[...] += val


SUB = 16


def _for_chunks(fn, unroll=1, rows=ROWS):
    def step(i, carry):
        fn(pl.ds(pl.multiple_of(i * SUB, SUB), SUB))
        return carry

    lax.fori_loop(0, rows // SUB, step, 0, unroll=unroll)


PAIR = 2 * ROWS


def _pair_spec(width, col=0):
    return pl.BlockSpec((PAIR, width), lambda i: (i, col))


def _halves():
    return [slice(h * ROWS, (h + 1) * ROWS) for h in range(2)]


def rms_mod_matmul(x, g, mod, sh_row, sc_row, b, b_rows, name, after=None):
    n = N_DEV * b_rows

    def body(x_ref, g_ref, mod_ref, b_ref, o_ref, h_ref):
        for rs in _halves():
            xx = x_ref[rs, :]
            r = lax.rsqrt(_mean(xx * xx) + EPS)
            h = (xx * r * g_ref[...] * (1.0 + mod_ref[sc_row:sc_row + 1, :]) + mod_ref[sh_row:sh_row + 1, :]).astype(BF16)
            h_ref[rs, :] = h
            o_ref[rs, :] = lax.dot_general(h, b_ref[...].reshape(n, D_MODEL), NT, preferred_element_type=F32)

    body, more_specs, more = _with_after(body, 4, after)
    return pl.pallas_call(
        body, out_shape=(jax.ShapeDtypeStruct((SEQ, n), F32), jax.ShapeDtypeStruct((SEQ, D_MODEL), BF16)),
        grid=(SEQ // PAIR,),
        in_specs=[_pair_spec(D_MODEL), _vec_spec(D_MODEL), _vec_spec(D_MODEL, 8),
                  pl.BlockSpec((N_DEV, b_rows, D_MODEL), lambda i: (0, 0, 0))] + more_specs,
        out_specs=(_pair_spec(n), _pair_spec(D_MODEL)), name=name, compiler_params=_cp("parallel"))(x, g, mod, b, *more)


def norm_concat_matmul(mix_a, att, gao, w, name):
    def body(a_ref, att_ref, g_ref, w_ref, y_ref, mixed_ref):
        for rs in _halves():
            aa = att_ref[rs, :]
            mixed_ref[rs, 0:D_CONV] = a_ref[rs, :]
            mixed_ref[rs, D_CONV:] = (aa * lax.rsqrt(_mean(aa * aa) + EPS) * g_ref[...]).astype(BF16)
            y_ref[rs, :] = jnp.dot(mixed_ref[rs, :], w_ref[...], preferred_element_type=F32)

    return pl.pallas_call(
        body, out_shape=(jax.ShapeDtypeStruct((SEQ, D_MODEL), F32), jax.ShapeDtypeStruct((SEQ, D_MODEL), BF16)),
        grid=(SEQ // PAIR,),
        in_specs=[_pair_spec(D_CONV), _pair_spec(D_ATTN), _vec_spec(D_ATTN), pl.BlockSpec(w.shape, lambda i: (0, 0))],
        out_specs=(_pair_spec(D_MODEL), _pair_spec(D_MODEL)), name=name, compiler_params=_cp("parallel"))(mix_a, att, gao, w)


def resid_rms_mod_matmul(x, y, g, mod, ga_row, sh_row, sc_row, w, name):
    n = w.shape[0]

    def body(x_ref, y_ref, g_ref, mod_ref, w_ref, o_ref, x1_ref, h_ref):
        x1 = x_ref[...] + mod_ref[ga_row:ga_row + 1, :] * y_ref[...]
        x1_ref[...] = x1
        r = lax.rsqrt(_mean(x1 * x1) + EPS)
        h = (x1 * r * g_ref[...] * (1.0 + mod_ref[sc_row:sc_row + 1, :]) + mod_ref[sh_row:sh_row + 1, :]).astype(BF16)
        h_ref[...] = h
        o_ref[...] = lax.dot_general(h, w_ref[...], NT, preferred_element_type=F32)

    return pl.pallas_call(
        body,
        out_shape=(jax.ShapeDtypeStruct((SEQ, n), F32), jax.ShapeDtypeStruct((SEQ, D_MODEL), F32),
                   jax.ShapeDtypeStruct((SEQ, D_MODEL), BF16)),
        grid=(SEQ // ROWS,),
        in_specs=[_row_spec(D_MODEL), _row_spec(D_MODEL), _vec_spec(D_MODEL), _vec_spec(D_MODEL, 8),
                  pl.BlockSpec(w.shape, lambda i: (0, 0))],
        out_specs=(_row_spec(n), _row_spec(D_MODEL), _row_spec(D_MODEL)),
        name=name, compiler_params=_cp("parallel"))(x, y, g, mod, w)


def matmul_combine_bwd(dy, w, att, gao, name, after=None):
    def body(dy_ref, w_ref, att_ref, g_ref, dm_ref, do_ref, dd_ref, dg_ref):
        @pl.when(pl.program_id(0) == 0)
        def _():
            dg_ref[...] = jnp.zeros_like(dg_ref)

        same_head = (jnp.right_shift(lax.broadcasted_iota(jnp.int32, (D_ATTN, D_ATTN), 0), 6)
                     == jnp.right_shift(lax.broadcasted_iota(jnp.int32, (D_ATTN, D_ATTN), 1), 6)).astype(F32)
        for rs in _halves():
            dmixed = lax.dot_general(dy_ref[rs, :], w_ref[...], NT, preferred_element_type=F32)
            dm_ref[rs, :] = dmixed
            att = att_ref[rs, :]
            r = lax.rsqrt(_mean(att * att) + EPS)
            xn = att * r
            dm = dmixed[:, D_CONV:]
            dg_ref[...] += _rsum(dm * xn)
            dyn = dm * g_ref[...]
            do = r * (dyn - xn * _mean(dyn * xn))
            do_ref[rs, :] = do
            dd_ref[rs, :] = jnp.dot(do * att, same_head, preferred_element_type=F32, precision=lax.Precision.HIGHEST)

    rs = _pair_spec(D_ATTN)
    f = jax.ShapeDtypeStruct((SEQ, D_ATTN), F32)
    body, more_specs, more = _with_after(body, 4, after)
    return pl.pallas_call(
        body, out_shape=(jax.ShapeDtypeStruct((SEQ, D_MODEL), F32), f, f, jax.ShapeDtypeStruct((1, D_ATTN), F32)),
        grid=(SEQ // PAIR,),
        in_specs=[_pair_spec(D_MODEL), pl.BlockSpec(w.shape, lambda i: (0, 0)), rs, _vec_spec(D_ATTN)] + more_specs,
        out_specs=(_pair_spec(D_MODEL), rs, rs, _vec_spec(D_ATTN)),
        name=name, compiler_params=_cp("arbitrary"))(dy, w, att, gao, *more)


def matmul_loss_bwd(act, w, x1, tgt, g, mod, ga_row, name):
    def body(a_ref, w_ref, x1_ref, t_ref, g_ref, mod_ref, loss_ref, dx2_ref, dy2_ref, dg_ref, dga_ref):
        @pl.when(pl.program_id(0) == 0)
        def _():
            loss_ref[...] = jnp.zeros_like(loss_ref)
            dg_ref[...] = jnp.zeros_like(dg_ref)
            dga_ref[...] = jnp.zeros_like(dga_ref)

        ga = mod_ref[ga_row:ga_row + 1, :]
        for rs in _halves():
            y2 = jnp.dot(a_ref[rs, :], w_ref[...], preferred_element_type=F32)
            x2 = x1_ref[rs, :] + ga * y2
            r = lax.rsqrt(_mean(x2 * x2) + EPS)
            xn = x2 * r
            err = xn * g_ref[...] - t_ref[rs, :]
            loss_ref[...] += jnp.broadcast_to(0.5 * jnp.sum(_mean(err * err)), (8, 128))
            dy = err * (1.0 / D_MODEL)
            dg_ref[...] += _rsum(dy * xn)
            dxn = dy * g_ref[...]
            dx2 = r * (dxn - xn * _mean(dxn * xn))
            dx2_ref[rs, :] = dx2
            dy2_ref[rs, :] = (dx2 * ga).astype(BF16)
            dga_ref[...] += _rsum(dx2 * y2)

    vec = jax.ShapeDtypeStruct((1, D_MODEL), F32)
    rows = _pair_spec
    return pl.pallas_call(
        body,
        out_shape=(jax.ShapeDtypeStruct((8, 128), F32), jax.ShapeDtypeStruct((SEQ, D_MODEL), F32),
                   jax.ShapeDtypeStruct((SEQ, D_MODEL), BF16), vec, vec),
        grid=(SEQ // PAIR,),
        in_specs=[rows(act.shape[1]), pl.BlockSpec(w.shape, lambda i: (0, 0)), rows(D_MODEL), rows(D_MODEL),
                  _vec_spec(D_MODEL), _vec_spec(D_MODEL, 8)],
        out_specs=(pl.BlockSpec((8, 128), lambda i: (0, 0)), rows(D_MODEL), rows(D_MODEL),
                   _vec_spec(D_MODEL), _vec_spec(D_MODEL)),
        name=name, compiler_params=_cp("arbitrary"))(act, w, x1, tgt, g, mod)


def matmul_rms_mod_bwd(a, b, x, dres, g, mod, sc_row, y, ga_row, name, b_rows=None, after=None):
    gated = y is not None
    pieces = isinstance(a, tuple)
    tm = PAIR if pieces else ROWS
    blocks = [slice(h * ROWS, (h + 1) * ROWS) for h in range(tm // ROWS)]
    rows = lambda width: pl.BlockSpec((tm, width), lambda i: (i, 0))
    if pieces:
        a1, a3 = a
        a_args = [a1, a3]
        a_specs = [rows(a1.shape[1]), pl.BlockSpec((3, tm, a3.shape[2]), lambda i: (0, i, 0))]
        b_arg, b_spec = b, pl.BlockSpec((N_DEV, b_rows, D_MODEL), lambda i: (0, 0, 0))
    else:
        k2 = a.shape[2]
        a_args = [a]
        a_specs = [pl.BlockSpec((2, tm, k2), lambda i: (0, i, 0))]
        b_arg, b_spec = b.reshape(2, k2, D_MODEL), pl.BlockSpec((2, k2, D_MODEL), lambda i: (0, 0, 0))
    n_a = len(a_args)

    def body(*refs):
        a_refs, (b_ref, x_ref, dres_ref, g_ref, mod_ref) = refs[:n_a], refs[n_a:n_a + 5]
        if gated:
            y_ref, dx_ref, dsh_ref, dsc_ref, dg_ref, dy_ref, dga_ref = refs[n_a + 5:]
        else:
            dx_ref, dsh_ref, dsc_ref, dg_ref = refs[n_a + 5:]

        @pl.when(pl.program_id(0) == 0)
        def _():
            for ref in (dsh_ref, dsc_ref, dg_ref) + ((dga_ref,) if gated else ()):
                ref[...] = jnp.zeros_like(ref)

        gg = g_ref[...]
        for rs in blocks:
            if pieces:
                wv = b_ref[...].reshape(N_DEV * b_rows, D_MODEL)
                k1, k3 = a_refs[0].shape[1], a_refs[1].shape[2]
                dh = jnp.dot(a_refs[0][rs, :], wv[0:k1], preferred_element_type=F32)
                for t in range(3):
                    dh = dh + jnp.dot(a_refs[1][t, rs, :], wv[k1 + t * k3:k1 + (t + 1) * k3], preferred_element_type=F32)
            else:
                dh = (jnp.dot(a_refs[0][0, rs, :], b_ref[0], preferred_element_type=F32)
                      + jnp.dot(a_refs[0][1, rs, :], b_ref[1], preferred_element_type=F32))
            xx = x_ref[rs, :]
            r = lax.rsqrt(_mean(xx * xx) + EPS)
            xn = xx * r
            dsh_ref[...] += _rsum(dh)
            dsc_ref[...] += _rsum(dh * (xn * gg))
            dt = dh * (1.0 + mod_ref[sc_row:sc_row + 1, :])
            dg_ref[...] += _rsum(dt * xn)
            dxn = dt * gg
            dx = dres_ref[rs, :] + r * (dxn - xn * _mean(dxn * xn))
            dx_ref[rs, :] = dx
            if gated:
                dga_ref[...] += _rsum(dx * y_ref[rs, :])
                dy_ref[rs, :] = (dx * mod_ref[ga_row:ga_row + 1, :]).astype(BF16)

    vec = jax.ShapeDtypeStruct((1, D_MODEL), F32)
    in_specs = a_specs + [b_spec, rows(D_MODEL), rows(D_MODEL), _vec_spec(D_MODEL), _vec_spec(D_MODEL, 8)]
    out_shape = [jax.ShapeDtypeStruct((SEQ, D_MODEL), F32), vec, vec, vec]
    out_specs = [rows(D_MODEL), _vec_spec(D_MODEL), _vec_spec(D_MODEL), _vec_spec(D_MODEL)]
    args = a_args + [b_arg, x, dres, g, mod]
    if gated:
        in_specs.append(rows(D_MODEL))
        out_shape += [jax.ShapeDtypeStruct((SEQ, D_MODEL), BF16), vec]
        out_specs += [rows(D_MODEL), _vec_spec(D_MODEL)]
        args.append(y)
    body, more_specs, more = _with_after(body, len(args), after)
    return pl.pallas_call(
        body, out_shape=tuple(out_shape), grid=(SEQ // tm,), in_specs=in_specs + more_specs, out_specs=tuple(out_specs),
        name=name, compiler_params=_cp("arbitrary"))(*args, *more)


def _prev_halo(halo, width, col):
    per = ROWS // halo
    return pl.BlockSpec((halo, width), lambda i: (jnp.maximum(i * per - 1, 0), col))


def _next_halo(halo, width, col):
    per = ROWS // halo
    last = SEQ // halo - 1
    return pl.BlockSpec((halo, width), lambda i: (jnp.minimum((i + 1) * per, last), col))


CONV_PAD = ROWS + CONV_HALO


def _shift_copies(sh):
    for b in range(1, 8):
        sh[b, 0:CONV_PAD - 8, :] = sh[0, pl.ds(b, CONV_PAD - 8), :]


def _tap(sh, rs_start, offset):
    return sh[offset % 8, pl.ds(pl.multiple_of(rs_start + (offset // 8) * 8, 8), SUB), :]


def _conv_glu(av_ref, ag_ref, avh_ref, agh_ref, sh):
    i = pl.program_id(0)
    hv = avh_ref[...] * _sig(agh_ref[...])
    sh[0, 0:CONV_HALO, :] = jnp.where(i > 0, hv, 0.0)

    def glu(rs):
        sh[0, pl.ds(pl.multiple_of(rs.start + CONV_HALO, SUB), SUB), :] = av_ref[rs, :] * _sig(ag_ref[rs, :])

    _for_chunks(glu)
    _shift_copies(sh)


def _conv_norm(u1, lg_ref, lb_ref):
    mu = _mean(u1)
    cen = u1 - mu
    rs = lax.rsqrt(_mean(cen * cen) + EPS)
    z = cen * rs
    ln = z * lg_ref[...] + lb_ref[...]
    s = _sig(ln)
    return z, rs, ln, s, ln * s


def conv_module_fwd(proj, wc, bc, lg, lb, gco, name):
    def body(av_ref, ag_ref, avh_ref, agh_ref, wc_ref, bc_ref, lg_ref, lb_ref, gco_ref, out_ref, u1_ref, sh):
        _conv_glu(av_ref, ag_ref, avh_ref, agh_ref, sh)

        def conv(rs):
            u1 = jnp.broadcast_to(bc_ref[...], (SUB, D_CONV))
            for j in range(CONV_K):
                u1 = u1 + wc_ref[j:j + 1, :] * _tap(sh, rs.start, CONV_HALO - (CONV_K - 1) + j)
            u1_ref[rs, :] = u1

        _for_chunks(conv)
        _, _, _, _, u2 = _conv_norm(u1_ref[...], lg_ref, lb_ref)
        rc = lax.rsqrt(_mean(u2 * u2) + EPS)
        out_ref[...] = (u2 * rc * gco_ref[...]).astype(BF16)

    v = _vec_spec(D_CONV)
    return pl.pallas_call(
        body, out_shape=(jax.ShapeDtypeStruct((SEQ, D_CONV), BF16), jax.ShapeDtypeStruct((SEQ, D_CONV), F32)),
        grid=(SEQ // ROWS,),
        in_specs=[_row_spec(D_CONV, 0), _row_spec(D_CONV, 1), _prev_halo(CONV_HALO, D_CONV, 0),
                  _prev_halo(CONV_HALO, D_CONV, 1), _vec_spec(D_CONV, CONV_K), v, v, v, v],
        out_specs=(_row_spec(D_CONV), _row_spec(D_CONV)), scratch_shapes=[pltpu.VMEM((8, CONV_PAD, D_CONV), F32)],
        name=name, compiler_params=_cp("parallel"))(proj, proj, proj, proj, wc, bc, lg, lb, gco)


def conv_module_bwd_a(proj, u1, dmixed, lg, lb, gco, name):
    def body(av_ref, ag_ref, avh_ref, agh_ref, u1_ref, dm_ref, lg_ref, lb_ref, gco_ref,
             du1_ref, dgco_ref, dlg_ref, dlb_ref, dbc_ref, dwc_ref, sh, acc):
        first = pl.program_id(0) == 0
        _conv_glu(av_ref, ag_ref, avh_ref, agh_ref, sh)
        z, rs, ln, s, u2 = _conv_norm(u1_ref[...], lg_ref, lb_ref)
        rc = lax.rsqrt(_mean(u2 * u2) + EPS)
        xn = u2 * rc
        dm = dm_ref[...]
        _acc(dgco_ref, _rsum(dm * xn), first)
        dyn = dm * gco_ref[...]
        du2 = rc * (dyn - xn * _mean(dyn * xn))
        dln = du2 * (s * (1.0 + ln * (1.0 - s)))
        _acc(dlg_ref, _rsum(dln * z), first)
        _acc(dlb_ref, _rsum(dln), first)
        dz = dln * lg_ref[...]
        du1 = rs * (dz - _mean(dz) - z * _mean(dz * z))
        du1_ref[...] = du1
        _acc(dbc_ref, _rsum(du1), first)
        acc[...] = jnp.zeros_like(acc)

        def taps(rs):
            d = du1_ref[rs, :]
            for j in range(CONV_K):
                acc[j] += d * _tap(sh, rs.start, CONV_HALO - (CONV_K - 1) + j)

        _for_chunks(taps)

        @pl.when(first)
        def _():
            dwc_ref[...] = jnp.zeros_like(dwc_ref)

        for j in range(CONV_K):
            dwc_ref[j:j + 1, :] += _rsum(acc[j])

    v = _vec_spec(D_CONV)
    vec = jax.ShapeDtypeStruct((1, D_CONV), F32)
    return pl.pallas_call(
        body,
        out_shape=(jax.ShapeDtypeStruct((SEQ, D_CONV), F32), vec, vec, vec, vec, jax.ShapeDtypeStruct((CONV_K, D_CONV), F32)),
        grid=(SEQ // ROWS,),
        in_specs=[_row_spec(D_CONV, 0), _row_spec(D_CONV, 1), _prev_halo(CONV_HALO, D_CONV, 0),
                  _prev_halo(CONV_HALO, D_CONV, 1), _row_spec(D_CONV, 0), _row_spec(D_CONV, 0), v, v, v],
        out_specs=(_row_spec(D_CONV), v, v, v, v, _vec_spec(D_CONV, CONV_K)),
        scratch_shapes=[pltpu.VMEM((8, CONV_PAD, D_CONV), F32), pltpu.VMEM((CONV_K, SUB, D_CONV), F32)],
        name=name, compiler_params=_cp("arbitrary"))(proj, proj, proj, proj, u1, dmixed, lg, lb, gco)


def conv_module_bwd_b(proj, du1, wc, name):
    def body(av_ref, ag_ref, du1_ref, du1n_ref, wc_ref, out_ref, sh):
        i = pl.program_id(0)
        sh[0, 0:ROWS, :] = du1_ref[...]
        sh[0, ROWS:, :] = jnp.where(i < SEQ // ROWS - 1, du1n_ref[...], 0.0)
        _shift_copies(sh)

        def chunk(rs):
            du0 = jnp.zeros((SUB, D_CONV), F32)
            for j in range(CONV_K):
                du0 = du0 + wc_ref[j:j + 1, :] * _tap(sh, rs.start, CONV_K - 1 - j)
            sg = _sig(ag_ref[rs, :])
            out_ref[rs, 0:D_CONV] = (du0 * sg).astype(BF16)
            out_ref[rs, D_CONV:] = (du0 * av_ref[rs, :] * sg * (1.0 - sg)).astype(BF16)

        _for_chunks(chunk)

    return pl.pallas_call(
        body, out_shape=jax.ShapeDtypeStruct((SEQ, 2 * D_CONV), BF16), grid=(SEQ // ROWS,),
        in_specs=[_row_spec(D_CONV, 0), _row_spec(D_CONV, 1), _row_spec(D_CONV, 0), _next_halo(CONV_HALO, D_CONV, 0),
                  _vec_spec(D_CONV, CONV_K)],
        out_specs=_row_spec(2 * D_CONV), scratch_shapes=[pltpu.VMEM((8, CONV_PAD, D_CONV), F32)],
        name=name, compiler_params=_cp("parallel"))(proj, proj, du1, du1, wc)


def _rows(start, size, r):
    return pl.ds(start, size) if r == 1 else pl.ds(start, size, stride=r)


def _unit_rows(r, rho, n, nb):
    win = 2 * ATTN_BLOCK if nb > 1 else ATTN_BLOCK
    if isinstance(n, int):
        kb = max(n - 1, 0)
        q_rows = _rows(rho + r * ATTN_BLOCK * n, ATTN_BLOCK, r)
        k_rows = _rows(rho + r * ATTN_BLOCK * kb, win, r)
    else:
        kb = jnp.maximum(n - 1, 0)
        q_rows = pl.ds(pl.multiple_of(n * ATTN_BLOCK, ATTN_BLOCK), ATTN_BLOCK)
        k_rows = pl.ds(pl.multiple_of(kb * ATTN_BLOCK, ATTN_BLOCK), win)
    return q_rows, k_rows, n - kb


def _band_bias(first_block, transposed):
    shape = (2 * ATTN_BLOCK, ATTN_BLOCK) if transposed else (ATTN_BLOCK, 2 * ATTN_BLOCK)
    q_axis = 1 if transposed else 0
    dist = (0 if first_block else ATTN_BLOCK) + lax.broadcasted_iota(jnp.int32, shape, q_axis) \
        - lax.broadcasted_iota(jnp.int32, shape, 1 - q_axis)
    return jnp.where((dist >= 0) & (dist <= ATTN_BLOCK), 0.0, NEG)


def _per_head(x):
    lane = lax.broadcasted_iota(jnp.int32, x.shape, 1)
    zero = jnp.zeros_like(x)
    return [jnp.where(lane < HEAD_DIM, x, zero), jnp.where(lane >= HEAD_DIM, x, zero)]


SCALE = HEAD_DIM ** -0.5


def _masked_scores(q2, k2, bias):
    return [lax.dot_general(qh, k2, NT, preferred_element_type=F32) + bias for qh in _per_head(q2)]


def _attn_units(r, nb, unit):
    if r == 1:
        def four(i, carry):
            for k in range(4):
                unit(0, 4 * i + k)
            return carry
        lax.fori_loop(0, nb // 4, four, 0)
    else:
        for rho in range(r):
            for n in range(nb):
                unit(rho, n)


N_UNITS = 16


def attn_fwd_all(proj, name):
    def body(q_ref, k_ref, v_ref, att_ref, lse_ref, s_scr, p_scr, lse_scr, den_scr, bias_scr):
        bias_scr[0] = _band_bias(True, False)
        bias_scr[1] = _band_bias(False, False)
        for idx, (sub_len, r) in enumerate(PATTERNS):
            nb = sub_len // ATTN_BLOCK
            win = 2 * ATTN_BLOCK if nb > 1 else ATTN_BLOCK

            def scores(rho, n, r=r, nb=nb, win=win):
                u = rho * nb + n
                q_rows, k_rows, variant = _unit_rows(r, rho, n, nb)
                ss = _masked_scores((q_ref[q_rows, :] * SCALE).astype(BF16), k_ref[k_rows, :].astype(BF16),
                                    bias_scr[variant, :, 0:win])
                for h in range(2):
                    s_scr[2 * u + h, :, 0:win] = ss[h]

            _attn_units(r, nb, scores)

            def softmax(u, carry, win=win):
                lses, dens = [], []
                for h in range(2):
                    sc = s_scr[2 * u + h, :, 0:win]
                    m = jnp.max(sc, axis=1, keepdims=True)
                    p = jnp.exp(sc - m)
                    den = jnp.sum(p, axis=1, keepdims=True)
                    p_scr[2 * u + h, :, 0:win] = p.astype(BF16)
                    lses.append(jnp.broadcast_to(m + jnp.log(den), (ATTN_BLOCK, HEAD_DIM)))
                    dens.append(jnp.broadcast_to(den, (ATTN_BLOCK, HEAD_DIM)))
                lse_scr[u] = jnp.concatenate(lses, axis=1)
                den_scr[u] = jnp.concatenate(dens, axis=1)
                return carry

            lax.fori_loop(0, N_UNITS, softmax, 0, unroll=2)

            def outputs(rho, n, r=r, nb=nb, win=win, idx=idx):
                u = rho * nb + n
                q_rows, k_rows, _ = _unit_rows(r, rho, n, nb)
                vs = _per_head(v_ref[k_rows, :].astype(BF16))
                o = (jnp.dot(p_scr[2 * u, :, 0:win], vs[0], preferred_element_type=F32)
                     + jnp.dot(p_scr[2 * u + 1, :, 0:win], vs[1], preferred_element_type=F32)) / den_scr[u]
                lse = lse_scr[u]
                if idx > 0:
                    old = lse_ref[q_rows, :]
                    top = jnp.maximum(old, lse)
                    new = top + jnp.log(jnp.exp(old - top) + jnp.exp(lse - top))
                    o = att_ref[q_rows, :] * jnp.exp(old - new) + o * jnp.exp(lse - new)
                    lse = new
                att_ref[q_rows, :] = o
                lse_ref[q_rows, :] = lse

            _attn_units(r, nb, outputs)

    blk = lambda first: pl.BlockSpec((SEQ, 128), lambda g: (0, first + g))
    shp = jax.ShapeDtypeStruct((SEQ, D_ATTN), F32)
    big = (2 * N_UNITS, ATTN_BLOCK, 2 * ATTN_BLOCK)
    small = pltpu.VMEM((N_UNITS, ATTN_BLOCK, 128), F32)
    return pl.pallas_call(
        body, out_shape=(shp, shp), grid=(4,), in_specs=[blk(8), blk(12), blk(16)], out_specs=(blk(0), blk(0)),
        scratch_shapes=[pltpu.VMEM(big, F32), pltpu.VMEM(big, BF16), small, small,
                        pltpu.VMEM((2, ATTN_BLOCK, 2 * ATTN_BLOCK), F32)],
        name=name, compiler_params=_cp("parallel"))(proj, proj, proj)


def attn_bwd_all(proj, do, lse, dd, name):
    def body(q_ref, k_ref, v_ref, do_ref, l_ref, dd_ref, out_ref, dq_s, dk_s, dv_s,
             s_scr, dp_scr, ds_scr, st_scr, dpt_scr, pt_scr, dst_scr, qb_scr, kb_scr, dob_scr, bias_scr, bias_t_scr):
        for first_block in (True, False):
            bias_scr[1 - int(first_block)] = _band_bias(first_block, False)
            bias_t_scr[1 - int(first_block)] = _band_bias(first_block, True)
        dq_s[...] = jnp.zeros_like(dq_s)
        dk_s[...] = jnp.zeros_like(dk_s)
        dv_s[...] = jnp.zeros_like(dv_s)
        for sub_len, r in PATTERNS:
            nb = sub_len // ATTN_BLOCK
            win = 2 * ATTN_BLOCK if nb > 1 else ATTN_BLOCK

            def scores(rho, n, r=r, nb=nb, win=win):
                u = rho * nb + n
                q_rows, k_rows, variant = _unit_rows(r, rho, n, nb)
                bias, bias_t = bias_scr[variant, :, 0:win], bias_t_scr[variant, 0:win, :]
                q2 = (q_ref[q_rows, :] * SCALE).astype(BF16)
                kf = k_ref[k_rows, :]
                k2 = kf.astype(BF16)
                do2 = do_ref[q_rows, :].astype(BF16)
                qb_scr[u] = q2
                kb_scr[u, 0:win, :] = (kf * SCALE).astype(BF16)
                dob_scr[u] = do2
                l2 = l_ref[q_rows, :]
                d2 = dd_ref[q_rows, :]
                l2t = l2.T
                d2t = d2.T
                v2 = v_ref[k_rows, :].astype(BF16)
                qs, dos = _per_head(q2), _per_head(do2)
                for h in range(2):
                    c0 = h * HEAD_DIM
                    sc = lax.dot_general(qs[h], k2, NT, preferred_element_type=F32)
                    s_scr[2 * u + h, :, 0:win] = sc + bias - l2[:, c0:c0 + 1]
                    dp_scr[2 * u + h, :, 0:win] = lax.dot_general(dos[h], v2, NT, preferred_element_type=F32) \
                        - d2[:, c0:c0 + 1]
                    sct = lax.dot_general(k2, qs[h], NT, preferred_element_type=F32)
                    st_scr[2 * u + h, 0:win, :] = sct + bias_t - l2t[c0:c0 + 1, :]
                    dpt_scr[2 * u + h, 0:win, :] = lax.dot_general(v2, dos[h], NT, preferred_element_type=F32) \
                        - d2t[c0:c0 + 1, :]

            _attn_units(r, nb, scores)

            def pointwise(hu, carry, win=win):
                ds_scr[hu, :, 0:win] = (jnp.exp(s_scr[hu, :, 0:win]) * dp_scr[hu, :, 0:win]).astype(BF16)
                pt = jnp.exp(st_scr[hu, 0:win, :])
                pt_scr[hu, 0:win, :] = pt.astype(BF16)
                dst_scr[hu, 0:win, :] = (pt * dpt_scr[hu, 0:win, :]).astype(BF16)
                return carry

            lax.fori_loop(0, 2 * N_UNITS, pointwise, 0, unroll=4)

            def grads(rho, n, r=r, nb=nb, win=win):
                u = rho * nb + n
                q_rows, k_rows, _ = _unit_rows(r, rho, n, nb)
                qs, ks, dos = _per_head(qb_scr[u]), _per_head(kb_scr[u, 0:win, :]), _per_head(dob_scr[u])

                def both(scr, rows, rhs):
                    return (jnp.dot(scr[(2 * u,) + rows], rhs[0], preferred_element_type=F32)
                            + jnp.dot(scr[(2 * u + 1,) + rows], rhs[1], preferred_element_type=F32))

                dq_s[q_rows, :] += both(ds_scr, (slice(None), slice(0, win)), ks)
                dk_s[k_rows, :] += both(dst_scr, (slice(0, win), slice(None)), qs)
                dv_s[k_rows, :] += both(pt_scr, (slice(0, win), slice(None)), dos)

            _attn_units(r, nb, grads)
        out_ref[0] = dq_s[...].astype(BF16)
        out_ref[1] = dk_s[...].astype(BF16)
        out_ref[2] = dv_s[...].astype(BF16)

    blk = lambda first: pl.BlockSpec((SEQ, 128), lambda g: (0, first + g))
    acc = pltpu.VMEM((SEQ, 128), F32)
    big = (2 * N_UNITS, ATTN_BLOCK, 2 * ATTN_BLOCK)
    big_t = (2 * N_UNITS, 2 * ATTN_BLOCK, ATTN_BLOCK)
    return pl.pallas_call(
        body, out_shape=jax.ShapeDtypeStruct((3, SEQ, D_ATTN), BF16), grid=(4,),
        in_specs=[blk(8), blk(12), blk(16), blk(0), blk(0), blk(0)],
        out_specs=pl.BlockSpec((3, SEQ, 128), lambda g: (0, 0, g)),
        scratch_shapes=[acc, acc, acc, pltpu.VMEM(big, F32), pltpu.VMEM(big, F32), pltpu.VMEM(big, BF16),
                        pltpu.VMEM(big_t, F32), pltpu.VMEM(big_t, F32), pltpu.VMEM(big_t, BF16), pltpu.VMEM(big_t, BF16),
                        pltpu.VMEM((N_UNITS, ATTN_BLOCK, 128), BF16),
                        pltpu.VMEM((N_UNITS, 2 * ATTN_BLOCK, 128), BF16), pltpu.VMEM((N_UNITS, ATTN_BLOCK, 128), BF16),
                        pltpu.VMEM((2, ATTN_BLOCK, 2 * ATTN_BLOCK), F32), pltpu.VMEM((2, 2 * ATTN_BLOCK, ATTN_BLOCK), F32)],
        name=name, compiler_params=_cp("parallel"))(proj, proj, proj, do, lse, dd)


N_FT = D_FF // FFN_TN


def _ffn_specs():
    per = ROWS // FFN_HALO
    cur_g = pl.BlockSpec((ROWS, FFN_TN), lambda j, i: (i, j))
    cur_v = pl.BlockSpec((ROWS, FFN_TN), lambda j, i: (i, j + N_FT))
    halo_g = pl.BlockSpec((FFN_HALO, FFN_TN), lambda j, i: (jnp.maximum(i * per - 1, 0), j))
    halo_v = pl.BlockSpec((FFN_HALO, FFN_TN), lambda j, i: (jnp.maximum(i * per - 1, 0), j + N_FT))
    w_g = pl.BlockSpec((FFN_K, FFN_TN), lambda j, i: (0, j))
    w_v = pl.BlockSpec((FFN_K, FFN_TN), lambda j, i: (0, j + N_FT))
    b_g = pl.BlockSpec((1, FFN_TN), lambda j, i: (0, j))
    b_v = pl.BlockSpec((1, FFN_TN), lambda j, i: (0, j + N_FT))
    return [cur_g, cur_v, halo_g, halo_v, w_g, w_v, b_g, b_v]


def matmul(a, b, kind, out_dtype, tm, tn, name, b_rows=None):
    stacked = b_rows is not None
    b_shape = (N_DEV * b_rows, b.shape[2]) if stacked else b.shape
    if kind == "nn":
        (m, k), n = a.shape, b_shape[1]
        a_spec = pl.BlockSpec((tm, k), lambda j, i: (i, 0))
        b_spec = pl.BlockSpec((k, tn), lambda j, i: (0, j))
        dims = (((1,), (0,)), ((), ()))
    elif kind == "nt":
        (m, k), n = a.shape, b_shape[0]
        a_spec = pl.BlockSpec((tm, k), lambda j, i: (i, 0))
        b_spec = pl.BlockSpec((tn, k), lambda j, i: (j, 0))
        dims = (((1,), (1,)), ((), ()))
    else:
        (k, m), n = a.shape, b_shape[1]
        a_spec = pl.BlockSpec((k, tm), lambda j, i: (0, i))
        b_spec = pl.BlockSpec((k, tn), lambda j, i: (0, j))
        dims = (((0,), (0,)), ((), ()))
    assert m % tm == 0 and n % tn == 0, (name, m, n, tm, tn)
    if stacked:
        assert b_spec.block_shape[0] == b_shape[0] and kind in ("nn", "nt")
        width = b_spec.block_shape[1]
        b_spec = pl.BlockSpec((N_DEV, b_rows, width), (lambda j, i: (0, 0, j)) if kind == "nn" else (lambda j, i: (0, 0, 0)))

    def body(a_ref, b_ref, o_ref):
        bb = b_ref[...].reshape(b_shape[0], -1) if stacked else b_ref[...]
        o_ref[...] = lax.dot_general(a_ref[...], bb, dims, preferred_element_type=F32).astype(o_ref.dtype)

    return pl.pallas_call(
        body, out_shape=jax.ShapeDtypeStruct((m, n), out_dtype), grid=(n // tn, m // tm),
        in_specs=[a_spec, b_spec], out_specs=pl.BlockSpec((tm, tn), lambda j, i: (i, j)),
        name=name, compiler_params=_cp("parallel", "parallel"))(a, b)


def matmul_tn_pieces(a1, a3, b, name):
    k, n = b.shape
    tm = a3.shape[2]
    n1 = a1.shape[1] // tm

    def body(a1_ref, a3_ref, b_ref, o_ref):
        i = pl.program_id(0)
        tn_dims = (((0,), (0,)), ((), ()))

        @pl.when(i < n1)
        def _():
            o_ref[...] = lax.dot_general(a1_ref[...], b_ref[...], tn_dims, preferred_element_type=F32).astype(BF16)

        @pl.when(i >= n1)
        def _():
            o_ref[...] = lax.dot_general(a3_ref[0], b_ref[...], tn_dims, preferred_element_type=F32).astype(BF16)

    return pl.pallas_call(
        body, out_shape=jax.ShapeDtypeStruct((a1.shape[1] + 3 * tm, n), BF16), grid=(n1 + 3,),
        in_specs=[pl.BlockSpec((k, tm), lambda i: (0, jnp.minimum(i, n1 - 1))),
                  pl.BlockSpec((1, k, tm), lambda i: (jnp.maximum(i - n1, 0), 0, 0)),
                  pl.BlockSpec((k, n), lambda i: (0, 0))],
        out_specs=pl.BlockSpec((tm, n), lambda i: (i, 0)), name=name, compiler_params=_cp("parallel"))(a1, a3, b)


def matmul_tn_halves(a, b, tm, name):
    _, k, m = a.shape
    n = b.shape[1]

    def body(a_ref, b_ref, o_ref):
        o_ref[0] = lax.dot_general(a_ref[0], b_ref[...], (((0,), (0,)), ((), ())), preferred_element_type=F32).astype(BF16)

    return pl.pallas_call(
        body, out_shape=jax.ShapeDtypeStruct((2, m, n), BF16), grid=(2, m // tm),
        in_specs=[pl.BlockSpec((1, k, tm), lambda h, i: (h, 0, i)), pl.BlockSpec((k, n), lambda h, i: (0, 0))],
        out_specs=pl.BlockSpec((1, tm, n), lambda h, i: (h, i, 0)), name=name,
        compiler_params=_cp("parallel", "parallel"))(a, b).reshape(2 * m, n)


def _row_spec(width, col=0):
    return pl.BlockSpec((ROWS, width), lambda i: (i, col))


def _vec_spec(width, rows=1):
    return pl.BlockSpec((rows, width), lambda i: (0, 0))


FFN_RB = 64


def _ffn_lane_blocks(fn):
    for c in range(FFN_TN // 128):
        fn(slice(c * 128, (c + 1) * 128))


def _ffn_taps(cur_ref, halo_ref, head, ls, r0):
    if r0 == 0:
        head[0:FFN_HALO, :] = jnp.where(pl.program_id(1) > 0, halo_ref[:, ls], 0.0)
        head[FFN_HALO:, :] = cur_ref[0:FFN_RB, ls]
        return tuple(head[pl.ds(FFN_HALO - k, FFN_RB), :] for k in (2, 1, 0))
    return tuple(cur_ref[pl.ds(r0 - k, FFN_RB), ls] for k in (2, 1, 0))


def _ffn_conv(taps, w, b):
    return b + w[0] * taps[0] + w[1] * taps[1] + w[2] * taps[2]


def ffn_act_fwd(up0, wf, bf, name, after=None):
    def body(g_ref, v_ref, gh_ref, vh_ref, wg_ref, wv_ref, bg_ref, bv_ref, act_ref, head_g, head_v):
        def lanes(ls):
            wg = [wg_ref[t:t + 1, ls] for t in range(FFN_K)]
            wv = [wv_ref[t:t + 1, ls] for t in range(FFN_K)]
            for r0 in range(0, ROWS, FFN_RB):
                gate = _ffn_conv(_ffn_taps(g_ref, gh_ref, head_g, ls, r0), wg, bg_ref[:, ls])
                val = _ffn_conv(_ffn_taps(v_ref, vh_ref, head_v, ls, r0), wv, bv_ref[:, ls])
                act_ref[r0:r0 + FFN_RB, ls] = (gate * _sig(gate) * val).astype(BF16)

        _ffn_lane_blocks(lanes)

    head = pltpu.VMEM((FFN_HALO + FFN_RB, 128), F32)
    body, more_specs, more = _with_after(body, 8, after)
    return pl.pallas_call(
        body, out_shape=jax.ShapeDtypeStruct((SEQ, D_FF), BF16), grid=(N_FT, SEQ // ROWS),
        in_specs=_ffn_specs() + more_specs, out_specs=pl.BlockSpec((ROWS, FFN_TN), lambda j, i: (i, j)),
        scratch_shapes=[head, head], name=name,
        compiler_params=_cp("parallel", "parallel"))(up0, up0, up0, up0, wf, wf, bf, bf, *more)


def ffn_bwd(up0, dact, wf, bf, name, after=None):
    per = ROWS // FFN_HALO
    last = SEQ // FFN_HALO - 1

    def body(g_ref, v_ref, gh_ref, vh_ref, wg_ref, wv_ref, bg_ref, bv_ref, da_ref, gn_ref, vn_ref, dan_ref,
             out_ref, dbg_ref, dbv_ref, dwg_ref, dwv_ref, pad, head_g, head_v, dgp, dvp, acc):
        i = pl.program_id(1)
        first = i == 0
        acc[...] = jnp.zeros_like(acc)

        def grads(gate, val, da):
            s = _sig(gate)
            return da * val * (s * (1.0 + gate * (1.0 - s))), da * (gate * s)

        fold = lambda q: jnp.sum(q.reshape(FFN_RB // 8, 8, 128), axis=0)

        def lanes(ls):
            wg = [wg_ref[t:t + 1, ls] for t in range(FFN_K)]
            wv = [wv_ref[t:t + 1, ls] for t in range(FFN_K)]
            sums = [jnp.zeros((8, 128), F32)] * (2 + 2 * FFN_K)
            for r0 in range(0, ROWS, FFN_RB):
                gs = _ffn_taps(g_ref, gh_ref, head_g, ls, r0)
                vs = _ffn_taps(v_ref, vh_ref, head_v, ls, r0)
                dgate, dval = grads(_ffn_conv(gs, wg, bg_ref[:, ls]), _ffn_conv(vs, wv, bv_ref[:, ls]),
                                    da_ref[r0:r0 + FFN_RB, ls])
                dgp[r0:r0 + FFN_RB, ls] = dgate
                dvp[r0:r0 + FFN_RB, ls] = dval
                new = [dgate, dval] + [dgate * gs[t] for t in range(FFN_K)] + [dval * vs[t] for t in range(FFN_K)]
                sums = [a + fold(q) for a, q in zip(sums, new)]
            for k in range(2 + 2 * FFN_K):
                acc[k, 0:8, ls] = sums[k]

        _ffn_lane_blocks(lanes)
        _acc(dbg_ref, _rsum(acc[0]), first)
        _acc(dbv_ref, _rsum(acc[1]), first)
        _acc(dwg_ref, jnp.concatenate([_rsum(acc[2 + t]) for t in range(FFN_K)], axis=0), first)
        _acc(dwv_ref, jnp.concatenate([_rsum(acc[5 + t]) for t in range(FFN_K)], axis=0), first)

        def conv_next(cur_ref, nxt_ref, w_ref, b_ref):
            pad[0:FFN_HALO, :] = cur_ref[ROWS - FFN_HALO:, :]
            pad[FFN_HALO:, :] = nxt_ref[...]
            return (b_ref[...] + w_ref[0:1, :] * pad[pl.ds(FFN_HALO - 2, FFN_HALO), :]
                    + w_ref[1:2, :] * pad[pl.ds(FFN_HALO - 1, FFN_HALO), :] + w_ref[2:3, :] * nxt_ref[...])

        gate_n = conv_next(g_ref, gn_ref, wg_ref, bg_ref)
        val_n = conv_next(v_ref, vn_ref, wv_ref, bv_ref)
        dgate_n, dval_n = grads(gate_n, val_n, dan_ref[...])
        inside = i < SEQ // ROWS - 1
        dgp[ROWS:, :] = jnp.where(inside, dgate_n, 0.0)
        dvp[ROWS:, :] = jnp.where(inside, dval_n, 0.0)

        def back(ls):
            for half, (dp, w_ref) in enumerate(((dgp, wg_ref), (dvp, wv_ref))):
                w = [w_ref[t:t + 1, ls] for t in range(FFN_K)]
                for r0 in range(0, ROWS, FFN_RB):
                    out_ref[half, r0:r0 + FFN_RB, ls] = (
                        w[2] * dp[r0:r0 + FFN_RB, ls] + w[1] * dp[pl.ds(r0 + 1, FFN_RB), ls]
                        + w[0] * dp[pl.ds(r0 + 2, FFN_RB), ls]).astype(BF16)

        _ffn_lane_blocks(back)

    body, more_specs, more = _with_after(body, 12, after)
    head = pltpu.VMEM((FFN_HALO + FFN_RB, 128), F32)
    ext = pltpu.VMEM((ROWS + FFN_HALO, FFN_TN), F32)
    vec = jax.ShapeDtypeStruct((1, D_FF), F32)
    taps = jax.ShapeDtypeStruct((FFN_K, D_FF), F32)
    cur = pl.BlockSpec((ROWS, FFN_TN), lambda j, i: (i, j))
    nxt = lambda off: pl.BlockSpec((FFN_HALO, FFN_TN), lambda j, i: (jnp.minimum((i + 1) * per, last), j + off))
    vs = pl.BlockSpec((1, FFN_TN), lambda j, i: (0, j))
    ts = pl.BlockSpec((FFN_K, FFN_TN), lambda j, i: (0, j))
    return pl.pallas_call(
        body, out_shape=(jax.ShapeDtypeStruct((2, SEQ, D_FF), BF16), vec, vec, taps, taps), grid=(N_FT, SEQ // ROWS),
        in_specs=_ffn_specs() + [cur, nxt(0), nxt(N_FT), nxt(0)] + more_specs,
        out_specs=(pl.BlockSpec((2, ROWS, FFN_TN), lambda j, i: (0, i, j)), vs, vs, ts, ts),
        scratch_shapes=[pltpu.VMEM((2 * FFN_HALO, FFN_TN), F32), head, head, ext, ext,
                        pltpu.VMEM((2 + 2 * FFN_K, SUB, FFN_TN), F32)],
        name=name, compiler_params=_cp("parallel", "arbitrary"))(up0, up0, up0, up0, wf, wf, bf, bf, dact, up0, up0, dact,
                                                                 *more)


def ada_fwd(c_all, w_ada, b_cols, name):
    def body(c_ref, w_ref, b_ref, o_ref):
        cc = c_ref[...]
        sc = (cc * _sig(cc)).astype(BF16)
        o_ref[...] = jnp.dot(sc, w_ref[...].astype(BF16), preferred_element_type=F32) + b_ref[...]

    return pl.pallas_call(body, out_shape=jax.ShapeDtypeStruct((N_DEV, w_ada.shape[1]), F32), name=name,
                          compiler_params=_cp())(c_all, w_ada, b_cols)


def _adam(w, g, m, v):
    m = ADAM_B1 * m + (1.0 - ADAM_B1) * g
    v = ADAM_B2 * v + (1.0 - ADAM_B2) * (g * g)
    m_hat = m / (1.0 - ADAM_B1 ** ADAM_STEP)
    v_hat = v / (1.0 - ADAM_B2 ** ADAM_STEP)
    delta = -ADAM_LR * (m_hat / (jnp.sqrt(v_hat) + ADAM_EPS) + ADAM_WD * w)
    return delta, m, v


def ada_bwd_adamw(c_all, dmod_cols, w, m, v, name):
    rows, cols = w.shape
    tr = 256

    def body(c_ref, dm_ref, w_ref, m_ref, v_ref, g_ref, d_ref, nm_ref, nv_ref):
        cc = c_ref[...]
        sc = (cc * _sig(cc)).T
        g = sc[:, 0:1] * dm_ref[0:1, :]
        for b in range(1, N_DEV):
            g = g + sc[:, b:b + 1] * dm_ref[b:b + 1, :]
        g_ref[...] = g
        d_ref[...], nm_ref[...], nv_ref[...] = _adam(w_ref[...], g, m_ref[...], v_ref[...])

    blk = pl.BlockSpec((tr, cols), lambda i: (i, 0))
    shp = jax.ShapeDtypeStruct((rows, cols), F32)
    return pl.pallas_call(
        body, out_shape=(shp, shp, shp, shp), grid=(rows // tr,),
        in_specs=[pl.BlockSpec((N_DEV, tr), lambda i: (0, i)), pl.BlockSpec((N_DEV, cols), lambda i: (0, 0)), blk, blk, blk],
        out_specs=(blk, blk, blk, blk), name=name, compiler_params=_cp("parallel"))(c_all, dmod_cols, w, m, v)


def sum_adamw(parts, mine, me, w, m, v, tr, name):
    n_parts, rows, cols = parts.shape

    def body(me_ref, p_ref, own_ref, w_ref, m_ref, v_ref, g_ref, d_ref, nm_ref, nv_ref):
        def chunk(rs):
            g = own_ref[0, rs, :].astype(F32)
            for k in range(1, n_parts):
                g = g + p_ref[k, rs, :].astype(F32)
            g_ref[rs, :] = g
            d_ref[rs, :], nm_ref[rs, :], nv_ref[rs, :] = _adam(w_ref[rs, :], g, m_ref[rs, :], v_ref[rs, :])

        _for_chunks(chunk, 2, tr)

    blk = pl.BlockSpec((tr, cols), lambda i, me_ref: (i, 0))
    shp = jax.ShapeDtypeStruct((rows, cols), F32)
    grid_spec = pltpu.PrefetchScalarGridSpec(
        num_scalar_prefetch=1, grid=(rows // tr,),
        in_specs=[pl.BlockSpec((n_parts, tr, cols), lambda i, me_ref: (0, i, 0)),
                  pl.BlockSpec((1, tr, cols), lambda i, me_ref: (me_ref[0], i, 0)), blk, blk, blk],
        out_specs=(blk, blk, blk, blk))
    return pl.pallas_call(body, out_shape=(shp, shp, shp, shp), grid_spec=grid_spec, name=name,
                          compiler_params=_cp("parallel"))(me, parts, mine, w, m, v)


MESH = pl.DeviceIdType.MESH


def all_gather(block, name, after=None):
    extra = () if after is None else (after,)

    def body(x_ref, *refs):
        out_ref, send_sems, recv_sems, local_sem = refs[len(extra):]
        x, y, c = lax.axis_index("x"), lax.axis_index("y"), lax.axis_index("c")
        me, sibling = (x, y, c), (x, y, 1 - c)
        chips = [(1 - x, y), (x, 1 - y), (1 - x, 1 - y)]

        def slot(px, py, pc):
            return out_ref.at[4 * px + 2 * py + pc]

        def copy(k, blk, to, src=None):
            return pltpu.make_async_remote_copy(
                src_ref=slot(*blk) if src is None else src, dst_ref=slot(*blk),
                send_sem=send_sems.at[k], recv_sem=recv_sems.at[k], device_id=to, device_id_type=MESH)

        mine = pltpu.make_async_copy(x_ref, slot(*me), local_sem)
        mine.start()
        first = [copy(0, me, sibling, src=x_ref)]
        first += [copy(1 + j, me, (*chip, c), src=x_ref) for j, chip in enumerate(chips)]
        for cp in first:
            cp.start()
        passed = [copy(4 + j, (*chip, c), sibling) for j, chip in enumerate(chips)]
        for j, chip in enumerate(chips):
            copy(1 + j, (*chip, c), me).wait_recv()
            passed[j].start()
        copy(0, sibling, me).wait_recv()
        for j, chip in enumerate(chips):
            copy(4 + j, (*chip, 1 - c), me).wait_recv()
        for cp in first + passed:
            cp.wait_send()
        mine.wait()

    return pl.pallas_call(
        body, out_shape=jax.ShapeDtypeStruct((N_DEV,) + block.shape, block.dtype), in_specs=[ANY] * (1 + len(extra)), out_specs=ANY,
        scratch_shapes=[pltpu.SemaphoreType.DMA((7,)), pltpu.SemaphoreType.DMA((7,)), pltpu.SemaphoreType.DMA],
        name=name)(block, *extra)


HBM = pl.BlockSpec(memory_space=pltpu.HBM)
SEM = pl.BlockSpec(memory_space=pltpu.SEMAPHORE)
EFFECT = pltpu.SideEffectType.DATAFLOW_SIDE_EFFECTING


def _peer_copies(src_ref, land_ref, send_sems, recv_sems, gather):
    x, y, c = lax.axis_index("x"), lax.axis_index("y"), lax.axis_index("c")
    me = 4 * x + 2 * y + c
    copies = []
    for k in range(1, N_DEV):
        px = 1 - x if k & 4 else x
        py = 1 - y if k & 2 else y
        pc = 1 - c if k & 1 else c
        copies.append(pltpu.make_async_remote_copy(
            src_ref=src_ref if gather else src_ref.at[4 * px + 2 * py + pc],
            dst_ref=land_ref.at[me] if gather else land_ref.at[k],
            send_sem=send_sems.at[k - 1], recv_sem=recv_sems.at[k - 1], device_id=(px, py, pc), device_id_type=MESH))
    return copies


def exchange_start(srcs, gather, name, after=None):
    n = len(srcs)
    land_shapes = [(N_DEV,) + src.shape if gather else src.shape for src in srcs]
    extra = () if after is None else (after,)

    def body(*refs):
        src_refs, land_refs = refs[0:n], refs[n:2 * n]
        outs = refs[2 * n + len(extra):]
        for k in range(n):
            for cp in _peer_copies(src_refs[k], land_refs[k], outs[4 * k], outs[4 * k + 1], gather):
                cp.start()
        token = outs[4 * n]
        token[...] = jnp.zeros_like(token)

    out_shape, out_specs, aliases = [], [], {}
    for k, src in enumerate(srcs):
        out_shape += [pltpu.SemaphoreType.DMA((N_DEV - 1,)), pltpu.SemaphoreType.DMA((N_DEV - 1,)),
                      pltpu.HBM(src.shape, src.dtype), pltpu.HBM(land_shapes[k], src.dtype)]
        out_specs += [SEM, SEM, HBM, HBM]
        aliases[k] = 4 * k + 2
        aliases[n + k] = 4 * k + 3
    out_shape.append(jax.ShapeDtypeStruct((8, 128), F32))
    out_specs.append(pl.BlockSpec(memory_space=pltpu.VMEM))
    res = pl.pallas_call(
        body, name=name, out_shape=tuple(out_shape), in_specs=(HBM,) * (2 * n) + (ANY,) * len(extra),
        out_specs=tuple(out_specs), input_output_aliases=aliases,
        compiler_params=pltpu.CompilerParams(has_side_effects=EFFECT),
    )(*[pltpu.with_memory_space_constraint(src, pltpu.HBM) for src in srcs],
      *[pltpu.with_memory_space_constraint(lax.empty(shp, src.dtype), pltpu.HBM) for shp, src in zip(land_shapes, srcs)],
      *extra)
    return [tuple(res[4 * k:4 * k + 4]) for k in range(n)], res[4 * n]


def _stage1_peer(i):
    x, y, c = lax.axis_index("x"), lax.axis_index("y"), lax.axis_index("c")
    if i == 0:
        return (x, y, 1 - c)
    return (1 - x if i & 1 else x, 1 - y if i & 2 else y, c)


def _slot_of(peer):
    return 4 * peer[0] + 2 * peer[1] + peer[2]


def _stage1_copy(i, src_ref, land_ref, send_sems, recv_sems):
    me = _slot_of((lax.axis_index("x"), lax.axis_index("y"), lax.axis_index("c")))
    return pltpu.make_async_remote_copy(src_ref=src_ref, dst_ref=land_ref.at[me], send_sem=send_sems.at[i],
                                        recv_sem=recv_sems.at[i], device_id=_stage1_peer(i), device_id_type=MESH)


def _stage2_copy(j, land_ref, send_sems, recv_sems):
    slot = _slot_of(_stage1_peer(j + 1))
    return pltpu.make_async_remote_copy(src_ref=land_ref.at[slot], dst_ref=land_ref.at[slot], send_sem=send_sems.at[j],
                                        recv_sem=recv_sems.at[j], device_id=_stage1_peer(0), device_id_type=MESH)


def gather2_start(srcs, name, after=None):
    n = len(srcs)
    extra = () if after is None else (after,)

    def body(*refs):
        src_refs, land_refs = refs[0:n], refs[n:2 * n]
        outs = refs[2 * n + len(extra):]
        for k in range(n):
            for i in range(4):
                _stage1_copy(i, src_refs[k], land_refs[k], outs[4 * k], outs[4 * k + 1]).start()
        outs[4 * n][...] = jnp.zeros((8, 128), F32)

    out_shape, out_specs, aliases = [], [], {}
    for k, src in enumerate(srcs):
        out_shape += [pltpu.SemaphoreType.DMA((4,)), pltpu.SemaphoreType.DMA((4,)),
                      pltpu.HBM(src.shape, src.dtype), pltpu.HBM((N_DEV,) + src.shape, src.dtype)]
        out_specs += [SEM, SEM, HBM, HBM]
        aliases[k] = 4 * k + 2
        aliases[n + k] = 4 * k + 3
    out_shape.append(jax.ShapeDtypeStruct((8, 128), F32))
    out_specs.append(pl.BlockSpec(memory_space=pltpu.VMEM))
    res = pl.pallas_call(
        body, name=name, out_shape=tuple(out_shape), in_specs=(HBM,) * (2 * n) + (ANY,) * len(extra),
        out_specs=tuple(out_specs), input_output_aliases=aliases,
        compiler_params=pltpu.CompilerParams(has_side_effects=EFFECT),
    )(*[pltpu.with_memory_space_constraint(src, pltpu.HBM) for src in srcs],
      *[pltpu.with_memory_space_constraint(lax.empty((N_DEV,) + src.shape, src.dtype), pltpu.HBM) for src in srcs], *extra)
    return [dict(send1=res[4 * k], recv1=res[4 * k + 1], src=res[4 * k + 2], land=res[4 * k + 3]) for k in range(n)], \
        res[4 * n]


def gather2_pass(handles, after, name):
    n = len(handles)

    def body(*refs):
        src_refs, land_refs, recv1 = refs[0:n], refs[n:2 * n], refs[2 * n:3 * n]
        outs = refs[3 * n + 1:]
        for k in range(n):
            for j in range(3):
                _stage1_copy(j + 1, src_refs[k], land_refs[k], recv1[k], recv1[k]).wait_recv()
                _stage2_copy(j, land_refs[k], outs[3 * k], outs[3 * k + 1]).start()
        outs[3 * n][...] = jnp.zeros((8, 128), F32)

    out_shape, out_specs, aliases = [], [], {}
    for k, h in enumerate(handles):
        out_shape += [pltpu.SemaphoreType.DMA((3,)), pltpu.SemaphoreType.DMA((3,)), pltpu.HBM(h["land"].shape, h["land"].dtype)]
        out_specs += [SEM, SEM, HBM]
        aliases[n + k] = 3 * k + 2
    out_shape.append(jax.ShapeDtypeStruct((8, 128), F32))
    out_specs.append(pl.BlockSpec(memory_space=pltpu.VMEM))
    res = pl.pallas_call(
        body, name=name, out_shape=tuple(out_shape), in_specs=(HBM,) * (2 * n) + (SEM,) * n + (ANY,),
        out_specs=tuple(out_specs), input_output_aliases=aliases,
        compiler_params=pltpu.CompilerParams(has_side_effects=EFFECT),
    )(*[h["src"] for h in handles], *[h["land"] for h in handles], *[h["recv1"] for h in handles], after)
    return [dict(h, send2=res[3 * k], recv2=res[3 * k + 1], land=res[3 * k + 2]) for k, h in enumerate(handles)], res[3 * n]


def gather2_wait(h, after, name):
    def body(src_ref, land_ref, send1, recv1, send2, recv2, after_ref, src_dead, got_ref):
        for i in range(4):
            _stage1_copy(i, src_ref, land_ref, send1, recv1).wait_send()
        _stage1_copy(0, src_ref, land_ref, send1, recv1).wait_recv()
        for j in range(3):
            cp = _stage2_copy(j, land_ref, send2, recv2)
            cp.wait_send()
            cp.wait_recv()

    return pl.pallas_call(
        body, name=name,
        out_shape=(pltpu.HBM(h["src"].shape, h["src"].dtype), pltpu.HBM(h["land"].shape, h["land"].dtype)),
        in_specs=(HBM, HBM, SEM, SEM, SEM, SEM, ANY), out_specs=(HBM, HBM), input_output_aliases={0: 0, 1: 1},
        compiler_params=pltpu.CompilerParams(has_side_effects=EFFECT),
    )(h["src"], h["land"], h["send1"], h["recv1"], h["send2"], h["recv2"], after)[1]


def exchange_wait(handles, after, gather, name):
    send_sems, recv_sems, src_thru, land_thru = handles

    def body(src_ref, land_ref, send_sems, recv_sems, after_ref, src_dead, got_ref):
        for cp in _peer_copies(src_ref, land_ref, send_sems, recv_sems, gather):
            cp.wait_send()
            cp.wait_recv()

    return pl.pallas_call(
        body, name=name,
        out_shape=(pltpu.HBM(src_thru.shape, src_thru.dtype), pltpu.HBM(land_thru.shape, land_thru.dtype)),
        in_specs=(HBM, HBM, SEM, SEM, ANY), out_specs=(HBM, HBM), input_output_aliases={0: 0, 1: 1},
        compiler_params=pltpu.CompilerParams(has_side_effects=EFFECT),
    )(src_thru, land_thru, send_sems, recv_sems, after)


def local_step(x, tgt, mod, started, get_w, put_grad, wc, wf, g_mix, bc, lg, lb, gco, gao, g_ffn, bf, g_fin):
    w_in = get_w("w_in", mod)
    proj, h1 = rms_mod_matmul(x, g_mix, mod, 0, 1, w_in, D_IN // N_DEV, "proj_fwd", after=started)
    mix_a, u1 = conv_module_fwd(proj, wc, bc, lg, lb, gco, "conv_module_fwd")
    att, lse = attn_fwd_all(proj, "attn_fwd")
    w_out = get_w("w_out", att)
    y1, mixed = norm_concat_matmul(mix_a, att, gao, w_out, "out_proj_fwd")
    w_up = get_w("w_up", y1)
    up0, x1, h2 = resid_rms_mod_matmul(x, y1, g_ffn, mod, 2, 3, 4, w_up, "up_fwd")
    passed = get_w("w_down", up0, only_pass_on=True)
    act = ffn_act_fwd(up0, wf, bf, "ffn_act_fwd", after=passed)
    w_down = get_w("w_down", act)
    loss_t, dx2, dy2, d_gfin, d_gaf = matmul_loss_bwd(act, w_down, x1, tgt, g_fin, mod, 5, "down_fwd_loss")
    dact = matmul(dy2, w_down, "nt", F32, 512, FFN_TN, "down_bwd_x")
    dw_down = matmul(act, dy2, "tn", BF16, 256, D_MODEL, "down_bwd_w")
    dup0, dbf_g, dbf_v, dwf_g, dwf_v = ffn_bwd(up0, dact, wf, bf, "ffn_bwd", after=put_grad("w_down", dw_down))
    dw_up = matmul_tn_halves(dup0, h2, 256, "up_bwd_w")
    dx1, d_shf, d_scf, d_gffn, dy1, d_gam = matmul_rms_mod_bwd(
        dup0, w_up, x1, dx2, g_ffn, mod, 4, y1, 2, "up_bwd_x", after=put_grad("w_up", dw_up))
    dw_out = matmul(mixed, dy1, "tn", BF16, 256, D_MODEL, "out_proj_bwd_w")
    dmixed, do, dd, d_gao = matmul_combine_bwd(dy1, w_out, att, gao, "out_proj_bwd_x", after=put_grad("w_out", dw_out))
    dqkv = attn_bwd_all(proj, do, lse, dd, "attn_bwd")
    du1, d_gco, d_lg, d_lb, d_bc, d_wc = conv_module_bwd_a(proj, u1, dmixed, lg, lb, gco, "conv_module_bwd_a")
    dproj_a = conv_module_bwd_b(proj, du1, wc, "conv_module_bwd_b")
    dw_in = matmul_tn_pieces(dproj_a, dqkv, h1, "proj_bwd_w")
    dx, d_shm, d_scm, d_gmix = matmul_rms_mod_bwd(
        (dproj_a, dqkv), w_in, x, dx1, g_mix, mod, 1, None, 0, "proj_bwd_x", b_rows=D_IN // N_DEV,
        after=put_grad("w_in", dw_in))
    dmod = jnp.concatenate([d_shm, d_scm, d_gam, d_shf, d_scf, d_gaf], axis=1)
    small = dict(g_norm_mix=d_gmix, b_conv_dw=d_bc, ln_conv_g=d_lg, ln_conv_b=d_lb, g_conv_out=d_gco, g_attn_out=d_gao,
                 g_norm_ffn=d_gffn, b_ffn_dw=jnp.concatenate([dbf_g, dbf_v], axis=1), g_final=d_gfin,
                 w_conv_dw=d_wc, w_ffn_dw=jnp.concatenate([dwf_g, dwf_v], axis=1), dmod=dmod, loss=loss_t[0:1, 0:1])
    return dx, small


PACK_W = 7168
TAPS_PER_ROW = PACK_W // D_CONV
PACKED_AT = dict(
    b_ada=(0, 0, N_MOD * D_MODEL), g_norm_mix=(0, 6144, D_MODEL),
    b_ffn_dw=(1, 0, 2 * D_FF), g_norm_ffn=(1, 5632, D_MODEL), b_conv_dw=(1, 6656, D_CONV),
    g_final=(2, 5632, D_MODEL), ln_conv_g=(2, 6656, D_CONV),
    ln_conv_b=(3, 5632, D_CONV), g_conv_out=(3, 6144, D_CONV), g_attn_out=(3, 6656, D_ATTN))
SMALL_ORDER = list(PACKED_AT)
LOSS_AT = (4, 2 * D_FF)


def pack_small(t):
    cat = lambda *parts: jnp.concatenate(parts, axis=1)
    wf = t["w_ffn_dw"]
    taps = jnp.pad(t["w_conv_dw"].reshape(1, CONV_K * D_CONV), ((0, 0), (0, 3 * PACK_W - CONV_K * D_CONV)))
    return jnp.concatenate([
        cat(t["dmod"], t["g_norm_mix"]),
        cat(t["b_ffn_dw"], t["g_norm_ffn"], t["b_conv_dw"]),
        cat(wf[0:1], t["g_final"], t["ln_conv_g"]),
        cat(wf[1:2], t["ln_conv_b"], t["g_conv_out"], t["g_attn_out"]),
        cat(wf[2:3], jnp.pad(t["loss"], ((0, 0), (0, PACK_W - 2 * D_FF - 1)))),
        taps.reshape(3, PACK_W)], axis=0)


def small_adamw(parts, wmv, name):
    def body(*refs):
        p_ref = refs[0]
        ins = refs[1:1 + 3 * len(SMALL_ORDER)]
        outs = refs[1 + 3 * len(SMALL_ORDER):]
        g = p_ref[0]
        for k in range(1, N_DEV):
            g = g + p_ref[k]
        for i, n in enumerate(SMALL_ORDER):
            row, lane, width = PACKED_AT[n]
            gp = g[row:row + 1, lane:lane + width]
            w_ref, m_ref, v_ref = ins[3 * i:3 * i + 3]
            g_ref, d_ref, nm_ref, nv_ref = outs[4 * i:4 * i + 4]
            g_ref[...] = gp
            d_ref[...], nm_ref[...], nv_ref[...] = _adam(w_ref[...], gp, m_ref[...], v_ref[...])
        wc_ref, wf_ref, loss_ref = outs[4 * len(SMALL_ORDER):]
        for j in range(CONV_K):
            row, lane = 5 + j // TAPS_PER_ROW, (j % TAPS_PER_ROW) * D_CONV
            wc_ref[j:j + 1, :] = g[row:row + 1, lane:lane + D_CONV]
        wf_ref[...] = g[2:2 + FFN_K, 0:2 * D_FF]
        loss_ref[...] = jnp.broadcast_to(g[LOSS_AT[0]:LOSS_AT[0] + 1, LOSS_AT[1]:LOSS_AT[1] + 1], (8, 128))

    args, out_shape = [parts], []
    for n in SMALL_ORDER:
        args += list(wmv[n])
        out_shape += [jax.ShapeDtypeStruct(wmv[n][0].shape, F32)] * 4
    out_shape += [jax.ShapeDtypeStruct((CONV_K, D_CONV), F32), jax.ShapeDtypeStruct((FFN_K, 2 * D_FF), F32),
                  jax.ShapeDtypeStruct((8, 128), F32)]
    res = pl.pallas_call(body, out_shape=tuple(out_shape), name=name, compiler_params=_cp())(*args)
    per = {n: tuple(res[4 * i:4 * i + 4]) for i, n in enumerate(SMALL_ORDER)}
    return per, res[-3], res[-2], res[-1][0, 0]


def shard_adamw(items, name):
    def body(*refs):
        ins, outs = refs[:4 * len(items)], refs[4 * len(items):]
        for i in range(len(items)):
            g_ref, w_ref, m_ref, v_ref = ins[4 * i:4 * i + 4]
            og_ref, d_ref, nm_ref, nv_ref = outs[4 * i:4 * i + 4]
            og_ref[...] = g_ref[...]
            d_ref[...], nm_ref[...], nv_ref[...] = _adam(w_ref[...], g_ref[...], m_ref[...], v_ref[...])

    args = [a for item in items for a in item]
    out_shape = tuple(jax.ShapeDtypeStruct(item[1].shape, F32) for item in items for _ in range(4))
    res = pl.pallas_call(body, out_shape=out_shape, name=name, compiler_params=_cp())(*args)
    return [tuple(res[4 * i:4 * i + 4]) for i in range(len(items))]


def _shard(full, n_cols, me):
    return lax.dynamic_slice(full, (0, me * n_cols), (full.shape[0], n_cols))


WEIGHTS = ["w_ada", "b_ada", "g_norm_mix", "w_in", "w_conv_dw", "b_conv_dw", "ln_conv_g", "ln_conv_b", "g_conv_out",
           "g_attn_out", "w_out", "g_norm_ffn", "w_up", "w_ffn_dw", "b_ffn_dw", "w_down", "g_final"]


def kernel(x, c, w_ada, b_ada, g_norm_mix, w_in, w_conv_dw, b_conv_dw, ln_conv_g, ln_conv_b, g_conv_out, g_attn_out, w_out, g_norm_ffn, w_up, w_ffn_dw, b_ffn_dw, w_down, g_final, loss_target, m_w_ada, m_b_ada, m_g_norm_mix, m_w_in, m_w_conv_dw, m_b_conv_dw, m_ln_conv_g, m_ln_conv_b, m_g_conv_out, m_g_attn_out, m_w_out, m_g_norm_ffn, m_w_up, m_w_ffn_dw, m_b_ffn_dw, m_w_down, m_g_final, v_w_ada, v_b_ada, v_g_norm_mix, v_w_in, v_w_conv_dw, v_b_conv_dw, v_ln_conv_g, v_ln_conv_b, v_g_conv_out, v_g_attn_out, v_w_out, v_g_norm_ffn, v_w_up, v_w_ffn_dw, v_b_ffn_dw, v_w_down, v_g_final):
    args = dict(locals())
    me = 4 * lax.axis_index("x") + 2 * lax.axis_index("y") + lax.axis_index("c")
    me1 = me.astype(jnp.int32).reshape(1)

    def flat(name, prefix=""):
        a = args[prefix + name]
        return a.reshape(a.shape[-2] if a.ndim > 1 else 1, a.shape[-1])

    def flat_t(name, prefix=""):
        return args[prefix + name][0].T

    n_in, n_up, r_out, r_down = w_in.shape[2], w_up.shape[2], w_out.shape[1], w_down.shape[1]
    n_ada, n_wc, n_wf = w_ada.shape[2], w_conv_dw.shape[2], w_ffn_dw.shape[2]
    taps_c = jnp.pad(flat("w_conv_dw").reshape(1, CONV_K * n_wc), ((0, 0), (0, 2 * D_MODEL - CONV_K * n_wc)))
    taps_f = jnp.pad(flat("w_ffn_dw").reshape(1, FFN_K * n_wf), ((0, 0), (0, 3 * D_MODEL - FFN_K * n_wf)))
    first = jnp.concatenate([c, taps_c.reshape(2, D_MODEL), taps_f.reshape(3, D_MODEL), jnp.zeros((2, D_MODEL), F32)], axis=0)
    w_in_block = flat_t("w_in").astype(BF16)
    hi = lax.reduce_precision(first, 8, 7)
    mid = lax.reduce_precision(first - hi, 8, 7)
    low = lax.reduce_precision(first - hi - mid, 8, 7)
    terms = jnp.concatenate([hi, mid, low, jnp.zeros((8, D_MODEL), F32)], axis=0).astype(BF16)
    first_block = all_gather(jnp.concatenate([w_in_block, terms], axis=0), "gather_c_taps_w_in")
    terms = first_block[:, n_in:n_in + 24, :].astype(F32)
    first_all = (terms[:, 0:8] + terms[:, 8:16]) + terms[:, 16:24]
    c_all = first_all[:, 0, :]
    wc_full = first_all[:, 1:3, :].reshape(N_DEV, 2 * D_MODEL)[:, :CONV_K * n_wc].reshape(N_DEV, CONV_K, n_wc)
    wc_full = wc_full.transpose(1, 0, 2).reshape(CONV_K, D_CONV)
    wf_full = first_all[:, 3:6, :].reshape(N_DEV, 3 * D_MODEL)[:, :FFN_K * n_wf].reshape(N_DEV, FFN_K, n_wf)
    wf_full = wf_full.transpose(1, 0, 2).reshape(FFN_K, 2 * D_FF)
    mod_cols = ada_fwd(c_all, flat("w_ada"), _shard(flat("b_ada"), n_ada, me), "ada_fwd")
    mod_all = all_gather(mod_cols, "gather_mod")
    mod = lax.dynamic_index_in_dim(mod_all, me, axis=1, keepdims=False).reshape(N_MOD, D_MODEL)
    mod = jnp.pad(mod, ((0, 2), (0, 0)))

    order = ("w_out", "w_up", "w_down")
    blocks = dict(w_up=flat_t("w_up").astype(BF16), w_out=flat("w_out").astype(BF16), w_down=flat("w_down").astype(BF16))
    handles, tok = gather2_start([blocks[name] for name in order], "gather_weights_start", mod_all)
    gathers = dict(zip(order, handles))

    def pass_on(name, after):
        if "send2" not in gathers[name]:
            group = ("w_out", "w_up") if name != "w_down" else ("w_down",)
            passed, token = gather2_pass([gathers[w] for w in group], after, f"gather_{name}_pass")
            gathers.update(zip(group, passed))
            return token

    def gathered(name, after):
        pass_on(name, after)
        land = gather2_wait(gathers[name], after, f"gather_{name}_wait")
        return lax.dynamic_update_index_in_dim(land, blocks[name], me, axis=0)

    def get_w(name, after, only_pass_on=False):
        if name == "w_in":
            return first_block
        if only_pass_on:
            return pass_on(name, after)
        return gathered(name, after).reshape(-1, D_MODEL)

    exchanges = {}

    def put_grad(name, dw, after=None):
        dev_major = dw.reshape(N_DEV, -1, D_MODEL)
        (exchanges[name],), token = exchange_start([dev_major], False, f"exchange_{name}_start", after)
        return token

    grad_x, small = local_step(
        x[0], loss_target[0], mod, tok, get_w, put_grad, wc_full, wf_full,
        flat("g_norm_mix"), flat("b_conv_dw"), flat("ln_conv_g"), flat("ln_conv_b"), flat("g_conv_out"),
        flat("g_attn_out"), flat("g_norm_ffn"), flat("b_ffn_dw"), flat("g_final"))

    out = {}

    def finish(name, tr, after, then=None):
        mine, parts = exchange_wait(exchanges[name], after, False, f"exchange_{name}_wait")
        if then is not None:
            then(parts)
        if name in ("w_in", "w_up"):
            res = sum_adamw(parts, mine, me1, flat_t(name), flat_t(name, "m_"), flat_t(name, "v_"), tr, "adamw_" + name)
            out[name] = tuple(r.T for r in res)
        else:
            res = out[name] = sum_adamw(parts, mine, me1, flat(name), flat(name, "m_"), flat(name, "v_"), tr,
                                        "adamw_" + name)
        return res[0]

    after = finish("w_down", r_down, grad_x)
    after = finish("w_up", n_up // 2, after)
    after = finish("w_out", r_out, after)
    packed = pack_small(small)
    small_gather = []
    after = finish("w_in", n_in, after, then=lambda parts: small_gather.append(
        gather2_start([packed], "gather_small_start", parts)[0][0]))
    (handle,), _ = gather2_pass(small_gather, after, "gather_small_pass")
    small_all = lax.dynamic_update_index_in_dim(
        gather2_wait(handle, after, "gather_small_wait"), packed, me, axis=0)

    wmv = {n: (flat(n), flat(n, "m_"), flat(n, "v_")) for n in SMALL_ORDER}
    per, g_wc, g_wf, loss = small_adamw(small_all, wmv, "adamw_small")
    out.update(per)
    taps = shard_adamw([(_shard(g_wc, n_wc, me), flat("w_conv_dw"), flat("w_conv_dw", "m_"), flat("w_conv_dw", "v_")),
                        (_shard(g_wf, n_wf, me), flat("w_ffn_dw"), flat("w_ffn_dw", "m_"), flat("w_ffn_dw", "v_"))],
                       "adamw_taps")
    out["w_conv_dw"], out["w_ffn_dw"] = taps

    dmod_cols = _shard(small_all[:, 0, :], n_ada, me)
    out["w_ada"] = ada_bwd_adamw(c_all, dmod_cols, flat("w_ada"), flat("w_ada", "m_"), flat("w_ada", "v_"), "adamw_w_ada")

    result = [loss, grad_x[None]]
    for k in range(4):
        result += [out[n][k].reshape(args[n].shape) for n in WEIGHTS]
    return tuple(result)
```

```python
import jax
import jax.numpy as jnp
from jax import lax
from jax.experimental import pallas as pl
from jax.experimental.pallas import tpu as pltpu

F32 = jnp.float32
BF16 = jnp.bfloat16

N_DEV = 8
SEQ = 2048
D_MODEL = 1024
D_CONV = 512
D_ATTN = 512
HEAD_DIM = 64
CONV_K = 31
D_FF = 2816
FFN_K = 3
D_IN = 2 * D_CONV + 3 * D_ATTN
N_MOD = 6
EPS = 1e-6
ATTN_BLOCK = 128
PATTERNS = ((2048, 1), (512, 4), (128, 16))
NEG = -1e30

ADAM_LR, ADAM_B1, ADAM_B2, ADAM_EPS, ADAM_WD, ADAM_STEP = 0.001, 0.9, 0.999, 1e-08, 0.01, 10

ROWS = 256
CONV_HALO = 32
FFN_HALO = 8
FFN_TN = 1408
VMEM_LIMIT = 56 * 1024 * 1024


NT = (((1,), (1,)), ((), ()))
ANY = pl.BlockSpec(memory_space=pl.ANY)


def _cp(*sem):
    return pltpu.CompilerParams(dimension_semantics=sem if sem else None, vmem_limit_bytes=VMEM_LIMIT)


def _with_after(body, n_in, after):
    if after is None:
        return body, [], []
    return (lambda *refs: body(*refs[:n_in], *refs[n_in + 1:])), [ANY], [after]


def _sig(x):
    return 1.0 / (1.0 + jnp.exp(-x))


def _rsum(x):
    return jnp.sum(x, axis=0, keepdims=True)


def _mean(x):
    return jnp.mean(x, axis=-1, keepdims=True)


def _acc(ref, val, first):
    @pl.when(first)
    def _():
        ref[...] = val

    @pl.when(jnp.logical_not(first))
    def _():
        ref[...] += val


SUB = 16


def _for_chunks(fn, unroll=1, rows=ROWS):
    def step(i, carry):
        fn(pl.ds(pl.multiple_of(i * SUB, SUB), SUB))
        return carry

    lax.fori_loop(0, rows // SUB, step, 0, unroll=unroll)


PAIR = 2 * ROWS


def _pair_spec(width, col=0):
    return pl.BlockSpec((PAIR, width), lambda i: (i, col))


def _halves():
    return [slice(h * ROWS, (h + 1) * ROWS) for h in range(2)]


def rms_mod_matmul(x, g, mod, sh_row, sc_row, b, b_rows, name, after=None):
    n = N_DEV * b_rows

    def body(x_ref, g_ref, mod_ref, b_ref, o_ref, h_ref):
        for rs in _halves():
            xx = x_ref[rs, :]
            r = lax.rsqrt(_mean(xx * xx) + EPS)
            h = (xx * r * g_ref[...] * (1.0 + mod_ref[sc_row:sc_row + 1, :]) + mod_ref[sh_row:sh_row + 1, :]).astype(BF16)
            h_ref[rs, :] = h
            o_ref[rs, :] = lax.dot_general(h, b_ref[...].reshape(n, D_MODEL), NT, preferred_element_type=F32)

    body, more_specs, more = _with_after(body, 4, after)
    return pl.pallas_call(
        body, out_shape=(jax.ShapeDtypeStruct((SEQ, n), F32), jax.ShapeDtypeStruct((SEQ, D_MODEL), BF16)),
        grid=(SEQ // PAIR,),
        in_specs=[_pair_spec(D_MODEL), _vec_spec(D_MODEL), _vec_spec(D_MODEL, 8),
                  pl.BlockSpec((N_DEV, b_rows, D_MODEL), lambda i: (0, 0, 0))] + more_specs,
        out_specs=(_pair_spec(n), _pair_spec(D_MODEL)), name=name, compiler_params=_cp("parallel"))(x, g, mod, b, *more)


def norm_concat_matmul(mix_a, att, gao, w, name):
    def body(a_ref, att_ref, g_ref, w_ref, y_ref, mixed_ref):
        for rs in _halves():
            aa = att_ref[rs, :]
            mixed_ref[rs, 0:D_CONV] = a_ref[rs, :]
            mixed_ref[rs, D_CONV:] = (aa * lax.rsqrt(_mean(aa * aa) + EPS) * g_ref[...]).astype(BF16)
            y_ref[rs, :] = jnp.dot(mixed_ref[rs, :], w_ref[...], preferred_element_type=F32)

    return pl.pallas_call(
        body, out_shape=(jax.ShapeDtypeStruct((SEQ, D_MODEL), F32), jax.ShapeDtypeStruct((SEQ, D_MODEL), BF16)),
        grid=(SEQ // PAIR,),
        in_specs=[_pair_spec(D_CONV), _pair_spec(D_ATTN), _vec_spec(D_ATTN), pl.BlockSpec(w.shape, lambda i: (0, 0))],
        out_specs=(_pair_spec(D_MODEL), _pair_spec(D_MODEL)), name=name, compiler_params=_cp("parallel"))(mix_a, att, gao, w)


def resid_rms_mod_matmul(x, y, g, mod, ga_row, sh_row, sc_row, w, name):
    n = w.shape[0]

    def body(x_ref, y_ref, g_ref, mod_ref, w_ref, o_ref, x1_ref, h_ref):
        for rs in _halves():
            x1 = x_ref[rs, :] + mod_ref[ga_row:ga_row + 1, :] * y_ref[rs, :]
            x1_ref[rs, :] = x1
            r = lax.rsqrt(_mean(x1 * x1) + EPS)
            h = (x1 * r * g_ref[...] * (1.0 + mod_ref[sc_row:sc_row + 1, :]) + mod_ref[sh_row:sh_row + 1, :]).astype(BF16)
            h_ref[rs, :] = h
            o_ref[rs, :] = lax.dot_general(h, w_ref[...], NT, preferred_element_type=F32)

    return pl.pallas_call(
        body,
        out_shape=(jax.ShapeDtypeStruct((SEQ, n), F32), jax.ShapeDtypeStruct((SEQ, D_MODEL), F32),
                   jax.ShapeDtypeStruct((SEQ, D_MODEL), BF16)),
        grid=(SEQ // PAIR,),
        in_specs=[_pair_spec(D_MODEL), _pair_spec(D_MODEL), _vec_spec(D_MODEL), _vec_spec(D_MODEL, 8),
                  pl.BlockSpec(w.shape, lambda i: (0, 0))],
        out_specs=(_pair_spec(n), _pair_spec(D_MODEL), _pair_spec(D_MODEL)),
        name=name, compiler_params=_cp("parallel"))(x, y, g, mod, w)


def matmul_combine_bwd(dy, w, att, gao, name, after=None):
    def body(dy_ref, w_ref, att_ref, g_ref, dm_ref, do_ref, dd_ref, dg_ref):
        @pl.when(pl.program_id(0) == 0)
        def _():
            dg_ref[...] = jnp.zeros_like(dg_ref)

        same_head = (jnp.right_shift(lax.broadcasted_iota(jnp.int32, (D_ATTN, D_ATTN), 0), 6)
                     == jnp.right_shift(lax.broadcasted_iota(jnp.int32, (D_ATTN, D_ATTN), 1), 6)).astype(F32)
        for rs in _halves():
            dmixed = lax.dot_general(dy_ref[rs, :], w_ref[...], NT, preferred_element_type=F32)
            dm_ref[rs, :] = dmixed
            att = att_ref[rs, :]
            r = lax.rsqrt(_mean(att * att) + EPS)
            xn = att * r
            dm = dmixed[:, D_CONV:]
            dg_ref[...] += _rsum(dm * xn)
            dyn = dm * g_ref[...]
            do = r * (dyn - xn * _mean(dyn * xn))
            do_ref[rs, :] = do
            dd_ref[rs, :] = jnp.dot(do * att, same_head, preferred_element_type=F32, precision=lax.Precision.HIGHEST)

    rs = _pair_spec(D_ATTN)
    f = jax.ShapeDtypeStruct((SEQ, D_ATTN), F32)
    body, more_specs, more = _with_after(body, 4, after)
    return pl.pallas_call(
        body, out_shape=(jax.ShapeDtypeStruct((SEQ, D_MODEL), F32), f, f, jax.ShapeDtypeStruct((1, D_ATTN), F32)),
        grid=(SEQ // PAIR,),
        in_specs=[_pair_spec(D_MODEL), pl.BlockSpec(w.shape, lambda i: (0, 0)), rs, _vec_spec(D_ATTN)] + more_specs,
        out_specs=(_pair_spec(D_MODEL), rs, rs, _vec_spec(D_ATTN)),
        name=name, compiler_params=_cp("arbitrary"))(dy, w, att, gao, *more)


def matmul_loss_bwd(act, w, x1, tgt, g, mod, ga_row, name):
    def body(a_ref, w_ref, x1_ref, t_ref, g_ref, mod_ref, loss_ref, dx2_ref, dy2_ref, dg_ref, dga_ref):
        @pl.when(pl.program_id(0) == 0)
        def _():
            loss_ref[...] = jnp.zeros_like(loss_ref)
            dg_ref[...] = jnp.zeros_like(dg_ref)
            dga_ref[...] = jnp.zeros_like(dga_ref)

        ga = mod_ref[ga_row:ga_row + 1, :]
        for rs in _halves():
            y2 = jnp.dot(a_ref[rs, :], w_ref[...], preferred_element_type=F32)
            x2 = x1_ref[rs, :] + ga * y2
            r = lax.rsqrt(_mean(x2 * x2) + EPS)
            xn = x2 * r
            err = xn * g_ref[...] - t_ref[rs, :]
            loss_ref[...] += jnp.broadcast_to(0.5 * jnp.sum(_mean(err * err)), (8, 128))
            dy = err * (1.0 / D_MODEL)
            dg_ref[...] += _rsum(dy * xn)
            dxn = dy * g_ref[...]
            dx2 = r * (dxn - xn * _mean(dxn * xn))
            dx2_ref[rs, :] = dx2
            dy2_ref[rs, :] = (dx2 * ga).astype(BF16)
            dga_ref[...] += _rsum(dx2 * y2)

    vec = jax.ShapeDtypeStruct((1, D_MODEL), F32)
    rows = _pair_spec
    return pl.pallas_call(
        body,
        out_shape=(jax.ShapeDtypeStruct((8, 128), F32), jax.ShapeDtypeStruct((SEQ, D_MODEL), F32),
                   jax.ShapeDtypeStruct((SEQ, D_MODEL), BF16), vec, vec),
        grid=(SEQ // PAIR,),
        in_specs=[rows(act.shape[1]), pl.BlockSpec(w.shape, lambda i: (0, 0)), rows(D_MODEL), rows(D_MODEL),
                  _vec_spec(D_MODEL), _vec_spec(D_MODEL, 8)],
        out_specs=(pl.BlockSpec((8, 128), lambda i: (0, 0)), rows(D_MODEL), rows(D_MODEL),
                   _vec_spec(D_MODEL), _vec_spec(D_MODEL)),
        name=name, compiler_params=_cp("arbitrary"))(act, w, x1, tgt, g, mod)


def matmul_rms_mod_bwd(a, b, x, dres, g, mod, sc_row, y, ga_row, name, b_rows=None, after=None):
    gated = y is not None
    pieces = isinstance(a, tuple)
    tm = PAIR
    blocks = [slice(h * ROWS, (h + 1) * ROWS) for h in range(tm // ROWS)]
    rows = lambda width: pl.BlockSpec((tm, width), lambda i: (i, 0))
    if pieces:
        a1, a3 = a
        a_args = [a1, a3]
        a_specs = [rows(a1.shape[1]), pl.BlockSpec((3, tm, a3.shape[2]), lambda i: (0, i, 0))]
        b_arg, b_spec = b, pl.BlockSpec((N_DEV, b_rows, D_MODEL), lambda i: (0, 0, 0))
    else:
        k2 = a.shape[2]
        a_args = [a]
        a_specs = [pl.BlockSpec((2, tm, k2), lambda i: (0, i, 0))]
        b_arg, b_spec = b.reshape(2, k2, D_MODEL), pl.BlockSpec((2, k2, D_MODEL), lambda i: (0, 0, 0))
    n_a = len(a_args)

    def body(*refs):
        a_refs, (b_ref, x_ref, dres_ref, g_ref, mod_ref) = refs[:n_a], refs[n_a:n_a + 5]
        if gated:
            y_ref, dx_ref, dsh_ref, dsc_ref, dg_ref, dy_ref, dga_ref = refs[n_a + 5:]
        else:
            dx_ref, dsh_ref, dsc_ref, dg_ref = refs[n_a + 5:]

        @pl.when(pl.program_id(0) == 0)
        def _():
            for ref in (dsh_ref, dsc_ref, dg_ref) + ((dga_ref,) if gated else ()):
                ref[...] = jnp.zeros_like(ref)

        gg = g_ref[...]
        for rs in blocks:
            if pieces:
                wv = b_ref[...].reshape(N_DEV * b_rows, D_MODEL)
                k1, k3 = a_refs[0].shape[1], a_refs[1].shape[2]
                dh = jnp.dot(a_refs[0][rs, :], wv[0:k1], preferred_element_type=F32)
                for t in range(3):
                    dh = dh + jnp.dot(a_refs[1][t, rs, :], wv[k1 + t * k3:k1 + (t + 1) * k3], preferred_element_type=F32)
            else:
                dh = (jnp.dot(a_refs[0][0, rs, :], b_ref[0], preferred_element_type=F32)
                      + jnp.dot(a_refs[0][1, rs, :], b_ref[1], preferred_element_type=F32))
            xx = x_ref[rs, :]
            r = lax.rsqrt(_mean(xx * xx) + EPS)
            xn = xx * r
            dsh_ref[...] += _rsum(dh)
            dsc_ref[...] += _rsum(dh * (xn * gg))
            dt = dh * (1.0 + mod_ref[sc_row:sc_row + 1, :])
            dg_ref[...] += _rsum(dt * xn)
            dxn = dt * gg
            dx = dres_ref[rs, :] + r * (dxn - xn * _mean(dxn * xn))
            dx_ref[rs, :] = dx
            if gated:
                dga_ref[...] += _rsum(dx * y_ref[rs, :])
                dy_ref[rs, :] = (dx * mod_ref[ga_row:ga_row + 1, :]).astype(BF16)

    vec = jax.ShapeDtypeStruct((1, D_MODEL), F32)
    in_specs = a_specs + [b_spec, rows(D_MODEL), rows(D_MODEL), _vec_spec(D_MODEL), _vec_spec(D_MODEL, 8)]
    out_shape = [jax.ShapeDtypeStruct((SEQ, D_MODEL), F32), vec, vec, vec]
    out_specs = [rows(D_MODEL), _vec_spec(D_MODEL), _vec_spec(D_MODEL), _vec_spec(D_MODEL)]
    args = a_args + [b_arg, x, dres, g, mod]
    if gated:
        in_specs.append(rows(D_MODEL))
        out_shape += [jax.ShapeDtypeStruct((SEQ, D_MODEL), BF16), vec]
        out_specs += [rows(D_MODEL), _vec_spec(D_MODEL)]
        args.append(y)
    body, more_specs, more = _with_after(body, len(args), after)
    return pl.pallas_call(
        body, out_shape=tuple(out_shape), grid=(SEQ // tm,), in_specs=in_specs + more_specs, out_specs=tuple(out_specs),
        name=name, compiler_params=_cp("arbitrary"))(*args, *more)


def _prev_halo(halo, width, col):
    per = ROWS // halo
    return pl.BlockSpec((halo, width), lambda i: (jnp.maximum(i * per - 1, 0), col))


def _next_halo(halo, width, col):
    per = ROWS // halo
    last = SEQ // halo - 1
    return pl.BlockSpec((halo, width), lambda i: (jnp.minimum((i + 1) * per, last), col))


CONV_PAD = ROWS + CONV_HALO


def _shift_copies(sh):
    for b in range(1, 8):
        sh[b, 0:CONV_PAD - 8, :] = sh[0, pl.ds(b, CONV_PAD - 8), :]


def _tap(sh, rs_start, offset):
    return sh[offset % 8, pl.ds(pl.multiple_of(rs_start + (offset // 8) * 8, 8), SUB), :]


def _conv_glu(av_ref, ag_ref, avh_ref, agh_ref, sh):
    i = pl.program_id(0)
    hv = avh_ref[...] * _sig(agh_ref[...])
    sh[0, 0:CONV_HALO, :] = jnp.where(i > 0, hv, 0.0)

    def glu(rs):
        sh[0, pl.ds(pl.multiple_of(rs.start + CONV_HALO, SUB), SUB), :] = av_ref[rs, :] * _sig(ag_ref[rs, :])

    _for_chunks(glu)
    _shift_copies(sh)


def _conv_norm(u1, lg_ref, lb_ref):
    mu = _mean(u1)
    cen = u1 - mu
    rs = lax.rsqrt(_mean(cen * cen) + EPS)
    z = cen * rs
    ln = z * lg_ref[...] + lb_ref[...]
    s = _sig(ln)
    return z, rs, ln, s, ln * s


def conv_module_fwd(proj, wc, bc, lg, lb, gco, name):
    def body(av_ref, ag_ref, avh_ref, agh_ref, wc_ref, bc_ref, lg_ref, lb_ref, gco_ref, out_ref, u1_ref, sh):
        _conv_glu(av_ref, ag_ref, avh_ref, agh_ref, sh)

        def conv(rs):
            u1 = jnp.broadcast_to(bc_ref[...], (SUB, D_CONV))
            for j in range(CONV_K):
                u1 = u1 + wc_ref[j:j + 1, :] * _tap(sh, rs.start, CONV_HALO - (CONV_K - 1) + j)
            u1_ref[rs, :] = u1

        _for_chunks(conv)
        _, _, _, _, u2 = _conv_norm(u1_ref[...], lg_ref, lb_ref)
        rc = lax.rsqrt(_mean(u2 * u2) + EPS)
        out_ref[...] = (u2 * rc * gco_ref[...]).astype(BF16)

    v = _vec_spec(D_CONV)
    return pl.pallas_call(
        body, out_shape=(jax.ShapeDtypeStruct((SEQ, D_CONV), BF16), jax.ShapeDtypeStruct((SEQ, D_CONV), F32)),
        grid=(SEQ // ROWS,),
        in_specs=[_row_spec(D_CONV, 0), _row_spec(D_CONV, 1), _prev_halo(CONV_HALO, D_CONV, 0),
                  _prev_halo(CONV_HALO, D_CONV, 1), _vec_spec(D_CONV, CONV_K), v, v, v, v],
        out_specs=(_row_spec(D_CONV), _row_spec(D_CONV)), scratch_shapes=[pltpu.VMEM((8, CONV_PAD, D_CONV), F32)],
        name=name, compiler_params=_cp("parallel"))(proj, proj, proj, proj, wc, bc, lg, lb, gco)


def conv_module_bwd_a(proj, u1, dmixed, lg, lb, gco, name):
    def body(av_ref, ag_ref, avh_ref, agh_ref, u1_ref, dm_ref, lg_ref, lb_ref, gco_ref,
             du1_ref, dgco_ref, dlg_ref, dlb_ref, dbc_ref, dwc_ref, sh, acc):
        first = pl.program_id(0) == 0
        _conv_glu(av_ref, ag_ref, avh_ref, agh_ref, sh)
        z, rs, ln, s, u2 = _conv_norm(u1_ref[...], lg_ref, lb_ref)
        rc = lax.rsqrt(_mean(u2 * u2) + EPS)
        xn = u2 * rc
        dm = dm_ref[...]
        _acc(dgco_ref, _rsum(dm * xn), first)
        dyn = dm * gco_ref[...]
        du2 = rc * (dyn - xn * _mean(dyn * xn))
        dln = du2 * (s * (1.0 + ln * (1.0 - s)))
        _acc(dlg_ref, _rsum(dln * z), first)
        _acc(dlb_ref, _rsum(dln), first)
        dz = dln * lg_ref[...]
        du1 = rs * (dz - _mean(dz) - z * _mean(dz * z))
        du1_ref[...] = du1
        _acc(dbc_ref, _rsum(du1), first)
        acc[...] = jnp.zeros_like(acc)

        def taps(rs):
            d = du1_ref[rs, :]
            for j in range(CONV_K):
                acc[j] += d * _tap(sh, rs.start, CONV_HALO - (CONV_K - 1) + j)

        _for_chunks(taps)

        @pl.when(first)
        def _():
            dwc_ref[...] = jnp.zeros_like(dwc_ref)

        for j in range(CONV_K):
            dwc_ref[j:j + 1, :] += _rsum(acc[j])

    v = _vec_spec(D_CONV)
    vec = jax.ShapeDtypeStruct((1, D_CONV), F32)
    return pl.pallas_call(
        body,
        out_shape=(jax.ShapeDtypeStruct((SEQ, D_CONV), F32), vec, vec, vec, vec, jax.ShapeDtypeStruct((CONV_K, D_CONV), F32)),
        grid=(SEQ // ROWS,),
        in_specs=[_row_spec(D_CONV, 0), _row_spec(D_CONV, 1), _prev_halo(CONV_HALO, D_CONV, 0),
                  _prev_halo(CONV_HALO, D_CONV, 1), _row_spec(D_CONV, 0), _row_spec(D_CONV, 0), v, v, v],
        out_specs=(_row_spec(D_CONV), v, v, v, v, _vec_spec(D_CONV, CONV_K)),
        scratch_shapes=[pltpu.VMEM((8, CONV_PAD, D_CONV), F32), pltpu.VMEM((CONV_K, SUB, D_CONV), F32)],
        name=name, compiler_params=_cp("arbitrary"))(proj, proj, proj, proj, u1, dmixed, lg, lb, gco)


def conv_module_bwd_b(proj, du1, wc, name):
    def body(av_ref, ag_ref, du1_ref, du1n_ref, wc_ref, out_ref, sh):
        i = pl.program_id(0)
        sh[0, 0:ROWS, :] = du1_ref[...]
        sh[0, ROWS:, :] = jnp.where(i < SEQ // ROWS - 1, du1n_ref[...], 0.0)
        _shift_copies(sh)

        def chunk(rs):
            du0 = jnp.zeros((SUB, D_CONV), F32)
            for j in range(CONV_K):
                du0 = du0 + wc_ref[j:j + 1, :] * _tap(sh, rs.start, CONV_K - 1 - j)
            sg = _sig(ag_ref[rs, :])
            out_ref[rs, 0:D_CONV] = (du0 * sg).astype(BF16)
            out_ref[rs, D_CONV:] = (du0 * av_ref[rs, :] * sg * (1.0 - sg)).astype(BF16)

        _for_chunks(chunk)

    return pl.pallas_call(
        body, out_shape=jax.ShapeDtypeStruct((SEQ, 2 * D_CONV), BF16), grid=(SEQ // ROWS,),
        in_specs=[_row_spec(D_CONV, 0), _row_spec(D_CONV, 1), _row_spec(D_CONV, 0), _next_halo(CONV_HALO, D_CONV, 0),
                  _vec_spec(D_CONV, CONV_K)],
        out_specs=_row_spec(2 * D_CONV), scratch_shapes=[pltpu.VMEM((8, CONV_PAD, D_CONV), F32)],
        name=name, compiler_params=_cp("parallel"))(proj, proj, du1, du1, wc)


def _rows(start, size, r):
    return pl.ds(start, size) if r == 1 else pl.ds(start, size, stride=r)


def _unit_rows(r, rho, n, nb):
    win = 2 * ATTN_BLOCK if nb > 1 else ATTN_BLOCK
    if isinstance(n, int):
        kb = max(n - 1, 0)
        q_rows = _rows(rho + r * ATTN_BLOCK * n, ATTN_BLOCK, r)
        k_rows = _rows(rho + r * ATTN_BLOCK * kb, win, r)
    else:
        kb = jnp.maximum(n - 1, 0)
        q_rows = pl.ds(pl.multiple_of(n * ATTN_BLOCK, ATTN_BLOCK), ATTN_BLOCK)
        k_rows = pl.ds(pl.multiple_of(kb * ATTN_BLOCK, ATTN_BLOCK), win)
    return q_rows, k_rows, n - kb


def _band_bias(first_block, transposed):
    shape = (2 * ATTN_BLOCK, ATTN_BLOCK) if transposed else (ATTN_BLOCK, 2 * ATTN_BLOCK)
    q_axis = 1 if transposed else 0
    dist = (0 if first_block else ATTN_BLOCK) + lax.broadcasted_iota(jnp.int32, shape, q_axis) \
        - lax.broadcasted_iota(jnp.int32, shape, 1 - q_axis)
    return jnp.where((dist >= 0) & (dist <= ATTN_BLOCK), 0.0, NEG)


def _per_head(x):
    lane = lax.broadcasted_iota(jnp.int32, x.shape, 1)
    zero = jnp.zeros_like(x)
    return [jnp.where(lane < HEAD_DIM, x, zero), jnp.where(lane >= HEAD_DIM, x, zero)]


SCALE = HEAD_DIM ** -0.5


def _masked_scores(q2, k2, bias):
    return [lax.dot_general(qh, k2, NT, preferred_element_type=F32) + bias for qh in _per_head(q2)]


def _attn_units(r, nb, unit):
    if r == 1:
        def four(i, carry):
            for k in range(4):
                unit(0, 4 * i + k)
            return carry
        lax.fori_loop(0, nb // 4, four, 0)
    else:
        for rho in range(r):
            for n in range(nb):
                unit(rho, n)


N_UNITS = 16


def attn_fwd_all(proj, name):
    def body(q_ref, k_ref, v_ref, att_ref, lse_ref, s_scr, p_scr, lse_scr, den_scr, bias_scr):
        bias_scr[0] = _band_bias(True, False)
        bias_scr[1] = _band_bias(False, False)
        for idx, (sub_len, r) in enumerate(PATTERNS):
            nb = sub_len // ATTN_BLOCK
            win = 2 * ATTN_BLOCK if nb > 1 else ATTN_BLOCK

            def scores(rho, n, r=r, nb=nb, win=win):
                u = rho * nb + n
                q_rows, k_rows, variant = _unit_rows(r, rho, n, nb)
                ss = _masked_scores((q_ref[q_rows, :] * SCALE).astype(BF16), k_ref[k_rows, :].astype(BF16),
                                    bias_scr[variant, :, 0:win])
                for h in range(2):
                    s_scr[2 * u + h, :, 0:win] = ss[h]

            _attn_units(r, nb, scores)

            def softmax(u, carry, win=win):
                lses, dens = [], []
                for h in range(2):
                    sc = s_scr[2 * u + h, :, 0:win]
                    m = jnp.max(sc, axis=1, keepdims=True)
                    p = jnp.exp(sc - m)
                    den = jnp.sum(p, axis=1, keepdims=True)
                    p_scr[2 * u + h, :, 0:win] = p.astype(BF16)
                    lses.append(jnp.broadcast_to(m + jnp.log(den), (ATTN_BLOCK, HEAD_DIM)))
                    dens.append(jnp.broadcast_to(den, (ATTN_BLOCK, HEAD_DIM)))
                lse_scr[u] = jnp.concatenate(lses, axis=1)
                den_scr[u] = jnp.concatenate(dens, axis=1)
                return carry

            lax.fori_loop(0, N_UNITS, softmax, 0, unroll=2)

            def outputs(rho, n, r=r, nb=nb, win=win, idx=idx):
                u = rho * nb + n
                q_rows, k_rows, _ = _unit_rows(r, rho, n, nb)
                vs = _per_head(v_ref[k_rows, :].astype(BF16))
                o = (jnp.dot(p_scr[2 * u, :, 0:win], vs[0], preferred_element_type=F32)
                     + jnp.dot(p_scr[2 * u + 1, :, 0:win], vs[1], preferred_element_type=F32)) / den_scr[u]
                lse = lse_scr[u]
                if idx > 0:
                    old = lse_ref[q_rows, :]
                    top = jnp.maximum(old, lse)
                    new = top + jnp.log(jnp.exp(old - top) + jnp.exp(lse - top))
                    o = att_ref[q_rows, :] * jnp.exp(old - new) + o * jnp.exp(lse - new)
                    lse = new
                att_ref[q_rows, :] = o
                lse_ref[q_rows, :] = lse

            _attn_units(r, nb, outputs)

    blk = lambda first: pl.BlockSpec((SEQ, 128), lambda g: (0, first + g))
    shp = jax.ShapeDtypeStruct((SEQ, D_ATTN), F32)
    big = (2 * N_UNITS, ATTN_BLOCK, 2 * ATTN_BLOCK)
    small = pltpu.VMEM((N_UNITS, ATTN_BLOCK, 128), F32)
    return pl.pallas_call(
        body, out_shape=(shp, shp), grid=(4,), in_specs=[blk(8), blk(12), blk(16)], out_specs=(blk(0), blk(0)),
        scratch_shapes=[pltpu.VMEM(big, F32), pltpu.VMEM(big, BF16), small, small,
                        pltpu.VMEM((2, ATTN_BLOCK, 2 * ATTN_BLOCK), F32)],
        name=name, compiler_params=_cp("parallel"))(proj, proj, proj)


def attn_bwd_all(proj, do, lse, dd, name):
    def body(q_ref, k_ref, v_ref, do_ref, l_ref, dd_ref, out_ref, dq_s, dk_s, dv_s,
             s_scr, dp_scr, ds_scr, st_scr, dpt_scr, pt_scr, dst_scr, qb_scr, kb_scr, dob_scr, bias_scr, bias_t_scr):
        for first_block in (True, False):
            bias_scr[1 - int(first_block)] = _band_bias(first_block, False)
            bias_t_scr[1 - int(first_block)] = _band_bias(first_block, True)
        dq_s[...] = jnp.zeros_like(dq_s)
        dk_s[...] = jnp.zeros_like(dk_s)
        dv_s[...] = jnp.zeros_like(dv_s)
        for sub_len, r in PATTERNS:
            nb = sub_len // ATTN_BLOCK
            win = 2 * ATTN_BLOCK if nb > 1 else ATTN_BLOCK

            def scores(rho, n, r=r, nb=nb, win=win):
                u = rho * nb + n
                q_rows, k_rows, variant = _unit_rows(r, rho, n, nb)
                bias, bias_t = bias_scr[variant, :, 0:win], bias_t_scr[variant, 0:win, :]
                q2 = (q_ref[q_rows, :] * SCALE).astype(BF16)
                kf = k_ref[k_rows, :]
                k2 = kf.astype(BF16)
                do2 = do_ref[q_rows, :].astype(BF16)
                qb_scr[u] = q2
                kb_scr[u, 0:win, :] = (kf * SCALE).astype(BF16)
                dob_scr[u] = do2
                l2 = l_ref[q_rows, :]
                d2 = dd_ref[q_rows, :]
                l2t = l2.T
                d2t = d2.T
                v2 = v_ref[k_rows, :].astype(BF16)
                qs, dos = _per_head(q2), _per_head(do2)
                for h in range(2):
                    c0 = h * HEAD_DIM
                    sc = lax.dot_general(qs[h], k2, NT, preferred_element_type=F32)
                    s_scr[2 * u + h, :, 0:win] = sc + bias - l2[:, c0:c0 + 1]
                    dp_scr[2 * u + h, :, 0:win] = lax.dot_general(dos[h], v2, NT, preferred_element_type=F32) \
                        - d2[:, c0:c0 + 1]
                    sct = lax.dot_general(k2, qs[h], NT, preferred_element_type=F32)
                    st_scr[2 * u + h, 0:win, :] = sct + bias_t - l2t[c0:c0 + 1, :]
                    dpt_scr[2 * u + h, 0:win, :] = lax.dot_general(v2, dos[h], NT, preferred_element_type=F32) \
                        - d2t[c0:c0 + 1, :]

            _attn_units(r, nb, scores)

            def pointwise(hu, carry, win=win):
                ds_scr[hu, :, 0:win] = (jnp.exp(s_scr[hu, :, 0:win]) * dp_scr[hu, :, 0:win]).astype(BF16)
                pt = jnp.exp(st_scr[hu, 0:win, :])
                pt_scr[hu, 0:win, :] = pt.astype(BF16)
                dst_scr[hu, 0:win, :] = (pt * dpt_scr[hu, 0:win, :]).astype(BF16)
                return carry

            lax.fori_loop(0, 2 * N_UNITS, pointwise, 0, unroll=4)

            def grads(rho, n, r=r, nb=nb, win=win):
                u = rho * nb + n
                q_rows, k_rows, _ = _unit_rows(r, rho, n, nb)
                qs, ks, dos = _per_head(qb_scr[u]), _per_head(kb_scr[u, 0:win, :]), _per_head(dob_scr[u])

                def both(scr, rows, rhs):
                    return (jnp.dot(scr[(2 * u,) + rows], rhs[0], preferred_element_type=F32)
                            + jnp.dot(scr[(2 * u + 1,) + rows], rhs[1], preferred_element_type=F32))

                dq_s[q_rows, :] += both(ds_scr, (slice(None), slice(0, win)), ks)
                dk_s[k_rows, :] += both(dst_scr, (slice(0, win), slice(None)), qs)
                dv_s[k_rows, :] += both(pt_scr, (slice(0, win), slice(None)), dos)

            _attn_units(r, nb, grads)
        out_ref[0] = dq_s[...].astype(BF16)
        out_ref[1] = dk_s[...].astype(BF16)
        out_ref[2] = dv_s[...].astype(BF16)

    blk = lambda first: pl.BlockSpec((SEQ, 128), lambda g: (0, first + g))
    acc = pltpu.VMEM((SEQ, 128), F32)
    big = (2 * N_UNITS, ATTN_BLOCK, 2 * ATTN_BLOCK)
    big_t = (2 * N_UNITS, 2 * ATTN_BLOCK, ATTN_BLOCK)
    return pl.pallas_call(
        body, out_shape=jax.ShapeDtypeStruct((3, SEQ, D_ATTN), BF16), grid=(4,),
        in_specs=[blk(8), blk(12), blk(16), blk(0), blk(0), blk(0)],
        out_specs=pl.BlockSpec((3, SEQ, 128), lambda g: (0, 0, g)),
        scratch_shapes=[acc, acc, acc, pltpu.VMEM(big, F32), pltpu.VMEM(big, F32), pltpu.VMEM(big, BF16),
                        pltpu.VMEM(big_t, F32), pltpu.VMEM(big_t, F32), pltpu.VMEM(big_t, BF16), pltpu.VMEM(big_t, BF16),
                        pltpu.VMEM((N_UNITS, ATTN_BLOCK, 128), BF16),
                        pltpu.VMEM((N_UNITS, 2 * ATTN_BLOCK, 128), BF16), pltpu.VMEM((N_UNITS, ATTN_BLOCK, 128), BF16),
                        pltpu.VMEM((2, ATTN_BLOCK, 2 * ATTN_BLOCK), F32), pltpu.VMEM((2, 2 * ATTN_BLOCK, ATTN_BLOCK), F32)],
        name=name, compiler_params=_cp("parallel"))(proj, proj, proj, do, lse, dd)


N_FT = D_FF // FFN_TN


def _ffn_specs():
    per = ROWS // FFN_HALO
    cur_g = pl.BlockSpec((ROWS, FFN_TN), lambda j, i: (i, j))
    cur_v = pl.BlockSpec((ROWS, FFN_TN), lambda j, i: (i, j + N_FT))
    halo_g = pl.BlockSpec((FFN_HALO, FFN_TN), lambda j, i: (jnp.maximum(i * per - 1, 0), j))
    halo_v = pl.BlockSpec((FFN_HALO, FFN_TN), lambda j, i: (jnp.maximum(i * per - 1, 0), j + N_FT))
    w_g = pl.BlockSpec((FFN_K, FFN_TN), lambda j, i: (0, j))
    w_v = pl.BlockSpec((FFN_K, FFN_TN), lambda j, i: (0, j + N_FT))
    b_g = pl.BlockSpec((1, FFN_TN), lambda j, i: (0, j))
    b_v = pl.BlockSpec((1, FFN_TN), lambda j, i: (0, j + N_FT))
    return [cur_g, cur_v, halo_g, halo_v, w_g, w_v, b_g, b_v]


def matmul(a, b, kind, out_dtype, tm, tn, name, b_rows=None):
    stacked = b_rows is not None
    b_shape = (N_DEV * b_rows, b.shape[2]) if stacked else b.shape
    if kind == "nn":
        (m, k), n = a.shape, b_shape[1]
        a_spec = pl.BlockSpec((tm, k), lambda j, i: (i, 0))
        b_spec = pl.BlockSpec((k, tn), lambda j, i: (0, j))
        dims = (((1,), (0,)), ((), ()))
    elif kind == "nt":
        (m, k), n = a.shape, b_shape[0]
        a_spec = pl.BlockSpec((tm, k), lambda j, i: (i, 0))
        b_spec = pl.BlockSpec((tn, k), lambda j, i: (j, 0))
        dims = (((1,), (1,)), ((), ()))
    else:
        (k, m), n = a.shape, b_shape[1]
        a_spec = pl.BlockSpec((k, tm), lambda j, i: (0, i))
        b_spec = pl.BlockSpec((k, tn), lambda j, i: (0, j))
        dims = (((0,), (0,)), ((), ()))
    assert m % tm == 0 and n % tn == 0, (name, m, n, tm, tn)
    if stacked:
        assert b_spec.block_shape[0] == b_shape[0] and kind in ("nn", "nt")
        width = b_spec.block_shape[1]
        b_spec = pl.BlockSpec((N_DEV, b_rows, width), (lambda j, i: (0, 0, j)) if kind == "nn" else (lambda j, i: (0, 0, 0)))

    def body(a_ref, b_ref, o_ref):
        bb = b_ref[...].reshape(b_shape[0], -1) if stacked else b_ref[...]
        o_ref[...] = lax.dot_general(a_ref[...], bb, dims, preferred_element_type=F32).astype(o_ref.dtype)

    return pl.pallas_call(
        body, out_shape=jax.ShapeDtypeStruct((m, n), out_dtype), grid=(n // tn, m // tm),
        in_specs=[a_spec, b_spec], out_specs=pl.BlockSpec((tm, tn), lambda j, i: (i, j)),
        name=name, compiler_params=_cp("parallel", "parallel"))(a, b)


def matmul_tn_pieces(a1, a3, b, name):
    k, n = b.shape
    tm = a3.shape[2]
    n1 = a1.shape[1] // tm

    def body(a1_ref, a3_ref, b_ref, o_ref):
        i = pl.program_id(0)
        tn_dims = (((0,), (0,)), ((), ()))

        @pl.when(i < n1)
        def _():
            o_ref[...] = lax.dot_general(a1_ref[...], b_ref[...], tn_dims, preferred_element_type=F32).astype(BF16)

        @pl.when(i >= n1)
        def _():
            o_ref[...] = lax.dot_general(a3_ref[0], b_ref[...], tn_dims, preferred_element_type=F32).astype(BF16)

    return pl.pallas_call(
        body, out_shape=jax.ShapeDtypeStruct((a1.shape[1] + 3 * tm, n), BF16), grid=(n1 + 3,),
        in_specs=[pl.BlockSpec((k, tm), lambda i: (0, jnp.minimum(i, n1 - 1))),
                  pl.BlockSpec((1, k, tm), lambda i: (jnp.maximum(i - n1, 0), 0, 0)),
                  pl.BlockSpec((k, n), lambda i: (0, 0))],
        out_specs=pl.BlockSpec((tm, n), lambda i: (i, 0)), name=name, compiler_params=_cp("parallel"))(a1, a3, b)


def matmul_tn_halves(a, b, tm, name):
    _, k, m = a.shape
    n = b.shape[1]

    def body(a_ref, b_ref, o_ref):
        o_ref[0] = lax.dot_general(a_ref[0], b_ref[...], (((0,), (0,)), ((), ())), preferred_element_type=F32).astype(BF16)

    return pl.pallas_call(
        body, out_shape=jax.ShapeDtypeStruct((2, m, n), BF16), grid=(2, m // tm),
        in_specs=[pl.BlockSpec((1, k, tm), lambda h, i: (h, 0, i)), pl.BlockSpec((k, n), lambda h, i: (0, 0))],
        out_specs=pl.BlockSpec((1, tm, n), lambda h, i: (h, i, 0)), name=name,
        compiler_params=_cp("parallel", "parallel"))(a, b).reshape(2 * m, n)


def _row_spec(width, col=0):
    return pl.BlockSpec((ROWS, width), lambda i: (i, col))


def _vec_spec(width, rows=1):
    return pl.BlockSpec((rows, width), lambda i: (0, 0))


FFN_RB = 64


def _ffn_lane_blocks(fn):
    for c in range(FFN_TN // 128):
        fn(slice(c * 128, (c + 1) * 128))


def _ffn_taps(cur_ref, halo_ref, head, ls, r0):
    if r0 == 0:
        head[0:FFN_HALO, :] = jnp.where(pl.program_id(1) > 0, halo_ref[:, ls], 0.0)
        head[FFN_HALO:, :] = cur_ref[0:FFN_RB, ls]
        return tuple(head[pl.ds(FFN_HALO - k, FFN_RB), :] for k in (2, 1, 0))
    return tuple(cur_ref[pl.ds(r0 - k, FFN_RB), ls] for k in (2, 1, 0))


def _ffn_conv(taps, w, b):
    return b + w[0] * taps[0] + w[1] * taps[1] + w[2] * taps[2]


def ffn_act_fwd(up0, wf, bf, name, after=None):
    def body(g_ref, v_ref, gh_ref, vh_ref, wg_ref, wv_ref, bg_ref, bv_ref, act_ref, head_g, head_v):
        def lanes(ls):
            wg = [wg_ref[t:t + 1, ls] for t in range(FFN_K)]
            wv = [wv_ref[t:t + 1, ls] for t in range(FFN_K)]
            for r0 in range(0, ROWS, FFN_RB):
                gate = _ffn_conv(_ffn_taps(g_ref, gh_ref, head_g, ls, r0), wg, bg_ref[:, ls])
                val = _ffn_conv(_ffn_taps(v_ref, vh_ref, head_v, ls, r0), wv, bv_ref[:, ls])
                act_ref[r0:r0 + FFN_RB, ls] = (gate * _sig(gate) * val).astype(BF16)

        _ffn_lane_blocks(lanes)

    head = pltpu.VMEM((FFN_HALO + FFN_RB, 128), F32)
    body, more_specs, more = _with_after(body, 8, after)
    return pl.pallas_call(
        body, out_shape=jax.ShapeDtypeStruct((SEQ, D_FF), BF16), grid=(N_FT, SEQ // ROWS),
        in_specs=_ffn_specs() + more_specs, out_specs=pl.BlockSpec((ROWS, FFN_TN), lambda j, i: (i, j)),
        scratch_shapes=[head, head], name=name,
        compiler_params=_cp("parallel", "parallel"))(up0, up0, up0, up0, wf, wf, bf, bf, *more)


def ffn_bwd(up0, dact, wf, bf, name, after=None):
    per = ROWS // FFN_HALO
    last = SEQ // FFN_HALO - 1

    def body(g_ref, v_ref, gh_ref, vh_ref, wg_ref, wv_ref, bg_ref, bv_ref, da_ref, gn_ref, vn_ref, dan_ref,
             out_ref, dbg_ref, dbv_ref, dwg_ref, dwv_ref, pad, head_g, head_v, dgp, dvp, acc):
        i = pl.program_id(1)
        first = i == 0
        acc[...] = jnp.zeros_like(acc)

        def grads(gate, val, da):
            s = _sig(gate)
            return da * val * (s * (1.0 + gate * (1.0 - s))), da * (gate * s)

        fold = lambda q: jnp.sum(q.reshape(FFN_RB // 8, 8, 128), axis=0)

        def lanes(ls):
            wg = [wg_ref[t:t + 1, ls] for t in range(FFN_K)]
            wv = [wv_ref[t:t + 1, ls] for t in range(FFN_K)]
            sums = [jnp.zeros((8, 128), F32)] * (2 + 2 * FFN_K)
            for r0 in range(0, ROWS, FFN_RB):
                gs = _ffn_taps(g_ref, gh_ref, head_g, ls, r0)
                vs = _ffn_taps(v_ref, vh_ref, head_v, ls, r0)
                dgate, dval = grads(_ffn_conv(gs, wg, bg_ref[:, ls]), _ffn_conv(vs, wv, bv_ref[:, ls]),
                                    da_ref[r0:r0 + FFN_RB, ls])
                dgp[r0:r0 + FFN_RB, ls] = dgate
                dvp[r0:r0 + FFN_RB, ls] = dval
                new = [dgate, dval] + [dgate * gs[t] for t in range(FFN_K)] + [dval * vs[t] for t in range(FFN_K)]
                sums = [a + fold(q) for a, q in zip(sums, new)]
            for k in range(2 + 2 * FFN_K):
                acc[k, 0:8, ls] = sums[k]

        _ffn_lane_blocks(lanes)
        _acc(dbg_ref, _rsum(acc[0]), first)
        _acc(dbv_ref, _rsum(acc[1]), first)
        _acc(dwg_ref, jnp.concatenate([_rsum(acc[2 + t]) for t in range(FFN_K)], axis=0), first)
        _acc(dwv_ref, jnp.concatenate([_rsum(acc[5 + t]) for t in range(FFN_K)], axis=0), first)

        def conv_next(cur_ref, nxt_ref, w_ref, b_ref):
            pad[0:FFN_HALO, :] = cur_ref[ROWS - FFN_HALO:, :]
            pad[FFN_HALO:, :] = nxt_ref[...]
            return (b_ref[...] + w_ref[0:1, :] * pad[pl.ds(FFN_HALO - 2, FFN_HALO), :]
                    + w_ref[1:2, :] * pad[pl.ds(FFN_HALO - 1, FFN_HALO), :] + w_ref[2:3, :] * nxt_ref[...])

        gate_n = conv_next(g_ref, gn_ref, wg_ref, bg_ref)
        val_n = conv_next(v_ref, vn_ref, wv_ref, bv_ref)
        dgate_n, dval_n = grads(gate_n, val_n, dan_ref[...])
        inside = i < SEQ // ROWS - 1
        dgp[ROWS:, :] = jnp.where(inside, dgate_n, 0.0)
        dvp[ROWS:, :] = jnp.where(inside, dval_n, 0.0)

        def back(ls):
            for half, (dp, w_ref) in enumerate(((dgp, wg_ref), (dvp, wv_ref))):
                w = [w_ref[t:t + 1, ls] for t in range(FFN_K)]
                for r0 in range(0, ROWS, FFN_RB):
                    out_ref[half, r0:r0 + FFN_RB, ls] = (
                        w[2] * dp[r0:r0 + FFN_RB, ls] + w[1] * dp[pl.ds(r0 + 1, FFN_RB), ls]
                        + w[0] * dp[pl.ds(r0 + 2, FFN_RB), ls]).astype(BF16)

        _ffn_lane_blocks(back)

    body, more_specs, more = _with_after(body, 12, after)
    head = pltpu.VMEM((FFN_HALO + FFN_RB, 128), F32)
    ext = pltpu.VMEM((ROWS + FFN_HALO, FFN_TN), F32)
    vec = jax.ShapeDtypeStruct((1, D_FF), F32)
    taps = jax.ShapeDtypeStruct((FFN_K, D_FF), F32)
    cur = pl.BlockSpec((ROWS, FFN_TN), lambda j, i: (i, j))
    nxt = lambda off: pl.BlockSpec((FFN_HALO, FFN_TN), lambda j, i: (jnp.minimum((i + 1) * per, last), j + off))
    vs = pl.BlockSpec((1, FFN_TN), lambda j, i: (0, j))
    ts = pl.BlockSpec((FFN_K, FFN_TN), lambda j, i: (0, j))
    return pl.pallas_call(
        body, out_shape=(jax.ShapeDtypeStruct((2, SEQ, D_FF), BF16), vec, vec, taps, taps), grid=(N_FT, SEQ // ROWS),
        in_specs=_ffn_specs() + [cur, nxt(0), nxt(N_FT), nxt(0)] + more_specs,
        out_specs=(pl.BlockSpec((2, ROWS, FFN_TN), lambda j, i: (0, i, j)), vs, vs, ts, ts),
        scratch_shapes=[pltpu.VMEM((2 * FFN_HALO, FFN_TN), F32), head, head, ext, ext,
                        pltpu.VMEM((2 + 2 * FFN_K, SUB, FFN_TN), F32)],
        name=name, compiler_params=_cp("parallel", "arbitrary"))(up0, up0, up0, up0, wf, wf, bf, bf, dact, up0, up0, dact,
                                                                 *more)


def ada_fwd(c_all, w_ada, b_cols, name):
    def body(c_ref, w_ref, b_ref, o_ref):
        cc = c_ref[...]
        sc = (cc * _sig(cc)).astype(BF16)
        o_ref[...] = jnp.dot(sc, w_ref[...].astype(BF16), preferred_element_type=F32) + b_ref[...]

    return pl.pallas_call(body, out_shape=jax.ShapeDtypeStruct((N_DEV, w_ada.shape[1]), F32), name=name,
                          compiler_params=_cp())(c_all, w_ada, b_cols)


def _adam(w, g, m, v):
    m = ADAM_B1 * m + (1.0 - ADAM_B1) * g
    v = ADAM_B2 * v + (1.0 - ADAM_B2) * (g * g)
    m_hat = m / (1.0 - ADAM_B1 ** ADAM_STEP)
    v_hat = v / (1.0 - ADAM_B2 ** ADAM_STEP)
    delta = -ADAM_LR * (m_hat / (jnp.sqrt(v_hat) + ADAM_EPS) + ADAM_WD * w)
    return delta, m, v


def ada_bwd_adamw(c_all, dmod_cols, w, m, v, name):
    rows, cols = w.shape
    tr = 256

    def body(c_ref, dm_ref, w_ref, m_ref, v_ref, g_ref, d_ref, nm_ref, nv_ref):
        cc = c_ref[...]
        sc = (cc * _sig(cc)).T
        g = sc[:, 0:1] * dm_ref[0:1, :]
        for b in range(1, N_DEV):
            g = g + sc[:, b:b + 1] * dm_ref[b:b + 1, :]
        g_ref[...] = g
        d_ref[...], nm_ref[...], nv_ref[...] = _adam(w_ref[...], g, m_ref[...], v_ref[...])

    blk = pl.BlockSpec((tr, cols), lambda i: (i, 0))
    shp = jax.ShapeDtypeStruct((rows, cols), F32)
    return pl.pallas_call(
        body, out_shape=(shp, shp, shp, shp), grid=(rows // tr,),
        in_specs=[pl.BlockSpec((N_DEV, tr), lambda i: (0, i)), pl.BlockSpec((N_DEV, cols), lambda i: (0, 0)), blk, blk, blk],
        out_specs=(blk, blk, blk, blk), name=name, compiler_params=_cp("parallel"))(c_all, dmod_cols, w, m, v)


def sum_adamw(parts, mine, me, w, m, v, tr, name):
    n_parts, rows, cols = parts.shape

    def body(me_ref, p_ref, own_ref, w_ref, m_ref, v_ref, g_ref, d_ref, nm_ref, nv_ref):
        def chunk(rs):
            g = own_ref[0, rs, :].astype(F32)
            for k in range(1, n_parts):
                g = g + p_ref[k, rs, :].astype(F32)
            g_ref[rs, :] = g
            d_ref[rs, :], nm_ref[rs, :], nv_ref[rs, :] = _adam(w_ref[rs, :], g, m_ref[rs, :], v_ref[rs, :])

        _for_chunks(chunk, 2, tr)

    blk = pl.BlockSpec((tr, cols), lambda i, me_ref: (i, 0))
    shp = jax.ShapeDtypeStruct((rows, cols), F32)
    grid_spec = pltpu.PrefetchScalarGridSpec(
        num_scalar_prefetch=1, grid=(rows // tr,),
        in_specs=[pl.BlockSpec((n_parts, tr, cols), lambda i, me_ref: (0, i, 0)),
                  pl.BlockSpec((1, tr, cols), lambda i, me_ref: (me_ref[0], i, 0)), blk, blk, blk],
        out_specs=(blk, blk, blk, blk))
    return pl.pallas_call(body, out_shape=(shp, shp, shp, shp), grid_spec=grid_spec, name=name,
                          compiler_params=_cp("parallel"))(me, parts, mine, w, m, v)


MESH = pl.DeviceIdType.MESH


def all_gather(block, name, after=None):
    extra = () if after is None else (after,)

    def body(x_ref, *refs):
        out_ref, send_sems, recv_sems, local_sem = refs[len(extra):]
        x, y, c = lax.axis_index("x"), lax.axis_index("y"), lax.axis_index("c")
        me, sibling = (x, y, c), (x, y, 1 - c)
        chips = [(1 - x, y), (x, 1 - y), (1 - x, 1 - y)]

        def slot(px, py, pc):
            return out_ref.at[4 * px + 2 * py + pc]

        def copy(k, blk, to, src=None):
            return pltpu.make_async_remote_copy(
                src_ref=slot(*blk) if src is None else src, dst_ref=slot(*blk),
                send_sem=send_sems.at[k], recv_sem=recv_sems.at[k], device_id=to, device_id_type=MESH)

        mine = pltpu.make_async_copy(x_ref, slot(*me), local_sem)
        mine.start()
        first = [copy(0, me, sibling, src=x_ref)]
        first += [copy(1 + j, me, (*chip, c), src=x_ref) for j, chip in enumerate(chips)]
        for cp in first:
            cp.start()
        passed = [copy(4 + j, (*chip, c), sibling) for j, chip in enumerate(chips)]
        for j, chip in enumerate(chips):
            copy(1 + j, (*chip, c), me).wait_recv()
            passed[j].start()
        copy(0, sibling, me).wait_recv()
        for j, chip in enumerate(chips):
            copy(4 + j, (*chip, 1 - c), me).wait_recv()
        for cp in first + passed:
            cp.wait_send()
        mine.wait()

    return pl.pallas_call(
        body, out_shape=jax.ShapeDtypeStruct((N_DEV,) + block.shape, block.dtype), in_specs=[ANY] * (1 + len(extra)), out_specs=ANY,
        scratch_shapes=[pltpu.SemaphoreType.DMA((7,)), pltpu.SemaphoreType.DMA((7,)), pltpu.SemaphoreType.DMA],
        name=name)(block, *extra)


HBM = pl.BlockSpec(memory_space=pltpu.HBM)
SEM = pl.BlockSpec(memory_space=pltpu.SEMAPHORE)
EFFECT = pltpu.SideEffectType.DATAFLOW_SIDE_EFFECTING


def _peer_copies(src_ref, land_ref, send_sems, recv_sems, gather):
    x, y, c = lax.axis_index("x"), lax.axis_index("y"), lax.axis_index("c")
    me = 4 * x + 2 * y + c
    copies = []
    for k in range(1, N_DEV):
        px = 1 - x if k & 4 else x
        py = 1 - y if k & 2 else y
        pc = 1 - c if k & 1 else c
        copies.append(pltpu.make_async_remote_copy(
            src_ref=src_ref if gather else src_ref.at[4 * px + 2 * py + pc],
            dst_ref=land_ref.at[me] if gather else land_ref.at[k],
            send_sem=send_sems.at[k - 1], recv_sem=recv_sems.at[k - 1], device_id=(px, py, pc), device_id_type=MESH))
    return copies


def exchange_start(srcs, gather, name, after=None):
    n = len(srcs)
    land_shapes = [(N_DEV,) + src.shape if gather else src.shape for src in srcs]
    extra = () if after is None else (after,)

    def body(*refs):
        src_refs, land_refs = refs[0:n], refs[n:2 * n]
        outs = refs[2 * n + len(extra):]
        for k in range(n):
            for cp in _peer_copies(src_refs[k], land_refs[k], outs[4 * k], outs[4 * k + 1], gather):
                cp.start()
        token = outs[4 * n]
        token[...] = jnp.zeros_like(token)

    out_shape, out_specs, aliases = [], [], {}
    for k, src in enumerate(srcs):
        out_shape += [pltpu.SemaphoreType.DMA((N_DEV - 1,)), pltpu.SemaphoreType.DMA((N_DEV - 1,)),
                      pltpu.HBM(src.shape, src.dtype), pltpu.HBM(land_shapes[k], src.dtype)]
        out_specs += [SEM, SEM, HBM, HBM]
        aliases[k] = 4 * k + 2
        aliases[n + k] = 4 * k + 3
    out_shape.append(jax.ShapeDtypeStruct((8, 128), F32))
    out_specs.append(pl.BlockSpec(memory_space=pltpu.VMEM))
    res = pl.pallas_call(
        body, name=name, out_shape=tuple(out_shape), in_specs=(HBM,) * (2 * n) + (ANY,) * len(extra),
        out_specs=tuple(out_specs), input_output_aliases=aliases,
        compiler_params=pltpu.CompilerParams(has_side_effects=EFFECT),
    )(*[pltpu.with_memory_space_constraint(src, pltpu.HBM) for src in srcs],
      *[pltpu.with_memory_space_constraint(lax.empty(shp, src.dtype), pltpu.HBM) for shp, src in zip(land_shapes, srcs)],
      *extra)
    return [tuple(res[4 * k:4 * k + 4]) for k in range(n)], res[4 * n]


def _stage1_peer(i):
    x, y, c = lax.axis_index("x"), lax.axis_index("y"), lax.axis_index("c")
    if i == 0:
        return (x, y, 1 - c)
    return (1 - x if i & 1 else x, 1 - y if i & 2 else y, c)


def _slot_of(peer):
    return 4 * peer[0] + 2 * peer[1] + peer[2]


def _stage1_copy(i, src_ref, land_ref, send_sems, recv_sems):
    me = _slot_of((lax.axis_index("x"), lax.axis_index("y"), lax.axis_index("c")))
    return pltpu.make_async_remote_copy(src_ref=src_ref, dst_ref=land_ref.at[me], send_sem=send_sems.at[i],
                                        recv_sem=recv_sems.at[i], device_id=_stage1_peer(i), device_id_type=MESH)


def _stage2_copy(j, land_ref, send_sems, recv_sems):
    slot = _slot_of(_stage1_peer(j + 1))
    return pltpu.make_async_remote_copy(src_ref=land_ref.at[slot], dst_ref=land_ref.at[slot], send_sem=send_sems.at[j],
                                        recv_sem=recv_sems.at[j], device_id=_stage1_peer(0), device_id_type=MESH)


def gather2_start(srcs, name, after=None):
    n = len(srcs)
    extra = () if after is None else (after,)

    def body(*refs):
        src_refs, land_refs = refs[0:n], refs[n:2 * n]
        outs = refs[2 * n + len(extra):]
        for k in range(n):
            for i in range(4):
                _stage1_copy(i, src_refs[k], land_refs[k], outs[4 * k], outs[4 * k + 1]).start()
        outs[4 * n][...] = jnp.zeros((8, 128), F32)

    out_shape, out_specs, aliases = [], [], {}
    for k, src in enumerate(srcs):
        out_shape += [pltpu.SemaphoreType.DMA((4,)), pltpu.SemaphoreType.DMA((4,)),
                      pltpu.HBM(src.shape, src.dtype), pltpu.HBM((N_DEV,) + src.shape, src.dtype)]
        out_specs += [SEM, SEM, HBM, HBM]
        aliases[k] = 4 * k + 2
        aliases[n + k] = 4 * k + 3
    out_shape.append(jax.ShapeDtypeStruct((8, 128), F32))
    out_specs.append(pl.BlockSpec(memory_space=pltpu.VMEM))
    res = pl.pallas_call(
        body, name=name, out_shape=tuple(out_shape), in_specs=(HBM,) * (2 * n) + (ANY,) * len(extra),
        out_specs=tuple(out_specs), input_output_aliases=aliases,
        compiler_params=pltpu.CompilerParams(has_side_effects=EFFECT),
    )(*[pltpu.with_memory_space_constraint(src, pltpu.HBM) for src in srcs],
      *[pltpu.with_memory_space_constraint(lax.empty((N_DEV,) + src.shape, src.dtype), pltpu.HBM) for src in srcs], *extra)
    return [dict(send1=res[4 * k], recv1=res[4 * k + 1], src=res[4 * k + 2], land=res[4 * k + 3]) for k in range(n)], \
        res[4 * n]


def gather2_pass(handles, after, name):
    n = len(handles)

    def body(*refs):
        src_refs, land_refs, recv1 = refs[0:n], refs[n:2 * n], refs[2 * n:3 * n]
        outs = refs[3 * n + 1:]
        for k in range(n):
            for j in range(3):
                _stage1_copy(j + 1, src_refs[k], land_refs[k], recv1[k], recv1[k]).wait_recv()
                _stage2_copy(j, land_refs[k], outs[3 * k], outs[3 * k + 1]).start()
        outs[3 * n][...] = jnp.zeros((8, 128), F32)

    out_shape, out_specs, aliases = [], [], {}
    for k, h in enumerate(handles):
        out_shape += [pltpu.SemaphoreType.DMA((3,)), pltpu.SemaphoreType.DMA((3,)), pltpu.HBM(h["land"].shape, h["land"].dtype)]
        out_specs += [SEM, SEM, HBM]
        aliases[n + k] = 3 * k + 2
    out_shape.append(jax.ShapeDtypeStruct((8, 128), F32))
    out_specs.append(pl.BlockSpec(memory_space=pltpu.VMEM))
    res = pl.pallas_call(
        body, name=name, out_shape=tuple(out_shape), in_specs=(HBM,) * (2 * n) + (SEM,) * n + (ANY,),
        out_specs=tuple(out_specs), input_output_aliases=aliases,
        compiler_params=pltpu.CompilerParams(has_side_effects=EFFECT),
    )(*[h["src"] for h in handles], *[h["land"] for h in handles], *[h["recv1"] for h in handles], after)
    return [dict(h, send2=res[3 * k], recv2=res[3 * k + 1], land=res[3 * k + 2]) for k, h in enumerate(handles)], res[3 * n]


def gather2_wait(h, after, name):
    def body(src_ref, land_ref, send1, recv1, send2, recv2, after_ref, src_dead, got_ref):
        for i in range(4):
            _stage1_copy(i, src_ref, land_ref, send1, recv1).wait_send()
        _stage1_copy(0, src_ref, land_ref, send1, recv1).wait_recv()
        for j in range(3):
            cp = _stage2_copy(j, land_ref, send2, recv2)
            cp.wait_send()
            cp.wait_recv()

    return pl.pallas_call(
        body, name=name,
        out_shape=(pltpu.HBM(h["src"].shape, h["src"].dtype), pltpu.HBM(h["land"].shape, h["land"].dtype)),
        in_specs=(HBM, HBM, SEM, SEM, SEM, SEM, ANY), out_specs=(HBM, HBM), input_output_aliases={0: 0, 1: 1},
        compiler_params=pltpu.CompilerParams(has_side_effects=EFFECT),
    )(h["src"], h["land"], h["send1"], h["recv1"], h["send2"], h["recv2"], after)[1]


def exchange_wait(handles, after, gather, name):
    send_sems, recv_sems, src_thru, land_thru = handles

    def body(src_ref, land_ref, send_sems, recv_sems, after_ref, src_dead, got_ref):
        for cp in _peer_copies(src_ref, land_ref, send_sems, recv_sems, gather):
            cp.wait_send()
            cp.wait_recv()

    return pl.pallas_call(
        body, name=name,
        out_shape=(pltpu.HBM(src_thru.shape, src_thru.dtype), pltpu.HBM(land_thru.shape, land_thru.dtype)),
        in_specs=(HBM, HBM, SEM, SEM, ANY), out_specs=(HBM, HBM), input_output_aliases={0: 0, 1: 1},
        compiler_params=pltpu.CompilerParams(has_side_effects=EFFECT),
    )(src_thru, land_thru, send_sems, recv_sems, after)


def local_step(x, tgt, mod, started, get_w, put_grad, wc, wf, g_mix, bc, lg, lb, gco, gao, g_ffn, bf, g_fin):
    w_in = get_w("w_in", mod)
    proj, h1 = rms_mod_matmul(x, g_mix, mod, 0, 1, w_in, D_IN // N_DEV, "proj_fwd", after=started)
    mix_a, u1 = conv_module_fwd(proj, wc, bc, lg, lb, gco, "conv_module_fwd")
    att, lse = attn_fwd_all(proj, "attn_fwd")
    w_out = get_w("w_out", att)
    y1, mixed = norm_concat_matmul(mix_a, att, gao, w_out, "out_proj_fwd")
    w_up = get_w("w_up", y1)
    up0, x1, h2 = resid_rms_mod_matmul(x, y1, g_ffn, mod, 2, 3, 4, w_up, "up_fwd")
    passed = get_w("w_down", up0, only_pass_on=True)
    act = ffn_act_fwd(up0, wf, bf, "ffn_act_fwd", after=passed)
    w_down = get_w("w_down", act)
    loss_t, dx2, dy2, d_gfin, d_gaf = matmul_loss_bwd(act, w_down, x1, tgt, g_fin, mod, 5, "down_fwd_loss")
    dact = matmul(dy2, w_down, "nt", F32, 512, FFN_TN, "down_bwd_x")
    dw_down = matmul(act, dy2, "tn", BF16, 256, D_MODEL, "down_bwd_w")
    dup0, dbf_g, dbf_v, dwf_g, dwf_v = ffn_bwd(up0, dact, wf, bf, "ffn_bwd", after=put_grad("w_down", dw_down))
    dw_up = matmul_tn_halves(dup0, h2, 256, "up_bwd_w")
    dx1, d_shf, d_scf, d_gffn, dy1, d_gam = matmul_rms_mod_bwd(
        dup0, w_up, x1, dx2, g_ffn, mod, 4, y1, 2, "up_bwd_x", after=put_grad("w_up", dw_up))
    dw_out = matmul(mixed, dy1, "tn", BF16, 256, D_MODEL, "out_proj_bwd_w")
    dmixed, do, dd, d_gao = matmul_combine_bwd(dy1, w_out, att, gao, "out_proj_bwd_x", after=put_grad("w_out", dw_out))
    dqkv = attn_bwd_all(proj, do, lse, dd, "attn_bwd")
    du1, d_gco, d_lg, d_lb, d_bc, d_wc = conv_module_bwd_a(proj, u1, dmixed, lg, lb, gco, "conv_module_bwd_a")
    dproj_a = conv_module_bwd_b(proj, du1, wc, "conv_module_bwd_b")
    dw_in = matmul_tn_pieces(dproj_a, dqkv, h1, "proj_bwd_w")
    dx, d_shm, d_scm, d_gmix = matmul_rms_mod_bwd(
        (dproj_a, dqkv), w_in, x, dx1, g_mix, mod, 1, None, 0, "proj_bwd_x", b_rows=D_IN // N_DEV,
        after=put_grad("w_in", dw_in))
    dmod = jnp.concatenate([d_shm, d_scm, d_gam, d_shf, d_scf, d_gaf], axis=1)
    small = dict(g_norm_mix=d_gmix, b_conv_dw=d_bc, ln_conv_g=d_lg, ln_conv_b=d_lb, g_conv_out=d_gco, g_attn_out=d_gao,
                 g_norm_ffn=d_gffn, b_ffn_dw=jnp.concatenate([dbf_g, dbf_v], axis=1), g_final=d_gfin,
                 w_conv_dw=d_wc, w_ffn_dw=jnp.concatenate([dwf_g, dwf_v], axis=1), dmod=dmod, loss=loss_t[0:1, 0:1])
    return dx, small


PACK_W = 7168
TAPS_PER_ROW = PACK_W // D_CONV
PACKED_AT = dict(
    b_ada=(0, 0, N_MOD * D_MODEL), g_norm_mix=(0, 6144, D_MODEL),
    b_ffn_dw=(1, 0, 2 * D_FF), g_norm_ffn=(1, 5632, D_MODEL), b_conv_dw=(1, 6656, D_CONV),
    g_final=(2, 5632, D_MODEL), ln_conv_g=(2, 6656, D_CONV),
    ln_conv_b=(3, 5632, D_CONV), g_conv_out=(3, 6144, D_CONV), g_attn_out=(3, 6656, D_ATTN))
SMALL_ORDER = list(PACKED_AT)
LOSS_AT = (4, 2 * D_FF)


def pack_small(t):
    cat = lambda *parts: jnp.concatenate(parts, axis=1)
    wf = t["w_ffn_dw"]
    taps = jnp.pad(t["w_conv_dw"].reshape(1, CONV_K * D_CONV), ((0, 0), (0, 3 * PACK_W - CONV_K * D_CONV)))
    return jnp.concatenate([
        cat(t["dmod"], t["g_norm_mix"]),
        cat(t["b_ffn_dw"], t["g_norm_ffn"], t["b_conv_dw"]),
        cat(wf[0:1], t["g_final"], t["ln_conv_g"]),
        cat(wf[1:2], t["ln_conv_b"], t["g_conv_out"], t["g_attn_out"]),
        cat(wf[2:3], jnp.pad(t["loss"], ((0, 0), (0, PACK_W - 2 * D_FF - 1)))),
        taps.reshape(3, PACK_W)], axis=0)


def small_adamw(parts, wmv, name):
    def body(*refs):
        p_ref = refs[0]
        ins = refs[1:1 + 3 * len(SMALL_ORDER)]
        outs = refs[1 + 3 * len(SMALL_ORDER):]
        g = p_ref[0]
        for k in range(1, N_DEV):
            g = g + p_ref[k]
        for i, n in enumerate(SMALL_ORDER):
            row, lane, width = PACKED_AT[n]
            gp = g[row:row + 1, lane:lane + width]
            w_ref, m_ref, v_ref = ins[3 * i:3 * i + 3]
            g_ref, d_ref, nm_ref, nv_ref = outs[4 * i:4 * i + 4]
            g_ref[...] = gp
            d_ref[...], nm_ref[...], nv_ref[...] = _adam(w_ref[...], gp, m_ref[...], v_ref[...])
        wc_ref, wf_ref, loss_ref = outs[4 * len(SMALL_ORDER):]
        for j in range(CONV_K):
            row, lane = 5 + j // TAPS_PER_ROW, (j % TAPS_PER_ROW) * D_CONV
            wc_ref[j:j + 1, :] = g[row:row + 1, lane:lane + D_CONV]
        wf_ref[...] = g[2:2 + FFN_K, 0:2 * D_FF]
        loss_ref[...] = jnp.broadcast_to(g[LOSS_AT[0]:LOSS_AT[0] + 1, LOSS_AT[1]:LOSS_AT[1] + 1], (8, 128))

    args, out_shape = [parts], []
    for n in SMALL_ORDER:
        args += list(wmv[n])
        out_shape += [jax.ShapeDtypeStruct(wmv[n][0].shape, F32)] * 4
    out_shape += [jax.ShapeDtypeStruct((CONV_K, D_CONV), F32), jax.ShapeDtypeStruct((FFN_K, 2 * D_FF), F32),
                  jax.ShapeDtypeStruct((8, 128), F32)]
    res = pl.pallas_call(body, out_shape=tuple(out_shape), name=name, compiler_params=_cp())(*args)
    per = {n: tuple(res[4 * i:4 * i + 4]) for i, n in enumerate(SMALL_ORDER)}
    return per, res[-3], res[-2], res[-1][0, 0]


def shard_adamw(items, name):
    def body(*refs):
        ins, outs = refs[:4 * len(items)], refs[4 * len(items):]
        for i in range(len(items)):
            g_ref, w_ref, m_ref, v_ref = ins[4 * i:4 * i + 4]
            og_ref, d_ref, nm_ref, nv_ref = outs[4 * i:4 * i + 4]
            og_ref[...] = g_ref[...]
            d_ref[...], nm_ref[...], nv_ref[...] = _adam(w_ref[...], g_ref[...], m_ref[...], v_ref[...])

    args = [a for item in items for a in item]
    out_shape = tuple(jax.ShapeDtypeStruct(item[1].shape, F32) for item in items for _ in range(4))
    res = pl.pallas_call(body, out_shape=out_shape, name=name, compiler_params=_cp())(*args)
    return [tuple(res[4 * i:4 * i + 4]) for i in range(len(items))]


def _shard(full, n_cols, me):
    return lax.dynamic_slice(full, (0, me * n_cols), (full.shape[0], n_cols))


WEIGHTS = ["w_ada", "b_ada", "g_norm_mix", "w_in", "w_conv_dw", "b_conv_dw", "ln_conv_g", "ln_conv_b", "g_conv_out",
           "g_attn_out", "w_out", "g_norm_ffn", "w_up", "w_ffn_dw", "b_ffn_dw", "w_down", "g_final"]


def kernel(x, c, w_ada, b_ada, g_norm_mix, w_in, w_conv_dw, b_conv_dw, ln_conv_g, ln_conv_b, g_conv_out, g_attn_out, w_out, g_norm_ffn, w_up, w_ffn_dw, b_ffn_dw, w_down, g_final, loss_target, m_w_ada, m_b_ada, m_g_norm_mix, m_w_in, m_w_conv_dw, m_b_conv_dw, m_ln_conv_g, m_ln_conv_b, m_g_conv_out, m_g_attn_out, m_w_out, m_g_norm_ffn, m_w_up, m_w_ffn_dw, m_b_ffn_dw, m_w_down, m_g_final, v_w_ada, v_b_ada, v_g_norm_mix, v_w_in, v_w_conv_dw, v_b_conv_dw, v_ln_conv_g, v_ln_conv_b, v_g_conv_out, v_g_attn_out, v_w_out, v_g_norm_ffn, v_w_up, v_w_ffn_dw, v_b_ffn_dw, v_w_down, v_g_final):
    args = dict(locals())
    me = 4 * lax.axis_index("x") + 2 * lax.axis_index("y") + lax.axis_index("c")
    me1 = me.astype(jnp.int32).reshape(1)

    def flat(name, prefix=""):
        a = args[prefix + name]
        return a.reshape(a.shape[-2] if a.ndim > 1 else 1, a.shape[-1])

    def flat_t(name, prefix=""):
        return args[prefix + name][0].T

    n_in, n_up, r_out, r_down = w_in.shape[2], w_up.shape[2], w_out.shape[1], w_down.shape[1]
    n_ada, n_wc, n_wf = w_ada.shape[2], w_conv_dw.shape[2], w_ffn_dw.shape[2]
    taps_c = jnp.pad(flat("w_conv_dw").reshape(1, CONV_K * n_wc), ((0, 0), (0, 2 * D_MODEL - CONV_K * n_wc)))
    taps_f = jnp.pad(flat("w_ffn_dw").reshape(1, FFN_K * n_wf), ((0, 0), (0, 3 * D_MODEL - FFN_K * n_wf)))
    first = jnp.concatenate([c, taps_c.reshape(2, D_MODEL), taps_f.reshape(3, D_MODEL), jnp.zeros((2, D_MODEL), F32)], axis=0)
    w_in_block = flat_t("w_in").astype(BF16)
    hi = lax.reduce_precision(first, 8, 7)
    mid = lax.reduce_precision(first - hi, 8, 7)
    low = lax.reduce_precision(first - hi - mid, 8, 7)
    terms = jnp.concatenate([hi, mid, low, jnp.zeros((8, D_MODEL), F32)], axis=0).astype(BF16)
    first_block = all_gather(jnp.concatenate([w_in_block, terms], axis=0), "gather_c_taps_w_in")
    terms = first_block[:, n_in:n_in + 24, :].astype(F32)
    first_all = (terms[:, 0:8] + terms[:, 8:16]) + terms[:, 16:24]
    c_all = first_all[:, 0, :]
    wc_full = first_all[:, 1:3, :].reshape(N_DEV, 2 * D_MODEL)[:, :CONV_K * n_wc].reshape(N_DEV, CONV_K, n_wc)
    wc_full = wc_full.transpose(1, 0, 2).reshape(CONV_K, D_CONV)
    wf_full = first_all[:, 3:6, :].reshape(N_DEV, 3 * D_MODEL)[:, :FFN_K * n_wf].reshape(N_DEV, FFN_K, n_wf)
    wf_full = wf_full.transpose(1, 0, 2).reshape(FFN_K, 2 * D_FF)
    mod_cols = ada_fwd(c_all, flat("w_ada"), _shard(flat("b_ada"), n_ada, me), "ada_fwd")
    mod_all = all_gather(mod_cols, "gather_mod")
    mod = lax.dynamic_index_in_dim(mod_all, me, axis=1, keepdims=False).reshape(N_MOD, D_MODEL)
    mod = jnp.pad(mod, ((0, 2), (0, 0)))

    order = ("w_out", "w_up", "w_down")
    blocks = dict(w_up=flat_t("w_up").astype(BF16), w_out=flat("w_out").astype(BF16), w_down=flat("w_down").astype(BF16))
    handles, tok = gather2_start([blocks[name] for name in order], "gather_weights_start", mod_all)
    gathers = dict(zip(order, handles))

    def pass_on(name, after):
        if "send2" not in gathers[name]:
            group = ("w_out", "w_up") if name != "w_down" else ("w_down",)
            passed, token = gather2_pass([gathers[w] for w in group], after, f"gather_{name}_pass")
            gathers.update(zip(group, passed))
            return token

    def gathered(name, after):
        pass_on(name, after)
        land = gather2_wait(gathers[name], after, f"gather_{name}_wait")
        return lax.dynamic_update_index_in_dim(land, blocks[name], me, axis=0)

    def get_w(name, after, only_pass_on=False):
        if name == "w_in":
            return first_block
        if only_pass_on:
            return pass_on(name, after)
        return gathered(name, after).reshape(-1, D_MODEL)

    exchanges = {}

    def put_grad(name, dw, after=None):
        dev_major = dw.reshape(N_DEV, -1, D_MODEL)
        (exchanges[name],), token = exchange_start([dev_major], False, f"exchange_{name}_start", after)
        return token

    grad_x, small = local_step(
        x[0], loss_target[0], mod, tok, get_w, put_grad, wc_full, wf_full,
        flat("g_norm_mix"), flat("b_conv_dw"), flat("ln_conv_g"), flat("ln_conv_b"), flat("g_conv_out"),
        flat("g_attn_out"), flat("g_norm_ffn"), flat("b_ffn_dw"), flat("g_final"))

    out = {}

    def finish(name, tr, after, then=None):
        mine, parts = exchange_wait(exchanges[name], after, False, f"exchange_{name}_wait")
        if then is not None:
            then(parts)
        if name in ("w_in", "w_up"):
            res = sum_adamw(parts, mine, me1, flat_t(name), flat_t(name, "m_"), flat_t(name, "v_"), tr, "adamw_" + name)
            out[name] = tuple(r.T for r in res)
        else:
            res = out[name] = sum_adamw(parts, mine, me1, flat(name), flat(name, "m_"), flat(name, "v_"), tr,
                                        "adamw_" + name)
        return res[0]

    after = finish("w_down", r_down, grad_x)
    after = finish("w_up", n_up // 2, after)
    after = finish("w_out", r_out, after)
    packed = pack_small(small)
    small_gather = []
    after = finish("w_in", n_in, after, then=lambda parts: small_gather.append(
        gather2_start([packed], "gather_small_start", parts)[0][0]))
    (handle,), _ = gather2_pass(small_gather, after, "gather_small_pass")
    small_all = lax.dynamic_update_index_in_dim(
        gather2_wait(handle, after, "gather_small_wait"), packed, me, axis=0)

    wmv = {n: (flat(n), flat(n, "m_"), flat(n, "v_")) for n in SMALL_ORDER}
    per, g_wc, g_wf, loss = small_adamw(small_all, wmv, "adamw_small")
    out.update(per)
    taps = shard_adamw([(_shard(g_wc, n_wc, me), flat("w_conv_dw"), flat("w_conv_dw", "m_"), flat("w_conv_dw", "v_")),
                        (_shard(g_wf, n_wf, me), flat("w_ffn_dw"), flat("w_ffn_dw", "m_"), flat("w_ffn_dw", "v_"))],
                       "adamw_taps")
    out["w_conv_dw"], out["w_ffn_dw"] = taps

    dmod_cols = _shard(small_all[:, 0, :], n_ada, me)
    out["w_ada"] = ada_bwd_adamw(c_all, dmod_cols, flat("w_ada"), flat("w_ada", "m_"), flat("w_ada", "v_"), "adamw_w_ada")

    result = [loss, grad_x[None]]
    for k in range(4):
        result += [out[n][k].reshape(args[n].shape) for n in WEIGHTS]
    return tuple(result)
```

```python
import jax
import jax.numpy as jnp
from jax import lax
from jax.experimental import pallas as pl
from jax.experimental.pallas import tpu as pltpu

F32 = jnp.float32
BF16 = jnp.bfloat16

N_DEV = 8
SEQ = 2048
D_MODEL = 1024
D_CONV = 512
D_ATTN = 512
HEAD_DIM = 64
CONV_K = 31
D_FF = 2816
FFN_K = 3
D_IN = 2 * D_CONV + 3 * D_ATTN
N_MOD = 6
EPS = 1e-6
ATTN_BLOCK = 128
PATTERNS = ((2048, 1), (512, 4), (128, 16))
NEG = -1e30

ADAM_LR, ADAM_B1, ADAM_B2, ADAM_EPS, ADAM_WD, ADAM_STEP = 0.001, 0.9, 0.999, 1e-08, 0.01, 10

ROWS = 256
CONV_HALO = 32
FFN_HALO = 8
FFN_TN = 1408
VMEM_LIMIT = 56 * 1024 * 1024


NT = (((1,), (1,)), ((), ()))
ANY = pl.BlockSpec(memory_space=pl.ANY)


def _cp(*sem):
    return pltpu.CompilerParams(dimension_semantics=sem if sem else None, vmem_limit_bytes=VMEM_LIMIT)


def _with_after(body, n_in, after):
    if after is None:
        return body, [], []
    return (lambda *refs: body(*refs[:n_in], *refs[n_in + 1:])), [ANY], [after]


def _sig(x):
    return 1.0 / (1.0 + jnp.exp(-x))


def _rsum(x):
    return jnp.sum(x, axis=0, keepdims=True)


def _mean(x):
    return jnp.mean(x, axis=-1, keepdims=True)


def _acc(ref, val, first):
    @pl.when(first)
    def _():
        ref[...] = val

    @pl.when(jnp.logical_not(first))
    def _():
        ref[...] += val


SUB = 16


def _for_chunks(fn, unroll=1, rows=ROWS):
    def step(i, carry):
        fn(pl.ds(pl.multiple_of(i * SUB, SUB), SUB))
        return carry

    lax.fori_loop(0, rows // SUB, step, 0, unroll=unroll)


PAIR = 2 * ROWS


def _pair_spec(width, col=0):
    return pl.BlockSpec((PAIR, width), lambda i: (i, col))


def _halves():
    return [slice(h * ROWS, (h + 1) * ROWS) for h in range(2)]


def rms_mod_matmul(x, g, mod, sh_row, sc_row, b, b_rows, name, after=None):
    n = N_DEV * b_rows

    def body(x_ref, g_ref, mod_ref, b_ref, o_ref, h_ref):
        for rs in _halves():
            xx = x_ref[rs, :]
            r = lax.rsqrt(_mean(xx * xx) + EPS)
            h = (xx * r * g_ref[...] * (1.0 + mod_ref[sc_row:sc_row + 1, :]) + mod_ref[sh_row:sh_row + 1, :]).astype(BF16)
            h_ref[rs, :] = h
            o_ref[rs, :] = lax.dot_general(h, b_ref[...].reshape(n, D_MODEL), NT, preferred_element_type=F32)

    body, more_specs, more = _with_after(body, 4, after)
    return pl.pallas_call(
        body, out_shape=(jax.ShapeDtypeStruct((SEQ, n), F32), jax.ShapeDtypeStruct((SEQ, D_MODEL), BF16)),
        grid=(SEQ // PAIR,),
        in_specs=[_pair_spec(D_MODEL), _vec_spec(D_MODEL), _vec_spec(D_MODEL, 8),
                  pl.BlockSpec((N_DEV, b_rows, D_MODEL), lambda i: (0, 0, 0))] + more_specs,
        out_specs=(_pair_spec(n), _pair_spec(D_MODEL)), name=name, compiler_params=_cp("parallel"))(x, g, mod, b, *more)


def norm_concat_matmul(mix_a, att, gao, w, name):
    def body(a_ref, att_ref, g_ref, w_ref, y_ref, mixed_ref):
        for rs in _halves():
            aa = att_ref[rs, :]
            mixed_ref[rs, 0:D_CONV] = a_ref[rs, :]
            mixed_ref[rs, D_CONV:] = (aa * lax.rsqrt(_mean(aa * aa) + EPS) * g_ref[...]).astype(BF16)
            y_ref[rs, :] = jnp.dot(mixed_ref[rs, :], w_ref[...], preferred_element_type=F32)

    return pl.pallas_call(
        body, out_shape=(jax.ShapeDtypeStruct((SEQ, D_MODEL), F32), jax.ShapeDtypeStruct((SEQ, D_MODEL), BF16)),
        grid=(SEQ // PAIR,),
        in_specs=[_pair_spec(D_CONV), _pair_spec(D_ATTN), _vec_spec(D_ATTN), pl.BlockSpec(w.shape, lambda i: (0, 0))],
        out_specs=(_pair_spec(D_MODEL), _pair_spec(D_MODEL)), name=name, compiler_params=_cp("parallel"))(mix_a, att, gao, w)


def resid_rms_mod_matmul(x, y, g, mod, ga_row, sh_row, sc_row, w, name):
    n = w.shape[0]

    def body(x_ref, y_ref, g_ref, mod_ref, w_ref, o_ref, x1_ref, h_ref):
        x1 = x_ref[...] + mod_ref[ga_row:ga_row + 1, :] * y_ref[...]
        x1_ref[...] = x1
        r = lax.rsqrt(_mean(x1 * x1) + EPS)
        h = (x1 * r * g_ref[...] * (1.0 + mod_ref[sc_row:sc_row + 1, :]) + mod_ref[sh_row:sh_row + 1, :]).astype(BF16)
        h_ref[...] = h
        o_ref[...] = lax.dot_general(h, w_ref[...], NT, preferred_element_type=F32)

    return pl.pallas_call(
        body,
        out_shape=(jax.ShapeDtypeStruct((SEQ, n), F32), jax.ShapeDtypeStruct((SEQ, D_MODEL), F32),
                   jax.ShapeDtypeStruct((SEQ, D_MODEL), BF16)),
        grid=(SEQ // ROWS,),
        in_specs=[_row_spec(D_MODEL), _row_spec(D_MODEL), _vec_spec(D_MODEL), _vec_spec(D_MODEL, 8),
                  pl.BlockSpec(w.shape, lambda i: (0, 0))],
        out_specs=(_row_spec(n), _row_spec(D_MODEL), _row_spec(D_MODEL)),
        name=name, compiler_params=_cp("parallel"))(x, y, g, mod, w)


def matmul_combine_bwd(dy, w, att, gao, name, after=None):
    def body(dy_ref, w_ref, att_ref, g_ref, dm_ref, do_ref, dd_ref, dg_ref):
        @pl.when(pl.program_id(0) == 0)
        def _():
            dg_ref[...] = jnp.zeros_like(dg_ref)

        same_head = (jnp.right_shift(lax.broadcasted_iota(jnp.int32, (D_ATTN, D_ATTN), 0), 6)
                     == jnp.right_shift(lax.broadcasted_iota(jnp.int32, (D_ATTN, D_ATTN), 1), 6)).astype(F32)
        for rs in _halves():
            dmixed = lax.dot_general(dy_ref[rs, :], w_ref[...], NT, preferred_element_type=F32)
            dm_ref[rs, :] = dmixed
            att = att_ref[rs, :]
            r = lax.rsqrt(_mean(att * att) + EPS)
            xn = att * r
            dm = dmixed[:, D_CONV:]
            dg_ref[...] += _rsum(dm * xn)
            dyn = dm * g_ref[...]
            do = r * (dyn - xn * _mean(dyn * xn))
            do_ref[rs, :] = do
            dd_ref[rs, :] = jnp.dot(do * att, same_head, preferred_element_type=F32, precision=lax.Precision.HIGHEST)

    rs = _pair_spec(D_ATTN)
    f = jax.ShapeDtypeStruct((SEQ, D_ATTN), F32)
    body, more_specs, more = _with_after(body, 4, after)
    return pl.pallas_call(
        body, out_shape=(jax.ShapeDtypeStruct((SEQ, D_MODEL), F32), f, f, jax.ShapeDtypeStruct((1, D_ATTN), F32)),
        grid=(SEQ // PAIR,),
        in_specs=[_pair_spec(D_MODEL), pl.BlockSpec(w.shape, lambda i: (0, 0)), rs, _vec_spec(D_ATTN)] + more_specs,
        out_specs=(_pair_spec(D_MODEL), rs, rs, _vec_spec(D_ATTN)),
        name=name, compiler_params=_cp("arbitrary"))(dy, w, att, gao, *more)


def matmul_loss_bwd(act, w, x1, tgt, g, mod, ga_row, name):
    def body(a_ref, w_ref, x1_ref, t_ref, g_ref, mod_ref, loss_ref, dx2_ref, dy2_ref, dg_ref, dga_ref):
        @pl.when(pl.program_id(0) == 0)
        def _():
            loss_ref[...] = jnp.zeros_like(loss_ref)
            dg_ref[...] = jnp.zeros_like(dg_ref)
            dga_ref[...] = jnp.zeros_like(dga_ref)

        ga = mod_ref[ga_row:ga_row + 1, :]
        for rs in _halves():
            y2 = jnp.dot(a_ref[rs, :], w_ref[...], preferred_element_type=F32)
            x2 = x1_ref[rs, :] + ga * y2
            r = lax.rsqrt(_mean(x2 * x2) + EPS)
            xn = x2 * r
            err = xn * g_ref[...] - t_ref[rs, :]
            loss_ref[...] += jnp.broadcast_to(0.5 * jnp.sum(_mean(err * err)), (8, 128))
            dy = err * (1.0 / D_MODEL)
            dg_ref[...] += _rsum(dy * xn)
            dxn = dy * g_ref[...]
            dx2 = r * (dxn - xn * _mean(dxn * xn))
            dx2_ref[rs, :] = dx2
            dy2_ref[rs, :] = (dx2 * ga).astype(BF16)
            dga_ref[...] += _rsum(dx2 * y2)

    vec = jax.ShapeDtypeStruct((1, D_MODEL), F32)
    rows = _pair_spec
    return pl.pallas_call(
        body,
        out_shape=(jax.ShapeDtypeStruct((8, 128), F32), jax.ShapeDtypeStruct((SEQ, D_MODEL), F32),
                   jax.ShapeDtypeStruct((SEQ, D_MODEL), BF16), vec, vec),
        grid=(SEQ // PAIR,),
        in_specs=[rows(act.shape[1]), pl.BlockSpec(w.shape, lambda i: (0, 0)), rows(D_MODEL), rows(D_MODEL),
                  _vec_spec(D_MODEL), _vec_spec(D_MODEL, 8)],
        out_specs=(pl.BlockSpec((8, 128), lambda i: (0, 0)), rows(D_MODEL), rows(D_MODEL),
                   _vec_spec(D_MODEL), _vec_spec(D_MODEL)),
        name=name, compiler_params=_cp("arbitrary"))(act, w, x1, tgt, g, mod)


def matmul_rms_mod_bwd(a, b, x, dres, g, mod, sc_row, y, ga_row, name, b_rows=None, after=None):
    gated = y is not None
    pieces = isinstance(a, tuple)
    tm = PAIR if pieces else ROWS
    blocks = [slice(h * ROWS, (h + 1) * ROWS) for h in range(tm // ROWS)]
    rows = lambda width: pl.BlockSpec((tm, width), lambda i: (i, 0))
    if pieces:
        a1, a3 = a
        a_args = [a1, a3]
        a_specs = [rows(a1.shape[1]), pl.BlockSpec((3, tm, a3.shape[2]), lambda i: (0, i, 0))]
        b_arg, b_spec = b, pl.BlockSpec((N_DEV, b_rows, D_MODEL), lambda i: (0, 0, 0))
    else:
        k2 = a.shape[2]
        a_args = [a]
        a_specs = [pl.BlockSpec((2, tm, k2), lambda i: (0, i, 0))]
        b_arg, b_spec = b.reshape(2, k2, D_MODEL), pl.BlockSpec((2, k2, D_MODEL), lambda i: (0, 0, 0))
    n_a = len(a_args)

    def body(*refs):
        a_refs, (b_ref, x_ref, dres_ref, g_ref, mod_ref) = refs[:n_a], refs[n_a:n_a + 5]
        if gated:
            y_ref, dx_ref, dsh_ref, dsc_ref, dg_ref, dy_ref, dga_ref = refs[n_a + 5:]
        else:
            dx_ref, dsh_ref, dsc_ref, dg_ref = refs[n_a + 5:]

        @pl.when(pl.program_id(0) == 0)
        def _():
            for ref in (dsh_ref, dsc_ref, dg_ref) + ((dga_ref,) if gated else ()):
                ref[...] = jnp.zeros_like(ref)

        gg = g_ref[...]
        for rs in blocks:
            if pieces:
                wv = b_ref[...].reshape(N_DEV * b_rows, D_MODEL)
                k1, k3 = a_refs[0].shape[1], a_refs[1].shape[2]
                dh = jnp.dot(a_refs[0][rs, :], wv[0:k1], preferred_element_type=F32)
                for t in range(3):
                    dh = dh + jnp.dot(a_refs[1][t, rs, :], wv[k1 + t * k3:k1 + (t + 1) * k3], preferred_element_type=F32)
            else:
                dh = (jnp.dot(a_refs[0][0, rs, :], b_ref[0], preferred_element_type=F32)
                      + jnp.dot(a_refs[0][1, rs, :], b_ref[1], preferred_element_type=F32))
            xx = x_ref[rs, :]
            r = lax.rsqrt(_mean(xx * xx) + EPS)
            xn = xx * r
            dsh_ref[...] += _rsum(dh)
            dsc_ref[...] += _rsum(dh * (xn * gg))
            dt = dh * (1.0 + mod_ref[sc_row:sc_row + 1, :])
            dg_ref[...] += _rsum(dt * xn)
            dxn = dt * gg
            dx = dres_ref[rs, :] + r * (dxn - xn * _mean(dxn * xn))
            dx_ref[rs, :] = dx
            if gated:
                dga_ref[...] += _rsum(dx * y_ref[rs, :])
                dy_ref[rs, :] = (dx * mod_ref[ga_row:ga_row + 1, :]).astype(BF16)

    vec = jax.ShapeDtypeStruct((1, D_MODEL), F32)
    in_specs = a_specs + [b_spec, rows(D_MODEL), rows(D_MODEL), _vec_spec(D_MODEL), _vec_spec(D_MODEL, 8)]
    out_shape = [jax.ShapeDtypeStruct((SEQ, D_MODEL), F32), vec, vec, vec]
    out_specs = [rows(D_MODEL), _vec_spec(D_MODEL), _vec_spec(D_MODEL), _vec_spec(D_MODEL)]
    args = a_args + [b_arg, x, dres, g, mod]
    if gated:
        in_specs.append(rows(D_MODEL))
        out_shape += [jax.ShapeDtypeStruct((SEQ, D_MODEL), BF16), vec]
        out_specs += [rows(D_MODEL), _vec_spec(D_MODEL)]
        args.append(y)
    body, more_specs, more = _with_after(body, len(args), after)
    return pl.pallas_call(
        body, out_shape=tuple(out_shape), grid=(SEQ // tm,), in_specs=in_specs + more_specs, out_specs=tuple(out_specs),
        name=name, compiler_params=_cp("arbitrary"))(*args, *more)


def _prev_halo(halo, width, col):
    per = ROWS // halo
    return pl.BlockSpec((halo, width), lambda i: (jnp.maximum(i * per - 1, 0), col))


def _next_halo(halo, width, col):
    per = ROWS // halo
    last = SEQ // halo - 1
    return pl.BlockSpec((halo, width), lambda i: (jnp.minimum((i + 1) * per, last), col))


CONV_PAD = ROWS + CONV_HALO


def _shift_copies(sh):
    for b in range(1, 8):
        sh[b, 0:CONV_PAD - 8, :] = sh[0, pl.ds(b, CONV_PAD - 8), :]


def _tap(sh, rs_start, offset):
    return sh[offset % 8, pl.ds(pl.multiple_of(rs_start + (offset // 8) * 8, 8), SUB), :]


def _conv_glu(av_ref, ag_ref, avh_ref, agh_ref, sh):
    i = pl.program_id(0)
    hv = avh_ref[...] * _sig(agh_ref[...])
    sh[0, 0:CONV_HALO, :] = jnp.where(i > 0, hv, 0.0)

    def glu(rs):
        sh[0, pl.ds(pl.multiple_of(rs.start + CONV_HALO, SUB), SUB), :] = av_ref[rs, :] * _sig(ag_ref[rs, :])

    _for_chunks(glu)
    _shift_copies(sh)


def _conv_norm(u1, lg_ref, lb_ref):
    mu = _mean(u1)
    cen = u1 - mu
    rs = lax.rsqrt(_mean(cen * cen) + EPS)
    z = cen * rs
    ln = z * lg_ref[...] + lb_ref[...]
    s = _sig(ln)
    return z, rs, ln, s, ln * s


def conv_module_fwd(proj, wc, bc, lg, lb, gco, name):
    def body(av_ref, ag_ref, avh_ref, agh_ref, wc_ref, bc_ref, lg_ref, lb_ref, gco_ref, out_ref, u1_ref, sh):
        _conv_glu(av_ref, ag_ref, avh_ref, agh_ref, sh)

        def conv(rs):
            u1 = jnp.broadcast_to(bc_ref[...], (SUB, D_CONV))
            for j in range(CONV_K):
                u1 = u1 + wc_ref[j:j + 1, :] * _tap(sh, rs.start, CONV_HALO - (CONV_K - 1) + j)
            u1_ref[rs, :] = u1

        _for_chunks(conv)
        _, _, _, _, u2 = _conv_norm(u1_ref[...], lg_ref, lb_ref)
        rc = lax.rsqrt(_mean(u2 * u2) + EPS)
        out_ref[...] = (u2 * rc * gco_ref[...]).astype(BF16)

    v = _vec_spec(D_CONV)
    return pl.pallas_call(
        body, out_shape=(jax.ShapeDtypeStruct((SEQ, D_CONV), BF16), jax.ShapeDtypeStruct((SEQ, D_CONV), F32)),
        grid=(SEQ // ROWS,),
        in_specs=[_row_spec(D_CONV, 0), _row_spec(D_CONV, 1), _prev_halo(CONV_HALO, D_CONV, 0),
                  _prev_halo(CONV_HALO, D_CONV, 1), _vec_spec(D_CONV, CONV_K), v, v, v, v],
        out_specs=(_row_spec(D_CONV), _row_spec(D_CONV)), scratch_shapes=[pltpu.VMEM((8, CONV_PAD, D_CONV), F32)],
        name=name, compiler_params=_cp("parallel"))(proj, proj, proj, proj, wc, bc, lg, lb, gco)


def conv_module_bwd_a(proj, u1, dmixed, lg, lb, gco, name):
    def body(av_ref, ag_ref, avh_ref, agh_ref, u1_ref, dm_ref, lg_ref, lb_ref, gco_ref,
             du1_ref, dgco_ref, dlg_ref, dlb_ref, dbc_ref, dwc_ref, sh, acc):
        first = pl.program_id(0) == 0
        _conv_glu(av_ref, ag_ref, avh_ref, agh_ref, sh)
        z, rs, ln, s, u2 = _conv_norm(u1_ref[...], lg_ref, lb_ref)
        rc = lax.rsqrt(_mean(u2 * u2) + EPS)
        xn = u2 * rc
        dm = dm_ref[...]
        _acc(dgco_ref, _rsum(dm * xn), first)
        dyn = dm * gco_ref[...]
        du2 = rc * (dyn - xn * _mean(dyn * xn))
        dln = du2 * (s * (1.0 + ln * (1.0 - s)))
        _acc(dlg_ref, _rsum(dln * z), first)
        _acc(dlb_ref, _rsum(dln), first)
        dz = dln * lg_ref[...]
        du1 = rs * (dz - _mean(dz) - z * _mean(dz * z))
        du1_ref[...] = du1
        _acc(dbc_ref, _rsum(du1), first)
        acc[...] = jnp.zeros_like(acc)

        def taps(rs):
            d = du1_ref[rs, :]
            for j in range(CONV_K):
                acc[j] += d * _tap(sh, rs.start, CONV_HALO - (CONV_K - 1) + j)

        _for_chunks(taps)

        @pl.when(first)
        def _():
            dwc_ref[...] = jnp.zeros_like(dwc_ref)

        for j in range(CONV_K):
            dwc_ref[j:j + 1, :] += _rsum(acc[j])

    v = _vec_spec(D_CONV)
    vec = jax.ShapeDtypeStruct((1, D_CONV), F32)
    return pl.pallas_call(
        body,
        out_shape=(jax.ShapeDtypeStruct((SEQ, D_CONV), F32), vec, vec, vec, vec, jax.ShapeDtypeStruct((CONV_K, D_CONV), F32)),
        grid=(SEQ // ROWS,),
        in_specs=[_row_spec(D_CONV, 0), _row_spec(D_CONV, 1), _prev_halo(CONV_HALO, D_CONV, 0),
                  _prev_halo(CONV_HALO, D_CONV, 1), _row_spec(D_CONV, 0), _row_spec(D_CONV, 0), v, v, v],
        out_specs=(_row_spec(D_CONV), v, v, v, v, _vec_spec(D_CONV, CONV_K)),
        scratch_shapes=[pltpu.VMEM((8, CONV_PAD, D_CONV), F32), pltpu.VMEM((CONV_K, SUB, D_CONV), F32)],
        name=name, compiler_params=_cp("arbitrary"))(proj, proj, proj, proj, u1, dmixed, lg, lb, gco)


def conv_module_bwd_b(proj, du1, wc, name):
    def body(av_ref, ag_ref, du1_ref, du1n_ref, wc_ref, out_ref, sh):
        i = pl.program_id(0)
        sh[0, 0:ROWS, :] = du1_ref[...]
        sh[0, ROWS:, :] = jnp.where(i < SEQ // ROWS - 1, du1n_ref[...], 0.0)
        _shift_copies(sh)

        def chunk(rs):
            du0 = jnp.zeros((SUB, D_CONV), F32)
            for j in range(CONV_K):
                du0 = du0 + wc_ref[j:j + 1, :] * _tap(sh, rs.start, CONV_K - 1 - j)
            sg = _sig(ag_ref[rs, :])
            out_ref[rs, 0:D_CONV] = (du0 * sg).astype(BF16)
            out_ref[rs, D_CONV:] = (du0 * av_ref[rs, :] * sg * (1.0 - sg)).astype(BF16)

        _for_chunks(chunk)

    return pl.pallas_call(
        body, out_shape=jax.ShapeDtypeStruct((SEQ, 2 * D_CONV), BF16), grid=(SEQ // ROWS,),
        in_specs=[_row_spec(D_CONV, 0), _row_spec(D_CONV, 1), _row_spec(D_CONV, 0), _next_halo(CONV_HALO, D_CONV, 0),
                  _vec_spec(D_CONV, CONV_K)],
        out_specs=_row_spec(2 * D_CONV), scratch_shapes=[pltpu.VMEM((8, CONV_PAD, D_CONV), F32)],
        name=name, compiler_params=_cp("parallel"))(proj, proj, du1, du1, wc)


def _rows(start, size, r):
    return pl.ds(start, size) if r == 1 else pl.ds(start, size, stride=r)


def _unit_rows(r, rho, n, nb):
    win = 2 * ATTN_BLOCK if nb > 1 else ATTN_BLOCK
    if isinstance(n, int):
        kb = max(n - 1, 0)
        q_rows = _rows(rho + r * ATTN_BLOCK * n, ATTN_BLOCK, r)
        k_rows = _rows(rho + r * ATTN_BLOCK * kb, win, r)
    else:
        kb = jnp.maximum(n - 1, 0)
        q_rows = pl.ds(pl.multiple_of(n * ATTN_BLOCK, ATTN_BLOCK), ATTN_BLOCK)
        k_rows = pl.ds(pl.multiple_of(kb * ATTN_BLOCK, ATTN_BLOCK), win)
    return q_rows, k_rows, n - kb


def _band_bias(first_block, transposed):
    shape = (2 * ATTN_BLOCK, ATTN_BLOCK) if transposed else (ATTN_BLOCK, 2 * ATTN_BLOCK)
    q_axis = 1 if transposed else 0
    dist = (0 if first_block else ATTN_BLOCK) + lax.broadcasted_iota(jnp.int32, shape, q_axis) \
        - lax.broadcasted_iota(jnp.int32, shape, 1 - q_axis)
    return jnp.where((dist >= 0) & (dist <= ATTN_BLOCK), 0.0, NEG)


def _per_head(x):
    lane = lax.broadcasted_iota(jnp.int32, x.shape, 1)
    zero = jnp.zeros_like(x)
    return [jnp.where(lane < HEAD_DIM, x, zero), jnp.where(lane >= HEAD_DIM, x, zero)]


SCALE = HEAD_DIM ** -0.5


def _masked_scores(q2, k2, bias):
    return [lax.dot_general(qh, k2, NT, preferred_element_type=F32) + bias for qh in _per_head(q2)]


def _attn_units(r, nb, unit):
    if r == 1:
        def four(i, carry):
            for k in range(4):
                unit(0, 4 * i + k)
            return carry
        lax.fori_loop(0, nb // 4, four, 0)
    else:
        for rho in range(r):
            for n in range(nb):
                unit(rho, n)


N_UNITS = 16


def attn_fwd_all(proj, name):
    def body(q_ref, k_ref, v_ref, att_ref, lse_ref, s_scr, p_scr, lse_scr, den_scr, bias_scr):
        bias_scr[0] = _band_bias(True, False)
        bias_scr[1] = _band_bias(False, False)
        for idx, (sub_len, r) in enumerate(PATTERNS):
            nb = sub_len // ATTN_BLOCK
            win = 2 * ATTN_BLOCK if nb > 1 else ATTN_BLOCK

            def scores(rho, n, r=r, nb=nb, win=win):
                u = rho * nb + n
                q_rows, k_rows, variant = _unit_rows(r, rho, n, nb)
                ss = _masked_scores((q_ref[q_rows, :] * SCALE).astype(BF16), k_ref[k_rows, :].astype(BF16),
                                    bias_scr[variant, :, 0:win])
                for h in range(2):
                    s_scr[2 * u + h, :, 0:win] = ss[h]

            _attn_units(r, nb, scores)

            def softmax(u, carry, win=win):
                lses, dens = [], []
                for h in range(2):
                    sc = s_scr[2 * u + h, :, 0:win]
                    m = jnp.max(sc, axis=1, keepdims=True)
                    p = jnp.exp(sc - m)
                    den = jnp.sum(p, axis=1, keepdims=True)
                    p_scr[2 * u + h, :, 0:win] = p.astype(BF16)
                    lses.append(jnp.broadcast_to(m + jnp.log(den), (ATTN_BLOCK, HEAD_DIM)))
                    dens.append(jnp.broadcast_to(den, (ATTN_BLOCK, HEAD_DIM)))
                lse_scr[u] = jnp.concatenate(lses, axis=1)
                den_scr[u] = jnp.concatenate(dens, axis=1)
                return carry

            lax.fori_loop(0, N_UNITS, softmax, 0, unroll=2)

            def outputs(rho, n, r=r, nb=nb, win=win, idx=idx):
                u = rho * nb + n
                q_rows, k_rows, _ = _unit_rows(r, rho, n, nb)
                vs = _per_head(v_ref[k_rows, :].astype(BF16))
                o = (jnp.dot(p_scr[2 * u, :, 0:win], vs[0], preferred_element_type=F32)
                     + jnp.dot(p_scr[2 * u + 1, :, 0:win], vs[1], preferred_element_type=F32)) / den_scr[u]
                lse = lse_scr[u]
                if idx > 0:
                    old = lse_ref[q_rows, :]
                    top = jnp.maximum(old, lse)
                    new = top + jnp.log(jnp.exp(old - top) + jnp.exp(lse - top))
                    o = att_ref[q_rows, :] * jnp.exp(old - new) + o * jnp.exp(lse - new)
                    lse = new
                att_ref[q_rows, :] = o
                lse_ref[q_rows, :] = lse

            _attn_units(r, nb, outputs)

    blk = lambda first: pl.BlockSpec((SEQ, 128), lambda g: (0, first + g))
    shp = jax.ShapeDtypeStruct((SEQ, D_ATTN), F32)
    big = (2 * N_UNITS, ATTN_BLOCK, 2 * ATTN_BLOCK)
    small = pltpu.VMEM((N_UNITS, ATTN_BLOCK, 128), F32)
    return pl.pallas_call(
        body, out_shape=(shp, shp), grid=(4,), in_specs=[blk(8), blk(12), blk(16)], out_specs=(blk(0), blk(0)),
        scratch_shapes=[pltpu.VMEM(big, F32), pltpu.VMEM(big, BF16), small, small,
                        pltpu.VMEM((2, ATTN_BLOCK, 2 * ATTN_BLOCK), F32)],
        name=name, compiler_params=_cp("parallel"))(proj, proj, proj)


def attn_bwd_all(proj, do, lse, dd, name):
    def body(q_ref, k_ref, v_ref, do_ref, l_ref, dd_ref, out_ref, dq_s, dk_s, dv_s,
             s_scr, dp_scr, ds_scr, st_scr, dpt_scr, pt_scr, dst_scr, qb_scr, kb_scr, dob_scr, bias_scr, bias_t_scr):
        for first_block in (True, False):
            bias_scr[1 - int(first_block)] = _band_bias(first_block, False)
            bias_t_scr[1 - int(first_block)] = _band_bias(first_block, True)
        dq_s[...] = jnp.zeros_like(dq_s)
        dk_s[...] = jnp.zeros_like(dk_s)
        dv_s[...] = jnp.zeros_like(dv_s)
        for sub_len, r in PATTERNS:
            nb = sub_len // ATTN_BLOCK
            win = 2 * ATTN_BLOCK if nb > 1 else ATTN_BLOCK

            def scores(rho, n, r=r, nb=nb, win=win):
                u = rho * nb + n
                q_rows, k_rows, variant = _unit_rows(r, rho, n, nb)
                bias, bias_t = bias_scr[variant, :, 0:win], bias_t_scr[variant, 0:win, :]
                q2 = (q_ref[q_rows, :] * SCALE).astype(BF16)
                kf = k_ref[k_rows, :]
                k2 = kf.astype(BF16)
                do2 = do_ref[q_rows, :].astype(BF16)
                qb_scr[u] = q2
                kb_scr[u, 0:win, :] = (kf * SCALE).astype(BF16)
                dob_scr[u] = do2
                l2 = l_ref[q_rows, :]
                d2 = dd_ref[q_rows, :]
                l2t = l2.T
                d2t = d2.T
                v2 = v_ref[k_rows, :].astype(BF16)
                qs, dos = _per_head(q2), _per_head(do2)
                for h in range(2):
                    c0 = h * HEAD_DIM
                    sc = lax.dot_general(qs[h], k2, NT, preferred_element_type=F32)
                    s_scr[2 * u + h, :, 0:win] = sc + bias - l2[:, c0:c0 + 1]
                    dp_scr[2 * u + h, :, 0:win] = lax.dot_general(dos[h], v2, NT, preferred_element_type=F32) \
                        - d2[:, c0:c0 + 1]
                    sct = lax.dot_general(k2, qs[h], NT, preferred_element_type=F32)
                    st_scr[2 * u + h, 0:win, :] = sct + bias_t - l2t[c0:c0 + 1, :]
                    dpt_scr[2 * u + h, 0:win, :] = lax.dot_general(v2, dos[h], NT, preferred_element_type=F32) \
                        - d2t[c0:c0 + 1, :]

            _attn_units(r, nb, scores)

            def pointwise(hu, carry, win=win):
                ds_scr[hu, :, 0:win] = (jnp.exp(s_scr[hu, :, 0:win]) * dp_scr[hu, :, 0:win]).astype(BF16)
                pt = jnp.exp(st_scr[hu, 0:win, :])
                pt_scr[hu, 0:win, :] = pt.astype(BF16)
                dst_scr[hu, 0:win, :] = (pt * dpt_scr[hu, 0:win, :]).astype(BF16)
                return carry

            lax.fori_loop(0, 2 * N_UNITS, pointwise, 0, unroll=4)

            def grads(rho, n, r=r, nb=nb, win=win):
                u = rho * nb + n
                q_rows, k_rows, _ = _unit_rows(r, rho, n, nb)
                qs, ks, dos = _per_head(qb_scr[u]), _per_head(kb_scr[u, 0:win, :]), _per_head(dob_scr[u])

                def both(scr, rows, rhs):
                    return (jnp.dot(scr[(2 * u,) + rows], rhs[0], preferred_element_type=F32)
                            + jnp.dot(scr[(2 * u + 1,) + rows], rhs[1], preferred_element_type=F32))

                dq_s[q_rows, :] += both(ds_scr, (slice(None), slice(0, win)), ks)
                dk_s[k_rows, :] += both(dst_scr, (slice(0, win), slice(None)), qs)
                dv_s[k_rows, :] += both(pt_scr, (slice(0, win), slice(None)), dos)

            _attn_units(r, nb, grads)
        out_ref[0] = dq_s[...].astype(BF16)
        out_ref[1] = dk_s[...].astype(BF16)
        out_ref[2] = dv_s[...].astype(BF16)

    blk = lambda first: pl.BlockSpec((SEQ, 128), lambda g: (0, first + g))
    acc = pltpu.VMEM((SEQ, 128), F32)
    big = (2 * N_UNITS, ATTN_BLOCK, 2 * ATTN_BLOCK)
    big_t = (2 * N_UNITS, 2 * ATTN_BLOCK, ATTN_BLOCK)
    return pl.pallas_call(
        body, out_shape=jax.ShapeDtypeStruct((3, SEQ, D_ATTN), BF16), grid=(4,),
        in_specs=[blk(8), blk(12), blk(16), blk(0), blk(0), blk(0)],
        out_specs=pl.BlockSpec((3, SEQ, 128), lambda g: (0, 0, g)),
        scratch_shapes=[acc, acc, acc, pltpu.VMEM(big, F32), pltpu.VMEM(big, F32), pltpu.VMEM(big, BF16),
                        pltpu.VMEM(big_t, F32), pltpu.VMEM(big_t, F32), pltpu.VMEM(big_t, BF16), pltpu.VMEM(big_t, BF16),
                        pltpu.VMEM((N_UNITS, ATTN_BLOCK, 128), BF16),
                        pltpu.VMEM((N_UNITS, 2 * ATTN_BLOCK, 128), BF16), pltpu.VMEM((N_UNITS, ATTN_BLOCK, 128), BF16),
                        pltpu.VMEM((2, ATTN_BLOCK, 2 * ATTN_BLOCK), F32), pltpu.VMEM((2, 2 * ATTN_BLOCK, ATTN_BLOCK), F32)],
        name=name, compiler_params=_cp("parallel"))(proj, proj, proj, do, lse, dd)


N_FT = D_FF // FFN_TN


def _ffn_specs():
    per = ROWS // FFN_HALO
    cur_g = pl.BlockSpec((ROWS, FFN_TN), lambda j, i: (i, j))
    cur_v = pl.BlockSpec((ROWS, FFN_TN), lambda j, i: (i, j + N_FT))
    halo_g = pl.BlockSpec((FFN_HALO, FFN_TN), lambda j, i: (jnp.maximum(i * per - 1, 0), j))
    halo_v = pl.BlockSpec((FFN_HALO, FFN_TN), lambda j, i: (jnp.maximum(i * per - 1, 0), j + N_FT))
    w_g = pl.BlockSpec((FFN_K, FFN_TN), lambda j, i: (0, j))
    w_v = pl.BlockSpec((FFN_K, FFN_TN), lambda j, i: (0, j + N_FT))
    b_g = pl.BlockSpec((1, FFN_TN), lambda j, i: (0, j))
    b_v = pl.BlockSpec((1, FFN_TN), lambda j, i: (0, j + N_FT))
    return [cur_g, cur_v, halo_g, halo_v, w_g, w_v, b_g, b_v]


def matmul(a, b, kind, out_dtype, tm, tn, name, b_rows=None):
    stacked = b_rows is not None
    b_shape = (N_DEV * b_rows, b.shape[2]) if stacked else b.shape
    if kind == "nn":
        (m, k), n = a.shape, b_shape[1]
        a_spec = pl.BlockSpec((tm, k), lambda j, i: (i, 0))
        b_spec = pl.BlockSpec((k, tn), lambda j, i: (0, j))
        dims = (((1,), (0,)), ((), ()))
    elif kind == "nt":
        (m, k), n = a.shape, b_shape[0]
        a_spec = pl.BlockSpec((tm, k), lambda j, i: (i, 0))
        b_spec = pl.BlockSpec((tn, k), lambda j, i: (j, 0))
        dims = (((1,), (1,)), ((), ()))
    else:
        (k, m), n = a.shape, b_shape[1]
        a_spec = pl.BlockSpec((k, tm), lambda j, i: (0, i))
        b_spec = pl.BlockSpec((k, tn), lambda j, i: (0, j))
        dims = (((0,), (0,)), ((), ()))
    assert m % tm == 0 and n % tn == 0, (name, m, n, tm, tn)
    if stacked:
        assert b_spec.block_shape[0] == b_shape[0] and kind in ("nn", "nt")
        width = b_spec.block_shape[1]
        b_spec = pl.BlockSpec((N_DEV, b_rows, width), (lambda j, i: (0, 0, j)) if kind == "nn" else (lambda j, i: (0, 0, 0)))

    def body(a_ref, b_ref, o_ref):
        bb = b_ref[...].reshape(b_shape[0], -1) if stacked else b_ref[...]
        o_ref[...] = lax.dot_general(a_ref[...], bb, dims, preferred_element_type=F32).astype(o_ref.dtype)

    return pl.pallas_call(
        body, out_shape=jax.ShapeDtypeStruct((m, n), out_dtype), grid=(n // tn, m // tm),
        in_specs=[a_spec, b_spec], out_specs=pl.BlockSpec((tm, tn), lambda j, i: (i, j)),
        name=name, compiler_params=_cp("parallel", "parallel"))(a, b)


def matmul_tn_pieces(a1, a3, b, name):
    k, n = b.shape
    tm = a3.shape[2]
    n1 = a1.shape[1] // tm

    def body(a1_ref, a3_ref, b_ref, o_ref):
        i = pl.program_id(0)
        tn_dims = (((0,), (0,)), ((), ()))

        @pl.when(i < n1)
        def _():
            o_ref[...] = lax.dot_general(a1_ref[...], b_ref[...], tn_dims, preferred_element_type=F32).astype(BF16)

        @pl.when(i >= n1)
        def _():
            o_ref[...] = lax.dot_general(a3_ref[0], b_ref[...], tn_dims, preferred_element_type=F32).astype(BF16)

    return pl.pallas_call(
        body, out_shape=jax.ShapeDtypeStruct((a1.shape[1] + 3 * tm, n), BF16), grid=(n1 + 3,),
        in_specs=[pl.BlockSpec((k, tm), lambda i: (0, jnp.minimum(i, n1 - 1))),
                  pl.BlockSpec((1, k, tm), lambda i: (jnp.maximum(i - n1, 0), 0, 0)),
                  pl.BlockSpec((k, n), lambda i: (0, 0))],
        out_specs=pl.BlockSpec((tm, n), lambda i: (i, 0)), name=name, compiler_params=_cp("parallel"))(a1, a3, b)


def matmul_tn_halves(a, b, tm, name):
    _, k, m = a.shape
    n = b.shape[1]

    def body(a_ref, b_ref, o_ref):
        o_ref[0] = lax.dot_general(a_ref[0], b_ref[...], (((0,), (0,)), ((), ())), preferred_element_type=F32).astype(BF16)

    return pl.pallas_call(
        body, out_shape=jax.ShapeDtypeStruct((2, m, n), BF16), grid=(2, m // tm),
        in_specs=[pl.BlockSpec((1, k, tm), lambda h, i: (h, 0, i)), pl.BlockSpec((k, n), lambda h, i: (0, 0))],
        out_specs=pl.BlockSpec((1, tm, n), lambda h, i: (h, i, 0)), name=name,
        compiler_params=_cp("parallel", "parallel"))(a, b).reshape(2 * m, n)


def _row_spec(width, col=0):
    return pl.BlockSpec((ROWS, width), lambda i: (i, col))


def _vec_spec(width, rows=1):
    return pl.BlockSpec((rows, width), lambda i: (0, 0))


FFN_RB = 128


def _ffn_lane_blocks(fn):
    for c in range(FFN_TN // 128):
        fn(slice(c * 128, (c + 1) * 128))


def _ffn_taps(cur_ref, halo_ref, head, ls, r0):
    if r0 == 0:
        head[0:FFN_HALO, :] = jnp.where(pl.program_id(1) > 0, halo_ref[:, ls], 0.0)
        head[FFN_HALO:, :] = cur_ref[0:FFN_RB, ls]
        return tuple(head[pl.ds(FFN_HALO - k, FFN_RB), :] for k in (2, 1, 0))
    return tuple(cur_ref[pl.ds(r0 - k, FFN_RB), ls] for k in (2, 1, 0))


def _ffn_conv(taps, w, b):
    return b + w[0] * taps[0] + w[1] * taps[1] + w[2] * taps[2]


def ffn_act_fwd(up0, wf, bf, name, after=None):
    def body(g_ref, v_ref, gh_ref, vh_ref, wg_ref, wv_ref, bg_ref, bv_ref, act_ref, head_g, head_v):
        def lanes(ls):
            wg = [wg_ref[t:t + 1, ls] for t in range(FFN_K)]
            wv = [wv_ref[t:t + 1, ls] for t in range(FFN_K)]
            for r0 in range(0, ROWS, FFN_RB):
                gate = _ffn_conv(_ffn_taps(g_ref, gh_ref, head_g, ls, r0), wg, bg_ref[:, ls])
                val = _ffn_conv(_ffn_taps(v_ref, vh_ref, head_v, ls, r0), wv, bv_ref[:, ls])
                act_ref[r0:r0 + FFN_RB, ls] = (gate * _sig(gate) * val).astype(BF16)

        _ffn_lane_blocks(lanes)

    head = pltpu.VMEM((FFN_HALO + FFN_RB, 128), F32)
    body, more_specs, more = _with_after(body, 8, after)
    return pl.pallas_call(
        body, out_shape=jax.ShapeDtypeStruct((SEQ, D_FF), BF16), grid=(N_FT, SEQ // ROWS),
        in_specs=_ffn_specs() + more_specs, out_specs=pl.BlockSpec((ROWS, FFN_TN), lambda j, i: (i, j)),
        scratch_shapes=[head, head], name=name,
        compiler_params=_cp("parallel", "parallel"))(up0, up0, up0, up0, wf, wf, bf, bf, *more)


def ffn_bwd(up0, dact, wf, bf, name, after=None):
    per = ROWS // FFN_HALO
    last = SEQ // FFN_HALO - 1

    def body(g_ref, v_ref, gh_ref, vh_ref, wg_ref, wv_ref, bg_ref, bv_ref, da_ref, gn_ref, vn_ref, dan_ref,
             out_ref, dbg_ref, dbv_ref, dwg_ref, dwv_ref, pad, head_g, head_v, dgp, dvp, acc):
        i = pl.program_id(1)
        first = i == 0
        acc[...] = jnp.zeros_like(acc)

        def grads(gate, val, da):
            s = _sig(gate)
            return da * val * (s * (1.0 + gate * (1.0 - s))), da * (gate * s)

        fold = lambda q: jnp.sum(q.reshape(FFN_RB // 8, 8, 128), axis=0)

        def lanes(ls):
            wg = [wg_ref[t:t + 1, ls] for t in range(FFN_K)]
            wv = [wv_ref[t:t + 1, ls] for t in range(FFN_K)]
            sums = [jnp.zeros((8, 128), F32)] * (2 + 2 * FFN_K)
            for r0 in range(0, ROWS, FFN_RB):
                gs = _ffn_taps(g_ref, gh_ref, head_g, ls, r0)
                vs = _ffn_taps(v_ref, vh_ref, head_v, ls, r0)
                dgate, dval = grads(_ffn_conv(gs, wg, bg_ref[:, ls]), _ffn_conv(vs, wv, bv_ref[:, ls]),
                                    da_ref[r0:r0 + FFN_RB, ls])
                dgp[r0:r0 + FFN_RB, ls] = dgate
                dvp[r0:r0 + FFN_RB, ls] = dval
                new = [dgate, dval] + [dgate * gs[t] for t in range(FFN_K)] + [dval * vs[t] for t in range(FFN_K)]
                sums = [a + fold(q) for a, q in zip(sums, new)]
            for k in range(2 + 2 * FFN_K):
                acc[k, 0:8, ls] = sums[k]

        _ffn_lane_blocks(lanes)
        _acc(dbg_ref, _rsum(acc[0]), first)
        _acc(dbv_ref, _rsum(acc[1]), first)
        _acc(dwg_ref, jnp.concatenate([_rsum(acc[2 + t]) for t in range(FFN_K)], axis=0), first)
        _acc(dwv_ref, jnp.concatenate([_rsum(acc[5 + t]) for t in range(FFN_K)], axis=0), first)

        def conv_next(cur_ref, nxt_ref, w_ref, b_ref):
            pad[0:FFN_HALO, :] = cur_ref[ROWS - FFN_HALO:, :]
            pad[FFN_HALO:, :] = nxt_ref[...]
            return (b_ref[...] + w_ref[0:1, :] * pad[pl.ds(FFN_HALO - 2, FFN_HALO), :]
                    + w_ref[1:2, :] * pad[pl.ds(FFN_HALO - 1, FFN_HALO), :] + w_ref[2:3, :] * nxt_ref[...])

        gate_n = conv_next(g_ref, gn_ref, wg_ref, bg_ref)
        val_n = conv_next(v_ref, vn_ref, wv_ref, bv_ref)
        dgate_n, dval_n = grads(gate_n, val_n, dan_ref[...])
        inside = i < SEQ // ROWS - 1
        dgp[ROWS:, :] = jnp.where(inside, dgate_n, 0.0)
        dvp[ROWS:, :] = jnp.where(inside, dval_n, 0.0)

        def back(ls):
            for half, (dp, w_ref) in enumerate(((dgp, wg_ref), (dvp, wv_ref))):
                w = [w_ref[t:t + 1, ls] for t in range(FFN_K)]
                for r0 in range(0, ROWS, FFN_RB):
                    out_ref[half, r0:r0 + FFN_RB, ls] = (
                        w[2] * dp[r0:r0 + FFN_RB, ls] + w[1] * dp[pl.ds(r0 + 1, FFN_RB), ls]
                        + w[0] * dp[pl.ds(r0 + 2, FFN_RB), ls]).astype(BF16)

        _ffn_lane_blocks(back)

    body, more_specs, more = _with_after(body, 12, after)
    head = pltpu.VMEM((FFN_HALO + FFN_RB, 128), F32)
    ext = pltpu.VMEM((ROWS + FFN_HALO, FFN_TN), F32)
    vec = jax.ShapeDtypeStruct((1, D_FF), F32)
    taps = jax.ShapeDtypeStruct((FFN_K, D_FF), F32)
    cur = pl.BlockSpec((ROWS, FFN_TN), lambda j, i: (i, j))
    nxt = lambda off: pl.BlockSpec((FFN_HALO, FFN_TN), lambda j, i: (jnp.minimum((i + 1) * per, last), j + off))
    vs = pl.BlockSpec((1, FFN_TN), lambda j, i: (0, j))
    ts = pl.BlockSpec((FFN_K, FFN_TN), lambda j, i: (0, j))
    return pl.pallas_call(
        body, out_shape=(jax.ShapeDtypeStruct((2, SEQ, D_FF), BF16), vec, vec, taps, taps), grid=(N_FT, SEQ // ROWS),
        in_specs=_ffn_specs() + [cur, nxt(0), nxt(N_FT), nxt(0)] + more_specs,
        out_specs=(pl.BlockSpec((2, ROWS, FFN_TN), lambda j, i: (0, i, j)), vs, vs, ts, ts),
        scratch_shapes=[pltpu.VMEM((2 * FFN_HALO, FFN_TN), F32), head, head, ext, ext,
                        pltpu.VMEM((2 + 2 * FFN_K, SUB, FFN_TN), F32)],
        name=name, compiler_params=_cp("parallel", "arbitrary"))(up0, up0, up0, up0, wf, wf, bf, bf, dact, up0, up0, dact,
                                                                 *more)


def ada_fwd(c_all, w_ada, b_cols, name):
    def body(c_ref, w_ref, b_ref, o_ref):
        cc = c_ref[...]
        sc = (cc * _sig(cc)).astype(BF16)
        o_ref[...] = jnp.dot(sc, w_ref[...].astype(BF16), preferred_element_type=F32) + b_ref[...]

    return pl.pallas_call(body, out_shape=jax.ShapeDtypeStruct((N_DEV, w_ada.shape[1]), F32), name=name,
                          compiler_params=_cp())(c_all, w_ada, b_cols)


def _adam(w, g, m, v):
    m = ADAM_B1 * m + (1.0 - ADAM_B1) * g
    v = ADAM_B2 * v + (1.0 - ADAM_B2) * (g * g)
    m_hat = m / (1.0 - ADAM_B1 ** ADAM_STEP)
    v_hat = v / (1.0 - ADAM_B2 ** ADAM_STEP)
    delta = -ADAM_LR * (m_hat / (jnp.sqrt(v_hat) + ADAM_EPS) + ADAM_WD * w)
    return delta, m, v


def ada_bwd_adamw(c_all, dmod_cols, w, m, v, name):
    rows, cols = w.shape
    tr = 256

    def body(c_ref, dm_ref, w_ref, m_ref, v_ref, g_ref, d_ref, nm_ref, nv_ref):
        cc = c_ref[...]
        sc = (cc * _sig(cc)).T
        g = sc[:, 0:1] * dm_ref[0:1, :]
        for b in range(1, N_DEV):
            g = g + sc[:, b:b + 1] * dm_ref[b:b + 1, :]
        g_ref[...] = g
        d_ref[...], nm_ref[...], nv_ref[...] = _adam(w_ref[...], g, m_ref[...], v_ref[...])

    blk = pl.BlockSpec((tr, cols), lambda i: (i, 0))
    shp = jax.ShapeDtypeStruct((rows, cols), F32)
    return pl.pallas_call(
        body, out_shape=(shp, shp, shp, shp), grid=(rows // tr,),
        in_specs=[pl.BlockSpec((N_DEV, tr), lambda i: (0, i)), pl.BlockSpec((N_DEV, cols), lambda i: (0, 0)), blk, blk, blk],
        out_specs=(blk, blk, blk, blk), name=name, compiler_params=_cp("parallel"))(c_all, dmod_cols, w, m, v)


def sum_adamw(parts, mine, me, w, m, v, tr, name):
    n_parts, rows, cols = parts.shape

    def body(me_ref, p_ref, own_ref, w_ref, m_ref, v_ref, g_ref, d_ref, nm_ref, nv_ref):
        def chunk(rs):
            g = own_ref[0, rs, :].astype(F32)
            for k in range(1, n_parts):
                g = g + p_ref[k, rs, :].astype(F32)
            g_ref[rs, :] = g
            d_ref[rs, :], nm_ref[rs, :], nv_ref[rs, :] = _adam(w_ref[rs, :], g, m_ref[rs, :], v_ref[rs, :])

        _for_chunks(chunk, 2, tr)

    blk = pl.BlockSpec((tr, cols), lambda i, me_ref: (i, 0))
    shp = jax.ShapeDtypeStruct((rows, cols), F32)
    grid_spec = pltpu.PrefetchScalarGridSpec(
        num_scalar_prefetch=1, grid=(rows // tr,),
        in_specs=[pl.BlockSpec((n_parts, tr, cols), lambda i, me_ref: (0, i, 0)),
                  pl.BlockSpec((1, tr, cols), lambda i, me_ref: (me_ref[0], i, 0)), blk, blk, blk],
        out_specs=(blk, blk, blk, blk))
    return pl.pallas_call(body, out_shape=(shp, shp, shp, shp), grid_spec=grid_spec, name=name,
                          compiler_params=_cp("parallel"))(me, parts, mine, w, m, v)


MESH = pl.DeviceIdType.MESH


def all_gather(block, name, after=None):
    extra = () if after is None else (after,)

    def body(x_ref, *refs):
        out_ref, send_sems, recv_sems, local_sem = refs[len(extra):]
        x, y, c = lax.axis_index("x"), lax.axis_index("y"), lax.axis_index("c")
        me, sibling = (x, y, c), (x, y, 1 - c)
        chips = [(1 - x, y), (x, 1 - y), (1 - x, 1 - y)]

        def slot(px, py, pc):
            return out_ref.at[4 * px + 2 * py + pc]

        def copy(k, blk, to, src=None):
            return pltpu.make_async_remote_copy(
                src_ref=slot(*blk) if src is None else src, dst_ref=slot(*blk),
                send_sem=send_sems.at[k], recv_sem=recv_sems.at[k], device_id=to, device_id_type=MESH)

        mine = pltpu.make_async_copy(x_ref, slot(*me), local_sem)
        mine.start()
        first = [copy(0, me, sibling, src=x_ref)]
        first += [copy(1 + j, me, (*chip, c), src=x_ref) for j, chip in enumerate(chips)]
        for cp in first:
            cp.start()
        passed = [copy(4 + j, (*chip, c), sibling) for j, chip in enumerate(chips)]
        for j, chip in enumerate(chips):
            copy(1 + j, (*chip, c), me).wait_recv()
            passed[j].start()
        copy(0, sibling, me).wait_recv()
        for j, chip in enumerate(chips):
            copy(4 + j, (*chip, 1 - c), me).wait_recv()
        for cp in first + passed:
            cp.wait_send()
        mine.wait()

    return pl.pallas_call(
        body, out_shape=jax.ShapeDtypeStruct((N_DEV,) + block.shape, block.dtype), in_specs=[ANY] * (1 + len(extra)), out_specs=ANY,
        scratch_shapes=[pltpu.SemaphoreType.DMA((7,)), pltpu.SemaphoreType.DMA((7,)), pltpu.SemaphoreType.DMA],
        name=name)(block, *extra)


HBM = pl.BlockSpec(memory_space=pltpu.HBM)
SEM = pl.BlockSpec(memory_space=pltpu.SEMAPHORE)
EFFECT = pltpu.SideEffectType.DATAFLOW_SIDE_EFFECTING


def _peer_copies(src_ref, land_ref, send_sems, recv_sems, gather):
    x, y, c = lax.axis_index("x"), lax.axis_index("y"), lax.axis_index("c")
    me = 4 * x + 2 * y + c
    copies = []
    for k in range(1, N_DEV):
        px = 1 - x if k & 4 else x
        py = 1 - y if k & 2 else y
        pc = 1 - c if k & 1 else c
        copies.append(pltpu.make_async_remote_copy(
            src_ref=src_ref if gather else src_ref.at[4 * px + 2 * py + pc],
            dst_ref=land_ref.at[me] if gather else land_ref.at[k],
            send_sem=send_sems.at[k - 1], recv_sem=recv_sems.at[k - 1], device_id=(px, py, pc), device_id_type=MESH))
    return copies


def exchange_start(srcs, gather, name, after=None):
    n = len(srcs)
    land_shapes = [(N_DEV,) + src.shape if gather else src.shape for src in srcs]
    extra = () if after is None else (after,)

    def body(*refs):
        src_refs, land_refs = refs[0:n], refs[n:2 * n]
        outs = refs[2 * n + len(extra):]
        for k in range(n):
            for cp in _peer_copies(src_refs[k], land_refs[k], outs[4 * k], outs[4 * k + 1], gather):
                cp.start()
        token = outs[4 * n]
        token[...] = jnp.zeros_like(token)

    out_shape, out_specs, aliases = [], [], {}
    for k, src in enumerate(srcs):
        out_shape += [pltpu.SemaphoreType.DMA((N_DEV - 1,)), pltpu.SemaphoreType.DMA((N_DEV - 1,)),
                      pltpu.HBM(src.shape, src.dtype), pltpu.HBM(land_shapes[k], src.dtype)]
        out_specs += [SEM, SEM, HBM, HBM]
        aliases[k] = 4 * k + 2
        aliases[n + k] = 4 * k + 3
    out_shape.append(jax.ShapeDtypeStruct((8, 128), F32))
    out_specs.append(pl.BlockSpec(memory_space=pltpu.VMEM))
    res = pl.pallas_call(
        body, name=name, out_shape=tuple(out_shape), in_specs=(HBM,) * (2 * n) + (ANY,) * len(extra),
        out_specs=tuple(out_specs), input_output_aliases=aliases,
        compiler_params=pltpu.CompilerParams(has_side_effects=EFFECT),
    )(*[pltpu.with_memory_space_constraint(src, pltpu.HBM) for src in srcs],
      *[pltpu.with_memory_space_constraint(lax.empty(shp, src.dtype), pltpu.HBM) for shp, src in zip(land_shapes, srcs)],
      *extra)
    return [tuple(res[4 * k:4 * k + 4]) for k in range(n)], res[4 * n]


def _stage1_peer(i):
    x, y, c = lax.axis_index("x"), lax.axis_index("y"), lax.axis_index("c")
    if i == 0:
        return (x, y, 1 - c)
    return (1 - x if i & 1 else x, 1 - y if i & 2 else y, c)


def _slot_of(peer):
    return 4 * peer[0] + 2 * peer[1] + peer[2]


def _stage1_copy(i, src_ref, land_ref, send_sems, recv_sems):
    me = _slot_of((lax.axis_index("x"), lax.axis_index("y"), lax.axis_index("c")))
    return pltpu.make_async_remote_copy(src_ref=src_ref, dst_ref=land_ref.at[me], send_sem=send_sems.at[i],
                                        recv_sem=recv_sems.at[i], device_id=_stage1_peer(i), device_id_type=MESH)


def _stage2_copy(j, land_ref, send_sems, recv_sems):
    slot = _slot_of(_stage1_peer(j + 1))
    return pltpu.make_async_remote_copy(src_ref=land_ref.at[slot], dst_ref=land_ref.at[slot], send_sem=send_sems.at[j],
                                        recv_sem=recv_sems.at[j], device_id=_stage1_peer(0), device_id_type=MESH)


def gather2_start(srcs, name, after=None):
    n = len(srcs)
    extra = () if after is None else (after,)

    def body(*refs):
        src_refs, land_refs = refs[0:n], refs[n:2 * n]
        outs = refs[2 * n + len(extra):]
        for k in range(n):
            for i in range(4):
                _stage1_copy(i, src_refs[k], land_refs[k], outs[4 * k], outs[4 * k + 1]).start()
        outs[4 * n][...] = jnp.zeros((8, 128), F32)

    out_shape, out_specs, aliases = [], [], {}
    for k, src in enumerate(srcs):
        out_shape += [pltpu.SemaphoreType.DMA((4,)), pltpu.SemaphoreType.DMA((4,)),
                      pltpu.HBM(src.shape, src.dtype), pltpu.HBM((N_DEV,) + src.shape, src.dtype)]
        out_specs += [SEM, SEM, HBM, HBM]
        aliases[k] = 4 * k + 2
        aliases[n + k] = 4 * k + 3
    out_shape.append(jax.ShapeDtypeStruct((8, 128), F32))
    out_specs.append(pl.BlockSpec(memory_space=pltpu.VMEM))
    res = pl.pallas_call(
        body, name=name, out_shape=tuple(out_shape), in_specs=(HBM,) * (2 * n) + (ANY,) * len(extra),
        out_specs=tuple(out_specs), input_output_aliases=aliases,
        compiler_params=pltpu.CompilerParams(has_side_effects=EFFECT),
    )(*[pltpu.with_memory_space_constraint(src, pltpu.HBM) for src in srcs],
      *[pltpu.with_memory_space_constraint(lax.empty((N_DEV,) + src.shape, src.dtype), pltpu.HBM) for src in srcs], *extra)
    return [dict(send1=res[4 * k], recv1=res[4 * k + 1], src=res[4 * k + 2], land=res[4 * k + 3]) for k in range(n)], \
        res[4 * n]


def gather2_pass(handles, after, name):
    n = len(handles)

    def body(*refs):
        src_refs, land_refs, recv1 = refs[0:n], refs[n:2 * n], refs[2 * n:3 * n]
        outs = refs[3 * n + 1:]
        for k in range(n):
            for j in range(3):
                _stage1_copy(j + 1, src_refs[k], land_refs[k], recv1[k], recv1[k]).wait_recv()
                _stage2_copy(j, land_refs[k], outs[3 * k], outs[3 * k + 1]).start()
        outs[3 * n][...] = jnp.zeros((8, 128), F32)

    out_shape, out_specs, aliases = [], [], {}
    for k, h in enumerate(handles):
        out_shape += [pltpu.SemaphoreType.DMA((3,)), pltpu.SemaphoreType.DMA((3,)), pltpu.HBM(h["land"].shape, h["land"].dtype)]
        out_specs += [SEM, SEM, HBM]
        aliases[n + k] = 3 * k + 2
    out_shape.append(jax.ShapeDtypeStruct((8, 128), F32))
    out_specs.append(pl.BlockSpec(memory_space=pltpu.VMEM))
    res = pl.pallas_call(
        body, name=name, out_shape=tuple(out_shape), in_specs=(HBM,) * (2 * n) + (SEM,) * n + (ANY,),
        out_specs=tuple(out_specs), input_output_aliases=aliases,
        compiler_params=pltpu.CompilerParams(has_side_effects=EFFECT),
    )(*[h["src"] for h in handles], *[h["land"] for h in handles], *[h["recv1"] for h in handles], after)
    return [dict(h, send2=res[3 * k], recv2=res[3 * k + 1], land=res[3 * k + 2]) for k, h in enumerate(handles)], res[3 * n]


def gather2_wait(h, after, name):
    def body(src_ref, land_ref, send1, recv1, send2, recv2, after_ref, src_dead, got_ref):
        for i in range(4):
            _stage1_copy(i, src_ref, land_ref, send1, recv1).wait_send()
        _stage1_copy(0, src_ref, land_ref, send1, recv1).wait_recv()
        for j in range(3):
            cp = _stage2_copy(j, land_ref, send2, recv2)
            cp.wait_send()
            cp.wait_recv()

    return pl.pallas_call(
        body, name=name,
        out_shape=(pltpu.HBM(h["src"].shape, h["src"].dtype), pltpu.HBM(h["land"].shape, h["land"].dtype)),
        in_specs=(HBM, HBM, SEM, SEM, SEM, SEM, ANY), out_specs=(HBM, HBM), input_output_aliases={0: 0, 1: 1},
        compiler_params=pltpu.CompilerParams(has_side_effects=EFFECT),
    )(h["src"], h["land"], h["send1"], h["recv1"], h["send2"], h["recv2"], after)[1]


def exchange_wait(handles, after, gather, name):
    send_sems, recv_sems, src_thru, land_thru = handles

    def body(src_ref, land_ref, send_sems, recv_sems, after_ref, src_dead, got_ref):
        for cp in _peer_copies(src_ref, land_ref, send_sems, recv_sems, gather):
            cp.wait_send()
            cp.wait_recv()

    return pl.pallas_call(
        body, name=name,
        out_shape=(pltpu.HBM(src_thru.shape, src_thru.dtype), pltpu.HBM(land_thru.shape, land_thru.dtype)),
        in_specs=(HBM, HBM, SEM, SEM, ANY), out_specs=(HBM, HBM), input_output_aliases={0: 0, 1: 1},
        compiler_params=pltpu.CompilerParams(has_side_effects=EFFECT),
    )(src_thru, land_thru, send_sems, recv_sems, after)


def local_step(x, tgt, mod, started, get_w, put_grad, wc, wf, g_mix, bc, lg, lb, gco, gao, g_ffn, bf, g_fin):
    w_in = get_w("w_in", mod)
    proj, h1 = rms_mod_matmul(x, g_mix, mod, 0, 1, w_in, D_IN // N_DEV, "proj_fwd", after=started)
    mix_a, u1 = conv_module_fwd(proj, wc, bc, lg, lb, gco, "conv_module_fwd")
    att, lse = attn_fwd_all(proj, "attn_fwd")
    w_out = get_w("w_out", att)
    y1, mixed = norm_concat_matmul(mix_a, att, gao, w_out, "out_proj_fwd")
    w_up = get_w("w_up", y1)
    up0, x1, h2 = resid_rms_mod_matmul(x, y1, g_ffn, mod, 2, 3, 4, w_up, "up_fwd")
    passed = get_w("w_down", up0, only_pass_on=True)
    act = ffn_act_fwd(up0, wf, bf, "ffn_act_fwd", after=passed)
    w_down = get_w("w_down", act)
    loss_t, dx2, dy2, d_gfin, d_gaf = matmul_loss_bwd(act, w_down, x1, tgt, g_fin, mod, 5, "down_fwd_loss")
    dact = matmul(dy2, w_down, "nt", F32, 512, FFN_TN, "down_bwd_x")
    dw_down = matmul(act, dy2, "tn", BF16, 256, D_MODEL, "down_bwd_w")
    dup0, dbf_g, dbf_v, dwf_g, dwf_v = ffn_bwd(up0, dact, wf, bf, "ffn_bwd", after=put_grad("w_down", dw_down))
    dw_up = matmul_tn_halves(dup0, h2, 256, "up_bwd_w")
    dx1, d_shf, d_scf, d_gffn, dy1, d_gam = matmul_rms_mod_bwd(
        dup0, w_up, x1, dx2, g_ffn, mod, 4, y1, 2, "up_bwd_x", after=put_grad("w_up", dw_up))
    dw_out = matmul(mixed, dy1, "tn", BF16, 256, D_MODEL, "out_proj_bwd_w")
    dmixed, do, dd, d_gao = matmul_combine_bwd(dy1, w_out, att, gao, "out_proj_bwd_x", after=put_grad("w_out", dw_out))
    dqkv = attn_bwd_all(proj, do, lse, dd, "attn_bwd")
    du1, d_gco, d_lg, d_lb, d_bc, d_wc = conv_module_bwd_a(proj, u1, dmixed, lg, lb, gco, "conv_module_bwd_a")
    dproj_a = conv_module_bwd_b(proj, du1, wc, "conv_module_bwd_b")
    dw_in = matmul_tn_pieces(dproj_a, dqkv, h1, "proj_bwd_w")
    dx, d_shm, d_scm, d_gmix = matmul_rms_mod_bwd(
        (dproj_a, dqkv), w_in, x, dx1, g_mix, mod, 1, None, 0, "proj_bwd_x", b_rows=D_IN // N_DEV,
        after=put_grad("w_in", dw_in))
    dmod = jnp.concatenate([d_shm, d_scm, d_gam, d_shf, d_scf, d_gaf], axis=1)
    small = dict(g_norm_mix=d_gmix, b_conv_dw=d_bc, ln_conv_g=d_lg, ln_conv_b=d_lb, g_conv_out=d_gco, g_attn_out=d_gao,
                 g_norm_ffn=d_gffn, b_ffn_dw=jnp.concatenate([dbf_g, dbf_v], axis=1), g_final=d_gfin,
                 w_conv_dw=d_wc, w_ffn_dw=jnp.concatenate([dwf_g, dwf_v], axis=1), dmod=dmod, loss=loss_t[0:1, 0:1])
    return dx, small


PACK_W = 7168
TAPS_PER_ROW = PACK_W // D_CONV
PACKED_AT = dict(
    b_ada=(0, 0, N_MOD * D_MODEL), g_norm_mix=(0, 6144, D_MODEL),
    b_ffn_dw=(1, 0, 2 * D_FF), g_norm_ffn=(1, 5632, D_MODEL), b_conv_dw=(1, 6656, D_CONV),
    g_final=(2, 5632, D_MODEL), ln_conv_g=(2, 6656, D_CONV),
    ln_conv_b=(3, 5632, D_CONV), g_conv_out=(3, 6144, D_CONV), g_attn_out=(3, 6656, D_ATTN))
SMALL_ORDER = list(PACKED_AT)
LOSS_AT = (4, 2 * D_FF)


def pack_small(t):
    cat = lambda *parts: jnp.concatenate(parts, axis=1)
    wf = t["w_ffn_dw"]
    taps = jnp.pad(t["w_conv_dw"].reshape(1, CONV_K * D_CONV), ((0, 0), (0, 3 * PACK_W - CONV_K * D_CONV)))
    return jnp.concatenate([
        cat(t["dmod"], t["g_norm_mix"]),
        cat(t["b_ffn_dw"], t["g_norm_ffn"], t["b_conv_dw"]),
        cat(wf[0:1], t["g_final"], t["ln_conv_g"]),
        cat(wf[1:2], t["ln_conv_b"], t["g_conv_out"], t["g_attn_out"]),
        cat(wf[2:3], jnp.pad(t["loss"], ((0, 0), (0, PACK_W - 2 * D_FF - 1)))),
        taps.reshape(3, PACK_W)], axis=0)


def small_adamw(parts, wmv, name):
    def body(*refs):
        p_ref = refs[0]
        ins = refs[1:1 + 3 * len(SMALL_ORDER)]
        outs = refs[1 + 3 * len(SMALL_ORDER):]
        g = p_ref[0]
        for k in range(1, N_DEV):
            g = g + p_ref[k]
        for i, n in enumerate(SMALL_ORDER):
            row, lane, width = PACKED_AT[n]
            gp = g[row:row + 1, lane:lane + width]
            w_ref, m_ref, v_ref = ins[3 * i:3 * i + 3]
            g_ref, d_ref, nm_ref, nv_ref = outs[4 * i:4 * i + 4]
            g_ref[...] = gp
            d_ref[...], nm_ref[...], nv_ref[...] = _adam(w_ref[...], gp, m_ref[...], v_ref[...])
        wc_ref, wf_ref, loss_ref = outs[4 * len(SMALL_ORDER):]
        for j in range(CONV_K):
            row, lane = 5 + j // TAPS_PER_ROW, (j % TAPS_PER_ROW) * D_CONV
            wc_ref[j:j + 1, :] = g[row:row + 1, lane:lane + D_CONV]
        wf_ref[...] = g[2:2 + FFN_K, 0:2 * D_FF]
        loss_ref[...] = jnp.broadcast_to(g[LOSS_AT[0]:LOSS_AT[0] + 1, LOSS_AT[1]:LOSS_AT[1] + 1], (8, 128))

    args, out_shape = [parts], []
    for n in SMALL_ORDER:
        args += list(wmv[n])
        out_shape += [jax.ShapeDtypeStruct(wmv[n][0].shape, F32)] * 4
    out_shape += [jax.ShapeDtypeStruct((CONV_K, D_CONV), F32), jax.ShapeDtypeStruct((FFN_K, 2 * D_FF), F32),
                  jax.ShapeDtypeStruct((8, 128), F32)]
    res = pl.pallas_call(body, out_shape=tuple(out_shape), name=name, compiler_params=_cp())(*args)
    per = {n: tuple(res[4 * i:4 * i + 4]) for i, n in enumerate(SMALL_ORDER)}
    return per, res[-3], res[-2], res[-1][0, 0]


def shard_adamw(items, name):
    def body(*refs):
        ins, outs = refs[:4 * len(items)], refs[4 * len(items):]
        for i in range(len(items)):
            g_ref, w_ref, m_ref, v_ref = ins[4 * i:4 * i + 4]
            og_ref, d_ref, nm_ref, nv_ref = outs[4 * i:4 * i + 4]
            og_ref[...] = g_ref[...]
            d_ref[...], nm_ref[...], nv_ref[...] = _adam(w_ref[...], g_ref[...], m_ref[...], v_ref[...])

    args = [a for item in items for a in item]
    out_shape = tuple(jax.ShapeDtypeStruct(item[1].shape, F32) for item in items for _ in range(4))
    res = pl.pallas_call(body, out_shape=out_shape, name=name, compiler_params=_cp())(*args)
    return [tuple(res[4 * i:4 * i + 4]) for i in range(len(items))]


def _shard(full, n_cols, me):
    return lax.dynamic_slice(full, (0, me * n_cols), (full.shape[0], n_cols))


WEIGHTS = ["w_ada", "b_ada", "g_norm_mix", "w_in", "w_conv_dw", "b_conv_dw", "ln_conv_g", "ln_conv_b", "g_conv_out",
           "g_attn_out", "w_out", "g_norm_ffn", "w_up", "w_ffn_dw", "b_ffn_dw", "w_down", "g_final"]


def kernel(x, c, w_ada, b_ada, g_norm_mix, w_in, w_conv_dw, b_conv_dw, ln_conv_g, ln_conv_b, g_conv_out, g_attn_out, w_out, g_norm_ffn, w_up, w_ffn_dw, b_ffn_dw, w_down, g_final, loss_target, m_w_ada, m_b_ada, m_g_norm_mix, m_w_in, m_w_conv_dw, m_b_conv_dw, m_ln_conv_g, m_ln_conv_b, m_g_conv_out, m_g_attn_out, m_w_out, m_g_norm_ffn, m_w_up, m_w_ffn_dw, m_b_ffn_dw, m_w_down, m_g_final, v_w_ada, v_b_ada, v_g_norm_mix, v_w_in, v_w_conv_dw, v_b_conv_dw, v_ln_conv_g, v_ln_conv_b, v_g_conv_out, v_g_attn_out, v_w_out, v_g_norm_ffn, v_w_up, v_w_ffn_dw, v_b_ffn_dw, v_w_down, v_g_final):
    args = dict(locals())
    me = 4 * lax.axis_index("x") + 2 * lax.axis_index("y") + lax.axis_index("c")
    me1 = me.astype(jnp.int32).reshape(1)

    def flat(name, prefix=""):
        a = args[prefix + name]
        return a.reshape(a.shape[-2] if a.ndim > 1 else 1, a.shape[-1])

    def flat_t(name, prefix=""):
        return args[prefix + name][0].T

    n_in, n_up, r_out, r_down = w_in.shape[2], w_up.shape[2], w_out.shape[1], w_down.shape[1]
    n_ada, n_wc, n_wf = w_ada.shape[2], w_conv_dw.shape[2], w_ffn_dw.shape[2]
    taps_c = jnp.pad(flat("w_conv_dw").reshape(1, CONV_K * n_wc), ((0, 0), (0, 2 * D_MODEL - CONV_K * n_wc)))
    taps_f = jnp.pad(flat("w_ffn_dw").reshape(1, FFN_K * n_wf), ((0, 0), (0, 3 * D_MODEL - FFN_K * n_wf)))
    first = jnp.concatenate([c, taps_c.reshape(2, D_MODEL), taps_f.reshape(3, D_MODEL), jnp.zeros((2, D_MODEL), F32)], axis=0)
    w_in_block = flat_t("w_in").astype(BF16)
    hi = lax.reduce_precision(first, 8, 7)
    mid = lax.reduce_precision(first - hi, 8, 7)
    low = lax.reduce_precision(first - hi - mid, 8, 7)
    terms = jnp.concatenate([hi, mid, low, jnp.zeros((8, D_MODEL), F32)], axis=0).astype(BF16)
    first_block = all_gather(jnp.concatenate([w_in_block, terms], axis=0), "gather_c_taps_w_in")
    terms = first_block[:, n_in:n_in + 24, :].astype(F32)
    first_all = (terms[:, 0:8] + terms[:, 8:16]) + terms[:, 16:24]
    c_all = first_all[:, 0, :]
    wc_full = first_all[:, 1:3, :].reshape(N_DEV, 2 * D_MODEL)[:, :CONV_K * n_wc].reshape(N_DEV, CONV_K, n_wc)
    wc_full = wc_full.transpose(1, 0, 2).reshape(CONV_K, D_CONV)
    wf_full = first_all[:, 3:6, :].reshape(N_DEV, 3 * D_MODEL)[:, :FFN_K * n_wf].reshape(N_DEV, FFN_K, n_wf)
    wf_full = wf_full.transpose(1, 0, 2).reshape(FFN_K, 2 * D_FF)
    mod_cols = ada_fwd(c_all, flat("w_ada"), _shard(flat("b_ada"), n_ada, me), "ada_fwd")
    mod_all = all_gather(mod_cols, "gather_mod")
    mod = lax.dynamic_index_in_dim(mod_all, me, axis=1, keepdims=False).reshape(N_MOD, D_MODEL)
    mod = jnp.pad(mod, ((0, 2), (0, 0)))

    order = ("w_out", "w_up", "w_down")
    blocks = dict(w_up=flat_t("w_up").astype(BF16), w_out=flat("w_out").astype(BF16), w_down=flat("w_down").astype(BF16))
    handles, tok = gather2_start([blocks[name] for name in order], "gather_weights_start", mod_all)
    gathers = dict(zip(order, handles))

    def pass_on(name, after):
        if "send2" not in gathers[name]:
            group = ("w_out", "w_up") if name != "w_down" else ("w_down",)
            passed, token = gather2_pass([gathers[w] for w in group], after, f"gather_{name}_pass")
            gathers.update(zip(group, passed))
            return token

    def gathered(name, after):
        pass_on(name, after)
        land = gather2_wait(gathers[name], after, f"gather_{name}_wait")
        return lax.dynamic_update_index_in_dim(land, blocks[name], me, axis=0)

    def get_w(name, after, only_pass_on=False):
        if name == "w_in":
            return first_block
        if only_pass_on:
            return pass_on(name, after)
        return gathered(name, after).reshape(-1, D_MODEL)

    exchanges = {}

    def put_grad(name, dw, after=None):
        dev_major = dw.reshape(N_DEV, -1, D_MODEL)
        (exchanges[name],), token = exchange_start([dev_major], False, f"exchange_{name}_start", after)
        return token

    grad_x, small = local_step(
        x[0], loss_target[0], mod, tok, get_w, put_grad, wc_full, wf_full,
        flat("g_norm_mix"), flat("b_conv_dw"), flat("ln_conv_g"), flat("ln_conv_b"), flat("g_conv_out"),
        flat("g_attn_out"), flat("g_norm_ffn"), flat("b_ffn_dw"), flat("g_final"))

    out = {}

    def finish(name, tr, after, then=None):
        mine, parts = exchange_wait(exchanges[name], after, False, f"exchange_{name}_wait")
        if then is not None:
            then(parts)
        if name in ("w_in", "w_up"):
            res = sum_adamw(parts, mine, me1, flat_t(name), flat_t(name, "m_"), flat_t(name, "v_"), tr, "adamw_" + name)
            out[name] = tuple(r.T for r in res)
        else:
            res = out[name] = sum_adamw(parts, mine, me1, flat(name), flat(name, "m_"), flat(name, "v_"), tr,
                                        "adamw_" + name)
        return res[0]

    after = finish("w_down", r_down, grad_x)
    after = finish("w_up", n_up // 2, after)
    after = finish("w_out", r_out, after)
    packed = pack_small(small)
    small_gather = []
    after = finish("w_in", n_in, after, then=lambda parts: small_gather.append(
        gather2_start([packed], "gather_small_start", parts)[0][0]))
    (handle,), _ = gather2_pass(small_gather, after, "gather_small_pass")
    small_all = lax.dynamic_update_index_in_dim(
        gather2_wait(handle, after, "gather_small_wait"), packed, me, axis=0)

    wmv = {n: (flat(n), flat(n, "m_"), flat(n, "v_")) for n in SMALL_ORDER}
    per, g_wc, g_wf, loss = small_adamw(small_all, wmv, "adamw_small")
    out.update(per)
    taps = shard_adamw([(_shard(g_wc, n_wc, me), flat("w_conv_dw"), flat("w_conv_dw", "m_"), flat("w_conv_dw", "v_")),
                        (_shard(g_wf, n_wf, me), flat("w_ffn_dw"), flat("w_ffn_dw", "m_"), flat("w_ffn_dw", "v_"))],
                       "adamw_taps")
    out["w_conv_dw"], out["w_ffn_dw"] = taps

    dmod_cols = _shard(small_all[:, 0, :], n_ada, me)
    out["w_ada"] = ada_bwd_adamw(c_all, dmod_cols, flat("w_ada"), flat("w_ada", "m_"), flat("w_ada", "v_"), "adamw_w_ada")

    result = [loss, grad_x[None]]
    for k in range(4):
        result += [out[n][k].reshape(args[n].shape) for n in WEIGHTS]
    return tuple(result)
```

```python
import jax
import jax.numpy as jnp
from jax import lax
from jax.experimental import pallas as pl
from jax.experimental.pallas import tpu as pltpu

F32 = jnp.float32
BF16 = jnp.bfloat16

N_DEV = 8
SEQ = 2048
D_MODEL = 1024
D_CONV = 512
D_ATTN = 512
HEAD_DIM = 64
CONV_K = 31
D_FF = 2816
FFN_K = 3
D_IN = 2 * D_CONV + 3 * D_ATTN
N_MOD = 6
EPS = 1e-6
ATTN_BLOCK = 128
PATTERNS = ((2048, 1), (512, 4), (128, 16))
NEG = -1e30

ADAM_LR, ADAM_B1, ADAM_B2, ADAM_EPS, ADAM_WD, ADAM_STEP = 0.001, 0.9, 0.999, 1e-08, 0.01, 10

ROWS = 256
CONV_HALO = 32
FFN_HALO = 8
FFN_TN = 1408
VMEM_LIMIT = 56 * 1024 * 1024


NT = (((1,), (1,)), ((), ()))
ANY = pl.BlockSpec(memory_space=pl.ANY)


def _cp(*sem):
    return pltpu.CompilerParams(dimension_semantics=sem if sem else None, vmem_limit_bytes=VMEM_LIMIT)


def _with_after(body, n_in, after):
    if after is None:
        return body, [], []
    return (lambda *refs: body(*refs[:n_in], *refs[n_in + 1:])), [ANY], [after]


def _sig(x):
    return 1.0 / (1.0 + jnp.exp(-x))


def _rsum(x):
    return jnp.sum(x, axis=0, keepdims=True)


def _mean(x):
    return jnp.mean(x, axis=-1, keepdims=True)


def _acc(ref, val, first):
    @pl.when(first)
    def _():
        ref[...] = val

    @pl.when(jnp.logical_not(first))
    def _():
        ref[...] += val


SUB = 32


def _for_chunks(fn, unroll=1, rows=ROWS):
    def step(i, carry):
        fn(pl.ds(pl.multiple_of(i * SUB, SUB), SUB))
        return carry

    lax.fori_loop(0, rows // SUB, step, 0, unroll=unroll)


PAIR = 2 * ROWS


def _pair_spec(width, col=0):
    return pl.BlockSpec((PAIR, width), lambda i: (i, col))


def _halves():
    return [slice(h * ROWS, (h + 1) * ROWS) for h in range(2)]


def rms_mod_matmul(x, g, mod, sh_row, sc_row, b, b_rows, name, after=None):
    n = N_DEV * b_rows

    def body(x_ref, g_ref, mod_ref, b_ref, o_ref, h_ref):
        for rs in _halves():
            xx = x_ref[rs, :]
            r = lax.rsqrt(_mean(xx * xx) + EPS)
            h = (xx * r * g_ref[...] * (1.0 + mod_ref[sc_row:sc_row + 1, :]) + mod_ref[sh_row:sh_row + 1, :]).astype(BF16)
            h_ref[rs, :] = h
            o_ref[rs, :] = lax.dot_general(h, b_ref[...].reshape(n, D_MODEL), NT, preferred_element_type=F32)

    body, more_specs, more = _with_after(body, 4, after)
    return pl.pallas_call(
        body, out_shape=(jax.ShapeDtypeStruct((SEQ, n), F32), jax.ShapeDtypeStruct((SEQ, D_MODEL), BF16)),
        grid=(SEQ // PAIR,),
        in_specs=[_pair_spec(D_MODEL), _vec_spec(D_MODEL), _vec_spec(D_MODEL, 8),
                  pl.BlockSpec((N_DEV, b_rows, D_MODEL), lambda i: (0, 0, 0))] + more_specs,
        out_specs=(_pair_spec(n), _pair_spec(D_MODEL)), name=name, compiler_params=_cp("parallel"))(x, g, mod, b, *more)


def norm_concat_matmul(mix_a, att, gao, w, name):
    def body(a_ref, att_ref, g_ref, w_ref, y_ref, mixed_ref):
        for rs in _halves():
            aa = att_ref[rs, :]
            mixed_ref[rs, 0:D_CONV] = a_ref[rs, :]
            mixed_ref[rs, D_CONV:] = (aa * lax.rsqrt(_mean(aa * aa) + EPS) * g_ref[...]).astype(BF16)
            y_ref[rs, :] = jnp.dot(mixed_ref[rs, :], w_ref[...], preferred_element_type=F32)

    return pl.pallas_call(
        body, out_shape=(jax.ShapeDtypeStruct((SEQ, D_MODEL), F32), jax.ShapeDtypeStruct((SEQ, D_MODEL), BF16)),
        grid=(SEQ // PAIR,),
        in_specs=[_pair_spec(D_CONV), _pair_spec(D_ATTN), _vec_spec(D_ATTN), pl.BlockSpec(w.shape, lambda i: (0, 0))],
        out_specs=(_pair_spec(D_MODEL), _pair_spec(D_MODEL)), name=name, compiler_params=_cp("parallel"))(mix_a, att, gao, w)


def resid_rms_mod_matmul(x, y, g, mod, ga_row, sh_row, sc_row, w, name):
    n = w.shape[0]

    def body(x_ref, y_ref, g_ref, mod_ref, w_ref, o_ref, x1_ref, h_ref):
        x1 = x_ref[...] + mod_ref[ga_row:ga_row + 1, :] * y_ref[...]
        x1_ref[...] = x1
        r = lax.rsqrt(_mean(x1 * x1) + EPS)
        h = (x1 * r * g_ref[...] * (1.0 + mod_ref[sc_row:sc_row + 1, :]) + mod_ref[sh_row:sh_row + 1, :]).astype(BF16)
        h_ref[...] = h
        o_ref[...] = lax.dot_general(h, w_ref[...], NT, preferred_element_type=F32)

    return pl.pallas_call(
        body,
        out_shape=(jax.ShapeDtypeStruct((SEQ, n), F32), jax.ShapeDtypeStruct((SEQ, D_MODEL), F32),
                   jax.ShapeDtypeStruct((SEQ, D_MODEL), BF16)),
        grid=(SEQ // ROWS,),
        in_specs=[_row_spec(D_MODEL), _row_spec(D_MODEL), _vec_spec(D_MODEL), _vec_spec(D_MODEL, 8),
                  pl.BlockSpec(w.shape, lambda i: (0, 0))],
        out_specs=(_row_spec(n), _row_spec(D_MODEL), _row_spec(D_MODEL)),
        name=name, compiler_params=_cp("parallel"))(x, y, g, mod, w)


def matmul_combine_bwd(dy, w, att, gao, name, after=None):
    def body(dy_ref, w_ref, att_ref, g_ref, dm_ref, do_ref, dd_ref, dg_ref):
        @pl.when(pl.program_id(0) == 0)
        def _():
            dg_ref[...] = jnp.zeros_like(dg_ref)

        same_head = (jnp.right_shift(lax.broadcasted_iota(jnp.int32, (D_ATTN, D_ATTN), 0), 6)
                     == jnp.right_shift(lax.broadcasted_iota(jnp.int32, (D_ATTN, D_ATTN), 1), 6)).astype(F32)
        for rs in _halves():
            dmixed = lax.dot_general(dy_ref[rs, :], w_ref[...], NT, preferred_element_type=F32)
            dm_ref[rs, :] = dmixed
            att = att_ref[rs, :]
            r = lax.rsqrt(_mean(att * att) + EPS)
            xn = att * r
            dm = dmixed[:, D_CONV:]
            dg_ref[...] += _rsum(dm * xn)
            dyn = dm * g_ref[...]
            do = r * (dyn - xn * _mean(dyn * xn))
            do_ref[rs, :] = do
            dd_ref[rs, :] = jnp.dot(do * att, same_head, preferred_element_type=F32, precision=lax.Precision.HIGHEST)

    rs = _pair_spec(D_ATTN)
    f = jax.ShapeDtypeStruct((SEQ, D_ATTN), F32)
    body, more_specs, more = _with_after(body, 4, after)
    return pl.pallas_call(
        body, out_shape=(jax.ShapeDtypeStruct((SEQ, D_MODEL), F32), f, f, jax.ShapeDtypeStruct((1, D_ATTN), F32)),
        grid=(SEQ // PAIR,),
        in_specs=[_pair_spec(D_MODEL), pl.BlockSpec(w.shape, lambda i: (0, 0)), rs, _vec_spec(D_ATTN)] + more_specs,
        out_specs=(_pair_spec(D_MODEL), rs, rs, _vec_spec(D_ATTN)),
        name=name, compiler_params=_cp("arbitrary"))(dy, w, att, gao, *more)


def matmul_loss_bwd(act, w, x1, tgt, g, mod, ga_row, name):
    def body(a_ref, w_ref, x1_ref, t_ref, g_ref, mod_ref, loss_ref, dx2_ref, dy2_ref, dg_ref, dga_ref):
        @pl.when(pl.program_id(0) == 0)
        def _():
            loss_ref[...] = jnp.zeros_like(loss_ref)
            dg_ref[...] = jnp.zeros_like(dg_ref)
            dga_ref[...] = jnp.zeros_like(dga_ref)

        ga = mod_ref[ga_row:ga_row + 1, :]
        for rs in _halves():
            y2 = jnp.dot(a_ref[rs, :], w_ref[...], preferred_element_type=F32)
            x2 = x1_ref[rs, :] + ga * y2
            r = lax.rsqrt(_mean(x2 * x2) + EPS)
            xn = x2 * r
            err = xn * g_ref[...] - t_ref[rs, :]
            loss_ref[...] += jnp.broadcast_to(0.5 * jnp.sum(_mean(err * err)), (8, 128))
            dy = err * (1.0 / D_MODEL)
            dg_ref[...] += _rsum(dy * xn)
            dxn = dy * g_ref[...]
            dx2 = r * (dxn - xn * _mean(dxn * xn))
            dx2_ref[rs, :] = dx2
            dy2_ref[rs, :] = (dx2 * ga).astype(BF16)
            dga_ref[...] += _rsum(dx2 * y2)

    vec = jax.ShapeDtypeStruct((1, D_MODEL), F32)
    rows = _pair_spec
    return pl.pallas_call(
        body,
        out_shape=(jax.ShapeDtypeStruct((8, 128), F32), jax.ShapeDtypeStruct((SEQ, D_MODEL), F32),
                   jax.ShapeDtypeStruct((SEQ, D_MODEL), BF16), vec, vec),
        grid=(SEQ // PAIR,),
        in_specs=[rows(act.shape[1]), pl.BlockSpec(w.shape, lambda i: (0, 0)), rows(D_MODEL), rows(D_MODEL),
                  _vec_spec(D_MODEL), _vec_spec(D_MODEL, 8)],
        out_specs=(pl.BlockSpec((8, 128), lambda i: (0, 0)), rows(D_MODEL), rows(D_MODEL),
                   _vec_spec(D_MODEL), _vec_spec(D_MODEL)),
        name=name, compiler_params=_cp("arbitrary"))(act, w, x1, tgt, g, mod)


def matmul_rms_mod_bwd(a, b, x, dres, g, mod, sc_row, y, ga_row, name, b_rows=None, after=None):
    gated = y is not None
    pieces = isinstance(a, tuple)
    tm = PAIR if pieces else ROWS
    blocks = [slice(h * ROWS, (h + 1) * ROWS) for h in range(tm // ROWS)]
    rows = lambda width: pl.BlockSpec((tm, width), lambda i: (i, 0))
    if pieces:
        a1, a3 = a
        a_args = [a1, a3]
        a_specs = [rows(a1.shape[1]), pl.BlockSpec((3, tm, a3.shape[2]), lambda i: (0, i, 0))]
        b_arg, b_spec = b, pl.BlockSpec((N_DEV, b_rows, D_MODEL), lambda i: (0, 0, 0))
    else:
        k2 = a.shape[2]
        a_args = [a]
        a_specs = [pl.BlockSpec((2, tm, k2), lambda i: (0, i, 0))]
        b_arg, b_spec = b.reshape(2, k2, D_MODEL), pl.BlockSpec((2, k2, D_MODEL), lambda i: (0, 0, 0))
    n_a = len(a_args)

    def body(*refs):
        a_refs, (b_ref, x_ref, dres_ref, g_ref, mod_ref) = refs[:n_a], refs[n_a:n_a + 5]
        if gated:
            y_ref, dx_ref, dsh_ref, dsc_ref, dg_ref, dy_ref, dga_ref = refs[n_a + 5:]
        else:
            dx_ref, dsh_ref, dsc_ref, dg_ref = refs[n_a + 5:]

        @pl.when(pl.program_id(0) == 0)
        def _():
            for ref in (dsh_ref, dsc_ref, dg_ref) + ((dga_ref,) if gated else ()):
                ref[...] = jnp.zeros_like(ref)

        gg = g_ref[...]
        for rs in blocks:
            if pieces:
                wv = b_ref[...].reshape(N_DEV * b_rows, D_MODEL)
                k1, k3 = a_refs[0].shape[1], a_refs[1].shape[2]
                dh = jnp.dot(a_refs[0][rs, :], wv[0:k1], preferred_element_type=F32)
                for t in range(3):
                    dh = dh + jnp.dot(a_refs[1][t, rs, :], wv[k1 + t * k3:k1 + (t + 1) * k3], preferred_element_type=F32)
            else:
                dh = (jnp.dot(a_refs[0][0, rs, :], b_ref[0], preferred_element_type=F32)
                      + jnp.dot(a_refs[0][1, rs, :], b_ref[1], preferred_element_type=F32))
            xx = x_ref[rs, :]
            r = lax.rsqrt(_mean(xx * xx) + EPS)
            xn = xx * r
            dsh_ref[...] += _rsum(dh)
            dsc_ref[...] += _rsum(dh * (xn * gg))
            dt = dh * (1.0 + mod_ref[sc_row:sc_row + 1, :])
            dg_ref[...] += _rsum(dt * xn)
            dxn = dt * gg
            dx = dres_ref[rs, :] + r * (dxn - xn * _mean(dxn * xn))
            dx_ref[rs, :] = dx
            if gated:
                dga_ref[...] += _rsum(dx * y_ref[rs, :])
                dy_ref[rs, :] = (dx * mod_ref[ga_row:ga_row + 1, :]).astype(BF16)

    vec = jax.ShapeDtypeStruct((1, D_MODEL), F32)
    in_specs = a_specs + [b_spec, rows(D_MODEL), rows(D_MODEL), _vec_spec(D_MODEL), _vec_spec(D_MODEL, 8)]
    out_shape = [jax.ShapeDtypeStruct((SEQ, D_MODEL), F32), vec, vec, vec]
    out_specs = [rows(D_MODEL), _vec_spec(D_MODEL), _vec_spec(D_MODEL), _vec_spec(D_MODEL)]
    args = a_args + [b_arg, x, dres, g, mod]
    if gated:
        in_specs.append(rows(D_MODEL))
        out_shape += [jax.ShapeDtypeStruct((SEQ, D_MODEL), BF16), vec]
        out_specs += [rows(D_MODEL), _vec_spec(D_MODEL)]
        args.append(y)
    body, more_specs, more = _with_after(body, len(args), after)
    return pl.pallas_call(
        body, out_shape=tuple(out_shape), grid=(SEQ // tm,), in_specs=in_specs + more_specs, out_specs=tuple(out_specs),
        name=name, compiler_params=_cp("arbitrary"))(*args, *more)


def _prev_halo(halo, width, col):
    per = ROWS // halo
    return pl.BlockSpec((halo, width), lambda i: (jnp.maximum(i * per - 1, 0), col))


def _next_halo(halo, width, col):
    per = ROWS // halo
    last = SEQ // halo - 1
    return pl.BlockSpec((halo, width), lambda i: (jnp.minimum((i + 1) * per, last), col))


CONV_PAD = ROWS + CONV_HALO


def _shift_copies(sh):
    for b in range(1, 8):
        sh[b, 0:CONV_PAD - 8, :] = sh[0, pl.ds(b, CONV_PAD - 8), :]


def _tap(sh, rs_start, offset):
    return sh[offset % 8, pl.ds(pl.multiple_of(rs_start + (offset // 8) * 8, 8), SUB), :]


def _conv_glu(av_ref, ag_ref, avh_ref, agh_ref, sh):
    i = pl.program_id(0)
    hv = avh_ref[...] * _sig(agh_ref[...])
    sh[0, 0:CONV_HALO, :] = jnp.where(i > 0, hv, 0.0)

    def glu(rs):
        sh[0, pl.ds(pl.multiple_of(rs.start + CONV_HALO, SUB), SUB), :] = av_ref[rs, :] * _sig(ag_ref[rs, :])

    _for_chunks(glu)
    _shift_copies(sh)


def _conv_norm(u1, lg_ref, lb_ref):
    mu = _mean(u1)
    cen = u1 - mu
    rs = lax.rsqrt(_mean(cen * cen) + EPS)
    z = cen * rs
    ln = z * lg_ref[...] + lb_ref[...]
    s = _sig(ln)
    return z, rs, ln, s, ln * s


def conv_module_fwd(proj, wc, bc, lg, lb, gco, name):
    def body(av_ref, ag_ref, avh_ref, agh_ref, wc_ref, bc_ref, lg_ref, lb_ref, gco_ref, out_ref, u1_ref, sh):
        _conv_glu(av_ref, ag_ref, avh_ref, agh_ref, sh)

        def conv(rs):
            u1 = jnp.broadcast_to(bc_ref[...], (SUB, D_CONV))
            for j in range(CONV_K):
                u1 = u1 + wc_ref[j:j + 1, :] * _tap(sh, rs.start, CONV_HALO - (CONV_K - 1) + j)
            u1_ref[rs, :] = u1

        _for_chunks(conv)
        _, _, _, _, u2 = _conv_norm(u1_ref[...], lg_ref, lb_ref)
        rc = lax.rsqrt(_mean(u2 * u2) + EPS)
        out_ref[...] = (u2 * rc * gco_ref[...]).astype(BF16)

    v = _vec_spec(D_CONV)
    return pl.pallas_call(
        body, out_shape=(jax.ShapeDtypeStruct((SEQ, D_CONV), BF16), jax.ShapeDtypeStruct((SEQ, D_CONV), F32)),
        grid=(SEQ // ROWS,),
        in_specs=[_row_spec(D_CONV, 0), _row_spec(D_CONV, 1), _prev_halo(CONV_HALO, D_CONV, 0),
                  _prev_halo(CONV_HALO, D_CONV, 1), _vec_spec(D_CONV, CONV_K), v, v, v, v],
        out_specs=(_row_spec(D_CONV), _row_spec(D_CONV)), scratch_shapes=[pltpu.VMEM((8, CONV_PAD, D_CONV), F32)],
        name=name, compiler_params=_cp("parallel"))(proj, proj, proj, proj, wc, bc, lg, lb, gco)


def conv_module_bwd_a(proj, u1, dmixed, lg, lb, gco, name):
    def body(av_ref, ag_ref, avh_ref, agh_ref, u1_ref, dm_ref, lg_ref, lb_ref, gco_ref,
             du1_ref, dgco_ref, dlg_ref, dlb_ref, dbc_ref, dwc_ref, sh, acc):
        first = pl.program_id(0) == 0
        _conv_glu(av_ref, ag_ref, avh_ref, agh_ref, sh)
        z, rs, ln, s, u2 = _conv_norm(u1_ref[...], lg_ref, lb_ref)
        rc = lax.rsqrt(_mean(u2 * u2) + EPS)
        xn = u2 * rc
        dm = dm_ref[...]
        _acc(dgco_ref, _rsum(dm * xn), first)
        dyn = dm * gco_ref[...]
        du2 = rc * (dyn - xn * _mean(dyn * xn))
        dln = du2 * (s * (1.0 + ln * (1.0 - s)))
        _acc(dlg_ref, _rsum(dln * z), first)
        _acc(dlb_ref, _rsum(dln), first)
        dz = dln * lg_ref[...]
        du1 = rs * (dz - _mean(dz) - z * _mean(dz * z))
        du1_ref[...] = du1
        _acc(dbc_ref, _rsum(du1), first)
        acc[...] = jnp.zeros_like(acc)

        def taps(rs):
            d = du1_ref[rs, :]
            for j in range(CONV_K):
                acc[j] += d * _tap(sh, rs.start, CONV_HALO - (CONV_K - 1) + j)

        _for_chunks(taps)

        @pl.when(first)
        def _():
            dwc_ref[...] = jnp.zeros_like(dwc_ref)

        for j in range(CONV_K):
            dwc_ref[j:j + 1, :] += _rsum(acc[j])

    v = _vec_spec(D_CONV)
    vec = jax.ShapeDtypeStruct((1, D_CONV), F32)
    return pl.pallas_call(
        body,
        out_shape=(jax.ShapeDtypeStruct((SEQ, D_CONV), F32), vec, vec, vec, vec, jax.ShapeDtypeStruct((CONV_K, D_CONV), F32)),
        grid=(SEQ // ROWS,),
        in_specs=[_row_spec(D_CONV, 0), _row_spec(D_CONV, 1), _prev_halo(CONV_HALO, D_CONV, 0),
                  _prev_halo(CONV_HALO, D_CONV, 1), _row_spec(D_CONV, 0), _row_spec(D_CONV, 0), v, v, v],
        out_specs=(_row_spec(D_CONV), v, v, v, v, _vec_spec(D_CONV, CONV_K)),
        scratch_shapes=[pltpu.VMEM((8, CONV_PAD, D_CONV), F32), pltpu.VMEM((CONV_K, SUB, D_CONV), F32)],
        name=name, compiler_params=_cp("arbitrary"))(proj, proj, proj, proj, u1, dmixed, lg, lb, gco)


def conv_module_bwd_b(proj, du1, wc, name):
    def body(av_ref, ag_ref, du1_ref, du1n_ref, wc_ref, out_ref, sh):
        i = pl.program_id(0)
        sh[0, 0:ROWS, :] = du1_ref[...]
        sh[0, ROWS:, :] = jnp.where(i < SEQ // ROWS - 1, du1n_ref[...], 0.0)
        _shift_copies(sh)

        def chunk(rs):
            du0 = jnp.zeros((SUB, D_CONV), F32)
            for j in range(CONV_K):
                du0 = du0 + wc_ref[j:j + 1, :] * _tap(sh, rs.start, CONV_K - 1 - j)
            sg = _sig(ag_ref[rs, :])
            out_ref[rs, 0:D_CONV] = (du0 * sg).astype(BF16)
            out_ref[rs, D_CONV:] = (du0 * av_ref[rs, :] * sg * (1.0 - sg)).astype(BF16)

        _for_chunks(chunk)

    return pl.pallas_call(
        body, out_shape=jax.ShapeDtypeStruct((SEQ, 2 * D_CONV), BF16), grid=(SEQ // ROWS,),
        in_specs=[_row_spec(D_CONV, 0), _row_spec(D_CONV, 1), _row_spec(D_CONV, 0), _next_halo(CONV_HALO, D_CONV, 0),
                  _vec_spec(D_CONV, CONV_K)],
        out_specs=_row_spec(2 * D_CONV), scratch_shapes=[pltpu.VMEM((8, CONV_PAD, D_CONV), F32)],
        name=name, compiler_params=_cp("parallel"))(proj, proj, du1, du1, wc)


def _rows(start, size, r):
    return pl.ds(start, size) if r == 1 else pl.ds(start, size, stride=r)


def _unit_rows(r, rho, n, nb):
    win = 2 * ATTN_BLOCK if nb > 1 else ATTN_BLOCK
    if isinstance(n, int):
        kb = max(n - 1, 0)
        q_rows = _rows(rho + r * ATTN_BLOCK * n, ATTN_BLOCK, r)
        k_rows = _rows(rho + r * ATTN_BLOCK * kb, win, r)
    else:
        kb = jnp.maximum(n - 1, 0)
        q_rows = pl.ds(pl.multiple_of(n * ATTN_BLOCK, ATTN_BLOCK), ATTN_BLOCK)
        k_rows = pl.ds(pl.multiple_of(kb * ATTN_BLOCK, ATTN_BLOCK), win)
    return q_rows, k_rows, n - kb


def _band_bias(first_block, transposed):
    shape = (2 * ATTN_BLOCK, ATTN_BLOCK) if transposed else (ATTN_BLOCK, 2 * ATTN_BLOCK)
    q_axis = 1 if transposed else 0
    dist = (0 if first_block else ATTN_BLOCK) + lax.broadcasted_iota(jnp.int32, shape, q_axis) \
        - lax.broadcasted_iota(jnp.int32, shape, 1 - q_axis)
    return jnp.where((dist >= 0) & (dist <= ATTN_BLOCK), 0.0, NEG)


def _per_head(x):
    lane = lax.broadcasted_iota(jnp.int32, x.shape, 1)
    zero = jnp.zeros_like(x)
    return [jnp.where(lane < HEAD_DIM, x, zero), jnp.where(lane >= HEAD_DIM, x, zero)]


SCALE = HEAD_DIM ** -0.5


def _masked_scores(q2, k2, bias):
    return [lax.dot_general(qh, k2, NT, preferred_element_type=F32) + bias for qh in _per_head(q2)]


def _attn_units(r, nb, unit):
    if r == 1:
        def four(i, carry):
            for k in range(4):
                unit(0, 4 * i + k)
            return carry
        lax.fori_loop(0, nb // 4, four, 0)
    else:
        for rho in range(r):
            for n in range(nb):
                unit(rho, n)


N_UNITS = 16


def attn_fwd_all(proj, name):
    def body(q_ref, k_ref, v_ref, att_ref, lse_ref, s_scr, p_scr, lse_scr, den_scr, bias_scr):
        bias_scr[0] = _band_bias(True, False)
        bias_scr[1] = _band_bias(False, False)
        for idx, (sub_len, r) in enumerate(PATTERNS):
            nb = sub_len // ATTN_BLOCK
            win = 2 * ATTN_BLOCK if nb > 1 else ATTN_BLOCK

            def scores(rho, n, r=r, nb=nb, win=win):
                u = rho * nb + n
                q_rows, k_rows, variant = _unit_rows(r, rho, n, nb)
                ss = _masked_scores((q_ref[q_rows, :] * SCALE).astype(BF16), k_ref[k_rows, :].astype(BF16),
                                    bias_scr[variant, :, 0:win])
                for h in range(2):
                    s_scr[2 * u + h, :, 0:win] = ss[h]

            _attn_units(r, nb, scores)

            def softmax(u, carry, win=win):
                lses, dens = [], []
                for h in range(2):
                    sc = s_scr[2 * u + h, :, 0:win]
                    m = jnp.max(sc, axis=1, keepdims=True)
                    p = jnp.exp(sc - m)
                    den = jnp.sum(p, axis=1, keepdims=True)
                    p_scr[2 * u + h, :, 0:win] = p.astype(BF16)
                    lses.append(jnp.broadcast_to(m + jnp.log(den), (ATTN_BLOCK, HEAD_DIM)))
                    dens.append(jnp.broadcast_to(den, (ATTN_BLOCK, HEAD_DIM)))
                lse_scr[u] = jnp.concatenate(lses, axis=1)
                den_scr[u] = jnp.concatenate(dens, axis=1)
                return carry

            lax.fori_loop(0, N_UNITS, softmax, 0, unroll=2)

            def outputs(rho, n, r=r, nb=nb, win=win, idx=idx):
                u = rho * nb + n
                q_rows, k_rows, _ = _unit_rows(r, rho, n, nb)
                vs = _per_head(v_ref[k_rows, :].astype(BF16))
                o = (jnp.dot(p_scr[2 * u, :, 0:win], vs[0], preferred_element_type=F32)
                     + jnp.dot(p_scr[2 * u + 1, :, 0:win], vs[1], preferred_element_type=F32)) / den_scr[u]
                lse = lse_scr[u]
                if idx > 0:
                    old = lse_ref[q_rows, :]
                    top = jnp.maximum(old, lse)
                    new = top + jnp.log(jnp.exp(old - top) + jnp.exp(lse - top))
                    o = att_ref[q_rows, :] * jnp.exp(old - new) + o * jnp.exp(lse - new)
                    lse = new
                att_ref[q_rows, :] = o
                lse_ref[q_rows, :] = lse

            _attn_units(r, nb, outputs)

    blk = lambda first: pl.BlockSpec((SEQ, 128), lambda g: (0, first + g))
    shp = jax.ShapeDtypeStruct((SEQ, D_ATTN), F32)
    big = (2 * N_UNITS, ATTN_BLOCK, 2 * ATTN_BLOCK)
    small = pltpu.VMEM((N_UNITS, ATTN_BLOCK, 128), F32)
    return pl.pallas_call(
        body, out_shape=(shp, shp), grid=(4,), in_specs=[blk(8), blk(12), blk(16)], out_specs=(blk(0), blk(0)),
        scratch_shapes=[pltpu.VMEM(big, F32), pltpu.VMEM(big, BF16), small, small,
                        pltpu.VMEM((2, ATTN_BLOCK, 2 * ATTN_BLOCK), F32)],
        name=name, compiler_params=_cp("parallel"))(proj, proj, proj)


def attn_bwd_all(proj, do, lse, dd, name):
    def body(q_ref, k_ref, v_ref, do_ref, l_ref, dd_ref, out_ref, dq_s, dk_s, dv_s,
             s_scr, dp_scr, ds_scr, st_scr, dpt_scr, pt_scr, dst_scr, qb_scr, kb_scr, dob_scr, bias_scr, bias_t_scr):
        for first_block in (True, False):
            bias_scr[1 - int(first_block)] = _band_bias(first_block, False)
            bias_t_scr[1 - int(first_block)] = _band_bias(first_block, True)
        dq_s[...] = jnp.zeros_like(dq_s)
        dk_s[...] = jnp.zeros_like(dk_s)
        dv_s[...] = jnp.zeros_like(dv_s)
        for sub_len, r in PATTERNS:
            nb = sub_len // ATTN_BLOCK
            win = 2 * ATTN_BLOCK if nb > 1 else ATTN_BLOCK

            def scores(rho, n, r=r, nb=nb, win=win):
                u = rho * nb + n
                q_rows, k_rows, variant = _unit_rows(r, rho, n, nb)
                bias, bias_t = bias_scr[variant, :, 0:win], bias_t_scr[variant, 0:win, :]
                q2 = (q_ref[q_rows, :] * SCALE).astype(BF16)
                kf = k_ref[k_rows, :]
                k2 = kf.astype(BF16)
                do2 = do_ref[q_rows, :].astype(BF16)
                qb_scr[u] = q2
                kb_scr[u, 0:win, :] = (kf * SCALE).astype(BF16)
                dob_scr[u] = do2
                l2 = l_ref[q_rows, :]
                d2 = dd_ref[q_rows, :]
                l2t = l2.T
                d2t = d2.T
                v2 = v_ref[k_rows, :].astype(BF16)
                qs, dos = _per_head(q2), _per_head(do2)
                for h in range(2):
                    c0 = h * HEAD_DIM
                    sc = lax.dot_general(qs[h], k2, NT, preferred_element_type=F32)
                    s_scr[2 * u + h, :, 0:win] = sc + bias - l2[:, c0:c0 + 1]
                    dp_scr[2 * u + h, :, 0:win] = lax.dot_general(dos[h], v2, NT, preferred_element_type=F32) \
                        - d2[:, c0:c0 + 1]
                    sct = lax.dot_general(k2, qs[h], NT, preferred_element_type=F32)
                    st_scr[2 * u + h, 0:win, :] = sct + bias_t - l2t[c0:c0 + 1, :]
                    dpt_scr[2 * u + h, 0:win, :] = lax.dot_general(v2, dos[h], NT, preferred_element_type=F32) \
                        - d2t[c0:c0 + 1, :]

            _attn_units(r, nb, scores)

            def pointwise(hu, carry, win=win):
                ds_scr[hu, :, 0:win] = (jnp.exp(s_scr[hu, :, 0:win]) * dp_scr[hu, :, 0:win]).astype(BF16)
                pt = jnp.exp(st_scr[hu, 0:win, :])
                pt_scr[hu, 0:win, :] = pt.astype(BF16)
                dst_scr[hu, 0:win, :] = (pt * dpt_scr[hu, 0:win, :]).astype(BF16)
                return carry

            lax.fori_loop(0, 2 * N_UNITS, pointwise, 0, unroll=4)

            def grads(rho, n, r=r, nb=nb, win=win):
                u = rho * nb + n
                q_rows, k_rows, _ = _unit_rows(r, rho, n, nb)
                qs, ks, dos = _per_head(qb_scr[u]), _per_head(kb_scr[u, 0:win, :]), _per_head(dob_scr[u])

                def both(scr, rows, rhs):
                    return (jnp.dot(scr[(2 * u,) + rows], rhs[0], preferred_element_type=F32)
                            + jnp.dot(scr[(2 * u + 1,) + rows], rhs[1], preferred_element_type=F32))

                dq_s[q_rows, :] += both(ds_scr, (slice(None), slice(0, win)), ks)
                dk_s[k_rows, :] += both(dst_scr, (slice(0, win), slice(None)), qs)
                dv_s[k_rows, :] += both(pt_scr, (slice(0, win), slice(None)), dos)

            _attn_units(r, nb, grads)
        out_ref[0] = dq_s[...].astype(BF16)
        out_ref[1] = dk_s[...].astype(BF16)
        out_ref[2] = dv_s[...].astype(BF16)

    blk = lambda first: pl.BlockSpec((SEQ, 128), lambda g: (0, first + g))
    acc = pltpu.VMEM((SEQ, 128), F32)
    big = (2 * N_UNITS, ATTN_BLOCK, 2 * ATTN_BLOCK)
    big_t = (2 * N_UNITS, 2 * ATTN_BLOCK, ATTN_BLOCK)
    return pl.pallas_call(
        body, out_shape=jax.ShapeDtypeStruct((3, SEQ, D_ATTN), BF16), grid=(4,),
        in_specs=[blk(8), blk(12), blk(16), blk(0), blk(0), blk(0)],
        out_specs=pl.BlockSpec((3, SEQ, 128), lambda g: (0, 0, g)),
        scratch_shapes=[acc, acc, acc, pltpu.VMEM(big, F32), pltpu.VMEM(big, F32), pltpu.VMEM(big, BF16),
                        pltpu.VMEM(big_t, F32), pltpu.VMEM(big_t, F32), pltpu.VMEM(big_t, BF16), pltpu.VMEM(big_t, BF16),
                        pltpu.VMEM((N_UNITS, ATTN_BLOCK, 128), BF16),
                        pltpu.VMEM((N_UNITS, 2 * ATTN_BLOCK, 128), BF16), pltpu.VMEM((N_UNITS, ATTN_BLOCK, 128), BF16),
                        pltpu.VMEM((2, ATTN_BLOCK, 2 * ATTN_BLOCK), F32), pltpu.VMEM((2, 2 * ATTN_BLOCK, ATTN_BLOCK), F32)],
        name=name, compiler_params=_cp("parallel"))(proj, proj, proj, do, lse, dd)


N_FT = D_FF // FFN_TN


def _ffn_specs():
    per = ROWS // FFN_HALO
    cur_g = pl.BlockSpec((ROWS, FFN_TN), lambda j, i: (i, j))
    cur_v = pl.BlockSpec((ROWS, FFN_TN), lambda j, i: (i, j + N_FT))
    halo_g = pl.BlockSpec((FFN_HALO, FFN_TN), lambda j, i: (jnp.maximum(i * per - 1, 0), j))
    halo_v = pl.BlockSpec((FFN_HALO, FFN_TN), lambda j, i: (jnp.maximum(i * per - 1, 0), j + N_FT))
    w_g = pl.BlockSpec((FFN_K, FFN_TN), lambda j, i: (0, j))
    w_v = pl.BlockSpec((FFN_K, FFN_TN), lambda j, i: (0, j + N_FT))
    b_g = pl.BlockSpec((1, FFN_TN), lambda j, i: (0, j))
    b_v = pl.BlockSpec((1, FFN_TN), lambda j, i: (0, j + N_FT))
    return [cur_g, cur_v, halo_g, halo_v, w_g, w_v, b_g, b_v]


def matmul(a, b, kind, out_dtype, tm, tn, name, b_rows=None):
    stacked = b_rows is not None
    b_shape = (N_DEV * b_rows, b.shape[2]) if stacked else b.shape
    if kind == "nn":
        (m, k), n = a.shape, b_shape[1]
        a_spec = pl.BlockSpec((tm, k), lambda j, i: (i, 0))
        b_spec = pl.BlockSpec((k, tn), lambda j, i: (0, j))
        dims = (((1,), (0,)), ((), ()))
    elif kind == "nt":
        (m, k), n = a.shape, b_shape[0]
        a_spec = pl.BlockSpec((tm, k), lambda j, i: (i, 0))
        b_spec = pl.BlockSpec((tn, k), lambda j, i: (j, 0))
        dims = (((1,), (1,)), ((), ()))
    else:
        (k, m), n = a.shape, b_shape[1]
        a_spec = pl.BlockSpec((k, tm), lambda j, i: (0, i))
        b_spec = pl.BlockSpec((k, tn), lambda j, i: (0, j))
        dims = (((0,), (0,)), ((), ()))
    assert m % tm == 0 and n % tn == 0, (name, m, n, tm, tn)
    if stacked:
        assert b_spec.block_shape[0] == b_shape[0] and kind in ("nn", "nt")
        width = b_spec.block_shape[1]
        b_spec = pl.BlockSpec((N_DEV, b_rows, width), (lambda j, i: (0, 0, j)) if kind == "nn" else (lambda j, i: (0, 0, 0)))

    def body(a_ref, b_ref, o_ref):
        bb = b_ref[...].reshape(b_shape[0], -1) if stacked else b_ref[...]
        o_ref[...] = lax.dot_general(a_ref[...], bb, dims, preferred_element_type=F32).astype(o_ref.dtype)

    return pl.pallas_call(
        body, out_shape=jax.ShapeDtypeStruct((m, n), out_dtype), grid=(n // tn, m // tm),
        in_specs=[a_spec, b_spec], out_specs=pl.BlockSpec((tm, tn), lambda j, i: (i, j)),
        name=name, compiler_params=_cp("parallel", "parallel"))(a, b)


def matmul_tn_pieces(a1, a3, b, name):
    k, n = b.shape
    tm = a3.shape[2]
    n1 = a1.shape[1] // tm

    def body(a1_ref, a3_ref, b_ref, o_ref):
        i = pl.program_id(0)
        tn_dims = (((0,), (0,)), ((), ()))

        @pl.when(i < n1)
        def _():
            o_ref[...] = lax.dot_general(a1_ref[...], b_ref[...], tn_dims, preferred_element_type=F32).astype(BF16)

        @pl.when(i >= n1)
        def _():
            o_ref[...] = lax.dot_general(a3_ref[0], b_ref[...], tn_dims, preferred_element_type=F32).astype(BF16)

    return pl.pallas_call(
        body, out_shape=jax.ShapeDtypeStruct((a1.shape[1] + 3 * tm, n), BF16), grid=(n1 + 3,),
        in_specs=[pl.BlockSpec((k, tm), lambda i: (0, jnp.minimum(i, n1 - 1))),
                  pl.BlockSpec((1, k, tm), lambda i: (jnp.maximum(i - n1, 0), 0, 0)),
                  pl.BlockSpec((k, n), lambda i: (0, 0))],
        out_specs=pl.BlockSpec((tm, n), lambda i: (i, 0)), name=name, compiler_params=_cp("parallel"))(a1, a3, b)


def matmul_tn_halves(a, b, tm, name):
    _, k, m = a.shape
    n = b.shape[1]

    def body(a_ref, b_ref, o_ref):
        o_ref[0] = lax.dot_general(a_ref[0], b_ref[...], (((0,), (0,)), ((), ())), preferred_element_type=F32).astype(BF16)

    return pl.pallas_call(
        body, out_shape=jax.ShapeDtypeStruct((2, m, n), BF16), grid=(2, m // tm),
        in_specs=[pl.BlockSpec((1, k, tm), lambda h, i: (h, 0, i)), pl.BlockSpec((k, n), lambda h, i: (0, 0))],
        out_specs=pl.BlockSpec((1, tm, n), lambda h, i: (h, i, 0)), name=name,
        compiler_params=_cp("parallel", "parallel"))(a, b).reshape(2 * m, n)


def _row_spec(width, col=0):
    return pl.BlockSpec((ROWS, width), lambda i: (i, col))


def _vec_spec(width, rows=1):
    return pl.BlockSpec((rows, width), lambda i: (0, 0))


FFN_RB = 128


def _ffn_lane_blocks(fn):
    for c in range(FFN_TN // 128):
        fn(slice(c * 128, (c + 1) * 128))


def _ffn_taps(cur_ref, halo_ref, head, ls, r0):
    if r0 == 0:
        head[0:FFN_HALO, :] = jnp.where(pl.program_id(1) > 0, halo_ref[:, ls], 0.0)
        head[FFN_HALO:, :] = cur_ref[0:FFN_RB, ls]
        return tuple(head[pl.ds(FFN_HALO - k, FFN_RB), :] for k in (2, 1, 0))
    return tuple(cur_ref[pl.ds(r0 - k, FFN_RB), ls] for k in (2, 1, 0))


def _ffn_conv(taps, w, b):
    return b + w[0] * taps[0] + w[1] * taps[1] + w[2] * taps[2]


def ffn_act_fwd(up0, wf, bf, name, after=None):
    def body(g_ref, v_ref, gh_ref, vh_ref, wg_ref, wv_ref, bg_ref, bv_ref, act_ref, head_g, head_v):
        def lanes(ls):
            wg = [wg_ref[t:t + 1, ls] for t in range(FFN_K)]
            wv = [wv_ref[t:t + 1, ls] for t in range(FFN_K)]
            for r0 in range(0, ROWS, FFN_RB):
                gate = _ffn_conv(_ffn_taps(g_ref, gh_ref, head_g, ls, r0), wg, bg_ref[:, ls])
                val = _ffn_conv(_ffn_taps(v_ref, vh_ref, head_v, ls, r0), wv, bv_ref[:, ls])
                act_ref[r0:r0 + FFN_RB, ls] = (gate * _sig(gate) * val).astype(BF16)

        _ffn_lane_blocks(lanes)

    head = pltpu.VMEM((FFN_HALO + FFN_RB, 128), F32)
    body, more_specs, more = _with_after(body, 8, after)
    return pl.pallas_call(
        body, out_shape=jax.ShapeDtypeStruct((SEQ, D_FF), BF16), grid=(N_FT, SEQ // ROWS),
        in_specs=_ffn_specs() + more_specs, out_specs=pl.BlockSpec((ROWS, FFN_TN), lambda j, i: (i, j)),
        scratch_shapes=[head, head], name=name,
        compiler_params=_cp("parallel", "parallel"))(up0, up0, up0, up0, wf, wf, bf, bf, *more)


def ffn_bwd(up0, dact, wf, bf, name, after=None):
    per = ROWS // FFN_HALO
    last = SEQ // FFN_HALO - 1

    def body(g_ref, v_ref, gh_ref, vh_ref, wg_ref, wv_ref, bg_ref, bv_ref, da_ref, gn_ref, vn_ref, dan_ref,
             out_ref, dbg_ref, dbv_ref, dwg_ref, dwv_ref, pad, head_g, head_v, dgp, dvp, acc):
        i = pl.program_id(1)
        first = i == 0
        acc[...] = jnp.zeros_like(acc)

        def grads(gate, val, da):
            s = _sig(gate)
            return da * val * (s * (1.0 + gate * (1.0 - s))), da * (gate * s)

        fold = lambda q: jnp.sum(q.reshape(FFN_RB // 8, 8, 128), axis=0)

        def lanes(ls):
            wg = [wg_ref[t:t + 1, ls] for t in range(FFN_K)]
            wv = [wv_ref[t:t + 1, ls] for t in range(FFN_K)]
            sums = [jnp.zeros((8, 128), F32)] * (2 + 2 * FFN_K)
            for r0 in range(0, ROWS, FFN_RB):
                gs = _ffn_taps(g_ref, gh_ref, head_g, ls, r0)
                vs = _ffn_taps(v_ref, vh_ref, head_v, ls, r0)
                dgate, dval = grads(_ffn_conv(gs, wg, bg_ref[:, ls]), _ffn_conv(vs, wv, bv_ref[:, ls]),
                                    da_ref[r0:r0 + FFN_RB, ls])
                dgp[r0:r0 + FFN_RB, ls] = dgate
                dvp[r0:r0 + FFN_RB, ls] = dval
                new = [dgate, dval] + [dgate * gs[t] for t in range(FFN_K)] + [dval * vs[t] for t in range(FFN_K)]
                sums = [a + fold(q) for a, q in zip(sums, new)]
            for k in range(2 + 2 * FFN_K):
                acc[k, 0:8, ls] = sums[k]

        _ffn_lane_blocks(lanes)
        _acc(dbg_ref, _rsum(acc[0]), first)
        _acc(dbv_ref, _rsum(acc[1]), first)
        _acc(dwg_ref, jnp.concatenate([_rsum(acc[2 + t]) for t in range(FFN_K)], axis=0), first)
        _acc(dwv_ref, jnp.concatenate([_rsum(acc[5 + t]) for t in range(FFN_K)], axis=0), first)

        def conv_next(cur_ref, nxt_ref, w_ref, b_ref):
            pad[0:FFN_HALO, :] = cur_ref[ROWS - FFN_HALO:, :]
            pad[FFN_HALO:, :] = nxt_ref[...]
            return (b_ref[...] + w_ref[0:1, :] * pad[pl.ds(FFN_HALO - 2, FFN_HALO), :]
                    + w_ref[1:2, :] * pad[pl.ds(FFN_HALO - 1, FFN_HALO), :] + w_ref[2:3, :] * nxt_ref[...])

        gate_n = conv_next(g_ref, gn_ref, wg_ref, bg_ref)
        val_n = conv_next(v_ref, vn_ref, wv_ref, bv_ref)
        dgate_n, dval_n = grads(gate_n, val_n, dan_ref[...])
        inside = i < SEQ // ROWS - 1
        dgp[ROWS:, :] = jnp.where(inside, dgate_n, 0.0)
        dvp[ROWS:, :] = jnp.where(inside, dval_n, 0.0)

        def back(ls):
            for half, (dp, w_ref) in enumerate(((dgp, wg_ref), (dvp, wv_ref))):
                w = [w_ref[t:t + 1, ls] for t in range(FFN_K)]
                for r0 in range(0, ROWS, FFN_RB):
                    out_ref[half, r0:r0 + FFN_RB, ls] = (
                        w[2] * dp[r0:r0 + FFN_RB, ls] + w[1] * dp[pl.ds(r0 + 1, FFN_RB), ls]
                        + w[0] * dp[pl.ds(r0 + 2, FFN_RB), ls]).astype(BF16)

        _ffn_lane_blocks(back)

    body, more_specs, more = _with_after(body, 12, after)
    head = pltpu.VMEM((FFN_HALO + FFN_RB, 128), F32)
    ext = pltpu.VMEM((ROWS + FFN_HALO, FFN_TN), F32)
    vec = jax.ShapeDtypeStruct((1, D_FF), F32)
    taps = jax.ShapeDtypeStruct((FFN_K, D_FF), F32)
    cur = pl.BlockSpec((ROWS, FFN_TN), lambda j, i: (i, j))
    nxt = lambda off: pl.BlockSpec((FFN_HALO, FFN_TN), lambda j, i: (jnp.minimum((i + 1) * per, last), j + off))
    vs = pl.BlockSpec((1, FFN_TN), lambda j, i: (0, j))
    ts = pl.BlockSpec((FFN_K, FFN_TN), lambda j, i: (0, j))
    return pl.pallas_call(
        body, out_shape=(jax.ShapeDtypeStruct((2, SEQ, D_FF), BF16), vec, vec, taps, taps), grid=(N_FT, SEQ // ROWS),
        in_specs=_ffn_specs() + [cur, nxt(0), nxt(N_FT), nxt(0)] + more_specs,
        out_specs=(pl.BlockSpec((2, ROWS, FFN_TN), lambda j, i: (0, i, j)), vs, vs, ts, ts),
        scratch_shapes=[pltpu.VMEM((2 * FFN_HALO, FFN_TN), F32), head, head, ext, ext,
                        pltpu.VMEM((2 + 2 * FFN_K, SUB, FFN_TN), F32)],
        name=name, compiler_params=_cp("parallel", "arbitrary"))(up0, up0, up0, up0, wf, wf, bf, bf, dact, up0, up0, dact,
                                                                 *more)


def ada_fwd(c_all, w_ada, b_cols, name):
    def body(c_ref, w_ref, b_ref, o_ref):
        cc = c_ref[...]
        sc = (cc * _sig(cc)).astype(BF16)
        o_ref[...] = jnp.dot(sc, w_ref[...].astype(BF16), preferred_element_type=F32) + b_ref[...]

    return pl.pallas_call(body, out_shape=jax.ShapeDtypeStruct((N_DEV, w_ada.shape[1]), F32), name=name,
                          compiler_params=_cp())(c_all, w_ada, b_cols)


def _adam(w, g, m, v):
    m = ADAM_B1 * m + (1.0 - ADAM_B1) * g
    v = ADAM_B2 * v + (1.0 - ADAM_B2) * (g * g)
    m_hat = m / (1.0 - ADAM_B1 ** ADAM_STEP)
    v_hat = v / (1.0 - ADAM_B2 ** ADAM_STEP)
    delta = -ADAM_LR * (m_hat / (jnp.sqrt(v_hat) + ADAM_EPS) + ADAM_WD * w)
    return delta, m, v


def ada_bwd_adamw(c_all, dmod_cols, w, m, v, name):
    rows, cols = w.shape
    tr = 256

    def body(c_ref, dm_ref, w_ref, m_ref, v_ref, g_ref, d_ref, nm_ref, nv_ref):
        cc = c_ref[...]
        sc = (cc * _sig(cc)).T
        g = sc[:, 0:1] * dm_ref[0:1, :]
        for b in range(1, N_DEV):
            g = g + sc[:, b:b + 1] * dm_ref[b:b + 1, :]
        g_ref[...] = g
        d_ref[...], nm_ref[...], nv_ref[...] = _adam(w_ref[...], g, m_ref[...], v_ref[...])

    blk = pl.BlockSpec((tr, cols), lambda i: (i, 0))
    shp = jax.ShapeDtypeStruct((rows, cols), F32)
    return pl.pallas_call(
        body, out_shape=(shp, shp, shp, shp), grid=(rows // tr,),
        in_specs=[pl.BlockSpec((N_DEV, tr), lambda i: (0, i)), pl.BlockSpec((N_DEV, cols), lambda i: (0, 0)), blk, blk, blk],
        out_specs=(blk, blk, blk, blk), name=name, compiler_params=_cp("parallel"))(c_all, dmod_cols, w, m, v)


def sum_adamw(parts, mine, me, w, m, v, tr, name):
    n_parts, rows, cols = parts.shape

    def body(me_ref, p_ref, own_ref, w_ref, m_ref, v_ref, g_ref, d_ref, nm_ref, nv_ref):
        def chunk(rs):
            g = own_ref[0, rs, :].astype(F32)
            for k in range(1, n_parts):
                g = g + p_ref[k, rs, :].astype(F32)
            g_ref[rs, :] = g
            d_ref[rs, :], nm_ref[rs, :], nv_ref[rs, :] = _adam(w_ref[rs, :], g, m_ref[rs, :], v_ref[rs, :])

        _for_chunks(chunk, 2, tr)

    blk = pl.BlockSpec((tr, cols), lambda i, me_ref: (i, 0))
    shp = jax.ShapeDtypeStruct((rows, cols), F32)
    grid_spec = pltpu.PrefetchScalarGridSpec(
        num_scalar_prefetch=1, grid=(rows // tr,),
        in_specs=[pl.BlockSpec((n_parts, tr, cols), lambda i, me_ref: (0, i, 0)),
                  pl.BlockSpec((1, tr, cols), lambda i, me_ref: (me_ref[0], i, 0)), blk, blk, blk],
        out_specs=(blk, blk, blk, blk))
    return pl.pallas_call(body, out_shape=(shp, shp, shp, shp), grid_spec=grid_spec, name=name,
                          compiler_params=_cp("parallel"))(me, parts, mine, w, m, v)


MESH = pl.DeviceIdType.MESH


def all_gather(block, name, after=None):
    extra = () if after is None else (after,)

    def body(x_ref, *refs):
        out_ref, send_sems, recv_sems, local_sem = refs[len(extra):]
        x, y, c = lax.axis_index("x"), lax.axis_index("y"), lax.axis_index("c")
        me, sibling = (x, y, c), (x, y, 1 - c)
        chips = [(1 - x, y), (x, 1 - y), (1 - x, 1 - y)]

        def slot(px, py, pc):
            return out_ref.at[4 * px + 2 * py + pc]

        def copy(k, blk, to, src=None):
            return pltpu.make_async_remote_copy(
                src_ref=slot(*blk) if src is None else src, dst_ref=slot(*blk),
                send_sem=send_sems.at[k], recv_sem=recv_sems.at[k], device_id=to, device_id_type=MESH)

        mine = pltpu.make_async_copy(x_ref, slot(*me), local_sem)
        mine.start()
        first = [copy(0, me, sibling, src=x_ref)]
        first += [copy(1 + j, me, (*chip, c), src=x_ref) for j, chip in enumerate(chips)]
        for cp in first:
            cp.start()
        passed = [copy(4 + j, (*chip, c), sibling) for j, chip in enumerate(chips)]
        for j, chip in enumerate(chips):
            copy(1 + j, (*chip, c), me).wait_recv()
            passed[j].start()
        copy(0, sibling, me).wait_recv()
        for j, chip in enumerate(chips):
            copy(4 + j, (*chip, 1 - c), me).wait_recv()
        for cp in first + passed:
            cp.wait_send()
        mine.wait()

    return pl.pallas_call(
        body, out_shape=jax.ShapeDtypeStruct((N_DEV,) + block.shape, block.dtype), in_specs=[ANY] * (1 + len(extra)), out_specs=ANY,
        scratch_shapes=[pltpu.SemaphoreType.DMA((7,)), pltpu.SemaphoreType.DMA((7,)), pltpu.SemaphoreType.DMA],
        name=name)(block, *extra)


HBM = pl.BlockSpec(memory_space=pltpu.HBM)
SEM = pl.BlockSpec(memory_space=pltpu.SEMAPHORE)
EFFECT = pltpu.SideEffectType.DATAFLOW_SIDE_EFFECTING


def _peer_copies(src_ref, land_ref, send_sems, recv_sems, gather):
    x, y, c = lax.axis_index("x"), lax.axis_index("y"), lax.axis_index("c")
    me = 4 * x + 2 * y + c
    copies = []
    for k in range(1, N_DEV):
        px = 1 - x if k & 4 else x
        py = 1 - y if k & 2 else y
        pc = 1 - c if k & 1 else c
        copies.append(pltpu.make_async_remote_copy(
            src_ref=src_ref if gather else src_ref.at[4 * px + 2 * py + pc],
            dst_ref=land_ref.at[me] if gather else land_ref.at[k],
            send_sem=send_sems.at[k - 1], recv_sem=recv_sems.at[k - 1], device_id=(px, py, pc), device_id_type=MESH))
    return copies


def exchange_start(srcs, gather, name, after=None):
    n = len(srcs)
    land_shapes = [(N_DEV,) + src.shape if gather else src.shape for src in srcs]
    extra = () if after is None else (after,)

    def body(*refs):
        src_refs, land_refs = refs[0:n], refs[n:2 * n]
        outs = refs[2 * n + len(extra):]
        for k in range(n):
            for cp in _peer_copies(src_refs[k], land_refs[k], outs[4 * k], outs[4 * k + 1], gather):
                cp.start()
        token = outs[4 * n]
        token[...] = jnp.zeros_like(token)

    out_shape, out_specs, aliases = [], [], {}
    for k, src in enumerate(srcs):
        out_shape += [pltpu.SemaphoreType.DMA((N_DEV - 1,)), pltpu.SemaphoreType.DMA((N_DEV - 1,)),
                      pltpu.HBM(src.shape, src.dtype), pltpu.HBM(land_shapes[k], src.dtype)]
        out_specs += [SEM, SEM, HBM, HBM]
        aliases[k] = 4 * k + 2
        aliases[n + k] = 4 * k + 3
    out_shape.append(jax.ShapeDtypeStruct((8, 128), F32))
    out_specs.append(pl.BlockSpec(memory_space=pltpu.VMEM))
    res = pl.pallas_call(
        body, name=name, out_shape=tuple(out_shape), in_specs=(HBM,) * (2 * n) + (ANY,) * len(extra),
        out_specs=tuple(out_specs), input_output_aliases=aliases,
        compiler_params=pltpu.CompilerParams(has_side_effects=EFFECT),
    )(*[pltpu.with_memory_space_constraint(src, pltpu.HBM) for src in srcs],
      *[pltpu.with_memory_space_constraint(lax.empty(shp, src.dtype), pltpu.HBM) for shp, src in zip(land_shapes, srcs)],
      *extra)
    return [tuple(res[4 * k:4 * k + 4]) for k in range(n)], res[4 * n]


def _stage1_peer(i):
    x, y, c = lax.axis_index("x"), lax.axis_index("y"), lax.axis_index("c")
    if i == 0:
        return (x, y, 1 - c)
    return (1 - x if i & 1 else x, 1 - y if i & 2 else y, c)


def _slot_of(peer):
    return 4 * peer[0] + 2 * peer[1] + peer[2]


def _stage1_copy(i, src_ref, land_ref, send_sems, recv_sems):
    me = _slot_of((lax.axis_index("x"), lax.axis_index("y"), lax.axis_index("c")))
    return pltpu.make_async_remote_copy(src_ref=src_ref, dst_ref=land_ref.at[me], send_sem=send_sems.at[i],
                                        recv_sem=recv_sems.at[i], device_id=_stage1_peer(i), device_id_type=MESH)


def _stage2_copy(j, land_ref, send_sems, recv_sems):
    slot = _slot_of(_stage1_peer(j + 1))
    return pltpu.make_async_remote_copy(src_ref=land_ref.at[slot], dst_ref=land_ref.at[slot], send_sem=send_sems.at[j],
                                        recv_sem=recv_sems.at[j], device_id=_stage1_peer(0), device_id_type=MESH)


def gather2_start(srcs, name, after=None):
    n = len(srcs)
    extra = () if after is None else (after,)

    def body(*refs):
        src_refs, land_refs = refs[0:n], refs[n:2 * n]
        outs = refs[2 * n + len(extra):]
        for k in range(n):
            for i in range(4):
                _stage1_copy(i, src_refs[k], land_refs[k], outs[4 * k], outs[4 * k + 1]).start()
        outs[4 * n][...] = jnp.zeros((8, 128), F32)

    out_shape, out_specs, aliases = [], [], {}
    for k, src in enumerate(srcs):
        out_shape += [pltpu.SemaphoreType.DMA((4,)), pltpu.SemaphoreType.DMA((4,)),
                      pltpu.HBM(src.shape, src.dtype), pltpu.HBM((N_DEV,) + src.shape, src.dtype)]
        out_specs += [SEM, SEM, HBM, HBM]
        aliases[k] = 4 * k + 2
        aliases[n + k] = 4 * k + 3
    out_shape.append(jax.ShapeDtypeStruct((8, 128), F32))
    out_specs.append(pl.BlockSpec(memory_space=pltpu.VMEM))
    res = pl.pallas_call(
        body, name=name, out_shape=tuple(out_shape), in_specs=(HBM,) * (2 * n) + (ANY,) * len(extra),
        out_specs=tuple(out_specs), input_output_aliases=aliases,
        compiler_params=pltpu.CompilerParams(has_side_effects=EFFECT),
    )(*[pltpu.with_memory_space_constraint(src, pltpu.HBM) for src in srcs],
      *[pltpu.with_memory_space_constraint(lax.empty((N_DEV,) + src.shape, src.dtype), pltpu.HBM) for src in srcs], *extra)
    return [dict(send1=res[4 * k], recv1=res[4 * k + 1], src=res[4 * k + 2], land=res[4 * k + 3]) for k in range(n)], \
        res[4 * n]


def gather2_pass(handles, after, name):
    n = len(handles)

    def body(*refs):
        src_refs, land_refs, recv1 = refs[0:n], refs[n:2 * n], refs[2 * n:3 * n]
        outs = refs[3 * n + 1:]
        for k in range(n):
            for j in range(3):
                _stage1_copy(j + 1, src_refs[k], land_refs[k], recv1[k], recv1[k]).wait_recv()
                _stage2_copy(j, land_refs[k], outs[3 * k], outs[3 * k + 1]).start()
        outs[3 * n][...] = jnp.zeros((8, 128), F32)

    out_shape, out_specs, aliases = [], [], {}
    for k, h in enumerate(handles):
        out_shape += [pltpu.SemaphoreType.DMA((3,)), pltpu.SemaphoreType.DMA((3,)), pltpu.HBM(h["land"].shape, h["land"].dtype)]
        out_specs += [SEM, SEM, HBM]
        aliases[n + k] = 3 * k + 2
    out_shape.append(jax.ShapeDtypeStruct((8, 128), F32))
    out_specs.append(pl.BlockSpec(memory_space=pltpu.VMEM))
    res = pl.pallas_call(
        body, name=name, out_shape=tuple(out_shape), in_specs=(HBM,) * (2 * n) + (SEM,) * n + (ANY,),
        out_specs=tuple(out_specs), input_output_aliases=aliases,
        compiler_params=pltpu.CompilerParams(has_side_effects=EFFECT),
    )(*[h["src"] for h in handles], *[h["land"] for h in handles], *[h["recv1"] for h in handles], after)
    return [dict(h, send2=res[3 * k], recv2=res[3 * k + 1], land=res[3 * k + 2]) for k, h in enumerate(handles)], res[3 * n]


def gather2_wait(h, after, name):
    def body(src_ref, land_ref, send1, recv1, send2, recv2, after_ref, src_dead, got_ref):
        for i in range(4):
            _stage1_copy(i, src_ref, land_ref, send1, recv1).wait_send()
        _stage1_copy(0, src_ref, land_ref, send1, recv1).wait_recv()
        for j in range(3):
            cp = _stage2_copy(j, land_ref, send2, recv2)
            cp.wait_send()
            cp.wait_recv()

    return pl.pallas_call(
        body, name=name,
        out_shape=(pltpu.HBM(h["src"].shape, h["src"].dtype), pltpu.HBM(h["land"].shape, h["land"].dtype)),
        in_specs=(HBM, HBM, SEM, SEM, SEM, SEM, ANY), out_specs=(HBM, HBM), input_output_aliases={0: 0, 1: 1},
        compiler_params=pltpu.CompilerParams(has_side_effects=EFFECT),
    )(h["src"], h["land"], h["send1"], h["recv1"], h["send2"], h["recv2"], after)[1]


def exchange_wait(handles, after, gather, name):
    send_sems, recv_sems, src_thru, land_thru = handles

    def body(src_ref, land_ref, send_sems, recv_sems, after_ref, src_dead, got_ref):
        for cp in _peer_copies(src_ref, land_ref, send_sems, recv_sems, gather):
            cp.wait_send()
            cp.wait_recv()

    return pl.pallas_call(
        body, name=name,
        out_shape=(pltpu.HBM(src_thru.shape, src_thru.dtype), pltpu.HBM(land_thru.shape, land_thru.dtype)),
        in_specs=(HBM, HBM, SEM, SEM, ANY), out_specs=(HBM, HBM), input_output_aliases={0: 0, 1: 1},
        compiler_params=pltpu.CompilerParams(has_side_effects=EFFECT),
    )(src_thru, land_thru, send_sems, recv_sems, after)


def local_step(x, tgt, mod, started, get_w, put_grad, wc, wf, g_mix, bc, lg, lb, gco, gao, g_ffn, bf, g_fin):
    w_in = get_w("w_in", mod)
    proj, h1 = rms_mod_matmul(x, g_mix, mod, 0, 1, w_in, D_IN // N_DEV, "proj_fwd", after=started)
    mix_a, u1 = conv_module_fwd(proj, wc, bc, lg, lb, gco, "conv_module_fwd")
    att, lse = attn_fwd_all(proj, "attn_fwd")
    w_out = get_w("w_out", att)
    y1, mixed = norm_concat_matmul(mix_a, att, gao, w_out, "out_proj_fwd")
    w_up = get_w("w_up", y1)
    up0, x1, h2 = resid_rms_mod_matmul(x, y1, g_ffn, mod, 2, 3, 4, w_up, "up_fwd")
    passed = get_w("w_down", up0, only_pass_on=True)
    act = ffn_act_fwd(up0, wf, bf, "ffn_act_fwd", after=passed)
    w_down = get_w("w_down", act)
    loss_t, dx2, dy2, d_gfin, d_gaf = matmul_loss_bwd(act, w_down, x1, tgt, g_fin, mod, 5, "down_fwd_loss")
    dact = matmul(dy2, w_down, "nt", F32, 512, FFN_TN, "down_bwd_x")
    dw_down = matmul(act, dy2, "tn", BF16, 256, D_MODEL, "down_bwd_w")
    dup0, dbf_g, dbf_v, dwf_g, dwf_v = ffn_bwd(up0, dact, wf, bf, "ffn_bwd", after=put_grad("w_down", dw_down))
    dw_up = matmul_tn_halves(dup0, h2, 256, "up_bwd_w")
    dx1, d_shf, d_scf, d_gffn, dy1, d_gam = matmul_rms_mod_bwd(
        dup0, w_up, x1, dx2, g_ffn, mod, 4, y1, 2, "up_bwd_x", after=put_grad("w_up", dw_up))
    dw_out = matmul(mixed, dy1, "tn", BF16, 256, D_MODEL, "out_proj_bwd_w")
    dmixed, do, dd, d_gao = matmul_combine_bwd(dy1, w_out, att, gao, "out_proj_bwd_x", after=put_grad("w_out", dw_out))
    dqkv = attn_bwd_all(proj, do, lse, dd, "attn_bwd")
    du1, d_gco, d_lg, d_lb, d_bc, d_wc = conv_module_bwd_a(proj, u1, dmixed, lg, lb, gco, "conv_module_bwd_a")
    dproj_a = conv_module_bwd_b(proj, du1, wc, "conv_module_bwd_b")
    dw_in = matmul_tn_pieces(dproj_a, dqkv, h1, "proj_bwd_w")
    dx, d_shm, d_scm, d_gmix = matmul_rms_mod_bwd(
        (dproj_a, dqkv), w_in, x, dx1, g_mix, mod, 1, None, 0, "proj_bwd_x", b_rows=D_IN // N_DEV,
        after=put_grad("w_in", dw_in))
    dmod = jnp.concatenate([d_shm, d_scm, d_gam, d_shf, d_scf, d_gaf], axis=1)
    small = dict(g_norm_mix=d_gmix, b_conv_dw=d_bc, ln_conv_g=d_lg, ln_conv_b=d_lb, g_conv_out=d_gco, g_attn_out=d_gao,
                 g_norm_ffn=d_gffn, b_ffn_dw=jnp.concatenate([dbf_g, dbf_v], axis=1), g_final=d_gfin,
                 w_conv_dw=d_wc, w_ffn_dw=jnp.concatenate([dwf_g, dwf_v], axis=1), dmod=dmod, loss=loss_t[0:1, 0:1])
    return dx, small


PACK_W = 7168
TAPS_PER_ROW = PACK_W // D_CONV
PACKED_AT = dict(
    b_ada=(0, 0, N_MOD * D_MODEL), g_norm_mix=(0, 6144, D_MODEL),
    b_ffn_dw=(1, 0, 2 * D_FF), g_norm_ffn=(1, 5632, D_MODEL), b_conv_dw=(1, 6656, D_CONV),
    g_final=(2, 5632, D_MODEL), ln_conv_g=(2, 6656, D_CONV),
    ln_conv_b=(3, 5632, D_CONV), g_conv_out=(3, 6144, D_CONV), g_attn_out=(3, 6656, D_ATTN))
SMALL_ORDER = list(PACKED_AT)
LOSS_AT = (4, 2 * D_FF)


def pack_small(t):
    cat = lambda *parts: jnp.concatenate(parts, axis=1)
    wf = t["w_ffn_dw"]
    taps = jnp.pad(t["w_conv_dw"].reshape(1, CONV_K * D_CONV), ((0, 0), (0, 3 * PACK_W - CONV_K * D_CONV)))
    return jnp.concatenate([
        cat(t["dmod"], t["g_norm_mix"]),
        cat(t["b_ffn_dw"], t["g_norm_ffn"], t["b_conv_dw"]),
        cat(wf[0:1], t["g_final"], t["ln_conv_g"]),
        cat(wf[1:2], t["ln_conv_b"], t["g_conv_out"], t["g_attn_out"]),
        cat(wf[2:3], jnp.pad(t["loss"], ((0, 0), (0, PACK_W - 2 * D_FF - 1)))),
        taps.reshape(3, PACK_W)], axis=0)


def small_adamw(parts, wmv, name):
    def body(*refs):
        p_ref = refs[0]
        ins = refs[1:1 + 3 * len(SMALL_ORDER)]
        outs = refs[1 + 3 * len(SMALL_ORDER):]
        g = p_ref[0]
        for k in range(1, N_DEV):
            g = g + p_ref[k]
        for i, n in enumerate(SMALL_ORDER):
            row, lane, width = PACKED_AT[n]
            gp = g[row:row + 1, lane:lane + width]
            w_ref, m_ref, v_ref = ins[3 * i:3 * i + 3]
            g_ref, d_ref, nm_ref, nv_ref = outs[4 * i:4 * i + 4]
            g_ref[...] = gp
            d_ref[...], nm_ref[...], nv_ref[...] = _adam(w_ref[...], gp, m_ref[...], v_ref[...])
        wc_ref, wf_ref, loss_ref = outs[4 * len(SMALL_ORDER):]
        for j in range(CONV_K):
            row, lane = 5 + j // TAPS_PER_ROW, (j % TAPS_PER_ROW) * D_CONV
            wc_ref[j:j + 1, :] = g[row:row + 1, lane:lane + D_CONV]
        wf_ref[...] = g[2:2 + FFN_K, 0:2 * D_FF]
        loss_ref[...] = jnp.broadcast_to(g[LOSS_AT[0]:LOSS_AT[0] + 1, LOSS_AT[1]:LOSS_AT[1] + 1], (8, 128))

    args, out_shape = [parts], []
    for n in SMALL_ORDER:
        args += list(wmv[n])
        out_shape += [jax.ShapeDtypeStruct(wmv[n][0].shape, F32)] * 4
    out_shape += [jax.ShapeDtypeStruct((CONV_K, D_CONV), F32), jax.ShapeDtypeStruct((FFN_K, 2 * D_FF), F32),
                  jax.ShapeDtypeStruct((8, 128), F32)]
    res = pl.pallas_call(body, out_shape=tuple(out_shape), name=name, compiler_params=_cp())(*args)
    per = {n: tuple(res[4 * i:4 * i + 4]) for i, n in enumerate(SMALL_ORDER)}
    return per, res[-3], res[-2], res[-1][0, 0]


def shard_adamw(items, name):
    def body(*refs):
        ins, outs = refs[:4 * len(items)], refs[4 * len(items):]
        for i in range(len(items)):
            g_ref, w_ref, m_ref, v_ref = ins[4 * i:4 * i + 4]
            og_ref, d_ref, nm_ref, nv_ref = outs[4 * i:4 * i + 4]
            og_ref[...] = g_ref[...]
            d_ref[...], nm_ref[...], nv_ref[...] = _adam(w_ref[...], g_ref[...], m_ref[...], v_ref[...])

    args = [a for item in items for a in item]
    out_shape = tuple(jax.ShapeDtypeStruct(item[1].shape, F32) for item in items for _ in range(4))
    res = pl.pallas_call(body, out_shape=out_shape, name=name, compiler_params=_cp())(*args)
    return [tuple(res[4 * i:4 * i + 4]) for i in range(len(items))]


def _shard(full, n_cols, me):
    return lax.dynamic_slice(full, (0, me * n_cols), (full.shape[0], n_cols))


WEIGHTS = ["w_ada", "b_ada", "g_norm_mix", "w_in", "w_conv_dw", "b_conv_dw", "ln_conv_g", "ln_conv_b", "g_conv_out",
           "g_attn_out", "w_out", "g_norm_ffn", "w_up", "w_ffn_dw", "b_ffn_dw", "w_down", "g_final"]


def kernel(x, c, w_ada, b_ada, g_norm_mix, w_in, w_conv_dw, b_conv_dw, ln_conv_g, ln_conv_b, g_conv_out, g_attn_out, w_out, g_norm_ffn, w_up, w_ffn_dw, b_ffn_dw, w_down, g_final, loss_target, m_w_ada, m_b_ada, m_g_norm_mix, m_w_in, m_w_conv_dw, m_b_conv_dw, m_ln_conv_g, m_ln_conv_b, m_g_conv_out, m_g_attn_out, m_w_out, m_g_norm_ffn, m_w_up, m_w_ffn_dw, m_b_ffn_dw, m_w_down, m_g_final, v_w_ada, v_b_ada, v_g_norm_mix, v_w_in, v_w_conv_dw, v_b_conv_dw, v_ln_conv_g, v_ln_conv_b, v_g_conv_out, v_g_attn_out, v_w_out, v_g_norm_ffn, v_w_up, v_w_ffn_dw, v_b_ffn_dw, v_w_down, v_g_final):
    args = dict(locals())
    me = 4 * lax.axis_index("x") + 2 * lax.axis_index("y") + lax.axis_index("c")
    me1 = me.astype(jnp.int32).reshape(1)

    def flat(name, prefix=""):
        a = args[prefix + name]
        return a.reshape(a.shape[-2] if a.ndim > 1 else 1, a.shape[-1])

    def flat_t(name, prefix=""):
        return args[prefix + name][0].T

    n_in, n_up, r_out, r_down = w_in.shape[2], w_up.shape[2], w_out.shape[1], w_down.shape[1]
    n_ada, n_wc, n_wf = w_ada.shape[2], w_conv_dw.shape[2], w_ffn_dw.shape[2]
    taps_c = jnp.pad(flat("w_conv_dw").reshape(1, CONV_K * n_wc), ((0, 0), (0, 2 * D_MODEL - CONV_K * n_wc)))
    taps_f = jnp.pad(flat("w_ffn_dw").reshape(1, FFN_K * n_wf), ((0, 0), (0, 3 * D_MODEL - FFN_K * n_wf)))
    first = jnp.concatenate([c, taps_c.reshape(2, D_MODEL), taps_f.reshape(3, D_MODEL), jnp.zeros((2, D_MODEL), F32)], axis=0)
    w_in_block = flat_t("w_in").astype(BF16)
    hi = lax.reduce_precision(first, 8, 7)
    mid = lax.reduce_precision(first - hi, 8, 7)
    low = lax.reduce_precision(first - hi - mid, 8, 7)
    terms = jnp.concatenate([hi, mid, low, jnp.zeros((8, D_MODEL), F32)], axis=0).astype(BF16)
    first_block = all_gather(jnp.concatenate([w_in_block, terms], axis=0), "gather_c_taps_w_in")
    terms = first_block[:, n_in:n_in + 24, :].astype(F32)
    first_all = (terms[:, 0:8] + terms[:, 8:16]) + terms[:, 16:24]
    c_all = first_all[:, 0, :]
    wc_full = first_all[:, 1:3, :].reshape(N_DEV, 2 * D_MODEL)[:, :CONV_K * n_wc].reshape(N_DEV, CONV_K, n_wc)
    wc_full = wc_full.transpose(1, 0, 2).reshape(CONV_K, D_CONV)
    wf_full = first_all[:, 3:6, :].reshape(N_DEV, 3 * D_MODEL)[:, :FFN_K * n_wf].reshape(N_DEV, FFN_K, n_wf)
    wf_full = wf_full.transpose(1, 0, 2).reshape(FFN_K, 2 * D_FF)
    mod_cols = ada_fwd(c_all, flat("w_ada"), _shard(flat("b_ada"), n_ada, me), "ada_fwd")
    mod_all = all_gather(mod_cols, "gather_mod")
    mod = lax.dynamic_index_in_dim(mod_all, me, axis=1, keepdims=False).reshape(N_MOD, D_MODEL)
    mod = jnp.pad(mod, ((0, 2), (0, 0)))

    order = ("w_out", "w_up", "w_down")
    blocks = dict(w_up=flat_t("w_up").astype(BF16), w_out=flat("w_out").astype(BF16), w_down=flat("w_down").astype(BF16))
    handles, tok = gather2_start([blocks[name] for name in order], "gather_weights_start", mod_all)
    gathers = dict(zip(order, handles))

    def pass_on(name, after):
        if "send2" not in gathers[name]:
            group = ("w_out", "w_up") if name != "w_down" else ("w_down",)
            passed, token = gather2_pass([gathers[w] for w in group], after, f"gather_{name}_pass")
            gathers.update(zip(group, passed))
            return token

    def gathered(name, after):
        pass_on(name, after)
        land = gather2_wait(gathers[name], after, f"gather_{name}_wait")
        return lax.dynamic_update_index_in_dim(land, blocks[name], me, axis=0)

    def get_w(name, after, only_pass_on=False):
        if name == "w_in":
            return first_block
        if only_pass_on:
            return pass_on(name, after)
        return gathered(name, after).reshape(-1, D_MODEL)

    exchanges = {}

    def put_grad(name, dw, after=None):
        dev_major = dw.reshape(N_DEV, -1, D_MODEL)
        (exchanges[name],), token = exchange_start([dev_major], False, f"exchange_{name}_start", after)
        return token

    grad_x, small = local_step(
        x[0], loss_target[0], mod, tok, get_w, put_grad, wc_full, wf_full,
        flat("g_norm_mix"), flat("b_conv_dw"), flat("ln_conv_g"), flat("ln_conv_b"), flat("g_conv_out"),
        flat("g_attn_out"), flat("g_norm_ffn"), flat("b_ffn_dw"), flat("g_final"))

    out = {}

    def finish(name, tr, after, then=None):
        mine, parts = exchange_wait(exchanges[name], after, False, f"exchange_{name}_wait")
        if then is not None:
            then(parts)
        if name in ("w_in", "w_up"):
            res = sum_adamw(parts, mine, me1, flat_t(name), flat_t(name, "m_"), flat_t(name, "v_"), tr, "adamw_" + name)
            out[name] = tuple(r.T for r in res)
        else:
            res = out[name] = sum_adamw(parts, mine, me1, flat(name), flat(name, "m_"), flat(name, "v_"), tr,
                                        "adamw_" + name)
        return res[0]

    after = finish("w_down", r_down, grad_x)
    after = finish("w_up", n_up // 2, after)
    after = finish("w_out", r_out, after)
    packed = pack_small(small)
    small_gather = []
    after = finish("w_in", n_in, after, then=lambda parts: small_gather.append(
        gather2_start([packed], "gather_small_start", parts)[0][0]))
    (handle,), _ = gather2_pass(small_gather, after, "gather_small_pass")
    small_all = lax.dynamic_update_index_in_dim(
        gather2_wait(handle, after, "gather_small_wait"), packed, me, axis=0)

    wmv = {n: (flat(n), flat(n, "m_"), flat(n, "v_")) for n in SMALL_ORDER}
    per, g_wc, g_wf, loss = small_adamw(small_all, wmv, "adamw_small")
    out.update(per)
    taps = shard_adamw([(_shard(g_wc, n_wc, me), flat("w_conv_dw"), flat("w_conv_dw", "m_"), flat("w_conv_dw", "v_")),
                        (_shard(g_wf, n_wf, me), flat("w_ffn_dw"), flat("w_ffn_dw", "m_"), flat("w_ffn_dw", "v_"))],
                       "adamw_taps")
    out["w_conv_dw"], out["w_ffn_dw"] = taps

    dmod_cols = _shard(small_all[:, 0, :], n_ada, me)
    out["w_ada"] = ada_bwd_adamw(c_all, dmod_cols, flat("w_ada"), flat("w_ada", "m_"), flat("w_ada", "v_"), "adamw_w_ada")

    result = [loss, grad_x[None]]
    for k in range(4):
        result += [out[n][k].reshape(args[n].shape) for n in WEIGHTS]
    return tuple(result)
```

```python
import jax
import jax.numpy as jnp
from jax import lax
from jax.experimental import pallas as pl
from jax.experimental.pallas import tpu as pltpu

F32 = jnp.float32
BF16 = jnp.bfloat16

N_DEV = 8
SEQ = 2048
D_MODEL = 1024
D_CONV = 512
D_ATTN = 512
HEAD_DIM = 64
CONV_K = 31
D_FF = 2816
FFN_K = 3
D_IN = 2 * D_CONV + 3 * D_ATTN
N_MOD = 6
EPS = 1e-6
ATTN_BLOCK = 128
PATTERNS = ((2048, 1), (512, 4), (128, 16))
NEG = -1e30

ADAM_LR, ADAM_B1, ADAM_B2, ADAM_EPS, ADAM_WD, ADAM_STEP = 0.001, 0.9, 0.999, 1e-08, 0.01, 10

ROWS = 256
CONV_HALO = 32
FFN_HALO = 8
FFN_TN = 1408
VMEM_LIMIT = 56 * 1024 * 1024


NT = (((1,), (1,)), ((), ()))
ANY = pl.BlockSpec(memory_space=pl.ANY)


def _cp(*sem):
    return pltpu.CompilerParams(dimension_semantics=sem if sem else None, vmem_limit_bytes=VMEM_LIMIT)


def _with_after(body, n_in, after):
    if after is None:
        return body, [], []
    return (lambda *refs: body(*refs[:n_in], *refs[n_in + 1:])), [ANY], [after]


def _sig(x):
    return 1.0 / (1.0 + jnp.exp(-x))


def _rsum(x):
    return jnp.sum(x, axis=0, keepdims=True)


def _mean(x):
    return jnp.mean(x, axis=-1, keepdims=True)


def _acc(ref, val, first):
    @pl.when(first)
    def _():
        ref[...] = val

    @pl.when(jnp.logical_not(first))
    def _():
        ref[...] += val


SUB = 16


def _for_chunks(fn, unroll=1, rows=ROWS):
    def step(i, carry):
        fn(pl.ds(pl.multiple_of(i * SUB, SUB), SUB))
        return carry

    lax.fori_loop(0, rows // SUB, step, 0, unroll=unroll)


PAIR = 2 * ROWS


def _pair_spec(width, col=0):
    return pl.BlockSpec((PAIR, width), lambda i: (i, col))


def _halves():
    return [slice(h * ROWS, (h + 1) * ROWS) for h in range(2)]


def rms_mod_matmul(x, g, mod, sh_row, sc_row, b, b_rows, name, after=None):
    n = N_DEV * b_rows

    def body(x_ref, g_ref, mod_ref, b_ref, o_ref, h_ref):
        for rs in _halves():
            xx = x_ref[rs, :]
            r = lax.rsqrt(_mean(xx * xx) + EPS)
            h = (xx * r * g_ref[...] * (1.0 + mod_ref[sc_row:sc_row + 1, :]) + mod_ref[sh_row:sh_row + 1, :]).astype(BF16)
            h_ref[rs, :] = h
            o_ref[rs, :] = lax.dot_general(h, b_ref[...].reshape(n, D_MODEL), NT, preferred_element_type=F32)

    body, more_specs, more = _with_after(body, 4, after)
    return pl.pallas_call(
        body, out_shape=(jax.ShapeDtypeStruct((SEQ, n), F32), jax.ShapeDtypeStruct((SEQ, D_MODEL), BF16)),
        grid=(SEQ // PAIR,),
        in_specs=[_pair_spec(D_MODEL), _vec_spec(D_MODEL), _vec_spec(D_MODEL, 8),
                  pl.BlockSpec((N_DEV, b_rows, D_MODEL), lambda i: (0, 0, 0))] + more_specs,
        out_specs=(_pair_spec(n), _pair_spec(D_MODEL)), name=name, compiler_params=_cp("parallel"))(x, g, mod, b, *more)


def norm_concat_matmul(mix_a, att, gao, w, name):
    def body(a_ref, att_ref, g_ref, w_ref, y_ref, mixed_ref):
        for rs in _halves():
            aa = att_ref[rs, :]
            mixed_ref[rs, 0:D_CONV] = a_ref[rs, :]
            mixed_ref[rs, D_CONV:] = (aa * lax.rsqrt(_mean(aa * aa) + EPS) * g_ref[...]).astype(BF16)
            y_ref[rs, :] = jnp.dot(mixed_ref[rs, :], w_ref[...], preferred_element_type=F32)

    return pl.pallas_call(
        body, out_shape=(jax.ShapeDtypeStruct((SEQ, D_MODEL), F32), jax.ShapeDtypeStruct((SEQ, D_MODEL), BF16)),
        grid=(SEQ // PAIR,),
        in_specs=[_pair_spec(D_CONV), _pair_spec(D_ATTN), _vec_spec(D_ATTN), pl.BlockSpec(w.shape, lambda i: (0, 0))],
        out_specs=(_pair_spec(D_MODEL), _pair_spec(D_MODEL)), name=name, compiler_params=_cp("parallel"))(mix_a, att, gao, w)


def resid_rms_mod_matmul(x, y, g, mod, ga_row, sh_row, sc_row, w, name):
    n = w.shape[0]

    def body(x_ref, y_ref, g_ref, mod_ref, w_ref, o_ref, x1_ref, h_ref):
        x1 = x_ref[...] + mod_ref[ga_row:ga_row + 1, :] * y_ref[...]
        x1_ref[...] = x1
        r = lax.rsqrt(_mean(x1 * x1) + EPS)
        h = (x1 * r * g_ref[...] * (1.0 + mod_ref[sc_row:sc_row + 1, :]) + mod_ref[sh_row:sh_row + 1, :]).astype(BF16)
        h_ref[...] = h
        o_ref[...] = lax.dot_general(h, w_ref[...], NT, preferred_element_type=F32)

    return pl.pallas_call(
        body,
        out_shape=(jax.ShapeDtypeStruct((SEQ, n), F32), jax.ShapeDtypeStruct((SEQ, D_MODEL), F32),
                   jax.ShapeDtypeStruct((SEQ, D_MODEL), BF16)),
        grid=(SEQ // ROWS,),
        in_specs=[_row_spec(D_MODEL), _row_spec(D_MODEL), _vec_spec(D_MODEL), _vec_spec(D_MODEL, 8),
                  pl.BlockSpec(w.shape, lambda i: (0, 0))],
        out_specs=(_row_spec(n), _row_spec(D_MODEL), _row_spec(D_MODEL)),
        name=name, compiler_params=_cp("parallel"))(x, y, g, mod, w)


def matmul_combine_bwd(dy, w, att, gao, name, after=None):
    def body(dy_ref, w_ref, att_ref, g_ref, dm_ref, do_ref, dd_ref, dg_ref):
        @pl.when(pl.program_id(0) == 0)
        def _():
            dg_ref[...] = jnp.zeros_like(dg_ref)

        same_head = (jnp.right_shift(lax.broadcasted_iota(jnp.int32, (D_ATTN, D_ATTN), 0), 6)
                     == jnp.right_shift(lax.broadcasted_iota(jnp.int32, (D_ATTN, D_ATTN), 1), 6)).astype(F32)
        for rs in _halves():
            dmixed = lax.dot_general(dy_ref[rs, :], w_ref[...], NT, preferred_element_type=F32)
            dm_ref[rs, :] = dmixed
            att = att_ref[rs, :]
            r = lax.rsqrt(_mean(att * att) + EPS)
            xn = att * r
            dm = dmixed[:, D_CONV:]
            dg_ref[...] += _rsum(dm * xn)
            dyn = dm * g_ref[...]
            do = r * (dyn - xn * _mean(dyn * xn))
            do_ref[rs, :] = do
            dd_ref[rs, :] = jnp.dot(do * att, same_head, preferred_element_type=F32, precision=lax.Precision.HIGHEST)

    rs = _pair_spec(D_ATTN)
    f = jax.ShapeDtypeStruct((SEQ, D_ATTN), F32)
    body, more_specs, more = _with_after(body, 4, after)
    return pl.pallas_call(
        body, out_shape=(jax.ShapeDtypeStruct((SEQ, D_MODEL), F32), f, f, jax.ShapeDtypeStruct((1, D_ATTN), F32)),
        grid=(SEQ // PAIR,),
        in_specs=[_pair_spec(D_MODEL), pl.BlockSpec(w.shape, lambda i: (0, 0)), rs, _vec_spec(D_ATTN)] + more_specs,
        out_specs=(_pair_spec(D_MODEL), rs, rs, _vec_spec(D_ATTN)),
        name=name, compiler_params=_cp("arbitrary"))(dy, w, att, gao, *more)


def matmul_loss_bwd(act, w, x1, tgt, g, mod, ga_row, name):
    def body(a_ref, w_ref, x1_ref, t_ref, g_ref, mod_ref, loss_ref, dx2_ref, dy2_ref, dg_ref, dga_ref):
        @pl.when(pl.program_id(0) == 0)
        def _():
            loss_ref[...] = jnp.zeros_like(loss_ref)
            dg_ref[...] = jnp.zeros_like(dg_ref)
            dga_ref[...] = jnp.zeros_like(dga_ref)

        ga = mod_ref[ga_row:ga_row + 1, :]
        for rs in _halves():
            y2 = jnp.dot(a_ref[rs, :], w_ref[...], preferred_element_type=F32)
            x2 = x1_ref[rs, :] + ga * y2
            r = lax.rsqrt(_mean(x2 * x2) + EPS)
            xn = x2 * r
            err = xn * g_ref[...] - t_ref[rs, :]
            loss_ref[...] += jnp.broadcast_to(0.5 * jnp.sum(_mean(err * err)), (8, 128))
            dy = err * (1.0 / D_MODEL)
            dg_ref[...] += _rsum(dy * xn)
            dxn = dy * g_ref[...]
            dx2 = r * (dxn - xn * _mean(dxn * xn))
            dx2_ref[rs, :] = dx2
            dy2_ref[rs, :] = (dx2 * ga).astype(BF16)
            dga_ref[...] += _rsum(dx2 * y2)

    vec = jax.ShapeDtypeStruct((1, D_MODEL), F32)
    rows = _pair_spec
    return pl.pallas_call(
        body,
        out_shape=(jax.ShapeDtypeStruct((8, 128), F32), jax.ShapeDtypeStruct((SEQ, D_MODEL), F32),
                   jax.ShapeDtypeStruct((SEQ, D_MODEL), BF16), vec, vec),
        grid=(SEQ // PAIR,),
        in_specs=[rows(act.shape[1]), pl.BlockSpec(w.shape, lambda i: (0, 0)), rows(D_MODEL), rows(D_MODEL),
                  _vec_spec(D_MODEL), _vec_spec(D_MODEL, 8)],
        out_specs=(pl.BlockSpec((8, 128), lambda i: (0, 0)), rows(D_MODEL), rows(D_MODEL),
                   _vec_spec(D_MODEL), _vec_spec(D_MODEL)),
        name=name, compiler_params=_cp("arbitrary"))(act, w, x1, tgt, g, mod)


def matmul_rms_mod_bwd(a, b, x, dres, g, mod, sc_row, y, ga_row, name, b_rows=None, after=None):
    gated = y is not None
    pieces = isinstance(a, tuple)
    tm = PAIR if pieces else ROWS
    blocks = [slice(h * ROWS, (h + 1) * ROWS) for h in range(tm // ROWS)]
    rows = lambda width: pl.BlockSpec((tm, width), lambda i: (i, 0))
    if pieces:
        a1, a3 = a
        a_args = [a1, a3]
        a_specs = [rows(a1.shape[1]), pl.BlockSpec((3, tm, a3.shape[2]), lambda i: (0, i, 0))]
        b_arg, b_spec = b, pl.BlockSpec((N_DEV, b_rows, D_MODEL), lambda i: (0, 0, 0))
    else:
        k2 = a.shape[2]
        a_args = [a]
        a_specs = [pl.BlockSpec((2, tm, k2), lambda i: (0, i, 0))]
        b_arg, b_spec = b.reshape(2, k2, D_MODEL), pl.BlockSpec((2, k2, D_MODEL), lambda i: (0, 0, 0))
    n_a = len(a_args)

    def body(*refs):
        a_refs, (b_ref, x_ref, dres_ref, g_ref, mod_ref) = refs[:n_a], refs[n_a:n_a + 5]
        if gated:
            y_ref, dx_ref, dsh_ref, dsc_ref, dg_ref, dy_ref, dga_ref = refs[n_a + 5:]
        else:
            dx_ref, dsh_ref, dsc_ref, dg_ref = refs[n_a + 5:]

        @pl.when(pl.program_id(0) == 0)
        def _():
            for ref in (dsh_ref, dsc_ref, dg_ref) + ((dga_ref,) if gated else ()):
                ref[...] = jnp.zeros_like(ref)

        gg = g_ref[...]
        for rs in blocks:
            if pieces:
                wv = b_ref[...].reshape(N_DEV * b_rows, D_MODEL)
                k1, k3 = a_refs[0].shape[1], a_refs[1].shape[2]
                dh = jnp.dot(a_refs[0][rs, :], wv[0:k1], preferred_element_type=F32)
                for t in range(3):
                    dh = dh + jnp.dot(a_refs[1][t, rs, :], wv[k1 + t * k3:k1 + (t + 1) * k3], preferred_element_type=F32)
            else:
                dh = (jnp.dot(a_refs[0][0, rs, :], b_ref[0], preferred_element_type=F32)
                      + jnp.dot(a_refs[0][1, rs, :], b_ref[1], preferred_element_type=F32))
            xx = x_ref[rs, :]
            r = lax.rsqrt(_mean(xx * xx) + EPS)
            xn = xx * r
            dsh_ref[...] += _rsum(dh)
            dsc_ref[...] += _rsum(dh * (xn * gg))
            dt = dh * (1.0 + mod_ref[sc_row:sc_row + 1, :])
            dg_ref[...] += _rsum(dt * xn)
            dxn = dt * gg
            dx = dres_ref[rs, :] + r * (dxn - xn * _mean(dxn * xn))
            dx_ref[rs, :] = dx
            if gated:
                dga_ref[...] += _rsum(dx * y_ref[rs, :])
                dy_ref[rs, :] = (dx * mod_ref[ga_row:ga_row + 1, :]).astype(BF16)

    vec = jax.ShapeDtypeStruct((1, D_MODEL), F32)
    in_specs = a_specs + [b_spec, rows(D_MODEL), rows(D_MODEL), _vec_spec(D_MODEL), _vec_spec(D_MODEL, 8)]
    out_shape = [jax.ShapeDtypeStruct((SEQ, D_MODEL), F32), vec, vec, vec]
    out_specs = [rows(D_MODEL), _vec_spec(D_MODEL), _vec_spec(D_MODEL), _vec_spec(D_MODEL)]
    args = a_args + [b_arg, x, dres, g, mod]
    if gated:
        in_specs.append(rows(D_MODEL))
        out_shape += [jax.ShapeDtypeStruct((SEQ, D_MODEL), BF16), vec]
        out_specs += [rows(D_MODEL), _vec_spec(D_MODEL)]
        args.append(y)
    body, more_specs, more = _with_after(body, len(args), after)
    return pl.pallas_call(
        body, out_shape=tuple(out_shape), grid=(SEQ // tm,), in_specs=in_specs + more_specs, out_specs=tuple(out_specs),
        name=name, compiler_params=_cp("arbitrary"))(*args, *more)


def _prev_halo(halo, width, col):
    per = ROWS // halo
    return pl.BlockSpec((halo, width), lambda i: (jnp.maximum(i * per - 1, 0), col))


def _next_halo(halo, width, col):
    per = ROWS // halo
    last = SEQ // halo - 1
    return pl.BlockSpec((halo, width), lambda i: (jnp.minimum((i + 1) * per, last), col))


CONV_PAD = ROWS + CONV_HALO


def _shift_copies(sh):
    for b in range(1, 8):
        sh[b, 0:CONV_PAD - 8, :] = sh[0, pl.ds(b, CONV_PAD - 8), :]


def _tap(sh, rs_start, offset):
    return sh[offset % 8, pl.ds(pl.multiple_of(rs_start + (offset // 8) * 8, 8), SUB), :]


def _conv_glu(av_ref, ag_ref, avh_ref, agh_ref, sh):
    i = pl.program_id(0)
    hv = avh_ref[...] * _sig(agh_ref[...])
    sh[0, 0:CONV_HALO, :] = jnp.where(i > 0, hv, 0.0)

    def glu(rs):
        sh[0, pl.ds(pl.multiple_of(rs.start + CONV_HALO, SUB), SUB), :] = av_ref[rs, :] * _sig(ag_ref[rs, :])

    _for_chunks(glu)
    _shift_copies(sh)


def _conv_norm(u1, lg_ref, lb_ref):
    mu = _mean(u1)
    cen = u1 - mu
    rs = lax.rsqrt(_mean(cen * cen) + EPS)
    z = cen * rs
    ln = z * lg_ref[...] + lb_ref[...]
    s = _sig(ln)
    return z, rs, ln, s, ln * s


def conv_module_fwd(proj, wc, bc, lg, lb, gco, name):
    def body(av_ref, ag_ref, avh_ref, agh_ref, wc_ref, bc_ref, lg_ref, lb_ref, gco_ref, out_ref, u1_ref, sh):
        _conv_glu(av_ref, ag_ref, avh_ref, agh_ref, sh)

        def conv(rs):
            u1 = jnp.broadcast_to(bc_ref[...], (SUB, D_CONV))
            for j in range(CONV_K):
                u1 = u1 + wc_ref[j:j + 1, :] * _tap(sh, rs.start, CONV_HALO - (CONV_K - 1) + j)
            u1_ref[rs, :] = u1

        _for_chunks(conv)
        _, _, _, _, u2 = _conv_norm(u1_ref[...], lg_ref, lb_ref)
        rc = lax.rsqrt(_mean(u2 * u2) + EPS)
        out_ref[...] = (u2 * rc * gco_ref[...]).astype(BF16)

    v = _vec_spec(D_CONV)
    return pl.pallas_call(
        body, out_shape=(jax.ShapeDtypeStruct((SEQ, D_CONV), BF16), jax.ShapeDtypeStruct((SEQ, D_CONV), F32)),
        grid=(SEQ // ROWS,),
        in_specs=[_row_spec(D_CONV, 0), _row_spec(D_CONV, 1), _prev_halo(CONV_HALO, D_CONV, 0),
                  _prev_halo(CONV_HALO, D_CONV, 1), _vec_spec(D_CONV, CONV_K), v, v, v, v],
        out_specs=(_row_spec(D_CONV), _row_spec(D_CONV)), scratch_shapes=[pltpu.VMEM((8, CONV_PAD, D_CONV), F32)],
        name=name, compiler_params=_cp("parallel"))(proj, proj, proj, proj, wc, bc, lg, lb, gco)


def conv_module_bwd_a(proj, u1, dmixed, lg, lb, gco, name):
    def body(av_ref, ag_ref, avh_ref, agh_ref, u1_ref, dm_ref, lg_ref, lb_ref, gco_ref,
             du1_ref, dgco_ref, dlg_ref, dlb_ref, dbc_ref, dwc_ref, sh, acc):
        first = pl.program_id(0) == 0
        _conv_glu(av_ref, ag_ref, avh_ref, agh_ref, sh)
        z, rs, ln, s, u2 = _conv_norm(u1_ref[...], lg_ref, lb_ref)
        rc = lax.rsqrt(_mean(u2 * u2) + EPS)
        xn = u2 * rc
        dm = dm_ref[...]
        _acc(dgco_ref, _rsum(dm * xn), first)
        dyn = dm * gco_ref[...]
        du2 = rc * (dyn - xn * _mean(dyn * xn))
        dln = du2 * (s * (1.0 + ln * (1.0 - s)))
        _acc(dlg_ref, _rsum(dln * z), first)
        _acc(dlb_ref, _rsum(dln), first)
        dz = dln * lg_ref[...]
        du1 = rs * (dz - _mean(dz) - z * _mean(dz * z))
        du1_ref[...] = du1
        _acc(dbc_ref, _rsum(du1), first)
        acc[...] = jnp.zeros_like(acc)

        def taps(rs):
            d = du1_ref[rs, :]
            for j in range(CONV_K):
                acc[j] += d * _tap(sh, rs.start, CONV_HALO - (CONV_K - 1) + j)

        _for_chunks(taps)

        @pl.when(first)
        def _():
            dwc_ref[...] = jnp.zeros_like(dwc_ref)

        for j in range(CONV_K):
            dwc_ref[j:j + 1, :] += _rsum(acc[j])

    v = _vec_spec(D_CONV)
    vec = jax.ShapeDtypeStruct((1, D_CONV), F32)
    return pl.pallas_call(
        body,
        out_shape=(jax.ShapeDtypeStruct((SEQ, D_CONV), F32), vec, vec, vec, vec, jax.ShapeDtypeStruct((CONV_K, D_CONV), F32)),
        grid=(SEQ // ROWS,),
        in_specs=[_row_spec(D_CONV, 0), _row_spec(D_CONV, 1), _prev_halo(CONV_HALO, D_CONV, 0),
                  _prev_halo(CONV_HALO, D_CONV, 1), _row_spec(D_CONV, 0), _row_spec(D_CONV, 0), v, v, v],
        out_specs=(_row_spec(D_CONV), v, v, v, v, _vec_spec(D_CONV, CONV_K)),
        scratch_shapes=[pltpu.VMEM((8, CONV_PAD, D_CONV), F32), pltpu.VMEM((CONV_K, SUB, D_CONV), F32)],
        name=name, compiler_params=_cp("arbitrary"))(proj, proj, proj, proj, u1, dmixed, lg, lb, gco)


def conv_module_bwd_b(proj, du1, wc, name):
    def body(av_ref, ag_ref, du1_ref, du1n_ref, wc_ref, out_ref, sh):
        i = pl.program_id(0)
        sh[0, 0:ROWS, :] = du1_ref[...]
        sh[0, ROWS:, :] = jnp.where(i < SEQ // ROWS - 1, du1n_ref[...], 0.0)
        _shift_copies(sh)

        def chunk(rs):
            du0 = jnp.zeros((SUB, D_CONV), F32)
            for j in range(CONV_K):
                du0 = du0 + wc_ref[j:j + 1, :] * _tap(sh, rs.start, CONV_K - 1 - j)
            sg = _sig(ag_ref[rs, :])
            out_ref[rs, 0:D_CONV] = (du0 * sg).astype(BF16)
            out_ref[rs, D_CONV:] = (du0 * av_ref[rs, :] * sg * (1.0 - sg)).astype(BF16)

        _for_chunks(chunk)

    return pl.pallas_call(
        body, out_shape=jax.ShapeDtypeStruct((SEQ, 2 * D_CONV), BF16), grid=(SEQ // ROWS,),
        in_specs=[_row_spec(D_CONV, 0), _row_spec(D_CONV, 1), _row_spec(D_CONV, 0), _next_halo(CONV_HALO, D_CONV, 0),
                  _vec_spec(D_CONV, CONV_K)],
        out_specs=_row_spec(2 * D_CONV), scratch_shapes=[pltpu.VMEM((8, CONV_PAD, D_CONV), F32)],
        name=name, compiler_params=_cp("parallel"))(proj, proj, du1, du1, wc)


def _rows(start, size, r):
    return pl.ds(start, size) if r == 1 else pl.ds(start, size, stride=r)


def _unit_rows(r, rho, n, nb):
    win = 2 * ATTN_BLOCK if nb > 1 else ATTN_BLOCK
    if isinstance(n, int):
        kb = max(n - 1, 0)
        q_rows = _rows(rho + r * ATTN_BLOCK * n, ATTN_BLOCK, r)
        k_rows = _rows(rho + r * ATTN_BLOCK * kb, win, r)
    else:
        kb = jnp.maximum(n - 1, 0)
        q_rows = pl.ds(pl.multiple_of(n * ATTN_BLOCK, ATTN_BLOCK), ATTN_BLOCK)
        k_rows = pl.ds(pl.multiple_of(kb * ATTN_BLOCK, ATTN_BLOCK), win)
    return q_rows, k_rows, n - kb


def _band_bias(first_block, transposed):
    shape = (2 * ATTN_BLOCK, ATTN_BLOCK) if transposed else (ATTN_BLOCK, 2 * ATTN_BLOCK)
    q_axis = 1 if transposed else 0
    dist = (0 if first_block else ATTN_BLOCK) + lax.broadcasted_iota(jnp.int32, shape, q_axis) \
        - lax.broadcasted_iota(jnp.int32, shape, 1 - q_axis)
    return jnp.where((dist >= 0) & (dist <= ATTN_BLOCK), 0.0, NEG)


def _per_head(x):
    lane = lax.broadcasted_iota(jnp.int32, x.shape, 1)
    zero = jnp.zeros_like(x)
    return [jnp.where(lane < HEAD_DIM, x, zero), jnp.where(lane >= HEAD_DIM, x, zero)]


SCALE = HEAD_DIM ** -0.5


def _masked_scores(q2, k2, bias):
    return [lax.dot_general(qh, k2, NT, preferred_element_type=F32) + bias for qh in _per_head(q2)]


def _attn_units(r, nb, unit):
    if r == 1:
        def four(i, carry):
            for k in range(4):
                unit(0, 4 * i + k)
            return carry
        lax.fori_loop(0, nb // 4, four, 0)
    else:
        for rho in range(r):
            for n in range(nb):
                unit(rho, n)


N_UNITS = 16


def attn_fwd_all(proj, name):
    def body(q_ref, k_ref, v_ref, att_ref, lse_ref, s_scr, p_scr, lse_scr, den_scr, bias_scr):
        bias_scr[0] = _band_bias(True, False)
        bias_scr[1] = _band_bias(False, False)
        for idx, (sub_len, r) in enumerate(PATTERNS):
            nb = sub_len // ATTN_BLOCK
            win = 2 * ATTN_BLOCK if nb > 1 else ATTN_BLOCK

            def scores(rho, n, r=r, nb=nb, win=win):
                u = rho * nb + n
                q_rows, k_rows, variant = _unit_rows(r, rho, n, nb)
                ss = _masked_scores((q_ref[q_rows, :] * SCALE).astype(BF16), k_ref[k_rows, :].astype(BF16),
                                    bias_scr[variant, :, 0:win])
                for h in range(2):
                    s_scr[2 * u + h, :, 0:win] = ss[h]

            _attn_units(r, nb, scores)

            def softmax(u, carry, win=win):
                lses, dens = [], []
                for h in range(2):
                    sc = s_scr[2 * u + h, :, 0:win]
                    m = jnp.max(sc, axis=1, keepdims=True)
                    p = jnp.exp(sc - m)
                    den = jnp.sum(p, axis=1, keepdims=True)
                    p_scr[2 * u + h, :, 0:win] = p.astype(BF16)
                    lses.append(jnp.broadcast_to(m + jnp.log(den), (ATTN_BLOCK, HEAD_DIM)))
                    dens.append(jnp.broadcast_to(den, (ATTN_BLOCK, HEAD_DIM)))
                lse_scr[u] = jnp.concatenate(lses, axis=1)
                den_scr[u] = jnp.concatenate(dens, axis=1)
                return carry

            lax.fori_loop(0, N_UNITS, softmax, 0, unroll=2)

            def outputs(rho, n, r=r, nb=nb, win=win, idx=idx):
                u = rho * nb + n
                q_rows, k_rows, _ = _unit_rows(r, rho, n, nb)
                vs = _per_head(v_ref[k_rows, :].astype(BF16))
                o = (jnp.dot(p_scr[2 * u, :, 0:win], vs[0], preferred_element_type=F32)
                     + jnp.dot(p_scr[2 * u + 1, :, 0:win], vs[1], preferred_element_type=F32)) / den_scr[u]
                lse = lse_scr[u]
                if idx > 0:
                    old = lse_ref[q_rows, :]
                    top = jnp.maximum(old, lse)
                    new = top + jnp.log(jnp.exp(old - top) + jnp.exp(lse - top))
                    o = att_ref[q_rows, :] * jnp.exp(old - new) + o * jnp.exp(lse - new)
                    lse = new
                att_ref[q_rows, :] = o
                lse_ref[q_rows, :] = lse

            _attn_units(r, nb, outputs)

    blk = lambda first: pl.BlockSpec((SEQ, 128), lambda g: (0, first + g))
    shp = jax.ShapeDtypeStruct((SEQ, D_ATTN), F32)
    big = (2 * N_UNITS, ATTN_BLOCK, 2 * ATTN_BLOCK)
    small = pltpu.VMEM((N_UNITS, ATTN_BLOCK, 128), F32)
    return pl.pallas_call(
        body, out_shape=(shp, shp), grid=(4,), in_specs=[blk(8), blk(12), blk(16)], out_specs=(blk(0), blk(0)),
        scratch_shapes=[pltpu.VMEM(big, F32), pltpu.VMEM(big, BF16), small, small,
                        pltpu.VMEM((2, ATTN_BLOCK, 2 * ATTN_BLOCK), F32)],
        name=name, compiler_params=_cp("parallel"))(proj, proj, proj)


def attn_bwd_all(proj, do, lse, dd, name):
    def body(q_ref, k_ref, v_ref, do_ref, l_ref, dd_ref, out_ref, dq_s, dk_s, dv_s,
             s_scr, dp_scr, ds_scr, st_scr, dpt_scr, pt_scr, dst_scr, qb_scr, kb_scr, dob_scr, bias_scr, bias_t_scr):
        for first_block in (True, False):
            bias_scr[1 - int(first_block)] = _band_bias(first_block, False)
            bias_t_scr[1 - int(first_block)] = _band_bias(first_block, True)
        dq_s[...] = jnp.zeros_like(dq_s)
        dk_s[...] = jnp.zeros_like(dk_s)
        dv_s[...] = jnp.zeros_like(dv_s)
        for sub_len, r in PATTERNS:
            nb = sub_len // ATTN_BLOCK
            win = 2 * ATTN_BLOCK if nb > 1 else ATTN_BLOCK

            def scores(rho, n, r=r, nb=nb, win=win):
                u = rho * nb + n
                q_rows, k_rows, variant = _unit_rows(r, rho, n, nb)
                bias, bias_t = bias_scr[variant, :, 0:win], bias_t_scr[variant, 0:win, :]
                q2 = (q_ref[q_rows, :] * SCALE).astype(BF16)
                kf = k_ref[k_rows, :]
                k2 = kf.astype(BF16)
                do2 = do_ref[q_rows, :].astype(BF16)
                qb_scr[u] = q2
                kb_scr[u, 0:win, :] = (kf * SCALE).astype(BF16)
                dob_scr[u] = do2
                l2 = l_ref[q_rows, :]
                d2 = dd_ref[q_rows, :]
                l2t = l2.T
                d2t = d2.T
                v2 = v_ref[k_rows, :].astype(BF16)
                qs, dos = _per_head(q2), _per_head(do2)
                for h in range(2):
                    c0 = h * HEAD_DIM
                    sc = lax.dot_general(qs[h], k2, NT, preferred_element_type=F32)
                    s_scr[2 * u + h, :, 0:win] = sc + bias - l2[:, c0:c0 + 1]
                    dp_scr[2 * u + h, :, 0:win] = lax.dot_general(dos[h], v2, NT, preferred_element_type=F32) \
                        - d2[:, c0:c0 + 1]
                    sct = lax.dot_general(k2, qs[h], NT, preferred_element_type=F32)
                    st_scr[2 * u + h, 0:win, :] = sct + bias_t - l2t[c0:c0 + 1, :]
                    dpt_scr[2 * u + h, 0:win, :] = lax.dot_general(v2, dos[h], NT, preferred_element_type=F32) \
                        - d2t[c0:c0 + 1, :]

            _attn_units(r, nb, scores)

            def pointwise(hu, carry, win=win):
                ds_scr[hu, :, 0:win] = (jnp.exp(s_scr[hu, :, 0:win]) * dp_scr[hu, :, 0:win]).astype(BF16)
                pt = jnp.exp(st_scr[hu, 0:win, :])
                pt_scr[hu, 0:win, :] = pt.astype(BF16)
                dst_scr[hu, 0:win, :] = (pt * dpt_scr[hu, 0:win, :]).astype(BF16)
                return carry

            lax.fori_loop(0, 2 * N_UNITS, pointwise, 0, unroll=4)

            def grads(rho, n, r=r, nb=nb, win=win):
                u = rho * nb + n
                q_rows, k_rows, _ = _unit_rows(r, rho, n, nb)
                qs, ks, dos = _per_head(qb_scr[u]), _per_head(kb_scr[u, 0:win, :]), _per_head(dob_scr[u])

                def both(scr, rows, rhs):
                    return (jnp.dot(scr[(2 * u,) + rows], rhs[0], preferred_element_type=F32)
                            + jnp.dot(scr[(2 * u + 1,) + rows], rhs[1], preferred_element_type=F32))

                dq_s[q_rows, :] += both(ds_scr, (slice(None), slice(0, win)), ks)
                dk_s[k_rows, :] += both(dst_scr, (slice(0, win), slice(None)), qs)
                dv_s[k_rows, :] += both(pt_scr, (slice(0, win), slice(None)), dos)

            _attn_units(r, nb, grads)
        out_ref[0] = dq_s[...].astype(BF16)
        out_ref[1] = dk_s[...].astype(BF16)
        out_ref[2] = dv_s[...].astype(BF16)

    blk = lambda first: pl.BlockSpec((SEQ, 128), lambda g: (0, first + g))
    acc = pltpu.VMEM((SEQ, 128), F32)
    big = (2 * N_UNITS, ATTN_BLOCK, 2 * ATTN_BLOCK)
    big_t = (2 * N_UNITS, 2 * ATTN_BLOCK, ATTN_BLOCK)
    return pl.pallas_call(
        body, out_shape=jax.ShapeDtypeStruct((3, SEQ, D_ATTN), BF16), grid=(4,),
        in_specs=[blk(8), blk(12), blk(16), blk(0), blk(0), blk(0)],
        out_specs=pl.BlockSpec((3, SEQ, 128), lambda g: (0, 0, g)),
        scratch_shapes=[acc, acc, acc, pltpu.VMEM(big, F32), pltpu.VMEM(big, F32), pltpu.VMEM(big, BF16),
                        pltpu.VMEM(big_t, F32), pltpu.VMEM(big_t, F32), pltpu.VMEM(big_t, BF16), pltpu.VMEM(big_t, BF16),
                        pltpu.VMEM((N_UNITS, ATTN_BLOCK, 128), BF16),
                        pltpu.VMEM((N_UNITS, 2 * ATTN_BLOCK, 128), BF16), pltpu.VMEM((N_UNITS, ATTN_BLOCK, 128), BF16),
                        pltpu.VMEM((2, ATTN_BLOCK, 2 * ATTN_BLOCK), F32), pltpu.VMEM((2, 2 * ATTN_BLOCK, ATTN_BLOCK), F32)],
        name=name, compiler_params=_cp("parallel"))(proj, proj, proj, do, lse, dd)


N_FT = D_FF // FFN_TN


def _ffn_specs():
    per = ROWS // FFN_HALO
    cur_g = pl.BlockSpec((ROWS, FFN_TN), lambda j, i: (i, j))
    cur_v = pl.BlockSpec((ROWS, FFN_TN), lambda j, i: (i, j + N_FT))
    halo_g = pl.BlockSpec((FFN_HALO, FFN_TN), lambda j, i: (jnp.maximum(i * per - 1, 0), j))
    halo_v = pl.BlockSpec((FFN_HALO, FFN_TN), lambda j, i: (jnp.maximum(i * per - 1, 0), j + N_FT))
    w_g = pl.BlockSpec((FFN_K, FFN_TN), lambda j, i: (0, j))
    w_v = pl.BlockSpec((FFN_K, FFN_TN), lambda j, i: (0, j + N_FT))
    b_g = pl.BlockSpec((1, FFN_TN), lambda j, i: (0, j))
    b_v = pl.BlockSpec((1, FFN_TN), lambda j, i: (0, j + N_FT))
    return [cur_g, cur_v, halo_g, halo_v, w_g, w_v, b_g, b_v]


def matmul(a, b, kind, out_dtype, tm, tn, name, b_rows=None):
    stacked = b_rows is not None
    b_shape = (N_DEV * b_rows, b.shape[2]) if stacked else b.shape
    if kind == "nn":
        (m, k), n = a.shape, b_shape[1]
        a_spec = pl.BlockSpec((tm, k), lambda j, i: (i, 0))
        b_spec = pl.BlockSpec((k, tn), lambda j, i: (0, j))
        dims = (((1,), (0,)), ((), ()))
    elif kind == "nt":
        (m, k), n = a.shape, b_shape[0]
        a_spec = pl.BlockSpec((tm, k), lambda j, i: (i, 0))
        b_spec = pl.BlockSpec((tn, k), lambda j, i: (j, 0))
        dims = (((1,), (1,)), ((), ()))
    else:
        (k, m), n = a.shape, b_shape[1]
        a_spec = pl.BlockSpec((k, tm), lambda j, i: (0, i))
        b_spec = pl.BlockSpec((k, tn), lambda j, i: (0, j))
        dims = (((0,), (0,)), ((), ()))
    assert m % tm == 0 and n % tn == 0, (name, m, n, tm, tn)
    if stacked:
        assert b_spec.block_shape[0] == b_shape[0] and kind in ("nn", "nt")
        width = b_spec.block_shape[1]
        b_spec = pl.BlockSpec((N_DEV, b_rows, width), (lambda j, i: (0, 0, j)) if kind == "nn" else (lambda j, i: (0, 0, 0)))

    def body(a_ref, b_ref, o_ref):
        bb = b_ref[...].reshape(b_shape[0], -1) if stacked else b_ref[...]
        o_ref[...] = lax.dot_general(a_ref[...], bb, dims, preferred_element_type=F32).astype(o_ref.dtype)

    return pl.pallas_call(
        body, out_shape=jax.ShapeDtypeStruct((m, n), out_dtype), grid=(n // tn, m // tm),
        in_specs=[a_spec, b_spec], out_specs=pl.BlockSpec((tm, tn), lambda j, i: (i, j)),
        name=name, compiler_params=_cp("parallel", "parallel"))(a, b)


def matmul_tn_pieces(a1, a3, b, name):
    k, n = b.shape
    tm = a3.shape[2]
    n1 = a1.shape[1] // tm

    def body(a1_ref, a3_ref, b_ref, o_ref):
        i = pl.program_id(0)
        tn_dims = (((0,), (0,)), ((), ()))

        @pl.when(i < n1)
        def _():
            o_ref[...] = lax.dot_general(a1_ref[...], b_ref[...], tn_dims, preferred_element_type=F32).astype(BF16)

        @pl.when(i >= n1)
        def _():
            o_ref[...] = lax.dot_general(a3_ref[0], b_ref[...], tn_dims, preferred_element_type=F32).astype(BF16)

    return pl.pallas_call(
        body, out_shape=jax.ShapeDtypeStruct((a1.shape[1] + 3 * tm, n), BF16), grid=(n1 + 3,),
        in_specs=[pl.BlockSpec((k, tm), lambda i: (0, jnp.minimum(i, n1 - 1))),
                  pl.BlockSpec((1, k, tm), lambda i: (jnp.maximum(i - n1, 0), 0, 0)),
                  pl.BlockSpec((k, n), lambda i: (0, 0))],
        out_specs=pl.BlockSpec((tm, n), lambda i: (i, 0)), name=name, compiler_params=_cp("parallel"))(a1, a3, b)


def matmul_tn_halves(a, b, tm, name):
    _, k, m = a.shape
    n = b.shape[1]

    def body(a_ref, b_ref, o_ref):
        o_ref[0] = lax.dot_general(a_ref[0], b_ref[...], (((0,), (0,)), ((), ())), preferred_element_type=F32).astype(BF16)

    return pl.pallas_call(
        body, out_shape=jax.ShapeDtypeStruct((2, m, n), BF16), grid=(2, m // tm),
        in_specs=[pl.BlockSpec((1, k, tm), lambda h, i: (h, 0, i)), pl.BlockSpec((k, n), lambda h, i: (0, 0))],
        out_specs=pl.BlockSpec((1, tm, n), lambda h, i: (h, i, 0)), name=name,
        compiler_params=_cp("parallel", "parallel"))(a, b).reshape(2 * m, n)


def _row_spec(width, col=0):
    return pl.BlockSpec((ROWS, width), lambda i: (i, col))


def _vec_spec(width, rows=1):
    return pl.BlockSpec((rows, width), lambda i: (0, 0))


FFN_RB = 128


def _ffn_lane_blocks(fn):
    for c in range(FFN_TN // 128):
        fn(slice(c * 128, (c + 1) * 128))


def _ffn_taps(cur_ref, halo_ref, head, ls, r0):
    if r0 == 0:
        head[0:FFN_HALO, :] = jnp.where(pl.program_id(1) > 0, halo_ref[:, ls], 0.0)
        head[FFN_HALO:, :] = cur_ref[0:FFN_RB, ls]
        return tuple(head[pl.ds(FFN_HALO - k, FFN_RB), :] for k in (2, 1, 0))
    return tuple(cur_ref[pl.ds(r0 - k, FFN_RB), ls] for k in (2, 1, 0))


def _ffn_conv(taps, w, b):
    return b + w[0] * taps[0] + w[1] * taps[1] + w[2] * taps[2]


def ffn_act_fwd(up0, wf, bf, name, after=None):
    def body(g_ref, v_ref, gh_ref, vh_ref, wg_ref, wv_ref, bg_ref, bv_ref, act_ref, head_g, head_v):
        def lanes(ls):
            wg = [wg_ref[t:t + 1, ls] for t in range(FFN_K)]
            wv = [wv_ref[t:t + 1, ls] for t in range(FFN_K)]
            for r0 in range(0, ROWS, FFN_RB):
                gate = _ffn_conv(_ffn_taps(g_ref, gh_ref, head_g, ls, r0), wg, bg_ref[:, ls])
                val = _ffn_conv(_ffn_taps(v_ref, vh_ref, head_v, ls, r0), wv, bv_ref[:, ls])
                act_ref[r0:r0 + FFN_RB, ls] = (gate * _sig(gate) * val).astype(BF16)

        _ffn_lane_blocks(lanes)

    head = pltpu.VMEM((FFN_HALO + FFN_RB, 128), F32)
    body, more_specs, more = _with_after(body, 8, after)
    return pl.pallas_call(
        body, out_shape=jax.ShapeDtypeStruct((SEQ, D_FF), BF16), grid=(N_FT, SEQ // ROWS),
        in_specs=_ffn_specs() + more_specs, out_specs=pl.BlockSpec((ROWS, FFN_TN), lambda j, i: (i, j)),
        scratch_shapes=[head, head], name=name,
        compiler_params=_cp("parallel", "parallel"))(up0, up0, up0, up0, wf, wf, bf, bf, *more)


def ffn_bwd(up0, dact, wf, bf, name, after=None):
    per = ROWS // FFN_HALO
    last = SEQ // FFN_HALO - 1

    def body(g_ref, v_ref, gh_ref, vh_ref, wg_ref, wv_ref, bg_ref, bv_ref, da_ref, gn_ref, vn_ref, dan_ref,
             out_ref, dbg_ref, dbv_ref, dwg_ref, dwv_ref, pad, head_g, head_v, dgp, dvp, acc):
        i = pl.program_id(1)
        first = i == 0
        acc[...] = jnp.zeros_like(acc)

        def grads(gate, val, da):
            s = _sig(gate)
            return da * val * (s * (1.0 + gate * (1.0 - s))), da * (gate * s)

        fold = lambda q: jnp.sum(q.reshape(FFN_RB // 8, 8, 128), axis=0)

        def lanes(ls):
            wg = [wg_ref[t:t + 1, ls] for t in range(FFN_K)]
            wv = [wv_ref[t:t + 1, ls] for t in range(FFN_K)]
            sums = [jnp.zeros((8, 128), F32)] * (2 + 2 * FFN_K)
            for r0 in range(0, ROWS, FFN_RB):
                gs = _ffn_taps(g_ref, gh_ref, head_g, ls, r0)
                vs = _ffn_taps(v_ref, vh_ref, head_v, ls, r0)
                dgate, dval = grads(_ffn_conv(gs, wg, bg_ref[:, ls]), _ffn_conv(vs, wv, bv_ref[:, ls]),
                                    da_ref[r0:r0 + FFN_RB, ls])
                dgp[r0:r0 + FFN_RB, ls] = dgate
                dvp[r0:r0 + FFN_RB, ls] = dval
                new = [dgate, dval] + [dgate * gs[t] for t in range(FFN_K)] + [dval * vs[t] for t in range(FFN_K)]
                sums = [a + fold(q) for a, q in zip(sums, new)]
            for k in range(2 + 2 * FFN_K):
                acc[k, 0:8, ls] = sums[k]

        _ffn_lane_blocks(lanes)
        _acc(dbg_ref, _rsum(acc[0]), first)
        _acc(dbv_ref, _rsum(acc[1]), first)
        _acc(dwg_ref, jnp.concatenate([_rsum(acc[2 + t]) for t in range(FFN_K)], axis=0), first)
        _acc(dwv_ref, jnp.concatenate([_rsum(acc[5 + t]) for t in range(FFN_K)], axis=0), first)

        def conv_next(cur_ref, nxt_ref, w_ref, b_ref):
            pad[0:FFN_HALO, :] = cur_ref[ROWS - FFN_HALO:, :]
            pad[FFN_HALO:, :] = nxt_ref[...]
            return (b_ref[...] + w_ref[0:1, :] * pad[pl.ds(FFN_HALO - 2, FFN_HALO), :]
                    + w_ref[1:2, :] * pad[pl.ds(FFN_HALO - 1, FFN_HALO), :] + w_ref[2:3, :] * nxt_ref[...])

        gate_n = conv_next(g_ref, gn_ref, wg_ref, bg_ref)
        val_n = conv_next(v_ref, vn_ref, wv_ref, bv_ref)
        dgate_n, dval_n = grads(gate_n, val_n, dan_ref[...])
        inside = i < SEQ // ROWS - 1
        dgp[ROWS:, :] = jnp.where(inside, dgate_n, 0.0)
        dvp[ROWS:, :] = jnp.where(inside, dval_n, 0.0)

        def back(ls):
            for half, (dp, w_ref) in enumerate(((dgp, wg_ref), (dvp, wv_ref))):
                w = [w_ref[t:t + 1, ls] for t in range(FFN_K)]
                for r0 in range(0, ROWS, FFN_RB):
                    out_ref[half, r0:r0 + FFN_RB, ls] = (
                        w[2] * dp[r0:r0 + FFN_RB, ls] + w[1] * dp[pl.ds(r0 + 1, FFN_RB), ls]
                        + w[0] * dp[pl.ds(r0 + 2, FFN_RB), ls]).astype(BF16)

        _ffn_lane_blocks(back)

    body, more_specs, more = _with_after(body, 12, after)
    head = pltpu.VMEM((FFN_HALO + FFN_RB, 128), F32)
    ext = pltpu.VMEM((ROWS + FFN_HALO, FFN_TN), F32)
    vec = jax.ShapeDtypeStruct((1, D_FF), F32)
    taps = jax.ShapeDtypeStruct((FFN_K, D_FF), F32)
    cur = pl.BlockSpec((ROWS, FFN_TN), lambda j, i: (i, j))
    nxt = lambda off: pl.BlockSpec((FFN_HALO, FFN_TN), lambda j, i: (jnp.minimum((i + 1) * per, last), j + off))
    vs = pl.BlockSpec((1, FFN_TN), lambda j, i: (0, j))
    ts = pl.BlockSpec((FFN_K, FFN_TN), lambda j, i: (0, j))
    return pl.pallas_call(
        body, out_shape=(jax.ShapeDtypeStruct((2, SEQ, D_FF), BF16), vec, vec, taps, taps), grid=(N_FT, SEQ // ROWS),
        in_specs=_ffn_specs() + [cur, nxt(0), nxt(N_FT), nxt(0)] + more_specs,
        out_specs=(pl.BlockSpec((2, ROWS, FFN_TN), lambda j, i: (0, i, j)), vs, vs, ts, ts),
        scratch_shapes=[pltpu.VMEM((2 * FFN_HALO, FFN_TN), F32), head, head, ext, ext,
                        pltpu.VMEM((2 + 2 * FFN_K, SUB, FFN_TN), F32)],
        name=name, compiler_params=_cp("parallel", "arbitrary"))(up0, up0, up0, up0, wf, wf, bf, bf, dact, up0, up0, dact,
                                                                 *more)


def _adam(w, g, m, v):
    m = ADAM_B1 * m + (1.0 - ADAM_B1) * g
    v = ADAM_B2 * v + (1.0 - ADAM_B2) * (g * g)
    m_hat = m / (1.0 - ADAM_B1 ** ADAM_STEP)
    v_hat = v / (1.0 - ADAM_B2 ** ADAM_STEP)
    delta = -ADAM_LR * (m_hat / (jnp.sqrt(v_hat) + ADAM_EPS) + ADAM_WD * w)
    return delta, m, v


def ada_bwd_adamw(c_all, dmod_cols, w, m, v, name):
    rows, cols = w.shape
    tr = 256

    def body(c_ref, dm_ref, w_ref, m_ref, v_ref, g_ref, d_ref, nm_ref, nv_ref):
        cc = c_ref[...]
        sc = (cc * _sig(cc)).T
        g = sc[:, 0:1] * dm_ref[0:1, :]
        for b in range(1, N_DEV):
            g = g + sc[:, b:b + 1] * dm_ref[b:b + 1, :]
        g_ref[...] = g
        d_ref[...], nm_ref[...], nv_ref[...] = _adam(w_ref[...], g, m_ref[...], v_ref[...])

    blk = pl.BlockSpec((tr, cols), lambda i: (i, 0))
    shp = jax.ShapeDtypeStruct((rows, cols), F32)
    return pl.pallas_call(
        body, out_shape=(shp, shp, shp, shp), grid=(rows // tr,),
        in_specs=[pl.BlockSpec((N_DEV, tr), lambda i: (0, i)), pl.BlockSpec((N_DEV, cols), lambda i: (0, 0)), blk, blk, blk],
        out_specs=(blk, blk, blk, blk), name=name, compiler_params=_cp("parallel"))(c_all, dmod_cols, w, m, v)


def sum_adamw(parts, mine, me, w, m, v, tr, name):
    n_parts, rows, cols = parts.shape

    def body(me_ref, p_ref, own_ref, w_ref, m_ref, v_ref, g_ref, d_ref, nm_ref, nv_ref):
        def chunk(rs):
            g = own_ref[0, rs, :].astype(F32)
            for k in range(1, n_parts):
                g = g + p_ref[k, rs, :].astype(F32)
            g_ref[rs, :] = g
            d_ref[rs, :], nm_ref[rs, :], nv_ref[rs, :] = _adam(w_ref[rs, :], g, m_ref[rs, :], v_ref[rs, :])

        _for_chunks(chunk, 2, tr)

    blk = pl.BlockSpec((tr, cols), lambda i, me_ref: (i, 0))
    shp = jax.ShapeDtypeStruct((rows, cols), F32)
    grid_spec = pltpu.PrefetchScalarGridSpec(
        num_scalar_prefetch=1, grid=(rows // tr,),
        in_specs=[pl.BlockSpec((n_parts, tr, cols), lambda i, me_ref: (0, i, 0)),
                  pl.BlockSpec((1, tr, cols), lambda i, me_ref: (me_ref[0], i, 0)), blk, blk, blk],
        out_specs=(blk, blk, blk, blk))
    return pl.pallas_call(body, out_shape=(shp, shp, shp, shp), grid_spec=grid_spec, name=name,
                          compiler_params=_cp("parallel"))(me, parts, mine, w, m, v)


MESH = pl.DeviceIdType.MESH


def all_gather(block, name, after=None):
    extra = () if after is None else (after,)

    def body(x_ref, *refs):
        out_ref, send_sems, recv_sems, local_sem = refs[len(extra):]
        x, y, c = lax.axis_index("x"), lax.axis_index("y"), lax.axis_index("c")
        me, sibling = (x, y, c), (x, y, 1 - c)
        chips = [(1 - x, y), (x, 1 - y), (1 - x, 1 - y)]

        def slot(px, py, pc):
            return out_ref.at[4 * px + 2 * py + pc]

        def copy(k, blk, to, src=None):
            return pltpu.make_async_remote_copy(
                src_ref=slot(*blk) if src is None else src, dst_ref=slot(*blk),
                send_sem=send_sems.at[k], recv_sem=recv_sems.at[k], device_id=to, device_id_type=MESH)

        mine = pltpu.make_async_copy(x_ref, slot(*me), local_sem)
        mine.start()
        first = [copy(0, me, sibling, src=x_ref)]
        first += [copy(1 + j, me, (*chip, c), src=x_ref) for j, chip in enumerate(chips)]
        for cp in first:
            cp.start()
        passed = [copy(4 + j, (*chip, c), sibling) for j, chip in enumerate(chips)]
        for j, chip in enumerate(chips):
            copy(1 + j, (*chip, c), me).wait_recv()
            passed[j].start()
        copy(0, sibling, me).wait_recv()
        for j, chip in enumerate(chips):
            copy(4 + j, (*chip, 1 - c), me).wait_recv()
        for cp in first + passed:
            cp.wait_send()
        mine.wait()

    return pl.pallas_call(
        body, out_shape=jax.ShapeDtypeStruct((N_DEV,) + block.shape, block.dtype), in_specs=[ANY] * (1 + len(extra)), out_specs=ANY,
        scratch_shapes=[pltpu.SemaphoreType.DMA((7,)), pltpu.SemaphoreType.DMA((7,)), pltpu.SemaphoreType.DMA],
        name=name)(block, *extra)


def ada_gather(c_all, w_ada, b_cols, name):
    n = w_ada.shape[1]

    def body(c_ref, w_ref, b_ref, out_ref, x_ref, send_sems, recv_sems, local_sem):
        cc = c_ref[...]
        sc = (cc * _sig(cc)).astype(BF16)
        x_ref[...] = jnp.dot(sc, w_ref[...].astype(BF16), preferred_element_type=F32) + b_ref[...]
        x, y, c = lax.axis_index("x"), lax.axis_index("y"), lax.axis_index("c")
        me, sibling = (x, y, c), (x, y, 1 - c)
        chips = [(1 - x, y), (x, 1 - y), (1 - x, 1 - y)]

        def slot(px, py, pc):
            return out_ref.at[4 * px + 2 * py + pc]

        def copy(k, blk, to, src=None):
            return pltpu.make_async_remote_copy(
                src_ref=slot(*blk) if src is None else src, dst_ref=slot(*blk),
                send_sem=send_sems.at[k], recv_sem=recv_sems.at[k], device_id=to, device_id_type=MESH)

        mine = pltpu.make_async_copy(x_ref, slot(*me), local_sem)
        mine.start()
        first = [copy(0, me, sibling, src=x_ref)]
        first += [copy(1 + j, me, (*chip, c), src=x_ref) for j, chip in enumerate(chips)]
        for cp in first:
            cp.start()
        passed = [copy(4 + j, (*chip, c), sibling) for j, chip in enumerate(chips)]
        for j, chip in enumerate(chips):
            copy(1 + j, (*chip, c), me).wait_recv()
            passed[j].start()
        copy(0, sibling, me).wait_recv()
        for j, chip in enumerate(chips):
            copy(4 + j, (*chip, 1 - c), me).wait_recv()
        for cp in first + passed:
            cp.wait_send()
        mine.wait()

    vmem = pl.BlockSpec(memory_space=pltpu.VMEM)
    return pl.pallas_call(
        body, out_shape=jax.ShapeDtypeStruct((N_DEV, N_DEV, n), F32), in_specs=[vmem, vmem, vmem], out_specs=ANY,
        scratch_shapes=[pltpu.VMEM((N_DEV, n), F32), pltpu.SemaphoreType.DMA((7,)), pltpu.SemaphoreType.DMA((7,)),
                        pltpu.SemaphoreType.DMA],
        name=name, compiler_params=pltpu.CompilerParams(vmem_limit_bytes=VMEM_LIMIT))(c_all, w_ada, b_cols)


HBM = pl.BlockSpec(memory_space=pltpu.HBM)
SEM = pl.BlockSpec(memory_space=pltpu.SEMAPHORE)
EFFECT = pltpu.SideEffectType.DATAFLOW_SIDE_EFFECTING


def _peer_copies(src_ref, land_ref, send_sems, recv_sems, gather):
    x, y, c = lax.axis_index("x"), lax.axis_index("y"), lax.axis_index("c")
    me = 4 * x + 2 * y + c
    copies = []
    for k in range(1, N_DEV):
        px = 1 - x if k & 4 else x
        py = 1 - y if k & 2 else y
        pc = 1 - c if k & 1 else c
        copies.append(pltpu.make_async_remote_copy(
            src_ref=src_ref if gather else src_ref.at[4 * px + 2 * py + pc],
            dst_ref=land_ref.at[me] if gather else land_ref.at[k],
            send_sem=send_sems.at[k - 1], recv_sem=recv_sems.at[k - 1], device_id=(px, py, pc), device_id_type=MESH))
    return copies


def exchange_start(srcs, gather, name, after=None):
    n = len(srcs)
    land_shapes = [(N_DEV,) + src.shape if gather else src.shape for src in srcs]
    extra = () if after is None else (after,)

    def body(*refs):
        src_refs, land_refs = refs[0:n], refs[n:2 * n]
        outs = refs[2 * n + len(extra):]
        for k in range(n):
            for cp in _peer_copies(src_refs[k], land_refs[k], outs[4 * k], outs[4 * k + 1], gather):
                cp.start()
        token = outs[4 * n]
        token[...] = jnp.zeros_like(token)

    out_shape, out_specs, aliases = [], [], {}
    for k, src in enumerate(srcs):
        out_shape += [pltpu.SemaphoreType.DMA((N_DEV - 1,)), pltpu.SemaphoreType.DMA((N_DEV - 1,)),
                      pltpu.HBM(src.shape, src.dtype), pltpu.HBM(land_shapes[k], src.dtype)]
        out_specs += [SEM, SEM, HBM, HBM]
        aliases[k] = 4 * k + 2
        aliases[n + k] = 4 * k + 3
    out_shape.append(jax.ShapeDtypeStruct((8, 128), F32))
    out_specs.append(pl.BlockSpec(memory_space=pltpu.VMEM))
    res = pl.pallas_call(
        body, name=name, out_shape=tuple(out_shape), in_specs=(HBM,) * (2 * n) + (ANY,) * len(extra),
        out_specs=tuple(out_specs), input_output_aliases=aliases,
        compiler_params=pltpu.CompilerParams(has_side_effects=EFFECT),
    )(*[pltpu.with_memory_space_constraint(src, pltpu.HBM) for src in srcs],
      *[pltpu.with_memory_space_constraint(lax.empty(shp, src.dtype), pltpu.HBM) for shp, src in zip(land_shapes, srcs)],
      *extra)
    return [tuple(res[4 * k:4 * k + 4]) for k in range(n)], res[4 * n]


def _stage1_peer(i):
    x, y, c = lax.axis_index("x"), lax.axis_index("y"), lax.axis_index("c")
    if i == 0:
        return (x, y, 1 - c)
    return (1 - x if i & 1 else x, 1 - y if i & 2 else y, c)


def _slot_of(peer):
    return 4 * peer[0] + 2 * peer[1] + peer[2]


def _stage1_copy(i, src_ref, land_ref, send_sems, recv_sems):
    me = _slot_of((lax.axis_index("x"), lax.axis_index("y"), lax.axis_index("c")))
    return pltpu.make_async_remote_copy(src_ref=src_ref, dst_ref=land_ref.at[me], send_sem=send_sems.at[i],
                                        recv_sem=recv_sems.at[i], device_id=_stage1_peer(i), device_id_type=MESH)


def _stage2_copy(j, land_ref, send_sems, recv_sems):
    slot = _slot_of(_stage1_peer(j + 1))
    return pltpu.make_async_remote_copy(src_ref=land_ref.at[slot], dst_ref=land_ref.at[slot], send_sem=send_sems.at[j],
                                        recv_sem=recv_sems.at[j], device_id=_stage1_peer(0), device_id_type=MESH)


def gather2_start(srcs, name, after=None):
    n = len(srcs)
    extra = () if after is None else (after,)

    def body(*refs):
        src_refs, land_refs = refs[0:n], refs[n:2 * n]
        outs = refs[2 * n + len(extra):]
        for k in range(n):
            for i in range(4):
                _stage1_copy(i, src_refs[k], land_refs[k], outs[4 * k], outs[4 * k + 1]).start()
        outs[4 * n][...] = jnp.zeros((8, 128), F32)

    out_shape, out_specs, aliases = [], [], {}
    for k, src in enumerate(srcs):
        out_shape += [pltpu.SemaphoreType.DMA((4,)), pltpu.SemaphoreType.DMA((4,)),
                      pltpu.HBM(src.shape, src.dtype), pltpu.HBM((N_DEV,) + src.shape, src.dtype)]
        out_specs += [SEM, SEM, HBM, HBM]
        aliases[k] = 4 * k + 2
        aliases[n + k] = 4 * k + 3
    out_shape.append(jax.ShapeDtypeStruct((8, 128), F32))
    out_specs.append(pl.BlockSpec(memory_space=pltpu.VMEM))
    res = pl.pallas_call(
        body, name=name, out_shape=tuple(out_shape), in_specs=(HBM,) * (2 * n) + (ANY,) * len(extra),
        out_specs=tuple(out_specs), input_output_aliases=aliases,
        compiler_params=pltpu.CompilerParams(has_side_effects=EFFECT),
    )(*[pltpu.with_memory_space_constraint(src, pltpu.HBM) for src in srcs],
      *[pltpu.with_memory_space_constraint(lax.empty((N_DEV,) + src.shape, src.dtype), pltpu.HBM) for src in srcs], *extra)
    return [dict(send1=res[4 * k], recv1=res[4 * k + 1], src=res[4 * k + 2], land=res[4 * k + 3]) for k in range(n)], \
        res[4 * n]


def gather2_pass(handles, after, name):
    n = len(handles)

    def body(*refs):
        src_refs, land_refs, recv1 = refs[0:n], refs[n:2 * n], refs[2 * n:3 * n]
        outs = refs[3 * n + 1:]
        for k in range(n):
            for j in range(3):
                _stage1_copy(j + 1, src_refs[k], land_refs[k], recv1[k], recv1[k]).wait_recv()
                _stage2_copy(j, land_refs[k], outs[3 * k], outs[3 * k + 1]).start()
        outs[3 * n][...] = jnp.zeros((8, 128), F32)

    out_shape, out_specs, aliases = [], [], {}
    for k, h in enumerate(handles):
        out_shape += [pltpu.SemaphoreType.DMA((3,)), pltpu.SemaphoreType.DMA((3,)), pltpu.HBM(h["land"].shape, h["land"].dtype)]
        out_specs += [SEM, SEM, HBM]
        aliases[n + k] = 3 * k + 2
    out_shape.append(jax.ShapeDtypeStruct((8, 128), F32))
    out_specs.append(pl.BlockSpec(memory_space=pltpu.VMEM))
    res = pl.pallas_call(
        body, name=name, out_shape=tuple(out_shape), in_specs=(HBM,) * (2 * n) + (SEM,) * n + (ANY,),
        out_specs=tuple(out_specs), input_output_aliases=aliases,
        compiler_params=pltpu.CompilerParams(has_side_effects=EFFECT),
    )(*[h["src"] for h in handles], *[h["land"] for h in handles], *[h["recv1"] for h in handles], after)
    return [dict(h, send2=res[3 * k], recv2=res[3 * k + 1], land=res[3 * k + 2]) for k, h in enumerate(handles)], res[3 * n]


def gather2_wait(h, after, name):
    def body(src_ref, land_ref, send1, recv1, send2, recv2, after_ref, src_dead, got_ref):
        for i in range(4):
            _stage1_copy(i, src_ref, land_ref, send1, recv1).wait_send()
        _stage1_copy(0, src_ref, land_ref, send1, recv1).wait_recv()
        for j in range(3):
            cp = _stage2_copy(j, land_ref, send2, recv2)
            cp.wait_send()
            cp.wait_recv()

    return pl.pallas_call(
        body, name=name,
        out_shape=(pltpu.HBM(h["src"].shape, h["src"].dtype), pltpu.HBM(h["land"].shape, h["land"].dtype)),
        in_specs=(HBM, HBM, SEM, SEM, SEM, SEM, ANY), out_specs=(HBM, HBM), input_output_aliases={0: 0, 1: 1},
        compiler_params=pltpu.CompilerParams(has_side_effects=EFFECT),
    )(h["src"], h["land"], h["send1"], h["recv1"], h["send2"], h["recv2"], after)[1]


def exchange_wait(handles, after, gather, name):
    send_sems, recv_sems, src_thru, land_thru = handles

    def body(src_ref, land_ref, send_sems, recv_sems, after_ref, src_dead, got_ref):
        for cp in _peer_copies(src_ref, land_ref, send_sems, recv_sems, gather):
            cp.wait_send()
            cp.wait_recv()

    return pl.pallas_call(
        body, name=name,
        out_shape=(pltpu.HBM(src_thru.shape, src_thru.dtype), pltpu.HBM(land_thru.shape, land_thru.dtype)),
        in_specs=(HBM, HBM, SEM, SEM, ANY), out_specs=(HBM, HBM), input_output_aliases={0: 0, 1: 1},
        compiler_params=pltpu.CompilerParams(has_side_effects=EFFECT),
    )(src_thru, land_thru, send_sems, recv_sems, after)


def local_step(x, tgt, mod, started, get_w, put_grad, wc, wf, g_mix, bc, lg, lb, gco, gao, g_ffn, bf, g_fin):
    w_in = get_w("w_in", mod)
    proj, h1 = rms_mod_matmul(x, g_mix, mod, 0, 1, w_in, D_IN // N_DEV, "proj_fwd", after=started)
    mix_a, u1 = conv_module_fwd(proj, wc, bc, lg, lb, gco, "conv_module_fwd")
    att, lse = attn_fwd_all(proj, "attn_fwd")
    w_out = get_w("w_out", att)
    y1, mixed = norm_concat_matmul(mix_a, att, gao, w_out, "out_proj_fwd")
    w_up = get_w("w_up", y1)
    up0, x1, h2 = resid_rms_mod_matmul(x, y1, g_ffn, mod, 2, 3, 4, w_up, "up_fwd")
    passed = get_w("w_down", up0, only_pass_on=True)
    act = ffn_act_fwd(up0, wf, bf, "ffn_act_fwd", after=passed)
    w_down = get_w("w_down", act)
    loss_t, dx2, dy2, d_gfin, d_gaf = matmul_loss_bwd(act, w_down, x1, tgt, g_fin, mod, 5, "down_fwd_loss")
    dact = matmul(dy2, w_down, "nt", F32, 512, FFN_TN, "down_bwd_x")
    dw_down = matmul(act, dy2, "tn", BF16, 256, D_MODEL, "down_bwd_w")
    dup0, dbf_g, dbf_v, dwf_g, dwf_v = ffn_bwd(up0, dact, wf, bf, "ffn_bwd", after=put_grad("w_down", dw_down))
    dw_up = matmul_tn_halves(dup0, h2, 256, "up_bwd_w")
    dx1, d_shf, d_scf, d_gffn, dy1, d_gam = matmul_rms_mod_bwd(
        dup0, w_up, x1, dx2, g_ffn, mod, 4, y1, 2, "up_bwd_x", after=put_grad("w_up", dw_up))
    dw_out = matmul(mixed, dy1, "tn", BF16, 256, D_MODEL, "out_proj_bwd_w")
    dmixed, do, dd, d_gao = matmul_combine_bwd(dy1, w_out, att, gao, "out_proj_bwd_x", after=put_grad("w_out", dw_out))
    dqkv = attn_bwd_all(proj, do, lse, dd, "attn_bwd")
    du1, d_gco, d_lg, d_lb, d_bc, d_wc = conv_module_bwd_a(proj, u1, dmixed, lg, lb, gco, "conv_module_bwd_a")
    dproj_a = conv_module_bwd_b(proj, du1, wc, "conv_module_bwd_b")
    dw_in = matmul_tn_pieces(dproj_a, dqkv, h1, "proj_bwd_w")
    dx, d_shm, d_scm, d_gmix = matmul_rms_mod_bwd(
        (dproj_a, dqkv), w_in, x, dx1, g_mix, mod, 1, None, 0, "proj_bwd_x", b_rows=D_IN // N_DEV,
        after=put_grad("w_in", dw_in))
    dmod = jnp.concatenate([d_shm, d_scm, d_gam, d_shf, d_scf, d_gaf], axis=1)
    small = dict(g_norm_mix=d_gmix, b_conv_dw=d_bc, ln_conv_g=d_lg, ln_conv_b=d_lb, g_conv_out=d_gco, g_attn_out=d_gao,
                 g_norm_ffn=d_gffn, b_ffn_dw=jnp.concatenate([dbf_g, dbf_v], axis=1), g_final=d_gfin,
                 w_conv_dw=d_wc, w_ffn_dw=jnp.concatenate([dwf_g, dwf_v], axis=1), dmod=dmod, loss=loss_t[0:1, 0:1])
    return dx, small


PACK_W = 7168
TAPS_PER_ROW = PACK_W // D_CONV
PACKED_AT = dict(
    b_ada=(0, 0, N_MOD * D_MODEL), g_norm_mix=(0, 6144, D_MODEL),
    b_ffn_dw=(1, 0, 2 * D_FF), g_norm_ffn=(1, 5632, D_MODEL), b_conv_dw=(1, 6656, D_CONV),
    g_final=(2, 5632, D_MODEL), ln_conv_g=(2, 6656, D_CONV),
    ln_conv_b=(3, 5632, D_CONV), g_conv_out=(3, 6144, D_CONV), g_attn_out=(3, 6656, D_ATTN))
SMALL_ORDER = list(PACKED_AT)
LOSS_AT = (4, 2 * D_FF)


def pack_small(t):
    cat = lambda *parts: jnp.concatenate(parts, axis=1)
    wf = t["w_ffn_dw"]
    taps = jnp.pad(t["w_conv_dw"].reshape(1, CONV_K * D_CONV), ((0, 0), (0, 3 * PACK_W - CONV_K * D_CONV)))
    return jnp.concatenate([
        cat(t["dmod"], t["g_norm_mix"]),
        cat(t["b_ffn_dw"], t["g_norm_ffn"], t["b_conv_dw"]),
        cat(wf[0:1], t["g_final"], t["ln_conv_g"]),
        cat(wf[1:2], t["ln_conv_b"], t["g_conv_out"], t["g_attn_out"]),
        cat(wf[2:3], jnp.pad(t["loss"], ((0, 0), (0, PACK_W - 2 * D_FF - 1)))),
        taps.reshape(3, PACK_W)], axis=0)


def small_adamw(parts, wmv, name):
    def body(*refs):
        p_ref = refs[0]
        ins = refs[1:1 + 3 * len(SMALL_ORDER)]
        outs = refs[1 + 3 * len(SMALL_ORDER):]
        g = p_ref[0]
        for k in range(1, N_DEV):
            g = g + p_ref[k]
        for i, n in enumerate(SMALL_ORDER):
            row, lane, width = PACKED_AT[n]
            gp = g[row:row + 1, lane:lane + width]
            w_ref, m_ref, v_ref = ins[3 * i:3 * i + 3]
            g_ref, d_ref, nm_ref, nv_ref = outs[4 * i:4 * i + 4]
            g_ref[...] = gp
            d_ref[...], nm_ref[...], nv_ref[...] = _adam(w_ref[...], gp, m_ref[...], v_ref[...])
        wc_ref, wf_ref, loss_ref = outs[4 * len(SMALL_ORDER):]
        for j in range(CONV_K):
            row, lane = 5 + j // TAPS_PER_ROW, (j % TAPS_PER_ROW) * D_CONV
            wc_ref[j:j + 1, :] = g[row:row + 1, lane:lane + D_CONV]
        wf_ref[...] = g[2:2 + FFN_K, 0:2 * D_FF]
        loss_ref[...] = jnp.broadcast_to(g[LOSS_AT[0]:LOSS_AT[0] + 1, LOSS_AT[1]:LOSS_AT[1] + 1], (8, 128))

    args, out_shape = [parts], []
    for n in SMALL_ORDER:
        args += list(wmv[n])
        out_shape += [jax.ShapeDtypeStruct(wmv[n][0].shape, F32)] * 4
    out_shape += [jax.ShapeDtypeStruct((CONV_K, D_CONV), F32), jax.ShapeDtypeStruct((FFN_K, 2 * D_FF), F32),
                  jax.ShapeDtypeStruct((8, 128), F32)]
    res = pl.pallas_call(body, out_shape=tuple(out_shape), name=name, compiler_params=_cp())(*args)
    per = {n: tuple(res[4 * i:4 * i + 4]) for i, n in enumerate(SMALL_ORDER)}
    return per, res[-3], res[-2], res[-1][0, 0]


def shard_adamw(items, name):
    def body(*refs):
        ins, outs = refs[:4 * len(items)], refs[4 * len(items):]
        for i in range(len(items)):
            g_ref, w_ref, m_ref, v_ref = ins[4 * i:4 * i + 4]
            og_ref, d_ref, nm_ref, nv_ref = outs[4 * i:4 * i + 4]
            og_ref[...] = g_ref[...]
            d_ref[...], nm_ref[...], nv_ref[...] = _adam(w_ref[...], g_ref[...], m_ref[...], v_ref[...])

    args = [a for item in items for a in item]
    out_shape = tuple(jax.ShapeDtypeStruct(item[1].shape, F32) for item in items for _ in range(4))
    res = pl.pallas_call(body, out_shape=out_shape, name=name, compiler_params=_cp())(*args)
    return [tuple(res[4 * i:4 * i + 4]) for i in range(len(items))]


def _shard(full, n_cols, me):
    return lax.dynamic_slice(full, (0, me * n_cols), (full.shape[0], n_cols))


WEIGHTS = ["w_ada", "b_ada", "g_norm_mix", "w_in", "w_conv_dw", "b_conv_dw", "ln_conv_g", "ln_conv_b", "g_conv_out",
           "g_attn_out", "w_out", "g_norm_ffn", "w_up", "w_ffn_dw", "b_ffn_dw", "w_down", "g_final"]


def kernel(x, c, w_ada, b_ada, g_norm_mix, w_in, w_conv_dw, b_conv_dw, ln_conv_g, ln_conv_b, g_conv_out, g_attn_out, w_out, g_norm_ffn, w_up, w_ffn_dw, b_ffn_dw, w_down, g_final, loss_target, m_w_ada, m_b_ada, m_g_norm_mix, m_w_in, m_w_conv_dw, m_b_conv_dw, m_ln_conv_g, m_ln_conv_b, m_g_conv_out, m_g_attn_out, m_w_out, m_g_norm_ffn, m_w_up, m_w_ffn_dw, m_b_ffn_dw, m_w_down, m_g_final, v_w_ada, v_b_ada, v_g_norm_mix, v_w_in, v_w_conv_dw, v_b_conv_dw, v_ln_conv_g, v_ln_conv_b, v_g_conv_out, v_g_attn_out, v_w_out, v_g_norm_ffn, v_w_up, v_w_ffn_dw, v_b_ffn_dw, v_w_down, v_g_final):
    args = dict(locals())
    me = 4 * lax.axis_index("x") + 2 * lax.axis_index("y") + lax.axis_index("c")
    me1 = me.astype(jnp.int32).reshape(1)

    def flat(name, prefix=""):
        a = args[prefix + name]
        return a.reshape(a.shape[-2] if a.ndim > 1 else 1, a.shape[-1])

    def flat_t(name, prefix=""):
        return args[prefix + name][0].T

    n_in, n_up, r_out, r_down = w_in.shape[2], w_up.shape[2], w_out.shape[1], w_down.shape[1]
    n_ada, n_wc, n_wf = w_ada.shape[2], w_conv_dw.shape[2], w_ffn_dw.shape[2]
    taps_c = jnp.pad(flat("w_conv_dw").reshape(1, CONV_K * n_wc), ((0, 0), (0, 2 * D_MODEL - CONV_K * n_wc)))
    taps_f = jnp.pad(flat("w_ffn_dw").reshape(1, FFN_K * n_wf), ((0, 0), (0, 3 * D_MODEL - FFN_K * n_wf)))
    first = jnp.concatenate([c, taps_c.reshape(2, D_MODEL), taps_f.reshape(3, D_MODEL), jnp.zeros((2, D_MODEL), F32)], axis=0)
    w_in_block = flat_t("w_in").astype(BF16)
    hi = lax.reduce_precision(first, 8, 7)
    mid = lax.reduce_precision(first - hi, 8, 7)
    low = lax.reduce_precision(first - hi - mid, 8, 7)
    terms = jnp.concatenate([hi, mid, low, jnp.zeros((8, D_MODEL), F32)], axis=0).astype(BF16)
    first_block = all_gather(jnp.concatenate([w_in_block, terms], axis=0), "gather_c_taps_w_in")
    terms = first_block[:, n_in:n_in + 24, :].astype(F32)
    first_all = (terms[:, 0:8] + terms[:, 8:16]) + terms[:, 16:24]
    c_all = first_all[:, 0, :]
    wc_full = first_all[:, 1:3, :].reshape(N_DEV, 2 * D_MODEL)[:, :CONV_K * n_wc].reshape(N_DEV, CONV_K, n_wc)
    wc_full = wc_full.transpose(1, 0, 2).reshape(CONV_K, D_CONV)
    wf_full = first_all[:, 3:6, :].reshape(N_DEV, 3 * D_MODEL)[:, :FFN_K * n_wf].reshape(N_DEV, FFN_K, n_wf)
    wf_full = wf_full.transpose(1, 0, 2).reshape(FFN_K, 2 * D_FF)
    mod_all = ada_gather(c_all, flat("w_ada"), _shard(flat("b_ada"), n_ada, me), "ada_gather_mod")
    mod = lax.dynamic_index_in_dim(mod_all, me, axis=1, keepdims=False).reshape(N_MOD, D_MODEL)
    mod = jnp.pad(mod, ((0, 2), (0, 0)))

    order = ("w_out", "w_up", "w_down")
    blocks = dict(w_up=flat_t("w_up").astype(BF16), w_out=flat("w_out").astype(BF16), w_down=flat("w_down").astype(BF16))
    handles, tok = gather2_start([blocks[name] for name in order], "gather_weights_start", mod_all)
    gathers = dict(zip(order, handles))

    def pass_on(name, after):
        if "send2" not in gathers[name]:
            group = ("w_out", "w_up") if name != "w_down" else ("w_down",)
            passed, token = gather2_pass([gathers[w] for w in group], after, f"gather_{name}_pass")
            gathers.update(zip(group, passed))
            return token

    def gathered(name, after):
        pass_on(name, after)
        land = gather2_wait(gathers[name], after, f"gather_{name}_wait")
        return lax.dynamic_update_index_in_dim(land, blocks[name], me, axis=0)

    def get_w(name, after, only_pass_on=False):
        if name == "w_in":
            return first_block
        if only_pass_on:
            return pass_on(name, after)
        return gathered(name, after).reshape(-1, D_MODEL)

    exchanges = {}

    def put_grad(name, dw, after=None):
        dev_major = dw.reshape(N_DEV, -1, D_MODEL)
        (exchanges[name],), token = exchange_start([dev_major], False, f"exchange_{name}_start", after)
        return token

    grad_x, small = local_step(
        x[0], loss_target[0], mod, tok, get_w, put_grad, wc_full, wf_full,
        flat("g_norm_mix"), flat("b_conv_dw"), flat("ln_conv_g"), flat("ln_conv_b"), flat("g_conv_out"),
        flat("g_attn_out"), flat("g_norm_ffn"), flat("b_ffn_dw"), flat("g_final"))

    out = {}

    def finish(name, tr, after, then=None):
        mine, parts = exchange_wait(exchanges[name], after, False, f"exchange_{name}_wait")
        if then is not None:
            then(parts)
        if name in ("w_in", "w_up"):
            res = sum_adamw(parts, mine, me1, flat_t(name), flat_t(name, "m_"), flat_t(name, "v_"), tr, "adamw_" + name)
            out[name] = tuple(r.T for r in res)
        else:
            res = out[name] = sum_adamw(parts, mine, me1, flat(name), flat(name, "m_"), flat(name, "v_"), tr,
                                        "adamw_" + name)
        return res[0]

    after = finish("w_down", r_down, grad_x)
    after = finish("w_up", n_up // 2, after)
    after = finish("w_out", r_out, after)
    packed = pack_small(small)
    small_gather = []
    after = finish("w_in", n_in, after, then=lambda parts: small_gather.append(
        gather2_start([packed], "gather_small_start", parts)[0][0]))
    (handle,), _ = gather2_pass(small_gather, after, "gather_small_pass")
    small_all = lax.dynamic_update_index_in_dim(
        gather2_wait(handle, after, "gather_small_wait"), packed, me, axis=0)

    wmv = {n: (flat(n), flat(n, "m_"), flat(n, "v_")) for n in SMALL_ORDER}
    per, g_wc, g_wf, loss = small_adamw(small_all, wmv, "adamw_small")
    out.update(per)
    taps = shard_adamw([(_shard(g_wc, n_wc, me), flat("w_conv_dw"), flat("w_conv_dw", "m_"), flat("w_conv_dw", "v_")),
                        (_shard(g_wf, n_wf, me), flat("w_ffn_dw"), flat("w_ffn_dw", "m_"), flat("w_ffn_dw", "v_"))],
                       "adamw_taps")
    out["w_conv_dw"], out["w_ffn_dw"] = taps

    dmod_cols = _shard(small_all[:, 0, :], n_ada, me)
    out["w_ada"] = ada_bwd_adamw(c_all, dmod_cols, flat("w_ada"), flat("w_ada", "m_"), flat("w_ada", "v_"), "adamw_w_ada")

    result = [loss, grad_x[None]]
    for k in range(4):
        result += [out[n][k].reshape(args[n].shape) for n in WEIGHTS]
    return tuple(result)
```

```python
import jax
import jax.numpy as jnp
from jax import lax
from jax.experimental import pallas as pl
from jax.experimental.pallas import tpu as pltpu

F32 = jnp.float32
BF16 = jnp.bfloat16

N_DEV = 8
SEQ = 2048
D_MODEL = 1024
D_CONV = 512
D_ATTN = 512
HEAD_DIM = 64
CONV_K = 31
D_FF = 2816
FFN_K = 3
D_IN = 2 * D_CONV + 3 * D_ATTN
N_MOD = 6
EPS = 1e-6
ATTN_BLOCK = 128
PATTERNS = ((2048, 1), (512, 4), (128, 16))
NEG = -1e30

ADAM_LR, ADAM_B1, ADAM_B2, ADAM_EPS, ADAM_WD, ADAM_STEP = 0.001, 0.9, 0.999, 1e-08, 0.01, 10

ROWS = 256
CONV_HALO = 32
FFN_HALO = 8
FFN_TN = 1408
VMEM_LIMIT = 56 * 1024 * 1024


NT = (((1,), (1,)), ((), ()))
ANY = pl.BlockSpec(memory_space=pl.ANY)


def _cp(*sem):
    return pltpu.CompilerParams(dimension_semantics=sem if sem else None, vmem_limit_bytes=VMEM_LIMIT)


def _with_after(body, n_in, after):
    if after is None:
        return body, [], []
    return (lambda *refs: body(*refs[:n_in], *refs[n_in + 1:])), [ANY], [after]


def _sig(x):
    return 1.0 / (1.0 + jnp.exp(-x))


def _rsum(x):
    return jnp.sum(x, axis=0, keepdims=True)


def _mean(x):
    return jnp.mean(x, axis=-1, keepdims=True)


def _acc(ref, val, first):
    @pl.when(first)
    def _():
        ref[...] = val

    @pl.when(jnp.logical_not(first))
    def _():
        ref[...] += val


SUB = 16


def _for_chunks(fn, unroll=1, rows=ROWS):
    def step(i, carry):
        fn(pl.ds(pl.multiple_of(i * SUB, SUB), SUB))
        return carry

    lax.fori_loop(0, rows // SUB, step, 0, unroll=unroll)


PAIR = 2 * ROWS


def _pair_spec(width, col=0):
    return pl.BlockSpec((PAIR, width), lambda i: (i, col))


def _halves():
    return [slice(h * ROWS, (h + 1) * ROWS) for h in range(2)]


def rms_mod_matmul(x, g, mod, sh_row, sc_row, b, b_rows, name, after=None):
    n = N_DEV * b_rows

    def body(x_ref, g_ref, mod_ref, b_ref, o_ref, h_ref):
        for rs in _halves():
            xx = x_ref[rs, :]
            r = lax.rsqrt(_mean(xx * xx) + EPS)
            h = (xx * r * g_ref[...] * (1.0 + mod_ref[sc_row:sc_row + 1, :]) + mod_ref[sh_row:sh_row + 1, :]).astype(BF16)
            h_ref[rs, :] = h
            o_ref[rs, :] = lax.dot_general(h, b_ref[...].reshape(n, D_MODEL), NT, preferred_element_type=F32)

    body, more_specs, more = _with_after(body, 4, after)
    return pl.pallas_call(
        body, out_shape=(jax.ShapeDtypeStruct((SEQ, n), F32), jax.ShapeDtypeStruct((SEQ, D_MODEL), BF16)),
        grid=(SEQ // PAIR,),
        in_specs=[_pair_spec(D_MODEL), _vec_spec(D_MODEL), _vec_spec(D_MODEL, 8),
                  pl.BlockSpec((N_DEV, b_rows, D_MODEL), lambda i: (0, 0, 0))] + more_specs,
        out_specs=(_pair_spec(n), _pair_spec(D_MODEL)), name=name, compiler_params=_cp("parallel"))(x, g, mod, b, *more)


def norm_concat_matmul(mix_a, att, gao, w, name):
    def body(a_ref, att_ref, g_ref, w_ref, y_ref, mixed_ref):
        for rs in _halves():
            aa = att_ref[rs, :]
            mixed_ref[rs, 0:D_CONV] = a_ref[rs, :]
            mixed_ref[rs, D_CONV:] = (aa * lax.rsqrt(_mean(aa * aa) + EPS) * g_ref[...]).astype(BF16)
            y_ref[rs, :] = jnp.dot(mixed_ref[rs, :], w_ref[...], preferred_element_type=F32)

    return pl.pallas_call(
        body, out_shape=(jax.ShapeDtypeStruct((SEQ, D_MODEL), F32), jax.ShapeDtypeStruct((SEQ, D_MODEL), BF16)),
        grid=(SEQ // PAIR,),
        in_specs=[_pair_spec(D_CONV), _pair_spec(D_ATTN), _vec_spec(D_ATTN), pl.BlockSpec(w.shape, lambda i: (0, 0))],
        out_specs=(_pair_spec(D_MODEL), _pair_spec(D_MODEL)), name=name, compiler_params=_cp("parallel"))(mix_a, att, gao, w)


def resid_rms_mod_matmul(x, y, g, mod, ga_row, sh_row, sc_row, w, name):
    n = w.shape[0]

    def body(x_ref, y_ref, g_ref, mod_ref, w_ref, o_ref, x1_ref, h_ref):
        x1 = x_ref[...] + mod_ref[ga_row:ga_row + 1, :] * y_ref[...]
        x1_ref[...] = x1
        r = lax.rsqrt(_mean(x1 * x1) + EPS)
        h = (x1 * r * g_ref[...] * (1.0 + mod_ref[sc_row:sc_row + 1, :]) + mod_ref[sh_row:sh_row + 1, :]).astype(BF16)
        h_ref[...] = h
        o_ref[...] = lax.dot_general(h, w_ref[...], NT, preferred_element_type=F32)

    return pl.pallas_call(
        body,
        out_shape=(jax.ShapeDtypeStruct((SEQ, n), F32), jax.ShapeDtypeStruct((SEQ, D_MODEL), F32),
                   jax.ShapeDtypeStruct((SEQ, D_MODEL), BF16)),
        grid=(SEQ // ROWS,),
        in_specs=[_row_spec(D_MODEL), _row_spec(D_MODEL), _vec_spec(D_MODEL), _vec_spec(D_MODEL, 8),
                  pl.BlockSpec(w.shape, lambda i: (0, 0))],
        out_specs=(_row_spec(n), _row_spec(D_MODEL), _row_spec(D_MODEL)),
        name=name, compiler_params=_cp("parallel"))(x, y, g, mod, w)


def matmul_combine_bwd(dy, w, att, gao, name, after=None):
    def body(dy_ref, w_ref, att_ref, g_ref, dm_ref, do_ref, dd_ref, dg_ref):
        @pl.when(pl.program_id(0) == 0)
        def _():
            dg_ref[...] = jnp.zeros_like(dg_ref)

        same_head = (jnp.right_shift(lax.broadcasted_iota(jnp.int32, (D_ATTN, D_ATTN), 0), 6)
                     == jnp.right_shift(lax.broadcasted_iota(jnp.int32, (D_ATTN, D_ATTN), 1), 6)).astype(F32)
        for rs in _halves():
            dmixed = lax.dot_general(dy_ref[rs, :], w_ref[...], NT, preferred_element_type=F32)
            dm_ref[rs, :] = dmixed
            att = att_ref[rs, :]
            r = lax.rsqrt(_mean(att * att) + EPS)
            xn = att * r
            dm = dmixed[:, D_CONV:]
            dg_ref[...] += _rsum(dm * xn)
            dyn = dm * g_ref[...]
            do = r * (dyn - xn * _mean(dyn * xn))
            do_ref[rs, :] = do
            dd_ref[rs, :] = jnp.dot(do * att, same_head, preferred_element_type=F32, precision=lax.Precision.HIGHEST)

    rs = _pair_spec(D_ATTN)
    f = jax.ShapeDtypeStruct((SEQ, D_ATTN), F32)
    body, more_specs, more = _with_after(body, 4, after)
    return pl.pallas_call(
        body, out_shape=(jax.ShapeDtypeStruct((SEQ, D_MODEL), F32), f, f, jax.ShapeDtypeStruct((1, D_ATTN), F32)),
        grid=(SEQ // PAIR,),
        in_specs=[_pair_spec(D_MODEL), pl.BlockSpec(w.shape, lambda i: (0, 0)), rs, _vec_spec(D_ATTN)] + more_specs,
        out_specs=(_pair_spec(D_MODEL), rs, rs, _vec_spec(D_ATTN)),
        name=name, compiler_params=_cp("arbitrary"))(dy, w, att, gao, *more)


def matmul_loss_bwd(act, w, x1, tgt, g, mod, ga_row, name):
    def body(a_ref, w_ref, x1_ref, t_ref, g_ref, mod_ref, loss_ref, dx2_ref, dy2_ref, dg_ref, dga_ref):
        @pl.when(pl.program_id(0) == 0)
        def _():
            loss_ref[...] = jnp.zeros_like(loss_ref)
            dg_ref[...] = jnp.zeros_like(dg_ref)
            dga_ref[...] = jnp.zeros_like(dga_ref)

        ga = mod_ref[ga_row:ga_row + 1, :]
        for rs in _halves():
            y2 = jnp.dot(a_ref[rs, :], w_ref[...], preferred_element_type=F32)
            x2 = x1_ref[rs, :] + ga * y2
            r = lax.rsqrt(_mean(x2 * x2) + EPS)
            xn = x2 * r
            err = xn * g_ref[...] - t_ref[rs, :]
            loss_ref[...] += jnp.broadcast_to(0.5 * jnp.sum(_mean(err * err)), (8, 128))
            dy = err * (1.0 / D_MODEL)
            dg_ref[...] += _rsum(dy * xn)
            dxn = dy * g_ref[...]
            dx2 = r * (dxn - xn * _mean(dxn * xn))
            dx2_ref[rs, :] = dx2
            dy2_ref[rs, :] = (dx2 * ga).astype(BF16)
            dga_ref[...] += _rsum(dx2 * y2)

    vec = jax.ShapeDtypeStruct((1, D_MODEL), F32)
    rows = _pair_spec
    return pl.pallas_call(
        body,
        out_shape=(jax.ShapeDtypeStruct((8, 128), F32), jax.ShapeDtypeStruct((SEQ, D_MODEL), F32),
                   jax.ShapeDtypeStruct((SEQ, D_MODEL), BF16), vec, vec),
        grid=(SEQ // PAIR,),
        in_specs=[rows(act.shape[1]), pl.BlockSpec(w.shape, lambda i: (0, 0)), rows(D_MODEL), rows(D_MODEL),
                  _vec_spec(D_MODEL), _vec_spec(D_MODEL, 8)],
        out_specs=(pl.BlockSpec((8, 128), lambda i: (0, 0)), rows(D_MODEL), rows(D_MODEL),
                   _vec_spec(D_MODEL), _vec_spec(D_MODEL)),
        name=name, compiler_params=_cp("arbitrary"))(act, w, x1, tgt, g, mod)


def matmul_rms_mod_bwd(a, b, x, dres, g, mod, sc_row, y, ga_row, name, b_rows=None, after=None):
    gated = y is not None
    pieces = isinstance(a, tuple)
    tm = PAIR if pieces else ROWS
    blocks = [slice(h * ROWS, (h + 1) * ROWS) for h in range(tm // ROWS)]
    rows = lambda width: pl.BlockSpec((tm, width), lambda i: (i, 0))
    if pieces:
        a1, a3 = a
        a_args = [a1, a3]
        a_specs = [rows(a1.shape[1]), pl.BlockSpec((3, tm, a3.shape[2]), lambda i: (0, i, 0))]
        b_arg, b_spec = b, pl.BlockSpec((N_DEV, b_rows, D_MODEL), lambda i: (0, 0, 0))
    else:
        k2 = a.shape[2]
        a_args = [a]
        a_specs = [pl.BlockSpec((2, tm, k2), lambda i: (0, i, 0))]
        b_arg, b_spec = b.reshape(2, k2, D_MODEL), pl.BlockSpec((2, k2, D_MODEL), lambda i: (0, 0, 0))
    n_a = len(a_args)

    def body(*refs):
        a_refs, (b_ref, x_ref, dres_ref, g_ref, mod_ref) = refs[:n_a], refs[n_a:n_a + 5]
        if gated:
            y_ref, dx_ref, dsh_ref, dsc_ref, dg_ref, dy_ref, dga_ref = refs[n_a + 5:]
        else:
            dx_ref, dsh_ref, dsc_ref, dg_ref = refs[n_a + 5:]

        @pl.when(pl.program_id(0) == 0)
        def _():
            for ref in (dsh_ref, dsc_ref, dg_ref) + ((dga_ref,) if gated else ()):
                ref[...] = jnp.zeros_like(ref)

        gg = g_ref[...]
        for rs in blocks:
            if pieces:
                wv = b_ref[...].reshape(N_DEV * b_rows, D_MODEL)
                k1, k3 = a_refs[0].shape[1], a_refs[1].shape[2]
                dh = jnp.dot(a_refs[0][rs, :], wv[0:k1], preferred_element_type=F32)
                for t in range(3):
                    dh = dh + jnp.dot(a_refs[1][t, rs, :], wv[k1 + t * k3:k1 + (t + 1) * k3], preferred_element_type=F32)
            else:
                dh = (jnp.dot(a_refs[0][0, rs, :], b_ref[0], preferred_element_type=F32)
                      + jnp.dot(a_refs[0][1, rs, :], b_ref[1], preferred_element_type=F32))
            xx = x_ref[rs, :]
            r = lax.rsqrt(_mean(xx * xx) + EPS)
            xn = xx * r
            dsh_ref[...] += _rsum(dh)
            dsc_ref[...] += _rsum(dh * (xn * gg))
            dt = dh * (1.0 + mod_ref[sc_row:sc_row + 1, :])
            dg_ref[...] += _rsum(dt * xn)
            dxn = dt * gg
            dx = dres_ref[rs, :] + r * (dxn - xn * _mean(dxn * xn))
            dx_ref[rs, :] = dx
            if gated:
                dga_ref[...] += _rsum(dx * y_ref[rs, :])
                dy_ref[rs, :] = (dx * mod_ref[ga_row:ga_row + 1, :]).astype(BF16)

    vec = jax.ShapeDtypeStruct((1, D_MODEL), F32)
    in_specs = a_specs + [b_spec, rows(D_MODEL), rows(D_MODEL), _vec_spec(D_MODEL), _vec_spec(D_MODEL, 8)]
    out_shape = [jax.ShapeDtypeStruct((SEQ, D_MODEL), F32), vec, vec, vec]
    out_specs = [rows(D_MODEL), _vec_spec(D_MODEL), _vec_spec(D_MODEL), _vec_spec(D_MODEL)]
    args = a_args + [b_arg, x, dres, g, mod]
    if gated:
        in_specs.append(rows(D_MODEL))
        out_shape += [jax.ShapeDtypeStruct((SEQ, D_MODEL), BF16), vec]
        out_specs += [rows(D_MODEL), _vec_spec(D_MODEL)]
        args.append(y)
    body, more_specs, more = _with_after(body, len(args), after)
    return pl.pallas_call(
        body, out_shape=tuple(out_shape), grid=(SEQ // tm,), in_specs=in_specs + more_specs, out_specs=tuple(out_specs),
        name=name, compiler_params=_cp("arbitrary"))(*args, *more)


def _prev_halo(halo, width, col):
    per = ROWS // halo
    return pl.BlockSpec((halo, width), lambda i: (jnp.maximum(i * per - 1, 0), col))


def _next_halo(halo, width, col):
    per = ROWS // halo
    last = SEQ // halo - 1
    return pl.BlockSpec((halo, width), lambda i: (jnp.minimum((i + 1) * per, last), col))


CONV_PAD = ROWS + CONV_HALO


def _shift_copies(sh):
    for b in range(1, 8):
        sh[b, 0:CONV_PAD - 8, :] = sh[0, pl.ds(b, CONV_PAD - 8), :]


def _tap(sh, rs_start, offset):
    return sh[offset % 8, pl.ds(pl.multiple_of(rs_start + (offset // 8) * 8, 8), SUB), :]


def _conv_glu(av_ref, ag_ref, avh_ref, agh_ref, sh):
    i = pl.program_id(0)
    hv = avh_ref[...] * _sig(agh_ref[...])
    sh[0, 0:CONV_HALO, :] = jnp.where(i > 0, hv, 0.0)

    def glu(rs):
        sh[0, pl.ds(pl.multiple_of(rs.start + CONV_HALO, SUB), SUB), :] = av_ref[rs, :] * _sig(ag_ref[rs, :])

    _for_chunks(glu)
    _shift_copies(sh)


def _conv_norm(u1, lg_ref, lb_ref):
    mu = _mean(u1)
    cen = u1 - mu
    rs = lax.rsqrt(_mean(cen * cen) + EPS)
    z = cen * rs
    ln = z * lg_ref[...] + lb_ref[...]
    s = _sig(ln)
    return z, rs, ln, s, ln * s


def conv_module_fwd(proj, wc, bc, lg, lb, gco, name):
    def body(av_ref, ag_ref, avh_ref, agh_ref, wc_ref, bc_ref, lg_ref, lb_ref, gco_ref, out_ref, u1_ref, sh):
        _conv_glu(av_ref, ag_ref, avh_ref, agh_ref, sh)

        def conv(rs):
            u1 = jnp.broadcast_to(bc_ref[...], (SUB, D_CONV))
            for j in range(CONV_K):
                u1 = u1 + wc_ref[j:j + 1, :] * _tap(sh, rs.start, CONV_HALO - (CONV_K - 1) + j)
            u1_ref[rs, :] = u1

        _for_chunks(conv)
        _, _, _, _, u2 = _conv_norm(u1_ref[...], lg_ref, lb_ref)
        rc = lax.rsqrt(_mean(u2 * u2) + EPS)
        out_ref[...] = (u2 * rc * gco_ref[...]).astype(BF16)

    v = _vec_spec(D_CONV)
    return pl.pallas_call(
        body, out_shape=(jax.ShapeDtypeStruct((SEQ, D_CONV), BF16), jax.ShapeDtypeStruct((SEQ, D_CONV), F32)),
        grid=(SEQ // ROWS,),
        in_specs=[_row_spec(D_CONV, 0), _row_spec(D_CONV, 1), _prev_halo(CONV_HALO, D_CONV, 0),
                  _prev_halo(CONV_HALO, D_CONV, 1), _vec_spec(D_CONV, CONV_K), v, v, v, v],
        out_specs=(_row_spec(D_CONV), _row_spec(D_CONV)), scratch_shapes=[pltpu.VMEM((8, CONV_PAD, D_CONV), F32)],
        name=name, compiler_params=_cp("parallel"))(proj, proj, proj, proj, wc, bc, lg, lb, gco)


def conv_module_bwd_a(proj, u1, dmixed, lg, lb, gco, name):
    def body(av_ref, ag_ref, avh_ref, agh_ref, u1_ref, dm_ref, lg_ref, lb_ref, gco_ref,
             du1_ref, dgco_ref, dlg_ref, dlb_ref, dbc_ref, dwc_ref, sh, acc):
        first = pl.program_id(0) == 0
        _conv_glu(av_ref, ag_ref, avh_ref, agh_ref, sh)
        z, rs, ln, s, u2 = _conv_norm(u1_ref[...], lg_ref, lb_ref)
        rc = lax.rsqrt(_mean(u2 * u2) + EPS)
        xn = u2 * rc
        dm = dm_ref[...]
        _acc(dgco_ref, _rsum(dm * xn), first)
        dyn = dm * gco_ref[...]
        du2 = rc * (dyn - xn * _mean(dyn * xn))
        dln = du2 * (s * (1.0 + ln * (1.0 - s)))
        _acc(dlg_ref, _rsum(dln * z), first)
        _acc(dlb_ref, _rsum(dln), first)
        dz = dln * lg_ref[...]
        du1 = rs * (dz - _mean(dz) - z * _mean(dz * z))
        du1_ref[...] = du1
        _acc(dbc_ref, _rsum(du1), first)
        acc[...] = jnp.zeros_like(acc)

        def taps(rs):
            d = du1_ref[rs, :]
            for j in range(CONV_K):
                acc[j] += d * _tap(sh, rs.start, CONV_HALO - (CONV_K - 1) + j)

        _for_chunks(taps)

        @pl.when(first)
        def _():
            dwc_ref[...] = jnp.zeros_like(dwc_ref)

        for j in range(CONV_K):
            dwc_ref[j:j + 1, :] += _rsum(acc[j])

    v = _vec_spec(D_CONV)
    vec = jax.ShapeDtypeStruct((1, D_CONV), F32)
    return pl.pallas_call(
        body,
        out_shape=(jax.ShapeDtypeStruct((SEQ, D_CONV), F32), vec, vec, vec, vec, jax.ShapeDtypeStruct((CONV_K, D_CONV), F32)),
        grid=(SEQ // ROWS,),
        in_specs=[_row_spec(D_CONV, 0), _row_spec(D_CONV, 1), _prev_halo(CONV_HALO, D_CONV, 0),
                  _prev_halo(CONV_HALO, D_CONV, 1), _row_spec(D_CONV, 0), _row_spec(D_CONV, 0), v, v, v],
        out_specs=(_row_spec(D_CONV), v, v, v, v, _vec_spec(D_CONV, CONV_K)),
        scratch_shapes=[pltpu.VMEM((8, CONV_PAD, D_CONV), F32), pltpu.VMEM((CONV_K, SUB, D_CONV), F32)],
        name=name, compiler_params=_cp("arbitrary"))(proj, proj, proj, proj, u1, dmixed, lg, lb, gco)


def conv_module_bwd_b(proj, du1, wc, name):
    def body(av_ref, ag_ref, du1_ref, du1n_ref, wc_ref, out_ref, sh):
        i = pl.program_id(0)
        sh[0, 0:ROWS, :] = du1_ref[...]
        sh[0, ROWS:, :] = jnp.where(i < SEQ // ROWS - 1, du1n_ref[...], 0.0)
        _shift_copies(sh)

        def chunk(rs):
            du0 = jnp.zeros((SUB, D_CONV), F32)
            for j in range(CONV_K):
                du0 = du0 + wc_ref[j:j + 1, :] * _tap(sh, rs.start, CONV_K - 1 - j)
            sg = _sig(ag_ref[rs, :])
            out_ref[rs, 0:D_CONV] = (du0 * sg).astype(BF16)
            out_ref[rs, D_CONV:] = (du0 * av_ref[rs, :] * sg * (1.0 - sg)).astype(BF16)

        _for_chunks(chunk)

    return pl.pallas_call(
        body, out_shape=jax.ShapeDtypeStruct((SEQ, 2 * D_CONV), BF16), grid=(SEQ // ROWS,),
        in_specs=[_row_spec(D_CONV, 0), _row_spec(D_CONV, 1), _row_spec(D_CONV, 0), _next_halo(CONV_HALO, D_CONV, 0),
                  _vec_spec(D_CONV, CONV_K)],
        out_specs=_row_spec(2 * D_CONV), scratch_shapes=[pltpu.VMEM((8, CONV_PAD, D_CONV), F32)],
        name=name, compiler_params=_cp("parallel"))(proj, proj, du1, du1, wc)


def _rows(start, size, r):
    return pl.ds(start, size) if r == 1 else pl.ds(start, size, stride=r)


def _unit_rows(r, rho, n, nb):
    win = 2 * ATTN_BLOCK if nb > 1 else ATTN_BLOCK
    if isinstance(n, int):
        kb = max(n - 1, 0)
        q_rows = _rows(rho + r * ATTN_BLOCK * n, ATTN_BLOCK, r)
        k_rows = _rows(rho + r * ATTN_BLOCK * kb, win, r)
    else:
        kb = jnp.maximum(n - 1, 0)
        q_rows = pl.ds(pl.multiple_of(n * ATTN_BLOCK, ATTN_BLOCK), ATTN_BLOCK)
        k_rows = pl.ds(pl.multiple_of(kb * ATTN_BLOCK, ATTN_BLOCK), win)
    return q_rows, k_rows, n - kb


def _band_bias(first_block, transposed):
    shape = (2 * ATTN_BLOCK, ATTN_BLOCK) if transposed else (ATTN_BLOCK, 2 * ATTN_BLOCK)
    q_axis = 1 if transposed else 0
    dist = (0 if first_block else ATTN_BLOCK) + lax.broadcasted_iota(jnp.int32, shape, q_axis) \
        - lax.broadcasted_iota(jnp.int32, shape, 1 - q_axis)
    return jnp.where((dist >= 0) & (dist <= ATTN_BLOCK), 0.0, NEG)


def _per_head(x):
    lane = lax.broadcasted_iota(jnp.int32, x.shape, 1)
    zero = jnp.zeros_like(x)
    return [jnp.where(lane < HEAD_DIM, x, zero), jnp.where(lane >= HEAD_DIM, x, zero)]


SCALE = HEAD_DIM ** -0.5


def _masked_scores(q2, k2, bias):
    return [lax.dot_general(qh, k2, NT, preferred_element_type=F32) + bias for qh in _per_head(q2)]


def _attn_units(r, nb, unit):
    if r == 1:
        def four(i, carry):
            for k in range(4):
                unit(0, 4 * i + k)
            return carry
        lax.fori_loop(0, nb // 4, four, 0)
    else:
        for rho in range(r):
            for n in range(nb):
                unit(rho, n)


N_UNITS = 16


def attn_fwd_all(proj, name):
    def body(q_ref, k_ref, v_ref, att_ref, lse_ref, s_scr, p_scr, lse_scr, den_scr, bias_scr):
        bias_scr[0] = _band_bias(True, False)
        bias_scr[1] = _band_bias(False, False)
        for idx, (sub_len, r) in enumerate(PATTERNS):
            nb = sub_len // ATTN_BLOCK
            win = 2 * ATTN_BLOCK if nb > 1 else ATTN_BLOCK

            def scores(rho, n, r=r, nb=nb, win=win):
                u = rho * nb + n
                q_rows, k_rows, variant = _unit_rows(r, rho, n, nb)
                ss = _masked_scores((q_ref[q_rows, :] * SCALE).astype(BF16), k_ref[k_rows, :].astype(BF16),
                                    bias_scr[variant, :, 0:win])
                for h in range(2):
                    s_scr[2 * u + h, :, 0:win] = ss[h]

            _attn_units(r, nb, scores)

            def softmax(u, carry, win=win):
                lses, dens = [], []
                for h in range(2):
                    sc = s_scr[2 * u + h, :, 0:win]
                    m = jnp.max(sc, axis=1, keepdims=True)
                    p = jnp.exp(sc - m)
                    den = jnp.sum(p, axis=1, keepdims=True)
                    p_scr[2 * u + h, :, 0:win] = p.astype(BF16)
                    lses.append(jnp.broadcast_to(m + jnp.log(den), (ATTN_BLOCK, HEAD_DIM)))
                    dens.append(jnp.broadcast_to(den, (ATTN_BLOCK, HEAD_DIM)))
                lse_scr[u] = jnp.concatenate(lses, axis=1)
                den_scr[u] = jnp.concatenate(dens, axis=1)
                return carry

            lax.fori_loop(0, N_UNITS, softmax, 0, unroll=2)

            def outputs(rho, n, r=r, nb=nb, win=win, idx=idx):
                u = rho * nb + n
                q_rows, k_rows, _ = _unit_rows(r, rho, n, nb)
                vs = _per_head(v_ref[k_rows, :].astype(BF16))
                o = (jnp.dot(p_scr[2 * u, :, 0:win], vs[0], preferred_element_type=F32)
                     + jnp.dot(p_scr[2 * u + 1, :, 0:win], vs[1], preferred_element_type=F32)) / den_scr[u]
                lse = lse_scr[u]
                if idx > 0:
                    old = lse_ref[q_rows, :]
                    top = jnp.maximum(old, lse)
                    new = top + jnp.log(jnp.exp(old - top) + jnp.exp(lse - top))
                    o = att_ref[q_rows, :] * jnp.exp(old - new) + o * jnp.exp(lse - new)
                    lse = new
                att_ref[q_rows, :] = o
                lse_ref[q_rows, :] = lse

            _attn_units(r, nb, outputs)

    blk = lambda first: pl.BlockSpec((SEQ, 128), lambda g: (0, first + g))
    shp = jax.ShapeDtypeStruct((SEQ, D_ATTN), F32)
    big = (2 * N_UNITS, ATTN_BLOCK, 2 * ATTN_BLOCK)
    small = pltpu.VMEM((N_UNITS, ATTN_BLOCK, 128), F32)
    return pl.pallas_call(
        body, out_shape=(shp, shp), grid=(4,), in_specs=[blk(8), blk(12), blk(16)], out_specs=(blk(0), blk(0)),
        scratch_shapes=[pltpu.VMEM(big, F32), pltpu.VMEM(big, BF16), small, small,
                        pltpu.VMEM((2, ATTN_BLOCK, 2 * ATTN_BLOCK), F32)],
        name=name, compiler_params=_cp("parallel"))(proj, proj, proj)


def attn_bwd_all(proj, do, lse, dd, name):
    def body(q_ref, k_ref, v_ref, do_ref, l_ref, dd_ref, out_ref, dq_s, dk_s, dv_s,
             s_scr, dp_scr, ds_scr, st_scr, dpt_scr, pt_scr, dst_scr, qb_scr, kb_scr, dob_scr, bias_scr, bias_t_scr):
        for first_block in (True, False):
            bias_scr[1 - int(first_block)] = _band_bias(first_block, False)
            bias_t_scr[1 - int(first_block)] = _band_bias(first_block, True)
        dq_s[...] = jnp.zeros_like(dq_s)
        dk_s[...] = jnp.zeros_like(dk_s)
        dv_s[...] = jnp.zeros_like(dv_s)
        for sub_len, r in PATTERNS:
            nb = sub_len // ATTN_BLOCK
            win = 2 * ATTN_BLOCK if nb > 1 else ATTN_BLOCK

            def scores(rho, n, r=r, nb=nb, win=win):
                u = rho * nb + n
                q_rows, k_rows, variant = _unit_rows(r, rho, n, nb)
                bias, bias_t = bias_scr[variant, :, 0:win], bias_t_scr[variant, 0:win, :]
                q2 = (q_ref[q_rows, :] * SCALE).astype(BF16)
                kf = k_ref[k_rows, :]
                k2 = kf.astype(BF16)
                do2 = do_ref[q_rows, :].astype(BF16)
                qb_scr[u] = q2
                kb_scr[u, 0:win, :] = (kf * SCALE).astype(BF16)
                dob_scr[u] = do2
                l2 = l_ref[q_rows, :]
                d2 = dd_ref[q_rows, :]
                l2t = l2.T
                d2t = d2.T
                v2 = v_ref[k_rows, :].astype(BF16)
                qs, dos = _per_head(q2), _per_head(do2)
                for h in range(2):
                    c0 = h * HEAD_DIM
                    sc = lax.dot_general(qs[h], k2, NT, preferred_element_type=F32)
                    s_scr[2 * u + h, :, 0:win] = sc + bias - l2[:, c0:c0 + 1]
                    dp_scr[2 * u + h, :, 0:win] = lax.dot_general(dos[h], v2, NT, preferred_element_type=F32) \
                        - d2[:, c0:c0 + 1]
                    sct = lax.dot_general(k2, qs[h], NT, preferred_element_type=F32)
                    st_scr[2 * u + h, 0:win, :] = sct + bias_t - l2t[c0:c0 + 1, :]
                    dpt_scr[2 * u + h, 0:win, :] = lax.dot_general(v2, dos[h], NT, preferred_element_type=F32) \
                        - d2t[c0:c0 + 1, :]

            _attn_units(r, nb, scores)

            def pointwise(hu, carry, win=win):
                ds_scr[hu, :, 0:win] = (jnp.exp(s_scr[hu, :, 0:win]) * dp_scr[hu, :, 0:win]).astype(BF16)
                pt = jnp.exp(st_scr[hu, 0:win, :])
                pt_scr[hu, 0:win, :] = pt.astype(BF16)
                dst_scr[hu, 0:win, :] = (pt * dpt_scr[hu, 0:win, :]).astype(BF16)
                return carry

            lax.fori_loop(0, 2 * N_UNITS, pointwise, 0, unroll=4)

            def grads(rho, n, r=r, nb=nb, win=win):
                u = rho * nb + n
                q_rows, k_rows, _ = _unit_rows(r, rho, n, nb)
                qs, ks, dos = _per_head(qb_scr[u]), _per_head(kb_scr[u, 0:win, :]), _per_head(dob_scr[u])

                def both(scr, rows, rhs):
                    return (jnp.dot(scr[(2 * u,) + rows], rhs[0], preferred_element_type=F32)
                            + jnp.dot(scr[(2 * u + 1,) + rows], rhs[1], preferred_element_type=F32))

                dq_s[q_rows, :] += both(ds_scr, (slice(None), slice(0, win)), ks)
                dk_s[k_rows, :] += both(dst_scr, (slice(0, win), slice(None)), qs)
                dv_s[k_rows, :] += both(pt_scr, (slice(0, win), slice(None)), dos)

            _attn_units(r, nb, grads)
        out_ref[0] = dq_s[...].astype(BF16)
        out_ref[1] = dk_s[...].astype(BF16)
        out_ref[2] = dv_s[...].astype(BF16)

    blk = lambda first: pl.BlockSpec((SEQ, 128), lambda g: (0, first + g))
    acc = pltpu.VMEM((SEQ, 128), F32)
    big = (2 * N_UNITS, ATTN_BLOCK, 2 * ATTN_BLOCK)
    big_t = (2 * N_UNITS, 2 * ATTN_BLOCK, ATTN_BLOCK)
    return pl.pallas_call(
        body, out_shape=jax.ShapeDtypeStruct((3, SEQ, D_ATTN), BF16), grid=(4,),
        in_specs=[blk(8), blk(12), blk(16), blk(0), blk(0), blk(0)],
        out_specs=pl.BlockSpec((3, SEQ, 128), lambda g: (0, 0, g)),
        scratch_shapes=[acc, acc, acc, pltpu.VMEM(big, F32), pltpu.VMEM(big, F32), pltpu.VMEM(big, BF16),
                        pltpu.VMEM(big_t, F32), pltpu.VMEM(big_t, F32), pltpu.VMEM(big_t, BF16), pltpu.VMEM(big_t, BF16),
                        pltpu.VMEM((N_UNITS, ATTN_BLOCK, 128), BF16),
                        pltpu.VMEM((N_UNITS, 2 * ATTN_BLOCK, 128), BF16), pltpu.VMEM((N_UNITS, ATTN_BLOCK, 128), BF16),
                        pltpu.VMEM((2, ATTN_BLOCK, 2 * ATTN_BLOCK), F32), pltpu.VMEM((2, 2 * ATTN_BLOCK, ATTN_BLOCK), F32)],
        name=name, compiler_params=_cp("parallel"))(proj, proj, proj, do, lse, dd)


N_FT = D_FF // FFN_TN


def _ffn_specs():
    per = ROWS // FFN_HALO
    cur_g = pl.BlockSpec((ROWS, FFN_TN), lambda j, i: (i, j))
    cur_v = pl.BlockSpec((ROWS, FFN_TN), lambda j, i: (i, j + N_FT))
    halo_g = pl.BlockSpec((FFN_HALO, FFN_TN), lambda j, i: (jnp.maximum(i * per - 1, 0), j))
    halo_v = pl.BlockSpec((FFN_HALO, FFN_TN), lambda j, i: (jnp.maximum(i * per - 1, 0), j + N_FT))
    w_g = pl.BlockSpec((FFN_K, FFN_TN), lambda j, i: (0, j))
    w_v = pl.BlockSpec((FFN_K, FFN_TN), lambda j, i: (0, j + N_FT))
    b_g = pl.BlockSpec((1, FFN_TN), lambda j, i: (0, j))
    b_v = pl.BlockSpec((1, FFN_TN), lambda j, i: (0, j + N_FT))
    return [cur_g, cur_v, halo_g, halo_v, w_g, w_v, b_g, b_v]


def matmul(a, b, kind, out_dtype, tm, tn, name, b_rows=None):
    stacked = b_rows is not None
    b_shape = (N_DEV * b_rows, b.shape[2]) if stacked else b.shape
    if kind == "nn":
        (m, k), n = a.shape, b_shape[1]
        a_spec = pl.BlockSpec((tm, k), lambda j, i: (i, 0))
        b_spec = pl.BlockSpec((k, tn), lambda j, i: (0, j))
        dims = (((1,), (0,)), ((), ()))
    elif kind == "nt":
        (m, k), n = a.shape, b_shape[0]
        a_spec = pl.BlockSpec((tm, k), lambda j, i: (i, 0))
        b_spec = pl.BlockSpec((tn, k), lambda j, i: (j, 0))
        dims = (((1,), (1,)), ((), ()))
    else:
        (k, m), n = a.shape, b_shape[1]
        a_spec = pl.BlockSpec((k, tm), lambda j, i: (0, i))
        b_spec = pl.BlockSpec((k, tn), lambda j, i: (0, j))
        dims = (((0,), (0,)), ((), ()))
    assert m % tm == 0 and n % tn == 0, (name, m, n, tm, tn)
    if stacked:
        assert b_spec.block_shape[0] == b_shape[0] and kind in ("nn", "nt")
        width = b_spec.block_shape[1]
        b_spec = pl.BlockSpec((N_DEV, b_rows, width), (lambda j, i: (0, 0, j)) if kind == "nn" else (lambda j, i: (0, 0, 0)))

    def body(a_ref, b_ref, o_ref):
        bb = b_ref[...].reshape(b_shape[0], -1) if stacked else b_ref[...]
        o_ref[...] = lax.dot_general(a_ref[...], bb, dims, preferred_element_type=F32).astype(o_ref.dtype)

    return pl.pallas_call(
        body, out_shape=jax.ShapeDtypeStruct((m, n), out_dtype), grid=(n // tn, m // tm),
        in_specs=[a_spec, b_spec], out_specs=pl.BlockSpec((tm, tn), lambda j, i: (i, j)),
        name=name, compiler_params=_cp("parallel", "parallel"))(a, b)


def matmul_tn_pieces(a1, a3, b, name):
    k, n = b.shape
    tm = a3.shape[2]
    n1 = a1.shape[1] // tm

    def body(a1_ref, a3_ref, b_ref, o_ref):
        i = pl.program_id(0)
        tn_dims = (((0,), (0,)), ((), ()))

        @pl.when(i < n1)
        def _():
            o_ref[...] = lax.dot_general(a1_ref[...], b_ref[...], tn_dims, preferred_element_type=F32).astype(BF16)

        @pl.when(i >= n1)
        def _():
            o_ref[...] = lax.dot_general(a3_ref[0], b_ref[...], tn_dims, preferred_element_type=F32).astype(BF16)

    return pl.pallas_call(
        body, out_shape=jax.ShapeDtypeStruct((a1.shape[1] + 3 * tm, n), BF16), grid=(n1 + 3,),
        in_specs=[pl.BlockSpec((k, tm), lambda i: (0, jnp.minimum(i, n1 - 1))),
                  pl.BlockSpec((1, k, tm), lambda i: (jnp.maximum(i - n1, 0), 0, 0)),
                  pl.BlockSpec((k, n), lambda i: (0, 0))],
        out_specs=pl.BlockSpec((tm, n), lambda i: (i, 0)), name=name, compiler_params=_cp("parallel"))(a1, a3, b)


def matmul_tn_halves(a, b, tm, name):
    _, k, m = a.shape
    n = b.shape[1]

    def body(a_ref, b_ref, o_ref):
        o_ref[0] = lax.dot_general(a_ref[0], b_ref[...], (((0,), (0,)), ((), ())), preferred_element_type=F32).astype(BF16)

    return pl.pallas_call(
        body, out_shape=jax.ShapeDtypeStruct((2, m, n), BF16), grid=(2, m // tm),
        in_specs=[pl.BlockSpec((1, k, tm), lambda h, i: (h, 0, i)), pl.BlockSpec((k, n), lambda h, i: (0, 0))],
        out_specs=pl.BlockSpec((1, tm, n), lambda h, i: (h, i, 0)), name=name,
        compiler_params=_cp("parallel", "parallel"))(a, b).reshape(2 * m, n)


def _row_spec(width, col=0):
    return pl.BlockSpec((ROWS, width), lambda i: (i, col))


def _vec_spec(width, rows=1):
    return pl.BlockSpec((rows, width), lambda i: (0, 0))


FFN_RB = 128


def _ffn_lane_blocks(fn):
    for c in range(FFN_TN // 128):
        fn(slice(c * 128, (c + 1) * 128))


def _ffn_taps(cur_ref, halo_ref, head, ls, r0):
    if r0 == 0:
        head[0:FFN_HALO, :] = jnp.where(pl.program_id(1) > 0, halo_ref[:, ls], 0.0)
        head[FFN_HALO:, :] = cur_ref[0:FFN_RB, ls]
        return tuple(head[pl.ds(FFN_HALO - k, FFN_RB), :] for k in (2, 1, 0))
    return tuple(cur_ref[pl.ds(r0 - k, FFN_RB), ls] for k in (2, 1, 0))


def _ffn_conv(taps, w, b):
    return b + w[0] * taps[0] + w[1] * taps[1] + w[2] * taps[2]


def ffn_act_fwd(up0, wf, bf, name, after=None):
    def body(g_ref, v_ref, gh_ref, vh_ref, wg_ref, wv_ref, bg_ref, bv_ref, act_ref, head_g, head_v):
        def lanes(ls):
            wg = [wg_ref[t:t + 1, ls] for t in range(FFN_K)]
            wv = [wv_ref[t:t + 1, ls] for t in range(FFN_K)]
            for r0 in range(0, ROWS, FFN_RB):
                gate = _ffn_conv(_ffn_taps(g_ref, gh_ref, head_g, ls, r0), wg, bg_ref[:, ls])
                val = _ffn_conv(_ffn_taps(v_ref, vh_ref, head_v, ls, r0), wv, bv_ref[:, ls])
                act_ref[r0:r0 + FFN_RB, ls] = (gate * _sig(gate) * val).astype(BF16)

        _ffn_lane_blocks(lanes)

    head = pltpu.VMEM((FFN_HALO + FFN_RB, 128), F32)
    body, more_specs, more = _with_after(body, 8, after)
    return pl.pallas_call(
        body, out_shape=jax.ShapeDtypeStruct((SEQ, D_FF), BF16), grid=(N_FT, SEQ // ROWS),
        in_specs=_ffn_specs() + more_specs, out_specs=pl.BlockSpec((ROWS, FFN_TN), lambda j, i: (i, j)),
        scratch_shapes=[head, head], name=name,
        compiler_params=_cp("parallel", "parallel"))(up0, up0, up0, up0, wf, wf, bf, bf, *more)


def ffn_bwd(up0, dact, wf, bf, name, after=None):
    per = ROWS // FFN_HALO
    last = SEQ // FFN_HALO - 1

    def body(g_ref, v_ref, gh_ref, vh_ref, wg_ref, wv_ref, bg_ref, bv_ref, da_ref, gn_ref, vn_ref, dan_ref,
             out_ref, dbg_ref, dbv_ref, dwg_ref, dwv_ref, pad, head_g, head_v, dgp, dvp, acc):
        i = pl.program_id(1)
        first = i == 0
        acc[...] = jnp.zeros_like(acc)

        def grads(gate, val, da):
            s = _sig(gate)
            return da * val * (s * (1.0 + gate * (1.0 - s))), da * (gate * s)

        fold = lambda q: jnp.sum(q.reshape(FFN_RB // 8, 8, 128), axis=0)

        def lanes(ls):
            wg = [wg_ref[t:t + 1, ls] for t in range(FFN_K)]
            wv = [wv_ref[t:t + 1, ls] for t in range(FFN_K)]
            sums = [jnp.zeros((8, 128), F32)] * (2 + 2 * FFN_K)
            for r0 in range(0, ROWS, FFN_RB):
                gs = _ffn_taps(g_ref, gh_ref, head_g, ls, r0)
                vs = _ffn_taps(v_ref, vh_ref, head_v, ls, r0)
                dgate, dval = grads(_ffn_conv(gs, wg, bg_ref[:, ls]), _ffn_conv(vs, wv, bv_ref[:, ls]),
                                    da_ref[r0:r0 + FFN_RB, ls])
                dgp[r0:r0 + FFN_RB, ls] = dgate
                dvp[r0:r0 + FFN_RB, ls] = dval
                new = [dgate, dval] + [dgate * gs[t] for t in range(FFN_K)] + [dval * vs[t] for t in range(FFN_K)]
                sums = [a + fold(q) for a, q in zip(sums, new)]
            for k in range(2 + 2 * FFN_K):
                acc[k, 0:8, ls] = sums[k]

        _ffn_lane_blocks(lanes)
        _acc(dbg_ref, _rsum(acc[0]), first)
        _acc(dbv_ref, _rsum(acc[1]), first)
        _acc(dwg_ref, jnp.concatenate([_rsum(acc[2 + t]) for t in range(FFN_K)], axis=0), first)
        _acc(dwv_ref, jnp.concatenate([_rsum(acc[5 + t]) for t in range(FFN_K)], axis=0), first)

        def conv_next(cur_ref, nxt_ref, w_ref, b_ref):
            pad[0:FFN_HALO, :] = cur_ref[ROWS - FFN_HALO:, :]
            pad[FFN_HALO:, :] = nxt_ref[...]
            return (b_ref[...] + w_ref[0:1, :] * pad[pl.ds(FFN_HALO - 2, FFN_HALO), :]
                    + w_ref[1:2, :] * pad[pl.ds(FFN_HALO - 1, FFN_HALO), :] + w_ref[2:3, :] * nxt_ref[...])

        gate_n = conv_next(g_ref, gn_ref, wg_ref, bg_ref)
        val_n = conv_next(v_ref, vn_ref, wv_ref, bv_ref)
        dgate_n, dval_n = grads(gate_n, val_n, dan_ref[...])
        inside = i < SEQ // ROWS - 1
        dgp[ROWS:, :] = jnp.where(inside, dgate_n, 0.0)
        dvp[ROWS:, :] = jnp.where(inside, dval_n, 0.0)

        def back(ls):
            for half, (dp, w_ref) in enumerate(((dgp, wg_ref), (dvp, wv_ref))):
                w = [w_ref[t:t + 1, ls] for t in range(FFN_K)]
                for r0 in range(0, ROWS, FFN_RB):
                    out_ref[half, r0:r0 + FFN_RB, ls] = (
                        w[2] * dp[r0:r0 + FFN_RB, ls] + w[1] * dp[pl.ds(r0 + 1, FFN_RB), ls]
                        + w[0] * dp[pl.ds(r0 + 2, FFN_RB), ls]).astype(BF16)

        _ffn_lane_blocks(back)

    body, more_specs, more = _with_after(body, 12, after)
    head = pltpu.VMEM((FFN_HALO + FFN_RB, 128), F32)
    ext = pltpu.VMEM((ROWS + FFN_HALO, FFN_TN), F32)
    vec = jax.ShapeDtypeStruct((1, D_FF), F32)
    taps = jax.ShapeDtypeStruct((FFN_K, D_FF), F32)
    cur = pl.BlockSpec((ROWS, FFN_TN), lambda j, i: (i, j))
    nxt = lambda off: pl.BlockSpec((FFN_HALO, FFN_TN), lambda j, i: (jnp.minimum((i + 1) * per, last), j + off))
    vs = pl.BlockSpec((1, FFN_TN), lambda j, i: (0, j))
    ts = pl.BlockSpec((FFN_K, FFN_TN), lambda j, i: (0, j))
    return pl.pallas_call(
        body, out_shape=(jax.ShapeDtypeStruct((2, SEQ, D_FF), BF16), vec, vec, taps, taps), grid=(N_FT, SEQ // ROWS),
        in_specs=_ffn_specs() + [cur, nxt(0), nxt(N_FT), nxt(0)] + more_specs,
        out_specs=(pl.BlockSpec((2, ROWS, FFN_TN), lambda j, i: (0, i, j)), vs, vs, ts, ts),
        scratch_shapes=[pltpu.VMEM((2 * FFN_HALO, FFN_TN), F32), head, head, ext, ext,
                        pltpu.VMEM((2 + 2 * FFN_K, SUB, FFN_TN), F32)],
        name=name, compiler_params=_cp("parallel", "arbitrary"))(up0, up0, up0, up0, wf, wf, bf, bf, dact, up0, up0, dact,
                                                                 *more)


def ada_fwd(c_all, w_ada, b_cols, name):
    def body(c_ref, w_ref, b_ref, o_ref):
        cc = c_ref[...]
        sc = (cc * _sig(cc)).astype(BF16)
        o_ref[...] = jnp.dot(sc, w_ref[...].astype(BF16), preferred_element_type=F32) + b_ref[...]

    return pl.pallas_call(body, out_shape=jax.ShapeDtypeStruct((N_DEV, w_ada.shape[1]), F32), name=name,
                          compiler_params=_cp())(c_all, w_ada, b_cols)


def _adam(w, g, m, v):
    m = ADAM_B1 * m + (1.0 - ADAM_B1) * g
    v = ADAM_B2 * v + (1.0 - ADAM_B2) * (g * g)
    m_hat = m / (1.0 - ADAM_B1 ** ADAM_STEP)
    v_hat = v / (1.0 - ADAM_B2 ** ADAM_STEP)
    delta = -ADAM_LR * (m_hat / (jnp.sqrt(v_hat) + ADAM_EPS) + ADAM_WD * w)
    return delta, m, v


def ada_bwd_adamw(c_all, dmod_cols, w, m, v, name):
    rows, cols = w.shape
    tr = 256

    def body(c_ref, dm_ref, w_ref, m_ref, v_ref, g_ref, d_ref, nm_ref, nv_ref):
        cc = c_ref[...]
        sc = (cc * _sig(cc)).T
        g = sc[:, 0:1] * dm_ref[0:1, :]
        for b in range(1, N_DEV):
            g = g + sc[:, b:b + 1] * dm_ref[b:b + 1, :]
        g_ref[...] = g
        d_ref[...], nm_ref[...], nv_ref[...] = _adam(w_ref[...], g, m_ref[...], v_ref[...])

    blk = pl.BlockSpec((tr, cols), lambda i: (i, 0))
    shp = jax.ShapeDtypeStruct((rows, cols), F32)
    return pl.pallas_call(
        body, out_shape=(shp, shp, shp, shp), grid=(rows // tr,),
        in_specs=[pl.BlockSpec((N_DEV, tr), lambda i: (0, i)), pl.BlockSpec((N_DEV, cols), lambda i: (0, 0)), blk, blk, blk],
        out_specs=(blk, blk, blk, blk), name=name, compiler_params=_cp("parallel"))(c_all, dmod_cols, w, m, v)


def sum_adamw(parts, mine, me, w, m, v, tr, name):
    n_parts, rows, cols = parts.shape

    def body(me_ref, p_ref, own_ref, w_ref, m_ref, v_ref, g_ref, d_ref, nm_ref, nv_ref):
        def chunk(rs):
            g = own_ref[0, rs, :].astype(F32)
            for k in range(1, n_parts):
                g = g + p_ref[k, rs, :].astype(F32)
            g_ref[rs, :] = g
            d_ref[rs, :], nm_ref[rs, :], nv_ref[rs, :] = _adam(w_ref[rs, :], g, m_ref[rs, :], v_ref[rs, :])

        _for_chunks(chunk, 2, tr)

    blk = pl.BlockSpec((tr, cols), lambda i, me_ref: (i, 0))
    shp = jax.ShapeDtypeStruct((rows, cols), F32)
    grid_spec = pltpu.PrefetchScalarGridSpec(
        num_scalar_prefetch=1, grid=(rows // tr,),
        in_specs=[pl.BlockSpec((n_parts, tr, cols), lambda i, me_ref: (0, i, 0)),
                  pl.BlockSpec((1, tr, cols), lambda i, me_ref: (me_ref[0], i, 0)), blk, blk, blk],
        out_specs=(blk, blk, blk, blk))
    return pl.pallas_call(body, out_shape=(shp, shp, shp, shp), grid_spec=grid_spec, name=name,
                          compiler_params=_cp("parallel"))(me, parts, mine, w, m, v)


MESH = pl.DeviceIdType.MESH


def all_gather(block, name, after=None):
    extra = () if after is None else (after,)

    def body(x_ref, *refs):
        out_ref, send_sems, recv_sems, local_sem = refs[len(extra):]
        x, y, c = lax.axis_index("x"), lax.axis_index("y"), lax.axis_index("c")
        me, sibling = (x, y, c), (x, y, 1 - c)
        chips = [(1 - x, y), (x, 1 - y), (1 - x, 1 - y)]

        def slot(px, py, pc):
            return out_ref.at[4 * px + 2 * py + pc]

        def copy(k, blk, to, src=None):
            return pltpu.make_async_remote_copy(
                src_ref=slot(*blk) if src is None else src, dst_ref=slot(*blk),
                send_sem=send_sems.at[k], recv_sem=recv_sems.at[k], device_id=to, device_id_type=MESH)

        mine = pltpu.make_async_copy(x_ref, slot(*me), local_sem)
        mine.start()
        first = [copy(0, me, sibling, src=x_ref)]
        first += [copy(1 + j, me, (*chip, c), src=x_ref) for j, chip in enumerate(chips)]
        for cp in first:
            cp.start()
        passed = [copy(4 + j, (*chip, c), sibling) for j, chip in enumerate(chips)]
        for j, chip in enumerate(chips):
            copy(1 + j, (*chip, c), me).wait_recv()
            passed[j].start()
        copy(0, sibling, me).wait_recv()
        for j, chip in enumerate(chips):
            copy(4 + j, (*chip, 1 - c), me).wait_recv()
        for cp in first + passed:
            cp.wait_send()
        mine.wait()

    return pl.pallas_call(
        body, out_shape=jax.ShapeDtypeStruct((N_DEV,) + block.shape, block.dtype), in_specs=[ANY] * (1 + len(extra)), out_specs=ANY,
        scratch_shapes=[pltpu.SemaphoreType.DMA((7,)), pltpu.SemaphoreType.DMA((7,)), pltpu.SemaphoreType.DMA],
        name=name)(block, *extra)


HBM = pl.BlockSpec(memory_space=pltpu.HBM)
SEM = pl.BlockSpec(memory_space=pltpu.SEMAPHORE)
EFFECT = pltpu.SideEffectType.DATAFLOW_SIDE_EFFECTING


def _peer_copies(src_ref, land_ref, send_sems, recv_sems, gather):
    x, y, c = lax.axis_index("x"), lax.axis_index("y"), lax.axis_index("c")
    me = 4 * x + 2 * y + c
    copies = []
    for k in range(1, N_DEV):
        px = 1 - x if k & 4 else x
        py = 1 - y if k & 2 else y
        pc = 1 - c if k & 1 else c
        copies.append(pltpu.make_async_remote_copy(
            src_ref=src_ref if gather else src_ref.at[4 * px + 2 * py + pc],
            dst_ref=land_ref.at[me] if gather else land_ref.at[k],
            send_sem=send_sems.at[k - 1], recv_sem=recv_sems.at[k - 1], device_id=(px, py, pc), device_id_type=MESH))
    return copies


def exchange_start(srcs, gather, name, after=None):
    n = len(srcs)
    land_shapes = [(N_DEV,) + src.shape if gather else src.shape for src in srcs]
    extra = () if after is None else (after,)

    def body(*refs):
        src_refs, land_refs = refs[0:n], refs[n:2 * n]
        outs = refs[2 * n + len(extra):]
        for k in range(n):
            for cp in _peer_copies(src_refs[k], land_refs[k], outs[4 * k], outs[4 * k + 1], gather):
                cp.start()
        token = outs[4 * n]
        token[...] = jnp.zeros_like(token)

    out_shape, out_specs, aliases = [], [], {}
    for k, src in enumerate(srcs):
        out_shape += [pltpu.SemaphoreType.DMA((N_DEV - 1,)), pltpu.SemaphoreType.DMA((N_DEV - 1,)),
                      pltpu.HBM(src.shape, src.dtype), pltpu.HBM(land_shapes[k], src.dtype)]
        out_specs += [SEM, SEM, HBM, HBM]
        aliases[k] = 4 * k + 2
        aliases[n + k] = 4 * k + 3
    out_shape.append(jax.ShapeDtypeStruct((8, 128), F32))
    out_specs.append(pl.BlockSpec(memory_space=pltpu.VMEM))
    res = pl.pallas_call(
        body, name=name, out_shape=tuple(out_shape), in_specs=(HBM,) * (2 * n) + (ANY,) * len(extra),
        out_specs=tuple(out_specs), input_output_aliases=aliases,
        compiler_params=pltpu.CompilerParams(has_side_effects=EFFECT),
    )(*[pltpu.with_memory_space_constraint(src, pltpu.HBM) for src in srcs],
      *[pltpu.with_memory_space_constraint(lax.empty(shp, src.dtype), pltpu.HBM) for shp, src in zip(land_shapes, srcs)],
      *extra)
    return [tuple(res[4 * k:4 * k + 4]) for k in range(n)], res[4 * n]


def _stage1_peer(i):
    x, y, c = lax.axis_index("x"), lax.axis_index("y"), lax.axis_index("c")
    if i == 0:
        return (x, y, 1 - c)
    return (1 - x if i & 1 else x, 1 - y if i & 2 else y, c)


def _slot_of(peer):
    return 4 * peer[0] + 2 * peer[1] + peer[2]


def _stage1_copy(i, src_ref, land_ref, send_sems, recv_sems):
    me = _slot_of((lax.axis_index("x"), lax.axis_index("y"), lax.axis_index("c")))
    return pltpu.make_async_remote_copy(src_ref=src_ref, dst_ref=land_ref.at[me], send_sem=send_sems.at[i],
                                        recv_sem=recv_sems.at[i], device_id=_stage1_peer(i), device_id_type=MESH)


def _stage2_copy(j, land_ref, send_sems, recv_sems):
    slot = _slot_of(_stage1_peer(j + 1))
    return pltpu.make_async_remote_copy(src_ref=land_ref.at[slot], dst_ref=land_ref.at[slot], send_sem=send_sems.at[j],
                                        recv_sem=recv_sems.at[j], device_id=_stage1_peer(0), device_id_type=MESH)


def gather2_start(srcs, name, after=None):
    n = len(srcs)
    extra = () if after is None else (after,)

    def body(*refs):
        src_refs, land_refs = refs[0:n], refs[n:2 * n]
        outs = refs[2 * n + len(extra):]
        for k in range(n):
            for i in range(4):
                _stage1_copy(i, src_refs[k], land_refs[k], outs[4 * k], outs[4 * k + 1]).start()
        outs[4 * n][...] = jnp.zeros((8, 128), F32)

    out_shape, out_specs, aliases = [], [], {}
    for k, src in enumerate(srcs):
        out_shape += [pltpu.SemaphoreType.DMA((4,)), pltpu.SemaphoreType.DMA((4,)),
                      pltpu.HBM(src.shape, src.dtype), pltpu.HBM((N_DEV,) + src.shape, src.dtype)]
        out_specs += [SEM, SEM, HBM, HBM]
        aliases[k] = 4 * k + 2
        aliases[n + k] = 4 * k + 3
    out_shape.append(jax.ShapeDtypeStruct((8, 128), F32))
    out_specs.append(pl.BlockSpec(memory_space=pltpu.VMEM))
    res = pl.pallas_call(
        body, name=name, out_shape=tuple(out_shape), in_specs=(HBM,) * (2 * n) + (ANY,) * len(extra),
        out_specs=tuple(out_specs), input_output_aliases=aliases,
        compiler_params=pltpu.CompilerParams(has_side_effects=EFFECT),
    )(*[pltpu.with_memory_space_constraint(src, pltpu.HBM) for src in srcs],
      *[pltpu.with_memory_space_constraint(lax.empty((N_DEV,) + src.shape, src.dtype), pltpu.HBM) for src in srcs], *extra)
    return [dict(send1=res[4 * k], recv1=res[4 * k + 1], src=res[4 * k + 2], land=res[4 * k + 3]) for k in range(n)], \
        res[4 * n]


def gather2_pass(handles, after, name):
    n = len(handles)

    def body(*refs):
        src_refs, land_refs, recv1 = refs[0:n], refs[n:2 * n], refs[2 * n:3 * n]
        outs = refs[3 * n + 1:]
        for k in range(n):
            for j in range(3):
                _stage1_copy(j + 1, src_refs[k], land_refs[k], recv1[k], recv1[k]).wait_recv()
                _stage2_copy(j, land_refs[k], outs[3 * k], outs[3 * k + 1]).start()
        outs[3 * n][...] = jnp.zeros((8, 128), F32)

    out_shape, out_specs, aliases = [], [], {}
    for k, h in enumerate(handles):
        out_shape += [pltpu.SemaphoreType.DMA((3,)), pltpu.SemaphoreType.DMA((3,)), pltpu.HBM(h["land"].shape, h["land"].dtype)]
        out_specs += [SEM, SEM, HBM]
        aliases[n + k] = 3 * k + 2
    out_shape.append(jax.ShapeDtypeStruct((8, 128), F32))
    out_specs.append(pl.BlockSpec(memory_space=pltpu.VMEM))
    res = pl.pallas_call(
        body, name=name, out_shape=tuple(out_shape), in_specs=(HBM,) * (2 * n) + (SEM,) * n + (ANY,),
        out_specs=tuple(out_specs), input_output_aliases=aliases,
        compiler_params=pltpu.CompilerParams(has_side_effects=EFFECT),
    )(*[h["src"] for h in handles], *[h["land"] for h in handles], *[h["recv1"] for h in handles], after)
    return [dict(h, send2=res[3 * k], recv2=res[3 * k + 1], land=res[3 * k + 2]) for k, h in enumerate(handles)], res[3 * n]


def gather2_wait(h, after, name):
    def body(src_ref, land_ref, send1, recv1, send2, recv2, after_ref, src_dead, got_ref):
        for i in range(4):
            _stage1_copy(i, src_ref, land_ref, send1, recv1).wait_send()
        _stage1_copy(0, src_ref, land_ref, send1, recv1).wait_recv()
        for j in range(3):
            cp = _stage2_copy(j, land_ref, send2, recv2)
            cp.wait_send()
            cp.wait_recv()

    return pl.pallas_call(
        body, name=name,
        out_shape=(pltpu.HBM(h["src"].shape, h["src"].dtype), pltpu.HBM(h["land"].shape, h["land"].dtype)),
        in_specs=(HBM, HBM, SEM, SEM, SEM, SEM, ANY), out_specs=(HBM, HBM), input_output_aliases={0: 0, 1: 1},
        compiler_params=pltpu.CompilerParams(has_side_effects=EFFECT),
    )(h["src"], h["land"], h["send1"], h["recv1"], h["send2"], h["recv2"], after)[1]


def exchange_wait(handles, after, gather, name):
    send_sems, recv_sems, src_thru, land_thru = handles

    def body(src_ref, land_ref, send_sems, recv_sems, after_ref, src_dead, got_ref):
        for cp in _peer_copies(src_ref, land_ref, send_sems, recv_sems, gather):
            cp.wait_send()
            cp.wait_recv()

    return pl.pallas_call(
        body, name=name,
        out_shape=(pltpu.HBM(src_thru.shape, src_thru.dtype), pltpu.HBM(land_thru.shape, land_thru.dtype)),
        in_specs=(HBM, HBM, SEM, SEM, ANY), out_specs=(HBM, HBM), input_output_aliases={0: 0, 1: 1},
        compiler_params=pltpu.CompilerParams(has_side_effects=EFFECT),
    )(src_thru, land_thru, send_sems, recv_sems, after)


def local_step(x, tgt, mod, started, get_w, put_grad, wc, wf, g_mix, bc, lg, lb, gco, gao, g_ffn, bf, g_fin):
    w_in = get_w("w_in", mod)
    proj, h1 = rms_mod_matmul(x, g_mix, mod, 0, 1, w_in, D_IN // N_DEV, "proj_fwd", after=started)
    mix_a, u1 = conv_module_fwd(proj, wc, bc, lg, lb, gco, "conv_module_fwd")
    att, lse = attn_fwd_all(proj, "attn_fwd")
    w_out = get_w("w_out", att)
    y1, mixed = norm_concat_matmul(mix_a, att, gao, w_out, "out_proj_fwd")
    w_up = get_w("w_up", y1)
    up0, x1, h2 = resid_rms_mod_matmul(x, y1, g_ffn, mod, 2, 3, 4, w_up, "up_fwd")
    passed = get_w("w_down", up0, only_pass_on=True)
    act = ffn_act_fwd(up0, wf, bf, "ffn_act_fwd", after=passed)
    w_down = get_w("w_down", act)
    loss_t, dx2, dy2, d_gfin, d_gaf = matmul_loss_bwd(act, w_down, x1, tgt, g_fin, mod, 5, "down_fwd_loss")
    dact = matmul(dy2, w_down, "nt", F32, 512, FFN_TN, "down_bwd_x")
    dw_down = matmul(act, dy2, "tn", BF16, 256, D_MODEL, "down_bwd_w")
    dup0, dbf_g, dbf_v, dwf_g, dwf_v = ffn_bwd(up0, dact, wf, bf, "ffn_bwd", after=put_grad("w_down", dw_down))
    dw_up = matmul_tn_halves(dup0, h2, 256, "up_bwd_w")
    dx1, d_shf, d_scf, d_gffn, dy1, d_gam = matmul_rms_mod_bwd(
        dup0, w_up, x1, dx2, g_ffn, mod, 4, y1, 2, "up_bwd_x", after=put_grad("w_up", dw_up))
    dw_out = matmul(mixed, dy1, "tn", BF16, 256, D_MODEL, "out_proj_bwd_w")
    dmixed, do, dd, d_gao = matmul_combine_bwd(dy1, w_out, att, gao, "out_proj_bwd_x", after=put_grad("w_out", dw_out))
    dqkv = attn_bwd_all(proj, do, lse, dd, "attn_bwd")
    du1, d_gco, d_lg, d_lb, d_bc, d_wc = conv_module_bwd_a(proj, u1, dmixed, lg, lb, gco, "conv_module_bwd_a")
    dproj_a = conv_module_bwd_b(proj, du1, wc, "conv_module_bwd_b")
    dw_in = matmul_tn_pieces(dproj_a, dqkv, h1, "proj_bwd_w")
    dx, d_shm, d_scm, d_gmix = matmul_rms_mod_bwd(
        (dproj_a, dqkv), w_in, x, dx1, g_mix, mod, 1, None, 0, "proj_bwd_x", b_rows=D_IN // N_DEV,
        after=put_grad("w_in", dw_in))
    dmod = jnp.concatenate([d_shm, d_scm, d_gam, d_shf, d_scf, d_gaf], axis=1)
    small = dict(g_norm_mix=d_gmix, b_conv_dw=d_bc, ln_conv_g=d_lg, ln_conv_b=d_lb, g_conv_out=d_gco, g_attn_out=d_gao,
                 g_norm_ffn=d_gffn, b_ffn_dw=jnp.concatenate([dbf_g, dbf_v], axis=1), g_final=d_gfin,
                 w_conv_dw=d_wc, w_ffn_dw=jnp.concatenate([dwf_g, dwf_v], axis=1), dmod=dmod, loss=loss_t[0:1, 0:1])
    return dx, small


PACK_W = 7168
TAPS_PER_ROW = PACK_W // D_CONV
PACKED_AT = dict(
    b_ada=(0, 0, N_MOD * D_MODEL), g_norm_mix=(0, 6144, D_MODEL),
    b_ffn_dw=(1, 0, 2 * D_FF), g_norm_ffn=(1, 5632, D_MODEL), b_conv_dw=(1, 6656, D_CONV),
    g_final=(2, 5632, D_MODEL), ln_conv_g=(2, 6656, D_CONV),
    ln_conv_b=(3, 5632, D_CONV), g_conv_out=(3, 6144, D_CONV), g_attn_out=(3, 6656, D_ATTN))
SMALL_ORDER = list(PACKED_AT)
LOSS_AT = (4, 2 * D_FF)


def pack_small(t):
    cat = lambda *parts: jnp.concatenate(parts, axis=1)
    wf = t["w_ffn_dw"]
    taps = jnp.pad(t["w_conv_dw"].reshape(1, CONV_K * D_CONV), ((0, 0), (0, 3 * PACK_W - CONV_K * D_CONV)))
    return jnp.concatenate([
        cat(t["dmod"], t["g_norm_mix"]),
        cat(t["b_ffn_dw"], t["g_norm_ffn"], t["b_conv_dw"]),
        cat(wf[0:1], t["g_final"], t["ln_conv_g"]),
        cat(wf[1:2], t["ln_conv_b"], t["g_conv_out"], t["g_attn_out"]),
        cat(wf[2:3], jnp.pad(t["loss"], ((0, 0), (0, PACK_W - 2 * D_FF - 1)))),
        taps.reshape(3, PACK_W)], axis=0)


def small_adamw(parts, wmv, name):
    def body(*refs):
        p_ref = refs[0]
        ins = refs[1:1 + 3 * len(SMALL_ORDER)]
        outs = refs[1 + 3 * len(SMALL_ORDER):]
        g = p_ref[0]
        for k in range(1, N_DEV):
            g = g + p_ref[k]
        for i, n in enumerate(SMALL_ORDER):
            row, lane, width = PACKED_AT[n]
            gp = g[row:row + 1, lane:lane + width]
            w_ref, m_ref, v_ref = ins[3 * i:3 * i + 3]
            g_ref, d_ref, nm_ref, nv_ref = outs[4 * i:4 * i + 4]
            g_ref[...] = gp
            d_ref[...], nm_ref[...], nv_ref[...] = _adam(w_ref[...], gp, m_ref[...], v_ref[...])
        wc_ref, wf_ref, loss_ref = outs[4 * len(SMALL_ORDER):]
        for j in range(CONV_K):
            row, lane = 5 + j // TAPS_PER_ROW, (j % TAPS_PER_ROW) * D_CONV
            wc_ref[j:j + 1, :] = g[row:row + 1, lane:lane + D_CONV]
        wf_ref[...] = g[2:2 + FFN_K, 0:2 * D_FF]
        loss_ref[...] = jnp.broadcast_to(g[LOSS_AT[0]:LOSS_AT[0] + 1, LOSS_AT[1]:LOSS_AT[1] + 1], (8, 128))

    args, out_shape = [parts], []
    for n in SMALL_ORDER:
        args += list(wmv[n])
        out_shape += [jax.ShapeDtypeStruct(wmv[n][0].shape, F32)] * 4
    out_shape += [jax.ShapeDtypeStruct((CONV_K, D_CONV), F32), jax.ShapeDtypeStruct((FFN_K, 2 * D_FF), F32),
                  jax.ShapeDtypeStruct((8, 128), F32)]
    res = pl.pallas_call(body, out_shape=tuple(out_shape), name=name, compiler_params=_cp())(*args)
    per = {n: tuple(res[4 * i:4 * i + 4]) for i, n in enumerate(SMALL_ORDER)}
    return per, res[-3], res[-2], res[-1][0, 0]


def shard_adamw(items, name):
    def body(*refs):
        ins, outs = refs[:4 * len(items)], refs[4 * len(items):]
        for i in range(len(items)):
            g_ref, w_ref, m_ref, v_ref = ins[4 * i:4 * i + 4]
            og_ref, d_ref, nm_ref, nv_ref = outs[4 * i:4 * i + 4]
            og_ref[...] = g_ref[...]
            d_ref[...], nm_ref[...], nv_ref[...] = _adam(w_ref[...], g_ref[...], m_ref[...], v_ref[...])

    args = [a for item in items for a in item]
    out_shape = tuple(jax.ShapeDtypeStruct(item[1].shape, F32) for item in items for _ in range(4))
    res = pl.pallas_call(body, out_shape=out_shape, name=name, compiler_params=_cp())(*args)
    return [tuple(res[4 * i:4 * i + 4]) for i in range(len(items))]


def _shard(full, n_cols, me):
    return lax.dynamic_slice(full, (0, me * n_cols), (full.shape[0], n_cols))


WEIGHTS = ["w_ada", "b_ada", "g_norm_mix", "w_in", "w_conv_dw", "b_conv_dw", "ln_conv_g", "ln_conv_b", "g_conv_out",
           "g_attn_out", "w_out", "g_norm_ffn", "w_up", "w_ffn_dw", "b_ffn_dw", "w_down", "g_final"]


def kernel(x, c, w_ada, b_ada, g_norm_mix, w_in, w_conv_dw, b_conv_dw, ln_conv_g, ln_conv_b, g_conv_out, g_attn_out, w_out, g_norm_ffn, w_up, w_ffn_dw, b_ffn_dw, w_down, g_final, loss_target, m_w_ada, m_b_ada, m_g_norm_mix, m_w_in, m_w_conv_dw, m_b_conv_dw, m_ln_conv_g, m_ln_conv_b, m_g_conv_out, m_g_attn_out, m_w_out, m_g_norm_ffn, m_w_up, m_w_ffn_dw, m_b_ffn_dw, m_w_down, m_g_final, v_w_ada, v_b_ada, v_g_norm_mix, v_w_in, v_w_conv_dw, v_b_conv_dw, v_ln_conv_g, v_ln_conv_b, v_g_conv_out, v_g_attn_out, v_w_out, v_g_norm_ffn, v_w_up, v_w_ffn_dw, v_b_ffn_dw, v_w_down, v_g_final):
    args = dict(locals())
    me = 4 * lax.axis_index("x") + 2 * lax.axis_index("y") + lax.axis_index("c")
    me1 = me.astype(jnp.int32).reshape(1)

    def flat(name, prefix=""):
        a = args[prefix + name]
        return a.reshape(a.shape[-2] if a.ndim > 1 else 1, a.shape[-1])

    def flat_t(name, prefix=""):
        return args[prefix + name][0].T

    n_in, n_up, r_out, r_down = w_in.shape[2], w_up.shape[2], w_out.shape[1], w_down.shape[1]
    n_ada, n_wc, n_wf = w_ada.shape[2], w_conv_dw.shape[2], w_ffn_dw.shape[2]
    taps_c = jnp.pad(flat("w_conv_dw").reshape(1, CONV_K * n_wc), ((0, 0), (0, 2 * D_MODEL - CONV_K * n_wc)))
    taps_f = jnp.pad(flat("w_ffn_dw").reshape(1, FFN_K * n_wf), ((0, 0), (0, 3 * D_MODEL - FFN_K * n_wf)))
    first = jnp.concatenate([c, taps_c.reshape(2, D_MODEL), taps_f.reshape(3, D_MODEL), jnp.zeros((2, D_MODEL), F32)], axis=0)
    w_in_block = flat_t("w_in").astype(BF16)
    hi = lax.reduce_precision(first, 8, 7)
    mid = lax.reduce_precision(first - hi, 8, 7)
    low = lax.reduce_precision(first - hi - mid, 8, 7)
    terms = jnp.concatenate([hi, mid, low, jnp.zeros((8, D_MODEL), F32)], axis=0).astype(BF16)
    first_block = all_gather(jnp.concatenate([w_in_block, terms], axis=0), "gather_c_taps_w_in")
    terms = first_block[:, n_in:n_in + 24, :].astype(F32)
    first_all = (terms[:, 0:8] + terms[:, 8:16]) + terms[:, 16:24]
    c_all = first_all[:, 0, :]
    wc_full = first_all[:, 1:3, :].reshape(N_DEV, 2 * D_MODEL)[:, :CONV_K * n_wc].reshape(N_DEV, CONV_K, n_wc)
    wc_full = wc_full.transpose(1, 0, 2).reshape(CONV_K, D_CONV)
    wf_full = first_all[:, 3:6, :].reshape(N_DEV, 3 * D_MODEL)[:, :FFN_K * n_wf].reshape(N_DEV, FFN_K, n_wf)
    wf_full = wf_full.transpose(1, 0, 2).reshape(FFN_K, 2 * D_FF)
    mod_cols = ada_fwd(c_all, flat("w_ada"), _shard(flat("b_ada"), n_ada, me), "ada_fwd")
    mod_all = all_gather(mod_cols, "gather_mod")
    mod = lax.dynamic_index_in_dim(mod_all, me, axis=1, keepdims=False).reshape(N_MOD, D_MODEL)
    mod = jnp.pad(mod, ((0, 2), (0, 0)))

    order = ("w_out", "w_up", "w_down")
    blocks = dict(w_up=flat_t("w_up").astype(BF16), w_out=flat("w_out").astype(BF16), w_down=flat("w_down").astype(BF16))
    handles, tok = gather2_start([blocks[name] for name in order], "gather_weights_start", mod_all)
    gathers = dict(zip(order, handles))

    def pass_on(name, after):
        if "send2" not in gathers[name]:
            group = ("w_out", "w_up") if name != "w_down" else ("w_down",)
            passed, token = gather2_pass([gathers[w] for w in group], after, f"gather_{name}_pass")
            gathers.update(zip(group, passed))
            return token

    def gathered(name, after):
        pass_on(name, after)
        land = gather2_wait(gathers[name], after, f"gather_{name}_wait")
        return lax.dynamic_update_index_in_dim(land, blocks[name], me, axis=0)

    def get_w(name, after, only_pass_on=False):
        if name == "w_in":
            return first_block
        if only_pass_on:
            return pass_on(name, after)
        return gathered(name, after).reshape(-1, D_MODEL)

    exchanges = {}

    def put_grad(name, dw, after=None):
        dev_major = dw.reshape(N_DEV, -1, D_MODEL)
        (exchanges[name],), token = exchange_start([dev_major], False, f"exchange_{name}_start", after)
        return token

    grad_x, small = local_step(
        x[0], loss_target[0], mod, tok, get_w, put_grad, wc_full, wf_full,
        flat("g_norm_mix"), flat("b_conv_dw"), flat("ln_conv_g"), flat("ln_conv_b"), flat("g_conv_out"),
        flat("g_attn_out"), flat("g_norm_ffn"), flat("b_ffn_dw"), flat("g_final"))

    out = {}

    def finish(name, tr, after, then=None):
        mine, parts = exchange_wait(exchanges[name], after, False, f"exchange_{name}_wait")
        if then is not None:
            then(parts)
        if name in ("w_in", "w_up"):
            res = sum_adamw(parts, mine, me1, flat_t(name), flat_t(name, "m_"), flat_t(name, "v_"), tr, "adamw_" + name)
            out[name] = tuple(r.T for r in res)
        else:
            res = out[name] = sum_adamw(parts, mine, me1, flat(name), flat(name, "m_"), flat(name, "v_"), tr,
                                        "adamw_" + name)
        return res[0]

    after = finish("w_down", r_down, grad_x)
    after = finish("w_up", n_up // 2, after)
    packed = pack_small(small)
    small_gather = []
    after = finish("w_in", n_in, after, then=lambda parts: small_gather.append(
        gather2_start([packed], "gather_small_start", parts)[0][0]))
    after = finish("w_out", r_out, after)
    (handle,), _ = gather2_pass(small_gather, after, "gather_small_pass")
    small_all = lax.dynamic_update_index_in_dim(
        gather2_wait(handle, after, "gather_small_wait"), packed, me, axis=0)

    wmv = {n: (flat(n), flat(n, "m_"), flat(n, "v_")) for n in SMALL_ORDER}
    per, g_wc, g_wf, loss = small_adamw(small_all, wmv, "adamw_small")
    out.update(per)
    taps = shard_adamw([(_shard(g_wc, n_wc, me), flat("w_conv_dw"), flat("w_conv_dw", "m_"), flat("w_conv_dw", "v_")),
                        (_shard(g_wf, n_wf, me), flat("w_ffn_dw"), flat("w_ffn_dw", "m_"), flat("w_ffn_dw", "v_"))],
                       "adamw_taps")
    out["w_conv_dw"], out["w_ffn_dw"] = taps

    dmod_cols = _shard(small_all[:, 0, :], n_ada, me)
    out["w_ada"] = ada_bwd_adamw(c_all, dmod_cols, flat("w_ada"), flat("w_ada", "m_"), flat("w_ada", "v_"), "adamw_w_ada")

    result = [loss, grad_x[None]]
    for k in range(4):
        result += [out[n][k].reshape(args[n].shape) for n in WEIGHTS]
    return tuple(result)
```
